```python
import math
import jax
import jax.numpy as jnp
from jax import lax
import numpy as np

D_MODEL = 2048
BATCH = 8
SEQ = 2048
DEPTH = 1

ATTN_HEAD_DIM = 64
D_ATTN = D_MODEL // 2
N_ATTN_HEADS = D_ATTN // ATTN_HEAD_DIM
ROT_DIM = ATTN_HEAD_DIM // 4
ROPE_THETA = 500000.0
DILATED_PATTERNS = ((128, 1), (512, 4), (2048, 16))
ATTN_BLOCK = 128
SSM_HEAD_DIM = 64
D_SSM = D_MODEL // 2
N_SSM_HEADS = D_SSM // SSM_HEAD_DIM
SSM_GROUPS = 4
SSM_STATE = 128
CONV_WIDTH = 4
SSD_CHUNK = 128
D_CONV = D_SSM + 2 * SSM_GROUPS * SSM_STATE
DT_MIN = 1e-3
DT_MAX = 1e-1
D_MIX = D_ATTN + D_SSM
D_IN = 3 * D_ATTN + D_SSM + D_CONV + N_SSM_HEADS
N_MEM = 256
N_CROSS_HEADS = 4
CROSS_HEAD_DIM = 128
D_CROSS = N_CROSS_HEADS * CROSS_HEAD_DIM
D_FF = 4 * D_MODEL
EPS = 1e-6

kernel_name = "hymba_ssd_dilated_attn_memxattn_sqrelu"


def rms_norm(x, g):
    xf = x.astype(jnp.float32)
    y = xf * lax.rsqrt(jnp.mean(xf * xf, axis=-1, keepdims=True) + EPS)
    return (y * g.astype(jnp.float32)).astype(x.dtype)


def partial_rope(x, positions):
    half = ROT_DIM // 2
    inv_freq = ROPE_THETA ** (-2.0 * jnp.arange(half, dtype=jnp.float32) / ROT_DIM)
    ang = positions.astype(jnp.float32)[..., None] * inv_freq
    cos = jnp.cos(ang)[:, :, None, :]
    sin = jnp.sin(ang)[:, :, None, :]
    xf = x.astype(jnp.float32)
    x1 = xf[..., :half]
    x2 = xf[..., half:ROT_DIM]
    out = jnp.concatenate([x1 * cos - x2 * sin, x2 * cos + x1 * sin, xf[..., ROT_DIM:]], axis=-1)
    return out.astype(x.dtype)


def dilated_window_branch(q, k, v, window, dilation):
    bsz, s_len, n_h, hd = q.shape
    steps = window // dilation
    span = dilation * ATTN_BLOCK
    s_pad = -(-s_len // span) * span
    nb = s_pad // span

    def to_blocks(t):
        t = jnp.pad(t, ((0, 0), (0, s_pad - s_len), (0, 0), (0, 0)))
        return t.reshape(bsz, nb, ATTN_BLOCK, dilation, n_h, hd)

    qb, kb, vb = to_blocks(q), to_blocks(k), to_blocks(v)

    def band(t):
        prev = jnp.concatenate([jnp.zeros_like(t[:, :1]), t[:, :-1]], axis=1)
        return jnp.concatenate([prev, t], axis=2)

    kband, vband = band(kb), band(vb)
    s = jnp.einsum('bnqrhd,bnkrhd->bnrhqk', qb, kband)
    qi = jnp.arange(ATTN_BLOCK)[:, None]
    kj = jnp.arange(2 * ATTN_BLOCK)[None, :]
    dist = qi + ATTN_BLOCK - kj
    in_window = (dist >= 0) & (dist <= steps)
    has_prev = (jnp.arange(nb)[:, None, None] > 0) | (kj[None] >= ATTN_BLOCK)
    mask = in_window[None] & has_prev
    s = jnp.where(mask[None, :, None, None], s, -jnp.inf)
    m = jnp.max(s, axis=-1, keepdims=True)
    p = jnp.exp(s - m)
    denom = jnp.sum(p, axis=-1)
    lse = m[..., 0] + jnp.log(denom)
    o = jnp.einsum('bnrhqk,bnkrhd->bnqrhd', p, vband)
    o = o / jnp.transpose(denom, (0, 1, 4, 2, 3))[..., None]
    o = o.reshape(bsz, s_pad, n_h, hd)[:, :s_len]
    lse = jnp.transpose(lse, (0, 1, 4, 2, 3)).reshape(bsz, s_pad, n_h)[:, :s_len]
    return o, lse


def dilated_attention(q, k, v, positions, g_q, g_k):
    q = partial_rope(rms_norm(q, g_q), positions).astype(jnp.float32) * (ATTN_HEAD_DIM ** -0.5)
    k = partial_rope(rms_norm(k, g_k), positions).astype(jnp.float32)
    v = v.astype(jnp.float32)
    outs, lses = [], []
    for window, dilation in DILATED_PATTERNS:
        o, l = dilated_window_branch(q, k, v, window, dilation)
        outs.append(o)
        lses.append(l)
    wts = jax.nn.softmax(jnp.stack(lses, axis=0), axis=0)
    return jnp.einsum('gbsh,gbshd->bshd', wts, jnp.stack(outs, axis=0))


def ssd_chunked(x, dt, a, b_mat, c_mat):
    bsz, l_len, n_h, p_dim = x.shape
    g, n = b_mat.shape[2], b_mat.shape[3]
    hg = n_h // g
    nc = l_len // SSD_CHUNK
    q = SSD_CHUNK
    xc = x.reshape(bsz, nc, q, g, hg, p_dim)
    dtc = dt.reshape(bsz, nc, q, g, hg)
    bc = b_mat.reshape(bsz, nc, q, g, n)
    cc = c_mat.reshape(bsz, nc, q, g, n)
    a_cs = jnp.cumsum(dtc * a.reshape(g, hg), axis=2)
    seg = a_cs[:, :, :, None] - a_cs[:, :, None, :]
    causal = jnp.tril(jnp.ones((q, q), dtype=bool))[:, :, None, None]
    l_mat = jnp.exp(jnp.where(causal, seg, -jnp.inf))
    cb = jnp.einsum('bclgn,bcsgn->bclsg', cc, bc)
    w = cb[..., None] * l_mat * dtc[:, :, None]
    y_diag = jnp.einsum('bclsgh,bcsghp->bclghp', w, xc)
    decay_states = jnp.exp(a_cs[:, :, -1:] - a_cs)
    states = jnp.einsum('bcsgn,bcsgh,bcsghp->bcghpn', bc, decay_states * dtc, xc)
    chunk_decay = jnp.exp(a_cs[:, :, -1])

    def step(h, inp):
        s_c, a_c = inp
        return h * a_c[..., None, None] + s_c, h

    h0 = jnp.zeros((bsz, g, hg, p_dim, n), jnp.float32)
    _, prev = lax.scan(step, h0, (jnp.moveaxis(states, 1, 0), jnp.moveaxis(chunk_decay, 1, 0)))
    prev = jnp.moveaxis(prev, 0, 1)
    y_off = jnp.einsum('bclgn,bcghpn->bclghp', cc, prev) * jnp.exp(a_cs)[..., None]
    return (y_diag + y_off).reshape(bsz, l_len, n_h, p_dim)


def ssd_mixer(z, xbc, dt_raw, conv_w, conv_b, dt_bias, a_log, d_skip, g_out):
    bsz, l_len, _ = xbc.shape
    xbc = lax.conv_general_dilated(
        xbc, conv_w.astype(xbc.dtype)[:, None, :], window_strides=(1,),
        padding=[(CONV_WIDTH - 1, 0)], dimension_numbers=('NWC', 'WIO', 'NWC'),
        feature_group_count=D_CONV) + conv_b.astype(xbc.dtype)
    xbc = jax.nn.silu(xbc)
    xs = xbc[..., :D_SSM].astype(jnp.float32).reshape(bsz, l_len, N_SSM_HEADS, SSM_HEAD_DIM)
    b_mat = xbc[..., D_SSM:D_SSM + SSM_GROUPS * SSM_STATE].astype(jnp.float32).reshape(bsz, l_len, SSM_GROUPS, SSM_STATE)
    c_mat = xbc[..., D_SSM + SSM_GROUPS * SSM_STATE:].astype(jnp.float32).reshape(bsz, l_len, SSM_GROUPS, SSM_STATE)
    dt = jax.nn.softplus(dt_raw.astype(jnp.float32) + dt_bias.astype(jnp.float32))
    a = -jnp.exp(a_log.astype(jnp.float32))
    y = ssd_chunked(xs, dt, a, b_mat, c_mat) + d_skip.astype(jnp.float32)[:, None] * xs
    y = y.reshape(bsz, l_len, D_SSM) * jax.nn.silu(z.astype(jnp.float32))
    y = rms_norm(y.reshape(bsz, l_len, SSM_GROUPS, D_SSM // SSM_GROUPS),
                 g_out.reshape(SSM_GROUPS, D_SSM // SSM_GROUPS))
    return y.reshape(bsz, l_len, D_SSM).astype(z.dtype)


def cross_attention(h, mem_h, w_q, w_kv, w_o, g_q, g_k):
    bsz, s_len, _ = h.shape
    q = (h @ w_q).reshape(bsz, s_len, N_CROSS_HEADS, CROSS_HEAD_DIM)
    kv = (mem_h @ w_kv).reshape(bsz, -1, 2, N_CROSS_HEADS, CROSS_HEAD_DIM)
    k, v = kv[:, :, 0], kv[:, :, 1]
    q = rms_norm(q, g_q).astype(jnp.float32) * (CROSS_HEAD_DIM ** -0.5)
    k = rms_norm(k, g_k).astype(jnp.float32)
    p = jax.nn.softmax(jnp.einsum('bshd,bmhd->bhsm', q, k), axis=-1)
    o = jnp.einsum('bhsm,bmhd->bshd', p, v.astype(jnp.float32)).astype(h.dtype)
    return o.reshape(bsz, s_len, D_CROSS) @ w_o


def _fwd_setup_inputs(seed: int = 0) -> dict:
    key = jax.random.key(seed)
    ks = jax.random.split(key, 26)
    f32 = jnp.float32

    def nrm(k, shape, scale):
        return jax.random.normal(k, shape, f32) * scale

    def gain(k, shape):
        return 1.0 + 0.02 * jax.random.normal(k, shape, f32)

    dt0 = jnp.exp(jax.random.uniform(ks[9], (DEPTH, N_SSM_HEADS), f32, math.log(DT_MIN), math.log(DT_MAX)))
    return {
        "x": nrm(ks[0], (BATCH, SEQ, D_MODEL), 1.0),
        "mem": nrm(ks[1], (BATCH, N_MEM, D_MODEL), 1.0),
        "positions": jnp.broadcast_to(jnp.arange(SEQ, dtype=jnp.int32), (BATCH, SEQ)),
        "g_mix": gain(ks[2], (DEPTH, D_MODEL)),
        "w_in": nrm(ks[3], (DEPTH, D_MODEL, D_IN), D_MODEL ** -0.5),
        "g_q": gain(ks[4], (DEPTH, ATTN_HEAD_DIM)),
        "g_k": gain(ks[5], (DEPTH, ATTN_HEAD_DIM)),
        "g_attn_out": gain(ks[6], (DEPTH, D_ATTN)),
        "conv_w": nrm(ks[7], (DEPTH, CONV_WIDTH, D_CONV), CONV_WIDTH ** -0.5),
        "conv_b": nrm(ks[8], (DEPTH, D_CONV), 0.02),
        "dt_bias": dt0 + jnp.log(-jnp.expm1(-dt0)),
        "a_log": jnp.log(jax.random.uniform(ks[10], (DEPTH, N_SSM_HEADS), f32, 1.0, 16.0)),
        "d_skip": 1.0 + 0.1 * jax.random.normal(ks[11], (DEPTH, N_SSM_HEADS), f32),
        "g_ssm_out": gain(ks[12], (DEPTH, D_SSM)),
        "w_out": nrm(ks[13], (DEPTH, D_MIX, D_MODEL), D_MIX ** -0.5),
        "g_cross": gain(ks[14], (DEPTH, D_MODEL)),
        "g_mem": gain(ks[15], (DEPTH, D_MODEL)),
        "w_cq": nrm(ks[16], (DEPTH, D_MODEL, D_CROSS), D_MODEL ** -0.5),
        "w_ckv": nrm(ks[17], (DEPTH, D_MODEL, 2 * D_CROSS), D_MODEL ** -0.5),
        "g_cq": gain(ks[18], (DEPTH, CROSS_HEAD_DIM)),
        "g_ck": gain(ks[19], (DEPTH, CROSS_HEAD_DIM)),
        "w_co": nrm(ks[20], (DEPTH, D_CROSS, D_MODEL), D_CROSS ** -0.5),
        "g_mlp": gain(ks[21], (DEPTH, D_MODEL)),
        "w_up": nrm(ks[22], (DEPTH, D_MODEL, D_FF), D_MODEL ** -0.5),
        "w_down": nrm(ks[23], (DEPTH, D_FF, D_MODEL), D_FF ** -0.5),
    }


def _fwd_reference(x, mem, positions, g_mix, w_in, g_q, g_k, g_attn_out, conv_w, conv_b, dt_bias,
              a_log, d_skip, g_ssm_out, w_out, g_cross, g_mem, w_cq, w_ckv, g_cq, g_ck, w_co,
              g_mlp, w_up, w_down):
    bsz, s_len, _ = x.shape
    splits = np.cumsum([D_ATTN, D_ATTN, D_ATTN, D_SSM, D_CONV]).tolist()
    for i in range(DEPTH):
        h = rms_norm(x, g_mix[i])
        q, k, v, z, xbc, dt_raw = jnp.split(h @ w_in[i], splits, axis=-1)
        shp = (bsz, s_len, N_ATTN_HEADS, ATTN_HEAD_DIM)
        attn = dilated_attention(q.reshape(shp), k.reshape(shp), v.reshape(shp), positions, g_q[i], g_k[i])
        attn = rms_norm(attn.reshape(bsz, s_len, D_ATTN), g_attn_out[i]).astype(x.dtype)
        ssm = ssd_mixer(z, xbc, dt_raw, conv_w[i], conv_b[i], dt_bias[i], a_log[i], d_skip[i], g_ssm_out[i])
        x = x + jnp.concatenate([attn, ssm], axis=-1) @ w_out[i]
        x = x + cross_attention(rms_norm(x, g_cross[i]), rms_norm(mem, g_mem[i]),
                                w_cq[i], w_ckv[i], w_co[i], g_cq[i], g_ck[i])
        hm = rms_norm(x, g_mlp[i])
        x = x + jnp.square(jax.nn.relu(hm @ w_up[i])) @ w_down[i]
    return x


import jax as _jax
import jax.numpy as _jnp

TWIN_FORMAT = 'train_step'
FWD_PARAMS = ['x', 'mem', 'positions', 'g_mix', 'w_in', 'g_q', 'g_k', 'g_attn_out', 'conv_w', 'conv_b', 'dt_bias', 'a_log', 'd_skip', 'g_ssm_out', 'w_out', 'g_cross', 'g_mem', 'w_cq', 'w_ckv', 'g_cq', 'g_ck', 'w_co', 'g_mlp', 'w_up', 'w_down']
TWIN_WEIGHTS = ['g_mix', 'w_in', 'g_q', 'g_k', 'g_attn_out', 'conv_w', 'conv_b', 'dt_bias', 'a_log', 'd_skip', 'g_ssm_out', 'w_out', 'g_cross', 'g_mem', 'w_cq', 'w_ckv', 'g_cq', 'g_ck', 'w_co', 'g_mlp', 'w_up', 'w_down']
TWIN_DIFF_INPUT = 'x'
TWIN_INPUTS = ['x', 'mem', 'positions', 'g_mix', 'w_in', 'g_q', 'g_k', 'g_attn_out', 'conv_w', 'conv_b', 'dt_bias', 'a_log', 'd_skip', 'g_ssm_out', 'w_out', 'g_cross', 'g_mem', 'w_cq', 'w_ckv', 'g_cq', 'g_ck', 'w_co', 'g_mlp', 'w_up', 'w_down', 'loss_target', 'm_g_mix', 'm_w_in', 'm_g_q', 'm_g_k', 'm_g_attn_out', 'm_conv_w', 'm_conv_b', 'm_dt_bias', 'm_a_log', 'm_d_skip', 'm_g_ssm_out', 'm_w_out', 'm_g_cross', 'm_g_mem', 'm_w_cq', 'm_w_ckv', 'm_g_cq', 'm_g_ck', 'm_w_co', 'm_g_mlp', 'm_w_up', 'm_w_down', 'v_g_mix', 'v_w_in', 'v_g_q', 'v_g_k', 'v_g_attn_out', 'v_conv_w', 'v_conv_b', 'v_dt_bias', 'v_a_log', 'v_d_skip', 'v_g_ssm_out', 'v_w_out', 'v_g_cross', 'v_g_mem', 'v_w_cq', 'v_w_ckv', 'v_g_cq', 'v_g_ck', 'v_w_co', 'v_g_mlp', 'v_w_up', 'v_w_down']
TWIN_OUTPUTS = ['loss', 'grad_x', 'grad_g_mix', 'grad_w_in', 'grad_g_q', 'grad_g_k', 'grad_g_attn_out', 'grad_conv_w', 'grad_conv_b', 'grad_dt_bias', 'grad_a_log', 'grad_d_skip', 'grad_g_ssm_out', 'grad_w_out', 'grad_g_cross', 'grad_g_mem', 'grad_w_cq', 'grad_w_ckv', 'grad_g_cq', 'grad_g_ck', 'grad_w_co', 'grad_g_mlp', 'grad_w_up', 'grad_w_down', 'delta_g_mix', 'delta_w_in', 'delta_g_q', 'delta_g_k', 'delta_g_attn_out', 'delta_conv_w', 'delta_conv_b', 'delta_dt_bias', 'delta_a_log', 'delta_d_skip', 'delta_g_ssm_out', 'delta_w_out', 'delta_g_cross', 'delta_g_mem', 'delta_w_cq', 'delta_w_ckv', 'delta_g_cq', 'delta_g_ck', 'delta_w_co', 'delta_g_mlp', 'delta_w_up', 'delta_w_down', 'new_m_g_mix', 'new_m_w_in', 'new_m_g_q', 'new_m_g_k', 'new_m_g_attn_out', 'new_m_conv_w', 'new_m_conv_b', 'new_m_dt_bias', 'new_m_a_log', 'new_m_d_skip', 'new_m_g_ssm_out', 'new_m_w_out', 'new_m_g_cross', 'new_m_g_mem', 'new_m_w_cq', 'new_m_w_ckv', 'new_m_g_cq', 'new_m_g_ck', 'new_m_w_co', 'new_m_g_mlp', 'new_m_w_up', 'new_m_w_down', 'new_v_g_mix', 'new_v_w_in', 'new_v_g_q', 'new_v_g_k', 'new_v_g_attn_out', 'new_v_conv_w', 'new_v_conv_b', 'new_v_dt_bias', 'new_v_a_log', 'new_v_d_skip', 'new_v_g_ssm_out', 'new_v_w_out', 'new_v_g_cross', 'new_v_g_mem', 'new_v_w_cq', 'new_v_w_ckv', 'new_v_g_cq', 'new_v_g_ck', 'new_v_w_co', 'new_v_g_mlp', 'new_v_w_up', 'new_v_w_down']
TWIN_LEAF_KINDS = {'loss': 'loss', 'grad_x': 'grad_x', 'grad_g_mix': 'grad_w', 'grad_w_in': 'grad_w', 'grad_g_q': 'grad_w', 'grad_g_k': 'grad_w', 'grad_g_attn_out': 'grad_w', 'grad_conv_w': 'grad_w', 'grad_conv_b': 'grad_w', 'grad_dt_bias': 'grad_w', 'grad_a_log': 'grad_w', 'grad_d_skip': 'grad_w', 'grad_g_ssm_out': 'grad_w', 'grad_w_out': 'grad_w', 'grad_g_cross': 'grad_w', 'grad_g_mem': 'grad_w', 'grad_w_cq': 'grad_w', 'grad_w_ckv': 'grad_w', 'grad_g_cq': 'grad_w', 'grad_g_ck': 'grad_w', 'grad_w_co': 'grad_w', 'grad_g_mlp': 'grad_w', 'grad_w_up': 'grad_w', 'grad_w_down': 'grad_w', 'delta_g_mix': 'delta_w', 'delta_w_in': 'delta_w', 'delta_g_q': 'delta_w', 'delta_g_k': 'delta_w', 'delta_g_attn_out': 'delta_w', 'delta_conv_w': 'delta_w', 'delta_conv_b': 'delta_w', 'delta_dt_bias': 'delta_w', 'delta_a_log': 'delta_w', 'delta_d_skip': 'delta_w', 'delta_g_ssm_out': 'delta_w', 'delta_w_out': 'delta_w', 'delta_g_cross': 'delta_w', 'delta_g_mem': 'delta_w', 'delta_w_cq': 'delta_w', 'delta_w_ckv': 'delta_w', 'delta_g_cq': 'delta_w', 'delta_g_ck': 'delta_w', 'delta_w_co': 'delta_w', 'delta_g_mlp': 'delta_w', 'delta_w_up': 'delta_w', 'delta_w_down': 'delta_w', 'new_m_g_mix': 'new_m', 'new_m_w_in': 'new_m', 'new_m_g_q': 'new_m', 'new_m_g_k': 'new_m', 'new_m_g_attn_out': 'new_m', 'new_m_conv_w': 'new_m', 'new_m_conv_b': 'new_m', 'new_m_dt_bias': 'new_m', 'new_m_a_log': 'new_m', 'new_m_d_skip': 'new_m', 'new_m_g_ssm_out': 'new_m', 'new_m_w_out': 'new_m', 'new_m_g_cross': 'new_m', 'new_m_g_mem': 'new_m', 'new_m_w_cq': 'new_m', 'new_m_w_ckv': 'new_m', 'new_m_g_cq': 'new_m', 'new_m_g_ck': 'new_m', 'new_m_w_co': 'new_m', 'new_m_g_mlp': 'new_m', 'new_m_w_up': 'new_m', 'new_m_w_down': 'new_m', 'new_v_g_mix': 'new_v', 'new_v_w_in': 'new_v', 'new_v_g_q': 'new_v', 'new_v_g_k': 'new_v', 'new_v_g_attn_out': 'new_v', 'new_v_conv_w': 'new_v', 'new_v_conv_b': 'new_v', 'new_v_dt_bias': 'new_v', 'new_v_a_log': 'new_v', 'new_v_d_skip': 'new_v', 'new_v_g_ssm_out': 'new_v', 'new_v_w_out': 'new_v', 'new_v_g_cross': 'new_v', 'new_v_g_mem': 'new_v', 'new_v_w_cq': 'new_v', 'new_v_w_ckv': 'new_v', 'new_v_g_cq': 'new_v', 'new_v_g_ck': 'new_v', 'new_v_w_co': 'new_v', 'new_v_g_mlp': 'new_v', 'new_v_w_up': 'new_v', 'new_v_w_down': 'new_v'}


def _forward(args):
    return _fwd_reference(*[args[k] for k in FWD_PARAMS])


def _output_shape():
    out = _jax.eval_shape(lambda: _forward(_fwd_setup_inputs(0)))
    return out.shape, out.dtype

N_MICROBATCH = 1
ADAM_LR = 0.001
ADAM_B1 = 0.9
ADAM_B2 = 0.999
ADAM_EPS = 1e-08
ADAM_WD = 0.01
ADAM_STEP = 10
PER_EXAMPLE_BATCH_AXIS = {'x': 0, 'mem': 0, 'positions': 0, 'loss_target': 0}
SHARED_INPUTS = []
_WEIGHT_DTYPES = {'g_mix': _jnp.float32, 'w_in': _jnp.float32, 'g_q': _jnp.float32, 'g_k': _jnp.float32, 'g_attn_out': _jnp.float32, 'conv_w': _jnp.float32, 'conv_b': _jnp.float32, 'dt_bias': _jnp.float32, 'a_log': _jnp.float32, 'd_skip': _jnp.float32, 'g_ssm_out': _jnp.float32, 'w_out': _jnp.float32, 'g_cross': _jnp.float32, 'g_mem': _jnp.float32, 'w_cq': _jnp.float32, 'w_ckv': _jnp.float32, 'g_cq': _jnp.float32, 'g_ck': _jnp.float32, 'w_co': _jnp.float32, 'g_mlp': _jnp.float32, 'w_up': _jnp.float32, 'w_down': _jnp.float32}
MOMENT_SCALE = {'g_mix': 5.504852e-01, 'w_in': 3.135064e-01, 'g_q': 9.489815e-01, 'g_k': 9.361181e-01, 'g_attn_out': 7.888813e+00, 'conv_w': 5.192273e-01, 'conv_b': 1.791508e+00, 'dt_bias': 6.513492e-01, 'a_log': 2.937040e+00, 'd_skip': 4.150479e+00, 'g_ssm_out': 1.014221e+01, 'w_out': 8.463929e-01, 'g_cross': 3.598480e-02, 'g_mem': 1.998143e-01, 'w_cq': 7.011705e-02, 'w_ckv': 2.627343e-01, 'g_cq': 1.249407e+00, 'g_ck': 1.250065e+00, 'w_co': 1.862732e-01, 'g_mlp': 2.371647e+01, 'w_up': 3.646133e-01, 'w_down': 1.984672e+00}


def _to_microbatches(a, axis):
    t = _jnp.moveaxis(a, axis, 0)
    t = t.reshape((N_MICROBATCH, t.shape[0] // N_MICROBATCH) + t.shape[1:])
    return _jnp.moveaxis(t, 1, axis + 1)


def setup_inputs(seed: int = 0) -> dict:
    inp = _fwd_setup_inputs(seed)
    key = _jax.random.fold_in(_jax.random.key(seed), 7919)
    shape, _ = _output_shape()
    out = dict(inp)
    out["loss_target"] = _jax.random.normal(_jax.random.fold_in(key, 0), shape, _jnp.float32)
    for i, name in enumerate(TWIN_WEIGHTS):
        w = inp[name].astype(_jnp.float32)
        if MOMENT_SCALE is None:
            s = _jnp.sqrt(_jnp.mean(_jnp.square(w)) + 1e-30)
        else:
            s = MOMENT_SCALE[name]
        km, kv = _jax.random.split(_jax.random.fold_in(key, i + 1))
        out[name] = w
        out["m_" + name] = s * _jax.random.normal(km, w.shape, _jnp.float32)
        out["v_" + name] = (s * s) * _jax.random.uniform(kv, w.shape, _jnp.float32, 0.5, 1.5)
    if N_MICROBATCH > 1:
        for name, axis in PER_EXAMPLE_BATCH_AXIS.items():
            out[name] = _to_microbatches(out[name], axis)
    return {'x': out['x'], 'mem': out['mem'], 'positions': out['positions'], 'g_mix': out['g_mix'], 'w_in': out['w_in'], 'g_q': out['g_q'], 'g_k': out['g_k'], 'g_attn_out': out['g_attn_out'], 'conv_w': out['conv_w'], 'conv_b': out['conv_b'], 'dt_bias': out['dt_bias'], 'a_log': out['a_log'], 'd_skip': out['d_skip'], 'g_ssm_out': out['g_ssm_out'], 'w_out': out['w_out'], 'g_cross': out['g_cross'], 'g_mem': out['g_mem'], 'w_cq': out['w_cq'], 'w_ckv': out['w_ckv'], 'g_cq': out['g_cq'], 'g_ck': out['g_ck'], 'w_co': out['w_co'], 'g_mlp': out['g_mlp'], 'w_up': out['w_up'], 'w_down': out['w_down'], 'loss_target': out['loss_target'], 'm_g_mix': out['m_g_mix'], 'm_w_in': out['m_w_in'], 'm_g_q': out['m_g_q'], 'm_g_k': out['m_g_k'], 'm_g_attn_out': out['m_g_attn_out'], 'm_conv_w': out['m_conv_w'], 'm_conv_b': out['m_conv_b'], 'm_dt_bias': out['m_dt_bias'], 'm_a_log': out['m_a_log'], 'm_d_skip': out['m_d_skip'], 'm_g_ssm_out': out['m_g_ssm_out'], 'm_w_out': out['m_w_out'], 'm_g_cross': out['m_g_cross'], 'm_g_mem': out['m_g_mem'], 'm_w_cq': out['m_w_cq'], 'm_w_ckv': out['m_w_ckv'], 'm_g_cq': out['m_g_cq'], 'm_g_ck': out['m_g_ck'], 'm_w_co': out['m_w_co'], 'm_g_mlp': out['m_g_mlp'], 'm_w_up': out['m_w_up'], 'm_w_down': out['m_w_down'], 'v_g_mix': out['v_g_mix'], 'v_w_in': out['v_w_in'], 'v_g_q': out['v_g_q'], 'v_g_k': out['v_g_k'], 'v_g_attn_out': out['v_g_attn_out'], 'v_conv_w': out['v_conv_w'], 'v_conv_b': out['v_conv_b'], 'v_dt_bias': out['v_dt_bias'], 'v_a_log': out['v_a_log'], 'v_d_skip': out['v_d_skip'], 'v_g_ssm_out': out['v_g_ssm_out'], 'v_w_out': out['v_w_out'], 'v_g_cross': out['v_g_cross'], 'v_g_mem': out['v_g_mem'], 'v_w_cq': out['v_w_cq'], 'v_w_ckv': out['v_w_ckv'], 'v_g_cq': out['v_g_cq'], 'v_g_ck': out['v_g_ck'], 'v_w_co': out['v_w_co'], 'v_g_mlp': out['v_g_mlp'], 'v_w_up': out['v_w_up'], 'v_w_down': out['v_w_down']}


def _loss(weights, diff, rest, loss_target):
    with _jax.named_scope("forward"):
        args = {**rest, TWIN_DIFF_INPUT: diff, **{k: w.astype(_WEIGHT_DTYPES[k]) for k, w in weights.items()}}
        y = _forward(args)
    with _jax.named_scope("loss_head"):
        err = _jnp.square(y.astype(_jnp.float32) - loss_target)
        return 0.5 * _jnp.sum(_jnp.mean(err, axis=-1)) if err.ndim else 0.5 * err


def _adamw(w, g, m, v):
    m = ADAM_B1 * m + (1.0 - ADAM_B1) * g
    v = ADAM_B2 * v + (1.0 - ADAM_B2) * _jnp.square(g)
    m_hat = m / (1.0 - ADAM_B1 ** ADAM_STEP)
    v_hat = v / (1.0 - ADAM_B2 ** ADAM_STEP)
    delta = -ADAM_LR * (m_hat / (_jnp.sqrt(v_hat) + ADAM_EPS) + ADAM_WD * w)
    return delta, m, v


def reference(x, mem, positions, g_mix, w_in, g_q, g_k, g_attn_out, conv_w, conv_b, dt_bias, a_log, d_skip, g_ssm_out, w_out, g_cross, g_mem, w_cq, w_ckv, g_cq, g_ck, w_co, g_mlp, w_up, w_down, loss_target, m_g_mix, m_w_in, m_g_q, m_g_k, m_g_attn_out, m_conv_w, m_conv_b, m_dt_bias, m_a_log, m_d_skip, m_g_ssm_out, m_w_out, m_g_cross, m_g_mem, m_w_cq, m_w_ckv, m_g_cq, m_g_ck, m_w_co, m_g_mlp, m_w_up, m_w_down, v_g_mix, v_w_in, v_g_q, v_g_k, v_g_attn_out, v_conv_w, v_conv_b, v_dt_bias, v_a_log, v_d_skip, v_g_ssm_out, v_w_out, v_g_cross, v_g_mem, v_w_cq, v_w_ckv, v_g_cq, v_g_ck, v_w_co, v_g_mlp, v_w_up, v_w_down):
    given = dict(x=x, mem=mem, positions=positions, g_mix=g_mix, w_in=w_in, g_q=g_q, g_k=g_k, g_attn_out=g_attn_out, conv_w=conv_w, conv_b=conv_b, dt_bias=dt_bias, a_log=a_log, d_skip=d_skip, g_ssm_out=g_ssm_out, w_out=w_out, g_cross=g_cross, g_mem=g_mem, w_cq=w_cq, w_ckv=w_ckv, g_cq=g_cq, g_ck=g_ck, w_co=w_co, g_mlp=g_mlp, w_up=w_up, w_down=w_down, loss_target=loss_target, m_g_mix=m_g_mix, m_w_in=m_w_in, m_g_q=m_g_q, m_g_k=m_g_k, m_g_attn_out=m_g_attn_out, m_conv_w=m_conv_w, m_conv_b=m_conv_b, m_dt_bias=m_dt_bias, m_a_log=m_a_log, m_d_skip=m_d_skip, m_g_ssm_out=m_g_ssm_out, m_w_out=m_w_out, m_g_cross=m_g_cross, m_g_mem=m_g_mem, m_w_cq=m_w_cq, m_w_ckv=m_w_ckv, m_g_cq=m_g_cq, m_g_ck=m_g_ck, m_w_co=m_w_co, m_g_mlp=m_g_mlp, m_w_up=m_w_up, m_w_down=m_w_down, v_g_mix=v_g_mix, v_w_in=v_w_in, v_g_q=v_g_q, v_g_k=v_g_k, v_g_attn_out=v_g_attn_out, v_conv_w=v_conv_w, v_conv_b=v_conv_b, v_dt_bias=v_dt_bias, v_a_log=v_a_log, v_d_skip=v_d_skip, v_g_ssm_out=v_g_ssm_out, v_w_out=v_w_out, v_g_cross=v_g_cross, v_g_mem=v_g_mem, v_w_cq=v_w_cq, v_w_ckv=v_w_ckv, v_g_cq=v_g_cq, v_g_ck=v_g_ck, v_w_co=v_w_co, v_g_mlp=v_g_mlp, v_w_up=v_w_up, v_w_down=v_w_down)
    weights = {n: given[n] for n in TWIN_WEIGHTS}
    shared = {n: given[n] for n in SHARED_INPUTS}
    per_example = {n: given[n] for n in ['x', 'mem', 'positions']}
    grad_fn = _jax.value_and_grad(_loss, argnums=(0, 1))

    def one_microbatch(ex, loss_target):
        ex = dict(ex)
        diff = ex.pop(TWIN_DIFF_INPUT)
        return grad_fn(weights, diff, {**shared, **ex}, loss_target)

    if N_MICROBATCH == 1:
        loss, (grad_w, grad_x) = one_microbatch(per_example, given["loss_target"])
    else:
        def body(carry, xs):
            loss_sum, grad_sum = carry
            l_k, (gw_k, gx_k) = one_microbatch(xs[0], xs[1])
            with _jax.named_scope("update"):
                return (loss_sum + l_k, _jax.tree.map(_jnp.add, grad_sum, gw_k)), gx_k

        init = (_jnp.zeros((), _jnp.float32), _jax.tree.map(_jnp.zeros_like, weights))
        (loss, grad_w), grad_x = _jax.lax.scan(body, init, (per_example, given["loss_target"]))
    with _jax.named_scope("update"):
        delta_w, new_m, new_v = {}, {}, {}
        for n in TWIN_WEIGHTS:
            delta_w[n], new_m[n], new_v[n] = _adamw(weights[n], grad_w[n], given["m_" + n], given["v_" + n])
    return (loss, grad_x, *[grad_w[n] for n in TWIN_WEIGHTS], *[delta_w[n] for n in TWIN_WEIGHTS],
            *[new_m[n] for n in TWIN_WEIGHTS], *[new_v[n] for n in TWIN_WEIGHTS])
```

```python
import functools
import math

import numpy as np
import jax
import jax.numpy as jnp
from jax import lax
from jax.experimental import pallas as pl
from jax.experimental.pallas import tpu as pltpu

F32 = jnp.float32
BF = jnp.bfloat16

N_DEV = 8
S = 2048
D = 2048
D_ATTN = 1024
N_HEADS = 16
HEAD = 64
ROT = 16
ROPE_THETA = 500000.0
D_SSM = 1024
D_CONV = 2048
CONV_W = 4
N_MEM = 256
D_CROSS = 512
CROSS_HEAD = 128
D_FF = 8192
D_IN = 6160
D_INP = 6272
EPS = 1e-6
BLK = 128
NB = S // BLK
LANES = 128
NEG = -1e30

ADAM_LR = 0.001
ADAM_B1 = 0.9
ADAM_B2 = 0.999
ADAM_EPS = 1e-08
ADAM_WD = 0.01
ADAM_STEP = 10

VMEM_LIMIT_BYTES = 48 * 1024 * 1024
ROW_TILE = 256


def _params(*sem):
    return pltpu.CompilerParams(dimension_semantics=sem, vmem_limit_bytes=VMEM_LIMIT_BYTES)


def _matmul(a, b, *, mode, name, tm, tn, tk, out_dtypes=(F32,), b_cb=None, out_cb=None,
            extras=(), epilogue=None):
    if mode == "tn":
        kk, m = a.shape
    else:
        m, kk = a.shape
    if b_cb is None:
        br, bc = b.shape
    else:
        br, bc = b.shape[1], N_DEV * b_cb
    n = br if mode == "nt" else bc
    assert m % tm == 0 and n % tn == 0 and kk % tk == 0, (name, m, n, kk)
    nk = kk // tk
    grid = (m // tm, n // tn, nk)

    if mode == "tn":
        a_spec = pl.BlockSpec((tk, tm), lambda i, j, k: (k, i))
    else:
        a_spec = pl.BlockSpec((tm, tk), lambda i, j, k: (i, k))
    if mode == "nt":
        b_blk, b_idx = (tn, tk), (lambda i, j, k: (j, k))
    else:
        b_blk, b_idx = (tk, tn), (lambda i, j, k: (k, j))
    if b_cb is None:
        b_spec = pl.BlockSpec(b_blk, b_idx)
    else:
        tc = b_blk[1]
        assert b_cb % tc == 0
        per = b_cb // tc

        def b_idx3(i, j, k):
            r, c = b_idx(i, j, k)
            return (c // per, r, c % per)
        b_spec = pl.BlockSpec((None,) + b_blk, b_idx3)
    ex_specs = [pl.BlockSpec((tm, tn), lambda i, j, k: (i, j)) for _ in extras]
    if out_cb is None:
        out_shape = [jax.ShapeDtypeStruct((m, n), dt) for dt in out_dtypes]
        out_specs = [pl.BlockSpec((tm, tn), lambda i, j, k: (i, j)) for _ in out_dtypes]
    else:
        assert out_cb % tn == 0
        pero = out_cb // tn
        out_shape = [jax.ShapeDtypeStruct((N_DEV, m, out_cb), dt) for dt in out_dtypes]
        out_specs = [pl.BlockSpec((None, tm, tn), lambda i, j, k: (j // pero, i, j % pero)) for _ in out_dtypes]
    dn = {"nn": (((1,), (0,)), ((), ())), "nt": (((1,), (1,)), ((), ())), "tn": (((0,), (0,)), ((), ()))}[mode]
    ne, no = len(extras), len(out_dtypes)

    def kern(a_ref, b_ref, *rest):
        ex_refs, out_refs, acc_ref = rest[:ne], rest[ne:ne + no], rest[ne + no]
        k = pl.program_id(2)

        @pl.when(k == 0)
        def _():
            acc_ref[...] = jnp.zeros_like(acc_ref)

        acc_ref[...] += lax.dot_general(a_ref[...].astype(BF), b_ref[...].astype(BF), dn,
                                        preferred_element_type=F32)

        @pl.when(k == nk - 1)
        def _():
            acc = acc_ref[...]
            outs = (acc,) if epilogue is None else epilogue(acc, *[r[...] for r in ex_refs])
            for r, o in zip(out_refs, outs):
                r[...] = o.astype(r.dtype)

    res = pl.pallas_call(
        kern, name=name, grid=grid, in_specs=[a_spec, b_spec] + ex_specs, out_specs=out_specs,
        out_shape=out_shape, scratch_shapes=[pltpu.VMEM((tm, tn), F32)],
        compiler_params=_params("parallel", "parallel", "arbitrary"))(a, b, *extras)
    return res[0] if no == 1 else tuple(res)


def _rowwise(body, *, name, nrows, tile, row_ins, full_ins=(), row_outs=(), acc_outs=(), scratch=(),
             reverse=False):
    n = nrows // tile

    def ridx(i):
        return (n - 1 - i) if reverse else i

    in_specs, args = [], []
    for arr, width, cb in row_ins:
        in_specs.append(pl.BlockSpec((tile, width), lambda i, cb=cb: (ridx(i), cb)))
        args.append(arr)
    for arr in full_ins:
        in_specs.append(pl.BlockSpec(arr.shape, lambda i, nd=arr.ndim: (0,) * nd))
        args.append(arr)
    out_shape, out_specs = [], []
    for width, dt in row_outs:
        out_shape.append(jax.ShapeDtypeStruct((nrows, width), dt))
        out_specs.append(pl.BlockSpec((tile, width), lambda i: (ridx(i), 0)))
    for shp, dt in acc_outs:
        out_shape.append(jax.ShapeDtypeStruct(shp, dt))
        out_specs.append(pl.BlockSpec(shp, lambda i, nd=len(shp): (0,) * nd))

    def kern(*refs):
        body(pl.program_id(0), n, *refs)

    return pl.pallas_call(kern, name=name, grid=(n,), in_specs=in_specs, out_specs=out_specs,
                          out_shape=out_shape, scratch_shapes=list(scratch),
                          compiler_params=_params("arbitrary"))(*args)


def _accum(ref, val, i):
    @pl.when(i == 0)
    def _():
        ref[...] = val

    @pl.when(i > 0)
    def _():
        ref[...] += val


def _sigmoid(x):
    return 1.0 / (1.0 + jnp.exp(-x))


def _rms_n(x):
    r = lax.rsqrt(jnp.mean(x * x, axis=-1, keepdims=True) + EPS)
    return x * r, r


def _rms_bwd(x, g, dh):
    n, r = _rms_n(x)
    dn = dh * g
    dx = r * (dn - n * jnp.mean(dn * n, axis=-1, keepdims=True))
    return dx, jnp.sum(dh * n, axis=0, keepdims=True)


def _seg_sum(x, seg):
    t, w = x.shape
    tiles = [x[:, LANES * j:LANES * (j + 1)] for j in range(w // LANES)]
    outs = []
    if seg == 64:
        lo = lax.broadcasted_iota(jnp.int32, (t, LANES), 1) < 64
        for xt in tiles:
            s_lo = jnp.sum(jnp.where(lo, xt, 0.0), axis=-1, keepdims=True)
            s_hi = jnp.sum(jnp.where(lo, 0.0, xt), axis=-1, keepdims=True)
            outs.append(jnp.where(lo, s_lo, s_hi))
    else:
        sums = [jnp.sum(xt, axis=-1, keepdims=True) for xt in tiles]
        if seg == 256:
            sums = [sums[2 * (j // 2)] + sums[2 * (j // 2) + 1] for j in range(len(sums))]
        else:
            assert seg == 128
        outs = [jnp.broadcast_to(s, (t, LANES)) for s in sums]
    return outs[0] if len(outs) == 1 else jnp.concatenate(outs, axis=1)


def _seg_rms_n(x, seg):
    r = lax.rsqrt(_seg_sum(x * x, seg) * (1.0 / seg) + EPS)
    return x * r, r


def _seg_rms_bwd(x, g, dy, seg):
    n, r = _seg_rms_n(x, seg)
    dn = dy * g
    dx = r * (dn - n * (_seg_sum(dn * n, seg) * (1.0 / seg)))
    return dx, jnp.sum(dy * n, axis=0, keepdims=True)


def _rope_tables(pos_col, inv_tab, t):
    ang = pos_col.astype(F32) * inv_tab
    idx = lax.broadcasted_iota(jnp.int32, (t, LANES), 1) % HEAD
    cos, sin = jnp.cos(ang), jnp.sin(ang)
    c = jnp.where(idx < ROT, cos, 1.0)
    sg = jnp.where(idx < ROT // 2, -sin, jnp.where(idx < ROT, sin, 0.0))
    return c, sg, idx < ROT // 2


def _rope(x, c, sg, lo):
    partner = jnp.where(lo, pltpu.roll(x, LANES - ROT // 2, 1), pltpu.roll(x, ROT // 2, 1))
    return x * c + partner * sg


def _fold_heads(v):
    v8 = jnp.broadcast_to(v, (8, LANES))
    return (v8 + pltpu.roll(v8, HEAD, 1))[0:1]


def _dot(a, b, dn):
    return lax.dot_general(a, b, (dn, ((), ())), preferred_element_type=F32)


NN = ((1,), (0,))
NT = ((1,), (1,))
TN = ((0,), (0,))


def _dot3(l01, x):
    x1 = x.astype(BF)
    r1 = x - x1.astype(F32)
    x2 = r1.astype(BF)
    x3 = (r1 - x2.astype(F32)).astype(BF)
    return _dot(l01, x1, NN) + _dot(l01, x2, NN) + _dot(l01, x3, NN)


def _rms_fwd(x, g, name):
    rows = x.shape[0]

    def body(i, n, x_ref, g_ref, o_ref):
        nx, _ = _rms_n(x_ref[...])
        o_ref[...] = (nx * g_ref[...]).astype(BF)

    return _rowwise(body, name=name, nrows=rows, tile=min(ROW_TILE, rows), row_ins=[(x, D, 0)],
                    full_ins=[g], row_outs=[(D, BF)])[0]


def _qk_prep(proj, pos_col, gq128, gk128, inv_tab):
    scale = HEAD ** -0.5

    def body(i, n, q_ref, k_ref, v_ref, p_ref, gq_ref, gk_ref, it_ref, qo_ref, ko_ref, vo_ref):
        t = q_ref.shape[0]
        c, sg, lo = _rope_tables(p_ref[...], it_ref[...], t)
        for j in range(D_ATTN // LANES):
            sl = slice(LANES * j, LANES * (j + 1))
            qn, _ = _seg_rms_n(q_ref[:, sl], HEAD)
            kn, _ = _seg_rms_n(k_ref[:, sl], HEAD)
            qo_ref[:, sl] = (_rope(qn * gq_ref[...], c, sg, lo) * scale).astype(BF)
            ko_ref[:, sl] = _rope(kn * gk_ref[...], c, sg, lo).astype(BF)
        vo_ref[...] = v_ref[...].astype(BF)

    return _rowwise(body, name="qk_prep", nrows=S, tile=ROW_TILE,
                    row_ins=[(proj, D_ATTN, 0), (proj, D_ATTN, 1), (proj, D_ATTN, 2), (pos_col, 1, 0)],
                    full_ins=[gq128, gk128, inv_tab], row_outs=[(D_ATTN, BF)] * 3)


def _band_scores(q, k_ref, b, nb, h_mask):
    row = lax.broadcasted_iota(jnp.int32, (BLK, BLK), 0)
    col = lax.broadcasted_iota(jnp.int32, (BLK, BLK), 1)
    qm = jnp.where(h_mask, q.astype(F32), 0.0).astype(BF)
    kc = k_ref[pl.ds(pl.multiple_of(b * BLK, BLK), BLK), :]
    s_c = jnp.where(col <= row, _dot(qm, kc, NT), NEG)
    if nb == 1:
        return qm, kc, s_c, None, None
    pb = jnp.maximum(b - 1, 0)
    kp = k_ref[pl.ds(pl.multiple_of(pb * BLK, BLK), BLK), :]
    ok = jnp.logical_and(col >= row, (b % nb) != 0)
    s_p = jnp.where(ok, _dot(qm, kp, NT), NEG)
    return qm, kc, s_c, kp, s_p


def _attn_fwd(q, k, v, nb, name):
    def kern(q_ref, k_ref, v_ref, o_ref, l_ref):
        b = pl.program_id(1)
        q = q_ref[...]
        half0 = lax.broadcasted_iota(jnp.int32, (BLK, LANES), 1) < HEAD
        o_h, l_h = [], []
        for h in range(2):
            hm = half0 if h == 0 else jnp.logical_not(half0)
            _, _, s_c, _, s_p = _band_scores(q, k_ref, b, nb, hm)
            m = jnp.max(s_c, axis=-1, keepdims=True)
            if nb > 1:
                m = jnp.maximum(m, jnp.max(s_p, axis=-1, keepdims=True))
            p_c = jnp.exp(s_c - m)
            den = jnp.sum(p_c, axis=-1, keepdims=True)
            vc = v_ref[pl.ds(pl.multiple_of(b * BLK, BLK), BLK), :]
            o = _dot(p_c.astype(BF), vc, NN)
            if nb > 1:
                p_p = jnp.exp(s_p - m)
                den = den + jnp.sum(p_p, axis=-1, keepdims=True)
                pb = jnp.maximum(b - 1, 0)
                vp = v_ref[pl.ds(pl.multiple_of(pb * BLK, BLK), BLK), :]
                o = o + _dot(p_p.astype(BF), vp, NN)
            o_h.append(o / den)
            l_h.append(m + jnp.log(den))
        o_ref[...] = jnp.where(half0, o_h[0], o_h[1])
        l_ref[...] = jnp.where(half0, l_h[0], l_h[1])

    tile = pl.BlockSpec((BLK, LANES), lambda p, b: (b, p))
    seq = pl.BlockSpec((S, LANES), lambda p, b: (0, p))
    return pl.pallas_call(
        kern, name=name, grid=(D_ATTN // LANES, NB), in_specs=[tile, seq, seq], out_specs=[tile, tile],
        out_shape=[jax.ShapeDtypeStruct((S, D_ATTN), F32)] * 2,
        compiler_params=_params("parallel", "arbitrary"))(q, k, v)


def _attn_merge(os_, ls_, g_attn):
    def body(i, n, o1, o2, o3, l1, l2, l3, g_ref, a_ref, lse_ref, an_ref):
        la, lb, lc = l1[...], l2[...], l3[...]
        m = jnp.maximum(jnp.maximum(la, lb), lc)
        ea, eb, ec = jnp.exp(la - m), jnp.exp(lb - m), jnp.exp(lc - m)
        den = ea + eb + ec
        attn = (ea * o1[...] + eb * o2[...] + ec * o3[...]) / den
        a_ref[...] = attn
        lse_ref[...] = m + jnp.log(den)
        nx, _ = _rms_n(attn)
        an_ref[...] = (nx * g_ref[...]).astype(BF)

    return _rowwise(body, name="attn_merge", nrows=S, tile=ROW_TILE,
                    row_ins=[(a, D_ATTN, 0) for a in (*os_, *ls_)], full_ins=[g_attn],
                    row_outs=[(D_ATTN, F32), (D_ATTN, F32), (D_ATTN, BF)])


def _conv_taps(x, w_ref):
    rows = lax.broadcasted_iota(jnp.int32, x.shape, 0)
    shifted = []
    for w in range(CONV_W):
        k = CONV_W - 1 - w
        shifted.append(x if k == 0 else jnp.where(rows >= k, pltpu.roll(x, k, 0), 0.0))
    acc = shifted[0] * w_ref[0:1, :]
    for w in range(1, CONV_W):
        acc = acc + shifted[w] * w_ref[w:w + 1, :]
    return acc, shifted


def _conv_fwd(proj, conv_w, conv_b):
    tc = 256

    def kern(x_ref, w_ref, b_ref, o_ref):
        c, _ = _conv_taps(x_ref[...], w_ref)
        c = c + b_ref[...]
        o_ref[...] = c * _sigmoid(c)

    return pl.pallas_call(
        kern, name="conv_fwd", grid=(D_CONV // tc,),
        in_specs=[pl.BlockSpec((S, tc), lambda c: (0, 4 * D_ATTN // tc + c)),
                  pl.BlockSpec((CONV_W, tc), lambda c: (0, c)), pl.BlockSpec((1, tc), lambda c: (0, c))],
        out_specs=pl.BlockSpec((S, tc), lambda c: (0, c)),
        out_shape=jax.ShapeDtypeStruct((S, D_CONV), F32), compiler_params=_params("parallel"))(proj, conv_w, conv_b)


def _softplus(x):
    return jnp.maximum(x, 0.0) + jnp.log1p(jnp.exp(-jnp.abs(x)))


def _ssd_prep(proj, dt_bias128, a_log128):
    def body(i, n, raw_ref, b_ref, al_ref, dt_ref, a_ref, carry):
        @pl.when(i == 0)
        def _():
            carry[...] = jnp.zeros_like(carry)

        dt = _softplus(raw_ref[...] + b_ref[...])
        da = dt * (-jnp.exp(al_ref[...]))
        tri = (lax.broadcasted_iota(jnp.int32, (BLK, BLK), 0) >= lax.broadcasted_iota(jnp.int32, (BLK, BLK), 1))
        cs = _dot3(tri.astype(BF), da) + carry[0:1, :]
        dt_ref[...] = dt
        a_ref[...] = cs
        carry[...] = jnp.broadcast_to(cs[BLK - 1:BLK, :], carry.shape)

    return _rowwise(body, name="ssd_prep", nrows=S, tile=BLK, row_ins=[(proj, LANES, (D_INP - LANES) // LANES)],
                    full_ins=[dt_bias128, a_log128], row_outs=[(LANES, F32), (LANES, F32)],
                    scratch=[pltpu.VMEM((8, LANES), F32)])


def _ssd_decay(a_col, at_row, causal):
    diff = jnp.where(causal, a_col - at_row, 0.0)
    return jnp.where(causal, jnp.exp(diff), 0.0)


def _ssd_fwd(xbc, a_b, at_r, dt_r):
    def kern(c_ref, b_ref, x_ref, ab_ref, at_ref, dt_ref, y_ref):
        i = pl.program_id(1)
        ci = c_ref[...].astype(BF)
        half0 = lax.broadcasted_iota(jnp.int32, (BLK, LANES), 1) < HEAD
        a_cols = (ab_ref[:, 0:1], ab_ref[:, HEAD:HEAD + 1])
        rowg = i * BLK + lax.broadcasted_iota(jnp.int32, (BLK, BLK), 0)

        def step(j, acc):
            off = pl.multiple_of(j * BLK, BLK)
            bj = b_ref[pl.ds(off, BLK), :].astype(BF)
            xj = x_ref[pl.ds(off, BLK), :]
            cb = _dot(ci, bj, NT)
            causal = (j * BLK + lax.broadcasted_iota(jnp.int32, (BLK, BLK), 1)) <= rowg
            at, dtj = at_ref[j], dt_ref[j]
            for h in range(2):
                hm = half0 if h == 0 else jnp.logical_not(half0)
                w = cb * _ssd_decay(a_cols[h], at[h:h + 1, :], causal) * dtj[h:h + 1, :]
                acc = acc + _dot(w.astype(BF), jnp.where(hm, xj, 0.0).astype(BF), NN)
            return acc

        y_ref[...] = lax.fori_loop(0, i + 1, step, jnp.zeros((BLK, LANES), F32))

    npair = D_SSM // LANES
    rowvec = pl.BlockSpec((None, NB, 2, BLK), lambda p, i: (p, 0, 0, 0))
    return pl.pallas_call(
        kern, name="ssd_fwd", grid=(npair, NB),
        in_specs=[pl.BlockSpec((BLK, LANES), lambda p, i: (i, 12 + p // 2)),
                  pl.BlockSpec((S, LANES), lambda p, i: (0, 8 + p // 2)),
                  pl.BlockSpec((S, LANES), lambda p, i: (0, p)),
                  pl.BlockSpec((BLK, LANES), lambda p, i: (i, p)), rowvec, rowvec],
        out_specs=pl.BlockSpec((BLK, LANES), lambda p, i: (i, p)),
        out_shape=jax.ShapeDtypeStruct((S, D_SSM), F32),
        compiler_params=_params("parallel", "arbitrary"))(xbc, xbc, xbc, a_b, at_r, dt_r)


def _ssd_post(y_ssd, xbc, proj, dskip_b, g_ssm):
    def body(i, n, y_ref, xs_ref, z_ref, d_ref, g_ref, o_ref):
        z = z_ref[...]
        y2 = (y_ref[...] + d_ref[...] * xs_ref[...]) * (z * _sigmoid(z))
        nx, _ = _seg_rms_n(y2, 256)
        o_ref[...] = (nx * g_ref[...]).astype(BF)

    return _rowwise(body, name="ssd_post", nrows=S, tile=ROW_TILE,
                    row_ins=[(y_ssd, D_SSM, 0), (xbc, D_SSM, 0), (proj, D_SSM, 3)], full_ins=[dskip_b, g_ssm],
                    row_outs=[(D_SSM, BF)])[0]


def _cross_heads(q, kv_ref, gq, gk):
    out = []
    for h in range(D_CROSS // CROSS_HEAD):
        sl = slice(CROSS_HEAD * h, CROSS_HEAD * (h + 1))
        nq, rq = _rms_n(q[:, sl])
        nk, rk = _rms_n(kv_ref[:, sl])
        v = kv_ref[:, D_CROSS + CROSS_HEAD * h:D_CROSS + CROSS_HEAD * (h + 1)]
        out.append((sl, nq, rq, nk, rk, v))
    return out


def _cross_fwd(qc, kv, g_cq, g_ck):
    scale = CROSS_HEAD ** -0.5

    def body(i, n, q_ref, kv_ref, gq_ref, gk_ref, o_ref):
        for sl, nq, _, nk, _, v in _cross_heads(q_ref[...], kv_ref, gq_ref[...], gk_ref[...]):
            qn = (nq * gq_ref[...] * scale).astype(BF)
            kn = (nk * gk_ref[...]).astype(BF)
            s = _dot(qn, kn, NT)
            e = jnp.exp(s - jnp.max(s, axis=-1, keepdims=True))
            p = e / jnp.sum(e, axis=-1, keepdims=True)
            o_ref[:, sl] = _dot(p.astype(BF), v.astype(BF), NN).astype(BF)

    return _rowwise(body, name="cross_fwd", nrows=S, tile=ROW_TILE, row_ins=[(qc, D_CROSS, 0)],
                    full_ins=[kv, g_cq, g_ck], row_outs=[(D_CROSS, BF)])[0]


def _loss_sum(dy):
    def body(i, n, d_ref, o_ref):
        d = d_ref[...]
        s = jnp.sum(jnp.sum(d * d, axis=0, keepdims=True), axis=1, keepdims=True)
        _accum(o_ref, s, i)

    return _rowwise(body, name="loss_sum", nrows=S, tile=ROW_TILE, row_ins=[(dy, D, 0)],
                    acc_outs=[((1, 1), F32)])[0]


def _rms_bwd_call(x, g, dh, dres, name):
    rows = x.shape[0]
    has_res = dres is not None

    def body(i, n, *refs):
        if has_res:
            x_ref, dh_ref, dr_ref, g_ref, dx_ref, dg_ref = refs
        else:
            x_ref, dh_ref, g_ref, dg_ref = refs
        dx, dg = _rms_bwd(x_ref[...], g_ref[...], dh_ref[...])
        if has_res:
            dx_ref[...] = dx + dr_ref[...]
        _accum(dg_ref, dg, i)

    row_ins = [(x, D, 0), (dh, D, 0)] + ([(dres, D, 0)] if has_res else [])
    return _rowwise(body, name=name, nrows=rows, tile=min(ROW_TILE, rows), row_ins=row_ins, full_ins=[g],
                    row_outs=[(D, F32)] if has_res else [], acc_outs=[((1, D), F32)])


def _cross_bwd(qc, doc, kv, g_cq, g_ck):
    scale = CROSS_HEAD ** -0.5

    def body(i, n, q_ref, do_ref, kv_ref, gq_ref, gk_ref, dq_ref, dkn_ref, dv_ref, dgq_ref):
        dgq = jnp.zeros((1, CROSS_HEAD), F32)
        dkn_parts, dv_parts = [], []
        for sl, nq, rq, nk, _, v in _cross_heads(q_ref[...], kv_ref, gq_ref[...], gk_ref[...]):
            qn = (nq * gq_ref[...] * scale).astype(BF)
            kn = (nk * gk_ref[...]).astype(BF)
            s = _dot(qn, kn, NT)
            e = jnp.exp(s - jnp.max(s, axis=-1, keepdims=True))
            p = e / jnp.sum(e, axis=-1, keepdims=True)
            do = do_ref[:, sl].astype(BF)
            dv_parts.append(_dot(p.astype(BF), do, TN))
            dp = _dot(do, v.astype(BF), NT)
            ds = (p * (dp - jnp.sum(dp * p, axis=-1, keepdims=True))).astype(BF)
            dqn = _dot(ds, kn, NN)
            dkn_parts.append(_dot(ds, qn, TN))
            dn = dqn * (gq_ref[...] * scale)
            dq_ref[:, sl] = (rq * (dn - nq * jnp.mean(dn * nq, axis=-1, keepdims=True))).astype(BF)
            dgq = dgq + jnp.sum(dqn * scale * nq, axis=0, keepdims=True)
        _accum(dkn_ref, jnp.concatenate(dkn_parts, axis=1), i)
        _accum(dv_ref, jnp.concatenate(dv_parts, axis=1), i)
        _accum(dgq_ref, dgq, i)

    return _rowwise(body, name="cross_bwd", nrows=S, tile=ROW_TILE, row_ins=[(qc, D_CROSS, 0), (doc, D_CROSS, 0)],
                    full_ins=[kv, g_cq, g_ck], row_outs=[(D_CROSS, BF)],
                    acc_outs=[((N_MEM, D_CROSS), F32), ((N_MEM, D_CROSS), F32), ((1, CROSS_HEAD), F32)])


def _cross_kv_bwd(kv, dkn, dv, g_ck):
    def body(i, n, kv_ref, dkn_ref, dv_ref, gk_ref, dkv_ref, dgk_ref):
        dgk = jnp.zeros((1, CROSS_HEAD), F32)
        for h in range(D_CROSS // CROSS_HEAD):
            sl = slice(CROSS_HEAD * h, CROSS_HEAD * (h + 1))
            dk, dg = _rms_bwd(kv_ref[:, sl], gk_ref[...], dkn_ref[:, sl])
            dkv_ref[:, sl] = dk.astype(BF)
            dgk = dgk + dg
        dkv_ref[:, D_CROSS:] = dv_ref[...].astype(BF)
        dgk_ref[...] = dgk

    return _rowwise(body, name="cross_kv_bwd", nrows=N_MEM, tile=N_MEM,
                    row_ins=[(kv, 2 * D_CROSS, 0), (dkn, D_CROSS, 0), (dv, D_CROSS, 0)], full_ins=[g_ck],
                    row_outs=[(2 * D_CROSS, BF)], acc_outs=[((1, CROSS_HEAD), F32)])


def _attn_merge_bwd(dmix, attn, g_attn):
    def body(i, n, dn_ref, a_ref, g_ref, da_ref, dl_ref, dg_ref):
        attn_v = a_ref[...]
        da, dg = _rms_bwd(attn_v, g_ref[...], dn_ref[...])
        da_ref[...] = da.astype(BF)
        dl_ref[...] = _seg_sum(da * attn_v, HEAD)
        _accum(dg_ref, dg, i)

    return _rowwise(body, name="attn_merge_bwd", nrows=S, tile=ROW_TILE,
                    row_ins=[(dmix, D_ATTN, 0), (attn, D_ATTN, 0)], full_ins=[g_attn],
                    row_outs=[(D_ATTN, BF), (D_ATTN, F32)], acc_outs=[((1, D_ATTN), F32)])


def _attn_bwd(q, k, v, do, lse, delta, nb, name):
    def kern(q_ref, k_ref, v_ref, do_ref, l_ref, d_ref, dq_ref, dk_ref, dv_ref):
        b = pl.program_id(1)

        @pl.when(b == 0)
        def _():
            dk_ref[...] = jnp.zeros_like(dk_ref)
            dv_ref[...] = jnp.zeros_like(dv_ref)

        q, do = q_ref[...], do_ref[...]
        half0 = lax.broadcasted_iota(jnp.int32, (BLK, LANES), 1) < HEAD
        cur = pl.ds(pl.multiple_of(b * BLK, BLK), BLK)
        prev = pl.ds(pl.multiple_of(jnp.maximum(b - 1, 0) * BLK, BLK), BLK)
        vc = v_ref[cur, :]
        dq_h = []
        dk_c = jnp.zeros((BLK, LANES), F32)
        dv_c = jnp.zeros((BLK, LANES), F32)
        dk_p = jnp.zeros((BLK, LANES), F32)
        dv_p = jnp.zeros((BLK, LANES), F32)
        for h in range(2):
            hm = half0 if h == 0 else jnp.logical_not(half0)
            lse_h = l_ref[:, HEAD * h:HEAD * h + 1]
            del_h = d_ref[:, HEAD * h:HEAD * h + 1]
            qm, kc, s_c, kp, s_p = _band_scores(q, k_ref, b, nb, hm)
            dom = jnp.where(hm, do.astype(F32), 0.0).astype(BF)
            p_c = jnp.exp(s_c - lse_h)
            ds_c = (p_c * (_dot(dom, vc, NT) - del_h)).astype(BF)
            dq = _dot(ds_c, kc, NN)
            dk_c = dk_c + _dot(ds_c, qm, TN)
            dv_c = dv_c + _dot(p_c.astype(BF), dom, TN)
            if nb > 1:
                vp = v_ref[prev, :]
                p_p = jnp.exp(s_p - lse_h)
                ds_p = (p_p * (_dot(dom, vp, NT) - del_h)).astype(BF)
                dq = dq + _dot(ds_p, kp, NN)
                dk_p = dk_p + _dot(ds_p, qm, TN)
                dv_p = dv_p + _dot(p_p.astype(BF), dom, TN)
            dq_h.append(dq)
        dq_ref[...] = jnp.where(half0, dq_h[0], dq_h[1])
        dk_ref[cur, :] += dk_c
        dv_ref[cur, :] += dv_c
        if nb > 1:
            dk_ref[prev, :] += dk_p
            dv_ref[prev, :] += dv_p

    tile = pl.BlockSpec((BLK, LANES), lambda p, b: (b, p))
    seq = pl.BlockSpec((S, LANES), lambda p, b: (0, p))
    return pl.pallas_call(
        kern, name=name, grid=(D_ATTN // LANES, NB), in_specs=[tile, seq, seq, tile, tile, tile],
        out_specs=[tile, seq, seq], out_shape=[jax.ShapeDtypeStruct((S, D_ATTN), F32)] * 3,
        compiler_params=_params("parallel", "arbitrary"))(q, k, v, do, lse, delta)


def _qk_bwd(dqs, dks, dvs, proj, pos_col, gq128, gk128, inv_tab):
    scale = HEAD ** -0.5

    def body(i, n, *refs):
        dq_refs, dk_refs, dv_refs = refs[0:3], refs[3:6], refs[6:9]
        q_ref, k_ref, p_ref, gq_ref, gk_ref, it_ref = refs[9:15]
        dqo_ref, dko_ref, dvo_ref, dgq_ref, dgk_ref = refs[15:20]
        t = q_ref.shape[0]
        c, sg, lo = _rope_tables(p_ref[...], it_ref[...], t)
        dgq = jnp.zeros((1, LANES), F32)
        dgk = jnp.zeros((1, LANES), F32)
        for j in range(D_ATTN // LANES):
            sl = slice(LANES * j, LANES * (j + 1))
            dqr = _rope(dq_refs[0][:, sl] + dq_refs[1][:, sl] + dq_refs[2][:, sl], c, -sg, lo) * scale
            dkr = _rope(dk_refs[0][:, sl] + dk_refs[1][:, sl] + dk_refs[2][:, sl], c, -sg, lo)
            dq, gq = _seg_rms_bwd(q_ref[:, sl], gq_ref[...], dqr, HEAD)
            dk, gk = _seg_rms_bwd(k_ref[:, sl], gk_ref[...], dkr, HEAD)
            dqo_ref[:, sl] = dq.astype(BF)
            dko_ref[:, sl] = dk.astype(BF)
            dgq, dgk = dgq + gq, dgk + gk
        dvo_ref[...] = (dv_refs[0][...] + dv_refs[1][...] + dv_refs[2][...]).astype(BF)
        _accum(dgq_ref, _fold_heads(dgq), i)
        _accum(dgk_ref, _fold_heads(dgk), i)

    return _rowwise(body, name="qk_bwd", nrows=S, tile=ROW_TILE,
                    row_ins=[(a, D_ATTN, 0) for a in (*dqs, *dks, *dvs)]
                    + [(proj, D_ATTN, 0), (proj, D_ATTN, 1), (pos_col, 1, 0)],
                    full_ins=[gq128, gk128, inv_tab], row_outs=[(D_ATTN, BF)] * 3,
                    acc_outs=[((1, LANES), F32)] * 2)


def _ssd_post_bwd(y_ssd, xbc, proj, dmix, dskip_b, g_ssm):
    def body(i, n, y_ref, xs_ref, z_ref, do_ref, d_ref, g_ref, dy_ref, dz_ref, dxs_ref, dd_ref, dg_ref):
        z, xs = z_ref[...], xs_ref[...]
        sg = _sigmoid(z)
        gate = z * sg
        y = y_ref[...] + d_ref[...] * xs
        dy2, dg = _seg_rms_bwd(y * gate, g_ref[...], do_ref[...], 256)
        dy = dy2 * gate
        dy_ref[...] = dy.astype(BF)
        dz_ref[...] = (dy2 * y * (sg * (1.0 + z * (1.0 - sg)))).astype(BF)
        dxs_ref[...] = dy * d_ref[...]
        _accum(dd_ref, jnp.sum(dy * xs, axis=0, keepdims=True), i)
        _accum(dg_ref, dg, i)

    return _rowwise(body, name="ssd_post_bwd", nrows=S, tile=ROW_TILE,
                    row_ins=[(y_ssd, D_SSM, 0), (xbc, D_SSM, 0), (proj, D_SSM, 3), (dmix, D_SSM, 1)],
                    full_ins=[dskip_b, g_ssm], row_outs=[(D_SSM, BF), (D_SSM, BF), (D_SSM, F32)],
                    acc_outs=[((1, D_SSM), F32), ((1, D_SSM), F32)])


def _ssd_bwd(xbc, dy, a_b, at_r, dt_r):
    def kern(c_ref, b_ref, x_ref, dy_ref, ab_ref, at_ref, dt_ref,
             dc_ref, daq_ref, dx_ref, db_ref, dak_ref, ddt_ref):
        i = pl.program_id(1)

        @pl.when(i == 0)
        def _():
            dx_ref[...] = jnp.zeros_like(dx_ref)
            db_ref[...] = jnp.zeros_like(db_ref)
            dak_ref[...] = jnp.zeros_like(dak_ref)
            ddt_ref[...] = jnp.zeros_like(ddt_ref)

        ci = c_ref[...].astype(BF)
        dyv = dy_ref[...]
        half0 = lax.broadcasted_iota(jnp.int32, (BLK, LANES), 1) < HEAD
        hms = (half0, jnp.logical_not(half0))
        dym = [jnp.where(hm, dyv.astype(F32), 0.0).astype(BF) for hm in hms]
        a_cols = (ab_ref[:, 0:1], ab_ref[:, HEAD:HEAD + 1])
        rowg = i * BLK + lax.broadcasted_iota(jnp.int32, (BLK, BLK), 0)

        def step(j, carry):
            dc_acc, r0, r1 = carry
            rows = [r0, r1]
            off = pl.multiple_of(j * BLK, BLK)
            bj = b_ref[pl.ds(off, BLK), :].astype(BF)
            xj = x_ref[pl.ds(off, BLK), :]
            cb = _dot(ci, bj, NT)
            causal = (j * BLK + lax.broadcasted_iota(jnp.int32, (BLK, BLK), 1)) <= rowg
            at, dtj = at_ref[j], dt_ref[j]
            dcb = jnp.zeros((BLK, BLK), F32)
            dxj = jnp.zeros((BLK, LANES), F32)
            for h in range(2):
                lm = _ssd_decay(a_cols[h], at[h:h + 1, :], causal)
                dth = dtj[h:h + 1, :]
                dw = _dot(dym[h], jnp.where(hms[h], xj, 0.0).astype(BF), NT)
                g = dw * cb * lm
                w = cb * lm * dth
                dxj = dxj + _dot(w.astype(BF), dym[h], TN)
                gcol = jnp.sum(g, axis=0, keepdims=True)
                ddt_ref[j, h:h + 1, :] += gcol
                dak_ref[j, h:h + 1, :] += gcol * dth
                rows[h] = rows[h] + jnp.sum(g * dth, axis=1, keepdims=True)
                dcb = dcb + dw * lm * dth
            dcb = dcb.astype(BF)
            dx_ref[pl.ds(off, BLK), :] += dxj
            db_ref[pl.ds(off, BLK), :] += _dot(dcb, ci, TN)
            return dc_acc + _dot(dcb, bj, NN), rows[0], rows[1]

        zcol = jnp.zeros((BLK, 1), F32)
        dc, r0, r1 = lax.fori_loop(0, i + 1, step, (jnp.zeros((BLK, LANES), F32), zcol, zcol))
        dc_ref[...] = dc
        daq_ref[...] = jnp.where(half0, r0, r1)

    npair = D_SSM // LANES
    rowvec = pl.BlockSpec((None, NB, 2, BLK), lambda p, i: (p, 0, 0, 0))
    tile = pl.BlockSpec((BLK, LANES), lambda p, i: (i, p))
    seq = pl.BlockSpec((S, LANES), lambda p, i: (0, p))
    return pl.pallas_call(
        kern, name="ssd_bwd", grid=(npair, NB),
        in_specs=[pl.BlockSpec((BLK, LANES), lambda p, i: (i, 12 + p // 2)),
                  pl.BlockSpec((S, LANES), lambda p, i: (0, 8 + p // 2)), seq, tile, tile, rowvec, rowvec],
        out_specs=[tile, tile, seq, seq, rowvec, rowvec],
        out_shape=[jax.ShapeDtypeStruct((S, D_SSM), F32)] * 4
        + [jax.ShapeDtypeStruct((npair, NB, 2, BLK), F32)] * 2,
        compiler_params=_params("parallel", "arbitrary"))(xbc, xbc, xbc, dy, a_b, at_r, dt_r)


def _ssd_prep_bwd(daq, dak, ddt_direct, dt, proj, dt_bias128, a_log128):
    def body(i, n, daq_ref, dak_ref, dd_ref, dt_ref, raw_ref, b_ref, al_ref, draw_ref, dal_ref, db_ref, carry):
        @pl.when(i == 0)
        def _():
            carry[...] = jnp.zeros_like(carry)

        a = -jnp.exp(al_ref[...])
        tri = (lax.broadcasted_iota(jnp.int32, (BLK, BLK), 0) <= lax.broadcasted_iota(jnp.int32, (BLK, BLK), 1))
        rev = _dot3(tri.astype(BF), daq_ref[...] - dak_ref[...]) + carry[0:1, :]
        carry[...] = jnp.broadcast_to(rev[0:1, :], carry.shape)
        dtv = dt_ref[...]
        draw = (dd_ref[...] + a * rev) * _sigmoid(raw_ref[...] + b_ref[...])
        draw_ref[...] = draw.astype(BF)
        _accum(dal_ref, jnp.sum(dtv * rev, axis=0, keepdims=True) * a, i)
        _accum(db_ref, jnp.sum(draw, axis=0, keepdims=True), i)

    return _rowwise(body, name="ssd_prep_bwd", nrows=S, tile=BLK, reverse=True,
                    row_ins=[(daq, LANES, 0), (dak, LANES, 0), (ddt_direct, LANES, 0), (dt, LANES, 0),
                             (proj, LANES, (D_INP - LANES) // LANES)],
                    full_ins=[dt_bias128, a_log128], row_outs=[(LANES, BF)],
                    acc_outs=[((1, LANES), F32), ((1, LANES), F32)], scratch=[pltpu.VMEM((8, LANES), F32)])


def _conv_bwd(proj, conv_w, conv_b, d1, d2, map1, map2, col0, ntiles, name):
    base = (4 * D_ATTN + col0) // LANES

    def kern(x_ref, w_ref, b_ref, d1_ref, d2_ref, dx_ref, dw_ref, db_ref):
        x = x_ref[...]
        c, shifted = _conv_taps(x, w_ref)
        c = c + b_ref[...]
        sg = _sigmoid(c)
        dc = (d1_ref[...] + d2_ref[...]) * (sg * (1.0 + c * (1.0 - sg)))
        rows = lax.broadcasted_iota(jnp.int32, x.shape, 0)
        dx = jnp.zeros_like(x)
        for w in range(CONV_W):
            k = CONV_W - 1 - w
            up = dc if k == 0 else jnp.where(rows < S - k, pltpu.roll(dc, S - k, 0), 0.0)
            dx = dx + up * w_ref[w:w + 1, :]
            dw_ref[w:w + 1, :] = jnp.sum(dc * shifted[w], axis=0, keepdims=True)
        dx_ref[...] = dx.astype(BF)
        db_ref[...] = jnp.sum(dc, axis=0, keepdims=True)

    c0 = col0 // LANES
    return pl.pallas_call(
        kern, name=name, grid=(ntiles,),
        in_specs=[pl.BlockSpec((S, LANES), lambda c: (0, base + c)),
                  pl.BlockSpec((CONV_W, LANES), lambda c: (0, c0 + c)),
                  pl.BlockSpec((1, LANES), lambda c: (0, c0 + c)),
                  pl.BlockSpec((S, LANES), lambda c: (0, map1(c))),
                  pl.BlockSpec((S, LANES), lambda c: (0, map2(c)))],
        out_specs=[pl.BlockSpec((S, LANES), lambda c: (0, c)), pl.BlockSpec((CONV_W, LANES), lambda c: (0, c)),
                   pl.BlockSpec((1, LANES), lambda c: (0, c))],
        out_shape=[jax.ShapeDtypeStruct((S, ntiles * LANES), BF), jax.ShapeDtypeStruct((CONV_W, ntiles * LANES), F32),
                   jax.ShapeDtypeStruct((1, ntiles * LANES), F32)],
        compiler_params=_params("parallel"))(proj, conv_w, conv_b, d1, d2)


def _adamw(w, m, v, parts, name):
    r, c = w.shape
    tile = r if r <= 512 else (128 if c > 1024 else 256)
    assert r % tile == 0
    c1 = 1.0 - ADAM_B1 ** ADAM_STEP
    c2 = 1.0 - ADAM_B2 ** ADAM_STEP

    def kern(w_ref, m_ref, v_ref, p_ref, g_ref, d_ref, mo_ref, vo_ref):
        g = p_ref[0].astype(F32)
        for k in range(1, N_DEV):
            g = g + p_ref[k].astype(F32)
        mn = ADAM_B1 * m_ref[...] + (1.0 - ADAM_B1) * g
        vn = ADAM_B2 * v_ref[...] + (1.0 - ADAM_B2) * (g * g)
        g_ref[...] = g
        mo_ref[...] = mn
        vo_ref[...] = vn
        d_ref[...] = -ADAM_LR * ((mn / c1) / (jnp.sqrt(vn / c2) + ADAM_EPS) + ADAM_WD * w_ref[...])

    blk = pl.BlockSpec((tile, c), lambda i: (i, 0))
    return pl.pallas_call(
        kern, name=name, grid=(r // tile,),
        in_specs=[blk, blk, blk, pl.BlockSpec((N_DEV, tile, c), lambda i: (0, i, 0))],
        out_specs=[blk] * 4, out_shape=[jax.ShapeDtypeStruct((r, c), F32)] * 4,
        compiler_params=_params("parallel"))(w, m, v, parts)


MESH = pl.DeviceIdType.MESH
ANY = pl.BlockSpec(memory_space=pl.ANY)


def _all_gather(arrs, name):
    na = len(arrs)

    def kern(*refs):
        ins, outs = refs[:na], refs[na:2 * na]
        send_sems, recv_sems, local_sems = refs[2 * na:]
        x, y, c = lax.axis_index("x"), lax.axis_index("y"), lax.axis_index("c")
        me, sibling = (x, y, c), (x, y, 1 - c)
        chips = [(1 - x, y), (x, 1 - y), (1 - x, 1 - y)]

        def copy(a, k, block, to, src=None):
            px, py, pc = block
            dst = outs[a].at[4 * px + 2 * py + pc]
            return pltpu.make_async_remote_copy(
                src_ref=dst if src is None else src, dst_ref=dst, send_sem=send_sems.at[a, k],
                recv_sem=recv_sems.at[a, k], device_id=to, device_id_type=MESH)

        mine = [pltpu.make_async_copy(ins[a], outs[a].at[4 * x + 2 * y + c], local_sems.at[a]) for a in range(na)]
        for cp in mine:
            cp.start()
        first = []
        for a in range(na):
            first.append(copy(a, 0, me, sibling, src=ins[a]))
            first += [copy(a, 1 + j, me, (*chip, c), src=ins[a]) for j, chip in enumerate(chips)]
        for cp in first:
            cp.start()
        passed = []
        for a in range(na):
            for j, chip in enumerate(chips):
                copy(a, 1 + j, (*chip, c), me).wait_recv()
                fwd = copy(a, 4 + j, (*chip, c), sibling)
                fwd.start()
                passed.append(fwd)
        for a in range(na):
            copy(a, 0, sibling, me).wait_recv()
            for j, chip in enumerate(chips):
                copy(a, 4 + j, (*chip, 1 - c), me).wait_recv()
        for cp in first + passed:
            cp.wait_send()
        for cp in mine:
            cp.wait()

    return pl.pallas_call(
        kern, name=name, in_specs=[ANY] * na, out_specs=[ANY] * na,
        out_shape=[jax.ShapeDtypeStruct((N_DEV,) + a.shape, a.dtype) for a in arrs],
        scratch_shapes=[pltpu.SemaphoreType.DMA((na, 7)), pltpu.SemaphoreType.DMA((na, 7)),
                        pltpu.SemaphoreType.DMA((na,))])(*arrs)


def _reduce_scatter_send(parts, name):
    na = len(parts)
    flips = [(dx, dy, dc) for dx in (0, 1) for dy in (0, 1) for dc in (0, 1)][1:]

    def kern(*refs):
        ins, outs = refs[:na], refs[na:2 * na]
        send_sems, recv_sems, local_sems = refs[2 * na:]
        x, y, c = lax.axis_index("x"), lax.axis_index("y"), lax.axis_index("c")
        my = 4 * x + 2 * y + c
        peers = [((1 - x) if dx else x, (1 - y) if dy else y, (1 - c) if dc else c) for dx, dy, dc in flips]

        def copy(a, k):
            px, py, pc = peers[k]
            return pltpu.make_async_remote_copy(
                src_ref=ins[a].at[4 * px + 2 * py + pc], dst_ref=outs[a].at[my], send_sem=send_sems.at[a, k],
                recv_sem=recv_sems.at[a, k], device_id=peers[k], device_id_type=MESH)

        def landing(a, k):
            px, py, pc = peers[k]
            slot = outs[a].at[4 * px + 2 * py + pc]
            return pltpu.make_async_remote_copy(
                src_ref=slot, dst_ref=slot, send_sem=send_sems.at[a, k], recv_sem=recv_sems.at[a, k],
                device_id=peers[k], device_id_type=MESH)

        mine = [pltpu.make_async_copy(ins[a].at[my], outs[a].at[my], local_sems.at[a]) for a in range(na)]
        for cp in mine:
            cp.start()
        sends = [copy(a, k) for a in range(na) for k in range(7)]
        for cp in sends:
            cp.start()
        for a in range(na):
            for k in range(7):
                landing(a, k).wait_recv()
        for cp in sends:
            cp.wait_send()
        for cp in mine:
            cp.wait()

    return pl.pallas_call(
        kern, name=name, in_specs=[ANY] * na, out_specs=[ANY] * na,
        out_shape=[jax.ShapeDtypeStruct(p.shape, p.dtype) for p in parts],
        scratch_shapes=[pltpu.SemaphoreType.DMA((na, 7)), pltpu.SemaphoreType.DMA((na, 7)),
                        pltpu.SemaphoreType.DMA((na,))])(*parts)


def _perm(t, d):
    return t if d == 1 else t.reshape(S // d, d, t.shape[-1]).transpose(1, 0, 2).reshape(S, t.shape[-1])


def _unperm(t, d):
    return t if d == 1 else t.reshape(d, S // d, t.shape[-1]).transpose(1, 0, 2).reshape(S, t.shape[-1])


def _pad_lanes(v, width=LANES):
    return jnp.pad(v, ((0, 0), (0, width - v.shape[1])))


def _row_layout(t16):
    return t16.T.reshape(N_HEADS // 2, 2, NB, BLK).transpose(0, 2, 1, 3)


def _row_layout_inv(r):
    return r.transpose(0, 2, 1, 3).reshape(N_HEADS, S).T


SMALL = (("g_mix", D), ("g_q", HEAD), ("g_k", HEAD), ("g_attn_out", D_ATTN), ("conv_b", D_CONV),
         ("dt_bias", 16), ("a_log", 16), ("d_skip", 16), ("g_ssm_out", D_SSM), ("g_cross", D),
         ("g_mem", D), ("g_cq", CROSS_HEAD), ("g_ck", CROSS_HEAD), ("g_mlp", D))
SMALL_ROWS = -(-sum(-(-w // LANES) for _, w in SMALL) // 8) * 8


def _pack_small(vals):
    rows = [_pad_lanes(vals[n], -(-w // LANES) * LANES).reshape(-1, LANES) for n, w in SMALL]
    packed = jnp.concatenate(rows, axis=0)
    return jnp.pad(packed, ((0, SMALL_ROWS - packed.shape[0]), (0, 0)))


def _unpack_small(packed):
    out, r = {}, 0
    for n, w in SMALL:
        nr = -(-w // LANES)
        out[n] = packed[r:r + nr].reshape(1, nr * LANES)[:, :w]
        r += nr
    return out


BIG = ("w_in", "w_out", "w_cq", "w_ckv", "w_co", "w_up", "w_down")
WEIGHTS = ("g_mix", "w_in", "g_q", "g_k", "g_attn_out", "conv_w", "conv_b", "dt_bias", "a_log", "d_skip",
           "g_ssm_out", "w_out", "g_cross", "g_mem", "w_cq", "w_ckv", "g_cq", "g_ck", "w_co", "g_mlp", "w_up", "w_down")


def _local_step(x, mem, pos_col, target, p, wf):
    inv = np.zeros((1, LANES), np.float32)
    freq = (ROPE_THETA ** (-2.0 * np.arange(ROT // 2, dtype=np.float32) / ROT)).astype(np.float32)
    for l in range(LANES):
        if l % HEAD < ROT:
            inv[0, l] = freq[(l % HEAD) % (ROT // 2)]
    inv_tab = jnp.asarray(inv)
    gq128, gk128 = jnp.tile(p["g_q"], (1, 2)), jnp.tile(p["g_k"], (1, 2))
    dtb128, alog128 = _pad_lanes(p["dt_bias"]), _pad_lanes(p["a_log"])
    dskip_b = jnp.repeat(p["d_skip"], HEAD, axis=1)
    conv_w, conv_b = wf["conv_w"], p["conv_b"]

    h = _rms_fwd(x, p["g_mix"], "rms_mix")
    proj = _matmul(h, wf["w_in"], mode="nn", name="mm_in", tm=1024, tn=896, tk=512)
    qr, kr, vb = _qk_prep(proj, pos_col, gq128, gk128, inv_tab)
    pats = ((1, NB), (4, NB // 4), (16, 1))
    qs = [_perm(qr, d) for d, _ in pats]
    ks = [_perm(kr, d) for d, _ in pats]
    vs = [_perm(vb, d) for d, _ in pats]
    os_, ls_ = [], []
    for (d, nb), qd, kd, vd in zip(pats, qs, ks, vs):
        o, l = _attn_fwd(qd, kd, vd, nb, "attn_fwd_d%d" % d)
        os_.append(_unperm(o, d))
        ls_.append(_unperm(l, d))
    attn, lse, attn_n = _attn_merge(os_, ls_, p["g_attn_out"])

    xbc = _conv_fwd(proj, conv_w, conv_b)
    dt, a_cs = _ssd_prep(proj, dtb128, alog128)
    a_b = jnp.repeat(a_cs[:, :N_HEADS], HEAD, axis=1)
    at_r, dt_r = _row_layout(a_cs[:, :N_HEADS]), _row_layout(dt[:, :N_HEADS])
    y_ssd = _ssd_fwd(xbc, a_b, at_r, dt_r)
    ssm_n = _ssd_post(y_ssd, xbc, proj, dskip_b, p["g_ssm_out"])

    mix = jnp.concatenate([attn_n, ssm_n], axis=1)
    x1 = _matmul(mix, wf["w_out"], mode="nn", name="mm_out", tm=1024, tn=1024, tk=512,
                 extras=(x,), epilogue=lambda acc, r: (acc + r,))
    hc = _rms_fwd(x1, p["g_cross"], "rms_cross")
    memh = _rms_fwd(mem, p["g_mem"], "rms_mem")
    qc = _matmul(hc, wf["w_cq"], mode="nn", name="mm_cq", tm=1024, tn=512, tk=512)
    kv = _matmul(memh, wf["w_ckv"], mode="nn", name="mm_ckv", tm=N_MEM, tn=1024, tk=512)
    oc = _cross_fwd(qc, kv, p["g_cq"], p["g_ck"])
    x2 = _matmul(oc, wf["w_co"], mode="nn", name="mm_co", tm=1024, tn=256, tk=512, b_cb=256,
                 extras=(x1,), epilogue=lambda acc, r: (acc + r,))
    hm = _rms_fwd(x2, p["g_mlp"], "rms_mlp")

    def up_epi(acc):
        ru = jnp.maximum(acc, 0.0)
        return ru * ru, 2.0 * ru

    act, relu2 = _matmul(hm, wf["w_up"], mode="nn", name="mm_up", tm=1024, tn=1024, tk=512, b_cb=1024,
                         out_dtypes=(BF, BF), epilogue=up_epi)
    dy = _matmul(act, wf["w_down"], mode="nn", name="mm_down", tm=1024, tn=1024, tk=512, extras=(x2, target),
                 epilogue=lambda acc, r, t: ((acc + r - t) * (1.0 / D),))
    loss_part = _loss_sum(dy)[0, 0] * (0.5 * D)

    gb, gs = {}, {}
    du = _matmul(dy, wf["w_down"], mode="nt", name="mm_down_dx", tm=1024, tn=1024, tk=512, extras=(relu2,),
                 out_dtypes=(BF,), epilogue=lambda acc, r: (acc * r.astype(F32),))
    gb["w_down"] = _matmul(act, dy, mode="tn", name="mm_down_dw", tm=1024, tn=1024, tk=512,
                           out_dtypes=(BF,)).reshape(N_DEV, D_FF // N_DEV, D)
    gb["w_up"] = _matmul(hm, du, mode="tn", name="mm_up_dw", tm=1024, tn=1024, tk=512, out_dtypes=(BF,), out_cb=1024)
    dhm = _matmul(du, wf["w_up"], mode="nt", name="mm_up_dx", tm=1024, tn=1024, tk=512, b_cb=1024)
    dx2, gs["g_mlp"] = _rms_bwd_call(x2, p["g_mlp"], dhm, dy, "rms_mlp_bwd")

    doc = _matmul(dx2, wf["w_co"], mode="nt", name="mm_co_dx", tm=1024, tn=512, tk=256, b_cb=256)
    gb["w_co"] = _matmul(oc, dx2, mode="tn", name="mm_co_dw", tm=512, tn=256, tk=512, out_dtypes=(BF,), out_cb=256)
    dqc, dkn, dvc, gs["g_cq"] = _cross_bwd(qc, doc, kv, p["g_cq"], p["g_ck"])
    dkv, gs["g_ck"] = _cross_kv_bwd(kv, dkn, dvc, p["g_ck"])
    gb["w_cq"] = _matmul(hc, dqc, mode="tn", name="mm_cq_dw", tm=1024, tn=512, tk=512,
                         out_dtypes=(BF,)).reshape(N_DEV, D // N_DEV, D_CROSS)
    dhc = _matmul(dqc, wf["w_cq"], mode="nt", name="mm_cq_dx", tm=1024, tn=1024, tk=512)
    gb["w_ckv"] = _matmul(memh, dkv, mode="tn", name="mm_ckv_dw", tm=1024, tn=1024, tk=N_MEM,
                          out_dtypes=(BF,)).reshape(N_DEV, D // N_DEV, 2 * D_CROSS)
    dmemh = _matmul(dkv, wf["w_ckv"], mode="nt", name="mm_ckv_dx", tm=N_MEM, tn=1024, tk=512)
    (gs["g_mem"],) = _rms_bwd_call(mem, p["g_mem"], dmemh, None, "rms_mem_bwd")
    dx1, gs["g_cross"] = _rms_bwd_call(x1, p["g_cross"], dhc, dx2, "rms_cross_bwd")

    dmix = _matmul(dx1, wf["w_out"], mode="nt", name="mm_out_dx", tm=1024, tn=1024, tk=512)
    gb["w_out"] = _matmul(mix, dx1, mode="tn", name="mm_out_dw", tm=1024, tn=1024, tk=512,
                          out_dtypes=(BF,)).reshape(N_DEV, D // N_DEV, D)

    dattn, delta, gs["g_attn_out"] = _attn_merge_bwd(dmix, attn, p["g_attn_out"])
    dqs, dks, dvs = [], [], []
    for (d, nb), qd, kd, vd in zip(pats, qs, ks, vs):
        dq, dk, dv = _attn_bwd(qd, kd, vd, _perm(dattn, d), _perm(lse, d), _perm(delta, d), nb, "attn_bwd_d%d" % d)
        dqs.append(_unperm(dq, d))
        dks.append(_unperm(dk, d))
        dvs.append(_unperm(dv, d))
    dq_pre, dk_pre, dv_pre, dgq, dgk = _qk_bwd(dqs, dks, dvs, proj, pos_col, gq128, gk128, inv_tab)
    gs["g_q"], gs["g_k"] = dgq[:, :HEAD], dgk[:, :HEAD]

    dy_ssd, dz, dxs_skip, dd_lane, gs["g_ssm_out"] = _ssd_post_bwd(y_ssd, xbc, proj, dmix, dskip_b, p["g_ssm_out"])
    gs["d_skip"] = dd_lane.reshape(N_HEADS, HEAD).sum(axis=1).reshape(1, N_HEADS)
    dc_pair, daq_b, dx_ssd, db_pair, dak_r, ddt_r = _ssd_bwd(xbc, dy_ssd, a_b, at_r, dt_r)
    daq = _pad_lanes(daq_b[:, ::HEAD])
    dak = _pad_lanes(_row_layout_inv(dak_r))
    ddt_direct = _pad_lanes(_row_layout_inv(ddt_r))
    ddt_raw, dalog, dbias = _ssd_prep_bwd(daq, dak, ddt_direct, dt, proj, dtb128, alog128)
    gs["a_log"], gs["dt_bias"] = dalog[:, :N_HEADS], dbias[:, :N_HEADS]
    same = lambda c: c
    dxbc_x, dcw_x, dcb_x = _conv_bwd(proj, conv_w, conv_b, dxs_skip, dx_ssd, same, same, 0, 8, "conv_bwd_x")
    dxbc_b, dcw_b, dcb_b = _conv_bwd(proj, conv_w, conv_b, db_pair, db_pair, lambda c: 2 * c, lambda c: 2 * c + 1,
                                     D_SSM, 4, "conv_bwd_b")
    dxbc_c, dcw_c, dcb_c = _conv_bwd(proj, conv_w, conv_b, dc_pair, dc_pair, lambda c: 2 * c, lambda c: 2 * c + 1,
                                     D_SSM + 512, 4, "conv_bwd_c")
    g_conv_w = jnp.concatenate([dcw_x, dcw_b, dcw_c], axis=1)
    gs["conv_b"] = jnp.concatenate([dcb_x, dcb_b, dcb_c], axis=1)

    dproj = jnp.concatenate([dq_pre, dk_pre, dv_pre, dz, dxbc_x, dxbc_b, dxbc_c, ddt_raw], axis=1)
    dw_in = _matmul(h, dproj, mode="tn", name="mm_in_dw", tm=1024, tn=896, tk=512, out_dtypes=(BF,))
    gb["w_in"] = dw_in[:, :D_IN].reshape(D, N_DEV, D_IN // N_DEV).transpose(1, 0, 2)
    dh = _matmul(dproj, wf["w_in"], mode="nt", name="mm_in_dx", tm=1024, tn=1024, tk=896)
    grad_x, gs["g_mix"] = _rms_bwd_call(x, p["g_mix"], dh, dx1, "rms_mix_bwd")
    return loss_part, grad_x, gb, gs, g_conv_w


def kernel(x, mem, positions, g_mix, w_in, g_q, g_k, g_attn_out, conv_w, conv_b, dt_bias, a_log, d_skip, g_ssm_out, w_out, g_cross, g_mem, w_cq, w_ckv, g_cq, g_ck, w_co, g_mlp, w_up, w_down, loss_target, m_g_mix, m_w_in, m_g_q, m_g_k, m_g_attn_out, m_conv_w, m_conv_b, m_dt_bias, m_a_log, m_d_skip, m_g_ssm_out, m_w_out, m_g_cross, m_g_mem, m_w_cq, m_w_ckv, m_g_cq, m_g_ck, m_w_co, m_g_mlp, m_w_up, m_w_down, v_g_mix, v_w_in, v_g_q, v_g_k, v_g_attn_out, v_conv_w, v_conv_b, v_dt_bias, v_a_log, v_d_skip, v_g_ssm_out, v_w_out, v_g_cross, v_g_mem, v_w_cq, v_w_ckv, v_g_cq, v_g_ck, v_w_co, v_g_mlp, v_w_up, v_w_down):
    args = dict(locals())
    w = {n: args[n] for n in WEIGHTS}
    m = {n: args["m_" + n] for n in WEIGHTS}
    v = {n: args["v_" + n] for n in WEIGHTS}
    small = {n: w[n] for n, _ in SMALL}
    me = 4 * lax.axis_index("x") + 2 * lax.axis_index("y") + lax.axis_index("c")

    shards = [w[n][0].astype(BF) for n in BIG] + [w["conv_w"][0]]
    gathered = _all_gather(shards, "ag_weights")
    wf = dict(zip(BIG, gathered[:len(BIG)]))
    w_in_full = wf["w_in"].transpose(1, 0, 2).reshape(D, D_IN)
    wf["w_in"] = jnp.pad(w_in_full, ((0, 0), (0, D_INP - D_IN)))
    for n in ("w_out", "w_cq", "w_ckv", "w_down"):
        wf[n] = wf[n].reshape(-1, wf[n].shape[-1])
    wf["conv_w"] = gathered[-1].transpose(1, 0, 2).reshape(CONV_W, D_CONV)

    loss_part, grad_x, gb, gs, g_conv_w = _local_step(
        x[0], mem[0], positions.reshape(S, 1), loss_target[0], small, wf)
    loss = lax.psum(loss_part, ("x", "y", "c"))

    recv = _reduce_scatter_send([gb[n] for n in BIG], "rs_grads")
    out = {}
    for n, parts in zip(BIG, recv):
        out[n] = [t.reshape(w[n].shape) for t in _adamw(w[n][0], m[n][0], v[n][0], parts, "adamw_" + n)]

    sm_parts, cw_parts = _all_gather([_pack_small(gs), g_conv_w], "ag_small_grads")
    sm = [_unpack_small(t) for t in _adamw(_pack_small(small), _pack_small({n: m[n] for n, _ in SMALL}),
                                           _pack_small({n: v[n] for n, _ in SMALL}), sm_parts, "adamw_small")]
    for n, _ in SMALL:
        out[n] = [t[n] for t in sm]
    cols = D_CONV // N_DEV
    cw_mine = lax.dynamic_slice_in_dim(cw_parts, me * cols, cols, axis=2)
    out["conv_w"] = [t.reshape(w["conv_w"].shape) for t in
                     _adamw(w["conv_w"][0], m["conv_w"][0], v["conv_w"][0], cw_mine, "adamw_conv_w")]

    res = [loss, grad_x.reshape(x.shape)]
    for k in range(4):
        res += [out[n][k] for n in WEIGHTS]
    return tuple(res)
```

```python
import functools
import math

import numpy as np
import jax
import jax.numpy as jnp
from jax import lax
from jax.experimental import pallas as pl
from jax.experimental.pallas import tpu as pltpu

F32 = jnp.float32
BF = jnp.bfloat16

N_DEV = 8
S = 2048
D = 2048
D_ATTN = 1024
N_HEADS = 16
HEAD = 64
ROT = 16
ROPE_THETA = 500000.0
D_SSM = 1024
D_CONV = 2048
CONV_W = 4
N_MEM = 256
D_CROSS = 512
CROSS_HEAD = 128
D_FF = 8192
D_IN = 6160
D_INP = 6272
EPS = 1e-6
BLK = 128
NB = S // BLK
LANES = 128
NEG = -1e30

ADAM_LR = 0.001
ADAM_B1 = 0.9
ADAM_B2 = 0.999
ADAM_EPS = 1e-08
ADAM_WD = 0.01
ADAM_STEP = 10

VMEM_LIMIT_BYTES = 48 * 1024 * 1024
ROW_TILE = 256


def _params(*sem):
    return pltpu.CompilerParams(dimension_semantics=sem, vmem_limit_bytes=VMEM_LIMIT_BYTES)


def _matmul(a, b, *, mode, name, tm, tn, tk, out_dtypes=(F32,), b_cb=None, out_cb=None,
            extras=(), epilogue=None):
    if mode == "tn":
        kk, m = a.shape
    else:
        m, kk = a.shape
    if b_cb is None:
        br, bc = b.shape
    else:
        br, bc = b.shape[1], N_DEV * b_cb
    n = br if mode == "nt" else bc
    assert m % tm == 0 and n % tn == 0 and kk % tk == 0, (name, m, n, kk)
    nk = kk // tk
    grid = (m // tm, n // tn, nk)

    if mode == "tn":
        a_spec = pl.BlockSpec((tk, tm), lambda i, j, k: (k, i))
    else:
        a_spec = pl.BlockSpec((tm, tk), lambda i, j, k: (i, k))
    if mode == "nt":
        b_blk, b_idx = (tn, tk), (lambda i, j, k: (j, k))
    else:
        b_blk, b_idx = (tk, tn), (lambda i, j, k: (k, j))
    if b_cb is None:
        b_spec = pl.BlockSpec(b_blk, b_idx)
    else:
        tc = b_blk[1]
        assert b_cb % tc == 0
        per = b_cb // tc

        def b_idx3(i, j, k):
            r, c = b_idx(i, j, k)
            return (c // per, r, c % per)
        b_spec = pl.BlockSpec((None,) + b_blk, b_idx3)
    ex_specs = [pl.BlockSpec((tm, tn), lambda i, j, k: (i, j)) for _ in extras]
    if out_cb is None:
        out_shape = [jax.ShapeDtypeStruct((m, n), dt) for dt in out_dtypes]
        out_specs = [pl.BlockSpec((tm, tn), lambda i, j, k: (i, j)) for _ in out_dtypes]
    else:
        assert out_cb % tn == 0
        pero = out_cb // tn
        out_shape = [jax.ShapeDtypeStruct((N_DEV, m, out_cb), dt) for dt in out_dtypes]
        out_specs = [pl.BlockSpec((None, tm, tn), lambda i, j, k: (j // pero, i, j % pero)) for _ in out_dtypes]
    dn = {"nn": (((1,), (0,)), ((), ())), "nt": (((1,), (1,)), ((), ())), "tn": (((0,), (0,)), ((), ()))}[mode]
    ne, no = len(extras), len(out_dtypes)

    def kern(a_ref, b_ref, *rest):
        ex_refs, out_refs, acc_ref = rest[:ne], rest[ne:ne + no], rest[ne + no]
        k = pl.program_id(2)

        @pl.when(k == 0)
        def _():
            acc_ref[...] = jnp.zeros_like(acc_ref)

        acc_ref[...] += lax.dot_general(a_ref[...].astype(BF), b_ref[...].astype(BF), dn,
                                        preferred_element_type=F32)

        @pl.when(k == nk - 1)
        def _():
            acc = acc_ref[...]
            outs = (acc,) if epilogue is None else epilogue(acc, *[r[...] for r in ex_refs])
            for r, o in zip(out_refs, outs):
                r[...] = o.astype(r.dtype)

    res = pl.pallas_call(
        kern, name=name, grid=grid, in_specs=[a_spec, b_spec] + ex_specs, out_specs=out_specs,
        out_shape=out_shape, scratch_shapes=[pltpu.VMEM((tm, tn), F32)],
        compiler_params=_params("parallel", "parallel", "arbitrary"))(a, b, *extras)
    return res[0] if no == 1 else tuple(res)


def _rowwise(body, *, name, nrows, tile, row_ins, full_ins=(), row_outs=(), acc_outs=(), scratch=(),
             reverse=False):
    n = nrows // tile

    def ridx(i):
        return (n - 1 - i) if reverse else i

    in_specs, args = [], []
    for arr, width, cb in row_ins:
        in_specs.append(pl.BlockSpec((tile, width), lambda i, cb=cb: (ridx(i), cb)))
        args.append(arr)
    for arr in full_ins:
        in_specs.append(pl.BlockSpec(arr.shape, lambda i, nd=arr.ndim: (0,) * nd))
        args.append(arr)
    out_shape, out_specs = [], []
    for width, dt in row_outs:
        out_shape.append(jax.ShapeDtypeStruct((nrows, width), dt))
        out_specs.append(pl.BlockSpec((tile, width), lambda i: (ridx(i), 0)))
    for shp, dt in acc_outs:
        out_shape.append(jax.ShapeDtypeStruct(shp, dt))
        out_specs.append(pl.BlockSpec(shp, lambda i, nd=len(shp): (0,) * nd))

    def kern(*refs):
        body(pl.program_id(0), n, *refs)

    return pl.pallas_call(kern, name=name, grid=(n,), in_specs=in_specs, out_specs=out_specs,
                          out_shape=out_shape, scratch_shapes=list(scratch),
                          compiler_params=_params("arbitrary"))(*args)


def _accum(ref, val, i):
    @pl.when(i == 0)
    def _():
        ref[...] = val

    @pl.when(i > 0)
    def _():
        ref[...] += val


def _sigmoid(x):
    return 1.0 / (1.0 + jnp.exp(-x))


def _rms_n(x):
    r = lax.rsqrt(jnp.mean(x * x, axis=-1, keepdims=True) + EPS)
    return x * r, r


def _rms_bwd(x, g, dh):
    n, r = _rms_n(x)
    dn = dh * g
    dx = r * (dn - n * jnp.mean(dn * n, axis=-1, keepdims=True))
    return dx, jnp.sum(dh * n, axis=0, keepdims=True)


def _seg_sum(x, seg):
    t, w = x.shape
    tiles = [x[:, LANES * j:LANES * (j + 1)] for j in range(w // LANES)]
    outs = []
    if seg == 64:
        lo = lax.broadcasted_iota(jnp.int32, (t, LANES), 1) < 64
        for xt in tiles:
            s_lo = jnp.sum(jnp.where(lo, xt, 0.0), axis=-1, keepdims=True)
            s_hi = jnp.sum(jnp.where(lo, 0.0, xt), axis=-1, keepdims=True)
            outs.append(jnp.where(lo, s_lo, s_hi))
    else:
        sums = [jnp.sum(xt, axis=-1, keepdims=True) for xt in tiles]
        if seg == 256:
            sums = [sums[2 * (j // 2)] + sums[2 * (j // 2) + 1] for j in range(len(sums))]
        else:
            assert seg == 128
        outs = [jnp.broadcast_to(s, (t, LANES)) for s in sums]
    return outs[0] if len(outs) == 1 else jnp.concatenate(outs, axis=1)


def _seg_rms_n(x, seg):
    r = lax.rsqrt(_seg_sum(x * x, seg) * (1.0 / seg) + EPS)
    return x * r, r


def _seg_rms_bwd(x, g, dy, seg):
    n, r = _seg_rms_n(x, seg)
    dn = dy * g
    dx = r * (dn - n * (_seg_sum(dn * n, seg) * (1.0 / seg)))
    return dx, jnp.sum(dy * n, axis=0, keepdims=True)


def _rope_tables(pos_col, inv_tab, t):
    ang = pos_col.astype(F32) * inv_tab
    idx = lax.broadcasted_iota(jnp.int32, (t, LANES), 1) % HEAD
    cos, sin = jnp.cos(ang), jnp.sin(ang)
    c = jnp.where(idx < ROT, cos, 1.0)
    sg = jnp.where(idx < ROT // 2, -sin, jnp.where(idx < ROT, sin, 0.0))
    return c, sg, idx < ROT // 2


def _rope(x, c, sg, lo):
    partner = jnp.where(lo, pltpu.roll(x, LANES - ROT // 2, 1), pltpu.roll(x, ROT // 2, 1))
    return x * c + partner * sg


def _fold_heads(v):
    v8 = jnp.broadcast_to(v, (8, LANES))
    return (v8 + pltpu.roll(v8, HEAD, 1))[0:1]


def _dot(a, b, dn):
    return lax.dot_general(a, b, (dn, ((), ())), preferred_element_type=F32)


NN = ((1,), (0,))
NT = ((1,), (1,))
TN = ((0,), (0,))


def _dot3(l01, x):
    x1 = x.astype(BF)
    r1 = x - x1.astype(F32)
    x2 = r1.astype(BF)
    x3 = (r1 - x2.astype(F32)).astype(BF)
    return _dot(l01, x1, NN) + _dot(l01, x2, NN) + _dot(l01, x3, NN)


def _rms_fwd(x, g, name):
    rows = x.shape[0]

    def body(i, n, x_ref, g_ref, o_ref):
        nx, _ = _rms_n(x_ref[...])
        o_ref[...] = (nx * g_ref[...]).astype(BF)

    return _rowwise(body, name=name, nrows=rows, tile=min(ROW_TILE, rows), row_ins=[(x, D, 0)],
                    full_ins=[g], row_outs=[(D, BF)])[0]


def _qk_prep(proj, pos_col, gq128, gk128, inv_tab):
    scale = HEAD ** -0.5

    def body(i, n, q_ref, k_ref, v_ref, p_ref, gq_ref, gk_ref, it_ref, qo_ref, ko_ref, vo_ref):
        t = q_ref.shape[0]
        c, sg, lo = _rope_tables(p_ref[...], it_ref[...], t)
        for j in range(D_ATTN // LANES):
            sl = slice(LANES * j, LANES * (j + 1))
            qn, _ = _seg_rms_n(q_ref[:, sl], HEAD)
            kn, _ = _seg_rms_n(k_ref[:, sl], HEAD)
            qo_ref[:, sl] = (_rope(qn * gq_ref[...], c, sg, lo) * scale).astype(BF)
            ko_ref[:, sl] = _rope(kn * gk_ref[...], c, sg, lo).astype(BF)
        vo_ref[...] = v_ref[...].astype(BF)

    return _rowwise(body, name="qk_prep", nrows=S, tile=ROW_TILE,
                    row_ins=[(proj, D_ATTN, 0), (proj, D_ATTN, 1), (proj, D_ATTN, 2), (pos_col, 1, 0)],
                    full_ins=[gq128, gk128, inv_tab], row_outs=[(D_ATTN, BF)] * 3)


def _band_scores(q, k_ref, b, nb, h_mask):
    row = lax.broadcasted_iota(jnp.int32, (BLK, BLK), 0)
    col = lax.broadcasted_iota(jnp.int32, (BLK, BLK), 1)
    qm = jnp.where(h_mask, q.astype(F32), 0.0).astype(BF)
    kc = k_ref[pl.ds(pl.multiple_of(b * BLK, BLK), BLK), :]
    s_c = jnp.where(col <= row, _dot(qm, kc, NT), NEG)
    if nb == 1:
        return qm, kc, s_c, None, None
    pb = jnp.maximum(b - 1, 0)
    kp = k_ref[pl.ds(pl.multiple_of(pb * BLK, BLK), BLK), :]
    ok = jnp.logical_and(col >= row, (b % nb) != 0)
    s_p = jnp.where(ok, _dot(qm, kp, NT), NEG)
    return qm, kc, s_c, kp, s_p


def _attn_fwd(q, k, v, nb, name):
    def kern(q_ref, k_ref, v_ref, o_ref, l_ref):
        b = pl.program_id(1)
        q = q_ref[...]
        half0 = lax.broadcasted_iota(jnp.int32, (BLK, LANES), 1) < HEAD
        o_h, l_h = [], []
        for h in range(2):
            hm = half0 if h == 0 else jnp.logical_not(half0)
            _, _, s_c, _, s_p = _band_scores(q, k_ref, b, nb, hm)
            m = jnp.max(s_c, axis=-1, keepdims=True)
            if nb > 1:
                m = jnp.maximum(m, jnp.max(s_p, axis=-1, keepdims=True))
            p_c = jnp.exp(s_c - m)
            den = jnp.sum(p_c, axis=-1, keepdims=True)
            vc = v_ref[pl.ds(pl.multiple_of(b * BLK, BLK), BLK), :]
            o = _dot(p_c.astype(BF), vc, NN)
            if nb > 1:
                p_p = jnp.exp(s_p - m)
                den = den + jnp.sum(p_p, axis=-1, keepdims=True)
                pb = jnp.maximum(b - 1, 0)
                vp = v_ref[pl.ds(pl.multiple_of(pb * BLK, BLK), BLK), :]
                o = o + _dot(p_p.astype(BF), vp, NN)
            o_h.append(o / den)
            l_h.append(m + jnp.log(den))
        o_ref[...] = jnp.where(half0, o_h[0], o_h[1])
        l_ref[...] = jnp.where(half0, l_h[0], l_h[1])

    tile = pl.BlockSpec((BLK, LANES), lambda p, b: (b, p))
    seq = pl.BlockSpec((S, LANES), lambda p, b: (0, p))
    return pl.pallas_call(
        kern, name=name, grid=(D_ATTN // LANES, NB), in_specs=[tile, seq, seq], out_specs=[tile, tile],
        out_shape=[jax.ShapeDtypeStruct((S, D_ATTN), F32)] * 2,
        compiler_params=_params("parallel", "arbitrary"))(q, k, v)


def _attn_merge(os_, ls_, g_attn):
    def body(i, n, o1, o2, o3, l1, l2, l3, g_ref, a_ref, lse_ref, an_ref):
        la, lb, lc = l1[...], l2[...], l3[...]
        m = jnp.maximum(jnp.maximum(la, lb), lc)
        ea, eb, ec = jnp.exp(la - m), jnp.exp(lb - m), jnp.exp(lc - m)
        den = ea + eb + ec
        attn = (ea * o1[...] + eb * o2[...] + ec * o3[...]) / den
        a_ref[...] = attn
        lse_ref[...] = m + jnp.log(den)
        nx, _ = _rms_n(attn)
        an_ref[...] = (nx * g_ref[...]).astype(BF)

    return _rowwise(body, name="attn_merge", nrows=S, tile=ROW_TILE,
                    row_ins=[(a, D_ATTN, 0) for a in (*os_, *ls_)], full_ins=[g_attn],
                    row_outs=[(D_ATTN, F32), (D_ATTN, F32), (D_ATTN, BF)])


def _conv_taps(x, w_ref):
    rows = lax.broadcasted_iota(jnp.int32, x.shape, 0)
    shifted = []
    for w in range(CONV_W):
        k = CONV_W - 1 - w
        shifted.append(x if k == 0 else jnp.where(rows >= k, pltpu.roll(x, k, 0), 0.0))
    acc = shifted[0] * w_ref[0:1, :]
    for w in range(1, CONV_W):
        acc = acc + shifted[w] * w_ref[w:w + 1, :]
    return acc, shifted


def _conv_fwd(proj, conv_w, conv_b):
    tc = 256

    def kern(x_ref, w_ref, b_ref, o_ref):
        c, _ = _conv_taps(x_ref[...], w_ref)
        c = c + b_ref[...]
        o_ref[...] = c * _sigmoid(c)

    return pl.pallas_call(
        kern, name="conv_fwd", grid=(D_CONV // tc,),
        in_specs=[pl.BlockSpec((S, tc), lambda c: (0, 4 * D_ATTN // tc + c)),
                  pl.BlockSpec((CONV_W, tc), lambda c: (0, c)), pl.BlockSpec((1, tc), lambda c: (0, c))],
        out_specs=pl.BlockSpec((S, tc), lambda c: (0, c)),
        out_shape=jax.ShapeDtypeStruct((S, D_CONV), F32), compiler_params=_params("parallel"))(proj, conv_w, conv_b)


def _softplus(x):
    return jnp.maximum(x, 0.0) + jnp.log1p(jnp.exp(-jnp.abs(x)))


def _ssd_prep(proj, dt_bias128, a_log128):
    def body(i, n, raw_ref, b_ref, al_ref, dt_ref, a_ref, carry):
        @pl.when(i == 0)
        def _():
            carry[...] = jnp.zeros_like(carry)

        dt = _softplus(raw_ref[...] + b_ref[...])
        da = dt * (-jnp.exp(al_ref[...]))
        tri = (lax.broadcasted_iota(jnp.int32, (BLK, BLK), 0) >= lax.broadcasted_iota(jnp.int32, (BLK, BLK), 1))
        cs = _dot3(tri.astype(BF), da) + carry[0:1, :]
        dt_ref[...] = dt
        a_ref[...] = cs
        carry[...] = jnp.broadcast_to(cs[BLK - 1:BLK, :], carry.shape)

    return _rowwise(body, name="ssd_prep", nrows=S, tile=BLK, row_ins=[(proj, LANES, (D_INP - LANES) // LANES)],
                    full_ins=[dt_bias128, a_log128], row_outs=[(LANES, F32), (LANES, F32)],
                    scratch=[pltpu.VMEM((8, LANES), F32)])


def _ssd_decay(a_col, at_row, causal):
    diff = jnp.where(causal, a_col - at_row, 0.0)
    return jnp.where(causal, jnp.exp(diff), 0.0)


def _ssd_fwd(xbc, a_b, at_r, dt_r):
    def kern(c_ref, b_ref, x_ref, ab_ref, at_ref, dt_ref, y_ref):
        i = pl.program_id(1)
        ci = c_ref[...].astype(BF)
        half0 = lax.broadcasted_iota(jnp.int32, (BLK, LANES), 1) < HEAD
        a_cols = (ab_ref[:, 0:1], ab_ref[:, HEAD:HEAD + 1])
        rowg = i * BLK + lax.broadcasted_iota(jnp.int32, (BLK, BLK), 0)

        def step(j, acc):
            off = pl.multiple_of(j * BLK, BLK)
            bj = b_ref[pl.ds(off, BLK), :].astype(BF)
            xj = x_ref[pl.ds(off, BLK), :]
            cb = _dot(ci, bj, NT)
            causal = (j * BLK + lax.broadcasted_iota(jnp.int32, (BLK, BLK), 1)) <= rowg
            at, dtj = at_ref[j], dt_ref[j]
            for h in range(2):
                hm = half0 if h == 0 else jnp.logical_not(half0)
                w = cb * _ssd_decay(a_cols[h], at[h:h + 1, :], causal) * dtj[h:h + 1, :]
                acc = acc + _dot(w.astype(BF), jnp.where(hm, xj, 0.0).astype(BF), NN)
            return acc

        y_ref[...] = lax.fori_loop(0, i + 1, step, jnp.zeros((BLK, LANES), F32))

    npair = D_SSM // LANES
    rowvec = pl.BlockSpec((None, NB, 2, BLK), lambda p, i: (p, 0, 0, 0))
    return pl.pallas_call(
        kern, name="ssd_fwd", grid=(npair, NB),
        in_specs=[pl.BlockSpec((BLK, LANES), lambda p, i: (i, 12 + p // 2)),
                  pl.BlockSpec((S, LANES), lambda p, i: (0, 8 + p // 2)),
                  pl.BlockSpec((S, LANES), lambda p, i: (0, p)),
                  pl.BlockSpec((BLK, LANES), lambda p, i: (i, p)), rowvec, rowvec],
        out_specs=pl.BlockSpec((BLK, LANES), lambda p, i: (i, p)),
        out_shape=jax.ShapeDtypeStruct((S, D_SSM), F32),
        compiler_params=_params("parallel", "arbitrary"))(xbc, xbc, xbc, a_b, at_r, dt_r)


def _ssd_post(y_ssd, xbc, proj, dskip_b, g_ssm):
    def body(i, n, y_ref, xs_ref, z_ref, d_ref, g_ref, o_ref):
        z = z_ref[...]
        y2 = (y_ref[...] + d_ref[...] * xs_ref[...]) * (z * _sigmoid(z))
        nx, _ = _seg_rms_n(y2, 256)
        o_ref[...] = (nx * g_ref[...]).astype(BF)

    return _rowwise(body, name="ssd_post", nrows=S, tile=ROW_TILE,
                    row_ins=[(y_ssd, D_SSM, 0), (xbc, D_SSM, 0), (proj, D_SSM, 3)], full_ins=[dskip_b, g_ssm],
                    row_outs=[(D_SSM, BF)])[0]


def _cross_heads(q, kv_ref, gq, gk):
    out = []
    for h in range(D_CROSS // CROSS_HEAD):
        sl = slice(CROSS_HEAD * h, CROSS_HEAD * (h + 1))
        nq, rq = _rms_n(q[:, sl])
        nk, rk = _rms_n(kv_ref[:, sl])
        v = kv_ref[:, D_CROSS + CROSS_HEAD * h:D_CROSS + CROSS_HEAD * (h + 1)]
        out.append((sl, nq, rq, nk, rk, v))
    return out


def _cross_fwd(qc, kv, g_cq, g_ck):
    scale = CROSS_HEAD ** -0.5

    def body(i, n, q_ref, kv_ref, gq_ref, gk_ref, o_ref):
        for sl, nq, _, nk, _, v in _cross_heads(q_ref[...], kv_ref, gq_ref[...], gk_ref[...]):
            qn = (nq * gq_ref[...] * scale).astype(BF)
            kn = (nk * gk_ref[...]).astype(BF)
            s = _dot(qn, kn, NT)
            e = jnp.exp(s - jnp.max(s, axis=-1, keepdims=True))
            p = e / jnp.sum(e, axis=-1, keepdims=True)
            o_ref[:, sl] = _dot(p.astype(BF), v.astype(BF), NN).astype(BF)

    return _rowwise(body, name="cross_fwd", nrows=S, tile=ROW_TILE, row_ins=[(qc, D_CROSS, 0)],
                    full_ins=[kv, g_cq, g_ck], row_outs=[(D_CROSS, BF)])[0]


def _loss_sum(dy):
    def body(i, n, d_ref, o_ref):
        d = d_ref[...]
        s = jnp.sum(jnp.sum(d * d, axis=0, keepdims=True), axis=1, keepdims=True)
        _accum(o_ref, s, i)

    return _rowwise(body, name="loss_sum", nrows=S, tile=ROW_TILE, row_ins=[(dy, D, 0)],
                    acc_outs=[((1, 1), F32)])[0]


def _rms_bwd_call(x, g, dh, dres, name):
    rows = x.shape[0]
    has_res = dres is not None

    def body(i, n, *refs):
        if has_res:
            x_ref, dh_ref, dr_ref, g_ref, dx_ref, dg_ref = refs
        else:
            x_ref, dh_ref, g_ref, dg_ref = refs
        dx, dg = _rms_bwd(x_ref[...], g_ref[...], dh_ref[...])
        if has_res:
            dx_ref[...] = dx + dr_ref[...]
        _accum(dg_ref, dg, i)

    row_ins = [(x, D, 0), (dh, D, 0)] + ([(dres, D, 0)] if has_res else [])
    return _rowwise(body, name=name, nrows=rows, tile=min(ROW_TILE, rows), row_ins=row_ins, full_ins=[g],
                    row_outs=[(D, F32)] if has_res else [], acc_outs=[((1, D), F32)])


def _cross_bwd(qc, doc, kv, g_cq, g_ck):
    scale = CROSS_HEAD ** -0.5

    def body(i, n, q_ref, do_ref, kv_ref, gq_ref, gk_ref, dq_ref, dkn_ref, dv_ref, dgq_ref):
        dgq = jnp.zeros((1, CROSS_HEAD), F32)
        dkn_parts, dv_parts = [], []
        for sl, nq, rq, nk, _, v in _cross_heads(q_ref[...], kv_ref, gq_ref[...], gk_ref[...]):
            qn = (nq * gq_ref[...] * scale).astype(BF)
            kn = (nk * gk_ref[...]).astype(BF)
            s = _dot(qn, kn, NT)
            e = jnp.exp(s - jnp.max(s, axis=-1, keepdims=True))
            p = e / jnp.sum(e, axis=-1, keepdims=True)
            do = do_ref[:, sl].astype(BF)
            dv_parts.append(_dot(p.astype(BF), do, TN))
            dp = _dot(do, v.astype(BF), NT)
            ds = (p * (dp - jnp.sum(dp * p, axis=-1, keepdims=True))).astype(BF)
            dqn = _dot(ds, kn, NN)
            dkn_parts.append(_dot(ds, qn, TN))
            dn = dqn * (gq_ref[...] * scale)
            dq_ref[:, sl] = (rq * (dn - nq * jnp.mean(dn * nq, axis=-1, keepdims=True))).astype(BF)
            dgq = dgq + jnp.sum(dqn * scale * nq, axis=0, keepdims=True)
        _accum(dkn_ref, jnp.concatenate(dkn_parts, axis=1), i)
        _accum(dv_ref, jnp.concatenate(dv_parts, axis=1), i)
        _accum(dgq_ref, dgq, i)

    return _rowwise(body, name="cross_bwd", nrows=S, tile=ROW_TILE, row_ins=[(qc, D_CROSS, 0), (doc, D_CROSS, 0)],
                    full_ins=[kv, g_cq, g_ck], row_outs=[(D_CROSS, BF)],
                    acc_outs=[((N_MEM, D_CROSS), F32), ((N_MEM, D_CROSS), F32), ((1, CROSS_HEAD), F32)])


def _cross_kv_bwd(kv, dkn, dv, g_ck):
    def body(i, n, kv_ref, dkn_ref, dv_ref, gk_ref, dkv_ref, dgk_ref):
        dgk = jnp.zeros((1, CROSS_HEAD), F32)
        for h in range(D_CROSS // CROSS_HEAD):
            sl = slice(CROSS_HEAD * h, CROSS_HEAD * (h + 1))
            dk, dg = _rms_bwd(kv_ref[:, sl], gk_ref[...], dkn_ref[:, sl])
            dkv_ref[:, sl] = dk.astype(BF)
            dgk = dgk + dg
        dkv_ref[:, D_CROSS:] = dv_ref[...].astype(BF)
        dgk_ref[...] = dgk

    return _rowwise(body, name="cross_kv_bwd", nrows=N_MEM, tile=N_MEM,
                    row_ins=[(kv, 2 * D_CROSS, 0), (dkn, D_CROSS, 0), (dv, D_CROSS, 0)], full_ins=[g_ck],
                    row_outs=[(2 * D_CROSS, BF)], acc_outs=[((1, CROSS_HEAD), F32)])


def _attn_merge_bwd(dmix, attn, g_attn):
    def body(i, n, dn_ref, a_ref, g_ref, da_ref, dl_ref, dg_ref):
        attn_v = a_ref[...]
        da, dg = _rms_bwd(attn_v, g_ref[...], dn_ref[...])
        da_ref[...] = da.astype(BF)
        dl_ref[...] = _seg_sum(da * attn_v, HEAD)
        _accum(dg_ref, dg, i)

    return _rowwise(body, name="attn_merge_bwd", nrows=S, tile=ROW_TILE,
                    row_ins=[(dmix, D_ATTN, 0), (attn, D_ATTN, 0)], full_ins=[g_attn],
                    row_outs=[(D_ATTN, BF), (D_ATTN, F32)], acc_outs=[((1, D_ATTN), F32)])


def _attn_bwd(q, k, v, do, lse, delta, nb, name):
    def kern(q_ref, k_ref, v_ref, do_ref, l_ref, d_ref, dq_ref, dk_ref, dv_ref):
        b = pl.program_id(1)

        @pl.when(b == 0)
        def _():
            dk_ref[...] = jnp.zeros_like(dk_ref)
            dv_ref[...] = jnp.zeros_like(dv_ref)

        q, do = q_ref[...], do_ref[...]
        half0 = lax.broadcasted_iota(jnp.int32, (BLK, LANES), 1) < HEAD
        cur = pl.ds(pl.multiple_of(b * BLK, BLK), BLK)
        prev = pl.ds(pl.multiple_of(jnp.maximum(b - 1, 0) * BLK, BLK), BLK)
        vc = v_ref[cur, :]
        dq_h = []
        dk_c = jnp.zeros((BLK, LANES), F32)
        dv_c = jnp.zeros((BLK, LANES), F32)
        dk_p = jnp.zeros((BLK, LANES), F32)
        dv_p = jnp.zeros((BLK, LANES), F32)
        for h in range(2):
            hm = half0 if h == 0 else jnp.logical_not(half0)
            lse_h = l_ref[:, HEAD * h:HEAD * h + 1]
            del_h = d_ref[:, HEAD * h:HEAD * h + 1]
            qm, kc, s_c, kp, s_p = _band_scores(q, k_ref, b, nb, hm)
            dom = jnp.where(hm, do.astype(F32), 0.0).astype(BF)
            p_c = jnp.exp(s_c - lse_h)
            ds_c = (p_c * (_dot(dom, vc, NT) - del_h)).astype(BF)
            dq = _dot(ds_c, kc, NN)
            dk_c = dk_c + _dot(ds_c, qm, TN)
            dv_c = dv_c + _dot(p_c.astype(BF), dom, TN)
            if nb > 1:
                vp = v_ref[prev, :]
                p_p = jnp.exp(s_p - lse_h)
                ds_p = (p_p * (_dot(dom, vp, NT) - del_h)).astype(BF)
                dq = dq + _dot(ds_p, kp, NN)
                dk_p = dk_p + _dot(ds_p, qm, TN)
                dv_p = dv_p + _dot(p_p.astype(BF), dom, TN)
            dq_h.append(dq)
        dq_ref[...] = jnp.where(half0, dq_h[0], dq_h[1])
        dk_ref[cur, :] += dk_c
        dv_ref[cur, :] += dv_c
        if nb > 1:
            dk_ref[prev, :] += dk_p
            dv_ref[prev, :] += dv_p

    tile = pl.BlockSpec((BLK, LANES), lambda p, b: (b, p))
    seq = pl.BlockSpec((S, LANES), lambda p, b: (0, p))
    return pl.pallas_call(
        kern, name=name, grid=(D_ATTN // LANES, NB), in_specs=[tile, seq, seq, tile, tile, tile],
        out_specs=[tile, seq, seq], out_shape=[jax.ShapeDtypeStruct((S, D_ATTN), F32)] * 3,
        compiler_params=_params("parallel", "arbitrary"))(q, k, v, do, lse, delta)


def _qk_bwd(dqs, dks, dvs, proj, pos_col, gq128, gk128, inv_tab):
    scale = HEAD ** -0.5

    def body(i, n, *refs):
        dq_refs, dk_refs, dv_refs = refs[0:3], refs[3:6], refs[6:9]
        q_ref, k_ref, p_ref, gq_ref, gk_ref, it_ref = refs[9:15]
        dqo_ref, dko_ref, dvo_ref, dgq_ref, dgk_ref = refs[15:20]
        t = q_ref.shape[0]
        c, sg, lo = _rope_tables(p_ref[...], it_ref[...], t)
        dgq = jnp.zeros((1, LANES), F32)
        dgk = jnp.zeros((1, LANES), F32)
        for j in range(D_ATTN // LANES):
            sl = slice(LANES * j, LANES * (j + 1))
            dqr = _rope(dq_refs[0][:, sl] + dq_refs[1][:, sl] + dq_refs[2][:, sl], c, -sg, lo) * scale
            dkr = _rope(dk_refs[0][:, sl] + dk_refs[1][:, sl] + dk_refs[2][:, sl], c, -sg, lo)
            dq, gq = _seg_rms_bwd(q_ref[:, sl], gq_ref[...], dqr, HEAD)
            dk, gk = _seg_rms_bwd(k_ref[:, sl], gk_ref[...], dkr, HEAD)
            dqo_ref[:, sl] = dq.astype(BF)
            dko_ref[:, sl] = dk.astype(BF)
            dgq, dgk = dgq + gq, dgk + gk
        dvo_ref[...] = (dv_refs[0][...] + dv_refs[1][...] + dv_refs[2][...]).astype(BF)
        _accum(dgq_ref, _fold_heads(dgq), i)
        _accum(dgk_ref, _fold_heads(dgk), i)

    return _rowwise(body, name="qk_bwd", nrows=S, tile=ROW_TILE,
                    row_ins=[(a, D_ATTN, 0) for a in (*dqs, *dks, *dvs)]
                    + [(proj, D_ATTN, 0), (proj, D_ATTN, 1), (pos_col, 1, 0)],
                    full_ins=[gq128, gk128, inv_tab], row_outs=[(D_ATTN, BF)] * 3,
                    acc_outs=[((1, LANES), F32)] * 2)


def _ssd_post_bwd(y_ssd, xbc, proj, dmix, dskip_b, g_ssm):
    def body(i, n, y_ref, xs_ref, z_ref, do_ref, d_ref, g_ref, dy_ref, dz_ref, dxs_ref, dd_ref, dg_ref):
        z, xs = z_ref[...], xs_ref[...]
        sg = _sigmoid(z)
        gate = z * sg
        y = y_ref[...] + d_ref[...] * xs
        dy2, dg = _seg_rms_bwd(y * gate, g_ref[...], do_ref[...], 256)
        dy = dy2 * gate
        dy_ref[...] = dy.astype(BF)
        dz_ref[...] = (dy2 * y * (sg * (1.0 + z * (1.0 - sg)))).astype(BF)
        dxs_ref[...] = dy * d_ref[...]
        _accum(dd_ref, jnp.sum(dy * xs, axis=0, keepdims=True), i)
        _accum(dg_ref, dg, i)

    return _rowwise(body, name="ssd_post_bwd", nrows=S, tile=ROW_TILE,
                    row_ins=[(y_ssd, D_SSM, 0), (xbc, D_SSM, 0), (proj, D_SSM, 3), (dmix, D_SSM, 1)],
                    full_ins=[dskip_b, g_ssm], row_outs=[(D_SSM, BF), (D_SSM, BF), (D_SSM, F32)],
                    acc_outs=[((1, D_SSM), F32), ((1, D_SSM), F32)])


def _ssd_bwd(xbc, dy, a_b, at_r, dt_r):
    def kern(c_ref, b_ref, x_ref, dy_ref, ab_ref, at_ref, dt_ref,
             dc_ref, daq_ref, dx_ref, db_ref, dak_ref, ddt_ref):
        i = pl.program_id(1)

        @pl.when(i == 0)
        def _():
            dx_ref[...] = jnp.zeros_like(dx_ref)
            db_ref[...] = jnp.zeros_like(db_ref)
            dak_ref[...] = jnp.zeros_like(dak_ref)
            ddt_ref[...] = jnp.zeros_like(ddt_ref)

        ci = c_ref[...].astype(BF)
        dyv = dy_ref[...]
        half0 = lax.broadcasted_iota(jnp.int32, (BLK, LANES), 1) < HEAD
        hms = (half0, jnp.logical_not(half0))
        dym = [jnp.where(hm, dyv.astype(F32), 0.0).astype(BF) for hm in hms]
        a_cols = (ab_ref[:, 0:1], ab_ref[:, HEAD:HEAD + 1])
        rowg = i * BLK + lax.broadcasted_iota(jnp.int32, (BLK, BLK), 0)

        def step(j, carry):
            dc_acc, r0, r1 = carry
            rows = [r0, r1]
            off = pl.multiple_of(j * BLK, BLK)
            bj = b_ref[pl.ds(off, BLK), :].astype(BF)
            xj = x_ref[pl.ds(off, BLK), :]
            cb = _dot(ci, bj, NT)
            causal = (j * BLK + lax.broadcasted_iota(jnp.int32, (BLK, BLK), 1)) <= rowg
            at, dtj = at_ref[j], dt_ref[j]
            dcb = jnp.zeros((BLK, BLK), F32)
            dxj = jnp.zeros((BLK, LANES), F32)
            for h in range(2):
                lm = _ssd_decay(a_cols[h], at[h:h + 1, :], causal)
                dth = dtj[h:h + 1, :]
                dw = _dot(dym[h], jnp.where(hms[h], xj, 0.0).astype(BF), NT)
                g = dw * cb * lm
                w = cb * lm * dth
                dxj = dxj + _dot(w.astype(BF), dym[h], TN)
                gcol = jnp.sum(g, axis=0, keepdims=True)
                ddt_ref[j, h:h + 1, :] += gcol
                dak_ref[j, h:h + 1, :] += gcol * dth
                rows[h] = rows[h] + jnp.sum(g * dth, axis=1, keepdims=True)
                dcb = dcb + dw * lm * dth
            dcb = dcb.astype(BF)
            dx_ref[pl.ds(off, BLK), :] += dxj
            db_ref[pl.ds(off, BLK), :] += _dot(dcb, ci, TN)
            return dc_acc + _dot(dcb, bj, NN), rows[0], rows[1]

        zcol = jnp.zeros((BLK, 1), F32)
        dc, r0, r1 = lax.fori_loop(0, i + 1, step, (jnp.zeros((BLK, LANES), F32), zcol, zcol))
        dc_ref[...] = dc
        daq_ref[...] = jnp.where(half0, r0, r1)

    npair = D_SSM // LANES
    rowvec = pl.BlockSpec((None, NB, 2, BLK), lambda p, i: (p, 0, 0, 0))
    tile = pl.BlockSpec((BLK, LANES), lambda p, i: (i, p))
    seq = pl.BlockSpec((S, LANES), lambda p, i: (0, p))
    return pl.pallas_call(
        kern, name="ssd_bwd", grid=(npair, NB),
        in_specs=[pl.BlockSpec((BLK, LANES), lambda p, i: (i, 12 + p // 2)),
                  pl.BlockSpec((S, LANES), lambda p, i: (0, 8 + p // 2)), seq, tile, tile, rowvec, rowvec],
        out_specs=[tile, tile, seq, seq, rowvec, rowvec],
        out_shape=[jax.ShapeDtypeStruct((S, D_SSM), F32)] * 4
        + [jax.ShapeDtypeStruct((npair, NB, 2, BLK), F32)] * 2,
        compiler_params=_params("parallel", "arbitrary"))(xbc, xbc, xbc, dy, a_b, at_r, dt_r)


def _ssd_prep_bwd(daq, dak, ddt_direct, dt, proj, dt_bias128, a_log128):
    def body(i, n, daq_ref, dak_ref, dd_ref, dt_ref, raw_ref, b_ref, al_ref, draw_ref, dal_ref, db_ref, carry):
        @pl.when(i == 0)
        def _():
            carry[...] = jnp.zeros_like(carry)

        a = -jnp.exp(al_ref[...])
        tri = (lax.broadcasted_iota(jnp.int32, (BLK, BLK), 0) <= lax.broadcasted_iota(jnp.int32, (BLK, BLK), 1))
        rev = _dot3(tri.astype(BF), daq_ref[...] - dak_ref[...]) + carry[0:1, :]
        carry[...] = jnp.broadcast_to(rev[0:1, :], carry.shape)
        dtv = dt_ref[...]
        draw = (dd_ref[...] + a * rev) * _sigmoid(raw_ref[...] + b_ref[...])
        draw_ref[...] = draw.astype(BF)
        _accum(dal_ref, jnp.sum(dtv * rev, axis=0, keepdims=True) * a, i)
        _accum(db_ref, jnp.sum(draw, axis=0, keepdims=True), i)

    return _rowwise(body, name="ssd_prep_bwd", nrows=S, tile=BLK, reverse=True,
                    row_ins=[(daq, LANES, 0), (dak, LANES, 0), (ddt_direct, LANES, 0), (dt, LANES, 0),
                             (proj, LANES, (D_INP - LANES) // LANES)],
                    full_ins=[dt_bias128, a_log128], row_outs=[(LANES, BF)],
                    acc_outs=[((1, LANES), F32), ((1, LANES), F32)], scratch=[pltpu.VMEM((8, LANES), F32)])


def _conv_bwd(proj, conv_w, conv_b, d1, d2, map1, map2, col0, ntiles, name):
    base = (4 * D_ATTN + col0) // LANES

    def kern(x_ref, w_ref, b_ref, d1_ref, d2_ref, dx_ref, dw_ref, db_ref):
        x = x_ref[...]
        c, shifted = _conv_taps(x, w_ref)
        c = c + b_ref[...]
        sg = _sigmoid(c)
        dc = (d1_ref[...] + d2_ref[...]) * (sg * (1.0 + c * (1.0 - sg)))
        rows = lax.broadcasted_iota(jnp.int32, x.shape, 0)
        dx = jnp.zeros_like(x)
        for w in range(CONV_W):
            k = CONV_W - 1 - w
            up = dc if k == 0 else jnp.where(rows < S - k, pltpu.roll(dc, S - k, 0), 0.0)
            dx = dx + up * w_ref[w:w + 1, :]
            dw_ref[w:w + 1, :] = jnp.sum(dc * shifted[w], axis=0, keepdims=True)
        dx_ref[...] = dx.astype(BF)
        db_ref[...] = jnp.sum(dc, axis=0, keepdims=True)

    c0 = col0 // LANES
    return pl.pallas_call(
        kern, name=name, grid=(ntiles,),
        in_specs=[pl.BlockSpec((S, LANES), lambda c: (0, base + c)),
                  pl.BlockSpec((CONV_W, LANES), lambda c: (0, c0 + c)),
                  pl.BlockSpec((1, LANES), lambda c: (0, c0 + c)),
                  pl.BlockSpec((S, LANES), lambda c: (0, map1(c))),
                  pl.BlockSpec((S, LANES), lambda c: (0, map2(c)))],
        out_specs=[pl.BlockSpec((S, LANES), lambda c: (0, c)), pl.BlockSpec((CONV_W, LANES), lambda c: (0, c)),
                   pl.BlockSpec((1, LANES), lambda c: (0, c))],
        out_shape=[jax.ShapeDtypeStruct((S, ntiles * LANES), BF), jax.ShapeDtypeStruct((CONV_W, ntiles * LANES), F32),
                   jax.ShapeDtypeStruct((1, ntiles * LANES), F32)],
        compiler_params=_params("parallel"))(proj, conv_w, conv_b, d1, d2)


def _adamw(w, m, v, parts, name):
    r, c = w.shape
    n_parts = parts.shape[0]
    tile = r if r <= 512 else (128 if c > 1024 else 256)
    assert r % tile == 0
    c1 = 1.0 - ADAM_B1 ** ADAM_STEP
    c2 = 1.0 - ADAM_B2 ** ADAM_STEP

    def kern(w_ref, m_ref, v_ref, p_ref, g_ref, d_ref, mo_ref, vo_ref):
        g = p_ref[0].astype(F32)
        for k in range(1, n_parts):
            g = g + p_ref[k].astype(F32)
        mn = ADAM_B1 * m_ref[...] + (1.0 - ADAM_B1) * g
        vn = ADAM_B2 * v_ref[...] + (1.0 - ADAM_B2) * (g * g)
        g_ref[...] = g
        mo_ref[...] = mn
        vo_ref[...] = vn
        d_ref[...] = -ADAM_LR * ((mn / c1) / (jnp.sqrt(vn / c2) + ADAM_EPS) + ADAM_WD * w_ref[...])

    blk = pl.BlockSpec((tile, c), lambda i: (i, 0))
    return pl.pallas_call(
        kern, name=name, grid=(r // tile,),
        in_specs=[blk, blk, blk, pl.BlockSpec((n_parts, tile, c), lambda i: (0, i, 0))],
        out_specs=[blk] * 4, out_shape=[jax.ShapeDtypeStruct((r, c), F32)] * 4,
        compiler_params=_params("parallel"))(w, m, v, parts)


MESH = pl.DeviceIdType.MESH
ANY = pl.BlockSpec(memory_space=pl.ANY)


def _all_gather(arrs, name):
    na = len(arrs)

    def kern(*refs):
        ins, outs = refs[:na], refs[na:2 * na]
        send_sems, recv_sems, local_sems = refs[2 * na:]
        x, y, c = lax.axis_index("x"), lax.axis_index("y"), lax.axis_index("c")
        me, sibling = (x, y, c), (x, y, 1 - c)
        chips = [(1 - x, y), (x, 1 - y), (1 - x, 1 - y)]

        def copy(a, k, block, to, src=None):
            px, py, pc = block
            dst = outs[a].at[4 * px + 2 * py + pc]
            return pltpu.make_async_remote_copy(
                src_ref=dst if src is None else src, dst_ref=dst, send_sem=send_sems.at[a, k],
                recv_sem=recv_sems.at[a, k], device_id=to, device_id_type=MESH)

        mine = [pltpu.make_async_copy(ins[a], outs[a].at[4 * x + 2 * y + c], local_sems.at[a]) for a in range(na)]
        for cp in mine:
            cp.start()
        first = []
        for a in range(na):
            first.append(copy(a, 0, me, sibling, src=ins[a]))
            first += [copy(a, 1 + j, me, (*chip, c), src=ins[a]) for j, chip in enumerate(chips)]
        for cp in first:
            cp.start()
        passed = []
        for a in range(na):
            for j, chip in enumerate(chips):
                copy(a, 1 + j, (*chip, c), me).wait_recv()
                fwd = copy(a, 4 + j, (*chip, c), sibling)
                fwd.start()
                passed.append(fwd)
        for a in range(na):
            copy(a, 0, sibling, me).wait_recv()
            for j, chip in enumerate(chips):
                copy(a, 4 + j, (*chip, 1 - c), me).wait_recv()
        for cp in first + passed:
            cp.wait_send()
        for cp in mine:
            cp.wait()

    return pl.pallas_call(
        kern, name=name, in_specs=[ANY] * na, out_specs=[ANY] * na,
        out_shape=[jax.ShapeDtypeStruct((N_DEV,) + a.shape, a.dtype) for a in arrs],
        scratch_shapes=[pltpu.SemaphoreType.DMA((na, 7)), pltpu.SemaphoreType.DMA((na, 7)),
                        pltpu.SemaphoreType.DMA((na,))])(*arrs)


HBM = pl.BlockSpec(memory_space=pltpu.HBM)
SEM = pl.BlockSpec(memory_space=pltpu.SEMAPHORE)
EFFECT = pltpu.SideEffectType.DATAFLOW_SIDE_EFFECTING
N_CHIP = 4


def _in_hbm(a):
    return pltpu.with_memory_space_constraint(a, pltpu.HBM)


def _chip_copies(kind, src_refs, land_refs, send_sems, recv_sems):
    x, y, c = lax.axis_index("x"), lax.axis_index("y"), lax.axis_index("c")
    cps = []
    for a in range(len(src_refs)):
        for j, (px, py) in enumerate([(1 - x, y), (x, 1 - y), (1 - x, 1 - y)]):
            if kind == "gather":
                src, dst = src_refs[a], land_refs[a].at[4 * x + 2 * y + c]
            else:
                src, dst = src_refs[a].at[2 * px + py], land_refs[a].at[2 * x + y]
            cps.append(pltpu.make_async_remote_copy(
                src_ref=src, dst_ref=dst, send_sem=send_sems.at[3 * a + j], recv_sem=recv_sems.at[3 * a + j],
                device_id=(px, py, c), device_id_type=MESH))
    return cps


def _exchange_start(kind, srcs, lands, after, name):
    na = len(srcs)
    n_after = 0 if after is None else 1

    def kern(*refs):
        src_refs, land_refs = refs[:na], refs[na:2 * na]
        send_sems, recv_sems = refs[2 * na + n_after], refs[2 * na + n_after + 1]
        token = refs[-1]
        for cp in _chip_copies(kind, src_refs, land_refs, send_sems, recv_sems):
            cp.start()
        token[...] = jnp.zeros_like(token)

    bufs = list(srcs) + list(lands)
    res = pl.pallas_call(
        kern, name=name,
        out_shape=(pltpu.SemaphoreType.DMA((3 * na,)), pltpu.SemaphoreType.DMA((3 * na,)))
        + tuple(pltpu.HBM(b.shape, b.dtype) for b in bufs) + (jax.ShapeDtypeStruct((8, LANES), F32),),
        in_specs=[HBM] * (2 * na) + [ANY] * n_after,
        out_specs=(SEM, SEM) + (HBM,) * (2 * na) + (pl.BlockSpec(memory_space=pltpu.VMEM),),
        input_output_aliases={i: 2 + i for i in range(2 * na)},
        compiler_params=pltpu.CompilerParams(has_side_effects=EFFECT),
    )(*[_in_hbm(b) for b in bufs], *([] if after is None else [after]))
    return (res[0], res[1]), list(res[2:2 + na]), list(res[2 + na:2 + 2 * na]), res[-1]


def _exchange_wait(kind, sems, srcs, lands, after, name):
    na = len(srcs)

    def kern(*refs):
        src_refs, land_refs = refs[:na], refs[na:2 * na]
        send_sems, recv_sems = refs[2 * na], refs[2 * na + 1]
        for cp in _chip_copies(kind, src_refs, land_refs, send_sems, recv_sems):
            cp.wait_send()
            cp.wait_recv()

    bufs = list(srcs) + list(lands)
    res = pl.pallas_call(
        kern, name=name, out_shape=tuple(pltpu.HBM(b.shape, b.dtype) for b in bufs),
        in_specs=[HBM] * (2 * na) + [SEM, SEM, ANY], out_specs=(HBM,) * (2 * na),
        input_output_aliases={i: i for i in range(2 * na)},
        compiler_params=pltpu.CompilerParams(has_side_effects=EFFECT),
    )(*bufs, sems[0], sems[1], after)
    return list(res[:na]), list(res[na:])


def _gather_finish(shards, lands, name):
    na = len(shards)

    def kern(*refs):
        ins, outs = refs[:na], refs[2 * na:3 * na]
        send_sems, recv_sems, local_sems = refs[3 * na:]
        x, y, c = lax.axis_index("x"), lax.axis_index("y"), lax.axis_index("c")
        chips = [(1 - x, y), (x, 1 - y), (1 - x, 1 - y)]

        def copy(a, k, slot, pc, src=None):
            dst = outs[a].at[slot + pc]
            return pltpu.make_async_remote_copy(
                src_ref=dst if src is None else src, dst_ref=dst, send_sem=send_sems.at[a, k],
                recv_sem=recv_sems.at[a, k], device_id=(x, y, 1 - c), device_id_type=MESH)

        mine = [pltpu.make_async_copy(ins[a], outs[a].at[4 * x + 2 * y + c], local_sems.at[a]) for a in range(na)]
        sends = []
        for a in range(na):
            sends.append(copy(a, 0, 4 * x + 2 * y, c, src=ins[a]))
            sends += [copy(a, 1 + j, 4 * px + 2 * py, c) for j, (px, py) in enumerate(chips)]
        for cp in mine + sends:
            cp.start()
        for a in range(na):
            copy(a, 0, 4 * x + 2 * y, 1 - c).wait_recv()
            for j, (px, py) in enumerate(chips):
                copy(a, 1 + j, 4 * px + 2 * py, 1 - c).wait_recv()
        for cp in sends:
            cp.wait_send()
        for cp in mine:
            cp.wait()

    return pl.pallas_call(
        kern, name=name, in_specs=[ANY] * (2 * na), out_specs=[ANY] * na,
        out_shape=[jax.ShapeDtypeStruct(l.shape, l.dtype) for l in lands],
        input_output_aliases={na + a: a for a in range(na)},
        scratch_shapes=[pltpu.SemaphoreType.DMA((na, 4)), pltpu.SemaphoreType.DMA((na, 4)),
                        pltpu.SemaphoreType.DMA((na,))])(*shards, *lands)


def _pair_exchange(parts, name):
    na = len(parts)

    def kern(*refs):
        ins, mine, got = refs[:na], refs[na:2 * na], refs[2 * na:3 * na]
        send_sems, recv_sems, local_sems = refs[3 * na:]
        x, y, c = lax.axis_index("x"), lax.axis_index("y"), lax.axis_index("c")
        local, sends = [], []
        for a in range(na):
            for q in range(N_CHIP):
                local.append(pltpu.make_async_copy(ins[a].at[2 * q + c], mine[a].at[q], local_sems.at[a, q]))
                sends.append(pltpu.make_async_remote_copy(
                    src_ref=ins[a].at[2 * q + 1 - c], dst_ref=got[a].at[q], send_sem=send_sems.at[a, q],
                    recv_sem=recv_sems.at[a, q], device_id=(x, y, 1 - c), device_id_type=MESH))
        for cp in local + sends:
            cp.start()
        for cp in sends:
            cp.wait()
        for cp in local:
            cp.wait()

    half = [jax.ShapeDtypeStruct((N_CHIP,) + p.shape[1:], p.dtype) for p in parts]
    res = pl.pallas_call(
        kern, name=name, in_specs=[ANY] * na, out_specs=[ANY] * (2 * na), out_shape=half + half,
        scratch_shapes=[pltpu.SemaphoreType.DMA((na, N_CHIP)), pltpu.SemaphoreType.DMA((na, N_CHIP)),
                        pltpu.SemaphoreType.DMA((na, N_CHIP))])(*parts)
    return res[:na], res[na:]


def _pair_sum(a, b, name):
    shp = a.shape
    rows = shp[0] * shp[1]

    def body(i, n, a_ref, b_ref, q_ref, l_ref):
        s = (a_ref[...].astype(F32) + b_ref[...].astype(F32)).astype(BF)
        q_ref[...] = s
        l_ref[...] = s

    q, l = _rowwise(body, name=name, nrows=rows, tile=ROW_TILE,
                    row_ins=[(a.reshape(rows, shp[2]), shp[2], 0), (b.reshape(rows, shp[2]), shp[2], 0)],
                    row_outs=[(shp[2], BF), (shp[2], BF)])
    return q.reshape(shp), l.reshape(shp)


def _perm(t, d):
    return t if d == 1 else t.reshape(S // d, d, t.shape[-1]).transpose(1, 0, 2).reshape(S, t.shape[-1])


def _unperm(t, d):
    return t if d == 1 else t.reshape(d, S // d, t.shape[-1]).transpose(1, 0, 2).reshape(S, t.shape[-1])


def _pad_lanes(v, width=LANES):
    return jnp.pad(v, ((0, 0), (0, width - v.shape[1])))


def _row_layout(t16):
    return t16.T.reshape(N_HEADS // 2, 2, NB, BLK).transpose(0, 2, 1, 3)


def _row_layout_inv(r):
    return r.transpose(0, 2, 1, 3).reshape(N_HEADS, S).T


SMALL = (("g_mix", D), ("g_q", HEAD), ("g_k", HEAD), ("g_attn_out", D_ATTN), ("conv_b", D_CONV),
         ("dt_bias", 16), ("a_log", 16), ("d_skip", 16), ("g_ssm_out", D_SSM), ("g_cross", D),
         ("g_mem", D), ("g_cq", CROSS_HEAD), ("g_ck", CROSS_HEAD), ("g_mlp", D))
SMALL_ROWS = -(-sum(-(-w // LANES) for _, w in SMALL) // 8) * 8


def _pack_small(vals):
    rows = [_pad_lanes(vals[n], -(-w // LANES) * LANES).reshape(-1, LANES) for n, w in SMALL]
    packed = jnp.concatenate(rows, axis=0)
    return jnp.pad(packed, ((0, SMALL_ROWS - packed.shape[0]), (0, 0)))


def _unpack_small(packed):
    out, r = {}, 0
    for n, w in SMALL:
        nr = -(-w // LANES)
        out[n] = packed[r:r + nr].reshape(1, nr * LANES)[:, :w]
        r += nr
    return out


BIG = ("w_in", "w_out", "w_cq", "w_ckv", "w_co", "w_up", "w_down")
WEIGHTS = ("g_mix", "w_in", "g_q", "g_k", "g_attn_out", "conv_w", "conv_b", "dt_bias", "a_log", "d_skip",
           "g_ssm_out", "w_out", "g_cross", "g_mem", "w_cq", "w_ckv", "g_cq", "g_ck", "w_co", "g_mlp", "w_up", "w_down")


REST = ("w_out", "w_cq", "w_ckv", "w_co", "w_up", "w_down")


class _Overlap:
    def __init__(self, shards, after):
        lands = [lax.empty((N_DEV,) + s.shape, s.dtype) for s in shards]
        self.gather = _exchange_start("gather", shards, lands, after, "ag_rest_start")
        self.token = self.gather[3]
        self.groups = []

    def rest(self, after):
        sems, srcs, lands, _ = self.gather
        srcs, lands = _exchange_wait("gather", sems, srcs, lands, after, "ag_rest_wait")
        wf = dict(zip(REST, _gather_finish(srcs, lands, "ag_rest_finish")))
        for n in ("w_out", "w_cq", "w_ckv", "w_down"):
            wf[n] = wf[n].reshape(-1, wf[n].shape[-1])
        return wf

    def emit(self, grads):
        names, tag = list(grads), "_%d" % len(self.groups)
        mine, got = _pair_exchange([grads[n] for n in names], "rs_pair" + tag)
        sums = [_pair_sum(a, b, "rs_sum_" + n) for n, a, b in zip(names, mine, got)]
        sems, srcs, lands, token = _exchange_start("scatter", [q for q, _ in sums], [l for _, l in sums], None,
                                                   "rs_chip_start" + tag)
        self.groups.append((names, sems, srcs, lands))
        return token

    def finish(self, gi, after):
        names, sems, srcs, lands = self.groups[gi]
        _, lands = _exchange_wait("scatter", sems, srcs, lands, after, "rs_chip_wait_%d" % gi)
        return dict(zip(names, lands))


def _tie(v, token):
    return v if token is None else v + token[0:1, 0:1]


def _local_step(x, mem, pos_col, target, p, wf, hooks=None):
    p = dict(p)
    inv = np.zeros((1, LANES), np.float32)
    freq = (ROPE_THETA ** (-2.0 * np.arange(ROT // 2, dtype=np.float32) / ROT)).astype(np.float32)
    for l in range(LANES):
        if l % HEAD < ROT:
            inv[0, l] = freq[(l % HEAD) % (ROT // 2)]
    inv_tab = jnp.asarray(inv)
    gq128, gk128 = jnp.tile(p["g_q"], (1, 2)), jnp.tile(p["g_k"], (1, 2))
    dtb128, alog128 = _pad_lanes(p["dt_bias"]), _pad_lanes(p["a_log"])
    dskip_b = jnp.repeat(p["d_skip"], HEAD, axis=1)
    conv_w, conv_b = wf["conv_w"], p["conv_b"]

    h = _rms_fwd(x, p["g_mix"], "rms_mix")
    proj = _matmul(h, wf["w_in"], mode="nn", name="mm_in", tm=1024, tn=896, tk=512)
    qr, kr, vb = _qk_prep(proj, pos_col, gq128, gk128, inv_tab)
    pats = ((1, NB), (4, NB // 4), (16, 1))
    qs = [_perm(qr, d) for d, _ in pats]
    ks = [_perm(kr, d) for d, _ in pats]
    vs = [_perm(vb, d) for d, _ in pats]
    os_, ls_ = [], []
    for (d, nb), qd, kd, vd in zip(pats, qs, ks, vs):
        o, l = _attn_fwd(qd, kd, vd, nb, "attn_fwd_d%d" % d)
        os_.append(_unperm(o, d))
        ls_.append(_unperm(l, d))
    attn, lse, attn_n = _attn_merge(os_, ls_, p["g_attn_out"])

    xbc = _conv_fwd(proj, conv_w, conv_b)
    dt, a_cs = _ssd_prep(proj, dtb128, alog128)
    a_b = jnp.repeat(a_cs[:, :N_HEADS], HEAD, axis=1)
    at_r, dt_r = _row_layout(a_cs[:, :N_HEADS]), _row_layout(dt[:, :N_HEADS])
    y_ssd = _ssd_fwd(xbc, a_b, at_r, dt_r)
    ssm_n = _ssd_post(y_ssd, xbc, proj, dskip_b, p["g_ssm_out"])

    if hooks is not None:
        wf = {**wf, **hooks.rest(ssm_n)}
    mix = jnp.concatenate([attn_n, ssm_n], axis=1)
    x1 = _matmul(mix, wf["w_out"], mode="nn", name="mm_out", tm=1024, tn=1024, tk=512,
                 extras=(x,), epilogue=lambda acc, r: (acc + r,))
    hc = _rms_fwd(x1, p["g_cross"], "rms_cross")
    memh = _rms_fwd(mem, p["g_mem"], "rms_mem")
    qc = _matmul(hc, wf["w_cq"], mode="nn", name="mm_cq", tm=1024, tn=512, tk=512)
    kv = _matmul(memh, wf["w_ckv"], mode="nn", name="mm_ckv", tm=N_MEM, tn=1024, tk=512)
    oc = _cross_fwd(qc, kv, p["g_cq"], p["g_ck"])
    x2 = _matmul(oc, wf["w_co"], mode="nn", name="mm_co", tm=1024, tn=256, tk=512, b_cb=256,
                 extras=(x1,), epilogue=lambda acc, r: (acc + r,))
    hm = _rms_fwd(x2, p["g_mlp"], "rms_mlp")

    def up_epi(acc):
        ru = jnp.maximum(acc, 0.0)
        return ru * ru, 2.0 * ru

    act, relu2 = _matmul(hm, wf["w_up"], mode="nn", name="mm_up", tm=1024, tn=1024, tk=512, b_cb=1024,
                         out_dtypes=(BF, BF), epilogue=up_epi)
    dy = _matmul(act, wf["w_down"], mode="nn", name="mm_down", tm=1024, tn=1024, tk=512, extras=(x2, target),
                 epilogue=lambda acc, r, t: ((acc + r - t) * (1.0 / D),))
    loss_part = _loss_sum(dy)[0, 0] * (0.5 * D)

    gb, gs = {}, {}
    du = _matmul(dy, wf["w_down"], mode="nt", name="mm_down_dx", tm=1024, tn=1024, tk=512, extras=(relu2,),
                 out_dtypes=(BF,), epilogue=lambda acc, r: (acc * r.astype(F32),))
    gb["w_down"] = _matmul(act, dy, mode="tn", name="mm_down_dw", tm=1024, tn=1024, tk=512,
                           out_dtypes=(BF,)).reshape(N_DEV, D_FF // N_DEV, D)
    gb["w_up"] = _matmul(hm, du, mode="tn", name="mm_up_dw", tm=1024, tn=1024, tk=512, out_dtypes=(BF,), out_cb=1024)
    if hooks is not None:
        p["g_mlp"] = _tie(p["g_mlp"], hooks.emit({n: gb[n] for n in ("w_down", "w_up")}))
    dhm = _matmul(du, wf["w_up"], mode="nt", name="mm_up_dx", tm=1024, tn=1024, tk=512, b_cb=1024)
    dx2, gs["g_mlp"] = _rms_bwd_call(x2, p["g_mlp"], dhm, dy, "rms_mlp_bwd")

    doc = _matmul(dx2, wf["w_co"], mode="nt", name="mm_co_dx", tm=1024, tn=512, tk=256, b_cb=256)
    gb["w_co"] = _matmul(oc, dx2, mode="tn", name="mm_co_dw", tm=512, tn=256, tk=512, out_dtypes=(BF,), out_cb=256)
    dqc, dkn, dvc, gs["g_cq"] = _cross_bwd(qc, doc, kv, p["g_cq"], p["g_ck"])
    dkv, gs["g_ck"] = _cross_kv_bwd(kv, dkn, dvc, p["g_ck"])
    gb["w_cq"] = _matmul(hc, dqc, mode="tn", name="mm_cq_dw", tm=1024, tn=512, tk=512,
                         out_dtypes=(BF,)).reshape(N_DEV, D // N_DEV, D_CROSS)
    dhc = _matmul(dqc, wf["w_cq"], mode="nt", name="mm_cq_dx", tm=1024, tn=1024, tk=512)
    gb["w_ckv"] = _matmul(memh, dkv, mode="tn", name="mm_ckv_dw", tm=1024, tn=1024, tk=N_MEM,
                          out_dtypes=(BF,)).reshape(N_DEV, D // N_DEV, 2 * D_CROSS)
    dmemh = _matmul(dkv, wf["w_ckv"], mode="nt", name="mm_ckv_dx", tm=N_MEM, tn=1024, tk=512)
    (gs["g_mem"],) = _rms_bwd_call(mem, p["g_mem"], dmemh, None, "rms_mem_bwd")
    dx1, gs["g_cross"] = _rms_bwd_call(x1, p["g_cross"], dhc, dx2, "rms_cross_bwd")

    dmix = _matmul(dx1, wf["w_out"], mode="nt", name="mm_out_dx", tm=1024, tn=1024, tk=512)
    gb["w_out"] = _matmul(mix, dx1, mode="tn", name="mm_out_dw", tm=1024, tn=1024, tk=512,
                          out_dtypes=(BF,)).reshape(N_DEV, D // N_DEV, D)
    if hooks is not None:
        p["g_attn_out"] = _tie(p["g_attn_out"], hooks.emit({n: gb[n] for n in ("w_co", "w_cq", "w_ckv", "w_out")}))

    dattn, delta, gs["g_attn_out"] = _attn_merge_bwd(dmix, attn, p["g_attn_out"])
    dqs, dks, dvs = [], [], []
    for (d, nb), qd, kd, vd in zip(pats, qs, ks, vs):
        dq, dk, dv = _attn_bwd(qd, kd, vd, _perm(dattn, d), _perm(lse, d), _perm(delta, d), nb, "attn_bwd_d%d" % d)
        dqs.append(_unperm(dq, d))
        dks.append(_unperm(dk, d))
        dvs.append(_unperm(dv, d))
    dq_pre, dk_pre, dv_pre, dgq, dgk = _qk_bwd(dqs, dks, dvs, proj, pos_col, gq128, gk128, inv_tab)
    gs["g_q"], gs["g_k"] = dgq[:, :HEAD], dgk[:, :HEAD]

    dy_ssd, dz, dxs_skip, dd_lane, gs["g_ssm_out"] = _ssd_post_bwd(y_ssd, xbc, proj, dmix, dskip_b, p["g_ssm_out"])
    gs["d_skip"] = dd_lane.reshape(N_HEADS, HEAD).sum(axis=1).reshape(1, N_HEADS)
    dc_pair, daq_b, dx_ssd, db_pair, dak_r, ddt_r = _ssd_bwd(xbc, dy_ssd, a_b, at_r, dt_r)
    daq = _pad_lanes(daq_b[:, ::HEAD])
    dak = _pad_lanes(_row_layout_inv(dak_r))
    ddt_direct = _pad_lanes(_row_layout_inv(ddt_r))
    ddt_raw, dalog, dbias = _ssd_prep_bwd(daq, dak, ddt_direct, dt, proj, dtb128, alog128)
    gs["a_log"], gs["dt_bias"] = dalog[:, :N_HEADS], dbias[:, :N_HEADS]
    same = lambda c: c
    dxbc_x, dcw_x, dcb_x = _conv_bwd(proj, conv_w, conv_b, dxs_skip, dx_ssd, same, same, 0, 8, "conv_bwd_x")
    dxbc_b, dcw_b, dcb_b = _conv_bwd(proj, conv_w, conv_b, db_pair, db_pair, lambda c: 2 * c, lambda c: 2 * c + 1,
                                     D_SSM, 4, "conv_bwd_b")
    dxbc_c, dcw_c, dcb_c = _conv_bwd(proj, conv_w, conv_b, dc_pair, dc_pair, lambda c: 2 * c, lambda c: 2 * c + 1,
                                     D_SSM + 512, 4, "conv_bwd_c")
    g_conv_w = jnp.concatenate([dcw_x, dcw_b, dcw_c], axis=1)
    gs["conv_b"] = jnp.concatenate([dcb_x, dcb_b, dcb_c], axis=1)

    dproj = jnp.concatenate([dq_pre, dk_pre, dv_pre, dz, dxbc_x, dxbc_b, dxbc_c, ddt_raw], axis=1)
    dw_in = _matmul(h, dproj, mode="tn", name="mm_in_dw", tm=1024, tn=896, tk=512, out_dtypes=(BF,))
    gb["w_in"] = dw_in[:, :D_IN].reshape(D, N_DEV, D_IN // N_DEV).transpose(1, 0, 2)
    if hooks is not None:
        p["g_mix"] = _tie(p["g_mix"], hooks.emit({"w_in": gb["w_in"]}))
    dh = _matmul(dproj, wf["w_in"], mode="nt", name="mm_in_dx", tm=1024, tn=1024, tk=896)
    grad_x, gs["g_mix"] = _rms_bwd_call(x, p["g_mix"], dh, dx1, "rms_mix_bwd")
    return loss_part, grad_x, gb, gs, g_conv_w


def kernel(x, mem, positions, g_mix, w_in, g_q, g_k, g_attn_out, conv_w, conv_b, dt_bias, a_log, d_skip, g_ssm_out, w_out, g_cross, g_mem, w_cq, w_ckv, g_cq, g_ck, w_co, g_mlp, w_up, w_down, loss_target, m_g_mix, m_w_in, m_g_q, m_g_k, m_g_attn_out, m_conv_w, m_conv_b, m_dt_bias, m_a_log, m_d_skip, m_g_ssm_out, m_w_out, m_g_cross, m_g_mem, m_w_cq, m_w_ckv, m_g_cq, m_g_ck, m_w_co, m_g_mlp, m_w_up, m_w_down, v_g_mix, v_w_in, v_g_q, v_g_k, v_g_attn_out, v_conv_w, v_conv_b, v_dt_bias, v_a_log, v_d_skip, v_g_ssm_out, v_w_out, v_g_cross, v_g_mem, v_w_cq, v_w_ckv, v_g_cq, v_g_ck, v_w_co, v_g_mlp, v_w_up, v_w_down):
    args = dict(locals())
    w = {n: args[n] for n in WEIGHTS}
    m = {n: args["m_" + n] for n in WEIGHTS}
    v = {n: args["v_" + n] for n in WEIGHTS}
    small = {n: w[n] for n, _ in SMALL}
    me = 4 * lax.axis_index("x") + 2 * lax.axis_index("y") + lax.axis_index("c")

    w_in_g, conv_w_g = _all_gather([w["w_in"][0].astype(BF), w["conv_w"][0]], "ag_first")
    wf = {"w_in": jnp.pad(w_in_g.transpose(1, 0, 2).reshape(D, D_IN), ((0, 0), (0, D_INP - D_IN))),
          "conv_w": conv_w_g.transpose(1, 0, 2).reshape(CONV_W, D_CONV)}
    overlap = _Overlap([w[n][0].astype(BF) for n in REST], w_in_g)
    p = dict(small)
    p["g_mix"] = _tie(p["g_mix"], overlap.token)

    loss_part, grad_x, _, gs, g_conv_w = _local_step(
        x[0], mem[0], positions.reshape(S, 1), loss_target[0], p, wf, overlap)
    loss = lax.psum(loss_part, ("x", "y", "c"))

    out = {}
    last = grad_x
    for gi in range(3):
        for n, parts in overlap.finish(gi, last).items():
            res_n = _adamw(w[n][0], m[n][0], v[n][0], parts, "adamw_" + n)
            out[n] = [t.reshape(w[n].shape) for t in res_n]
            last = res_n[0]

    sm_parts, cw_parts = _all_gather([_pack_small(gs), g_conv_w], "ag_small_grads")
    sm = [_unpack_small(t) for t in _adamw(_pack_small(small), _pack_small({n: m[n] for n, _ in SMALL}),
                                           _pack_small({n: v[n] for n, _ in SMALL}), sm_parts, "adamw_small")]
    for n, _ in SMALL:
        out[n] = [t[n] for t in sm]
    cols = D_CONV // N_DEV
    cw_mine = lax.dynamic_slice_in_dim(cw_parts, me * cols, cols, axis=2)
    out["conv_w"] = [t.reshape(w["conv_w"].shape) for t in
                     _adamw(w["conv_w"][0], m["conv_w"][0], v["conv_w"][0], cw_mine, "adamw_conv_w")]

    res = [loss, grad_x.reshape(x.shape)]
    for k in range(4):
        res += [out[n][k] for n in WEIGHTS]
    return tuple(res)
```

```python
import functools
import math

import numpy as np
import jax
import jax.numpy as jnp
from jax import lax
from jax.experimental import pallas as pl
from jax.experimental.pallas import tpu as pltpu

F32 = jnp.float32
BF = jnp.bfloat16

N_DEV = 8
S = 2048
D = 2048
D_ATTN = 1024
N_HEADS = 16
HEAD = 64
ROT = 16
ROPE_THETA = 500000.0
D_SSM = 1024
D_CONV = 2048
CONV_W = 4
N_MEM = 256
D_CROSS = 512
CROSS_HEAD = 128
D_FF = 8192
D_IN = 6160
D_INP = 6272
EPS = 1e-6
BLK = 128
NB = S // BLK
LANES = 128
NEG = -1e30

ADAM_LR = 0.001
ADAM_B1 = 0.9
ADAM_B2 = 0.999
ADAM_EPS = 1e-08
ADAM_WD = 0.01
ADAM_STEP = 10

VMEM_LIMIT_BYTES = 48 * 1024 * 1024
ROW_TILE = 256


def _params(*sem):
    return pltpu.CompilerParams(dimension_semantics=sem, vmem_limit_bytes=VMEM_LIMIT_BYTES)


def _matmul(a, b, *, mode, name, tm, tn, tk, out_dtypes=(F32,), b_cb=None, out_cb=None,
            extras=(), epilogue=None):
    if mode == "tn":
        kk, m = a.shape
    else:
        m, kk = a.shape
    if b_cb is None:
        br, bc = b.shape
    else:
        br, bc = b.shape[1], N_DEV * b_cb
    n = br if mode == "nt" else bc
    assert m % tm == 0 and n % tn == 0 and kk % tk == 0, (name, m, n, kk)
    nk = kk // tk
    grid = (m // tm, n // tn, nk)

    if mode == "tn":
        a_spec = pl.BlockSpec((tk, tm), lambda i, j, k: (k, i))
    else:
        a_spec = pl.BlockSpec((tm, tk), lambda i, j, k: (i, k))
    if mode == "nt":
        b_blk, b_idx = (tn, tk), (lambda i, j, k: (j, k))
    else:
        b_blk, b_idx = (tk, tn), (lambda i, j, k: (k, j))
    if b_cb is None:
        b_spec = pl.BlockSpec(b_blk, b_idx)
    else:
        tc = b_blk[1]
        assert b_cb % tc == 0
        per = b_cb // tc

        def b_idx3(i, j, k):
            r, c = b_idx(i, j, k)
            return (c // per, r, c % per)
        b_spec = pl.BlockSpec((None,) + b_blk, b_idx3)
    ex_specs = [pl.BlockSpec((tm, tn), lambda i, j, k: (i, j)) for _ in extras]
    if out_cb is None:
        out_shape = [jax.ShapeDtypeStruct((m, n), dt) for dt in out_dtypes]
        out_specs = [pl.BlockSpec((tm, tn), lambda i, j, k: (i, j)) for _ in out_dtypes]
    else:
        assert out_cb % tn == 0
        pero = out_cb // tn
        out_shape = [jax.ShapeDtypeStruct((N_DEV, m, out_cb), dt) for dt in out_dtypes]
        out_specs = [pl.BlockSpec((None, tm, tn), lambda i, j, k: (j // pero, i, j % pero)) for _ in out_dtypes]
    dn = {"nn": (((1,), (0,)), ((), ())), "nt": (((1,), (1,)), ((), ())), "tn": (((0,), (0,)), ((), ()))}[mode]
    ne, no = len(extras), len(out_dtypes)

    def kern(a_ref, b_ref, *rest):
        ex_refs, out_refs, acc_ref = rest[:ne], rest[ne:ne + no], rest[ne + no]
        k = pl.program_id(2)

        @pl.when(k == 0)
        def _():
            acc_ref[...] = jnp.zeros_like(acc_ref)

        acc_ref[...] += lax.dot_general(a_ref[...].astype(BF), b_ref[...].astype(BF), dn,
                                        preferred_element_type=F32)

        @pl.when(k == nk - 1)
        def _():
            acc = acc_ref[...]
            outs = (acc,) if epilogue is None else epilogue(acc, *[r[...] for r in ex_refs])
            for r, o in zip(out_refs, outs):
                r[...] = o.astype(r.dtype)

    res = pl.pallas_call(
        kern, name=name, grid=grid, in_specs=[a_spec, b_spec] + ex_specs, out_specs=out_specs,
        out_shape=out_shape, scratch_shapes=[pltpu.VMEM((tm, tn), F32)],
        compiler_params=_params("parallel", "parallel", "arbitrary"))(a, b, *extras)
    return res[0] if no == 1 else tuple(res)


def _rowwise(body, *, name, nrows, tile, row_ins, full_ins=(), row_outs=(), acc_outs=(), scratch=(),
             reverse=False):
    n = nrows // tile

    def ridx(i):
        return (n - 1 - i) if reverse else i

    in_specs, args = [], []
    for arr, width, cb in row_ins:
        in_specs.append(pl.BlockSpec((tile, width), lambda i, cb=cb: (ridx(i), cb)))
        args.append(arr)
    for arr in full_ins:
        in_specs.append(pl.BlockSpec(arr.shape, lambda i, nd=arr.ndim: (0,) * nd))
        args.append(arr)
    out_shape, out_specs = [], []
    for width, dt in row_outs:
        out_shape.append(jax.ShapeDtypeStruct((nrows, width), dt))
        out_specs.append(pl.BlockSpec((tile, width), lambda i: (ridx(i), 0)))
    for shp, dt in acc_outs:
        out_shape.append(jax.ShapeDtypeStruct(shp, dt))
        out_specs.append(pl.BlockSpec(shp, lambda i, nd=len(shp): (0,) * nd))

    def kern(*refs):
        body(pl.program_id(0), n, *refs)

    return pl.pallas_call(kern, name=name, grid=(n,), in_specs=in_specs, out_specs=out_specs,
                          out_shape=out_shape, scratch_shapes=list(scratch),
                          compiler_params=_params("arbitrary"))(*args)


def _accum(ref, val, i):
    @pl.when(i == 0)
    def _():
        ref[...] = val

    @pl.when(i > 0)
    def _():
        ref[...] += val


def _sigmoid(x):
    return 1.0 / (1.0 + jnp.exp(-x))


def _rms_n(x):
    r = lax.rsqrt(jnp.mean(x * x, axis=-1, keepdims=True) + EPS)
    return x * r, r


def _rms_bwd(x, g, dh):
    n, r = _rms_n(x)
    dn = dh * g
    dx = r * (dn - n * jnp.mean(dn * n, axis=-1, keepdims=True))
    return dx, jnp.sum(dh * n, axis=0, keepdims=True)


def _seg_sum(x, seg):
    t, w = x.shape
    tiles = [x[:, LANES * j:LANES * (j + 1)] for j in range(w // LANES)]
    outs = []
    if seg == 64:
        lo = lax.broadcasted_iota(jnp.int32, (t, LANES), 1) < 64
        for xt in tiles:
            s_lo = jnp.sum(jnp.where(lo, xt, 0.0), axis=-1, keepdims=True)
            s_hi = jnp.sum(jnp.where(lo, 0.0, xt), axis=-1, keepdims=True)
            outs.append(jnp.where(lo, s_lo, s_hi))
    else:
        sums = [jnp.sum(xt, axis=-1, keepdims=True) for xt in tiles]
        if seg == 256:
            sums = [sums[2 * (j // 2)] + sums[2 * (j // 2) + 1] for j in range(len(sums))]
        else:
            assert seg == 128
        outs = [jnp.broadcast_to(s, (t, LANES)) for s in sums]
    return outs[0] if len(outs) == 1 else jnp.concatenate(outs, axis=1)


def _seg_rms_n(x, seg):
    r = lax.rsqrt(_seg_sum(x * x, seg) * (1.0 / seg) + EPS)
    return x * r, r


def _seg_rms_bwd(x, g, dy, seg):
    n, r = _seg_rms_n(x, seg)
    dn = dy * g
    dx = r * (dn - n * (_seg_sum(dn * n, seg) * (1.0 / seg)))
    return dx, jnp.sum(dy * n, axis=0, keepdims=True)


def _rope_tables(pos_col, inv_tab, t):
    ang = pos_col.astype(F32) * inv_tab
    idx = lax.broadcasted_iota(jnp.int32, (t, LANES), 1) % HEAD
    cos, sin = jnp.cos(ang), jnp.sin(ang)
    c = jnp.where(idx < ROT, cos, 1.0)
    sg = jnp.where(idx < ROT // 2, -sin, jnp.where(idx < ROT, sin, 0.0))
    return c, sg, idx < ROT // 2


def _rope(x, c, sg, lo):
    partner = jnp.where(lo, pltpu.roll(x, LANES - ROT // 2, 1), pltpu.roll(x, ROT // 2, 1))
    return x * c + partner * sg


def _fold_heads(v):
    v8 = jnp.broadcast_to(v, (8, LANES))
    return (v8 + pltpu.roll(v8, HEAD, 1))[0:1]


def _dot(a, b, dn):
    return lax.dot_general(a, b, (dn, ((), ())), preferred_element_type=F32)


NN = ((1,), (0,))
NT = ((1,), (1,))
TN = ((0,), (0,))


def _dot3(l01, x):
    x1 = x.astype(BF)
    r1 = x - x1.astype(F32)
    x2 = r1.astype(BF)
    x3 = (r1 - x2.astype(F32)).astype(BF)
    return _dot(l01, x1, NN) + _dot(l01, x2, NN) + _dot(l01, x3, NN)


def _rms_fwd(x, g, name):
    rows = x.shape[0]

    def body(i, n, x_ref, g_ref, o_ref):
        nx, _ = _rms_n(x_ref[...])
        o_ref[...] = (nx * g_ref[...]).astype(BF)

    return _rowwise(body, name=name, nrows=rows, tile=min(ROW_TILE, rows), row_ins=[(x, D, 0)],
                    full_ins=[g], row_outs=[(D, BF)])[0]


def _qk_prep(proj, pos_col, gq128, gk128, inv_tab):
    scale = HEAD ** -0.5

    def body(i, n, q_ref, k_ref, v_ref, p_ref, gq_ref, gk_ref, it_ref, qo_ref, ko_ref, vo_ref):
        t = q_ref.shape[0]
        c, sg, lo = _rope_tables(p_ref[...], it_ref[...], t)
        for j in range(D_ATTN // LANES):
            sl = slice(LANES * j, LANES * (j + 1))
            qn, _ = _seg_rms_n(q_ref[:, sl], HEAD)
            kn, _ = _seg_rms_n(k_ref[:, sl], HEAD)
            qo_ref[:, sl] = (_rope(qn * gq_ref[...], c, sg, lo) * scale).astype(BF)
            ko_ref[:, sl] = _rope(kn * gk_ref[...], c, sg, lo).astype(BF)
        vo_ref[...] = v_ref[...].astype(BF)

    return _rowwise(body, name="qk_prep", nrows=S, tile=ROW_TILE,
                    row_ins=[(proj, D_ATTN, 0), (proj, D_ATTN, 1), (proj, D_ATTN, 2), (pos_col, 1, 0)],
                    full_ins=[gq128, gk128, inv_tab], row_outs=[(D_ATTN, BF)] * 3)


def _band_scores(q, k_ref, b, nb, h_mask):
    row = lax.broadcasted_iota(jnp.int32, (BLK, BLK), 0)
    col = lax.broadcasted_iota(jnp.int32, (BLK, BLK), 1)
    qm = jnp.where(h_mask, q.astype(F32), 0.0).astype(BF)
    kc = k_ref[pl.ds(pl.multiple_of(b * BLK, BLK), BLK), :]
    s_c = jnp.where(col <= row, _dot(qm, kc, NT), NEG)
    if nb == 1:
        return qm, kc, s_c, None, None
    pb = jnp.maximum(b - 1, 0)
    kp = k_ref[pl.ds(pl.multiple_of(pb * BLK, BLK), BLK), :]
    ok = jnp.logical_and(col >= row, (b % nb) != 0)
    s_p = jnp.where(ok, _dot(qm, kp, NT), NEG)
    return qm, kc, s_c, kp, s_p


def _attn_fwd(q, k, v, nb, name):
    def kern(q_ref, k_ref, v_ref, o_ref, l_ref):
        b = pl.program_id(1)
        q = q_ref[...]
        half0 = lax.broadcasted_iota(jnp.int32, (BLK, LANES), 1) < HEAD
        o_h, l_h = [], []
        for h in range(2):
            hm = half0 if h == 0 else jnp.logical_not(half0)
            _, _, s_c, _, s_p = _band_scores(q, k_ref, b, nb, hm)
            m = jnp.max(s_c, axis=-1, keepdims=True)
            if nb > 1:
                m = jnp.maximum(m, jnp.max(s_p, axis=-1, keepdims=True))
            p_c = jnp.exp(s_c - m)
            den = jnp.sum(p_c, axis=-1, keepdims=True)
            vc = v_ref[pl.ds(pl.multiple_of(b * BLK, BLK), BLK), :]
            o = _dot(p_c.astype(BF), vc, NN)
            if nb > 1:
                p_p = jnp.exp(s_p - m)
                den = den + jnp.sum(p_p, axis=-1, keepdims=True)
                pb = jnp.maximum(b - 1, 0)
                vp = v_ref[pl.ds(pl.multiple_of(pb * BLK, BLK), BLK), :]
                o = o + _dot(p_p.astype(BF), vp, NN)
            o_h.append(o / den)
            l_h.append(m + jnp.log(den))
        o_ref[...] = jnp.where(half0, o_h[0], o_h[1])
        l_ref[...] = jnp.where(half0, l_h[0], l_h[1])

    tile = pl.BlockSpec((BLK, LANES), lambda p, b: (b, p))
    seq = pl.BlockSpec((S, LANES), lambda p, b: (0, p))
    return pl.pallas_call(
        kern, name=name, grid=(D_ATTN // LANES, NB), in_specs=[tile, seq, seq], out_specs=[tile, tile],
        out_shape=[jax.ShapeDtypeStruct((S, D_ATTN), F32)] * 2,
        compiler_params=_params("parallel", "arbitrary"))(q, k, v)


def _attn_merge(os_, ls_, g_attn):
    def body(i, n, o1, o2, o3, l1, l2, l3, g_ref, a_ref, lse_ref, an_ref):
        la, lb, lc = l1[...], l2[...], l3[...]
        m = jnp.maximum(jnp.maximum(la, lb), lc)
        ea, eb, ec = jnp.exp(la - m), jnp.exp(lb - m), jnp.exp(lc - m)
        den = ea + eb + ec
        attn = (ea * o1[...] + eb * o2[...] + ec * o3[...]) / den
        a_ref[...] = attn
        lse_ref[...] = m + jnp.log(den)
        nx, _ = _rms_n(attn)
        an_ref[...] = (nx * g_ref[...]).astype(BF)

    return _rowwise(body, name="attn_merge", nrows=S, tile=ROW_TILE,
                    row_ins=[(a, D_ATTN, 0) for a in (*os_, *ls_)], full_ins=[g_attn],
                    row_outs=[(D_ATTN, F32), (D_ATTN, F32), (D_ATTN, BF)])


def _conv_taps(x, w_ref):
    rows = lax.broadcasted_iota(jnp.int32, x.shape, 0)
    shifted = []
    for w in range(CONV_W):
        k = CONV_W - 1 - w
        shifted.append(x if k == 0 else jnp.where(rows >= k, pltpu.roll(x, k, 0), 0.0))
    acc = shifted[0] * w_ref[0:1, :]
    for w in range(1, CONV_W):
        acc = acc + shifted[w] * w_ref[w:w + 1, :]
    return acc, shifted


def _conv_fwd(proj, conv_w, conv_b):
    tc = 256

    def kern(x_ref, w_ref, b_ref, o_ref):
        c, _ = _conv_taps(x_ref[...], w_ref)
        c = c + b_ref[...]
        o_ref[...] = c * _sigmoid(c)

    return pl.pallas_call(
        kern, name="conv_fwd", grid=(D_CONV // tc,),
        in_specs=[pl.BlockSpec((S, tc), lambda c: (0, 4 * D_ATTN // tc + c)),
                  pl.BlockSpec((CONV_W, tc), lambda c: (0, c)), pl.BlockSpec((1, tc), lambda c: (0, c))],
        out_specs=pl.BlockSpec((S, tc), lambda c: (0, c)),
        out_shape=jax.ShapeDtypeStruct((S, D_CONV), F32), compiler_params=_params("parallel"))(proj, conv_w, conv_b)


def _softplus(x):
    return jnp.maximum(x, 0.0) + jnp.log1p(jnp.exp(-jnp.abs(x)))


def _ssd_prep(proj, dt_bias128, a_log128):
    def body(i, n, raw_ref, b_ref, al_ref, dt_ref, a_ref, carry):
        @pl.when(i == 0)
        def _():
            carry[...] = jnp.zeros_like(carry)

        dt = _softplus(raw_ref[...] + b_ref[...])
        da = dt * (-jnp.exp(al_ref[...]))
        tri = (lax.broadcasted_iota(jnp.int32, (BLK, BLK), 0) >= lax.broadcasted_iota(jnp.int32, (BLK, BLK), 1))
        cs = _dot3(tri.astype(BF), da) + carry[0:1, :]
        dt_ref[...] = dt
        a_ref[...] = cs
        carry[...] = jnp.broadcast_to(cs[BLK - 1:BLK, :], carry.shape)

    return _rowwise(body, name="ssd_prep", nrows=S, tile=BLK, row_ins=[(proj, LANES, (D_INP - LANES) // LANES)],
                    full_ins=[dt_bias128, a_log128], row_outs=[(LANES, F32), (LANES, F32)],
                    scratch=[pltpu.VMEM((8, LANES), F32)])


def _ssd_decay(a_col, at_row, causal):
    diff = jnp.where(causal, a_col - at_row, 0.0)
    return jnp.where(causal, jnp.exp(diff), 0.0)


def _ssd_fwd(xbc, a_b, at_r, dt_r):
    def kern(c_ref, b_ref, x_ref, ab_ref, at_ref, dt_ref, y_ref):
        i = pl.program_id(1)
        ci = c_ref[...].astype(BF)
        half0 = lax.broadcasted_iota(jnp.int32, (BLK, LANES), 1) < HEAD
        a_cols = (ab_ref[:, 0:1], ab_ref[:, HEAD:HEAD + 1])
        rowg = i * BLK + lax.broadcasted_iota(jnp.int32, (BLK, BLK), 0)

        def step(j, acc):
            off = pl.multiple_of(j * BLK, BLK)
            bj = b_ref[pl.ds(off, BLK), :].astype(BF)
            xj = x_ref[pl.ds(off, BLK), :]
            cb = _dot(ci, bj, NT)
            causal = (j * BLK + lax.broadcasted_iota(jnp.int32, (BLK, BLK), 1)) <= rowg
            at, dtj = at_ref[j], dt_ref[j]
            for h in range(2):
                hm = half0 if h == 0 else jnp.logical_not(half0)
                w = cb * _ssd_decay(a_cols[h], at[h:h + 1, :], causal) * dtj[h:h + 1, :]
                acc = acc + _dot(w.astype(BF), jnp.where(hm, xj, 0.0).astype(BF), NN)
            return acc

        y_ref[...] = lax.fori_loop(0, i + 1, step, jnp.zeros((BLK, LANES), F32))

    npair = D_SSM // LANES
    rowvec = pl.BlockSpec((None, NB, 2, BLK), lambda p, i: (p, 0, 0, 0))
    return pl.pallas_call(
        kern, name="ssd_fwd", grid=(npair, NB),
        in_specs=[pl.BlockSpec((BLK, LANES), lambda p, i: (i, 12 + p // 2)),
                  pl.BlockSpec((S, LANES), lambda p, i: (0, 8 + p // 2)),
                  pl.BlockSpec((S, LANES), lambda p, i: (0, p)),
                  pl.BlockSpec((BLK, LANES), lambda p, i: (i, p)), rowvec, rowvec],
        out_specs=pl.BlockSpec((BLK, LANES), lambda p, i: (i, p)),
        out_shape=jax.ShapeDtypeStruct((S, D_SSM), F32),
        compiler_params=_params("parallel", "arbitrary"))(xbc, xbc, xbc, a_b, at_r, dt_r)


def _ssd_post(y_ssd, xbc, proj, dskip_b, g_ssm):
    def body(i, n, y_ref, xs_ref, z_ref, d_ref, g_ref, o_ref):
        z = z_ref[...]
        y2 = (y_ref[...] + d_ref[...] * xs_ref[...]) * (z * _sigmoid(z))
        nx, _ = _seg_rms_n(y2, 256)
        o_ref[...] = (nx * g_ref[...]).astype(BF)

    return _rowwise(body, name="ssd_post", nrows=S, tile=ROW_TILE,
                    row_ins=[(y_ssd, D_SSM, 0), (xbc, D_SSM, 0), (proj, D_SSM, 3)], full_ins=[dskip_b, g_ssm],
                    row_outs=[(D_SSM, BF)])[0]


def _cross_heads(q, kv_ref, gq, gk):
    out = []
    for h in range(D_CROSS // CROSS_HEAD):
        sl = slice(CROSS_HEAD * h, CROSS_HEAD * (h + 1))
        nq, rq = _rms_n(q[:, sl])
        nk, rk = _rms_n(kv_ref[:, sl])
        v = kv_ref[:, D_CROSS + CROSS_HEAD * h:D_CROSS + CROSS_HEAD * (h + 1)]
        out.append((sl, nq, rq, nk, rk, v))
    return out


def _cross_fwd(qc, kv, g_cq, g_ck):
    scale = CROSS_HEAD ** -0.5

    def body(i, n, q_ref, kv_ref, gq_ref, gk_ref, o_ref):
        for sl, nq, _, nk, _, v in _cross_heads(q_ref[...], kv_ref, gq_ref[...], gk_ref[...]):
            qn = (nq * gq_ref[...] * scale).astype(BF)
            kn = (nk * gk_ref[...]).astype(BF)
            s = _dot(qn, kn, NT)
            e = jnp.exp(s - jnp.max(s, axis=-1, keepdims=True))
            p = e / jnp.sum(e, axis=-1, keepdims=True)
            o_ref[:, sl] = _dot(p.astype(BF), v.astype(BF), NN).astype(BF)

    return _rowwise(body, name="cross_fwd", nrows=S, tile=ROW_TILE, row_ins=[(qc, D_CROSS, 0)],
                    full_ins=[kv, g_cq, g_ck], row_outs=[(D_CROSS, BF)])[0]


def _loss_sum(dy):
    def body(i, n, d_ref, o_ref):
        d = d_ref[...]
        s = jnp.sum(jnp.sum(d * d, axis=0, keepdims=True), axis=1, keepdims=True)
        _accum(o_ref, s, i)

    return _rowwise(body, name="loss_sum", nrows=S, tile=ROW_TILE, row_ins=[(dy, D, 0)],
                    acc_outs=[((1, 1), F32)])[0]


def _rms_bwd_call(x, g, dh, dres, name):
    rows = x.shape[0]
    has_res = dres is not None

    def body(i, n, *refs):
        if has_res:
            x_ref, dh_ref, dr_ref, g_ref, dx_ref, dg_ref = refs
        else:
            x_ref, dh_ref, g_ref, dg_ref = refs
        dx, dg = _rms_bwd(x_ref[...], g_ref[...], dh_ref[...])
        if has_res:
            dx_ref[...] = dx + dr_ref[...]
        _accum(dg_ref, dg, i)

    row_ins = [(x, D, 0), (dh, D, 0)] + ([(dres, D, 0)] if has_res else [])
    return _rowwise(body, name=name, nrows=rows, tile=min(ROW_TILE, rows), row_ins=row_ins, full_ins=[g],
                    row_outs=[(D, F32)] if has_res else [], acc_outs=[((1, D), F32)])


def _cross_bwd(qc, doc, kv, g_cq, g_ck):
    scale = CROSS_HEAD ** -0.5

    def body(i, n, q_ref, do_ref, kv_ref, gq_ref, gk_ref, dq_ref, dkn_ref, dv_ref, dgq_ref):
        dgq = jnp.zeros((1, CROSS_HEAD), F32)
        dkn_parts, dv_parts = [], []
        for sl, nq, rq, nk, _, v in _cross_heads(q_ref[...], kv_ref, gq_ref[...], gk_ref[...]):
            qn = (nq * gq_ref[...] * scale).astype(BF)
            kn = (nk * gk_ref[...]).astype(BF)
            s = _dot(qn, kn, NT)
            e = jnp.exp(s - jnp.max(s, axis=-1, keepdims=True))
            p = e / jnp.sum(e, axis=-1, keepdims=True)
            do = do_ref[:, sl].astype(BF)
            dv_parts.append(_dot(p.astype(BF), do, TN))
            dp = _dot(do, v.astype(BF), NT)
            ds = (p * (dp - jnp.sum(dp * p, axis=-1, keepdims=True))).astype(BF)
            dqn = _dot(ds, kn, NN)
            dkn_parts.append(_dot(ds, qn, TN))
            dn = dqn * (gq_ref[...] * scale)
            dq_ref[:, sl] = (rq * (dn - nq * jnp.mean(dn * nq, axis=-1, keepdims=True))).astype(BF)
            dgq = dgq + jnp.sum(dqn * scale * nq, axis=0, keepdims=True)
        _accum(dkn_ref, jnp.concatenate(dkn_parts, axis=1), i)
        _accum(dv_ref, jnp.concatenate(dv_parts, axis=1), i)
        _accum(dgq_ref, dgq, i)

    return _rowwise(body, name="cross_bwd", nrows=S, tile=ROW_TILE, row_ins=[(qc, D_CROSS, 0), (doc, D_CROSS, 0)],
                    full_ins=[kv, g_cq, g_ck], row_outs=[(D_CROSS, BF)],
                    acc_outs=[((N_MEM, D_CROSS), F32), ((N_MEM, D_CROSS), F32), ((1, CROSS_HEAD), F32)])


def _cross_kv_bwd(kv, dkn, dv, g_ck):
    def body(i, n, kv_ref, dkn_ref, dv_ref, gk_ref, dkv_ref, dgk_ref):
        dgk = jnp.zeros((1, CROSS_HEAD), F32)
        for h in range(D_CROSS // CROSS_HEAD):
            sl = slice(CROSS_HEAD * h, CROSS_HEAD * (h + 1))
            dk, dg = _rms_bwd(kv_ref[:, sl], gk_ref[...], dkn_ref[:, sl])
            dkv_ref[:, sl] = dk.astype(BF)
            dgk = dgk + dg
        dkv_ref[:, D_CROSS:] = dv_ref[...].astype(BF)
        dgk_ref[...] = dgk

    return _rowwise(body, name="cross_kv_bwd", nrows=N_MEM, tile=N_MEM,
                    row_ins=[(kv, 2 * D_CROSS, 0), (dkn, D_CROSS, 0), (dv, D_CROSS, 0)], full_ins=[g_ck],
                    row_outs=[(2 * D_CROSS, BF)], acc_outs=[((1, CROSS_HEAD), F32)])


def _attn_merge_bwd(dmix, attn, g_attn):
    def body(i, n, dn_ref, a_ref, g_ref, da_ref, dl_ref, dg_ref):
        attn_v = a_ref[...]
        da, dg = _rms_bwd(attn_v, g_ref[...], dn_ref[...])
        da_ref[...] = da.astype(BF)
        dl_ref[...] = _seg_sum(da * attn_v, HEAD)
        _accum(dg_ref, dg, i)

    return _rowwise(body, name="attn_merge_bwd", nrows=S, tile=ROW_TILE,
                    row_ins=[(dmix, D_ATTN, 0), (attn, D_ATTN, 0)], full_ins=[g_attn],
                    row_outs=[(D_ATTN, BF), (D_ATTN, F32)], acc_outs=[((1, D_ATTN), F32)])


def _attn_bwd(q, k, v, do, lse, delta, nb, name):
    def kern(q_ref, k_ref, v_ref, do_ref, l_ref, d_ref, dq_ref, dk_ref, dv_ref):
        b = pl.program_id(1)

        @pl.when(b == 0)
        def _():
            dk_ref[...] = jnp.zeros_like(dk_ref)
            dv_ref[...] = jnp.zeros_like(dv_ref)

        q, do = q_ref[...], do_ref[...]
        half0 = lax.broadcasted_iota(jnp.int32, (BLK, LANES), 1) < HEAD
        cur = pl.ds(pl.multiple_of(b * BLK, BLK), BLK)
        prev = pl.ds(pl.multiple_of(jnp.maximum(b - 1, 0) * BLK, BLK), BLK)
        vc = v_ref[cur, :]
        dq_h = []
        dk_c = jnp.zeros((BLK, LANES), F32)
        dv_c = jnp.zeros((BLK, LANES), F32)
        dk_p = jnp.zeros((BLK, LANES), F32)
        dv_p = jnp.zeros((BLK, LANES), F32)
        for h in range(2):
            hm = half0 if h == 0 else jnp.logical_not(half0)
            lse_h = l_ref[:, HEAD * h:HEAD * h + 1]
            del_h = d_ref[:, HEAD * h:HEAD * h + 1]
            qm, kc, s_c, kp, s_p = _band_scores(q, k_ref, b, nb, hm)
            dom = jnp.where(hm, do.astype(F32), 0.0).astype(BF)
            p_c = jnp.exp(s_c - lse_h)
            ds_c = (p_c * (_dot(dom, vc, NT) - del_h)).astype(BF)
            dq = _dot(ds_c, kc, NN)
            dk_c = dk_c + _dot(ds_c, qm, TN)
            dv_c = dv_c + _dot(p_c.astype(BF), dom, TN)
            if nb > 1:
                vp = v_ref[prev, :]
                p_p = jnp.exp(s_p - lse_h)
                ds_p = (p_p * (_dot(dom, vp, NT) - del_h)).astype(BF)
                dq = dq + _dot(ds_p, kp, NN)
                dk_p = dk_p + _dot(ds_p, qm, TN)
                dv_p = dv_p + _dot(p_p.astype(BF), dom, TN)
            dq_h.append(dq)
        dq_ref[...] = jnp.where(half0, dq_h[0], dq_h[1])
        dk_ref[cur, :] += dk_c
        dv_ref[cur, :] += dv_c
        if nb > 1:
            dk_ref[prev, :] += dk_p
            dv_ref[prev, :] += dv_p

    tile = pl.BlockSpec((BLK, LANES), lambda p, b: (b, p))
    seq = pl.BlockSpec((S, LANES), lambda p, b: (0, p))
    return pl.pallas_call(
        kern, name=name, grid=(D_ATTN // LANES, NB), in_specs=[tile, seq, seq, tile, tile, tile],
        out_specs=[tile, seq, seq], out_shape=[jax.ShapeDtypeStruct((S, D_ATTN), F32)] * 3,
        compiler_params=_params("parallel", "arbitrary"))(q, k, v, do, lse, delta)


def _qk_bwd(dqs, dks, dvs, proj, pos_col, gq128, gk128, inv_tab):
    scale = HEAD ** -0.5

    def body(i, n, *refs):
        dq_refs, dk_refs, dv_refs = refs[0:3], refs[3:6], refs[6:9]
        q_ref, k_ref, p_ref, gq_ref, gk_ref, it_ref = refs[9:15]
        dqo_ref, dko_ref, dvo_ref, dgq_ref, dgk_ref = refs[15:20]
        t = q_ref.shape[0]
        c, sg, lo = _rope_tables(p_ref[...], it_ref[...], t)
        dgq = jnp.zeros((1, LANES), F32)
        dgk = jnp.zeros((1, LANES), F32)
        for j in range(D_ATTN // LANES):
            sl = slice(LANES * j, LANES * (j + 1))
            dqr = _rope(dq_refs[0][:, sl] + dq_refs[1][:, sl] + dq_refs[2][:, sl], c, -sg, lo) * scale
            dkr = _rope(dk_refs[0][:, sl] + dk_refs[1][:, sl] + dk_refs[2][:, sl], c, -sg, lo)
            dq, gq = _seg_rms_bwd(q_ref[:, sl], gq_ref[...], dqr, HEAD)
            dk, gk = _seg_rms_bwd(k_ref[:, sl], gk_ref[...], dkr, HEAD)
            dqo_ref[:, sl] = dq.astype(BF)
            dko_ref[:, sl] = dk.astype(BF)
            dgq, dgk = dgq + gq, dgk + gk
        dvo_ref[...] = (dv_refs[0][...] + dv_refs[1][...] + dv_refs[2][...]).astype(BF)
        _accum(dgq_ref, _fold_heads(dgq), i)
        _accum(dgk_ref, _fold_heads(dgk), i)

    return _rowwise(body, name="qk_bwd", nrows=S, tile=ROW_TILE,
                    row_ins=[(a, D_ATTN, 0) for a in (*dqs, *dks, *dvs)]
                    + [(proj, D_ATTN, 0), (proj, D_ATTN, 1), (pos_col, 1, 0)],
                    full_ins=[gq128, gk128, inv_tab], row_outs=[(D_ATTN, BF)] * 3,
                    acc_outs=[((1, LANES), F32)] * 2)


def _ssd_post_bwd(y_ssd, xbc, proj, dmix, dskip_b, g_ssm):
    def body(i, n, y_ref, xs_ref, z_ref, do_ref, d_ref, g_ref, dy_ref, dz_ref, dxs_ref, dd_ref, dg_ref):
        z, xs = z_ref[...], xs_ref[...]
        sg = _sigmoid(z)
        gate = z * sg
        y = y_ref[...] + d_ref[...] * xs
        dy2, dg = _seg_rms_bwd(y * gate, g_ref[...], do_ref[...], 256)
        dy = dy2 * gate
        dy_ref[...] = dy.astype(BF)
        dz_ref[...] = (dy2 * y * (sg * (1.0 + z * (1.0 - sg)))).astype(BF)
        dxs_ref[...] = dy * d_ref[...]
        _accum(dd_ref, jnp.sum(dy * xs, axis=0, keepdims=True), i)
        _accum(dg_ref, dg, i)

    return _rowwise(body, name="ssd_post_bwd", nrows=S, tile=ROW_TILE,
                    row_ins=[(y_ssd, D_SSM, 0), (xbc, D_SSM, 0), (proj, D_SSM, 3), (dmix, D_SSM, 1)],
                    full_ins=[dskip_b, g_ssm], row_outs=[(D_SSM, BF), (D_SSM, BF), (D_SSM, F32)],
                    acc_outs=[((1, D_SSM), F32), ((1, D_SSM), F32)])


def _ssd_bwd(xbc, dy, a_b, at_r, dt_r):
    def kern(c_ref, b_ref, x_ref, dy_ref, ab_ref, at_ref, dt_ref,
             dc_ref, daq_ref, dx_ref, db_ref, dak_ref, ddt_ref):
        i = pl.program_id(1)

        @pl.when(i == 0)
        def _():
            dx_ref[...] = jnp.zeros_like(dx_ref)
            db_ref[...] = jnp.zeros_like(db_ref)
            dak_ref[...] = jnp.zeros_like(dak_ref)
            ddt_ref[...] = jnp.zeros_like(ddt_ref)

        ci = c_ref[...].astype(BF)
        dyv = dy_ref[...]
        half0 = lax.broadcasted_iota(jnp.int32, (BLK, LANES), 1) < HEAD
        hms = (half0, jnp.logical_not(half0))
        dym = [jnp.where(hm, dyv.astype(F32), 0.0).astype(BF) for hm in hms]
        a_cols = (ab_ref[:, 0:1], ab_ref[:, HEAD:HEAD + 1])
        rowg = i * BLK + lax.broadcasted_iota(jnp.int32, (BLK, BLK), 0)

        def step(j, carry):
            dc_acc, r0, r1 = carry
            rows = [r0, r1]
            off = pl.multiple_of(j * BLK, BLK)
            bj = b_ref[pl.ds(off, BLK), :].astype(BF)
            xj = x_ref[pl.ds(off, BLK), :]
            cb = _dot(ci, bj, NT)
            causal = (j * BLK + lax.broadcasted_iota(jnp.int32, (BLK, BLK), 1)) <= rowg
            at, dtj = at_ref[j], dt_ref[j]
            dcb = jnp.zeros((BLK, BLK), F32)
            dxj = jnp.zeros((BLK, LANES), F32)
            for h in range(2):
                lm = _ssd_decay(a_cols[h], at[h:h + 1, :], causal)
                dth = dtj[h:h + 1, :]
                dw = _dot(dym[h], jnp.where(hms[h], xj, 0.0).astype(BF), NT)
                g = dw * cb * lm
                w = cb * lm * dth
                dxj = dxj + _dot(w.astype(BF), dym[h], TN)
                gcol = jnp.sum(g, axis=0, keepdims=True)
                ddt_ref[j, h:h + 1, :] += gcol
                dak_ref[j, h:h + 1, :] += gcol * dth
                rows[h] = rows[h] + jnp.sum(g * dth, axis=1, keepdims=True)
                dcb = dcb + dw * lm * dth
            dcb = dcb.astype(BF)
            dx_ref[pl.ds(off, BLK), :] += dxj
            db_ref[pl.ds(off, BLK), :] += _dot(dcb, ci, TN)
            return dc_acc + _dot(dcb, bj, NN), rows[0], rows[1]

        zcol = jnp.zeros((BLK, 1), F32)
        dc, r0, r1 = lax.fori_loop(0, i + 1, step, (jnp.zeros((BLK, LANES), F32), zcol, zcol))
        dc_ref[...] = dc
        daq_ref[...] = jnp.where(half0, r0, r1)

    npair = D_SSM // LANES
    rowvec = pl.BlockSpec((None, NB, 2, BLK), lambda p, i: (p, 0, 0, 0))
    tile = pl.BlockSpec((BLK, LANES), lambda p, i: (i, p))
    seq = pl.BlockSpec((S, LANES), lambda p, i: (0, p))
    return pl.pallas_call(
        kern, name="ssd_bwd", grid=(npair, NB),
        in_specs=[pl.BlockSpec((BLK, LANES), lambda p, i: (i, 12 + p // 2)),
                  pl.BlockSpec((S, LANES), lambda p, i: (0, 8 + p // 2)), seq, tile, tile, rowvec, rowvec],
        out_specs=[tile, tile, seq, seq, rowvec, rowvec],
        out_shape=[jax.ShapeDtypeStruct((S, D_SSM), F32)] * 4
        + [jax.ShapeDtypeStruct((npair, NB, 2, BLK), F32)] * 2,
        compiler_params=_params("parallel", "arbitrary"))(xbc, xbc, xbc, dy, a_b, at_r, dt_r)


def _ssd_prep_bwd(daq, dak, ddt_direct, dt, proj, dt_bias128, a_log128):
    def body(i, n, daq_ref, dak_ref, dd_ref, dt_ref, raw_ref, b_ref, al_ref, draw_ref, dal_ref, db_ref, carry):
        @pl.when(i == 0)
        def _():
            carry[...] = jnp.zeros_like(carry)

        a = -jnp.exp(al_ref[...])
        tri = (lax.broadcasted_iota(jnp.int32, (BLK, BLK), 0) <= lax.broadcasted_iota(jnp.int32, (BLK, BLK), 1))
        rev = _dot3(tri.astype(BF), daq_ref[...] - dak_ref[...]) + carry[0:1, :]
        carry[...] = jnp.broadcast_to(rev[0:1, :], carry.shape)
        dtv = dt_ref[...]
        draw = (dd_ref[...] + a * rev) * _sigmoid(raw_ref[...] + b_ref[...])
        draw_ref[...] = draw.astype(BF)
        _accum(dal_ref, jnp.sum(dtv * rev, axis=0, keepdims=True) * a, i)
        _accum(db_ref, jnp.sum(draw, axis=0, keepdims=True), i)

    return _rowwise(body, name="ssd_prep_bwd", nrows=S, tile=BLK, reverse=True,
                    row_ins=[(daq, LANES, 0), (dak, LANES, 0), (ddt_direct, LANES, 0), (dt, LANES, 0),
                             (proj, LANES, (D_INP - LANES) // LANES)],
                    full_ins=[dt_bias128, a_log128], row_outs=[(LANES, BF)],
                    acc_outs=[((1, LANES), F32), ((1, LANES), F32)], scratch=[pltpu.VMEM((8, LANES), F32)])


def _conv_bwd(proj, conv_w, conv_b, d1, d2, map1, map2, col0, ntiles, name):
    base = (4 * D_ATTN + col0) // LANES

    def kern(x_ref, w_ref, b_ref, d1_ref, d2_ref, dx_ref, dw_ref, db_ref):
        x = x_ref[...]
        c, shifted = _conv_taps(x, w_ref)
        c = c + b_ref[...]
        sg = _sigmoid(c)
        dc = (d1_ref[...] + d2_ref[...]) * (sg * (1.0 + c * (1.0 - sg)))
        rows = lax.broadcasted_iota(jnp.int32, x.shape, 0)
        dx = jnp.zeros_like(x)
        for w in range(CONV_W):
            k = CONV_W - 1 - w
            up = dc if k == 0 else jnp.where(rows < S - k, pltpu.roll(dc, S - k, 0), 0.0)
            dx = dx + up * w_ref[w:w + 1, :]
            dw_ref[w:w + 1, :] = jnp.sum(dc * shifted[w], axis=0, keepdims=True)
        dx_ref[...] = dx.astype(BF)
        db_ref[...] = jnp.sum(dc, axis=0, keepdims=True)

    c0 = col0 // LANES
    return pl.pallas_call(
        kern, name=name, grid=(ntiles,),
        in_specs=[pl.BlockSpec((S, LANES), lambda c: (0, base + c)),
                  pl.BlockSpec((CONV_W, LANES), lambda c: (0, c0 + c)),
                  pl.BlockSpec((1, LANES), lambda c: (0, c0 + c)),
                  pl.BlockSpec((S, LANES), lambda c: (0, map1(c))),
                  pl.BlockSpec((S, LANES), lambda c: (0, map2(c)))],
        out_specs=[pl.BlockSpec((S, LANES), lambda c: (0, c)), pl.BlockSpec((CONV_W, LANES), lambda c: (0, c)),
                   pl.BlockSpec((1, LANES), lambda c: (0, c))],
        out_shape=[jax.ShapeDtypeStruct((S, ntiles * LANES), BF), jax.ShapeDtypeStruct((CONV_W, ntiles * LANES), F32),
                   jax.ShapeDtypeStruct((1, ntiles * LANES), F32)],
        compiler_params=_params("parallel"))(proj, conv_w, conv_b, d1, d2)


def _adamw(w, m, v, parts, name):
    r, c = w.shape
    n_parts = parts.shape[0]
    tile = r if r <= 512 else (128 if c > 1024 else 256)
    assert r % tile == 0
    c1 = 1.0 - ADAM_B1 ** ADAM_STEP
    c2 = 1.0 - ADAM_B2 ** ADAM_STEP

    def kern(w_ref, m_ref, v_ref, p_ref, g_ref, d_ref, mo_ref, vo_ref):
        g = p_ref[0].astype(F32)
        for k in range(1, n_parts):
            g = g + p_ref[k].astype(F32)
        mn = ADAM_B1 * m_ref[...] + (1.0 - ADAM_B1) * g
        vn = ADAM_B2 * v_ref[...] + (1.0 - ADAM_B2) * (g * g)
        g_ref[...] = g
        mo_ref[...] = mn
        vo_ref[...] = vn
        d_ref[...] = -ADAM_LR * ((mn / c1) / (jnp.sqrt(vn / c2) + ADAM_EPS) + ADAM_WD * w_ref[...])

    blk = pl.BlockSpec((tile, c), lambda i: (i, 0))
    return pl.pallas_call(
        kern, name=name, grid=(r // tile,),
        in_specs=[blk, blk, blk, pl.BlockSpec((n_parts, tile, c), lambda i: (0, i, 0))],
        out_specs=[blk] * 4, out_shape=[jax.ShapeDtypeStruct((r, c), F32)] * 4,
        compiler_params=_params("parallel"))(w, m, v, parts)


MESH = pl.DeviceIdType.MESH
ANY = pl.BlockSpec(memory_space=pl.ANY)


def _put_own(gathered, shard):
    me = 4 * lax.axis_index("x") + 2 * lax.axis_index("y") + lax.axis_index("c")
    return lax.dynamic_update_slice(gathered, shard[None], (me,) + (0,) * shard.ndim)


def _all_gather(arrs, name):
    na = len(arrs)

    def kern(*refs):
        ins, outs = refs[:na], refs[na:2 * na]
        send_sems, recv_sems = refs[2 * na:]
        x, y, c = lax.axis_index("x"), lax.axis_index("y"), lax.axis_index("c")
        me, sibling = (x, y, c), (x, y, 1 - c)
        chips = [(1 - x, y), (x, 1 - y), (1 - x, 1 - y)]

        def copy(a, k, block, to, src=None):
            px, py, pc = block
            dst = outs[a].at[4 * px + 2 * py + pc]
            return pltpu.make_async_remote_copy(
                src_ref=dst if src is None else src, dst_ref=dst, send_sem=send_sems.at[a, k],
                recv_sem=recv_sems.at[a, k], device_id=to, device_id_type=MESH)

        first = []
        for a in range(na):
            first.append(copy(a, 0, me, sibling, src=ins[a]))
            first += [copy(a, 1 + j, me, (*chip, c), src=ins[a]) for j, chip in enumerate(chips)]
        for cp in first:
            cp.start()
        passed = []
        for a in range(na):
            for j, chip in enumerate(chips):
                copy(a, 1 + j, (*chip, c), me).wait_recv()
                fwd = copy(a, 4 + j, (*chip, c), sibling)
                fwd.start()
                passed.append(fwd)
        for a in range(na):
            copy(a, 0, sibling, me).wait_recv()
            for j, chip in enumerate(chips):
                copy(a, 4 + j, (*chip, 1 - c), me).wait_recv()
        for cp in first + passed:
            cp.wait_send()

    outs = pl.pallas_call(
        kern, name=name, in_specs=[ANY] * na, out_specs=[ANY] * na,
        out_shape=[jax.ShapeDtypeStruct((N_DEV,) + a.shape, a.dtype) for a in arrs],
        scratch_shapes=[pltpu.SemaphoreType.DMA((na, 7)), pltpu.SemaphoreType.DMA((na, 7))])(*arrs)
    return [_put_own(o, a) for o, a in zip(outs, arrs)]


HBM = pl.BlockSpec(memory_space=pltpu.HBM)
SEM = pl.BlockSpec(memory_space=pltpu.SEMAPHORE)
EFFECT = pltpu.SideEffectType.DATAFLOW_SIDE_EFFECTING
N_CHIP = 4


def _in_hbm(a):
    return pltpu.with_memory_space_constraint(a, pltpu.HBM)


def _chip_copies(kind, src_refs, land_refs, send_sems, recv_sems):
    x, y, c = lax.axis_index("x"), lax.axis_index("y"), lax.axis_index("c")
    cps = []
    for a in range(len(src_refs)):
        for j, (px, py) in enumerate([(1 - x, y), (x, 1 - y), (1 - x, 1 - y)]):
            if kind == "gather":
                src, dst = src_refs[a], land_refs[a].at[4 * x + 2 * y + c]
            else:
                src, dst = src_refs[a].at[2 * px + py], land_refs[a].at[2 * x + y]
            cps.append(pltpu.make_async_remote_copy(
                src_ref=src, dst_ref=dst, send_sem=send_sems.at[3 * a + j], recv_sem=recv_sems.at[3 * a + j],
                device_id=(px, py, c), device_id_type=MESH))
    return cps


def _exchange_start(kind, srcs, lands, after, name):
    na = len(srcs)
    n_after = 0 if after is None else 1

    def kern(*refs):
        src_refs, land_refs = refs[:na], refs[na:2 * na]
        send_sems, recv_sems = refs[2 * na + n_after], refs[2 * na + n_after + 1]
        token = refs[-1]
        for cp in _chip_copies(kind, src_refs, land_refs, send_sems, recv_sems):
            cp.start()
        token[...] = jnp.zeros_like(token)

    bufs = list(srcs) + list(lands)
    res = pl.pallas_call(
        kern, name=name,
        out_shape=(pltpu.SemaphoreType.DMA((3 * na,)), pltpu.SemaphoreType.DMA((3 * na,)))
        + tuple(pltpu.HBM(b.shape, b.dtype) for b in bufs) + (jax.ShapeDtypeStruct((8, LANES), F32),),
        in_specs=[HBM] * (2 * na) + [ANY] * n_after,
        out_specs=(SEM, SEM) + (HBM,) * (2 * na) + (pl.BlockSpec(memory_space=pltpu.VMEM),),
        input_output_aliases={i: 2 + i for i in range(2 * na)},
        compiler_params=pltpu.CompilerParams(has_side_effects=EFFECT),
    )(*[_in_hbm(b) for b in bufs], *([] if after is None else [after]))
    return (res[0], res[1]), list(res[2:2 + na]), list(res[2 + na:2 + 2 * na]), res[-1]


def _exchange_wait(kind, sems, srcs, lands, after, name):
    na = len(srcs)

    def kern(*refs):
        src_refs, land_refs = refs[:na], refs[na:2 * na]
        send_sems, recv_sems = refs[2 * na], refs[2 * na + 1]
        for cp in _chip_copies(kind, src_refs, land_refs, send_sems, recv_sems):
            cp.wait_send()
            cp.wait_recv()

    bufs = list(srcs) + list(lands)
    res = pl.pallas_call(
        kern, name=name, out_shape=tuple(pltpu.HBM(b.shape, b.dtype) for b in bufs),
        in_specs=[HBM] * (2 * na) + [SEM, SEM, ANY], out_specs=(HBM,) * (2 * na),
        input_output_aliases={i: i for i in range(2 * na)},
        compiler_params=pltpu.CompilerParams(has_side_effects=EFFECT),
    )(*bufs, sems[0], sems[1], after)
    return list(res[:na]), list(res[na:])


def _gather_finish(shards, lands, name):
    na = len(shards)

    def kern(*refs):
        ins, outs = refs[:na], refs[2 * na:3 * na]
        send_sems, recv_sems = refs[3 * na:]
        x, y, c = lax.axis_index("x"), lax.axis_index("y"), lax.axis_index("c")
        chips = [(1 - x, y), (x, 1 - y), (1 - x, 1 - y)]

        def copy(a, k, slot, pc, src=None):
            dst = outs[a].at[slot + pc]
            return pltpu.make_async_remote_copy(
                src_ref=dst if src is None else src, dst_ref=dst, send_sem=send_sems.at[a, k],
                recv_sem=recv_sems.at[a, k], device_id=(x, y, 1 - c), device_id_type=MESH)

        sends = []
        for a in range(na):
            sends.append(copy(a, 0, 4 * x + 2 * y, c, src=ins[a]))
            sends += [copy(a, 1 + j, 4 * px + 2 * py, c) for j, (px, py) in enumerate(chips)]
        for cp in sends:
            cp.start()
        for a in range(na):
            copy(a, 0, 4 * x + 2 * y, 1 - c).wait_recv()
            for j, (px, py) in enumerate(chips):
                copy(a, 1 + j, 4 * px + 2 * py, 1 - c).wait_recv()
        for cp in sends:
            cp.wait_send()

    return pl.pallas_call(
        kern, name=name, in_specs=[ANY] * (2 * na), out_specs=[ANY] * na,
        out_shape=[jax.ShapeDtypeStruct(l.shape, l.dtype) for l in lands],
        input_output_aliases={na + a: a for a in range(na)},
        scratch_shapes=[pltpu.SemaphoreType.DMA((na, 4)), pltpu.SemaphoreType.DMA((na, 4))])(*shards, *lands)


def _pair_exchange(parts, name):
    na = len(parts)

    def kern(*refs):
        ins, got = refs[:na], refs[na:2 * na]
        send_sems, recv_sems = refs[2 * na:]
        x, y, c = lax.axis_index("x"), lax.axis_index("y"), lax.axis_index("c")
        sends = []
        for a in range(na):
            for q in range(N_CHIP):
                sends.append(pltpu.make_async_remote_copy(
                    src_ref=ins[a].at[2 * q + 1 - c], dst_ref=got[a].at[q], send_sem=send_sems.at[a, q],
                    recv_sem=recv_sems.at[a, q], device_id=(x, y, 1 - c), device_id_type=MESH))
        for cp in sends:
            cp.start()
        for cp in sends:
            cp.wait()

    return pl.pallas_call(
        kern, name=name, in_specs=[ANY] * na, out_specs=[ANY] * na,
        out_shape=[jax.ShapeDtypeStruct((N_CHIP,) + p.shape[1:], p.dtype) for p in parts],
        scratch_shapes=[pltpu.SemaphoreType.DMA((na, N_CHIP)), pltpu.SemaphoreType.DMA((na, N_CHIP))])(*parts)


def _pair_sum(parts, got, name):
    _, r, cdim = parts.shape
    tile = min(r, ROW_TILE)
    core = lax.axis_index("c").astype(jnp.int32).reshape(1)

    def kern(c_ref, a_ref, b_ref, q_ref, l_ref):
        s = (a_ref[...].astype(F32) + b_ref[...].astype(F32)).astype(BF)
        q_ref[...] = s
        l_ref[...] = s

    blk = pl.BlockSpec((None, tile, cdim), lambda q, i, c_ref: (q, i, 0))
    out = jax.ShapeDtypeStruct((N_CHIP, r, cdim), BF)
    return pl.pallas_call(
        kern, name=name, out_shape=[out, out],
        grid_spec=pltpu.PrefetchScalarGridSpec(
            num_scalar_prefetch=1, grid=(N_CHIP, r // tile),
            in_specs=[pl.BlockSpec((None, tile, cdim), lambda q, i, c_ref: (2 * q + c_ref[0], i, 0)), blk],
            out_specs=[blk, blk]),
        compiler_params=_params("parallel", "parallel"))(core, parts, got)


def _perm(t, d):
    return t if d == 1 else t.reshape(S // d, d, t.shape[-1]).transpose(1, 0, 2).reshape(S, t.shape[-1])


def _unperm(t, d):
    return t if d == 1 else t.reshape(d, S // d, t.shape[-1]).transpose(1, 0, 2).reshape(S, t.shape[-1])


def _pad_lanes(v, width=LANES):
    return jnp.pad(v, ((0, 0), (0, width - v.shape[1])))


def _row_layout(t16):
    return t16.T.reshape(N_HEADS // 2, 2, NB, BLK).transpose(0, 2, 1, 3)


def _row_layout_inv(r):
    return r.transpose(0, 2, 1, 3).reshape(N_HEADS, S).T


SMALL = (("g_mix", D), ("g_q", HEAD), ("g_k", HEAD), ("g_attn_out", D_ATTN), ("conv_b", D_CONV),
         ("dt_bias", 16), ("a_log", 16), ("d_skip", 16), ("g_ssm_out", D_SSM), ("g_cross", D),
         ("g_mem", D), ("g_cq", CROSS_HEAD), ("g_ck", CROSS_HEAD), ("g_mlp", D))
SMALL_ROWS = -(-sum(-(-w // LANES) for _, w in SMALL) // 8) * 8


def _pack_small(vals):
    rows = [_pad_lanes(vals[n], -(-w // LANES) * LANES).reshape(-1, LANES) for n, w in SMALL]
    packed = jnp.concatenate(rows, axis=0)
    return jnp.pad(packed, ((0, SMALL_ROWS - packed.shape[0]), (0, 0)))


def _unpack_small(packed):
    out, r = {}, 0
    for n, w in SMALL:
        nr = -(-w // LANES)
        out[n] = packed[r:r + nr].reshape(1, nr * LANES)[:, :w]
        r += nr
    return out


BIG = ("w_in", "w_out", "w_cq", "w_ckv", "w_co", "w_up", "w_down")
WEIGHTS = ("g_mix", "w_in", "g_q", "g_k", "g_attn_out", "conv_w", "conv_b", "dt_bias", "a_log", "d_skip",
           "g_ssm_out", "w_out", "g_cross", "g_mem", "w_cq", "w_ckv", "g_cq", "g_ck", "w_co", "g_mlp", "w_up", "w_down")


REST = ("w_out", "w_cq", "w_ckv", "w_co", "w_up", "w_down")


class _Overlap:
    def __init__(self, shards, after):
        lands = [_put_own(lax.empty((N_DEV,) + s.shape, s.dtype), s) for s in shards]
        self.gather = _exchange_start("gather", shards, lands, after, "ag_rest_start")
        self.token = self.gather[3]
        self.groups = []

    def rest(self, after):
        sems, srcs, lands, _ = self.gather
        srcs, lands = _exchange_wait("gather", sems, srcs, lands, after, "ag_rest_wait")
        wf = dict(zip(REST, _gather_finish(srcs, lands, "ag_rest_finish")))
        for n in ("w_out", "w_cq", "w_ckv", "w_down"):
            wf[n] = wf[n].reshape(-1, wf[n].shape[-1])
        return wf

    def emit(self, grads):
        names, tag = list(grads), "_%d" % len(self.groups)
        got = _pair_exchange([grads[n] for n in names], "rs_pair" + tag)
        sums = [_pair_sum(grads[n], b, "rs_sum_" + n) for n, b in zip(names, got)]
        sems, srcs, lands, token = _exchange_start("scatter", [q for q, _ in sums], [l for _, l in sums], None,
                                                   "rs_chip_start" + tag)
        self.groups.append((names, sems, srcs, lands))
        return token

    def finish(self, gi, after):
        names, sems, srcs, lands = self.groups[gi]
        _, lands = _exchange_wait("scatter", sems, srcs, lands, after, "rs_chip_wait_%d" % gi)
        return dict(zip(names, lands))


def _tie(v, token):
    return v if token is None else v + token[0:1, 0:1]


def _local_step(x, mem, pos_col, target, p, wf, hooks=None):
    p = dict(p)
    inv = np.zeros((1, LANES), np.float32)
    freq = (ROPE_THETA ** (-2.0 * np.arange(ROT // 2, dtype=np.float32) / ROT)).astype(np.float32)
    for l in range(LANES):
        if l % HEAD < ROT:
            inv[0, l] = freq[(l % HEAD) % (ROT // 2)]
    inv_tab = jnp.asarray(inv)
    gq128, gk128 = jnp.tile(p["g_q"], (1, 2)), jnp.tile(p["g_k"], (1, 2))
    dtb128, alog128 = _pad_lanes(p["dt_bias"]), _pad_lanes(p["a_log"])
    dskip_b = jnp.repeat(p["d_skip"], HEAD, axis=1)
    conv_w, conv_b = wf["conv_w"], p["conv_b"]

    h = _rms_fwd(x, p["g_mix"], "rms_mix")
    proj = _matmul(h, wf["w_in"], mode="nn", name="mm_in", tm=1024, tn=896, tk=512)
    qr, kr, vb = _qk_prep(proj, pos_col, gq128, gk128, inv_tab)
    pats = ((1, NB), (4, NB // 4), (16, 1))
    qs = [_perm(qr, d) for d, _ in pats]
    ks = [_perm(kr, d) for d, _ in pats]
    vs = [_perm(vb, d) for d, _ in pats]
    os_, ls_ = [], []
    for (d, nb), qd, kd, vd in zip(pats, qs, ks, vs):
        o, l = _attn_fwd(qd, kd, vd, nb, "attn_fwd_d%d" % d)
        os_.append(_unperm(o, d))
        ls_.append(_unperm(l, d))
    attn, lse, attn_n = _attn_merge(os_, ls_, p["g_attn_out"])

    xbc = _conv_fwd(proj, conv_w, conv_b)
    dt, a_cs = _ssd_prep(proj, dtb128, alog128)
    a_b = jnp.repeat(a_cs[:, :N_HEADS], HEAD, axis=1)
    at_r, dt_r = _row_layout(a_cs[:, :N_HEADS]), _row_layout(dt[:, :N_HEADS])
    y_ssd = _ssd_fwd(xbc, a_b, at_r, dt_r)
    ssm_n = _ssd_post(y_ssd, xbc, proj, dskip_b, p["g_ssm_out"])

    if hooks is not None:
        wf = {**wf, **hooks.rest(ssm_n)}
    mix = jnp.concatenate([attn_n, ssm_n], axis=1)
    x1 = _matmul(mix, wf["w_out"], mode="nn", name="mm_out", tm=1024, tn=1024, tk=512,
                 extras=(x,), epilogue=lambda acc, r: (acc + r,))
    hc = _rms_fwd(x1, p["g_cross"], "rms_cross")
    memh = _rms_fwd(mem, p["g_mem"], "rms_mem")
    qc = _matmul(hc, wf["w_cq"], mode="nn", name="mm_cq", tm=1024, tn=512, tk=512)
    kv = _matmul(memh, wf["w_ckv"], mode="nn", name="mm_ckv", tm=N_MEM, tn=1024, tk=512)
    oc = _cross_fwd(qc, kv, p["g_cq"], p["g_ck"])
    x2 = _matmul(oc, wf["w_co"], mode="nn", name="mm_co", tm=1024, tn=256, tk=512, b_cb=256,
                 extras=(x1,), epilogue=lambda acc, r: (acc + r,))
    hm = _rms_fwd(x2, p["g_mlp"], "rms_mlp")

    def up_epi(acc):
        ru = jnp.maximum(acc, 0.0)
        return ru * ru, 2.0 * ru

    act, relu2 = _matmul(hm, wf["w_up"], mode="nn", name="mm_up", tm=1024, tn=1024, tk=512, b_cb=1024,
                         out_dtypes=(BF, BF), epilogue=up_epi)
    dy = _matmul(act, wf["w_down"], mode="nn", name="mm_down", tm=1024, tn=1024, tk=512, extras=(x2, target),
                 epilogue=lambda acc, r, t: ((acc + r - t) * (1.0 / D),))
    loss_part = _loss_sum(dy)[0, 0] * (0.5 * D)

    gb, gs = {}, {}
    du = _matmul(dy, wf["w_down"], mode="nt", name="mm_down_dx", tm=1024, tn=1024, tk=512, extras=(relu2,),
                 out_dtypes=(BF,), epilogue=lambda acc, r: (acc * r.astype(F32),))
    gb["w_down"] = _matmul(act, dy, mode="tn", name="mm_down_dw", tm=1024, tn=1024, tk=512,
                           out_dtypes=(BF,)).reshape(N_DEV, D_FF // N_DEV, D)
    gb["w_up"] = _matmul(hm, du, mode="tn", name="mm_up_dw", tm=1024, tn=1024, tk=512, out_dtypes=(BF,), out_cb=1024)
    if hooks is not None:
        p["g_mlp"] = _tie(p["g_mlp"], hooks.emit({n: gb[n] for n in ("w_down", "w_up")}))
    dhm = _matmul(du, wf["w_up"], mode="nt", name="mm_up_dx", tm=1024, tn=1024, tk=512, b_cb=1024)
    dx2, gs["g_mlp"] = _rms_bwd_call(x2, p["g_mlp"], dhm, dy, "rms_mlp_bwd")

    doc = _matmul(dx2, wf["w_co"], mode="nt", name="mm_co_dx", tm=1024, tn=512, tk=256, b_cb=256)
    gb["w_co"] = _matmul(oc, dx2, mode="tn", name="mm_co_dw", tm=512, tn=256, tk=512, out_dtypes=(BF,), out_cb=256)
    dqc, dkn, dvc, gs["g_cq"] = _cross_bwd(qc, doc, kv, p["g_cq"], p["g_ck"])
    dkv, gs["g_ck"] = _cross_kv_bwd(kv, dkn, dvc, p["g_ck"])
    gb["w_cq"] = _matmul(hc, dqc, mode="tn", name="mm_cq_dw", tm=1024, tn=512, tk=512,
                         out_dtypes=(BF,)).reshape(N_DEV, D // N_DEV, D_CROSS)
    dhc = _matmul(dqc, wf["w_cq"], mode="nt", name="mm_cq_dx", tm=1024, tn=1024, tk=512)
    gb["w_ckv"] = _matmul(memh, dkv, mode="tn", name="mm_ckv_dw", tm=1024, tn=1024, tk=N_MEM,
                          out_dtypes=(BF,)).reshape(N_DEV, D // N_DEV, 2 * D_CROSS)
    dmemh = _matmul(dkv, wf["w_ckv"], mode="nt", name="mm_ckv_dx", tm=N_MEM, tn=1024, tk=512)
    (gs["g_mem"],) = _rms_bwd_call(mem, p["g_mem"], dmemh, None, "rms_mem_bwd")
    dx1, gs["g_cross"] = _rms_bwd_call(x1, p["g_cross"], dhc, dx2, "rms_cross_bwd")

    dmix = _matmul(dx1, wf["w_out"], mode="nt", name="mm_out_dx", tm=1024, tn=1024, tk=512)
    gb["w_out"] = _matmul(mix, dx1, mode="tn", name="mm_out_dw", tm=1024, tn=1024, tk=512,
                          out_dtypes=(BF,)).reshape(N_DEV, D // N_DEV, D)
    if hooks is not None:
        p["g_attn_out"] = _tie(p["g_attn_out"], hooks.emit({n: gb[n] for n in ("w_co", "w_cq", "w_ckv", "w_out")}))

    dattn, delta, gs["g_attn_out"] = _attn_merge_bwd(dmix, attn, p["g_attn_out"])
    dqs, dks, dvs = [], [], []
    for (d, nb), qd, kd, vd in zip(pats, qs, ks, vs):
        dq, dk, dv = _attn_bwd(qd, kd, vd, _perm(dattn, d), _perm(lse, d), _perm(delta, d), nb, "attn_bwd_d%d" % d)
        dqs.append(_unperm(dq, d))
        dks.append(_unperm(dk, d))
        dvs.append(_unperm(dv, d))
    dq_pre, dk_pre, dv_pre, dgq, dgk = _qk_bwd(dqs, dks, dvs, proj, pos_col, gq128, gk128, inv_tab)
    gs["g_q"], gs["g_k"] = dgq[:, :HEAD], dgk[:, :HEAD]

    dy_ssd, dz, dxs_skip, dd_lane, gs["g_ssm_out"] = _ssd_post_bwd(y_ssd, xbc, proj, dmix, dskip_b, p["g_ssm_out"])
    gs["d_skip"] = dd_lane.reshape(N_HEADS, HEAD).sum(axis=1).reshape(1, N_HEADS)
    dc_pair, daq_b, dx_ssd, db_pair, dak_r, ddt_r = _ssd_bwd(xbc, dy_ssd, a_b, at_r, dt_r)
    daq = _pad_lanes(daq_b[:, ::HEAD])
    dak = _pad_lanes(_row_layout_inv(dak_r))
    ddt_direct = _pad_lanes(_row_layout_inv(ddt_r))
    ddt_raw, dalog, dbias = _ssd_prep_bwd(daq, dak, ddt_direct, dt, proj, dtb128, alog128)
    gs["a_log"], gs["dt_bias"] = dalog[:, :N_HEADS], dbias[:, :N_HEADS]
    same = lambda c: c
    dxbc_x, dcw_x, dcb_x = _conv_bwd(proj, conv_w, conv_b, dxs_skip, dx_ssd, same, same, 0, 8, "conv_bwd_x")
    dxbc_b, dcw_b, dcb_b = _conv_bwd(proj, conv_w, conv_b, db_pair, db_pair, lambda c: 2 * c, lambda c: 2 * c + 1,
                                     D_SSM, 4, "conv_bwd_b")
    dxbc_c, dcw_c, dcb_c = _conv_bwd(proj, conv_w, conv_b, dc_pair, dc_pair, lambda c: 2 * c, lambda c: 2 * c + 1,
                                     D_SSM + 512, 4, "conv_bwd_c")
    g_conv_w = jnp.concatenate([dcw_x, dcw_b, dcw_c], axis=1)
    gs["conv_b"] = jnp.concatenate([dcb_x, dcb_b, dcb_c], axis=1)

    dproj = jnp.concatenate([dq_pre, dk_pre, dv_pre, dz, dxbc_x, dxbc_b, dxbc_c, ddt_raw], axis=1)
    dw_in = _matmul(h, dproj, mode="tn", name="mm_in_dw", tm=1024, tn=896, tk=512, out_dtypes=(BF,))
    gb["w_in"] = dw_in[:, :D_IN].reshape(D, N_DEV, D_IN // N_DEV).transpose(1, 0, 2)
    if hooks is not None:
        p["g_mix"] = _tie(p["g_mix"], hooks.emit({"w_in": gb["w_in"]}))
    dh = _matmul(dproj, wf["w_in"], mode="nt", name="mm_in_dx", tm=1024, tn=1024, tk=896)
    grad_x, gs["g_mix"] = _rms_bwd_call(x, p["g_mix"], dh, dx1, "rms_mix_bwd")
    return loss_part, grad_x, gb, gs, g_conv_w


def kernel(x, mem, positions, g_mix, w_in, g_q, g_k, g_attn_out, conv_w, conv_b, dt_bias, a_log, d_skip, g_ssm_out, w_out, g_cross, g_mem, w_cq, w_ckv, g_cq, g_ck, w_co, g_mlp, w_up, w_down, loss_target, m_g_mix, m_w_in, m_g_q, m_g_k, m_g_attn_out, m_conv_w, m_conv_b, m_dt_bias, m_a_log, m_d_skip, m_g_ssm_out, m_w_out, m_g_cross, m_g_mem, m_w_cq, m_w_ckv, m_g_cq, m_g_ck, m_w_co, m_g_mlp, m_w_up, m_w_down, v_g_mix, v_w_in, v_g_q, v_g_k, v_g_attn_out, v_conv_w, v_conv_b, v_dt_bias, v_a_log, v_d_skip, v_g_ssm_out, v_w_out, v_g_cross, v_g_mem, v_w_cq, v_w_ckv, v_g_cq, v_g_ck, v_w_co, v_g_mlp, v_w_up, v_w_down):
    args = dict(locals())
    w = {n: args[n] for n in WEIGHTS}
    m = {n: args["m_" + n] for n in WEIGHTS}
    v = {n: args["v_" + n] for n in WEIGHTS}
    small = {n: w[n] for n, _ in SMALL}
    me = 4 * lax.axis_index("x") + 2 * lax.axis_index("y") + lax.axis_index("c")

    w_in_g, conv_w_g = _all_gather([w["w_in"][0].astype(BF), w["conv_w"][0]], "ag_first")
    wf = {"w_in": jnp.pad(w_in_g.transpose(1, 0, 2).reshape(D, D_IN), ((0, 0), (0, D_INP - D_IN))),
          "conv_w": conv_w_g.transpose(1, 0, 2).reshape(CONV_W, D_CONV)}
    overlap = _Overlap([w[n][0].astype(BF) for n in REST], w_in_g)
    p = dict(small)
    p["g_mix"] = _tie(p["g_mix"], overlap.token)

    loss_part, grad_x, _, gs, g_conv_w = _local_step(
        x[0], mem[0], positions.reshape(S, 1), loss_target[0], p, wf, overlap)
    loss = lax.psum(loss_part, ("x", "y", "c"))

    out = {}
    last = grad_x
    for gi in range(3):
        for n, parts in overlap.finish(gi, last).items():
            res_n = _adamw(w[n][0], m[n][0], v[n][0], parts, "adamw_" + n)
            out[n] = [t.reshape(w[n].shape) for t in res_n]
            last = res_n[0]

    sm_parts, cw_parts = _all_gather([_pack_small(gs), g_conv_w], "ag_small_grads")
    sm = [_unpack_small(t) for t in _adamw(_pack_small(small), _pack_small({n: m[n] for n, _ in SMALL}),
                                           _pack_small({n: v[n] for n, _ in SMALL}), sm_parts, "adamw_small")]
    for n, _ in SMALL:
        out[n] = [t[n] for t in sm]
    cols = D_CONV // N_DEV
    cw_mine = lax.dynamic_slice_in_dim(cw_parts, me * cols, cols, axis=2)
    out["conv_w"] = [t.reshape(w["conv_w"].shape) for t in
                     _adamw(w["conv_w"][0], m["conv_w"][0], v["conv_w"][0], cw_mine, "adamw_conv_w")]

    res = [loss, grad_x.reshape(x.shape)]
    for k in range(4):
        res += [out[n][k] for n in WEIGHTS]
    return tuple(res)
```

```python
import functools
import math

import numpy as np
import jax
import jax.numpy as jnp
from jax import lax
from jax.experimental import pallas as pl
from jax.experimental.pallas import tpu as pltpu

F32 = jnp.float32
BF = jnp.bfloat16

N_DEV = 8
S = 2048
D = 2048
D_ATTN = 1024
N_HEADS = 16
HEAD = 64
ROT = 16
ROPE_THETA = 500000.0
D_SSM = 1024
D_CONV = 2048
CONV_W = 4
N_MEM = 256
D_CROSS = 512
CROSS_HEAD = 128
D_FF = 8192
D_IN = 6160
D_INP = 6272
EPS = 1e-6
BLK = 128
NB = S // BLK
LANES = 128
NEG = -1e30

ADAM_LR = 0.001
ADAM_B1 = 0.9
ADAM_B2 = 0.999
ADAM_EPS = 1e-08
ADAM_WD = 0.01
ADAM_STEP = 10

VMEM_LIMIT_BYTES = 48 * 1024 * 1024
ROW_TILE = 256


def _params(*sem):
    return pltpu.CompilerParams(dimension_semantics=sem, vmem_limit_bytes=VMEM_LIMIT_BYTES)


def _matmul(a, b, *, mode, name, tm, tn, tk, out_dtypes=(F32,), b_cb=None, out_cb=None,
            extras=(), epilogue=None):
    if mode == "tn":
        kk, m = a.shape
    else:
        m, kk = a.shape
    if b_cb is None:
        br, bc = b.shape
    else:
        br, bc = b.shape[1], N_DEV * b_cb
    n = br if mode == "nt" else bc
    assert m % tm == 0 and n % tn == 0 and kk % tk == 0, (name, m, n, kk)
    nk = kk // tk
    grid = (m // tm, n // tn, nk)

    if mode == "tn":
        a_spec = pl.BlockSpec((tk, tm), lambda i, j, k: (k, i))
    else:
        a_spec = pl.BlockSpec((tm, tk), lambda i, j, k: (i, k))
    if mode == "nt":
        b_blk, b_idx = (tn, tk), (lambda i, j, k: (j, k))
    else:
        b_blk, b_idx = (tk, tn), (lambda i, j, k: (k, j))
    if b_cb is None:
        b_spec = pl.BlockSpec(b_blk, b_idx)
    else:
        tc = b_blk[1]
        assert b_cb % tc == 0
        per = b_cb // tc

        def b_idx3(i, j, k):
            r, c = b_idx(i, j, k)
            return (c // per, r, c % per)
        b_spec = pl.BlockSpec((None,) + b_blk, b_idx3)
    ex_specs = [pl.BlockSpec((tm, tn), lambda i, j, k: (i, j)) for _ in extras]
    if out_cb is None:
        out_shape = [jax.ShapeDtypeStruct((m, n), dt) for dt in out_dtypes]
        out_specs = [pl.BlockSpec((tm, tn), lambda i, j, k: (i, j)) for _ in out_dtypes]
    else:
        assert out_cb % tn == 0
        pero = out_cb // tn
        out_shape = [jax.ShapeDtypeStruct((N_DEV, m, out_cb), dt) for dt in out_dtypes]
        out_specs = [pl.BlockSpec((None, tm, tn), lambda i, j, k: (j // pero, i, j % pero)) for _ in out_dtypes]
    dn = {"nn": (((1,), (0,)), ((), ())), "nt": (((1,), (1,)), ((), ())), "tn": (((0,), (0,)), ((), ()))}[mode]
    ne, no = len(extras), len(out_dtypes)

    def kern(a_ref, b_ref, *rest):
        ex_refs, out_refs, acc_ref = rest[:ne], rest[ne:ne + no], rest[ne + no]
        k = pl.program_id(2)

        @pl.when(k == 0)
        def _():
            acc_ref[...] = jnp.zeros_like(acc_ref)

        acc_ref[...] += lax.dot_general(a_ref[...].astype(BF), b_ref[...].astype(BF), dn,
                                        preferred_element_type=F32)

        @pl.when(k == nk - 1)
        def _():
            acc = acc_ref[...]
            outs = (acc,) if epilogue is None else epilogue(acc, *[r[...] for r in ex_refs])
            for r, o in zip(out_refs, outs):
                r[...] = o.astype(r.dtype)

    res = pl.pallas_call(
        kern, name=name, grid=grid, in_specs=[a_spec, b_spec] + ex_specs, out_specs=out_specs,
        out_shape=out_shape, scratch_shapes=[pltpu.VMEM((tm, tn), F32)],
        compiler_params=_params("parallel", "parallel", "arbitrary"))(a, b, *extras)
    return res[0] if no == 1 else tuple(res)


def _rowwise(body, *, name, nrows, tile, row_ins, full_ins=(), row_outs=(), acc_outs=(), scratch=(),
             reverse=False):
    n = nrows // tile

    def ridx(i):
        return (n - 1 - i) if reverse else i

    in_specs, args = [], []
    for arr, width, cb in row_ins:
        in_specs.append(pl.BlockSpec((tile, width), lambda i, cb=cb: (ridx(i), cb)))
        args.append(arr)
    for arr in full_ins:
        in_specs.append(pl.BlockSpec(arr.shape, lambda i, nd=arr.ndim: (0,) * nd))
        args.append(arr)
    out_shape, out_specs = [], []
    for width, dt in row_outs:
        out_shape.append(jax.ShapeDtypeStruct((nrows, width), dt))
        out_specs.append(pl.BlockSpec((tile, width), lambda i: (ridx(i), 0)))
    for shp, dt in acc_outs:
        out_shape.append(jax.ShapeDtypeStruct(shp, dt))
        out_specs.append(pl.BlockSpec(shp, lambda i, nd=len(shp): (0,) * nd))

    def kern(*refs):
        body(pl.program_id(0), n, *refs)

    return pl.pallas_call(kern, name=name, grid=(n,), in_specs=in_specs, out_specs=out_specs,
                          out_shape=out_shape, scratch_shapes=list(scratch),
                          compiler_params=_params("arbitrary"))(*args)


def _accum(ref, val, i):
    @pl.when(i == 0)
    def _():
        ref[...] = val

    @pl.when(i > 0)
    def _():
        ref[...] += val


def _sigmoid(x):
    return 1.0 / (1.0 + jnp.exp(-x))


def _rms_n(x):
    r = lax.rsqrt(jnp.mean(x * x, axis=-1, keepdims=True) + EPS)
    return x * r, r


def _rms_bwd(x, g, dh):
    n, r = _rms_n(x)
    dn = dh * g
    dx = r * (dn - n * jnp.mean(dn * n, axis=-1, keepdims=True))
    return dx, jnp.sum(dh * n, axis=0, keepdims=True)


def _seg_sum(x, seg):
    t, w = x.shape
    tiles = [x[:, LANES * j:LANES * (j + 1)] for j in range(w // LANES)]
    outs = []
    if seg == 64:
        lo = lax.broadcasted_iota(jnp.int32, (t, LANES), 1) < 64
        for xt in tiles:
            s_lo = jnp.sum(jnp.where(lo, xt, 0.0), axis=-1, keepdims=True)
            s_hi = jnp.sum(jnp.where(lo, 0.0, xt), axis=-1, keepdims=True)
            outs.append(jnp.where(lo, s_lo, s_hi))
    else:
        sums = [jnp.sum(xt, axis=-1, keepdims=True) for xt in tiles]
        if seg == 256:
            sums = [sums[2 * (j // 2)] + sums[2 * (j // 2) + 1] for j in range(len(sums))]
        else:
            assert seg == 128
        outs = [jnp.broadcast_to(s, (t, LANES)) for s in sums]
    return outs[0] if len(outs) == 1 else jnp.concatenate(outs, axis=1)


def _seg_rms_n(x, seg):
    r = lax.rsqrt(_seg_sum(x * x, seg) * (1.0 / seg) + EPS)
    return x * r, r


def _seg_rms_bwd(x, g, dy, seg):
    n, r = _seg_rms_n(x, seg)
    dn = dy * g
    dx = r * (dn - n * (_seg_sum(dn * n, seg) * (1.0 / seg)))
    return dx, jnp.sum(dy * n, axis=0, keepdims=True)


def _rope_tables(pos_col, inv_tab, t):
    ang = pos_col.astype(F32) * inv_tab
    idx = lax.broadcasted_iota(jnp.int32, (t, LANES), 1) % HEAD
    cos, sin = jnp.cos(ang), jnp.sin(ang)
    c = jnp.where(idx < ROT, cos, 1.0)
    sg = jnp.where(idx < ROT // 2, -sin, jnp.where(idx < ROT, sin, 0.0))
    return c, sg, idx < ROT // 2


def _rope(x, c, sg, lo):
    partner = jnp.where(lo, pltpu.roll(x, LANES - ROT // 2, 1), pltpu.roll(x, ROT // 2, 1))
    return x * c + partner * sg


def _fold_heads(v):
    v8 = jnp.broadcast_to(v, (8, LANES))
    return (v8 + pltpu.roll(v8, HEAD, 1))[0:1]


def _dot(a, b, dn):
    return lax.dot_general(a, b, (dn, ((), ())), preferred_element_type=F32)


NN = ((1,), (0,))
NT = ((1,), (1,))
TN = ((0,), (0,))


def _dot3(l01, x):
    x1 = x.astype(BF)
    r1 = x - x1.astype(F32)
    x2 = r1.astype(BF)
    x3 = (r1 - x2.astype(F32)).astype(BF)
    return _dot(l01, x1, NN) + _dot(l01, x2, NN) + _dot(l01, x3, NN)


def _rms_fwd(x, g, name):
    rows = x.shape[0]

    def body(i, n, x_ref, g_ref, o_ref):
        nx, _ = _rms_n(x_ref[...])
        o_ref[...] = (nx * g_ref[...]).astype(BF)

    return _rowwise(body, name=name, nrows=rows, tile=min(ROW_TILE, rows), row_ins=[(x, D, 0)],
                    full_ins=[g], row_outs=[(D, BF)])[0]


def _qk_prep(proj, pos_col, gq128, gk128, inv_tab):
    scale = HEAD ** -0.5

    def body(i, n, q_ref, k_ref, v_ref, p_ref, gq_ref, gk_ref, it_ref, qo_ref, ko_ref, vo_ref):
        t = q_ref.shape[0]
        c, sg, lo = _rope_tables(p_ref[...], it_ref[...], t)
        for j in range(D_ATTN // LANES):
            sl = slice(LANES * j, LANES * (j + 1))
            qn, _ = _seg_rms_n(q_ref[:, sl], HEAD)
            kn, _ = _seg_rms_n(k_ref[:, sl], HEAD)
            qo_ref[:, sl] = (_rope(qn * gq_ref[...], c, sg, lo) * scale).astype(BF)
            ko_ref[:, sl] = _rope(kn * gk_ref[...], c, sg, lo).astype(BF)
        vo_ref[...] = v_ref[...].astype(BF)

    return _rowwise(body, name="qk_prep", nrows=S, tile=ROW_TILE,
                    row_ins=[(proj, D_ATTN, 0), (proj, D_ATTN, 1), (proj, D_ATTN, 2), (pos_col, 1, 0)],
                    full_ins=[gq128, gk128, inv_tab], row_outs=[(D_ATTN, BF)] * 3)


def _band_scores(q, k_ref, b, nb, h_mask):
    row = lax.broadcasted_iota(jnp.int32, (BLK, BLK), 0)
    col = lax.broadcasted_iota(jnp.int32, (BLK, BLK), 1)
    qm = jnp.where(h_mask, q.astype(F32), 0.0).astype(BF)
    kc = k_ref[pl.ds(pl.multiple_of(b * BLK, BLK), BLK), :]
    s_c = jnp.where(col <= row, _dot(qm, kc, NT), NEG)
    if nb == 1:
        return qm, kc, s_c, None, None
    pb = jnp.maximum(b - 1, 0)
    kp = k_ref[pl.ds(pl.multiple_of(pb * BLK, BLK), BLK), :]
    ok = jnp.logical_and(col >= row, (b % nb) != 0)
    s_p = jnp.where(ok, _dot(qm, kp, NT), NEG)
    return qm, kc, s_c, kp, s_p


def _attn_fwd(q, k, v, nb, name):
    def kern(q_ref, k_ref, v_ref, o_ref, l_ref):
        b = pl.program_id(1)
        q = q_ref[...]
        half0 = lax.broadcasted_iota(jnp.int32, (BLK, LANES), 1) < HEAD
        o_h, l_h = [], []
        for h in range(2):
            hm = half0 if h == 0 else jnp.logical_not(half0)
            _, _, s_c, _, s_p = _band_scores(q, k_ref, b, nb, hm)
            m = jnp.max(s_c, axis=-1, keepdims=True)
            if nb > 1:
                m = jnp.maximum(m, jnp.max(s_p, axis=-1, keepdims=True))
            p_c = jnp.exp(s_c - m)
            den = jnp.sum(p_c, axis=-1, keepdims=True)
            vc = v_ref[pl.ds(pl.multiple_of(b * BLK, BLK), BLK), :]
            o = _dot(p_c.astype(BF), vc, NN)
            if nb > 1:
                p_p = jnp.exp(s_p - m)
                den = den + jnp.sum(p_p, axis=-1, keepdims=True)
                pb = jnp.maximum(b - 1, 0)
                vp = v_ref[pl.ds(pl.multiple_of(pb * BLK, BLK), BLK), :]
                o = o + _dot(p_p.astype(BF), vp, NN)
            o_h.append(o / den)
            l_h.append(m + jnp.log(den))
        o_ref[...] = jnp.where(half0, o_h[0], o_h[1])
        l_ref[...] = jnp.where(half0, l_h[0], l_h[1])

    tile = pl.BlockSpec((BLK, LANES), lambda p, b: (b, p))
    seq = pl.BlockSpec((S, LANES), lambda p, b: (0, p))
    return pl.pallas_call(
        kern, name=name, grid=(D_ATTN // LANES, NB), in_specs=[tile, seq, seq], out_specs=[tile, tile],
        out_shape=[jax.ShapeDtypeStruct((S, D_ATTN), F32)] * 2,
        compiler_params=_params("parallel", "arbitrary"))(q, k, v)


def _attn_merge(os_, ls_, g_attn):
    def body(i, n, o1, o2, o3, l1, l2, l3, g_ref, a_ref, lse_ref, an_ref):
        la, lb, lc = l1[...], l2[...], l3[...]
        m = jnp.maximum(jnp.maximum(la, lb), lc)
        ea, eb, ec = jnp.exp(la - m), jnp.exp(lb - m), jnp.exp(lc - m)
        den = ea + eb + ec
        attn = (ea * o1[...] + eb * o2[...] + ec * o3[...]) / den
        a_ref[...] = attn
        lse_ref[...] = m + jnp.log(den)
        nx, _ = _rms_n(attn)
        an_ref[...] = (nx * g_ref[...]).astype(BF)

    return _rowwise(body, name="attn_merge", nrows=S, tile=ROW_TILE,
                    row_ins=[(a, D_ATTN, 0) for a in (*os_, *ls_)], full_ins=[g_attn],
                    row_outs=[(D_ATTN, F32), (D_ATTN, F32), (D_ATTN, BF)])


def _conv_taps(x, w_ref):
    rows = lax.broadcasted_iota(jnp.int32, x.shape, 0)
    shifted = []
    for w in range(CONV_W):
        k = CONV_W - 1 - w
        shifted.append(x if k == 0 else jnp.where(rows >= k, pltpu.roll(x, k, 0), 0.0))
    acc = shifted[0] * w_ref[0:1, :]
    for w in range(1, CONV_W):
        acc = acc + shifted[w] * w_ref[w:w + 1, :]
    return acc, shifted


def _conv_fwd(proj, conv_w, conv_b):
    tc = 256

    def kern(x_ref, w_ref, b_ref, o_ref):
        c, _ = _conv_taps(x_ref[...], w_ref)
        c = c + b_ref[...]
        o_ref[...] = c * _sigmoid(c)

    return pl.pallas_call(
        kern, name="conv_fwd", grid=(D_CONV // tc,),
        in_specs=[pl.BlockSpec((S, tc), lambda c: (0, 4 * D_ATTN // tc + c)),
                  pl.BlockSpec((CONV_W, tc), lambda c: (0, c)), pl.BlockSpec((1, tc), lambda c: (0, c))],
        out_specs=pl.BlockSpec((S, tc), lambda c: (0, c)),
        out_shape=jax.ShapeDtypeStruct((S, D_CONV), F32), compiler_params=_params("parallel"))(proj, conv_w, conv_b)


def _softplus(x):
    return jnp.maximum(x, 0.0) + jnp.log1p(jnp.exp(-jnp.abs(x)))


def _ssd_prep(proj, dt_bias128, a_log128):
    def body(i, n, raw_ref, b_ref, al_ref, dt_ref, a_ref, carry):
        @pl.when(i == 0)
        def _():
            carry[...] = jnp.zeros_like(carry)

        dt = _softplus(raw_ref[...] + b_ref[...])
        da = dt * (-jnp.exp(al_ref[...]))
        tri = (lax.broadcasted_iota(jnp.int32, (BLK, BLK), 0) >= lax.broadcasted_iota(jnp.int32, (BLK, BLK), 1))
        cs = _dot3(tri.astype(BF), da) + carry[0:1, :]
        dt_ref[...] = dt
        a_ref[...] = cs
        carry[...] = jnp.broadcast_to(cs[BLK - 1:BLK, :], carry.shape)

    return _rowwise(body, name="ssd_prep", nrows=S, tile=BLK, row_ins=[(proj, LANES, (D_INP - LANES) // LANES)],
                    full_ins=[dt_bias128, a_log128], row_outs=[(LANES, F32), (LANES, F32)],
                    scratch=[pltpu.VMEM((8, LANES), F32)])


def _ssd_decay(a_col, at_row, causal):
    diff = jnp.where(causal, a_col - at_row, 0.0)
    return jnp.where(causal, jnp.exp(diff), 0.0)


def _ssd_fwd(xbc, a_b, dt_b, at_r, dt_r):
    def kern(c_ref, b_ref, x_ref, ab_ref, dtb_ref, at_ref, dt_ref, y_ref, hall_ref, h_scr, aprev_scr):
        i = pl.program_id(1)

        @pl.when(i == 0)
        def _():
            h_scr[...] = jnp.zeros_like(h_scr)
            aprev_scr[...] = jnp.zeros_like(aprev_scr)

        ci, bi, x = c_ref[...].astype(BF), b_ref[...].astype(BF), x_ref[...]
        ab = ab_ref[...]
        a_end = ab[BLK - 1:BLK, :]
        alpha = jnp.exp(ab - aprev_scr[0:1, :])
        beta = jnp.exp(a_end - ab) * dtb_ref[...]
        h = h_scr[...]
        hall_ref[...] = h
        half0 = lax.broadcasted_iota(jnp.int32, (BLK, LANES), 1) < HEAD
        causal = lax.broadcasted_iota(jnp.int32, (BLK, BLK), 1) <= lax.broadcasted_iota(jnp.int32, (BLK, BLK), 0)
        cb = _dot(ci, bi, NT)
        at, dtj = at_ref[...], dt_ref[...]
        y = alpha * _dot(ci, h.astype(BF), NN)
        for hd in range(2):
            hm = half0 if hd == 0 else jnp.logical_not(half0)
            w = cb * _ssd_decay(ab[:, HEAD * hd:HEAD * hd + 1], at[hd:hd + 1, :], causal) * dtj[hd:hd + 1, :]
            y = y + _dot(w.astype(BF), jnp.where(hm, x, 0.0).astype(BF), NN)
        y_ref[...] = y
        h_scr[...] = alpha[BLK - 1:BLK, :] * h + _dot(bi, (beta * x).astype(BF), TN)
        aprev_scr[...] = jnp.broadcast_to(a_end, aprev_scr.shape)

    npair = D_SSM // LANES
    rowvec = pl.BlockSpec((None, None, 2, BLK), lambda p, i: (p, i, 0, 0))
    tile = pl.BlockSpec((BLK, LANES), lambda p, i: (i, p))
    return pl.pallas_call(
        kern, name="ssd_fwd", grid=(npair, NB),
        in_specs=[pl.BlockSpec((BLK, LANES), lambda p, i: (i, 12 + p // 2)),
                  pl.BlockSpec((BLK, LANES), lambda p, i: (i, 8 + p // 2)), tile, tile, tile, rowvec, rowvec],
        out_specs=[tile, tile], out_shape=[jax.ShapeDtypeStruct((S, D_SSM), F32)] * 2,
        scratch_shapes=[pltpu.VMEM((BLK, LANES), F32), pltpu.VMEM((8, LANES), F32)],
        compiler_params=_params("parallel", "arbitrary"))(xbc, xbc, xbc, a_b, dt_b, at_r, dt_r)


def _ssd_post(y_ssd, xbc, proj, dskip_b, g_ssm):
    def body(i, n, y_ref, xs_ref, z_ref, d_ref, g_ref, o_ref):
        z = z_ref[...]
        y2 = (y_ref[...] + d_ref[...] * xs_ref[...]) * (z * _sigmoid(z))
        nx, _ = _seg_rms_n(y2, 256)
        o_ref[...] = (nx * g_ref[...]).astype(BF)

    return _rowwise(body, name="ssd_post", nrows=S, tile=ROW_TILE,
                    row_ins=[(y_ssd, D_SSM, 0), (xbc, D_SSM, 0), (proj, D_SSM, 3)], full_ins=[dskip_b, g_ssm],
                    row_outs=[(D_SSM, BF)])[0]


def _cross_heads(q, kv_ref, gq, gk):
    out = []
    for h in range(D_CROSS // CROSS_HEAD):
        sl = slice(CROSS_HEAD * h, CROSS_HEAD * (h + 1))
        nq, rq = _rms_n(q[:, sl])
        nk, rk = _rms_n(kv_ref[:, sl])
        v = kv_ref[:, D_CROSS + CROSS_HEAD * h:D_CROSS + CROSS_HEAD * (h + 1)]
        out.append((sl, nq, rq, nk, rk, v))
    return out


def _cross_fwd(qc, kv, g_cq, g_ck):
    scale = CROSS_HEAD ** -0.5

    def body(i, n, q_ref, kv_ref, gq_ref, gk_ref, o_ref):
        for sl, nq, _, nk, _, v in _cross_heads(q_ref[...], kv_ref, gq_ref[...], gk_ref[...]):
            qn = (nq * gq_ref[...] * scale).astype(BF)
            kn = (nk * gk_ref[...]).astype(BF)
            s = _dot(qn, kn, NT)
            e = jnp.exp(s - jnp.max(s, axis=-1, keepdims=True))
            p = e / jnp.sum(e, axis=-1, keepdims=True)
            o_ref[:, sl] = _dot(p.astype(BF), v.astype(BF), NN).astype(BF)

    return _rowwise(body, name="cross_fwd", nrows=S, tile=ROW_TILE, row_ins=[(qc, D_CROSS, 0)],
                    full_ins=[kv, g_cq, g_ck], row_outs=[(D_CROSS, BF)])[0]


def _loss_sum(dy):
    def body(i, n, d_ref, o_ref):
        d = d_ref[...]
        s = jnp.sum(jnp.sum(d * d, axis=0, keepdims=True), axis=1, keepdims=True)
        _accum(o_ref, s, i)

    return _rowwise(body, name="loss_sum", nrows=S, tile=ROW_TILE, row_ins=[(dy, D, 0)],
                    acc_outs=[((1, 1), F32)])[0]


def _rms_bwd_call(x, g, dh, dres, name):
    rows = x.shape[0]
    has_res = dres is not None

    def body(i, n, *refs):
        if has_res:
            x_ref, dh_ref, dr_ref, g_ref, dx_ref, dg_ref = refs
        else:
            x_ref, dh_ref, g_ref, dg_ref = refs
        dx, dg = _rms_bwd(x_ref[...], g_ref[...], dh_ref[...])
        if has_res:
            dx_ref[...] = dx + dr_ref[...]
        _accum(dg_ref, dg, i)

    row_ins = [(x, D, 0), (dh, D, 0)] + ([(dres, D, 0)] if has_res else [])
    return _rowwise(body, name=name, nrows=rows, tile=min(ROW_TILE, rows), row_ins=row_ins, full_ins=[g],
                    row_outs=[(D, F32)] if has_res else [], acc_outs=[((1, D), F32)])


def _cross_bwd(qc, doc, kv, g_cq, g_ck):
    scale = CROSS_HEAD ** -0.5

    def body(i, n, q_ref, do_ref, kv_ref, gq_ref, gk_ref, dq_ref, dkn_ref, dv_ref, dgq_ref):
        dgq = jnp.zeros((1, CROSS_HEAD), F32)
        dkn_parts, dv_parts = [], []
        for sl, nq, rq, nk, _, v in _cross_heads(q_ref[...], kv_ref, gq_ref[...], gk_ref[...]):
            qn = (nq * gq_ref[...] * scale).astype(BF)
            kn = (nk * gk_ref[...]).astype(BF)
            s = _dot(qn, kn, NT)
            e = jnp.exp(s - jnp.max(s, axis=-1, keepdims=True))
            p = e / jnp.sum(e, axis=-1, keepdims=True)
            do = do_ref[:, sl].astype(BF)
            dv_parts.append(_dot(p.astype(BF), do, TN))
            dp = _dot(do, v.astype(BF), NT)
            ds = (p * (dp - jnp.sum(dp * p, axis=-1, keepdims=True))).astype(BF)
            dqn = _dot(ds, kn, NN)
            dkn_parts.append(_dot(ds, qn, TN))
            dn = dqn * (gq_ref[...] * scale)
            dq_ref[:, sl] = (rq * (dn - nq * jnp.mean(dn * nq, axis=-1, keepdims=True))).astype(BF)
            dgq = dgq + jnp.sum(dqn * scale * nq, axis=0, keepdims=True)
        _accum(dkn_ref, jnp.concatenate(dkn_parts, axis=1), i)
        _accum(dv_ref, jnp.concatenate(dv_parts, axis=1), i)
        _accum(dgq_ref, dgq, i)

    return _rowwise(body, name="cross_bwd", nrows=S, tile=ROW_TILE, row_ins=[(qc, D_CROSS, 0), (doc, D_CROSS, 0)],
                    full_ins=[kv, g_cq, g_ck], row_outs=[(D_CROSS, BF)],
                    acc_outs=[((N_MEM, D_CROSS), F32), ((N_MEM, D_CROSS), F32), ((1, CROSS_HEAD), F32)])


def _cross_kv_bwd(kv, dkn, dv, g_ck):
    def body(i, n, kv_ref, dkn_ref, dv_ref, gk_ref, dkv_ref, dgk_ref):
        dgk = jnp.zeros((1, CROSS_HEAD), F32)
        for h in range(D_CROSS // CROSS_HEAD):
            sl = slice(CROSS_HEAD * h, CROSS_HEAD * (h + 1))
            dk, dg = _rms_bwd(kv_ref[:, sl], gk_ref[...], dkn_ref[:, sl])
            dkv_ref[:, sl] = dk.astype(BF)
            dgk = dgk + dg
        dkv_ref[:, D_CROSS:] = dv_ref[...].astype(BF)
        dgk_ref[...] = dgk

    return _rowwise(body, name="cross_kv_bwd", nrows=N_MEM, tile=N_MEM,
                    row_ins=[(kv, 2 * D_CROSS, 0), (dkn, D_CROSS, 0), (dv, D_CROSS, 0)], full_ins=[g_ck],
                    row_outs=[(2 * D_CROSS, BF)], acc_outs=[((1, CROSS_HEAD), F32)])


def _attn_merge_bwd(dmix, attn, g_attn):
    def body(i, n, dn_ref, a_ref, g_ref, da_ref, dl_ref, dg_ref):
        attn_v = a_ref[...]
        da, dg = _rms_bwd(attn_v, g_ref[...], dn_ref[...])
        da_ref[...] = da.astype(BF)
        dl_ref[...] = _seg_sum(da * attn_v, HEAD)
        _accum(dg_ref, dg, i)

    return _rowwise(body, name="attn_merge_bwd", nrows=S, tile=ROW_TILE,
                    row_ins=[(dmix, D_ATTN, 0), (attn, D_ATTN, 0)], full_ins=[g_attn],
                    row_outs=[(D_ATTN, BF), (D_ATTN, F32)], acc_outs=[((1, D_ATTN), F32)])


def _attn_bwd(q, k, v, do, lse, delta, nb, name):
    def kern(q_ref, k_ref, v_ref, do_ref, l_ref, d_ref, dq_ref, dk_ref, dv_ref):
        b = pl.program_id(1)

        @pl.when(b == 0)
        def _():
            dk_ref[...] = jnp.zeros_like(dk_ref)
            dv_ref[...] = jnp.zeros_like(dv_ref)

        q, do = q_ref[...], do_ref[...]
        half0 = lax.broadcasted_iota(jnp.int32, (BLK, LANES), 1) < HEAD
        cur = pl.ds(pl.multiple_of(b * BLK, BLK), BLK)
        prev = pl.ds(pl.multiple_of(jnp.maximum(b - 1, 0) * BLK, BLK), BLK)
        vc = v_ref[cur, :]
        dq_h = []
        dk_c = jnp.zeros((BLK, LANES), F32)
        dv_c = jnp.zeros((BLK, LANES), F32)
        dk_p = jnp.zeros((BLK, LANES), F32)
        dv_p = jnp.zeros((BLK, LANES), F32)
        for h in range(2):
            hm = half0 if h == 0 else jnp.logical_not(half0)
            lse_h = l_ref[:, HEAD * h:HEAD * h + 1]
            del_h = d_ref[:, HEAD * h:HEAD * h + 1]
            qm, kc, s_c, kp, s_p = _band_scores(q, k_ref, b, nb, hm)
            dom = jnp.where(hm, do.astype(F32), 0.0).astype(BF)
            p_c = jnp.exp(s_c - lse_h)
            ds_c = (p_c * (_dot(dom, vc, NT) - del_h)).astype(BF)
            dq = _dot(ds_c, kc, NN)
            dk_c = dk_c + _dot(ds_c, qm, TN)
            dv_c = dv_c + _dot(p_c.astype(BF), dom, TN)
            if nb > 1:
                vp = v_ref[prev, :]
                p_p = jnp.exp(s_p - lse_h)
                ds_p = (p_p * (_dot(dom, vp, NT) - del_h)).astype(BF)
                dq = dq + _dot(ds_p, kp, NN)
                dk_p = dk_p + _dot(ds_p, qm, TN)
                dv_p = dv_p + _dot(p_p.astype(BF), dom, TN)
            dq_h.append(dq)
        dq_ref[...] = jnp.where(half0, dq_h[0], dq_h[1])
        dk_ref[cur, :] += dk_c
        dv_ref[cur, :] += dv_c
        if nb > 1:
            dk_ref[prev, :] += dk_p
            dv_ref[prev, :] += dv_p

    tile = pl.BlockSpec((BLK, LANES), lambda p, b: (b, p))
    seq = pl.BlockSpec((S, LANES), lambda p, b: (0, p))
    return pl.pallas_call(
        kern, name=name, grid=(D_ATTN // LANES, NB), in_specs=[tile, seq, seq, tile, tile, tile],
        out_specs=[tile, seq, seq], out_shape=[jax.ShapeDtypeStruct((S, D_ATTN), F32)] * 3,
        compiler_params=_params("parallel", "arbitrary"))(q, k, v, do, lse, delta)


def _qk_bwd(dqs, dks, dvs, proj, pos_col, gq128, gk128, inv_tab):
    scale = HEAD ** -0.5

    def body(i, n, *refs):
        dq_refs, dk_refs, dv_refs = refs[0:3], refs[3:6], refs[6:9]
        q_ref, k_ref, p_ref, gq_ref, gk_ref, it_ref = refs[9:15]
        dqo_ref, dko_ref, dvo_ref, dgq_ref, dgk_ref = refs[15:20]
        t = q_ref.shape[0]
        c, sg, lo = _rope_tables(p_ref[...], it_ref[...], t)
        dgq = jnp.zeros((1, LANES), F32)
        dgk = jnp.zeros((1, LANES), F32)
        for j in range(D_ATTN // LANES):
            sl = slice(LANES * j, LANES * (j + 1))
            dqr = _rope(dq_refs[0][:, sl] + dq_refs[1][:, sl] + dq_refs[2][:, sl], c, -sg, lo) * scale
            dkr = _rope(dk_refs[0][:, sl] + dk_refs[1][:, sl] + dk_refs[2][:, sl], c, -sg, lo)
            dq, gq = _seg_rms_bwd(q_ref[:, sl], gq_ref[...], dqr, HEAD)
            dk, gk = _seg_rms_bwd(k_ref[:, sl], gk_ref[...], dkr, HEAD)
            dqo_ref[:, sl] = dq.astype(BF)
            dko_ref[:, sl] = dk.astype(BF)
            dgq, dgk = dgq + gq, dgk + gk
        dvo_ref[...] = (dv_refs[0][...] + dv_refs[1][...] + dv_refs[2][...]).astype(BF)
        _accum(dgq_ref, _fold_heads(dgq), i)
        _accum(dgk_ref, _fold_heads(dgk), i)

    return _rowwise(body, name="qk_bwd", nrows=S, tile=ROW_TILE,
                    row_ins=[(a, D_ATTN, 0) for a in (*dqs, *dks, *dvs)]
                    + [(proj, D_ATTN, 0), (proj, D_ATTN, 1), (pos_col, 1, 0)],
                    full_ins=[gq128, gk128, inv_tab], row_outs=[(D_ATTN, BF)] * 3,
                    acc_outs=[((1, LANES), F32)] * 2)


def _ssd_post_bwd(y_ssd, xbc, proj, dmix, dskip_b, g_ssm):
    def body(i, n, y_ref, xs_ref, z_ref, do_ref, d_ref, g_ref, dy_ref, dz_ref, dxs_ref, dd_ref, dg_ref):
        z, xs = z_ref[...], xs_ref[...]
        sg = _sigmoid(z)
        gate = z * sg
        y = y_ref[...] + d_ref[...] * xs
        dy2, dg = _seg_rms_bwd(y * gate, g_ref[...], do_ref[...], 256)
        dy = dy2 * gate
        dy_ref[...] = dy.astype(BF)
        dz_ref[...] = (dy2 * y * (sg * (1.0 + z * (1.0 - sg)))).astype(BF)
        dxs_ref[...] = dy * d_ref[...]
        _accum(dd_ref, jnp.sum(dy * xs, axis=0, keepdims=True), i)
        _accum(dg_ref, dg, i)

    return _rowwise(body, name="ssd_post_bwd", nrows=S, tile=ROW_TILE,
                    row_ins=[(y_ssd, D_SSM, 0), (xbc, D_SSM, 0), (proj, D_SSM, 3), (dmix, D_SSM, 1)],
                    full_ins=[dskip_b, g_ssm], row_outs=[(D_SSM, BF), (D_SSM, BF), (D_SSM, F32)],
                    acc_outs=[((1, D_SSM), F32), ((1, D_SSM), F32)])


def _ssd_bwd(xbc, dy, h_all, a_b, dt_b, at_r, dt_r):
    def kern(c_ref, b_ref, x_ref, dy_ref, hall_ref, ab_ref, abp_ref, dtb_ref, at_ref, dt_ref,
             dc_ref, daq_ref, dx_ref, db_ref, dak_ref, ddt_ref, ddtb_ref, dh_scr, carry_scr):
        i = pl.program_id(1)

        @pl.when(i == 0)
        def _():
            dh_scr[...] = jnp.zeros_like(dh_scr)
            carry_scr[...] = jnp.zeros_like(carry_scr)

        ci, bi, x = c_ref[...].astype(BF), b_ref[...].astype(BF), x_ref[...]
        dyv = dy_ref[...].astype(F32)
        ab, dtb = ab_ref[...], dtb_ref[...]
        a_prev = jnp.where(i == NB - 1, 0.0, abp_ref[BLK - 1:BLK, :])
        a_end = ab[BLK - 1:BLK, :]
        alpha = jnp.exp(ab - a_prev)
        e_end = jnp.exp(a_end - ab)
        beta = e_end * dtb
        t_all = alpha[BLK - 1:BLK, :]
        hc = hall_ref[...]
        hcb = hc.astype(BF)
        dh_next = dh_scr[...]
        dhb = dh_next.astype(BF)
        half0 = lax.broadcasted_iota(jnp.int32, (BLK, LANES), 1) < HEAD
        hms = (half0, jnp.logical_not(half0))
        causal = lax.broadcasted_iota(jnp.int32, (BLK, BLK), 1) <= lax.broadcasted_iota(jnp.int32, (BLK, BLK), 0)
        row = lax.broadcasted_iota(jnp.int32, (BLK, LANES), 0)

        cb = _dot(ci, bi, NT)
        at, dtj = at_ref[...], dt_ref[...]
        dcb = jnp.zeros((BLK, BLK), F32)
        dx = jnp.zeros((BLK, LANES), F32)
        rows = []
        for hd in range(2):
            dym = jnp.where(hms[hd], dyv, 0.0).astype(BF)
            lm = _ssd_decay(ab[:, HEAD * hd:HEAD * hd + 1], at[hd:hd + 1, :], causal)
            dth = dtj[hd:hd + 1, :]
            dw = _dot(dym, jnp.where(hms[hd], x, 0.0).astype(BF), NT)
            g = dw * cb * lm
            dx = dx + _dot((cb * lm * dth).astype(BF), dym, TN)
            gcol = jnp.sum(g, axis=0, keepdims=True)
            ddt_ref[hd:hd + 1, :] = gcol
            dak_ref[hd:hd + 1, :] = gcol * dth
            rows.append(jnp.sum(g * dth, axis=1, keepdims=True))
            dcb = dcb + dw * lm * dth
        dcb = dcb.astype(BF)

        g1 = (alpha * dyv).astype(BF)
        dalpha = dyv * _dot(ci, hcb, NN)
        g2 = _dot(bi, dhb, NN)
        dbeta = x * g2
        bx = (beta * x).astype(BF)
        dc_ref[...] = _dot(dcb, bi, NN) + _dot(g1, hcb, NT)
        db_ref[...] = _dot(dcb, ci, TN) + _dot(bx, dhb, NT)
        dx_ref[...] = dx + beta * g2
        ddtb_ref[...] = _seg_sum(e_end * dbeta, HEAD)
        ada, bdb = alpha * dalpha, beta * dbeta
        t_dt = t_all * jnp.sum(dh_next * hc, axis=0, keepdims=True)
        end_term = _seg_sum(jnp.broadcast_to(jnp.sum(bdb, axis=0, keepdims=True) + t_dt, (8, LANES)), HEAD)[0:1]
        prev_term = _seg_sum(jnp.broadcast_to(jnp.sum(ada, axis=0, keepdims=True) + t_dt, (8, LANES)), HEAD)[0:1]
        daq = jnp.where(half0, rows[0], rows[1]) + _seg_sum(ada - bdb, HEAD)
        daq_ref[...] = daq + jnp.where(row == BLK - 1, end_term - carry_scr[0:1, :], 0.0)
        carry_scr[...] = jnp.broadcast_to(prev_term, carry_scr.shape)
        dh_scr[...] = t_all * dh_next + _dot(ci, g1, TN)

    npair = D_SSM // LANES
    rev = lambda i: NB - 1 - i
    rowvec = pl.BlockSpec((None, None, 2, BLK), lambda p, i: (p, rev(i), 0, 0))
    tile = pl.BlockSpec((BLK, LANES), lambda p, i: (rev(i), p))
    prev_tile = pl.BlockSpec((BLK, LANES), lambda p, i: (jnp.maximum(rev(i) - 1, 0), p))
    return pl.pallas_call(
        kern, name="ssd_bwd", grid=(npair, NB),
        in_specs=[pl.BlockSpec((BLK, LANES), lambda p, i: (rev(i), 12 + p // 2)),
                  pl.BlockSpec((BLK, LANES), lambda p, i: (rev(i), 8 + p // 2)),
                  tile, tile, tile, tile, prev_tile, tile, rowvec, rowvec],
        out_specs=[tile, tile, tile, tile, rowvec, rowvec, tile],
        out_shape=[jax.ShapeDtypeStruct((S, D_SSM), F32)] * 4
        + [jax.ShapeDtypeStruct((npair, NB, 2, BLK), F32)] * 2 + [jax.ShapeDtypeStruct((S, D_SSM), F32)],
        scratch_shapes=[pltpu.VMEM((BLK, LANES), F32), pltpu.VMEM((8, LANES), F32)],
        compiler_params=_params("parallel", "arbitrary"))(xbc, xbc, xbc, dy, h_all, a_b, a_b, dt_b, at_r, dt_r)


def _ssd_prep_bwd(daq, dak, ddt_row, ddt_lane, dt, proj, dt_bias128, a_log128):
    def body(i, n, daq_ref, dak_ref, dd_ref, dd2_ref, dt_ref, raw_ref, b_ref, al_ref, draw_ref, dal_ref, db_ref,
             carry):
        @pl.when(i == 0)
        def _():
            carry[...] = jnp.zeros_like(carry)

        a = -jnp.exp(al_ref[...])
        tri = (lax.broadcasted_iota(jnp.int32, (BLK, BLK), 0) <= lax.broadcasted_iota(jnp.int32, (BLK, BLK), 1))
        rev = _dot3(tri.astype(BF), daq_ref[...] - dak_ref[...]) + carry[0:1, :]
        carry[...] = jnp.broadcast_to(rev[0:1, :], carry.shape)
        dtv = dt_ref[...]
        draw = (dd_ref[...] + dd2_ref[...] + a * rev) * _sigmoid(raw_ref[...] + b_ref[...])
        draw_ref[...] = draw.astype(BF)
        _accum(dal_ref, jnp.sum(dtv * rev, axis=0, keepdims=True) * a, i)
        _accum(db_ref, jnp.sum(draw, axis=0, keepdims=True), i)

    return _rowwise(body, name="ssd_prep_bwd", nrows=S, tile=BLK, reverse=True,
                    row_ins=[(daq, LANES, 0), (dak, LANES, 0), (ddt_row, LANES, 0), (ddt_lane, LANES, 0), (dt, LANES, 0),
                             (proj, LANES, (D_INP - LANES) // LANES)],
                    full_ins=[dt_bias128, a_log128], row_outs=[(LANES, BF)],
                    acc_outs=[((1, LANES), F32), ((1, LANES), F32)], scratch=[pltpu.VMEM((8, LANES), F32)])


def _conv_bwd(proj, conv_w, conv_b, d1, d2, map1, map2, col0, ntiles, name):
    base = (4 * D_ATTN + col0) // LANES

    def kern(x_ref, w_ref, b_ref, d1_ref, d2_ref, dx_ref, dw_ref, db_ref):
        x = x_ref[...]
        c, shifted = _conv_taps(x, w_ref)
        c = c + b_ref[...]
        sg = _sigmoid(c)
        dc = (d1_ref[...] + d2_ref[...]) * (sg * (1.0 + c * (1.0 - sg)))
        rows = lax.broadcasted_iota(jnp.int32, x.shape, 0)
        dx = jnp.zeros_like(x)
        for w in range(CONV_W):
            k = CONV_W - 1 - w
            up = dc if k == 0 else jnp.where(rows < S - k, pltpu.roll(dc, S - k, 0), 0.0)
            dx = dx + up * w_ref[w:w + 1, :]
            dw_ref[w:w + 1, :] = jnp.sum(dc * shifted[w], axis=0, keepdims=True)
        dx_ref[...] = dx.astype(BF)
        db_ref[...] = jnp.sum(dc, axis=0, keepdims=True)

    c0 = col0 // LANES
    return pl.pallas_call(
        kern, name=name, grid=(ntiles,),
        in_specs=[pl.BlockSpec((S, LANES), lambda c: (0, base + c)),
                  pl.BlockSpec((CONV_W, LANES), lambda c: (0, c0 + c)),
                  pl.BlockSpec((1, LANES), lambda c: (0, c0 + c)),
                  pl.BlockSpec((S, LANES), lambda c: (0, map1(c))),
                  pl.BlockSpec((S, LANES), lambda c: (0, map2(c)))],
        out_specs=[pl.BlockSpec((S, LANES), lambda c: (0, c)), pl.BlockSpec((CONV_W, LANES), lambda c: (0, c)),
                   pl.BlockSpec((1, LANES), lambda c: (0, c))],
        out_shape=[jax.ShapeDtypeStruct((S, ntiles * LANES), BF), jax.ShapeDtypeStruct((CONV_W, ntiles * LANES), F32),
                   jax.ShapeDtypeStruct((1, ntiles * LANES), F32)],
        compiler_params=_params("parallel"))(proj, conv_w, conv_b, d1, d2)


def _adamw(w, m, v, parts, name):
    r, c = w.shape
    n_parts = parts.shape[0]
    tile = r if r <= 512 else (128 if c > 1024 else 256)
    assert r % tile == 0
    c1 = 1.0 - ADAM_B1 ** ADAM_STEP
    c2 = 1.0 - ADAM_B2 ** ADAM_STEP

    def kern(w_ref, m_ref, v_ref, p_ref, g_ref, d_ref, mo_ref, vo_ref):
        g = p_ref[0].astype(F32)
        for k in range(1, n_parts):
            g = g + p_ref[k].astype(F32)
        mn = ADAM_B1 * m_ref[...] + (1.0 - ADAM_B1) * g
        vn = ADAM_B2 * v_ref[...] + (1.0 - ADAM_B2) * (g * g)
        g_ref[...] = g
        mo_ref[...] = mn
        vo_ref[...] = vn
        d_ref[...] = -ADAM_LR * ((mn / c1) / (jnp.sqrt(vn / c2) + ADAM_EPS) + ADAM_WD * w_ref[...])

    blk = pl.BlockSpec((tile, c), lambda i: (i, 0))
    return pl.pallas_call(
        kern, name=name, grid=(r // tile,),
        in_specs=[blk, blk, blk, pl.BlockSpec((n_parts, tile, c), lambda i: (0, i, 0))],
        out_specs=[blk] * 4, out_shape=[jax.ShapeDtypeStruct((r, c), F32)] * 4,
        compiler_params=_params("parallel"))(w, m, v, parts)


MESH = pl.DeviceIdType.MESH
ANY = pl.BlockSpec(memory_space=pl.ANY)


def _put_own(gathered, shard):
    me = 4 * lax.axis_index("x") + 2 * lax.axis_index("y") + lax.axis_index("c")
    return lax.dynamic_update_slice(gathered, shard[None], (me,) + (0,) * shard.ndim)


def _all_gather(arrs, name, after=None):
    na = len(arrs)
    n_after = 0 if after is None else 1

    def kern(*refs):
        ins, outs = refs[:na], refs[na + n_after:2 * na + n_after]
        send_sems, recv_sems = refs[2 * na + n_after:]
        x, y, c = lax.axis_index("x"), lax.axis_index("y"), lax.axis_index("c")
        me, sibling = (x, y, c), (x, y, 1 - c)
        chips = [(1 - x, y), (x, 1 - y), (1 - x, 1 - y)]

        def copy(a, k, block, to, src=None):
            px, py, pc = block
            dst = outs[a].at[4 * px + 2 * py + pc]
            return pltpu.make_async_remote_copy(
                src_ref=dst if src is None else src, dst_ref=dst, send_sem=send_sems.at[a, k],
                recv_sem=recv_sems.at[a, k], device_id=to, device_id_type=MESH)

        first = []
        for a in range(na):
            first.append(copy(a, 0, me, sibling, src=ins[a]))
            first += [copy(a, 1 + j, me, (*chip, c), src=ins[a]) for j, chip in enumerate(chips)]
        for cp in first:
            cp.start()
        passed = []
        for a in range(na):
            for j, chip in enumerate(chips):
                copy(a, 1 + j, (*chip, c), me).wait_recv()
                fwd = copy(a, 4 + j, (*chip, c), sibling)
                fwd.start()
                passed.append(fwd)
        for a in range(na):
            copy(a, 0, sibling, me).wait_recv()
            for j, chip in enumerate(chips):
                copy(a, 4 + j, (*chip, 1 - c), me).wait_recv()
        for cp in first + passed:
            cp.wait_send()

    outs = pl.pallas_call(
        kern, name=name, in_specs=[ANY] * (na + n_after), out_specs=[ANY] * na,
        out_shape=[jax.ShapeDtypeStruct((N_DEV,) + a.shape, a.dtype) for a in arrs],
        scratch_shapes=[pltpu.SemaphoreType.DMA((na, 7)), pltpu.SemaphoreType.DMA((na, 7))],
    )(*arrs, *([] if after is None else [after]))
    return [_put_own(o, a) for o, a in zip(outs, arrs)]


HBM = pl.BlockSpec(memory_space=pltpu.HBM)
SEM = pl.BlockSpec(memory_space=pltpu.SEMAPHORE)
EFFECT = pltpu.SideEffectType.DATAFLOW_SIDE_EFFECTING
N_CHIP = 4


def _in_hbm(a):
    return pltpu.with_memory_space_constraint(a, pltpu.HBM)


def _chip_copies(kind, src_refs, land_refs, send_sems, recv_sems):
    x, y, c = lax.axis_index("x"), lax.axis_index("y"), lax.axis_index("c")
    cps = []
    for a in range(len(src_refs)):
        for j, (px, py) in enumerate([(1 - x, y), (x, 1 - y), (1 - x, 1 - y)]):
            if kind == "gather":
                src, dst = src_refs[a], land_refs[a].at[4 * x + 2 * y + c]
            else:
                src, dst = src_refs[a].at[2 * px + py], land_refs[a].at[2 * x + y]
            cps.append(pltpu.make_async_remote_copy(
                src_ref=src, dst_ref=dst, send_sem=send_sems.at[3 * a + j], recv_sem=recv_sems.at[3 * a + j],
                device_id=(px, py, c), device_id_type=MESH))
    return cps


def _exchange_start(kind, srcs, lands, after, name):
    na = len(srcs)
    n_after = 0 if after is None else 1

    def kern(*refs):
        src_refs, land_refs = refs[:na], refs[na:2 * na]
        send_sems, recv_sems = refs[2 * na + n_after], refs[2 * na + n_after + 1]
        token = refs[-1]
        for cp in _chip_copies(kind, src_refs, land_refs, send_sems, recv_sems):
            cp.start()
        token[...] = jnp.zeros_like(token)

    bufs = list(srcs) + list(lands)
    res = pl.pallas_call(
        kern, name=name,
        out_shape=(pltpu.SemaphoreType.DMA((3 * na,)), pltpu.SemaphoreType.DMA((3 * na,)))
        + tuple(pltpu.HBM(b.shape, b.dtype) for b in bufs) + (jax.ShapeDtypeStruct((8, LANES), F32),),
        in_specs=[HBM] * (2 * na) + [ANY] * n_after,
        out_specs=(SEM, SEM) + (HBM,) * (2 * na) + (pl.BlockSpec(memory_space=pltpu.VMEM),),
        input_output_aliases={i: 2 + i for i in range(2 * na)},
        compiler_params=pltpu.CompilerParams(has_side_effects=EFFECT),
    )(*[_in_hbm(b) for b in bufs], *([] if after is None else [after]))
    return (res[0], res[1]), list(res[2:2 + na]), list(res[2 + na:2 + 2 * na]), res[-1]


def _exchange_wait(kind, sems, srcs, lands, after, name):
    na = len(srcs)

    def kern(*refs):
        src_refs, land_refs = refs[:na], refs[na:2 * na]
        send_sems, recv_sems = refs[2 * na], refs[2 * na + 1]
        for cp in _chip_copies(kind, src_refs, land_refs, send_sems, recv_sems):
            cp.wait_send()
            cp.wait_recv()

    bufs = list(srcs) + list(lands)
    res = pl.pallas_call(
        kern, name=name, out_shape=tuple(pltpu.HBM(b.shape, b.dtype) for b in bufs),
        in_specs=[HBM] * (2 * na) + [SEM, SEM, ANY], out_specs=(HBM,) * (2 * na),
        input_output_aliases={i: i for i in range(2 * na)},
        compiler_params=pltpu.CompilerParams(has_side_effects=EFFECT),
    )(*bufs, sems[0], sems[1], after)
    return list(res[:na]), list(res[na:])


def _gather_finish(shards, lands, name):
    na = len(shards)

    def kern(*refs):
        ins, outs = refs[:na], refs[2 * na:3 * na]
        send_sems, recv_sems = refs[3 * na:]
        x, y, c = lax.axis_index("x"), lax.axis_index("y"), lax.axis_index("c")
        chips = [(1 - x, y), (x, 1 - y), (1 - x, 1 - y)]

        def copy(a, k, slot, pc, src=None):
            dst = outs[a].at[slot + pc]
            return pltpu.make_async_remote_copy(
                src_ref=dst if src is None else src, dst_ref=dst, send_sem=send_sems.at[a, k],
                recv_sem=recv_sems.at[a, k], device_id=(x, y, 1 - c), device_id_type=MESH)

        sends = []
        for a in range(na):
            sends.append(copy(a, 0, 4 * x + 2 * y, c, src=ins[a]))
            sends += [copy(a, 1 + j, 4 * px + 2 * py, c) for j, (px, py) in enumerate(chips)]
        for cp in sends:
            cp.start()
        for a in range(na):
            copy(a, 0, 4 * x + 2 * y, 1 - c).wait_recv()
            for j, (px, py) in enumerate(chips):
                copy(a, 1 + j, 4 * px + 2 * py, 1 - c).wait_recv()
        for cp in sends:
            cp.wait_send()

    return pl.pallas_call(
        kern, name=name, in_specs=[ANY] * (2 * na), out_specs=[ANY] * na,
        out_shape=[jax.ShapeDtypeStruct(l.shape, l.dtype) for l in lands],
        input_output_aliases={na + a: a for a in range(na)},
        scratch_shapes=[pltpu.SemaphoreType.DMA((na, 4)), pltpu.SemaphoreType.DMA((na, 4))])(*shards, *lands)


def _pair_exchange(parts, name):
    na = len(parts)

    def kern(*refs):
        ins, got = refs[:na], refs[na:2 * na]
        send_sems, recv_sems = refs[2 * na:]
        x, y, c = lax.axis_index("x"), lax.axis_index("y"), lax.axis_index("c")
        sends = []
        for a in range(na):
            for q in range(N_CHIP):
                sends.append(pltpu.make_async_remote_copy(
                    src_ref=ins[a].at[2 * q + 1 - c], dst_ref=got[a].at[q], send_sem=send_sems.at[a, q],
                    recv_sem=recv_sems.at[a, q], device_id=(x, y, 1 - c), device_id_type=MESH))
        for cp in sends:
            cp.start()
        for cp in sends:
            cp.wait()

    return pl.pallas_call(
        kern, name=name, in_specs=[ANY] * na, out_specs=[ANY] * na,
        out_shape=[jax.ShapeDtypeStruct((N_CHIP,) + p.shape[1:], p.dtype) for p in parts],
        scratch_shapes=[pltpu.SemaphoreType.DMA((na, N_CHIP)), pltpu.SemaphoreType.DMA((na, N_CHIP))])(*parts)


def _pair_sum(parts, got, name):
    _, r, cdim = parts.shape
    tile = min(r, ROW_TILE)
    core = lax.axis_index("c").astype(jnp.int32).reshape(1)

    def kern(c_ref, a_ref, b_ref, q_ref, l_ref):
        s = (a_ref[...].astype(F32) + b_ref[...].astype(F32)).astype(BF)
        q_ref[...] = s
        l_ref[...] = s

    blk = pl.BlockSpec((None, tile, cdim), lambda q, i, c_ref: (q, i, 0))
    out = jax.ShapeDtypeStruct((N_CHIP, r, cdim), BF)
    return pl.pallas_call(
        kern, name=name, out_shape=[out, out],
        grid_spec=pltpu.PrefetchScalarGridSpec(
            num_scalar_prefetch=1, grid=(N_CHIP, r // tile),
            in_specs=[pl.BlockSpec((None, tile, cdim), lambda q, i, c_ref: (2 * q + c_ref[0], i, 0)), blk],
            out_specs=[blk, blk]),
        compiler_params=_params("parallel", "parallel"))(core, parts, got)


def _perm(t, d):
    return t if d == 1 else t.reshape(S // d, d, t.shape[-1]).transpose(1, 0, 2).reshape(S, t.shape[-1])


def _unperm(t, d):
    return t if d == 1 else t.reshape(d, S // d, t.shape[-1]).transpose(1, 0, 2).reshape(S, t.shape[-1])


def _pad_lanes(v, width=LANES):
    return jnp.pad(v, ((0, 0), (0, width - v.shape[1])))


def _row_layout(t16):
    return t16.T.reshape(N_HEADS // 2, 2, NB, BLK).transpose(0, 2, 1, 3)


def _row_layout_inv(r):
    return r.transpose(0, 2, 1, 3).reshape(N_HEADS, S).T


SMALL = (("g_mix", D), ("g_q", HEAD), ("g_k", HEAD), ("g_attn_out", D_ATTN), ("conv_b", D_CONV),
         ("dt_bias", 16), ("a_log", 16), ("d_skip", 16), ("g_ssm_out", D_SSM), ("g_cross", D),
         ("g_mem", D), ("g_cq", CROSS_HEAD), ("g_ck", CROSS_HEAD), ("g_mlp", D))
SMALL_ROWS = -(-sum(-(-w // LANES) for _, w in SMALL) // 8) * 8


def _pack_small(vals):
    rows = [_pad_lanes(vals[n], -(-w // LANES) * LANES).reshape(-1, LANES) for n, w in SMALL]
    packed = jnp.concatenate(rows, axis=0)
    return jnp.pad(packed, ((0, SMALL_ROWS - packed.shape[0]), (0, 0)))


def _unpack_small(packed):
    out, r = {}, 0
    for n, w in SMALL:
        nr = -(-w // LANES)
        out[n] = packed[r:r + nr].reshape(1, nr * LANES)[:, :w]
        r += nr
    return out


BIG = ("w_in", "w_out", "w_cq", "w_ckv", "w_co", "w_up", "w_down")
WEIGHTS = ("g_mix", "w_in", "g_q", "g_k", "g_attn_out", "conv_w", "conv_b", "dt_bias", "a_log", "d_skip",
           "g_ssm_out", "w_out", "g_cross", "g_mem", "w_cq", "w_ckv", "g_cq", "g_ck", "w_co", "g_mlp", "w_up", "w_down")


REST = ("w_out", "w_cq", "w_ckv", "w_co", "w_up", "w_down")


class _Overlap:
    def __init__(self, shards, after):
        lands = [_put_own(lax.empty((N_DEV,) + s.shape, s.dtype), s) for s in shards]
        self.gather = _exchange_start("gather", shards, lands, after, "ag_rest_start")
        self.token = self.gather[3]
        self.groups = []

    def rest(self, after):
        sems, srcs, lands, _ = self.gather
        srcs, lands = _exchange_wait("gather", sems, srcs, lands, after, "ag_rest_wait")
        wf = dict(zip(REST, _gather_finish(srcs, lands, "ag_rest_finish")))
        for n in ("w_out", "w_cq", "w_ckv", "w_down"):
            wf[n] = wf[n].reshape(-1, wf[n].shape[-1])
        return wf

    def emit(self, grads):
        names, tag = list(grads), "_%d" % len(self.groups)
        got = _pair_exchange([grads[n] for n in names], "rs_pair" + tag)
        sums = [_pair_sum(grads[n], b, "rs_sum_" + n) for n, b in zip(names, got)]
        sems, srcs, lands, token = _exchange_start("scatter", [q for q, _ in sums], [l for _, l in sums], None,
                                                   "rs_chip_start" + tag)
        self.groups.append((names, sems, srcs, lands))
        return token

    def finish(self, gi, after):
        names, sems, srcs, lands = self.groups[gi]
        _, lands = _exchange_wait("scatter", sems, srcs, lands, after, "rs_chip_wait_%d" % gi)
        return dict(zip(names, lands))


def _tie(v, token):
    return v if token is None else v + token[0:1, 0:1]


def _local_step(x, mem, pos_col, target, p, wf, hooks=None):
    p = dict(p)
    inv = np.zeros((1, LANES), np.float32)
    freq = (ROPE_THETA ** (-2.0 * np.arange(ROT // 2, dtype=np.float32) / ROT)).astype(np.float32)
    for l in range(LANES):
        if l % HEAD < ROT:
            inv[0, l] = freq[(l % HEAD) % (ROT // 2)]
    inv_tab = jnp.asarray(inv)
    gq128, gk128 = jnp.tile(p["g_q"], (1, 2)), jnp.tile(p["g_k"], (1, 2))
    dtb128, alog128 = _pad_lanes(p["dt_bias"]), _pad_lanes(p["a_log"])
    dskip_b = jnp.repeat(p["d_skip"], HEAD, axis=1)
    conv_w, conv_b = wf["conv_w"], p["conv_b"]

    h = _rms_fwd(x, p["g_mix"], "rms_mix")
    proj = _matmul(h, wf["w_in"], mode="nn", name="mm_in", tm=1024, tn=896, tk=512)
    qr, kr, vb = _qk_prep(proj, pos_col, gq128, gk128, inv_tab)
    pats = ((1, NB), (4, NB // 4), (16, 1))
    qs = [_perm(qr, d) for d, _ in pats]
    ks = [_perm(kr, d) for d, _ in pats]
    vs = [_perm(vb, d) for d, _ in pats]
    os_, ls_ = [], []
    for (d, nb), qd, kd, vd in zip(pats, qs, ks, vs):
        o, l = _attn_fwd(qd, kd, vd, nb, "attn_fwd_d%d" % d)
        os_.append(_unperm(o, d))
        ls_.append(_unperm(l, d))
    attn, lse, attn_n = _attn_merge(os_, ls_, p["g_attn_out"])

    xbc = _conv_fwd(proj, conv_w, conv_b)
    dt, a_cs = _ssd_prep(proj, dtb128, alog128)
    a_b = jnp.repeat(a_cs[:, :N_HEADS], HEAD, axis=1)
    dt_b = jnp.repeat(dt[:, :N_HEADS], HEAD, axis=1)
    at_r, dt_r = _row_layout(a_cs[:, :N_HEADS]), _row_layout(dt[:, :N_HEADS])
    y_ssd, h_all = _ssd_fwd(xbc, a_b, dt_b, at_r, dt_r)
    ssm_n = _ssd_post(y_ssd, xbc, proj, dskip_b, p["g_ssm_out"])

    if hooks is not None:
        wf = {**wf, **hooks.rest(ssm_n)}
    mix = jnp.concatenate([attn_n, ssm_n], axis=1)
    x1 = _matmul(mix, wf["w_out"], mode="nn", name="mm_out", tm=1024, tn=1024, tk=512,
                 extras=(x,), epilogue=lambda acc, r: (acc + r,))
    hc = _rms_fwd(x1, p["g_cross"], "rms_cross")
    memh = _rms_fwd(mem, p["g_mem"], "rms_mem")
    qc = _matmul(hc, wf["w_cq"], mode="nn", name="mm_cq", tm=1024, tn=512, tk=512)
    kv = _matmul(memh, wf["w_ckv"], mode="nn", name="mm_ckv", tm=N_MEM, tn=1024, tk=512)
    oc = _cross_fwd(qc, kv, p["g_cq"], p["g_ck"])
    x2 = _matmul(oc, wf["w_co"], mode="nn", name="mm_co", tm=1024, tn=256, tk=512, b_cb=256,
                 extras=(x1,), epilogue=lambda acc, r: (acc + r,))
    hm = _rms_fwd(x2, p["g_mlp"], "rms_mlp")

    def up_epi(acc):
        ru = jnp.maximum(acc, 0.0)
        return ru * ru, 2.0 * ru

    act, relu2 = _matmul(hm, wf["w_up"], mode="nn", name="mm_up", tm=1024, tn=1024, tk=512, b_cb=1024,
                         out_dtypes=(BF, BF), epilogue=up_epi)
    dy = _matmul(act, wf["w_down"], mode="nn", name="mm_down", tm=1024, tn=1024, tk=512, extras=(x2, target),
                 epilogue=lambda acc, r, t: ((acc + r - t) * (1.0 / D),))
    loss_part = _loss_sum(dy)[0, 0] * (0.5 * D)

    gb, gs = {}, {}
    du = _matmul(dy, wf["w_down"], mode="nt", name="mm_down_dx", tm=1024, tn=1024, tk=512, extras=(relu2,),
                 out_dtypes=(BF,), epilogue=lambda acc, r: (acc * r.astype(F32),))
    gb["w_down"] = _matmul(act, dy, mode="tn", name="mm_down_dw", tm=1024, tn=1024, tk=512,
                           out_dtypes=(BF,)).reshape(N_DEV, D_FF // N_DEV, D)
    gb["w_up"] = _matmul(hm, du, mode="tn", name="mm_up_dw", tm=1024, tn=1024, tk=512, out_dtypes=(BF,), out_cb=1024)
    if hooks is not None:
        p["g_mlp"] = _tie(p["g_mlp"], hooks.emit({n: gb[n] for n in ("w_down", "w_up")}))
    dhm = _matmul(du, wf["w_up"], mode="nt", name="mm_up_dx", tm=1024, tn=1024, tk=512, b_cb=1024)
    dx2, gs["g_mlp"] = _rms_bwd_call(x2, p["g_mlp"], dhm, dy, "rms_mlp_bwd")

    doc = _matmul(dx2, wf["w_co"], mode="nt", name="mm_co_dx", tm=1024, tn=512, tk=256, b_cb=256)
    gb["w_co"] = _matmul(oc, dx2, mode="tn", name="mm_co_dw", tm=512, tn=256, tk=512, out_dtypes=(BF,), out_cb=256)
    dqc, dkn, dvc, gs["g_cq"] = _cross_bwd(qc, doc, kv, p["g_cq"], p["g_ck"])
    dkv, gs["g_ck"] = _cross_kv_bwd(kv, dkn, dvc, p["g_ck"])
    gb["w_cq"] = _matmul(hc, dqc, mode="tn", name="mm_cq_dw", tm=1024, tn=512, tk=512,
                         out_dtypes=(BF,)).reshape(N_DEV, D // N_DEV, D_CROSS)
    dhc = _matmul(dqc, wf["w_cq"], mode="nt", name="mm_cq_dx", tm=1024, tn=1024, tk=512)
    gb["w_ckv"] = _matmul(memh, dkv, mode="tn", name="mm_ckv_dw", tm=1024, tn=1024, tk=N_MEM,
                          out_dtypes=(BF,)).reshape(N_DEV, D // N_DEV, 2 * D_CROSS)
    dmemh = _matmul(dkv, wf["w_ckv"], mode="nt", name="mm_ckv_dx", tm=N_MEM, tn=1024, tk=512)
    (gs["g_mem"],) = _rms_bwd_call(mem, p["g_mem"], dmemh, None, "rms_mem_bwd")
    dx1, gs["g_cross"] = _rms_bwd_call(x1, p["g_cross"], dhc, dx2, "rms_cross_bwd")

    dmix = _matmul(dx1, wf["w_out"], mode="nt", name="mm_out_dx", tm=1024, tn=1024, tk=512)
    gb["w_out"] = _matmul(mix, dx1, mode="tn", name="mm_out_dw", tm=1024, tn=1024, tk=512,
                          out_dtypes=(BF,)).reshape(N_DEV, D // N_DEV, D)
    if hooks is not None:
        p["g_attn_out"] = _tie(p["g_attn_out"], hooks.emit({n: gb[n] for n in ("w_co", "w_cq", "w_ckv", "w_out")}))

    dattn, delta, gs["g_attn_out"] = _attn_merge_bwd(dmix, attn, p["g_attn_out"])
    dqs, dks, dvs = [], [], []
    for (d, nb), qd, kd, vd in zip(pats, qs, ks, vs):
        dq, dk, dv = _attn_bwd(qd, kd, vd, _perm(dattn, d), _perm(lse, d), _perm(delta, d), nb, "attn_bwd_d%d" % d)
        dqs.append(_unperm(dq, d))
        dks.append(_unperm(dk, d))
        dvs.append(_unperm(dv, d))
    dq_pre, dk_pre, dv_pre, dgq, dgk = _qk_bwd(dqs, dks, dvs, proj, pos_col, gq128, gk128, inv_tab)
    gs["g_q"], gs["g_k"] = dgq[:, :HEAD], dgk[:, :HEAD]

    dy_ssd, dz, dxs_skip, dd_lane, gs["g_ssm_out"] = _ssd_post_bwd(y_ssd, xbc, proj, dmix, dskip_b, p["g_ssm_out"])
    gs["d_skip"] = dd_lane.reshape(N_HEADS, HEAD).sum(axis=1).reshape(1, N_HEADS)
    dc_pair, daq_b, dx_ssd, db_pair, dak_r, ddt_r, ddt_b = _ssd_bwd(xbc, dy_ssd, h_all, a_b, dt_b, at_r, dt_r)
    daq = _pad_lanes(daq_b[:, ::HEAD])
    dak = _pad_lanes(_row_layout_inv(dak_r))
    ddt_direct = _pad_lanes(_row_layout_inv(ddt_r))
    ddt_raw, dalog, dbias = _ssd_prep_bwd(daq, dak, ddt_direct, _pad_lanes(ddt_b[:, ::HEAD]), dt, proj,
                                          dtb128, alog128)
    gs["a_log"], gs["dt_bias"] = dalog[:, :N_HEADS], dbias[:, :N_HEADS]
    same = lambda c: c
    dxbc_x, dcw_x, dcb_x = _conv_bwd(proj, conv_w, conv_b, dxs_skip, dx_ssd, same, same, 0, 8, "conv_bwd_x")
    dxbc_b, dcw_b, dcb_b = _conv_bwd(proj, conv_w, conv_b, db_pair, db_pair, lambda c: 2 * c, lambda c: 2 * c + 1,
                                     D_SSM, 4, "conv_bwd_b")
    dxbc_c, dcw_c, dcb_c = _conv_bwd(proj, conv_w, conv_b, dc_pair, dc_pair, lambda c: 2 * c, lambda c: 2 * c + 1,
                                     D_SSM + 512, 4, "conv_bwd_c")
    g_conv_w = jnp.concatenate([dcw_x, dcw_b, dcw_c], axis=1)
    gs["conv_b"] = jnp.concatenate([dcb_x, dcb_b, dcb_c], axis=1)

    dproj = jnp.concatenate([dq_pre, dk_pre, dv_pre, dz, dxbc_x, dxbc_b, dxbc_c, ddt_raw], axis=1)
    dw_in = _matmul(h, dproj, mode="tn", name="mm_in_dw", tm=1024, tn=896, tk=512, out_dtypes=(BF,))
    gb["w_in"] = dw_in[:, :D_IN].reshape(D, N_DEV, D_IN // N_DEV).transpose(1, 0, 2)
    if hooks is not None:
        p["g_mix"] = _tie(p["g_mix"], hooks.emit({"w_in": gb["w_in"]}))
    dh = _matmul(dproj, wf["w_in"], mode="nt", name="mm_in_dx", tm=1024, tn=1024, tk=896)
    grad_x, gs["g_mix"] = _rms_bwd_call(x, p["g_mix"], dh, dx1, "rms_mix_bwd")
    return loss_part, grad_x, gb, gs, g_conv_w


def kernel(x, mem, positions, g_mix, w_in, g_q, g_k, g_attn_out, conv_w, conv_b, dt_bias, a_log, d_skip, g_ssm_out, w_out, g_cross, g_mem, w_cq, w_ckv, g_cq, g_ck, w_co, g_mlp, w_up, w_down, loss_target, m_g_mix, m_w_in, m_g_q, m_g_k, m_g_attn_out, m_conv_w, m_conv_b, m_dt_bias, m_a_log, m_d_skip, m_g_ssm_out, m_w_out, m_g_cross, m_g_mem, m_w_cq, m_w_ckv, m_g_cq, m_g_ck, m_w_co, m_g_mlp, m_w_up, m_w_down, v_g_mix, v_w_in, v_g_q, v_g_k, v_g_attn_out, v_conv_w, v_conv_b, v_dt_bias, v_a_log, v_d_skip, v_g_ssm_out, v_w_out, v_g_cross, v_g_mem, v_w_cq, v_w_ckv, v_g_cq, v_g_ck, v_w_co, v_g_mlp, v_w_up, v_w_down):
    args = dict(locals())
    w = {n: args[n] for n in WEIGHTS}
    m = {n: args["m_" + n] for n in WEIGHTS}
    v = {n: args["v_" + n] for n in WEIGHTS}
    small = {n: w[n] for n, _ in SMALL}
    me = 4 * lax.axis_index("x") + 2 * lax.axis_index("y") + lax.axis_index("c")

    w_in_g, conv_w_g = _all_gather([w["w_in"][0].astype(BF), w["conv_w"][0]], "ag_first")
    wf = {"w_in": jnp.pad(w_in_g.transpose(1, 0, 2).reshape(D, D_IN), ((0, 0), (0, D_INP - D_IN))),
          "conv_w": conv_w_g.transpose(1, 0, 2).reshape(CONV_W, D_CONV)}
    overlap = _Overlap([w[n][0].astype(BF) for n in REST], w_in_g)
    p = dict(small)
    p["g_mix"] = _tie(p["g_mix"], overlap.token)

    loss_part, grad_x, _, gs, g_conv_w = _local_step(
        x[0], mem[0], positions.reshape(S, 1), loss_target[0], p, wf, overlap)
    loss = lax.psum(loss_part, ("x", "y", "c"))

    out = {}
    last = grad_x
    for gi in range(3):
        for n, parts in overlap.finish(gi, last).items():
            res_n = _adamw(w[n][0], m[n][0], v[n][0], parts, "adamw_" + n)
            out[n] = [t.reshape(w[n].shape) for t in res_n]
            last = res_n[0]

    sm_parts, cw_parts = _all_gather([_pack_small(gs), g_conv_w], "ag_small_grads", after=last)
    sm = [_unpack_small(t) for t in _adamw(_pack_small(small), _pack_small({n: m[n] for n, _ in SMALL}),
                                           _pack_small({n: v[n] for n, _ in SMALL}), sm_parts, "adamw_small")]
    for n, _ in SMALL:
        out[n] = [t[n] for t in sm]
    cols = D_CONV // N_DEV
    cw_mine = lax.dynamic_slice_in_dim(cw_parts, me * cols, cols, axis=2)
    out["conv_w"] = [t.reshape(w["conv_w"].shape) for t in
                     _adamw(w["conv_w"][0], m["conv_w"][0], v["conv_w"][0], cw_mine, "adamw_conv_w")]

    res = [loss, grad_x.reshape(x.shape)]
    for k in range(4):
        res += [out[n][k] for n in WEIGHTS]
    return tuple(res)
```

```python
import functools
import math

import numpy as np
import jax
import jax.numpy as jnp
from jax import lax
from jax.experimental import pallas as pl
from jax.experimental.pallas import tpu as pltpu

F32 = jnp.float32
BF = jnp.bfloat16

N_DEV = 8
S = 2048
D = 2048
D_ATTN = 1024
N_HEADS = 16
HEAD = 64
ROT = 16
ROPE_THETA = 500000.0
D_SSM = 1024
D_CONV = 2048
CONV_W = 4
N_MEM = 256
D_CROSS = 512
CROSS_HEAD = 128
D_FF = 8192
D_IN = 6160
D_INP = 6272
EPS = 1e-6
BLK = 128
NB = S // BLK
LANES = 128
NEG = -1e30

ADAM_LR = 0.001
ADAM_B1 = 0.9
ADAM_B2 = 0.999
ADAM_EPS = 1e-08
ADAM_WD = 0.01
ADAM_STEP = 10

VMEM_LIMIT_BYTES = 48 * 1024 * 1024
ROW_TILE = 256


def _params(*sem):
    return pltpu.CompilerParams(dimension_semantics=sem, vmem_limit_bytes=VMEM_LIMIT_BYTES)


def _matmul(a, b, *, mode, name, tm, tn, tk, out_dtypes=(F32,), b_cb=None, out_cb=None,
            extras=(), epilogue=None, after=None):
    if mode == "tn":
        kk, m = a.shape
    else:
        m, kk = a.shape
    if b_cb is None:
        br, bc = b.shape
    else:
        br, bc = b.shape[1], N_DEV * b_cb
    n = br if mode == "nt" else bc
    assert m % tm == 0 and n % tn == 0 and kk % tk == 0, (name, m, n, kk)
    nk = kk // tk
    grid = (m // tm, n // tn, nk)

    if mode == "tn":
        a_spec = pl.BlockSpec((tk, tm), lambda i, j, k: (k, i))
    else:
        a_spec = pl.BlockSpec((tm, tk), lambda i, j, k: (i, k))
    if mode == "nt":
        b_blk, b_idx = (tn, tk), (lambda i, j, k: (j, k))
    else:
        b_blk, b_idx = (tk, tn), (lambda i, j, k: (k, j))
    if b_cb is None:
        b_spec = pl.BlockSpec(b_blk, b_idx)
    else:
        tc = b_blk[1]
        assert b_cb % tc == 0
        per = b_cb // tc

        def b_idx3(i, j, k):
            r, c = b_idx(i, j, k)
            return (c // per, r, c % per)
        b_spec = pl.BlockSpec((None,) + b_blk, b_idx3)
    ex_specs = [pl.BlockSpec((tm, tn), lambda i, j, k: (i, j)) for _ in extras]
    if out_cb is None:
        out_shape = [jax.ShapeDtypeStruct((m, n), dt) for dt in out_dtypes]
        out_specs = [pl.BlockSpec((tm, tn), lambda i, j, k: (i, j)) for _ in out_dtypes]
    else:
        assert out_cb % tn == 0
        pero = out_cb // tn
        out_shape = [jax.ShapeDtypeStruct((N_DEV, m, out_cb), dt) for dt in out_dtypes]
        out_specs = [pl.BlockSpec((None, tm, tn), lambda i, j, k: (j // pero, i, j % pero)) for _ in out_dtypes]
    dn = {"nn": (((1,), (0,)), ((), ())), "nt": (((1,), (1,)), ((), ())), "tn": (((0,), (0,)), ((), ()))}[mode]
    ne, no = len(extras), len(out_dtypes)
    n_after = 0 if after is None else 1

    def kern(a_ref, b_ref, *rest):
        ex_refs, out_refs = rest[:ne], rest[ne + n_after:ne + n_after + no]

        def finish(acc):
            outs = (acc,) if epilogue is None else epilogue(acc, *[r[...] for r in ex_refs])
            for r, o in zip(out_refs, outs):
                r[...] = o.astype(r.dtype)

        part = lax.dot_general(a_ref[...].astype(BF), b_ref[...].astype(BF), dn, preferred_element_type=F32)
        if nk == 1:
            finish(part)
            return
        acc_ref = rest[ne + n_after + no]
        k = pl.program_id(2)

        @pl.when(k == 0)
        def _():
            acc_ref[...] = part

        @pl.when(k > 0)
        def _():
            acc_ref[...] += part

        @pl.when(k == nk - 1)
        def _():
            finish(acc_ref[...])

    res = pl.pallas_call(
        kern, name=name, grid=grid, in_specs=[a_spec, b_spec] + ex_specs + [ANY] * n_after, out_specs=out_specs,
        out_shape=out_shape, scratch_shapes=[pltpu.VMEM((tm, tn), F32)] if nk > 1 else [],
        compiler_params=_params("parallel", "parallel", "arbitrary"),
    )(a, b, *extras, *([] if after is None else [after]))
    return res[0] if no == 1 else tuple(res)


def _rowwise(body, *, name, nrows, tile, row_ins, full_ins=(), row_outs=(), acc_outs=(), scratch=(),
             reverse=False):
    n = nrows // tile

    def ridx(i):
        return (n - 1 - i) if reverse else i

    in_specs, args = [], []
    for arr, width, cb in row_ins:
        in_specs.append(pl.BlockSpec((tile, width), lambda i, cb=cb: (ridx(i), cb)))
        args.append(arr)
    for arr in full_ins:
        in_specs.append(pl.BlockSpec(arr.shape, lambda i, nd=arr.ndim: (0,) * nd))
        args.append(arr)
    out_shape, out_specs = [], []
    for width, dt in row_outs:
        out_shape.append(jax.ShapeDtypeStruct((nrows, width), dt))
        out_specs.append(pl.BlockSpec((tile, width), lambda i: (ridx(i), 0)))
    for shp, dt in acc_outs:
        out_shape.append(jax.ShapeDtypeStruct(shp, dt))
        out_specs.append(pl.BlockSpec(shp, lambda i, nd=len(shp): (0,) * nd))

    def kern(*refs):
        body(pl.program_id(0), n, *refs)

    return pl.pallas_call(kern, name=name, grid=(n,), in_specs=in_specs, out_specs=out_specs,
                          out_shape=out_shape, scratch_shapes=list(scratch),
                          compiler_params=_params("arbitrary"))(*args)


def _accum(ref, val, i):
    @pl.when(i == 0)
    def _():
        ref[...] = val

    @pl.when(i > 0)
    def _():
        ref[...] += val


def _sigmoid(x):
    return 1.0 / (1.0 + jnp.exp(-x))


def _rms_n(x):
    r = lax.rsqrt(jnp.mean(x * x, axis=-1, keepdims=True) + EPS)
    return x * r, r


def _rms_bwd(x, g, dh):
    n, r = _rms_n(x)
    dn = dh * g
    dx = r * (dn - n * jnp.mean(dn * n, axis=-1, keepdims=True))
    return dx, jnp.sum(dh * n, axis=0, keepdims=True)


def _seg_sum(x, seg):
    t, w = x.shape
    tiles = [x[:, LANES * j:LANES * (j + 1)] for j in range(w // LANES)]
    outs = []
    if seg == 64:
        lo = lax.broadcasted_iota(jnp.int32, (t, LANES), 1) < 64
        for xt in tiles:
            s_lo = jnp.sum(jnp.where(lo, xt, 0.0), axis=-1, keepdims=True)
            s_hi = jnp.sum(jnp.where(lo, 0.0, xt), axis=-1, keepdims=True)
            outs.append(jnp.where(lo, s_lo, s_hi))
    else:
        sums = [jnp.sum(xt, axis=-1, keepdims=True) for xt in tiles]
        if seg == 256:
            sums = [sums[2 * (j // 2)] + sums[2 * (j // 2) + 1] for j in range(len(sums))]
        else:
            assert seg == 128
        outs = [jnp.broadcast_to(s, (t, LANES)) for s in sums]
    return outs[0] if len(outs) == 1 else jnp.concatenate(outs, axis=1)


def _seg_rms_n(x, seg):
    r = lax.rsqrt(_seg_sum(x * x, seg) * (1.0 / seg) + EPS)
    return x * r, r


def _seg_rms_bwd(x, g, dy, seg):
    n, r = _seg_rms_n(x, seg)
    dn = dy * g
    dx = r * (dn - n * (_seg_sum(dn * n, seg) * (1.0 / seg)))
    return dx, jnp.sum(dy * n, axis=0, keepdims=True)


def _rope_tables(pos_col, inv_tab, t):
    ang = pos_col.astype(F32) * inv_tab
    idx = lax.broadcasted_iota(jnp.int32, (t, LANES), 1) % HEAD
    cos, sin = jnp.cos(ang), jnp.sin(ang)
    c = jnp.where(idx < ROT, cos, 1.0)
    sg = jnp.where(idx < ROT // 2, -sin, jnp.where(idx < ROT, sin, 0.0))
    return c, sg, idx < ROT // 2


def _rope(x, c, sg, lo):
    partner = jnp.where(lo, pltpu.roll(x, LANES - ROT // 2, 1), pltpu.roll(x, ROT // 2, 1))
    return x * c + partner * sg


def _fold_heads(v):
    v8 = jnp.broadcast_to(v, (8, LANES))
    return (v8 + pltpu.roll(v8, HEAD, 1))[0:1]


def _dot(a, b, dn):
    return lax.dot_general(a, b, (dn, ((), ())), preferred_element_type=F32)


NN = ((1,), (0,))
NT = ((1,), (1,))
TN = ((0,), (0,))


def _dot3(l01, x):
    x1 = x.astype(BF)
    r1 = x - x1.astype(F32)
    x2 = r1.astype(BF)
    x3 = (r1 - x2.astype(F32)).astype(BF)
    return _dot(l01, x1, NN) + _dot(l01, x2, NN) + _dot(l01, x3, NN)


def _rms_fwd(x, g, name):
    rows = x.shape[0]

    def body(i, n, x_ref, g_ref, o_ref):
        nx, _ = _rms_n(x_ref[...])
        o_ref[...] = (nx * g_ref[...]).astype(BF)

    return _rowwise(body, name=name, nrows=rows, tile=min(ROW_TILE, rows), row_ins=[(x, D, 0)],
                    full_ins=[g], row_outs=[(D, BF)])[0]


def _qk_prep(proj, pos_col, gq128, gk128, inv_tab):
    scale = HEAD ** -0.5

    def body(i, n, q_ref, k_ref, v_ref, p_ref, gq_ref, gk_ref, it_ref, qo_ref, ko_ref, vo_ref):
        t = q_ref.shape[0]
        c, sg, lo = _rope_tables(p_ref[...], it_ref[...], t)
        for j in range(D_ATTN // LANES):
            sl = slice(LANES * j, LANES * (j + 1))
            qn, _ = _seg_rms_n(q_ref[:, sl], HEAD)
            kn, _ = _seg_rms_n(k_ref[:, sl], HEAD)
            qo_ref[:, sl] = (_rope(qn * gq_ref[...], c, sg, lo) * scale).astype(BF)
            ko_ref[:, sl] = _rope(kn * gk_ref[...], c, sg, lo).astype(BF)
        vo_ref[...] = v_ref[...].astype(BF)

    return _rowwise(body, name="qk_prep", nrows=S, tile=ROW_TILE,
                    row_ins=[(proj, D_ATTN, 0), (proj, D_ATTN, 1), (proj, D_ATTN, 2), (pos_col, 1, 0)],
                    full_ins=[gq128, gk128, inv_tab], row_outs=[(D_ATTN, BF)] * 3)


ATTN_QB = 4


def _band(b, nb):
    width = BLK if nb == 1 else 2 * BLK
    start = b * BLK if nb == 1 else jnp.maximum(b - 1, 0) * BLK
    qg = b * BLK + lax.broadcasted_iota(jnp.int32, (BLK, width), 0)
    kg = start + lax.broadcasted_iota(jnp.int32, (BLK, width), 1)
    valid = kg <= qg
    if nb > 1:
        valid = valid & (kg >= qg - BLK) & (kg >= (b // nb) * (nb * BLK))
    return pl.ds(pl.multiple_of(start, BLK), width), valid


def _attn_fwd(q, k, v, nb, name):
    def kern(q_ref, k_ref, v_ref, o_ref, l_ref):
        half0 = lax.broadcasted_iota(jnp.int32, (BLK, LANES), 1) < HEAD
        for bb in range(ATTN_QB):
            rows = slice(bb * BLK, (bb + 1) * BLK)
            band, valid = _band(pl.program_id(1) * ATTN_QB + bb, nb)
            q = q_ref[rows, :].astype(F32)
            kb, vb = k_ref[band, :], v_ref[band, :]
            o_h, l_h = [], []
            for h in range(2):
                hm = half0 if h == 0 else jnp.logical_not(half0)
                s = jnp.where(valid, _dot(jnp.where(hm, q, 0.0).astype(BF), kb, NT), NEG)
                m = jnp.max(s, axis=-1, keepdims=True)
                p = jnp.exp(s - m)
                den = jnp.sum(p, axis=-1, keepdims=True)
                o_h.append(_dot(p.astype(BF), vb, NN) / den)
                l_h.append(m + jnp.log(den))
            o_ref[rows, :] = jnp.where(half0, o_h[0], o_h[1])
            l_ref[rows, :] = jnp.where(half0, l_h[0], l_h[1])

    tile = pl.BlockSpec((ATTN_QB * BLK, LANES), lambda p, b: (b, p))
    seq = pl.BlockSpec((S, LANES), lambda p, b: (0, p))
    return pl.pallas_call(
        kern, name=name, grid=(D_ATTN // LANES, NB // ATTN_QB), in_specs=[tile, seq, seq], out_specs=[tile, tile],
        out_shape=[jax.ShapeDtypeStruct((S, D_ATTN), F32)] * 2,
        compiler_params=_params("parallel", "arbitrary"))(q, k, v)


def _attn_merge(os_, ls_, g_attn):
    def body(i, n, o1, o2, o3, l1, l2, l3, g_ref, a_ref, lse_ref, an_ref):
        la, lb, lc = l1[...], l2[...], l3[...]
        m = jnp.maximum(jnp.maximum(la, lb), lc)
        ea, eb, ec = jnp.exp(la - m), jnp.exp(lb - m), jnp.exp(lc - m)
        den = ea + eb + ec
        attn = (ea * o1[...] + eb * o2[...] + ec * o3[...]) / den
        a_ref[...] = attn
        lse_ref[...] = m + jnp.log(den)
        nx, _ = _rms_n(attn)
        an_ref[...] = (nx * g_ref[...]).astype(BF)

    return _rowwise(body, name="attn_merge", nrows=S, tile=ROW_TILE,
                    row_ins=[(a, D_ATTN, 0) for a in (*os_, *ls_)], full_ins=[g_attn],
                    row_outs=[(D_ATTN, F32), (D_ATTN, F32), (D_ATTN, BF)])


def _conv_taps(x, w_ref):
    rows = lax.broadcasted_iota(jnp.int32, x.shape, 0)
    shifted = []
    for w in range(CONV_W):
        k = CONV_W - 1 - w
        shifted.append(x if k == 0 else jnp.where(rows >= k, pltpu.roll(x, k, 0), 0.0))
    acc = shifted[0] * w_ref[0:1, :]
    for w in range(1, CONV_W):
        acc = acc + shifted[w] * w_ref[w:w + 1, :]
    return acc, shifted


def _conv_fwd(proj, conv_w, conv_b):
    tc = 256

    def kern(x_ref, w_ref, b_ref, o_ref):
        c, _ = _conv_taps(x_ref[...], w_ref)
        c = c + b_ref[...]
        o_ref[...] = c * _sigmoid(c)

    return pl.pallas_call(
        kern, name="conv_fwd", grid=(D_CONV // tc,),
        in_specs=[pl.BlockSpec((S, tc), lambda c: (0, 4 * D_ATTN // tc + c)),
                  pl.BlockSpec((CONV_W, tc), lambda c: (0, c)), pl.BlockSpec((1, tc), lambda c: (0, c))],
        out_specs=pl.BlockSpec((S, tc), lambda c: (0, c)),
        out_shape=jax.ShapeDtypeStruct((S, D_CONV), F32), compiler_params=_params("parallel"))(proj, conv_w, conv_b)


def _softplus(x):
    return jnp.maximum(x, 0.0) + jnp.log1p(jnp.exp(-jnp.abs(x)))


def _ssd_prep(proj, dt_bias128, a_log128):
    def body(i, n, raw_ref, b_ref, al_ref, dt_ref, a_ref, carry):
        @pl.when(i == 0)
        def _():
            carry[...] = jnp.zeros_like(carry)

        dt = _softplus(raw_ref[...] + b_ref[...])
        da = dt * (-jnp.exp(al_ref[...]))
        tri = (lax.broadcasted_iota(jnp.int32, (BLK, BLK), 0) >= lax.broadcasted_iota(jnp.int32, (BLK, BLK), 1))
        cs = _dot3(tri.astype(BF), da) + carry[0:1, :]
        dt_ref[...] = dt
        a_ref[...] = cs
        carry[...] = jnp.broadcast_to(cs[BLK - 1:BLK, :], carry.shape)

    return _rowwise(body, name="ssd_prep", nrows=S, tile=BLK, row_ins=[(proj, LANES, (D_INP - LANES) // LANES)],
                    full_ins=[dt_bias128, a_log128], row_outs=[(LANES, F32), (LANES, F32)],
                    scratch=[pltpu.VMEM((8, LANES), F32)])


def _ssd_decay(a_col, at_row, causal):
    diff = jnp.where(causal, a_col - at_row, 0.0)
    return jnp.where(causal, jnp.exp(diff), 0.0)


def _ssd_fwd(xbc, a_b, dt_b, at_r, dt_r):
    def kern(c_ref, b_ref, x_ref, ab_ref, dtb_ref, at_ref, dt_ref, y_ref, hall_ref, h_scr, aprev_scr):
        i = pl.program_id(1)

        @pl.when(i == 0)
        def _():
            h_scr[...] = jnp.zeros_like(h_scr)
            aprev_scr[...] = jnp.zeros_like(aprev_scr)

        ci, bi, x = c_ref[...].astype(BF), b_ref[...].astype(BF), x_ref[...]
        ab = ab_ref[...]
        a_end = ab[BLK - 1:BLK, :]
        alpha = jnp.exp(ab - aprev_scr[0:1, :])
        beta = jnp.exp(a_end - ab) * dtb_ref[...]
        h = h_scr[...]
        hall_ref[...] = h
        half0 = lax.broadcasted_iota(jnp.int32, (BLK, LANES), 1) < HEAD
        causal = lax.broadcasted_iota(jnp.int32, (BLK, BLK), 1) <= lax.broadcasted_iota(jnp.int32, (BLK, BLK), 0)
        cb = _dot(ci, bi, NT)
        at, dtj = at_ref[...], dt_ref[...]
        y = alpha * _dot(ci, h.astype(BF), NN)
        for hd in range(2):
            hm = half0 if hd == 0 else jnp.logical_not(half0)
            w = cb * _ssd_decay(ab[:, HEAD * hd:HEAD * hd + 1], at[hd:hd + 1, :], causal) * dtj[hd:hd + 1, :]
            y = y + _dot(w.astype(BF), jnp.where(hm, x, 0.0).astype(BF), NN)
        y_ref[...] = y
        h_scr[...] = alpha[BLK - 1:BLK, :] * h + _dot(bi, (beta * x).astype(BF), TN)
        aprev_scr[...] = jnp.broadcast_to(a_end, aprev_scr.shape)

    npair = D_SSM // LANES
    rowvec = pl.BlockSpec((None, None, 2, BLK), lambda p, i: (p, i, 0, 0))
    tile = pl.BlockSpec((BLK, LANES), lambda p, i: (i, p))
    return pl.pallas_call(
        kern, name="ssd_fwd", grid=(npair, NB),
        in_specs=[pl.BlockSpec((BLK, LANES), lambda p, i: (i, 12 + p // 2)),
                  pl.BlockSpec((BLK, LANES), lambda p, i: (i, 8 + p // 2)), tile, tile, tile, rowvec, rowvec],
        out_specs=[tile, tile], out_shape=[jax.ShapeDtypeStruct((S, D_SSM), F32)] * 2,
        scratch_shapes=[pltpu.VMEM((BLK, LANES), F32), pltpu.VMEM((8, LANES), F32)],
        compiler_params=_params("parallel", "arbitrary"))(xbc, xbc, xbc, a_b, dt_b, at_r, dt_r)


def _ssd_post(y_ssd, xbc, proj, dskip_b, g_ssm):
    def body(i, n, y_ref, xs_ref, z_ref, d_ref, g_ref, o_ref):
        z = z_ref[...]
        y2 = (y_ref[...] + d_ref[...] * xs_ref[...]) * (z * _sigmoid(z))
        nx, _ = _seg_rms_n(y2, 256)
        o_ref[...] = (nx * g_ref[...]).astype(BF)

    return _rowwise(body, name="ssd_post", nrows=S, tile=ROW_TILE,
                    row_ins=[(y_ssd, D_SSM, 0), (xbc, D_SSM, 0), (proj, D_SSM, 3)], full_ins=[dskip_b, g_ssm],
                    row_outs=[(D_SSM, BF)])[0]


def _cross_heads(q, kv_ref, gq, gk):
    out = []
    for h in range(D_CROSS // CROSS_HEAD):
        sl = slice(CROSS_HEAD * h, CROSS_HEAD * (h + 1))
        nq, rq = _rms_n(q[:, sl])
        nk, rk = _rms_n(kv_ref[:, sl])
        v = kv_ref[:, D_CROSS + CROSS_HEAD * h:D_CROSS + CROSS_HEAD * (h + 1)]
        out.append((sl, nq, rq, nk, rk, v))
    return out


def _cross_fwd(qc, kv, g_cq, g_ck):
    scale = CROSS_HEAD ** -0.5

    def body(i, n, q_ref, kv_ref, gq_ref, gk_ref, o_ref):
        for sl, nq, _, nk, _, v in _cross_heads(q_ref[...], kv_ref, gq_ref[...], gk_ref[...]):
            qn = (nq * gq_ref[...] * scale).astype(BF)
            kn = (nk * gk_ref[...]).astype(BF)
            s = _dot(qn, kn, NT)
            e = jnp.exp(s - jnp.max(s, axis=-1, keepdims=True))
            p = e / jnp.sum(e, axis=-1, keepdims=True)
            o_ref[:, sl] = _dot(p.astype(BF), v.astype(BF), NN).astype(BF)

    return _rowwise(body, name="cross_fwd", nrows=S, tile=ROW_TILE, row_ins=[(qc, D_CROSS, 0)],
                    full_ins=[kv, g_cq, g_ck], row_outs=[(D_CROSS, BF)])[0]


def _loss_sum(dy):
    def body(i, n, d_ref, o_ref):
        d = d_ref[...]
        s = jnp.sum(jnp.sum(d * d, axis=0, keepdims=True), axis=1, keepdims=True)
        _accum(o_ref, s, i)

    return _rowwise(body, name="loss_sum", nrows=S, tile=ROW_TILE, row_ins=[(dy, D, 0)],
                    acc_outs=[((1, 1), F32)])[0]


def _rms_bwd_call(x, g, dh, dres, name):
    rows = x.shape[0]
    has_res = dres is not None

    def body(i, n, *refs):
        if has_res:
            x_ref, dh_ref, dr_ref, g_ref, dx_ref, dxb_ref, dg_ref = refs
        else:
            x_ref, dh_ref, g_ref, dg_ref = refs
        dx, dg = _rms_bwd(x_ref[...], g_ref[...], dh_ref[...])
        if has_res:
            dx = dx + dr_ref[...]
            dx_ref[...] = dx
            dxb_ref[...] = dx.astype(BF)
        _accum(dg_ref, dg, i)

    row_ins = [(x, D, 0), (dh, D, 0)] + ([(dres, D, 0)] if has_res else [])
    return _rowwise(body, name=name, nrows=rows, tile=min(ROW_TILE, rows), row_ins=row_ins, full_ins=[g],
                    row_outs=[(D, F32), (D, BF)] if has_res else [], acc_outs=[((1, D), F32)])


def _cross_bwd(qc, doc, kv, g_cq, g_ck):
    scale = CROSS_HEAD ** -0.5

    def body(i, n, q_ref, do_ref, kv_ref, gq_ref, gk_ref, dq_ref, dkn_ref, dv_ref, dgq_ref):
        dgq = jnp.zeros((1, CROSS_HEAD), F32)
        dkn_parts, dv_parts = [], []
        for sl, nq, rq, nk, _, v in _cross_heads(q_ref[...], kv_ref, gq_ref[...], gk_ref[...]):
            qn = (nq * gq_ref[...] * scale).astype(BF)
            kn = (nk * gk_ref[...]).astype(BF)
            s = _dot(qn, kn, NT)
            e = jnp.exp(s - jnp.max(s, axis=-1, keepdims=True))
            p = e / jnp.sum(e, axis=-1, keepdims=True)
            do = do_ref[:, sl].astype(BF)
            dv_parts.append(_dot(p.astype(BF), do, TN))
            dp = _dot(do, v.astype(BF), NT)
            ds = (p * (dp - jnp.sum(dp * p, axis=-1, keepdims=True))).astype(BF)
            dqn = _dot(ds, kn, NN)
            dkn_parts.append(_dot(ds, qn, TN))
            dn = dqn * (gq_ref[...] * scale)
            dq_ref[:, sl] = (rq * (dn - nq * jnp.mean(dn * nq, axis=-1, keepdims=True))).astype(BF)
            dgq = dgq + jnp.sum(dqn * scale * nq, axis=0, keepdims=True)
        _accum(dkn_ref, jnp.concatenate(dkn_parts, axis=1), i)
        _accum(dv_ref, jnp.concatenate(dv_parts, axis=1), i)
        _accum(dgq_ref, dgq, i)

    return _rowwise(body, name="cross_bwd", nrows=S, tile=ROW_TILE, row_ins=[(qc, D_CROSS, 0), (doc, D_CROSS, 0)],
                    full_ins=[kv, g_cq, g_ck], row_outs=[(D_CROSS, BF)],
                    acc_outs=[((N_MEM, D_CROSS), F32), ((N_MEM, D_CROSS), F32), ((1, CROSS_HEAD), F32)])


def _cross_kv_bwd(kv, dkn, dv, g_ck):
    def body(i, n, kv_ref, dkn_ref, dv_ref, gk_ref, dkv_ref, dgk_ref):
        dgk = jnp.zeros((1, CROSS_HEAD), F32)
        for h in range(D_CROSS // CROSS_HEAD):
            sl = slice(CROSS_HEAD * h, CROSS_HEAD * (h + 1))
            dk, dg = _rms_bwd(kv_ref[:, sl], gk_ref[...], dkn_ref[:, sl])
            dkv_ref[:, sl] = dk.astype(BF)
            dgk = dgk + dg
        dkv_ref[:, D_CROSS:] = dv_ref[...].astype(BF)
        dgk_ref[...] = dgk

    return _rowwise(body, name="cross_kv_bwd", nrows=N_MEM, tile=N_MEM,
                    row_ins=[(kv, 2 * D_CROSS, 0), (dkn, D_CROSS, 0), (dv, D_CROSS, 0)], full_ins=[g_ck],
                    row_outs=[(2 * D_CROSS, BF)], acc_outs=[((1, CROSS_HEAD), F32)])


def _attn_merge_bwd(dmix, attn, g_attn):
    def body(i, n, dn_ref, a_ref, g_ref, da_ref, dl_ref, dg_ref):
        attn_v = a_ref[...]
        da, dg = _rms_bwd(attn_v, g_ref[...], dn_ref[...])
        da_ref[...] = da.astype(BF)
        dl_ref[...] = _seg_sum(da * attn_v, HEAD)
        _accum(dg_ref, dg, i)

    return _rowwise(body, name="attn_merge_bwd", nrows=S, tile=ROW_TILE,
                    row_ins=[(dmix, D_ATTN, 0), (attn, D_ATTN, 0)], full_ins=[g_attn],
                    row_outs=[(D_ATTN, BF), (D_ATTN, F32)], acc_outs=[((1, D_ATTN), F32)])


def _attn_bwd(q, k, v, do, lse, delta, nb, name):
    def kern(q_ref, k_ref, v_ref, do_ref, l_ref, d_ref, dq_ref, dk_ref, dv_ref):
        @pl.when(pl.program_id(1) == 0)
        def _():
            dk_ref[...] = jnp.zeros_like(dk_ref)
            dv_ref[...] = jnp.zeros_like(dv_ref)

        half0 = lax.broadcasted_iota(jnp.int32, (BLK, LANES), 1) < HEAD
        updates = []
        for bb in range(ATTN_QB):
            rows = slice(bb * BLK, (bb + 1) * BLK)
            band, valid = _band(pl.program_id(1) * ATTN_QB + bb, nb)
            q, do = q_ref[rows, :].astype(F32), do_ref[rows, :].astype(F32)
            kb, vb = k_ref[band, :], v_ref[band, :]
            dq_h, dk, dv = [], None, None
            for h in range(2):
                hm = half0 if h == 0 else jnp.logical_not(half0)
                qm = jnp.where(hm, q, 0.0).astype(BF)
                dom = jnp.where(hm, do, 0.0).astype(BF)
                s = jnp.where(valid, _dot(qm, kb, NT), NEG)
                p = jnp.exp(s - l_ref[rows, HEAD * h:HEAD * h + 1])
                ds = (p * (_dot(dom, vb, NT) - d_ref[rows, HEAD * h:HEAD * h + 1])).astype(BF)
                dq_h.append(_dot(ds, kb, NN))
                dk_h, dv_h = _dot(ds, qm, TN), _dot(p.astype(BF), dom, TN)
                dk, dv = (dk_h, dv_h) if h == 0 else (dk + dk_h, dv + dv_h)
            dq_ref[rows, :] = jnp.where(half0, dq_h[0], dq_h[1])
            updates.append((band, dk, dv))
        for band, dk, dv in updates:
            dk_ref[band, :] += dk
            dv_ref[band, :] += dv

    tile = pl.BlockSpec((ATTN_QB * BLK, LANES), lambda p, b: (b, p))
    seq = pl.BlockSpec((S, LANES), lambda p, b: (0, p))
    return pl.pallas_call(
        kern, name=name, grid=(D_ATTN // LANES, NB // ATTN_QB), in_specs=[tile, seq, seq, tile, tile, tile],
        out_specs=[tile, seq, seq], out_shape=[jax.ShapeDtypeStruct((S, D_ATTN), F32)] * 3,
        compiler_params=_params("parallel", "arbitrary"))(q, k, v, do, lse, delta)


def _qk_bwd(dqs, dks, dvs, proj, pos_col, gq128, gk128, inv_tab):
    scale = HEAD ** -0.5

    def body(i, n, *refs):
        dq_refs, dk_refs, dv_refs = refs[0:3], refs[3:6], refs[6:9]
        q_ref, k_ref, p_ref, gq_ref, gk_ref, it_ref = refs[9:15]
        dqo_ref, dko_ref, dvo_ref, dgq_ref, dgk_ref = refs[15:20]
        t = q_ref.shape[0]
        c, sg, lo = _rope_tables(p_ref[...], it_ref[...], t)
        dgq = jnp.zeros((1, LANES), F32)
        dgk = jnp.zeros((1, LANES), F32)
        for j in range(D_ATTN // LANES):
            sl = slice(LANES * j, LANES * (j + 1))
            dqr = _rope(dq_refs[0][:, sl] + dq_refs[1][:, sl] + dq_refs[2][:, sl], c, -sg, lo) * scale
            dkr = _rope(dk_refs[0][:, sl] + dk_refs[1][:, sl] + dk_refs[2][:, sl], c, -sg, lo)
            dq, gq = _seg_rms_bwd(q_ref[:, sl], gq_ref[...], dqr, HEAD)
            dk, gk = _seg_rms_bwd(k_ref[:, sl], gk_ref[...], dkr, HEAD)
            dqo_ref[:, sl] = dq.astype(BF)
            dko_ref[:, sl] = dk.astype(BF)
            dgq, dgk = dgq + gq, dgk + gk
        dvo_ref[...] = (dv_refs[0][...] + dv_refs[1][...] + dv_refs[2][...]).astype(BF)
        _accum(dgq_ref, _fold_heads(dgq), i)
        _accum(dgk_ref, _fold_heads(dgk), i)

    return _rowwise(body, name="qk_bwd", nrows=S, tile=ROW_TILE,
                    row_ins=[(a, D_ATTN, 0) for a in (*dqs, *dks, *dvs)]
                    + [(proj, D_ATTN, 0), (proj, D_ATTN, 1), (pos_col, 1, 0)],
                    full_ins=[gq128, gk128, inv_tab], row_outs=[(D_ATTN, BF)] * 3,
                    acc_outs=[((1, LANES), F32)] * 2)


def _ssd_post_bwd(y_ssd, xbc, proj, dmix, dskip_b, g_ssm):
    def body(i, n, y_ref, xs_ref, z_ref, do_ref, d_ref, g_ref, dy_ref, dz_ref, dxs_ref, dd_ref, dg_ref):
        z, xs = z_ref[...], xs_ref[...]
        sg = _sigmoid(z)
        gate = z * sg
        y = y_ref[...] + d_ref[...] * xs
        dy2, dg = _seg_rms_bwd(y * gate, g_ref[...], do_ref[...], 256)
        dy = dy2 * gate
        dy_ref[...] = dy.astype(BF)
        dz_ref[...] = (dy2 * y * (sg * (1.0 + z * (1.0 - sg)))).astype(BF)
        dxs_ref[...] = dy * d_ref[...]
        _accum(dd_ref, jnp.sum(dy * xs, axis=0, keepdims=True), i)
        _accum(dg_ref, dg, i)

    return _rowwise(body, name="ssd_post_bwd", nrows=S, tile=ROW_TILE,
                    row_ins=[(y_ssd, D_SSM, 0), (xbc, D_SSM, 0), (proj, D_SSM, 3), (dmix, D_SSM, 1)],
                    full_ins=[dskip_b, g_ssm], row_outs=[(D_SSM, BF), (D_SSM, BF), (D_SSM, F32)],
                    acc_outs=[((1, D_SSM), F32), ((1, D_SSM), F32)])


def _ssd_bwd(xbc, dy, h_all, a_b, dt_b, at_r, dt_r):
    def kern(c_ref, b_ref, x_ref, dy_ref, hall_ref, ab_ref, abp_ref, dtb_ref, at_ref, dt_ref,
             dc_ref, daq_ref, dx_ref, db_ref, dak_ref, ddt_ref, ddtb_ref, dh_scr, carry_scr):
        i = pl.program_id(1)

        @pl.when(i == 0)
        def _():
            dh_scr[...] = jnp.zeros_like(dh_scr)
            carry_scr[...] = jnp.zeros_like(carry_scr)

        ci, bi, x = c_ref[...].astype(BF), b_ref[...].astype(BF), x_ref[...]
        dyv = dy_ref[...].astype(F32)
        ab, dtb = ab_ref[...], dtb_ref[...]
        a_prev = jnp.where(i == NB - 1, 0.0, abp_ref[BLK - 1:BLK, :])
        a_end = ab[BLK - 1:BLK, :]
        alpha = jnp.exp(ab - a_prev)
        e_end = jnp.exp(a_end - ab)
        beta = e_end * dtb
        t_all = alpha[BLK - 1:BLK, :]
        hc = hall_ref[...]
        hcb = hc.astype(BF)
        dh_next = dh_scr[...]
        dhb = dh_next.astype(BF)
        half0 = lax.broadcasted_iota(jnp.int32, (BLK, LANES), 1) < HEAD
        hms = (half0, jnp.logical_not(half0))
        causal = lax.broadcasted_iota(jnp.int32, (BLK, BLK), 1) <= lax.broadcasted_iota(jnp.int32, (BLK, BLK), 0)
        row = lax.broadcasted_iota(jnp.int32, (BLK, LANES), 0)

        cb = _dot(ci, bi, NT)
        at, dtj = at_ref[...], dt_ref[...]
        dcb = jnp.zeros((BLK, BLK), F32)
        dx = jnp.zeros((BLK, LANES), F32)
        rows = []
        for hd in range(2):
            dym = jnp.where(hms[hd], dyv, 0.0).astype(BF)
            lm = _ssd_decay(ab[:, HEAD * hd:HEAD * hd + 1], at[hd:hd + 1, :], causal)
            dth = dtj[hd:hd + 1, :]
            dw = _dot(dym, jnp.where(hms[hd], x, 0.0).astype(BF), NT)
            g = dw * cb * lm
            dx = dx + _dot((cb * lm * dth).astype(BF), dym, TN)
            gcol = jnp.sum(g, axis=0, keepdims=True)
            ddt_ref[hd:hd + 1, :] = gcol
            dak_ref[hd:hd + 1, :] = gcol * dth
            rows.append(jnp.sum(g * dth, axis=1, keepdims=True))
            dcb = dcb + dw * lm * dth
        dcb = dcb.astype(BF)

        g1 = (alpha * dyv).astype(BF)
        dalpha = dyv * _dot(ci, hcb, NN)
        g2 = _dot(bi, dhb, NN)
        dbeta = x * g2
        bx = (beta * x).astype(BF)
        dc_ref[...] = _dot(dcb, bi, NN) + _dot(g1, hcb, NT)
        db_ref[...] = _dot(dcb, ci, TN) + _dot(bx, dhb, NT)
        dx_ref[...] = dx + beta * g2
        ddtb_ref[...] = _seg_sum(e_end * dbeta, HEAD)
        ada, bdb = alpha * dalpha, beta * dbeta
        t_dt = t_all * jnp.sum(dh_next * hc, axis=0, keepdims=True)
        end_term = _seg_sum(jnp.broadcast_to(jnp.sum(bdb, axis=0, keepdims=True) + t_dt, (8, LANES)), HEAD)[0:1]
        prev_term = _seg_sum(jnp.broadcast_to(jnp.sum(ada, axis=0, keepdims=True) + t_dt, (8, LANES)), HEAD)[0:1]
        daq = jnp.where(half0, rows[0], rows[1]) + _seg_sum(ada - bdb, HEAD)
        daq_ref[...] = daq + jnp.where(row == BLK - 1, end_term - carry_scr[0:1, :], 0.0)
        carry_scr[...] = jnp.broadcast_to(prev_term, carry_scr.shape)
        dh_scr[...] = t_all * dh_next + _dot(ci, g1, TN)

    npair = D_SSM // LANES
    rev = lambda i: NB - 1 - i
    rowvec = pl.BlockSpec((None, None, 2, BLK), lambda p, i: (p, rev(i), 0, 0))
    tile = pl.BlockSpec((BLK, LANES), lambda p, i: (rev(i), p))
    prev_tile = pl.BlockSpec((BLK, LANES), lambda p, i: (jnp.maximum(rev(i) - 1, 0), p))
    return pl.pallas_call(
        kern, name="ssd_bwd", grid=(npair, NB),
        in_specs=[pl.BlockSpec((BLK, LANES), lambda p, i: (rev(i), 12 + p // 2)),
                  pl.BlockSpec((BLK, LANES), lambda p, i: (rev(i), 8 + p // 2)),
                  tile, tile, tile, tile, prev_tile, tile, rowvec, rowvec],
        out_specs=[tile, tile, tile, tile, rowvec, rowvec, tile],
        out_shape=[jax.ShapeDtypeStruct((S, D_SSM), F32)] * 4
        + [jax.ShapeDtypeStruct((npair, NB, 2, BLK), F32)] * 2 + [jax.ShapeDtypeStruct((S, D_SSM), F32)],
        scratch_shapes=[pltpu.VMEM((BLK, LANES), F32), pltpu.VMEM((8, LANES), F32)],
        compiler_params=_params("parallel", "arbitrary"))(xbc, xbc, xbc, dy, h_all, a_b, a_b, dt_b, at_r, dt_r)


def _ssd_prep_bwd(daq, dak, ddt_row, ddt_lane, dt, proj, dt_bias128, a_log128):
    def body(i, n, daq_ref, dak_ref, dd_ref, dd2_ref, dt_ref, raw_ref, b_ref, al_ref, draw_ref, dal_ref, db_ref,
             carry):
        @pl.when(i == 0)
        def _():
            carry[...] = jnp.zeros_like(carry)

        a = -jnp.exp(al_ref[...])
        tri = (lax.broadcasted_iota(jnp.int32, (BLK, BLK), 0) <= lax.broadcasted_iota(jnp.int32, (BLK, BLK), 1))
        rev = _dot3(tri.astype(BF), daq_ref[...] - dak_ref[...]) + carry[0:1, :]
        carry[...] = jnp.broadcast_to(rev[0:1, :], carry.shape)
        dtv = dt_ref[...]
        draw = (dd_ref[...] + dd2_ref[...] + a * rev) * _sigmoid(raw_ref[...] + b_ref[...])
        draw_ref[...] = draw.astype(BF)
        _accum(dal_ref, jnp.sum(dtv * rev, axis=0, keepdims=True) * a, i)
        _accum(db_ref, jnp.sum(draw, axis=0, keepdims=True), i)

    return _rowwise(body, name="ssd_prep_bwd", nrows=S, tile=BLK, reverse=True,
                    row_ins=[(daq, LANES, 0), (dak, LANES, 0), (ddt_row, LANES, 0), (ddt_lane, LANES, 0), (dt, LANES, 0),
                             (proj, LANES, (D_INP - LANES) // LANES)],
                    full_ins=[dt_bias128, a_log128], row_outs=[(LANES, BF)],
                    acc_outs=[((1, LANES), F32), ((1, LANES), F32)], scratch=[pltpu.VMEM((8, LANES), F32)])


def _conv_bwd(proj, conv_w, conv_b, d1, d2, map1, map2, col0, ntiles, name):
    base = (4 * D_ATTN + col0) // LANES

    def kern(x_ref, w_ref, b_ref, d1_ref, d2_ref, dx_ref, dw_ref, db_ref):
        x = x_ref[...]
        c, shifted = _conv_taps(x, w_ref)
        c = c + b_ref[...]
        sg = _sigmoid(c)
        dc = (d1_ref[...] + d2_ref[...]) * (sg * (1.0 + c * (1.0 - sg)))
        rows = lax.broadcasted_iota(jnp.int32, x.shape, 0)
        dx = jnp.zeros_like(x)
        for w in range(CONV_W):
            k = CONV_W - 1 - w
            up = dc if k == 0 else jnp.where(rows < S - k, pltpu.roll(dc, S - k, 0), 0.0)
            dx = dx + up * w_ref[w:w + 1, :]
            dw_ref[w:w + 1, :] = jnp.sum(dc * shifted[w], axis=0, keepdims=True)
        dx_ref[...] = dx.astype(BF)
        db_ref[...] = jnp.sum(dc, axis=0, keepdims=True)

    c0 = col0 // LANES
    return pl.pallas_call(
        kern, name=name, grid=(ntiles,),
        in_specs=[pl.BlockSpec((S, LANES), lambda c: (0, base + c)),
                  pl.BlockSpec((CONV_W, LANES), lambda c: (0, c0 + c)),
                  pl.BlockSpec((1, LANES), lambda c: (0, c0 + c)),
                  pl.BlockSpec((S, LANES), lambda c: (0, map1(c))),
                  pl.BlockSpec((S, LANES), lambda c: (0, map2(c)))],
        out_specs=[pl.BlockSpec((S, LANES), lambda c: (0, c)), pl.BlockSpec((CONV_W, LANES), lambda c: (0, c)),
                   pl.BlockSpec((1, LANES), lambda c: (0, c))],
        out_shape=[jax.ShapeDtypeStruct((S, ntiles * LANES), BF), jax.ShapeDtypeStruct((CONV_W, ntiles * LANES), F32),
                   jax.ShapeDtypeStruct((1, ntiles * LANES), F32)],
        compiler_params=_params("parallel"))(proj, conv_w, conv_b, d1, d2)


def _adamw(w, m, v, parts, name):
    r, c = w.shape
    n_parts = parts.shape[0]
    tile = r if r <= 512 else (128 if c > 1024 else 256)
    assert r % tile == 0
    c1 = 1.0 - ADAM_B1 ** ADAM_STEP
    c2 = 1.0 - ADAM_B2 ** ADAM_STEP

    def kern(w_ref, m_ref, v_ref, p_ref, g_ref, d_ref, mo_ref, vo_ref):
        g = p_ref[0].astype(F32)
        for k in range(1, n_parts):
            g = g + p_ref[k].astype(F32)
        mn = ADAM_B1 * m_ref[...] + (1.0 - ADAM_B1) * g
        vn = ADAM_B2 * v_ref[...] + (1.0 - ADAM_B2) * (g * g)
        g_ref[...] = g
        mo_ref[...] = mn
        vo_ref[...] = vn
        d_ref[...] = -ADAM_LR * ((mn / c1) / (jnp.sqrt(vn / c2) + ADAM_EPS) + ADAM_WD * w_ref[...])

    blk = pl.BlockSpec((tile, c), lambda i: (i, 0))
    return pl.pallas_call(
        kern, name=name, grid=(r // tile,),
        in_specs=[blk, blk, blk, pl.BlockSpec((n_parts, tile, c), lambda i: (0, i, 0))],
        out_specs=[blk] * 4, out_shape=[jax.ShapeDtypeStruct((r, c), F32)] * 4,
        compiler_params=_params("parallel"))(w, m, v, parts)


MESH = pl.DeviceIdType.MESH
ANY = pl.BlockSpec(memory_space=pl.ANY)


def _put_own(gathered, shard):
    me = 4 * lax.axis_index("x") + 2 * lax.axis_index("y") + lax.axis_index("c")
    return lax.dynamic_update_slice(gathered, shard[None], (me,) + (0,) * shard.ndim)


def _all_gather(arrs, name, after=None):
    na = len(arrs)
    n_after = 0 if after is None else 1

    def kern(*refs):
        ins, outs = refs[:na], refs[na + n_after:2 * na + n_after]
        send_sems, recv_sems = refs[2 * na + n_after:]
        x, y, c = lax.axis_index("x"), lax.axis_index("y"), lax.axis_index("c")
        me, sibling = (x, y, c), (x, y, 1 - c)
        chips = [(1 - x, y), (x, 1 - y), (1 - x, 1 - y)]

        def copy(a, k, block, to, src=None):
            px, py, pc = block
            dst = outs[a].at[4 * px + 2 * py + pc]
            return pltpu.make_async_remote_copy(
                src_ref=dst if src is None else src, dst_ref=dst, send_sem=send_sems.at[a, k],
                recv_sem=recv_sems.at[a, k], device_id=to, device_id_type=MESH)

        first = []
        for a in range(na):
            first.append(copy(a, 0, me, sibling, src=ins[a]))
            first += [copy(a, 1 + j, me, (*chip, c), src=ins[a]) for j, chip in enumerate(chips)]
        for cp in first:
            cp.start()
        passed = []
        for a in range(na):
            for j, chip in enumerate(chips):
                copy(a, 1 + j, (*chip, c), me).wait_recv()
                fwd = copy(a, 4 + j, (*chip, c), sibling)
                fwd.start()
                passed.append(fwd)
        for a in range(na):
            copy(a, 0, sibling, me).wait_recv()
            for j, chip in enumerate(chips):
                copy(a, 4 + j, (*chip, 1 - c), me).wait_recv()
        for cp in first + passed:
            cp.wait_send()

    outs = pl.pallas_call(
        kern, name=name, in_specs=[ANY] * (na + n_after), out_specs=[ANY] * na,
        out_shape=[jax.ShapeDtypeStruct((N_DEV,) + a.shape, a.dtype) for a in arrs],
        scratch_shapes=[pltpu.SemaphoreType.DMA((na, 7)), pltpu.SemaphoreType.DMA((na, 7))],
    )(*arrs, *([] if after is None else [after]))
    return [_put_own(o, a) for o, a in zip(outs, arrs)]


HBM = pl.BlockSpec(memory_space=pltpu.HBM)
SEM = pl.BlockSpec(memory_space=pltpu.SEMAPHORE)
EFFECT = pltpu.SideEffectType.DATAFLOW_SIDE_EFFECTING
N_CHIP = 4


def _in_hbm(a):
    return pltpu.with_memory_space_constraint(a, pltpu.HBM)


def _chip_copies(kind, src_refs, land_refs, send_sems, recv_sems):
    x, y, c = lax.axis_index("x"), lax.axis_index("y"), lax.axis_index("c")
    cps = []
    for a in range(len(src_refs)):
        for j, (px, py) in enumerate([(1 - x, y), (x, 1 - y), (1 - x, 1 - y)]):
            if kind == "gather":
                src, dst = src_refs[a], land_refs[a].at[4 * x + 2 * y + c]
            else:
                src, dst = src_refs[a].at[2 * px + py], land_refs[a].at[2 * x + y]
            cps.append(pltpu.make_async_remote_copy(
                src_ref=src, dst_ref=dst, send_sem=send_sems.at[3 * a + j], recv_sem=recv_sems.at[3 * a + j],
                device_id=(px, py, c), device_id_type=MESH))
    return cps


def _exchange_start(kind, srcs, lands, after, name):
    na = len(srcs)
    n_after = 0 if after is None else 1

    def kern(*refs):
        src_refs, land_refs = refs[:na], refs[na:2 * na]
        send_sems, recv_sems = refs[2 * na + n_after], refs[2 * na + n_after + 1]
        token = refs[-1]
        for cp in _chip_copies(kind, src_refs, land_refs, send_sems, recv_sems):
            cp.start()
        token[...] = jnp.zeros_like(token)

    bufs = list(srcs) + list(lands)
    res = pl.pallas_call(
        kern, name=name,
        out_shape=(pltpu.SemaphoreType.DMA((3 * na,)), pltpu.SemaphoreType.DMA((3 * na,)))
        + tuple(pltpu.HBM(b.shape, b.dtype) for b in bufs) + (jax.ShapeDtypeStruct((8, LANES), F32),),
        in_specs=[HBM] * (2 * na) + [ANY] * n_after,
        out_specs=(SEM, SEM) + (HBM,) * (2 * na) + (pl.BlockSpec(memory_space=pltpu.VMEM),),
        input_output_aliases={i: 2 + i for i in range(2 * na)},
        compiler_params=pltpu.CompilerParams(has_side_effects=EFFECT),
    )(*[_in_hbm(b) for b in bufs], *([] if after is None else [after]))
    return (res[0], res[1]), list(res[2:2 + na]), list(res[2 + na:2 + 2 * na]), res[-1]


def _exchange_wait(kind, sems, srcs, lands, after, name):
    na = len(srcs)

    def kern(*refs):
        src_refs, land_refs = refs[:na], refs[na:2 * na]
        send_sems, recv_sems = refs[2 * na], refs[2 * na + 1]
        for cp in _chip_copies(kind, src_refs, land_refs, send_sems, recv_sems):
            cp.wait_send()
            cp.wait_recv()

    bufs = list(srcs) + list(lands)
    res = pl.pallas_call(
        kern, name=name, out_shape=tuple(pltpu.HBM(b.shape, b.dtype) for b in bufs),
        in_specs=[HBM] * (2 * na) + [SEM, SEM, ANY], out_specs=(HBM,) * (2 * na),
        input_output_aliases={i: i for i in range(2 * na)},
        compiler_params=pltpu.CompilerParams(has_side_effects=EFFECT),
    )(*bufs, sems[0], sems[1], after)
    return list(res[:na]), list(res[na:])


def _gather_finish(shards, lands, name):
    na = len(shards)

    def kern(*refs):
        ins, outs = refs[:na], refs[2 * na:3 * na]
        send_sems, recv_sems = refs[3 * na:]
        x, y, c = lax.axis_index("x"), lax.axis_index("y"), lax.axis_index("c")
        chips = [(1 - x, y), (x, 1 - y), (1 - x, 1 - y)]

        def copy(a, k, slot, pc, src=None):
            dst = outs[a].at[slot + pc]
            return pltpu.make_async_remote_copy(
                src_ref=dst if src is None else src, dst_ref=dst, send_sem=send_sems.at[a, k],
                recv_sem=recv_sems.at[a, k], device_id=(x, y, 1 - c), device_id_type=MESH)

        sends = []
        for a in range(na):
            sends.append(copy(a, 0, 4 * x + 2 * y, c, src=ins[a]))
            sends += [copy(a, 1 + j, 4 * px + 2 * py, c) for j, (px, py) in enumerate(chips)]
        for cp in sends:
            cp.start()
        for a in range(na):
            copy(a, 0, 4 * x + 2 * y, 1 - c).wait_recv()
            for j, (px, py) in enumerate(chips):
                copy(a, 1 + j, 4 * px + 2 * py, 1 - c).wait_recv()
        for cp in sends:
            cp.wait_send()

    return pl.pallas_call(
        kern, name=name, in_specs=[ANY] * (2 * na), out_specs=[ANY] * na,
        out_shape=[jax.ShapeDtypeStruct(l.shape, l.dtype) for l in lands],
        input_output_aliases={na + a: a for a in range(na)},
        scratch_shapes=[pltpu.SemaphoreType.DMA((na, 4)), pltpu.SemaphoreType.DMA((na, 4))])(*shards, *lands)


def _pair_exchange(parts, name):
    na = len(parts)

    def kern(*refs):
        ins, got = refs[:na], refs[na:2 * na]
        send_sems, recv_sems = refs[2 * na:]
        x, y, c = lax.axis_index("x"), lax.axis_index("y"), lax.axis_index("c")
        sends = []
        for a in range(na):
            for q in range(N_CHIP):
                sends.append(pltpu.make_async_remote_copy(
                    src_ref=ins[a].at[2 * q + 1 - c], dst_ref=got[a].at[q], send_sem=send_sems.at[a, q],
                    recv_sem=recv_sems.at[a, q], device_id=(x, y, 1 - c), device_id_type=MESH))
        for cp in sends:
            cp.start()
        for cp in sends:
            cp.wait()

    return pl.pallas_call(
        kern, name=name, in_specs=[ANY] * na, out_specs=[ANY] * na,
        out_shape=[jax.ShapeDtypeStruct((N_CHIP,) + p.shape[1:], p.dtype) for p in parts],
        scratch_shapes=[pltpu.SemaphoreType.DMA((na, N_CHIP)), pltpu.SemaphoreType.DMA((na, N_CHIP))])(*parts)


def _pair_sum(parts, got, name):
    _, r, cdim = parts.shape
    tile = min(r, ROW_TILE)
    core = lax.axis_index("c").astype(jnp.int32).reshape(1)

    def kern(c_ref, a_ref, b_ref, q_ref, l_ref):
        s = (a_ref[...].astype(F32) + b_ref[...].astype(F32)).astype(BF)
        q_ref[...] = s
        l_ref[...] = s

    blk = pl.BlockSpec((None, tile, cdim), lambda q, i, c_ref: (q, i, 0))
    out = jax.ShapeDtypeStruct((N_CHIP, r, cdim), BF)
    return pl.pallas_call(
        kern, name=name, out_shape=[out, out],
        grid_spec=pltpu.PrefetchScalarGridSpec(
            num_scalar_prefetch=1, grid=(N_CHIP, r // tile),
            in_specs=[pl.BlockSpec((None, tile, cdim), lambda q, i, c_ref: (2 * q + c_ref[0], i, 0)), blk],
            out_specs=[blk, blk]),
        compiler_params=_params("parallel", "parallel"))(core, parts, got)


def _perm(t, d):
    return t if d == 1 else t.reshape(S // d, d, t.shape[-1]).transpose(1, 0, 2).reshape(S, t.shape[-1])


def _unperm(t, d):
    return t if d == 1 else t.reshape(d, S // d, t.shape[-1]).transpose(1, 0, 2).reshape(S, t.shape[-1])


def _pad_lanes(v, width=LANES):
    return jnp.pad(v, ((0, 0), (0, width - v.shape[1])))


def _row_layout(t16):
    return t16.T.reshape(N_HEADS // 2, 2, NB, BLK).transpose(0, 2, 1, 3)


def _row_layout_inv(r):
    return r.transpose(0, 2, 1, 3).reshape(N_HEADS, S).T


SMALL = (("g_mix", D), ("g_q", HEAD), ("g_k", HEAD), ("g_attn_out", D_ATTN), ("conv_b", D_CONV),
         ("dt_bias", 16), ("a_log", 16), ("d_skip", 16), ("g_ssm_out", D_SSM), ("g_cross", D),
         ("g_mem", D), ("g_cq", CROSS_HEAD), ("g_ck", CROSS_HEAD), ("g_mlp", D))
SMALL_ROWS = -(-sum(-(-w // LANES) for _, w in SMALL) // 8) * 8


def _pack_small(vals):
    rows = [_pad_lanes(vals[n], -(-w // LANES) * LANES).reshape(-1, LANES) for n, w in SMALL]
    packed = jnp.concatenate(rows, axis=0)
    return jnp.pad(packed, ((0, SMALL_ROWS - packed.shape[0]), (0, 0)))


def _unpack_small(packed):
    out, r = {}, 0
    for n, w in SMALL:
        nr = -(-w // LANES)
        out[n] = packed[r:r + nr].reshape(1, nr * LANES)[:, :w]
        r += nr
    return out


BIG = ("w_in", "w_out", "w_cq", "w_ckv", "w_co", "w_up", "w_down")
WEIGHTS = ("g_mix", "w_in", "g_q", "g_k", "g_attn_out", "conv_w", "conv_b", "dt_bias", "a_log", "d_skip",
           "g_ssm_out", "w_out", "g_cross", "g_mem", "w_cq", "w_ckv", "g_cq", "g_ck", "w_co", "g_mlp", "w_up", "w_down")


REST = ("w_out", "w_cq", "w_ckv", "w_co", "w_up", "w_down")


class _Overlap:
    def __init__(self, shards, after):
        lands = [_put_own(lax.empty((N_DEV,) + s.shape, s.dtype), s) for s in shards]
        self.gather = _exchange_start("gather", shards, lands, after, "ag_rest_start")
        self.token = self.gather[3]
        self.groups = []

    def rest(self, after):
        sems, srcs, lands, _ = self.gather
        srcs, lands = _exchange_wait("gather", sems, srcs, lands, after, "ag_rest_wait")
        wf = dict(zip(REST, _gather_finish(srcs, lands, "ag_rest_finish")))
        for n in ("w_out", "w_cq", "w_ckv", "w_down"):
            wf[n] = wf[n].reshape(-1, wf[n].shape[-1])
        return wf

    def emit(self, grads):
        names, tag = list(grads), "_%d" % len(self.groups)
        got = _pair_exchange([grads[n] for n in names], "rs_pair" + tag)
        sums = [_pair_sum(grads[n], b, "rs_sum_" + n) for n, b in zip(names, got)]
        sems, srcs, lands, token = _exchange_start("scatter", [q for q, _ in sums], [l for _, l in sums], None,
                                                   "rs_chip_start" + tag)
        self.groups.append((names, sems, srcs, lands))
        return token

    def finish(self, gi, after):
        names, sems, srcs, lands = self.groups[gi]
        _, lands = _exchange_wait("scatter", sems, srcs, lands, after, "rs_chip_wait_%d" % gi)
        return dict(zip(names, lands))


def _tie(v, token):
    return v if token is None else v + token[0:1, 0:1]


def _local_step(x, mem, pos_col, target, p, wf, hooks=None):
    p = dict(p)
    inv = np.zeros((1, LANES), np.float32)
    freq = (ROPE_THETA ** (-2.0 * np.arange(ROT // 2, dtype=np.float32) / ROT)).astype(np.float32)
    for l in range(LANES):
        if l % HEAD < ROT:
            inv[0, l] = freq[(l % HEAD) % (ROT // 2)]
    inv_tab = jnp.asarray(inv)
    gq128, gk128 = jnp.tile(p["g_q"], (1, 2)), jnp.tile(p["g_k"], (1, 2))
    dtb128, alog128 = _pad_lanes(p["dt_bias"]), _pad_lanes(p["a_log"])
    dskip_b = jnp.repeat(p["d_skip"], HEAD, axis=1)
    conv_w, conv_b = wf["conv_w"], p["conv_b"]

    h = _rms_fwd(x, p["g_mix"], "rms_mix")
    proj = _matmul(h, wf["w_in"], mode="nn", name="mm_in", tm=1024, tn=896, tk=D)
    qr, kr, vb = _qk_prep(proj, pos_col, gq128, gk128, inv_tab)
    pats = ((1, NB), (4, NB // 4), (16, 1))
    qs = [_perm(qr, d) for d, _ in pats]
    ks = [_perm(kr, d) for d, _ in pats]
    vs = [_perm(vb, d) for d, _ in pats]
    os_, ls_ = [], []
    for (d, nb), qd, kd, vd in zip(pats, qs, ks, vs):
        o, l = _attn_fwd(qd, kd, vd, nb, "attn_fwd_d%d" % d)
        os_.append(_unperm(o, d))
        ls_.append(_unperm(l, d))
    attn, lse, attn_n = _attn_merge(os_, ls_, p["g_attn_out"])

    xbc = _conv_fwd(proj, conv_w, conv_b)
    dt, a_cs = _ssd_prep(proj, dtb128, alog128)
    a_b = jnp.repeat(a_cs[:, :N_HEADS], HEAD, axis=1)
    dt_b = jnp.repeat(dt[:, :N_HEADS], HEAD, axis=1)
    at_r, dt_r = _row_layout(a_cs[:, :N_HEADS]), _row_layout(dt[:, :N_HEADS])
    y_ssd, h_all = _ssd_fwd(xbc, a_b, dt_b, at_r, dt_r)
    ssm_n = _ssd_post(y_ssd, xbc, proj, dskip_b, p["g_ssm_out"])

    if hooks is not None:
        wf = {**wf, **hooks.rest(ssm_n)}
    mix = jnp.concatenate([attn_n, ssm_n], axis=1)
    x1 = _matmul(mix, wf["w_out"], mode="nn", name="mm_out", tm=1024, tn=1024, tk=D,
                 extras=(x,), epilogue=lambda acc, r: (acc + r,))
    hc = _rms_fwd(x1, p["g_cross"], "rms_cross")
    memh = _rms_fwd(mem, p["g_mem"], "rms_mem")
    qc = _matmul(hc, wf["w_cq"], mode="nn", name="mm_cq", tm=1024, tn=512, tk=D)
    kv = _matmul(memh, wf["w_ckv"], mode="nn", name="mm_ckv", tm=N_MEM, tn=1024, tk=D)
    oc = _cross_fwd(qc, kv, p["g_cq"], p["g_ck"])
    x2 = _matmul(oc, wf["w_co"], mode="nn", name="mm_co", tm=1024, tn=256, tk=512, b_cb=256,
                 extras=(x1,), epilogue=lambda acc, r: (acc + r,))
    hm = _rms_fwd(x2, p["g_mlp"], "rms_mlp")

    def up_epi(acc):
        ru = jnp.maximum(acc, 0.0)
        return ru * ru, 2.0 * ru

    act, relu2 = _matmul(hm, wf["w_up"], mode="nn", name="mm_up", tm=1024, tn=1024, tk=D, b_cb=1024,
                         out_dtypes=(BF, BF), epilogue=up_epi)

    def loss_epi(acc, r, t):
        d = (acc + r - t) * (1.0 / D)
        return d, d

    dy, dyb = _matmul(act, wf["w_down"], mode="nn", name="mm_down", tm=512, tn=1024, tk=2048, extras=(x2, target),
                      out_dtypes=(F32, BF), epilogue=loss_epi)
    loss_part = _loss_sum(dy)[0, 0] * (0.5 * D)

    gb, gs = {}, {}
    du = _matmul(dyb, wf["w_down"], mode="nt", name="mm_down_dx", tm=1024, tn=1024, tk=D, extras=(relu2,),
                 out_dtypes=(BF,), epilogue=lambda acc, r: (acc * r.astype(F32),))
    gb["w_down"] = _matmul(act, dyb, mode="tn", name="mm_down_dw", tm=1024, tn=1024, tk=S,
                           out_dtypes=(BF,)).reshape(N_DEV, D_FF // N_DEV, D)
    gb["w_up"] = _matmul(hm, du, mode="tn", name="mm_up_dw", tm=1024, tn=1024, tk=S, out_dtypes=(BF,), out_cb=1024)
    if hooks is not None:
        p["g_mlp"] = _tie(p["g_mlp"], hooks.emit({n: gb[n] for n in ("w_down", "w_up")}))
    dhm = _matmul(du, wf["w_up"], mode="nt", name="mm_up_dx", tm=1024, tn=1024, tk=1024, b_cb=1024)
    dx2, dx2b, gs["g_mlp"] = _rms_bwd_call(x2, p["g_mlp"], dhm, dy, "rms_mlp_bwd")

    doc = _matmul(dx2b, wf["w_co"], mode="nt", name="mm_co_dx", tm=1024, tn=512, tk=256, b_cb=256)
    gb["w_co"] = _matmul(oc, dx2b, mode="tn", name="mm_co_dw", tm=512, tn=256, tk=S, out_dtypes=(BF,), out_cb=256)
    dqc, dkn, dvc, gs["g_cq"] = _cross_bwd(qc, doc, kv, p["g_cq"], p["g_ck"])
    dkv, gs["g_ck"] = _cross_kv_bwd(kv, dkn, dvc, p["g_ck"])
    gb["w_cq"] = _matmul(hc, dqc, mode="tn", name="mm_cq_dw", tm=1024, tn=512, tk=S,
                         out_dtypes=(BF,)).reshape(N_DEV, D // N_DEV, D_CROSS)
    dhc = _matmul(dqc, wf["w_cq"], mode="nt", name="mm_cq_dx", tm=1024, tn=1024, tk=512)
    gb["w_ckv"] = _matmul(memh, dkv, mode="tn", name="mm_ckv_dw", tm=1024, tn=1024, tk=N_MEM,
                          out_dtypes=(BF,)).reshape(N_DEV, D // N_DEV, 2 * D_CROSS)
    dmemh = _matmul(dkv, wf["w_ckv"], mode="nt", name="mm_ckv_dx", tm=N_MEM, tn=1024, tk=1024)
    (gs["g_mem"],) = _rms_bwd_call(mem, p["g_mem"], dmemh, None, "rms_mem_bwd")
    dx1, dx1b, gs["g_cross"] = _rms_bwd_call(x1, p["g_cross"], dhc, dx2, "rms_cross_bwd")

    dmix = _matmul(dx1b, wf["w_out"], mode="nt", name="mm_out_dx", tm=1024, tn=1024, tk=D)
    gb["w_out"] = _matmul(mix, dx1b, mode="tn", name="mm_out_dw", tm=1024, tn=1024, tk=S,
                          out_dtypes=(BF,)).reshape(N_DEV, D // N_DEV, D)
    if hooks is not None:
        p["g_attn_out"] = _tie(p["g_attn_out"], hooks.emit({n: gb[n] for n in ("w_co", "w_cq", "w_ckv", "w_out")}))

    dattn, delta, gs["g_attn_out"] = _attn_merge_bwd(dmix, attn, p["g_attn_out"])
    dqs, dks, dvs = [], [], []
    for (d, nb), qd, kd, vd in zip(pats, qs, ks, vs):
        dq, dk, dv = _attn_bwd(qd, kd, vd, _perm(dattn, d), _perm(lse, d), _perm(delta, d), nb, "attn_bwd_d%d" % d)
        dqs.append(_unperm(dq, d))
        dks.append(_unperm(dk, d))
        dvs.append(_unperm(dv, d))
    dq_pre, dk_pre, dv_pre, dgq, dgk = _qk_bwd(dqs, dks, dvs, proj, pos_col, gq128, gk128, inv_tab)
    gs["g_q"], gs["g_k"] = dgq[:, :HEAD], dgk[:, :HEAD]

    dy_ssd, dz, dxs_skip, dd_lane, gs["g_ssm_out"] = _ssd_post_bwd(y_ssd, xbc, proj, dmix, dskip_b, p["g_ssm_out"])
    gs["d_skip"] = dd_lane.reshape(N_HEADS, HEAD).sum(axis=1).reshape(1, N_HEADS)
    dc_pair, daq_b, dx_ssd, db_pair, dak_r, ddt_r, ddt_b = _ssd_bwd(xbc, dy_ssd, h_all, a_b, dt_b, at_r, dt_r)
    daq = _pad_lanes(daq_b[:, ::HEAD])
    dak = _pad_lanes(_row_layout_inv(dak_r))
    ddt_direct = _pad_lanes(_row_layout_inv(ddt_r))
    ddt_raw, dalog, dbias = _ssd_prep_bwd(daq, dak, ddt_direct, _pad_lanes(ddt_b[:, ::HEAD]), dt, proj,
                                          dtb128, alog128)
    gs["a_log"], gs["dt_bias"] = dalog[:, :N_HEADS], dbias[:, :N_HEADS]
    same = lambda c: c
    dxbc_x, dcw_x, dcb_x = _conv_bwd(proj, conv_w, conv_b, dxs_skip, dx_ssd, same, same, 0, 8, "conv_bwd_x")
    dxbc_b, dcw_b, dcb_b = _conv_bwd(proj, conv_w, conv_b, db_pair, db_pair, lambda c: 2 * c, lambda c: 2 * c + 1,
                                     D_SSM, 4, "conv_bwd_b")
    dxbc_c, dcw_c, dcb_c = _conv_bwd(proj, conv_w, conv_b, dc_pair, dc_pair, lambda c: 2 * c, lambda c: 2 * c + 1,
                                     D_SSM + 512, 4, "conv_bwd_c")
    g_conv_w = jnp.concatenate([dcw_x, dcw_b, dcw_c], axis=1)
    gs["conv_b"] = jnp.concatenate([dcb_x, dcb_b, dcb_c], axis=1)

    dproj = jnp.concatenate([dq_pre, dk_pre, dv_pre, dz, dxbc_x, dxbc_b, dxbc_c, ddt_raw], axis=1)
    dw_in = _matmul(h, dproj, mode="tn", name="mm_in_dw", tm=1024, tn=896, tk=S, out_dtypes=(BF,))
    gb["w_in"] = dw_in[:, :D_IN].reshape(D, N_DEV, D_IN // N_DEV).transpose(1, 0, 2)
    token = None if hooks is None else hooks.emit({"w_in": gb["w_in"]})
    dh = _matmul(dproj, wf["w_in"], mode="nt", name="mm_in_dx", tm=1024, tn=1024, tk=896, after=token)
    grad_x, _, gs["g_mix"] = _rms_bwd_call(x, p["g_mix"], dh, dx1, "rms_mix_bwd")
    return loss_part, grad_x, gb, gs, g_conv_w


def kernel(x, mem, positions, g_mix, w_in, g_q, g_k, g_attn_out, conv_w, conv_b, dt_bias, a_log, d_skip, g_ssm_out, w_out, g_cross, g_mem, w_cq, w_ckv, g_cq, g_ck, w_co, g_mlp, w_up, w_down, loss_target, m_g_mix, m_w_in, m_g_q, m_g_k, m_g_attn_out, m_conv_w, m_conv_b, m_dt_bias, m_a_log, m_d_skip, m_g_ssm_out, m_w_out, m_g_cross, m_g_mem, m_w_cq, m_w_ckv, m_g_cq, m_g_ck, m_w_co, m_g_mlp, m_w_up, m_w_down, v_g_mix, v_w_in, v_g_q, v_g_k, v_g_attn_out, v_conv_w, v_conv_b, v_dt_bias, v_a_log, v_d_skip, v_g_ssm_out, v_w_out, v_g_cross, v_g_mem, v_w_cq, v_w_ckv, v_g_cq, v_g_ck, v_w_co, v_g_mlp, v_w_up, v_w_down):
    args = dict(locals())
    w = {n: args[n] for n in WEIGHTS}
    m = {n: args["m_" + n] for n in WEIGHTS}
    v = {n: args["v_" + n] for n in WEIGHTS}
    small = {n: w[n] for n, _ in SMALL}
    me = 4 * lax.axis_index("x") + 2 * lax.axis_index("y") + lax.axis_index("c")

    w_in_g, conv_w_g = _all_gather([w["w_in"][0].astype(BF), w["conv_w"][0]], "ag_first")
    wf = {"w_in": jnp.pad(w_in_g.transpose(1, 0, 2).reshape(D, D_IN), ((0, 0), (0, D_INP - D_IN))),
          "conv_w": conv_w_g.transpose(1, 0, 2).reshape(CONV_W, D_CONV)}
    overlap = _Overlap([w[n][0].astype(BF) for n in REST], w_in_g)
    p = dict(small)
    p["g_mix"] = _tie(p["g_mix"], overlap.token)

    loss_part, grad_x, _, gs, g_conv_w = _local_step(
        x[0], mem[0], positions.reshape(S, 1), loss_target[0], p, wf, overlap)
    loss = lax.psum(loss_part, ("x", "y", "c"))

    out = {}
    last = grad_x
    for gi in range(3):
        for n, parts in overlap.finish(gi, last).items():
            res_n = _adamw(w[n][0], m[n][0], v[n][0], parts, "adamw_" + n)
            out[n] = [t.reshape(w[n].shape) for t in res_n]
            last = res_n[0]

    sm_parts, cw_parts = _all_gather([_pack_small(gs), g_conv_w], "ag_small_grads", after=last)
    sm = [_unpack_small(t) for t in _adamw(_pack_small(small), _pack_small({n: m[n] for n, _ in SMALL}),
                                           _pack_small({n: v[n] for n, _ in SMALL}), sm_parts, "adamw_small")]
    for n, _ in SMALL:
        out[n] = [t[n] for t in sm]
    cols = D_CONV // N_DEV
    cw_mine = lax.dynamic_slice_in_dim(cw_parts, me * cols, cols, axis=2)
    out["conv_w"] = [t.reshape(w["conv_w"].shape) for t in
                     _adamw(w["conv_w"][0], m["conv_w"][0], v["conv_w"][0], cw_mine, "adamw_conv_w")]

    res = [loss, grad_x.reshape(x.shape)]
    for k in range(4):
        res += [out[n][k] for n in WEIGHTS]
    return tuple(res)
```

```python
import functools
import math

import numpy as np
import jax
import jax.numpy as jnp
from jax import lax
from jax.experimental import pallas as pl
from jax.experimental.pallas import tpu as pltpu

F32 = jnp.float32
BF = jnp.bfloat16

N_DEV = 8
S = 2048
D = 2048
D_ATTN = 1024
N_HEADS = 16
HEAD = 64
ROT = 16
ROPE_THETA = 500000.0
D_SSM = 1024
D_CONV = 2048
CONV_W = 4
N_MEM = 256
D_CROSS = 512
CROSS_HEAD = 128
D_FF = 8192
D_IN = 6160
D_INP = 6272
EPS = 1e-6
BLK = 128
NB = S // BLK
LANES = 128
NEG = -1e30

ADAM_LR = 0.001
ADAM_B1 = 0.9
ADAM_B2 = 0.999
ADAM_EPS = 1e-08
ADAM_WD = 0.01
ADAM_STEP = 10

VMEM_LIMIT_BYTES = 48 * 1024 * 1024
ROW_TILE = 256


def _params(*sem):
    return pltpu.CompilerParams(dimension_semantics=sem, vmem_limit_bytes=VMEM_LIMIT_BYTES)


def _matmul(a, b, *, mode, name, tm, tn, tk, out_dtypes=(F32,), b_cb=None, out_cb=None,
            extras=(), epilogue=None, after=None):
    if mode == "tn":
        kk, m = a.shape
    else:
        m, kk = a.shape
    if b_cb is None:
        br, bc = b.shape
    else:
        br, bc = b.shape[1], N_DEV * b_cb
    n = br if mode == "nt" else bc
    assert m % tm == 0 and n % tn == 0 and kk % tk == 0, (name, m, n, kk)
    nk = kk // tk
    grid = (m // tm, n // tn, nk)

    if mode == "tn":
        a_spec = pl.BlockSpec((tk, tm), lambda i, j, k: (k, i))
    else:
        a_spec = pl.BlockSpec((tm, tk), lambda i, j, k: (i, k))
    if mode == "nt":
        b_blk, b_idx = (tn, tk), (lambda i, j, k: (j, k))
    else:
        b_blk, b_idx = (tk, tn), (lambda i, j, k: (k, j))
    if b_cb is None:
        b_spec = pl.BlockSpec(b_blk, b_idx)
    else:
        tc = b_blk[1]
        assert b_cb % tc == 0
        per = b_cb // tc

        def b_idx3(i, j, k):
            r, c = b_idx(i, j, k)
            return (c // per, r, c % per)
        b_spec = pl.BlockSpec((None,) + b_blk, b_idx3)
    ex_specs = [pl.BlockSpec((tm, tn), lambda i, j, k: (i, j)) for _ in extras]
    if out_cb is None:
        out_shape = [jax.ShapeDtypeStruct((m, n), dt) for dt in out_dtypes]
        out_specs = [pl.BlockSpec((tm, tn), lambda i, j, k: (i, j)) for _ in out_dtypes]
    else:
        assert out_cb % tn == 0
        pero = out_cb // tn
        out_shape = [jax.ShapeDtypeStruct((N_DEV, m, out_cb), dt) for dt in out_dtypes]
        out_specs = [pl.BlockSpec((None, tm, tn), lambda i, j, k: (j // pero, i, j % pero)) for _ in out_dtypes]
    dn = {"nn": (((1,), (0,)), ((), ())), "nt": (((1,), (1,)), ((), ())), "tn": (((0,), (0,)), ((), ()))}[mode]
    ne, no = len(extras), len(out_dtypes)
    n_after = 0 if after is None else 1

    def kern(a_ref, b_ref, *rest):
        ex_refs, out_refs = rest[:ne], rest[ne + n_after:ne + n_after + no]

        def finish(acc):
            outs = (acc,) if epilogue is None else epilogue(acc, *[r[...] for r in ex_refs])
            for r, o in zip(out_refs, outs):
                r[...] = o.astype(r.dtype)

        part = lax.dot_general(a_ref[...].astype(BF), b_ref[...].astype(BF), dn, preferred_element_type=F32)
        if nk == 1:
            finish(part)
            return
        acc_ref = rest[ne + n_after + no]
        k = pl.program_id(2)

        @pl.when(k == 0)
        def _():
            acc_ref[...] = part

        @pl.when(k > 0)
        def _():
            acc_ref[...] += part

        @pl.when(k == nk - 1)
        def _():
            finish(acc_ref[...])

    res = pl.pallas_call(
        kern, name=name, grid=grid, in_specs=[a_spec, b_spec] + ex_specs + [ANY] * n_after, out_specs=out_specs,
        out_shape=out_shape, scratch_shapes=[pltpu.VMEM((tm, tn), F32)] if nk > 1 else [],
        compiler_params=_params("parallel", "parallel", "arbitrary"),
    )(a, b, *extras, *([] if after is None else [after]))
    return res[0] if no == 1 else tuple(res)


def _rowwise(body, *, name, nrows, tile, row_ins, full_ins=(), row_outs=(), acc_outs=(), scratch=(),
             reverse=False):
    n = nrows // tile

    def ridx(i):
        return (n - 1 - i) if reverse else i

    in_specs, args = [], []
    for arr, width, cb in row_ins:
        in_specs.append(pl.BlockSpec((tile, width), lambda i, cb=cb: (ridx(i), cb)))
        args.append(arr)
    for arr in full_ins:
        in_specs.append(pl.BlockSpec(arr.shape, lambda i, nd=arr.ndim: (0,) * nd))
        args.append(arr)
    out_shape, out_specs = [], []
    for width, dt in row_outs:
        out_shape.append(jax.ShapeDtypeStruct((nrows, width), dt))
        out_specs.append(pl.BlockSpec((tile, width), lambda i: (ridx(i), 0)))
    for shp, dt in acc_outs:
        out_shape.append(jax.ShapeDtypeStruct(shp, dt))
        out_specs.append(pl.BlockSpec(shp, lambda i, nd=len(shp): (0,) * nd))

    def kern(*refs):
        body(pl.program_id(0), n, *refs)

    return pl.pallas_call(kern, name=name, grid=(n,), in_specs=in_specs, out_specs=out_specs,
                          out_shape=out_shape, scratch_shapes=list(scratch),
                          compiler_params=_params("arbitrary"))(*args)


def _accum(ref, val, i):
    @pl.when(i == 0)
    def _():
        ref[...] = val

    @pl.when(i > 0)
    def _():
        ref[...] += val


def _sigmoid(x):
    return 1.0 / (1.0 + jnp.exp(-x))


def _rms_n(x):
    r = lax.rsqrt(jnp.mean(x * x, axis=-1, keepdims=True) + EPS)
    return x * r, r


def _rms_bwd(x, g, dh):
    n, r = _rms_n(x)
    dn = dh * g
    dx = r * (dn - n * jnp.mean(dn * n, axis=-1, keepdims=True))
    return dx, jnp.sum(dh * n, axis=0, keepdims=True)


def _seg_sum(x, seg):
    t, w = x.shape
    tiles = [x[:, LANES * j:LANES * (j + 1)] for j in range(w // LANES)]
    outs = []
    if seg == 64:
        lo = lax.broadcasted_iota(jnp.int32, (t, LANES), 1) < 64
        for xt in tiles:
            s_lo = jnp.sum(jnp.where(lo, xt, 0.0), axis=-1, keepdims=True)
            s_hi = jnp.sum(jnp.where(lo, 0.0, xt), axis=-1, keepdims=True)
            outs.append(jnp.where(lo, s_lo, s_hi))
    else:
        sums = [jnp.sum(xt, axis=-1, keepdims=True) for xt in tiles]
        if seg == 256:
            sums = [sums[2 * (j // 2)] + sums[2 * (j // 2) + 1] for j in range(len(sums))]
        else:
            assert seg == 128
        outs = [jnp.broadcast_to(s, (t, LANES)) for s in sums]
    return outs[0] if len(outs) == 1 else jnp.concatenate(outs, axis=1)


def _seg_rms_n(x, seg):
    r = lax.rsqrt(_seg_sum(x * x, seg) * (1.0 / seg) + EPS)
    return x * r, r


def _seg_rms_bwd(x, g, dy, seg):
    n, r = _seg_rms_n(x, seg)
    dn = dy * g
    dx = r * (dn - n * (_seg_sum(dn * n, seg) * (1.0 / seg)))
    return dx, jnp.sum(dy * n, axis=0, keepdims=True)


def _rope_tables(pos_col, inv_tab, t):
    ang = pos_col.astype(F32) * inv_tab
    idx = lax.broadcasted_iota(jnp.int32, (t, LANES), 1) % HEAD
    cos, sin = jnp.cos(ang), jnp.sin(ang)
    c = jnp.where(idx < ROT, cos, 1.0)
    sg = jnp.where(idx < ROT // 2, -sin, jnp.where(idx < ROT, sin, 0.0))
    return c, sg, idx < ROT // 2


def _rope(x, c, sg, lo):
    partner = jnp.where(lo, pltpu.roll(x, LANES - ROT // 2, 1), pltpu.roll(x, ROT // 2, 1))
    return x * c + partner * sg


def _fold_heads(v):
    v8 = jnp.broadcast_to(v, (8, LANES))
    return (v8 + pltpu.roll(v8, HEAD, 1))[0:1]


def _dot(a, b, dn):
    return lax.dot_general(a, b, (dn, ((), ())), preferred_element_type=F32)


NN = ((1,), (0,))
NT = ((1,), (1,))
TN = ((0,), (0,))


def _dot3(l01, x):
    x1 = x.astype(BF)
    r1 = x - x1.astype(F32)
    x2 = r1.astype(BF)
    x3 = (r1 - x2.astype(F32)).astype(BF)
    return _dot(l01, x1, NN) + _dot(l01, x2, NN) + _dot(l01, x3, NN)


def _rms_fwd(x, g, name):
    rows = x.shape[0]

    def body(i, n, x_ref, g_ref, o_ref):
        nx, _ = _rms_n(x_ref[...])
        o_ref[...] = (nx * g_ref[...]).astype(BF)

    return _rowwise(body, name=name, nrows=rows, tile=min(ROW_TILE, rows), row_ins=[(x, D, 0)],
                    full_ins=[g], row_outs=[(D, BF)])[0]


def _qk_prep(proj, pos_col, gq128, gk128, inv_tab):
    scale = HEAD ** -0.5

    def body(i, n, q_ref, k_ref, p_ref, gq_ref, gk_ref, it_ref, qo_ref, ko_ref):
        t = q_ref.shape[0]
        c, sg, lo = _rope_tables(p_ref[...], it_ref[...], t)
        for j in range(D_ATTN // LANES):
            sl = slice(LANES * j, LANES * (j + 1))
            qn, _ = _seg_rms_n(q_ref[:, sl], HEAD)
            kn, _ = _seg_rms_n(k_ref[:, sl], HEAD)
            qo_ref[:, sl] = _rope(qn * gq_ref[...], c, sg, lo) * scale
            ko_ref[:, sl] = _rope(kn * gk_ref[...], c, sg, lo)

    return _rowwise(body, name="qk_prep", nrows=S, tile=ROW_TILE,
                    row_ins=[(proj, D_ATTN, 0), (proj, D_ATTN, 1), (pos_col, 1, 0)],
                    full_ins=[gq128, gk128, inv_tab], row_outs=[(D_ATTN, F32)] * 2)


ATTN_QB = 4
V_COLS = pl.BlockSpec((S, LANES), lambda p, b: (0, 2 * D_ATTN // LANES + p))


def _band(b, d):
    nb = NB // d
    r, n = b // nb, b % nb
    width, kn = (BLK, n) if nb == 1 else (2 * BLK, jnp.maximum(n - 1, 0))

    def rows(first_member, count):
        if d == 1:
            return pl.ds(pl.multiple_of(first_member, BLK), count)
        return pl.ds(first_member * d + r, count, stride=d)

    qm = n * BLK + lax.broadcasted_iota(jnp.int32, (BLK, width), 0)
    km = kn * BLK + lax.broadcasted_iota(jnp.int32, (BLK, width), 1)
    valid = km <= qm
    if nb > 1:
        valid = valid & (km >= qm - BLK)
    return rows(n * BLK, BLK), rows(kn * BLK, width), valid


def _attn_fwd(q, k, v, d, name):
    def kern(q_ref, k_ref, v_ref, o_ref, l_ref):
        half0 = lax.broadcasted_iota(jnp.int32, (BLK, LANES), 1) < HEAD
        for bb in range(ATTN_QB):
            q_rows, k_rows, valid = _band(pl.program_id(1) * ATTN_QB + bb, d)
            q = q_ref[q_rows, :]
            kb, vb = k_ref[k_rows, :].astype(BF), v_ref[k_rows, :].astype(BF)
            o_h, l_h = [], []
            for h in range(2):
                hm = half0 if h == 0 else jnp.logical_not(half0)
                s = jnp.where(valid, _dot(jnp.where(hm, q, 0.0).astype(BF), kb, NT), NEG)
                m = jnp.max(s, axis=-1, keepdims=True)
                p = jnp.exp(s - m)
                den = jnp.sum(p, axis=-1, keepdims=True)
                o_h.append(_dot(p.astype(BF), vb, NN) / den)
                l_h.append(m + jnp.log(den))
            o_ref[q_rows, :] = jnp.where(half0, o_h[0], o_h[1])
            l_ref[q_rows, :] = jnp.where(half0, l_h[0], l_h[1])

    seq = pl.BlockSpec((S, LANES), lambda p, b: (0, p))
    return pl.pallas_call(
        kern, name=name, grid=(D_ATTN // LANES, NB // ATTN_QB), in_specs=[seq, seq, V_COLS], out_specs=[seq, seq],
        out_shape=[jax.ShapeDtypeStruct((S, D_ATTN), F32)] * 2,
        compiler_params=_params("parallel", "arbitrary"))(q, k, v)


def _attn_merge(os_, ls_, g_attn):
    def body(i, n, o1, o2, o3, l1, l2, l3, g_ref, a_ref, lse_ref, an_ref):
        la, lb, lc = l1[...], l2[...], l3[...]
        m = jnp.maximum(jnp.maximum(la, lb), lc)
        ea, eb, ec = jnp.exp(la - m), jnp.exp(lb - m), jnp.exp(lc - m)
        den = ea + eb + ec
        attn = (ea * o1[...] + eb * o2[...] + ec * o3[...]) / den
        a_ref[...] = attn
        lse_ref[...] = m + jnp.log(den)
        nx, _ = _rms_n(attn)
        an_ref[...] = (nx * g_ref[...]).astype(BF)

    return _rowwise(body, name="attn_merge", nrows=S, tile=ROW_TILE,
                    row_ins=[(a, D_ATTN, 0) for a in (*os_, *ls_)], full_ins=[g_attn],
                    row_outs=[(D_ATTN, F32), (D_ATTN, F32), (D_ATTN, BF)])


def _conv_taps(x, w_ref):
    rows = lax.broadcasted_iota(jnp.int32, x.shape, 0)
    shifted = []
    for w in range(CONV_W):
        k = CONV_W - 1 - w
        shifted.append(x if k == 0 else jnp.where(rows >= k, pltpu.roll(x, k, 0), 0.0))
    acc = shifted[0] * w_ref[0:1, :]
    for w in range(1, CONV_W):
        acc = acc + shifted[w] * w_ref[w:w + 1, :]
    return acc, shifted


def _conv_fwd(proj, conv_w, conv_b):
    tc = 256

    def kern(x_ref, w_ref, b_ref, o_ref):
        c, _ = _conv_taps(x_ref[...], w_ref)
        c = c + b_ref[...]
        o_ref[...] = c * _sigmoid(c)

    return pl.pallas_call(
        kern, name="conv_fwd", grid=(D_CONV // tc,),
        in_specs=[pl.BlockSpec((S, tc), lambda c: (0, 4 * D_ATTN // tc + c)),
                  pl.BlockSpec((CONV_W, tc), lambda c: (0, c)), pl.BlockSpec((1, tc), lambda c: (0, c))],
        out_specs=pl.BlockSpec((S, tc), lambda c: (0, c)),
        out_shape=jax.ShapeDtypeStruct((S, D_CONV), F32), compiler_params=_params("parallel"))(proj, conv_w, conv_b)


def _softplus(x):
    return jnp.maximum(x, 0.0) + jnp.log1p(jnp.exp(-jnp.abs(x)))


def _ssd_prep(proj, dt_bias128, a_log128):
    def body(i, n, raw_ref, b_ref, al_ref, dt_ref, a_ref, carry):
        @pl.when(i == 0)
        def _():
            carry[...] = jnp.zeros_like(carry)

        dt = _softplus(raw_ref[...] + b_ref[...])
        da = dt * (-jnp.exp(al_ref[...]))
        tri = (lax.broadcasted_iota(jnp.int32, (BLK, BLK), 0) >= lax.broadcasted_iota(jnp.int32, (BLK, BLK), 1))
        cs = _dot3(tri.astype(BF), da) + carry[0:1, :]
        dt_ref[...] = dt
        a_ref[...] = cs
        carry[...] = jnp.broadcast_to(cs[BLK - 1:BLK, :], carry.shape)

    return _rowwise(body, name="ssd_prep", nrows=S, tile=BLK, row_ins=[(proj, LANES, (D_INP - LANES) // LANES)],
                    full_ins=[dt_bias128, a_log128], row_outs=[(LANES, F32), (LANES, F32)],
                    scratch=[pltpu.VMEM((8, LANES), F32)])


def _ssd_decay(a_col, at_row, causal):
    diff = jnp.where(causal, a_col - at_row, 0.0)
    return jnp.where(causal, jnp.exp(diff), 0.0)


def _ssd_fwd(xbc, a_b, dt_b, at_r, dt_r):
    def kern(c_ref, b_ref, x_ref, ab_ref, dtb_ref, at_ref, dt_ref, y_ref, hall_ref, h_scr, aprev_scr):
        i = pl.program_id(1)

        @pl.when(i == 0)
        def _():
            h_scr[...] = jnp.zeros_like(h_scr)
            aprev_scr[...] = jnp.zeros_like(aprev_scr)

        ci, bi, x = c_ref[...].astype(BF), b_ref[...].astype(BF), x_ref[...]
        ab = ab_ref[...]
        a_end = ab[BLK - 1:BLK, :]
        alpha = jnp.exp(ab - aprev_scr[0:1, :])
        beta = jnp.exp(a_end - ab) * dtb_ref[...]
        h = h_scr[...]
        hall_ref[...] = h
        half0 = lax.broadcasted_iota(jnp.int32, (BLK, LANES), 1) < HEAD
        causal = lax.broadcasted_iota(jnp.int32, (BLK, BLK), 1) <= lax.broadcasted_iota(jnp.int32, (BLK, BLK), 0)
        cb = _dot(ci, bi, NT)
        at, dtj = at_ref[...], dt_ref[...]
        y = alpha * _dot(ci, h.astype(BF), NN)
        for hd in range(2):
            hm = half0 if hd == 0 else jnp.logical_not(half0)
            w = cb * _ssd_decay(ab[:, HEAD * hd:HEAD * hd + 1], at[hd:hd + 1, :], causal) * dtj[hd:hd + 1, :]
            y = y + _dot(w.astype(BF), jnp.where(hm, x, 0.0).astype(BF), NN)
        y_ref[...] = y
        h_scr[...] = alpha[BLK - 1:BLK, :] * h + _dot(bi, (beta * x).astype(BF), TN)
        aprev_scr[...] = jnp.broadcast_to(a_end, aprev_scr.shape)

    npair = D_SSM // LANES
    rowvec = pl.BlockSpec((None, None, 2, BLK), lambda p, i: (p, i, 0, 0))
    tile = pl.BlockSpec((BLK, LANES), lambda p, i: (i, p))
    return pl.pallas_call(
        kern, name="ssd_fwd", grid=(npair, NB),
        in_specs=[pl.BlockSpec((BLK, LANES), lambda p, i: (i, 12 + p // 2)),
                  pl.BlockSpec((BLK, LANES), lambda p, i: (i, 8 + p // 2)), tile, tile, tile, rowvec, rowvec],
        out_specs=[tile, tile], out_shape=[jax.ShapeDtypeStruct((S, D_SSM), F32)] * 2,
        scratch_shapes=[pltpu.VMEM((BLK, LANES), F32), pltpu.VMEM((8, LANES), F32)],
        compiler_params=_params("parallel", "arbitrary"))(xbc, xbc, xbc, a_b, dt_b, at_r, dt_r)


def _ssd_post(y_ssd, xbc, proj, dskip_b, g_ssm):
    def body(i, n, y_ref, xs_ref, z_ref, d_ref, g_ref, o_ref):
        z = z_ref[...]
        y2 = (y_ref[...] + d_ref[...] * xs_ref[...]) * (z * _sigmoid(z))
        nx, _ = _seg_rms_n(y2, 256)
        o_ref[...] = (nx * g_ref[...]).astype(BF)

    return _rowwise(body, name="ssd_post", nrows=S, tile=ROW_TILE,
                    row_ins=[(y_ssd, D_SSM, 0), (xbc, D_SSM, 0), (proj, D_SSM, 3)], full_ins=[dskip_b, g_ssm],
                    row_outs=[(D_SSM, BF)])[0]


def _cross_heads(q, kv_ref, gq, gk):
    out = []
    for h in range(D_CROSS // CROSS_HEAD):
        sl = slice(CROSS_HEAD * h, CROSS_HEAD * (h + 1))
        nq, rq = _rms_n(q[:, sl])
        nk, rk = _rms_n(kv_ref[:, sl])
        v = kv_ref[:, D_CROSS + CROSS_HEAD * h:D_CROSS + CROSS_HEAD * (h + 1)]
        out.append((sl, nq, rq, nk, rk, v))
    return out


def _cross_fwd(qc, kv, g_cq, g_ck):
    scale = CROSS_HEAD ** -0.5

    def body(i, n, q_ref, kv_ref, gq_ref, gk_ref, o_ref):
        for sl, nq, _, nk, _, v in _cross_heads(q_ref[...], kv_ref, gq_ref[...], gk_ref[...]):
            qn = (nq * gq_ref[...] * scale).astype(BF)
            kn = (nk * gk_ref[...]).astype(BF)
            s = _dot(qn, kn, NT)
            e = jnp.exp(s - jnp.max(s, axis=-1, keepdims=True))
            p = e / jnp.sum(e, axis=-1, keepdims=True)
            o_ref[:, sl] = _dot(p.astype(BF), v.astype(BF), NN).astype(BF)

    return _rowwise(body, name="cross_fwd", nrows=S, tile=ROW_TILE, row_ins=[(qc, D_CROSS, 0)],
                    full_ins=[kv, g_cq, g_ck], row_outs=[(D_CROSS, BF)])[0]


def _loss_sum(dy):
    def body(i, n, d_ref, o_ref):
        d = d_ref[...]
        s = jnp.sum(jnp.sum(d * d, axis=0, keepdims=True), axis=1, keepdims=True)
        _accum(o_ref, s, i)

    return _rowwise(body, name="loss_sum", nrows=S, tile=ROW_TILE, row_ins=[(dy, D, 0)],
                    acc_outs=[((1, 1), F32)])[0]


def _rms_bwd_call(x, g, dh, dres, name):
    rows = x.shape[0]
    has_res = dres is not None

    def body(i, n, *refs):
        if has_res:
            x_ref, dh_ref, dr_ref, g_ref, dx_ref, dxb_ref, dg_ref = refs
        else:
            x_ref, dh_ref, g_ref, dg_ref = refs
        dx, dg = _rms_bwd(x_ref[...], g_ref[...], dh_ref[...])
        if has_res:
            dx = dx + dr_ref[...]
            dx_ref[...] = dx
            dxb_ref[...] = dx.astype(BF)
        _accum(dg_ref, dg, i)

    row_ins = [(x, D, 0), (dh, D, 0)] + ([(dres, D, 0)] if has_res else [])
    return _rowwise(body, name=name, nrows=rows, tile=min(ROW_TILE, rows), row_ins=row_ins, full_ins=[g],
                    row_outs=[(D, F32), (D, BF)] if has_res else [], acc_outs=[((1, D), F32)])


def _cross_bwd(qc, doc, kv, g_cq, g_ck):
    scale = CROSS_HEAD ** -0.5

    def body(i, n, q_ref, do_ref, kv_ref, gq_ref, gk_ref, dq_ref, dkn_ref, dv_ref, dgq_ref):
        dgq = jnp.zeros((1, CROSS_HEAD), F32)
        dkn_parts, dv_parts = [], []
        for sl, nq, rq, nk, _, v in _cross_heads(q_ref[...], kv_ref, gq_ref[...], gk_ref[...]):
            qn = (nq * gq_ref[...] * scale).astype(BF)
            kn = (nk * gk_ref[...]).astype(BF)
            s = _dot(qn, kn, NT)
            e = jnp.exp(s - jnp.max(s, axis=-1, keepdims=True))
            p = e / jnp.sum(e, axis=-1, keepdims=True)
            do = do_ref[:, sl].astype(BF)
            dv_parts.append(_dot(p.astype(BF), do, TN))
            dp = _dot(do, v.astype(BF), NT)
            ds = (p * (dp - jnp.sum(dp * p, axis=-1, keepdims=True))).astype(BF)
            dqn = _dot(ds, kn, NN)
            dkn_parts.append(_dot(ds, qn, TN))
            dn = dqn * (gq_ref[...] * scale)
            dq_ref[:, sl] = (rq * (dn - nq * jnp.mean(dn * nq, axis=-1, keepdims=True))).astype(BF)
            dgq = dgq + jnp.sum(dqn * scale * nq, axis=0, keepdims=True)
        _accum(dkn_ref, jnp.concatenate(dkn_parts, axis=1), i)
        _accum(dv_ref, jnp.concatenate(dv_parts, axis=1), i)
        _accum(dgq_ref, dgq, i)

    return _rowwise(body, name="cross_bwd", nrows=S, tile=ROW_TILE, row_ins=[(qc, D_CROSS, 0), (doc, D_CROSS, 0)],
                    full_ins=[kv, g_cq, g_ck], row_outs=[(D_CROSS, BF)],
                    acc_outs=[((N_MEM, D_CROSS), F32), ((N_MEM, D_CROSS), F32), ((1, CROSS_HEAD), F32)])


def _cross_kv_bwd(kv, dkn, dv, g_ck):
    def body(i, n, kv_ref, dkn_ref, dv_ref, gk_ref, dkv_ref, dgk_ref):
        dgk = jnp.zeros((1, CROSS_HEAD), F32)
        for h in range(D_CROSS // CROSS_HEAD):
            sl = slice(CROSS_HEAD * h, CROSS_HEAD * (h + 1))
            dk, dg = _rms_bwd(kv_ref[:, sl], gk_ref[...], dkn_ref[:, sl])
            dkv_ref[:, sl] = dk.astype(BF)
            dgk = dgk + dg
        dkv_ref[:, D_CROSS:] = dv_ref[...].astype(BF)
        dgk_ref[...] = dgk

    return _rowwise(body, name="cross_kv_bwd", nrows=N_MEM, tile=N_MEM,
                    row_ins=[(kv, 2 * D_CROSS, 0), (dkn, D_CROSS, 0), (dv, D_CROSS, 0)], full_ins=[g_ck],
                    row_outs=[(2 * D_CROSS, BF)], acc_outs=[((1, CROSS_HEAD), F32)])


def _attn_merge_bwd(dmix, attn, g_attn):
    def body(i, n, dn_ref, a_ref, g_ref, da_ref, dl_ref, dg_ref):
        attn_v = a_ref[...]
        da, dg = _rms_bwd(attn_v, g_ref[...], dn_ref[...])
        da_ref[...] = da
        dl_ref[...] = _seg_sum(da * attn_v, HEAD)
        _accum(dg_ref, dg, i)

    return _rowwise(body, name="attn_merge_bwd", nrows=S, tile=ROW_TILE,
                    row_ins=[(dmix, D_ATTN, 0), (attn, D_ATTN, 0)], full_ins=[g_attn],
                    row_outs=[(D_ATTN, F32), (D_ATTN, F32)], acc_outs=[((1, D_ATTN), F32)])


def _attn_bwd(q, k, v, do, lse, delta, d, name):
    def kern(q_ref, k_ref, v_ref, do_ref, l_ref, d_ref, dq_ref, dk_ref, dv_ref):
        @pl.when(pl.program_id(1) == 0)
        def _():
            dk_ref[...] = jnp.zeros_like(dk_ref)
            dv_ref[...] = jnp.zeros_like(dv_ref)

        half0 = lax.broadcasted_iota(jnp.int32, (BLK, LANES), 1) < HEAD
        updates = []
        for bb in range(ATTN_QB):
            q_rows, k_rows, valid = _band(pl.program_id(1) * ATTN_QB + bb, d)
            q, do = q_ref[q_rows, :], do_ref[q_rows, :]
            lse_t, del_t = l_ref[q_rows, :], d_ref[q_rows, :]
            kb, vb = k_ref[k_rows, :].astype(BF), v_ref[k_rows, :].astype(BF)
            dq_h, dk, dv = [], None, None
            for h in range(2):
                hm = half0 if h == 0 else jnp.logical_not(half0)
                qm = jnp.where(hm, q, 0.0).astype(BF)
                dom = jnp.where(hm, do, 0.0).astype(BF)
                s = jnp.where(valid, _dot(qm, kb, NT), NEG)
                p = jnp.exp(s - lse_t[:, HEAD * h:HEAD * h + 1])
                ds = (p * (_dot(dom, vb, NT) - del_t[:, HEAD * h:HEAD * h + 1])).astype(BF)
                dq_h.append(_dot(ds, kb, NN))
                dk_h, dv_h = _dot(ds, qm, TN), _dot(p.astype(BF), dom, TN)
                dk, dv = (dk_h, dv_h) if h == 0 else (dk + dk_h, dv + dv_h)
            dq_ref[q_rows, :] = jnp.where(half0, dq_h[0], dq_h[1])
            updates.append((k_rows, dk, dv))
        for k_rows, dk, dv in updates:
            dk_ref[k_rows, :] += dk
            dv_ref[k_rows, :] += dv

    seq = pl.BlockSpec((S, LANES), lambda p, b: (0, p))
    return pl.pallas_call(
        kern, name=name, grid=(D_ATTN // LANES, NB // ATTN_QB), in_specs=[seq, seq, V_COLS, seq, seq, seq],
        out_specs=[seq, seq, seq], out_shape=[jax.ShapeDtypeStruct((S, D_ATTN), F32)] * 3,
        compiler_params=_params("parallel", "arbitrary"))(q, k, v, do, lse, delta)


def _qk_bwd(dqs, dks, dvs, proj, pos_col, gq128, gk128, inv_tab):
    scale = HEAD ** -0.5

    def body(i, n, *refs):
        dq_refs, dk_refs, dv_refs = refs[0:3], refs[3:6], refs[6:9]
        q_ref, k_ref, p_ref, gq_ref, gk_ref, it_ref = refs[9:15]
        dqo_ref, dko_ref, dvo_ref, dgq_ref, dgk_ref = refs[15:20]
        t = q_ref.shape[0]
        c, sg, lo = _rope_tables(p_ref[...], it_ref[...], t)
        dgq = jnp.zeros((1, LANES), F32)
        dgk = jnp.zeros((1, LANES), F32)
        for j in range(D_ATTN // LANES):
            sl = slice(LANES * j, LANES * (j + 1))
            dqr = _rope(dq_refs[0][:, sl] + dq_refs[1][:, sl] + dq_refs[2][:, sl], c, -sg, lo) * scale
            dkr = _rope(dk_refs[0][:, sl] + dk_refs[1][:, sl] + dk_refs[2][:, sl], c, -sg, lo)
            dq, gq = _seg_rms_bwd(q_ref[:, sl], gq_ref[...], dqr, HEAD)
            dk, gk = _seg_rms_bwd(k_ref[:, sl], gk_ref[...], dkr, HEAD)
            dqo_ref[:, sl] = dq.astype(BF)
            dko_ref[:, sl] = dk.astype(BF)
            dgq, dgk = dgq + gq, dgk + gk
        dvo_ref[...] = (dv_refs[0][...] + dv_refs[1][...] + dv_refs[2][...]).astype(BF)
        _accum(dgq_ref, _fold_heads(dgq), i)
        _accum(dgk_ref, _fold_heads(dgk), i)

    return _rowwise(body, name="qk_bwd", nrows=S, tile=ROW_TILE,
                    row_ins=[(a, D_ATTN, 0) for a in (*dqs, *dks, *dvs)]
                    + [(proj, D_ATTN, 0), (proj, D_ATTN, 1), (pos_col, 1, 0)],
                    full_ins=[gq128, gk128, inv_tab], row_outs=[(D_ATTN, BF)] * 3,
                    acc_outs=[((1, LANES), F32)] * 2)


def _ssd_post_bwd(y_ssd, xbc, proj, dmix, dskip_b, g_ssm):
    def body(i, n, y_ref, xs_ref, z_ref, do_ref, d_ref, g_ref, dy_ref, dz_ref, dxs_ref, dd_ref, dg_ref):
        z, xs = z_ref[...], xs_ref[...]
        sg = _sigmoid(z)
        gate = z * sg
        y = y_ref[...] + d_ref[...] * xs
        dy2, dg = _seg_rms_bwd(y * gate, g_ref[...], do_ref[...], 256)
        dy = dy2 * gate
        dy_ref[...] = dy.astype(BF)
        dz_ref[...] = (dy2 * y * (sg * (1.0 + z * (1.0 - sg)))).astype(BF)
        dxs_ref[...] = dy * d_ref[...]
        _accum(dd_ref, jnp.sum(dy * xs, axis=0, keepdims=True), i)
        _accum(dg_ref, dg, i)

    return _rowwise(body, name="ssd_post_bwd", nrows=S, tile=ROW_TILE,
                    row_ins=[(y_ssd, D_SSM, 0), (xbc, D_SSM, 0), (proj, D_SSM, 3), (dmix, D_SSM, 1)],
                    full_ins=[dskip_b, g_ssm], row_outs=[(D_SSM, BF), (D_SSM, BF), (D_SSM, F32)],
                    acc_outs=[((1, D_SSM), F32), ((1, D_SSM), F32)])


def _ssd_bwd(xbc, dy, h_all, a_b, dt_b, at_r, dt_r):
    def kern(c_ref, b_ref, x_ref, dy_ref, hall_ref, ab_ref, abp_ref, dtb_ref, at_ref, dt_ref,
             dc_ref, daq_ref, dx_ref, db_ref, dak_ref, ddt_ref, ddtb_ref, dh_scr, carry_scr):
        i = pl.program_id(1)

        @pl.when(i == 0)
        def _():
            dh_scr[...] = jnp.zeros_like(dh_scr)
            carry_scr[...] = jnp.zeros_like(carry_scr)

        ci, bi, x = c_ref[...].astype(BF), b_ref[...].astype(BF), x_ref[...]
        dyv = dy_ref[...].astype(F32)
        ab, dtb = ab_ref[...], dtb_ref[...]
        a_prev = jnp.where(i == NB - 1, 0.0, abp_ref[BLK - 1:BLK, :])
        a_end = ab[BLK - 1:BLK, :]
        alpha = jnp.exp(ab - a_prev)
        e_end = jnp.exp(a_end - ab)
        beta = e_end * dtb
        t_all = alpha[BLK - 1:BLK, :]
        hc = hall_ref[...]
        hcb = hc.astype(BF)
        dh_next = dh_scr[...]
        dhb = dh_next.astype(BF)
        half0 = lax.broadcasted_iota(jnp.int32, (BLK, LANES), 1) < HEAD
        hms = (half0, jnp.logical_not(half0))
        causal = lax.broadcasted_iota(jnp.int32, (BLK, BLK), 1) <= lax.broadcasted_iota(jnp.int32, (BLK, BLK), 0)
        row = lax.broadcasted_iota(jnp.int32, (BLK, LANES), 0)

        cb = _dot(ci, bi, NT)
        at, dtj = at_ref[...], dt_ref[...]
        dcb = jnp.zeros((BLK, BLK), F32)
        dx = jnp.zeros((BLK, LANES), F32)
        rows = []
        for hd in range(2):
            dym = jnp.where(hms[hd], dyv, 0.0).astype(BF)
            lm = _ssd_decay(ab[:, HEAD * hd:HEAD * hd + 1], at[hd:hd + 1, :], causal)
            dth = dtj[hd:hd + 1, :]
            dw = _dot(dym, jnp.where(hms[hd], x, 0.0).astype(BF), NT)
            g = dw * cb * lm
            dx = dx + _dot((cb * lm * dth).astype(BF), dym, TN)
            gcol = jnp.sum(g, axis=0, keepdims=True)
            ddt_ref[hd:hd + 1, :] = gcol
            dak_ref[hd:hd + 1, :] = gcol * dth
            rows.append(jnp.sum(g * dth, axis=1, keepdims=True))
            dcb = dcb + dw * lm * dth
        dcb = dcb.astype(BF)

        g1 = (alpha * dyv).astype(BF)
        dalpha = dyv * _dot(ci, hcb, NN)
        g2 = _dot(bi, dhb, NN)
        dbeta = x * g2
        bx = (beta * x).astype(BF)
        dc_ref[...] = _dot(dcb, bi, NN) + _dot(g1, hcb, NT)
        db_ref[...] = _dot(dcb, ci, TN) + _dot(bx, dhb, NT)
        dx_ref[...] = dx + beta * g2
        ddtb_ref[...] = _seg_sum(e_end * dbeta, HEAD)
        ada, bdb = alpha * dalpha, beta * dbeta
        t_dt = t_all * jnp.sum(dh_next * hc, axis=0, keepdims=True)
        end_term = _seg_sum(jnp.broadcast_to(jnp.sum(bdb, axis=0, keepdims=True) + t_dt, (8, LANES)), HEAD)[0:1]
        prev_term = _seg_sum(jnp.broadcast_to(jnp.sum(ada, axis=0, keepdims=True) + t_dt, (8, LANES)), HEAD)[0:1]
        daq = jnp.where(half0, rows[0], rows[1]) + _seg_sum(ada - bdb, HEAD)
        daq_ref[...] = daq + jnp.where(row == BLK - 1, end_term - carry_scr[0:1, :], 0.0)
        carry_scr[...] = jnp.broadcast_to(prev_term, carry_scr.shape)
        dh_scr[...] = t_all * dh_next + _dot(ci, g1, TN)

    npair = D_SSM // LANES
    rev = lambda i: NB - 1 - i
    rowvec = pl.BlockSpec((None, None, 2, BLK), lambda p, i: (p, rev(i), 0, 0))
    tile = pl.BlockSpec((BLK, LANES), lambda p, i: (rev(i), p))
    prev_tile = pl.BlockSpec((BLK, LANES), lambda p, i: (jnp.maximum(rev(i) - 1, 0), p))
    return pl.pallas_call(
        kern, name="ssd_bwd", grid=(npair, NB),
        in_specs=[pl.BlockSpec((BLK, LANES), lambda p, i: (rev(i), 12 + p // 2)),
                  pl.BlockSpec((BLK, LANES), lambda p, i: (rev(i), 8 + p // 2)),
                  tile, tile, tile, tile, prev_tile, tile, rowvec, rowvec],
        out_specs=[tile, tile, tile, tile, rowvec, rowvec, tile],
        out_shape=[jax.ShapeDtypeStruct((S, D_SSM), F32)] * 4
        + [jax.ShapeDtypeStruct((npair, NB, 2, BLK), F32)] * 2 + [jax.ShapeDtypeStruct((S, D_SSM), F32)],
        scratch_shapes=[pltpu.VMEM((BLK, LANES), F32), pltpu.VMEM((8, LANES), F32)],
        compiler_params=_params("parallel", "arbitrary"))(xbc, xbc, xbc, dy, h_all, a_b, a_b, dt_b, at_r, dt_r)


def _ssd_prep_bwd(daq, dak, ddt_row, ddt_lane, dt, proj, dt_bias128, a_log128):
    def body(i, n, daq_ref, dak_ref, dd_ref, dd2_ref, dt_ref, raw_ref, b_ref, al_ref, draw_ref, dal_ref, db_ref,
             carry):
        @pl.when(i == 0)
        def _():
            carry[...] = jnp.zeros_like(carry)

        a = -jnp.exp(al_ref[...])
        tri = (lax.broadcasted_iota(jnp.int32, (BLK, BLK), 0) <= lax.broadcasted_iota(jnp.int32, (BLK, BLK), 1))
        rev = _dot3(tri.astype(BF), daq_ref[...] - dak_ref[...]) + carry[0:1, :]
        carry[...] = jnp.broadcast_to(rev[0:1, :], carry.shape)
        dtv = dt_ref[...]
        draw = (dd_ref[...] + dd2_ref[...] + a * rev) * _sigmoid(raw_ref[...] + b_ref[...])
        draw_ref[...] = draw.astype(BF)
        _accum(dal_ref, jnp.sum(dtv * rev, axis=0, keepdims=True) * a, i)
        _accum(db_ref, jnp.sum(draw, axis=0, keepdims=True), i)

    return _rowwise(body, name="ssd_prep_bwd", nrows=S, tile=BLK, reverse=True,
                    row_ins=[(daq, LANES, 0), (dak, LANES, 0), (ddt_row, LANES, 0), (ddt_lane, LANES, 0), (dt, LANES, 0),
                             (proj, LANES, (D_INP - LANES) // LANES)],
                    full_ins=[dt_bias128, a_log128], row_outs=[(LANES, BF)],
                    acc_outs=[((1, LANES), F32), ((1, LANES), F32)], scratch=[pltpu.VMEM((8, LANES), F32)])


def _conv_bwd(proj, conv_w, conv_b, d1, d2, map1, map2, col0, ntiles, name):
    base = (4 * D_ATTN + col0) // LANES

    def kern(x_ref, w_ref, b_ref, d1_ref, d2_ref, dx_ref, dw_ref, db_ref):
        x = x_ref[...]
        c, shifted = _conv_taps(x, w_ref)
        c = c + b_ref[...]
        sg = _sigmoid(c)
        dc = (d1_ref[...] + d2_ref[...]) * (sg * (1.0 + c * (1.0 - sg)))
        rows = lax.broadcasted_iota(jnp.int32, x.shape, 0)
        dx = jnp.zeros_like(x)
        for w in range(CONV_W):
            k = CONV_W - 1 - w
            up = dc if k == 0 else jnp.where(rows < S - k, pltpu.roll(dc, S - k, 0), 0.0)
            dx = dx + up * w_ref[w:w + 1, :]
            dw_ref[w:w + 1, :] = jnp.sum(dc * shifted[w], axis=0, keepdims=True)
        dx_ref[...] = dx.astype(BF)
        db_ref[...] = jnp.sum(dc, axis=0, keepdims=True)

    c0 = col0 // LANES
    return pl.pallas_call(
        kern, name=name, grid=(ntiles,),
        in_specs=[pl.BlockSpec((S, LANES), lambda c: (0, base + c)),
                  pl.BlockSpec((CONV_W, LANES), lambda c: (0, c0 + c)),
                  pl.BlockSpec((1, LANES), lambda c: (0, c0 + c)),
                  pl.BlockSpec((S, LANES), lambda c: (0, map1(c))),
                  pl.BlockSpec((S, LANES), lambda c: (0, map2(c)))],
        out_specs=[pl.BlockSpec((S, LANES), lambda c: (0, c)), pl.BlockSpec((CONV_W, LANES), lambda c: (0, c)),
                   pl.BlockSpec((1, LANES), lambda c: (0, c))],
        out_shape=[jax.ShapeDtypeStruct((S, ntiles * LANES), BF), jax.ShapeDtypeStruct((CONV_W, ntiles * LANES), F32),
                   jax.ShapeDtypeStruct((1, ntiles * LANES), F32)],
        compiler_params=_params("parallel"))(proj, conv_w, conv_b, d1, d2)


def _adamw(w, m, v, parts, name):
    r, c = w.shape
    n_parts = parts.shape[0]
    tile = r if r <= 512 else (128 if c > 1024 else 256)
    assert r % tile == 0
    c1 = 1.0 - ADAM_B1 ** ADAM_STEP
    c2 = 1.0 - ADAM_B2 ** ADAM_STEP

    def kern(w_ref, m_ref, v_ref, p_ref, g_ref, d_ref, mo_ref, vo_ref):
        g = p_ref[0].astype(F32)
        for k in range(1, n_parts):
            g = g + p_ref[k].astype(F32)
        mn = ADAM_B1 * m_ref[...] + (1.0 - ADAM_B1) * g
        vn = ADAM_B2 * v_ref[...] + (1.0 - ADAM_B2) * (g * g)
        g_ref[...] = g
        mo_ref[...] = mn
        vo_ref[...] = vn
        d_ref[...] = -ADAM_LR * ((mn / c1) / (jnp.sqrt(vn / c2) + ADAM_EPS) + ADAM_WD * w_ref[...])

    blk = pl.BlockSpec((tile, c), lambda i: (i, 0))
    return pl.pallas_call(
        kern, name=name, grid=(r // tile,),
        in_specs=[blk, blk, blk, pl.BlockSpec((n_parts, tile, c), lambda i: (0, i, 0))],
        out_specs=[blk] * 4, out_shape=[jax.ShapeDtypeStruct((r, c), F32)] * 4,
        compiler_params=_params("parallel"))(w, m, v, parts)


MESH = pl.DeviceIdType.MESH
ANY = pl.BlockSpec(memory_space=pl.ANY)


def _put_own(gathered, shard):
    me = 4 * lax.axis_index("x") + 2 * lax.axis_index("y") + lax.axis_index("c")
    return lax.dynamic_update_slice(gathered, shard[None], (me,) + (0,) * shard.ndim)


def _all_gather(arrs, name, after=None):
    na = len(arrs)
    n_after = 0 if after is None else 1

    def kern(*refs):
        ins, outs = refs[:na], refs[na + n_after:2 * na + n_after]
        send_sems, recv_sems = refs[2 * na + n_after:]
        x, y, c = lax.axis_index("x"), lax.axis_index("y"), lax.axis_index("c")
        me, sibling = (x, y, c), (x, y, 1 - c)
        chips = [(1 - x, y), (x, 1 - y), (1 - x, 1 - y)]

        def copy(a, k, block, to, src=None):
            px, py, pc = block
            dst = outs[a].at[4 * px + 2 * py + pc]
            return pltpu.make_async_remote_copy(
                src_ref=dst if src is None else src, dst_ref=dst, send_sem=send_sems.at[a, k],
                recv_sem=recv_sems.at[a, k], device_id=to, device_id_type=MESH)

        first = []
        for a in range(na):
            first.append(copy(a, 0, me, sibling, src=ins[a]))
            first += [copy(a, 1 + j, me, (*chip, c), src=ins[a]) for j, chip in enumerate(chips)]
        for cp in first:
            cp.start()
        passed = []
        for a in range(na):
            for j, chip in enumerate(chips):
                copy(a, 1 + j, (*chip, c), me).wait_recv()
                fwd = copy(a, 4 + j, (*chip, c), sibling)
                fwd.start()
                passed.append(fwd)
        for a in range(na):
            copy(a, 0, sibling, me).wait_recv()
            for j, chip in enumerate(chips):
                copy(a, 4 + j, (*chip, 1 - c), me).wait_recv()
        for cp in first + passed:
            cp.wait_send()

    outs = pl.pallas_call(
        kern, name=name, in_specs=[ANY] * (na + n_after), out_specs=[ANY] * na,
        out_shape=[jax.ShapeDtypeStruct((N_DEV,) + a.shape, a.dtype) for a in arrs],
        scratch_shapes=[pltpu.SemaphoreType.DMA((na, 7)), pltpu.SemaphoreType.DMA((na, 7))],
    )(*arrs, *([] if after is None else [after]))
    return [_put_own(o, a) for o, a in zip(outs, arrs)]


HBM = pl.BlockSpec(memory_space=pltpu.HBM)
SEM = pl.BlockSpec(memory_space=pltpu.SEMAPHORE)
EFFECT = pltpu.SideEffectType.DATAFLOW_SIDE_EFFECTING
N_CHIP = 4


def _in_hbm(a):
    return pltpu.with_memory_space_constraint(a, pltpu.HBM)


def _chip_copies(kind, src_refs, land_refs, send_sems, recv_sems):
    x, y, c = lax.axis_index("x"), lax.axis_index("y"), lax.axis_index("c")
    cps = []
    for a in range(len(src_refs)):
        for j, (px, py) in enumerate([(1 - x, y), (x, 1 - y), (1 - x, 1 - y)]):
            if kind == "gather":
                src, dst = src_refs[a], land_refs[a].at[4 * x + 2 * y + c]
            else:
                src, dst = src_refs[a].at[2 * px + py], land_refs[a].at[2 * x + y]
            cps.append(pltpu.make_async_remote_copy(
                src_ref=src, dst_ref=dst, send_sem=send_sems.at[3 * a + j], recv_sem=recv_sems.at[3 * a + j],
                device_id=(px, py, c), device_id_type=MESH))
    return cps


def _exchange_start(kind, srcs, lands, after, name):
    na = len(srcs)
    n_after = 0 if after is None else 1

    def kern(*refs):
        src_refs, land_refs = refs[:na], refs[na:2 * na]
        send_sems, recv_sems = refs[2 * na + n_after], refs[2 * na + n_after + 1]
        token = refs[-1]
        for cp in _chip_copies(kind, src_refs, land_refs, send_sems, recv_sems):
            cp.start()
        token[...] = jnp.zeros_like(token)

    bufs = list(srcs) + list(lands)
    res = pl.pallas_call(
        kern, name=name,
        out_shape=(pltpu.SemaphoreType.DMA((3 * na,)), pltpu.SemaphoreType.DMA((3 * na,)))
        + tuple(pltpu.HBM(b.shape, b.dtype) for b in bufs) + (jax.ShapeDtypeStruct((8, LANES), F32),),
        in_specs=[HBM] * (2 * na) + [ANY] * n_after,
        out_specs=(SEM, SEM) + (HBM,) * (2 * na) + (pl.BlockSpec(memory_space=pltpu.VMEM),),
        input_output_aliases={i: 2 + i for i in range(2 * na)},
        compiler_params=pltpu.CompilerParams(has_side_effects=EFFECT),
    )(*[_in_hbm(b) for b in bufs], *([] if after is None else [after]))
    return (res[0], res[1]), list(res[2:2 + na]), list(res[2 + na:2 + 2 * na]), res[-1]


def _exchange_wait(kind, sems, srcs, lands, after, name):
    na = len(srcs)

    def kern(*refs):
        src_refs, land_refs = refs[:na], refs[na:2 * na]
        send_sems, recv_sems = refs[2 * na], refs[2 * na + 1]
        for cp in _chip_copies(kind, src_refs, land_refs, send_sems, recv_sems):
            cp.wait_send()
            cp.wait_recv()

    bufs = list(srcs) + list(lands)
    res = pl.pallas_call(
        kern, name=name, out_shape=tuple(pltpu.HBM(b.shape, b.dtype) for b in bufs),
        in_specs=[HBM] * (2 * na) + [SEM, SEM, ANY], out_specs=(HBM,) * (2 * na),
        input_output_aliases={i: i for i in range(2 * na)},
        compiler_params=pltpu.CompilerParams(has_side_effects=EFFECT),
    )(*bufs, sems[0], sems[1], after)
    return list(res[:na]), list(res[na:])


def _gather_finish(shards, lands, name):
    na = len(shards)

    def kern(*refs):
        ins, outs = refs[:na], refs[2 * na:3 * na]
        send_sems, recv_sems = refs[3 * na:]
        x, y, c = lax.axis_index("x"), lax.axis_index("y"), lax.axis_index("c")
        chips = [(1 - x, y), (x, 1 - y), (1 - x, 1 - y)]

        def copy(a, k, slot, pc, src=None):
            dst = outs[a].at[slot + pc]
            return pltpu.make_async_remote_copy(
                src_ref=dst if src is None else src, dst_ref=dst, send_sem=send_sems.at[a, k],
                recv_sem=recv_sems.at[a, k], device_id=(x, y, 1 - c), device_id_type=MESH)

        sends = []
        for a in range(na):
            sends.append(copy(a, 0, 4 * x + 2 * y, c, src=ins[a]))
            sends += [copy(a, 1 + j, 4 * px + 2 * py, c) for j, (px, py) in enumerate(chips)]
        for cp in sends:
            cp.start()
        for a in range(na):
            copy(a, 0, 4 * x + 2 * y, 1 - c).wait_recv()
            for j, (px, py) in enumerate(chips):
                copy(a, 1 + j, 4 * px + 2 * py, 1 - c).wait_recv()
        for cp in sends:
            cp.wait_send()

    return pl.pallas_call(
        kern, name=name, in_specs=[ANY] * (2 * na), out_specs=[ANY] * na,
        out_shape=[jax.ShapeDtypeStruct(l.shape, l.dtype) for l in lands],
        input_output_aliases={na + a: a for a in range(na)},
        scratch_shapes=[pltpu.SemaphoreType.DMA((na, 4)), pltpu.SemaphoreType.DMA((na, 4))])(*shards, *lands)


def _pair_exchange(parts, name):
    na = len(parts)

    def kern(*refs):
        ins, got = refs[:na], refs[na:2 * na]
        send_sems, recv_sems = refs[2 * na:]
        x, y, c = lax.axis_index("x"), lax.axis_index("y"), lax.axis_index("c")
        sends = []
        for a in range(na):
            for q in range(N_CHIP):
                sends.append(pltpu.make_async_remote_copy(
                    src_ref=ins[a].at[2 * q + 1 - c], dst_ref=got[a].at[q], send_sem=send_sems.at[a, q],
                    recv_sem=recv_sems.at[a, q], device_id=(x, y, 1 - c), device_id_type=MESH))
        for cp in sends:
            cp.start()
        for cp in sends:
            cp.wait()

    return pl.pallas_call(
        kern, name=name, in_specs=[ANY] * na, out_specs=[ANY] * na,
        out_shape=[jax.ShapeDtypeStruct((N_CHIP,) + p.shape[1:], p.dtype) for p in parts],
        scratch_shapes=[pltpu.SemaphoreType.DMA((na, N_CHIP)), pltpu.SemaphoreType.DMA((na, N_CHIP))])(*parts)


def _pair_sum(parts, got, name):
    _, r, cdim = parts.shape
    tile = min(r, ROW_TILE)
    core = lax.axis_index("c").astype(jnp.int32).reshape(1)

    def kern(c_ref, a_ref, b_ref, q_ref, l_ref):
        s = (a_ref[...].astype(F32) + b_ref[...].astype(F32)).astype(BF)
        q_ref[...] = s
        l_ref[...] = s

    blk = pl.BlockSpec((None, tile, cdim), lambda q, i, c_ref: (q, i, 0))
    out = jax.ShapeDtypeStruct((N_CHIP, r, cdim), BF)
    return pl.pallas_call(
        kern, name=name, out_shape=[out, out],
        grid_spec=pltpu.PrefetchScalarGridSpec(
            num_scalar_prefetch=1, grid=(N_CHIP, r // tile),
            in_specs=[pl.BlockSpec((None, tile, cdim), lambda q, i, c_ref: (2 * q + c_ref[0], i, 0)), blk],
            out_specs=[blk, blk]),
        compiler_params=_params("parallel", "parallel"))(core, parts, got)


W_IN_COLS = D_IN // N_DEV


def _w_in_from_blocks(w8):
    def kern(x_ref, o_ref):
        for j in range(N_DEV):
            o_ref[:, W_IN_COLS * j:W_IN_COLS * (j + 1)] = x_ref[j]
        o_ref[:, D_IN:] = jnp.zeros((ROW_TILE, D_INP - D_IN), o_ref.dtype)

    return pl.pallas_call(
        kern, name="w_in_from_blocks", grid=(D // ROW_TILE,),
        in_specs=[pl.BlockSpec((N_DEV, ROW_TILE, W_IN_COLS), lambda i: (0, i, 0))],
        out_specs=pl.BlockSpec((ROW_TILE, D_INP), lambda i: (i, 0)),
        out_shape=jax.ShapeDtypeStruct((D, D_INP), w8.dtype), compiler_params=_params("parallel"))(w8)


def _w_in_to_blocks(g):
    def kern(x_ref, o_ref):
        for j in range(N_DEV):
            o_ref[j] = x_ref[:, W_IN_COLS * j:W_IN_COLS * (j + 1)]

    return pl.pallas_call(
        kern, name="w_in_to_blocks", grid=(D // ROW_TILE,),
        in_specs=[pl.BlockSpec((ROW_TILE, D_INP), lambda i: (i, 0))],
        out_specs=pl.BlockSpec((N_DEV, ROW_TILE, W_IN_COLS), lambda i: (0, i, 0)),
        out_shape=jax.ShapeDtypeStruct((N_DEV, D, W_IN_COLS), g.dtype), compiler_params=_params("parallel"))(g)


def _pad_lanes(v, width=LANES):
    return jnp.pad(v, ((0, 0), (0, width - v.shape[1])))


def _row_layout(t16):
    return t16.T.reshape(N_HEADS // 2, 2, NB, BLK).transpose(0, 2, 1, 3)


def _row_layout_inv(r):
    return r.transpose(0, 2, 1, 3).reshape(N_HEADS, S).T


SMALL = (("g_mix", D), ("g_q", HEAD), ("g_k", HEAD), ("g_attn_out", D_ATTN), ("conv_b", D_CONV),
         ("dt_bias", 16), ("a_log", 16), ("d_skip", 16), ("g_ssm_out", D_SSM), ("g_cross", D),
         ("g_mem", D), ("g_cq", CROSS_HEAD), ("g_ck", CROSS_HEAD), ("g_mlp", D))
SMALL_ROWS = -(-sum(-(-w // LANES) for _, w in SMALL) // 8) * 8


def _pack_small(vals):
    rows = [_pad_lanes(vals[n], -(-w // LANES) * LANES).reshape(-1, LANES) for n, w in SMALL]
    packed = jnp.concatenate(rows, axis=0)
    return jnp.pad(packed, ((0, SMALL_ROWS - packed.shape[0]), (0, 0)))


def _unpack_small(packed):
    out, r = {}, 0
    for n, w in SMALL:
        nr = -(-w // LANES)
        out[n] = packed[r:r + nr].reshape(1, nr * LANES)[:, :w]
        r += nr
    return out


BIG = ("w_in", "w_out", "w_cq", "w_ckv", "w_co", "w_up", "w_down")
WEIGHTS = ("g_mix", "w_in", "g_q", "g_k", "g_attn_out", "conv_w", "conv_b", "dt_bias", "a_log", "d_skip",
           "g_ssm_out", "w_out", "g_cross", "g_mem", "w_cq", "w_ckv", "g_cq", "g_ck", "w_co", "g_mlp", "w_up", "w_down")


REST = ("w_out", "w_cq", "w_ckv", "w_co", "w_up", "w_down")


class _Overlap:
    def __init__(self, shards, after):
        lands = [_put_own(lax.empty((N_DEV,) + s.shape, s.dtype), s) for s in shards]
        self.gather = _exchange_start("gather", shards, lands, after, "ag_rest_start")
        self.token = self.gather[3]
        self.groups = []

    def rest(self, after):
        sems, srcs, lands, _ = self.gather
        srcs, lands = _exchange_wait("gather", sems, srcs, lands, after, "ag_rest_wait")
        wf = dict(zip(REST, _gather_finish(srcs, lands, "ag_rest_finish")))
        for n in ("w_out", "w_cq", "w_ckv", "w_down"):
            wf[n] = wf[n].reshape(-1, wf[n].shape[-1])
        return wf

    def emit(self, grads):
        names, tag = list(grads), "_%d" % len(self.groups)
        got = _pair_exchange([grads[n] for n in names], "rs_pair" + tag)
        sums = [_pair_sum(grads[n], b, "rs_sum_" + n) for n, b in zip(names, got)]
        sems, srcs, lands, token = _exchange_start("scatter", [q for q, _ in sums], [l for _, l in sums], None,
                                                   "rs_chip_start" + tag)
        self.groups.append((names, sems, srcs, lands))
        return token

    def finish(self, gi, after):
        names, sems, srcs, lands = self.groups[gi]
        _, lands = _exchange_wait("scatter", sems, srcs, lands, after, "rs_chip_wait_%d" % gi)
        return dict(zip(names, lands))


def _tie(v, token):
    return v if token is None else v + token[0:1, 0:1]


def _local_step(x, mem, pos_col, target, p, wf, hooks=None):
    p = dict(p)
    inv = np.zeros((1, LANES), np.float32)
    freq = (ROPE_THETA ** (-2.0 * np.arange(ROT // 2, dtype=np.float32) / ROT)).astype(np.float32)
    for l in range(LANES):
        if l % HEAD < ROT:
            inv[0, l] = freq[(l % HEAD) % (ROT // 2)]
    inv_tab = jnp.asarray(inv)
    gq128, gk128 = jnp.tile(p["g_q"], (1, 2)), jnp.tile(p["g_k"], (1, 2))
    dtb128, alog128 = _pad_lanes(p["dt_bias"]), _pad_lanes(p["a_log"])
    dskip_b = jnp.repeat(p["d_skip"], HEAD, axis=1)
    conv_w, conv_b = wf["conv_w"], p["conv_b"]

    h = _rms_fwd(x, p["g_mix"], "rms_mix")
    proj = _matmul(h, wf["w_in"], mode="nn", name="mm_in", tm=1024, tn=896, tk=D)
    qr, kr = _qk_prep(proj, pos_col, gq128, gk128, inv_tab)
    dilations = (1, 4, 16)
    fwd = [_attn_fwd(qr, kr, proj, d, "attn_fwd_d%d" % d) for d in dilations]
    attn, lse, attn_n = _attn_merge([o for o, _ in fwd], [l for _, l in fwd], p["g_attn_out"])

    xbc = _conv_fwd(proj, conv_w, conv_b)
    dt, a_cs = _ssd_prep(proj, dtb128, alog128)
    a_b = jnp.repeat(a_cs[:, :N_HEADS], HEAD, axis=1)
    dt_b = jnp.repeat(dt[:, :N_HEADS], HEAD, axis=1)
    at_r, dt_r = _row_layout(a_cs[:, :N_HEADS]), _row_layout(dt[:, :N_HEADS])
    y_ssd, h_all = _ssd_fwd(xbc, a_b, dt_b, at_r, dt_r)
    ssm_n = _ssd_post(y_ssd, xbc, proj, dskip_b, p["g_ssm_out"])

    if hooks is not None:
        wf = {**wf, **hooks.rest(ssm_n)}
    mix = jnp.concatenate([attn_n, ssm_n], axis=1)
    x1 = _matmul(mix, wf["w_out"], mode="nn", name="mm_out", tm=1024, tn=1024, tk=D,
                 extras=(x,), epilogue=lambda acc, r: (acc + r,))
    hc = _rms_fwd(x1, p["g_cross"], "rms_cross")
    memh = _rms_fwd(mem, p["g_mem"], "rms_mem")
    qc = _matmul(hc, wf["w_cq"], mode="nn", name="mm_cq", tm=1024, tn=512, tk=D)
    kv = _matmul(memh, wf["w_ckv"], mode="nn", name="mm_ckv", tm=N_MEM, tn=1024, tk=D)
    oc = _cross_fwd(qc, kv, p["g_cq"], p["g_ck"])
    x2 = _matmul(oc, wf["w_co"], mode="nn", name="mm_co", tm=1024, tn=256, tk=512, b_cb=256,
                 extras=(x1,), epilogue=lambda acc, r: (acc + r,))
    hm = _rms_fwd(x2, p["g_mlp"], "rms_mlp")

    def up_epi(acc):
        ru = jnp.maximum(acc, 0.0)
        return ru * ru, 2.0 * ru

    act, relu2 = _matmul(hm, wf["w_up"], mode="nn", name="mm_up", tm=1024, tn=1024, tk=D, b_cb=1024,
                         out_dtypes=(BF, BF), epilogue=up_epi)

    def loss_epi(acc, r, t):
        d = (acc + r - t) * (1.0 / D)
        return d, d

    dy, dyb = _matmul(act, wf["w_down"], mode="nn", name="mm_down", tm=512, tn=1024, tk=2048, extras=(x2, target),
                      out_dtypes=(F32, BF), epilogue=loss_epi)
    loss_part = _loss_sum(dy)[0, 0] * (0.5 * D)

    gb, gs = {}, {}
    du = _matmul(dyb, wf["w_down"], mode="nt", name="mm_down_dx", tm=1024, tn=1024, tk=D, extras=(relu2,),
                 out_dtypes=(BF,), epilogue=lambda acc, r: (acc * r.astype(F32),))
    gb["w_down"] = _matmul(act, dyb, mode="tn", name="mm_down_dw", tm=1024, tn=1024, tk=S,
                           out_dtypes=(BF,)).reshape(N_DEV, D_FF // N_DEV, D)
    gb["w_up"] = _matmul(hm, du, mode="tn", name="mm_up_dw", tm=1024, tn=1024, tk=S, out_dtypes=(BF,), out_cb=1024)
    if hooks is not None:
        p["g_mlp"] = _tie(p["g_mlp"], hooks.emit({n: gb[n] for n in ("w_down", "w_up")}))
    dhm = _matmul(du, wf["w_up"], mode="nt", name="mm_up_dx", tm=1024, tn=1024, tk=1024, b_cb=1024)
    dx2, dx2b, gs["g_mlp"] = _rms_bwd_call(x2, p["g_mlp"], dhm, dy, "rms_mlp_bwd")

    doc = _matmul(dx2b, wf["w_co"], mode="nt", name="mm_co_dx", tm=1024, tn=512, tk=256, b_cb=256)
    gb["w_co"] = _matmul(oc, dx2b, mode="tn", name="mm_co_dw", tm=512, tn=256, tk=S, out_dtypes=(BF,), out_cb=256)
    dqc, dkn, dvc, gs["g_cq"] = _cross_bwd(qc, doc, kv, p["g_cq"], p["g_ck"])
    dkv, gs["g_ck"] = _cross_kv_bwd(kv, dkn, dvc, p["g_ck"])
    gb["w_cq"] = _matmul(hc, dqc, mode="tn", name="mm_cq_dw", tm=1024, tn=512, tk=S,
                         out_dtypes=(BF,)).reshape(N_DEV, D // N_DEV, D_CROSS)
    dhc = _matmul(dqc, wf["w_cq"], mode="nt", name="mm_cq_dx", tm=1024, tn=1024, tk=512)
    gb["w_ckv"] = _matmul(memh, dkv, mode="tn", name="mm_ckv_dw", tm=1024, tn=1024, tk=N_MEM,
                          out_dtypes=(BF,)).reshape(N_DEV, D // N_DEV, 2 * D_CROSS)
    dmemh = _matmul(dkv, wf["w_ckv"], mode="nt", name="mm_ckv_dx", tm=N_MEM, tn=1024, tk=1024)
    (gs["g_mem"],) = _rms_bwd_call(mem, p["g_mem"], dmemh, None, "rms_mem_bwd")
    dx1, dx1b, gs["g_cross"] = _rms_bwd_call(x1, p["g_cross"], dhc, dx2, "rms_cross_bwd")

    dmix = _matmul(dx1b, wf["w_out"], mode="nt", name="mm_out_dx", tm=1024, tn=1024, tk=D)
    gb["w_out"] = _matmul(mix, dx1b, mode="tn", name="mm_out_dw", tm=1024, tn=1024, tk=S,
                          out_dtypes=(BF,)).reshape(N_DEV, D // N_DEV, D)
    if hooks is not None:
        p["g_attn_out"] = _tie(p["g_attn_out"], hooks.emit({n: gb[n] for n in ("w_co", "w_cq", "w_ckv", "w_out")}))

    dattn, delta, gs["g_attn_out"] = _attn_merge_bwd(dmix, attn, p["g_attn_out"])
    bwd = [_attn_bwd(qr, kr, proj, dattn, lse, delta, d, "attn_bwd_d%d" % d) for d in dilations]
    dq_pre, dk_pre, dv_pre, dgq, dgk = _qk_bwd([t[0] for t in bwd], [t[1] for t in bwd], [t[2] for t in bwd],
                                               proj, pos_col, gq128, gk128, inv_tab)
    gs["g_q"], gs["g_k"] = dgq[:, :HEAD], dgk[:, :HEAD]

    dy_ssd, dz, dxs_skip, dd_lane, gs["g_ssm_out"] = _ssd_post_bwd(y_ssd, xbc, proj, dmix, dskip_b, p["g_ssm_out"])
    gs["d_skip"] = dd_lane.reshape(N_HEADS, HEAD).sum(axis=1).reshape(1, N_HEADS)
    dc_pair, daq_b, dx_ssd, db_pair, dak_r, ddt_r, ddt_b = _ssd_bwd(xbc, dy_ssd, h_all, a_b, dt_b, at_r, dt_r)
    daq = _pad_lanes(daq_b[:, ::HEAD])
    dak = _pad_lanes(_row_layout_inv(dak_r))
    ddt_direct = _pad_lanes(_row_layout_inv(ddt_r))
    ddt_raw, dalog, dbias = _ssd_prep_bwd(daq, dak, ddt_direct, _pad_lanes(ddt_b[:, ::HEAD]), dt, proj,
                                          dtb128, alog128)
    gs["a_log"], gs["dt_bias"] = dalog[:, :N_HEADS], dbias[:, :N_HEADS]
    same = lambda c: c
    dxbc_x, dcw_x, dcb_x = _conv_bwd(proj, conv_w, conv_b, dxs_skip, dx_ssd, same, same, 0, 8, "conv_bwd_x")
    dxbc_b, dcw_b, dcb_b = _conv_bwd(proj, conv_w, conv_b, db_pair, db_pair, lambda c: 2 * c, lambda c: 2 * c + 1,
                                     D_SSM, 4, "conv_bwd_b")
    dxbc_c, dcw_c, dcb_c = _conv_bwd(proj, conv_w, conv_b, dc_pair, dc_pair, lambda c: 2 * c, lambda c: 2 * c + 1,
                                     D_SSM + 512, 4, "conv_bwd_c")
    g_conv_w = jnp.concatenate([dcw_x, dcw_b, dcw_c], axis=1)
    gs["conv_b"] = jnp.concatenate([dcb_x, dcb_b, dcb_c], axis=1)

    dproj = jnp.concatenate([dq_pre, dk_pre, dv_pre, dz, dxbc_x, dxbc_b, dxbc_c, ddt_raw], axis=1)
    dw_in = _matmul(h, dproj, mode="tn", name="mm_in_dw", tm=1024, tn=896, tk=S, out_dtypes=(BF,))
    gb["w_in"] = _w_in_to_blocks(dw_in)
    token = None if hooks is None else hooks.emit({"w_in": gb["w_in"]})
    dh = _matmul(dproj, wf["w_in"], mode="nt", name="mm_in_dx", tm=1024, tn=1024, tk=896, after=token)
    grad_x, _, gs["g_mix"] = _rms_bwd_call(x, p["g_mix"], dh, dx1, "rms_mix_bwd")
    return loss_part, grad_x, gb, gs, g_conv_w


def kernel(x, mem, positions, g_mix, w_in, g_q, g_k, g_attn_out, conv_w, conv_b, dt_bias, a_log, d_skip, g_ssm_out, w_out, g_cross, g_mem, w_cq, w_ckv, g_cq, g_ck, w_co, g_mlp, w_up, w_down, loss_target, m_g_mix, m_w_in, m_g_q, m_g_k, m_g_attn_out, m_conv_w, m_conv_b, m_dt_bias, m_a_log, m_d_skip, m_g_ssm_out, m_w_out, m_g_cross, m_g_mem, m_w_cq, m_w_ckv, m_g_cq, m_g_ck, m_w_co, m_g_mlp, m_w_up, m_w_down, v_g_mix, v_w_in, v_g_q, v_g_k, v_g_attn_out, v_conv_w, v_conv_b, v_dt_bias, v_a_log, v_d_skip, v_g_ssm_out, v_w_out, v_g_cross, v_g_mem, v_w_cq, v_w_ckv, v_g_cq, v_g_ck, v_w_co, v_g_mlp, v_w_up, v_w_down):
    args = dict(locals())
    w = {n: args[n] for n in WEIGHTS}
    m = {n: args["m_" + n] for n in WEIGHTS}
    v = {n: args["v_" + n] for n in WEIGHTS}
    small = {n: w[n] for n, _ in SMALL}
    me = 4 * lax.axis_index("x") + 2 * lax.axis_index("y") + lax.axis_index("c")

    w_in_g, conv_w_g = _all_gather([w["w_in"][0].astype(BF), w["conv_w"][0]], "ag_first")
    wf = {"w_in": _w_in_from_blocks(w_in_g), "conv_w": conv_w_g.transpose(1, 0, 2).reshape(CONV_W, D_CONV)}
    overlap = _Overlap([w[n][0].astype(BF) for n in REST], w_in_g)
    p = dict(small)
    p["g_mix"] = _tie(p["g_mix"], overlap.token)

    loss_part, grad_x, _, gs, g_conv_w = _local_step(
        x[0], mem[0], positions.reshape(S, 1), loss_target[0], p, wf, overlap)
    loss = lax.psum(loss_part, ("x", "y", "c"))

    out = {}
    last = grad_x
    for gi in range(3):
        for n, parts in overlap.finish(gi, last).items():
            res_n = _adamw(w[n][0], m[n][0], v[n][0], parts, "adamw_" + n)
            out[n] = [t.reshape(w[n].shape) for t in res_n]
            last = res_n[0]

    sm_parts, cw_parts = _all_gather([_pack_small(gs), g_conv_w], "ag_small_grads", after=last)
    sm = [_unpack_small(t) for t in _adamw(_pack_small(small), _pack_small({n: m[n] for n, _ in SMALL}),
                                           _pack_small({n: v[n] for n, _ in SMALL}), sm_parts, "adamw_small")]
    for n, _ in SMALL:
        out[n] = [t[n] for t in sm]
    cols = D_CONV // N_DEV
    cw_mine = lax.dynamic_slice_in_dim(cw_parts, me * cols, cols, axis=2)
    out["conv_w"] = [t.reshape(w["conv_w"].shape) for t in
                     _adamw(w["conv_w"][0], m["conv_w"][0], v["conv_w"][0], cw_mine, "adamw_conv_w")]

    res = [loss, grad_x.reshape(x.shape)]
    for k in range(4):
        res += [out[n][k] for n in WEIGHTS]
    return tuple(res)
```

```python
import functools
import math

import numpy as np
import jax
import jax.numpy as jnp
from jax import lax
from jax.experimental import pallas as pl
from jax.experimental.pallas import tpu as pltpu

F32 = jnp.float32
BF = jnp.bfloat16

N_DEV = 8
S = 2048
D = 2048
D_ATTN = 1024
N_HEADS = 16
HEAD = 64
ROT = 16
ROPE_THETA = 500000.0
D_SSM = 1024
D_CONV = 2048
CONV_W = 4
N_MEM = 256
D_CROSS = 512
CROSS_HEAD = 128
D_FF = 8192
D_IN = 6160
D_INP = 6272
EPS = 1e-6
BLK = 128
NB = S // BLK
LANES = 128
NEG = -1e30

ADAM_LR = 0.001
ADAM_B1 = 0.9
ADAM_B2 = 0.999
ADAM_EPS = 1e-08
ADAM_WD = 0.01
ADAM_STEP = 10

VMEM_LIMIT_BYTES = 48 * 1024 * 1024
ROW_TILE = 256


def _params(*sem):
    return pltpu.CompilerParams(dimension_semantics=sem, vmem_limit_bytes=VMEM_LIMIT_BYTES)


def _matmul(a, b, *, mode, name, tm, tn, tk, out_dtypes=(F32,), b_cb=None, out_cb=None,
            extras=(), epilogue=None, after=None):
    if mode == "tn":
        kk, m = a.shape
    else:
        m, kk = a.shape
    if b_cb is None:
        br, bc = b.shape
    else:
        br, bc = b.shape[1], N_DEV * b_cb
    n = br if mode == "nt" else bc
    assert m % tm == 0 and n % tn == 0 and kk % tk == 0, (name, m, n, kk)
    nk = kk // tk
    grid = (m // tm, n // tn, nk)

    if mode == "tn":
        a_spec = pl.BlockSpec((tk, tm), lambda i, j, k: (k, i))
    else:
        a_spec = pl.BlockSpec((tm, tk), lambda i, j, k: (i, k))
    if mode == "nt":
        b_blk, b_idx = (tn, tk), (lambda i, j, k: (j, k))
    else:
        b_blk, b_idx = (tk, tn), (lambda i, j, k: (k, j))
    if b_cb is None:
        b_spec = pl.BlockSpec(b_blk, b_idx)
    else:
        tc = b_blk[1]
        assert b_cb % tc == 0
        per = b_cb // tc

        def b_idx3(i, j, k):
            r, c = b_idx(i, j, k)
            return (c // per, r, c % per)
        b_spec = pl.BlockSpec((None,) + b_blk, b_idx3)
    ex_specs = [pl.BlockSpec((tm, tn), lambda i, j, k: (i, j)) for _ in extras]
    if out_cb is None:
        out_shape = [jax.ShapeDtypeStruct((m, n), dt) for dt in out_dtypes]
        out_specs = [pl.BlockSpec((tm, tn), lambda i, j, k: (i, j)) for _ in out_dtypes]
    else:
        assert out_cb % tn == 0
        pero = out_cb // tn
        out_shape = [jax.ShapeDtypeStruct((N_DEV, m, out_cb), dt) for dt in out_dtypes]
        out_specs = [pl.BlockSpec((None, tm, tn), lambda i, j, k: (j // pero, i, j % pero)) for _ in out_dtypes]
    dn = {"nn": (((1,), (0,)), ((), ())), "nt": (((1,), (1,)), ((), ())), "tn": (((0,), (0,)), ((), ()))}[mode]
    ne, no = len(extras), len(out_dtypes)
    n_after = 0 if after is None else 1

    def kern(a_ref, b_ref, *rest):
        ex_refs, out_refs = rest[:ne], rest[ne + n_after:ne + n_after + no]

        def finish(acc):
            outs = (acc,) if epilogue is None else epilogue(acc, *[r[...] for r in ex_refs])
            for r, o in zip(out_refs, outs):
                r[...] = o.astype(r.dtype)

        part = lax.dot_general(a_ref[...].astype(BF), b_ref[...].astype(BF), dn, preferred_element_type=F32)
        if nk == 1:
            finish(part)
            return
        acc_ref = rest[ne + n_after + no]
        k = pl.program_id(2)

        @pl.when(k == 0)
        def _():
            acc_ref[...] = part

        @pl.when(k > 0)
        def _():
            acc_ref[...] += part

        @pl.when(k == nk - 1)
        def _():
            finish(acc_ref[...])

    res = pl.pallas_call(
        kern, name=name, grid=grid, in_specs=[a_spec, b_spec] + ex_specs + [ANY] * n_after, out_specs=out_specs,
        out_shape=out_shape, scratch_shapes=[pltpu.VMEM((tm, tn), F32)] if nk > 1 else [],
        compiler_params=_params("parallel", "parallel", "arbitrary"),
    )(a, b, *extras, *([] if after is None else [after]))
    return res[0] if no == 1 else tuple(res)


def _rowwise(body, *, name, nrows, tile, row_ins, full_ins=(), row_outs=(), acc_outs=(), scratch=(),
             reverse=False):
    n = nrows // tile

    def ridx(i):
        return (n - 1 - i) if reverse else i

    in_specs, args = [], []
    for arr, width, cb in row_ins:
        in_specs.append(pl.BlockSpec((tile, width), lambda i, cb=cb: (ridx(i), cb)))
        args.append(arr)
    for arr in full_ins:
        in_specs.append(pl.BlockSpec(arr.shape, lambda i, nd=arr.ndim: (0,) * nd))
        args.append(arr)
    out_shape, out_specs = [], []
    for width, dt in row_outs:
        out_shape.append(jax.ShapeDtypeStruct((nrows, width), dt))
        out_specs.append(pl.BlockSpec((tile, width), lambda i: (ridx(i), 0)))
    for shp, dt in acc_outs:
        out_shape.append(jax.ShapeDtypeStruct(shp, dt))
        out_specs.append(pl.BlockSpec(shp, lambda i, nd=len(shp): (0,) * nd))

    def kern(*refs):
        body(pl.program_id(0), n, *refs)

    return pl.pallas_call(kern, name=name, grid=(n,), in_specs=in_specs, out_specs=out_specs,
                          out_shape=out_shape, scratch_shapes=list(scratch),
                          compiler_params=_params("arbitrary"))(*args)


def _accum(ref, val, i):
    @pl.when(i == 0)
    def _():
        ref[...] = val

    @pl.when(i > 0)
    def _():
        ref[...] += val


def _sigmoid(x):
    return 1.0 / (1.0 + jnp.exp(-x))


def _rms_n(x):
    r = lax.rsqrt(jnp.mean(x * x, axis=-1, keepdims=True) + EPS)
    return x * r, r


def _rms_bwd(x, g, dh):
    n, r = _rms_n(x)
    dn = dh * g
    dx = r * (dn - n * jnp.mean(dn * n, axis=-1, keepdims=True))
    return dx, jnp.sum(dh * n, axis=0, keepdims=True)


def _seg_sum(x, seg):
    t, w = x.shape
    tiles = [x[:, LANES * j:LANES * (j + 1)] for j in range(w // LANES)]
    outs = []
    if seg == 64:
        lo = lax.broadcasted_iota(jnp.int32, (t, LANES), 1) < 64
        for xt in tiles:
            s_lo = jnp.sum(jnp.where(lo, xt, 0.0), axis=-1, keepdims=True)
            s_hi = jnp.sum(jnp.where(lo, 0.0, xt), axis=-1, keepdims=True)
            outs.append(jnp.where(lo, s_lo, s_hi))
    else:
        sums = [jnp.sum(xt, axis=-1, keepdims=True) for xt in tiles]
        if seg == 256:
            sums = [sums[2 * (j // 2)] + sums[2 * (j // 2) + 1] for j in range(len(sums))]
        else:
            assert seg == 128
        outs = [jnp.broadcast_to(s, (t, LANES)) for s in sums]
    return outs[0] if len(outs) == 1 else jnp.concatenate(outs, axis=1)


def _seg_rms_n(x, seg):
    r = lax.rsqrt(_seg_sum(x * x, seg) * (1.0 / seg) + EPS)
    return x * r, r


def _seg_rms_bwd(x, g, dy, seg):
    n, r = _seg_rms_n(x, seg)
    dn = dy * g
    dx = r * (dn - n * (_seg_sum(dn * n, seg) * (1.0 / seg)))
    return dx, jnp.sum(dy * n, axis=0, keepdims=True)


def _rope_tables(pos_col, inv_tab, t):
    ang = pos_col.astype(F32) * inv_tab
    idx = lax.broadcasted_iota(jnp.int32, (t, LANES), 1) % HEAD
    cos, sin = jnp.cos(ang), jnp.sin(ang)
    c = jnp.where(idx < ROT, cos, 1.0)
    sg = jnp.where(idx < ROT // 2, -sin, jnp.where(idx < ROT, sin, 0.0))
    return c, sg, idx < ROT // 2


def _rope(x, c, sg, lo):
    partner = jnp.where(lo, pltpu.roll(x, LANES - ROT // 2, 1), pltpu.roll(x, ROT // 2, 1))
    return x * c + partner * sg


def _fold_heads(v):
    v8 = jnp.broadcast_to(v, (8, LANES))
    return (v8 + pltpu.roll(v8, HEAD, 1))[0:1]


def _dot(a, b, dn):
    return lax.dot_general(a, b, (dn, ((), ())), preferred_element_type=F32)


NN = ((1,), (0,))
NT = ((1,), (1,))
TN = ((0,), (0,))


def _dot3(l01, x):
    x1 = x.astype(BF)
    r1 = x - x1.astype(F32)
    x2 = r1.astype(BF)
    x3 = (r1 - x2.astype(F32)).astype(BF)
    return _dot(l01, x1, NN) + _dot(l01, x2, NN) + _dot(l01, x3, NN)


def _rms_fwd(x, g, name):
    rows = x.shape[0]

    def body(i, n, x_ref, g_ref, o_ref):
        nx, _ = _rms_n(x_ref[...])
        o_ref[...] = (nx * g_ref[...]).astype(BF)

    return _rowwise(body, name=name, nrows=rows, tile=min(ROW_TILE, rows), row_ins=[(x, D, 0)],
                    full_ins=[g], row_outs=[(D, BF)])[0]


def _qk_prep(proj, pos_col, gq128, gk128, inv_tab):
    scale = HEAD ** -0.5

    def body(i, n, q_ref, k_ref, p_ref, gq_ref, gk_ref, it_ref, qo_ref, ko_ref):
        t = q_ref.shape[0]
        c, sg, lo = _rope_tables(p_ref[...], it_ref[...], t)
        for j in range(D_ATTN // LANES):
            sl = slice(LANES * j, LANES * (j + 1))
            qn, _ = _seg_rms_n(q_ref[:, sl], HEAD)
            kn, _ = _seg_rms_n(k_ref[:, sl], HEAD)
            qo_ref[:, sl] = _rope(qn * gq_ref[...], c, sg, lo) * scale
            ko_ref[:, sl] = _rope(kn * gk_ref[...], c, sg, lo)

    return _rowwise(body, name="qk_prep", nrows=S, tile=ROW_TILE,
                    row_ins=[(proj, D_ATTN, 0), (proj, D_ATTN, 1), (pos_col, 1, 0)],
                    full_ins=[gq128, gk128, inv_tab], row_outs=[(D_ATTN, F32)] * 2)


ATTN_QB = 4
V_COLS = pl.BlockSpec((S, LANES), lambda p, b: (0, 2 * D_ATTN // LANES + p))


def _band(b, d):
    nb = NB // d
    r, n = b // nb, b % nb
    width, kn = (BLK, n) if nb == 1 else (2 * BLK, jnp.maximum(n - 1, 0))

    def rows(first_member, count):
        if d == 1:
            return pl.ds(pl.multiple_of(first_member, BLK), count)
        return pl.ds(first_member * d + r, count, stride=d)

    qm = n * BLK + lax.broadcasted_iota(jnp.int32, (BLK, width), 0)
    km = kn * BLK + lax.broadcasted_iota(jnp.int32, (BLK, width), 1)
    valid = km <= qm
    if nb > 1:
        valid = valid & (km >= qm - BLK)
    return rows(n * BLK, BLK), rows(kn * BLK, width), valid


def _attn_fwd(q, k, v, d, name):
    def kern(q_ref, k_ref, v_ref, o_ref, l_ref):
        half0 = lax.broadcasted_iota(jnp.int32, (BLK, LANES), 1) < HEAD
        for bb in range(ATTN_QB):
            q_rows, k_rows, valid = _band(pl.program_id(1) * ATTN_QB + bb, d)
            q = q_ref[q_rows, :]
            kb, vb = k_ref[k_rows, :].astype(BF), v_ref[k_rows, :].astype(BF)
            o_h, l_h = [], []
            for h in range(2):
                hm = half0 if h == 0 else jnp.logical_not(half0)
                s = jnp.where(valid, _dot(jnp.where(hm, q, 0.0).astype(BF), kb, NT), NEG)
                m = jnp.max(s, axis=-1, keepdims=True)
                p = jnp.exp(s - m)
                den = jnp.sum(p, axis=-1, keepdims=True)
                o_h.append(_dot(p.astype(BF), vb, NN) / den)
                l_h.append(m + jnp.log(den))
            o_ref[q_rows, :] = jnp.where(half0, o_h[0], o_h[1])
            l_ref[q_rows, :] = jnp.where(half0, l_h[0], l_h[1])

    seq = pl.BlockSpec((S, LANES), lambda p, b: (0, p))
    return pl.pallas_call(
        kern, name=name, grid=(D_ATTN // LANES, NB // ATTN_QB), in_specs=[seq, seq, V_COLS], out_specs=[seq, seq],
        out_shape=[jax.ShapeDtypeStruct((S, D_ATTN), F32)] * 2,
        compiler_params=_params("parallel", "arbitrary"))(q, k, v)


def _attn_merge(os_, ls_, g_attn):
    def body(i, n, o1, o2, o3, l1, l2, l3, g_ref, a_ref, lse_ref, an_ref):
        la, lb, lc = l1[...], l2[...], l3[...]
        m = jnp.maximum(jnp.maximum(la, lb), lc)
        ea, eb, ec = jnp.exp(la - m), jnp.exp(lb - m), jnp.exp(lc - m)
        den = ea + eb + ec
        attn = (ea * o1[...] + eb * o2[...] + ec * o3[...]) / den
        a_ref[...] = attn
        lse_ref[...] = m + jnp.log(den)
        nx, _ = _rms_n(attn)
        an_ref[...] = (nx * g_ref[...]).astype(BF)

    return _rowwise(body, name="attn_merge", nrows=S, tile=ROW_TILE,
                    row_ins=[(a, D_ATTN, 0) for a in (*os_, *ls_)], full_ins=[g_attn],
                    row_outs=[(D_ATTN, F32), (D_ATTN, F32), (D_ATTN, BF)])


def _conv_taps(x, w_ref):
    rows = lax.broadcasted_iota(jnp.int32, x.shape, 0)
    shifted = []
    for w in range(CONV_W):
        k = CONV_W - 1 - w
        shifted.append(x if k == 0 else jnp.where(rows >= k, pltpu.roll(x, k, 0), 0.0))
    acc = shifted[0] * w_ref[0:1, :]
    for w in range(1, CONV_W):
        acc = acc + shifted[w] * w_ref[w:w + 1, :]
    return acc, shifted


def _conv_fwd(proj, conv_w, conv_b):
    tc = 256

    def kern(x_ref, w_ref, b_ref, o_ref):
        c, _ = _conv_taps(x_ref[...], w_ref)
        c = c + b_ref[...]
        o_ref[...] = c * _sigmoid(c)

    return pl.pallas_call(
        kern, name="conv_fwd", grid=(D_CONV // tc,),
        in_specs=[pl.BlockSpec((S, tc), lambda c: (0, 4 * D_ATTN // tc + c)),
                  pl.BlockSpec((CONV_W, tc), lambda c: (0, c)), pl.BlockSpec((1, tc), lambda c: (0, c))],
        out_specs=pl.BlockSpec((S, tc), lambda c: (0, c)),
        out_shape=jax.ShapeDtypeStruct((S, D_CONV), F32), compiler_params=_params("parallel"))(proj, conv_w, conv_b)


def _softplus(x):
    return jnp.maximum(x, 0.0) + jnp.log1p(jnp.exp(-jnp.abs(x)))


def _ssd_prep(proj, dt_bias128, a_log128):
    def body(i, n, raw_ref, b_ref, al_ref, dt_ref, a_ref, carry):
        @pl.when(i == 0)
        def _():
            carry[...] = jnp.zeros_like(carry)

        dt = _softplus(raw_ref[...] + b_ref[...])
        da = dt * (-jnp.exp(al_ref[...]))
        tri = (lax.broadcasted_iota(jnp.int32, (BLK, BLK), 0) >= lax.broadcasted_iota(jnp.int32, (BLK, BLK), 1))
        cs = _dot3(tri.astype(BF), da) + carry[0:1, :]
        dt_ref[...] = dt
        a_ref[...] = cs
        carry[...] = jnp.broadcast_to(cs[BLK - 1:BLK, :], carry.shape)

    return _rowwise(body, name="ssd_prep", nrows=S, tile=BLK, row_ins=[(proj, LANES, (D_INP - LANES) // LANES)],
                    full_ins=[dt_bias128, a_log128], row_outs=[(LANES, F32), (LANES, F32)],
                    scratch=[pltpu.VMEM((8, LANES), F32)])


def _ssd_decay(a_col, at_row, causal):
    diff = jnp.where(causal, a_col - at_row, 0.0)
    return jnp.where(causal, jnp.exp(diff), 0.0)


SSD_CB = 2


def _ssd_fwd(xbc, a_b, dt_b, at_r, dt_r):
    def kern(c_ref, b_ref, x_ref, ab_ref, dtb_ref, at_ref, dt_ref, y_ref, hall_ref, h_scr, aprev_scr):
        i = pl.program_id(1)

        @pl.when(i == 0)
        def _():
            h_scr[...] = jnp.zeros_like(h_scr)
            aprev_scr[...] = jnp.zeros_like(aprev_scr)

        half0 = lax.broadcasted_iota(jnp.int32, (BLK, LANES), 1) < HEAD
        causal = lax.broadcasted_iota(jnp.int32, (BLK, BLK), 1) <= lax.broadcasted_iota(jnp.int32, (BLK, BLK), 0)
        h, a_prev = h_scr[...], aprev_scr[0:1, :]
        for cc in range(SSD_CB):
            rows = slice(cc * BLK, (cc + 1) * BLK)
            ci, bi, x = c_ref[rows, :].astype(BF), b_ref[rows, :].astype(BF), x_ref[rows, :]
            ab = ab_ref[rows, :]
            a_end = ab[BLK - 1:BLK, :]
            alpha = jnp.exp(ab - a_prev)
            beta = jnp.exp(a_end - ab) * dtb_ref[rows, :]
            hall_ref[rows, :] = h
            cb = _dot(ci, bi, NT)
            at, dtj = at_ref[cc], dt_ref[cc]
            y = alpha * _dot(ci, h.astype(BF), NN)
            for hd in range(2):
                hm = half0 if hd == 0 else jnp.logical_not(half0)
                w = cb * _ssd_decay(ab[:, HEAD * hd:HEAD * hd + 1], at[hd:hd + 1, :], causal) * dtj[hd:hd + 1, :]
                y = y + _dot(w.astype(BF), jnp.where(hm, x, 0.0).astype(BF), NN)
            y_ref[rows, :] = y
            h = alpha[BLK - 1:BLK, :] * h + _dot(bi, (beta * x).astype(BF), TN)
            a_prev = a_end
        h_scr[...] = h
        aprev_scr[...] = jnp.broadcast_to(a_prev, aprev_scr.shape)

    npair = D_SSM // LANES
    rowvec = pl.BlockSpec((None, SSD_CB, 2, BLK), lambda p, i: (p, i, 0, 0))
    tile = pl.BlockSpec((SSD_CB * BLK, LANES), lambda p, i: (i, p))
    return pl.pallas_call(
        kern, name="ssd_fwd", grid=(npair, NB // SSD_CB),
        in_specs=[pl.BlockSpec((SSD_CB * BLK, LANES), lambda p, i: (i, 12 + p // 2)),
                  pl.BlockSpec((SSD_CB * BLK, LANES), lambda p, i: (i, 8 + p // 2)), tile, tile, tile, rowvec, rowvec],
        out_specs=[tile, tile], out_shape=[jax.ShapeDtypeStruct((S, D_SSM), F32)] * 2,
        scratch_shapes=[pltpu.VMEM((BLK, LANES), F32), pltpu.VMEM((8, LANES), F32)],
        compiler_params=_params("parallel", "arbitrary"))(xbc, xbc, xbc, a_b, dt_b, at_r, dt_r)


def _ssd_post(y_ssd, xbc, proj, dskip_b, g_ssm):
    def body(i, n, y_ref, xs_ref, z_ref, d_ref, g_ref, o_ref):
        z = z_ref[...]
        y2 = (y_ref[...] + d_ref[...] * xs_ref[...]) * (z * _sigmoid(z))
        nx, _ = _seg_rms_n(y2, 256)
        o_ref[...] = (nx * g_ref[...]).astype(BF)

    return _rowwise(body, name="ssd_post", nrows=S, tile=ROW_TILE,
                    row_ins=[(y_ssd, D_SSM, 0), (xbc, D_SSM, 0), (proj, D_SSM, 3)], full_ins=[dskip_b, g_ssm],
                    row_outs=[(D_SSM, BF)])[0]


def _cross_heads(q, kv_ref, gq, gk):
    out = []
    for h in range(D_CROSS // CROSS_HEAD):
        sl = slice(CROSS_HEAD * h, CROSS_HEAD * (h + 1))
        nq, rq = _rms_n(q[:, sl])
        nk, rk = _rms_n(kv_ref[:, sl])
        v = kv_ref[:, D_CROSS + CROSS_HEAD * h:D_CROSS + CROSS_HEAD * (h + 1)]
        out.append((sl, nq, rq, nk, rk, v))
    return out


def _cross_fwd(qc, kv, g_cq, g_ck):
    scale = CROSS_HEAD ** -0.5

    def body(i, n, q_ref, kv_ref, gq_ref, gk_ref, o_ref):
        for sl, nq, _, nk, _, v in _cross_heads(q_ref[...], kv_ref, gq_ref[...], gk_ref[...]):
            qn = (nq * gq_ref[...] * scale).astype(BF)
            kn = (nk * gk_ref[...]).astype(BF)
            s = _dot(qn, kn, NT)
            e = jnp.exp(s - jnp.max(s, axis=-1, keepdims=True))
            p = e / jnp.sum(e, axis=-1, keepdims=True)
            o_ref[:, sl] = _dot(p.astype(BF), v.astype(BF), NN).astype(BF)

    return _rowwise(body, name="cross_fwd", nrows=S, tile=ROW_TILE, row_ins=[(qc, D_CROSS, 0)],
                    full_ins=[kv, g_cq, g_ck], row_outs=[(D_CROSS, BF)])[0]


def _loss_sum(dy):
    def body(i, n, d_ref, o_ref):
        d = d_ref[...]
        s = jnp.sum(jnp.sum(d * d, axis=0, keepdims=True), axis=1, keepdims=True)
        _accum(o_ref, s, i)

    return _rowwise(body, name="loss_sum", nrows=S, tile=ROW_TILE, row_ins=[(dy, D, 0)],
                    acc_outs=[((1, 1), F32)])[0]


def _rms_bwd_call(x, g, dh, dres, name):
    rows = x.shape[0]
    has_res = dres is not None

    def body(i, n, *refs):
        if has_res:
            x_ref, dh_ref, dr_ref, g_ref, dx_ref, dxb_ref, dg_ref = refs
        else:
            x_ref, dh_ref, g_ref, dg_ref = refs
        dx, dg = _rms_bwd(x_ref[...], g_ref[...], dh_ref[...])
        if has_res:
            dx = dx + dr_ref[...]
            dx_ref[...] = dx
            dxb_ref[...] = dx.astype(BF)
        _accum(dg_ref, dg, i)

    row_ins = [(x, D, 0), (dh, D, 0)] + ([(dres, D, 0)] if has_res else [])
    return _rowwise(body, name=name, nrows=rows, tile=min(ROW_TILE, rows), row_ins=row_ins, full_ins=[g],
                    row_outs=[(D, F32), (D, BF)] if has_res else [], acc_outs=[((1, D), F32)])


def _cross_bwd(qc, doc, kv, g_cq, g_ck):
    scale = CROSS_HEAD ** -0.5

    def body(i, n, q_ref, do_ref, kv_ref, gq_ref, gk_ref, dq_ref, dkn_ref, dv_ref, dgq_ref):
        dgq = jnp.zeros((1, CROSS_HEAD), F32)
        dkn_parts, dv_parts = [], []
        for sl, nq, rq, nk, _, v in _cross_heads(q_ref[...], kv_ref, gq_ref[...], gk_ref[...]):
            qn = (nq * gq_ref[...] * scale).astype(BF)
            kn = (nk * gk_ref[...]).astype(BF)
            s = _dot(qn, kn, NT)
            e = jnp.exp(s - jnp.max(s, axis=-1, keepdims=True))
            p = e / jnp.sum(e, axis=-1, keepdims=True)
            do = do_ref[:, sl].astype(BF)
            dv_parts.append(_dot(p.astype(BF), do, TN))
            dp = _dot(do, v.astype(BF), NT)
            ds = (p * (dp - jnp.sum(dp * p, axis=-1, keepdims=True))).astype(BF)
            dqn = _dot(ds, kn, NN)
            dkn_parts.append(_dot(ds, qn, TN))
            dn = dqn * (gq_ref[...] * scale)
            dq_ref[:, sl] = (rq * (dn - nq * jnp.mean(dn * nq, axis=-1, keepdims=True))).astype(BF)
            dgq = dgq + jnp.sum(dqn * scale * nq, axis=0, keepdims=True)
        _accum(dkn_ref, jnp.concatenate(dkn_parts, axis=1), i)
        _accum(dv_ref, jnp.concatenate(dv_parts, axis=1), i)
        _accum(dgq_ref, dgq, i)

    return _rowwise(body, name="cross_bwd", nrows=S, tile=ROW_TILE, row_ins=[(qc, D_CROSS, 0), (doc, D_CROSS, 0)],
                    full_ins=[kv, g_cq, g_ck], row_outs=[(D_CROSS, BF)],
                    acc_outs=[((N_MEM, D_CROSS), F32), ((N_MEM, D_CROSS), F32), ((1, CROSS_HEAD), F32)])


def _cross_kv_bwd(kv, dkn, dv, g_ck):
    def body(i, n, kv_ref, dkn_ref, dv_ref, gk_ref, dkv_ref, dgk_ref):
        dgk = jnp.zeros((1, CROSS_HEAD), F32)
        for h in range(D_CROSS // CROSS_HEAD):
            sl = slice(CROSS_HEAD * h, CROSS_HEAD * (h + 1))
            dk, dg = _rms_bwd(kv_ref[:, sl], gk_ref[...], dkn_ref[:, sl])
            dkv_ref[:, sl] = dk.astype(BF)
            dgk = dgk + dg
        dkv_ref[:, D_CROSS:] = dv_ref[...].astype(BF)
        dgk_ref[...] = dgk

    return _rowwise(body, name="cross_kv_bwd", nrows=N_MEM, tile=N_MEM,
                    row_ins=[(kv, 2 * D_CROSS, 0), (dkn, D_CROSS, 0), (dv, D_CROSS, 0)], full_ins=[g_ck],
                    row_outs=[(2 * D_CROSS, BF)], acc_outs=[((1, CROSS_HEAD), F32)])


def _attn_merge_bwd(dmix, attn, g_attn):
    def body(i, n, dn_ref, a_ref, g_ref, da_ref, dl_ref, dg_ref):
        attn_v = a_ref[...]
        da, dg = _rms_bwd(attn_v, g_ref[...], dn_ref[...])
        da_ref[...] = da
        dl_ref[...] = _seg_sum(da * attn_v, HEAD)
        _accum(dg_ref, dg, i)

    return _rowwise(body, name="attn_merge_bwd", nrows=S, tile=ROW_TILE,
                    row_ins=[(dmix, D_ATTN, 0), (attn, D_ATTN, 0)], full_ins=[g_attn],
                    row_outs=[(D_ATTN, F32), (D_ATTN, F32)], acc_outs=[((1, D_ATTN), F32)])


def _attn_bwd(q, k, v, do, lse, delta, d, name):
    def kern(q_ref, k_ref, v_ref, do_ref, l_ref, d_ref, dq_ref, dk_ref, dv_ref):
        @pl.when(pl.program_id(1) == 0)
        def _():
            dk_ref[...] = jnp.zeros_like(dk_ref)
            dv_ref[...] = jnp.zeros_like(dv_ref)

        half0 = lax.broadcasted_iota(jnp.int32, (BLK, LANES), 1) < HEAD
        updates = []
        for bb in range(ATTN_QB):
            q_rows, k_rows, valid = _band(pl.program_id(1) * ATTN_QB + bb, d)
            q, do = q_ref[q_rows, :], do_ref[q_rows, :]
            lse_t, del_t = l_ref[q_rows, :], d_ref[q_rows, :]
            kb, vb = k_ref[k_rows, :].astype(BF), v_ref[k_rows, :].astype(BF)
            dq_h, dk, dv = [], None, None
            for h in range(2):
                hm = half0 if h == 0 else jnp.logical_not(half0)
                qm = jnp.where(hm, q, 0.0).astype(BF)
                dom = jnp.where(hm, do, 0.0).astype(BF)
                s = jnp.where(valid, _dot(qm, kb, NT), NEG)
                p = jnp.exp(s - lse_t[:, HEAD * h:HEAD * h + 1])
                ds = (p * (_dot(dom, vb, NT) - del_t[:, HEAD * h:HEAD * h + 1])).astype(BF)
                dq_h.append(_dot(ds, kb, NN))
                dk_h, dv_h = _dot(ds, qm, TN), _dot(p.astype(BF), dom, TN)
                dk, dv = (dk_h, dv_h) if h == 0 else (dk + dk_h, dv + dv_h)
            dq_ref[q_rows, :] = jnp.where(half0, dq_h[0], dq_h[1])
            updates.append((k_rows, dk, dv))
        for k_rows, dk, dv in updates:
            dk_ref[k_rows, :] += dk
            dv_ref[k_rows, :] += dv

    seq = pl.BlockSpec((S, LANES), lambda p, b: (0, p))
    return pl.pallas_call(
        kern, name=name, grid=(D_ATTN // LANES, NB // ATTN_QB), in_specs=[seq, seq, V_COLS, seq, seq, seq],
        out_specs=[seq, seq, seq], out_shape=[jax.ShapeDtypeStruct((S, D_ATTN), F32)] * 3,
        compiler_params=_params("parallel", "arbitrary"))(q, k, v, do, lse, delta)


def _qk_bwd(dqs, dks, dvs, proj, pos_col, gq128, gk128, inv_tab):
    scale = HEAD ** -0.5

    def body(i, n, *refs):
        dq_refs, dk_refs, dv_refs = refs[0:3], refs[3:6], refs[6:9]
        q_ref, k_ref, p_ref, gq_ref, gk_ref, it_ref = refs[9:15]
        dqo_ref, dko_ref, dvo_ref, dgq_ref, dgk_ref = refs[15:20]
        t = q_ref.shape[0]
        c, sg, lo = _rope_tables(p_ref[...], it_ref[...], t)
        dgq = jnp.zeros((1, LANES), F32)
        dgk = jnp.zeros((1, LANES), F32)
        for j in range(D_ATTN // LANES):
            sl = slice(LANES * j, LANES * (j + 1))
            dqr = _rope(dq_refs[0][:, sl] + dq_refs[1][:, sl] + dq_refs[2][:, sl], c, -sg, lo) * scale
            dkr = _rope(dk_refs[0][:, sl] + dk_refs[1][:, sl] + dk_refs[2][:, sl], c, -sg, lo)
            dq, gq = _seg_rms_bwd(q_ref[:, sl], gq_ref[...], dqr, HEAD)
            dk, gk = _seg_rms_bwd(k_ref[:, sl], gk_ref[...], dkr, HEAD)
            dqo_ref[:, sl] = dq.astype(BF)
            dko_ref[:, sl] = dk.astype(BF)
            dgq, dgk = dgq + gq, dgk + gk
        dvo_ref[...] = (dv_refs[0][...] + dv_refs[1][...] + dv_refs[2][...]).astype(BF)
        _accum(dgq_ref, _fold_heads(dgq), i)
        _accum(dgk_ref, _fold_heads(dgk), i)

    return _rowwise(body, name="qk_bwd", nrows=S, tile=ROW_TILE,
                    row_ins=[(a, D_ATTN, 0) for a in (*dqs, *dks, *dvs)]
                    + [(proj, D_ATTN, 0), (proj, D_ATTN, 1), (pos_col, 1, 0)],
                    full_ins=[gq128, gk128, inv_tab], row_outs=[(D_ATTN, BF)] * 3,
                    acc_outs=[((1, LANES), F32)] * 2)


def _ssd_post_bwd(y_ssd, xbc, proj, dmix, dskip_b, g_ssm):
    def body(i, n, y_ref, xs_ref, z_ref, do_ref, d_ref, g_ref, dy_ref, dz_ref, dxs_ref, dd_ref, dg_ref):
        z, xs = z_ref[...], xs_ref[...]
        sg = _sigmoid(z)
        gate = z * sg
        y = y_ref[...] + d_ref[...] * xs
        dy2, dg = _seg_rms_bwd(y * gate, g_ref[...], do_ref[...], 256)
        dy = dy2 * gate
        dy_ref[...] = dy.astype(BF)
        dz_ref[...] = (dy2 * y * (sg * (1.0 + z * (1.0 - sg)))).astype(BF)
        dxs_ref[...] = dy * d_ref[...]
        _accum(dd_ref, jnp.sum(dy * xs, axis=0, keepdims=True), i)
        _accum(dg_ref, dg, i)

    return _rowwise(body, name="ssd_post_bwd", nrows=S, tile=ROW_TILE,
                    row_ins=[(y_ssd, D_SSM, 0), (xbc, D_SSM, 0), (proj, D_SSM, 3), (dmix, D_SSM, 1)],
                    full_ins=[dskip_b, g_ssm], row_outs=[(D_SSM, BF), (D_SSM, BF), (D_SSM, F32)],
                    acc_outs=[((1, D_SSM), F32), ((1, D_SSM), F32)])


def _ssd_bwd(xbc, dy, h_all, a_b, dt_b, at_r, dt_r):
    def kern(c_ref, b_ref, x_ref, dy_ref, hall_ref, ab_ref, abp_ref, dtb_ref, at_ref, dt_ref,
             dc_ref, daq_ref, dx_ref, db_ref, dak_ref, ddt_ref, ddtb_ref, dh_scr, carry_scr):
        i = pl.program_id(1)

        @pl.when(i == 0)
        def _():
            dh_scr[...] = jnp.zeros_like(dh_scr)
            carry_scr[...] = jnp.zeros_like(carry_scr)

        half0 = lax.broadcasted_iota(jnp.int32, (BLK, LANES), 1) < HEAD
        hms = (half0, jnp.logical_not(half0))
        causal = lax.broadcasted_iota(jnp.int32, (BLK, BLK), 1) <= lax.broadcasted_iota(jnp.int32, (BLK, BLK), 0)
        row = lax.broadcasted_iota(jnp.int32, (BLK, LANES), 0)
        dh_next, carry = dh_scr[...], carry_scr[0:1, :]
        for cc in reversed(range(SSD_CB)):
            rows = slice(cc * BLK, (cc + 1) * BLK)
            ci, bi, x = c_ref[rows, :].astype(BF), b_ref[rows, :].astype(BF), x_ref[rows, :]
            dyv = dy_ref[rows, :].astype(F32)
            ab, dtb = ab_ref[rows, :], dtb_ref[rows, :]
            if cc > 0:
                a_prev = ab_ref[cc * BLK - 1:cc * BLK, :]
            else:
                a_prev = jnp.where(i == NB // SSD_CB - 1, 0.0, abp_ref[7:8, :])
            a_end = ab[BLK - 1:BLK, :]
            alpha = jnp.exp(ab - a_prev)
            e_end = jnp.exp(a_end - ab)
            beta = e_end * dtb
            t_all = alpha[BLK - 1:BLK, :]
            hc = hall_ref[rows, :]
            hcb, dhb = hc.astype(BF), dh_next.astype(BF)

            cb = _dot(ci, bi, NT)
            at, dtj = at_ref[cc], dt_ref[cc]
            dcb = jnp.zeros((BLK, BLK), F32)
            dx = jnp.zeros((BLK, LANES), F32)
            rsum = []
            for hd in range(2):
                dym = jnp.where(hms[hd], dyv, 0.0).astype(BF)
                lm = _ssd_decay(ab[:, HEAD * hd:HEAD * hd + 1], at[hd:hd + 1, :], causal)
                dth = dtj[hd:hd + 1, :]
                dw = _dot(dym, jnp.where(hms[hd], x, 0.0).astype(BF), NT)
                g = dw * cb * lm
                dx = dx + _dot((cb * lm * dth).astype(BF), dym, TN)
                gcol = jnp.sum(g, axis=0, keepdims=True)
                ddt_ref[cc, hd:hd + 1, :] = gcol
                dak_ref[cc, hd:hd + 1, :] = gcol * dth
                rsum.append(jnp.sum(g * dth, axis=1, keepdims=True))
                dcb = dcb + dw * lm * dth
            dcb = dcb.astype(BF)

            g1 = (alpha * dyv).astype(BF)
            dalpha = dyv * _dot(ci, hcb, NN)
            g2 = _dot(bi, dhb, NN)
            dbeta = x * g2
            bx = (beta * x).astype(BF)
            dc_ref[rows, :] = _dot(dcb, bi, NN) + _dot(g1, hcb, NT)
            db_ref[rows, :] = _dot(dcb, ci, TN) + _dot(bx, dhb, NT)
            dx_ref[rows, :] = dx + beta * g2
            ddtb_ref[rows, :] = _seg_sum(e_end * dbeta, HEAD)
            ada, bdb = alpha * dalpha, beta * dbeta
            t_dt = t_all * jnp.sum(dh_next * hc, axis=0, keepdims=True)
            end_term = _seg_sum(jnp.broadcast_to(jnp.sum(bdb, axis=0, keepdims=True) + t_dt, (8, LANES)), HEAD)[0:1]
            prev_term = _seg_sum(jnp.broadcast_to(jnp.sum(ada, axis=0, keepdims=True) + t_dt, (8, LANES)), HEAD)[0:1]
            daq = jnp.where(half0, rsum[0], rsum[1]) + _seg_sum(ada - bdb, HEAD)
            daq_ref[rows, :] = daq + jnp.where(row == BLK - 1, end_term - carry, 0.0)
            carry = prev_term
            dh_next = t_all * dh_next + _dot(ci, g1, TN)
        carry_scr[...] = jnp.broadcast_to(carry, carry_scr.shape)
        dh_scr[...] = dh_next

    npair = D_SSM // LANES
    rev = lambda i: NB // SSD_CB - 1 - i
    rowvec = pl.BlockSpec((None, SSD_CB, 2, BLK), lambda p, i: (p, rev(i), 0, 0))
    tile = pl.BlockSpec((SSD_CB * BLK, LANES), lambda p, i: (rev(i), p))
    prev_tile = pl.BlockSpec((8, LANES), lambda p, i: (jnp.maximum(rev(i) * (SSD_CB * BLK // 8) - 1, 0), p))
    return pl.pallas_call(
        kern, name="ssd_bwd", grid=(npair, NB // SSD_CB),
        in_specs=[pl.BlockSpec((SSD_CB * BLK, LANES), lambda p, i: (rev(i), 12 + p // 2)),
                  pl.BlockSpec((SSD_CB * BLK, LANES), lambda p, i: (rev(i), 8 + p // 2)),
                  tile, tile, tile, tile, prev_tile, tile, rowvec, rowvec],
        out_specs=[tile, tile, tile, tile, rowvec, rowvec, tile],
        out_shape=[jax.ShapeDtypeStruct((S, D_SSM), F32)] * 4
        + [jax.ShapeDtypeStruct((npair, NB, 2, BLK), F32)] * 2 + [jax.ShapeDtypeStruct((S, D_SSM), F32)],
        scratch_shapes=[pltpu.VMEM((BLK, LANES), F32), pltpu.VMEM((8, LANES), F32)],
        compiler_params=_params("parallel", "arbitrary"))(xbc, xbc, xbc, dy, h_all, a_b, a_b, dt_b, at_r, dt_r)


def _ssd_prep_bwd(daq, dak, ddt_row, ddt_lane, dt, proj, dt_bias128, a_log128):
    def body(i, n, daq_ref, dak_ref, dd_ref, dd2_ref, dt_ref, raw_ref, b_ref, al_ref, draw_ref, dal_ref, db_ref,
             carry):
        @pl.when(i == 0)
        def _():
            carry[...] = jnp.zeros_like(carry)

        a = -jnp.exp(al_ref[...])
        tri = (lax.broadcasted_iota(jnp.int32, (BLK, BLK), 0) <= lax.broadcasted_iota(jnp.int32, (BLK, BLK), 1))
        rev = _dot3(tri.astype(BF), daq_ref[...] - dak_ref[...]) + carry[0:1, :]
        carry[...] = jnp.broadcast_to(rev[0:1, :], carry.shape)
        dtv = dt_ref[...]
        draw = (dd_ref[...] + dd2_ref[...] + a * rev) * _sigmoid(raw_ref[...] + b_ref[...])
        draw_ref[...] = draw.astype(BF)
        _accum(dal_ref, jnp.sum(dtv * rev, axis=0, keepdims=True) * a, i)
        _accum(db_ref, jnp.sum(draw, axis=0, keepdims=True), i)

    return _rowwise(body, name="ssd_prep_bwd", nrows=S, tile=BLK, reverse=True,
                    row_ins=[(daq, LANES, 0), (dak, LANES, 0), (ddt_row, LANES, 0), (ddt_lane, LANES, 0), (dt, LANES, 0),
                             (proj, LANES, (D_INP - LANES) // LANES)],
                    full_ins=[dt_bias128, a_log128], row_outs=[(LANES, BF)],
                    acc_outs=[((1, LANES), F32), ((1, LANES), F32)], scratch=[pltpu.VMEM((8, LANES), F32)])


def _conv_bwd(proj, conv_w, conv_b, d1, d2, map1, map2, col0, ntiles, name):
    base = (4 * D_ATTN + col0) // LANES

    def kern(x_ref, w_ref, b_ref, d1_ref, d2_ref, dx_ref, dw_ref, db_ref):
        x = x_ref[...]
        c, shifted = _conv_taps(x, w_ref)
        c = c + b_ref[...]
        sg = _sigmoid(c)
        dc = (d1_ref[...] + d2_ref[...]) * (sg * (1.0 + c * (1.0 - sg)))
        rows = lax.broadcasted_iota(jnp.int32, x.shape, 0)
        dx = jnp.zeros_like(x)
        for w in range(CONV_W):
            k = CONV_W - 1 - w
            up = dc if k == 0 else jnp.where(rows < S - k, pltpu.roll(dc, S - k, 0), 0.0)
            dx = dx + up * w_ref[w:w + 1, :]
            dw_ref[w:w + 1, :] = jnp.sum(dc * shifted[w], axis=0, keepdims=True)
        dx_ref[...] = dx.astype(BF)
        db_ref[...] = jnp.sum(dc, axis=0, keepdims=True)

    c0 = col0 // LANES
    return pl.pallas_call(
        kern, name=name, grid=(ntiles,),
        in_specs=[pl.BlockSpec((S, LANES), lambda c: (0, base + c)),
                  pl.BlockSpec((CONV_W, LANES), lambda c: (0, c0 + c)),
                  pl.BlockSpec((1, LANES), lambda c: (0, c0 + c)),
                  pl.BlockSpec((S, LANES), lambda c: (0, map1(c))),
                  pl.BlockSpec((S, LANES), lambda c: (0, map2(c)))],
        out_specs=[pl.BlockSpec((S, LANES), lambda c: (0, c)), pl.BlockSpec((CONV_W, LANES), lambda c: (0, c)),
                   pl.BlockSpec((1, LANES), lambda c: (0, c))],
        out_shape=[jax.ShapeDtypeStruct((S, ntiles * LANES), BF), jax.ShapeDtypeStruct((CONV_W, ntiles * LANES), F32),
                   jax.ShapeDtypeStruct((1, ntiles * LANES), F32)],
        compiler_params=_params("parallel"))(proj, conv_w, conv_b, d1, d2)


def _adamw(w, m, v, parts, name):
    r, c = w.shape
    n_parts = parts.shape[0]
    tile = r if r <= 512 else (128 if c > 1024 else 256)
    assert r % tile == 0
    c1 = 1.0 - ADAM_B1 ** ADAM_STEP
    c2 = 1.0 - ADAM_B2 ** ADAM_STEP

    def kern(w_ref, m_ref, v_ref, p_ref, g_ref, d_ref, mo_ref, vo_ref):
        g = p_ref[0].astype(F32)
        for k in range(1, n_parts):
            g = g + p_ref[k].astype(F32)
        mn = ADAM_B1 * m_ref[...] + (1.0 - ADAM_B1) * g
        vn = ADAM_B2 * v_ref[...] + (1.0 - ADAM_B2) * (g * g)
        g_ref[...] = g
        mo_ref[...] = mn
        vo_ref[...] = vn
        d_ref[...] = -ADAM_LR * ((mn / c1) / (jnp.sqrt(vn / c2) + ADAM_EPS) + ADAM_WD * w_ref[...])

    blk = pl.BlockSpec((tile, c), lambda i: (i, 0))
    return pl.pallas_call(
        kern, name=name, grid=(r // tile,),
        in_specs=[blk, blk, blk, pl.BlockSpec((n_parts, tile, c), lambda i: (0, i, 0))],
        out_specs=[blk] * 4, out_shape=[jax.ShapeDtypeStruct((r, c), F32)] * 4,
        compiler_params=_params("parallel"))(w, m, v, parts)


MESH = pl.DeviceIdType.MESH
ANY = pl.BlockSpec(memory_space=pl.ANY)


def _put_own(gathered, shard):
    me = 4 * lax.axis_index("x") + 2 * lax.axis_index("y") + lax.axis_index("c")
    return lax.dynamic_update_slice(gathered, shard[None], (me,) + (0,) * shard.ndim)


def _all_gather(arrs, name, after=None):
    na = len(arrs)
    n_after = 0 if after is None else 1

    def kern(*refs):
        ins, outs = refs[:na], refs[na + n_after:2 * na + n_after]
        send_sems, recv_sems = refs[2 * na + n_after:]
        x, y, c = lax.axis_index("x"), lax.axis_index("y"), lax.axis_index("c")
        me, sibling = (x, y, c), (x, y, 1 - c)
        chips = [(1 - x, y), (x, 1 - y), (1 - x, 1 - y)]

        def copy(a, k, block, to, src=None):
            px, py, pc = block
            dst = outs[a].at[4 * px + 2 * py + pc]
            return pltpu.make_async_remote_copy(
                src_ref=dst if src is None else src, dst_ref=dst, send_sem=send_sems.at[a, k],
                recv_sem=recv_sems.at[a, k], device_id=to, device_id_type=MESH)

        first = []
        for a in range(na):
            first.append(copy(a, 0, me, sibling, src=ins[a]))
            first += [copy(a, 1 + j, me, (*chip, c), src=ins[a]) for j, chip in enumerate(chips)]
        for cp in first:
            cp.start()
        passed = []
        for a in range(na):
            for j, chip in enumerate(chips):
                copy(a, 1 + j, (*chip, c), me).wait_recv()
                fwd = copy(a, 4 + j, (*chip, c), sibling)
                fwd.start()
                passed.append(fwd)
        for a in range(na):
            copy(a, 0, sibling, me).wait_recv()
            for j, chip in enumerate(chips):
                copy(a, 4 + j, (*chip, 1 - c), me).wait_recv()
        for cp in first + passed:
            cp.wait_send()

    outs = pl.pallas_call(
        kern, name=name, in_specs=[ANY] * (na + n_after), out_specs=[ANY] * na,
        out_shape=[jax.ShapeDtypeStruct((N_DEV,) + a.shape, a.dtype) for a in arrs],
        scratch_shapes=[pltpu.SemaphoreType.DMA((na, 7)), pltpu.SemaphoreType.DMA((na, 7))],
    )(*arrs, *([] if after is None else [after]))
    return [_put_own(o, a) for o, a in zip(outs, arrs)]


HBM = pl.BlockSpec(memory_space=pltpu.HBM)
SEM = pl.BlockSpec(memory_space=pltpu.SEMAPHORE)
EFFECT = pltpu.SideEffectType.DATAFLOW_SIDE_EFFECTING
N_CHIP = 4
N_COPIES = {"gather": 3, "scatter": 3, "forward": 4, "pair": 4}


def _in_hbm(a):
    return pltpu.with_memory_space_constraint(a, pltpu.HBM)


def _chip_copies(kind, src_refs, land_refs, send_sems, recv_sems):
    x, y, c = lax.axis_index("x"), lax.axis_index("y"), lax.axis_index("c")
    chips = [(1 - x, y), (x, 1 - y), (1 - x, 1 - y)]
    n = N_COPIES[kind]
    cps = []

    def add(a, j, src, dst, to):
        cps.append(pltpu.make_async_remote_copy(
            src_ref=src, dst_ref=dst, send_sem=send_sems.at[n * a + j], recv_sem=recv_sems.at[n * a + j],
            device_id=to, device_id_type=MESH))

    for a in range(len(src_refs)):
        if kind == "gather":
            for j, (px, py) in enumerate(chips):
                add(a, j, src_refs[a], land_refs[a].at[4 * x + 2 * y + c], (px, py, c))
        elif kind == "scatter":
            for j, (px, py) in enumerate(chips):
                add(a, j, src_refs[a].at[2 * px + py], land_refs[a].at[2 * x + y], (px, py, c))
        elif kind == "forward":
            add(a, 0, src_refs[a], land_refs[a].at[4 * x + 2 * y + c], (x, y, 1 - c))
            for j, (px, py) in enumerate(chips):
                slot = land_refs[a].at[4 * px + 2 * py + c]
                add(a, 1 + j, slot, slot, (x, y, 1 - c))
        else:
            for q in range(N_CHIP):
                add(a, q, src_refs[a].at[2 * q + 1 - c], land_refs[a].at[q], (x, y, 1 - c))
    return cps


def _exchange_start(kind, srcs, lands, after, name):
    na = len(srcs)
    n_after = 0 if after is None else 1

    def kern(*refs):
        src_refs, land_refs = refs[:na], refs[na:2 * na]
        send_sems, recv_sems = refs[2 * na + n_after], refs[2 * na + n_after + 1]
        token = refs[-1]
        for cp in _chip_copies(kind, src_refs, land_refs, send_sems, recv_sems):
            cp.start()
        token[...] = jnp.zeros_like(token)

    bufs = list(srcs) + list(lands)
    res = pl.pallas_call(
        kern, name=name,
        out_shape=(pltpu.SemaphoreType.DMA((N_COPIES[kind] * na,)), pltpu.SemaphoreType.DMA((N_COPIES[kind] * na,)))
        + tuple(pltpu.HBM(b.shape, b.dtype) for b in bufs) + (jax.ShapeDtypeStruct((8, LANES), F32),),
        in_specs=[HBM] * (2 * na) + [ANY] * n_after,
        out_specs=(SEM, SEM) + (HBM,) * (2 * na) + (pl.BlockSpec(memory_space=pltpu.VMEM),),
        input_output_aliases={i: 2 + i for i in range(2 * na)},
        compiler_params=pltpu.CompilerParams(has_side_effects=EFFECT),
    )(*[_in_hbm(b) for b in bufs], *([] if after is None else [after]))
    return (res[0], res[1]), list(res[2:2 + na]), list(res[2 + na:2 + 2 * na]), res[-1]


def _exchange_wait(kind, sems, srcs, lands, after, name):
    na = len(srcs)

    def kern(*refs):
        src_refs, land_refs = refs[:na], refs[na:2 * na]
        send_sems, recv_sems = refs[2 * na], refs[2 * na + 1]
        for cp in _chip_copies(kind, src_refs, land_refs, send_sems, recv_sems):
            cp.wait_send()
            cp.wait_recv()

    bufs = list(srcs) + list(lands)
    res = pl.pallas_call(
        kern, name=name, out_shape=tuple(pltpu.HBM(b.shape, b.dtype) for b in bufs),
        in_specs=[HBM] * (2 * na) + [SEM, SEM, ANY], out_specs=(HBM,) * (2 * na),
        input_output_aliases={i: i for i in range(2 * na)},
        compiler_params=pltpu.CompilerParams(has_side_effects=EFFECT),
    )(*bufs, sems[0], sems[1], after)
    return list(res[:na]), list(res[na:])


def _gather_finish(shards, lands, name):
    na = len(shards)

    def kern(*refs):
        ins, outs = refs[:na], refs[2 * na:3 * na]
        send_sems, recv_sems = refs[3 * na:]
        x, y, c = lax.axis_index("x"), lax.axis_index("y"), lax.axis_index("c")
        chips = [(1 - x, y), (x, 1 - y), (1 - x, 1 - y)]

        def copy(a, k, slot, pc, src=None):
            dst = outs[a].at[slot + pc]
            return pltpu.make_async_remote_copy(
                src_ref=dst if src is None else src, dst_ref=dst, send_sem=send_sems.at[a, k],
                recv_sem=recv_sems.at[a, k], device_id=(x, y, 1 - c), device_id_type=MESH)

        sends = []
        for a in range(na):
            sends.append(copy(a, 0, 4 * x + 2 * y, c, src=ins[a]))
            sends += [copy(a, 1 + j, 4 * px + 2 * py, c) for j, (px, py) in enumerate(chips)]
        for cp in sends:
            cp.start()
        for a in range(na):
            copy(a, 0, 4 * x + 2 * y, 1 - c).wait_recv()
            for j, (px, py) in enumerate(chips):
                copy(a, 1 + j, 4 * px + 2 * py, 1 - c).wait_recv()
        for cp in sends:
            cp.wait_send()

    return pl.pallas_call(
        kern, name=name, in_specs=[ANY] * (2 * na), out_specs=[ANY] * na,
        out_shape=[jax.ShapeDtypeStruct(l.shape, l.dtype) for l in lands],
        input_output_aliases={na + a: a for a in range(na)},
        scratch_shapes=[pltpu.SemaphoreType.DMA((na, 4)), pltpu.SemaphoreType.DMA((na, 4))])(*shards, *lands)


def _pair_exchange(parts, name):
    na = len(parts)

    def kern(*refs):
        ins, got = refs[:na], refs[na:2 * na]
        send_sems, recv_sems = refs[2 * na:]
        x, y, c = lax.axis_index("x"), lax.axis_index("y"), lax.axis_index("c")
        sends = []
        for a in range(na):
            for q in range(N_CHIP):
                sends.append(pltpu.make_async_remote_copy(
                    src_ref=ins[a].at[2 * q + 1 - c], dst_ref=got[a].at[q], send_sem=send_sems.at[a, q],
                    recv_sem=recv_sems.at[a, q], device_id=(x, y, 1 - c), device_id_type=MESH))
        for cp in sends:
            cp.start()
        for cp in sends:
            cp.wait()

    return pl.pallas_call(
        kern, name=name, in_specs=[ANY] * na, out_specs=[ANY] * na,
        out_shape=[jax.ShapeDtypeStruct((N_CHIP,) + p.shape[1:], p.dtype) for p in parts],
        scratch_shapes=[pltpu.SemaphoreType.DMA((na, N_CHIP)), pltpu.SemaphoreType.DMA((na, N_CHIP))])(*parts)


def _pair_sum(parts, got, name):
    _, r, cdim = parts.shape
    tile = min(r, ROW_TILE)
    core = lax.axis_index("c").astype(jnp.int32).reshape(1)

    def kern(c_ref, a_ref, b_ref, q_ref, l_ref):
        s = (a_ref[...].astype(F32) + b_ref[...].astype(F32)).astype(BF)
        q_ref[...] = s
        l_ref[...] = s

    blk = pl.BlockSpec((None, tile, cdim), lambda q, i, c_ref: (q, i, 0))
    out = jax.ShapeDtypeStruct((N_CHIP, r, cdim), BF)
    return pl.pallas_call(
        kern, name=name, out_shape=[out, out],
        grid_spec=pltpu.PrefetchScalarGridSpec(
            num_scalar_prefetch=1, grid=(N_CHIP, r // tile),
            in_specs=[pl.BlockSpec((None, tile, cdim), lambda q, i, c_ref: (2 * q + c_ref[0], i, 0)), blk],
            out_specs=[blk, blk]),
        compiler_params=_params("parallel", "parallel"))(core, parts, got)


W_IN_COLS = D_IN // N_DEV


def _w_in_from_blocks(w8):
    def kern(x_ref, o_ref):
        for j in range(N_DEV):
            o_ref[:, W_IN_COLS * j:W_IN_COLS * (j + 1)] = x_ref[j]
        o_ref[:, D_IN:] = jnp.zeros((ROW_TILE, D_INP - D_IN), o_ref.dtype)

    return pl.pallas_call(
        kern, name="w_in_from_blocks", grid=(D // ROW_TILE,),
        in_specs=[pl.BlockSpec((N_DEV, ROW_TILE, W_IN_COLS), lambda i: (0, i, 0))],
        out_specs=pl.BlockSpec((ROW_TILE, D_INP), lambda i: (i, 0)),
        out_shape=jax.ShapeDtypeStruct((D, D_INP), w8.dtype), compiler_params=_params("parallel"))(w8)


def _w_in_to_blocks(g):
    def kern(x_ref, o_ref):
        for j in range(N_DEV):
            o_ref[j] = x_ref[:, W_IN_COLS * j:W_IN_COLS * (j + 1)]

    return pl.pallas_call(
        kern, name="w_in_to_blocks", grid=(D // ROW_TILE,),
        in_specs=[pl.BlockSpec((ROW_TILE, D_INP), lambda i: (i, 0))],
        out_specs=pl.BlockSpec((N_DEV, ROW_TILE, W_IN_COLS), lambda i: (0, i, 0)),
        out_shape=jax.ShapeDtypeStruct((N_DEV, D, W_IN_COLS), g.dtype), compiler_params=_params("parallel"))(g)


def _pad_lanes(v, width=LANES):
    return jnp.pad(v, ((0, 0), (0, width - v.shape[1])))


def _row_layout(t16):
    return t16.T.reshape(N_HEADS // 2, 2, NB, BLK).transpose(0, 2, 1, 3)


def _row_layout_inv(r):
    return r.transpose(0, 2, 1, 3).reshape(N_HEADS, S).T


SMALL = (("g_mix", D), ("g_q", HEAD), ("g_k", HEAD), ("g_attn_out", D_ATTN), ("conv_b", D_CONV),
         ("dt_bias", 16), ("a_log", 16), ("d_skip", 16), ("g_ssm_out", D_SSM), ("g_cross", D),
         ("g_mem", D), ("g_cq", CROSS_HEAD), ("g_ck", CROSS_HEAD), ("g_mlp", D))
SMALL_ROWS = -(-sum(-(-w // LANES) for _, w in SMALL) // 8) * 8


def _pack_small(vals):
    rows = [_pad_lanes(vals[n], -(-w // LANES) * LANES).reshape(-1, LANES) for n, w in SMALL]
    packed = jnp.concatenate(rows, axis=0)
    return jnp.pad(packed, ((0, SMALL_ROWS - packed.shape[0]), (0, 0)))


def _unpack_small(packed):
    out, r = {}, 0
    for n, w in SMALL:
        nr = -(-w // LANES)
        out[n] = packed[r:r + nr].reshape(1, nr * LANES)[:, :w]
        r += nr
    return out


BIG = ("w_in", "w_out", "w_cq", "w_ckv", "w_co", "w_up", "w_down")
WEIGHTS = ("g_mix", "w_in", "g_q", "g_k", "g_attn_out", "conv_w", "conv_b", "dt_bias", "a_log", "d_skip",
           "g_ssm_out", "w_out", "g_cross", "g_mem", "w_cq", "w_ckv", "g_cq", "g_ck", "w_co", "g_mlp", "w_up", "w_down")


REST = ("w_out", "w_cq", "w_ckv", "w_co", "w_up", "w_down")


class _Overlap:
    def __init__(self, shards, after):
        lands = [_put_own(lax.empty((N_DEV,) + s.shape, s.dtype), s) for s in shards]
        self.gather = _exchange_start("gather", shards, lands, after, "ag_rest_start")
        self.token = self.gather[3]
        self.groups = []
        self.pairs = {}

    def rest_mid(self, after):
        sems, srcs, lands, _ = self.gather
        srcs, lands = _exchange_wait("gather", sems, srcs, lands, after, "ag_rest_wait")
        self.forward = _exchange_start("forward", srcs, lands, None, "ag_fwd_start")
        return self.forward[3]

    def rest(self, after):
        sems, srcs, lands, _ = self.forward
        _, lands = _exchange_wait("forward", sems, srcs, lands, after, "ag_fwd_wait")
        wf = dict(zip(REST, lands))
        for n in ("w_out", "w_cq", "w_ckv", "w_down"):
            wf[n] = wf[n].reshape(-1, wf[n].shape[-1])
        return wf

    def pair_start(self, name, grad):
        got = lax.empty((N_CHIP,) + grad.shape[1:], grad.dtype)
        self.pairs[name] = _exchange_start("pair", [grad], [got], None, "rs_pair_start_" + name)
        return self.pairs[name][3]

    def emit(self, grads, after):
        names, tag = list(grads), "_%d" % len(self.groups)
        grads, got = dict(grads), {}
        for n in names:
            if n in self.pairs:
                sems, srcs, lands, _ = self.pairs[n]
                srcs, lands = _exchange_wait("pair", sems, srcs, lands, after, "rs_pair_wait_" + n)
                grads[n], got[n] = srcs[0], lands[0]
        sync = [n for n in names if n not in got]
        if sync:
            got.update(zip(sync, _pair_exchange([grads[n] for n in sync], "rs_pair" + tag)))
        sums = [_pair_sum(grads[n], got[n], "rs_sum_" + n) for n in names]
        sems, srcs, lands, token = _exchange_start("scatter", [q for q, _ in sums], [l for _, l in sums], None,
                                                   "rs_chip_start" + tag)
        self.groups.append((names, sems, srcs, lands))
        return token

    def finish(self, gi, after):
        names, sems, srcs, lands = self.groups[gi]
        _, lands = _exchange_wait("scatter", sems, srcs, lands, after, "rs_chip_wait_%d" % gi)
        return dict(zip(names, lands))


def _tie(v, token):
    return v if token is None else v + token[0:1, 0:1]


def _local_step(x, mem, pos_col, target, p, wf, hooks=None):
    p = dict(p)
    inv = np.zeros((1, LANES), np.float32)
    freq = (ROPE_THETA ** (-2.0 * np.arange(ROT // 2, dtype=np.float32) / ROT)).astype(np.float32)
    for l in range(LANES):
        if l % HEAD < ROT:
            inv[0, l] = freq[(l % HEAD) % (ROT // 2)]
    inv_tab = jnp.asarray(inv)
    gq128, gk128 = jnp.tile(p["g_q"], (1, 2)), jnp.tile(p["g_k"], (1, 2))
    dtb128, alog128 = _pad_lanes(p["dt_bias"]), _pad_lanes(p["a_log"])
    dskip_b = jnp.repeat(p["d_skip"], HEAD, axis=1)
    conv_w, conv_b = wf["conv_w"], p["conv_b"]

    h = _rms_fwd(x, p["g_mix"], "rms_mix")
    proj = _matmul(h, wf["w_in"], mode="nn", name="mm_in", tm=1024, tn=896, tk=D)
    qr, kr = _qk_prep(proj, pos_col, gq128, gk128, inv_tab)
    dilations = (1, 4, 16)
    fwd = [_attn_fwd(qr, kr, proj, d, "attn_fwd_d%d" % d) for d in dilations]
    attn, lse, attn_n = _attn_merge([o for o, _ in fwd], [l for _, l in fwd], p["g_attn_out"])

    xbc = _conv_fwd(proj, conv_w, conv_b)
    dt, a_cs = _ssd_prep(proj, dtb128, alog128)
    a_b = _tie(jnp.repeat(a_cs[:, :N_HEADS], HEAD, axis=1), None if hooks is None else hooks.rest_mid(xbc))
    dt_b = jnp.repeat(dt[:, :N_HEADS], HEAD, axis=1)
    at_r, dt_r = _row_layout(a_cs[:, :N_HEADS]), _row_layout(dt[:, :N_HEADS])
    y_ssd, h_all = _ssd_fwd(xbc, a_b, dt_b, at_r, dt_r)
    ssm_n = _ssd_post(y_ssd, xbc, proj, dskip_b, p["g_ssm_out"])

    if hooks is not None:
        wf = {**wf, **hooks.rest(ssm_n)}
    mix = jnp.concatenate([attn_n, ssm_n], axis=1)
    x1 = _matmul(mix, wf["w_out"], mode="nn", name="mm_out", tm=1024, tn=1024, tk=D,
                 extras=(x,), epilogue=lambda acc, r: (acc + r,))
    hc = _rms_fwd(x1, p["g_cross"], "rms_cross")
    memh = _rms_fwd(mem, p["g_mem"], "rms_mem")
    qc = _matmul(hc, wf["w_cq"], mode="nn", name="mm_cq", tm=1024, tn=512, tk=D)
    kv = _matmul(memh, wf["w_ckv"], mode="nn", name="mm_ckv", tm=N_MEM, tn=1024, tk=D)
    oc = _cross_fwd(qc, kv, p["g_cq"], p["g_ck"])
    x2 = _matmul(oc, wf["w_co"], mode="nn", name="mm_co", tm=1024, tn=256, tk=512, b_cb=256,
                 extras=(x1,), epilogue=lambda acc, r: (acc + r,))
    hm = _rms_fwd(x2, p["g_mlp"], "rms_mlp")

    def up_epi(acc):
        ru = jnp.maximum(acc, 0.0)
        return ru * ru, 2.0 * ru

    act, relu2 = _matmul(hm, wf["w_up"], mode="nn", name="mm_up", tm=1024, tn=1024, tk=D, b_cb=1024,
                         out_dtypes=(BF, BF), epilogue=up_epi)

    def loss_epi(acc, r, t):
        d = (acc + r - t) * (1.0 / D)
        return d, d

    dy, dyb = _matmul(act, wf["w_down"], mode="nn", name="mm_down", tm=512, tn=1024, tk=2048, extras=(x2, target),
                      out_dtypes=(F32, BF), epilogue=loss_epi)
    loss_part = _loss_sum(dy)[0, 0] * (0.5 * D)

    gb, gs = {}, {}
    gb["w_down"] = _matmul(act, dyb, mode="tn", name="mm_down_dw", tm=1024, tn=1024, tk=S,
                           out_dtypes=(BF,)).reshape(N_DEV, D_FF // N_DEV, D)
    token = None if hooks is None else hooks.pair_start("w_down", gb["w_down"])
    du = _matmul(dyb, wf["w_down"], mode="nt", name="mm_down_dx", tm=1024, tn=1024, tk=D, extras=(relu2,),
                 out_dtypes=(BF,), epilogue=lambda acc, r: (acc * r.astype(F32),), after=token)
    gb["w_up"] = _matmul(hm, du, mode="tn", name="mm_up_dw", tm=1024, tn=1024, tk=S, out_dtypes=(BF,), out_cb=1024)
    token = None if hooks is None else hooks.pair_start("w_up", gb["w_up"])
    dhm = _matmul(du, wf["w_up"], mode="nt", name="mm_up_dx", tm=1024, tn=1024, tk=1024, b_cb=1024, after=token)
    if hooks is not None:
        p["g_mlp"] = _tie(p["g_mlp"], hooks.emit({n: gb[n] for n in ("w_down", "w_up")}, dhm))
    dx2, dx2b, gs["g_mlp"] = _rms_bwd_call(x2, p["g_mlp"], dhm, dy, "rms_mlp_bwd")

    doc = _matmul(dx2b, wf["w_co"], mode="nt", name="mm_co_dx", tm=1024, tn=512, tk=256, b_cb=256)
    gb["w_co"] = _matmul(oc, dx2b, mode="tn", name="mm_co_dw", tm=512, tn=256, tk=S, out_dtypes=(BF,), out_cb=256)
    dqc, dkn, dvc, gs["g_cq"] = _cross_bwd(qc, doc, kv, p["g_cq"], p["g_ck"])
    dkv, gs["g_ck"] = _cross_kv_bwd(kv, dkn, dvc, p["g_ck"])
    gb["w_cq"] = _matmul(hc, dqc, mode="tn", name="mm_cq_dw", tm=1024, tn=512, tk=S,
                         out_dtypes=(BF,)).reshape(N_DEV, D // N_DEV, D_CROSS)
    dhc = _matmul(dqc, wf["w_cq"], mode="nt", name="mm_cq_dx", tm=1024, tn=1024, tk=512)
    gb["w_ckv"] = _matmul(memh, dkv, mode="tn", name="mm_ckv_dw", tm=1024, tn=1024, tk=N_MEM,
                          out_dtypes=(BF,)).reshape(N_DEV, D // N_DEV, 2 * D_CROSS)
    dmemh = _matmul(dkv, wf["w_ckv"], mode="nt", name="mm_ckv_dx", tm=N_MEM, tn=1024, tk=1024)
    (gs["g_mem"],) = _rms_bwd_call(mem, p["g_mem"], dmemh, None, "rms_mem_bwd")
    dx1, dx1b, gs["g_cross"] = _rms_bwd_call(x1, p["g_cross"], dhc, dx2, "rms_cross_bwd")

    dmix = _matmul(dx1b, wf["w_out"], mode="nt", name="mm_out_dx", tm=1024, tn=1024, tk=D)
    gb["w_out"] = _matmul(mix, dx1b, mode="tn", name="mm_out_dw", tm=1024, tn=1024, tk=S,
                          out_dtypes=(BF,)).reshape(N_DEV, D // N_DEV, D)
    if hooks is not None:
        p["g_attn_out"] = _tie(p["g_attn_out"],
                               hooks.emit({n: gb[n] for n in ("w_co", "w_cq", "w_ckv", "w_out")}, dmix))

    dattn, delta, gs["g_attn_out"] = _attn_merge_bwd(dmix, attn, p["g_attn_out"])
    bwd = [_attn_bwd(qr, kr, proj, dattn, lse, delta, d, "attn_bwd_d%d" % d) for d in dilations]
    dq_pre, dk_pre, dv_pre, dgq, dgk = _qk_bwd([t[0] for t in bwd], [t[1] for t in bwd], [t[2] for t in bwd],
                                               proj, pos_col, gq128, gk128, inv_tab)
    gs["g_q"], gs["g_k"] = dgq[:, :HEAD], dgk[:, :HEAD]

    dy_ssd, dz, dxs_skip, dd_lane, gs["g_ssm_out"] = _ssd_post_bwd(y_ssd, xbc, proj, dmix, dskip_b, p["g_ssm_out"])
    gs["d_skip"] = dd_lane.reshape(N_HEADS, HEAD).sum(axis=1).reshape(1, N_HEADS)
    dc_pair, daq_b, dx_ssd, db_pair, dak_r, ddt_r, ddt_b = _ssd_bwd(xbc, dy_ssd, h_all, a_b, dt_b, at_r, dt_r)
    daq = _pad_lanes(daq_b[:, ::HEAD])
    dak = _pad_lanes(_row_layout_inv(dak_r))
    ddt_direct = _pad_lanes(_row_layout_inv(ddt_r))
    ddt_raw, dalog, dbias = _ssd_prep_bwd(daq, dak, ddt_direct, _pad_lanes(ddt_b[:, ::HEAD]), dt, proj,
                                          dtb128, alog128)
    gs["a_log"], gs["dt_bias"] = dalog[:, :N_HEADS], dbias[:, :N_HEADS]
    same = lambda c: c
    dxbc_x, dcw_x, dcb_x = _conv_bwd(proj, conv_w, conv_b, dxs_skip, dx_ssd, same, same, 0, 8, "conv_bwd_x")
    dxbc_b, dcw_b, dcb_b = _conv_bwd(proj, conv_w, conv_b, db_pair, db_pair, lambda c: 2 * c, lambda c: 2 * c + 1,
                                     D_SSM, 4, "conv_bwd_b")
    dxbc_c, dcw_c, dcb_c = _conv_bwd(proj, conv_w, conv_b, dc_pair, dc_pair, lambda c: 2 * c, lambda c: 2 * c + 1,
                                     D_SSM + 512, 4, "conv_bwd_c")
    g_conv_w = jnp.concatenate([dcw_x, dcw_b, dcw_c], axis=1)
    gs["conv_b"] = jnp.concatenate([dcb_x, dcb_b, dcb_c], axis=1)

    dproj = jnp.concatenate([dq_pre, dk_pre, dv_pre, dz, dxbc_x, dxbc_b, dxbc_c, ddt_raw], axis=1)
    dw_in = _matmul(h, dproj, mode="tn", name="mm_in_dw", tm=1024, tn=896, tk=S, out_dtypes=(BF,))
    gb["w_in"] = _w_in_to_blocks(dw_in)
    token = None if hooks is None else hooks.emit({"w_in": gb["w_in"]}, dw_in)
    dh = _matmul(dproj, wf["w_in"], mode="nt", name="mm_in_dx", tm=1024, tn=1024, tk=896, after=token)
    grad_x, _, gs["g_mix"] = _rms_bwd_call(x, p["g_mix"], dh, dx1, "rms_mix_bwd")
    return loss_part, grad_x, gb, gs, g_conv_w


def kernel(x, mem, positions, g_mix, w_in, g_q, g_k, g_attn_out, conv_w, conv_b, dt_bias, a_log, d_skip, g_ssm_out, w_out, g_cross, g_mem, w_cq, w_ckv, g_cq, g_ck, w_co, g_mlp, w_up, w_down, loss_target, m_g_mix, m_w_in, m_g_q, m_g_k, m_g_attn_out, m_conv_w, m_conv_b, m_dt_bias, m_a_log, m_d_skip, m_g_ssm_out, m_w_out, m_g_cross, m_g_mem, m_w_cq, m_w_ckv, m_g_cq, m_g_ck, m_w_co, m_g_mlp, m_w_up, m_w_down, v_g_mix, v_w_in, v_g_q, v_g_k, v_g_attn_out, v_conv_w, v_conv_b, v_dt_bias, v_a_log, v_d_skip, v_g_ssm_out, v_w_out, v_g_cross, v_g_mem, v_w_cq, v_w_ckv, v_g_cq, v_g_ck, v_w_co, v_g_mlp, v_w_up, v_w_down):
    args = dict(locals())
    w = {n: args[n] for n in WEIGHTS}
    m = {n: args["m_" + n] for n in WEIGHTS}
    v = {n: args["v_" + n] for n in WEIGHTS}
    small = {n: w[n] for n, _ in SMALL}
    me = 4 * lax.axis_index("x") + 2 * lax.axis_index("y") + lax.axis_index("c")

    w_in_g, conv_w_g = _all_gather([w["w_in"][0].astype(BF), w["conv_w"][0]], "ag_first")
    wf = {"w_in": _w_in_from_blocks(w_in_g), "conv_w": conv_w_g.transpose(1, 0, 2).reshape(CONV_W, D_CONV)}
    overlap = _Overlap([w[n][0].astype(BF) for n in REST], w_in_g)
    p = dict(small)
    p["g_mix"] = _tie(p["g_mix"], overlap.token)

    loss_part, grad_x, _, gs, g_conv_w = _local_step(
        x[0], mem[0], positions.reshape(S, 1), loss_target[0], p, wf, overlap)
    loss = lax.psum(loss_part, ("x", "y", "c"))

    out = {}
    last = grad_x
    for gi in range(3):
        for n, parts in overlap.finish(gi, last).items():
            res_n = _adamw(w[n][0], m[n][0], v[n][0], parts, "adamw_" + n)
            out[n] = [t.reshape(w[n].shape) for t in res_n]
            last = res_n[0]

    sm_parts, cw_parts = _all_gather([_pack_small(gs), g_conv_w], "ag_small_grads", after=last)
    sm = [_unpack_small(t) for t in _adamw(_pack_small(small), _pack_small({n: m[n] for n, _ in SMALL}),
                                           _pack_small({n: v[n] for n, _ in SMALL}), sm_parts, "adamw_small")]
    for n, _ in SMALL:
        out[n] = [t[n] for t in sm]
    cols = D_CONV // N_DEV
    cw_mine = lax.dynamic_slice_in_dim(cw_parts, me * cols, cols, axis=2)
    out["conv_w"] = [t.reshape(w["conv_w"].shape) for t in
                     _adamw(w["conv_w"][0], m["conv_w"][0], v["conv_w"][0], cw_mine, "adamw_conv_w")]

    res = [loss, grad_x.reshape(x.shape)]
    for k in range(4):
        res += [out[n][k] for n in WEIGHTS]
    return tuple(res)
```

```python
import functools
import math

import numpy as np
import jax
import jax.numpy as jnp
from jax import lax
from jax.experimental import pallas as pl
from jax.experimental.pallas import tpu as pltpu

F32 = jnp.float32
BF = jnp.bfloat16

N_DEV = 8
S = 2048
D = 2048
D_ATTN = 1024
N_HEADS = 16
HEAD = 64
ROT = 16
ROPE_THETA = 500000.0
D_SSM = 1024
D_CONV = 2048
CONV_W = 4
N_MEM = 256
D_CROSS = 512
CROSS_HEAD = 128
D_FF = 8192
D_IN = 6160
D_INP = 6272
EPS = 1e-6
BLK = 128
NB = S // BLK
LANES = 128
NEG = -1e30

ADAM_LR = 0.001
ADAM_B1 = 0.9
ADAM_B2 = 0.999
ADAM_EPS = 1e-08
ADAM_WD = 0.01
ADAM_STEP = 10

VMEM_LIMIT_BYTES = 48 * 1024 * 1024
ROW_TILE = 256


def _params(*sem):
    return pltpu.CompilerParams(dimension_semantics=sem, vmem_limit_bytes=VMEM_LIMIT_BYTES)


def _matmul(a, b, *, mode, name, tm, tn, tk, out_dtypes=(F32,), b_cb=None, out_cb=None,
            extras=(), epilogue=None, after=None):
    if mode == "tn":
        kk, m = a.shape
    else:
        m, kk = a.shape
    if b_cb is None:
        br, bc = b.shape
    else:
        br, bc = b.shape[1], N_DEV * b_cb
    n = br if mode == "nt" else bc
    assert m % tm == 0 and n % tn == 0 and kk % tk == 0, (name, m, n, kk)
    nk = kk // tk
    grid = (m // tm, n // tn, nk)

    if mode == "tn":
        a_spec = pl.BlockSpec((tk, tm), lambda i, j, k: (k, i))
    else:
        a_spec = pl.BlockSpec((tm, tk), lambda i, j, k: (i, k))
    if mode == "nt":
        b_blk, b_idx = (tn, tk), (lambda i, j, k: (j, k))
    else:
        b_blk, b_idx = (tk, tn), (lambda i, j, k: (k, j))
    if b_cb is None:
        b_spec = pl.BlockSpec(b_blk, b_idx)
    else:
        tc = b_blk[1]
        assert b_cb % tc == 0
        per = b_cb // tc

        def b_idx3(i, j, k):
            r, c = b_idx(i, j, k)
            return (c // per, r, c % per)
        b_spec = pl.BlockSpec((None,) + b_blk, b_idx3)
    ex_specs = [pl.BlockSpec((tm, tn), lambda i, j, k: (i, j)) for _ in extras]
    if out_cb is None:
        out_shape = [jax.ShapeDtypeStruct((m, n), dt) for dt in out_dtypes]
        out_specs = [pl.BlockSpec((tm, tn), lambda i, j, k: (i, j)) for _ in out_dtypes]
    else:
        assert out_cb % tn == 0
        pero = out_cb // tn
        out_shape = [jax.ShapeDtypeStruct((N_DEV, m, out_cb), dt) for dt in out_dtypes]
        out_specs = [pl.BlockSpec((None, tm, tn), lambda i, j, k: (j // pero, i, j % pero)) for _ in out_dtypes]
    dn = {"nn": (((1,), (0,)), ((), ())), "nt": (((1,), (1,)), ((), ())), "tn": (((0,), (0,)), ((), ()))}[mode]
    ne, no = len(extras), len(out_dtypes)
    n_after = 0 if after is None else 1

    def kern(a_ref, b_ref, *rest):
        ex_refs, out_refs = rest[:ne], rest[ne + n_after:ne + n_after + no]

        def finish(acc):
            outs = (acc,) if epilogue is None else epilogue(acc, *[r[...] for r in ex_refs])
            for r, o in zip(out_refs, outs):
                r[...] = o.astype(r.dtype)

        part = lax.dot_general(a_ref[...].astype(BF), b_ref[...].astype(BF), dn, preferred_element_type=F32)
        if nk == 1:
            finish(part)
            return
        acc_ref = rest[ne + n_after + no]
        k = pl.program_id(2)

        @pl.when(k == 0)
        def _():
            acc_ref[...] = part

        @pl.when(k > 0)
        def _():
            acc_ref[...] += part

        @pl.when(k == nk - 1)
        def _():
            finish(acc_ref[...])

    res = pl.pallas_call(
        kern, name=name, grid=grid, in_specs=[a_spec, b_spec] + ex_specs + [ANY] * n_after, out_specs=out_specs,
        out_shape=out_shape, scratch_shapes=[pltpu.VMEM((tm, tn), F32)] if nk > 1 else [],
        compiler_params=_params("parallel", "parallel", "arbitrary"),
    )(a, b, *extras, *([] if after is None else [after]))
    return res[0] if no == 1 else tuple(res)


def _rowwise(body, *, name, nrows, tile, row_ins, full_ins=(), row_outs=(), acc_outs=(), scratch=(),
             reverse=False):
    n = nrows // tile

    def ridx(i):
        return (n - 1 - i) if reverse else i

    in_specs, args = [], []
    for arr, width, cb in row_ins:
        in_specs.append(pl.BlockSpec((tile, width), lambda i, cb=cb: (ridx(i), cb)))
        args.append(arr)
    for arr in full_ins:
        in_specs.append(pl.BlockSpec(arr.shape, lambda i, nd=arr.ndim: (0,) * nd))
        args.append(arr)
    out_shape, out_specs = [], []
    for width, dt in row_outs:
        out_shape.append(jax.ShapeDtypeStruct((nrows, width), dt))
        out_specs.append(pl.BlockSpec((tile, width), lambda i: (ridx(i), 0)))
    for shp, dt in acc_outs:
        out_shape.append(jax.ShapeDtypeStruct(shp, dt))
        out_specs.append(pl.BlockSpec(shp, lambda i, nd=len(shp): (0,) * nd))

    def kern(*refs):
        body(pl.program_id(0), n, *refs)

    return pl.pallas_call(kern, name=name, grid=(n,), in_specs=in_specs, out_specs=out_specs,
                          out_shape=out_shape, scratch_shapes=list(scratch),
                          compiler_params=_params("arbitrary"))(*args)


def _accum(ref, val, i):
    @pl.when(i == 0)
    def _():
        ref[...] = val

    @pl.when(i > 0)
    def _():
        ref[...] += val


def _sigmoid(x):
    return 1.0 / (1.0 + jnp.exp(-x))


def _rms_n(x):
    r = lax.rsqrt(jnp.mean(x * x, axis=-1, keepdims=True) + EPS)
    return x * r, r


def _rms_bwd(x, g, dh):
    n, r = _rms_n(x)
    dn = dh * g
    dx = r * (dn - n * jnp.mean(dn * n, axis=-1, keepdims=True))
    return dx, jnp.sum(dh * n, axis=0, keepdims=True)


def _seg_sum(x, seg):
    t, w = x.shape
    tiles = [x[:, LANES * j:LANES * (j + 1)] for j in range(w // LANES)]
    outs = []
    if seg == 64:
        lo = lax.broadcasted_iota(jnp.int32, (t, LANES), 1) < 64
        for xt in tiles:
            s_lo = jnp.sum(jnp.where(lo, xt, 0.0), axis=-1, keepdims=True)
            s_hi = jnp.sum(jnp.where(lo, 0.0, xt), axis=-1, keepdims=True)
            outs.append(jnp.where(lo, s_lo, s_hi))
    else:
        sums = [jnp.sum(xt, axis=-1, keepdims=True) for xt in tiles]
        if seg == 256:
            sums = [sums[2 * (j // 2)] + sums[2 * (j // 2) + 1] for j in range(len(sums))]
        else:
            assert seg == 128
        outs = [jnp.broadcast_to(s, (t, LANES)) for s in sums]
    return outs[0] if len(outs) == 1 else jnp.concatenate(outs, axis=1)


def _seg_rms_n(x, seg):
    r = lax.rsqrt(_seg_sum(x * x, seg) * (1.0 / seg) + EPS)
    return x * r, r


def _seg_rms_bwd(x, g, dy, seg):
    n, r = _seg_rms_n(x, seg)
    dn = dy * g
    dx = r * (dn - n * (_seg_sum(dn * n, seg) * (1.0 / seg)))
    return dx, jnp.sum(dy * n, axis=0, keepdims=True)


def _rope_tables(pos_col, inv_tab, t):
    ang = pos_col.astype(F32) * inv_tab
    idx = lax.broadcasted_iota(jnp.int32, (t, LANES), 1) % HEAD
    cos, sin = jnp.cos(ang), jnp.sin(ang)
    c = jnp.where(idx < ROT, cos, 1.0)
    sg = jnp.where(idx < ROT // 2, -sin, jnp.where(idx < ROT, sin, 0.0))
    return c, sg, idx < ROT // 2


def _rope(x, c, sg, lo):
    partner = jnp.where(lo, pltpu.roll(x, LANES - ROT // 2, 1), pltpu.roll(x, ROT // 2, 1))
    return x * c + partner * sg


def _fold_heads(v):
    v8 = jnp.broadcast_to(v, (8, LANES))
    return (v8 + pltpu.roll(v8, HEAD, 1))[0:1]


def _dot(a, b, dn):
    return lax.dot_general(a, b, (dn, ((), ())), preferred_element_type=F32)


NN = ((1,), (0,))
NT = ((1,), (1,))
TN = ((0,), (0,))


def _dot3(l01, x):
    x1 = x.astype(BF)
    r1 = x - x1.astype(F32)
    x2 = r1.astype(BF)
    x3 = (r1 - x2.astype(F32)).astype(BF)
    return _dot(l01, x1, NN) + _dot(l01, x2, NN) + _dot(l01, x3, NN)


def _rms_fwd(x, g, name):
    rows = x.shape[0]

    def body(i, n, x_ref, g_ref, o_ref):
        nx, _ = _rms_n(x_ref[...])
        o_ref[...] = (nx * g_ref[...]).astype(BF)

    return _rowwise(body, name=name, nrows=rows, tile=min(ROW_TILE, rows), row_ins=[(x, D, 0)],
                    full_ins=[g], row_outs=[(D, BF)])[0]


def _qk_prep(proj, pos_col, gq128, gk128, inv_tab):
    scale = HEAD ** -0.5

    def body(i, n, q_ref, k_ref, p_ref, gq_ref, gk_ref, it_ref, qo_ref, ko_ref):
        t = q_ref.shape[0]
        c, sg, lo = _rope_tables(p_ref[...], it_ref[...], t)
        for j in range(D_ATTN // LANES):
            sl = slice(LANES * j, LANES * (j + 1))
            qn, _ = _seg_rms_n(q_ref[:, sl], HEAD)
            kn, _ = _seg_rms_n(k_ref[:, sl], HEAD)
            qo_ref[:, sl] = _rope(qn * gq_ref[...], c, sg, lo) * scale
            ko_ref[:, sl] = _rope(kn * gk_ref[...], c, sg, lo)

    return _rowwise(body, name="qk_prep", nrows=S, tile=ROW_TILE,
                    row_ins=[(proj, D_ATTN, 0), (proj, D_ATTN, 1), (pos_col, 1, 0)],
                    full_ins=[gq128, gk128, inv_tab], row_outs=[(D_ATTN, F32)] * 2)


ATTN_QB = 4
V_COLS = pl.BlockSpec((S, LANES), lambda p, b: (0, 2 * D_ATTN // LANES + p))


def _band(b, d):
    nb = NB // d
    r, n = b // nb, b % nb
    width, kn = (BLK, n) if nb == 1 else (2 * BLK, jnp.maximum(n - 1, 0))

    def rows(first_member, count):
        if d == 1:
            return pl.ds(pl.multiple_of(first_member, BLK), count)
        return pl.ds(first_member * d + r, count, stride=d)

    qm = n * BLK + lax.broadcasted_iota(jnp.int32, (BLK, width), 0)
    km = kn * BLK + lax.broadcasted_iota(jnp.int32, (BLK, width), 1)
    valid = km <= qm
    if nb > 1:
        valid = valid & (km >= qm - BLK)
    return rows(n * BLK, BLK), rows(kn * BLK, width), valid


def _attn_fwd(q, k, v, d, name):
    def kern(q_ref, k_ref, v_ref, o_ref, l_ref):
        half0 = lax.broadcasted_iota(jnp.int32, (BLK, LANES), 1) < HEAD
        for bb in range(ATTN_QB):
            q_rows, k_rows, valid = _band(pl.program_id(1) * ATTN_QB + bb, d)
            q = q_ref[q_rows, :]
            kb, vb = k_ref[k_rows, :].astype(BF), v_ref[k_rows, :].astype(BF)
            o_h, l_h = [], []
            for h in range(2):
                hm = half0 if h == 0 else jnp.logical_not(half0)
                s = jnp.where(valid, _dot(jnp.where(hm, q, 0.0).astype(BF), kb, NT), NEG)
                m = jnp.max(s, axis=-1, keepdims=True)
                p = jnp.exp(s - m)
                den = jnp.sum(p, axis=-1, keepdims=True)
                o_h.append(_dot(p.astype(BF), vb, NN) / den)
                l_h.append(m + jnp.log(den))
            o_ref[q_rows, :] = jnp.where(half0, o_h[0], o_h[1])
            l_ref[q_rows, :] = jnp.where(half0, l_h[0], l_h[1])

    seq = pl.BlockSpec((S, LANES), lambda p, b: (0, p))
    return pl.pallas_call(
        kern, name=name, grid=(D_ATTN // LANES, NB // ATTN_QB), in_specs=[seq, seq, V_COLS], out_specs=[seq, seq],
        out_shape=[jax.ShapeDtypeStruct((S, D_ATTN), F32)] * 2,
        compiler_params=_params("parallel", "arbitrary"))(q, k, v)


def _attn_merge(os_, ls_, g_attn):
    def body(i, n, o1, o2, o3, l1, l2, l3, g_ref, a_ref, lse_ref, an_ref):
        la, lb, lc = l1[...], l2[...], l3[...]
        m = jnp.maximum(jnp.maximum(la, lb), lc)
        ea, eb, ec = jnp.exp(la - m), jnp.exp(lb - m), jnp.exp(lc - m)
        den = ea + eb + ec
        attn = (ea * o1[...] + eb * o2[...] + ec * o3[...]) / den
        a_ref[...] = attn
        lse_ref[...] = m + jnp.log(den)
        nx, _ = _rms_n(attn)
        an_ref[...] = (nx * g_ref[...]).astype(BF)

    return _rowwise(body, name="attn_merge", nrows=S, tile=ROW_TILE,
                    row_ins=[(a, D_ATTN, 0) for a in (*os_, *ls_)], full_ins=[g_attn],
                    row_outs=[(D_ATTN, F32), (D_ATTN, F32), (D_ATTN, BF)])


def _conv_taps(x, w_ref):
    rows = lax.broadcasted_iota(jnp.int32, x.shape, 0)
    shifted = []
    for w in range(CONV_W):
        k = CONV_W - 1 - w
        shifted.append(x if k == 0 else jnp.where(rows >= k, pltpu.roll(x, k, 0), 0.0))
    acc = shifted[0] * w_ref[0:1, :]
    for w in range(1, CONV_W):
        acc = acc + shifted[w] * w_ref[w:w + 1, :]
    return acc, shifted


def _conv_fwd(proj, conv_w, conv_b):
    tc = 256

    def kern(x_ref, w_ref, b_ref, o_ref):
        c, _ = _conv_taps(x_ref[...], w_ref)
        c = c + b_ref[...]
        o_ref[...] = c * _sigmoid(c)

    return pl.pallas_call(
        kern, name="conv_fwd", grid=(D_CONV // tc,),
        in_specs=[pl.BlockSpec((S, tc), lambda c: (0, 4 * D_ATTN // tc + c)),
                  pl.BlockSpec((CONV_W, tc), lambda c: (0, c)), pl.BlockSpec((1, tc), lambda c: (0, c))],
        out_specs=pl.BlockSpec((S, tc), lambda c: (0, c)),
        out_shape=jax.ShapeDtypeStruct((S, D_CONV), F32), compiler_params=_params("parallel"))(proj, conv_w, conv_b)


def _softplus(x):
    return jnp.maximum(x, 0.0) + jnp.log1p(jnp.exp(-jnp.abs(x)))


def _ssd_prep(proj, dt_bias128, a_log128):
    def body(i, n, raw_ref, b_ref, al_ref, dt_ref, a_ref, carry):
        @pl.when(i == 0)
        def _():
            carry[...] = jnp.zeros_like(carry)

        dt = _softplus(raw_ref[...] + b_ref[...])
        da = dt * (-jnp.exp(al_ref[...]))
        tri = (lax.broadcasted_iota(jnp.int32, (BLK, BLK), 0) >= lax.broadcasted_iota(jnp.int32, (BLK, BLK), 1))
        cs = _dot3(tri.astype(BF), da) + carry[0:1, :]
        dt_ref[...] = dt
        a_ref[...] = cs
        carry[...] = jnp.broadcast_to(cs[BLK - 1:BLK, :], carry.shape)

    return _rowwise(body, name="ssd_prep", nrows=S, tile=BLK, row_ins=[(proj, LANES, (D_INP - LANES) // LANES)],
                    full_ins=[dt_bias128, a_log128], row_outs=[(LANES, F32), (LANES, F32)],
                    scratch=[pltpu.VMEM((8, LANES), F32)])


def _ssd_decay(a_col, at_row, causal):
    diff = jnp.where(causal, a_col - at_row, 0.0)
    return jnp.where(causal, jnp.exp(diff), 0.0)


SSD_CB = 2


def _ssd_fwd(xbc, a_b, dt_b, at_r, dt_r):
    def kern(c_ref, b_ref, x_ref, ab_ref, dtb_ref, at_ref, dt_ref, y_ref, hall_ref, h_scr, aprev_scr):
        i = pl.program_id(1)

        @pl.when(i == 0)
        def _():
            h_scr[...] = jnp.zeros_like(h_scr)
            aprev_scr[...] = jnp.zeros_like(aprev_scr)

        half0 = lax.broadcasted_iota(jnp.int32, (BLK, LANES), 1) < HEAD
        causal = lax.broadcasted_iota(jnp.int32, (BLK, BLK), 1) <= lax.broadcasted_iota(jnp.int32, (BLK, BLK), 0)
        h, a_prev = h_scr[...], aprev_scr[0:1, :]
        for cc in range(SSD_CB):
            rows = slice(cc * BLK, (cc + 1) * BLK)
            ci, bi, x = c_ref[rows, :].astype(BF), b_ref[rows, :].astype(BF), x_ref[rows, :]
            ab = ab_ref[rows, :]
            a_end = ab[BLK - 1:BLK, :]
            alpha = jnp.exp(ab - a_prev)
            beta = jnp.exp(a_end - ab) * dtb_ref[rows, :]
            hall_ref[rows, :] = h
            cb = _dot(ci, bi, NT)
            at, dtj = at_ref[cc], dt_ref[cc]
            y = alpha * _dot(ci, h.astype(BF), NN)
            for hd in range(2):
                hm = half0 if hd == 0 else jnp.logical_not(half0)
                w = cb * _ssd_decay(ab[:, HEAD * hd:HEAD * hd + 1], at[hd:hd + 1, :], causal) * dtj[hd:hd + 1, :]
                y = y + _dot(w.astype(BF), jnp.where(hm, x, 0.0).astype(BF), NN)
            y_ref[rows, :] = y
            h = alpha[BLK - 1:BLK, :] * h + _dot(bi, (beta * x).astype(BF), TN)
            a_prev = a_end
        h_scr[...] = h
        aprev_scr[...] = jnp.broadcast_to(a_prev, aprev_scr.shape)

    npair = D_SSM // LANES
    rowvec = pl.BlockSpec((None, SSD_CB, 2, BLK), lambda p, i: (p, i, 0, 0))
    tile = pl.BlockSpec((SSD_CB * BLK, LANES), lambda p, i: (i, p))
    return pl.pallas_call(
        kern, name="ssd_fwd", grid=(npair, NB // SSD_CB),
        in_specs=[pl.BlockSpec((SSD_CB * BLK, LANES), lambda p, i: (i, 12 + p // 2)),
                  pl.BlockSpec((SSD_CB * BLK, LANES), lambda p, i: (i, 8 + p // 2)), tile, tile, tile, rowvec, rowvec],
        out_specs=[tile, tile], out_shape=[jax.ShapeDtypeStruct((S, D_SSM), F32)] * 2,
        scratch_shapes=[pltpu.VMEM((BLK, LANES), F32), pltpu.VMEM((8, LANES), F32)],
        compiler_params=_params("parallel", "arbitrary"))(xbc, xbc, xbc, a_b, dt_b, at_r, dt_r)


def _ssd_post(y_ssd, xbc, proj, dskip_b, g_ssm):
    def body(i, n, y_ref, xs_ref, z_ref, d_ref, g_ref, o_ref):
        z = z_ref[...]
        y2 = (y_ref[...] + d_ref[...] * xs_ref[...]) * (z * _sigmoid(z))
        nx, _ = _seg_rms_n(y2, 256)
        o_ref[...] = (nx * g_ref[...]).astype(BF)

    return _rowwise(body, name="ssd_post", nrows=S, tile=ROW_TILE,
                    row_ins=[(y_ssd, D_SSM, 0), (xbc, D_SSM, 0), (proj, D_SSM, 3)], full_ins=[dskip_b, g_ssm],
                    row_outs=[(D_SSM, BF)])[0]


def _cross_heads(q, kv_ref, gq, gk):
    out = []
    for h in range(D_CROSS // CROSS_HEAD):
        sl = slice(CROSS_HEAD * h, CROSS_HEAD * (h + 1))
        nq, rq = _rms_n(q[:, sl])
        nk, rk = _rms_n(kv_ref[:, sl])
        v = kv_ref[:, D_CROSS + CROSS_HEAD * h:D_CROSS + CROSS_HEAD * (h + 1)]
        out.append((sl, nq, rq, nk, rk, v))
    return out


def _cross_fwd(qc, kv, g_cq, g_ck):
    scale = CROSS_HEAD ** -0.5

    def body(i, n, q_ref, kv_ref, gq_ref, gk_ref, o_ref):
        for sl, nq, _, nk, _, v in _cross_heads(q_ref[...], kv_ref, gq_ref[...], gk_ref[...]):
            qn = (nq * gq_ref[...] * scale).astype(BF)
            kn = (nk * gk_ref[...]).astype(BF)
            s = _dot(qn, kn, NT)
            e = jnp.exp(s - jnp.max(s, axis=-1, keepdims=True))
            p = e / jnp.sum(e, axis=-1, keepdims=True)
            o_ref[:, sl] = _dot(p.astype(BF), v.astype(BF), NN).astype(BF)

    return _rowwise(body, name="cross_fwd", nrows=S, tile=ROW_TILE, row_ins=[(qc, D_CROSS, 0)],
                    full_ins=[kv, g_cq, g_ck], row_outs=[(D_CROSS, BF)])[0]


def _loss_sum(dy):
    def body(i, n, d_ref, o_ref):
        d = d_ref[...]
        s = jnp.sum(jnp.sum(d * d, axis=0, keepdims=True), axis=1, keepdims=True)
        _accum(o_ref, s, i)

    return _rowwise(body, name="loss_sum", nrows=S, tile=ROW_TILE, row_ins=[(dy, D, 0)],
                    acc_outs=[((1, 1), F32)])[0]


def _rms_bwd_call(x, g, dh, dres, name):
    rows = x.shape[0]
    has_res = dres is not None

    def body(i, n, *refs):
        if has_res:
            x_ref, dh_ref, dr_ref, g_ref, dx_ref, dxb_ref, dg_ref = refs
        else:
            x_ref, dh_ref, g_ref, dg_ref = refs
        dx, dg = _rms_bwd(x_ref[...], g_ref[...], dh_ref[...])
        if has_res:
            dx = dx + dr_ref[...]
            dx_ref[...] = dx
            dxb_ref[...] = dx.astype(BF)
        _accum(dg_ref, dg, i)

    row_ins = [(x, D, 0), (dh, D, 0)] + ([(dres, D, 0)] if has_res else [])
    return _rowwise(body, name=name, nrows=rows, tile=min(ROW_TILE, rows), row_ins=row_ins, full_ins=[g],
                    row_outs=[(D, F32), (D, BF)] if has_res else [], acc_outs=[((1, D), F32)])


def _cross_bwd(qc, doc, kv, g_cq, g_ck):
    scale = CROSS_HEAD ** -0.5

    def body(i, n, q_ref, do_ref, kv_ref, gq_ref, gk_ref, dq_ref, dkn_ref, dv_ref, dgq_ref):
        dgq = jnp.zeros((1, CROSS_HEAD), F32)
        dkn_parts, dv_parts = [], []
        for sl, nq, rq, nk, _, v in _cross_heads(q_ref[...], kv_ref, gq_ref[...], gk_ref[...]):
            qn = (nq * gq_ref[...] * scale).astype(BF)
            kn = (nk * gk_ref[...]).astype(BF)
            s = _dot(qn, kn, NT)
            e = jnp.exp(s - jnp.max(s, axis=-1, keepdims=True))
            p = e / jnp.sum(e, axis=-1, keepdims=True)
            do = do_ref[:, sl].astype(BF)
            dv_parts.append(_dot(p.astype(BF), do, TN))
            dp = _dot(do, v.astype(BF), NT)
            ds = (p * (dp - jnp.sum(dp * p, axis=-1, keepdims=True))).astype(BF)
            dqn = _dot(ds, kn, NN)
            dkn_parts.append(_dot(ds, qn, TN))
            dn = dqn * (gq_ref[...] * scale)
            dq_ref[:, sl] = (rq * (dn - nq * jnp.mean(dn * nq, axis=-1, keepdims=True))).astype(BF)
            dgq = dgq + jnp.sum(dqn * scale * nq, axis=0, keepdims=True)
        _accum(dkn_ref, jnp.concatenate(dkn_parts, axis=1), i)
        _accum(dv_ref, jnp.concatenate(dv_parts, axis=1), i)
        _accum(dgq_ref, dgq, i)

    return _rowwise(body, name="cross_bwd", nrows=S, tile=ROW_TILE, row_ins=[(qc, D_CROSS, 0), (doc, D_CROSS, 0)],
                    full_ins=[kv, g_cq, g_ck], row_outs=[(D_CROSS, BF)],
                    acc_outs=[((N_MEM, D_CROSS), F32), ((N_MEM, D_CROSS), F32), ((1, CROSS_HEAD), F32)])


def _cross_kv_bwd(kv, dkn, dv, g_ck):
    def body(i, n, kv_ref, dkn_ref, dv_ref, gk_ref, dkv_ref, dgk_ref):
        dgk = jnp.zeros((1, CROSS_HEAD), F32)
        for h in range(D_CROSS // CROSS_HEAD):
            sl = slice(CROSS_HEAD * h, CROSS_HEAD * (h + 1))
            dk, dg = _rms_bwd(kv_ref[:, sl], gk_ref[...], dkn_ref[:, sl])
            dkv_ref[:, sl] = dk.astype(BF)
            dgk = dgk + dg
        dkv_ref[:, D_CROSS:] = dv_ref[...].astype(BF)
        dgk_ref[...] = dgk

    return _rowwise(body, name="cross_kv_bwd", nrows=N_MEM, tile=N_MEM,
                    row_ins=[(kv, 2 * D_CROSS, 0), (dkn, D_CROSS, 0), (dv, D_CROSS, 0)], full_ins=[g_ck],
                    row_outs=[(2 * D_CROSS, BF)], acc_outs=[((1, CROSS_HEAD), F32)])


def _attn_merge_bwd(dmix, attn, g_attn):
    def body(i, n, dn_ref, a_ref, g_ref, da_ref, dl_ref, dg_ref):
        attn_v = a_ref[...]
        da, dg = _rms_bwd(attn_v, g_ref[...], dn_ref[...])
        da_ref[...] = da
        dl_ref[...] = _seg_sum(da * attn_v, HEAD)
        _accum(dg_ref, dg, i)

    return _rowwise(body, name="attn_merge_bwd", nrows=S, tile=ROW_TILE,
                    row_ins=[(dmix, D_ATTN, 0), (attn, D_ATTN, 0)], full_ins=[g_attn],
                    row_outs=[(D_ATTN, F32), (D_ATTN, F32)], acc_outs=[((1, D_ATTN), F32)])


def _attn_bwd(q, k, v, do, lse, delta, d, name):
    def kern(q_ref, k_ref, v_ref, do_ref, l_ref, d_ref, dq_ref, dk_ref, dv_ref):
        @pl.when(pl.program_id(1) == 0)
        def _():
            dk_ref[...] = jnp.zeros_like(dk_ref)
            dv_ref[...] = jnp.zeros_like(dv_ref)

        half0 = lax.broadcasted_iota(jnp.int32, (BLK, LANES), 1) < HEAD
        updates = []
        for bb in range(ATTN_QB):
            q_rows, k_rows, valid = _band(pl.program_id(1) * ATTN_QB + bb, d)
            q, do = q_ref[q_rows, :], do_ref[q_rows, :]
            lse_t, del_t = l_ref[q_rows, :], d_ref[q_rows, :]
            kb, vb = k_ref[k_rows, :].astype(BF), v_ref[k_rows, :].astype(BF)
            dq_h, dk, dv = [], None, None
            for h in range(2):
                hm = half0 if h == 0 else jnp.logical_not(half0)
                qm = jnp.where(hm, q, 0.0).astype(BF)
                dom = jnp.where(hm, do, 0.0).astype(BF)
                s = jnp.where(valid, _dot(qm, kb, NT), NEG)
                p = jnp.exp(s - lse_t[:, HEAD * h:HEAD * h + 1])
                ds = (p * (_dot(dom, vb, NT) - del_t[:, HEAD * h:HEAD * h + 1])).astype(BF)
                dq_h.append(_dot(ds, kb, NN))
                dk_h, dv_h = _dot(ds, qm, TN), _dot(p.astype(BF), dom, TN)
                dk, dv = (dk_h, dv_h) if h == 0 else (dk + dk_h, dv + dv_h)
            dq_ref[q_rows, :] = jnp.where(half0, dq_h[0], dq_h[1])
            updates.append((k_rows, dk, dv))
        for k_rows, dk, dv in updates:
            dk_ref[k_rows, :] += dk
            dv_ref[k_rows, :] += dv

    seq = pl.BlockSpec((S, LANES), lambda p, b: (0, p))
    return pl.pallas_call(
        kern, name=name, grid=(D_ATTN // LANES, NB // ATTN_QB), in_specs=[seq, seq, V_COLS, seq, seq, seq],
        out_specs=[seq, seq, seq], out_shape=[jax.ShapeDtypeStruct((S, D_ATTN), F32)] * 3,
        compiler_params=_params("parallel", "arbitrary"))(q, k, v, do, lse, delta)


def _qk_bwd(dqs, dks, dvs, proj, pos_col, gq128, gk128, inv_tab):
    scale = HEAD ** -0.5

    def body(i, n, *refs):
        dq_refs, dk_refs, dv_refs = refs[0:3], refs[3:6], refs[6:9]
        q_ref, k_ref, p_ref, gq_ref, gk_ref, it_ref = refs[9:15]
        dqo_ref, dko_ref, dvo_ref, dgq_ref, dgk_ref = refs[15:20]
        t = q_ref.shape[0]
        c, sg, lo = _rope_tables(p_ref[...], it_ref[...], t)
        dgq = jnp.zeros((1, LANES), F32)
        dgk = jnp.zeros((1, LANES), F32)
        for j in range(D_ATTN // LANES):
            sl = slice(LANES * j, LANES * (j + 1))
            dqr = _rope(dq_refs[0][:, sl] + dq_refs[1][:, sl] + dq_refs[2][:, sl], c, -sg, lo) * scale
            dkr = _rope(dk_refs[0][:, sl] + dk_refs[1][:, sl] + dk_refs[2][:, sl], c, -sg, lo)
            dq, gq = _seg_rms_bwd(q_ref[:, sl], gq_ref[...], dqr, HEAD)
            dk, gk = _seg_rms_bwd(k_ref[:, sl], gk_ref[...], dkr, HEAD)
            dqo_ref[:, sl] = dq.astype(BF)
            dko_ref[:, sl] = dk.astype(BF)
            dgq, dgk = dgq + gq, dgk + gk
        dvo_ref[...] = (dv_refs[0][...] + dv_refs[1][...] + dv_refs[2][...]).astype(BF)
        _accum(dgq_ref, _fold_heads(dgq), i)
        _accum(dgk_ref, _fold_heads(dgk), i)

    return _rowwise(body, name="qk_bwd", nrows=S, tile=ROW_TILE,
                    row_ins=[(a, D_ATTN, 0) for a in (*dqs, *dks, *dvs)]
                    + [(proj, D_ATTN, 0), (proj, D_ATTN, 1), (pos_col, 1, 0)],
                    full_ins=[gq128, gk128, inv_tab], row_outs=[(D_ATTN, BF)] * 3,
                    acc_outs=[((1, LANES), F32)] * 2)


def _ssd_post_bwd(y_ssd, xbc, proj, dmix, dskip_b, g_ssm):
    def body(i, n, y_ref, xs_ref, z_ref, do_ref, d_ref, g_ref, dy_ref, dz_ref, dxs_ref, dd_ref, dg_ref):
        z, xs = z_ref[...], xs_ref[...]
        sg = _sigmoid(z)
        gate = z * sg
        y = y_ref[...] + d_ref[...] * xs
        dy2, dg = _seg_rms_bwd(y * gate, g_ref[...], do_ref[...], 256)
        dy = dy2 * gate
        dy_ref[...] = dy.astype(BF)
        dz_ref[...] = (dy2 * y * (sg * (1.0 + z * (1.0 - sg)))).astype(BF)
        dxs_ref[...] = dy * d_ref[...]
        _accum(dd_ref, jnp.sum(dy * xs, axis=0, keepdims=True), i)
        _accum(dg_ref, dg, i)

    return _rowwise(body, name="ssd_post_bwd", nrows=S, tile=ROW_TILE,
                    row_ins=[(y_ssd, D_SSM, 0), (xbc, D_SSM, 0), (proj, D_SSM, 3), (dmix, D_SSM, 1)],
                    full_ins=[dskip_b, g_ssm], row_outs=[(D_SSM, BF), (D_SSM, BF), (D_SSM, F32)],
                    acc_outs=[((1, D_SSM), F32), ((1, D_SSM), F32)])


def _ssd_bwd(xbc, dy, h_all, a_b, dt_b, at_r, dt_r):
    def kern(c_ref, b_ref, x_ref, dy_ref, hall_ref, ab_ref, abp_ref, dtb_ref, at_ref, dt_ref,
             dc_ref, daq_ref, dx_ref, db_ref, dak_ref, ddt_ref, ddtb_ref, dh_scr, carry_scr):
        i = pl.program_id(1)

        @pl.when(i == 0)
        def _():
            dh_scr[...] = jnp.zeros_like(dh_scr)
            carry_scr[...] = jnp.zeros_like(carry_scr)

        half0 = lax.broadcasted_iota(jnp.int32, (BLK, LANES), 1) < HEAD
        hms = (half0, jnp.logical_not(half0))
        causal = lax.broadcasted_iota(jnp.int32, (BLK, BLK), 1) <= lax.broadcasted_iota(jnp.int32, (BLK, BLK), 0)
        row = lax.broadcasted_iota(jnp.int32, (BLK, LANES), 0)
        dh_next, carry = dh_scr[...], carry_scr[0:1, :]
        for cc in reversed(range(SSD_CB)):
            rows = slice(cc * BLK, (cc + 1) * BLK)
            ci, bi, x = c_ref[rows, :].astype(BF), b_ref[rows, :].astype(BF), x_ref[rows, :]
            dyv = dy_ref[rows, :].astype(F32)
            ab, dtb = ab_ref[rows, :], dtb_ref[rows, :]
            if cc > 0:
                a_prev = ab_ref[cc * BLK - 1:cc * BLK, :]
            else:
                a_prev = jnp.where(i == NB // SSD_CB - 1, 0.0, abp_ref[7:8, :])
            a_end = ab[BLK - 1:BLK, :]
            alpha = jnp.exp(ab - a_prev)
            e_end = jnp.exp(a_end - ab)
            beta = e_end * dtb
            t_all = alpha[BLK - 1:BLK, :]
            hc = hall_ref[rows, :]
            hcb, dhb = hc.astype(BF), dh_next.astype(BF)

            cb = _dot(ci, bi, NT)
            at, dtj = at_ref[cc], dt_ref[cc]
            dcb = jnp.zeros((BLK, BLK), F32)
            dx = jnp.zeros((BLK, LANES), F32)
            rsum = []
            for hd in range(2):
                dym = jnp.where(hms[hd], dyv, 0.0).astype(BF)
                lm = _ssd_decay(ab[:, HEAD * hd:HEAD * hd + 1], at[hd:hd + 1, :], causal)
                dth = dtj[hd:hd + 1, :]
                dw = _dot(dym, jnp.where(hms[hd], x, 0.0).astype(BF), NT)
                g = dw * cb * lm
                dx = dx + _dot((cb * lm * dth).astype(BF), dym, TN)
                gcol = jnp.sum(g, axis=0, keepdims=True)
                ddt_ref[cc, hd:hd + 1, :] = gcol
                dak_ref[cc, hd:hd + 1, :] = gcol * dth
                rsum.append(jnp.sum(g * dth, axis=1, keepdims=True))
                dcb = dcb + dw * lm * dth
            dcb = dcb.astype(BF)

            g1 = (alpha * dyv).astype(BF)
            dalpha = dyv * _dot(ci, hcb, NN)
            g2 = _dot(bi, dhb, NN)
            dbeta = x * g2
            bx = (beta * x).astype(BF)
            dc_ref[rows, :] = _dot(dcb, bi, NN) + _dot(g1, hcb, NT)
            db_ref[rows, :] = _dot(dcb, ci, TN) + _dot(bx, dhb, NT)
            dx_ref[rows, :] = dx + beta * g2
            ddtb_ref[rows, :] = _seg_sum(e_end * dbeta, HEAD)
            ada, bdb = alpha * dalpha, beta * dbeta
            t_dt = t_all * jnp.sum(dh_next * hc, axis=0, keepdims=True)
            end_term = _seg_sum(jnp.broadcast_to(jnp.sum(bdb, axis=0, keepdims=True) + t_dt, (8, LANES)), HEAD)[0:1]
            prev_term = _seg_sum(jnp.broadcast_to(jnp.sum(ada, axis=0, keepdims=True) + t_dt, (8, LANES)), HEAD)[0:1]
            daq = jnp.where(half0, rsum[0], rsum[1]) + _seg_sum(ada - bdb, HEAD)
            daq_ref[rows, :] = daq + jnp.where(row == BLK - 1, end_term - carry, 0.0)
            carry = prev_term
            dh_next = t_all * dh_next + _dot(ci, g1, TN)
        carry_scr[...] = jnp.broadcast_to(carry, carry_scr.shape)
        dh_scr[...] = dh_next

    npair = D_SSM // LANES
    rev = lambda i: NB // SSD_CB - 1 - i
    rowvec = pl.BlockSpec((None, SSD_CB, 2, BLK), lambda p, i: (p, rev(i), 0, 0))
    tile = pl.BlockSpec((SSD_CB * BLK, LANES), lambda p, i: (rev(i), p))
    prev_tile = pl.BlockSpec((8, LANES), lambda p, i: (jnp.maximum(rev(i) * (SSD_CB * BLK // 8) - 1, 0), p))
    return pl.pallas_call(
        kern, name="ssd_bwd", grid=(npair, NB // SSD_CB),
        in_specs=[pl.BlockSpec((SSD_CB * BLK, LANES), lambda p, i: (rev(i), 12 + p // 2)),
                  pl.BlockSpec((SSD_CB * BLK, LANES), lambda p, i: (rev(i), 8 + p // 2)),
                  tile, tile, tile, tile, prev_tile, tile, rowvec, rowvec],
        out_specs=[tile, tile, tile, tile, rowvec, rowvec, tile],
        out_shape=[jax.ShapeDtypeStruct((S, D_SSM), F32)] * 4
        + [jax.ShapeDtypeStruct((npair, NB, 2, BLK), F32)] * 2 + [jax.ShapeDtypeStruct((S, D_SSM), F32)],
        scratch_shapes=[pltpu.VMEM((BLK, LANES), F32), pltpu.VMEM((8, LANES), F32)],
        compiler_params=_params("parallel", "arbitrary"))(xbc, xbc, xbc, dy, h_all, a_b, a_b, dt_b, at_r, dt_r)


def _ssd_prep_bwd(daq, dak, ddt_row, ddt_lane, dt, proj, dt_bias128, a_log128):
    def body(i, n, daq_ref, dak_ref, dd_ref, dd2_ref, dt_ref, raw_ref, b_ref, al_ref, draw_ref, dal_ref, db_ref,
             carry):
        @pl.when(i == 0)
        def _():
            carry[...] = jnp.zeros_like(carry)

        a = -jnp.exp(al_ref[...])
        tri = (lax.broadcasted_iota(jnp.int32, (BLK, BLK), 0) <= lax.broadcasted_iota(jnp.int32, (BLK, BLK), 1))
        rev = _dot3(tri.astype(BF), daq_ref[...] - dak_ref[...]) + carry[0:1, :]
        carry[...] = jnp.broadcast_to(rev[0:1, :], carry.shape)
        dtv = dt_ref[...]
        draw = (dd_ref[...] + dd2_ref[...] + a * rev) * _sigmoid(raw_ref[...] + b_ref[...])
        draw_ref[...] = draw.astype(BF)
        _accum(dal_ref, jnp.sum(dtv * rev, axis=0, keepdims=True) * a, i)
        _accum(db_ref, jnp.sum(draw, axis=0, keepdims=True), i)

    return _rowwise(body, name="ssd_prep_bwd", nrows=S, tile=BLK, reverse=True,
                    row_ins=[(daq, LANES, 0), (dak, LANES, 0), (ddt_row, LANES, 0), (ddt_lane, LANES, 0), (dt, LANES, 0),
                             (proj, LANES, (D_INP - LANES) // LANES)],
                    full_ins=[dt_bias128, a_log128], row_outs=[(LANES, BF)],
                    acc_outs=[((1, LANES), F32), ((1, LANES), F32)], scratch=[pltpu.VMEM((8, LANES), F32)])


def _conv_bwd(proj, conv_w, conv_b, d1, d2, map1, map2, col0, ntiles, name):
    base = (4 * D_ATTN + col0) // LANES

    def kern(x_ref, w_ref, b_ref, d1_ref, d2_ref, dx_ref, dw_ref, db_ref):
        x = x_ref[...]
        c, shifted = _conv_taps(x, w_ref)
        c = c + b_ref[...]
        sg = _sigmoid(c)
        dc = (d1_ref[...] + d2_ref[...]) * (sg * (1.0 + c * (1.0 - sg)))
        rows = lax.broadcasted_iota(jnp.int32, x.shape, 0)
        dx = jnp.zeros_like(x)
        for w in range(CONV_W):
            k = CONV_W - 1 - w
            up = dc if k == 0 else jnp.where(rows < S - k, pltpu.roll(dc, S - k, 0), 0.0)
            dx = dx + up * w_ref[w:w + 1, :]
            dw_ref[w:w + 1, :] = jnp.sum(dc * shifted[w], axis=0, keepdims=True)
        dx_ref[...] = dx.astype(BF)
        db_ref[...] = jnp.sum(dc, axis=0, keepdims=True)

    c0 = col0 // LANES
    return pl.pallas_call(
        kern, name=name, grid=(ntiles,),
        in_specs=[pl.BlockSpec((S, LANES), lambda c: (0, base + c)),
                  pl.BlockSpec((CONV_W, LANES), lambda c: (0, c0 + c)),
                  pl.BlockSpec((1, LANES), lambda c: (0, c0 + c)),
                  pl.BlockSpec((S, LANES), lambda c: (0, map1(c))),
                  pl.BlockSpec((S, LANES), lambda c: (0, map2(c)))],
        out_specs=[pl.BlockSpec((S, LANES), lambda c: (0, c)), pl.BlockSpec((CONV_W, LANES), lambda c: (0, c)),
                   pl.BlockSpec((1, LANES), lambda c: (0, c))],
        out_shape=[jax.ShapeDtypeStruct((S, ntiles * LANES), BF), jax.ShapeDtypeStruct((CONV_W, ntiles * LANES), F32),
                   jax.ShapeDtypeStruct((1, ntiles * LANES), F32)],
        compiler_params=_params("parallel"))(proj, conv_w, conv_b, d1, d2)


def _adamw(w, m, v, parts, name):
    r, c = w.shape
    n_parts = parts.shape[0]
    tile = r if r <= 512 else (128 if c > 1024 else 256)
    assert r % tile == 0
    c1 = 1.0 - ADAM_B1 ** ADAM_STEP
    c2 = 1.0 - ADAM_B2 ** ADAM_STEP

    def kern(w_ref, m_ref, v_ref, p_ref, g_ref, d_ref, mo_ref, vo_ref):
        g = p_ref[0].astype(F32)
        for k in range(1, n_parts):
            g = g + p_ref[k].astype(F32)
        mn = ADAM_B1 * m_ref[...] + (1.0 - ADAM_B1) * g
        vn = ADAM_B2 * v_ref[...] + (1.0 - ADAM_B2) * (g * g)
        g_ref[...] = g
        mo_ref[...] = mn
        vo_ref[...] = vn
        d_ref[...] = -ADAM_LR * ((mn / c1) / (jnp.sqrt(vn / c2) + ADAM_EPS) + ADAM_WD * w_ref[...])

    blk = pl.BlockSpec((tile, c), lambda i: (i, 0))
    return pl.pallas_call(
        kern, name=name, grid=(r // tile,),
        in_specs=[blk, blk, blk, pl.BlockSpec((n_parts, tile, c), lambda i: (0, i, 0))],
        out_specs=[blk] * 4, out_shape=[jax.ShapeDtypeStruct((r, c), F32)] * 4,
        compiler_params=_params("parallel"))(w, m, v, parts)


MESH = pl.DeviceIdType.MESH
ANY = pl.BlockSpec(memory_space=pl.ANY)


def _put_own(gathered, shard):
    me = 4 * lax.axis_index("x") + 2 * lax.axis_index("y") + lax.axis_index("c")
    return lax.dynamic_update_slice(gathered, shard[None], (me,) + (0,) * shard.ndim)


def _all_gather(arrs, name, after=None):
    na = len(arrs)
    n_after = 0 if after is None else 1

    def kern(*refs):
        ins, outs = refs[:na], refs[na + n_after:2 * na + n_after]
        send_sems, recv_sems = refs[2 * na + n_after:]
        x, y, c = lax.axis_index("x"), lax.axis_index("y"), lax.axis_index("c")
        me, sibling = (x, y, c), (x, y, 1 - c)
        chips = [(1 - x, y), (x, 1 - y), (1 - x, 1 - y)]

        def copy(a, k, block, to, src=None):
            px, py, pc = block
            dst = outs[a].at[4 * px + 2 * py + pc]
            return pltpu.make_async_remote_copy(
                src_ref=dst if src is None else src, dst_ref=dst, send_sem=send_sems.at[a, k],
                recv_sem=recv_sems.at[a, k], device_id=to, device_id_type=MESH)

        first = []
        for a in range(na):
            first.append(copy(a, 0, me, sibling, src=ins[a]))
            first += [copy(a, 1 + j, me, (*chip, c), src=ins[a]) for j, chip in enumerate(chips)]
        for cp in first:
            cp.start()
        passed = []
        for a in range(na):
            for j, chip in enumerate(chips):
                copy(a, 1 + j, (*chip, c), me).wait_recv()
                fwd = copy(a, 4 + j, (*chip, c), sibling)
                fwd.start()
                passed.append(fwd)
        for a in range(na):
            copy(a, 0, sibling, me).wait_recv()
            for j, chip in enumerate(chips):
                copy(a, 4 + j, (*chip, 1 - c), me).wait_recv()
        for cp in first + passed:
            cp.wait_send()

    outs = pl.pallas_call(
        kern, name=name, in_specs=[ANY] * (na + n_after), out_specs=[ANY] * na,
        out_shape=[jax.ShapeDtypeStruct((N_DEV,) + a.shape, a.dtype) for a in arrs],
        scratch_shapes=[pltpu.SemaphoreType.DMA((na, 7)), pltpu.SemaphoreType.DMA((na, 7))],
    )(*arrs, *([] if after is None else [after]))
    return [_put_own(o, a) for o, a in zip(outs, arrs)]


HBM = pl.BlockSpec(memory_space=pltpu.HBM)
SEM = pl.BlockSpec(memory_space=pltpu.SEMAPHORE)
EFFECT = pltpu.SideEffectType.DATAFLOW_SIDE_EFFECTING
N_CHIP = 4
N_COPIES = {"gather": 3, "scatter": 3, "forward": 4, "pair": 4}


def _in_hbm(a):
    return pltpu.with_memory_space_constraint(a, pltpu.HBM)


def _chip_copies(kind, src_refs, land_refs, send_sems, recv_sems):
    x, y, c = lax.axis_index("x"), lax.axis_index("y"), lax.axis_index("c")
    chips = [(1 - x, y), (x, 1 - y), (1 - x, 1 - y)]
    n = N_COPIES[kind]
    cps = []

    def add(a, j, src, dst, to):
        cps.append(pltpu.make_async_remote_copy(
            src_ref=src, dst_ref=dst, send_sem=send_sems.at[n * a + j], recv_sem=recv_sems.at[n * a + j],
            device_id=to, device_id_type=MESH))

    for a in range(len(src_refs)):
        if kind == "gather":
            for j, (px, py) in enumerate(chips):
                add(a, j, src_refs[a], land_refs[a].at[4 * x + 2 * y + c], (px, py, c))
        elif kind == "scatter":
            for j, (px, py) in enumerate(chips):
                add(a, j, src_refs[a].at[2 * px + py], land_refs[a].at[2 * x + y], (px, py, c))
        elif kind == "forward":
            add(a, 0, src_refs[a], land_refs[a].at[4 * x + 2 * y + c], (x, y, 1 - c))
            for j, (px, py) in enumerate(chips):
                slot = land_refs[a].at[4 * px + 2 * py + c]
                add(a, 1 + j, slot, slot, (x, y, 1 - c))
        else:
            for q in range(N_CHIP):
                add(a, q, src_refs[a].at[2 * q + 1 - c], land_refs[a].at[q], (x, y, 1 - c))
    return cps


def _exchange_start(kind, srcs, lands, after, name):
    na = len(srcs)
    n_after = 0 if after is None else 1

    def kern(*refs):
        src_refs, land_refs = refs[:na], refs[na:2 * na]
        send_sems, recv_sems = refs[2 * na + n_after], refs[2 * na + n_after + 1]
        token = refs[-1]
        for cp in _chip_copies(kind, src_refs, land_refs, send_sems, recv_sems):
            cp.start()
        token[...] = jnp.zeros_like(token)

    bufs = list(srcs) + list(lands)
    res = pl.pallas_call(
        kern, name=name,
        out_shape=(pltpu.SemaphoreType.DMA((N_COPIES[kind] * na,)), pltpu.SemaphoreType.DMA((N_COPIES[kind] * na,)))
        + tuple(pltpu.HBM(b.shape, b.dtype) for b in bufs) + (jax.ShapeDtypeStruct((8, LANES), F32),),
        in_specs=[HBM] * (2 * na) + [ANY] * n_after,
        out_specs=(SEM, SEM) + (HBM,) * (2 * na) + (pl.BlockSpec(memory_space=pltpu.VMEM),),
        input_output_aliases={i: 2 + i for i in range(2 * na)},
        compiler_params=pltpu.CompilerParams(has_side_effects=EFFECT),
    )(*[_in_hbm(b) for b in bufs], *([] if after is None else [after]))
    return (res[0], res[1]), list(res[2:2 + na]), list(res[2 + na:2 + 2 * na]), res[-1]


def _exchange_wait(kind, sems, srcs, lands, after, name):
    na = len(srcs)

    def kern(*refs):
        src_refs, land_refs = refs[:na], refs[na:2 * na]
        send_sems, recv_sems = refs[2 * na], refs[2 * na + 1]
        for cp in _chip_copies(kind, src_refs, land_refs, send_sems, recv_sems):
            cp.wait_send()
            cp.wait_recv()

    bufs = list(srcs) + list(lands)
    res = pl.pallas_call(
        kern, name=name, out_shape=tuple(pltpu.HBM(b.shape, b.dtype) for b in bufs),
        in_specs=[HBM] * (2 * na) + [SEM, SEM, ANY], out_specs=(HBM,) * (2 * na),
        input_output_aliases={i: i for i in range(2 * na)},
        compiler_params=pltpu.CompilerParams(has_side_effects=EFFECT),
    )(*bufs, sems[0], sems[1], after)
    return list(res[:na]), list(res[na:])


def _gather_finish(shards, lands, name):
    na = len(shards)

    def kern(*refs):
        ins, outs = refs[:na], refs[2 * na:3 * na]
        send_sems, recv_sems = refs[3 * na:]
        x, y, c = lax.axis_index("x"), lax.axis_index("y"), lax.axis_index("c")
        chips = [(1 - x, y), (x, 1 - y), (1 - x, 1 - y)]

        def copy(a, k, slot, pc, src=None):
            dst = outs[a].at[slot + pc]
            return pltpu.make_async_remote_copy(
                src_ref=dst if src is None else src, dst_ref=dst, send_sem=send_sems.at[a, k],
                recv_sem=recv_sems.at[a, k], device_id=(x, y, 1 - c), device_id_type=MESH)

        sends = []
        for a in range(na):
            sends.append(copy(a, 0, 4 * x + 2 * y, c, src=ins[a]))
            sends += [copy(a, 1 + j, 4 * px + 2 * py, c) for j, (px, py) in enumerate(chips)]
        for cp in sends:
            cp.start()
        for a in range(na):
            copy(a, 0, 4 * x + 2 * y, 1 - c).wait_recv()
            for j, (px, py) in enumerate(chips):
                copy(a, 1 + j, 4 * px + 2 * py, 1 - c).wait_recv()
        for cp in sends:
            cp.wait_send()

    return pl.pallas_call(
        kern, name=name, in_specs=[ANY] * (2 * na), out_specs=[ANY] * na,
        out_shape=[jax.ShapeDtypeStruct(l.shape, l.dtype) for l in lands],
        input_output_aliases={na + a: a for a in range(na)},
        scratch_shapes=[pltpu.SemaphoreType.DMA((na, 4)), pltpu.SemaphoreType.DMA((na, 4))])(*shards, *lands)


def _pair_exchange(parts, name):
    na = len(parts)

    def kern(*refs):
        ins, got = refs[:na], refs[na:2 * na]
        send_sems, recv_sems = refs[2 * na:]
        x, y, c = lax.axis_index("x"), lax.axis_index("y"), lax.axis_index("c")
        sends = []
        for a in range(na):
            for q in range(N_CHIP):
                sends.append(pltpu.make_async_remote_copy(
                    src_ref=ins[a].at[2 * q + 1 - c], dst_ref=got[a].at[q], send_sem=send_sems.at[a, q],
                    recv_sem=recv_sems.at[a, q], device_id=(x, y, 1 - c), device_id_type=MESH))
        for cp in sends:
            cp.start()
        for cp in sends:
            cp.wait()

    return pl.pallas_call(
        kern, name=name, in_specs=[ANY] * na, out_specs=[ANY] * na,
        out_shape=[jax.ShapeDtypeStruct((N_CHIP,) + p.shape[1:], p.dtype) for p in parts],
        scratch_shapes=[pltpu.SemaphoreType.DMA((na, N_CHIP)), pltpu.SemaphoreType.DMA((na, N_CHIP))])(*parts)


def _pair_sum(parts, got, name):
    _, r, cdim = parts.shape
    tile = min(r, ROW_TILE)
    core = lax.axis_index("c").astype(jnp.int32).reshape(1)

    def kern(c_ref, a_ref, b_ref, q_ref, l_ref):
        s = (a_ref[...].astype(F32) + b_ref[...].astype(F32)).astype(BF)
        q_ref[...] = s
        l_ref[...] = s

    blk = pl.BlockSpec((None, tile, cdim), lambda q, i, c_ref: (q, i, 0))
    out = jax.ShapeDtypeStruct((N_CHIP, r, cdim), BF)
    return pl.pallas_call(
        kern, name=name, out_shape=[out, out],
        grid_spec=pltpu.PrefetchScalarGridSpec(
            num_scalar_prefetch=1, grid=(N_CHIP, r // tile),
            in_specs=[pl.BlockSpec((None, tile, cdim), lambda q, i, c_ref: (2 * q + c_ref[0], i, 0)), blk],
            out_specs=[blk, blk]),
        compiler_params=_params("parallel", "parallel"))(core, parts, got)


W_IN_COLS = D_IN // N_DEV


def _w_in_from_blocks(w8):
    def kern(x_ref, o_ref):
        for j in range(N_DEV):
            o_ref[:, W_IN_COLS * j:W_IN_COLS * (j + 1)] = x_ref[j]
        o_ref[:, D_IN:] = jnp.zeros((ROW_TILE, D_INP - D_IN), o_ref.dtype)

    return pl.pallas_call(
        kern, name="w_in_from_blocks", grid=(D // ROW_TILE,),
        in_specs=[pl.BlockSpec((N_DEV, ROW_TILE, W_IN_COLS), lambda i: (0, i, 0))],
        out_specs=pl.BlockSpec((ROW_TILE, D_INP), lambda i: (i, 0)),
        out_shape=jax.ShapeDtypeStruct((D, D_INP), w8.dtype), compiler_params=_params("parallel"))(w8)


def _w_in_to_blocks(g):
    def kern(x_ref, o_ref):
        for j in range(N_DEV):
            o_ref[j] = x_ref[:, W_IN_COLS * j:W_IN_COLS * (j + 1)]

    return pl.pallas_call(
        kern, name="w_in_to_blocks", grid=(D // ROW_TILE,),
        in_specs=[pl.BlockSpec((ROW_TILE, D_INP), lambda i: (i, 0))],
        out_specs=pl.BlockSpec((N_DEV, ROW_TILE, W_IN_COLS), lambda i: (0, i, 0)),
        out_shape=jax.ShapeDtypeStruct((N_DEV, D, W_IN_COLS), g.dtype), compiler_params=_params("parallel"))(g)


def _pad_lanes(v, width=LANES):
    return jnp.pad(v, ((0, 0), (0, width - v.shape[1])))


def _row_layout(t16):
    return t16.T.reshape(N_HEADS // 2, 2, NB, BLK).transpose(0, 2, 1, 3)


def _row_layout_inv(r):
    return r.transpose(0, 2, 1, 3).reshape(N_HEADS, S).T


SMALL = (("g_mix", D), ("g_q", HEAD), ("g_k", HEAD), ("g_attn_out", D_ATTN), ("conv_b", D_CONV),
         ("dt_bias", 16), ("a_log", 16), ("d_skip", 16), ("g_ssm_out", D_SSM), ("g_cross", D),
         ("g_mem", D), ("g_cq", CROSS_HEAD), ("g_ck", CROSS_HEAD), ("g_mlp", D))
SMALL_ROWS = -(-sum(-(-w // LANES) for _, w in SMALL) // 8) * 8


def _pack_small(vals):
    rows = [_pad_lanes(vals[n], -(-w // LANES) * LANES).reshape(-1, LANES) for n, w in SMALL]
    packed = jnp.concatenate(rows, axis=0)
    return jnp.pad(packed, ((0, SMALL_ROWS - packed.shape[0]), (0, 0)))


def _unpack_small(packed):
    out, r = {}, 0
    for n, w in SMALL:
        nr = -(-w // LANES)
        out[n] = packed[r:r + nr].reshape(1, nr * LANES)[:, :w]
        r += nr
    return out


BIG = ("w_in", "w_out", "w_cq", "w_ckv", "w_co", "w_up", "w_down")
WEIGHTS = ("g_mix", "w_in", "g_q", "g_k", "g_attn_out", "conv_w", "conv_b", "dt_bias", "a_log", "d_skip",
           "g_ssm_out", "w_out", "g_cross", "g_mem", "w_cq", "w_ckv", "g_cq", "g_ck", "w_co", "g_mlp", "w_up", "w_down")


REST = ("w_out", "w_cq", "w_ckv", "w_co", "w_up", "w_down")


class _Overlap:
    def __init__(self, shards, after):
        self.gather, self.forward = {}, {}
        for tag, names in (("a", REST[:4]), ("b", REST[4:])):
            lands = [_put_own(lax.empty((N_DEV,) + shards[n].shape, BF), shards[n]) for n in names]
            self.gather[tag] = _exchange_start("gather", [shards[n] for n in names], lands, after, "ag_start_" + tag)
            after = self.gather[tag][3]
        self.token = after
        self.groups = []
        self.pairs = {}

    def rest_mid(self, tag, after):
        sems, srcs, lands, _ = self.gather[tag]
        srcs, lands = _exchange_wait("gather", sems, srcs, lands, after, "ag_wait_" + tag)
        self.forward[tag] = _exchange_start("forward", srcs, lands, None, "ag_fwd_start_" + tag)
        return self.forward[tag][3]

    def rest(self, tag, after):
        sems, srcs, lands, _ = self.forward[tag]
        _, lands = _exchange_wait("forward", sems, srcs, lands, after, "ag_fwd_wait_" + tag)
        wf = dict(zip(REST[:4] if tag == "a" else REST[4:], lands))
        for n in wf:
            if n in ("w_out", "w_cq", "w_ckv", "w_down"):
                wf[n] = wf[n].reshape(-1, wf[n].shape[-1])
        return wf

    def pair_start(self, name, grad):
        got = lax.empty((N_CHIP,) + grad.shape[1:], grad.dtype)
        self.pairs[name] = _exchange_start("pair", [grad], [got], None, "rs_pair_start_" + name)
        return self.pairs[name][3]

    def emit(self, grads, after):
        names, tag = list(grads), "_%d" % len(self.groups)
        grads, got = dict(grads), {}
        for n in names:
            if n in self.pairs:
                sems, srcs, lands, _ = self.pairs[n]
                srcs, lands = _exchange_wait("pair", sems, srcs, lands, after, "rs_pair_wait_" + n)
                grads[n], got[n] = srcs[0], lands[0]
        sync = [n for n in names if n not in got]
        if sync:
            got.update(zip(sync, _pair_exchange([grads[n] for n in sync], "rs_pair" + tag)))
        sums = [_pair_sum(grads[n], got[n], "rs_sum_" + n) for n in names]
        sems, srcs, lands, token = _exchange_start("scatter", [q for q, _ in sums], [l for _, l in sums], None,
                                                   "rs_chip_start" + tag)
        self.groups.append((names, sems, srcs, lands))
        return token

    def finish(self, gi, after):
        names, sems, srcs, lands = self.groups[gi]
        _, lands = _exchange_wait("scatter", sems, srcs, lands, after, "rs_chip_wait_%d" % gi)
        return dict(zip(names, lands))


def _tie(v, token):
    return v if token is None else v + token[0:1, 0:1]


def _local_step(x, mem, pos_col, target, p, wf, hooks=None):
    p = dict(p)
    inv = np.zeros((1, LANES), np.float32)
    freq = (ROPE_THETA ** (-2.0 * np.arange(ROT // 2, dtype=np.float32) / ROT)).astype(np.float32)
    for l in range(LANES):
        if l % HEAD < ROT:
            inv[0, l] = freq[(l % HEAD) % (ROT // 2)]
    inv_tab = jnp.asarray(inv)
    gq128, gk128 = jnp.tile(p["g_q"], (1, 2)), jnp.tile(p["g_k"], (1, 2))
    dtb128, alog128 = _pad_lanes(p["dt_bias"]), _pad_lanes(p["a_log"])
    dskip_b = jnp.repeat(p["d_skip"], HEAD, axis=1)
    conv_w, conv_b = wf["conv_w"], p["conv_b"]

    h = _rms_fwd(x, p["g_mix"], "rms_mix")
    proj = _matmul(h, wf["w_in"], mode="nn", name="mm_in", tm=1024, tn=896, tk=D)
    qr, kr = _qk_prep(proj, pos_col, gq128, gk128, inv_tab)
    dilations = (1, 4, 16)
    fwd = [_attn_fwd(qr, kr, proj, d, "attn_fwd_d%d" % d) for d in dilations]
    attn, lse, attn_n = _attn_merge([o for o, _ in fwd], [l for _, l in fwd], p["g_attn_out"])

    xbc = _conv_fwd(proj, conv_w, conv_b)
    dt, a_cs = _ssd_prep(proj, dtb128, alog128)
    a_b = _tie(jnp.repeat(a_cs[:, :N_HEADS], HEAD, axis=1), None if hooks is None else hooks.rest_mid("a", xbc))
    dt_b = jnp.repeat(dt[:, :N_HEADS], HEAD, axis=1)
    at_r, dt_r = _row_layout(a_cs[:, :N_HEADS]), _row_layout(dt[:, :N_HEADS])
    y_ssd, h_all = _ssd_fwd(xbc, a_b, dt_b, at_r, dt_r)
    ssm_n = _ssd_post(y_ssd, xbc, proj, dskip_b, p["g_ssm_out"])

    if hooks is not None:
        wf = {**wf, **hooks.rest("a", ssm_n)}
    mix = jnp.concatenate([attn_n, ssm_n], axis=1)
    x1 = _matmul(mix, wf["w_out"], mode="nn", name="mm_out", tm=1024, tn=1024, tk=D,
                 extras=(x,), epilogue=lambda acc, r: (acc + r,))
    hc = _rms_fwd(x1, _tie(p["g_cross"], None if hooks is None else hooks.rest_mid("b", x1)), "rms_cross")
    memh = _rms_fwd(mem, p["g_mem"], "rms_mem")
    qc = _matmul(hc, wf["w_cq"], mode="nn", name="mm_cq", tm=1024, tn=512, tk=D)
    kv = _matmul(memh, wf["w_ckv"], mode="nn", name="mm_ckv", tm=N_MEM, tn=1024, tk=D)
    oc = _cross_fwd(qc, kv, p["g_cq"], p["g_ck"])
    x2 = _matmul(oc, wf["w_co"], mode="nn", name="mm_co", tm=1024, tn=256, tk=512, b_cb=256,
                 extras=(x1,), epilogue=lambda acc, r: (acc + r,))
    hm = _rms_fwd(x2, p["g_mlp"], "rms_mlp")
    if hooks is not None:
        wf = {**wf, **hooks.rest("b", hm)}

    def up_epi(acc):
        ru = jnp.maximum(acc, 0.0)
        return ru * ru, 2.0 * ru

    act, relu2 = _matmul(hm, wf["w_up"], mode="nn", name="mm_up", tm=1024, tn=1024, tk=D, b_cb=1024,
                         out_dtypes=(BF, BF), epilogue=up_epi)

    def loss_epi(acc, r, t):
        d = (acc + r - t) * (1.0 / D)
        return d, d

    dy, dyb = _matmul(act, wf["w_down"], mode="nn", name="mm_down", tm=512, tn=1024, tk=2048, extras=(x2, target),
                      out_dtypes=(F32, BF), epilogue=loss_epi)
    loss_part = _loss_sum(dy)[0, 0] * (0.5 * D)

    gb, gs = {}, {}
    gb["w_down"] = _matmul(act, dyb, mode="tn", name="mm_down_dw", tm=1024, tn=1024, tk=S,
                           out_dtypes=(BF,)).reshape(N_DEV, D_FF // N_DEV, D)
    token = None if hooks is None else hooks.pair_start("w_down", gb["w_down"])
    du = _matmul(dyb, wf["w_down"], mode="nt", name="mm_down_dx", tm=1024, tn=1024, tk=D, extras=(relu2,),
                 out_dtypes=(BF,), epilogue=lambda acc, r: (acc * r.astype(F32),), after=token)
    gb["w_up"] = _matmul(hm, du, mode="tn", name="mm_up_dw", tm=1024, tn=1024, tk=S, out_dtypes=(BF,), out_cb=1024)
    token = None if hooks is None else hooks.pair_start("w_up", gb["w_up"])
    dhm = _matmul(du, wf["w_up"], mode="nt", name="mm_up_dx", tm=1024, tn=1024, tk=1024, b_cb=1024, after=token)
    if hooks is not None:
        p["g_mlp"] = _tie(p["g_mlp"], hooks.emit({n: gb[n] for n in ("w_down", "w_up")}, dhm))
    dx2, dx2b, gs["g_mlp"] = _rms_bwd_call(x2, p["g_mlp"], dhm, dy, "rms_mlp_bwd")

    doc = _matmul(dx2b, wf["w_co"], mode="nt", name="mm_co_dx", tm=1024, tn=512, tk=256, b_cb=256)
    gb["w_co"] = _matmul(oc, dx2b, mode="tn", name="mm_co_dw", tm=512, tn=256, tk=S, out_dtypes=(BF,), out_cb=256)
    dqc, dkn, dvc, gs["g_cq"] = _cross_bwd(qc, doc, kv, p["g_cq"], p["g_ck"])
    dkv, gs["g_ck"] = _cross_kv_bwd(kv, dkn, dvc, p["g_ck"])
    gb["w_cq"] = _matmul(hc, dqc, mode="tn", name="mm_cq_dw", tm=1024, tn=512, tk=S,
                         out_dtypes=(BF,)).reshape(N_DEV, D // N_DEV, D_CROSS)
    dhc = _matmul(dqc, wf["w_cq"], mode="nt", name="mm_cq_dx", tm=1024, tn=1024, tk=512)
    gb["w_ckv"] = _matmul(memh, dkv, mode="tn", name="mm_ckv_dw", tm=1024, tn=1024, tk=N_MEM,
                          out_dtypes=(BF,)).reshape(N_DEV, D // N_DEV, 2 * D_CROSS)
    dmemh = _matmul(dkv, wf["w_ckv"], mode="nt", name="mm_ckv_dx", tm=N_MEM, tn=1024, tk=1024)
    (gs["g_mem"],) = _rms_bwd_call(mem, p["g_mem"], dmemh, None, "rms_mem_bwd")
    dx1, dx1b, gs["g_cross"] = _rms_bwd_call(x1, p["g_cross"], dhc, dx2, "rms_cross_bwd")

    dmix = _matmul(dx1b, wf["w_out"], mode="nt", name="mm_out_dx", tm=1024, tn=1024, tk=D)
    gb["w_out"] = _matmul(mix, dx1b, mode="tn", name="mm_out_dw", tm=1024, tn=1024, tk=S,
                          out_dtypes=(BF,)).reshape(N_DEV, D // N_DEV, D)
    if hooks is not None:
        p["g_attn_out"] = _tie(p["g_attn_out"],
                               hooks.emit({n: gb[n] for n in ("w_co", "w_cq", "w_ckv", "w_out")}, dmix))

    dattn, delta, gs["g_attn_out"] = _attn_merge_bwd(dmix, attn, p["g_attn_out"])
    bwd = [_attn_bwd(qr, kr, proj, dattn, lse, delta, d, "attn_bwd_d%d" % d) for d in dilations]
    dq_pre, dk_pre, dv_pre, dgq, dgk = _qk_bwd([t[0] for t in bwd], [t[1] for t in bwd], [t[2] for t in bwd],
                                               proj, pos_col, gq128, gk128, inv_tab)
    gs["g_q"], gs["g_k"] = dgq[:, :HEAD], dgk[:, :HEAD]

    dy_ssd, dz, dxs_skip, dd_lane, gs["g_ssm_out"] = _ssd_post_bwd(y_ssd, xbc, proj, dmix, dskip_b, p["g_ssm_out"])
    gs["d_skip"] = dd_lane.reshape(N_HEADS, HEAD).sum(axis=1).reshape(1, N_HEADS)
    dc_pair, daq_b, dx_ssd, db_pair, dak_r, ddt_r, ddt_b = _ssd_bwd(xbc, dy_ssd, h_all, a_b, dt_b, at_r, dt_r)
    daq = _pad_lanes(daq_b[:, ::HEAD])
    dak = _pad_lanes(_row_layout_inv(dak_r))
    ddt_direct = _pad_lanes(_row_layout_inv(ddt_r))
    ddt_raw, dalog, dbias = _ssd_prep_bwd(daq, dak, ddt_direct, _pad_lanes(ddt_b[:, ::HEAD]), dt, proj,
                                          dtb128, alog128)
    gs["a_log"], gs["dt_bias"] = dalog[:, :N_HEADS], dbias[:, :N_HEADS]
    same = lambda c: c
    dxbc_x, dcw_x, dcb_x = _conv_bwd(proj, conv_w, conv_b, dxs_skip, dx_ssd, same, same, 0, 8, "conv_bwd_x")
    dxbc_b, dcw_b, dcb_b = _conv_bwd(proj, conv_w, conv_b, db_pair, db_pair, lambda c: 2 * c, lambda c: 2 * c + 1,
                                     D_SSM, 4, "conv_bwd_b")
    dxbc_c, dcw_c, dcb_c = _conv_bwd(proj, conv_w, conv_b, dc_pair, dc_pair, lambda c: 2 * c, lambda c: 2 * c + 1,
                                     D_SSM + 512, 4, "conv_bwd_c")
    g_conv_w = jnp.concatenate([dcw_x, dcw_b, dcw_c], axis=1)
    gs["conv_b"] = jnp.concatenate([dcb_x, dcb_b, dcb_c], axis=1)

    dproj = jnp.concatenate([dq_pre, dk_pre, dv_pre, dz, dxbc_x, dxbc_b, dxbc_c, ddt_raw], axis=1)
    dw_in = _matmul(h, dproj, mode="tn", name="mm_in_dw", tm=1024, tn=896, tk=S, out_dtypes=(BF,))
    gb["w_in"] = _w_in_to_blocks(dw_in)
    token = None if hooks is None else hooks.emit({"w_in": gb["w_in"]}, dw_in)
    dh = _matmul(dproj, wf["w_in"], mode="nt", name="mm_in_dx", tm=1024, tn=1024, tk=896, after=token)
    grad_x, _, gs["g_mix"] = _rms_bwd_call(x, p["g_mix"], dh, dx1, "rms_mix_bwd")
    return loss_part, grad_x, gb, gs, g_conv_w


def kernel(x, mem, positions, g_mix, w_in, g_q, g_k, g_attn_out, conv_w, conv_b, dt_bias, a_log, d_skip, g_ssm_out, w_out, g_cross, g_mem, w_cq, w_ckv, g_cq, g_ck, w_co, g_mlp, w_up, w_down, loss_target, m_g_mix, m_w_in, m_g_q, m_g_k, m_g_attn_out, m_conv_w, m_conv_b, m_dt_bias, m_a_log, m_d_skip, m_g_ssm_out, m_w_out, m_g_cross, m_g_mem, m_w_cq, m_w_ckv, m_g_cq, m_g_ck, m_w_co, m_g_mlp, m_w_up, m_w_down, v_g_mix, v_w_in, v_g_q, v_g_k, v_g_attn_out, v_conv_w, v_conv_b, v_dt_bias, v_a_log, v_d_skip, v_g_ssm_out, v_w_out, v_g_cross, v_g_mem, v_w_cq, v_w_ckv, v_g_cq, v_g_ck, v_w_co, v_g_mlp, v_w_up, v_w_down):
    args = dict(locals())
    w = {n: args[n] for n in WEIGHTS}
    m = {n: args["m_" + n] for n in WEIGHTS}
    v = {n: args["v_" + n] for n in WEIGHTS}
    small = {n: w[n] for n, _ in SMALL}
    me = 4 * lax.axis_index("x") + 2 * lax.axis_index("y") + lax.axis_index("c")

    w_in_g, conv_w_g = _all_gather([w["w_in"][0].astype(BF), w["conv_w"][0]], "ag_first")
    wf = {"w_in": _w_in_from_blocks(w_in_g), "conv_w": conv_w_g.transpose(1, 0, 2).reshape(CONV_W, D_CONV)}
    overlap = _Overlap({n: w[n][0].astype(BF) for n in REST}, w_in_g)
    p = dict(small)
    p["g_mix"] = _tie(p["g_mix"], overlap.token)

    loss_part, grad_x, _, gs, g_conv_w = _local_step(
        x[0], mem[0], positions.reshape(S, 1), loss_target[0], p, wf, overlap)
    loss = lax.psum(loss_part, ("x", "y", "c"))

    out = {}
    last = grad_x
    for gi in range(3):
        for n, parts in overlap.finish(gi, last).items():
            res_n = _adamw(w[n][0], m[n][0], v[n][0], parts, "adamw_" + n)
            out[n] = [t.reshape(w[n].shape) for t in res_n]
            last = res_n[0]

    sm_parts, cw_parts = _all_gather([_pack_small(gs), g_conv_w], "ag_small_grads", after=last)
    sm = [_unpack_small(t) for t in _adamw(_pack_small(small), _pack_small({n: m[n] for n, _ in SMALL}),
                                           _pack_small({n: v[n] for n, _ in SMALL}), sm_parts, "adamw_small")]
    for n, _ in SMALL:
        out[n] = [t[n] for t in sm]
    cols = D_CONV // N_DEV
    cw_mine = lax.dynamic_slice_in_dim(cw_parts, me * cols, cols, axis=2)
    out["conv_w"] = [t.reshape(w["conv_w"].shape) for t in
                     _adamw(w["conv_w"][0], m["conv_w"][0], v["conv_w"][0], cw_mine, "adamw_conv_w")]

    res = [loss, grad_x.reshape(x.shape)]
    for k in range(4):
        res += [out[n][k] for n in WEIGHTS]
    return tuple(res)
```

```python
import functools
import math

import numpy as np
import jax
import jax.numpy as jnp
from jax import lax
from jax.experimental import pallas as pl
from jax.experimental.pallas import tpu as pltpu

F32 = jnp.float32
BF = jnp.bfloat16

N_DEV = 8
S = 2048
D = 2048
D_ATTN = 1024
N_HEADS = 16
HEAD = 64
ROT = 16
ROPE_THETA = 500000.0
D_SSM = 1024
D_CONV = 2048
CONV_W = 4
N_MEM = 256
D_CROSS = 512
CROSS_HEAD = 128
D_FF = 8192
D_IN = 6160
D_INP = 6272
EPS = 1e-6
BLK = 128
NB = S // BLK
LANES = 128
NEG = -1e30

ADAM_LR = 0.001
ADAM_B1 = 0.9
ADAM_B2 = 0.999
ADAM_EPS = 1e-08
ADAM_WD = 0.01
ADAM_STEP = 10

VMEM_LIMIT_BYTES = 48 * 1024 * 1024
ROW_TILE = 256


def _params(*sem):
    return pltpu.CompilerParams(dimension_semantics=sem, vmem_limit_bytes=VMEM_LIMIT_BYTES)


def _matmul(a, b, *, mode, name, tm, tn, tk, out_dtypes=(F32,), b_cb=None, out_cb=None,
            extras=(), epilogue=None, after=None):
    if mode == "tn":
        kk, m = a.shape
    else:
        m, kk = a.shape
    if b_cb is None:
        br, bc = b.shape
    else:
        br, bc = b.shape[1], N_DEV * b_cb
    n = br if mode == "nt" else bc
    assert m % tm == 0 and n % tn == 0 and kk % tk == 0, (name, m, n, kk)
    nk = kk // tk
    grid = (m // tm, n // tn, nk)

    if mode == "tn":
        a_spec = pl.BlockSpec((tk, tm), lambda i, j, k: (k, i))
    else:
        a_spec = pl.BlockSpec((tm, tk), lambda i, j, k: (i, k))
    if mode == "nt":
        b_blk, b_idx = (tn, tk), (lambda i, j, k: (j, k))
    else:
        b_blk, b_idx = (tk, tn), (lambda i, j, k: (k, j))
    if b_cb is None:
        b_spec = pl.BlockSpec(b_blk, b_idx)
    else:
        tc = b_blk[1]
        assert b_cb % tc == 0
        per = b_cb // tc

        def b_idx3(i, j, k):
            r, c = b_idx(i, j, k)
            return (c // per, r, c % per)
        b_spec = pl.BlockSpec((None,) + b_blk, b_idx3)
    ex_specs = [pl.BlockSpec((tm, tn), lambda i, j, k: (i, j)) for _ in extras]
    if out_cb is None:
        out_shape = [jax.ShapeDtypeStruct((m, n), dt) for dt in out_dtypes]
        out_specs = [pl.BlockSpec((tm, tn), lambda i, j, k: (i, j)) for _ in out_dtypes]
    else:
        assert out_cb % tn == 0
        pero = out_cb // tn
        out_shape = [jax.ShapeDtypeStruct((N_DEV, m, out_cb), dt) for dt in out_dtypes]
        out_specs = [pl.BlockSpec((None, tm, tn), lambda i, j, k: (j // pero, i, j % pero)) for _ in out_dtypes]
    dn = {"nn": (((1,), (0,)), ((), ())), "nt": (((1,), (1,)), ((), ())), "tn": (((0,), (0,)), ((), ()))}[mode]
    ne, no = len(extras), len(out_dtypes)
    n_after = 0 if after is None else 1

    def kern(a_ref, b_ref, *rest):
        ex_refs, out_refs = rest[:ne], rest[ne + n_after:ne + n_after + no]

        def finish(acc):
            outs = (acc,) if epilogue is None else epilogue(acc, *[r[...] for r in ex_refs])
            for r, o in zip(out_refs, outs):
                r[...] = o.astype(r.dtype)

        part = lax.dot_general(a_ref[...].astype(BF), b_ref[...].astype(BF), dn, preferred_element_type=F32)
        if nk == 1:
            finish(part)
            return
        acc_ref = rest[ne + n_after + no]
        k = pl.program_id(2)

        @pl.when(k == 0)
        def _():
            acc_ref[...] = part

        @pl.when(k > 0)
        def _():
            acc_ref[...] += part

        @pl.when(k == nk - 1)
        def _():
            finish(acc_ref[...])

    res = pl.pallas_call(
        kern, name=name, grid=grid, in_specs=[a_spec, b_spec] + ex_specs + [ANY] * n_after, out_specs=out_specs,
        out_shape=out_shape, scratch_shapes=[pltpu.VMEM((tm, tn), F32)] if nk > 1 else [],
        compiler_params=_params("parallel", "parallel", "arbitrary"),
    )(a, b, *extras, *([] if after is None else [after]))
    return res[0] if no == 1 else tuple(res)


def _rowwise(body, *, name, nrows, tile, row_ins, full_ins=(), row_outs=(), acc_outs=(), scratch=(),
             reverse=False):
    n = nrows // tile

    def ridx(i):
        return (n - 1 - i) if reverse else i

    in_specs, args = [], []
    for arr, width, cb in row_ins:
        in_specs.append(pl.BlockSpec((tile, width), lambda i, cb=cb: (ridx(i), cb)))
        args.append(arr)
    for arr in full_ins:
        in_specs.append(pl.BlockSpec(arr.shape, lambda i, nd=arr.ndim: (0,) * nd))
        args.append(arr)
    out_shape, out_specs = [], []
    for width, dt in row_outs:
        out_shape.append(jax.ShapeDtypeStruct((nrows, width), dt))
        out_specs.append(pl.BlockSpec((tile, width), lambda i: (ridx(i), 0)))
    for shp, dt in acc_outs:
        out_shape.append(jax.ShapeDtypeStruct(shp, dt))
        out_specs.append(pl.BlockSpec(shp, lambda i, nd=len(shp): (0,) * nd))

    def kern(*refs):
        body(pl.program_id(0), n, *refs)

    return pl.pallas_call(kern, name=name, grid=(n,), in_specs=in_specs, out_specs=out_specs,
                          out_shape=out_shape, scratch_shapes=list(scratch),
                          compiler_params=_params("arbitrary"))(*args)


def _accum(ref, val, i):
    @pl.when(i == 0)
    def _():
        ref[...] = val

    @pl.when(i > 0)
    def _():
        ref[...] += val


def _sigmoid(x):
    return 1.0 / (1.0 + jnp.exp(-x))


def _rms_n(x):
    r = lax.rsqrt(jnp.mean(x * x, axis=-1, keepdims=True) + EPS)
    return x * r, r


def _rms_bwd(x, g, dh):
    n, r = _rms_n(x)
    dn = dh * g
    dx = r * (dn - n * jnp.mean(dn * n, axis=-1, keepdims=True))
    return dx, jnp.sum(dh * n, axis=0, keepdims=True)


def _seg_sum(x, seg):
    t, w = x.shape
    tiles = [x[:, LANES * j:LANES * (j + 1)] for j in range(w // LANES)]
    outs = []
    if seg == 64:
        lo = lax.broadcasted_iota(jnp.int32, (t, LANES), 1) < 64
        for xt in tiles:
            s_lo = jnp.sum(jnp.where(lo, xt, 0.0), axis=-1, keepdims=True)
            s_hi = jnp.sum(jnp.where(lo, 0.0, xt), axis=-1, keepdims=True)
            outs.append(jnp.where(lo, s_lo, s_hi))
    else:
        sums = [jnp.sum(xt, axis=-1, keepdims=True) for xt in tiles]
        if seg == 256:
            sums = [sums[2 * (j // 2)] + sums[2 * (j // 2) + 1] for j in range(len(sums))]
        else:
            assert seg == 128
        outs = [jnp.broadcast_to(s, (t, LANES)) for s in sums]
    return outs[0] if len(outs) == 1 else jnp.concatenate(outs, axis=1)


def _seg_rms_n(x, seg):
    r = lax.rsqrt(_seg_sum(x * x, seg) * (1.0 / seg) + EPS)
    return x * r, r


def _seg_rms_bwd(x, g, dy, seg):
    n, r = _seg_rms_n(x, seg)
    dn = dy * g
    dx = r * (dn - n * (_seg_sum(dn * n, seg) * (1.0 / seg)))
    return dx, jnp.sum(dy * n, axis=0, keepdims=True)


def _rope_tables(pos_col, inv_tab, t):
    ang = pos_col.astype(F32) * inv_tab
    idx = lax.broadcasted_iota(jnp.int32, (t, LANES), 1) % HEAD
    cos, sin = jnp.cos(ang), jnp.sin(ang)
    c = jnp.where(idx < ROT, cos, 1.0)
    sg = jnp.where(idx < ROT // 2, -sin, jnp.where(idx < ROT, sin, 0.0))
    return c, sg, idx < ROT // 2


def _rope(x, c, sg, lo):
    partner = jnp.where(lo, pltpu.roll(x, LANES - ROT // 2, 1), pltpu.roll(x, ROT // 2, 1))
    return x * c + partner * sg


def _fold_heads(v):
    v8 = jnp.broadcast_to(v, (8, LANES))
    return (v8 + pltpu.roll(v8, HEAD, 1))[0:1]


def _dot(a, b, dn):
    return lax.dot_general(a, b, (dn, ((), ())), preferred_element_type=F32)


NN = ((1,), (0,))
NT = ((1,), (1,))
TN = ((0,), (0,))


def _dot3(l01, x):
    x1 = x.astype(BF)
    r1 = x - x1.astype(F32)
    x2 = r1.astype(BF)
    x3 = (r1 - x2.astype(F32)).astype(BF)
    return _dot(l01, x1, NN) + _dot(l01, x2, NN) + _dot(l01, x3, NN)


def _rms_fwd(x, g, name):
    rows = x.shape[0]

    def body(i, n, x_ref, g_ref, o_ref):
        nx, _ = _rms_n(x_ref[...])
        o_ref[...] = (nx * g_ref[...]).astype(BF)

    return _rowwise(body, name=name, nrows=rows, tile=min(ROW_TILE, rows), row_ins=[(x, D, 0)],
                    full_ins=[g], row_outs=[(D, BF)])[0]


def _qk_prep(proj, pos_col, gq128, gk128, inv_tab):
    scale = HEAD ** -0.5

    def body(i, n, q_ref, k_ref, p_ref, gq_ref, gk_ref, it_ref, qo_ref, ko_ref):
        t = q_ref.shape[0]
        c, sg, lo = _rope_tables(p_ref[...], it_ref[...], t)
        for j in range(D_ATTN // LANES):
            sl = slice(LANES * j, LANES * (j + 1))
            qn, _ = _seg_rms_n(q_ref[:, sl], HEAD)
            kn, _ = _seg_rms_n(k_ref[:, sl], HEAD)
            qo_ref[:, sl] = _rope(qn * gq_ref[...], c, sg, lo) * scale
            ko_ref[:, sl] = _rope(kn * gk_ref[...], c, sg, lo)

    return _rowwise(body, name="qk_prep", nrows=S, tile=ROW_TILE,
                    row_ins=[(proj, D_ATTN, 0), (proj, D_ATTN, 1), (pos_col, 1, 0)],
                    full_ins=[gq128, gk128, inv_tab], row_outs=[(D_ATTN, F32)] * 2)


ATTN_QB = 4
V_COLS = pl.BlockSpec((S, LANES), lambda p: (0, 2 * D_ATTN // LANES + p))


def _band(b, d):
    nb = NB // d
    r, n = b // nb, b % nb
    width, kn = (BLK, n) if nb == 1 else (2 * BLK, jnp.maximum(n - 1, 0))

    def rows(first_member, count):
        if d == 1:
            return pl.ds(pl.multiple_of(first_member, BLK), count)
        return pl.ds(first_member * d + r, count, stride=d)

    qm = n * BLK + lax.broadcasted_iota(jnp.int32, (BLK, width), 0)
    km = kn * BLK + lax.broadcasted_iota(jnp.int32, (BLK, width), 1)
    valid = km <= qm
    if nb > 1:
        valid = valid & (km >= qm - BLK)
    return rows(n * BLK, BLK), rows(kn * BLK, width), valid


def _attn_fwd(q, k, v, d, name):
    def kern(q_ref, k_ref, v_ref, o_ref, l_ref):
        half0 = lax.broadcasted_iota(jnp.int32, (BLK, LANES), 1) < HEAD

        def group(g, carry):
            for bb in range(ATTN_QB):
                q_rows, k_rows, valid = _band(g * ATTN_QB + bb, d)
                q = q_ref[q_rows, :]
                kb, vb = k_ref[k_rows, :].astype(BF), v_ref[k_rows, :].astype(BF)
                o_h, l_h = [], []
                for h in range(2):
                    hm = half0 if h == 0 else jnp.logical_not(half0)
                    s = jnp.where(valid, _dot(jnp.where(hm, q, 0.0).astype(BF), kb, NT), NEG)
                    m = jnp.max(s, axis=-1, keepdims=True)
                    p = jnp.exp(s - m)
                    den = jnp.sum(p, axis=-1, keepdims=True)
                    o_h.append(_dot(p.astype(BF), vb, NN) / den)
                    l_h.append(m + jnp.log(den))
                o_ref[q_rows, :] = jnp.where(half0, o_h[0], o_h[1])
                l_ref[q_rows, :] = jnp.where(half0, l_h[0], l_h[1])
            return carry

        lax.fori_loop(0, NB // ATTN_QB, group, 0)

    seq = pl.BlockSpec((S, LANES), lambda p: (0, p))
    return pl.pallas_call(
        kern, name=name, grid=(D_ATTN // LANES,), in_specs=[seq, seq, V_COLS], out_specs=[seq, seq],
        out_shape=[jax.ShapeDtypeStruct((S, D_ATTN), F32)] * 2,
        compiler_params=_params("parallel"))(q, k, v)


def _attn_merge(os_, ls_, g_attn):
    def body(i, n, o1, o2, o3, l1, l2, l3, g_ref, a_ref, lse_ref, an_ref):
        la, lb, lc = l1[...], l2[...], l3[...]
        m = jnp.maximum(jnp.maximum(la, lb), lc)
        ea, eb, ec = jnp.exp(la - m), jnp.exp(lb - m), jnp.exp(lc - m)
        den = ea + eb + ec
        attn = (ea * o1[...] + eb * o2[...] + ec * o3[...]) / den
        a_ref[...] = attn
        lse_ref[...] = m + jnp.log(den)
        nx, _ = _rms_n(attn)
        an_ref[...] = (nx * g_ref[...]).astype(BF)

    return _rowwise(body, name="attn_merge", nrows=S, tile=ROW_TILE,
                    row_ins=[(a, D_ATTN, 0) for a in (*os_, *ls_)], full_ins=[g_attn],
                    row_outs=[(D_ATTN, F32), (D_ATTN, F32), (D_ATTN, BF)])


def _conv_taps(x, w_ref):
    rows = lax.broadcasted_iota(jnp.int32, x.shape, 0)
    shifted = []
    for w in range(CONV_W):
        k = CONV_W - 1 - w
        shifted.append(x if k == 0 else jnp.where(rows >= k, pltpu.roll(x, k, 0), 0.0))
    acc = shifted[0] * w_ref[0:1, :]
    for w in range(1, CONV_W):
        acc = acc + shifted[w] * w_ref[w:w + 1, :]
    return acc, shifted


def _conv_fwd(proj, conv_w, conv_b):
    tc = 256

    def kern(x_ref, w_ref, b_ref, o_ref):
        c, _ = _conv_taps(x_ref[...], w_ref)
        c = c + b_ref[...]
        o_ref[...] = c * _sigmoid(c)

    return pl.pallas_call(
        kern, name="conv_fwd", grid=(D_CONV // tc,),
        in_specs=[pl.BlockSpec((S, tc), lambda c: (0, 4 * D_ATTN // tc + c)),
                  pl.BlockSpec((CONV_W, tc), lambda c: (0, c)), pl.BlockSpec((1, tc), lambda c: (0, c))],
        out_specs=pl.BlockSpec((S, tc), lambda c: (0, c)),
        out_shape=jax.ShapeDtypeStruct((S, D_CONV), F32), compiler_params=_params("parallel"))(proj, conv_w, conv_b)


def _softplus(x):
    return jnp.maximum(x, 0.0) + jnp.log1p(jnp.exp(-jnp.abs(x)))


def _ssd_prep(proj, dt_bias128, a_log128):
    def body(i, n, raw_ref, b_ref, al_ref, dt_ref, a_ref, carry):
        @pl.when(i == 0)
        def _():
            carry[...] = jnp.zeros_like(carry)

        dt = _softplus(raw_ref[...] + b_ref[...])
        da = dt * (-jnp.exp(al_ref[...]))
        tri = (lax.broadcasted_iota(jnp.int32, (BLK, BLK), 0) >= lax.broadcasted_iota(jnp.int32, (BLK, BLK), 1))
        cs = _dot3(tri.astype(BF), da) + carry[0:1, :]
        dt_ref[...] = dt
        a_ref[...] = cs
        carry[...] = jnp.broadcast_to(cs[BLK - 1:BLK, :], carry.shape)

    return _rowwise(body, name="ssd_prep", nrows=S, tile=BLK, row_ins=[(proj, LANES, (D_INP - LANES) // LANES)],
                    full_ins=[dt_bias128, a_log128], row_outs=[(LANES, F32), (LANES, F32)],
                    scratch=[pltpu.VMEM((8, LANES), F32)])


def _ssd_decay(a_col, at_row, causal):
    diff = jnp.where(causal, a_col - at_row, 0.0)
    return jnp.where(causal, jnp.exp(diff), 0.0)


SSD_CB = 2


def _ssd_fwd(xbc, a_b, dt_b, at_r, dt_r):
    def kern(c_ref, b_ref, x_ref, ab_ref, dtb_ref, at_ref, dt_ref, y_ref, hall_ref):
        half0 = lax.broadcasted_iota(jnp.int32, (BLK, LANES), 1) < HEAD
        causal = lax.broadcasted_iota(jnp.int32, (BLK, BLK), 1) <= lax.broadcasted_iota(jnp.int32, (BLK, BLK), 0)

        def step(i, carry):
            h, a_prev = carry
            for cc in range(SSD_CB):
                chunk = i * SSD_CB + cc
                rows = pl.ds(pl.multiple_of(chunk * BLK, BLK), BLK)
                ci, bi, x = c_ref[rows, :].astype(BF), b_ref[rows, :].astype(BF), x_ref[rows, :]
                ab = ab_ref[rows, :]
                a_end = ab[BLK - 1:BLK, :]
                alpha = jnp.exp(ab - a_prev)
                beta = jnp.exp(a_end - ab) * dtb_ref[rows, :]
                hall_ref[rows, :] = h
                cb = _dot(ci, bi, NT)
                at, dtj = at_ref[chunk], dt_ref[chunk]
                y = alpha * _dot(ci, h.astype(BF), NN)
                for hd in range(2):
                    hm = half0 if hd == 0 else jnp.logical_not(half0)
                    w = cb * _ssd_decay(ab[:, HEAD * hd:HEAD * hd + 1], at[hd:hd + 1, :], causal) * dtj[hd:hd + 1, :]
                    y = y + _dot(w.astype(BF), jnp.where(hm, x, 0.0).astype(BF), NN)
                y_ref[rows, :] = y
                h = alpha[BLK - 1:BLK, :] * h + _dot(bi, (beta * x).astype(BF), TN)
                a_prev = a_end
            return h, a_prev

        lax.fori_loop(0, NB // SSD_CB, step, (jnp.zeros((BLK, LANES), F32), jnp.zeros((1, LANES), F32)))

    npair = D_SSM // LANES
    rowvec = pl.BlockSpec((None, NB, 2, BLK), lambda p: (p, 0, 0, 0))
    seq = pl.BlockSpec((S, LANES), lambda p: (0, p))
    return pl.pallas_call(
        kern, name="ssd_fwd", grid=(npair,),
        in_specs=[pl.BlockSpec((S, LANES), lambda p: (0, 12 + p // 2)),
                  pl.BlockSpec((S, LANES), lambda p: (0, 8 + p // 2)), seq, seq, seq, rowvec, rowvec],
        out_specs=[seq, seq], out_shape=[jax.ShapeDtypeStruct((S, D_SSM), F32)] * 2,
        compiler_params=_params("parallel"))(xbc, xbc, xbc, a_b, dt_b, at_r, dt_r)


def _ssd_post(y_ssd, xbc, proj, dskip_b, g_ssm):
    def body(i, n, y_ref, xs_ref, z_ref, d_ref, g_ref, o_ref):
        z = z_ref[...]
        y2 = (y_ref[...] + d_ref[...] * xs_ref[...]) * (z * _sigmoid(z))
        nx, _ = _seg_rms_n(y2, 256)
        o_ref[...] = (nx * g_ref[...]).astype(BF)

    return _rowwise(body, name="ssd_post", nrows=S, tile=ROW_TILE,
                    row_ins=[(y_ssd, D_SSM, 0), (xbc, D_SSM, 0), (proj, D_SSM, 3)], full_ins=[dskip_b, g_ssm],
                    row_outs=[(D_SSM, BF)])[0]


def _cross_heads(q, kv_ref, gq, gk):
    out = []
    for h in range(D_CROSS // CROSS_HEAD):
        sl = slice(CROSS_HEAD * h, CROSS_HEAD * (h + 1))
        nq, rq = _rms_n(q[:, sl])
        nk, rk = _rms_n(kv_ref[:, sl])
        v = kv_ref[:, D_CROSS + CROSS_HEAD * h:D_CROSS + CROSS_HEAD * (h + 1)]
        out.append((sl, nq, rq, nk, rk, v))
    return out


def _cross_fwd(qc, kv, g_cq, g_ck):
    scale = CROSS_HEAD ** -0.5

    def body(i, n, q_ref, kv_ref, gq_ref, gk_ref, o_ref):
        for sl, nq, _, nk, _, v in _cross_heads(q_ref[...], kv_ref, gq_ref[...], gk_ref[...]):
            qn = (nq * gq_ref[...] * scale).astype(BF)
            kn = (nk * gk_ref[...]).astype(BF)
            s = _dot(qn, kn, NT)
            e = jnp.exp(s - jnp.max(s, axis=-1, keepdims=True))
            p = e / jnp.sum(e, axis=-1, keepdims=True)
            o_ref[:, sl] = _dot(p.astype(BF), v.astype(BF), NN).astype(BF)

    return _rowwise(body, name="cross_fwd", nrows=S, tile=ROW_TILE, row_ins=[(qc, D_CROSS, 0)],
                    full_ins=[kv, g_cq, g_ck], row_outs=[(D_CROSS, BF)])[0]


def _loss_sum(dy):
    def body(i, n, d_ref, o_ref):
        d = d_ref[...]
        s = jnp.sum(jnp.sum(d * d, axis=0, keepdims=True), axis=1, keepdims=True)
        _accum(o_ref, s, i)

    return _rowwise(body, name="loss_sum", nrows=S, tile=ROW_TILE, row_ins=[(dy, D, 0)],
                    acc_outs=[((1, 1), F32)])[0]


def _rms_bwd_call(x, g, dh, dres, name):
    rows = x.shape[0]
    has_res = dres is not None

    def body(i, n, *refs):
        if has_res:
            x_ref, dh_ref, dr_ref, g_ref, dx_ref, dxb_ref, dg_ref = refs
        else:
            x_ref, dh_ref, g_ref, dg_ref = refs
        dx, dg = _rms_bwd(x_ref[...], g_ref[...], dh_ref[...])
        if has_res:
            dx = dx + dr_ref[...]
            dx_ref[...] = dx
            dxb_ref[...] = dx.astype(BF)
        _accum(dg_ref, dg, i)

    row_ins = [(x, D, 0), (dh, D, 0)] + ([(dres, D, 0)] if has_res else [])
    return _rowwise(body, name=name, nrows=rows, tile=min(ROW_TILE, rows), row_ins=row_ins, full_ins=[g],
                    row_outs=[(D, F32), (D, BF)] if has_res else [], acc_outs=[((1, D), F32)])


def _cross_bwd(qc, doc, kv, g_cq, g_ck):
    scale = CROSS_HEAD ** -0.5

    def body(i, n, q_ref, do_ref, kv_ref, gq_ref, gk_ref, dq_ref, dkn_ref, dv_ref, dgq_ref):
        dgq = jnp.zeros((1, CROSS_HEAD), F32)
        dkn_parts, dv_parts = [], []
        for sl, nq, rq, nk, _, v in _cross_heads(q_ref[...], kv_ref, gq_ref[...], gk_ref[...]):
            qn = (nq * gq_ref[...] * scale).astype(BF)
            kn = (nk * gk_ref[...]).astype(BF)
            s = _dot(qn, kn, NT)
            e = jnp.exp(s - jnp.max(s, axis=-1, keepdims=True))
            p = e / jnp.sum(e, axis=-1, keepdims=True)
            do = do_ref[:, sl].astype(BF)
            dv_parts.append(_dot(p.astype(BF), do, TN))
            dp = _dot(do, v.astype(BF), NT)
            ds = (p * (dp - jnp.sum(dp * p, axis=-1, keepdims=True))).astype(BF)
            dqn = _dot(ds, kn, NN)
            dkn_parts.append(_dot(ds, qn, TN))
            dn = dqn * (gq_ref[...] * scale)
            dq_ref[:, sl] = (rq * (dn - nq * jnp.mean(dn * nq, axis=-1, keepdims=True))).astype(BF)
            dgq = dgq + jnp.sum(dqn * scale * nq, axis=0, keepdims=True)
        _accum(dkn_ref, jnp.concatenate(dkn_parts, axis=1), i)
        _accum(dv_ref, jnp.concatenate(dv_parts, axis=1), i)
        _accum(dgq_ref, dgq, i)

    return _rowwise(body, name="cross_bwd", nrows=S, tile=ROW_TILE, row_ins=[(qc, D_CROSS, 0), (doc, D_CROSS, 0)],
                    full_ins=[kv, g_cq, g_ck], row_outs=[(D_CROSS, BF)],
                    acc_outs=[((N_MEM, D_CROSS), F32), ((N_MEM, D_CROSS), F32), ((1, CROSS_HEAD), F32)])


def _cross_kv_bwd(kv, dkn, dv, g_ck):
    def body(i, n, kv_ref, dkn_ref, dv_ref, gk_ref, dkv_ref, dgk_ref):
        dgk = jnp.zeros((1, CROSS_HEAD), F32)
        for h in range(D_CROSS // CROSS_HEAD):
            sl = slice(CROSS_HEAD * h, CROSS_HEAD * (h + 1))
            dk, dg = _rms_bwd(kv_ref[:, sl], gk_ref[...], dkn_ref[:, sl])
            dkv_ref[:, sl] = dk.astype(BF)
            dgk = dgk + dg
        dkv_ref[:, D_CROSS:] = dv_ref[...].astype(BF)
        dgk_ref[...] = dgk

    return _rowwise(body, name="cross_kv_bwd", nrows=N_MEM, tile=N_MEM,
                    row_ins=[(kv, 2 * D_CROSS, 0), (dkn, D_CROSS, 0), (dv, D_CROSS, 0)], full_ins=[g_ck],
                    row_outs=[(2 * D_CROSS, BF)], acc_outs=[((1, CROSS_HEAD), F32)])


def _attn_merge_bwd(dmix, attn, g_attn):
    def body(i, n, dn_ref, a_ref, g_ref, da_ref, dl_ref, dg_ref):
        attn_v = a_ref[...]
        da, dg = _rms_bwd(attn_v, g_ref[...], dn_ref[...])
        da_ref[...] = da
        dl_ref[...] = _seg_sum(da * attn_v, HEAD)
        _accum(dg_ref, dg, i)

    return _rowwise(body, name="attn_merge_bwd", nrows=S, tile=ROW_TILE,
                    row_ins=[(dmix, D_ATTN, 0), (attn, D_ATTN, 0)], full_ins=[g_attn],
                    row_outs=[(D_ATTN, F32), (D_ATTN, F32)], acc_outs=[((1, D_ATTN), F32)])


def _attn_bwd(q, k, v, do, lse, delta, d, name):
    def kern(q_ref, k_ref, v_ref, do_ref, l_ref, d_ref, dq_ref, dk_ref, dv_ref):
        dk_ref[...] = jnp.zeros_like(dk_ref)
        dv_ref[...] = jnp.zeros_like(dv_ref)
        half0 = lax.broadcasted_iota(jnp.int32, (BLK, LANES), 1) < HEAD

        def group(g, carry):
            updates = []
            for bb in range(ATTN_QB):
                q_rows, k_rows, valid = _band(g * ATTN_QB + bb, d)
                q, do = q_ref[q_rows, :], do_ref[q_rows, :]
                lse_t, del_t = l_ref[q_rows, :], d_ref[q_rows, :]
                kb, vb = k_ref[k_rows, :].astype(BF), v_ref[k_rows, :].astype(BF)
                dq_h, dk, dv = [], None, None
                for h in range(2):
                    hm = half0 if h == 0 else jnp.logical_not(half0)
                    qm = jnp.where(hm, q, 0.0).astype(BF)
                    dom = jnp.where(hm, do, 0.0).astype(BF)
                    s = jnp.where(valid, _dot(qm, kb, NT), NEG)
                    p = jnp.exp(s - lse_t[:, HEAD * h:HEAD * h + 1])
                    ds = (p * (_dot(dom, vb, NT) - del_t[:, HEAD * h:HEAD * h + 1])).astype(BF)
                    dq_h.append(_dot(ds, kb, NN))
                    dk_h, dv_h = _dot(ds, qm, TN), _dot(p.astype(BF), dom, TN)
                    dk, dv = (dk_h, dv_h) if h == 0 else (dk + dk_h, dv + dv_h)
                dq_ref[q_rows, :] = jnp.where(half0, dq_h[0], dq_h[1])
                updates.append((k_rows, dk, dv))
            for k_rows, dk, dv in updates:
                dk_ref[k_rows, :] += dk
                dv_ref[k_rows, :] += dv
            return carry

        lax.fori_loop(0, NB // ATTN_QB, group, 0)

    seq = pl.BlockSpec((S, LANES), lambda p: (0, p))
    return pl.pallas_call(
        kern, name=name, grid=(D_ATTN // LANES,), in_specs=[seq, seq, V_COLS, seq, seq, seq],
        out_specs=[seq, seq, seq], out_shape=[jax.ShapeDtypeStruct((S, D_ATTN), F32)] * 3,
        compiler_params=_params("parallel"))(q, k, v, do, lse, delta)


def _qk_bwd(dqs, dks, dvs, proj, pos_col, gq128, gk128, inv_tab):
    scale = HEAD ** -0.5

    def body(i, n, *refs):
        dq_refs, dk_refs, dv_refs = refs[0:3], refs[3:6], refs[6:9]
        q_ref, k_ref, p_ref, gq_ref, gk_ref, it_ref = refs[9:15]
        dqo_ref, dko_ref, dvo_ref, dgq_ref, dgk_ref = refs[15:20]
        t = q_ref.shape[0]
        c, sg, lo = _rope_tables(p_ref[...], it_ref[...], t)
        dgq = jnp.zeros((1, LANES), F32)
        dgk = jnp.zeros((1, LANES), F32)
        for j in range(D_ATTN // LANES):
            sl = slice(LANES * j, LANES * (j + 1))
            dqr = _rope(dq_refs[0][:, sl] + dq_refs[1][:, sl] + dq_refs[2][:, sl], c, -sg, lo) * scale
            dkr = _rope(dk_refs[0][:, sl] + dk_refs[1][:, sl] + dk_refs[2][:, sl], c, -sg, lo)
            dq, gq = _seg_rms_bwd(q_ref[:, sl], gq_ref[...], dqr, HEAD)
            dk, gk = _seg_rms_bwd(k_ref[:, sl], gk_ref[...], dkr, HEAD)
            dqo_ref[:, sl] = dq.astype(BF)
            dko_ref[:, sl] = dk.astype(BF)
            dgq, dgk = dgq + gq, dgk + gk
        dvo_ref[...] = (dv_refs[0][...] + dv_refs[1][...] + dv_refs[2][...]).astype(BF)
        _accum(dgq_ref, _fold_heads(dgq), i)
        _accum(dgk_ref, _fold_heads(dgk), i)

    return _rowwise(body, name="qk_bwd", nrows=S, tile=ROW_TILE,
                    row_ins=[(a, D_ATTN, 0) for a in (*dqs, *dks, *dvs)]
                    + [(proj, D_ATTN, 0), (proj, D_ATTN, 1), (pos_col, 1, 0)],
                    full_ins=[gq128, gk128, inv_tab], row_outs=[(D_ATTN, BF)] * 3,
                    acc_outs=[((1, LANES), F32)] * 2)


def _ssd_post_bwd(y_ssd, xbc, proj, dmix, dskip_b, g_ssm):
    def body(i, n, y_ref, xs_ref, z_ref, do_ref, d_ref, g_ref, dy_ref, dz_ref, dxs_ref, dd_ref, dg_ref):
        z, xs = z_ref[...], xs_ref[...]
        sg = _sigmoid(z)
        gate = z * sg
        y = y_ref[...] + d_ref[...] * xs
        dy2, dg = _seg_rms_bwd(y * gate, g_ref[...], do_ref[...], 256)
        dy = dy2 * gate
        dy_ref[...] = dy.astype(BF)
        dz_ref[...] = (dy2 * y * (sg * (1.0 + z * (1.0 - sg)))).astype(BF)
        dxs_ref[...] = dy * d_ref[...]
        _accum(dd_ref, jnp.sum(dy * xs, axis=0, keepdims=True), i)
        _accum(dg_ref, dg, i)

    return _rowwise(body, name="ssd_post_bwd", nrows=S, tile=ROW_TILE,
                    row_ins=[(y_ssd, D_SSM, 0), (xbc, D_SSM, 0), (proj, D_SSM, 3), (dmix, D_SSM, 1)],
                    full_ins=[dskip_b, g_ssm], row_outs=[(D_SSM, BF), (D_SSM, BF), (D_SSM, F32)],
                    acc_outs=[((1, D_SSM), F32), ((1, D_SSM), F32)])


def _ssd_bwd(xbc, dy, h_all, a_b, dt_b, at_r, dt_r):
    def kern(c_ref, b_ref, x_ref, dy_ref, hall_ref, ab_ref, dtb_ref, at_ref, dt_ref,
             dc_ref, daq_ref, dx_ref, db_ref, dak_ref, ddt_ref, ddtb_ref):
        half0 = lax.broadcasted_iota(jnp.int32, (BLK, LANES), 1) < HEAD
        hms = (half0, jnp.logical_not(half0))
        causal = lax.broadcasted_iota(jnp.int32, (BLK, BLK), 1) <= lax.broadcasted_iota(jnp.int32, (BLK, BLK), 0)
        row = lax.broadcasted_iota(jnp.int32, (BLK, LANES), 0)

        def step(t, carry_in):
            dh_next, carry = carry_in
            for cc in reversed(range(SSD_CB)):
                chunk = (NB // SSD_CB - 1 - t) * SSD_CB + cc
                rows = pl.ds(pl.multiple_of(chunk * BLK, BLK), BLK)
                ci, bi, x = c_ref[rows, :].astype(BF), b_ref[rows, :].astype(BF), x_ref[rows, :]
                dyv = dy_ref[rows, :].astype(F32)
                ab, dtb = ab_ref[rows, :], dtb_ref[rows, :]
                before = ab_ref[pl.ds(pl.multiple_of(jnp.maximum(chunk * BLK - 8, 0), 8), 8), :][7:8, :]
                a_prev = jnp.where(chunk == 0, 0.0, before)
                a_end = ab[BLK - 1:BLK, :]
                alpha = jnp.exp(ab - a_prev)
                e_end = jnp.exp(a_end - ab)
                beta = e_end * dtb
                t_all = alpha[BLK - 1:BLK, :]
                hc = hall_ref[rows, :]
                hcb, dhb = hc.astype(BF), dh_next.astype(BF)

                cb = _dot(ci, bi, NT)
                at, dtj = at_ref[chunk], dt_ref[chunk]
                dcb = jnp.zeros((BLK, BLK), F32)
                dx = jnp.zeros((BLK, LANES), F32)
                rsum = []
                for hd in range(2):
                    dym = jnp.where(hms[hd], dyv, 0.0).astype(BF)
                    lm = _ssd_decay(ab[:, HEAD * hd:HEAD * hd + 1], at[hd:hd + 1, :], causal)
                    dth = dtj[hd:hd + 1, :]
                    dw = _dot(dym, jnp.where(hms[hd], x, 0.0).astype(BF), NT)
                    g = dw * cb * lm
                    dx = dx + _dot((cb * lm * dth).astype(BF), dym, TN)
                    gcol = jnp.sum(g, axis=0, keepdims=True)
                    ddt_ref[chunk, hd:hd + 1, :] = gcol
                    dak_ref[chunk, hd:hd + 1, :] = gcol * dth
                    rsum.append(jnp.sum(g * dth, axis=1, keepdims=True))
                    dcb = dcb + dw * lm * dth
                dcb = dcb.astype(BF)

                g1 = (alpha * dyv).astype(BF)
                dalpha = dyv * _dot(ci, hcb, NN)
                g2 = _dot(bi, dhb, NN)
                dbeta = x * g2
                bx = (beta * x).astype(BF)
                dc_ref[rows, :] = _dot(dcb, bi, NN) + _dot(g1, hcb, NT)
                db_ref[rows, :] = _dot(dcb, ci, TN) + _dot(bx, dhb, NT)
                dx_ref[rows, :] = dx + beta * g2
                ddtb_ref[rows, :] = _seg_sum(e_end * dbeta, HEAD)
                ada, bdb = alpha * dalpha, beta * dbeta
                t_dt = t_all * jnp.sum(dh_next * hc, axis=0, keepdims=True)
                end_term = _seg_sum(jnp.broadcast_to(jnp.sum(bdb, axis=0, keepdims=True) + t_dt, (8, LANES)), HEAD)[0:1]
                prev_term = _seg_sum(jnp.broadcast_to(jnp.sum(ada, axis=0, keepdims=True) + t_dt, (8, LANES)), HEAD)[0:1]
                daq = jnp.where(half0, rsum[0], rsum[1]) + _seg_sum(ada - bdb, HEAD)
                daq_ref[rows, :] = daq + jnp.where(row == BLK - 1, end_term - carry, 0.0)
                carry = prev_term
                dh_next = t_all * dh_next + _dot(ci, g1, TN)
            return dh_next, carry

        lax.fori_loop(0, NB // SSD_CB, step, (jnp.zeros((BLK, LANES), F32), jnp.zeros((1, LANES), F32)))

    npair = D_SSM // LANES
    rowvec = pl.BlockSpec((None, NB, 2, BLK), lambda p: (p, 0, 0, 0))
    seq = pl.BlockSpec((S, LANES), lambda p: (0, p))
    return pl.pallas_call(
        kern, name="ssd_bwd", grid=(npair,),
        in_specs=[pl.BlockSpec((S, LANES), lambda p: (0, 12 + p // 2)),
                  pl.BlockSpec((S, LANES), lambda p: (0, 8 + p // 2)),
                  seq, seq, seq, seq, seq, rowvec, rowvec],
        out_specs=[seq, seq, seq, seq, rowvec, rowvec, seq],
        out_shape=[jax.ShapeDtypeStruct((S, D_SSM), F32)] * 4
        + [jax.ShapeDtypeStruct((npair, NB, 2, BLK), F32)] * 2 + [jax.ShapeDtypeStruct((S, D_SSM), F32)],
        compiler_params=_params("parallel"))(xbc, xbc, xbc, dy, h_all, a_b, dt_b, at_r, dt_r)


def _ssd_prep_bwd(daq, dak, ddt_row, ddt_lane, dt, proj, dt_bias128, a_log128):
    def body(i, n, daq_ref, dak_ref, dd_ref, dd2_ref, dt_ref, raw_ref, b_ref, al_ref, draw_ref, dal_ref, db_ref,
             carry):
        @pl.when(i == 0)
        def _():
            carry[...] = jnp.zeros_like(carry)

        a = -jnp.exp(al_ref[...])
        tri = (lax.broadcasted_iota(jnp.int32, (BLK, BLK), 0) <= lax.broadcasted_iota(jnp.int32, (BLK, BLK), 1))
        rev = _dot3(tri.astype(BF), daq_ref[...] - dak_ref[...]) + carry[0:1, :]
        carry[...] = jnp.broadcast_to(rev[0:1, :], carry.shape)
        dtv = dt_ref[...]
        draw = (dd_ref[...] + dd2_ref[...] + a * rev) * _sigmoid(raw_ref[...] + b_ref[...])
        draw_ref[...] = draw.astype(BF)
        _accum(dal_ref, jnp.sum(dtv * rev, axis=0, keepdims=True) * a, i)
        _accum(db_ref, jnp.sum(draw, axis=0, keepdims=True), i)

    return _rowwise(body, name="ssd_prep_bwd", nrows=S, tile=BLK, reverse=True,
                    row_ins=[(daq, LANES, 0), (dak, LANES, 0), (ddt_row, LANES, 0), (ddt_lane, LANES, 0), (dt, LANES, 0),
                             (proj, LANES, (D_INP - LANES) // LANES)],
                    full_ins=[dt_bias128, a_log128], row_outs=[(LANES, BF)],
                    acc_outs=[((1, LANES), F32), ((1, LANES), F32)], scratch=[pltpu.VMEM((8, LANES), F32)])


def _conv_bwd(proj, conv_w, conv_b, d1, d2, map1, map2, col0, ntiles, name):
    base = (4 * D_ATTN + col0) // LANES

    def kern(x_ref, w_ref, b_ref, d1_ref, d2_ref, dx_ref, dw_ref, db_ref):
        x = x_ref[...]
        c, shifted = _conv_taps(x, w_ref)
        c = c + b_ref[...]
        sg = _sigmoid(c)
        dc = (d1_ref[...] + d2_ref[...]) * (sg * (1.0 + c * (1.0 - sg)))
        rows = lax.broadcasted_iota(jnp.int32, x.shape, 0)
        dx = jnp.zeros_like(x)
        for w in range(CONV_W):
            k = CONV_W - 1 - w
            up = dc if k == 0 else jnp.where(rows < S - k, pltpu.roll(dc, S - k, 0), 0.0)
            dx = dx + up * w_ref[w:w + 1, :]
            dw_ref[w:w + 1, :] = jnp.sum(dc * shifted[w], axis=0, keepdims=True)
        dx_ref[...] = dx.astype(BF)
        db_ref[...] = jnp.sum(dc, axis=0, keepdims=True)

    c0 = col0 // LANES
    return pl.pallas_call(
        kern, name=name, grid=(ntiles,),
        in_specs=[pl.BlockSpec((S, LANES), lambda c: (0, base + c)),
                  pl.BlockSpec((CONV_W, LANES), lambda c: (0, c0 + c)),
                  pl.BlockSpec((1, LANES), lambda c: (0, c0 + c)),
                  pl.BlockSpec((S, LANES), lambda c: (0, map1(c))),
                  pl.BlockSpec((S, LANES), lambda c: (0, map2(c)))],
        out_specs=[pl.BlockSpec((S, LANES), lambda c: (0, c)), pl.BlockSpec((CONV_W, LANES), lambda c: (0, c)),
                   pl.BlockSpec((1, LANES), lambda c: (0, c))],
        out_shape=[jax.ShapeDtypeStruct((S, ntiles * LANES), BF), jax.ShapeDtypeStruct((CONV_W, ntiles * LANES), F32),
                   jax.ShapeDtypeStruct((1, ntiles * LANES), F32)],
        compiler_params=_params("parallel"))(proj, conv_w, conv_b, d1, d2)


def _adamw(w, m, v, parts, name):
    r, c = w.shape
    n_parts = parts.shape[0]
    tile = r if r <= 512 else (128 if c > 1024 else 256)
    assert r % tile == 0
    c1 = 1.0 - ADAM_B1 ** ADAM_STEP
    c2 = 1.0 - ADAM_B2 ** ADAM_STEP

    def kern(w_ref, m_ref, v_ref, p_ref, g_ref, d_ref, mo_ref, vo_ref):
        g = p_ref[0].astype(F32)
        for k in range(1, n_parts):
            g = g + p_ref[k].astype(F32)
        mn = ADAM_B1 * m_ref[...] + (1.0 - ADAM_B1) * g
        vn = ADAM_B2 * v_ref[...] + (1.0 - ADAM_B2) * (g * g)
        g_ref[...] = g
        mo_ref[...] = mn
        vo_ref[...] = vn
        d_ref[...] = -ADAM_LR * ((mn / c1) / (jnp.sqrt(vn / c2) + ADAM_EPS) + ADAM_WD * w_ref[...])

    blk = pl.BlockSpec((tile, c), lambda i: (i, 0))
    return pl.pallas_call(
        kern, name=name, grid=(r // tile,),
        in_specs=[blk, blk, blk, pl.BlockSpec((n_parts, tile, c), lambda i: (0, i, 0))],
        out_specs=[blk] * 4, out_shape=[jax.ShapeDtypeStruct((r, c), F32)] * 4,
        compiler_params=_params("parallel"))(w, m, v, parts)


MESH = pl.DeviceIdType.MESH
ANY = pl.BlockSpec(memory_space=pl.ANY)


def _put_own(gathered, shard):
    me = 4 * lax.axis_index("x") + 2 * lax.axis_index("y") + lax.axis_index("c")
    return lax.dynamic_update_slice(gathered, shard[None], (me,) + (0,) * shard.ndim)


def _all_gather(arrs, name, after=None):
    na = len(arrs)
    n_after = 0 if after is None else 1

    def kern(*refs):
        ins, outs = refs[:na], refs[na + n_after:2 * na + n_after]
        send_sems, recv_sems = refs[2 * na + n_after:]
        x, y, c = lax.axis_index("x"), lax.axis_index("y"), lax.axis_index("c")
        me, sibling = (x, y, c), (x, y, 1 - c)
        chips = [(1 - x, y), (x, 1 - y), (1 - x, 1 - y)]

        def copy(a, k, block, to, src=None):
            px, py, pc = block
            dst = outs[a].at[4 * px + 2 * py + pc]
            return pltpu.make_async_remote_copy(
                src_ref=dst if src is None else src, dst_ref=dst, send_sem=send_sems.at[a, k],
                recv_sem=recv_sems.at[a, k], device_id=to, device_id_type=MESH)

        first = []
        for a in range(na):
            first.append(copy(a, 0, me, sibling, src=ins[a]))
            first += [copy(a, 1 + j, me, (*chip, c), src=ins[a]) for j, chip in enumerate(chips)]
        for cp in first:
            cp.start()
        passed = []
        for a in range(na):
            for j, chip in enumerate(chips):
                copy(a, 1 + j, (*chip, c), me).wait_recv()
                fwd = copy(a, 4 + j, (*chip, c), sibling)
                fwd.start()
                passed.append(fwd)
        for a in range(na):
            copy(a, 0, sibling, me).wait_recv()
            for j, chip in enumerate(chips):
                copy(a, 4 + j, (*chip, 1 - c), me).wait_recv()
        for cp in first + passed:
            cp.wait_send()

    outs = pl.pallas_call(
        kern, name=name, in_specs=[ANY] * (na + n_after), out_specs=[ANY] * na,
        out_shape=[jax.ShapeDtypeStruct((N_DEV,) + a.shape, a.dtype) for a in arrs],
        scratch_shapes=[pltpu.SemaphoreType.DMA((na, 7)), pltpu.SemaphoreType.DMA((na, 7))],
    )(*arrs, *([] if after is None else [after]))
    return [_put_own(o, a) for o, a in zip(outs, arrs)]


HBM = pl.BlockSpec(memory_space=pltpu.HBM)
SEM = pl.BlockSpec(memory_space=pltpu.SEMAPHORE)
EFFECT = pltpu.SideEffectType.DATAFLOW_SIDE_EFFECTING
N_CHIP = 4
N_COPIES = {"gather": 3, "scatter": 3, "forward": 4, "pair": 4}


def _in_hbm(a):
    return pltpu.with_memory_space_constraint(a, pltpu.HBM)


def _chip_copies(kind, src_refs, land_refs, send_sems, recv_sems):
    x, y, c = lax.axis_index("x"), lax.axis_index("y"), lax.axis_index("c")
    chips = [(1 - x, y), (x, 1 - y), (1 - x, 1 - y)]
    n = N_COPIES[kind]
    cps = []

    def add(a, j, src, dst, to):
        cps.append(pltpu.make_async_remote_copy(
            src_ref=src, dst_ref=dst, send_sem=send_sems.at[n * a + j], recv_sem=recv_sems.at[n * a + j],
            device_id=to, device_id_type=MESH))

    for a in range(len(src_refs)):
        if kind == "gather":
            for j, (px, py) in enumerate(chips):
                add(a, j, src_refs[a], land_refs[a].at[4 * x + 2 * y + c], (px, py, c))
        elif kind == "scatter":
            for j, (px, py) in enumerate(chips):
                add(a, j, src_refs[a].at[2 * px + py], land_refs[a].at[2 * x + y], (px, py, c))
        elif kind == "forward":
            add(a, 0, src_refs[a], land_refs[a].at[4 * x + 2 * y + c], (x, y, 1 - c))
            for j, (px, py) in enumerate(chips):
                slot = land_refs[a].at[4 * px + 2 * py + c]
                add(a, 1 + j, slot, slot, (x, y, 1 - c))
        else:
            for q in range(N_CHIP):
                add(a, q, src_refs[a].at[2 * q + 1 - c], land_refs[a].at[q], (x, y, 1 - c))
    return cps


def _exchange_start(kind, srcs, lands, after, name):
    na = len(srcs)
    n_after = 0 if after is None else 1

    def kern(*refs):
        src_refs, land_refs = refs[:na], refs[na:2 * na]
        send_sems, recv_sems = refs[2 * na + n_after], refs[2 * na + n_after + 1]
        token = refs[-1]
        for cp in _chip_copies(kind, src_refs, land_refs, send_sems, recv_sems):
            cp.start()
        token[...] = jnp.zeros_like(token)

    bufs = list(srcs) + list(lands)
    res = pl.pallas_call(
        kern, name=name,
        out_shape=(pltpu.SemaphoreType.DMA((N_COPIES[kind] * na,)), pltpu.SemaphoreType.DMA((N_COPIES[kind] * na,)))
        + tuple(pltpu.HBM(b.shape, b.dtype) for b in bufs) + (jax.ShapeDtypeStruct((8, LANES), F32),),
        in_specs=[HBM] * (2 * na) + [ANY] * n_after,
        out_specs=(SEM, SEM) + (HBM,) * (2 * na) + (pl.BlockSpec(memory_space=pltpu.VMEM),),
        input_output_aliases={i: 2 + i for i in range(2 * na)},
        compiler_params=pltpu.CompilerParams(has_side_effects=EFFECT),
    )(*[_in_hbm(b) for b in bufs], *([] if after is None else [after]))
    return (res[0], res[1]), list(res[2:2 + na]), list(res[2 + na:2 + 2 * na]), res[-1]


def _exchange_wait(kind, sems, srcs, lands, after, name):
    na = len(srcs)

    def kern(*refs):
        src_refs, land_refs = refs[:na], refs[na:2 * na]
        send_sems, recv_sems = refs[2 * na], refs[2 * na + 1]
        for cp in _chip_copies(kind, src_refs, land_refs, send_sems, recv_sems):
            cp.wait_send()
            cp.wait_recv()

    bufs = list(srcs) + list(lands)
    res = pl.pallas_call(
        kern, name=name, out_shape=tuple(pltpu.HBM(b.shape, b.dtype) for b in bufs),
        in_specs=[HBM] * (2 * na) + [SEM, SEM, ANY], out_specs=(HBM,) * (2 * na),
        input_output_aliases={i: i for i in range(2 * na)},
        compiler_params=pltpu.CompilerParams(has_side_effects=EFFECT),
    )(*bufs, sems[0], sems[1], after)
    return list(res[:na]), list(res[na:])


def _gather_finish(shards, lands, name):
    na = len(shards)

    def kern(*refs):
        ins, outs = refs[:na], refs[2 * na:3 * na]
        send_sems, recv_sems = refs[3 * na:]
        x, y, c = lax.axis_index("x"), lax.axis_index("y"), lax.axis_index("c")
        chips = [(1 - x, y), (x, 1 - y), (1 - x, 1 - y)]

        def copy(a, k, slot, pc, src=None):
            dst = outs[a].at[slot + pc]
            return pltpu.make_async_remote_copy(
                src_ref=dst if src is None else src, dst_ref=dst, send_sem=send_sems.at[a, k],
                recv_sem=recv_sems.at[a, k], device_id=(x, y, 1 - c), device_id_type=MESH)

        sends = []
        for a in range(na):
            sends.append(copy(a, 0, 4 * x + 2 * y, c, src=ins[a]))
            sends += [copy(a, 1 + j, 4 * px + 2 * py, c) for j, (px, py) in enumerate(chips)]
        for cp in sends:
            cp.start()
        for a in range(na):
            copy(a, 0, 4 * x + 2 * y, 1 - c).wait_recv()
            for j, (px, py) in enumerate(chips):
                copy(a, 1 + j, 4 * px + 2 * py, 1 - c).wait_recv()
        for cp in sends:
            cp.wait_send()

    return pl.pallas_call(
        kern, name=name, in_specs=[ANY] * (2 * na), out_specs=[ANY] * na,
        out_shape=[jax.ShapeDtypeStruct(l.shape, l.dtype) for l in lands],
        input_output_aliases={na + a: a for a in range(na)},
        scratch_shapes=[pltpu.SemaphoreType.DMA((na, 4)), pltpu.SemaphoreType.DMA((na, 4))])(*shards, *lands)


def _pair_exchange(parts, name):
    na = len(parts)

    def kern(*refs):
        ins, got = refs[:na], refs[na:2 * na]
        send_sems, recv_sems = refs[2 * na:]
        x, y, c = lax.axis_index("x"), lax.axis_index("y"), lax.axis_index("c")
        sends = []
        for a in range(na):
            for q in range(N_CHIP):
                sends.append(pltpu.make_async_remote_copy(
                    src_ref=ins[a].at[2 * q + 1 - c], dst_ref=got[a].at[q], send_sem=send_sems.at[a, q],
                    recv_sem=recv_sems.at[a, q], device_id=(x, y, 1 - c), device_id_type=MESH))
        for cp in sends:
            cp.start()
        for cp in sends:
            cp.wait()

    return pl.pallas_call(
        kern, name=name, in_specs=[ANY] * na, out_specs=[ANY] * na,
        out_shape=[jax.ShapeDtypeStruct((N_CHIP,) + p.shape[1:], p.dtype) for p in parts],
        scratch_shapes=[pltpu.SemaphoreType.DMA((na, N_CHIP)), pltpu.SemaphoreType.DMA((na, N_CHIP))])(*parts)


def _pair_sum(parts, got, name):
    _, r, cdim = parts.shape
    tile = min(r, ROW_TILE)
    core = lax.axis_index("c").astype(jnp.int32).reshape(1)

    def kern(c_ref, a_ref, b_ref, q_ref, l_ref):
        s = (a_ref[...].astype(F32) + b_ref[...].astype(F32)).astype(BF)
        q_ref[...] = s
        l_ref[...] = s

    blk = pl.BlockSpec((None, tile, cdim), lambda q, i, c_ref: (q, i, 0))
    out = jax.ShapeDtypeStruct((N_CHIP, r, cdim), BF)
    return pl.pallas_call(
        kern, name=name, out_shape=[out, out],
        grid_spec=pltpu.PrefetchScalarGridSpec(
            num_scalar_prefetch=1, grid=(N_CHIP, r // tile),
            in_specs=[pl.BlockSpec((None, tile, cdim), lambda q, i, c_ref: (2 * q + c_ref[0], i, 0)), blk],
            out_specs=[blk, blk]),
        compiler_params=_params("parallel", "parallel"))(core, parts, got)


W_IN_COLS = D_IN // N_DEV


def _w_in_from_blocks(w8):
    def kern(x_ref, o_ref):
        for j in range(N_DEV):
            o_ref[:, W_IN_COLS * j:W_IN_COLS * (j + 1)] = x_ref[j]
        o_ref[:, D_IN:] = jnp.zeros((ROW_TILE, D_INP - D_IN), o_ref.dtype)

    return pl.pallas_call(
        kern, name="w_in_from_blocks", grid=(D // ROW_TILE,),
        in_specs=[pl.BlockSpec((N_DEV, ROW_TILE, W_IN_COLS), lambda i: (0, i, 0))],
        out_specs=pl.BlockSpec((ROW_TILE, D_INP), lambda i: (i, 0)),
        out_shape=jax.ShapeDtypeStruct((D, D_INP), w8.dtype), compiler_params=_params("parallel"))(w8)


def _w_in_to_blocks(g):
    def kern(x_ref, o_ref):
        for j in range(N_DEV):
            o_ref[j] = x_ref[:, W_IN_COLS * j:W_IN_COLS * (j + 1)]

    return pl.pallas_call(
        kern, name="w_in_to_blocks", grid=(D // ROW_TILE,),
        in_specs=[pl.BlockSpec((ROW_TILE, D_INP), lambda i: (i, 0))],
        out_specs=pl.BlockSpec((N_DEV, ROW_TILE, W_IN_COLS), lambda i: (0, i, 0)),
        out_shape=jax.ShapeDtypeStruct((N_DEV, D, W_IN_COLS), g.dtype), compiler_params=_params("parallel"))(g)


def _pad_lanes(v, width=LANES):
    return jnp.pad(v, ((0, 0), (0, width - v.shape[1])))


def _row_layout(t16):
    return t16.T.reshape(N_HEADS // 2, 2, NB, BLK).transpose(0, 2, 1, 3)


def _row_layout_inv(r):
    return r.transpose(0, 2, 1, 3).reshape(N_HEADS, S).T


SMALL = (("g_mix", D), ("g_q", HEAD), ("g_k", HEAD), ("g_attn_out", D_ATTN), ("conv_b", D_CONV),
         ("dt_bias", 16), ("a_log", 16), ("d_skip", 16), ("g_ssm_out", D_SSM), ("g_cross", D),
         ("g_mem", D), ("g_cq", CROSS_HEAD), ("g_ck", CROSS_HEAD), ("g_mlp", D))
SMALL_ROWS = -(-sum(-(-w // LANES) for _, w in SMALL) // 8) * 8


def _pack_small(vals):
    rows = [_pad_lanes(vals[n], -(-w // LANES) * LANES).reshape(-1, LANES) for n, w in SMALL]
    packed = jnp.concatenate(rows, axis=0)
    return jnp.pad(packed, ((0, SMALL_ROWS - packed.shape[0]), (0, 0)))


def _unpack_small(packed):
    out, r = {}, 0
    for n, w in SMALL:
        nr = -(-w // LANES)
        out[n] = packed[r:r + nr].reshape(1, nr * LANES)[:, :w]
        r += nr
    return out


BIG = ("w_in", "w_out", "w_cq", "w_ckv", "w_co", "w_up", "w_down")
WEIGHTS = ("g_mix", "w_in", "g_q", "g_k", "g_attn_out", "conv_w", "conv_b", "dt_bias", "a_log", "d_skip",
           "g_ssm_out", "w_out", "g_cross", "g_mem", "w_cq", "w_ckv", "g_cq", "g_ck", "w_co", "g_mlp", "w_up", "w_down")


REST = ("w_out", "w_cq", "w_ckv", "w_co", "w_up", "w_down")


class _Overlap:
    def __init__(self, shards, after):
        self.gather, self.forward = {}, {}
        for tag, names in (("a", REST[:4]), ("b", REST[4:])):
            lands = [_put_own(lax.empty((N_DEV,) + shards[n].shape, BF), shards[n]) for n in names]
            self.gather[tag] = _exchange_start("gather", [shards[n] for n in names], lands, after, "ag_start_" + tag)
            after = self.gather[tag][3]
        self.token = after
        self.groups = []
        self.pairs = {}

    def rest_mid(self, tag, after):
        sems, srcs, lands, _ = self.gather[tag]
        srcs, lands = _exchange_wait("gather", sems, srcs, lands, after, "ag_wait_" + tag)
        self.forward[tag] = _exchange_start("forward", srcs, lands, None, "ag_fwd_start_" + tag)
        return self.forward[tag][3]

    def rest(self, tag, after):
        sems, srcs, lands, _ = self.forward[tag]
        _, lands = _exchange_wait("forward", sems, srcs, lands, after, "ag_fwd_wait_" + tag)
        wf = dict(zip(REST[:4] if tag == "a" else REST[4:], lands))
        for n in wf:
            if n in ("w_out", "w_cq", "w_ckv", "w_down"):
                wf[n] = wf[n].reshape(-1, wf[n].shape[-1])
        return wf

    def pair_start(self, name, grad):
        got = lax.empty((N_CHIP,) + grad.shape[1:], grad.dtype)
        self.pairs[name] = _exchange_start("pair", [grad], [got], None, "rs_pair_start_" + name)
        return self.pairs[name][3]

    def emit(self, grads, after):
        names, tag = list(grads), "_%d" % len(self.groups)
        grads, got = dict(grads), {}
        for n in names:
            if n in self.pairs:
                sems, srcs, lands, _ = self.pairs[n]
                srcs, lands = _exchange_wait("pair", sems, srcs, lands, after, "rs_pair_wait_" + n)
                grads[n], got[n] = srcs[0], lands[0]
        sync = [n for n in names if n not in got]
        if sync:
            got.update(zip(sync, _pair_exchange([grads[n] for n in sync], "rs_pair" + tag)))
        sums = [_pair_sum(grads[n], got[n], "rs_sum_" + n) for n in names]
        sems, srcs, lands, token = _exchange_start("scatter", [q for q, _ in sums], [l for _, l in sums], None,
                                                   "rs_chip_start" + tag)
        self.groups.append((names, sems, srcs, lands))
        return token

    def finish(self, gi, after):
        names, sems, srcs, lands = self.groups[gi]
        _, lands = _exchange_wait("scatter", sems, srcs, lands, after, "rs_chip_wait_%d" % gi)
        return dict(zip(names, lands))


def _tie(v, token):
    return v if token is None else v + token[0:1, 0:1]


def _local_step(x, mem, pos_col, target, p, wf, hooks=None):
    p = dict(p)
    inv = np.zeros((1, LANES), np.float32)
    freq = (ROPE_THETA ** (-2.0 * np.arange(ROT // 2, dtype=np.float32) / ROT)).astype(np.float32)
    for l in range(LANES):
        if l % HEAD < ROT:
            inv[0, l] = freq[(l % HEAD) % (ROT // 2)]
    inv_tab = jnp.asarray(inv)
    gq128, gk128 = jnp.tile(p["g_q"], (1, 2)), jnp.tile(p["g_k"], (1, 2))
    dtb128, alog128 = _pad_lanes(p["dt_bias"]), _pad_lanes(p["a_log"])
    dskip_b = jnp.repeat(p["d_skip"], HEAD, axis=1)
    conv_w, conv_b = wf["conv_w"], p["conv_b"]

    h = _rms_fwd(x, p["g_mix"], "rms_mix")
    proj = _matmul(h, wf["w_in"], mode="nn", name="mm_in", tm=1024, tn=896, tk=D)
    qr, kr = _qk_prep(proj, pos_col, gq128, gk128, inv_tab)
    dilations = (1, 4, 16)
    fwd = [_attn_fwd(qr, kr, proj, d, "attn_fwd_d%d" % d) for d in dilations]
    attn, lse, attn_n = _attn_merge([o for o, _ in fwd], [l for _, l in fwd], p["g_attn_out"])

    xbc = _conv_fwd(proj, conv_w, conv_b)
    dt, a_cs = _ssd_prep(proj, dtb128, alog128)
    a_b = _tie(jnp.repeat(a_cs[:, :N_HEADS], HEAD, axis=1), None if hooks is None else hooks.rest_mid("a", xbc))
    dt_b = jnp.repeat(dt[:, :N_HEADS], HEAD, axis=1)
    at_r, dt_r = _row_layout(a_cs[:, :N_HEADS]), _row_layout(dt[:, :N_HEADS])
    y_ssd, h_all = _ssd_fwd(xbc, a_b, dt_b, at_r, dt_r)
    ssm_n = _ssd_post(y_ssd, xbc, proj, dskip_b, p["g_ssm_out"])

    if hooks is not None:
        wf = {**wf, **hooks.rest("a", ssm_n)}
    mix = jnp.concatenate([attn_n, ssm_n], axis=1)
    x1 = _matmul(mix, wf["w_out"], mode="nn", name="mm_out", tm=1024, tn=1024, tk=D,
                 extras=(x,), epilogue=lambda acc, r: (acc + r,))
    hc = _rms_fwd(x1, _tie(p["g_cross"], None if hooks is None else hooks.rest_mid("b", x1)), "rms_cross")
    memh = _rms_fwd(mem, p["g_mem"], "rms_mem")
    qc = _matmul(hc, wf["w_cq"], mode="nn", name="mm_cq", tm=1024, tn=512, tk=D)
    kv = _matmul(memh, wf["w_ckv"], mode="nn", name="mm_ckv", tm=N_MEM, tn=1024, tk=D)
    oc = _cross_fwd(qc, kv, p["g_cq"], p["g_ck"])
    x2 = _matmul(oc, wf["w_co"], mode="nn", name="mm_co", tm=1024, tn=256, tk=512, b_cb=256,
                 extras=(x1,), epilogue=lambda acc, r: (acc + r,))
    hm = _rms_fwd(x2, p["g_mlp"], "rms_mlp")
    if hooks is not None:
        wf = {**wf, **hooks.rest("b", hm)}

    def up_epi(acc):
        ru = jnp.maximum(acc, 0.0)
        return ru * ru, 2.0 * ru

    act, relu2 = _matmul(hm, wf["w_up"], mode="nn", name="mm_up", tm=1024, tn=1024, tk=D, b_cb=1024,
                         out_dtypes=(BF, BF), epilogue=up_epi)

    def loss_epi(acc, r, t):
        d = (acc + r - t) * (1.0 / D)
        return d, d

    dy, dyb = _matmul(act, wf["w_down"], mode="nn", name="mm_down", tm=512, tn=1024, tk=2048, extras=(x2, target),
                      out_dtypes=(F32, BF), epilogue=loss_epi)
    loss_part = _loss_sum(dy)[0, 0] * (0.5 * D)

    gb, gs = {}, {}
    gb["w_down"] = _matmul(act, dyb, mode="tn", name="mm_down_dw", tm=1024, tn=1024, tk=S,
                           out_dtypes=(BF,)).reshape(N_DEV, D_FF // N_DEV, D)
    token = None if hooks is None else hooks.pair_start("w_down", gb["w_down"])
    du = _matmul(dyb, wf["w_down"], mode="nt", name="mm_down_dx", tm=1024, tn=1024, tk=D, extras=(relu2,),
                 out_dtypes=(BF,), epilogue=lambda acc, r: (acc * r.astype(F32),), after=token)
    gb["w_up"] = _matmul(hm, du, mode="tn", name="mm_up_dw", tm=1024, tn=1024, tk=S, out_dtypes=(BF,), out_cb=1024)
    token = None if hooks is None else hooks.pair_start("w_up", gb["w_up"])
    dhm = _matmul(du, wf["w_up"], mode="nt", name="mm_up_dx", tm=1024, tn=1024, tk=1024, b_cb=1024, after=token)
    if hooks is not None:
        p["g_mlp"] = _tie(p["g_mlp"], hooks.emit({n: gb[n] for n in ("w_down", "w_up")}, dhm))
    dx2, dx2b, gs["g_mlp"] = _rms_bwd_call(x2, p["g_mlp"], dhm, dy, "rms_mlp_bwd")

    doc = _matmul(dx2b, wf["w_co"], mode="nt", name="mm_co_dx", tm=1024, tn=512, tk=256, b_cb=256)
    gb["w_co"] = _matmul(oc, dx2b, mode="tn", name="mm_co_dw", tm=512, tn=256, tk=S, out_dtypes=(BF,), out_cb=256)
    dqc, dkn, dvc, gs["g_cq"] = _cross_bwd(qc, doc, kv, p["g_cq"], p["g_ck"])
    dkv, gs["g_ck"] = _cross_kv_bwd(kv, dkn, dvc, p["g_ck"])
    gb["w_cq"] = _matmul(hc, dqc, mode="tn", name="mm_cq_dw", tm=1024, tn=512, tk=S,
                         out_dtypes=(BF,)).reshape(N_DEV, D // N_DEV, D_CROSS)
    dhc = _matmul(dqc, wf["w_cq"], mode="nt", name="mm_cq_dx", tm=1024, tn=1024, tk=512)
    gb["w_ckv"] = _matmul(memh, dkv, mode="tn", name="mm_ckv_dw", tm=1024, tn=1024, tk=N_MEM,
                          out_dtypes=(BF,)).reshape(N_DEV, D // N_DEV, 2 * D_CROSS)
    dmemh = _matmul(dkv, wf["w_ckv"], mode="nt", name="mm_ckv_dx", tm=N_MEM, tn=1024, tk=1024)
    (gs["g_mem"],) = _rms_bwd_call(mem, p["g_mem"], dmemh, None, "rms_mem_bwd")
    dx1, dx1b, gs["g_cross"] = _rms_bwd_call(x1, p["g_cross"], dhc, dx2, "rms_cross_bwd")

    dmix = _matmul(dx1b, wf["w_out"], mode="nt", name="mm_out_dx", tm=1024, tn=1024, tk=D)
    gb["w_out"] = _matmul(mix, dx1b, mode="tn", name="mm_out_dw", tm=1024, tn=1024, tk=S,
                          out_dtypes=(BF,)).reshape(N_DEV, D // N_DEV, D)
    if hooks is not None:
        p["g_attn_out"] = _tie(p["g_attn_out"],
                               hooks.emit({n: gb[n] for n in ("w_co", "w_cq", "w_ckv", "w_out")}, dmix))

    dattn, delta, gs["g_attn_out"] = _attn_merge_bwd(dmix, attn, p["g_attn_out"])
    bwd = [_attn_bwd(qr, kr, proj, dattn, lse, delta, d, "attn_bwd_d%d" % d) for d in dilations]
    dq_pre, dk_pre, dv_pre, dgq, dgk = _qk_bwd([t[0] for t in bwd], [t[1] for t in bwd], [t[2] for t in bwd],
                                               proj, pos_col, gq128, gk128, inv_tab)
    gs["g_q"], gs["g_k"] = dgq[:, :HEAD], dgk[:, :HEAD]

    dy_ssd, dz, dxs_skip, dd_lane, gs["g_ssm_out"] = _ssd_post_bwd(y_ssd, xbc, proj, dmix, dskip_b, p["g_ssm_out"])
    gs["d_skip"] = dd_lane.reshape(N_HEADS, HEAD).sum(axis=1).reshape(1, N_HEADS)
    dc_pair, daq_b, dx_ssd, db_pair, dak_r, ddt_r, ddt_b = _ssd_bwd(xbc, dy_ssd, h_all, a_b, dt_b, at_r, dt_r)
    daq = _pad_lanes(daq_b[:, ::HEAD])
    dak = _pad_lanes(_row_layout_inv(dak_r))
    ddt_direct = _pad_lanes(_row_layout_inv(ddt_r))
    ddt_raw, dalog, dbias = _ssd_prep_bwd(daq, dak, ddt_direct, _pad_lanes(ddt_b[:, ::HEAD]), dt, proj,
                                          dtb128, alog128)
    gs["a_log"], gs["dt_bias"] = dalog[:, :N_HEADS], dbias[:, :N_HEADS]
    same = lambda c: c
    dxbc_x, dcw_x, dcb_x = _conv_bwd(proj, conv_w, conv_b, dxs_skip, dx_ssd, same, same, 0, 8, "conv_bwd_x")
    dxbc_b, dcw_b, dcb_b = _conv_bwd(proj, conv_w, conv_b, db_pair, db_pair, lambda c: 2 * c, lambda c: 2 * c + 1,
                                     D_SSM, 4, "conv_bwd_b")
    dxbc_c, dcw_c, dcb_c = _conv_bwd(proj, conv_w, conv_b, dc_pair, dc_pair, lambda c: 2 * c, lambda c: 2 * c + 1,
                                     D_SSM + 512, 4, "conv_bwd_c")
    g_conv_w = jnp.concatenate([dcw_x, dcw_b, dcw_c], axis=1)
    gs["conv_b"] = jnp.concatenate([dcb_x, dcb_b, dcb_c], axis=1)

    dproj = jnp.concatenate([dq_pre, dk_pre, dv_pre, dz, dxbc_x, dxbc_b, dxbc_c, ddt_raw], axis=1)
    dw_in = _matmul(h, dproj, mode="tn", name="mm_in_dw", tm=1024, tn=896, tk=S, out_dtypes=(BF,))
    gb["w_in"] = _w_in_to_blocks(dw_in)
    token = None if hooks is None else hooks.emit({"w_in": gb["w_in"]}, dw_in)
    dh = _matmul(dproj, wf["w_in"], mode="nt", name="mm_in_dx", tm=1024, tn=1024, tk=896, after=token)
    grad_x, _, gs["g_mix"] = _rms_bwd_call(x, p["g_mix"], dh, dx1, "rms_mix_bwd")
    return loss_part, grad_x, gb, gs, g_conv_w


def kernel(x, mem, positions, g_mix, w_in, g_q, g_k, g_attn_out, conv_w, conv_b, dt_bias, a_log, d_skip, g_ssm_out, w_out, g_cross, g_mem, w_cq, w_ckv, g_cq, g_ck, w_co, g_mlp, w_up, w_down, loss_target, m_g_mix, m_w_in, m_g_q, m_g_k, m_g_attn_out, m_conv_w, m_conv_b, m_dt_bias, m_a_log, m_d_skip, m_g_ssm_out, m_w_out, m_g_cross, m_g_mem, m_w_cq, m_w_ckv, m_g_cq, m_g_ck, m_w_co, m_g_mlp, m_w_up, m_w_down, v_g_mix, v_w_in, v_g_q, v_g_k, v_g_attn_out, v_conv_w, v_conv_b, v_dt_bias, v_a_log, v_d_skip, v_g_ssm_out, v_w_out, v_g_cross, v_g_mem, v_w_cq, v_w_ckv, v_g_cq, v_g_ck, v_w_co, v_g_mlp, v_w_up, v_w_down):
    args = dict(locals())
    w = {n: args[n] for n in WEIGHTS}
    m = {n: args["m_" + n] for n in WEIGHTS}
    v = {n: args["v_" + n] for n in WEIGHTS}
    small = {n: w[n] for n, _ in SMALL}
    me = 4 * lax.axis_index("x") + 2 * lax.axis_index("y") + lax.axis_index("c")

    w_in_g, conv_w_g = _all_gather([w["w_in"][0].astype(BF), w["conv_w"][0]], "ag_first")
    wf = {"w_in": _w_in_from_blocks(w_in_g), "conv_w": conv_w_g.transpose(1, 0, 2).reshape(CONV_W, D_CONV)}
    overlap = _Overlap({n: w[n][0].astype(BF) for n in REST}, w_in_g)
    p = dict(small)
    p["g_mix"] = _tie(p["g_mix"], overlap.token)

    loss_part, grad_x, _, gs, g_conv_w = _local_step(
        x[0], mem[0], positions.reshape(S, 1), loss_target[0], p, wf, overlap)
    loss = lax.psum(loss_part, ("x", "y", "c"))

    out = {}
    last = grad_x
    for gi in range(3):
        for n, parts in overlap.finish(gi, last).items():
            res_n = _adamw(w[n][0], m[n][0], v[n][0], parts, "adamw_" + n)
            out[n] = [t.reshape(w[n].shape) for t in res_n]
            last = res_n[0]

    sm_parts, cw_parts = _all_gather([_pack_small(gs), g_conv_w], "ag_small_grads", after=last)
    sm = [_unpack_small(t) for t in _adamw(_pack_small(small), _pack_small({n: m[n] for n, _ in SMALL}),
                                           _pack_small({n: v[n] for n, _ in SMALL}), sm_parts, "adamw_small")]
    for n, _ in SMALL:
        out[n] = [t[n] for t in sm]
    cols = D_CONV // N_DEV
    cw_mine = lax.dynamic_slice_in_dim(cw_parts, me * cols, cols, axis=2)
    out["conv_w"] = [t.reshape(w["conv_w"].shape) for t in
                     _adamw(w["conv_w"][0], m["conv_w"][0], v["conv_w"][0], cw_mine, "adamw_conv_w")]

    res = [loss, grad_x.reshape(x.shape)]
    for k in range(4):
        res += [out[n][k] for n in WEIGHTS]
    return tuple(res)
```

```python
import functools
import math

import numpy as np
import jax
import jax.numpy as jnp
from jax import lax
from jax.experimental import pallas as pl
from jax.experimental.pallas import tpu as pltpu

F32 = jnp.float32
BF = jnp.bfloat16

N_DEV = 8
S = 2048
D = 2048
D_ATTN = 1024
N_HEADS = 16
HEAD = 64
ROT = 16
ROPE_THETA = 500000.0
D_SSM = 1024
D_CONV = 2048
CONV_W = 4
N_MEM = 256
D_CROSS = 512
CROSS_HEAD = 128
D_FF = 8192
D_IN = 6160
D_INP = 6400
DT_COLBLK = (4 * D_ATTN + D_CONV) // 128
EPS = 1e-6
BLK = 128
NB = S // BLK
LANES = 128
NEG = -1e30

ADAM_LR = 0.001
ADAM_B1 = 0.9
ADAM_B2 = 0.999
ADAM_EPS = 1e-08
ADAM_WD = 0.01
ADAM_STEP = 10

VMEM_LIMIT_BYTES = 48 * 1024 * 1024
ROW_TILE = 256


def _params(*sem):
    return pltpu.CompilerParams(dimension_semantics=sem, vmem_limit_bytes=VMEM_LIMIT_BYTES)


def _matmul(a, b, *, mode, name, tm, tn, tk, out_dtypes=(F32,), b_cb=None, out_cb=None,
            extras=(), epilogue=None, after=None):
    if mode == "tn":
        kk, m = a.shape
    else:
        m, kk = a.shape
    if b_cb is None:
        br, bc = b.shape
    else:
        br, bc = b.shape[1], N_DEV * b_cb
    n = br if mode == "nt" else bc
    assert m % tm == 0 and n % tn == 0 and kk % tk == 0, (name, m, n, kk)
    nk = kk // tk
    grid = (m // tm, n // tn, nk)

    if mode == "tn":
        a_spec = pl.BlockSpec((tk, tm), lambda i, j, k: (k, i))
    else:
        a_spec = pl.BlockSpec((tm, tk), lambda i, j, k: (i, k))
    if mode == "nt":
        b_blk, b_idx = (tn, tk), (lambda i, j, k: (j, k))
    else:
        b_blk, b_idx = (tk, tn), (lambda i, j, k: (k, j))
    if b_cb is None:
        b_spec = pl.BlockSpec(b_blk, b_idx)
    else:
        tc = b_blk[1]
        assert b_cb % tc == 0
        per = b_cb // tc

        def b_idx3(i, j, k):
            r, c = b_idx(i, j, k)
            return (c // per, r, c % per)
        b_spec = pl.BlockSpec((None,) + b_blk, b_idx3)
    ex_specs = [pl.BlockSpec((tm, tn), lambda i, j, k: (i, j)) for _ in extras]
    if out_cb is None:
        out_shape = [jax.ShapeDtypeStruct((m, n), dt) for dt in out_dtypes]
        out_specs = [pl.BlockSpec((tm, tn), lambda i, j, k: (i, j)) for _ in out_dtypes]
    else:
        assert out_cb % tn == 0
        pero = out_cb // tn
        out_shape = [jax.ShapeDtypeStruct((N_DEV, m, out_cb), dt) for dt in out_dtypes]
        out_specs = [pl.BlockSpec((None, tm, tn), lambda i, j, k: (j // pero, i, j % pero)) for _ in out_dtypes]
    dn = {"nn": (((1,), (0,)), ((), ())), "nt": (((1,), (1,)), ((), ())), "tn": (((0,), (0,)), ((), ()))}[mode]
    ne, no = len(extras), len(out_dtypes)
    n_after = 0 if after is None else 1

    def kern(a_ref, b_ref, *rest):
        ex_refs, out_refs = rest[:ne], rest[ne + n_after:ne + n_after + no]

        def finish(acc):
            outs = (acc,) if epilogue is None else epilogue(acc, *[r[...] for r in ex_refs])
            for r, o in zip(out_refs, outs):
                r[...] = o.astype(r.dtype)

        part = lax.dot_general(a_ref[...].astype(BF), b_ref[...].astype(BF), dn, preferred_element_type=F32)
        if nk == 1:
            finish(part)
            return
        acc_ref = rest[ne + n_after + no]
        k = pl.program_id(2)

        @pl.when(k == 0)
        def _():
            acc_ref[...] = part

        @pl.when(k > 0)
        def _():
            acc_ref[...] += part

        @pl.when(k == nk - 1)
        def _():
            finish(acc_ref[...])

    res = pl.pallas_call(
        kern, name=name, grid=grid, in_specs=[a_spec, b_spec] + ex_specs + [ANY] * n_after, out_specs=out_specs,
        out_shape=out_shape, scratch_shapes=[pltpu.VMEM((tm, tn), F32)] if nk > 1 else [],
        compiler_params=_params("parallel", "parallel", "arbitrary"),
    )(a, b, *extras, *([] if after is None else [after]))
    return res[0] if no == 1 else tuple(res)


def _rowwise(body, *, name, nrows, tile, row_ins, full_ins=(), row_outs=(), acc_outs=(), scratch=(),
             reverse=False):
    n = nrows // tile

    def ridx(i):
        return (n - 1 - i) if reverse else i

    in_specs, args = [], []
    for arr, width, cb in row_ins:
        in_specs.append(pl.BlockSpec((tile, width), lambda i, cb=cb: (ridx(i), cb)))
        args.append(arr)
    for arr in full_ins:
        in_specs.append(pl.BlockSpec(arr.shape, lambda i, nd=arr.ndim: (0,) * nd))
        args.append(arr)
    out_shape, out_specs = [], []
    for width, dt in row_outs:
        out_shape.append(jax.ShapeDtypeStruct((nrows, width), dt))
        out_specs.append(pl.BlockSpec((tile, width), lambda i: (ridx(i), 0)))
    for shp, dt in acc_outs:
        out_shape.append(jax.ShapeDtypeStruct(shp, dt))
        out_specs.append(pl.BlockSpec(shp, lambda i, nd=len(shp): (0,) * nd))

    def kern(*refs):
        body(pl.program_id(0), n, *refs)

    return pl.pallas_call(kern, name=name, grid=(n,), in_specs=in_specs, out_specs=out_specs,
                          out_shape=out_shape, scratch_shapes=list(scratch),
                          compiler_params=_params("arbitrary"))(*args)


def _accum(ref, val, i):
    @pl.when(i == 0)
    def _():
        ref[...] = val

    @pl.when(i > 0)
    def _():
        ref[...] += val


def _sigmoid(x):
    return 1.0 / (1.0 + jnp.exp(-x))


def _rms_n(x):
    r = lax.rsqrt(jnp.mean(x * x, axis=-1, keepdims=True) + EPS)
    return x * r, r


def _rms_bwd(x, g, dh):
    n, r = _rms_n(x)
    dn = dh * g
    dx = r * (dn - n * jnp.mean(dn * n, axis=-1, keepdims=True))
    return dx, jnp.sum(dh * n, axis=0, keepdims=True)


def _seg_sum(x, seg):
    t, w = x.shape
    tiles = [x[:, LANES * j:LANES * (j + 1)] for j in range(w // LANES)]
    outs = []
    if seg == 64:
        lo = lax.broadcasted_iota(jnp.int32, (t, LANES), 1) < 64
        for xt in tiles:
            s_lo = jnp.sum(jnp.where(lo, xt, 0.0), axis=-1, keepdims=True)
            s_hi = jnp.sum(jnp.where(lo, 0.0, xt), axis=-1, keepdims=True)
            outs.append(jnp.where(lo, s_lo, s_hi))
    else:
        sums = [jnp.sum(xt, axis=-1, keepdims=True) for xt in tiles]
        if seg == 256:
            sums = [sums[2 * (j // 2)] + sums[2 * (j // 2) + 1] for j in range(len(sums))]
        else:
            assert seg == 128
        outs = [jnp.broadcast_to(s, (t, LANES)) for s in sums]
    return outs[0] if len(outs) == 1 else jnp.concatenate(outs, axis=1)


def _seg_rms_n(x, seg):
    r = lax.rsqrt(_seg_sum(x * x, seg) * (1.0 / seg) + EPS)
    return x * r, r


def _seg_rms_bwd(x, g, dy, seg):
    n, r = _seg_rms_n(x, seg)
    dn = dy * g
    dx = r * (dn - n * (_seg_sum(dn * n, seg) * (1.0 / seg)))
    return dx, jnp.sum(dy * n, axis=0, keepdims=True)


def _rope_tables(pos_col, inv_tab, t):
    ang = pos_col.astype(F32) * inv_tab
    idx = lax.broadcasted_iota(jnp.int32, (t, LANES), 1) % HEAD
    cos, sin = jnp.cos(ang), jnp.sin(ang)
    c = jnp.where(idx < ROT, cos, 1.0)
    sg = jnp.where(idx < ROT // 2, -sin, jnp.where(idx < ROT, sin, 0.0))
    return c, sg, idx < ROT // 2


def _rope(x, c, sg, lo):
    partner = jnp.where(lo, pltpu.roll(x, LANES - ROT // 2, 1), pltpu.roll(x, ROT // 2, 1))
    return x * c + partner * sg


def _fold_heads(v):
    v8 = jnp.broadcast_to(v, (8, LANES))
    return (v8 + pltpu.roll(v8, HEAD, 1))[0:1]


def _dot(a, b, dn):
    return lax.dot_general(a, b, (dn, ((), ())), preferred_element_type=F32)


NN = ((1,), (0,))
NT = ((1,), (1,))
TN = ((0,), (0,))


def _dot3(l01, x):
    x1 = x.astype(BF)
    r1 = x - x1.astype(F32)
    x2 = r1.astype(BF)
    x3 = (r1 - x2.astype(F32)).astype(BF)
    return _dot(l01, x1, NN) + _dot(l01, x2, NN) + _dot(l01, x3, NN)


def _rms_fwd(x, g, name):
    rows = x.shape[0]

    def body(i, n, x_ref, g_ref, o_ref):
        nx, _ = _rms_n(x_ref[...])
        o_ref[...] = (nx * g_ref[...]).astype(BF)

    return _rowwise(body, name=name, nrows=rows, tile=min(ROW_TILE, rows), row_ins=[(x, D, 0)],
                    full_ins=[g], row_outs=[(D, BF)])[0]


def _qk_prep(proj, pos_col, gq128, gk128, inv_tab):
    scale = HEAD ** -0.5

    def body(i, n, q_ref, k_ref, p_ref, gq_ref, gk_ref, it_ref, qo_ref, ko_ref):
        t = q_ref.shape[0]
        c, sg, lo = _rope_tables(p_ref[...], it_ref[...], t)
        for j in range(D_ATTN // LANES):
            sl = slice(LANES * j, LANES * (j + 1))
            qn, _ = _seg_rms_n(q_ref[:, sl], HEAD)
            kn, _ = _seg_rms_n(k_ref[:, sl], HEAD)
            qo_ref[:, sl] = _rope(qn * gq_ref[...], c, sg, lo) * scale
            ko_ref[:, sl] = _rope(kn * gk_ref[...], c, sg, lo)

    return _rowwise(body, name="qk_prep", nrows=S, tile=ROW_TILE,
                    row_ins=[(proj, D_ATTN, 0), (proj, D_ATTN, 1), (pos_col, 1, 0)],
                    full_ins=[gq128, gk128, inv_tab], row_outs=[(D_ATTN, F32)] * 2)


ATTN_QB = 4
V_COLS = pl.BlockSpec((S, LANES), lambda p: (0, 2 * D_ATTN // LANES + p))


def _band(b, d):
    nb = NB // d
    r, n = b // nb, b % nb
    width, kn = (BLK, n) if nb == 1 else (2 * BLK, jnp.maximum(n - 1, 0))

    def rows(first_member, count):
        if d == 1:
            return pl.ds(pl.multiple_of(first_member, BLK), count)
        return pl.ds(first_member * d + r, count, stride=d)

    qm = n * BLK + (lax.broadcasted_iota(jnp.int32, (2 * BLK, width), 0) & (BLK - 1))
    km = kn * BLK + lax.broadcasted_iota(jnp.int32, (2 * BLK, width), 1)
    valid = km <= qm
    if nb > 1:
        valid = valid & (km >= qm - BLK)
    return rows(n * BLK, BLK), rows(kn * BLK, width), valid


def _attn_fwd(q, k, v, d, name):
    def kern(q_ref, k_ref, v_ref, o_ref, l_ref):
        half0 = lax.broadcasted_iota(jnp.int32, (BLK, LANES), 1) < HEAD

        def group(g, carry):
            for bb in range(ATTN_QB):
                q_rows, k_rows, valid = _band(g * ATTN_QB + bb, d)
                q = q_ref[q_rows, :]
                kb, vb = k_ref[k_rows, :].astype(BF), v_ref[k_rows, :].astype(BF)
                q2 = jnp.concatenate([jnp.where(half0, q, 0.0), jnp.where(half0, 0.0, q)], axis=0).astype(BF)
                s = jnp.where(valid, _dot(q2, kb, NT), NEG)
                m = jnp.max(s, axis=-1, keepdims=True)
                p = jnp.exp(s - m)
                den = jnp.sum(p, axis=-1, keepdims=True)
                o2 = _dot(p.astype(BF), vb, NN) / den
                l2 = m + jnp.log(den)
                o_ref[q_rows, :] = jnp.where(half0, o2[:BLK], o2[BLK:])
                l_ref[q_rows, :] = jnp.where(half0, l2[:BLK], l2[BLK:])
            return carry

        lax.fori_loop(0, NB // ATTN_QB, group, 0)

    seq = pl.BlockSpec((S, LANES), lambda p: (0, p))
    return pl.pallas_call(
        kern, name=name, grid=(D_ATTN // LANES,), in_specs=[seq, seq, V_COLS], out_specs=[seq, seq],
        out_shape=[jax.ShapeDtypeStruct((S, D_ATTN), F32)] * 2,
        compiler_params=_params("parallel"))(q, k, v)


def _attn_merge(os_, ls_, g_attn):
    def body(i, n, o1, o2, o3, l1, l2, l3, g_ref, a_ref, lse_ref, an_ref):
        la, lb, lc = l1[...], l2[...], l3[...]
        m = jnp.maximum(jnp.maximum(la, lb), lc)
        ea, eb, ec = jnp.exp(la - m), jnp.exp(lb - m), jnp.exp(lc - m)
        den = ea + eb + ec
        attn = (ea * o1[...] + eb * o2[...] + ec * o3[...]) / den
        a_ref[...] = attn
        lse_ref[...] = m + jnp.log(den)
        nx, _ = _rms_n(attn)
        an_ref[...] = (nx * g_ref[...]).astype(BF)

    return _rowwise(body, name="attn_merge", nrows=S, tile=ROW_TILE,
                    row_ins=[(a, D_ATTN, 0) for a in (*os_, *ls_)], full_ins=[g_attn],
                    row_outs=[(D_ATTN, F32), (D_ATTN, F32), (D_ATTN, BF)])


def _conv_taps(x, w_ref):
    rows = lax.broadcasted_iota(jnp.int32, x.shape, 0)
    shifted = []
    for w in range(CONV_W):
        k = CONV_W - 1 - w
        shifted.append(x if k == 0 else jnp.where(rows >= k, pltpu.roll(x, k, 0), 0.0))
    acc = shifted[0] * w_ref[0:1, :]
    for w in range(1, CONV_W):
        acc = acc + shifted[w] * w_ref[w:w + 1, :]
    return acc, shifted


def _conv_fwd(proj, conv_w, conv_b):
    tc = 256

    def kern(x_ref, w_ref, b_ref, o_ref):
        c, _ = _conv_taps(x_ref[...], w_ref)
        c = c + b_ref[...]
        o_ref[...] = c * _sigmoid(c)

    return pl.pallas_call(
        kern, name="conv_fwd", grid=(D_CONV // tc,),
        in_specs=[pl.BlockSpec((S, tc), lambda c: (0, 4 * D_ATTN // tc + c)),
                  pl.BlockSpec((CONV_W, tc), lambda c: (0, c)), pl.BlockSpec((1, tc), lambda c: (0, c))],
        out_specs=pl.BlockSpec((S, tc), lambda c: (0, c)),
        out_shape=jax.ShapeDtypeStruct((S, D_CONV), F32), compiler_params=_params("parallel"))(proj, conv_w, conv_b)


def _softplus(x):
    return jnp.maximum(x, 0.0) + jnp.log1p(jnp.exp(-jnp.abs(x)))


def _ssd_prep(proj, dt_bias128, a_log128):
    def body(i, n, raw_ref, b_ref, al_ref, dt_ref, a_ref, carry):
        @pl.when(i == 0)
        def _():
            carry[...] = jnp.zeros_like(carry)

        dt = _softplus(raw_ref[...] + b_ref[...])
        da = dt * (-jnp.exp(al_ref[...]))
        tri = (lax.broadcasted_iota(jnp.int32, (BLK, BLK), 0) >= lax.broadcasted_iota(jnp.int32, (BLK, BLK), 1))
        cs = _dot3(tri.astype(BF), da) + carry[0:1, :]
        dt_ref[...] = dt
        a_ref[...] = cs
        carry[...] = jnp.broadcast_to(cs[BLK - 1:BLK, :], carry.shape)

    return _rowwise(body, name="ssd_prep", nrows=S, tile=BLK, row_ins=[(proj, LANES, DT_COLBLK)],
                    full_ins=[dt_bias128, a_log128], row_outs=[(LANES, F32), (LANES, F32)],
                    scratch=[pltpu.VMEM((8, LANES), F32)])


def _ssd_decay(a_col, at_row, causal):
    diff = jnp.where(causal, a_col - at_row, 0.0)
    return jnp.where(causal, jnp.exp(diff), 0.0)


SSD_CB = 2


def _ssd_fwd(xbc, a_b, dt_b, at_r, dt_r):
    def kern(c_ref, b_ref, x_ref, ab_ref, dtb_ref, at_ref, dt_ref, y_ref, hall_ref):
        half0 = lax.broadcasted_iota(jnp.int32, (BLK, LANES), 1) < HEAD
        causal = lax.broadcasted_iota(jnp.int32, (BLK, BLK), 1) <= lax.broadcasted_iota(jnp.int32, (BLK, BLK), 0)

        def step(i, carry):
            h, a_prev = carry
            for cc in range(SSD_CB):
                chunk = i * SSD_CB + cc
                rows = pl.ds(pl.multiple_of(chunk * BLK, BLK), BLK)
                ci, bi, x = c_ref[rows, :].astype(BF), b_ref[rows, :].astype(BF), x_ref[rows, :]
                ab = ab_ref[rows, :]
                a_end = ab[BLK - 1:BLK, :]
                alpha = jnp.exp(ab - a_prev)
                beta = jnp.exp(a_end - ab) * dtb_ref[rows, :]
                hall_ref[rows, :] = h
                cb = _dot(ci, bi, NT)
                at, dtj = at_ref[chunk], dt_ref[chunk]
                y = alpha * _dot(ci, h.astype(BF), NN)
                for hd in range(2):
                    hm = half0 if hd == 0 else jnp.logical_not(half0)
                    w = cb * _ssd_decay(ab[:, HEAD * hd:HEAD * hd + 1], at[hd:hd + 1, :], causal) * dtj[hd:hd + 1, :]
                    y = y + _dot(w.astype(BF), jnp.where(hm, x, 0.0).astype(BF), NN)
                y_ref[rows, :] = y
                h = alpha[BLK - 1:BLK, :] * h + _dot(bi, (beta * x).astype(BF), TN)
                a_prev = a_end
            return h, a_prev

        lax.fori_loop(0, NB // SSD_CB, step, (jnp.zeros((BLK, LANES), F32), jnp.zeros((1, LANES), F32)))

    npair = D_SSM // LANES
    rowvec = pl.BlockSpec((None, NB, 2, BLK), lambda p: (p, 0, 0, 0))
    seq = pl.BlockSpec((S, LANES), lambda p: (0, p))
    return pl.pallas_call(
        kern, name="ssd_fwd", grid=(npair,),
        in_specs=[pl.BlockSpec((S, LANES), lambda p: (0, 12 + p // 2)),
                  pl.BlockSpec((S, LANES), lambda p: (0, 8 + p // 2)), seq, seq, seq, rowvec, rowvec],
        out_specs=[seq, seq], out_shape=[jax.ShapeDtypeStruct((S, D_SSM), F32)] * 2,
        compiler_params=_params("parallel"))(xbc, xbc, xbc, a_b, dt_b, at_r, dt_r)


def _ssd_post(y_ssd, xbc, proj, dskip_b, g_ssm):
    def body(i, n, y_ref, xs_ref, z_ref, d_ref, g_ref, o_ref):
        z = z_ref[...]
        y2 = (y_ref[...] + d_ref[...] * xs_ref[...]) * (z * _sigmoid(z))
        nx, _ = _seg_rms_n(y2, 256)
        o_ref[...] = (nx * g_ref[...]).astype(BF)

    return _rowwise(body, name="ssd_post", nrows=S, tile=ROW_TILE,
                    row_ins=[(y_ssd, D_SSM, 0), (xbc, D_SSM, 0), (proj, D_SSM, 3)], full_ins=[dskip_b, g_ssm],
                    row_outs=[(D_SSM, BF)])[0]


def _cross_heads(q, kv_ref, gq, gk):
    out = []
    for h in range(D_CROSS // CROSS_HEAD):
        sl = slice(CROSS_HEAD * h, CROSS_HEAD * (h + 1))
        nq, rq = _rms_n(q[:, sl])
        nk, rk = _rms_n(kv_ref[:, sl])
        v = kv_ref[:, D_CROSS + CROSS_HEAD * h:D_CROSS + CROSS_HEAD * (h + 1)]
        out.append((sl, nq, rq, nk, rk, v))
    return out


def _cross_fwd(qc, kv, g_cq, g_ck):
    scale = CROSS_HEAD ** -0.5

    def body(i, n, q_ref, kv_ref, gq_ref, gk_ref, o_ref):
        for sl, nq, _, nk, _, v in _cross_heads(q_ref[...], kv_ref, gq_ref[...], gk_ref[...]):
            qn = (nq * gq_ref[...] * scale).astype(BF)
            kn = (nk * gk_ref[...]).astype(BF)
            s = _dot(qn, kn, NT)
            e = jnp.exp(s - jnp.max(s, axis=-1, keepdims=True))
            p = e / jnp.sum(e, axis=-1, keepdims=True)
            o_ref[:, sl] = _dot(p.astype(BF), v.astype(BF), NN).astype(BF)

    return _rowwise(body, name="cross_fwd", nrows=S, tile=ROW_TILE, row_ins=[(qc, D_CROSS, 0)],
                    full_ins=[kv, g_cq, g_ck], row_outs=[(D_CROSS, BF)])[0]


def _loss_sum(dy):
    def body(i, n, d_ref, o_ref):
        d = d_ref[...]
        s = jnp.sum(jnp.sum(d * d, axis=0, keepdims=True), axis=1, keepdims=True)
        _accum(o_ref, s, i)

    return _rowwise(body, name="loss_sum", nrows=S, tile=ROW_TILE, row_ins=[(dy, D, 0)],
                    acc_outs=[((1, 1), F32)])[0]


def _rms_bwd_call(x, g, dh, dres, name):
    rows = x.shape[0]
    has_res = dres is not None

    def body(i, n, *refs):
        if has_res:
            x_ref, dh_ref, dr_ref, g_ref, dx_ref, dxb_ref, dg_ref = refs
        else:
            x_ref, dh_ref, g_ref, dg_ref = refs
        dx, dg = _rms_bwd(x_ref[...], g_ref[...], dh_ref[...])
        if has_res:
            dx = dx + dr_ref[...]
            dx_ref[...] = dx
            dxb_ref[...] = dx.astype(BF)
        _accum(dg_ref, dg, i)

    row_ins = [(x, D, 0), (dh, D, 0)] + ([(dres, D, 0)] if has_res else [])
    return _rowwise(body, name=name, nrows=rows, tile=min(ROW_TILE, rows), row_ins=row_ins, full_ins=[g],
                    row_outs=[(D, F32), (D, BF)] if has_res else [], acc_outs=[((1, D), F32)])


def _cross_bwd(qc, doc, kv, g_cq, g_ck):
    scale = CROSS_HEAD ** -0.5

    def body(i, n, q_ref, do_ref, kv_ref, gq_ref, gk_ref, dq_ref, dkn_ref, dv_ref, dgq_ref):
        dgq = jnp.zeros((1, CROSS_HEAD), F32)
        dkn_parts, dv_parts = [], []
        for sl, nq, rq, nk, _, v in _cross_heads(q_ref[...], kv_ref, gq_ref[...], gk_ref[...]):
            qn = (nq * gq_ref[...] * scale).astype(BF)
            kn = (nk * gk_ref[...]).astype(BF)
            s = _dot(qn, kn, NT)
            e = jnp.exp(s - jnp.max(s, axis=-1, keepdims=True))
            p = e / jnp.sum(e, axis=-1, keepdims=True)
            do = do_ref[:, sl].astype(BF)
            dv_parts.append(_dot(p.astype(BF), do, TN))
            dp = _dot(do, v.astype(BF), NT)
            ds = (p * (dp - jnp.sum(dp * p, axis=-1, keepdims=True))).astype(BF)
            dqn = _dot(ds, kn, NN)
            dkn_parts.append(_dot(ds, qn, TN))
            dn = dqn * (gq_ref[...] * scale)
            dq_ref[:, sl] = (rq * (dn - nq * jnp.mean(dn * nq, axis=-1, keepdims=True))).astype(BF)
            dgq = dgq + jnp.sum(dqn * scale * nq, axis=0, keepdims=True)
        _accum(dkn_ref, jnp.concatenate(dkn_parts, axis=1), i)
        _accum(dv_ref, jnp.concatenate(dv_parts, axis=1), i)
        _accum(dgq_ref, dgq, i)

    return _rowwise(body, name="cross_bwd", nrows=S, tile=ROW_TILE, row_ins=[(qc, D_CROSS, 0), (doc, D_CROSS, 0)],
                    full_ins=[kv, g_cq, g_ck], row_outs=[(D_CROSS, BF)],
                    acc_outs=[((N_MEM, D_CROSS), F32), ((N_MEM, D_CROSS), F32), ((1, CROSS_HEAD), F32)])


def _cross_kv_bwd(kv, dkn, dv, g_ck):
    def body(i, n, kv_ref, dkn_ref, dv_ref, gk_ref, dkv_ref, dgk_ref):
        dgk = jnp.zeros((1, CROSS_HEAD), F32)
        for h in range(D_CROSS // CROSS_HEAD):
            sl = slice(CROSS_HEAD * h, CROSS_HEAD * (h + 1))
            dk, dg = _rms_bwd(kv_ref[:, sl], gk_ref[...], dkn_ref[:, sl])
            dkv_ref[:, sl] = dk.astype(BF)
            dgk = dgk + dg
        dkv_ref[:, D_CROSS:] = dv_ref[...].astype(BF)
        dgk_ref[...] = dgk

    return _rowwise(body, name="cross_kv_bwd", nrows=N_MEM, tile=N_MEM,
                    row_ins=[(kv, 2 * D_CROSS, 0), (dkn, D_CROSS, 0), (dv, D_CROSS, 0)], full_ins=[g_ck],
                    row_outs=[(2 * D_CROSS, BF)], acc_outs=[((1, CROSS_HEAD), F32)])


def _attn_merge_bwd(dmix, attn, g_attn):
    def body(i, n, dn_ref, a_ref, g_ref, da_ref, dl_ref, dg_ref):
        attn_v = a_ref[...]
        da, dg = _rms_bwd(attn_v, g_ref[...], dn_ref[...])
        da_ref[...] = da
        dl_ref[...] = _seg_sum(da * attn_v, HEAD)
        _accum(dg_ref, dg, i)

    return _rowwise(body, name="attn_merge_bwd", nrows=S, tile=ROW_TILE,
                    row_ins=[(dmix, D_ATTN, 0), (attn, D_ATTN, 0)], full_ins=[g_attn],
                    row_outs=[(D_ATTN, F32), (D_ATTN, F32)], acc_outs=[((1, D_ATTN), F32)])


def _attn_bwd(q, k, v, do, lse, delta, d, name):
    def kern(q_ref, k_ref, v_ref, do_ref, l_ref, d_ref, dq_ref, dk_ref, dv_ref):
        dk_ref[...] = jnp.zeros_like(dk_ref)
        dv_ref[...] = jnp.zeros_like(dv_ref)
        half0 = lax.broadcasted_iota(jnp.int32, (BLK, LANES), 1) < HEAD

        def group(g, carry):
            updates = []
            for bb in range(ATTN_QB):
                q_rows, k_rows, valid = _band(g * ATTN_QB + bb, d)
                q, do = q_ref[q_rows, :], do_ref[q_rows, :]
                lse_t, del_t = l_ref[q_rows, :], d_ref[q_rows, :]
                kb, vb = k_ref[k_rows, :].astype(BF), v_ref[k_rows, :].astype(BF)
                q2 = jnp.concatenate([jnp.where(half0, q, 0.0), jnp.where(half0, 0.0, q)], axis=0).astype(BF)
                do2 = jnp.concatenate([jnp.where(half0, do, 0.0), jnp.where(half0, 0.0, do)], axis=0).astype(BF)
                lse2 = jnp.concatenate([lse_t[:, 0:1], lse_t[:, HEAD:HEAD + 1]], axis=0)
                del2 = jnp.concatenate([del_t[:, 0:1], del_t[:, HEAD:HEAD + 1]], axis=0)
                s = jnp.where(valid, _dot(q2, kb, NT), NEG)
                p = jnp.exp(s - lse2)
                ds = (p * (_dot(do2, vb, NT) - del2)).astype(BF)
                dq2 = _dot(ds, kb, NN)
                dq_ref[q_rows, :] = jnp.where(half0, dq2[:BLK], dq2[BLK:])
                updates.append((k_rows, _dot(ds, q2, TN), _dot(p.astype(BF), do2, TN)))
            for k_rows, dk, dv in updates:
                dk_ref[k_rows, :] += dk
                dv_ref[k_rows, :] += dv
            return carry

        lax.fori_loop(0, NB // ATTN_QB, group, 0)

    seq = pl.BlockSpec((S, LANES), lambda p: (0, p))
    return pl.pallas_call(
        kern, name=name, grid=(D_ATTN // LANES,), in_specs=[seq, seq, V_COLS, seq, seq, seq],
        out_specs=[seq, seq, seq], out_shape=[jax.ShapeDtypeStruct((S, D_ATTN), F32)] * 3,
        compiler_params=_params("parallel"))(q, k, v, do, lse, delta)


def _qk_bwd(dqs, dks, dvs, proj, pos_col, gq128, gk128, inv_tab):
    scale = HEAD ** -0.5

    def body(i, n, *refs):
        dq_refs, dk_refs, dv_refs = refs[0:3], refs[3:6], refs[6:9]
        q_ref, k_ref, p_ref, gq_ref, gk_ref, it_ref = refs[9:15]
        dqo_ref, dko_ref, dvo_ref, dgq_ref, dgk_ref = refs[15:20]
        t = q_ref.shape[0]
        c, sg, lo = _rope_tables(p_ref[...], it_ref[...], t)
        dgq = jnp.zeros((1, LANES), F32)
        dgk = jnp.zeros((1, LANES), F32)
        for j in range(D_ATTN // LANES):
            sl = slice(LANES * j, LANES * (j + 1))
            dqr = _rope(dq_refs[0][:, sl] + dq_refs[1][:, sl] + dq_refs[2][:, sl], c, -sg, lo) * scale
            dkr = _rope(dk_refs[0][:, sl] + dk_refs[1][:, sl] + dk_refs[2][:, sl], c, -sg, lo)
            dq, gq = _seg_rms_bwd(q_ref[:, sl], gq_ref[...], dqr, HEAD)
            dk, gk = _seg_rms_bwd(k_ref[:, sl], gk_ref[...], dkr, HEAD)
            dqo_ref[:, sl] = dq.astype(BF)
            dko_ref[:, sl] = dk.astype(BF)
            dgq, dgk = dgq + gq, dgk + gk
        dvo_ref[...] = (dv_refs[0][...] + dv_refs[1][...] + dv_refs[2][...]).astype(BF)
        _accum(dgq_ref, _fold_heads(dgq), i)
        _accum(dgk_ref, _fold_heads(dgk), i)

    return _rowwise(body, name="qk_bwd", nrows=S, tile=ROW_TILE,
                    row_ins=[(a, D_ATTN, 0) for a in (*dqs, *dks, *dvs)]
                    + [(proj, D_ATTN, 0), (proj, D_ATTN, 1), (pos_col, 1, 0)],
                    full_ins=[gq128, gk128, inv_tab], row_outs=[(D_ATTN, BF)] * 3,
                    acc_outs=[((1, LANES), F32)] * 2)


def _ssd_post_bwd(y_ssd, xbc, proj, dmix, dskip_b, g_ssm):
    def body(i, n, y_ref, xs_ref, z_ref, do_ref, d_ref, g_ref, dy_ref, dz_ref, dxs_ref, dd_ref, dg_ref):
        z, xs = z_ref[...], xs_ref[...]
        sg = _sigmoid(z)
        gate = z * sg
        y = y_ref[...] + d_ref[...] * xs
        dy2, dg = _seg_rms_bwd(y * gate, g_ref[...], do_ref[...], 256)
        dy = dy2 * gate
        dy_ref[...] = dy.astype(BF)
        dz_ref[...] = (dy2 * y * (sg * (1.0 + z * (1.0 - sg)))).astype(BF)
        dxs_ref[...] = dy * d_ref[...]
        _accum(dd_ref, jnp.sum(dy * xs, axis=0, keepdims=True), i)
        _accum(dg_ref, dg, i)

    return _rowwise(body, name="ssd_post_bwd", nrows=S, tile=ROW_TILE,
                    row_ins=[(y_ssd, D_SSM, 0), (xbc, D_SSM, 0), (proj, D_SSM, 3), (dmix, D_SSM, 1)],
                    full_ins=[dskip_b, g_ssm], row_outs=[(D_SSM, BF), (D_SSM, BF), (D_SSM, F32)],
                    acc_outs=[((1, D_SSM), F32), ((1, D_SSM), F32)])


def _ssd_bwd(xbc, dy, h_all, a_b, dt_b, at_r, dt_r):
    def kern(c_ref, b_ref, x_ref, dy_ref, hall_ref, ab_ref, dtb_ref, at_ref, dt_ref,
             dc_ref, daq_ref, dx_ref, db_ref, dak_ref, ddt_ref, ddtb_ref):
        half0 = lax.broadcasted_iota(jnp.int32, (BLK, LANES), 1) < HEAD
        hms = (half0, jnp.logical_not(half0))
        causal = lax.broadcasted_iota(jnp.int32, (BLK, BLK), 1) <= lax.broadcasted_iota(jnp.int32, (BLK, BLK), 0)
        row = lax.broadcasted_iota(jnp.int32, (BLK, LANES), 0)

        def step(t, carry_in):
            dh_next, carry = carry_in
            for cc in reversed(range(SSD_CB)):
                chunk = (NB // SSD_CB - 1 - t) * SSD_CB + cc
                rows = pl.ds(pl.multiple_of(chunk * BLK, BLK), BLK)
                ci, bi, x = c_ref[rows, :].astype(BF), b_ref[rows, :].astype(BF), x_ref[rows, :]
                dyv = dy_ref[rows, :].astype(F32)
                ab, dtb = ab_ref[rows, :], dtb_ref[rows, :]
                before = ab_ref[pl.ds(pl.multiple_of(jnp.maximum(chunk * BLK - 8, 0), 8), 8), :][7:8, :]
                a_prev = jnp.where(chunk == 0, 0.0, before)
                a_end = ab[BLK - 1:BLK, :]
                alpha = jnp.exp(ab - a_prev)
                e_end = jnp.exp(a_end - ab)
                beta = e_end * dtb
                t_all = alpha[BLK - 1:BLK, :]
                hc = hall_ref[rows, :]
                hcb, dhb = hc.astype(BF), dh_next.astype(BF)

                cb = _dot(ci, bi, NT)
                at, dtj = at_ref[chunk], dt_ref[chunk]
                dcb = jnp.zeros((BLK, BLK), F32)
                dx = jnp.zeros((BLK, LANES), F32)
                rsum = []
                for hd in range(2):
                    dym = jnp.where(hms[hd], dyv, 0.0).astype(BF)
                    lm = _ssd_decay(ab[:, HEAD * hd:HEAD * hd + 1], at[hd:hd + 1, :], causal)
                    dth = dtj[hd:hd + 1, :]
                    dw = _dot(dym, jnp.where(hms[hd], x, 0.0).astype(BF), NT)
                    g = dw * cb * lm
                    dx = dx + _dot((cb * lm * dth).astype(BF), dym, TN)
                    gcol = jnp.sum(g, axis=0, keepdims=True)
                    ddt_ref[chunk, hd:hd + 1, :] = gcol
                    dak_ref[chunk, hd:hd + 1, :] = gcol * dth
                    rsum.append(jnp.sum(g * dth, axis=1, keepdims=True))
                    dcb = dcb + dw * lm * dth
                dcb = dcb.astype(BF)

                g1 = (alpha * dyv).astype(BF)
                dalpha = dyv * _dot(ci, hcb, NN)
                g2 = _dot(bi, dhb, NN)
                dbeta = x * g2
                bx = (beta * x).astype(BF)
                dc_ref[rows, :] = _dot(dcb, bi, NN) + _dot(g1, hcb, NT)
                db_ref[rows, :] = _dot(dcb, ci, TN) + _dot(bx, dhb, NT)
                dx_ref[rows, :] = dx + beta * g2
                ddtb_ref[rows, :] = _seg_sum(e_end * dbeta, HEAD)
                ada, bdb = alpha * dalpha, beta * dbeta
                t_dt = t_all * jnp.sum(dh_next * hc, axis=0, keepdims=True)
                end_term = _seg_sum(jnp.broadcast_to(jnp.sum(bdb, axis=0, keepdims=True) + t_dt, (8, LANES)), HEAD)[0:1]
                prev_term = _seg_sum(jnp.broadcast_to(jnp.sum(ada, axis=0, keepdims=True) + t_dt, (8, LANES)), HEAD)[0:1]
                daq = jnp.where(half0, rsum[0], rsum[1]) + _seg_sum(ada - bdb, HEAD)
                daq_ref[rows, :] = daq + jnp.where(row == BLK - 1, end_term - carry, 0.0)
                carry = prev_term
                dh_next = t_all * dh_next + _dot(ci, g1, TN)
            return dh_next, carry

        lax.fori_loop(0, NB // SSD_CB, step, (jnp.zeros((BLK, LANES), F32), jnp.zeros((1, LANES), F32)))

    npair = D_SSM // LANES
    rowvec = pl.BlockSpec((None, NB, 2, BLK), lambda p: (p, 0, 0, 0))
    seq = pl.BlockSpec((S, LANES), lambda p: (0, p))
    return pl.pallas_call(
        kern, name="ssd_bwd", grid=(npair,),
        in_specs=[pl.BlockSpec((S, LANES), lambda p: (0, 12 + p // 2)),
                  pl.BlockSpec((S, LANES), lambda p: (0, 8 + p // 2)),
                  seq, seq, seq, seq, seq, rowvec, rowvec],
        out_specs=[seq, seq, seq, seq, rowvec, rowvec, seq],
        out_shape=[jax.ShapeDtypeStruct((S, D_SSM), F32)] * 4
        + [jax.ShapeDtypeStruct((npair, NB, 2, BLK), F32)] * 2 + [jax.ShapeDtypeStruct((S, D_SSM), F32)],
        compiler_params=_params("parallel"))(xbc, xbc, xbc, dy, h_all, a_b, dt_b, at_r, dt_r)


def _ssd_prep_bwd(daq, dak, ddt_row, ddt_lane, dt, proj, dt_bias128, a_log128):
    def body(i, n, daq_ref, dak_ref, dd_ref, dd2_ref, dt_ref, raw_ref, b_ref, al_ref, draw_ref, dal_ref, db_ref,
             carry):
        @pl.when(i == 0)
        def _():
            carry[...] = jnp.zeros_like(carry)

        a = -jnp.exp(al_ref[...])
        tri = (lax.broadcasted_iota(jnp.int32, (BLK, BLK), 0) <= lax.broadcasted_iota(jnp.int32, (BLK, BLK), 1))
        rev = _dot3(tri.astype(BF), daq_ref[...] - dak_ref[...]) + carry[0:1, :]
        carry[...] = jnp.broadcast_to(rev[0:1, :], carry.shape)
        dtv = dt_ref[...]
        draw = (dd_ref[...] + dd2_ref[...] + a * rev) * _sigmoid(raw_ref[...] + b_ref[...])
        draw_ref[...] = draw.astype(BF)
        _accum(dal_ref, jnp.sum(dtv * rev, axis=0, keepdims=True) * a, i)
        _accum(db_ref, jnp.sum(draw, axis=0, keepdims=True), i)

    return _rowwise(body, name="ssd_prep_bwd", nrows=S, tile=BLK, reverse=True,
                    row_ins=[(daq, LANES, 0), (dak, LANES, 0), (ddt_row, LANES, 0), (ddt_lane, LANES, 0), (dt, LANES, 0),
                             (proj, LANES, DT_COLBLK)],
                    full_ins=[dt_bias128, a_log128], row_outs=[(LANES, BF)],
                    acc_outs=[((1, LANES), F32), ((1, LANES), F32)], scratch=[pltpu.VMEM((8, LANES), F32)])


def _conv_bwd(proj, conv_w, conv_b, d1, d2, map1, map2, col0, ntiles, name):
    base = (4 * D_ATTN + col0) // LANES

    def kern(x_ref, w_ref, b_ref, d1_ref, d2_ref, dx_ref, dw_ref, db_ref):
        x = x_ref[...]
        c, shifted = _conv_taps(x, w_ref)
        c = c + b_ref[...]
        sg = _sigmoid(c)
        dc = (d1_ref[...] + d2_ref[...]) * (sg * (1.0 + c * (1.0 - sg)))
        rows = lax.broadcasted_iota(jnp.int32, x.shape, 0)
        dx = jnp.zeros_like(x)
        for w in range(CONV_W):
            k = CONV_W - 1 - w
            up = dc if k == 0 else jnp.where(rows < S - k, pltpu.roll(dc, S - k, 0), 0.0)
            dx = dx + up * w_ref[w:w + 1, :]
            dw_ref[w:w + 1, :] = jnp.sum(dc * shifted[w], axis=0, keepdims=True)
        dx_ref[...] = dx.astype(BF)
        db_ref[...] = jnp.sum(dc, axis=0, keepdims=True)

    c0 = col0 // LANES
    return pl.pallas_call(
        kern, name=name, grid=(ntiles,),
        in_specs=[pl.BlockSpec((S, LANES), lambda c: (0, base + c)),
                  pl.BlockSpec((CONV_W, LANES), lambda c: (0, c0 + c)),
                  pl.BlockSpec((1, LANES), lambda c: (0, c0 + c)),
                  pl.BlockSpec((S, LANES), lambda c: (0, map1(c))),
                  pl.BlockSpec((S, LANES), lambda c: (0, map2(c)))],
        out_specs=[pl.BlockSpec((S, LANES), lambda c: (0, c)), pl.BlockSpec((CONV_W, LANES), lambda c: (0, c)),
                   pl.BlockSpec((1, LANES), lambda c: (0, c))],
        out_shape=[jax.ShapeDtypeStruct((S, ntiles * LANES), BF), jax.ShapeDtypeStruct((CONV_W, ntiles * LANES), F32),
                   jax.ShapeDtypeStruct((1, ntiles * LANES), F32)],
        compiler_params=_params("parallel"))(proj, conv_w, conv_b, d1, d2)


def _adamw(w, m, v, parts, name):
    r, c = w.shape
    n_parts = parts.shape[0]
    tile = r if r <= 512 else (128 if c > 1024 else 256)
    assert r % tile == 0
    c1 = 1.0 - ADAM_B1 ** ADAM_STEP
    c2 = 1.0 - ADAM_B2 ** ADAM_STEP

    def kern(w_ref, m_ref, v_ref, p_ref, g_ref, d_ref, mo_ref, vo_ref):
        g = p_ref[0].astype(F32)
        for k in range(1, n_parts):
            g = g + p_ref[k].astype(F32)
        mn = ADAM_B1 * m_ref[...] + (1.0 - ADAM_B1) * g
        vn = ADAM_B2 * v_ref[...] + (1.0 - ADAM_B2) * (g * g)
        g_ref[...] = g
        mo_ref[...] = mn
        vo_ref[...] = vn
        d_ref[...] = -ADAM_LR * ((mn / c1) / (jnp.sqrt(vn / c2) + ADAM_EPS) + ADAM_WD * w_ref[...])

    blk = pl.BlockSpec((tile, c), lambda i: (i, 0))
    return pl.pallas_call(
        kern, name=name, grid=(r // tile,),
        in_specs=[blk, blk, blk, pl.BlockSpec((n_parts, tile, c), lambda i: (0, i, 0))],
        out_specs=[blk] * 4, out_shape=[jax.ShapeDtypeStruct((r, c), F32)] * 4,
        compiler_params=_params("parallel"))(w, m, v, parts)


MESH = pl.DeviceIdType.MESH
ANY = pl.BlockSpec(memory_space=pl.ANY)


def _put_own(gathered, shard):
    me = 4 * lax.axis_index("x") + 2 * lax.axis_index("y") + lax.axis_index("c")
    return lax.dynamic_update_slice(gathered, shard[None], (me,) + (0,) * shard.ndim)


def _all_gather(arrs, name, after=None):
    na = len(arrs)
    n_after = 0 if after is None else 1

    def kern(*refs):
        ins, outs = refs[:na], refs[na + n_after:2 * na + n_after]
        send_sems, recv_sems = refs[2 * na + n_after:]
        x, y, c = lax.axis_index("x"), lax.axis_index("y"), lax.axis_index("c")
        me, sibling = (x, y, c), (x, y, 1 - c)
        chips = [(1 - x, y), (x, 1 - y), (1 - x, 1 - y)]

        def copy(a, k, block, to, src=None):
            px, py, pc = block
            dst = outs[a].at[4 * px + 2 * py + pc]
            return pltpu.make_async_remote_copy(
                src_ref=dst if src is None else src, dst_ref=dst, send_sem=send_sems.at[a, k],
                recv_sem=recv_sems.at[a, k], device_id=to, device_id_type=MESH)

        first = []
        for a in range(na):
            first.append(copy(a, 0, me, sibling, src=ins[a]))
            first += [copy(a, 1 + j, me, (*chip, c), src=ins[a]) for j, chip in enumerate(chips)]
        for cp in first:
            cp.start()
        passed = []
        for a in range(na):
            for j, chip in enumerate(chips):
                copy(a, 1 + j, (*chip, c), me).wait_recv()
                fwd = copy(a, 4 + j, (*chip, c), sibling)
                fwd.start()
                passed.append(fwd)
        for a in range(na):
            copy(a, 0, sibling, me).wait_recv()
            for j, chip in enumerate(chips):
                copy(a, 4 + j, (*chip, 1 - c), me).wait_recv()
        for cp in first + passed:
            cp.wait_send()

    outs = pl.pallas_call(
        kern, name=name, in_specs=[ANY] * (na + n_after), out_specs=[ANY] * na,
        out_shape=[jax.ShapeDtypeStruct((N_DEV,) + a.shape, a.dtype) for a in arrs],
        scratch_shapes=[pltpu.SemaphoreType.DMA((na, 7)), pltpu.SemaphoreType.DMA((na, 7))],
    )(*arrs, *([] if after is None else [after]))
    return [_put_own(o, a) for o, a in zip(outs, arrs)]


HBM = pl.BlockSpec(memory_space=pltpu.HBM)
SEM = pl.BlockSpec(memory_space=pltpu.SEMAPHORE)
EFFECT = pltpu.SideEffectType.DATAFLOW_SIDE_EFFECTING
N_CHIP = 4
N_COPIES = {"gather": 3, "scatter": 3, "forward": 4, "pair": 4}


def _in_hbm(a):
    return pltpu.with_memory_space_constraint(a, pltpu.HBM)


def _chip_copies(kind, src_refs, land_refs, send_sems, recv_sems):
    x, y, c = lax.axis_index("x"), lax.axis_index("y"), lax.axis_index("c")
    chips = [(1 - x, y), (x, 1 - y), (1 - x, 1 - y)]
    n = N_COPIES[kind]
    cps = []

    def add(a, j, src, dst, to):
        cps.append(pltpu.make_async_remote_copy(
            src_ref=src, dst_ref=dst, send_sem=send_sems.at[n * a + j], recv_sem=recv_sems.at[n * a + j],
            device_id=to, device_id_type=MESH))

    for a in range(len(src_refs)):
        if kind == "gather":
            for j, (px, py) in enumerate(chips):
                add(a, j, src_refs[a], land_refs[a].at[4 * x + 2 * y + c], (px, py, c))
        elif kind == "scatter":
            for j, (px, py) in enumerate(chips):
                add(a, j, src_refs[a].at[2 * px + py], land_refs[a].at[2 * x + y], (px, py, c))
        elif kind == "forward":
            add(a, 0, src_refs[a], land_refs[a].at[4 * x + 2 * y + c], (x, y, 1 - c))
            for j, (px, py) in enumerate(chips):
                slot = land_refs[a].at[4 * px + 2 * py + c]
                add(a, 1 + j, slot, slot, (x, y, 1 - c))
        else:
            for q in range(N_CHIP):
                add(a, q, src_refs[a].at[2 * q + 1 - c], land_refs[a].at[q], (x, y, 1 - c))
    return cps


def _exchange_start(kind, srcs, lands, after, name):
    na = len(srcs)
    n_after = 0 if after is None else 1

    def kern(*refs):
        src_refs, land_refs = refs[:na], refs[na:2 * na]
        send_sems, recv_sems = refs[2 * na + n_after], refs[2 * na + n_after + 1]
        token = refs[-1]
        for cp in _chip_copies(kind, src_refs, land_refs, send_sems, recv_sems):
            cp.start()
        token[...] = jnp.zeros_like(token)

    bufs = list(srcs) + list(lands)
    res = pl.pallas_call(
        kern, name=name,
        out_shape=(pltpu.SemaphoreType.DMA((N_COPIES[kind] * na,)), pltpu.SemaphoreType.DMA((N_COPIES[kind] * na,)))
        + tuple(pltpu.HBM(b.shape, b.dtype) for b in bufs) + (jax.ShapeDtypeStruct((8, LANES), F32),),
        in_specs=[HBM] * (2 * na) + [ANY] * n_after,
        out_specs=(SEM, SEM) + (HBM,) * (2 * na) + (pl.BlockSpec(memory_space=pltpu.VMEM),),
        input_output_aliases={i: 2 + i for i in range(2 * na)},
        compiler_params=pltpu.CompilerParams(has_side_effects=EFFECT),
    )(*[_in_hbm(b) for b in bufs], *([] if after is None else [after]))
    return (res[0], res[1]), list(res[2:2 + na]), list(res[2 + na:2 + 2 * na]), res[-1]


def _exchange_wait(kind, sems, srcs, lands, after, name):
    na = len(srcs)

    def kern(*refs):
        src_refs, land_refs = refs[:na], refs[na:2 * na]
        send_sems, recv_sems = refs[2 * na], refs[2 * na + 1]
        for cp in _chip_copies(kind, src_refs, land_refs, send_sems, recv_sems):
            cp.wait_send()
            cp.wait_recv()

    bufs = list(srcs) + list(lands)
    res = pl.pallas_call(
        kern, name=name, out_shape=tuple(pltpu.HBM(b.shape, b.dtype) for b in bufs),
        in_specs=[HBM] * (2 * na) + [SEM, SEM, ANY], out_specs=(HBM,) * (2 * na),
        input_output_aliases={i: i for i in range(2 * na)},
        compiler_params=pltpu.CompilerParams(has_side_effects=EFFECT),
    )(*bufs, sems[0], sems[1], after)
    return list(res[:na]), list(res[na:])


def _gather_finish(shards, lands, name):
    na = len(shards)

    def kern(*refs):
        ins, outs = refs[:na], refs[2 * na:3 * na]
        send_sems, recv_sems = refs[3 * na:]
        x, y, c = lax.axis_index("x"), lax.axis_index("y"), lax.axis_index("c")
        chips = [(1 - x, y), (x, 1 - y), (1 - x, 1 - y)]

        def copy(a, k, slot, pc, src=None):
            dst = outs[a].at[slot + pc]
            return pltpu.make_async_remote_copy(
                src_ref=dst if src is None else src, dst_ref=dst, send_sem=send_sems.at[a, k],
                recv_sem=recv_sems.at[a, k], device_id=(x, y, 1 - c), device_id_type=MESH)

        sends = []
        for a in range(na):
            sends.append(copy(a, 0, 4 * x + 2 * y, c, src=ins[a]))
            sends += [copy(a, 1 + j, 4 * px + 2 * py, c) for j, (px, py) in enumerate(chips)]
        for cp in sends:
            cp.start()
        for a in range(na):
            copy(a, 0, 4 * x + 2 * y, 1 - c).wait_recv()
            for j, (px, py) in enumerate(chips):
                copy(a, 1 + j, 4 * px + 2 * py, 1 - c).wait_recv()
        for cp in sends:
            cp.wait_send()

    return pl.pallas_call(
        kern, name=name, in_specs=[ANY] * (2 * na), out_specs=[ANY] * na,
        out_shape=[jax.ShapeDtypeStruct(l.shape, l.dtype) for l in lands],
        input_output_aliases={na + a: a for a in range(na)},
        scratch_shapes=[pltpu.SemaphoreType.DMA((na, 4)), pltpu.SemaphoreType.DMA((na, 4))])(*shards, *lands)


def _pair_exchange(parts, name):
    na = len(parts)

    def kern(*refs):
        ins, got = refs[:na], refs[na:2 * na]
        send_sems, recv_sems = refs[2 * na:]
        x, y, c = lax.axis_index("x"), lax.axis_index("y"), lax.axis_index("c")
        sends = []
        for a in range(na):
            for q in range(N_CHIP):
                sends.append(pltpu.make_async_remote_copy(
                    src_ref=ins[a].at[2 * q + 1 - c], dst_ref=got[a].at[q], send_sem=send_sems.at[a, q],
                    recv_sem=recv_sems.at[a, q], device_id=(x, y, 1 - c), device_id_type=MESH))
        for cp in sends:
            cp.start()
        for cp in sends:
            cp.wait()

    return pl.pallas_call(
        kern, name=name, in_specs=[ANY] * na, out_specs=[ANY] * na,
        out_shape=[jax.ShapeDtypeStruct((N_CHIP,) + p.shape[1:], p.dtype) for p in parts],
        scratch_shapes=[pltpu.SemaphoreType.DMA((na, N_CHIP)), pltpu.SemaphoreType.DMA((na, N_CHIP))])(*parts)


def _pair_sum(parts, got, name):
    _, r, cdim = parts.shape
    tile = min(r, ROW_TILE)
    core = lax.axis_index("c").astype(jnp.int32).reshape(1)

    def kern(c_ref, a_ref, b_ref, q_ref, l_ref):
        s = (a_ref[...].astype(F32) + b_ref[...].astype(F32)).astype(BF)
        q_ref[...] = s
        l_ref[...] = s

    blk = pl.BlockSpec((None, tile, cdim), lambda q, i, c_ref: (q, i, 0))
    out = jax.ShapeDtypeStruct((N_CHIP, r, cdim), BF)
    return pl.pallas_call(
        kern, name=name, out_shape=[out, out],
        grid_spec=pltpu.PrefetchScalarGridSpec(
            num_scalar_prefetch=1, grid=(N_CHIP, r // tile),
            in_specs=[pl.BlockSpec((None, tile, cdim), lambda q, i, c_ref: (2 * q + c_ref[0], i, 0)), blk],
            out_specs=[blk, blk]),
        compiler_params=_params("parallel", "parallel"))(core, parts, got)


W_IN_COLS = D_IN // N_DEV


def _w_in_from_blocks(w8):
    def kern(x_ref, o_ref):
        for j in range(N_DEV):
            o_ref[:, W_IN_COLS * j:W_IN_COLS * (j + 1)] = x_ref[j]
        o_ref[:, D_IN:] = jnp.zeros((ROW_TILE, D_INP - D_IN), o_ref.dtype)

    return pl.pallas_call(
        kern, name="w_in_from_blocks", grid=(D // ROW_TILE,),
        in_specs=[pl.BlockSpec((N_DEV, ROW_TILE, W_IN_COLS), lambda i: (0, i, 0))],
        out_specs=pl.BlockSpec((ROW_TILE, D_INP), lambda i: (i, 0)),
        out_shape=jax.ShapeDtypeStruct((D, D_INP), w8.dtype), compiler_params=_params("parallel"))(w8)


def _w_in_to_blocks(g):
    def kern(x_ref, o_ref):
        for j in range(N_DEV):
            o_ref[j] = x_ref[:, W_IN_COLS * j:W_IN_COLS * (j + 1)]

    return pl.pallas_call(
        kern, name="w_in_to_blocks", grid=(D // ROW_TILE,),
        in_specs=[pl.BlockSpec((ROW_TILE, D_INP), lambda i: (i, 0))],
        out_specs=pl.BlockSpec((N_DEV, ROW_TILE, W_IN_COLS), lambda i: (0, i, 0)),
        out_shape=jax.ShapeDtypeStruct((N_DEV, D, W_IN_COLS), g.dtype), compiler_params=_params("parallel"))(g)


def _pad_lanes(v, width=LANES):
    return jnp.pad(v, ((0, 0), (0, width - v.shape[1])))


def _row_layout(t16):
    return t16.T.reshape(N_HEADS // 2, 2, NB, BLK).transpose(0, 2, 1, 3)


def _row_layout_inv(r):
    return r.transpose(0, 2, 1, 3).reshape(N_HEADS, S).T


SMALL = (("g_mix", D), ("g_q", HEAD), ("g_k", HEAD), ("g_attn_out", D_ATTN), ("conv_b", D_CONV),
         ("dt_bias", 16), ("a_log", 16), ("d_skip", 16), ("g_ssm_out", D_SSM), ("g_cross", D),
         ("g_mem", D), ("g_cq", CROSS_HEAD), ("g_ck", CROSS_HEAD), ("g_mlp", D))
SMALL_ROWS = -(-sum(-(-w // LANES) for _, w in SMALL) // 8) * 8


def _pack_small(vals):
    rows = [_pad_lanes(vals[n], -(-w // LANES) * LANES).reshape(-1, LANES) for n, w in SMALL]
    packed = jnp.concatenate(rows, axis=0)
    return jnp.pad(packed, ((0, SMALL_ROWS - packed.shape[0]), (0, 0)))


def _unpack_small(packed):
    out, r = {}, 0
    for n, w in SMALL:
        nr = -(-w // LANES)
        out[n] = packed[r:r + nr].reshape(1, nr * LANES)[:, :w]
        r += nr
    return out


BIG = ("w_in", "w_out", "w_cq", "w_ckv", "w_co", "w_up", "w_down")
WEIGHTS = ("g_mix", "w_in", "g_q", "g_k", "g_attn_out", "conv_w", "conv_b", "dt_bias", "a_log", "d_skip",
           "g_ssm_out", "w_out", "g_cross", "g_mem", "w_cq", "w_ckv", "g_cq", "g_ck", "w_co", "g_mlp", "w_up", "w_down")


REST = ("w_out", "w_cq", "w_ckv", "w_co", "w_up", "w_down")


class _Overlap:
    def __init__(self, shards, after):
        self.gather, self.forward = {}, {}
        for tag, names in (("a", REST[:4]), ("b", REST[4:])):
            lands = [_put_own(lax.empty((N_DEV,) + shards[n].shape, BF), shards[n]) for n in names]
            self.gather[tag] = _exchange_start("gather", [shards[n] for n in names], lands, after, "ag_start_" + tag)
            after = self.gather[tag][3]
        self.token = after
        self.groups = []
        self.pairs = {}

    def rest_mid(self, tag, after):
        sems, srcs, lands, _ = self.gather[tag]
        srcs, lands = _exchange_wait("gather", sems, srcs, lands, after, "ag_wait_" + tag)
        self.forward[tag] = _exchange_start("forward", srcs, lands, None, "ag_fwd_start_" + tag)
        return self.forward[tag][3]

    def rest(self, tag, after):
        sems, srcs, lands, _ = self.forward[tag]
        _, lands = _exchange_wait("forward", sems, srcs, lands, after, "ag_fwd_wait_" + tag)
        wf = dict(zip(REST[:4] if tag == "a" else REST[4:], lands))
        for n in wf:
            if n in ("w_out", "w_cq", "w_ckv", "w_down"):
                wf[n] = wf[n].reshape(-1, wf[n].shape[-1])
        return wf

    def pair_start(self, name, grad):
        got = lax.empty((N_CHIP,) + grad.shape[1:], grad.dtype)
        self.pairs[name] = _exchange_start("pair", [grad], [got], None, "rs_pair_start_" + name)
        return self.pairs[name][3]

    def emit(self, grads, after):
        names, tag = list(grads), "_%d" % len(self.groups)
        grads, got = dict(grads), {}
        for n in names:
            if n in self.pairs:
                sems, srcs, lands, _ = self.pairs[n]
                srcs, lands = _exchange_wait("pair", sems, srcs, lands, after, "rs_pair_wait_" + n)
                grads[n], got[n] = srcs[0], lands[0]
        sync = [n for n in names if n not in got]
        if sync:
            got.update(zip(sync, _pair_exchange([grads[n] for n in sync], "rs_pair" + tag)))
        sums = [_pair_sum(grads[n], got[n], "rs_sum_" + n) for n in names]
        sems, srcs, lands, token = _exchange_start("scatter", [q for q, _ in sums], [l for _, l in sums], None,
                                                   "rs_chip_start" + tag)
        self.groups.append((names, sems, srcs, lands))
        return token

    def finish(self, gi, after):
        names, sems, srcs, lands = self.groups[gi]
        _, lands = _exchange_wait("scatter", sems, srcs, lands, after, "rs_chip_wait_%d" % gi)
        return dict(zip(names, lands))


def _tie(v, token):
    return v if token is None else v + token[0:1, 0:1]


def _local_step(x, mem, pos_col, target, p, wf, hooks=None):
    p = dict(p)
    inv = np.zeros((1, LANES), np.float32)
    freq = (ROPE_THETA ** (-2.0 * np.arange(ROT // 2, dtype=np.float32) / ROT)).astype(np.float32)
    for l in range(LANES):
        if l % HEAD < ROT:
            inv[0, l] = freq[(l % HEAD) % (ROT // 2)]
    inv_tab = jnp.asarray(inv)
    gq128, gk128 = jnp.tile(p["g_q"], (1, 2)), jnp.tile(p["g_k"], (1, 2))
    dtb128, alog128 = _pad_lanes(p["dt_bias"]), _pad_lanes(p["a_log"])
    dskip_b = jnp.repeat(p["d_skip"], HEAD, axis=1)
    conv_w, conv_b = wf["conv_w"], p["conv_b"]

    h = _rms_fwd(x, p["g_mix"], "rms_mix")
    proj = _matmul(h, wf["w_in"], mode="nn", name="mm_in", tm=1024, tn=1280, tk=D)
    qr, kr = _qk_prep(proj, pos_col, gq128, gk128, inv_tab)
    dilations = (1, 4, 16)
    fwd = [_attn_fwd(qr, kr, proj, d, "attn_fwd_d%d" % d) for d in dilations]
    attn, lse, attn_n = _attn_merge([o for o, _ in fwd], [l for _, l in fwd], p["g_attn_out"])

    xbc = _conv_fwd(proj, conv_w, conv_b)
    dt, a_cs = _ssd_prep(proj, dtb128, alog128)
    a_b = _tie(jnp.repeat(a_cs[:, :N_HEADS], HEAD, axis=1), None if hooks is None else hooks.rest_mid("a", xbc))
    dt_b = jnp.repeat(dt[:, :N_HEADS], HEAD, axis=1)
    at_r, dt_r = _row_layout(a_cs[:, :N_HEADS]), _row_layout(dt[:, :N_HEADS])
    y_ssd, h_all = _ssd_fwd(xbc, a_b, dt_b, at_r, dt_r)
    ssm_n = _ssd_post(y_ssd, xbc, proj, dskip_b, p["g_ssm_out"])

    if hooks is not None:
        wf = {**wf, **hooks.rest("a", ssm_n)}
    mix = jnp.concatenate([attn_n, ssm_n], axis=1)
    x1 = _matmul(mix, wf["w_out"], mode="nn", name="mm_out", tm=1024, tn=1024, tk=D,
                 extras=(x,), epilogue=lambda acc, r: (acc + r,))
    hc = _rms_fwd(x1, _tie(p["g_cross"], None if hooks is None else hooks.rest_mid("b", x1)), "rms_cross")
    memh = _rms_fwd(mem, p["g_mem"], "rms_mem")
    qc = _matmul(hc, wf["w_cq"], mode="nn", name="mm_cq", tm=1024, tn=512, tk=D)
    kv = _matmul(memh, wf["w_ckv"], mode="nn", name="mm_ckv", tm=N_MEM, tn=1024, tk=D)
    oc = _cross_fwd(qc, kv, p["g_cq"], p["g_ck"])
    x2 = _matmul(oc, wf["w_co"], mode="nn", name="mm_co", tm=1024, tn=256, tk=512, b_cb=256,
                 extras=(x1,), epilogue=lambda acc, r: (acc + r,))
    hm = _rms_fwd(x2, p["g_mlp"], "rms_mlp")
    if hooks is not None:
        wf = {**wf, **hooks.rest("b", hm)}

    def up_epi(acc):
        ru = jnp.maximum(acc, 0.0)
        return ru * ru, 2.0 * ru

    act, relu2 = _matmul(hm, wf["w_up"], mode="nn", name="mm_up", tm=1024, tn=1024, tk=D, b_cb=1024,
                         out_dtypes=(BF, BF), epilogue=up_epi)

    def loss_epi(acc, r, t):
        d = (acc + r - t) * (1.0 / D)
        return d, d

    dy, dyb = _matmul(act, wf["w_down"], mode="nn", name="mm_down", tm=512, tn=1024, tk=2048, extras=(x2, target),
                      out_dtypes=(F32, BF), epilogue=loss_epi)
    loss_part = _loss_sum(dy)[0, 0] * (0.5 * D)

    gb, gs = {}, {}
    gb["w_down"] = _matmul(act, dyb, mode="tn", name="mm_down_dw", tm=1024, tn=1024, tk=S,
                           out_dtypes=(BF,)).reshape(N_DEV, D_FF // N_DEV, D)
    token = None if hooks is None else hooks.pair_start("w_down", gb["w_down"])
    du = _matmul(dyb, wf["w_down"], mode="nt", name="mm_down_dx", tm=1024, tn=1024, tk=D, extras=(relu2,),
                 out_dtypes=(BF,), epilogue=lambda acc, r: (acc * r.astype(F32),), after=token)
    gb["w_up"] = _matmul(hm, du, mode="tn", name="mm_up_dw", tm=1024, tn=1024, tk=S, out_dtypes=(BF,), out_cb=1024)
    token = None if hooks is None else hooks.pair_start("w_up", gb["w_up"])
    dhm = _matmul(du, wf["w_up"], mode="nt", name="mm_up_dx", tm=1024, tn=1024, tk=1024, b_cb=1024, after=token)
    if hooks is not None:
        p["g_mlp"] = _tie(p["g_mlp"], hooks.emit({n: gb[n] for n in ("w_down", "w_up")}, dhm))
    dx2, dx2b, gs["g_mlp"] = _rms_bwd_call(x2, p["g_mlp"], dhm, dy, "rms_mlp_bwd")

    doc = _matmul(dx2b, wf["w_co"], mode="nt", name="mm_co_dx", tm=1024, tn=512, tk=256, b_cb=256)
    gb["w_co"] = _matmul(oc, dx2b, mode="tn", name="mm_co_dw", tm=512, tn=256, tk=S, out_dtypes=(BF,), out_cb=256)
    dqc, dkn, dvc, gs["g_cq"] = _cross_bwd(qc, doc, kv, p["g_cq"], p["g_ck"])
    dkv, gs["g_ck"] = _cross_kv_bwd(kv, dkn, dvc, p["g_ck"])
    gb["w_cq"] = _matmul(hc, dqc, mode="tn", name="mm_cq_dw", tm=1024, tn=512, tk=S,
                         out_dtypes=(BF,)).reshape(N_DEV, D // N_DEV, D_CROSS)
    dhc = _matmul(dqc, wf["w_cq"], mode="nt", name="mm_cq_dx", tm=1024, tn=1024, tk=512)
    gb["w_ckv"] = _matmul(memh, dkv, mode="tn", name="mm_ckv_dw", tm=1024, tn=1024, tk=N_MEM,
                          out_dtypes=(BF,)).reshape(N_DEV, D // N_DEV, 2 * D_CROSS)
    dmemh = _matmul(dkv, wf["w_ckv"], mode="nt", name="mm_ckv_dx", tm=N_MEM, tn=1024, tk=1024)
    (gs["g_mem"],) = _rms_bwd_call(mem, p["g_mem"], dmemh, None, "rms_mem_bwd")
    dx1, dx1b, gs["g_cross"] = _rms_bwd_call(x1, p["g_cross"], dhc, dx2, "rms_cross_bwd")

    dmix = _matmul(dx1b, wf["w_out"], mode="nt", name="mm_out_dx", tm=1024, tn=1024, tk=D)
    gb["w_out"] = _matmul(mix, dx1b, mode="tn", name="mm_out_dw", tm=1024, tn=1024, tk=S,
                          out_dtypes=(BF,)).reshape(N_DEV, D // N_DEV, D)
    if hooks is not None:
        p["g_attn_out"] = _tie(p["g_attn_out"],
                               hooks.emit({n: gb[n] for n in ("w_co", "w_cq", "w_ckv", "w_out")}, dmix))

    dattn, delta, gs["g_attn_out"] = _attn_merge_bwd(dmix, attn, p["g_attn_out"])
    bwd = [_attn_bwd(qr, kr, proj, dattn, lse, delta, d, "attn_bwd_d%d" % d) for d in dilations]
    dq_pre, dk_pre, dv_pre, dgq, dgk = _qk_bwd([t[0] for t in bwd], [t[1] for t in bwd], [t[2] for t in bwd],
                                               proj, pos_col, gq128, gk128, inv_tab)
    gs["g_q"], gs["g_k"] = dgq[:, :HEAD], dgk[:, :HEAD]

    dy_ssd, dz, dxs_skip, dd_lane, gs["g_ssm_out"] = _ssd_post_bwd(y_ssd, xbc, proj, dmix, dskip_b, p["g_ssm_out"])
    gs["d_skip"] = dd_lane.reshape(N_HEADS, HEAD).sum(axis=1).reshape(1, N_HEADS)
    dc_pair, daq_b, dx_ssd, db_pair, dak_r, ddt_r, ddt_b = _ssd_bwd(xbc, dy_ssd, h_all, a_b, dt_b, at_r, dt_r)
    daq = _pad_lanes(daq_b[:, ::HEAD])
    dak = _pad_lanes(_row_layout_inv(dak_r))
    ddt_direct = _pad_lanes(_row_layout_inv(ddt_r))
    ddt_raw, dalog, dbias = _ssd_prep_bwd(daq, dak, ddt_direct, _pad_lanes(ddt_b[:, ::HEAD]), dt, proj,
                                          dtb128, alog128)
    gs["a_log"], gs["dt_bias"] = dalog[:, :N_HEADS], dbias[:, :N_HEADS]
    same = lambda c: c
    dxbc_x, dcw_x, dcb_x = _conv_bwd(proj, conv_w, conv_b, dxs_skip, dx_ssd, same, same, 0, 8, "conv_bwd_x")
    dxbc_b, dcw_b, dcb_b = _conv_bwd(proj, conv_w, conv_b, db_pair, db_pair, lambda c: 2 * c, lambda c: 2 * c + 1,
                                     D_SSM, 4, "conv_bwd_b")
    dxbc_c, dcw_c, dcb_c = _conv_bwd(proj, conv_w, conv_b, dc_pair, dc_pair, lambda c: 2 * c, lambda c: 2 * c + 1,
                                     D_SSM + 512, 4, "conv_bwd_c")
    g_conv_w = jnp.concatenate([dcw_x, dcw_b, dcw_c], axis=1)
    gs["conv_b"] = jnp.concatenate([dcb_x, dcb_b, dcb_c], axis=1)

    pieces = [dq_pre, dk_pre, dv_pre, dz, dxbc_x, dxbc_b, dxbc_c, ddt_raw]
    pieces.append(jnp.zeros((S, D_INP - sum(t.shape[1] for t in pieces)), BF))
    dproj = jnp.concatenate(pieces, axis=1)
    dw_in = _matmul(h, dproj, mode="tn", name="mm_in_dw", tm=1024, tn=1280, tk=S, out_dtypes=(BF,))
    gb["w_in"] = _w_in_to_blocks(dw_in)
    token = None if hooks is None else hooks.emit({"w_in": gb["w_in"]}, dw_in)
    dh = _matmul(dproj, wf["w_in"], mode="nt", name="mm_in_dx", tm=1024, tn=512, tk=D_INP // 2, after=token)
    grad_x, _, gs["g_mix"] = _rms_bwd_call(x, p["g_mix"], dh, dx1, "rms_mix_bwd")
    return loss_part, grad_x, gb, gs, g_conv_w


def kernel(x, mem, positions, g_mix, w_in, g_q, g_k, g_attn_out, conv_w, conv_b, dt_bias, a_log, d_skip, g_ssm_out, w_out, g_cross, g_mem, w_cq, w_ckv, g_cq, g_ck, w_co, g_mlp, w_up, w_down, loss_target, m_g_mix, m_w_in, m_g_q, m_g_k, m_g_attn_out, m_conv_w, m_conv_b, m_dt_bias, m_a_log, m_d_skip, m_g_ssm_out, m_w_out, m_g_cross, m_g_mem, m_w_cq, m_w_ckv, m_g_cq, m_g_ck, m_w_co, m_g_mlp, m_w_up, m_w_down, v_g_mix, v_w_in, v_g_q, v_g_k, v_g_attn_out, v_conv_w, v_conv_b, v_dt_bias, v_a_log, v_d_skip, v_g_ssm_out, v_w_out, v_g_cross, v_g_mem, v_w_cq, v_w_ckv, v_g_cq, v_g_ck, v_w_co, v_g_mlp, v_w_up, v_w_down):
    args = dict(locals())
    w = {n: args[n] for n in WEIGHTS}
    m = {n: args["m_" + n] for n in WEIGHTS}
    v = {n: args["v_" + n] for n in WEIGHTS}
    small = {n: w[n] for n, _ in SMALL}
    me = 4 * lax.axis_index("x") + 2 * lax.axis_index("y") + lax.axis_index("c")

    w_in_g, conv_w_g = _all_gather([w["w_in"][0].astype(BF), w["conv_w"][0]], "ag_first")
    wf = {"w_in": _w_in_from_blocks(w_in_g), "conv_w": conv_w_g.transpose(1, 0, 2).reshape(CONV_W, D_CONV)}
    overlap = _Overlap({n: w[n][0].astype(BF) for n in REST}, w_in_g)
    p = dict(small)
    p["g_mix"] = _tie(p["g_mix"], overlap.token)

    loss_part, grad_x, _, gs, g_conv_w = _local_step(
        x[0], mem[0], positions.reshape(S, 1), loss_target[0], p, wf, overlap)
    loss = lax.psum(loss_part, ("x", "y", "c"))

    out = {}
    last = grad_x
    for gi in range(3):
        for n, parts in overlap.finish(gi, last).items():
            res_n = _adamw(w[n][0], m[n][0], v[n][0], parts, "adamw_" + n)
            out[n] = [t.reshape(w[n].shape) for t in res_n]
            last = res_n[0]

    sm_parts, cw_parts = _all_gather([_pack_small(gs), g_conv_w], "ag_small_grads", after=last)
    sm = [_unpack_small(t) for t in _adamw(_pack_small(small), _pack_small({n: m[n] for n, _ in SMALL}),
                                           _pack_small({n: v[n] for n, _ in SMALL}), sm_parts, "adamw_small")]
    for n, _ in SMALL:
        out[n] = [t[n] for t in sm]
    cols = D_CONV // N_DEV
    cw_mine = lax.dynamic_slice_in_dim(cw_parts, me * cols, cols, axis=2)
    out["conv_w"] = [t.reshape(w["conv_w"].shape) for t in
                     _adamw(w["conv_w"][0], m["conv_w"][0], v["conv_w"][0], cw_mine, "adamw_conv_w")]

    res = [loss, grad_x.reshape(x.shape)]
    for k in range(4):
        res += [out[n][k] for n in WEIGHTS]
    return tuple(res)
```

```python
import functools
import math

import numpy as np
import jax
import jax.numpy as jnp
from jax import lax
from jax.experimental import pallas as pl
from jax.experimental.pallas import tpu as pltpu

F32 = jnp.float32
BF = jnp.bfloat16

N_DEV = 8
S = 2048
D = 2048
D_ATTN = 1024
N_HEADS = 16
HEAD = 64
ROT = 16
ROPE_THETA = 500000.0
D_SSM = 1024
D_CONV = 2048
CONV_W = 4
N_MEM = 256
D_CROSS = 512
CROSS_HEAD = 128
D_FF = 8192
D_IN = 6160
D_INP = 6400
DT_COLBLK = (4 * D_ATTN + D_CONV) // 128
EPS = 1e-6
BLK = 128
NB = S // BLK
LANES = 128
NEG = -1e30

ADAM_LR = 0.001
ADAM_B1 = 0.9
ADAM_B2 = 0.999
ADAM_EPS = 1e-08
ADAM_WD = 0.01
ADAM_STEP = 10

VMEM_LIMIT_BYTES = 48 * 1024 * 1024
ROW_TILE = 256


def _params(*sem):
    return pltpu.CompilerParams(dimension_semantics=sem, vmem_limit_bytes=VMEM_LIMIT_BYTES)


def _matmul(a, b, *, mode, name, tm, tn, tk, out_dtypes=(F32,), b_cb=None, out_cb=None,
            extras=(), epilogue=None, after=None):
    if mode == "tn":
        kk, m = a.shape
    else:
        m, kk = a.shape
    if b_cb is None:
        br, bc = b.shape
    else:
        br, bc = b.shape[1], N_DEV * b_cb
    n = br if mode == "nt" else bc
    assert m % tm == 0 and n % tn == 0 and kk % tk == 0, (name, m, n, kk)
    nk = kk // tk
    grid = (m // tm, n // tn, nk)

    if mode == "tn":
        a_spec = pl.BlockSpec((tk, tm), lambda i, j, k: (k, i))
    else:
        a_spec = pl.BlockSpec((tm, tk), lambda i, j, k: (i, k))
    if mode == "nt":
        b_blk, b_idx = (tn, tk), (lambda i, j, k: (j, k))
    else:
        b_blk, b_idx = (tk, tn), (lambda i, j, k: (k, j))
    group = 1
    if b_cb is None:
        b_spec = pl.BlockSpec(b_blk, b_idx)
    elif mode == "nt" and tk > b_cb:
        assert tk % b_cb == 0
        group = tk // b_cb
        b_spec = pl.BlockSpec((group, tn, b_cb), lambda i, j, k: (k, j, 0))
    else:
        tc = b_blk[1]
        assert b_cb % tc == 0
        per = b_cb // tc

        def b_idx3(i, j, k):
            r, c = b_idx(i, j, k)
            return (c // per, r, c % per)
        b_spec = pl.BlockSpec((None,) + b_blk, b_idx3)
    ex_specs = [pl.BlockSpec((tm, tn), lambda i, j, k: (i, j)) for _ in extras]
    if out_cb is None:
        out_shape = [jax.ShapeDtypeStruct((m, n), dt) for dt in out_dtypes]
        out_specs = [pl.BlockSpec((tm, tn), lambda i, j, k: (i, j)) for _ in out_dtypes]
    else:
        assert out_cb % tn == 0
        pero = out_cb // tn
        out_shape = [jax.ShapeDtypeStruct((N_DEV, m, out_cb), dt) for dt in out_dtypes]
        out_specs = [pl.BlockSpec((None, tm, tn), lambda i, j, k: (j // pero, i, j % pero)) for _ in out_dtypes]
    dn = {"nn": (((1,), (0,)), ((), ())), "nt": (((1,), (1,)), ((), ())), "tn": (((0,), (0,)), ((), ()))}[mode]
    ne, no = len(extras), len(out_dtypes)
    n_after = 0 if after is None else 1

    def kern(a_ref, b_ref, *rest):
        ex_refs, out_refs = rest[:ne], rest[ne + n_after:ne + n_after + no]

        def finish(acc):
            outs = (acc,) if epilogue is None else epilogue(acc, *[r[...] for r in ex_refs])
            for r, o in zip(out_refs, outs):
                r[...] = o.astype(r.dtype)

        if group == 1:
            part = lax.dot_general(a_ref[...].astype(BF), b_ref[...].astype(BF), dn, preferred_element_type=F32)
        else:
            part = sum(lax.dot_general(a_ref[:, g * b_cb:(g + 1) * b_cb].astype(BF), b_ref[g].astype(BF), dn,
                                       preferred_element_type=F32) for g in range(group))
        if nk == 1:
            finish(part)
            return
        acc_ref = rest[ne + n_after + no]
        k = pl.program_id(2)

        @pl.when(k == 0)
        def _():
            acc_ref[...] = part

        @pl.when(k > 0)
        def _():
            acc_ref[...] += part

        @pl.when(k == nk - 1)
        def _():
            finish(acc_ref[...])

    res = pl.pallas_call(
        kern, name=name, grid=grid, in_specs=[a_spec, b_spec] + ex_specs + [ANY] * n_after, out_specs=out_specs,
        out_shape=out_shape, scratch_shapes=[pltpu.VMEM((tm, tn), F32)] if nk > 1 else [],
        compiler_params=_params("parallel", "parallel", "arbitrary"),
    )(a, b, *extras, *([] if after is None else [after]))
    return res[0] if no == 1 else tuple(res)


def _rowwise(body, *, name, nrows, tile, row_ins, full_ins=(), row_outs=(), acc_outs=(), scratch=(),
             reverse=False):
    n = nrows // tile

    def ridx(i):
        return (n - 1 - i) if reverse else i

    in_specs, args = [], []
    for arr, width, cb in row_ins:
        in_specs.append(pl.BlockSpec((tile, width), lambda i, cb=cb: (ridx(i), cb)))
        args.append(arr)
    for arr in full_ins:
        in_specs.append(pl.BlockSpec(arr.shape, lambda i, nd=arr.ndim: (0,) * nd))
        args.append(arr)
    out_shape, out_specs = [], []
    for width, dt in row_outs:
        out_shape.append(jax.ShapeDtypeStruct((nrows, width), dt))
        out_specs.append(pl.BlockSpec((tile, width), lambda i: (ridx(i), 0)))
    for shp, dt in acc_outs:
        out_shape.append(jax.ShapeDtypeStruct(shp, dt))
        out_specs.append(pl.BlockSpec(shp, lambda i, nd=len(shp): (0,) * nd))

    def kern(*refs):
        body(pl.program_id(0), n, *refs)

    return pl.pallas_call(kern, name=name, grid=(n,), in_specs=in_specs, out_specs=out_specs,
                          out_shape=out_shape, scratch_shapes=list(scratch),
                          compiler_params=_params("arbitrary"))(*args)


def _accum(ref, val, i):
    @pl.when(i == 0)
    def _():
        ref[...] = val

    @pl.when(i > 0)
    def _():
        ref[...] += val


def _sigmoid(x):
    return 1.0 / (1.0 + jnp.exp(-x))


def _rms_n(x):
    r = lax.rsqrt(jnp.mean(x * x, axis=-1, keepdims=True) + EPS)
    return x * r, r


def _rms_bwd(x, g, dh):
    n, r = _rms_n(x)
    dn = dh * g
    dx = r * (dn - n * jnp.mean(dn * n, axis=-1, keepdims=True))
    return dx, jnp.sum(dh * n, axis=0, keepdims=True)


def _seg_sum(x, seg):
    t, w = x.shape
    tiles = [x[:, LANES * j:LANES * (j + 1)] for j in range(w // LANES)]
    outs = []
    if seg == 64:
        lo = lax.broadcasted_iota(jnp.int32, (t, LANES), 1) < 64
        for xt in tiles:
            s_lo = jnp.sum(jnp.where(lo, xt, 0.0), axis=-1, keepdims=True)
            s_hi = jnp.sum(jnp.where(lo, 0.0, xt), axis=-1, keepdims=True)
            outs.append(jnp.where(lo, s_lo, s_hi))
    else:
        sums = [jnp.sum(xt, axis=-1, keepdims=True) for xt in tiles]
        if seg == 256:
            sums = [sums[2 * (j // 2)] + sums[2 * (j // 2) + 1] for j in range(len(sums))]
        else:
            assert seg == 128
        outs = [jnp.broadcast_to(s, (t, LANES)) for s in sums]
    return outs[0] if len(outs) == 1 else jnp.concatenate(outs, axis=1)


def _seg_rms_n(x, seg):
    r = lax.rsqrt(_seg_sum(x * x, seg) * (1.0 / seg) + EPS)
    return x * r, r


def _seg_rms_bwd(x, g, dy, seg):
    n, r = _seg_rms_n(x, seg)
    dn = dy * g
    dx = r * (dn - n * (_seg_sum(dn * n, seg) * (1.0 / seg)))
    return dx, jnp.sum(dy * n, axis=0, keepdims=True)


def _rope_tables(pos_col, inv_tab, t):
    ang = pos_col.astype(F32) * inv_tab
    idx = lax.broadcasted_iota(jnp.int32, (t, LANES), 1) % HEAD
    cos, sin = jnp.cos(ang), jnp.sin(ang)
    c = jnp.where(idx < ROT, cos, 1.0)
    sg = jnp.where(idx < ROT // 2, -sin, jnp.where(idx < ROT, sin, 0.0))
    return c, sg, idx < ROT // 2


def _rope(x, c, sg, lo):
    partner = jnp.where(lo, pltpu.roll(x, LANES - ROT // 2, 1), pltpu.roll(x, ROT // 2, 1))
    return x * c + partner * sg


def _fold_heads(v):
    v8 = jnp.broadcast_to(v, (8, LANES))
    return (v8 + pltpu.roll(v8, HEAD, 1))[0:1]


def _dot(a, b, dn):
    return lax.dot_general(a, b, (dn, ((), ())), preferred_element_type=F32)


NN = ((1,), (0,))
NT = ((1,), (1,))
TN = ((0,), (0,))


def _dot3(l01, x):
    x1 = x.astype(BF)
    r1 = x - x1.astype(F32)
    x2 = r1.astype(BF)
    x3 = (r1 - x2.astype(F32)).astype(BF)
    return _dot(l01, x1, NN) + _dot(l01, x2, NN) + _dot(l01, x3, NN)


def _rms_fwd(x, g, name):
    rows = x.shape[0]

    def body(i, n, x_ref, g_ref, o_ref):
        nx, _ = _rms_n(x_ref[...])
        o_ref[...] = (nx * g_ref[...]).astype(BF)

    return _rowwise(body, name=name, nrows=rows, tile=min(ROW_TILE, rows), row_ins=[(x, D, 0)],
                    full_ins=[g], row_outs=[(D, BF)])[0]


def _qk_prep(proj, pos_col, gq128, gk128, inv_tab):
    scale = HEAD ** -0.5

    def body(i, n, q_ref, k_ref, p_ref, gq_ref, gk_ref, it_ref, qo_ref, ko_ref):
        t = q_ref.shape[0]
        c, sg, lo = _rope_tables(p_ref[...], it_ref[...], t)
        for j in range(D_ATTN // LANES):
            sl = slice(LANES * j, LANES * (j + 1))
            qn, _ = _seg_rms_n(q_ref[:, sl], HEAD)
            kn, _ = _seg_rms_n(k_ref[:, sl], HEAD)
            qo_ref[:, sl] = _rope(qn * gq_ref[...], c, sg, lo) * scale
            ko_ref[:, sl] = _rope(kn * gk_ref[...], c, sg, lo)

    return _rowwise(body, name="qk_prep", nrows=S, tile=ROW_TILE,
                    row_ins=[(proj, D_ATTN, 0), (proj, D_ATTN, 1), (pos_col, 1, 0)],
                    full_ins=[gq128, gk128, inv_tab], row_outs=[(D_ATTN, F32)] * 2)


ATTN_QB = 4
V_COLS = pl.BlockSpec((S, LANES), lambda p: (0, 2 * D_ATTN // LANES + p))


def _band(b, d):
    nb = NB // d
    r, n = b // nb, b % nb
    width, kn = (BLK, n) if nb == 1 else (2 * BLK, jnp.maximum(n - 1, 0))

    def rows(first_member, count):
        if d == 1:
            return pl.ds(pl.multiple_of(first_member, BLK), count)
        return pl.ds(first_member * d + r, count, stride=d)

    qm = n * BLK + (lax.broadcasted_iota(jnp.int32, (2 * BLK, width), 0) & (BLK - 1))
    km = kn * BLK + lax.broadcasted_iota(jnp.int32, (2 * BLK, width), 1)
    valid = km <= qm
    if nb > 1:
        valid = valid & (km >= qm - BLK)
    return rows(n * BLK, BLK), rows(kn * BLK, width), valid


def _attn_fwd(q, k, v, d, name):
    def kern(q_ref, k_ref, v_ref, o_ref, l_ref):
        half0 = lax.broadcasted_iota(jnp.int32, (BLK, LANES), 1) < HEAD

        def group(g, carry):
            for bb in range(ATTN_QB):
                q_rows, k_rows, valid = _band(g * ATTN_QB + bb, d)
                q = q_ref[q_rows, :]
                kb, vb = k_ref[k_rows, :].astype(BF), v_ref[k_rows, :].astype(BF)
                q2 = jnp.concatenate([jnp.where(half0, q, 0.0), jnp.where(half0, 0.0, q)], axis=0).astype(BF)
                s = jnp.where(valid, _dot(q2, kb, NT), NEG)
                m = jnp.max(s, axis=-1, keepdims=True)
                p = jnp.exp(s - m)
                den = jnp.sum(p, axis=-1, keepdims=True)
                o2 = _dot(p.astype(BF), vb, NN) / den
                l2 = m + jnp.log(den)
                o_ref[q_rows, :] = jnp.where(half0, o2[:BLK], o2[BLK:])
                l_ref[q_rows, :] = jnp.where(half0, l2[:BLK], l2[BLK:])
            return carry

        lax.fori_loop(0, NB // ATTN_QB, group, 0)

    seq = pl.BlockSpec((S, LANES), lambda p: (0, p))
    return pl.pallas_call(
        kern, name=name, grid=(D_ATTN // LANES,), in_specs=[seq, seq, V_COLS], out_specs=[seq, seq],
        out_shape=[jax.ShapeDtypeStruct((S, D_ATTN), F32)] * 2,
        compiler_params=_params("parallel"))(q, k, v)


def _attn_merge(os_, ls_, g_attn):
    def body(i, n, o1, o2, o3, l1, l2, l3, g_ref, a_ref, lse_ref, an_ref):
        la, lb, lc = l1[...], l2[...], l3[...]
        m = jnp.maximum(jnp.maximum(la, lb), lc)
        ea, eb, ec = jnp.exp(la - m), jnp.exp(lb - m), jnp.exp(lc - m)
        den = ea + eb + ec
        attn = (ea * o1[...] + eb * o2[...] + ec * o3[...]) / den
        a_ref[...] = attn
        lse_ref[...] = m + jnp.log(den)
        nx, _ = _rms_n(attn)
        an_ref[...] = (nx * g_ref[...]).astype(BF)

    return _rowwise(body, name="attn_merge", nrows=S, tile=ROW_TILE,
                    row_ins=[(a, D_ATTN, 0) for a in (*os_, *ls_)], full_ins=[g_attn],
                    row_outs=[(D_ATTN, F32), (D_ATTN, F32), (D_ATTN, BF)])


def _conv_taps(x, w_ref):
    rows = lax.broadcasted_iota(jnp.int32, x.shape, 0)
    shifted = []
    for w in range(CONV_W):
        k = CONV_W - 1 - w
        shifted.append(x if k == 0 else jnp.where(rows >= k, pltpu.roll(x, k, 0), 0.0))
    acc = shifted[0] * w_ref[0:1, :]
    for w in range(1, CONV_W):
        acc = acc + shifted[w] * w_ref[w:w + 1, :]
    return acc, shifted


def _conv_fwd(proj, conv_w, conv_b):
    tc = 256

    def kern(x_ref, w_ref, b_ref, o_ref):
        c, _ = _conv_taps(x_ref[...], w_ref)
        c = c + b_ref[...]
        o_ref[...] = c * _sigmoid(c)

    return pl.pallas_call(
        kern, name="conv_fwd", grid=(D_CONV // tc,),
        in_specs=[pl.BlockSpec((S, tc), lambda c: (0, 4 * D_ATTN // tc + c)),
                  pl.BlockSpec((CONV_W, tc), lambda c: (0, c)), pl.BlockSpec((1, tc), lambda c: (0, c))],
        out_specs=pl.BlockSpec((S, tc), lambda c: (0, c)),
        out_shape=jax.ShapeDtypeStruct((S, D_CONV), F32), compiler_params=_params("parallel"))(proj, conv_w, conv_b)


def _softplus(x):
    return jnp.maximum(x, 0.0) + jnp.log1p(jnp.exp(-jnp.abs(x)))


def _ssd_prep(proj, dt_bias128, a_log128):
    def body(i, n, raw_ref, b_ref, al_ref, dt_ref, a_ref, carry):
        @pl.when(i == 0)
        def _():
            carry[...] = jnp.zeros_like(carry)

        dt = _softplus(raw_ref[...] + b_ref[...])
        da = dt * (-jnp.exp(al_ref[...]))
        tri = (lax.broadcasted_iota(jnp.int32, (BLK, BLK), 0) >= lax.broadcasted_iota(jnp.int32, (BLK, BLK), 1))
        cs = _dot3(tri.astype(BF), da) + carry[0:1, :]
        dt_ref[...] = dt
        a_ref[...] = cs
        carry[...] = jnp.broadcast_to(cs[BLK - 1:BLK, :], carry.shape)

    return _rowwise(body, name="ssd_prep", nrows=S, tile=BLK, row_ins=[(proj, LANES, DT_COLBLK)],
                    full_ins=[dt_bias128, a_log128], row_outs=[(LANES, F32), (LANES, F32)],
                    scratch=[pltpu.VMEM((8, LANES), F32)])


def _ssd_decay(a_col, at_row, causal):
    diff = jnp.where(causal, a_col - at_row, 0.0)
    return jnp.where(causal, jnp.exp(diff), 0.0)


SSD_CB = 2


def _ssd_fwd(xbc, a_b, dt_b, at_r, dt_r):
    def kern(c_ref, b_ref, x_ref, ab_ref, dtb_ref, at_ref, dt_ref, y_ref, hall_ref):
        half0 = lax.broadcasted_iota(jnp.int32, (BLK, LANES), 1) < HEAD
        causal = lax.broadcasted_iota(jnp.int32, (BLK, BLK), 1) <= lax.broadcasted_iota(jnp.int32, (BLK, BLK), 0)

        def step(i, carry):
            h, a_prev = carry
            for cc in range(SSD_CB):
                chunk = i * SSD_CB + cc
                rows = pl.ds(pl.multiple_of(chunk * BLK, BLK), BLK)
                ci, bi, x = c_ref[rows, :].astype(BF), b_ref[rows, :].astype(BF), x_ref[rows, :]
                ab = ab_ref[rows, :]
                a_end = ab[BLK - 1:BLK, :]
                alpha = jnp.exp(ab - a_prev)
                beta = jnp.exp(a_end - ab) * dtb_ref[rows, :]
                hall_ref[rows, :] = h
                cb = _dot(ci, bi, NT)
                at, dtj = at_ref[chunk], dt_ref[chunk]
                y = alpha * _dot(ci, h.astype(BF), NN)
                for hd in range(2):
                    hm = half0 if hd == 0 else jnp.logical_not(half0)
                    w = cb * _ssd_decay(ab[:, HEAD * hd:HEAD * hd + 1], at[hd:hd + 1, :], causal) * dtj[hd:hd + 1, :]
                    y = y + _dot(w.astype(BF), jnp.where(hm, x, 0.0).astype(BF), NN)
                y_ref[rows, :] = y
                h = alpha[BLK - 1:BLK, :] * h + _dot(bi, (beta * x).astype(BF), TN)
                a_prev = a_end
            return h, a_prev

        lax.fori_loop(0, NB // SSD_CB, step, (jnp.zeros((BLK, LANES), F32), jnp.zeros((1, LANES), F32)))

    npair = D_SSM // LANES
    rowvec = pl.BlockSpec((None, NB, 2, BLK), lambda p: (p, 0, 0, 0))
    seq = pl.BlockSpec((S, LANES), lambda p: (0, p))
    return pl.pallas_call(
        kern, name="ssd_fwd", grid=(npair,),
        in_specs=[pl.BlockSpec((S, LANES), lambda p: (0, 12 + p // 2)),
                  pl.BlockSpec((S, LANES), lambda p: (0, 8 + p // 2)), seq, seq, seq, rowvec, rowvec],
        out_specs=[seq, seq], out_shape=[jax.ShapeDtypeStruct((S, D_SSM), F32)] * 2,
        compiler_params=_params("parallel"))(xbc, xbc, xbc, a_b, dt_b, at_r, dt_r)


def _ssd_post(y_ssd, xbc, proj, dskip_b, g_ssm):
    def body(i, n, y_ref, xs_ref, z_ref, d_ref, g_ref, o_ref):
        z = z_ref[...]
        y2 = (y_ref[...] + d_ref[...] * xs_ref[...]) * (z * _sigmoid(z))
        nx, _ = _seg_rms_n(y2, 256)
        o_ref[...] = (nx * g_ref[...]).astype(BF)

    return _rowwise(body, name="ssd_post", nrows=S, tile=ROW_TILE,
                    row_ins=[(y_ssd, D_SSM, 0), (xbc, D_SSM, 0), (proj, D_SSM, 3)], full_ins=[dskip_b, g_ssm],
                    row_outs=[(D_SSM, BF)])[0]


def _cross_heads(q, kv_ref, gq, gk):
    out = []
    for h in range(D_CROSS // CROSS_HEAD):
        sl = slice(CROSS_HEAD * h, CROSS_HEAD * (h + 1))
        nq, rq = _rms_n(q[:, sl])
        nk, rk = _rms_n(kv_ref[:, sl])
        v = kv_ref[:, D_CROSS + CROSS_HEAD * h:D_CROSS + CROSS_HEAD * (h + 1)]
        out.append((sl, nq, rq, nk, rk, v))
    return out


def _cross_fwd(qc, kv, g_cq, g_ck):
    scale = CROSS_HEAD ** -0.5

    def body(i, n, q_ref, kv_ref, gq_ref, gk_ref, o_ref):
        for sl, nq, _, nk, _, v in _cross_heads(q_ref[...], kv_ref, gq_ref[...], gk_ref[...]):
            qn = (nq * gq_ref[...] * scale).astype(BF)
            kn = (nk * gk_ref[...]).astype(BF)
            s = _dot(qn, kn, NT)
            e = jnp.exp(s - jnp.max(s, axis=-1, keepdims=True))
            p = e / jnp.sum(e, axis=-1, keepdims=True)
            o_ref[:, sl] = _dot(p.astype(BF), v.astype(BF), NN).astype(BF)

    return _rowwise(body, name="cross_fwd", nrows=S, tile=ROW_TILE, row_ins=[(qc, D_CROSS, 0)],
                    full_ins=[kv, g_cq, g_ck], row_outs=[(D_CROSS, BF)])[0]


def _loss_sum(dy):
    def body(i, n, d_ref, o_ref):
        d = d_ref[...]
        s = jnp.sum(jnp.sum(d * d, axis=0, keepdims=True), axis=1, keepdims=True)
        _accum(o_ref, s, i)

    return _rowwise(body, name="loss_sum", nrows=S, tile=ROW_TILE, row_ins=[(dy, D, 0)],
                    acc_outs=[((1, 1), F32)])[0]


def _rms_bwd_call(x, g, dh, dres, name):
    rows = x.shape[0]
    has_res = dres is not None

    def body(i, n, *refs):
        if has_res:
            x_ref, dh_ref, dr_ref, g_ref, dx_ref, dxb_ref, dg_ref = refs
        else:
            x_ref, dh_ref, g_ref, dg_ref = refs
        dx, dg = _rms_bwd(x_ref[...], g_ref[...], dh_ref[...])
        if has_res:
            dx = dx + dr_ref[...]
            dx_ref[...] = dx
            dxb_ref[...] = dx.astype(BF)
        _accum(dg_ref, dg, i)

    row_ins = [(x, D, 0), (dh, D, 0)] + ([(dres, D, 0)] if has_res else [])
    return _rowwise(body, name=name, nrows=rows, tile=min(ROW_TILE, rows), row_ins=row_ins, full_ins=[g],
                    row_outs=[(D, F32), (D, BF)] if has_res else [], acc_outs=[((1, D), F32)])


def _cross_bwd(qc, doc, kv, g_cq, g_ck):
    scale = CROSS_HEAD ** -0.5

    def body(i, n, q_ref, do_ref, kv_ref, gq_ref, gk_ref, dq_ref, dkn_ref, dv_ref, dgq_ref):
        dgq = jnp.zeros((1, CROSS_HEAD), F32)
        dkn_parts, dv_parts = [], []
        for sl, nq, rq, nk, _, v in _cross_heads(q_ref[...], kv_ref, gq_ref[...], gk_ref[...]):
            qn = (nq * gq_ref[...] * scale).astype(BF)
            kn = (nk * gk_ref[...]).astype(BF)
            s = _dot(qn, kn, NT)
            e = jnp.exp(s - jnp.max(s, axis=-1, keepdims=True))
            p = e / jnp.sum(e, axis=-1, keepdims=True)
            do = do_ref[:, sl].astype(BF)
            dv_parts.append(_dot(p.astype(BF), do, TN))
            dp = _dot(do, v.astype(BF), NT)
            ds = (p * (dp - jnp.sum(dp * p, axis=-1, keepdims=True))).astype(BF)
            dqn = _dot(ds, kn, NN)
            dkn_parts.append(_dot(ds, qn, TN))
            dn = dqn * (gq_ref[...] * scale)
            dq_ref[:, sl] = (rq * (dn - nq * jnp.mean(dn * nq, axis=-1, keepdims=True))).astype(BF)
            dgq = dgq + jnp.sum(dqn * scale * nq, axis=0, keepdims=True)
        _accum(dkn_ref, jnp.concatenate(dkn_parts, axis=1), i)
        _accum(dv_ref, jnp.concatenate(dv_parts, axis=1), i)
        _accum(dgq_ref, dgq, i)

    return _rowwise(body, name="cross_bwd", nrows=S, tile=ROW_TILE, row_ins=[(qc, D_CROSS, 0), (doc, D_CROSS, 0)],
                    full_ins=[kv, g_cq, g_ck], row_outs=[(D_CROSS, BF)],
                    acc_outs=[((N_MEM, D_CROSS), F32), ((N_MEM, D_CROSS), F32), ((1, CROSS_HEAD), F32)])


def _cross_kv_bwd(kv, dkn, dv, g_ck):
    def body(i, n, kv_ref, dkn_ref, dv_ref, gk_ref, dkv_ref, dgk_ref):
        dgk = jnp.zeros((1, CROSS_HEAD), F32)
        for h in range(D_CROSS // CROSS_HEAD):
            sl = slice(CROSS_HEAD * h, CROSS_HEAD * (h + 1))
            dk, dg = _rms_bwd(kv_ref[:, sl], gk_ref[...], dkn_ref[:, sl])
            dkv_ref[:, sl] = dk.astype(BF)
            dgk = dgk + dg
        dkv_ref[:, D_CROSS:] = dv_ref[...].astype(BF)
        dgk_ref[...] = dgk

    return _rowwise(body, name="cross_kv_bwd", nrows=N_MEM, tile=N_MEM,
                    row_ins=[(kv, 2 * D_CROSS, 0), (dkn, D_CROSS, 0), (dv, D_CROSS, 0)], full_ins=[g_ck],
                    row_outs=[(2 * D_CROSS, BF)], acc_outs=[((1, CROSS_HEAD), F32)])


def _attn_merge_bwd(dmix, attn, g_attn):
    def body(i, n, dn_ref, a_ref, g_ref, da_ref, dl_ref, dg_ref):
        attn_v = a_ref[...]
        da, dg = _rms_bwd(attn_v, g_ref[...], dn_ref[...])
        da_ref[...] = da
        dl_ref[...] = _seg_sum(da * attn_v, HEAD)
        _accum(dg_ref, dg, i)

    return _rowwise(body, name="attn_merge_bwd", nrows=S, tile=ROW_TILE,
                    row_ins=[(dmix, D_ATTN, 0), (attn, D_ATTN, 0)], full_ins=[g_attn],
                    row_outs=[(D_ATTN, F32), (D_ATTN, F32)], acc_outs=[((1, D_ATTN), F32)])


def _attn_bwd(q, k, v, do, lse, delta, d, name):
    def kern(q_ref, k_ref, v_ref, do_ref, l_ref, d_ref, dq_ref, dk_ref, dv_ref):
        dk_ref[...] = jnp.zeros_like(dk_ref)
        dv_ref[...] = jnp.zeros_like(dv_ref)
        half0 = lax.broadcasted_iota(jnp.int32, (BLK, LANES), 1) < HEAD

        def group(g, carry):
            updates = []
            for bb in range(ATTN_QB):
                q_rows, k_rows, valid = _band(g * ATTN_QB + bb, d)
                q, do = q_ref[q_rows, :], do_ref[q_rows, :]
                lse_t, del_t = l_ref[q_rows, :], d_ref[q_rows, :]
                kb, vb = k_ref[k_rows, :].astype(BF), v_ref[k_rows, :].astype(BF)
                q2 = jnp.concatenate([jnp.where(half0, q, 0.0), jnp.where(half0, 0.0, q)], axis=0).astype(BF)
                do2 = jnp.concatenate([jnp.where(half0, do, 0.0), jnp.where(half0, 0.0, do)], axis=0).astype(BF)
                lse2 = jnp.concatenate([lse_t[:, 0:1], lse_t[:, HEAD:HEAD + 1]], axis=0)
                del2 = jnp.concatenate([del_t[:, 0:1], del_t[:, HEAD:HEAD + 1]], axis=0)
                s = jnp.where(valid, _dot(q2, kb, NT), NEG)
                p = jnp.exp(s - lse2)
                ds = (p * (_dot(do2, vb, NT) - del2)).astype(BF)
                dq2 = _dot(ds, kb, NN)
                dq_ref[q_rows, :] = jnp.where(half0, dq2[:BLK], dq2[BLK:])
                updates.append((k_rows, _dot(ds, q2, TN), _dot(p.astype(BF), do2, TN)))
            for k_rows, dk, dv in updates:
                dk_ref[k_rows, :] += dk
                dv_ref[k_rows, :] += dv
            return carry

        lax.fori_loop(0, NB // ATTN_QB, group, 0)

    seq = pl.BlockSpec((S, LANES), lambda p: (0, p))
    return pl.pallas_call(
        kern, name=name, grid=(D_ATTN // LANES,), in_specs=[seq, seq, V_COLS, seq, seq, seq],
        out_specs=[seq, seq, seq], out_shape=[jax.ShapeDtypeStruct((S, D_ATTN), F32)] * 3,
        compiler_params=_params("parallel"))(q, k, v, do, lse, delta)


def _qk_bwd(dqs, dks, dvs, proj, pos_col, gq128, gk128, inv_tab):
    scale = HEAD ** -0.5

    def body(i, n, *refs):
        dq_refs, dk_refs, dv_refs = refs[0:3], refs[3:6], refs[6:9]
        q_ref, k_ref, p_ref, gq_ref, gk_ref, it_ref = refs[9:15]
        dqo_ref, dko_ref, dvo_ref, dgq_ref, dgk_ref = refs[15:20]
        t = q_ref.shape[0]
        c, sg, lo = _rope_tables(p_ref[...], it_ref[...], t)
        dgq = jnp.zeros((1, LANES), F32)
        dgk = jnp.zeros((1, LANES), F32)
        for j in range(D_ATTN // LANES):
            sl = slice(LANES * j, LANES * (j + 1))
            dqr = _rope(dq_refs[0][:, sl] + dq_refs[1][:, sl] + dq_refs[2][:, sl], c, -sg, lo) * scale
            dkr = _rope(dk_refs[0][:, sl] + dk_refs[1][:, sl] + dk_refs[2][:, sl], c, -sg, lo)
            dq, gq = _seg_rms_bwd(q_ref[:, sl], gq_ref[...], dqr, HEAD)
            dk, gk = _seg_rms_bwd(k_ref[:, sl], gk_ref[...], dkr, HEAD)
            dqo_ref[:, sl] = dq.astype(BF)
            dko_ref[:, sl] = dk.astype(BF)
            dgq, dgk = dgq + gq, dgk + gk
        dvo_ref[...] = (dv_refs[0][...] + dv_refs[1][...] + dv_refs[2][...]).astype(BF)
        _accum(dgq_ref, _fold_heads(dgq), i)
        _accum(dgk_ref, _fold_heads(dgk), i)

    return _rowwise(body, name="qk_bwd", nrows=S, tile=ROW_TILE,
                    row_ins=[(a, D_ATTN, 0) for a in (*dqs, *dks, *dvs)]
                    + [(proj, D_ATTN, 0), (proj, D_ATTN, 1), (pos_col, 1, 0)],
                    full_ins=[gq128, gk128, inv_tab], row_outs=[(D_ATTN, BF)] * 3,
                    acc_outs=[((1, LANES), F32)] * 2)


def _ssd_post_bwd(y_ssd, xbc, proj, dmix, dskip_b, g_ssm):
    def body(i, n, y_ref, xs_ref, z_ref, do_ref, d_ref, g_ref, dy_ref, dz_ref, dxs_ref, dd_ref, dg_ref):
        z, xs = z_ref[...], xs_ref[...]
        sg = _sigmoid(z)
        gate = z * sg
        y = y_ref[...] + d_ref[...] * xs
        dy2, dg = _seg_rms_bwd(y * gate, g_ref[...], do_ref[...], 256)
        dy = dy2 * gate
        dy_ref[...] = dy.astype(BF)
        dz_ref[...] = (dy2 * y * (sg * (1.0 + z * (1.0 - sg)))).astype(BF)
        dxs_ref[...] = dy * d_ref[...]
        _accum(dd_ref, jnp.sum(dy * xs, axis=0, keepdims=True), i)
        _accum(dg_ref, dg, i)

    return _rowwise(body, name="ssd_post_bwd", nrows=S, tile=ROW_TILE,
                    row_ins=[(y_ssd, D_SSM, 0), (xbc, D_SSM, 0), (proj, D_SSM, 3), (dmix, D_SSM, 1)],
                    full_ins=[dskip_b, g_ssm], row_outs=[(D_SSM, BF), (D_SSM, BF), (D_SSM, F32)],
                    acc_outs=[((1, D_SSM), F32), ((1, D_SSM), F32)])


def _ssd_bwd(xbc, dy, h_all, a_b, dt_b, at_r, dt_r):
    def kern(c_ref, b_ref, x_ref, dy_ref, hall_ref, ab_ref, dtb_ref, at_ref, dt_ref,
             dc_ref, daq_ref, dx_ref, db_ref, dak_ref, ddt_ref, ddtb_ref):
        half0 = lax.broadcasted_iota(jnp.int32, (BLK, LANES), 1) < HEAD
        hms = (half0, jnp.logical_not(half0))
        causal = lax.broadcasted_iota(jnp.int32, (BLK, BLK), 1) <= lax.broadcasted_iota(jnp.int32, (BLK, BLK), 0)
        row = lax.broadcasted_iota(jnp.int32, (BLK, LANES), 0)

        def step(t, carry_in):
            dh_next, carry = carry_in
            for cc in reversed(range(SSD_CB)):
                chunk = (NB // SSD_CB - 1 - t) * SSD_CB + cc
                rows = pl.ds(pl.multiple_of(chunk * BLK, BLK), BLK)
                ci, bi, x = c_ref[rows, :].astype(BF), b_ref[rows, :].astype(BF), x_ref[rows, :]
                dyv = dy_ref[rows, :].astype(F32)
                ab, dtb = ab_ref[rows, :], dtb_ref[rows, :]
                before = ab_ref[pl.ds(pl.multiple_of(jnp.maximum(chunk * BLK - 8, 0), 8), 8), :][7:8, :]
                a_prev = jnp.where(chunk == 0, 0.0, before)
                a_end = ab[BLK - 1:BLK, :]
                alpha = jnp.exp(ab - a_prev)
                e_end = jnp.exp(a_end - ab)
                beta = e_end * dtb
                t_all = alpha[BLK - 1:BLK, :]
                hc = hall_ref[rows, :]
                hcb, dhb = hc.astype(BF), dh_next.astype(BF)

                cb = _dot(ci, bi, NT)
                at, dtj = at_ref[chunk], dt_ref[chunk]
                dcb = jnp.zeros((BLK, BLK), F32)
                dx = jnp.zeros((BLK, LANES), F32)
                rsum = []
                for hd in range(2):
                    dym = jnp.where(hms[hd], dyv, 0.0).astype(BF)
                    lm = _ssd_decay(ab[:, HEAD * hd:HEAD * hd + 1], at[hd:hd + 1, :], causal)
                    dth = dtj[hd:hd + 1, :]
                    dw = _dot(dym, jnp.where(hms[hd], x, 0.0).astype(BF), NT)
                    g = dw * cb * lm
                    dx = dx + _dot((cb * lm * dth).astype(BF), dym, TN)
                    gcol = jnp.sum(g, axis=0, keepdims=True)
                    ddt_ref[chunk, hd:hd + 1, :] = gcol
                    dak_ref[chunk, hd:hd + 1, :] = gcol * dth
                    rsum.append(jnp.sum(g * dth, axis=1, keepdims=True))
                    dcb = dcb + dw * lm * dth
                dcb = dcb.astype(BF)

                g1 = (alpha * dyv).astype(BF)
                dalpha = dyv * _dot(ci, hcb, NN)
                g2 = _dot(bi, dhb, NN)
                dbeta = x * g2
                bx = (beta * x).astype(BF)
                dc_ref[rows, :] = _dot(dcb, bi, NN) + _dot(g1, hcb, NT)
                db_ref[rows, :] = _dot(dcb, ci, TN) + _dot(bx, dhb, NT)
                dx_ref[rows, :] = dx + beta * g2
                ddtb_ref[rows, :] = _seg_sum(e_end * dbeta, HEAD)
                ada, bdb = alpha * dalpha, beta * dbeta
                t_dt = t_all * jnp.sum(dh_next * hc, axis=0, keepdims=True)
                end_term = _seg_sum(jnp.broadcast_to(jnp.sum(bdb, axis=0, keepdims=True) + t_dt, (8, LANES)), HEAD)[0:1]
                prev_term = _seg_sum(jnp.broadcast_to(jnp.sum(ada, axis=0, keepdims=True) + t_dt, (8, LANES)), HEAD)[0:1]
                daq = jnp.where(half0, rsum[0], rsum[1]) + _seg_sum(ada - bdb, HEAD)
                daq_ref[rows, :] = daq + jnp.where(row == BLK - 1, end_term - carry, 0.0)
                carry = prev_term
                dh_next = t_all * dh_next + _dot(ci, g1, TN)
            return dh_next, carry

        lax.fori_loop(0, NB // SSD_CB, step, (jnp.zeros((BLK, LANES), F32), jnp.zeros((1, LANES), F32)))

    npair = D_SSM // LANES
    rowvec = pl.BlockSpec((None, NB, 2, BLK), lambda p: (p, 0, 0, 0))
    seq = pl.BlockSpec((S, LANES), lambda p: (0, p))
    return pl.pallas_call(
        kern, name="ssd_bwd", grid=(npair,),
        in_specs=[pl.BlockSpec((S, LANES), lambda p: (0, 12 + p // 2)),
                  pl.BlockSpec((S, LANES), lambda p: (0, 8 + p // 2)),
                  seq, seq, seq, seq, seq, rowvec, rowvec],
        out_specs=[seq, seq, seq, seq, rowvec, rowvec, seq],
        out_shape=[jax.ShapeDtypeStruct((S, D_SSM), F32)] * 4
        + [jax.ShapeDtypeStruct((npair, NB, 2, BLK), F32)] * 2 + [jax.ShapeDtypeStruct((S, D_SSM), F32)],
        compiler_params=_params("parallel"))(xbc, xbc, xbc, dy, h_all, a_b, dt_b, at_r, dt_r)


def _ssd_prep_bwd(daq, dak, ddt_row, ddt_lane, dt, proj, dt_bias128, a_log128):
    def body(i, n, daq_ref, dak_ref, dd_ref, dd2_ref, dt_ref, raw_ref, b_ref, al_ref, draw_ref, dal_ref, db_ref,
             carry):
        @pl.when(i == 0)
        def _():
            carry[...] = jnp.zeros_like(carry)

        a = -jnp.exp(al_ref[...])
        tri = (lax.broadcasted_iota(jnp.int32, (BLK, BLK), 0) <= lax.broadcasted_iota(jnp.int32, (BLK, BLK), 1))
        rev = _dot3(tri.astype(BF), daq_ref[...] - dak_ref[...]) + carry[0:1, :]
        carry[...] = jnp.broadcast_to(rev[0:1, :], carry.shape)
        dtv = dt_ref[...]
        draw = (dd_ref[...] + dd2_ref[...] + a * rev) * _sigmoid(raw_ref[...] + b_ref[...])
        draw_ref[...] = draw.astype(BF)
        _accum(dal_ref, jnp.sum(dtv * rev, axis=0, keepdims=True) * a, i)
        _accum(db_ref, jnp.sum(draw, axis=0, keepdims=True), i)

    return _rowwise(body, name="ssd_prep_bwd", nrows=S, tile=BLK, reverse=True,
                    row_ins=[(daq, LANES, 0), (dak, LANES, 0), (ddt_row, LANES, 0), (ddt_lane, LANES, 0), (dt, LANES, 0),
                             (proj, LANES, DT_COLBLK)],
                    full_ins=[dt_bias128, a_log128], row_outs=[(LANES, BF)],
                    acc_outs=[((1, LANES), F32), ((1, LANES), F32)], scratch=[pltpu.VMEM((8, LANES), F32)])


def _conv_bwd(proj, conv_w, conv_b, d1, d2, map1, map2, col0, ntiles, name):
    base = (4 * D_ATTN + col0) // LANES

    def kern(x_ref, w_ref, b_ref, d1_ref, d2_ref, dx_ref, dw_ref, db_ref):
        x = x_ref[...]
        c, shifted = _conv_taps(x, w_ref)
        c = c + b_ref[...]
        sg = _sigmoid(c)
        dc = (d1_ref[...] + d2_ref[...]) * (sg * (1.0 + c * (1.0 - sg)))
        rows = lax.broadcasted_iota(jnp.int32, x.shape, 0)
        dx = jnp.zeros_like(x)
        for w in range(CONV_W):
            k = CONV_W - 1 - w
            up = dc if k == 0 else jnp.where(rows < S - k, pltpu.roll(dc, S - k, 0), 0.0)
            dx = dx + up * w_ref[w:w + 1, :]
            dw_ref[w:w + 1, :] = jnp.sum(dc * shifted[w], axis=0, keepdims=True)
        dx_ref[...] = dx.astype(BF)
        db_ref[...] = jnp.sum(dc, axis=0, keepdims=True)

    c0 = col0 // LANES
    return pl.pallas_call(
        kern, name=name, grid=(ntiles,),
        in_specs=[pl.BlockSpec((S, LANES), lambda c: (0, base + c)),
                  pl.BlockSpec((CONV_W, LANES), lambda c: (0, c0 + c)),
                  pl.BlockSpec((1, LANES), lambda c: (0, c0 + c)),
                  pl.BlockSpec((S, LANES), lambda c: (0, map1(c))),
                  pl.BlockSpec((S, LANES), lambda c: (0, map2(c)))],
        out_specs=[pl.BlockSpec((S, LANES), lambda c: (0, c)), pl.BlockSpec((CONV_W, LANES), lambda c: (0, c)),
                   pl.BlockSpec((1, LANES), lambda c: (0, c))],
        out_shape=[jax.ShapeDtypeStruct((S, ntiles * LANES), BF), jax.ShapeDtypeStruct((CONV_W, ntiles * LANES), F32),
                   jax.ShapeDtypeStruct((1, ntiles * LANES), F32)],
        compiler_params=_params("parallel"))(proj, conv_w, conv_b, d1, d2)


def _adamw(w, m, v, parts, name):
    r, c = w.shape
    n_parts = parts.shape[0]
    tile = r if r <= 512 else (128 if c > 1024 else 256)
    assert r % tile == 0
    c1 = 1.0 - ADAM_B1 ** ADAM_STEP
    c2 = 1.0 - ADAM_B2 ** ADAM_STEP

    def kern(w_ref, m_ref, v_ref, p_ref, g_ref, d_ref, mo_ref, vo_ref):
        g = p_ref[0].astype(F32)
        for k in range(1, n_parts):
            g = g + p_ref[k].astype(F32)
        mn = ADAM_B1 * m_ref[...] + (1.0 - ADAM_B1) * g
        vn = ADAM_B2 * v_ref[...] + (1.0 - ADAM_B2) * (g * g)
        g_ref[...] = g
        mo_ref[...] = mn
        vo_ref[...] = vn
        d_ref[...] = -ADAM_LR * ((mn / c1) / (jnp.sqrt(vn / c2) + ADAM_EPS) + ADAM_WD * w_ref[...])

    blk = pl.BlockSpec((tile, c), lambda i: (i, 0))
    return pl.pallas_call(
        kern, name=name, grid=(r // tile,),
        in_specs=[blk, blk, blk, pl.BlockSpec((n_parts, tile, c), lambda i: (0, i, 0))],
        out_specs=[blk] * 4, out_shape=[jax.ShapeDtypeStruct((r, c), F32)] * 4,
        compiler_params=_params("parallel"))(w, m, v, parts)


MESH = pl.DeviceIdType.MESH
ANY = pl.BlockSpec(memory_space=pl.ANY)


def _put_own(gathered, shard):
    me = 4 * lax.axis_index("x") + 2 * lax.axis_index("y") + lax.axis_index("c")
    return lax.dynamic_update_slice(gathered, shard[None], (me,) + (0,) * shard.ndim)


def _all_gather(arrs, name, after=None):
    na = len(arrs)
    n_after = 0 if after is None else 1

    def kern(*refs):
        ins, outs = refs[:na], refs[na + n_after:2 * na + n_after]
        send_sems, recv_sems = refs[2 * na + n_after:]
        x, y, c = lax.axis_index("x"), lax.axis_index("y"), lax.axis_index("c")
        me, sibling = (x, y, c), (x, y, 1 - c)
        chips = [(1 - x, y), (x, 1 - y), (1 - x, 1 - y)]

        def copy(a, k, block, to, src=None):
            px, py, pc = block
            dst = outs[a].at[4 * px + 2 * py + pc]
            return pltpu.make_async_remote_copy(
                src_ref=dst if src is None else src, dst_ref=dst, send_sem=send_sems.at[a, k],
                recv_sem=recv_sems.at[a, k], device_id=to, device_id_type=MESH)

        first = []
        for a in range(na):
            first.append(copy(a, 0, me, sibling, src=ins[a]))
            first += [copy(a, 1 + j, me, (*chip, c), src=ins[a]) for j, chip in enumerate(chips)]
        for cp in first:
            cp.start()
        passed = []
        for a in range(na):
            for j, chip in enumerate(chips):
                copy(a, 1 + j, (*chip, c), me).wait_recv()
                fwd = copy(a, 4 + j, (*chip, c), sibling)
                fwd.start()
                passed.append(fwd)
        for a in range(na):
            copy(a, 0, sibling, me).wait_recv()
            for j, chip in enumerate(chips):
                copy(a, 4 + j, (*chip, 1 - c), me).wait_recv()
        for cp in first + passed:
            cp.wait_send()

    outs = pl.pallas_call(
        kern, name=name, in_specs=[ANY] * (na + n_after), out_specs=[ANY] * na,
        out_shape=[jax.ShapeDtypeStruct((N_DEV,) + a.shape, a.dtype) for a in arrs],
        scratch_shapes=[pltpu.SemaphoreType.DMA((na, 7)), pltpu.SemaphoreType.DMA((na, 7))],
    )(*arrs, *([] if after is None else [after]))
    return [_put_own(o, a) for o, a in zip(outs, arrs)]


HBM = pl.BlockSpec(memory_space=pltpu.HBM)
SEM = pl.BlockSpec(memory_space=pltpu.SEMAPHORE)
EFFECT = pltpu.SideEffectType.DATAFLOW_SIDE_EFFECTING
N_CHIP = 4
N_COPIES = {"gather": 3, "scatter": 3, "forward": 4, "pair": 4}


def _in_hbm(a):
    return pltpu.with_memory_space_constraint(a, pltpu.HBM)


def _chip_copies(kind, src_refs, land_refs, send_sems, recv_sems):
    x, y, c = lax.axis_index("x"), lax.axis_index("y"), lax.axis_index("c")
    chips = [(1 - x, y), (x, 1 - y), (1 - x, 1 - y)]
    n = N_COPIES[kind]
    cps = []

    def add(a, j, src, dst, to):
        cps.append(pltpu.make_async_remote_copy(
            src_ref=src, dst_ref=dst, send_sem=send_sems.at[n * a + j], recv_sem=recv_sems.at[n * a + j],
            device_id=to, device_id_type=MESH))

    for a in range(len(src_refs)):
        if kind == "gather":
            for j, (px, py) in enumerate(chips):
                add(a, j, src_refs[a], land_refs[a].at[4 * x + 2 * y + c], (px, py, c))
        elif kind == "scatter":
            for j, (px, py) in enumerate(chips):
                add(a, j, src_refs[a].at[2 * px + py], land_refs[a].at[2 * x + y], (px, py, c))
        elif kind == "forward":
            add(a, 0, src_refs[a], land_refs[a].at[4 * x + 2 * y + c], (x, y, 1 - c))
            for j, (px, py) in enumerate(chips):
                slot = land_refs[a].at[4 * px + 2 * py + c]
                add(a, 1 + j, slot, slot, (x, y, 1 - c))
        else:
            for q in range(N_CHIP):
                add(a, q, src_refs[a].at[2 * q + 1 - c], land_refs[a].at[q], (x, y, 1 - c))
    return cps


def _exchange_start(kind, srcs, lands, after, name):
    na = len(srcs)
    n_after = 0 if after is None else 1

    def kern(*refs):
        src_refs, land_refs = refs[:na], refs[na:2 * na]
        send_sems, recv_sems = refs[2 * na + n_after], refs[2 * na + n_after + 1]
        token = refs[-1]
        for cp in _chip_copies(kind, src_refs, land_refs, send_sems, recv_sems):
            cp.start()
        token[...] = jnp.zeros_like(token)

    bufs = list(srcs) + list(lands)
    res = pl.pallas_call(
        kern, name=name,
        out_shape=(pltpu.SemaphoreType.DMA((N_COPIES[kind] * na,)), pltpu.SemaphoreType.DMA((N_COPIES[kind] * na,)))
        + tuple(pltpu.HBM(b.shape, b.dtype) for b in bufs) + (jax.ShapeDtypeStruct((8, LANES), F32),),
        in_specs=[HBM] * (2 * na) + [ANY] * n_after,
        out_specs=(SEM, SEM) + (HBM,) * (2 * na) + (pl.BlockSpec(memory_space=pltpu.VMEM),),
        input_output_aliases={i: 2 + i for i in range(2 * na)},
        compiler_params=pltpu.CompilerParams(has_side_effects=EFFECT),
    )(*[_in_hbm(b) for b in bufs], *([] if after is None else [after]))
    return (res[0], res[1]), list(res[2:2 + na]), list(res[2 + na:2 + 2 * na]), res[-1]


def _exchange_wait(kind, sems, srcs, lands, after, name):
    na = len(srcs)

    def kern(*refs):
        src_refs, land_refs = refs[:na], refs[na:2 * na]
        send_sems, recv_sems = refs[2 * na], refs[2 * na + 1]
        for cp in _chip_copies(kind, src_refs, land_refs, send_sems, recv_sems):
            cp.wait_send()
            cp.wait_recv()

    bufs = list(srcs) + list(lands)
    res = pl.pallas_call(
        kern, name=name, out_shape=tuple(pltpu.HBM(b.shape, b.dtype) for b in bufs),
        in_specs=[HBM] * (2 * na) + [SEM, SEM, ANY], out_specs=(HBM,) * (2 * na),
        input_output_aliases={i: i for i in range(2 * na)},
        compiler_params=pltpu.CompilerParams(has_side_effects=EFFECT),
    )(*bufs, sems[0], sems[1], after)
    return list(res[:na]), list(res[na:])


def _gather_finish(shards, lands, name):
    na = len(shards)

    def kern(*refs):
        ins, outs = refs[:na], refs[2 * na:3 * na]
        send_sems, recv_sems = refs[3 * na:]
        x, y, c = lax.axis_index("x"), lax.axis_index("y"), lax.axis_index("c")
        chips = [(1 - x, y), (x, 1 - y), (1 - x, 1 - y)]

        def copy(a, k, slot, pc, src=None):
            dst = outs[a].at[slot + pc]
            return pltpu.make_async_remote_copy(
                src_ref=dst if src is None else src, dst_ref=dst, send_sem=send_sems.at[a, k],
                recv_sem=recv_sems.at[a, k], device_id=(x, y, 1 - c), device_id_type=MESH)

        sends = []
        for a in range(na):
            sends.append(copy(a, 0, 4 * x + 2 * y, c, src=ins[a]))
            sends += [copy(a, 1 + j, 4 * px + 2 * py, c) for j, (px, py) in enumerate(chips)]
        for cp in sends:
            cp.start()
        for a in range(na):
            copy(a, 0, 4 * x + 2 * y, 1 - c).wait_recv()
            for j, (px, py) in enumerate(chips):
                copy(a, 1 + j, 4 * px + 2 * py, 1 - c).wait_recv()
        for cp in sends:
            cp.wait_send()

    return pl.pallas_call(
        kern, name=name, in_specs=[ANY] * (2 * na), out_specs=[ANY] * na,
        out_shape=[jax.ShapeDtypeStruct(l.shape, l.dtype) for l in lands],
        input_output_aliases={na + a: a for a in range(na)},
        scratch_shapes=[pltpu.SemaphoreType.DMA((na, 4)), pltpu.SemaphoreType.DMA((na, 4))])(*shards, *lands)


def _pair_exchange(parts, name):
    na = len(parts)

    def kern(*refs):
        ins, got = refs[:na], refs[na:2 * na]
        send_sems, recv_sems = refs[2 * na:]
        x, y, c = lax.axis_index("x"), lax.axis_index("y"), lax.axis_index("c")
        sends = []
        for a in range(na):
            for q in range(N_CHIP):
                sends.append(pltpu.make_async_remote_copy(
                    src_ref=ins[a].at[2 * q + 1 - c], dst_ref=got[a].at[q], send_sem=send_sems.at[a, q],
                    recv_sem=recv_sems.at[a, q], device_id=(x, y, 1 - c), device_id_type=MESH))
        for cp in sends:
            cp.start()
        for cp in sends:
            cp.wait()

    return pl.pallas_call(
        kern, name=name, in_specs=[ANY] * na, out_specs=[ANY] * na,
        out_shape=[jax.ShapeDtypeStruct((N_CHIP,) + p.shape[1:], p.dtype) for p in parts],
        scratch_shapes=[pltpu.SemaphoreType.DMA((na, N_CHIP)), pltpu.SemaphoreType.DMA((na, N_CHIP))])(*parts)


def _pair_sum(parts, got, name):
    _, r, cdim = parts.shape
    tile = min(r, ROW_TILE)
    x, y, c = lax.axis_index("x"), lax.axis_index("y"), lax.axis_index("c")
    where = jnp.stack([c, 2 * x + y]).astype(jnp.int32)

    def kern(w_ref, a_ref, b_ref, q_ref, own_ref):
        s = (a_ref[...].astype(F32) + b_ref[...].astype(F32)).astype(BF)
        q_ref[...] = s

        @pl.when(pl.program_id(1) == w_ref[1])
        def _():
            own_ref[...] = s

    blk = pl.BlockSpec((None, tile, cdim), lambda i, q, w_ref: (q, i, 0))
    sums, own = pl.pallas_call(
        kern, name=name,
        out_shape=[jax.ShapeDtypeStruct((N_CHIP, r, cdim), BF), jax.ShapeDtypeStruct((r, cdim), BF)],
        grid_spec=pltpu.PrefetchScalarGridSpec(
            num_scalar_prefetch=1, grid=(r // tile, N_CHIP),
            in_specs=[pl.BlockSpec((None, tile, cdim), lambda i, q, w_ref: (2 * q + w_ref[0], i, 0)), blk],
            out_specs=[blk, pl.BlockSpec((tile, cdim), lambda i, q, w_ref: (i, 0))]),
        compiler_params=_params("parallel", "arbitrary"))(where, parts, got)
    land = lax.dynamic_update_slice(lax.empty((N_CHIP, r, cdim), BF), own[None], (2 * x + y, 0, 0))
    return sums, land


W_IN_COLS = D_IN // N_DEV


def _w_in_from_blocks(w8):
    def kern(x_ref, o_ref):
        for j in range(N_DEV):
            o_ref[:, W_IN_COLS * j:W_IN_COLS * (j + 1)] = x_ref[j]
        o_ref[:, D_IN:] = jnp.zeros((ROW_TILE, D_INP - D_IN), o_ref.dtype)

    return pl.pallas_call(
        kern, name="w_in_from_blocks", grid=(D // ROW_TILE,),
        in_specs=[pl.BlockSpec((N_DEV, ROW_TILE, W_IN_COLS), lambda i: (0, i, 0))],
        out_specs=pl.BlockSpec((ROW_TILE, D_INP), lambda i: (i, 0)),
        out_shape=jax.ShapeDtypeStruct((D, D_INP), w8.dtype), compiler_params=_params("parallel"))(w8)


def _w_in_to_blocks(g):
    def kern(x_ref, o_ref):
        for j in range(N_DEV):
            o_ref[j] = x_ref[:, W_IN_COLS * j:W_IN_COLS * (j + 1)]

    return pl.pallas_call(
        kern, name="w_in_to_blocks", grid=(D // ROW_TILE,),
        in_specs=[pl.BlockSpec((ROW_TILE, D_INP), lambda i: (i, 0))],
        out_specs=pl.BlockSpec((N_DEV, ROW_TILE, W_IN_COLS), lambda i: (0, i, 0)),
        out_shape=jax.ShapeDtypeStruct((N_DEV, D, W_IN_COLS), g.dtype), compiler_params=_params("parallel"))(g)


def _pad_lanes(v, width=LANES):
    return jnp.pad(v, ((0, 0), (0, width - v.shape[1])))


def _row_layout(t16):
    return t16.T.reshape(N_HEADS // 2, 2, NB, BLK).transpose(0, 2, 1, 3)


def _row_layout_inv(r):
    return r.transpose(0, 2, 1, 3).reshape(N_HEADS, S).T


SMALL = (("g_mix", D), ("g_q", HEAD), ("g_k", HEAD), ("g_attn_out", D_ATTN), ("conv_b", D_CONV),
         ("dt_bias", 16), ("a_log", 16), ("d_skip", 16), ("g_ssm_out", D_SSM), ("g_cross", D),
         ("g_mem", D), ("g_cq", CROSS_HEAD), ("g_ck", CROSS_HEAD), ("g_mlp", D))
SMALL_ROWS = -(-sum(-(-w // LANES) for _, w in SMALL) // 8) * 8


def _pack_small(vals):
    rows = [_pad_lanes(vals[n], -(-w // LANES) * LANES).reshape(-1, LANES) for n, w in SMALL]
    packed = jnp.concatenate(rows, axis=0)
    return jnp.pad(packed, ((0, SMALL_ROWS - packed.shape[0]), (0, 0)))


def _unpack_small(packed):
    out, r = {}, 0
    for n, w in SMALL:
        nr = -(-w // LANES)
        out[n] = packed[r:r + nr].reshape(1, nr * LANES)[:, :w]
        r += nr
    return out


BIG = ("w_in", "w_out", "w_cq", "w_ckv", "w_co", "w_up", "w_down")
WEIGHTS = ("g_mix", "w_in", "g_q", "g_k", "g_attn_out", "conv_w", "conv_b", "dt_bias", "a_log", "d_skip",
           "g_ssm_out", "w_out", "g_cross", "g_mem", "w_cq", "w_ckv", "g_cq", "g_ck", "w_co", "g_mlp", "w_up", "w_down")


REST = ("w_out", "w_cq", "w_ckv", "w_co", "w_up", "w_down")


class _Overlap:
    def __init__(self, shards, after):
        self.gather, self.forward = {}, {}
        self.names = {"a": REST[:4], "b": REST[4:5], "c": REST[5:]}
        for tag, names in self.names.items():
            lands = [_put_own(lax.empty((N_DEV,) + shards[n].shape, BF), shards[n]) for n in names]
            self.gather[tag] = _exchange_start("gather", [shards[n] for n in names], lands, after, "ag_start_" + tag)
            after = self.gather[tag][3]
        self.token = after
        self.groups = []
        self.pairs = {}

    def rest_mid(self, tag, after):
        sems, srcs, lands, _ = self.gather[tag]
        srcs, lands = _exchange_wait("gather", sems, srcs, lands, after, "ag_wait_" + tag)
        self.forward[tag] = _exchange_start("forward", srcs, lands, None, "ag_fwd_start_" + tag)
        return self.forward[tag][3]

    def rest(self, tag, after):
        sems, srcs, lands, _ = self.forward[tag]
        _, lands = _exchange_wait("forward", sems, srcs, lands, after, "ag_fwd_wait_" + tag)
        wf = dict(zip(self.names[tag], lands))
        for n in wf:
            if n in ("w_out", "w_cq", "w_ckv", "w_down"):
                wf[n] = wf[n].reshape(-1, wf[n].shape[-1])
        return wf

    def pair_start(self, name, grad):
        got = lax.empty((N_CHIP,) + grad.shape[1:], grad.dtype)
        self.pairs[name] = _exchange_start("pair", [grad], [got], None, "rs_pair_start_" + name)
        return self.pairs[name][3]

    def emit(self, grads, after):
        names, tag = list(grads), "_%d" % len(self.groups)
        grads, got = dict(grads), {}
        for n in names:
            if n in self.pairs:
                sems, srcs, lands, _ = self.pairs[n]
                srcs, lands = _exchange_wait("pair", sems, srcs, lands, after, "rs_pair_wait_" + n)
                grads[n], got[n] = srcs[0], lands[0]
        sync = [n for n in names if n not in got]
        if sync:
            got.update(zip(sync, _pair_exchange([grads[n] for n in sync], "rs_pair" + tag)))
        sums = [_pair_sum(grads[n], got[n], "rs_sum_" + n) for n in names]
        sems, srcs, lands, token = _exchange_start("scatter", [q for q, _ in sums], [l for _, l in sums], None,
                                                   "rs_chip_start" + tag)
        self.groups.append((names, sems, srcs, lands))
        return token

    def finish(self, gi, after):
        names, sems, srcs, lands = self.groups[gi]
        _, lands = _exchange_wait("scatter", sems, srcs, lands, after, "rs_chip_wait_%d" % gi)
        return dict(zip(names, lands))


def _tie(v, token):
    return v if token is None else v + token[0:1, 0:1]


def _local_step(x, mem, pos_col, target, p, wf, hooks=None):
    p = dict(p)
    inv = np.zeros((1, LANES), np.float32)
    freq = (ROPE_THETA ** (-2.0 * np.arange(ROT // 2, dtype=np.float32) / ROT)).astype(np.float32)
    for l in range(LANES):
        if l % HEAD < ROT:
            inv[0, l] = freq[(l % HEAD) % (ROT // 2)]
    inv_tab = jnp.asarray(inv)
    gq128, gk128 = jnp.tile(p["g_q"], (1, 2)), jnp.tile(p["g_k"], (1, 2))
    dtb128, alog128 = _pad_lanes(p["dt_bias"]), _pad_lanes(p["a_log"])
    dskip_b = jnp.repeat(p["d_skip"], HEAD, axis=1)
    conv_w, conv_b = wf["conv_w"], p["conv_b"]

    h = _rms_fwd(x, p["g_mix"], "rms_mix")
    proj = _matmul(h, wf["w_in"], mode="nn", name="mm_in", tm=1024, tn=1280, tk=D)
    qr, kr = _qk_prep(proj, pos_col, gq128, gk128, inv_tab)
    dilations = (1, 4, 16)
    fwd = [_attn_fwd(qr, kr, proj, d, "attn_fwd_d%d" % d) for d in dilations]
    attn, lse, attn_n = _attn_merge([o for o, _ in fwd], [l for _, l in fwd], p["g_attn_out"])

    xbc = _conv_fwd(proj, conv_w, conv_b)
    dt, a_cs = _ssd_prep(proj, dtb128, alog128)
    a_b = _tie(jnp.repeat(a_cs[:, :N_HEADS], HEAD, axis=1), None if hooks is None else hooks.rest_mid("a", xbc))
    dt_b = jnp.repeat(dt[:, :N_HEADS], HEAD, axis=1)
    at_r, dt_r = _row_layout(a_cs[:, :N_HEADS]), _row_layout(dt[:, :N_HEADS])
    y_ssd, h_all = _ssd_fwd(xbc, a_b, dt_b, at_r, dt_r)
    ssm_n = _ssd_post(y_ssd, xbc, proj, dskip_b, p["g_ssm_out"])

    if hooks is not None:
        wf = {**wf, **hooks.rest("a", ssm_n)}
    mix = jnp.concatenate([attn_n, ssm_n], axis=1)
    x1 = _matmul(mix, wf["w_out"], mode="nn", name="mm_out", tm=1024, tn=1024, tk=D,
                 extras=(x,), epilogue=lambda acc, r: (acc + r,))
    hc = _rms_fwd(x1, _tie(p["g_cross"], None if hooks is None else hooks.rest_mid("b", x1)), "rms_cross")
    memh = _rms_fwd(mem, p["g_mem"], "rms_mem")
    qc = _matmul(hc, wf["w_cq"], mode="nn", name="mm_cq", tm=1024, tn=512, tk=D)
    kv = _matmul(memh, wf["w_ckv"], mode="nn", name="mm_ckv", tm=N_MEM, tn=1024, tk=D)
    oc = _cross_fwd(qc, kv, p["g_cq"], p["g_ck"])
    x2 = _matmul(oc, wf["w_co"], mode="nn", name="mm_co", tm=1024, tn=256, tk=512, b_cb=256,
                 extras=(x1,), epilogue=lambda acc, r: (acc + r,))
    hm = _rms_fwd(x2, p["g_mlp"], "rms_mlp")
    token = None
    if hooks is not None:
        wf = {**wf, **hooks.rest("b", hm)}
        token = hooks.rest_mid("c", hm)

    def up_epi(acc):
        ru = jnp.maximum(acc, 0.0)
        return ru * ru, 2.0 * ru

    act, relu2 = _matmul(hm, wf["w_up"], mode="nn", name="mm_up", tm=1024, tn=1024, tk=D, b_cb=1024,
                         out_dtypes=(BF, BF), epilogue=up_epi, after=token)
    if hooks is not None:
        wf = {**wf, **hooks.rest("c", act)}

    def loss_epi(acc, r, t):
        d = (acc + r - t) * (1.0 / D)
        return d, d

    dy, dyb = _matmul(act, wf["w_down"], mode="nn", name="mm_down", tm=512, tn=1024, tk=2048, extras=(x2, target),
                      out_dtypes=(F32, BF), epilogue=loss_epi)
    loss_part = _loss_sum(dy)[0, 0] * (0.5 * D)

    gb, gs = {}, {}
    gb["w_down"] = _matmul(act, dyb, mode="tn", name="mm_down_dw", tm=1024, tn=1024, tk=S,
                           out_dtypes=(BF,)).reshape(N_DEV, D_FF // N_DEV, D)
    token = None if hooks is None else hooks.pair_start("w_down", gb["w_down"])
    du = _matmul(dyb, wf["w_down"], mode="nt", name="mm_down_dx", tm=1024, tn=1024, tk=D, extras=(relu2,),
                 out_dtypes=(BF,), epilogue=lambda acc, r: (acc * r.astype(F32),), after=token)
    gb["w_up"] = _matmul(hm, du, mode="tn", name="mm_up_dw", tm=1024, tn=1024, tk=S, out_dtypes=(BF,), out_cb=1024)
    token = None if hooks is None else hooks.pair_start("w_up", gb["w_up"])
    dhm = _matmul(du, wf["w_up"], mode="nt", name="mm_up_dx", tm=1024, tn=1024, tk=2048, b_cb=1024, after=token)
    if hooks is not None:
        p["g_mlp"] = _tie(p["g_mlp"], hooks.emit({n: gb[n] for n in ("w_down", "w_up")}, dhm))
    dx2, dx2b, gs["g_mlp"] = _rms_bwd_call(x2, p["g_mlp"], dhm, dy, "rms_mlp_bwd")

    doc = _matmul(dx2b, wf["w_co"], mode="nt", name="mm_co_dx", tm=1024, tn=512, tk=256, b_cb=256)
    gb["w_co"] = _matmul(oc, dx2b, mode="tn", name="mm_co_dw", tm=512, tn=256, tk=S, out_dtypes=(BF,), out_cb=256)
    dqc, dkn, dvc, gs["g_cq"] = _cross_bwd(qc, doc, kv, p["g_cq"], p["g_ck"])
    dkv, gs["g_ck"] = _cross_kv_bwd(kv, dkn, dvc, p["g_ck"])
    gb["w_cq"] = _matmul(hc, dqc, mode="tn", name="mm_cq_dw", tm=1024, tn=512, tk=S,
                         out_dtypes=(BF,)).reshape(N_DEV, D // N_DEV, D_CROSS)
    dhc = _matmul(dqc, wf["w_cq"], mode="nt", name="mm_cq_dx", tm=1024, tn=1024, tk=512)
    gb["w_ckv"] = _matmul(memh, dkv, mode="tn", name="mm_ckv_dw", tm=1024, tn=1024, tk=N_MEM,
                          out_dtypes=(BF,)).reshape(N_DEV, D // N_DEV, 2 * D_CROSS)
    dmemh = _matmul(dkv, wf["w_ckv"], mode="nt", name="mm_ckv_dx", tm=N_MEM, tn=1024, tk=1024)
    (gs["g_mem"],) = _rms_bwd_call(mem, p["g_mem"], dmemh, None, "rms_mem_bwd")
    dx1, dx1b, gs["g_cross"] = _rms_bwd_call(x1, p["g_cross"], dhc, dx2, "rms_cross_bwd")

    dmix = _matmul(dx1b, wf["w_out"], mode="nt", name="mm_out_dx", tm=1024, tn=1024, tk=D)
    gb["w_out"] = _matmul(mix, dx1b, mode="tn", name="mm_out_dw", tm=1024, tn=1024, tk=S,
                          out_dtypes=(BF,)).reshape(N_DEV, D // N_DEV, D)
    if hooks is not None:
        p["g_attn_out"] = _tie(p["g_attn_out"],
                               hooks.emit({n: gb[n] for n in ("w_co", "w_cq", "w_ckv", "w_out")}, dmix))

    dattn, delta, gs["g_attn_out"] = _attn_merge_bwd(dmix, attn, p["g_attn_out"])
    bwd = [_attn_bwd(qr, kr, proj, dattn, lse, delta, d, "attn_bwd_d%d" % d) for d in dilations]
    dq_pre, dk_pre, dv_pre, dgq, dgk = _qk_bwd([t[0] for t in bwd], [t[1] for t in bwd], [t[2] for t in bwd],
                                               proj, pos_col, gq128, gk128, inv_tab)
    gs["g_q"], gs["g_k"] = dgq[:, :HEAD], dgk[:, :HEAD]

    dy_ssd, dz, dxs_skip, dd_lane, gs["g_ssm_out"] = _ssd_post_bwd(y_ssd, xbc, proj, dmix, dskip_b, p["g_ssm_out"])
    gs["d_skip"] = dd_lane.reshape(N_HEADS, HEAD).sum(axis=1).reshape(1, N_HEADS)
    dc_pair, daq_b, dx_ssd, db_pair, dak_r, ddt_r, ddt_b = _ssd_bwd(xbc, dy_ssd, h_all, a_b, dt_b, at_r, dt_r)
    daq = _pad_lanes(daq_b[:, ::HEAD])
    dak = _pad_lanes(_row_layout_inv(dak_r))
    ddt_direct = _pad_lanes(_row_layout_inv(ddt_r))
    ddt_raw, dalog, dbias = _ssd_prep_bwd(daq, dak, ddt_direct, _pad_lanes(ddt_b[:, ::HEAD]), dt, proj,
                                          dtb128, alog128)
    gs["a_log"], gs["dt_bias"] = dalog[:, :N_HEADS], dbias[:, :N_HEADS]
    same = lambda c: c
    dxbc_x, dcw_x, dcb_x = _conv_bwd(proj, conv_w, conv_b, dxs_skip, dx_ssd, same, same, 0, 8, "conv_bwd_x")
    dxbc_b, dcw_b, dcb_b = _conv_bwd(proj, conv_w, conv_b, db_pair, db_pair, lambda c: 2 * c, lambda c: 2 * c + 1,
                                     D_SSM, 4, "conv_bwd_b")
    dxbc_c, dcw_c, dcb_c = _conv_bwd(proj, conv_w, conv_b, dc_pair, dc_pair, lambda c: 2 * c, lambda c: 2 * c + 1,
                                     D_SSM + 512, 4, "conv_bwd_c")
    g_conv_w = jnp.concatenate([dcw_x, dcw_b, dcw_c], axis=1)
    gs["conv_b"] = jnp.concatenate([dcb_x, dcb_b, dcb_c], axis=1)

    pieces = [dq_pre, dk_pre, dv_pre, dz, dxbc_x, dxbc_b, dxbc_c, ddt_raw]
    pieces.append(jnp.zeros((S, D_INP - sum(t.shape[1] for t in pieces)), BF))
    dproj = jnp.concatenate(pieces, axis=1)
    dw_in = _matmul(h, dproj, mode="tn", name="mm_in_dw", tm=1024, tn=1280, tk=S, out_dtypes=(BF,))
    gb["w_in"] = _w_in_to_blocks(dw_in)
    token = None if hooks is None else hooks.emit({"w_in": gb["w_in"]}, dw_in)
    dh = _matmul(dproj, wf["w_in"], mode="nt", name="mm_in_dx", tm=1024, tn=512, tk=D_INP // 2, after=token)
    grad_x, _, gs["g_mix"] = _rms_bwd_call(x, p["g_mix"], dh, dx1, "rms_mix_bwd")
    return loss_part, grad_x, gb, gs, g_conv_w


def kernel(x, mem, positions, g_mix, w_in, g_q, g_k, g_attn_out, conv_w, conv_b, dt_bias, a_log, d_skip, g_ssm_out, w_out, g_cross, g_mem, w_cq, w_ckv, g_cq, g_ck, w_co, g_mlp, w_up, w_down, loss_target, m_g_mix, m_w_in, m_g_q, m_g_k, m_g_attn_out, m_conv_w, m_conv_b, m_dt_bias, m_a_log, m_d_skip, m_g_ssm_out, m_w_out, m_g_cross, m_g_mem, m_w_cq, m_w_ckv, m_g_cq, m_g_ck, m_w_co, m_g_mlp, m_w_up, m_w_down, v_g_mix, v_w_in, v_g_q, v_g_k, v_g_attn_out, v_conv_w, v_conv_b, v_dt_bias, v_a_log, v_d_skip, v_g_ssm_out, v_w_out, v_g_cross, v_g_mem, v_w_cq, v_w_ckv, v_g_cq, v_g_ck, v_w_co, v_g_mlp, v_w_up, v_w_down):
    args = dict(locals())
    w = {n: args[n] for n in WEIGHTS}
    m = {n: args["m_" + n] for n in WEIGHTS}
    v = {n: args["v_" + n] for n in WEIGHTS}
    small = {n: w[n] for n, _ in SMALL}
    me = 4 * lax.axis_index("x") + 2 * lax.axis_index("y") + lax.axis_index("c")

    w_in_g, conv_w_g = _all_gather([w["w_in"][0].astype(BF), w["conv_w"][0]], "ag_first")
    wf = {"w_in": _w_in_from_blocks(w_in_g), "conv_w": conv_w_g.transpose(1, 0, 2).reshape(CONV_W, D_CONV)}
    overlap = _Overlap({n: w[n][0].astype(BF) for n in REST}, w_in_g)
    p = dict(small)
    p["g_mix"] = _tie(p["g_mix"], overlap.token)

    loss_part, grad_x, _, gs, g_conv_w = _local_step(
        x[0], mem[0], positions.reshape(S, 1), loss_target[0], p, wf, overlap)
    loss = lax.psum(loss_part, ("x", "y", "c"))

    out = {}
    last = grad_x
    for gi in range(3):
        for n, parts in overlap.finish(gi, last).items():
            res_n = _adamw(w[n][0], m[n][0], v[n][0], parts, "adamw_" + n)
            out[n] = [t.reshape(w[n].shape) for t in res_n]
            last = res_n[0]

    sm_parts, cw_parts = _all_gather([_pack_small(gs), g_conv_w], "ag_small_grads", after=last)
    sm = [_unpack_small(t) for t in _adamw(_pack_small(small), _pack_small({n: m[n] for n, _ in SMALL}),
                                           _pack_small({n: v[n] for n, _ in SMALL}), sm_parts, "adamw_small")]
    for n, _ in SMALL:
        out[n] = [t[n] for t in sm]
    cols = D_CONV // N_DEV
    cw_mine = lax.dynamic_slice_in_dim(cw_parts, me * cols, cols, axis=2)
    out["conv_w"] = [t.reshape(w["conv_w"].shape) for t in
                     _adamw(w["conv_w"][0], m["conv_w"][0], v["conv_w"][0], cw_mine, "adamw_conv_w")]

    res = [loss, grad_x.reshape(x.shape)]
    for k in range(4):
        res += [out[n][k] for n in WEIGHTS]
    return tuple(res)
```

```python
import functools
import math

import numpy as np
import jax
import jax.numpy as jnp
from jax import lax
from jax.experimental import pallas as pl
from jax.experimental.pallas import tpu as pltpu

F32 = jnp.float32
BF = jnp.bfloat16

N_DEV = 8
S = 2048
D = 2048
D_ATTN = 1024
N_HEADS = 16
HEAD = 64
ROT = 16
ROPE_THETA = 500000.0
D_SSM = 1024
D_CONV = 2048
CONV_W = 4
N_MEM = 256
D_CROSS = 512
CROSS_HEAD = 128
D_FF = 8192
D_IN = 6160
D_INP = 6400
DT_COLBLK = (4 * D_ATTN + D_CONV) // 128
EPS = 1e-6
BLK = 128
NB = S // BLK
LANES = 128
NEG = -1e30

ADAM_LR = 0.001
ADAM_B1 = 0.9
ADAM_B2 = 0.999
ADAM_EPS = 1e-08
ADAM_WD = 0.01
ADAM_STEP = 10

VMEM_LIMIT_BYTES = 48 * 1024 * 1024
ROW_TILE = 256


def _params(*sem):
    return pltpu.CompilerParams(dimension_semantics=sem, vmem_limit_bytes=VMEM_LIMIT_BYTES)


def _matmul(a, b, *, mode, name, tm, tn, tk, out_dtypes=(F32,), b_cb=None, out_cb=None,
            extras=(), epilogue=None, after=None):
    if mode == "tn":
        kk, m = a.shape
    else:
        m, kk = a.shape
    if b_cb is None:
        br, bc = b.shape
    else:
        br, bc = b.shape[1], N_DEV * b_cb
    n = br if mode == "nt" else bc
    assert m % tm == 0 and n % tn == 0 and kk % tk == 0, (name, m, n, kk)
    nk = kk // tk
    grid = (m // tm, n // tn, nk)

    if mode == "tn":
        a_spec = pl.BlockSpec((tk, tm), lambda i, j, k: (k, i))
    else:
        a_spec = pl.BlockSpec((tm, tk), lambda i, j, k: (i, k))
    if mode == "nt":
        b_blk, b_idx = (tn, tk), (lambda i, j, k: (j, k))
    else:
        b_blk, b_idx = (tk, tn), (lambda i, j, k: (k, j))
    group = 1
    if b_cb is None:
        b_spec = pl.BlockSpec(b_blk, b_idx)
    elif mode == "nt" and tk > b_cb:
        assert tk % b_cb == 0
        group = tk // b_cb
        b_spec = pl.BlockSpec((group, tn, b_cb), lambda i, j, k: (k, j, 0))
    else:
        tc = b_blk[1]
        assert b_cb % tc == 0
        per = b_cb // tc

        def b_idx3(i, j, k):
            r, c = b_idx(i, j, k)
            return (c // per, r, c % per)
        b_spec = pl.BlockSpec((None,) + b_blk, b_idx3)
    ex_specs = [pl.BlockSpec((tm, tn), lambda i, j, k: (i, j)) for _ in extras]
    if out_cb is None:
        out_shape = [jax.ShapeDtypeStruct((m, n), dt) for dt in out_dtypes]
        out_specs = [pl.BlockSpec((tm, tn), lambda i, j, k: (i, j)) for _ in out_dtypes]
    else:
        assert out_cb % tn == 0
        pero = out_cb // tn
        out_shape = [jax.ShapeDtypeStruct((N_DEV, m, out_cb), dt) for dt in out_dtypes]
        out_specs = [pl.BlockSpec((None, tm, tn), lambda i, j, k: (j // pero, i, j % pero)) for _ in out_dtypes]
    dn = {"nn": (((1,), (0,)), ((), ())), "nt": (((1,), (1,)), ((), ())), "tn": (((0,), (0,)), ((), ()))}[mode]
    ne, no = len(extras), len(out_dtypes)
    n_after = 0 if after is None else 1

    def kern(a_ref, b_ref, *rest):
        ex_refs, out_refs = rest[:ne], rest[ne + n_after:ne + n_after + no]

        def finish(acc):
            outs = (acc,) if epilogue is None else epilogue(acc, *[r[...] for r in ex_refs])
            for r, o in zip(out_refs, outs):
                r[...] = o.astype(r.dtype)

        if group == 1:
            part = lax.dot_general(a_ref[...].astype(BF), b_ref[...].astype(BF), dn, preferred_element_type=F32)
        else:
            part = sum(lax.dot_general(a_ref[:, g * b_cb:(g + 1) * b_cb].astype(BF), b_ref[g].astype(BF), dn,
                                       preferred_element_type=F32) for g in range(group))
        if nk == 1:
            finish(part)
            return
        acc_ref = rest[ne + n_after + no]
        k = pl.program_id(2)

        @pl.when(k == 0)
        def _():
            acc_ref[...] = part

        @pl.when(k > 0)
        def _():
            acc_ref[...] += part

        @pl.when(k == nk - 1)
        def _():
            finish(acc_ref[...])

    res = pl.pallas_call(
        kern, name=name, grid=grid, in_specs=[a_spec, b_spec] + ex_specs + [ANY] * n_after, out_specs=out_specs,
        out_shape=out_shape, scratch_shapes=[pltpu.VMEM((tm, tn), F32)] if nk > 1 else [],
        compiler_params=_params("parallel", "parallel", "arbitrary"),
    )(a, b, *extras, *([] if after is None else [after]))
    return res[0] if no == 1 else tuple(res)


def _rowwise(body, *, name, nrows, tile, row_ins, full_ins=(), row_outs=(), acc_outs=(), scratch=(),
             reverse=False):
    n = nrows // tile

    def ridx(i):
        return (n - 1 - i) if reverse else i

    in_specs, args = [], []
    for arr, width, cb in row_ins:
        in_specs.append(pl.BlockSpec((tile, width), lambda i, cb=cb: (ridx(i), cb)))
        args.append(arr)
    for arr in full_ins:
        in_specs.append(pl.BlockSpec(arr.shape, lambda i, nd=arr.ndim: (0,) * nd))
        args.append(arr)
    out_shape, out_specs = [], []
    for width, dt in row_outs:
        out_shape.append(jax.ShapeDtypeStruct((nrows, width), dt))
        out_specs.append(pl.BlockSpec((tile, width), lambda i: (ridx(i), 0)))
    for shp, dt in acc_outs:
        out_shape.append(jax.ShapeDtypeStruct(shp, dt))
        out_specs.append(pl.BlockSpec(shp, lambda i, nd=len(shp): (0,) * nd))

    def kern(*refs):
        body(pl.program_id(0), n, *refs)

    return pl.pallas_call(kern, name=name, grid=(n,), in_specs=in_specs, out_specs=out_specs,
                          out_shape=out_shape, scratch_shapes=list(scratch),
                          compiler_params=_params("arbitrary"))(*args)


def _accum(ref, val, i):
    @pl.when(i == 0)
    def _():
        ref[...] = val

    @pl.when(i > 0)
    def _():
        ref[...] += val


def _sigmoid(x):
    return 1.0 / (1.0 + jnp.exp(-x))


def _rms_n(x):
    r = lax.rsqrt(jnp.mean(x * x, axis=-1, keepdims=True) + EPS)
    return x * r, r


def _rms_bwd(x, g, dh):
    n, r = _rms_n(x)
    dn = dh * g
    dx = r * (dn - n * jnp.mean(dn * n, axis=-1, keepdims=True))
    return dx, jnp.sum(dh * n, axis=0, keepdims=True)


def _seg_sum(x, seg):
    t, w = x.shape
    tiles = [x[:, LANES * j:LANES * (j + 1)] for j in range(w // LANES)]
    outs = []
    if seg == 64:
        lo = lax.broadcasted_iota(jnp.int32, (t, LANES), 1) < 64
        for xt in tiles:
            s_lo = jnp.sum(jnp.where(lo, xt, 0.0), axis=-1, keepdims=True)
            s_hi = jnp.sum(jnp.where(lo, 0.0, xt), axis=-1, keepdims=True)
            outs.append(jnp.where(lo, s_lo, s_hi))
    else:
        sums = [jnp.sum(xt, axis=-1, keepdims=True) for xt in tiles]
        if seg == 256:
            sums = [sums[2 * (j // 2)] + sums[2 * (j // 2) + 1] for j in range(len(sums))]
        else:
            assert seg == 128
        outs = [jnp.broadcast_to(s, (t, LANES)) for s in sums]
    return outs[0] if len(outs) == 1 else jnp.concatenate(outs, axis=1)


def _seg_rms_n(x, seg):
    r = lax.rsqrt(_seg_sum(x * x, seg) * (1.0 / seg) + EPS)
    return x * r, r


def _seg_rms_bwd(x, g, dy, seg):
    n, r = _seg_rms_n(x, seg)
    dn = dy * g
    dx = r * (dn - n * (_seg_sum(dn * n, seg) * (1.0 / seg)))
    return dx, jnp.sum(dy * n, axis=0, keepdims=True)


def _rope_tables(pos_col, inv_tab, t):
    ang = pos_col.astype(F32) * inv_tab
    idx = lax.broadcasted_iota(jnp.int32, (t, LANES), 1) % HEAD
    cos, sin = jnp.cos(ang), jnp.sin(ang)
    c = jnp.where(idx < ROT, cos, 1.0)
    sg = jnp.where(idx < ROT // 2, -sin, jnp.where(idx < ROT, sin, 0.0))
    return c, sg, idx < ROT // 2


def _rope(x, c, sg, lo):
    partner = jnp.where(lo, pltpu.roll(x, LANES - ROT // 2, 1), pltpu.roll(x, ROT // 2, 1))
    return x * c + partner * sg


def _fold_heads(v):
    v8 = jnp.broadcast_to(v, (8, LANES))
    return (v8 + pltpu.roll(v8, HEAD, 1))[0:1]


def _dot(a, b, dn):
    return lax.dot_general(a, b, (dn, ((), ())), preferred_element_type=F32)


NN = ((1,), (0,))
NT = ((1,), (1,))
TN = ((0,), (0,))


def _dot3(l01, x):
    x1 = x.astype(BF)
    r1 = x - x1.astype(F32)
    x2 = r1.astype(BF)
    x3 = (r1 - x2.astype(F32)).astype(BF)
    return _dot(l01, x1, NN) + _dot(l01, x2, NN) + _dot(l01, x3, NN)


def _rms_fwd(x, g, name):
    rows = x.shape[0]

    def body(i, n, x_ref, g_ref, o_ref):
        nx, _ = _rms_n(x_ref[...])
        o_ref[...] = (nx * g_ref[...]).astype(BF)

    return _rowwise(body, name=name, nrows=rows, tile=min(ROW_TILE, rows), row_ins=[(x, D, 0)],
                    full_ins=[g], row_outs=[(D, BF)])[0]


def _qk_prep(proj, pos_col, gq128, gk128, inv_tab):
    scale = HEAD ** -0.5

    def body(i, n, q_ref, k_ref, p_ref, gq_ref, gk_ref, it_ref, qo_ref, ko_ref):
        t = q_ref.shape[0]
        c, sg, lo = _rope_tables(p_ref[...], it_ref[...], t)
        for j in range(D_ATTN // LANES):
            sl = slice(LANES * j, LANES * (j + 1))
            qn, _ = _seg_rms_n(q_ref[:, sl], HEAD)
            kn, _ = _seg_rms_n(k_ref[:, sl], HEAD)
            qo_ref[:, sl] = _rope(qn * gq_ref[...], c, sg, lo) * scale
            ko_ref[:, sl] = _rope(kn * gk_ref[...], c, sg, lo)

    return _rowwise(body, name="qk_prep", nrows=S, tile=ROW_TILE,
                    row_ins=[(proj, D_ATTN, 0), (proj, D_ATTN, 1), (pos_col, 1, 0)],
                    full_ins=[gq128, gk128, inv_tab], row_outs=[(D_ATTN, F32)] * 2)


ATTN_QB = 4
V_COLS = pl.BlockSpec((S, LANES), lambda p: (0, 2 * D_ATTN // LANES + p))


def _band(b, d):
    nb = NB // d
    r, n = b // nb, b % nb
    width, kn = (BLK, n) if nb == 1 else (2 * BLK, jnp.maximum(n - 1, 0))

    def rows(first_member, count):
        if d == 1:
            return pl.ds(pl.multiple_of(first_member, BLK), count)
        return pl.ds(first_member * d + r, count, stride=d)

    qm = n * BLK + (lax.broadcasted_iota(jnp.int32, (2 * BLK, width), 0) & (BLK - 1))
    km = kn * BLK + lax.broadcasted_iota(jnp.int32, (2 * BLK, width), 1)
    valid = km <= qm
    if nb > 1:
        valid = valid & (km >= qm - BLK)
    return rows(n * BLK, BLK), rows(kn * BLK, width), valid


def _attn_fwd(q, k, v, d, name):
    def kern(q_ref, k_ref, v_ref, o_ref, l_ref):
        half0 = lax.broadcasted_iota(jnp.int32, (BLK, LANES), 1) < HEAD

        def group(g, carry):
            for bb in range(ATTN_QB):
                q_rows, k_rows, valid = _band(g * ATTN_QB + bb, d)
                q = q_ref[q_rows, :]
                kb, vb = k_ref[k_rows, :].astype(BF), v_ref[k_rows, :].astype(BF)
                q2 = jnp.concatenate([jnp.where(half0, q, 0.0), jnp.where(half0, 0.0, q)], axis=0).astype(BF)
                s = jnp.where(valid, _dot(q2, kb, NT), NEG)
                m = jnp.max(s, axis=-1, keepdims=True)
                p = jnp.exp(s - m)
                den = jnp.sum(p, axis=-1, keepdims=True)
                o2 = _dot(p.astype(BF), vb, NN) / den
                l2 = m + jnp.log(den)
                o_ref[q_rows, :] = jnp.where(half0, o2[:BLK], o2[BLK:])
                l_ref[q_rows, :] = jnp.where(half0, l2[:BLK], l2[BLK:])
            return carry

        lax.fori_loop(0, NB // ATTN_QB, group, 0)

    seq = pl.BlockSpec((S, LANES), lambda p: (0, p))
    return pl.pallas_call(
        kern, name=name, grid=(D_ATTN // LANES,), in_specs=[seq, seq, V_COLS], out_specs=[seq, seq],
        out_shape=[jax.ShapeDtypeStruct((S, D_ATTN), F32)] * 2,
        compiler_params=_params("parallel"))(q, k, v)


def _attn_merge(os_, ls_, g_attn):
    def body(i, n, o1, o2, o3, l1, l2, l3, g_ref, a_ref, lse_ref, an_ref):
        la, lb, lc = l1[...], l2[...], l3[...]
        m = jnp.maximum(jnp.maximum(la, lb), lc)
        ea, eb, ec = jnp.exp(la - m), jnp.exp(lb - m), jnp.exp(lc - m)
        den = ea + eb + ec
        attn = (ea * o1[...] + eb * o2[...] + ec * o3[...]) / den
        a_ref[...] = attn
        lse_ref[...] = m + jnp.log(den)
        nx, _ = _rms_n(attn)
        an_ref[...] = (nx * g_ref[...]).astype(BF)

    return _rowwise(body, name="attn_merge", nrows=S, tile=ROW_TILE,
                    row_ins=[(a, D_ATTN, 0) for a in (*os_, *ls_)], full_ins=[g_attn],
                    row_outs=[(D_ATTN, F32), (D_ATTN, F32), (D_ATTN, BF)])


def _conv_taps(x, w_ref):
    rows = lax.broadcasted_iota(jnp.int32, x.shape, 0)
    shifted = []
    for w in range(CONV_W):
        k = CONV_W - 1 - w
        shifted.append(x if k == 0 else jnp.where(rows >= k, pltpu.roll(x, k, 0), 0.0))
    acc = shifted[0] * w_ref[0:1, :]
    for w in range(1, CONV_W):
        acc = acc + shifted[w] * w_ref[w:w + 1, :]
    return acc, shifted


def _conv_fwd(proj, conv_w, conv_b):
    tc = 256

    def kern(x_ref, w_ref, b_ref, o_ref):
        c, _ = _conv_taps(x_ref[...], w_ref)
        c = c + b_ref[...]
        o_ref[...] = c * _sigmoid(c)

    return pl.pallas_call(
        kern, name="conv_fwd", grid=(D_CONV // tc,),
        in_specs=[pl.BlockSpec((S, tc), lambda c: (0, 4 * D_ATTN // tc + c)),
                  pl.BlockSpec((CONV_W, tc), lambda c: (0, c)), pl.BlockSpec((1, tc), lambda c: (0, c))],
        out_specs=pl.BlockSpec((S, tc), lambda c: (0, c)),
        out_shape=jax.ShapeDtypeStruct((S, D_CONV), F32), compiler_params=_params("parallel"))(proj, conv_w, conv_b)


def _softplus(x):
    return jnp.maximum(x, 0.0) + jnp.log1p(jnp.exp(-jnp.abs(x)))


def _dot3r(x, r01):
    x1 = x.astype(BF)
    r1 = x - x1.astype(F32)
    x2 = r1.astype(BF)
    x3 = (r1 - x2.astype(F32)).astype(BF)
    return _dot(x1, r01, NN) + _dot(x2, r01, NN) + _dot(x3, r01, NN)


def _spread_heads(v):
    sel = (lax.broadcasted_iota(jnp.int32, (LANES, D_SSM), 1) // HEAD
           == lax.broadcasted_iota(jnp.int32, (LANES, D_SSM), 0))
    return _dot3r(v, sel.astype(BF))


def _collect_heads(v):
    sel = (lax.broadcasted_iota(jnp.int32, (D_SSM, LANES), 0)
           == HEAD * lax.broadcasted_iota(jnp.int32, (D_SSM, LANES), 1))
    return _dot3r(v, sel.astype(BF))


def _ssd_prep(proj, dt_bias128, a_log128):
    def body(i, n, raw_ref, b_ref, al_ref, dt_ref, a_ref, dtb_ref, ab_ref, carry):
        @pl.when(i == 0)
        def _():
            carry[...] = jnp.zeros_like(carry)

        dt = _softplus(raw_ref[...] + b_ref[...])
        da = dt * (-jnp.exp(al_ref[...]))
        tri = (lax.broadcasted_iota(jnp.int32, (BLK, BLK), 0) >= lax.broadcasted_iota(jnp.int32, (BLK, BLK), 1))
        cs = _dot3(tri.astype(BF), da) + carry[0:1, :]
        dt_ref[...] = dt
        a_ref[...] = cs
        dtb_ref[...] = _spread_heads(dt)
        ab_ref[...] = _spread_heads(cs)
        carry[...] = jnp.broadcast_to(cs[BLK - 1:BLK, :], carry.shape)

    return _rowwise(body, name="ssd_prep", nrows=S, tile=BLK, row_ins=[(proj, LANES, DT_COLBLK)],
                    full_ins=[dt_bias128, a_log128],
                    row_outs=[(LANES, F32), (LANES, F32), (D_SSM, F32), (D_SSM, F32)],
                    scratch=[pltpu.VMEM((8, LANES), F32)])


def _ssd_decay(a_col, at_row, causal):
    diff = jnp.where(causal, a_col - at_row, 0.0)
    return jnp.where(causal, jnp.exp(diff), 0.0)


SSD_CB = 2


def _ssd_fwd(xbc, a_b, dt_b, at_r, dt_r):
    def kern(c_ref, b_ref, x_ref, ab_ref, dtb_ref, at_ref, dt_ref, y_ref, hall_ref):
        half0 = lax.broadcasted_iota(jnp.int32, (BLK, LANES), 1) < HEAD
        causal = lax.broadcasted_iota(jnp.int32, (BLK, BLK), 1) <= lax.broadcasted_iota(jnp.int32, (BLK, BLK), 0)

        def step(i, carry):
            h, a_prev = carry
            for cc in range(SSD_CB):
                chunk = i * SSD_CB + cc
                rows = pl.ds(pl.multiple_of(chunk * BLK, BLK), BLK)
                ci, bi, x = c_ref[rows, :].astype(BF), b_ref[rows, :].astype(BF), x_ref[rows, :]
                ab = ab_ref[rows, :]
                a_end = ab[BLK - 1:BLK, :]
                alpha = jnp.exp(ab - a_prev)
                beta = jnp.exp(a_end - ab) * dtb_ref[rows, :]
                hall_ref[rows, :] = h
                cb = _dot(ci, bi, NT)
                at, dtj = at_ref[chunk], dt_ref[chunk]
                y = alpha * _dot(ci, h.astype(BF), NN)
                for hd in range(2):
                    hm = half0 if hd == 0 else jnp.logical_not(half0)
                    w = cb * _ssd_decay(ab[:, HEAD * hd:HEAD * hd + 1], at[hd:hd + 1, :], causal) * dtj[hd:hd + 1, :]
                    y = y + _dot(w.astype(BF), jnp.where(hm, x, 0.0).astype(BF), NN)
                y_ref[rows, :] = y
                h = alpha[BLK - 1:BLK, :] * h + _dot(bi, (beta * x).astype(BF), TN)
                a_prev = a_end
            return h, a_prev

        lax.fori_loop(0, NB // SSD_CB, step, (jnp.zeros((BLK, LANES), F32), jnp.zeros((1, LANES), F32)))

    npair = D_SSM // LANES
    rowvec = pl.BlockSpec((None, NB, 2, BLK), lambda p: (p, 0, 0, 0))
    seq = pl.BlockSpec((S, LANES), lambda p: (0, p))
    return pl.pallas_call(
        kern, name="ssd_fwd", grid=(npair,),
        in_specs=[pl.BlockSpec((S, LANES), lambda p: (0, 12 + p // 2)),
                  pl.BlockSpec((S, LANES), lambda p: (0, 8 + p // 2)), seq, seq, seq, rowvec, rowvec],
        out_specs=[seq, seq], out_shape=[jax.ShapeDtypeStruct((S, D_SSM), F32)] * 2,
        compiler_params=_params("parallel"))(xbc, xbc, xbc, a_b, dt_b, at_r, dt_r)


def _ssd_post(y_ssd, xbc, proj, dskip_b, g_ssm):
    def body(i, n, y_ref, xs_ref, z_ref, d_ref, g_ref, o_ref):
        z = z_ref[...]
        y2 = (y_ref[...] + d_ref[...] * xs_ref[...]) * (z * _sigmoid(z))
        nx, _ = _seg_rms_n(y2, 256)
        o_ref[...] = (nx * g_ref[...]).astype(BF)

    return _rowwise(body, name="ssd_post", nrows=S, tile=ROW_TILE,
                    row_ins=[(y_ssd, D_SSM, 0), (xbc, D_SSM, 0), (proj, D_SSM, 3)], full_ins=[dskip_b, g_ssm],
                    row_outs=[(D_SSM, BF)])[0]


def _cross_heads(q, kv_ref, gq, gk):
    out = []
    for h in range(D_CROSS // CROSS_HEAD):
        sl = slice(CROSS_HEAD * h, CROSS_HEAD * (h + 1))
        nq, rq = _rms_n(q[:, sl])
        nk, rk = _rms_n(kv_ref[:, sl])
        v = kv_ref[:, D_CROSS + CROSS_HEAD * h:D_CROSS + CROSS_HEAD * (h + 1)]
        out.append((sl, nq, rq, nk, rk, v))
    return out


def _cross_fwd(qc, kv, g_cq, g_ck):
    scale = CROSS_HEAD ** -0.5

    def body(i, n, q_ref, kv_ref, gq_ref, gk_ref, o_ref):
        for sl, nq, _, nk, _, v in _cross_heads(q_ref[...], kv_ref, gq_ref[...], gk_ref[...]):
            qn = (nq * gq_ref[...] * scale).astype(BF)
            kn = (nk * gk_ref[...]).astype(BF)
            s = _dot(qn, kn, NT)
            e = jnp.exp(s - jnp.max(s, axis=-1, keepdims=True))
            p = e / jnp.sum(e, axis=-1, keepdims=True)
            o_ref[:, sl] = _dot(p.astype(BF), v.astype(BF), NN).astype(BF)

    return _rowwise(body, name="cross_fwd", nrows=S, tile=ROW_TILE, row_ins=[(qc, D_CROSS, 0)],
                    full_ins=[kv, g_cq, g_ck], row_outs=[(D_CROSS, BF)])[0]


def _loss_sum(dy):
    def body(i, n, d_ref, o_ref):
        d = d_ref[...]
        s = jnp.sum(jnp.sum(d * d, axis=0, keepdims=True), axis=1, keepdims=True)
        _accum(o_ref, s, i)

    return _rowwise(body, name="loss_sum", nrows=S, tile=ROW_TILE, row_ins=[(dy, D, 0)],
                    acc_outs=[((1, 1), F32)])[0]


def _rms_bwd_call(x, g, dh, dres, name):
    rows = x.shape[0]
    has_res = dres is not None

    def body(i, n, *refs):
        if has_res:
            x_ref, dh_ref, dr_ref, g_ref, dx_ref, dxb_ref, dg_ref = refs
        else:
            x_ref, dh_ref, g_ref, dg_ref = refs
        dx, dg = _rms_bwd(x_ref[...], g_ref[...], dh_ref[...])
        if has_res:
            dx = dx + dr_ref[...]
            dx_ref[...] = dx
            dxb_ref[...] = dx.astype(BF)
        _accum(dg_ref, dg, i)

    row_ins = [(x, D, 0), (dh, D, 0)] + ([(dres, D, 0)] if has_res else [])
    return _rowwise(body, name=name, nrows=rows, tile=min(ROW_TILE, rows), row_ins=row_ins, full_ins=[g],
                    row_outs=[(D, F32), (D, BF)] if has_res else [], acc_outs=[((1, D), F32)])


def _cross_bwd(qc, doc, kv, g_cq, g_ck):
    scale = CROSS_HEAD ** -0.5

    def body(i, n, q_ref, do_ref, kv_ref, gq_ref, gk_ref, dq_ref, dkn_ref, dv_ref, dgq_ref):
        dgq = jnp.zeros((1, CROSS_HEAD), F32)
        dkn_parts, dv_parts = [], []
        for sl, nq, rq, nk, _, v in _cross_heads(q_ref[...], kv_ref, gq_ref[...], gk_ref[...]):
            qn = (nq * gq_ref[...] * scale).astype(BF)
            kn = (nk * gk_ref[...]).astype(BF)
            s = _dot(qn, kn, NT)
            e = jnp.exp(s - jnp.max(s, axis=-1, keepdims=True))
            p = e / jnp.sum(e, axis=-1, keepdims=True)
            do = do_ref[:, sl].astype(BF)
            dv_parts.append(_dot(p.astype(BF), do, TN))
            dp = _dot(do, v.astype(BF), NT)
            ds = (p * (dp - jnp.sum(dp * p, axis=-1, keepdims=True))).astype(BF)
            dqn = _dot(ds, kn, NN)
            dkn_parts.append(_dot(ds, qn, TN))
            dn = dqn * (gq_ref[...] * scale)
            dq_ref[:, sl] = (rq * (dn - nq * jnp.mean(dn * nq, axis=-1, keepdims=True))).astype(BF)
            dgq = dgq + jnp.sum(dqn * scale * nq, axis=0, keepdims=True)
        _accum(dkn_ref, jnp.concatenate(dkn_parts, axis=1), i)
        _accum(dv_ref, jnp.concatenate(dv_parts, axis=1), i)
        _accum(dgq_ref, dgq, i)

    return _rowwise(body, name="cross_bwd", nrows=S, tile=ROW_TILE, row_ins=[(qc, D_CROSS, 0), (doc, D_CROSS, 0)],
                    full_ins=[kv, g_cq, g_ck], row_outs=[(D_CROSS, BF)],
                    acc_outs=[((N_MEM, D_CROSS), F32), ((N_MEM, D_CROSS), F32), ((1, CROSS_HEAD), F32)])


def _cross_kv_bwd(kv, dkn, dv, g_ck):
    def body(i, n, kv_ref, dkn_ref, dv_ref, gk_ref, dkv_ref, dgk_ref):
        dgk = jnp.zeros((1, CROSS_HEAD), F32)
        for h in range(D_CROSS // CROSS_HEAD):
            sl = slice(CROSS_HEAD * h, CROSS_HEAD * (h + 1))
            dk, dg = _rms_bwd(kv_ref[:, sl], gk_ref[...], dkn_ref[:, sl])
            dkv_ref[:, sl] = dk.astype(BF)
            dgk = dgk + dg
        dkv_ref[:, D_CROSS:] = dv_ref[...].astype(BF)
        dgk_ref[...] = dgk

    return _rowwise(body, name="cross_kv_bwd", nrows=N_MEM, tile=N_MEM,
                    row_ins=[(kv, 2 * D_CROSS, 0), (dkn, D_CROSS, 0), (dv, D_CROSS, 0)], full_ins=[g_ck],
                    row_outs=[(2 * D_CROSS, BF)], acc_outs=[((1, CROSS_HEAD), F32)])


def _attn_merge_bwd(dmix, attn, g_attn):
    def body(i, n, dn_ref, a_ref, g_ref, da_ref, dl_ref, dg_ref):
        attn_v = a_ref[...]
        da, dg = _rms_bwd(attn_v, g_ref[...], dn_ref[...])
        da_ref[...] = da
        dl_ref[...] = _seg_sum(da * attn_v, HEAD)
        _accum(dg_ref, dg, i)

    return _rowwise(body, name="attn_merge_bwd", nrows=S, tile=ROW_TILE,
                    row_ins=[(dmix, D_ATTN, 0), (attn, D_ATTN, 0)], full_ins=[g_attn],
                    row_outs=[(D_ATTN, F32), (D_ATTN, F32)], acc_outs=[((1, D_ATTN), F32)])


def _attn_bwd(q, k, v, do, lse, delta, d, name):
    def kern(q_ref, k_ref, v_ref, do_ref, l_ref, d_ref, dq_ref, dk_ref, dv_ref):
        dk_ref[...] = jnp.zeros_like(dk_ref)
        dv_ref[...] = jnp.zeros_like(dv_ref)
        half0 = lax.broadcasted_iota(jnp.int32, (BLK, LANES), 1) < HEAD

        def group(g, carry):
            updates = []
            for bb in range(ATTN_QB):
                q_rows, k_rows, valid = _band(g * ATTN_QB + bb, d)
                q, do = q_ref[q_rows, :], do_ref[q_rows, :]
                lse_t, del_t = l_ref[q_rows, :], d_ref[q_rows, :]
                kb, vb = k_ref[k_rows, :].astype(BF), v_ref[k_rows, :].astype(BF)
                q2 = jnp.concatenate([jnp.where(half0, q, 0.0), jnp.where(half0, 0.0, q)], axis=0).astype(BF)
                do2 = jnp.concatenate([jnp.where(half0, do, 0.0), jnp.where(half0, 0.0, do)], axis=0).astype(BF)
                lse2 = jnp.concatenate([lse_t[:, 0:1], lse_t[:, HEAD:HEAD + 1]], axis=0)
                del2 = jnp.concatenate([del_t[:, 0:1], del_t[:, HEAD:HEAD + 1]], axis=0)
                s = jnp.where(valid, _dot(q2, kb, NT), NEG)
                p = jnp.exp(s - lse2)
                ds = (p * (_dot(do2, vb, NT) - del2)).astype(BF)
                dq2 = _dot(ds, kb, NN)
                dq_ref[q_rows, :] = jnp.where(half0, dq2[:BLK], dq2[BLK:])
                updates.append((k_rows, _dot(ds, q2, TN), _dot(p.astype(BF), do2, TN)))
            for k_rows, dk, dv in updates:
                dk_ref[k_rows, :] += dk
                dv_ref[k_rows, :] += dv
            return carry

        lax.fori_loop(0, NB // ATTN_QB, group, 0)

    seq = pl.BlockSpec((S, LANES), lambda p: (0, p))
    return pl.pallas_call(
        kern, name=name, grid=(D_ATTN // LANES,), in_specs=[seq, seq, V_COLS, seq, seq, seq],
        out_specs=[seq, seq, seq], out_shape=[jax.ShapeDtypeStruct((S, D_ATTN), F32)] * 3,
        compiler_params=_params("parallel"))(q, k, v, do, lse, delta)


def _qk_bwd(dqs, dks, dvs, proj, pos_col, gq128, gk128, inv_tab):
    scale = HEAD ** -0.5

    def body(i, n, *refs):
        dq_refs, dk_refs, dv_refs = refs[0:3], refs[3:6], refs[6:9]
        q_ref, k_ref, p_ref, gq_ref, gk_ref, it_ref = refs[9:15]
        dqo_ref, dko_ref, dvo_ref, dgq_ref, dgk_ref = refs[15:20]
        t = q_ref.shape[0]
        c, sg, lo = _rope_tables(p_ref[...], it_ref[...], t)
        dgq = jnp.zeros((1, LANES), F32)
        dgk = jnp.zeros((1, LANES), F32)
        for j in range(D_ATTN // LANES):
            sl = slice(LANES * j, LANES * (j + 1))
            dqr = _rope(dq_refs[0][:, sl] + dq_refs[1][:, sl] + dq_refs[2][:, sl], c, -sg, lo) * scale
            dkr = _rope(dk_refs[0][:, sl] + dk_refs[1][:, sl] + dk_refs[2][:, sl], c, -sg, lo)
            dq, gq = _seg_rms_bwd(q_ref[:, sl], gq_ref[...], dqr, HEAD)
            dk, gk = _seg_rms_bwd(k_ref[:, sl], gk_ref[...], dkr, HEAD)
            dqo_ref[:, sl] = dq.astype(BF)
            dko_ref[:, sl] = dk.astype(BF)
            dgq, dgk = dgq + gq, dgk + gk
        dvo_ref[...] = (dv_refs[0][...] + dv_refs[1][...] + dv_refs[2][...]).astype(BF)
        _accum(dgq_ref, _fold_heads(dgq), i)
        _accum(dgk_ref, _fold_heads(dgk), i)

    return _rowwise(body, name="qk_bwd", nrows=S, tile=ROW_TILE,
                    row_ins=[(a, D_ATTN, 0) for a in (*dqs, *dks, *dvs)]
                    + [(proj, D_ATTN, 0), (proj, D_ATTN, 1), (pos_col, 1, 0)],
                    full_ins=[gq128, gk128, inv_tab], row_outs=[(D_ATTN, BF)] * 3,
                    acc_outs=[((1, LANES), F32)] * 2)


def _ssd_post_bwd(y_ssd, xbc, proj, dmix, dskip_b, g_ssm):
    def body(i, n, y_ref, xs_ref, z_ref, do_ref, d_ref, g_ref, dy_ref, dz_ref, dxs_ref, dd_ref, dg_ref):
        z, xs = z_ref[...], xs_ref[...]
        sg = _sigmoid(z)
        gate = z * sg
        y = y_ref[...] + d_ref[...] * xs
        dy2, dg = _seg_rms_bwd(y * gate, g_ref[...], do_ref[...], 256)
        dy = dy2 * gate
        dy_ref[...] = dy.astype(BF)
        dz_ref[...] = (dy2 * y * (sg * (1.0 + z * (1.0 - sg)))).astype(BF)
        dxs_ref[...] = dy * d_ref[...]
        _accum(dd_ref, jnp.sum(dy * xs, axis=0, keepdims=True), i)
        _accum(dg_ref, dg, i)

    return _rowwise(body, name="ssd_post_bwd", nrows=S, tile=ROW_TILE,
                    row_ins=[(y_ssd, D_SSM, 0), (xbc, D_SSM, 0), (proj, D_SSM, 3), (dmix, D_SSM, 1)],
                    full_ins=[dskip_b, g_ssm], row_outs=[(D_SSM, BF), (D_SSM, BF), (D_SSM, F32)],
                    acc_outs=[((1, D_SSM), F32), ((1, D_SSM), F32)])


def _ssd_bwd(xbc, dy, h_all, a_b, dt_b, at_r, dt_r):
    def kern(c_ref, b_ref, x_ref, dy_ref, hall_ref, ab_ref, dtb_ref, at_ref, dt_ref,
             dc_ref, daq_ref, dx_ref, db_ref, dak_ref, ddt_ref, ddtb_ref):
        half0 = lax.broadcasted_iota(jnp.int32, (BLK, LANES), 1) < HEAD
        hms = (half0, jnp.logical_not(half0))
        causal = lax.broadcasted_iota(jnp.int32, (BLK, BLK), 1) <= lax.broadcasted_iota(jnp.int32, (BLK, BLK), 0)
        row = lax.broadcasted_iota(jnp.int32, (BLK, LANES), 0)

        def step(t, carry_in):
            dh_next, carry = carry_in
            for cc in reversed(range(SSD_CB)):
                chunk = (NB // SSD_CB - 1 - t) * SSD_CB + cc
                rows = pl.ds(pl.multiple_of(chunk * BLK, BLK), BLK)
                ci, bi, x = c_ref[rows, :].astype(BF), b_ref[rows, :].astype(BF), x_ref[rows, :]
                dyv = dy_ref[rows, :].astype(F32)
                ab, dtb = ab_ref[rows, :], dtb_ref[rows, :]
                before = ab_ref[pl.ds(pl.multiple_of(jnp.maximum(chunk * BLK - 8, 0), 8), 8), :][7:8, :]
                a_prev = jnp.where(chunk == 0, 0.0, before)
                a_end = ab[BLK - 1:BLK, :]
                alpha = jnp.exp(ab - a_prev)
                e_end = jnp.exp(a_end - ab)
                beta = e_end * dtb
                t_all = alpha[BLK - 1:BLK, :]
                hc = hall_ref[rows, :]
                hcb, dhb = hc.astype(BF), dh_next.astype(BF)

                cb = _dot(ci, bi, NT)
                at, dtj = at_ref[chunk], dt_ref[chunk]
                dcb = jnp.zeros((BLK, BLK), F32)
                dx = jnp.zeros((BLK, LANES), F32)
                rsum = []
                for hd in range(2):
                    dym = jnp.where(hms[hd], dyv, 0.0).astype(BF)
                    lm = _ssd_decay(ab[:, HEAD * hd:HEAD * hd + 1], at[hd:hd + 1, :], causal)
                    dth = dtj[hd:hd + 1, :]
                    dw = _dot(dym, jnp.where(hms[hd], x, 0.0).astype(BF), NT)
                    g = dw * cb * lm
                    dx = dx + _dot((cb * lm * dth).astype(BF), dym, TN)
                    gcol = jnp.sum(g, axis=0, keepdims=True)
                    ddt_ref[chunk, hd:hd + 1, :] = gcol
                    dak_ref[chunk, hd:hd + 1, :] = gcol * dth
                    rsum.append(jnp.sum(g * dth, axis=1, keepdims=True))
                    dcb = dcb + dw * lm * dth
                dcb = dcb.astype(BF)

                g1 = (alpha * dyv).astype(BF)
                dalpha = dyv * _dot(ci, hcb, NN)
                g2 = _dot(bi, dhb, NN)
                dbeta = x * g2
                bx = (beta * x).astype(BF)
                dc_ref[rows, :] = _dot(dcb, bi, NN) + _dot(g1, hcb, NT)
                db_ref[rows, :] = _dot(dcb, ci, TN) + _dot(bx, dhb, NT)
                dx_ref[rows, :] = dx + beta * g2
                ddtb_ref[rows, :] = _seg_sum(e_end * dbeta, HEAD)
                ada, bdb = alpha * dalpha, beta * dbeta
                t_dt = t_all * jnp.sum(dh_next * hc, axis=0, keepdims=True)
                end_term = _seg_sum(jnp.broadcast_to(jnp.sum(bdb, axis=0, keepdims=True) + t_dt, (8, LANES)), HEAD)[0:1]
                prev_term = _seg_sum(jnp.broadcast_to(jnp.sum(ada, axis=0, keepdims=True) + t_dt, (8, LANES)), HEAD)[0:1]
                daq = jnp.where(half0, rsum[0], rsum[1]) + _seg_sum(ada - bdb, HEAD)
                daq_ref[rows, :] = daq + jnp.where(row == BLK - 1, end_term - carry, 0.0)
                carry = prev_term
                dh_next = t_all * dh_next + _dot(ci, g1, TN)
            return dh_next, carry

        lax.fori_loop(0, NB // SSD_CB, step, (jnp.zeros((BLK, LANES), F32), jnp.zeros((1, LANES), F32)))

    npair = D_SSM // LANES
    rowvec = pl.BlockSpec((None, NB, 2, BLK), lambda p: (p, 0, 0, 0))
    seq = pl.BlockSpec((S, LANES), lambda p: (0, p))
    return pl.pallas_call(
        kern, name="ssd_bwd", grid=(npair,),
        in_specs=[pl.BlockSpec((S, LANES), lambda p: (0, 12 + p // 2)),
                  pl.BlockSpec((S, LANES), lambda p: (0, 8 + p // 2)),
                  seq, seq, seq, seq, seq, rowvec, rowvec],
        out_specs=[seq, seq, seq, seq, rowvec, rowvec, seq],
        out_shape=[jax.ShapeDtypeStruct((S, D_SSM), F32)] * 4
        + [jax.ShapeDtypeStruct((npair, NB, 2, BLK), F32)] * 2 + [jax.ShapeDtypeStruct((S, D_SSM), F32)],
        compiler_params=_params("parallel"))(xbc, xbc, xbc, dy, h_all, a_b, dt_b, at_r, dt_r)


def _ssd_prep_bwd(daq, dak, ddt_row, ddt_lane, dt, proj, dt_bias128, a_log128):
    def body(i, n, daq_ref, dak_ref, dd_ref, dd2_ref, dt_ref, raw_ref, b_ref, al_ref, draw_ref, dal_ref, db_ref,
             carry):
        @pl.when(i == 0)
        def _():
            carry[...] = jnp.zeros_like(carry)

        a = -jnp.exp(al_ref[...])
        tri = (lax.broadcasted_iota(jnp.int32, (BLK, BLK), 0) <= lax.broadcasted_iota(jnp.int32, (BLK, BLK), 1))
        rev = _dot3(tri.astype(BF), _collect_heads(daq_ref[...]) - dak_ref[...]) + carry[0:1, :]
        carry[...] = jnp.broadcast_to(rev[0:1, :], carry.shape)
        dtv = dt_ref[...]
        draw = (dd_ref[...] + _collect_heads(dd2_ref[...]) + a * rev) * _sigmoid(raw_ref[...] + b_ref[...])
        draw_ref[...] = draw.astype(BF)
        _accum(dal_ref, jnp.sum(dtv * rev, axis=0, keepdims=True) * a, i)
        _accum(db_ref, jnp.sum(draw, axis=0, keepdims=True), i)

    return _rowwise(body, name="ssd_prep_bwd", nrows=S, tile=BLK, reverse=True,
                    row_ins=[(daq, D_SSM, 0), (dak, LANES, 0), (ddt_row, LANES, 0), (ddt_lane, D_SSM, 0), (dt, LANES, 0),
                             (proj, LANES, DT_COLBLK)],
                    full_ins=[dt_bias128, a_log128], row_outs=[(LANES, BF)],
                    acc_outs=[((1, LANES), F32), ((1, LANES), F32)], scratch=[pltpu.VMEM((8, LANES), F32)])


def _conv_bwd(proj, conv_w, conv_b, d1, d2, map1, map2, col0, ntiles, name):
    base = (4 * D_ATTN + col0) // LANES

    def kern(x_ref, w_ref, b_ref, d1_ref, d2_ref, dx_ref, dw_ref, db_ref):
        x = x_ref[...]
        c, shifted = _conv_taps(x, w_ref)
        c = c + b_ref[...]
        sg = _sigmoid(c)
        dc = (d1_ref[...] + d2_ref[...]) * (sg * (1.0 + c * (1.0 - sg)))
        rows = lax.broadcasted_iota(jnp.int32, x.shape, 0)
        dx = jnp.zeros_like(x)
        for w in range(CONV_W):
            k = CONV_W - 1 - w
            up = dc if k == 0 else jnp.where(rows < S - k, pltpu.roll(dc, S - k, 0), 0.0)
            dx = dx + up * w_ref[w:w + 1, :]
            dw_ref[w:w + 1, :] = jnp.sum(dc * shifted[w], axis=0, keepdims=True)
        dx_ref[...] = dx.astype(BF)
        db_ref[...] = jnp.sum(dc, axis=0, keepdims=True)

    c0 = col0 // LANES
    return pl.pallas_call(
        kern, name=name, grid=(ntiles,),
        in_specs=[pl.BlockSpec((S, LANES), lambda c: (0, base + c)),
                  pl.BlockSpec((CONV_W, LANES), lambda c: (0, c0 + c)),
                  pl.BlockSpec((1, LANES), lambda c: (0, c0 + c)),
                  pl.BlockSpec((S, LANES), lambda c: (0, map1(c))),
                  pl.BlockSpec((S, LANES), lambda c: (0, map2(c)))],
        out_specs=[pl.BlockSpec((S, LANES), lambda c: (0, c)), pl.BlockSpec((CONV_W, LANES), lambda c: (0, c)),
                   pl.BlockSpec((1, LANES), lambda c: (0, c))],
        out_shape=[jax.ShapeDtypeStruct((S, ntiles * LANES), BF), jax.ShapeDtypeStruct((CONV_W, ntiles * LANES), F32),
                   jax.ShapeDtypeStruct((1, ntiles * LANES), F32)],
        compiler_params=_params("parallel"))(proj, conv_w, conv_b, d1, d2)


def _adamw(w, m, v, parts, name):
    r, c = w.shape
    n_parts = parts.shape[0]
    tile = r if r <= 512 else (128 if c > 1024 else 256)
    assert r % tile == 0
    c1 = 1.0 - ADAM_B1 ** ADAM_STEP
    c2 = 1.0 - ADAM_B2 ** ADAM_STEP

    def kern(w_ref, m_ref, v_ref, p_ref, g_ref, d_ref, mo_ref, vo_ref):
        g = p_ref[0].astype(F32)
        for k in range(1, n_parts):
            g = g + p_ref[k].astype(F32)
        mn = ADAM_B1 * m_ref[...] + (1.0 - ADAM_B1) * g
        vn = ADAM_B2 * v_ref[...] + (1.0 - ADAM_B2) * (g * g)
        g_ref[...] = g
        mo_ref[...] = mn
        vo_ref[...] = vn
        d_ref[...] = -ADAM_LR * ((mn / c1) / (jnp.sqrt(vn / c2) + ADAM_EPS) + ADAM_WD * w_ref[...])

    blk = pl.BlockSpec((tile, c), lambda i: (i, 0))
    return pl.pallas_call(
        kern, name=name, grid=(r // tile,),
        in_specs=[blk, blk, blk, pl.BlockSpec((n_parts, tile, c), lambda i: (0, i, 0))],
        out_specs=[blk] * 4, out_shape=[jax.ShapeDtypeStruct((r, c), F32)] * 4,
        compiler_params=_params("parallel"))(w, m, v, parts)


MESH = pl.DeviceIdType.MESH
ANY = pl.BlockSpec(memory_space=pl.ANY)


def _put_own(gathered, shard):
    me = 4 * lax.axis_index("x") + 2 * lax.axis_index("y") + lax.axis_index("c")
    return lax.dynamic_update_slice(gathered, shard[None], (me,) + (0,) * shard.ndim)


def _all_gather(arrs, name, after=None):
    na = len(arrs)
    n_after = 0 if after is None else 1

    def kern(*refs):
        ins, outs = refs[:na], refs[na + n_after:2 * na + n_after]
        send_sems, recv_sems = refs[2 * na + n_after:]
        x, y, c = lax.axis_index("x"), lax.axis_index("y"), lax.axis_index("c")
        me, sibling = (x, y, c), (x, y, 1 - c)
        chips = [(1 - x, y), (x, 1 - y), (1 - x, 1 - y)]

        def copy(a, k, block, to, src=None):
            px, py, pc = block
            dst = outs[a].at[4 * px + 2 * py + pc]
            return pltpu.make_async_remote_copy(
                src_ref=dst if src is None else src, dst_ref=dst, send_sem=send_sems.at[a, k],
                recv_sem=recv_sems.at[a, k], device_id=to, device_id_type=MESH)

        first = []
        for a in range(na):
            first.append(copy(a, 0, me, sibling, src=ins[a]))
            first += [copy(a, 1 + j, me, (*chip, c), src=ins[a]) for j, chip in enumerate(chips)]
        for cp in first:
            cp.start()
        passed = []
        for a in range(na):
            for j, chip in enumerate(chips):
                copy(a, 1 + j, (*chip, c), me).wait_recv()
                fwd = copy(a, 4 + j, (*chip, c), sibling)
                fwd.start()
                passed.append(fwd)
        for a in range(na):
            copy(a, 0, sibling, me).wait_recv()
            for j, chip in enumerate(chips):
                copy(a, 4 + j, (*chip, 1 - c), me).wait_recv()
        for cp in first + passed:
            cp.wait_send()

    outs = pl.pallas_call(
        kern, name=name, in_specs=[ANY] * (na + n_after), out_specs=[ANY] * na,
        out_shape=[jax.ShapeDtypeStruct((N_DEV,) + a.shape, a.dtype) for a in arrs],
        scratch_shapes=[pltpu.SemaphoreType.DMA((na, 7)), pltpu.SemaphoreType.DMA((na, 7))],
    )(*arrs, *([] if after is None else [after]))
    return [_put_own(o, a) for o, a in zip(outs, arrs)]


HBM = pl.BlockSpec(memory_space=pltpu.HBM)
SEM = pl.BlockSpec(memory_space=pltpu.SEMAPHORE)
EFFECT = pltpu.SideEffectType.DATAFLOW_SIDE_EFFECTING
N_CHIP = 4
N_COPIES = {"gather": 3, "scatter": 3, "forward": 4, "pair": 4}


def _in_hbm(a):
    return pltpu.with_memory_space_constraint(a, pltpu.HBM)


def _chip_copies(kind, src_refs, land_refs, send_sems, recv_sems):
    x, y, c = lax.axis_index("x"), lax.axis_index("y"), lax.axis_index("c")
    chips = [(1 - x, y), (x, 1 - y), (1 - x, 1 - y)]
    n = N_COPIES[kind]
    cps = []

    def add(a, j, src, dst, to):
        cps.append(pltpu.make_async_remote_copy(
            src_ref=src, dst_ref=dst, send_sem=send_sems.at[n * a + j], recv_sem=recv_sems.at[n * a + j],
            device_id=to, device_id_type=MESH))

    for a in range(len(src_refs)):
        if kind == "gather":
            for j, (px, py) in enumerate(chips):
                add(a, j, src_refs[a], land_refs[a].at[4 * x + 2 * y + c], (px, py, c))
        elif kind == "scatter":
            for j, (px, py) in enumerate(chips):
                add(a, j, src_refs[a].at[2 * px + py], land_refs[a].at[2 * x + y], (px, py, c))
        elif kind == "forward":
            add(a, 0, src_refs[a], land_refs[a].at[4 * x + 2 * y + c], (x, y, 1 - c))
            for j, (px, py) in enumerate(chips):
                slot = land_refs[a].at[4 * px + 2 * py + c]
                add(a, 1 + j, slot, slot, (x, y, 1 - c))
        else:
            for q in range(N_CHIP):
                add(a, q, src_refs[a].at[2 * q + 1 - c], land_refs[a].at[q], (x, y, 1 - c))
    return cps


def _exchange_start(kind, srcs, lands, after, name):
    na = len(srcs)
    n_after = 0 if after is None else 1

    def kern(*refs):
        src_refs, land_refs = refs[:na], refs[na:2 * na]
        send_sems, recv_sems = refs[2 * na + n_after], refs[2 * na + n_after + 1]
        token = refs[-1]
        for cp in _chip_copies(kind, src_refs, land_refs, send_sems, recv_sems):
            cp.start()
        token[...] = jnp.zeros_like(token)

    bufs = list(srcs) + list(lands)
    res = pl.pallas_call(
        kern, name=name,
        out_shape=(pltpu.SemaphoreType.DMA((N_COPIES[kind] * na,)), pltpu.SemaphoreType.DMA((N_COPIES[kind] * na,)))
        + tuple(pltpu.HBM(b.shape, b.dtype) for b in bufs) + (jax.ShapeDtypeStruct((8, LANES), F32),),
        in_specs=[HBM] * (2 * na) + [ANY] * n_after,
        out_specs=(SEM, SEM) + (HBM,) * (2 * na) + (pl.BlockSpec(memory_space=pltpu.VMEM),),
        input_output_aliases={i: 2 + i for i in range(2 * na)},
        compiler_params=pltpu.CompilerParams(has_side_effects=EFFECT),
    )(*[_in_hbm(b) for b in bufs], *([] if after is None else [after]))
    return (res[0], res[1]), list(res[2:2 + na]), list(res[2 + na:2 + 2 * na]), res[-1]


def _exchange_wait(kind, sems, srcs, lands, after, name):
    na = len(srcs)

    def kern(*refs):
        src_refs, land_refs = refs[:na], refs[na:2 * na]
        send_sems, recv_sems = refs[2 * na], refs[2 * na + 1]
        for cp in _chip_copies(kind, src_refs, land_refs, send_sems, recv_sems):
            cp.wait_send()
            cp.wait_recv()

    bufs = list(srcs) + list(lands)
    res = pl.pallas_call(
        kern, name=name, out_shape=tuple(pltpu.HBM(b.shape, b.dtype) for b in bufs),
        in_specs=[HBM] * (2 * na) + [SEM, SEM, ANY], out_specs=(HBM,) * (2 * na),
        input_output_aliases={i: i for i in range(2 * na)},
        compiler_params=pltpu.CompilerParams(has_side_effects=EFFECT),
    )(*bufs, sems[0], sems[1], after)
    return list(res[:na]), list(res[na:])


def _gather_finish(shards, lands, name):
    na = len(shards)

    def kern(*refs):
        ins, outs = refs[:na], refs[2 * na:3 * na]
        send_sems, recv_sems = refs[3 * na:]
        x, y, c = lax.axis_index("x"), lax.axis_index("y"), lax.axis_index("c")
        chips = [(1 - x, y), (x, 1 - y), (1 - x, 1 - y)]

        def copy(a, k, slot, pc, src=None):
            dst = outs[a].at[slot + pc]
            return pltpu.make_async_remote_copy(
                src_ref=dst if src is None else src, dst_ref=dst, send_sem=send_sems.at[a, k],
                recv_sem=recv_sems.at[a, k], device_id=(x, y, 1 - c), device_id_type=MESH)

        sends = []
        for a in range(na):
            sends.append(copy(a, 0, 4 * x + 2 * y, c, src=ins[a]))
            sends += [copy(a, 1 + j, 4 * px + 2 * py, c) for j, (px, py) in enumerate(chips)]
        for cp in sends:
            cp.start()
        for a in range(na):
            copy(a, 0, 4 * x + 2 * y, 1 - c).wait_recv()
            for j, (px, py) in enumerate(chips):
                copy(a, 1 + j, 4 * px + 2 * py, 1 - c).wait_recv()
        for cp in sends:
            cp.wait_send()

    return pl.pallas_call(
        kern, name=name, in_specs=[ANY] * (2 * na), out_specs=[ANY] * na,
        out_shape=[jax.ShapeDtypeStruct(l.shape, l.dtype) for l in lands],
        input_output_aliases={na + a: a for a in range(na)},
        scratch_shapes=[pltpu.SemaphoreType.DMA((na, 4)), pltpu.SemaphoreType.DMA((na, 4))])(*shards, *lands)


def _pair_exchange(parts, name):
    na = len(parts)

    def kern(*refs):
        ins, got = refs[:na], refs[na:2 * na]
        send_sems, recv_sems = refs[2 * na:]
        x, y, c = lax.axis_index("x"), lax.axis_index("y"), lax.axis_index("c")
        sends = []
        for a in range(na):
            for q in range(N_CHIP):
                sends.append(pltpu.make_async_remote_copy(
                    src_ref=ins[a].at[2 * q + 1 - c], dst_ref=got[a].at[q], send_sem=send_sems.at[a, q],
                    recv_sem=recv_sems.at[a, q], device_id=(x, y, 1 - c), device_id_type=MESH))
        for cp in sends:
            cp.start()
        for cp in sends:
            cp.wait()

    return pl.pallas_call(
        kern, name=name, in_specs=[ANY] * na, out_specs=[ANY] * na,
        out_shape=[jax.ShapeDtypeStruct((N_CHIP,) + p.shape[1:], p.dtype) for p in parts],
        scratch_shapes=[pltpu.SemaphoreType.DMA((na, N_CHIP)), pltpu.SemaphoreType.DMA((na, N_CHIP))])(*parts)


def _pair_sum(parts, got, name):
    _, r, cdim = parts.shape
    tile = min(r, ROW_TILE)
    x, y, c = lax.axis_index("x"), lax.axis_index("y"), lax.axis_index("c")
    where = jnp.stack([c, 2 * x + y]).astype(jnp.int32)

    def kern(w_ref, a_ref, b_ref, q_ref, own_ref):
        s = (a_ref[...].astype(F32) + b_ref[...].astype(F32)).astype(BF)
        q_ref[...] = s

        @pl.when(pl.program_id(1) == w_ref[1])
        def _():
            own_ref[...] = s

    blk = pl.BlockSpec((None, tile, cdim), lambda i, q, w_ref: (q, i, 0))
    sums, own = pl.pallas_call(
        kern, name=name,
        out_shape=[jax.ShapeDtypeStruct((N_CHIP, r, cdim), BF), jax.ShapeDtypeStruct((r, cdim), BF)],
        grid_spec=pltpu.PrefetchScalarGridSpec(
            num_scalar_prefetch=1, grid=(r // tile, N_CHIP),
            in_specs=[pl.BlockSpec((None, tile, cdim), lambda i, q, w_ref: (2 * q + w_ref[0], i, 0)), blk],
            out_specs=[blk, pl.BlockSpec((tile, cdim), lambda i, q, w_ref: (i, 0))]),
        compiler_params=_params("parallel", "arbitrary"))(where, parts, got)
    land = lax.dynamic_update_slice(lax.empty((N_CHIP, r, cdim), BF), own[None], (2 * x + y, 0, 0))
    return sums, land


W_IN_COLS = D_IN // N_DEV


def _w_in_from_blocks(w8):
    def kern(x_ref, o_ref):
        for j in range(N_DEV):
            o_ref[:, W_IN_COLS * j:W_IN_COLS * (j + 1)] = x_ref[j]
        o_ref[:, D_IN:] = jnp.zeros((ROW_TILE, D_INP - D_IN), o_ref.dtype)

    return pl.pallas_call(
        kern, name="w_in_from_blocks", grid=(D // ROW_TILE,),
        in_specs=[pl.BlockSpec((N_DEV, ROW_TILE, W_IN_COLS), lambda i: (0, i, 0))],
        out_specs=pl.BlockSpec((ROW_TILE, D_INP), lambda i: (i, 0)),
        out_shape=jax.ShapeDtypeStruct((D, D_INP), w8.dtype), compiler_params=_params("parallel"))(w8)


def _w_in_to_blocks(g):
    def kern(x_ref, o_ref):
        for j in range(N_DEV):
            o_ref[j] = x_ref[:, W_IN_COLS * j:W_IN_COLS * (j + 1)]

    return pl.pallas_call(
        kern, name="w_in_to_blocks", grid=(D // ROW_TILE,),
        in_specs=[pl.BlockSpec((ROW_TILE, D_INP), lambda i: (i, 0))],
        out_specs=pl.BlockSpec((N_DEV, ROW_TILE, W_IN_COLS), lambda i: (0, i, 0)),
        out_shape=jax.ShapeDtypeStruct((N_DEV, D, W_IN_COLS), g.dtype), compiler_params=_params("parallel"))(g)


def _pad_lanes(v, width=LANES):
    return jnp.pad(v, ((0, 0), (0, width - v.shape[1])))


def _row_layout(t16):
    return t16.T.reshape(N_HEADS // 2, 2, NB, BLK).transpose(0, 2, 1, 3)


def _row_layout_inv(r):
    return r.transpose(0, 2, 1, 3).reshape(N_HEADS, S).T


SMALL = (("g_mix", D), ("g_q", HEAD), ("g_k", HEAD), ("g_attn_out", D_ATTN), ("conv_b", D_CONV),
         ("dt_bias", 16), ("a_log", 16), ("d_skip", 16), ("g_ssm_out", D_SSM), ("g_cross", D),
         ("g_mem", D), ("g_cq", CROSS_HEAD), ("g_ck", CROSS_HEAD), ("g_mlp", D))
SMALL_ROWS = -(-sum(-(-w // LANES) for _, w in SMALL) // 8) * 8


def _pack_small(vals):
    rows = [_pad_lanes(vals[n], -(-w // LANES) * LANES).reshape(-1, LANES) for n, w in SMALL]
    packed = jnp.concatenate(rows, axis=0)
    return jnp.pad(packed, ((0, SMALL_ROWS - packed.shape[0]), (0, 0)))


def _unpack_small(packed):
    out, r = {}, 0
    for n, w in SMALL:
        nr = -(-w // LANES)
        out[n] = packed[r:r + nr].reshape(1, nr * LANES)[:, :w]
        r += nr
    return out


BIG = ("w_in", "w_out", "w_cq", "w_ckv", "w_co", "w_up", "w_down")
WEIGHTS = ("g_mix", "w_in", "g_q", "g_k", "g_attn_out", "conv_w", "conv_b", "dt_bias", "a_log", "d_skip",
           "g_ssm_out", "w_out", "g_cross", "g_mem", "w_cq", "w_ckv", "g_cq", "g_ck", "w_co", "g_mlp", "w_up", "w_down")


REST = ("w_out", "w_cq", "w_ckv", "w_co", "w_up", "w_down")


class _Overlap:
    def __init__(self, shards, after):
        self.gather, self.forward = {}, {}
        self.names = {"a": REST[:4], "b": REST[4:5], "c": REST[5:]}
        for tag, names in self.names.items():
            lands = [_put_own(lax.empty((N_DEV,) + shards[n].shape, BF), shards[n]) for n in names]
            self.gather[tag] = _exchange_start("gather", [shards[n] for n in names], lands, after, "ag_start_" + tag)
            after = self.gather[tag][3]
        self.token = after
        self.groups = []
        self.pairs = {}

    def rest_mid(self, tag, after):
        sems, srcs, lands, _ = self.gather[tag]
        srcs, lands = _exchange_wait("gather", sems, srcs, lands, after, "ag_wait_" + tag)
        self.forward[tag] = _exchange_start("forward", srcs, lands, None, "ag_fwd_start_" + tag)
        return self.forward[tag][3]

    def rest(self, tag, after):
        sems, srcs, lands, _ = self.forward[tag]
        _, lands = _exchange_wait("forward", sems, srcs, lands, after, "ag_fwd_wait_" + tag)
        wf = dict(zip(self.names[tag], lands))
        for n in wf:
            if n in ("w_out", "w_cq", "w_ckv", "w_down"):
                wf[n] = wf[n].reshape(-1, wf[n].shape[-1])
        return wf

    def pair_start(self, name, grad):
        got = lax.empty((N_CHIP,) + grad.shape[1:], grad.dtype)
        self.pairs[name] = _exchange_start("pair", [grad], [got], None, "rs_pair_start_" + name)
        return self.pairs[name][3]

    def emit(self, grads, after):
        names, tag = list(grads), "_%d" % len(self.groups)
        grads, got = dict(grads), {}
        for n in names:
            if n in self.pairs:
                sems, srcs, lands, _ = self.pairs[n]
                srcs, lands = _exchange_wait("pair", sems, srcs, lands, after, "rs_pair_wait_" + n)
                grads[n], got[n] = srcs[0], lands[0]
        sync = [n for n in names if n not in got]
        if sync:
            got.update(zip(sync, _pair_exchange([grads[n] for n in sync], "rs_pair" + tag)))
        sums = [_pair_sum(grads[n], got[n], "rs_sum_" + n) for n in names]
        sems, srcs, lands, token = _exchange_start("scatter", [q for q, _ in sums], [l for _, l in sums], None,
                                                   "rs_chip_start" + tag)
        self.groups.append((names, sems, srcs, lands))
        return token

    def finish(self, gi, after):
        names, sems, srcs, lands = self.groups[gi]
        _, lands = _exchange_wait("scatter", sems, srcs, lands, after, "rs_chip_wait_%d" % gi)
        return dict(zip(names, lands))


def _tie(v, token):
    return v if token is None else v + token[0:1, 0:1]


def _local_step(x, mem, pos_col, target, p, wf, hooks=None):
    p = dict(p)
    inv = np.zeros((1, LANES), np.float32)
    freq = (ROPE_THETA ** (-2.0 * np.arange(ROT // 2, dtype=np.float32) / ROT)).astype(np.float32)
    for l in range(LANES):
        if l % HEAD < ROT:
            inv[0, l] = freq[(l % HEAD) % (ROT // 2)]
    inv_tab = jnp.asarray(inv)
    gq128, gk128 = jnp.tile(p["g_q"], (1, 2)), jnp.tile(p["g_k"], (1, 2))
    dtb128, alog128 = _pad_lanes(p["dt_bias"]), _pad_lanes(p["a_log"])
    dskip_b = jnp.repeat(p["d_skip"], HEAD, axis=1)
    conv_w, conv_b = wf["conv_w"], p["conv_b"]

    h = _rms_fwd(x, p["g_mix"], "rms_mix")
    proj = _matmul(h, wf["w_in"], mode="nn", name="mm_in", tm=1024, tn=1280, tk=D)
    qr, kr = _qk_prep(proj, pos_col, gq128, gk128, inv_tab)
    dilations = (1, 4, 16)
    fwd = [_attn_fwd(qr, kr, proj, d, "attn_fwd_d%d" % d) for d in dilations]
    attn, lse, attn_n = _attn_merge([o for o, _ in fwd], [l for _, l in fwd], p["g_attn_out"])

    xbc = _conv_fwd(proj, conv_w, conv_b)
    token = None if hooks is None else hooks.rest_mid("a", xbc)
    dt, a_cs, dt_b, a_b = _ssd_prep(proj, _tie(dtb128, token), alog128)
    at_r, dt_r = _row_layout(a_cs[:, :N_HEADS]), _row_layout(dt[:, :N_HEADS])
    y_ssd, h_all = _ssd_fwd(xbc, a_b, dt_b, at_r, dt_r)
    ssm_n = _ssd_post(y_ssd, xbc, proj, dskip_b, p["g_ssm_out"])

    if hooks is not None:
        wf = {**wf, **hooks.rest("a", ssm_n)}
    mix = jnp.concatenate([attn_n, ssm_n], axis=1)
    x1 = _matmul(mix, wf["w_out"], mode="nn", name="mm_out", tm=1024, tn=1024, tk=D,
                 extras=(x,), epilogue=lambda acc, r: (acc + r,))
    hc = _rms_fwd(x1, _tie(p["g_cross"], None if hooks is None else hooks.rest_mid("b", x1)), "rms_cross")
    memh = _rms_fwd(mem, p["g_mem"], "rms_mem")
    qc = _matmul(hc, wf["w_cq"], mode="nn", name="mm_cq", tm=1024, tn=512, tk=D)
    kv = _matmul(memh, wf["w_ckv"], mode="nn", name="mm_ckv", tm=N_MEM, tn=1024, tk=D)
    oc = _cross_fwd(qc, kv, p["g_cq"], p["g_ck"])
    x2 = _matmul(oc, wf["w_co"], mode="nn", name="mm_co", tm=1024, tn=256, tk=512, b_cb=256,
                 extras=(x1,), epilogue=lambda acc, r: (acc + r,))
    hm = _rms_fwd(x2, p["g_mlp"], "rms_mlp")
    token = None
    if hooks is not None:
        wf = {**wf, **hooks.rest("b", hm)}
        token = hooks.rest_mid("c", hm)

    def up_epi(acc):
        ru = jnp.maximum(acc, 0.0)
        return ru * ru, 2.0 * ru

    act, relu2 = _matmul(hm, wf["w_up"], mode="nn", name="mm_up", tm=1024, tn=1024, tk=D, b_cb=1024,
                         out_dtypes=(BF, BF), epilogue=up_epi, after=token)
    if hooks is not None:
        wf = {**wf, **hooks.rest("c", act)}

    def loss_epi(acc, r, t):
        d = (acc + r - t) * (1.0 / D)
        return d, d

    dy, dyb = _matmul(act, wf["w_down"], mode="nn", name="mm_down", tm=512, tn=1024, tk=2048, extras=(x2, target),
                      out_dtypes=(F32, BF), epilogue=loss_epi)
    loss_part = _loss_sum(dy)[0, 0] * (0.5 * D)

    gb, gs = {}, {}
    gb["w_down"] = _matmul(act, dyb, mode="tn", name="mm_down_dw", tm=1024, tn=1024, tk=S,
                           out_dtypes=(BF,)).reshape(N_DEV, D_FF // N_DEV, D)
    token = None if hooks is None else hooks.pair_start("w_down", gb["w_down"])
    du = _matmul(dyb, wf["w_down"], mode="nt", name="mm_down_dx", tm=1024, tn=1024, tk=D, extras=(relu2,),
                 out_dtypes=(BF,), epilogue=lambda acc, r: (acc * r.astype(F32),), after=token)
    gb["w_up"] = _matmul(hm, du, mode="tn", name="mm_up_dw", tm=1024, tn=1024, tk=S, out_dtypes=(BF,), out_cb=1024)
    token = None if hooks is None else hooks.pair_start("w_up", gb["w_up"])
    dhm = _matmul(du, wf["w_up"], mode="nt", name="mm_up_dx", tm=1024, tn=1024, tk=2048, b_cb=1024, after=token)
    if hooks is not None:
        p["g_mlp"] = _tie(p["g_mlp"], hooks.emit({n: gb[n] for n in ("w_down", "w_up")}, dhm))
    dx2, dx2b, gs["g_mlp"] = _rms_bwd_call(x2, p["g_mlp"], dhm, dy, "rms_mlp_bwd")

    doc = _matmul(dx2b, wf["w_co"], mode="nt", name="mm_co_dx", tm=1024, tn=512, tk=256, b_cb=256)
    gb["w_co"] = _matmul(oc, dx2b, mode="tn", name="mm_co_dw", tm=512, tn=256, tk=S, out_dtypes=(BF,), out_cb=256)
    dqc, dkn, dvc, gs["g_cq"] = _cross_bwd(qc, doc, kv, p["g_cq"], p["g_ck"])
    dkv, gs["g_ck"] = _cross_kv_bwd(kv, dkn, dvc, p["g_ck"])
    gb["w_cq"] = _matmul(hc, dqc, mode="tn", name="mm_cq_dw", tm=1024, tn=512, tk=S,
                         out_dtypes=(BF,)).reshape(N_DEV, D // N_DEV, D_CROSS)
    dhc = _matmul(dqc, wf["w_cq"], mode="nt", name="mm_cq_dx", tm=1024, tn=1024, tk=512)
    gb["w_ckv"] = _matmul(memh, dkv, mode="tn", name="mm_ckv_dw", tm=1024, tn=1024, tk=N_MEM,
                          out_dtypes=(BF,)).reshape(N_DEV, D // N_DEV, 2 * D_CROSS)
    dmemh = _matmul(dkv, wf["w_ckv"], mode="nt", name="mm_ckv_dx", tm=N_MEM, tn=1024, tk=1024)
    (gs["g_mem"],) = _rms_bwd_call(mem, p["g_mem"], dmemh, None, "rms_mem_bwd")
    dx1, dx1b, gs["g_cross"] = _rms_bwd_call(x1, p["g_cross"], dhc, dx2, "rms_cross_bwd")

    dmix = _matmul(dx1b, wf["w_out"], mode="nt", name="mm_out_dx", tm=1024, tn=1024, tk=D)
    gb["w_out"] = _matmul(mix, dx1b, mode="tn", name="mm_out_dw", tm=1024, tn=1024, tk=S,
                          out_dtypes=(BF,)).reshape(N_DEV, D // N_DEV, D)
    if hooks is not None:
        p["g_attn_out"] = _tie(p["g_attn_out"],
                               hooks.emit({n: gb[n] for n in ("w_co", "w_cq", "w_ckv", "w_out")}, dmix))

    dattn, delta, gs["g_attn_out"] = _attn_merge_bwd(dmix, attn, p["g_attn_out"])
    bwd = [_attn_bwd(qr, kr, proj, dattn, lse, delta, d, "attn_bwd_d%d" % d) for d in dilations]
    dq_pre, dk_pre, dv_pre, dgq, dgk = _qk_bwd([t[0] for t in bwd], [t[1] for t in bwd], [t[2] for t in bwd],
                                               proj, pos_col, gq128, gk128, inv_tab)
    gs["g_q"], gs["g_k"] = dgq[:, :HEAD], dgk[:, :HEAD]

    dy_ssd, dz, dxs_skip, dd_lane, gs["g_ssm_out"] = _ssd_post_bwd(y_ssd, xbc, proj, dmix, dskip_b, p["g_ssm_out"])
    gs["d_skip"] = dd_lane.reshape(N_HEADS, HEAD).sum(axis=1).reshape(1, N_HEADS)
    dc_pair, daq_b, dx_ssd, db_pair, dak_r, ddt_r, ddt_b = _ssd_bwd(xbc, dy_ssd, h_all, a_b, dt_b, at_r, dt_r)
    dak = _pad_lanes(_row_layout_inv(dak_r))
    ddt_direct = _pad_lanes(_row_layout_inv(ddt_r))
    ddt_raw, dalog, dbias = _ssd_prep_bwd(daq_b, dak, ddt_direct, ddt_b, dt, proj, dtb128, alog128)
    gs["a_log"], gs["dt_bias"] = dalog[:, :N_HEADS], dbias[:, :N_HEADS]
    same = lambda c: c
    dxbc_x, dcw_x, dcb_x = _conv_bwd(proj, conv_w, conv_b, dxs_skip, dx_ssd, same, same, 0, 8, "conv_bwd_x")
    dxbc_b, dcw_b, dcb_b = _conv_bwd(proj, conv_w, conv_b, db_pair, db_pair, lambda c: 2 * c, lambda c: 2 * c + 1,
                                     D_SSM, 4, "conv_bwd_b")
    dxbc_c, dcw_c, dcb_c = _conv_bwd(proj, conv_w, conv_b, dc_pair, dc_pair, lambda c: 2 * c, lambda c: 2 * c + 1,
                                     D_SSM + 512, 4, "conv_bwd_c")
    g_conv_w = jnp.concatenate([dcw_x, dcw_b, dcw_c], axis=1)
    gs["conv_b"] = jnp.concatenate([dcb_x, dcb_b, dcb_c], axis=1)

    pieces = [dq_pre, dk_pre, dv_pre, dz, dxbc_x, dxbc_b, dxbc_c, ddt_raw]
    pieces.append(jnp.zeros((S, D_INP - sum(t.shape[1] for t in pieces)), BF))
    dproj = jnp.concatenate(pieces, axis=1)
    dw_in = _matmul(h, dproj, mode="tn", name="mm_in_dw", tm=1024, tn=1280, tk=S, out_dtypes=(BF,))
    gb["w_in"] = _w_in_to_blocks(dw_in)
    token = None if hooks is None else hooks.emit({"w_in": gb["w_in"]}, dw_in)
    dh = _matmul(dproj, wf["w_in"], mode="nt", name="mm_in_dx", tm=1024, tn=512, tk=D_INP // 2, after=token)
    grad_x, _, gs["g_mix"] = _rms_bwd_call(x, p["g_mix"], dh, dx1, "rms_mix_bwd")
    return loss_part, grad_x, gb, gs, g_conv_w


def kernel(x, mem, positions, g_mix, w_in, g_q, g_k, g_attn_out, conv_w, conv_b, dt_bias, a_log, d_skip, g_ssm_out, w_out, g_cross, g_mem, w_cq, w_ckv, g_cq, g_ck, w_co, g_mlp, w_up, w_down, loss_target, m_g_mix, m_w_in, m_g_q, m_g_k, m_g_attn_out, m_conv_w, m_conv_b, m_dt_bias, m_a_log, m_d_skip, m_g_ssm_out, m_w_out, m_g_cross, m_g_mem, m_w_cq, m_w_ckv, m_g_cq, m_g_ck, m_w_co, m_g_mlp, m_w_up, m_w_down, v_g_mix, v_w_in, v_g_q, v_g_k, v_g_attn_out, v_conv_w, v_conv_b, v_dt_bias, v_a_log, v_d_skip, v_g_ssm_out, v_w_out, v_g_cross, v_g_mem, v_w_cq, v_w_ckv, v_g_cq, v_g_ck, v_w_co, v_g_mlp, v_w_up, v_w_down):
    args = dict(locals())
    w = {n: args[n] for n in WEIGHTS}
    m = {n: args["m_" + n] for n in WEIGHTS}
    v = {n: args["v_" + n] for n in WEIGHTS}
    small = {n: w[n] for n, _ in SMALL}
    me = 4 * lax.axis_index("x") + 2 * lax.axis_index("y") + lax.axis_index("c")

    w_in_g, conv_w_g = _all_gather([w["w_in"][0].astype(BF), w["conv_w"][0]], "ag_first")
    wf = {"w_in": _w_in_from_blocks(w_in_g), "conv_w": conv_w_g.transpose(1, 0, 2).reshape(CONV_W, D_CONV)}
    overlap = _Overlap({n: w[n][0].astype(BF) for n in REST}, w_in_g)
    p = dict(small)
    p["g_mix"] = _tie(p["g_mix"], overlap.token)

    loss_part, grad_x, _, gs, g_conv_w = _local_step(
        x[0], mem[0], positions.reshape(S, 1), loss_target[0], p, wf, overlap)
    loss = lax.psum(loss_part, ("x", "y", "c"))

    out = {}
    last = grad_x
    for gi in range(3):
        for n, parts in overlap.finish(gi, last).items():
            res_n = _adamw(w[n][0], m[n][0], v[n][0], parts, "adamw_" + n)
            out[n] = [t.reshape(w[n].shape) for t in res_n]
            last = res_n[0]

    sm_parts, cw_parts = _all_gather([_pack_small(gs), g_conv_w], "ag_small_grads", after=last)
    sm = [_unpack_small(t) for t in _adamw(_pack_small(small), _pack_small({n: m[n] for n, _ in SMALL}),
                                           _pack_small({n: v[n] for n, _ in SMALL}), sm_parts, "adamw_small")]
    for n, _ in SMALL:
        out[n] = [t[n] for t in sm]
    cols = D_CONV // N_DEV
    cw_mine = lax.dynamic_slice_in_dim(cw_parts, me * cols, cols, axis=2)
    out["conv_w"] = [t.reshape(w["conv_w"].shape) for t in
                     _adamw(w["conv_w"][0], m["conv_w"][0], v["conv_w"][0], cw_mine, "adamw_conv_w")]

    res = [loss, grad_x.reshape(x.shape)]
    for k in range(4):
        res += [out[n][k] for n in WEIGHTS]
    return tuple(res)
```

```python
import functools
import math

import numpy as np
import jax
import jax.numpy as jnp
from jax import lax
from jax.experimental import pallas as pl
from jax.experimental.pallas import tpu as pltpu

F32 = jnp.float32
BF = jnp.bfloat16

N_DEV = 8
S = 2048
D = 2048
D_ATTN = 1024
N_HEADS = 16
HEAD = 64
ROT = 16
ROPE_THETA = 500000.0
D_SSM = 1024
D_CONV = 2048
CONV_W = 4
N_MEM = 256
D_CROSS = 512
CROSS_HEAD = 128
D_FF = 8192
D_IN = 6160
D_INP = 6400
DT_COLBLK = (4 * D_ATTN + D_CONV) // 128
EPS = 1e-6
BLK = 128
NB = S // BLK
LANES = 128
NEG = -1e30

ADAM_LR = 0.001
ADAM_B1 = 0.9
ADAM_B2 = 0.999
ADAM_EPS = 1e-08
ADAM_WD = 0.01
ADAM_STEP = 10

VMEM_LIMIT_BYTES = 48 * 1024 * 1024
ROW_TILE = 256


def _params(*sem):
    return pltpu.CompilerParams(dimension_semantics=sem, vmem_limit_bytes=VMEM_LIMIT_BYTES)


def _matmul(a, b, *, mode, name, tm, tn, tk, out_dtypes=(F32,), b_cb=None, out_cb=None,
            extras=(), epilogue=None, after=None):
    if mode == "tn":
        kk, m = a.shape
    else:
        m, kk = a.shape
    if b_cb is None:
        br, bc = b.shape
    else:
        br, bc = b.shape[1], N_DEV * b_cb
    n = br if mode == "nt" else bc
    assert m % tm == 0 and n % tn == 0 and kk % tk == 0, (name, m, n, kk)
    nk = kk // tk
    grid = (m // tm, n // tn, nk)

    if mode == "tn":
        a_spec = pl.BlockSpec((tk, tm), lambda i, j, k: (k, i))
    else:
        a_spec = pl.BlockSpec((tm, tk), lambda i, j, k: (i, k))
    if mode == "nt":
        b_blk, b_idx = (tn, tk), (lambda i, j, k: (j, k))
    else:
        b_blk, b_idx = (tk, tn), (lambda i, j, k: (k, j))
    group = 1
    if b_cb is None:
        b_spec = pl.BlockSpec(b_blk, b_idx)
    elif mode == "nt" and tk > b_cb:
        assert tk % b_cb == 0
        group = tk // b_cb
        b_spec = pl.BlockSpec((group, tn, b_cb), lambda i, j, k: (k, j, 0))
    else:
        tc = b_blk[1]
        assert b_cb % tc == 0
        per = b_cb // tc

        def b_idx3(i, j, k):
            r, c = b_idx(i, j, k)
            return (c // per, r, c % per)
        b_spec = pl.BlockSpec((None,) + b_blk, b_idx3)
    ex_specs = [pl.BlockSpec((tm, tn), lambda i, j, k: (i, j)) for _ in extras]
    if out_cb is None:
        out_shape = [jax.ShapeDtypeStruct((m, n), dt) for dt in out_dtypes]
        out_specs = [pl.BlockSpec((tm, tn), lambda i, j, k: (i, j)) for _ in out_dtypes]
    else:
        assert out_cb % tn == 0
        pero = out_cb // tn
        out_shape = [jax.ShapeDtypeStruct((N_DEV, m, out_cb), dt) for dt in out_dtypes]
        out_specs = [pl.BlockSpec((None, tm, tn), lambda i, j, k: (j // pero, i, j % pero)) for _ in out_dtypes]
    dn = {"nn": (((1,), (0,)), ((), ())), "nt": (((1,), (1,)), ((), ())), "tn": (((0,), (0,)), ((), ()))}[mode]
    ne, no = len(extras), len(out_dtypes)
    n_after = 0 if after is None else 1

    def kern(a_ref, b_ref, *rest):
        ex_refs, out_refs = rest[:ne], rest[ne + n_after:ne + n_after + no]

        def finish(acc):
            outs = (acc,) if epilogue is None else epilogue(acc, *[r[...] for r in ex_refs])
            for r, o in zip(out_refs, outs):
                r[...] = o.astype(r.dtype)

        if group == 1:
            part = lax.dot_general(a_ref[...].astype(BF), b_ref[...].astype(BF), dn, preferred_element_type=F32)
        else:
            part = sum(lax.dot_general(a_ref[:, g * b_cb:(g + 1) * b_cb].astype(BF), b_ref[g].astype(BF), dn,
                                       preferred_element_type=F32) for g in range(group))
        if nk == 1:
            finish(part)
            return
        acc_ref = rest[ne + n_after + no]
        k = pl.program_id(2)

        @pl.when(k == 0)
        def _():
            acc_ref[...] = part

        @pl.when(k > 0)
        def _():
            acc_ref[...] += part

        @pl.when(k == nk - 1)
        def _():
            finish(acc_ref[...])

    res = pl.pallas_call(
        kern, name=name, grid=grid, in_specs=[a_spec, b_spec] + ex_specs + [ANY] * n_after, out_specs=out_specs,
        out_shape=out_shape, scratch_shapes=[pltpu.VMEM((tm, tn), F32)] if nk > 1 else [],
        compiler_params=_params("parallel", "parallel", "arbitrary"),
    )(a, b, *extras, *([] if after is None else [after]))
    return res[0] if no == 1 else tuple(res)


def _rowwise(body, *, name, nrows, tile, row_ins, full_ins=(), row_outs=(), acc_outs=(), scratch=(),
             reverse=False):
    n = nrows // tile

    def ridx(i):
        return (n - 1 - i) if reverse else i

    in_specs, args = [], []
    for arr, width, cb in row_ins:
        in_specs.append(pl.BlockSpec((tile, width), lambda i, cb=cb: (ridx(i), cb)))
        args.append(arr)
    for arr in full_ins:
        in_specs.append(pl.BlockSpec(arr.shape, lambda i, nd=arr.ndim: (0,) * nd))
        args.append(arr)
    out_shape, out_specs = [], []
    for width, dt in row_outs:
        out_shape.append(jax.ShapeDtypeStruct((nrows, width), dt))
        out_specs.append(pl.BlockSpec((tile, width), lambda i: (ridx(i), 0)))
    for shp, dt in acc_outs:
        out_shape.append(jax.ShapeDtypeStruct(shp, dt))
        out_specs.append(pl.BlockSpec(shp, lambda i, nd=len(shp): (0,) * nd))

    def kern(*refs):
        body(pl.program_id(0), n, *refs)

    return pl.pallas_call(kern, name=name, grid=(n,), in_specs=in_specs, out_specs=out_specs,
                          out_shape=out_shape, scratch_shapes=list(scratch),
                          compiler_params=_params("arbitrary"))(*args)


def _accum(ref, val, i):
    @pl.when(i == 0)
    def _():
        ref[...] = val

    @pl.when(i > 0)
    def _():
        ref[...] += val


def _sigmoid(x):
    return 1.0 / (1.0 + jnp.exp(-x))


def _rms_n(x):
    r = lax.rsqrt(jnp.mean(x * x, axis=-1, keepdims=True) + EPS)
    return x * r, r


def _rms_bwd(x, g, dh):
    n, r = _rms_n(x)
    dn = dh * g
    dx = r * (dn - n * jnp.mean(dn * n, axis=-1, keepdims=True))
    return dx, jnp.sum(dh * n, axis=0, keepdims=True)


def _seg_sum(x, seg):
    t, w = x.shape
    tiles = [x[:, LANES * j:LANES * (j + 1)] for j in range(w // LANES)]
    outs = []
    if seg == 64:
        lo = lax.broadcasted_iota(jnp.int32, (t, LANES), 1) < 64
        for xt in tiles:
            s_lo = jnp.sum(jnp.where(lo, xt, 0.0), axis=-1, keepdims=True)
            s_hi = jnp.sum(jnp.where(lo, 0.0, xt), axis=-1, keepdims=True)
            outs.append(jnp.where(lo, s_lo, s_hi))
    else:
        sums = [jnp.sum(xt, axis=-1, keepdims=True) for xt in tiles]
        if seg == 256:
            sums = [sums[2 * (j // 2)] + sums[2 * (j // 2) + 1] for j in range(len(sums))]
        else:
            assert seg == 128
        outs = [jnp.broadcast_to(s, (t, LANES)) for s in sums]
    return outs[0] if len(outs) == 1 else jnp.concatenate(outs, axis=1)


def _seg_rms_n(x, seg):
    r = lax.rsqrt(_seg_sum(x * x, seg) * (1.0 / seg) + EPS)
    return x * r, r


def _seg_rms_bwd(x, g, dy, seg):
    n, r = _seg_rms_n(x, seg)
    dn = dy * g
    dx = r * (dn - n * (_seg_sum(dn * n, seg) * (1.0 / seg)))
    return dx, jnp.sum(dy * n, axis=0, keepdims=True)


def _rope_tables(pos_col, inv_tab, t):
    ang = pos_col.astype(F32) * inv_tab
    idx = lax.broadcasted_iota(jnp.int32, (t, LANES), 1) % HEAD
    cos, sin = jnp.cos(ang), jnp.sin(ang)
    c = jnp.where(idx < ROT, cos, 1.0)
    sg = jnp.where(idx < ROT // 2, -sin, jnp.where(idx < ROT, sin, 0.0))
    return c, sg, idx < ROT // 2


def _rope(x, c, sg, lo):
    partner = jnp.where(lo, pltpu.roll(x, LANES - ROT // 2, 1), pltpu.roll(x, ROT // 2, 1))
    return x * c + partner * sg


def _fold_heads(v):
    v8 = jnp.broadcast_to(v, (8, LANES))
    return (v8 + pltpu.roll(v8, HEAD, 1))[0:1]


def _dot(a, b, dn):
    return lax.dot_general(a, b, (dn, ((), ())), preferred_element_type=F32)


NN = ((1,), (0,))
NT = ((1,), (1,))
TN = ((0,), (0,))


def _dot3(l01, x):
    x1 = x.astype(BF)
    r1 = x - x1.astype(F32)
    x2 = r1.astype(BF)
    x3 = (r1 - x2.astype(F32)).astype(BF)
    return _dot(l01, x1, NN) + _dot(l01, x2, NN) + _dot(l01, x3, NN)


def _rms_fwd(x, g, name):
    rows = x.shape[0]

    def body(i, n, x_ref, g_ref, o_ref):
        nx, _ = _rms_n(x_ref[...])
        o_ref[...] = (nx * g_ref[...]).astype(BF)

    return _rowwise(body, name=name, nrows=rows, tile=min(ROW_TILE, rows), row_ins=[(x, D, 0)],
                    full_ins=[g], row_outs=[(D, BF)])[0]


def _qk_prep(proj, pos_col, gq128, gk128, inv_tab):
    scale = HEAD ** -0.5

    def body(i, n, q_ref, k_ref, p_ref, gq_ref, gk_ref, it_ref, qo_ref, ko_ref):
        t = q_ref.shape[0]
        c, sg, lo = _rope_tables(p_ref[...], it_ref[...], t)
        for j in range(D_ATTN // LANES):
            sl = slice(LANES * j, LANES * (j + 1))
            qn, _ = _seg_rms_n(q_ref[:, sl], HEAD)
            kn, _ = _seg_rms_n(k_ref[:, sl], HEAD)
            qo_ref[:, sl] = _rope(qn * gq_ref[...], c, sg, lo) * scale
            ko_ref[:, sl] = _rope(kn * gk_ref[...], c, sg, lo)

    return _rowwise(body, name="qk_prep", nrows=S, tile=ROW_TILE,
                    row_ins=[(proj, D_ATTN, 0), (proj, D_ATTN, 1), (pos_col, 1, 0)],
                    full_ins=[gq128, gk128, inv_tab], row_outs=[(D_ATTN, F32)] * 2)


ATTN_QB = 4
V_COLS = pl.BlockSpec((S, LANES), lambda p: (0, 2 * D_ATTN // LANES + p))


def _band(b, d):
    nb = NB // d
    r, n = b // nb, b % nb
    width, kn = (BLK, n) if nb == 1 else (2 * BLK, jnp.maximum(n - 1, 0))

    def rows(first_member, count):
        if d == 1:
            return pl.ds(pl.multiple_of(first_member, BLK), count)
        return pl.ds(first_member * d + r, count, stride=d)

    qm = n * BLK + (lax.broadcasted_iota(jnp.int32, (2 * BLK, width), 0) & (BLK - 1))
    km = kn * BLK + lax.broadcasted_iota(jnp.int32, (2 * BLK, width), 1)
    valid = km <= qm
    if nb > 1:
        valid = valid & (km >= qm - BLK)
    return rows(n * BLK, BLK), rows(kn * BLK, width), valid


def _attn_fwd(q, k, v, d, name):
    def kern(q_ref, k_ref, v_ref, o_ref, l_ref):
        half0 = lax.broadcasted_iota(jnp.int32, (BLK, LANES), 1) < HEAD

        def group(g, carry):
            for bb in range(ATTN_QB):
                q_rows, k_rows, valid = _band(g * ATTN_QB + bb, d)
                q = q_ref[q_rows, :]
                kb, vb = k_ref[k_rows, :].astype(BF), v_ref[k_rows, :].astype(BF)
                q2 = jnp.concatenate([jnp.where(half0, q, 0.0), jnp.where(half0, 0.0, q)], axis=0).astype(BF)
                s = jnp.where(valid, _dot(q2, kb, NT), NEG)
                m = jnp.max(s, axis=-1, keepdims=True)
                p = jnp.exp(s - m)
                den = jnp.sum(p, axis=-1, keepdims=True)
                o2 = _dot(p.astype(BF), vb, NN) / den
                l2 = m + jnp.log(den)
                o_ref[q_rows, :] = jnp.where(half0, o2[:BLK], o2[BLK:])
                l_ref[q_rows, :] = jnp.where(half0, l2[:BLK], l2[BLK:])
            return carry

        lax.fori_loop(0, NB // ATTN_QB, group, 0)

    seq = pl.BlockSpec((S, LANES), lambda p: (0, p))
    return pl.pallas_call(
        kern, name=name, grid=(D_ATTN // LANES,), in_specs=[seq, seq, V_COLS], out_specs=[seq, seq],
        out_shape=[jax.ShapeDtypeStruct((S, D_ATTN), F32)] * 2,
        compiler_params=_params("parallel"))(q, k, v)


def _attn_merge(os_, ls_, g_attn):
    def body(i, n, o1, o2, o3, l1, l2, l3, g_ref, a_ref, lse_ref, an_ref):
        la, lb, lc = l1[...], l2[...], l3[...]
        m = jnp.maximum(jnp.maximum(la, lb), lc)
        ea, eb, ec = jnp.exp(la - m), jnp.exp(lb - m), jnp.exp(lc - m)
        den = ea + eb + ec
        attn = (ea * o1[...] + eb * o2[...] + ec * o3[...]) / den
        a_ref[...] = attn
        lse_ref[...] = m + jnp.log(den)
        nx, _ = _rms_n(attn)
        an_ref[...] = (nx * g_ref[...]).astype(BF)

    return _rowwise(body, name="attn_merge", nrows=S, tile=ROW_TILE,
                    row_ins=[(a, D_ATTN, 0) for a in (*os_, *ls_)], full_ins=[g_attn],
                    row_outs=[(D_ATTN, F32), (D_ATTN, F32), (D_ATTN, BF)])


def _conv_taps(x, w_ref):
    rows = lax.broadcasted_iota(jnp.int32, x.shape, 0)
    shifted = []
    for w in range(CONV_W):
        k = CONV_W - 1 - w
        shifted.append(x if k == 0 else jnp.where(rows >= k, pltpu.roll(x, k, 0), 0.0))
    acc = shifted[0] * w_ref[0:1, :]
    for w in range(1, CONV_W):
        acc = acc + shifted[w] * w_ref[w:w + 1, :]
    return acc, shifted


def _conv_fwd(proj, conv_w, conv_b):
    tc = 256

    def kern(x_ref, w_ref, b_ref, o_ref):
        c, _ = _conv_taps(x_ref[...], w_ref)
        c = c + b_ref[...]
        o_ref[...] = c * _sigmoid(c)

    return pl.pallas_call(
        kern, name="conv_fwd", grid=(D_CONV // tc,),
        in_specs=[pl.BlockSpec((S, tc), lambda c: (0, 4 * D_ATTN // tc + c)),
                  pl.BlockSpec((CONV_W, tc), lambda c: (0, c)), pl.BlockSpec((1, tc), lambda c: (0, c))],
        out_specs=pl.BlockSpec((S, tc), lambda c: (0, c)),
        out_shape=jax.ShapeDtypeStruct((S, D_CONV), F32), compiler_params=_params("parallel"))(proj, conv_w, conv_b)


def _softplus(x):
    return jnp.maximum(x, 0.0) + jnp.log1p(jnp.exp(-jnp.abs(x)))


def _dot3r(x, r01):
    x1 = x.astype(BF)
    r1 = x - x1.astype(F32)
    x2 = r1.astype(BF)
    x3 = (r1 - x2.astype(F32)).astype(BF)
    return _dot(x1, r01, NN) + _dot(x2, r01, NN) + _dot(x3, r01, NN)


def _spread_heads(v):
    sel = (lax.broadcasted_iota(jnp.int32, (LANES, D_SSM), 1) // HEAD
           == lax.broadcasted_iota(jnp.int32, (LANES, D_SSM), 0))
    return _dot3r(v, sel.astype(BF))


def _collect_heads(v):
    sel = (lax.broadcasted_iota(jnp.int32, (D_SSM, LANES), 0)
           == HEAD * lax.broadcasted_iota(jnp.int32, (D_SSM, LANES), 1))
    return _dot3r(v, sel.astype(BF))


def _ssd_prep(proj, dt_bias128, a_log128):
    def body(i, n, raw_ref, b_ref, al_ref, dt_ref, a_ref, dtb_ref, ab_ref, carry):
        @pl.when(i == 0)
        def _():
            carry[...] = jnp.zeros_like(carry)

        dt = _softplus(raw_ref[...] + b_ref[...])
        da = dt * (-jnp.exp(al_ref[...]))
        tri = (lax.broadcasted_iota(jnp.int32, (BLK, BLK), 0) >= lax.broadcasted_iota(jnp.int32, (BLK, BLK), 1))
        cs = _dot3(tri.astype(BF), da) + carry[0:1, :]
        dt_ref[...] = dt
        a_ref[...] = cs
        dtb_ref[...] = _spread_heads(dt)
        ab_ref[...] = _spread_heads(cs)
        carry[...] = jnp.broadcast_to(cs[BLK - 1:BLK, :], carry.shape)

    return _rowwise(body, name="ssd_prep", nrows=S, tile=BLK, row_ins=[(proj, LANES, DT_COLBLK)],
                    full_ins=[dt_bias128, a_log128],
                    row_outs=[(LANES, F32), (LANES, F32), (D_SSM, F32), (D_SSM, F32)],
                    scratch=[pltpu.VMEM((8, LANES), F32)])


def _ssd_decay(a_col, at_row, causal):
    diff = jnp.where(causal, a_col - at_row, 0.0)
    return jnp.where(causal, jnp.exp(diff), 0.0)


SSD_CB = 2


def _ssd_fwd(xbc, a_b, dt_b, at_r, dt_r):
    def kern(c_ref, b_ref, x_ref, ab_ref, dtb_ref, at_ref, dt_ref, y_ref, hall_ref):
        half0 = lax.broadcasted_iota(jnp.int32, (BLK, LANES), 1) < HEAD
        causal = lax.broadcasted_iota(jnp.int32, (BLK, BLK), 1) <= lax.broadcasted_iota(jnp.int32, (BLK, BLK), 0)

        def step(i, carry):
            h, a_prev = carry
            for cc in range(SSD_CB):
                chunk = i * SSD_CB + cc
                rows = pl.ds(pl.multiple_of(chunk * BLK, BLK), BLK)
                ci, bi, x = c_ref[rows, :].astype(BF), b_ref[rows, :].astype(BF), x_ref[rows, :]
                ab = ab_ref[rows, :]
                a_end = ab[BLK - 1:BLK, :]
                alpha = jnp.exp(ab - a_prev)
                beta = jnp.exp(a_end - ab) * dtb_ref[rows, :]
                hall_ref[rows, :] = h
                cb = _dot(ci, bi, NT)
                at, dtj = at_ref[chunk], dt_ref[chunk]
                y = alpha * _dot(ci, h.astype(BF), NN)
                for hd in range(2):
                    hm = half0 if hd == 0 else jnp.logical_not(half0)
                    w = cb * _ssd_decay(ab[:, HEAD * hd:HEAD * hd + 1], at[hd:hd + 1, :], causal) * dtj[hd:hd + 1, :]
                    y = y + _dot(w.astype(BF), jnp.where(hm, x, 0.0).astype(BF), NN)
                y_ref[rows, :] = y
                h = alpha[BLK - 1:BLK, :] * h + _dot(bi, (beta * x).astype(BF), TN)
                a_prev = a_end
            return h, a_prev

        lax.fori_loop(0, NB // SSD_CB, step, (jnp.zeros((BLK, LANES), F32), jnp.zeros((1, LANES), F32)))

    npair = D_SSM // LANES
    rowvec = pl.BlockSpec((None, NB, 2, BLK), lambda p: (p, 0, 0, 0))
    seq = pl.BlockSpec((S, LANES), lambda p: (0, p))
    return pl.pallas_call(
        kern, name="ssd_fwd", grid=(npair,),
        in_specs=[pl.BlockSpec((S, LANES), lambda p: (0, 12 + p // 2)),
                  pl.BlockSpec((S, LANES), lambda p: (0, 8 + p // 2)), seq, seq, seq, rowvec, rowvec],
        out_specs=[seq, seq], out_shape=[jax.ShapeDtypeStruct((S, D_SSM), F32)] * 2,
        compiler_params=_params("parallel"))(xbc, xbc, xbc, a_b, dt_b, at_r, dt_r)


def _ssd_post(y_ssd, xbc, proj, dskip_b, g_ssm):
    def body(i, n, y_ref, xs_ref, z_ref, d_ref, g_ref, o_ref):
        z = z_ref[...]
        y2 = (y_ref[...] + d_ref[...] * xs_ref[...]) * (z * _sigmoid(z))
        nx, _ = _seg_rms_n(y2, 256)
        o_ref[...] = (nx * g_ref[...]).astype(BF)

    return _rowwise(body, name="ssd_post", nrows=S, tile=ROW_TILE,
                    row_ins=[(y_ssd, D_SSM, 0), (xbc, D_SSM, 0), (proj, D_SSM, 3)], full_ins=[dskip_b, g_ssm],
                    row_outs=[(D_SSM, BF)])[0]


def _cross_heads(q, kv_ref, gq, gk):
    out = []
    for h in range(D_CROSS // CROSS_HEAD):
        sl = slice(CROSS_HEAD * h, CROSS_HEAD * (h + 1))
        nq, rq = _rms_n(q[:, sl])
        nk, rk = _rms_n(kv_ref[:, sl])
        v = kv_ref[:, D_CROSS + CROSS_HEAD * h:D_CROSS + CROSS_HEAD * (h + 1)]
        out.append((sl, nq, rq, nk, rk, v))
    return out


def _cross_fwd(qc, kv, g_cq, g_ck):
    scale = CROSS_HEAD ** -0.5

    def body(i, n, q_ref, kv_ref, gq_ref, gk_ref, o_ref):
        for sl, nq, _, nk, _, v in _cross_heads(q_ref[...], kv_ref, gq_ref[...], gk_ref[...]):
            qn = (nq * gq_ref[...] * scale).astype(BF)
            kn = (nk * gk_ref[...]).astype(BF)
            s = _dot(qn, kn, NT)
            e = jnp.exp(s - jnp.max(s, axis=-1, keepdims=True))
            p = e / jnp.sum(e, axis=-1, keepdims=True)
            o_ref[:, sl] = _dot(p.astype(BF), v.astype(BF), NN).astype(BF)

    return _rowwise(body, name="cross_fwd", nrows=S, tile=ROW_TILE, row_ins=[(qc, D_CROSS, 0)],
                    full_ins=[kv, g_cq, g_ck], row_outs=[(D_CROSS, BF)])[0]


def _loss_sum(dy):
    def body(i, n, d_ref, o_ref):
        d = d_ref[...]
        s = jnp.sum(jnp.sum(d * d, axis=0, keepdims=True), axis=1, keepdims=True)
        _accum(o_ref, s, i)

    return _rowwise(body, name="loss_sum", nrows=S, tile=ROW_TILE, row_ins=[(dy, D, 0)],
                    acc_outs=[((1, 1), F32)])[0]


def _rms_bwd_call(x, g, dh, dres, name):
    rows = x.shape[0]
    has_res = dres is not None

    def body(i, n, *refs):
        if has_res:
            x_ref, dh_ref, dr_ref, g_ref, dx_ref, dxb_ref, dg_ref = refs
        else:
            x_ref, dh_ref, g_ref, dg_ref = refs
        dx, dg = _rms_bwd(x_ref[...], g_ref[...], dh_ref[...])
        if has_res:
            dx = dx + dr_ref[...]
            dx_ref[...] = dx
            dxb_ref[...] = dx.astype(BF)
        _accum(dg_ref, dg, i)

    row_ins = [(x, D, 0), (dh, D, 0)] + ([(dres, D, 0)] if has_res else [])
    return _rowwise(body, name=name, nrows=rows, tile=min(ROW_TILE, rows), row_ins=row_ins, full_ins=[g],
                    row_outs=[(D, F32), (D, BF)] if has_res else [], acc_outs=[((1, D), F32)])


def _cross_bwd(qc, doc, kv, g_cq, g_ck):
    scale = CROSS_HEAD ** -0.5

    def body(i, n, q_ref, do_ref, kv_ref, gq_ref, gk_ref, dq_ref, dkn_ref, dv_ref, dgq_ref):
        dgq = jnp.zeros((1, CROSS_HEAD), F32)
        dkn_parts, dv_parts = [], []
        for sl, nq, rq, nk, _, v in _cross_heads(q_ref[...], kv_ref, gq_ref[...], gk_ref[...]):
            qn = (nq * gq_ref[...] * scale).astype(BF)
            kn = (nk * gk_ref[...]).astype(BF)
            s = _dot(qn, kn, NT)
            e = jnp.exp(s - jnp.max(s, axis=-1, keepdims=True))
            p = e / jnp.sum(e, axis=-1, keepdims=True)
            do = do_ref[:, sl].astype(BF)
            dv_parts.append(_dot(p.astype(BF), do, TN))
            dp = _dot(do, v.astype(BF), NT)
            ds = (p * (dp - jnp.sum(dp * p, axis=-1, keepdims=True))).astype(BF)
            dqn = _dot(ds, kn, NN)
            dkn_parts.append(_dot(ds, qn, TN))
            dn = dqn * (gq_ref[...] * scale)
            dq_ref[:, sl] = (rq * (dn - nq * jnp.mean(dn * nq, axis=-1, keepdims=True))).astype(BF)
            dgq = dgq + jnp.sum(dqn * scale * nq, axis=0, keepdims=True)
        _accum(dkn_ref, jnp.concatenate(dkn_parts, axis=1), i)
        _accum(dv_ref, jnp.concatenate(dv_parts, axis=1), i)
        _accum(dgq_ref, dgq, i)

    return _rowwise(body, name="cross_bwd", nrows=S, tile=ROW_TILE, row_ins=[(qc, D_CROSS, 0), (doc, D_CROSS, 0)],
                    full_ins=[kv, g_cq, g_ck], row_outs=[(D_CROSS, BF)],
                    acc_outs=[((N_MEM, D_CROSS), F32), ((N_MEM, D_CROSS), F32), ((1, CROSS_HEAD), F32)])


def _cross_kv_bwd(kv, dkn, dv, g_ck):
    def body(i, n, kv_ref, dkn_ref, dv_ref, gk_ref, dkv_ref, dgk_ref):
        dgk = jnp.zeros((1, CROSS_HEAD), F32)
        for h in range(D_CROSS // CROSS_HEAD):
            sl = slice(CROSS_HEAD * h, CROSS_HEAD * (h + 1))
            dk, dg = _rms_bwd(kv_ref[:, sl], gk_ref[...], dkn_ref[:, sl])
            dkv_ref[:, sl] = dk.astype(BF)
            dgk = dgk + dg
        dkv_ref[:, D_CROSS:] = dv_ref[...].astype(BF)
        dgk_ref[...] = dgk

    return _rowwise(body, name="cross_kv_bwd", nrows=N_MEM, tile=N_MEM,
                    row_ins=[(kv, 2 * D_CROSS, 0), (dkn, D_CROSS, 0), (dv, D_CROSS, 0)], full_ins=[g_ck],
                    row_outs=[(2 * D_CROSS, BF)], acc_outs=[((1, CROSS_HEAD), F32)])


def _attn_merge_bwd(dmix, attn, g_attn):
    def body(i, n, dn_ref, a_ref, g_ref, da_ref, dl_ref, dg_ref):
        attn_v = a_ref[...]
        da, dg = _rms_bwd(attn_v, g_ref[...], dn_ref[...])
        da_ref[...] = da
        dl_ref[...] = _seg_sum(da * attn_v, HEAD)
        _accum(dg_ref, dg, i)

    return _rowwise(body, name="attn_merge_bwd", nrows=S, tile=ROW_TILE,
                    row_ins=[(dmix, D_ATTN, 0), (attn, D_ATTN, 0)], full_ins=[g_attn],
                    row_outs=[(D_ATTN, F32), (D_ATTN, F32)], acc_outs=[((1, D_ATTN), F32)])


def _attn_bwd(q, k, v, do, lse, delta, d, name):
    def kern(q_ref, k_ref, v_ref, do_ref, l_ref, d_ref, dq_ref, dk_ref, dv_ref):
        dk_ref[...] = jnp.zeros_like(dk_ref)
        dv_ref[...] = jnp.zeros_like(dv_ref)
        half0 = lax.broadcasted_iota(jnp.int32, (BLK, LANES), 1) < HEAD

        def group(g, carry):
            updates = []
            for bb in range(ATTN_QB):
                q_rows, k_rows, valid = _band(g * ATTN_QB + bb, d)
                q, do = q_ref[q_rows, :], do_ref[q_rows, :]
                lse_t, del_t = l_ref[q_rows, :], d_ref[q_rows, :]
                kb, vb = k_ref[k_rows, :].astype(BF), v_ref[k_rows, :].astype(BF)
                q2 = jnp.concatenate([jnp.where(half0, q, 0.0), jnp.where(half0, 0.0, q)], axis=0).astype(BF)
                do2 = jnp.concatenate([jnp.where(half0, do, 0.0), jnp.where(half0, 0.0, do)], axis=0).astype(BF)
                lse2 = jnp.concatenate([lse_t[:, 0:1], lse_t[:, HEAD:HEAD + 1]], axis=0)
                del2 = jnp.concatenate([del_t[:, 0:1], del_t[:, HEAD:HEAD + 1]], axis=0)
                s = jnp.where(valid, _dot(q2, kb, NT), NEG)
                p = jnp.exp(s - lse2)
                ds = (p * (_dot(do2, vb, NT) - del2)).astype(BF)
                dq2 = _dot(ds, kb, NN)
                dq_ref[q_rows, :] = jnp.where(half0, dq2[:BLK], dq2[BLK:])
                updates.append((k_rows, _dot(ds, q2, TN), _dot(p.astype(BF), do2, TN)))
            for k_rows, dk, dv in updates:
                dk_ref[k_rows, :] += dk
                dv_ref[k_rows, :] += dv
            return carry

        lax.fori_loop(0, NB // ATTN_QB, group, 0)

    seq = pl.BlockSpec((S, LANES), lambda p: (0, p))
    return pl.pallas_call(
        kern, name=name, grid=(D_ATTN // LANES,), in_specs=[seq, seq, V_COLS, seq, seq, seq],
        out_specs=[seq, seq, seq], out_shape=[jax.ShapeDtypeStruct((S, D_ATTN), F32)] * 3,
        compiler_params=_params("parallel"))(q, k, v, do, lse, delta)


def _qk_bwd(dqs, dks, dvs, proj, pos_col, gq128, gk128, inv_tab):
    scale = HEAD ** -0.5

    def body(i, n, *refs):
        dq_refs, dk_refs, dv_refs = refs[0:3], refs[3:6], refs[6:9]
        q_ref, k_ref, p_ref, gq_ref, gk_ref, it_ref = refs[9:15]
        dqo_ref, dko_ref, dvo_ref, dgq_ref, dgk_ref = refs[15:20]
        t = q_ref.shape[0]
        c, sg, lo = _rope_tables(p_ref[...], it_ref[...], t)
        dgq = jnp.zeros((1, LANES), F32)
        dgk = jnp.zeros((1, LANES), F32)
        for j in range(D_ATTN // LANES):
            sl = slice(LANES * j, LANES * (j + 1))
            dqr = _rope(dq_refs[0][:, sl] + dq_refs[1][:, sl] + dq_refs[2][:, sl], c, -sg, lo) * scale
            dkr = _rope(dk_refs[0][:, sl] + dk_refs[1][:, sl] + dk_refs[2][:, sl], c, -sg, lo)
            dq, gq = _seg_rms_bwd(q_ref[:, sl], gq_ref[...], dqr, HEAD)
            dk, gk = _seg_rms_bwd(k_ref[:, sl], gk_ref[...], dkr, HEAD)
            dqo_ref[:, sl] = dq.astype(BF)
            dko_ref[:, sl] = dk.astype(BF)
            dgq, dgk = dgq + gq, dgk + gk
        dvo_ref[...] = (dv_refs[0][...] + dv_refs[1][...] + dv_refs[2][...]).astype(BF)
        _accum(dgq_ref, _fold_heads(dgq), i)
        _accum(dgk_ref, _fold_heads(dgk), i)

    return _rowwise(body, name="qk_bwd", nrows=S, tile=ROW_TILE,
                    row_ins=[(a, D_ATTN, 0) for a in (*dqs, *dks, *dvs)]
                    + [(proj, D_ATTN, 0), (proj, D_ATTN, 1), (pos_col, 1, 0)],
                    full_ins=[gq128, gk128, inv_tab], row_outs=[(D_ATTN, BF)] * 3,
                    acc_outs=[((1, LANES), F32)] * 2)


def _ssd_post_bwd(y_ssd, xbc, proj, dmix, dskip_b, g_ssm):
    def body(i, n, y_ref, xs_ref, z_ref, do_ref, d_ref, g_ref, dy_ref, dz_ref, dxs_ref, dd_ref, dg_ref):
        z, xs = z_ref[...], xs_ref[...]
        sg = _sigmoid(z)
        gate = z * sg
        y = y_ref[...] + d_ref[...] * xs
        dy2, dg = _seg_rms_bwd(y * gate, g_ref[...], do_ref[...], 256)
        dy = dy2 * gate
        dy_ref[...] = dy.astype(BF)
        dz_ref[...] = (dy2 * y * (sg * (1.0 + z * (1.0 - sg)))).astype(BF)
        dxs_ref[...] = dy * d_ref[...]
        _accum(dd_ref, jnp.sum(dy * xs, axis=0, keepdims=True), i)
        _accum(dg_ref, dg, i)

    return _rowwise(body, name="ssd_post_bwd", nrows=S, tile=ROW_TILE,
                    row_ins=[(y_ssd, D_SSM, 0), (xbc, D_SSM, 0), (proj, D_SSM, 3), (dmix, D_SSM, 1)],
                    full_ins=[dskip_b, g_ssm], row_outs=[(D_SSM, BF), (D_SSM, BF), (D_SSM, F32)],
                    acc_outs=[((1, D_SSM), F32), ((1, D_SSM), F32)])


def _ssd_bwd(xbc, dy, h_all, a_b, dt_b, at_r, dt_r):
    def kern(c_ref, b_ref, x_ref, dy_ref, hall_ref, ab_ref, dtb_ref, at_ref, dt_ref,
             dc_ref, daq_ref, dx_ref, db_ref, dak_ref, ddt_ref, ddtb_ref):
        half0 = lax.broadcasted_iota(jnp.int32, (BLK, LANES), 1) < HEAD
        hms = (half0, jnp.logical_not(half0))
        causal = lax.broadcasted_iota(jnp.int32, (BLK, BLK), 1) <= lax.broadcasted_iota(jnp.int32, (BLK, BLK), 0)
        row = lax.broadcasted_iota(jnp.int32, (BLK, LANES), 0)

        def step(t, carry_in):
            dh_next, carry = carry_in
            for cc in reversed(range(SSD_CB)):
                chunk = (NB // SSD_CB - 1 - t) * SSD_CB + cc
                rows = pl.ds(pl.multiple_of(chunk * BLK, BLK), BLK)
                ci, bi, x = c_ref[rows, :].astype(BF), b_ref[rows, :].astype(BF), x_ref[rows, :]
                dyv = dy_ref[rows, :].astype(F32)
                ab, dtb = ab_ref[rows, :], dtb_ref[rows, :]
                before = ab_ref[pl.ds(pl.multiple_of(jnp.maximum(chunk * BLK - 8, 0), 8), 8), :][7:8, :]
                a_prev = jnp.where(chunk == 0, 0.0, before)
                a_end = ab[BLK - 1:BLK, :]
                alpha = jnp.exp(ab - a_prev)
                e_end = jnp.exp(a_end - ab)
                beta = e_end * dtb
                t_all = alpha[BLK - 1:BLK, :]
                hc = hall_ref[rows, :]
                hcb, dhb = hc.astype(BF), dh_next.astype(BF)

                cb = _dot(ci, bi, NT)
                at, dtj = at_ref[chunk], dt_ref[chunk]
                dcb = jnp.zeros((BLK, BLK), F32)
                dx = jnp.zeros((BLK, LANES), F32)
                rsum = []
                for hd in range(2):
                    dym = jnp.where(hms[hd], dyv, 0.0).astype(BF)
                    lm = _ssd_decay(ab[:, HEAD * hd:HEAD * hd + 1], at[hd:hd + 1, :], causal)
                    dth = dtj[hd:hd + 1, :]
                    dw = _dot(dym, jnp.where(hms[hd], x, 0.0).astype(BF), NT)
                    g = dw * cb * lm
                    dx = dx + _dot((cb * lm * dth).astype(BF), dym, TN)
                    gcol = jnp.sum(g, axis=0, keepdims=True)
                    ddt_ref[chunk, hd:hd + 1, :] = gcol
                    dak_ref[chunk, hd:hd + 1, :] = gcol * dth
                    rsum.append(jnp.sum(g * dth, axis=1, keepdims=True))
                    dcb = dcb + dw * lm * dth
                dcb = dcb.astype(BF)

                g1 = (alpha * dyv).astype(BF)
                dalpha = dyv * _dot(ci, hcb, NN)
                g2 = _dot(bi, dhb, NN)
                dbeta = x * g2
                bx = (beta * x).astype(BF)
                dc_ref[rows, :] = _dot(dcb, bi, NN) + _dot(g1, hcb, NT)
                db_ref[rows, :] = _dot(dcb, ci, TN) + _dot(bx, dhb, NT)
                dx_ref[rows, :] = dx + beta * g2
                ddtb_ref[rows, :] = _seg_sum(e_end * dbeta, HEAD)
                ada, bdb = alpha * dalpha, beta * dbeta
                t_dt = t_all * jnp.sum(dh_next * hc, axis=0, keepdims=True)
                end_term = _seg_sum(jnp.broadcast_to(jnp.sum(bdb, axis=0, keepdims=True) + t_dt, (8, LANES)), HEAD)[0:1]
                prev_term = _seg_sum(jnp.broadcast_to(jnp.sum(ada, axis=0, keepdims=True) + t_dt, (8, LANES)), HEAD)[0:1]
                daq = jnp.where(half0, rsum[0], rsum[1]) + _seg_sum(ada - bdb, HEAD)
                daq_ref[rows, :] = daq + jnp.where(row == BLK - 1, end_term - carry, 0.0)
                carry = prev_term
                dh_next = t_all * dh_next + _dot(ci, g1, TN)
            return dh_next, carry

        lax.fori_loop(0, NB // SSD_CB, step, (jnp.zeros((BLK, LANES), F32), jnp.zeros((1, LANES), F32)))

    npair = D_SSM // LANES
    rowvec = pl.BlockSpec((None, NB, 2, BLK), lambda p: (p, 0, 0, 0))
    seq = pl.BlockSpec((S, LANES), lambda p: (0, p))
    return pl.pallas_call(
        kern, name="ssd_bwd", grid=(npair,),
        in_specs=[pl.BlockSpec((S, LANES), lambda p: (0, 12 + p // 2)),
                  pl.BlockSpec((S, LANES), lambda p: (0, 8 + p // 2)),
                  seq, seq, seq, seq, seq, rowvec, rowvec],
        out_specs=[seq, seq, seq, seq, rowvec, rowvec, seq],
        out_shape=[jax.ShapeDtypeStruct((S, D_SSM), F32)] * 4
        + [jax.ShapeDtypeStruct((npair, NB, 2, BLK), F32)] * 2 + [jax.ShapeDtypeStruct((S, D_SSM), F32)],
        compiler_params=_params("parallel"))(xbc, xbc, xbc, dy, h_all, a_b, dt_b, at_r, dt_r)


def _ssd_prep_bwd(daq, dak, ddt_row, ddt_lane, dt, proj, dt_bias128, a_log128):
    def body(i, n, daq_ref, dak_ref, dd_ref, dd2_ref, dt_ref, raw_ref, b_ref, al_ref, draw_ref, dal_ref, db_ref,
             carry):
        @pl.when(i == 0)
        def _():
            carry[...] = jnp.zeros_like(carry)

        a = -jnp.exp(al_ref[...])
        tri = (lax.broadcasted_iota(jnp.int32, (BLK, BLK), 0) <= lax.broadcasted_iota(jnp.int32, (BLK, BLK), 1))
        rev = _dot3(tri.astype(BF), _collect_heads(daq_ref[...]) - dak_ref[...]) + carry[0:1, :]
        carry[...] = jnp.broadcast_to(rev[0:1, :], carry.shape)
        dtv = dt_ref[...]
        draw = (dd_ref[...] + _collect_heads(dd2_ref[...]) + a * rev) * _sigmoid(raw_ref[...] + b_ref[...])
        draw_ref[...] = draw.astype(BF)
        _accum(dal_ref, jnp.sum(dtv * rev, axis=0, keepdims=True) * a, i)
        _accum(db_ref, jnp.sum(draw, axis=0, keepdims=True), i)

    return _rowwise(body, name="ssd_prep_bwd", nrows=S, tile=BLK, reverse=True,
                    row_ins=[(daq, D_SSM, 0), (dak, LANES, 0), (ddt_row, LANES, 0), (ddt_lane, D_SSM, 0), (dt, LANES, 0),
                             (proj, LANES, DT_COLBLK)],
                    full_ins=[dt_bias128, a_log128], row_outs=[(LANES, BF)],
                    acc_outs=[((1, LANES), F32), ((1, LANES), F32)], scratch=[pltpu.VMEM((8, LANES), F32)])


def _conv_bwd(proj, conv_w, conv_b, d1, d2, map1, map2, col0, ntiles, name):
    base = (4 * D_ATTN + col0) // LANES

    def kern(x_ref, w_ref, b_ref, d1_ref, d2_ref, dx_ref, dw_ref, db_ref):
        x = x_ref[...]
        c, shifted = _conv_taps(x, w_ref)
        c = c + b_ref[...]
        sg = _sigmoid(c)
        dc = (d1_ref[...] + d2_ref[...]) * (sg * (1.0 + c * (1.0 - sg)))
        rows = lax.broadcasted_iota(jnp.int32, x.shape, 0)
        dx = jnp.zeros_like(x)
        for w in range(CONV_W):
            k = CONV_W - 1 - w
            up = dc if k == 0 else jnp.where(rows < S - k, pltpu.roll(dc, S - k, 0), 0.0)
            dx = dx + up * w_ref[w:w + 1, :]
            dw_ref[w:w + 1, :] = jnp.sum(dc * shifted[w], axis=0, keepdims=True)
        dx_ref[...] = dx.astype(BF)
        db_ref[...] = jnp.sum(dc, axis=0, keepdims=True)

    c0 = col0 // LANES
    return pl.pallas_call(
        kern, name=name, grid=(ntiles,),
        in_specs=[pl.BlockSpec((S, LANES), lambda c: (0, base + c)),
                  pl.BlockSpec((CONV_W, LANES), lambda c: (0, c0 + c)),
                  pl.BlockSpec((1, LANES), lambda c: (0, c0 + c)),
                  pl.BlockSpec((S, LANES), lambda c: (0, map1(c))),
                  pl.BlockSpec((S, LANES), lambda c: (0, map2(c)))],
        out_specs=[pl.BlockSpec((S, LANES), lambda c: (0, c)), pl.BlockSpec((CONV_W, LANES), lambda c: (0, c)),
                   pl.BlockSpec((1, LANES), lambda c: (0, c))],
        out_shape=[jax.ShapeDtypeStruct((S, ntiles * LANES), BF), jax.ShapeDtypeStruct((CONV_W, ntiles * LANES), F32),
                   jax.ShapeDtypeStruct((1, ntiles * LANES), F32)],
        compiler_params=_params("parallel"))(proj, conv_w, conv_b, d1, d2)


def _adamw(w, m, v, parts, name):
    r, c = w.shape
    n_parts = parts.shape[0]
    tile = r if r <= 512 else (128 if c > 1024 else 256)
    assert r % tile == 0
    c1 = 1.0 - ADAM_B1 ** ADAM_STEP
    c2 = 1.0 - ADAM_B2 ** ADAM_STEP

    def kern(w_ref, m_ref, v_ref, p_ref, g_ref, d_ref, mo_ref, vo_ref):
        g = p_ref[0].astype(F32)
        for k in range(1, n_parts):
            g = g + p_ref[k].astype(F32)
        mn = ADAM_B1 * m_ref[...] + (1.0 - ADAM_B1) * g
        vn = ADAM_B2 * v_ref[...] + (1.0 - ADAM_B2) * (g * g)
        g_ref[...] = g
        mo_ref[...] = mn
        vo_ref[...] = vn
        d_ref[...] = -ADAM_LR * ((mn / c1) / (jnp.sqrt(vn / c2) + ADAM_EPS) + ADAM_WD * w_ref[...])

    blk = pl.BlockSpec((tile, c), lambda i: (i, 0))
    return pl.pallas_call(
        kern, name=name, grid=(r // tile,),
        in_specs=[blk, blk, blk, pl.BlockSpec((n_parts, tile, c), lambda i: (0, i, 0))],
        out_specs=[blk] * 4, out_shape=[jax.ShapeDtypeStruct((r, c), F32)] * 4,
        compiler_params=_params("parallel"))(w, m, v, parts)


MESH = pl.DeviceIdType.MESH
ANY = pl.BlockSpec(memory_space=pl.ANY)


def _put_own(gathered, shard):
    me = 4 * lax.axis_index("x") + 2 * lax.axis_index("y") + lax.axis_index("c")
    return lax.dynamic_update_slice(gathered, shard[None], (me,) + (0,) * shard.ndim)


def _all_gather(arrs, name, after=None):
    na = len(arrs)
    n_after = 0 if after is None else 1

    def kern(*refs):
        ins, outs = refs[:na], refs[na + n_after:2 * na + n_after]
        send_sems, recv_sems = refs[2 * na + n_after:]
        x, y, c = lax.axis_index("x"), lax.axis_index("y"), lax.axis_index("c")
        me, sibling = (x, y, c), (x, y, 1 - c)
        a_chip = (jnp.where(c == 0, 1 - x, x), jnp.where(c == 0, y, 1 - y))
        b_chip = (jnp.where(c == 0, x, 1 - x), jnp.where(c == 0, 1 - y, y))
        chips = [a_chip, b_chip, (1 - x, 1 - y)]

        def copy(a, k, block, to, src=None):
            px, py, pc = block
            dst = outs[a].at[4 * px + 2 * py + pc]
            return pltpu.make_async_remote_copy(
                src_ref=dst if src is None else src, dst_ref=dst, send_sem=send_sems.at[a, k],
                recv_sem=recv_sems.at[a, k], device_id=to, device_id_type=MESH)

        sends = []
        for a in range(na):
            sends.append(copy(a, 0, me, sibling, src=ins[a]))
            sends += [copy(a, 1 + j, me, (*chips[j], c), src=ins[a]) for j in range(2)]
        for cp in sends:
            cp.start()
        for a in range(na):
            for j, chip in enumerate(chips):
                copy(a, 1 + j, (*chip, c), me).wait_recv()
                later = [copy(a, 4 + j, (*chip, c), sibling)]
                if j == 0:
                    later.append(copy(a, 3, (*a_chip, c), (*b_chip, c)))
                for cp in later:
                    cp.start()
                sends += later
        for a in range(na):
            copy(a, 0, sibling, me).wait_recv()
            for j, chip in enumerate(chips):
                copy(a, 4 + j, (*chip, 1 - c), me).wait_recv()
        for cp in sends:
            cp.wait_send()

    outs = pl.pallas_call(
        kern, name=name, in_specs=[ANY] * (na + n_after), out_specs=[ANY] * na,
        out_shape=[jax.ShapeDtypeStruct((N_DEV,) + a.shape, a.dtype) for a in arrs],
        scratch_shapes=[pltpu.SemaphoreType.DMA((na, 7)), pltpu.SemaphoreType.DMA((na, 7))],
    )(*arrs, *([] if after is None else [after]))
    return [_put_own(o, a) for o, a in zip(outs, arrs)]


HBM = pl.BlockSpec(memory_space=pltpu.HBM)
SEM = pl.BlockSpec(memory_space=pltpu.SEMAPHORE)
EFFECT = pltpu.SideEffectType.DATAFLOW_SIDE_EFFECTING
N_CHIP = 4
N_COPIES = {"gather": 3, "scatter": 3, "forward": 4, "pair": 4}


def _in_hbm(a):
    return pltpu.with_memory_space_constraint(a, pltpu.HBM)


def _chip_copies(kind, src_refs, land_refs, send_sems, recv_sems):
    x, y, c = lax.axis_index("x"), lax.axis_index("y"), lax.axis_index("c")
    chips = [(1 - x, y), (x, 1 - y), (1 - x, 1 - y)]
    n = N_COPIES[kind]
    cps = []

    def add(a, j, src, dst, to):
        cps.append(pltpu.make_async_remote_copy(
            src_ref=src, dst_ref=dst, send_sem=send_sems.at[n * a + j], recv_sem=recv_sems.at[n * a + j],
            device_id=to, device_id_type=MESH))

    for a in range(len(src_refs)):
        if kind == "gather":
            for j, (px, py) in enumerate(chips):
                add(a, j, src_refs[a], land_refs[a].at[4 * x + 2 * y + c], (px, py, c))
        elif kind == "scatter":
            for j, (px, py) in enumerate(chips):
                add(a, j, src_refs[a].at[2 * px + py], land_refs[a].at[2 * x + y], (px, py, c))
        elif kind == "forward":
            add(a, 0, src_refs[a], land_refs[a].at[4 * x + 2 * y + c], (x, y, 1 - c))
            for j, (px, py) in enumerate(chips):
                slot = land_refs[a].at[4 * px + 2 * py + c]
                add(a, 1 + j, slot, slot, (x, y, 1 - c))
        else:
            for q in range(N_CHIP):
                add(a, q, src_refs[a].at[2 * q + 1 - c], land_refs[a].at[q], (x, y, 1 - c))
    return cps


def _exchange_start(kind, srcs, lands, after, name):
    na = len(srcs)
    n_after = 0 if after is None else 1

    def kern(*refs):
        src_refs, land_refs = refs[:na], refs[na:2 * na]
        send_sems, recv_sems = refs[2 * na + n_after], refs[2 * na + n_after + 1]
        token = refs[-1]
        for cp in _chip_copies(kind, src_refs, land_refs, send_sems, recv_sems):
            cp.start()
        token[...] = jnp.zeros_like(token)

    bufs = list(srcs) + list(lands)
    res = pl.pallas_call(
        kern, name=name,
        out_shape=(pltpu.SemaphoreType.DMA((N_COPIES[kind] * na,)), pltpu.SemaphoreType.DMA((N_COPIES[kind] * na,)))
        + tuple(pltpu.HBM(b.shape, b.dtype) for b in bufs) + (jax.ShapeDtypeStruct((8, LANES), F32),),
        in_specs=[HBM] * (2 * na) + [ANY] * n_after,
        out_specs=(SEM, SEM) + (HBM,) * (2 * na) + (pl.BlockSpec(memory_space=pltpu.VMEM),),
        input_output_aliases={i: 2 + i for i in range(2 * na)},
        compiler_params=pltpu.CompilerParams(has_side_effects=EFFECT),
    )(*[_in_hbm(b) for b in bufs], *([] if after is None else [after]))
    return (res[0], res[1]), list(res[2:2 + na]), list(res[2 + na:2 + 2 * na]), res[-1]


def _exchange_wait(kind, sems, srcs, lands, after, name):
    na = len(srcs)

    def kern(*refs):
        src_refs, land_refs = refs[:na], refs[na:2 * na]
        send_sems, recv_sems = refs[2 * na], refs[2 * na + 1]
        for cp in _chip_copies(kind, src_refs, land_refs, send_sems, recv_sems):
            cp.wait_send()
            cp.wait_recv()

    bufs = list(srcs) + list(lands)
    res = pl.pallas_call(
        kern, name=name, out_shape=tuple(pltpu.HBM(b.shape, b.dtype) for b in bufs),
        in_specs=[HBM] * (2 * na) + [SEM, SEM, ANY], out_specs=(HBM,) * (2 * na),
        input_output_aliases={i: i for i in range(2 * na)},
        compiler_params=pltpu.CompilerParams(has_side_effects=EFFECT),
    )(*bufs, sems[0], sems[1], after)
    return list(res[:na]), list(res[na:])


def _gather_finish(shards, lands, name):
    na = len(shards)

    def kern(*refs):
        ins, outs = refs[:na], refs[2 * na:3 * na]
        send_sems, recv_sems = refs[3 * na:]
        x, y, c = lax.axis_index("x"), lax.axis_index("y"), lax.axis_index("c")
        chips = [(1 - x, y), (x, 1 - y), (1 - x, 1 - y)]

        def copy(a, k, slot, pc, src=None):
            dst = outs[a].at[slot + pc]
            return pltpu.make_async_remote_copy(
                src_ref=dst if src is None else src, dst_ref=dst, send_sem=send_sems.at[a, k],
                recv_sem=recv_sems.at[a, k], device_id=(x, y, 1 - c), device_id_type=MESH)

        sends = []
        for a in range(na):
            sends.append(copy(a, 0, 4 * x + 2 * y, c, src=ins[a]))
            sends += [copy(a, 1 + j, 4 * px + 2 * py, c) for j, (px, py) in enumerate(chips)]
        for cp in sends:
            cp.start()
        for a in range(na):
            copy(a, 0, 4 * x + 2 * y, 1 - c).wait_recv()
            for j, (px, py) in enumerate(chips):
                copy(a, 1 + j, 4 * px + 2 * py, 1 - c).wait_recv()
        for cp in sends:
            cp.wait_send()

    return pl.pallas_call(
        kern, name=name, in_specs=[ANY] * (2 * na), out_specs=[ANY] * na,
        out_shape=[jax.ShapeDtypeStruct(l.shape, l.dtype) for l in lands],
        input_output_aliases={na + a: a for a in range(na)},
        scratch_shapes=[pltpu.SemaphoreType.DMA((na, 4)), pltpu.SemaphoreType.DMA((na, 4))])(*shards, *lands)


def _pair_exchange(parts, name):
    na = len(parts)

    def kern(*refs):
        ins, got = refs[:na], refs[na:2 * na]
        send_sems, recv_sems = refs[2 * na:]
        x, y, c = lax.axis_index("x"), lax.axis_index("y"), lax.axis_index("c")
        sends = []
        for a in range(na):
            for q in range(N_CHIP):
                sends.append(pltpu.make_async_remote_copy(
                    src_ref=ins[a].at[2 * q + 1 - c], dst_ref=got[a].at[q], send_sem=send_sems.at[a, q],
                    recv_sem=recv_sems.at[a, q], device_id=(x, y, 1 - c), device_id_type=MESH))
        for cp in sends:
            cp.start()
        for cp in sends:
            cp.wait()

    return pl.pallas_call(
        kern, name=name, in_specs=[ANY] * na, out_specs=[ANY] * na,
        out_shape=[jax.ShapeDtypeStruct((N_CHIP,) + p.shape[1:], p.dtype) for p in parts],
        scratch_shapes=[pltpu.SemaphoreType.DMA((na, N_CHIP)), pltpu.SemaphoreType.DMA((na, N_CHIP))])(*parts)


def _pair_sum(parts, got, name):
    _, r, cdim = parts.shape
    tile = min(r, ROW_TILE)
    x, y, c = lax.axis_index("x"), lax.axis_index("y"), lax.axis_index("c")
    where = jnp.stack([c, 2 * x + y]).astype(jnp.int32)

    def kern(w_ref, a_ref, b_ref, q_ref, own_ref):
        s = (a_ref[...].astype(F32) + b_ref[...].astype(F32)).astype(BF)
        q_ref[...] = s

        @pl.when(pl.program_id(1) == w_ref[1])
        def _():
            own_ref[...] = s

    blk = pl.BlockSpec((None, tile, cdim), lambda i, q, w_ref: (q, i, 0))
    sums, own = pl.pallas_call(
        kern, name=name,
        out_shape=[jax.ShapeDtypeStruct((N_CHIP, r, cdim), BF), jax.ShapeDtypeStruct((r, cdim), BF)],
        grid_spec=pltpu.PrefetchScalarGridSpec(
            num_scalar_prefetch=1, grid=(r // tile, N_CHIP),
            in_specs=[pl.BlockSpec((None, tile, cdim), lambda i, q, w_ref: (2 * q + w_ref[0], i, 0)), blk],
            out_specs=[blk, pl.BlockSpec((tile, cdim), lambda i, q, w_ref: (i, 0))]),
        compiler_params=_params("parallel", "arbitrary"))(where, parts, got)
    land = lax.dynamic_update_slice(lax.empty((N_CHIP, r, cdim), BF), own[None], (2 * x + y, 0, 0))
    return sums, land


W_IN_COLS = D_IN // N_DEV


def _w_in_from_blocks(w8):
    def kern(x_ref, o_ref):
        for j in range(N_DEV):
            o_ref[:, W_IN_COLS * j:W_IN_COLS * (j + 1)] = x_ref[j]
        o_ref[:, D_IN:] = jnp.zeros((ROW_TILE, D_INP - D_IN), o_ref.dtype)

    return pl.pallas_call(
        kern, name="w_in_from_blocks", grid=(D // ROW_TILE,),
        in_specs=[pl.BlockSpec((N_DEV, ROW_TILE, W_IN_COLS), lambda i: (0, i, 0))],
        out_specs=pl.BlockSpec((ROW_TILE, D_INP), lambda i: (i, 0)),
        out_shape=jax.ShapeDtypeStruct((D, D_INP), w8.dtype), compiler_params=_params("parallel"))(w8)


def _w_in_to_blocks(g):
    def kern(x_ref, o_ref):
        for j in range(N_DEV):
            o_ref[j] = x_ref[:, W_IN_COLS * j:W_IN_COLS * (j + 1)]

    return pl.pallas_call(
        kern, name="w_in_to_blocks", grid=(D // ROW_TILE,),
        in_specs=[pl.BlockSpec((ROW_TILE, D_INP), lambda i: (i, 0))],
        out_specs=pl.BlockSpec((N_DEV, ROW_TILE, W_IN_COLS), lambda i: (0, i, 0)),
        out_shape=jax.ShapeDtypeStruct((N_DEV, D, W_IN_COLS), g.dtype), compiler_params=_params("parallel"))(g)


def _pad_lanes(v, width=LANES):
    return jnp.pad(v, ((0, 0), (0, width - v.shape[1])))


def _row_layout(t16):
    return t16.T.reshape(N_HEADS // 2, 2, NB, BLK).transpose(0, 2, 1, 3)


def _row_layout_inv(r):
    return r.transpose(0, 2, 1, 3).reshape(N_HEADS, S).T


SMALL = (("g_mix", D), ("g_q", HEAD), ("g_k", HEAD), ("g_attn_out", D_ATTN), ("conv_b", D_CONV),
         ("dt_bias", 16), ("a_log", 16), ("d_skip", 16), ("g_ssm_out", D_SSM), ("g_cross", D),
         ("g_mem", D), ("g_cq", CROSS_HEAD), ("g_ck", CROSS_HEAD), ("g_mlp", D))
SMALL_ROWS = -(-sum(-(-w // LANES) for _, w in SMALL) // 8) * 8


def _pack_small(vals):
    rows = [_pad_lanes(vals[n], -(-w // LANES) * LANES).reshape(-1, LANES) for n, w in SMALL]
    packed = jnp.concatenate(rows, axis=0)
    return jnp.pad(packed, ((0, SMALL_ROWS - packed.shape[0]), (0, 0)))


def _unpack_small(packed):
    out, r = {}, 0
    for n, w in SMALL:
        nr = -(-w // LANES)
        out[n] = packed[r:r + nr].reshape(1, nr * LANES)[:, :w]
        r += nr
    return out


BIG = ("w_in", "w_out", "w_cq", "w_ckv", "w_co", "w_up", "w_down")
WEIGHTS = ("g_mix", "w_in", "g_q", "g_k", "g_attn_out", "conv_w", "conv_b", "dt_bias", "a_log", "d_skip",
           "g_ssm_out", "w_out", "g_cross", "g_mem", "w_cq", "w_ckv", "g_cq", "g_ck", "w_co", "g_mlp", "w_up", "w_down")


REST = ("w_out", "w_cq", "w_ckv", "w_co", "w_up", "w_down")


class _Overlap:
    def __init__(self, shards, after):
        self.gather, self.forward = {}, {}
        self.names = {"a": REST[:4], "b": REST[4:5], "c": REST[5:]}
        for tag, names in self.names.items():
            lands = [_put_own(lax.empty((N_DEV,) + shards[n].shape, BF), shards[n]) for n in names]
            self.gather[tag] = _exchange_start("gather", [shards[n] for n in names], lands, after, "ag_start_" + tag)
            after = self.gather[tag][3]
        self.token = after
        self.groups = []
        self.pairs = {}

    def rest_mid(self, tag, after):
        sems, srcs, lands, _ = self.gather[tag]
        srcs, lands = _exchange_wait("gather", sems, srcs, lands, after, "ag_wait_" + tag)
        self.forward[tag] = _exchange_start("forward", srcs, lands, None, "ag_fwd_start_" + tag)
        return self.forward[tag][3]

    def rest(self, tag, after):
        sems, srcs, lands, _ = self.forward[tag]
        _, lands = _exchange_wait("forward", sems, srcs, lands, after, "ag_fwd_wait_" + tag)
        wf = dict(zip(self.names[tag], lands))
        for n in wf:
            if n in ("w_out", "w_cq", "w_ckv", "w_down"):
                wf[n] = wf[n].reshape(-1, wf[n].shape[-1])
        return wf

    def pair_start(self, name, grad):
        got = lax.empty((N_CHIP,) + grad.shape[1:], grad.dtype)
        self.pairs[name] = _exchange_start("pair", [grad], [got], None, "rs_pair_start_" + name)
        return self.pairs[name][3]

    def emit(self, grads, after):
        names, tag = list(grads), "_%d" % len(self.groups)
        grads, got = dict(grads), {}
        for n in names:
            if n in self.pairs:
                sems, srcs, lands, _ = self.pairs[n]
                srcs, lands = _exchange_wait("pair", sems, srcs, lands, after, "rs_pair_wait_" + n)
                grads[n], got[n] = srcs[0], lands[0]
        sync = [n for n in names if n not in got]
        if sync:
            got.update(zip(sync, _pair_exchange([grads[n] for n in sync], "rs_pair" + tag)))
        sums = [_pair_sum(grads[n], got[n], "rs_sum_" + n) for n in names]
        sems, srcs, lands, token = _exchange_start("scatter", [q for q, _ in sums], [l for _, l in sums], None,
                                                   "rs_chip_start" + tag)
        self.groups.append((names, sems, srcs, lands))
        return token

    def finish(self, gi, after):
        names, sems, srcs, lands = self.groups[gi]
        _, lands = _exchange_wait("scatter", sems, srcs, lands, after, "rs_chip_wait_%d" % gi)
        return dict(zip(names, lands))


def _tie(v, token):
    return v if token is None else v + token[0:1, 0:1]


def _local_step(x, mem, pos_col, target, p, wf, hooks=None):
    p = dict(p)
    inv = np.zeros((1, LANES), np.float32)
    freq = (ROPE_THETA ** (-2.0 * np.arange(ROT // 2, dtype=np.float32) / ROT)).astype(np.float32)
    for l in range(LANES):
        if l % HEAD < ROT:
            inv[0, l] = freq[(l % HEAD) % (ROT // 2)]
    inv_tab = jnp.asarray(inv)
    gq128, gk128 = jnp.tile(p["g_q"], (1, 2)), jnp.tile(p["g_k"], (1, 2))
    dtb128, alog128 = _pad_lanes(p["dt_bias"]), _pad_lanes(p["a_log"])
    dskip_b = jnp.repeat(p["d_skip"], HEAD, axis=1)
    conv_w, conv_b = wf["conv_w"], p["conv_b"]

    h = _rms_fwd(x, p["g_mix"], "rms_mix")
    proj = _matmul(h, wf["w_in"], mode="nn", name="mm_in", tm=1024, tn=1280, tk=D)
    qr, kr = _qk_prep(proj, pos_col, gq128, gk128, inv_tab)
    dilations = (1, 4, 16)
    fwd = [_attn_fwd(qr, kr, proj, d, "attn_fwd_d%d" % d) for d in dilations]
    attn, lse, attn_n = _attn_merge([o for o, _ in fwd], [l for _, l in fwd], p["g_attn_out"])

    xbc = _conv_fwd(proj, conv_w, conv_b)
    token = None if hooks is None else hooks.rest_mid("a", xbc)
    dt, a_cs, dt_b, a_b = _ssd_prep(proj, _tie(dtb128, token), alog128)
    at_r, dt_r = _row_layout(a_cs[:, :N_HEADS]), _row_layout(dt[:, :N_HEADS])
    y_ssd, h_all = _ssd_fwd(xbc, a_b, dt_b, at_r, dt_r)
    ssm_n = _ssd_post(y_ssd, xbc, proj, dskip_b, p["g_ssm_out"])

    if hooks is not None:
        wf = {**wf, **hooks.rest("a", ssm_n)}
    mix = jnp.concatenate([attn_n, ssm_n], axis=1)
    x1 = _matmul(mix, wf["w_out"], mode="nn", name="mm_out", tm=1024, tn=1024, tk=D,
                 extras=(x,), epilogue=lambda acc, r: (acc + r,))
    hc = _rms_fwd(x1, _tie(p["g_cross"], None if hooks is None else hooks.rest_mid("b", x1)), "rms_cross")
    memh = _rms_fwd(mem, p["g_mem"], "rms_mem")
    qc = _matmul(hc, wf["w_cq"], mode="nn", name="mm_cq", tm=1024, tn=512, tk=D)
    kv = _matmul(memh, wf["w_ckv"], mode="nn", name="mm_ckv", tm=N_MEM, tn=1024, tk=D)
    oc = _cross_fwd(qc, kv, p["g_cq"], p["g_ck"])
    x2 = _matmul(oc, wf["w_co"], mode="nn", name="mm_co", tm=1024, tn=256, tk=512, b_cb=256,
                 extras=(x1,), epilogue=lambda acc, r: (acc + r,))
    hm = _rms_fwd(x2, p["g_mlp"], "rms_mlp")
    token = None
    if hooks is not None:
        wf = {**wf, **hooks.rest("b", hm)}
        token = hooks.rest_mid("c", hm)

    def up_epi(acc):
        ru = jnp.maximum(acc, 0.0)
        return ru * ru, 2.0 * ru

    act, relu2 = _matmul(hm, wf["w_up"], mode="nn", name="mm_up", tm=1024, tn=1024, tk=D, b_cb=1024,
                         out_dtypes=(BF, BF), epilogue=up_epi, after=token)
    if hooks is not None:
        wf = {**wf, **hooks.rest("c", act)}

    def loss_epi(acc, r, t):
        d = (acc + r - t) * (1.0 / D)
        return d, d

    dy, dyb = _matmul(act, wf["w_down"], mode="nn", name="mm_down", tm=512, tn=1024, tk=2048, extras=(x2, target),
                      out_dtypes=(F32, BF), epilogue=loss_epi)
    loss_part = _loss_sum(dy)[0, 0] * (0.5 * D)

    gb, gs = {}, {}
    gb["w_down"] = _matmul(act, dyb, mode="tn", name="mm_down_dw", tm=1024, tn=1024, tk=S,
                           out_dtypes=(BF,)).reshape(N_DEV, D_FF // N_DEV, D)
    token = None if hooks is None else hooks.pair_start("w_down", gb["w_down"])
    du = _matmul(dyb, wf["w_down"], mode="nt", name="mm_down_dx", tm=1024, tn=1024, tk=D, extras=(relu2,),
                 out_dtypes=(BF,), epilogue=lambda acc, r: (acc * r.astype(F32),), after=token)
    gb["w_up"] = _matmul(hm, du, mode="tn", name="mm_up_dw", tm=1024, tn=1024, tk=S, out_dtypes=(BF,), out_cb=1024)
    token = None if hooks is None else hooks.pair_start("w_up", gb["w_up"])
    dhm = _matmul(du, wf["w_up"], mode="nt", name="mm_up_dx", tm=1024, tn=1024, tk=2048, b_cb=1024, after=token)
    if hooks is not None:
        p["g_mlp"] = _tie(p["g_mlp"], hooks.emit({n: gb[n] for n in ("w_down", "w_up")}, dhm))
    dx2, dx2b, gs["g_mlp"] = _rms_bwd_call(x2, p["g_mlp"], dhm, dy, "rms_mlp_bwd")

    doc = _matmul(dx2b, wf["w_co"], mode="nt", name="mm_co_dx", tm=1024, tn=512, tk=256, b_cb=256)
    gb["w_co"] = _matmul(oc, dx2b, mode="tn", name="mm_co_dw", tm=512, tn=256, tk=S, out_dtypes=(BF,), out_cb=256)
    dqc, dkn, dvc, gs["g_cq"] = _cross_bwd(qc, doc, kv, p["g_cq"], p["g_ck"])
    dkv, gs["g_ck"] = _cross_kv_bwd(kv, dkn, dvc, p["g_ck"])
    gb["w_cq"] = _matmul(hc, dqc, mode="tn", name="mm_cq_dw", tm=1024, tn=512, tk=S,
                         out_dtypes=(BF,)).reshape(N_DEV, D // N_DEV, D_CROSS)
    dhc = _matmul(dqc, wf["w_cq"], mode="nt", name="mm_cq_dx", tm=1024, tn=1024, tk=512)
    gb["w_ckv"] = _matmul(memh, dkv, mode="tn", name="mm_ckv_dw", tm=1024, tn=1024, tk=N_MEM,
                          out_dtypes=(BF,)).reshape(N_DEV, D // N_DEV, 2 * D_CROSS)
    dmemh = _matmul(dkv, wf["w_ckv"], mode="nt", name="mm_ckv_dx", tm=N_MEM, tn=1024, tk=1024)
    (gs["g_mem"],) = _rms_bwd_call(mem, p["g_mem"], dmemh, None, "rms_mem_bwd")
    dx1, dx1b, gs["g_cross"] = _rms_bwd_call(x1, p["g_cross"], dhc, dx2, "rms_cross_bwd")

    dmix = _matmul(dx1b, wf["w_out"], mode="nt", name="mm_out_dx", tm=1024, tn=1024, tk=D)
    gb["w_out"] = _matmul(mix, dx1b, mode="tn", name="mm_out_dw", tm=1024, tn=1024, tk=S,
                          out_dtypes=(BF,)).reshape(N_DEV, D // N_DEV, D)
    if hooks is not None:
        p["g_attn_out"] = _tie(p["g_attn_out"],
                               hooks.emit({n: gb[n] for n in ("w_co", "w_cq", "w_ckv", "w_out")}, dmix))

    dattn, delta, gs["g_attn_out"] = _attn_merge_bwd(dmix, attn, p["g_attn_out"])
    bwd = [_attn_bwd(qr, kr, proj, dattn, lse, delta, d, "attn_bwd_d%d" % d) for d in dilations]
    dq_pre, dk_pre, dv_pre, dgq, dgk = _qk_bwd([t[0] for t in bwd], [t[1] for t in bwd], [t[2] for t in bwd],
                                               proj, pos_col, gq128, gk128, inv_tab)
    gs["g_q"], gs["g_k"] = dgq[:, :HEAD], dgk[:, :HEAD]

    dy_ssd, dz, dxs_skip, dd_lane, gs["g_ssm_out"] = _ssd_post_bwd(y_ssd, xbc, proj, dmix, dskip_b, p["g_ssm_out"])
    gs["d_skip"] = dd_lane.reshape(N_HEADS, HEAD).sum(axis=1).reshape(1, N_HEADS)
    dc_pair, daq_b, dx_ssd, db_pair, dak_r, ddt_r, ddt_b = _ssd_bwd(xbc, dy_ssd, h_all, a_b, dt_b, at_r, dt_r)
    dak = _pad_lanes(_row_layout_inv(dak_r))
    ddt_direct = _pad_lanes(_row_layout_inv(ddt_r))
    ddt_raw, dalog, dbias = _ssd_prep_bwd(daq_b, dak, ddt_direct, ddt_b, dt, proj, dtb128, alog128)
    gs["a_log"], gs["dt_bias"] = dalog[:, :N_HEADS], dbias[:, :N_HEADS]
    same = lambda c: c
    dxbc_x, dcw_x, dcb_x = _conv_bwd(proj, conv_w, conv_b, dxs_skip, dx_ssd, same, same, 0, 8, "conv_bwd_x")
    dxbc_b, dcw_b, dcb_b = _conv_bwd(proj, conv_w, conv_b, db_pair, db_pair, lambda c: 2 * c, lambda c: 2 * c + 1,
                                     D_SSM, 4, "conv_bwd_b")
    dxbc_c, dcw_c, dcb_c = _conv_bwd(proj, conv_w, conv_b, dc_pair, dc_pair, lambda c: 2 * c, lambda c: 2 * c + 1,
                                     D_SSM + 512, 4, "conv_bwd_c")
    g_conv_w = jnp.concatenate([dcw_x, dcw_b, dcw_c], axis=1)
    gs["conv_b"] = jnp.concatenate([dcb_x, dcb_b, dcb_c], axis=1)

    pieces = [dq_pre, dk_pre, dv_pre, dz, dxbc_x, dxbc_b, dxbc_c, ddt_raw]
    pieces.append(jnp.zeros((S, D_INP - sum(t.shape[1] for t in pieces)), BF))
    dproj = jnp.concatenate(pieces, axis=1)
    dw_in = _matmul(h, dproj, mode="tn", name="mm_in_dw", tm=1024, tn=1280, tk=S, out_dtypes=(BF,))
    gb["w_in"] = _w_in_to_blocks(dw_in)
    token = None if hooks is None else hooks.emit({"w_in": gb["w_in"]}, dw_in)
    dh = _matmul(dproj, wf["w_in"], mode="nt", name="mm_in_dx", tm=1024, tn=512, tk=D_INP // 2, after=token)
    grad_x, _, gs["g_mix"] = _rms_bwd_call(x, p["g_mix"], dh, dx1, "rms_mix_bwd")
    return loss_part, grad_x, gb, gs, g_conv_w


def kernel(x, mem, positions, g_mix, w_in, g_q, g_k, g_attn_out, conv_w, conv_b, dt_bias, a_log, d_skip, g_ssm_out, w_out, g_cross, g_mem, w_cq, w_ckv, g_cq, g_ck, w_co, g_mlp, w_up, w_down, loss_target, m_g_mix, m_w_in, m_g_q, m_g_k, m_g_attn_out, m_conv_w, m_conv_b, m_dt_bias, m_a_log, m_d_skip, m_g_ssm_out, m_w_out, m_g_cross, m_g_mem, m_w_cq, m_w_ckv, m_g_cq, m_g_ck, m_w_co, m_g_mlp, m_w_up, m_w_down, v_g_mix, v_w_in, v_g_q, v_g_k, v_g_attn_out, v_conv_w, v_conv_b, v_dt_bias, v_a_log, v_d_skip, v_g_ssm_out, v_w_out, v_g_cross, v_g_mem, v_w_cq, v_w_ckv, v_g_cq, v_g_ck, v_w_co, v_g_mlp, v_w_up, v_w_down):
    args = dict(locals())
    w = {n: args[n] for n in WEIGHTS}
    m = {n: args["m_" + n] for n in WEIGHTS}
    v = {n: args["v_" + n] for n in WEIGHTS}
    small = {n: w[n] for n, _ in SMALL}
    me = 4 * lax.axis_index("x") + 2 * lax.axis_index("y") + lax.axis_index("c")

    w_in_g, conv_w_g = _all_gather([w["w_in"][0].astype(BF), w["conv_w"][0]], "ag_first")
    wf = {"w_in": _w_in_from_blocks(w_in_g), "conv_w": conv_w_g.transpose(1, 0, 2).reshape(CONV_W, D_CONV)}
    overlap = _Overlap({n: w[n][0].astype(BF) for n in REST}, w_in_g)
    p = dict(small)
    p["g_mix"] = _tie(p["g_mix"], overlap.token)

    loss_part, grad_x, _, gs, g_conv_w = _local_step(
        x[0], mem[0], positions.reshape(S, 1), loss_target[0], p, wf, overlap)
    loss = lax.psum(loss_part, ("x", "y", "c"))

    out = {}
    last = grad_x
    for gi in range(3):
        for n, parts in overlap.finish(gi, last).items():
            res_n = _adamw(w[n][0], m[n][0], v[n][0], parts, "adamw_" + n)
            out[n] = [t.reshape(w[n].shape) for t in res_n]
            last = res_n[0]

    sm_parts, cw_parts = _all_gather([_pack_small(gs), g_conv_w], "ag_small_grads", after=last)
    sm = [_unpack_small(t) for t in _adamw(_pack_small(small), _pack_small({n: m[n] for n, _ in SMALL}),
                                           _pack_small({n: v[n] for n, _ in SMALL}), sm_parts, "adamw_small")]
    for n, _ in SMALL:
        out[n] = [t[n] for t in sm]
    cols = D_CONV // N_DEV
    cw_mine = lax.dynamic_slice_in_dim(cw_parts, me * cols, cols, axis=2)
    out["conv_w"] = [t.reshape(w["conv_w"].shape) for t in
                     _adamw(w["conv_w"][0], m["conv_w"][0], v["conv_w"][0], cw_mine, "adamw_conv_w")]

    res = [loss, grad_x.reshape(x.shape)]
    for k in range(4):
        res += [out[n][k] for n in WEIGHTS]
    return tuple(res)
```

```python
import functools
import math

import numpy as np
import jax
import jax.numpy as jnp
from jax import lax
from jax.experimental import pallas as pl
from jax.experimental.pallas import tpu as pltpu

F32 = jnp.float32
BF = jnp.bfloat16

N_DEV = 8
S = 2048
D = 2048
D_ATTN = 1024
N_HEADS = 16
HEAD = 64
ROT = 16
ROPE_THETA = 500000.0
D_SSM = 1024
D_CONV = 2048
CONV_W = 4
N_MEM = 256
D_CROSS = 512
CROSS_HEAD = 128
D_FF = 8192
D_IN = 6160
D_INP = 6400
DT_COLBLK = (4 * D_ATTN + D_CONV) // 128
EPS = 1e-6
BLK = 128
NB = S // BLK
LANES = 128
NEG = -1e30

ADAM_LR = 0.001
ADAM_B1 = 0.9
ADAM_B2 = 0.999
ADAM_EPS = 1e-08
ADAM_WD = 0.01
ADAM_STEP = 10

VMEM_LIMIT_BYTES = 48 * 1024 * 1024
ROW_TILE = 256


def _params(*sem):
    return pltpu.CompilerParams(dimension_semantics=sem, vmem_limit_bytes=VMEM_LIMIT_BYTES)


def _matmul(a, b, *, mode, name, tm, tn, tk, out_dtypes=(F32,), b_cb=None, out_cb=None,
            extras=(), epilogue=None, after=None):
    if mode == "tn":
        kk, m = a.shape
    else:
        m, kk = a.shape
    if b_cb is None:
        br, bc = b.shape
    else:
        br, bc = b.shape[1], N_DEV * b_cb
    n = br if mode == "nt" else bc
    assert m % tm == 0 and n % tn == 0 and kk % tk == 0, (name, m, n, kk)
    nk = kk // tk
    grid = (m // tm, n // tn, nk)

    if mode == "tn":
        a_spec = pl.BlockSpec((tk, tm), lambda i, j, k: (k, i))
    else:
        a_spec = pl.BlockSpec((tm, tk), lambda i, j, k: (i, k))
    if mode == "nt":
        b_blk, b_idx = (tn, tk), (lambda i, j, k: (j, k))
    else:
        b_blk, b_idx = (tk, tn), (lambda i, j, k: (k, j))
    group = 1
    if b_cb is None:
        b_spec = pl.BlockSpec(b_blk, b_idx)
    elif mode == "nt" and tk > b_cb:
        assert tk % b_cb == 0
        group = tk // b_cb
        b_spec = pl.BlockSpec((group, tn, b_cb), lambda i, j, k: (k, j, 0))
    else:
        tc = b_blk[1]
        assert b_cb % tc == 0
        per = b_cb // tc

        def b_idx3(i, j, k):
            r, c = b_idx(i, j, k)
            return (c // per, r, c % per)
        b_spec = pl.BlockSpec((None,) + b_blk, b_idx3)
    ex_specs = [pl.BlockSpec((tm, tn), lambda i, j, k: (i, j)) for _ in extras]
    if out_cb is None:
        out_shape = [jax.ShapeDtypeStruct((m, n), dt) for dt in out_dtypes]
        out_specs = [pl.BlockSpec((tm, tn), lambda i, j, k: (i, j)) for _ in out_dtypes]
    else:
        assert out_cb % tn == 0
        pero = out_cb // tn
        out_shape = [jax.ShapeDtypeStruct((N_DEV, m, out_cb), dt) for dt in out_dtypes]
        out_specs = [pl.BlockSpec((None, tm, tn), lambda i, j, k: (j // pero, i, j % pero)) for _ in out_dtypes]
    dn = {"nn": (((1,), (0,)), ((), ())), "nt": (((1,), (1,)), ((), ())), "tn": (((0,), (0,)), ((), ()))}[mode]
    ne, no = len(extras), len(out_dtypes)
    n_after = 0 if after is None else 1

    def kern(a_ref, b_ref, *rest):
        ex_refs, out_refs = rest[:ne], rest[ne + n_after:ne + n_after + no]

        def finish(acc):
            outs = (acc,) if epilogue is None else epilogue(acc, *[r[...] for r in ex_refs])
            for r, o in zip(out_refs, outs):
                r[...] = o.astype(r.dtype)

        if group == 1:
            part = lax.dot_general(a_ref[...].astype(BF), b_ref[...].astype(BF), dn, preferred_element_type=F32)
        else:
            part = sum(lax.dot_general(a_ref[:, g * b_cb:(g + 1) * b_cb].astype(BF), b_ref[g].astype(BF), dn,
                                       preferred_element_type=F32) for g in range(group))
        if nk == 1:
            finish(part)
            return
        acc_ref = rest[ne + n_after + no]
        k = pl.program_id(2)

        @pl.when(k == 0)
        def _():
            acc_ref[...] = part

        @pl.when(k > 0)
        def _():
            acc_ref[...] += part

        @pl.when(k == nk - 1)
        def _():
            finish(acc_ref[...])

    res = pl.pallas_call(
        kern, name=name, grid=grid, in_specs=[a_spec, b_spec] + ex_specs + [ANY] * n_after, out_specs=out_specs,
        out_shape=out_shape, scratch_shapes=[pltpu.VMEM((tm, tn), F32)] if nk > 1 else [],
        compiler_params=_params("parallel", "parallel", "arbitrary"),
    )(a, b, *extras, *([] if after is None else [after]))
    return res[0] if no == 1 else tuple(res)


def _rowwise(body, *, name, nrows, tile, row_ins, full_ins=(), row_outs=(), acc_outs=(), scratch=(),
             reverse=False):
    n = nrows // tile

    def ridx(i):
        return (n - 1 - i) if reverse else i

    in_specs, args = [], []
    for arr, width, cb in row_ins:
        in_specs.append(pl.BlockSpec((tile, width), lambda i, cb=cb: (ridx(i), cb)))
        args.append(arr)
    for arr in full_ins:
        in_specs.append(pl.BlockSpec(arr.shape, lambda i, nd=arr.ndim: (0,) * nd))
        args.append(arr)
    out_shape, out_specs = [], []
    for width, dt in row_outs:
        out_shape.append(jax.ShapeDtypeStruct((nrows, width), dt))
        out_specs.append(pl.BlockSpec((tile, width), lambda i: (ridx(i), 0)))
    for shp, dt in acc_outs:
        out_shape.append(jax.ShapeDtypeStruct(shp, dt))
        out_specs.append(pl.BlockSpec(shp, lambda i, nd=len(shp): (0,) * nd))

    def kern(*refs):
        body(pl.program_id(0), n, *refs)

    return pl.pallas_call(kern, name=name, grid=(n,), in_specs=in_specs, out_specs=out_specs,
                          out_shape=out_shape, scratch_shapes=list(scratch),
                          compiler_params=_params("arbitrary"))(*args)


def _accum(ref, val, i):
    @pl.when(i == 0)
    def _():
        ref[...] = val

    @pl.when(i > 0)
    def _():
        ref[...] += val


def _sigmoid(x):
    return 1.0 / (1.0 + jnp.exp(-x))


def _rms_n(x):
    r = lax.rsqrt(jnp.mean(x * x, axis=-1, keepdims=True) + EPS)
    return x * r, r


def _rms_bwd(x, g, dh):
    n, r = _rms_n(x)
    dn = dh * g
    dx = r * (dn - n * jnp.mean(dn * n, axis=-1, keepdims=True))
    return dx, jnp.sum(dh * n, axis=0, keepdims=True)


def _seg_sum(x, seg):
    t, w = x.shape
    tiles = [x[:, LANES * j:LANES * (j + 1)] for j in range(w // LANES)]
    outs = []
    if seg == 64:
        lo = lax.broadcasted_iota(jnp.int32, (t, LANES), 1) < 64
        for xt in tiles:
            s_lo = jnp.sum(jnp.where(lo, xt, 0.0), axis=-1, keepdims=True)
            s_hi = jnp.sum(jnp.where(lo, 0.0, xt), axis=-1, keepdims=True)
            outs.append(jnp.where(lo, s_lo, s_hi))
    else:
        sums = [jnp.sum(xt, axis=-1, keepdims=True) for xt in tiles]
        if seg == 256:
            sums = [sums[2 * (j // 2)] + sums[2 * (j // 2) + 1] for j in range(len(sums))]
        else:
            assert seg == 128
        outs = [jnp.broadcast_to(s, (t, LANES)) for s in sums]
    return outs[0] if len(outs) == 1 else jnp.concatenate(outs, axis=1)


def _seg_rms_n(x, seg):
    r = lax.rsqrt(_seg_sum(x * x, seg) * (1.0 / seg) + EPS)
    return x * r, r


def _seg_rms_bwd(x, g, dy, seg):
    n, r = _seg_rms_n(x, seg)
    dn = dy * g
    dx = r * (dn - n * (_seg_sum(dn * n, seg) * (1.0 / seg)))
    return dx, jnp.sum(dy * n, axis=0, keepdims=True)


def _rope_tables(pos_col, inv_tab, t):
    ang = pos_col.astype(F32) * inv_tab
    idx = lax.broadcasted_iota(jnp.int32, (t, LANES), 1) % HEAD
    cos, sin = jnp.cos(ang), jnp.sin(ang)
    c = jnp.where(idx < ROT, cos, 1.0)
    sg = jnp.where(idx < ROT // 2, -sin, jnp.where(idx < ROT, sin, 0.0))
    return c, sg, idx < ROT // 2


def _rope(x, c, sg, lo):
    partner = jnp.where(lo, pltpu.roll(x, LANES - ROT // 2, 1), pltpu.roll(x, ROT // 2, 1))
    return x * c + partner * sg


def _fold_heads(v):
    v8 = jnp.broadcast_to(v, (8, LANES))
    return (v8 + pltpu.roll(v8, HEAD, 1))[0:1]


def _dot(a, b, dn):
    return lax.dot_general(a, b, (dn, ((), ())), preferred_element_type=F32)


NN = ((1,), (0,))
NT = ((1,), (1,))
TN = ((0,), (0,))


def _dot3(l01, x):
    x1 = x.astype(BF)
    r1 = x - x1.astype(F32)
    x2 = r1.astype(BF)
    x3 = (r1 - x2.astype(F32)).astype(BF)
    return _dot(l01, x1, NN) + _dot(l01, x2, NN) + _dot(l01, x3, NN)


def _rms_fwd(x, g, name):
    rows = x.shape[0]

    def body(i, n, x_ref, g_ref, o_ref):
        nx, _ = _rms_n(x_ref[...])
        o_ref[...] = (nx * g_ref[...]).astype(BF)

    return _rowwise(body, name=name, nrows=rows, tile=min(ROW_TILE, rows), row_ins=[(x, D, 0)],
                    full_ins=[g], row_outs=[(D, BF)])[0]


def _qk_prep(proj, pos_col, gq128, gk128, inv_tab):
    scale = HEAD ** -0.5

    def body(i, n, q_ref, k_ref, p_ref, gq_ref, gk_ref, it_ref, qo_ref, ko_ref):
        t = q_ref.shape[0]
        c, sg, lo = _rope_tables(p_ref[...], it_ref[...], t)
        for j in range(D_ATTN // LANES):
            sl = slice(LANES * j, LANES * (j + 1))
            qn, _ = _seg_rms_n(q_ref[:, sl], HEAD)
            kn, _ = _seg_rms_n(k_ref[:, sl], HEAD)
            qo_ref[:, sl] = _rope(qn * gq_ref[...], c, sg, lo) * scale
            ko_ref[:, sl] = _rope(kn * gk_ref[...], c, sg, lo)

    return _rowwise(body, name="qk_prep", nrows=S, tile=ROW_TILE,
                    row_ins=[(proj, D_ATTN, 0), (proj, D_ATTN, 1), (pos_col, 1, 0)],
                    full_ins=[gq128, gk128, inv_tab], row_outs=[(D_ATTN, F32)] * 2)


ATTN_QB = 4
V_COLS = pl.BlockSpec((S, LANES), lambda p: (0, 2 * D_ATTN // LANES + p))


def _band(b, d):
    nb = NB // d
    r, n = b // nb, b % nb
    width, kn = (BLK, n) if nb == 1 else (2 * BLK, jnp.maximum(n - 1, 0))

    def rows(first_member, count):
        if d == 1:
            return pl.ds(pl.multiple_of(first_member, BLK), count)
        return pl.ds(first_member * d + r, count, stride=d)

    qm = n * BLK + (lax.broadcasted_iota(jnp.int32, (2 * BLK, width), 0) & (BLK - 1))
    km = kn * BLK + lax.broadcasted_iota(jnp.int32, (2 * BLK, width), 1)
    valid = km <= qm
    if nb > 1:
        valid = valid & (km >= qm - BLK)
    return rows(n * BLK, BLK), rows(kn * BLK, width), valid


DILATIONS = (1, 4, 16)


def _attn_fwd(q, k, v):
    def kern(q_ref, k_ref, v_ref, a_ref, lse_ref, *scr):
        half0 = lax.broadcasted_iota(jnp.int32, (BLK, LANES), 1) < HEAD
        for gi, d in enumerate(DILATIONS):
            o_ref, l_ref = scr[2 * gi], scr[2 * gi + 1]

            def group(g, carry, d=d, o_ref=o_ref, l_ref=l_ref):
                for bb in range(ATTN_QB):
                    q_rows, k_rows, valid = _band(g * ATTN_QB + bb, d)
                    q = q_ref[q_rows, :]
                    kb, vb = k_ref[k_rows, :].astype(BF), v_ref[k_rows, :].astype(BF)
                    q2 = jnp.concatenate([jnp.where(half0, q, 0.0), jnp.where(half0, 0.0, q)], axis=0).astype(BF)
                    s = jnp.where(valid, _dot(q2, kb, NT), NEG)
                    m = jnp.max(s, axis=-1, keepdims=True)
                    p = jnp.exp(s - m)
                    den = jnp.sum(p, axis=-1, keepdims=True)
                    o2 = _dot(p.astype(BF), vb, NN) / den
                    l2 = m + jnp.log(den)
                    o_ref[q_rows, :] = jnp.where(half0, o2[:BLK], o2[BLK:])
                    l_ref[q_rows, :] = jnp.where(half0, l2[:BLK], l2[BLK:])
                return carry

            lax.fori_loop(0, NB // ATTN_QB, group, 0)

        def merge(i, carry):
            rows = pl.ds(pl.multiple_of(i * ROW_TILE, ROW_TILE), ROW_TILE)
            la, lb, lc = scr[1][rows, :], scr[3][rows, :], scr[5][rows, :]
            m = jnp.maximum(jnp.maximum(la, lb), lc)
            ea, eb, ec = jnp.exp(la - m), jnp.exp(lb - m), jnp.exp(lc - m)
            den = ea + eb + ec
            a_ref[rows, :] = (ea * scr[0][rows, :] + eb * scr[2][rows, :] + ec * scr[4][rows, :]) / den
            lse_ref[rows, :] = m + jnp.log(den)
            return carry

        lax.fori_loop(0, S // ROW_TILE, merge, 0)

    seq = pl.BlockSpec((S, LANES), lambda p: (0, p))
    return pl.pallas_call(
        kern, name="attn_fwd", grid=(D_ATTN // LANES,), in_specs=[seq, seq, V_COLS], out_specs=[seq, seq],
        out_shape=[jax.ShapeDtypeStruct((S, D_ATTN), F32)] * 2,
        scratch_shapes=[pltpu.VMEM((S, LANES), F32)] * (2 * len(DILATIONS)),
        compiler_params=_params("parallel"))(q, k, v)


def _attn_norm(attn, g_attn):
    def body(i, n, a_ref, g_ref, an_ref):
        nx, _ = _rms_n(a_ref[...])
        an_ref[...] = (nx * g_ref[...]).astype(BF)

    return _rowwise(body, name="attn_norm", nrows=S, tile=ROW_TILE, row_ins=[(attn, D_ATTN, 0)],
                    full_ins=[g_attn], row_outs=[(D_ATTN, BF)])[0]


def _conv_taps(x, w_ref):
    rows = lax.broadcasted_iota(jnp.int32, x.shape, 0)
    shifted = []
    for w in range(CONV_W):
        k = CONV_W - 1 - w
        shifted.append(x if k == 0 else jnp.where(rows >= k, pltpu.roll(x, k, 0), 0.0))
    acc = shifted[0] * w_ref[0:1, :]
    for w in range(1, CONV_W):
        acc = acc + shifted[w] * w_ref[w:w + 1, :]
    return acc, shifted


def _conv_fwd(proj, conv_w, conv_b):
    tc = 256

    def kern(x_ref, w_ref, b_ref, o_ref):
        c, _ = _conv_taps(x_ref[...], w_ref)
        c = c + b_ref[...]
        o_ref[...] = c * _sigmoid(c)

    return pl.pallas_call(
        kern, name="conv_fwd", grid=(D_CONV // tc,),
        in_specs=[pl.BlockSpec((S, tc), lambda c: (0, 4 * D_ATTN // tc + c)),
                  pl.BlockSpec((CONV_W, tc), lambda c: (0, c)), pl.BlockSpec((1, tc), lambda c: (0, c))],
        out_specs=pl.BlockSpec((S, tc), lambda c: (0, c)),
        out_shape=jax.ShapeDtypeStruct((S, D_CONV), F32), compiler_params=_params("parallel"))(proj, conv_w, conv_b)


def _softplus(x):
    return jnp.maximum(x, 0.0) + jnp.log1p(jnp.exp(-jnp.abs(x)))


def _dot3r(x, r01):
    x1 = x.astype(BF)
    r1 = x - x1.astype(F32)
    x2 = r1.astype(BF)
    x3 = (r1 - x2.astype(F32)).astype(BF)
    return _dot(x1, r01, NN) + _dot(x2, r01, NN) + _dot(x3, r01, NN)


def _spread_heads(v):
    sel = (lax.broadcasted_iota(jnp.int32, (LANES, D_SSM), 1) // HEAD
           == lax.broadcasted_iota(jnp.int32, (LANES, D_SSM), 0))
    return _dot3r(v, sel.astype(BF))


def _collect_heads(v):
    sel = (lax.broadcasted_iota(jnp.int32, (D_SSM, LANES), 0)
           == HEAD * lax.broadcasted_iota(jnp.int32, (D_SSM, LANES), 1))
    return _dot3r(v, sel.astype(BF))


def _ssd_prep(proj, dt_bias128, a_log128):
    def body(i, n, raw_ref, b_ref, al_ref, dt_ref, a_ref, dtb_ref, ab_ref, carry):
        @pl.when(i == 0)
        def _():
            carry[...] = jnp.zeros_like(carry)

        dt = _softplus(raw_ref[...] + b_ref[...])
        da = dt * (-jnp.exp(al_ref[...]))
        tri = (lax.broadcasted_iota(jnp.int32, (BLK, BLK), 0) >= lax.broadcasted_iota(jnp.int32, (BLK, BLK), 1))
        cs = _dot3(tri.astype(BF), da) + carry[0:1, :]
        dt_ref[...] = dt
        a_ref[...] = cs
        dtb_ref[...] = _spread_heads(dt)
        ab_ref[...] = _spread_heads(cs)
        carry[...] = jnp.broadcast_to(cs[BLK - 1:BLK, :], carry.shape)

    return _rowwise(body, name="ssd_prep", nrows=S, tile=BLK, row_ins=[(proj, LANES, DT_COLBLK)],
                    full_ins=[dt_bias128, a_log128],
                    row_outs=[(LANES, F32), (LANES, F32), (D_SSM, F32), (D_SSM, F32)],
                    scratch=[pltpu.VMEM((8, LANES), F32)])


def _ssd_decay(a_col, at_row, causal):
    diff = jnp.where(causal, a_col - at_row, 0.0)
    return jnp.where(causal, jnp.exp(diff), 0.0)


SSD_CB = 2


def _ssd_fwd(xbc, a_b, dt_b, at_r, dt_r):
    def kern(c_ref, b_ref, x_ref, ab_ref, dtb_ref, at_ref, dt_ref, y_ref, hall_ref):
        half0 = lax.broadcasted_iota(jnp.int32, (BLK, LANES), 1) < HEAD
        causal = lax.broadcasted_iota(jnp.int32, (BLK, BLK), 1) <= lax.broadcasted_iota(jnp.int32, (BLK, BLK), 0)

        def step(i, carry):
            h, a_prev = carry
            for cc in range(SSD_CB):
                chunk = i * SSD_CB + cc
                rows = pl.ds(pl.multiple_of(chunk * BLK, BLK), BLK)
                ci, bi, x = c_ref[rows, :].astype(BF), b_ref[rows, :].astype(BF), x_ref[rows, :]
                ab = ab_ref[rows, :]
                a_end = ab[BLK - 1:BLK, :]
                alpha = jnp.exp(ab - a_prev)
                beta = jnp.exp(a_end - ab) * dtb_ref[rows, :]
                hall_ref[rows, :] = h
                cb = _dot(ci, bi, NT)
                at, dtj = at_ref[chunk], dt_ref[chunk]
                y = alpha * _dot(ci, h.astype(BF), NN)
                for hd in range(2):
                    hm = half0 if hd == 0 else jnp.logical_not(half0)
                    w = cb * _ssd_decay(ab[:, HEAD * hd:HEAD * hd + 1], at[hd:hd + 1, :], causal) * dtj[hd:hd + 1, :]
                    y = y + _dot(w.astype(BF), jnp.where(hm, x, 0.0).astype(BF), NN)
                y_ref[rows, :] = y
                h = alpha[BLK - 1:BLK, :] * h + _dot(bi, (beta * x).astype(BF), TN)
                a_prev = a_end
            return h, a_prev

        lax.fori_loop(0, NB // SSD_CB, step, (jnp.zeros((BLK, LANES), F32), jnp.zeros((1, LANES), F32)))

    npair = D_SSM // LANES
    rowvec = pl.BlockSpec((None, NB, 2, BLK), lambda p: (p, 0, 0, 0))
    seq = pl.BlockSpec((S, LANES), lambda p: (0, p))
    return pl.pallas_call(
        kern, name="ssd_fwd", grid=(npair,),
        in_specs=[pl.BlockSpec((S, LANES), lambda p: (0, 12 + p // 2)),
                  pl.BlockSpec((S, LANES), lambda p: (0, 8 + p // 2)), seq, seq, seq, rowvec, rowvec],
        out_specs=[seq, seq], out_shape=[jax.ShapeDtypeStruct((S, D_SSM), F32)] * 2,
        compiler_params=_params("parallel"))(xbc, xbc, xbc, a_b, dt_b, at_r, dt_r)


def _ssd_post(y_ssd, xbc, proj, dskip_b, g_ssm):
    def body(i, n, y_ref, xs_ref, z_ref, d_ref, g_ref, o_ref):
        z = z_ref[...]
        y2 = (y_ref[...] + d_ref[...] * xs_ref[...]) * (z * _sigmoid(z))
        nx, _ = _seg_rms_n(y2, 256)
        o_ref[...] = (nx * g_ref[...]).astype(BF)

    return _rowwise(body, name="ssd_post", nrows=S, tile=ROW_TILE,
                    row_ins=[(y_ssd, D_SSM, 0), (xbc, D_SSM, 0), (proj, D_SSM, 3)], full_ins=[dskip_b, g_ssm],
                    row_outs=[(D_SSM, BF)])[0]


def _cross_heads(q, kv_ref, gq, gk):
    out = []
    for h in range(D_CROSS // CROSS_HEAD):
        sl = slice(CROSS_HEAD * h, CROSS_HEAD * (h + 1))
        nq, rq = _rms_n(q[:, sl])
        nk, rk = _rms_n(kv_ref[:, sl])
        v = kv_ref[:, D_CROSS + CROSS_HEAD * h:D_CROSS + CROSS_HEAD * (h + 1)]
        out.append((sl, nq, rq, nk, rk, v))
    return out


def _cross_fwd(qc, kv, g_cq, g_ck):
    scale = CROSS_HEAD ** -0.5

    def body(i, n, q_ref, kv_ref, gq_ref, gk_ref, o_ref):
        for sl, nq, _, nk, _, v in _cross_heads(q_ref[...], kv_ref, gq_ref[...], gk_ref[...]):
            qn = (nq * gq_ref[...] * scale).astype(BF)
            kn = (nk * gk_ref[...]).astype(BF)
            s = _dot(qn, kn, NT)
            e = jnp.exp(s - jnp.max(s, axis=-1, keepdims=True))
            p = e / jnp.sum(e, axis=-1, keepdims=True)
            o_ref[:, sl] = _dot(p.astype(BF), v.astype(BF), NN).astype(BF)

    return _rowwise(body, name="cross_fwd", nrows=S, tile=ROW_TILE, row_ins=[(qc, D_CROSS, 0)],
                    full_ins=[kv, g_cq, g_ck], row_outs=[(D_CROSS, BF)])[0]


def _rms_bwd_call(x, g, dh, dres, name, sumsq_res=False):
    rows = x.shape[0]
    has_res = dres is not None

    def body(i, n, *refs):
        if has_res:
            x_ref, dh_ref, dr_ref, g_ref, dx_ref, dxb_ref, dg_ref = refs[:7]
        else:
            x_ref, dh_ref, g_ref, dg_ref = refs
        dx, dg = _rms_bwd(x_ref[...], g_ref[...], dh_ref[...])
        if has_res:
            dr = dr_ref[...]
            dx = dx + dr
            dx_ref[...] = dx
            dxb_ref[...] = dx.astype(BF)
            if sumsq_res:
                _accum(refs[7], jnp.sum(jnp.sum(dr * dr, axis=0, keepdims=True), axis=1, keepdims=True), i)
        _accum(dg_ref, dg, i)

    row_ins = [(x, D, 0), (dh, D, 0)] + ([(dres, D, 0)] if has_res else [])
    return _rowwise(body, name=name, nrows=rows, tile=min(ROW_TILE, rows), row_ins=row_ins, full_ins=[g],
                    row_outs=[(D, F32), (D, BF)] if has_res else [],
                    acc_outs=[((1, D), F32)] + ([((1, 1), F32)] if sumsq_res else []))


def _cross_bwd(qc, doc, kv, g_cq, g_ck):
    scale = CROSS_HEAD ** -0.5

    def body(i, n, q_ref, do_ref, kv_ref, gq_ref, gk_ref, dq_ref, dkn_ref, dv_ref, dgq_ref):
        dgq = jnp.zeros((1, CROSS_HEAD), F32)
        dkn_parts, dv_parts = [], []
        for sl, nq, rq, nk, _, v in _cross_heads(q_ref[...], kv_ref, gq_ref[...], gk_ref[...]):
            qn = (nq * gq_ref[...] * scale).astype(BF)
            kn = (nk * gk_ref[...]).astype(BF)
            s = _dot(qn, kn, NT)
            e = jnp.exp(s - jnp.max(s, axis=-1, keepdims=True))
            p = e / jnp.sum(e, axis=-1, keepdims=True)
            do = do_ref[:, sl].astype(BF)
            dv_parts.append(_dot(p.astype(BF), do, TN))
            dp = _dot(do, v.astype(BF), NT)
            ds = (p * (dp - jnp.sum(dp * p, axis=-1, keepdims=True))).astype(BF)
            dqn = _dot(ds, kn, NN)
            dkn_parts.append(_dot(ds, qn, TN))
            dn = dqn * (gq_ref[...] * scale)
            dq_ref[:, sl] = (rq * (dn - nq * jnp.mean(dn * nq, axis=-1, keepdims=True))).astype(BF)
            dgq = dgq + jnp.sum(dqn * scale * nq, axis=0, keepdims=True)
        _accum(dkn_ref, jnp.concatenate(dkn_parts, axis=1), i)
        _accum(dv_ref, jnp.concatenate(dv_parts, axis=1), i)
        _accum(dgq_ref, dgq, i)

    return _rowwise(body, name="cross_bwd", nrows=S, tile=ROW_TILE, row_ins=[(qc, D_CROSS, 0), (doc, D_CROSS, 0)],
                    full_ins=[kv, g_cq, g_ck], row_outs=[(D_CROSS, BF)],
                    acc_outs=[((N_MEM, D_CROSS), F32), ((N_MEM, D_CROSS), F32), ((1, CROSS_HEAD), F32)])


def _cross_kv_bwd(kv, dkn, dv, g_ck):
    def body(i, n, kv_ref, dkn_ref, dv_ref, gk_ref, dkv_ref, dgk_ref):
        dgk = jnp.zeros((1, CROSS_HEAD), F32)
        for h in range(D_CROSS // CROSS_HEAD):
            sl = slice(CROSS_HEAD * h, CROSS_HEAD * (h + 1))
            dk, dg = _rms_bwd(kv_ref[:, sl], gk_ref[...], dkn_ref[:, sl])
            dkv_ref[:, sl] = dk.astype(BF)
            dgk = dgk + dg
        dkv_ref[:, D_CROSS:] = dv_ref[...].astype(BF)
        dgk_ref[...] = dgk

    return _rowwise(body, name="cross_kv_bwd", nrows=N_MEM, tile=N_MEM,
                    row_ins=[(kv, 2 * D_CROSS, 0), (dkn, D_CROSS, 0), (dv, D_CROSS, 0)], full_ins=[g_ck],
                    row_outs=[(2 * D_CROSS, BF)], acc_outs=[((1, CROSS_HEAD), F32)])


def _attn_merge_bwd(dmix, attn, g_attn):
    def body(i, n, dn_ref, a_ref, g_ref, da_ref, dl_ref, dg_ref):
        attn_v = a_ref[...]
        da, dg = _rms_bwd(attn_v, g_ref[...], dn_ref[...])
        da_ref[...] = da
        dl_ref[...] = _seg_sum(da * attn_v, HEAD)
        _accum(dg_ref, dg, i)

    return _rowwise(body, name="attn_merge_bwd", nrows=S, tile=ROW_TILE,
                    row_ins=[(dmix, D_ATTN, 0), (attn, D_ATTN, 0)], full_ins=[g_attn],
                    row_outs=[(D_ATTN, F32), (D_ATTN, F32)], acc_outs=[((1, D_ATTN), F32)])


def _attn_bwd(q, k, v, do, lse, delta):
    def kern(q_ref, k_ref, v_ref, do_ref, l_ref, d_ref, dq_ref, dk_ref, dv_ref):
        dk_ref[...] = jnp.zeros_like(dk_ref)
        dv_ref[...] = jnp.zeros_like(dv_ref)
        half0 = lax.broadcasted_iota(jnp.int32, (BLK, LANES), 1) < HEAD
        for gi, d in enumerate(DILATIONS):

            def group(g, carry, d=d, first=(gi == 0)):
                updates = []
                for bb in range(ATTN_QB):
                    q_rows, k_rows, valid = _band(g * ATTN_QB + bb, d)
                    q, do = q_ref[q_rows, :], do_ref[q_rows, :]
                    lse_t, del_t = l_ref[q_rows, :], d_ref[q_rows, :]
                    kb, vb = k_ref[k_rows, :].astype(BF), v_ref[k_rows, :].astype(BF)
                    q2 = jnp.concatenate([jnp.where(half0, q, 0.0), jnp.where(half0, 0.0, q)], axis=0).astype(BF)
                    do2 = jnp.concatenate([jnp.where(half0, do, 0.0), jnp.where(half0, 0.0, do)], axis=0).astype(BF)
                    lse2 = jnp.concatenate([lse_t[:, 0:1], lse_t[:, HEAD:HEAD + 1]], axis=0)
                    del2 = jnp.concatenate([del_t[:, 0:1], del_t[:, HEAD:HEAD + 1]], axis=0)
                    s = jnp.where(valid, _dot(q2, kb, NT), NEG)
                    p = jnp.exp(s - lse2)
                    ds = (p * (_dot(do2, vb, NT) - del2)).astype(BF)
                    dq2 = _dot(ds, kb, NN)
                    dq = jnp.where(half0, dq2[:BLK], dq2[BLK:])
                    if first:
                        dq_ref[q_rows, :] = dq
                    else:
                        dq_ref[q_rows, :] += dq
                    updates.append((k_rows, _dot(ds, q2, TN), _dot(p.astype(BF), do2, TN)))
                for k_rows, dk, dv in updates:
                    dk_ref[k_rows, :] += dk
                    dv_ref[k_rows, :] += dv
                return carry

            lax.fori_loop(0, NB // ATTN_QB, group, 0)

    seq = pl.BlockSpec((S, LANES), lambda p: (0, p))
    return pl.pallas_call(
        kern, name="attn_bwd", grid=(D_ATTN // LANES,), in_specs=[seq, seq, V_COLS, seq, seq, seq],
        out_specs=[seq, seq, seq], out_shape=[jax.ShapeDtypeStruct((S, D_ATTN), F32)] * 3,
        compiler_params=_params("parallel"))(q, k, v, do, lse, delta)


def _qk_bwd(dq_rot, dk_rot, dv, proj, pos_col, gq128, gk128, inv_tab):
    scale = HEAD ** -0.5

    def body(i, n, dq_ref, dk_ref, dv_ref, q_ref, k_ref, p_ref, gq_ref, gk_ref, it_ref,
             dqo_ref, dko_ref, dvo_ref, dgq_ref, dgk_ref):
        t = q_ref.shape[0]
        c, sg, lo = _rope_tables(p_ref[...], it_ref[...], t)
        dgq = jnp.zeros((1, LANES), F32)
        dgk = jnp.zeros((1, LANES), F32)
        for j in range(D_ATTN // LANES):
            sl = slice(LANES * j, LANES * (j + 1))
            dqr = _rope(dq_ref[:, sl], c, -sg, lo) * scale
            dkr = _rope(dk_ref[:, sl], c, -sg, lo)
            dq, gq = _seg_rms_bwd(q_ref[:, sl], gq_ref[...], dqr, HEAD)
            dk, gk = _seg_rms_bwd(k_ref[:, sl], gk_ref[...], dkr, HEAD)
            dqo_ref[:, sl] = dq.astype(BF)
            dko_ref[:, sl] = dk.astype(BF)
            dgq, dgk = dgq + gq, dgk + gk
        dvo_ref[...] = dv_ref[...].astype(BF)
        _accum(dgq_ref, _fold_heads(dgq), i)
        _accum(dgk_ref, _fold_heads(dgk), i)

    return _rowwise(body, name="qk_bwd", nrows=S, tile=ROW_TILE,
                    row_ins=[(a, D_ATTN, 0) for a in (dq_rot, dk_rot, dv)]
                    + [(proj, D_ATTN, 0), (proj, D_ATTN, 1), (pos_col, 1, 0)],
                    full_ins=[gq128, gk128, inv_tab], row_outs=[(D_ATTN, BF)] * 3,
                    acc_outs=[((1, LANES), F32)] * 2)


def _ssd_post_bwd(y_ssd, xbc, proj, dmix, dskip_b, g_ssm):
    def body(i, n, y_ref, xs_ref, z_ref, do_ref, d_ref, g_ref, dy_ref, dz_ref, dxs_ref, dd_ref, dg_ref):
        z, xs = z_ref[...], xs_ref[...]
        sg = _sigmoid(z)
        gate = z * sg
        y = y_ref[...] + d_ref[...] * xs
        dy2, dg = _seg_rms_bwd(y * gate, g_ref[...], do_ref[...], 256)
        dy = dy2 * gate
        dy_ref[...] = dy.astype(BF)
        dz_ref[...] = (dy2 * y * (sg * (1.0 + z * (1.0 - sg)))).astype(BF)
        dxs_ref[...] = dy * d_ref[...]
        _accum(dd_ref, jnp.sum(dy * xs, axis=0, keepdims=True), i)
        _accum(dg_ref, dg, i)

    return _rowwise(body, name="ssd_post_bwd", nrows=S, tile=ROW_TILE,
                    row_ins=[(y_ssd, D_SSM, 0), (xbc, D_SSM, 0), (proj, D_SSM, 3), (dmix, D_SSM, 1)],
                    full_ins=[dskip_b, g_ssm], row_outs=[(D_SSM, BF), (D_SSM, BF), (D_SSM, F32)],
                    acc_outs=[((1, D_SSM), F32), ((1, D_SSM), F32)])


def _ssd_bwd(xbc, dy, h_all, a_b, dt_b, at_r, dt_r):
    def kern(c_ref, b_ref, x_ref, dy_ref, hall_ref, ab_ref, dtb_ref, at_ref, dt_ref,
             dc_ref, daq_ref, dx_ref, db_ref, dak_ref, ddt_ref, ddtb_ref):
        half0 = lax.broadcasted_iota(jnp.int32, (BLK, LANES), 1) < HEAD
        hms = (half0, jnp.logical_not(half0))
        causal = lax.broadcasted_iota(jnp.int32, (BLK, BLK), 1) <= lax.broadcasted_iota(jnp.int32, (BLK, BLK), 0)
        row = lax.broadcasted_iota(jnp.int32, (BLK, LANES), 0)

        def step(t, carry_in):
            dh_next, carry = carry_in
            for cc in reversed(range(SSD_CB)):
                chunk = (NB // SSD_CB - 1 - t) * SSD_CB + cc
                rows = pl.ds(pl.multiple_of(chunk * BLK, BLK), BLK)
                ci, bi, x = c_ref[rows, :].astype(BF), b_ref[rows, :].astype(BF), x_ref[rows, :]
                dyv = dy_ref[rows, :].astype(F32)
                ab, dtb = ab_ref[rows, :], dtb_ref[rows, :]
                before = ab_ref[pl.ds(pl.multiple_of(jnp.maximum(chunk * BLK - 8, 0), 8), 8), :][7:8, :]
                a_prev = jnp.where(chunk == 0, 0.0, before)
                a_end = ab[BLK - 1:BLK, :]
                alpha = jnp.exp(ab - a_prev)
                e_end = jnp.exp(a_end - ab)
                beta = e_end * dtb
                t_all = alpha[BLK - 1:BLK, :]
                hc = hall_ref[rows, :]
                hcb, dhb = hc.astype(BF), dh_next.astype(BF)

                cb = _dot(ci, bi, NT)
                at, dtj = at_ref[chunk], dt_ref[chunk]
                dcb = jnp.zeros((BLK, BLK), F32)
                dx = jnp.zeros((BLK, LANES), F32)
                rsum = []
                for hd in range(2):
                    dym = jnp.where(hms[hd], dyv, 0.0).astype(BF)
                    lm = _ssd_decay(ab[:, HEAD * hd:HEAD * hd + 1], at[hd:hd + 1, :], causal)
                    dth = dtj[hd:hd + 1, :]
                    dw = _dot(dym, jnp.where(hms[hd], x, 0.0).astype(BF), NT)
                    g = dw * cb * lm
                    dx = dx + _dot((cb * lm * dth).astype(BF), dym, TN)
                    gcol = jnp.sum(g, axis=0, keepdims=True)
                    ddt_ref[chunk, hd:hd + 1, :] = gcol
                    dak_ref[chunk, hd:hd + 1, :] = gcol * dth
                    rsum.append(jnp.sum(g * dth, axis=1, keepdims=True))
                    dcb = dcb + dw * lm * dth
                dcb = dcb.astype(BF)

                g1 = (alpha * dyv).astype(BF)
                dalpha = dyv * _dot(ci, hcb, NN)
                g2 = _dot(bi, dhb, NN)
                dbeta = x * g2
                bx = (beta * x).astype(BF)
                dc_ref[rows, :] = _dot(dcb, bi, NN) + _dot(g1, hcb, NT)
                db_ref[rows, :] = _dot(dcb, ci, TN) + _dot(bx, dhb, NT)
                dx_ref[rows, :] = dx + beta * g2
                ddtb_ref[rows, :] = _seg_sum(e_end * dbeta, HEAD)
                ada, bdb = alpha * dalpha, beta * dbeta
                t_dt = t_all * jnp.sum(dh_next * hc, axis=0, keepdims=True)
                end_term = _seg_sum(jnp.broadcast_to(jnp.sum(bdb, axis=0, keepdims=True) + t_dt, (8, LANES)), HEAD)[0:1]
                prev_term = _seg_sum(jnp.broadcast_to(jnp.sum(ada, axis=0, keepdims=True) + t_dt, (8, LANES)), HEAD)[0:1]
                daq = jnp.where(half0, rsum[0], rsum[1]) + _seg_sum(ada - bdb, HEAD)
                daq_ref[rows, :] = daq + jnp.where(row == BLK - 1, end_term - carry, 0.0)
                carry = prev_term
                dh_next = t_all * dh_next + _dot(ci, g1, TN)
            return dh_next, carry

        lax.fori_loop(0, NB // SSD_CB, step, (jnp.zeros((BLK, LANES), F32), jnp.zeros((1, LANES), F32)))

    npair = D_SSM // LANES
    rowvec = pl.BlockSpec((None, NB, 2, BLK), lambda p: (p, 0, 0, 0))
    seq = pl.BlockSpec((S, LANES), lambda p: (0, p))
    return pl.pallas_call(
        kern, name="ssd_bwd", grid=(npair,),
        in_specs=[pl.BlockSpec((S, LANES), lambda p: (0, 12 + p // 2)),
                  pl.BlockSpec((S, LANES), lambda p: (0, 8 + p // 2)),
                  seq, seq, seq, seq, seq, rowvec, rowvec],
        out_specs=[seq, seq, seq, seq, rowvec, rowvec, seq],
        out_shape=[jax.ShapeDtypeStruct((S, D_SSM), F32)] * 4
        + [jax.ShapeDtypeStruct((npair, NB, 2, BLK), F32)] * 2 + [jax.ShapeDtypeStruct((S, D_SSM), F32)],
        compiler_params=_params("parallel"))(xbc, xbc, xbc, dy, h_all, a_b, dt_b, at_r, dt_r)


def _ssd_prep_bwd(daq, dak, ddt_row, ddt_lane, dt, proj, dt_bias128, a_log128):
    def body(i, n, daq_ref, dak_ref, dd_ref, dd2_ref, dt_ref, raw_ref, b_ref, al_ref, draw_ref, dal_ref, db_ref,
             carry):
        @pl.when(i == 0)
        def _():
            carry[...] = jnp.zeros_like(carry)

        a = -jnp.exp(al_ref[...])
        tri = (lax.broadcasted_iota(jnp.int32, (BLK, BLK), 0) <= lax.broadcasted_iota(jnp.int32, (BLK, BLK), 1))
        rev = _dot3(tri.astype(BF), _collect_heads(daq_ref[...]) - dak_ref[...]) + carry[0:1, :]
        carry[...] = jnp.broadcast_to(rev[0:1, :], carry.shape)
        dtv = dt_ref[...]
        draw = (dd_ref[...] + _collect_heads(dd2_ref[...]) + a * rev) * _sigmoid(raw_ref[...] + b_ref[...])
        draw_ref[...] = draw.astype(BF)
        _accum(dal_ref, jnp.sum(dtv * rev, axis=0, keepdims=True) * a, i)
        _accum(db_ref, jnp.sum(draw, axis=0, keepdims=True), i)

    return _rowwise(body, name="ssd_prep_bwd", nrows=S, tile=BLK, reverse=True,
                    row_ins=[(daq, D_SSM, 0), (dak, LANES, 0), (ddt_row, LANES, 0), (ddt_lane, D_SSM, 0), (dt, LANES, 0),
                             (proj, LANES, DT_COLBLK)],
                    full_ins=[dt_bias128, a_log128], row_outs=[(LANES, BF)],
                    acc_outs=[((1, LANES), F32), ((1, LANES), F32)], scratch=[pltpu.VMEM((8, LANES), F32)])


def _conv_bwd(proj, conv_w, conv_b, d1, d2, map1, map2, col0, ntiles, name):
    base = (4 * D_ATTN + col0) // LANES

    def kern(x_ref, w_ref, b_ref, d1_ref, d2_ref, dx_ref, dw_ref, db_ref):
        x = x_ref[...]
        c, shifted = _conv_taps(x, w_ref)
        c = c + b_ref[...]
        sg = _sigmoid(c)
        dc = (d1_ref[...] + d2_ref[...]) * (sg * (1.0 + c * (1.0 - sg)))
        rows = lax.broadcasted_iota(jnp.int32, x.shape, 0)
        dx = jnp.zeros_like(x)
        for w in range(CONV_W):
            k = CONV_W - 1 - w
            up = dc if k == 0 else jnp.where(rows < S - k, pltpu.roll(dc, S - k, 0), 0.0)
            dx = dx + up * w_ref[w:w + 1, :]
            dw_ref[w:w + 1, :] = jnp.sum(dc * shifted[w], axis=0, keepdims=True)
        dx_ref[...] = dx.astype(BF)
        db_ref[...] = jnp.sum(dc, axis=0, keepdims=True)

    c0 = col0 // LANES
    return pl.pallas_call(
        kern, name=name, grid=(ntiles,),
        in_specs=[pl.BlockSpec((S, LANES), lambda c: (0, base + c)),
                  pl.BlockSpec((CONV_W, LANES), lambda c: (0, c0 + c)),
                  pl.BlockSpec((1, LANES), lambda c: (0, c0 + c)),
                  pl.BlockSpec((S, LANES), lambda c: (0, map1(c))),
                  pl.BlockSpec((S, LANES), lambda c: (0, map2(c)))],
        out_specs=[pl.BlockSpec((S, LANES), lambda c: (0, c)), pl.BlockSpec((CONV_W, LANES), lambda c: (0, c)),
                   pl.BlockSpec((1, LANES), lambda c: (0, c))],
        out_shape=[jax.ShapeDtypeStruct((S, ntiles * LANES), BF), jax.ShapeDtypeStruct((CONV_W, ntiles * LANES), F32),
                   jax.ShapeDtypeStruct((1, ntiles * LANES), F32)],
        compiler_params=_params("parallel"))(proj, conv_w, conv_b, d1, d2)


def _adamw(w, m, v, parts, name):
    r, c = w.shape
    n_parts = parts.shape[0]
    tile = r if r <= 512 else (128 if c > 1024 else 256)
    assert r % tile == 0
    c1 = 1.0 - ADAM_B1 ** ADAM_STEP
    c2 = 1.0 - ADAM_B2 ** ADAM_STEP

    def kern(w_ref, m_ref, v_ref, p_ref, g_ref, d_ref, mo_ref, vo_ref):
        g = p_ref[0].astype(F32)
        for k in range(1, n_parts):
            g = g + p_ref[k].astype(F32)
        mn = ADAM_B1 * m_ref[...] + (1.0 - ADAM_B1) * g
        vn = ADAM_B2 * v_ref[...] + (1.0 - ADAM_B2) * (g * g)
        g_ref[...] = g
        mo_ref[...] = mn
        vo_ref[...] = vn
        d_ref[...] = -ADAM_LR * ((mn / c1) / (jnp.sqrt(vn / c2) + ADAM_EPS) + ADAM_WD * w_ref[...])

    blk = pl.BlockSpec((tile, c), lambda i: (i, 0))
    return pl.pallas_call(
        kern, name=name, grid=(r // tile,),
        in_specs=[blk, blk, blk, pl.BlockSpec((n_parts, tile, c), lambda i: (0, i, 0))],
        out_specs=[blk] * 4, out_shape=[jax.ShapeDtypeStruct((r, c), F32)] * 4,
        compiler_params=_params("parallel"))(w, m, v, parts)


MESH = pl.DeviceIdType.MESH
ANY = pl.BlockSpec(memory_space=pl.ANY)


def _put_own(gathered, shard):
    me = 4 * lax.axis_index("x") + 2 * lax.axis_index("y") + lax.axis_index("c")
    return lax.dynamic_update_slice(gathered, shard[None], (me,) + (0,) * shard.ndim)


def _all_gather(arrs, name, after=None):
    na = len(arrs)
    n_after = 0 if after is None else 1

    def kern(*refs):
        ins, outs = refs[:na], refs[na + n_after:2 * na + n_after]
        send_sems, recv_sems = refs[2 * na + n_after:]
        x, y, c = lax.axis_index("x"), lax.axis_index("y"), lax.axis_index("c")
        me, sibling = (x, y, c), (x, y, 1 - c)
        a_chip = (jnp.where(c == 0, 1 - x, x), jnp.where(c == 0, y, 1 - y))
        b_chip = (jnp.where(c == 0, x, 1 - x), jnp.where(c == 0, 1 - y, y))
        chips = [a_chip, b_chip, (1 - x, 1 - y)]

        def copy(a, k, block, to, src=None):
            px, py, pc = block
            dst = outs[a].at[4 * px + 2 * py + pc]
            return pltpu.make_async_remote_copy(
                src_ref=dst if src is None else src, dst_ref=dst, send_sem=send_sems.at[a, k],
                recv_sem=recv_sems.at[a, k], device_id=to, device_id_type=MESH)

        sends = []
        for a in range(na):
            sends.append(copy(a, 0, me, sibling, src=ins[a]))
            sends += [copy(a, 1 + j, me, (*chips[j], c), src=ins[a]) for j in range(2)]
        for cp in sends:
            cp.start()
        for a in range(na):
            for j, chip in enumerate(chips):
                copy(a, 1 + j, (*chip, c), me).wait_recv()
                later = [copy(a, 4 + j, (*chip, c), sibling)]
                if j == 0:
                    later.append(copy(a, 3, (*a_chip, c), (*b_chip, c)))
                for cp in later:
                    cp.start()
                sends += later
        for a in range(na):
            copy(a, 0, sibling, me).wait_recv()
            for j, chip in enumerate(chips):
                copy(a, 4 + j, (*chip, 1 - c), me).wait_recv()
        for cp in sends:
            cp.wait_send()

    outs = pl.pallas_call(
        kern, name=name, in_specs=[ANY] * (na + n_after), out_specs=[ANY] * na,
        out_shape=[jax.ShapeDtypeStruct((N_DEV,) + a.shape, a.dtype) for a in arrs],
        scratch_shapes=[pltpu.SemaphoreType.DMA((na, 7)), pltpu.SemaphoreType.DMA((na, 7))],
    )(*arrs, *([] if after is None else [after]))
    return [_put_own(o, a) for o, a in zip(outs, arrs)]


HBM = pl.BlockSpec(memory_space=pltpu.HBM)
SEM = pl.BlockSpec(memory_space=pltpu.SEMAPHORE)
EFFECT = pltpu.SideEffectType.DATAFLOW_SIDE_EFFECTING
N_CHIP = 4
N_COPIES = {"gather": 3, "scatter": 3, "forward": 4, "pair": 4}


def _in_hbm(a):
    return pltpu.with_memory_space_constraint(a, pltpu.HBM)


def _chip_copies(kind, src_refs, land_refs, send_sems, recv_sems):
    x, y, c = lax.axis_index("x"), lax.axis_index("y"), lax.axis_index("c")
    chips = [(1 - x, y), (x, 1 - y), (1 - x, 1 - y)]
    n = N_COPIES[kind]
    cps = []

    def add(a, j, src, dst, to):
        cps.append(pltpu.make_async_remote_copy(
            src_ref=src, dst_ref=dst, send_sem=send_sems.at[n * a + j], recv_sem=recv_sems.at[n * a + j],
            device_id=to, device_id_type=MESH))

    for a in range(len(src_refs)):
        if kind == "gather":
            for j, (px, py) in enumerate(chips):
                add(a, j, src_refs[a], land_refs[a].at[4 * x + 2 * y + c], (px, py, c))
        elif kind == "scatter":
            for j, (px, py) in enumerate(chips):
                add(a, j, src_refs[a].at[2 * px + py], land_refs[a].at[2 * x + y], (px, py, c))
        elif kind == "forward":
            add(a, 0, src_refs[a], land_refs[a].at[4 * x + 2 * y + c], (x, y, 1 - c))
            for j, (px, py) in enumerate(chips):
                slot = land_refs[a].at[4 * px + 2 * py + c]
                add(a, 1 + j, slot, slot, (x, y, 1 - c))
        else:
            for q in range(N_CHIP):
                add(a, q, src_refs[a].at[2 * q + 1 - c], land_refs[a].at[q], (x, y, 1 - c))
    return cps


def _exchange_start(kind, srcs, lands, after, name):
    na = len(srcs)
    n_after = 0 if after is None else 1

    def kern(*refs):
        src_refs, land_refs = refs[:na], refs[na:2 * na]
        send_sems, recv_sems = refs[2 * na + n_after], refs[2 * na + n_after + 1]
        token = refs[-1]
        for cp in _chip_copies(kind, src_refs, land_refs, send_sems, recv_sems):
            cp.start()
        token[...] = jnp.zeros_like(token)

    bufs = list(srcs) + list(lands)
    res = pl.pallas_call(
        kern, name=name,
        out_shape=(pltpu.SemaphoreType.DMA((N_COPIES[kind] * na,)), pltpu.SemaphoreType.DMA((N_COPIES[kind] * na,)))
        + tuple(pltpu.HBM(b.shape, b.dtype) for b in bufs) + (jax.ShapeDtypeStruct((8, LANES), F32),),
        in_specs=[HBM] * (2 * na) + [ANY] * n_after,
        out_specs=(SEM, SEM) + (HBM,) * (2 * na) + (pl.BlockSpec(memory_space=pltpu.VMEM),),
        input_output_aliases={i: 2 + i for i in range(2 * na)},
        compiler_params=pltpu.CompilerParams(has_side_effects=EFFECT),
    )(*[_in_hbm(b) for b in bufs], *([] if after is None else [after]))
    return (res[0], res[1]), list(res[2:2 + na]), list(res[2 + na:2 + 2 * na]), res[-1]


def _exchange_wait(kind, sems, srcs, lands, after, name):
    na = len(srcs)

    def kern(*refs):
        src_refs, land_refs = refs[:na], refs[na:2 * na]
        send_sems, recv_sems = refs[2 * na], refs[2 * na + 1]
        for cp in _chip_copies(kind, src_refs, land_refs, send_sems, recv_sems):
            cp.wait_send()
            cp.wait_recv()

    bufs = list(srcs) + list(lands)
    res = pl.pallas_call(
        kern, name=name, out_shape=tuple(pltpu.HBM(b.shape, b.dtype) for b in bufs),
        in_specs=[HBM] * (2 * na) + [SEM, SEM, ANY], out_specs=(HBM,) * (2 * na),
        input_output_aliases={i: i for i in range(2 * na)},
        compiler_params=pltpu.CompilerParams(has_side_effects=EFFECT),
    )(*bufs, sems[0], sems[1], after)
    return list(res[:na]), list(res[na:])


def _gather_finish(shards, lands, name):
    na = len(shards)

    def kern(*refs):
        ins, outs = refs[:na], refs[2 * na:3 * na]
        send_sems, recv_sems = refs[3 * na:]
        x, y, c = lax.axis_index("x"), lax.axis_index("y"), lax.axis_index("c")
        chips = [(1 - x, y), (x, 1 - y), (1 - x, 1 - y)]

        def copy(a, k, slot, pc, src=None):
            dst = outs[a].at[slot + pc]
            return pltpu.make_async_remote_copy(
                src_ref=dst if src is None else src, dst_ref=dst, send_sem=send_sems.at[a, k],
                recv_sem=recv_sems.at[a, k], device_id=(x, y, 1 - c), device_id_type=MESH)

        sends = []
        for a in range(na):
            sends.append(copy(a, 0, 4 * x + 2 * y, c, src=ins[a]))
            sends += [copy(a, 1 + j, 4 * px + 2 * py, c) for j, (px, py) in enumerate(chips)]
        for cp in sends:
            cp.start()
        for a in range(na):
            copy(a, 0, 4 * x + 2 * y, 1 - c).wait_recv()
            for j, (px, py) in enumerate(chips):
                copy(a, 1 + j, 4 * px + 2 * py, 1 - c).wait_recv()
        for cp in sends:
            cp.wait_send()

    return pl.pallas_call(
        kern, name=name, in_specs=[ANY] * (2 * na), out_specs=[ANY] * na,
        out_shape=[jax.ShapeDtypeStruct(l.shape, l.dtype) for l in lands],
        input_output_aliases={na + a: a for a in range(na)},
        scratch_shapes=[pltpu.SemaphoreType.DMA((na, 4)), pltpu.SemaphoreType.DMA((na, 4))])(*shards, *lands)


def _pair_exchange(parts, name):
    na = len(parts)

    def kern(*refs):
        ins, got = refs[:na], refs[na:2 * na]
        send_sems, recv_sems = refs[2 * na:]
        x, y, c = lax.axis_index("x"), lax.axis_index("y"), lax.axis_index("c")
        sends = []
        for a in range(na):
            for q in range(N_CHIP):
                sends.append(pltpu.make_async_remote_copy(
                    src_ref=ins[a].at[2 * q + 1 - c], dst_ref=got[a].at[q], send_sem=send_sems.at[a, q],
                    recv_sem=recv_sems.at[a, q], device_id=(x, y, 1 - c), device_id_type=MESH))
        for cp in sends:
            cp.start()
        for cp in sends:
            cp.wait()

    return pl.pallas_call(
        kern, name=name, in_specs=[ANY] * na, out_specs=[ANY] * na,
        out_shape=[jax.ShapeDtypeStruct((N_CHIP,) + p.shape[1:], p.dtype) for p in parts],
        scratch_shapes=[pltpu.SemaphoreType.DMA((na, N_CHIP)), pltpu.SemaphoreType.DMA((na, N_CHIP))])(*parts)


def _pair_sum(parts, got, name):
    _, r, cdim = parts.shape
    tile = min(r, ROW_TILE)
    x, y, c = lax.axis_index("x"), lax.axis_index("y"), lax.axis_index("c")
    where = jnp.stack([c, 2 * x + y]).astype(jnp.int32)

    def kern(w_ref, a_ref, b_ref, q_ref, own_ref):
        s = (a_ref[...].astype(F32) + b_ref[...].astype(F32)).astype(BF)
        q_ref[...] = s

        @pl.when(pl.program_id(1) == w_ref[1])
        def _():
            own_ref[...] = s

    blk = pl.BlockSpec((None, tile, cdim), lambda i, q, w_ref: (q, i, 0))
    sums, own = pl.pallas_call(
        kern, name=name,
        out_shape=[jax.ShapeDtypeStruct((N_CHIP, r, cdim), BF), jax.ShapeDtypeStruct((r, cdim), BF)],
        grid_spec=pltpu.PrefetchScalarGridSpec(
            num_scalar_prefetch=1, grid=(r // tile, N_CHIP),
            in_specs=[pl.BlockSpec((None, tile, cdim), lambda i, q, w_ref: (2 * q + w_ref[0], i, 0)), blk],
            out_specs=[blk, pl.BlockSpec((tile, cdim), lambda i, q, w_ref: (i, 0))]),
        compiler_params=_params("parallel", "arbitrary"))(where, parts, got)
    land = lax.dynamic_update_slice(lax.empty((N_CHIP, r, cdim), BF), own[None], (2 * x + y, 0, 0))
    return sums, land


W_IN_COLS = D_IN // N_DEV


def _w_in_from_blocks(w8):
    def kern(x_ref, o_ref):
        for j in range(N_DEV):
            o_ref[:, W_IN_COLS * j:W_IN_COLS * (j + 1)] = x_ref[j]
        o_ref[:, D_IN:] = jnp.zeros((ROW_TILE, D_INP - D_IN), o_ref.dtype)

    return pl.pallas_call(
        kern, name="w_in_from_blocks", grid=(D // ROW_TILE,),
        in_specs=[pl.BlockSpec((N_DEV, ROW_TILE, W_IN_COLS), lambda i: (0, i, 0))],
        out_specs=pl.BlockSpec((ROW_TILE, D_INP), lambda i: (i, 0)),
        out_shape=jax.ShapeDtypeStruct((D, D_INP), w8.dtype), compiler_params=_params("parallel"))(w8)


def _w_in_to_blocks(g):
    def kern(x_ref, o_ref):
        for j in range(N_DEV):
            o_ref[j] = x_ref[:, W_IN_COLS * j:W_IN_COLS * (j + 1)]

    return pl.pallas_call(
        kern, name="w_in_to_blocks", grid=(D // ROW_TILE,),
        in_specs=[pl.BlockSpec((ROW_TILE, D_INP), lambda i: (i, 0))],
        out_specs=pl.BlockSpec((N_DEV, ROW_TILE, W_IN_COLS), lambda i: (0, i, 0)),
        out_shape=jax.ShapeDtypeStruct((N_DEV, D, W_IN_COLS), g.dtype), compiler_params=_params("parallel"))(g)


def _pad_lanes(v, width=LANES):
    return jnp.pad(v, ((0, 0), (0, width - v.shape[1])))


def _row_layout(t16):
    return t16.T.reshape(N_HEADS // 2, 2, NB, BLK).transpose(0, 2, 1, 3)


def _row_layout_inv(r):
    return r.transpose(0, 2, 1, 3).reshape(N_HEADS, S).T


SMALL = (("g_mix", D), ("g_q", HEAD), ("g_k", HEAD), ("g_attn_out", D_ATTN), ("conv_b", D_CONV),
         ("dt_bias", 16), ("a_log", 16), ("d_skip", 16), ("g_ssm_out", D_SSM), ("g_cross", D),
         ("g_mem", D), ("g_cq", CROSS_HEAD), ("g_ck", CROSS_HEAD), ("g_mlp", D))
SMALL_ROWS = -(-sum(-(-w // LANES) for _, w in SMALL) // 8) * 8


def _pack_small(vals):
    rows = [_pad_lanes(vals[n], -(-w // LANES) * LANES).reshape(-1, LANES) for n, w in SMALL]
    packed = jnp.concatenate(rows, axis=0)
    return jnp.pad(packed, ((0, SMALL_ROWS - packed.shape[0]), (0, 0)))


def _unpack_small(packed):
    out, r = {}, 0
    for n, w in SMALL:
        nr = -(-w // LANES)
        out[n] = packed[r:r + nr].reshape(1, nr * LANES)[:, :w]
        r += nr
    return out


BIG = ("w_in", "w_out", "w_cq", "w_ckv", "w_co", "w_up", "w_down")
WEIGHTS = ("g_mix", "w_in", "g_q", "g_k", "g_attn_out", "conv_w", "conv_b", "dt_bias", "a_log", "d_skip",
           "g_ssm_out", "w_out", "g_cross", "g_mem", "w_cq", "w_ckv", "g_cq", "g_ck", "w_co", "g_mlp", "w_up", "w_down")


REST = ("w_out", "w_cq", "w_ckv", "w_co", "w_up", "w_down")


class _Overlap:
    def __init__(self, shards, after):
        self.gather, self.forward = {}, {}
        self.names = {"a": REST[:4], "b": REST[4:5], "c": REST[5:]}
        for tag, names in self.names.items():
            lands = [_put_own(lax.empty((N_DEV,) + shards[n].shape, BF), shards[n]) for n in names]
            self.gather[tag] = _exchange_start("gather", [shards[n] for n in names], lands, after, "ag_start_" + tag)
            after = self.gather[tag][3]
        self.token = after
        self.groups = []
        self.pairs = {}

    def rest_mid(self, tag, after):
        sems, srcs, lands, _ = self.gather[tag]
        srcs, lands = _exchange_wait("gather", sems, srcs, lands, after, "ag_wait_" + tag)
        self.forward[tag] = _exchange_start("forward", srcs, lands, None, "ag_fwd_start_" + tag)
        return self.forward[tag][3]

    def rest(self, tag, after):
        sems, srcs, lands, _ = self.forward[tag]
        _, lands = _exchange_wait("forward", sems, srcs, lands, after, "ag_fwd_wait_" + tag)
        wf = dict(zip(self.names[tag], lands))
        for n in wf:
            if n in ("w_out", "w_cq", "w_ckv", "w_down"):
                wf[n] = wf[n].reshape(-1, wf[n].shape[-1])
        return wf

    def pair_start(self, name, grad):
        got = lax.empty((N_CHIP,) + grad.shape[1:], grad.dtype)
        self.pairs[name] = _exchange_start("pair", [grad], [got], None, "rs_pair_start_" + name)
        return self.pairs[name][3]

    def emit(self, grads, after):
        names, tag = list(grads), "_%d" % len(self.groups)
        grads, got = dict(grads), {}
        for n in names:
            if n in self.pairs:
                sems, srcs, lands, _ = self.pairs[n]
                srcs, lands = _exchange_wait("pair", sems, srcs, lands, after, "rs_pair_wait_" + n)
                grads[n], got[n] = srcs[0], lands[0]
        sync = [n for n in names if n not in got]
        if sync:
            got.update(zip(sync, _pair_exchange([grads[n] for n in sync], "rs_pair" + tag)))
        sums = [_pair_sum(grads[n], got[n], "rs_sum_" + n) for n in names]
        sems, srcs, lands, token = _exchange_start("scatter", [q for q, _ in sums], [l for _, l in sums], None,
                                                   "rs_chip_start" + tag)
        self.groups.append((names, sems, srcs, lands))
        return token

    def finish(self, gi, after):
        names, sems, srcs, lands = self.groups[gi]
        _, lands = _exchange_wait("scatter", sems, srcs, lands, after, "rs_chip_wait_%d" % gi)
        return dict(zip(names, lands))


def _tie(v, token):
    return v if token is None else v + token[0:1, 0:1]


def _local_step(x, mem, pos_col, target, p, wf, hooks=None):
    p = dict(p)
    inv = np.zeros((1, LANES), np.float32)
    freq = (ROPE_THETA ** (-2.0 * np.arange(ROT // 2, dtype=np.float32) / ROT)).astype(np.float32)
    for l in range(LANES):
        if l % HEAD < ROT:
            inv[0, l] = freq[(l % HEAD) % (ROT // 2)]
    inv_tab = jnp.asarray(inv)
    gq128, gk128 = jnp.tile(p["g_q"], (1, 2)), jnp.tile(p["g_k"], (1, 2))
    dtb128, alog128 = _pad_lanes(p["dt_bias"]), _pad_lanes(p["a_log"])
    dskip_b = jnp.repeat(p["d_skip"], HEAD, axis=1)
    conv_w, conv_b = wf["conv_w"], p["conv_b"]

    h = _rms_fwd(x, p["g_mix"], "rms_mix")
    proj = _matmul(h, wf["w_in"], mode="nn", name="mm_in", tm=1024, tn=1280, tk=D)
    qr, kr = _qk_prep(proj, pos_col, gq128, gk128, inv_tab)
    attn, lse = _attn_fwd(qr, kr, proj)
    attn_n = _attn_norm(attn, p["g_attn_out"])

    xbc = _conv_fwd(proj, conv_w, conv_b)
    token = None if hooks is None else hooks.rest_mid("a", xbc)
    dt, a_cs, dt_b, a_b = _ssd_prep(proj, _tie(dtb128, token), alog128)
    at_r, dt_r = _row_layout(a_cs[:, :N_HEADS]), _row_layout(dt[:, :N_HEADS])
    y_ssd, h_all = _ssd_fwd(xbc, a_b, dt_b, at_r, dt_r)
    ssm_n = _ssd_post(y_ssd, xbc, proj, dskip_b, p["g_ssm_out"])

    if hooks is not None:
        wf = {**wf, **hooks.rest("a", ssm_n)}
    mix = jnp.concatenate([attn_n, ssm_n], axis=1)
    x1 = _matmul(mix, wf["w_out"], mode="nn", name="mm_out", tm=1024, tn=1024, tk=D,
                 extras=(x,), epilogue=lambda acc, r: (acc + r,))
    hc = _rms_fwd(x1, _tie(p["g_cross"], None if hooks is None else hooks.rest_mid("b", x1)), "rms_cross")
    memh = _rms_fwd(mem, p["g_mem"], "rms_mem")
    qc = _matmul(hc, wf["w_cq"], mode="nn", name="mm_cq", tm=1024, tn=512, tk=D)
    kv = _matmul(memh, wf["w_ckv"], mode="nn", name="mm_ckv", tm=N_MEM, tn=1024, tk=D)
    oc = _cross_fwd(qc, kv, p["g_cq"], p["g_ck"])
    x2 = _matmul(oc, wf["w_co"], mode="nn", name="mm_co", tm=1024, tn=256, tk=512, b_cb=256,
                 extras=(x1,), epilogue=lambda acc, r: (acc + r,))
    hm = _rms_fwd(x2, p["g_mlp"], "rms_mlp")
    token = None
    if hooks is not None:
        wf = {**wf, **hooks.rest("b", hm)}
        token = hooks.rest_mid("c", hm)

    def up_epi(acc):
        ru = jnp.maximum(acc, 0.0)
        return ru * ru, 2.0 * ru

    act, relu2 = _matmul(hm, wf["w_up"], mode="nn", name="mm_up", tm=1024, tn=1024, tk=D, b_cb=1024,
                         out_dtypes=(BF, BF), epilogue=up_epi, after=token)
    if hooks is not None:
        wf = {**wf, **hooks.rest("c", act)}

    def loss_epi(acc, r, t):
        d = (acc + r - t) * (1.0 / D)
        return d, d

    dy, dyb = _matmul(act, wf["w_down"], mode="nn", name="mm_down", tm=512, tn=1024, tk=2048, extras=(x2, target),
                      out_dtypes=(F32, BF), epilogue=loss_epi)

    gb, gs = {}, {}
    gb["w_down"] = _matmul(act, dyb, mode="tn", name="mm_down_dw", tm=1024, tn=1024, tk=S,
                           out_dtypes=(BF,)).reshape(N_DEV, D_FF // N_DEV, D)
    token = None if hooks is None else hooks.pair_start("w_down", gb["w_down"])
    du = _matmul(dyb, wf["w_down"], mode="nt", name="mm_down_dx", tm=1024, tn=1024, tk=D, extras=(relu2,),
                 out_dtypes=(BF,), epilogue=lambda acc, r: (acc * r.astype(F32),), after=token)
    gb["w_up"] = _matmul(hm, du, mode="tn", name="mm_up_dw", tm=1024, tn=1024, tk=S, out_dtypes=(BF,), out_cb=1024)
    token = None if hooks is None else hooks.pair_start("w_up", gb["w_up"])
    dhm = _matmul(du, wf["w_up"], mode="nt", name="mm_up_dx", tm=1024, tn=1024, tk=2048, b_cb=1024, after=token)
    if hooks is not None:
        p["g_mlp"] = _tie(p["g_mlp"], hooks.emit({n: gb[n] for n in ("w_down", "w_up")}, dhm))
    dx2, dx2b, gs["g_mlp"], sumsq_dy = _rms_bwd_call(x2, p["g_mlp"], dhm, dy, "rms_mlp_bwd", sumsq_res=True)
    loss_part = sumsq_dy[0, 0] * (0.5 * D)

    doc = _matmul(dx2b, wf["w_co"], mode="nt", name="mm_co_dx", tm=1024, tn=512, tk=256, b_cb=256)
    gb["w_co"] = _matmul(oc, dx2b, mode="tn", name="mm_co_dw", tm=512, tn=256, tk=S, out_dtypes=(BF,), out_cb=256)
    dqc, dkn, dvc, gs["g_cq"] = _cross_bwd(qc, doc, kv, p["g_cq"], p["g_ck"])
    dkv, gs["g_ck"] = _cross_kv_bwd(kv, dkn, dvc, p["g_ck"])
    gb["w_cq"] = _matmul(hc, dqc, mode="tn", name="mm_cq_dw", tm=1024, tn=512, tk=S,
                         out_dtypes=(BF,)).reshape(N_DEV, D // N_DEV, D_CROSS)
    dhc = _matmul(dqc, wf["w_cq"], mode="nt", name="mm_cq_dx", tm=1024, tn=1024, tk=512)
    gb["w_ckv"] = _matmul(memh, dkv, mode="tn", name="mm_ckv_dw", tm=1024, tn=1024, tk=N_MEM,
                          out_dtypes=(BF,)).reshape(N_DEV, D // N_DEV, 2 * D_CROSS)
    dmemh = _matmul(dkv, wf["w_ckv"], mode="nt", name="mm_ckv_dx", tm=N_MEM, tn=1024, tk=1024)
    (gs["g_mem"],) = _rms_bwd_call(mem, p["g_mem"], dmemh, None, "rms_mem_bwd")
    dx1, dx1b, gs["g_cross"] = _rms_bwd_call(x1, p["g_cross"], dhc, dx2, "rms_cross_bwd")

    dmix = _matmul(dx1b, wf["w_out"], mode="nt", name="mm_out_dx", tm=1024, tn=1024, tk=D)
    gb["w_out"] = _matmul(mix, dx1b, mode="tn", name="mm_out_dw", tm=1024, tn=1024, tk=S,
                          out_dtypes=(BF,)).reshape(N_DEV, D // N_DEV, D)
    if hooks is not None:
        p["g_attn_out"] = _tie(p["g_attn_out"],
                               hooks.emit({n: gb[n] for n in ("w_co", "w_cq", "w_ckv", "w_out")}, dmix))

    dattn, delta, gs["g_attn_out"] = _attn_merge_bwd(dmix, attn, p["g_attn_out"])
    dq_rot, dk_rot, dv_attn = _attn_bwd(qr, kr, proj, dattn, lse, delta)
    dq_pre, dk_pre, dv_pre, dgq, dgk = _qk_bwd(dq_rot, dk_rot, dv_attn, proj, pos_col, gq128, gk128, inv_tab)
    gs["g_q"], gs["g_k"] = dgq[:, :HEAD], dgk[:, :HEAD]

    dy_ssd, dz, dxs_skip, dd_lane, gs["g_ssm_out"] = _ssd_post_bwd(y_ssd, xbc, proj, dmix, dskip_b, p["g_ssm_out"])
    gs["d_skip"] = dd_lane.reshape(N_HEADS, HEAD).sum(axis=1).reshape(1, N_HEADS)
    dc_pair, daq_b, dx_ssd, db_pair, dak_r, ddt_r, ddt_b = _ssd_bwd(xbc, dy_ssd, h_all, a_b, dt_b, at_r, dt_r)
    dak = _pad_lanes(_row_layout_inv(dak_r))
    ddt_direct = _pad_lanes(_row_layout_inv(ddt_r))
    ddt_raw, dalog, dbias = _ssd_prep_bwd(daq_b, dak, ddt_direct, ddt_b, dt, proj, dtb128, alog128)
    gs["a_log"], gs["dt_bias"] = dalog[:, :N_HEADS], dbias[:, :N_HEADS]
    same = lambda c: c
    dxbc_x, dcw_x, dcb_x = _conv_bwd(proj, conv_w, conv_b, dxs_skip, dx_ssd, same, same, 0, 8, "conv_bwd_x")
    dxbc_b, dcw_b, dcb_b = _conv_bwd(proj, conv_w, conv_b, db_pair, db_pair, lambda c: 2 * c, lambda c: 2 * c + 1,
                                     D_SSM, 4, "conv_bwd_b")
    dxbc_c, dcw_c, dcb_c = _conv_bwd(proj, conv_w, conv_b, dc_pair, dc_pair, lambda c: 2 * c, lambda c: 2 * c + 1,
                                     D_SSM + 512, 4, "conv_bwd_c")
    g_conv_w = jnp.concatenate([dcw_x, dcw_b, dcw_c], axis=1)
    gs["conv_b"] = jnp.concatenate([dcb_x, dcb_b, dcb_c], axis=1)

    pieces = [dq_pre, dk_pre, dv_pre, dz, dxbc_x, dxbc_b, dxbc_c, ddt_raw]
    pieces.append(jnp.zeros((S, D_INP - sum(t.shape[1] for t in pieces)), BF))
    dproj = jnp.concatenate(pieces, axis=1)
    dw_in = _matmul(h, dproj, mode="tn", name="mm_in_dw", tm=1024, tn=1280, tk=S, out_dtypes=(BF,))
    gb["w_in"] = _w_in_to_blocks(dw_in)
    token = None if hooks is None else hooks.emit({"w_in": gb["w_in"]}, dw_in)
    dh = _matmul(dproj, wf["w_in"], mode="nt", name="mm_in_dx", tm=1024, tn=512, tk=D_INP // 2, after=token)
    grad_x, _, gs["g_mix"] = _rms_bwd_call(x, p["g_mix"], dh, dx1, "rms_mix_bwd")
    return loss_part, grad_x, gb, gs, g_conv_w


def kernel(x, mem, positions, g_mix, w_in, g_q, g_k, g_attn_out, conv_w, conv_b, dt_bias, a_log, d_skip, g_ssm_out, w_out, g_cross, g_mem, w_cq, w_ckv, g_cq, g_ck, w_co, g_mlp, w_up, w_down, loss_target, m_g_mix, m_w_in, m_g_q, m_g_k, m_g_attn_out, m_conv_w, m_conv_b, m_dt_bias, m_a_log, m_d_skip, m_g_ssm_out, m_w_out, m_g_cross, m_g_mem, m_w_cq, m_w_ckv, m_g_cq, m_g_ck, m_w_co, m_g_mlp, m_w_up, m_w_down, v_g_mix, v_w_in, v_g_q, v_g_k, v_g_attn_out, v_conv_w, v_conv_b, v_dt_bias, v_a_log, v_d_skip, v_g_ssm_out, v_w_out, v_g_cross, v_g_mem, v_w_cq, v_w_ckv, v_g_cq, v_g_ck, v_w_co, v_g_mlp, v_w_up, v_w_down):
    args = dict(locals())
    w = {n: args[n] for n in WEIGHTS}
    m = {n: args["m_" + n] for n in WEIGHTS}
    v = {n: args["v_" + n] for n in WEIGHTS}
    small = {n: w[n] for n, _ in SMALL}
    me = 4 * lax.axis_index("x") + 2 * lax.axis_index("y") + lax.axis_index("c")

    w_in_g, conv_w_g = _all_gather([w["w_in"][0].astype(BF), w["conv_w"][0]], "ag_first")
    wf = {"w_in": _w_in_from_blocks(w_in_g), "conv_w": conv_w_g.transpose(1, 0, 2).reshape(CONV_W, D_CONV)}
    overlap = _Overlap({n: w[n][0].astype(BF) for n in REST}, w_in_g)
    p = dict(small)
    p["g_mix"] = _tie(p["g_mix"], overlap.token)

    loss_part, grad_x, _, gs, g_conv_w = _local_step(
        x[0], mem[0], positions.reshape(S, 1), loss_target[0], p, wf, overlap)
    loss = lax.psum(loss_part, ("x", "y", "c"))

    out = {}
    last = grad_x
    for gi in range(3):
        for n, parts in overlap.finish(gi, last).items():
            res_n = _adamw(w[n][0], m[n][0], v[n][0], parts, "adamw_" + n)
            out[n] = [t.reshape(w[n].shape) for t in res_n]
            last = res_n[0]

    sm_parts, cw_parts = _all_gather([_pack_small(gs), g_conv_w], "ag_small_grads", after=last)
    sm = [_unpack_small(t) for t in _adamw(_pack_small(small), _pack_small({n: m[n] for n, _ in SMALL}),
                                           _pack_small({n: v[n] for n, _ in SMALL}), sm_parts, "adamw_small")]
    for n, _ in SMALL:
        out[n] = [t[n] for t in sm]
    cols = D_CONV // N_DEV
    cw_mine = lax.dynamic_slice_in_dim(cw_parts, me * cols, cols, axis=2)
    out["conv_w"] = [t.reshape(w["conv_w"].shape) for t in
                     _adamw(w["conv_w"][0], m["conv_w"][0], v["conv_w"][0], cw_mine, "adamw_conv_w")]

    res = [loss, grad_x.reshape(x.shape)]
    for k in range(4):
        res += [out[n][k] for n in WEIGHTS]
    return tuple(res)
```

```python
import functools
import math

import numpy as np
import jax
import jax.numpy as jnp
from jax import lax
from jax.experimental import pallas as pl
from jax.experimental.pallas import tpu as pltpu

F32 = jnp.float32
BF = jnp.bfloat16

N_DEV = 8
S = 2048
D = 2048
D_ATTN = 1024
N_HEADS = 16
HEAD = 64
ROT = 16
ROPE_THETA = 500000.0
D_SSM = 1024
D_CONV = 2048
CONV_W = 4
N_MEM = 256
D_CROSS = 512
CROSS_HEAD = 128
D_FF = 8192
D_IN = 6160
D_INP = 6400
DT_COLBLK = (4 * D_ATTN + D_CONV) // 128
EPS = 1e-6
BLK = 128
NB = S // BLK
LANES = 128
NEG = -1e30

ADAM_LR = 0.001
ADAM_B1 = 0.9
ADAM_B2 = 0.999
ADAM_EPS = 1e-08
ADAM_WD = 0.01
ADAM_STEP = 10

VMEM_LIMIT_BYTES = 48 * 1024 * 1024
ROW_TILE = 256


def _params(*sem):
    return pltpu.CompilerParams(dimension_semantics=sem, vmem_limit_bytes=VMEM_LIMIT_BYTES)


def _matmul(a, b, *, mode, name, tm, tn, tk, out_dtypes=(F32,), b_cb=None, out_cb=None,
            extras=(), epilogue=None, after=None):
    if mode == "tn":
        kk, m = a.shape
    else:
        m, kk = a.shape
    if b_cb is None:
        br, bc = b.shape
    else:
        br, bc = b.shape[1], N_DEV * b_cb
    n = br if mode == "nt" else bc
    assert m % tm == 0 and n % tn == 0 and kk % tk == 0, (name, m, n, kk)
    nk = kk // tk
    grid = (m // tm, n // tn, nk)

    if mode == "tn":
        a_spec = pl.BlockSpec((tk, tm), lambda i, j, k: (k, i))
    else:
        a_spec = pl.BlockSpec((tm, tk), lambda i, j, k: (i, k))
    if mode == "nt":
        b_blk, b_idx = (tn, tk), (lambda i, j, k: (j, k))
    else:
        b_blk, b_idx = (tk, tn), (lambda i, j, k: (k, j))
    group = 1
    if b_cb is None:
        b_spec = pl.BlockSpec(b_blk, b_idx)
    elif mode == "nt" and tk > b_cb:
        assert tk % b_cb == 0
        group = tk // b_cb
        b_spec = pl.BlockSpec((group, tn, b_cb), lambda i, j, k: (k, j, 0))
    else:
        tc = b_blk[1]
        assert b_cb % tc == 0
        per = b_cb // tc

        def b_idx3(i, j, k):
            r, c = b_idx(i, j, k)
            return (c // per, r, c % per)
        b_spec = pl.BlockSpec((None,) + b_blk, b_idx3)
    ex_specs = [pl.BlockSpec((tm, tn), lambda i, j, k: (i, j)) for _ in extras]
    if out_cb is None:
        out_shape = [jax.ShapeDtypeStruct((m, n), dt) for dt in out_dtypes]
        out_specs = [pl.BlockSpec((tm, tn), lambda i, j, k: (i, j)) for _ in out_dtypes]
    else:
        assert out_cb % tn == 0
        pero = out_cb // tn
        out_shape = [jax.ShapeDtypeStruct((N_DEV, m, out_cb), dt) for dt in out_dtypes]
        out_specs = [pl.BlockSpec((None, tm, tn), lambda i, j, k: (j // pero, i, j % pero)) for _ in out_dtypes]
    dn = {"nn": (((1,), (0,)), ((), ())), "nt": (((1,), (1,)), ((), ())), "tn": (((0,), (0,)), ((), ()))}[mode]
    ne, no = len(extras), len(out_dtypes)
    n_after = 0 if after is None else 1

    def kern(a_ref, b_ref, *rest):
        ex_refs, out_refs = rest[:ne], rest[ne + n_after:ne + n_after + no]

        def finish(acc):
            outs = (acc,) if epilogue is None else epilogue(acc, *[r[...] for r in ex_refs])
            for r, o in zip(out_refs, outs):
                r[...] = o.astype(r.dtype)

        if group == 1:
            part = lax.dot_general(a_ref[...].astype(BF), b_ref[...].astype(BF), dn, preferred_element_type=F32)
        else:
            part = sum(lax.dot_general(a_ref[:, g * b_cb:(g + 1) * b_cb].astype(BF), b_ref[g].astype(BF), dn,
                                       preferred_element_type=F32) for g in range(group))
        if nk == 1:
            finish(part)
            return
        acc_ref = rest[ne + n_after + no]
        k = pl.program_id(2)

        @pl.when(k == 0)
        def _():
            acc_ref[...] = part

        @pl.when(k > 0)
        def _():
            acc_ref[...] += part

        @pl.when(k == nk - 1)
        def _():
            finish(acc_ref[...])

    res = pl.pallas_call(
        kern, name=name, grid=grid, in_specs=[a_spec, b_spec] + ex_specs + [ANY] * n_after, out_specs=out_specs,
        out_shape=out_shape, scratch_shapes=[pltpu.VMEM((tm, tn), F32)] if nk > 1 else [],
        compiler_params=_params("parallel", "parallel", "arbitrary"),
    )(a, b, *extras, *([] if after is None else [after]))
    return res[0] if no == 1 else tuple(res)


def _rowwise(body, *, name, nrows, tile, row_ins, full_ins=(), row_outs=(), acc_outs=(), scratch=(),
             reverse=False):
    n = nrows // tile

    def ridx(i):
        return (n - 1 - i) if reverse else i

    in_specs, args = [], []
    for arr, width, cb in row_ins:
        in_specs.append(pl.BlockSpec((tile, width), lambda i, cb=cb: (ridx(i), cb)))
        args.append(arr)
    for arr in full_ins:
        in_specs.append(pl.BlockSpec(arr.shape, lambda i, nd=arr.ndim: (0,) * nd))
        args.append(arr)
    out_shape, out_specs = [], []
    for width, dt in row_outs:
        out_shape.append(jax.ShapeDtypeStruct((nrows, width), dt))
        out_specs.append(pl.BlockSpec((tile, width), lambda i: (ridx(i), 0)))
    for shp, dt in acc_outs:
        out_shape.append(jax.ShapeDtypeStruct(shp, dt))
        out_specs.append(pl.BlockSpec(shp, lambda i, nd=len(shp): (0,) * nd))

    def kern(*refs):
        body(pl.program_id(0), n, *refs)

    return pl.pallas_call(kern, name=name, grid=(n,), in_specs=in_specs, out_specs=out_specs,
                          out_shape=out_shape, scratch_shapes=list(scratch),
                          compiler_params=_params("arbitrary"))(*args)


def _accum(ref, val, i):
    @pl.when(i == 0)
    def _():
        ref[...] = val

    @pl.when(i > 0)
    def _():
        ref[...] += val


def _sigmoid(x):
    return 1.0 / (1.0 + jnp.exp(-x))


def _rms_n(x):
    r = lax.rsqrt(jnp.mean(x * x, axis=-1, keepdims=True) + EPS)
    return x * r, r


def _rms_bwd(x, g, dh):
    n, r = _rms_n(x)
    dn = dh * g
    dx = r * (dn - n * jnp.mean(dn * n, axis=-1, keepdims=True))
    return dx, jnp.sum(dh * n, axis=0, keepdims=True)


def _seg_sum(x, seg):
    t, w = x.shape
    tiles = [x[:, LANES * j:LANES * (j + 1)] for j in range(w // LANES)]
    outs = []
    if seg == 64:
        lo = lax.broadcasted_iota(jnp.int32, (t, LANES), 1) < 64
        for xt in tiles:
            s_lo = jnp.sum(jnp.where(lo, xt, 0.0), axis=-1, keepdims=True)
            s_hi = jnp.sum(jnp.where(lo, 0.0, xt), axis=-1, keepdims=True)
            outs.append(jnp.where(lo, s_lo, s_hi))
    else:
        sums = [jnp.sum(xt, axis=-1, keepdims=True) for xt in tiles]
        if seg == 256:
            sums = [sums[2 * (j // 2)] + sums[2 * (j // 2) + 1] for j in range(len(sums))]
        else:
            assert seg == 128
        outs = [jnp.broadcast_to(s, (t, LANES)) for s in sums]
    return outs[0] if len(outs) == 1 else jnp.concatenate(outs, axis=1)


def _seg_rms_n(x, seg):
    r = lax.rsqrt(_seg_sum(x * x, seg) * (1.0 / seg) + EPS)
    return x * r, r


def _seg_rms_bwd(x, g, dy, seg):
    n, r = _seg_rms_n(x, seg)
    dn = dy * g
    dx = r * (dn - n * (_seg_sum(dn * n, seg) * (1.0 / seg)))
    return dx, jnp.sum(dy * n, axis=0, keepdims=True)


def _rope_tables(pos_col, inv_tab, t):
    ang = pos_col.astype(F32) * inv_tab
    idx = lax.broadcasted_iota(jnp.int32, (t, LANES), 1) % HEAD
    cos, sin = jnp.cos(ang), jnp.sin(ang)
    c = jnp.where(idx < ROT, cos, 1.0)
    sg = jnp.where(idx < ROT // 2, -sin, jnp.where(idx < ROT, sin, 0.0))
    return c, sg, idx < ROT // 2


def _rope(x, c, sg, lo):
    partner = jnp.where(lo, pltpu.roll(x, LANES - ROT // 2, 1), pltpu.roll(x, ROT // 2, 1))
    return x * c + partner * sg


def _fold_heads(v):
    v8 = jnp.broadcast_to(v, (8, LANES))
    return (v8 + pltpu.roll(v8, HEAD, 1))[0:1]


def _dot(a, b, dn):
    return lax.dot_general(a, b, (dn, ((), ())), preferred_element_type=F32)


NN = ((1,), (0,))
NT = ((1,), (1,))
TN = ((0,), (0,))


def _dot3(l01, x):
    x1 = x.astype(BF)
    r1 = x - x1.astype(F32)
    x2 = r1.astype(BF)
    x3 = (r1 - x2.astype(F32)).astype(BF)
    return _dot(l01, x1, NN) + _dot(l01, x2, NN) + _dot(l01, x3, NN)


def _rms_fwd(x, g, name):
    rows = x.shape[0]

    def body(i, n, x_ref, g_ref, o_ref):
        nx, _ = _rms_n(x_ref[...])
        o_ref[...] = (nx * g_ref[...]).astype(BF)

    return _rowwise(body, name=name, nrows=rows, tile=min(ROW_TILE, rows), row_ins=[(x, D, 0)],
                    full_ins=[g], row_outs=[(D, BF)])[0]


def _qk_prep(proj, pos_col, gq128, gk128, inv_tab):
    scale = HEAD ** -0.5

    def body(i, n, q_ref, k_ref, p_ref, gq_ref, gk_ref, it_ref, qo_ref, ko_ref):
        t = q_ref.shape[0]
        c, sg, lo = _rope_tables(p_ref[...], it_ref[...], t)
        for j in range(D_ATTN // LANES):
            sl = slice(LANES * j, LANES * (j + 1))
            qn, _ = _seg_rms_n(q_ref[:, sl], HEAD)
            kn, _ = _seg_rms_n(k_ref[:, sl], HEAD)
            qo_ref[:, sl] = _rope(qn * gq_ref[...], c, sg, lo) * scale
            ko_ref[:, sl] = _rope(kn * gk_ref[...], c, sg, lo)

    return _rowwise(body, name="qk_prep", nrows=S, tile=ROW_TILE,
                    row_ins=[(proj, D_ATTN, 0), (proj, D_ATTN, 1), (pos_col, 1, 0)],
                    full_ins=[gq128, gk128, inv_tab], row_outs=[(D_ATTN, F32)] * 2)


ATTN_QB = 16
V_COLS = pl.BlockSpec((S, LANES), lambda p: (0, 2 * D_ATTN // LANES + p))


def _band(b, d):
    nb = NB // d
    r, n = b // nb, b % nb
    width, kn = (BLK, n) if nb == 1 else (2 * BLK, jnp.maximum(n - 1, 0))

    def rows(first_member, count):
        if d == 1:
            return pl.ds(pl.multiple_of(first_member, BLK), count)
        return pl.ds(first_member * d + r, count, stride=d)

    qm = n * BLK + (lax.broadcasted_iota(jnp.int32, (2 * BLK, width), 0) & (BLK - 1))
    km = kn * BLK + lax.broadcasted_iota(jnp.int32, (2 * BLK, width), 1)
    valid = km <= qm
    if nb > 1:
        valid = valid & (km >= qm - BLK)
    return rows(n * BLK, BLK), rows(kn * BLK, width), valid


DILATIONS = (1, 4, 16)


def _attn_fwd(q, k, v):
    def kern(q_ref, k_ref, v_ref, a_ref, lse_ref, *scr):
        half0 = lax.broadcasted_iota(jnp.int32, (BLK, LANES), 1) < HEAD
        for gi, d in enumerate(DILATIONS):
            o_ref, l_ref = scr[2 * gi], scr[2 * gi + 1]

            def group(g, carry, d=d, o_ref=o_ref, l_ref=l_ref):
                for bb in range(ATTN_QB):
                    q_rows, k_rows, valid = _band(g * ATTN_QB + bb, d)
                    q = q_ref[q_rows, :]
                    kb, vb = k_ref[k_rows, :].astype(BF), v_ref[k_rows, :].astype(BF)
                    q2 = jnp.concatenate([jnp.where(half0, q, 0.0), jnp.where(half0, 0.0, q)], axis=0).astype(BF)
                    s = jnp.where(valid, _dot(q2, kb, NT), NEG)
                    m = jnp.max(s, axis=-1, keepdims=True)
                    p = jnp.exp(s - m)
                    den = jnp.sum(p, axis=-1, keepdims=True)
                    o2 = _dot(p.astype(BF), vb, NN) / den
                    l2 = m + jnp.log(den)
                    o_ref[q_rows, :] = jnp.where(half0, o2[:BLK], o2[BLK:])
                    l_ref[q_rows, :] = jnp.where(half0, l2[:BLK], l2[BLK:])
                return carry

            lax.fori_loop(0, NB // ATTN_QB, group, 0)

        def merge(i, carry):
            rows = pl.ds(pl.multiple_of(i * ROW_TILE, ROW_TILE), ROW_TILE)
            la, lb, lc = scr[1][rows, :], scr[3][rows, :], scr[5][rows, :]
            m = jnp.maximum(jnp.maximum(la, lb), lc)
            ea, eb, ec = jnp.exp(la - m), jnp.exp(lb - m), jnp.exp(lc - m)
            den = ea + eb + ec
            a_ref[rows, :] = (ea * scr[0][rows, :] + eb * scr[2][rows, :] + ec * scr[4][rows, :]) / den
            lse_ref[rows, :] = m + jnp.log(den)
            return carry

        lax.fori_loop(0, S // ROW_TILE, merge, 0)

    seq = pl.BlockSpec((S, LANES), lambda p: (0, p))
    return pl.pallas_call(
        kern, name="attn_fwd", grid=(D_ATTN // LANES,), in_specs=[seq, seq, V_COLS], out_specs=[seq, seq],
        out_shape=[jax.ShapeDtypeStruct((S, D_ATTN), F32)] * 2,
        scratch_shapes=[pltpu.VMEM((S, LANES), F32)] * (2 * len(DILATIONS)),
        compiler_params=_params("parallel"))(q, k, v)


def _attn_norm(attn, g_attn):
    def body(i, n, a_ref, g_ref, an_ref):
        nx, _ = _rms_n(a_ref[...])
        an_ref[...] = (nx * g_ref[...]).astype(BF)

    return _rowwise(body, name="attn_norm", nrows=S, tile=ROW_TILE, row_ins=[(attn, D_ATTN, 0)],
                    full_ins=[g_attn], row_outs=[(D_ATTN, BF)])[0]


def _conv_taps(x, w_ref):
    rows = lax.broadcasted_iota(jnp.int32, x.shape, 0)
    shifted = []
    for w in range(CONV_W):
        k = CONV_W - 1 - w
        shifted.append(x if k == 0 else jnp.where(rows >= k, pltpu.roll(x, k, 0), 0.0))
    acc = shifted[0] * w_ref[0:1, :]
    for w in range(1, CONV_W):
        acc = acc + shifted[w] * w_ref[w:w + 1, :]
    return acc, shifted


def _conv_fwd(proj, conv_w, conv_b):
    tc = 256

    def kern(x_ref, w_ref, b_ref, o_ref):
        c, _ = _conv_taps(x_ref[...], w_ref)
        c = c + b_ref[...]
        o_ref[...] = c * _sigmoid(c)

    return pl.pallas_call(
        kern, name="conv_fwd", grid=(D_CONV // tc,),
        in_specs=[pl.BlockSpec((S, tc), lambda c: (0, 4 * D_ATTN // tc + c)),
                  pl.BlockSpec((CONV_W, tc), lambda c: (0, c)), pl.BlockSpec((1, tc), lambda c: (0, c))],
        out_specs=pl.BlockSpec((S, tc), lambda c: (0, c)),
        out_shape=jax.ShapeDtypeStruct((S, D_CONV), F32), compiler_params=_params("parallel"))(proj, conv_w, conv_b)


def _softplus(x):
    return jnp.maximum(x, 0.0) + jnp.log1p(jnp.exp(-jnp.abs(x)))


def _dot3r(x, r01):
    x1 = x.astype(BF)
    r1 = x - x1.astype(F32)
    x2 = r1.astype(BF)
    x3 = (r1 - x2.astype(F32)).astype(BF)
    return _dot(x1, r01, NN) + _dot(x2, r01, NN) + _dot(x3, r01, NN)


def _spread_heads(v):
    sel = (lax.broadcasted_iota(jnp.int32, (LANES, D_SSM), 1) // HEAD
           == lax.broadcasted_iota(jnp.int32, (LANES, D_SSM), 0))
    return _dot3r(v, sel.astype(BF))


def _collect_heads(v):
    sel = (lax.broadcasted_iota(jnp.int32, (D_SSM, LANES), 0)
           == HEAD * lax.broadcasted_iota(jnp.int32, (D_SSM, LANES), 1))
    return _dot3r(v, sel.astype(BF))


def _ssd_prep(proj, dt_bias128, a_log128):
    def body(i, n, raw_ref, b_ref, al_ref, dt_ref, a_ref, dtb_ref, ab_ref, carry):
        @pl.when(i == 0)
        def _():
            carry[...] = jnp.zeros_like(carry)

        dt = _softplus(raw_ref[...] + b_ref[...])
        da = dt * (-jnp.exp(al_ref[...]))
        tri = (lax.broadcasted_iota(jnp.int32, (BLK, BLK), 0) >= lax.broadcasted_iota(jnp.int32, (BLK, BLK), 1))
        cs = _dot3(tri.astype(BF), da) + carry[0:1, :]
        dt_ref[...] = dt
        a_ref[...] = cs
        dtb_ref[...] = _spread_heads(dt)
        ab_ref[...] = _spread_heads(cs)
        carry[...] = jnp.broadcast_to(cs[BLK - 1:BLK, :], carry.shape)

    return _rowwise(body, name="ssd_prep", nrows=S, tile=BLK, row_ins=[(proj, LANES, DT_COLBLK)],
                    full_ins=[dt_bias128, a_log128],
                    row_outs=[(LANES, F32), (LANES, F32), (D_SSM, F32), (D_SSM, F32)],
                    scratch=[pltpu.VMEM((8, LANES), F32)])


def _ssd_decay(a_col, at_row, causal):
    diff = jnp.where(causal, a_col - at_row, 0.0)
    return jnp.where(causal, jnp.exp(diff), 0.0)


SSD_CB = 16


def _ssd_fwd(xbc, a_b, dt_b, at_r, dt_r):
    def kern(c_ref, b_ref, x_ref, ab_ref, dtb_ref, at_ref, dt_ref, y_ref, hall_ref):
        half0 = lax.broadcasted_iota(jnp.int32, (BLK, LANES), 1) < HEAD
        causal = lax.broadcasted_iota(jnp.int32, (BLK, BLK), 1) <= lax.broadcasted_iota(jnp.int32, (BLK, BLK), 0)

        def step(i, carry):
            h, a_prev = carry
            for cc in range(SSD_CB):
                chunk = i * SSD_CB + cc
                rows = pl.ds(pl.multiple_of(chunk * BLK, BLK), BLK)
                ci, bi, x = c_ref[rows, :].astype(BF), b_ref[rows, :].astype(BF), x_ref[rows, :]
                ab = ab_ref[rows, :]
                a_end = ab[BLK - 1:BLK, :]
                alpha = jnp.exp(ab - a_prev)
                beta = jnp.exp(a_end - ab) * dtb_ref[rows, :]
                hall_ref[rows, :] = h
                cb = _dot(ci, bi, NT)
                at, dtj = at_ref[chunk], dt_ref[chunk]
                y = alpha * _dot(ci, h.astype(BF), NN)
                for hd in range(2):
                    hm = half0 if hd == 0 else jnp.logical_not(half0)
                    w = cb * _ssd_decay(ab[:, HEAD * hd:HEAD * hd + 1], at[hd:hd + 1, :], causal) * dtj[hd:hd + 1, :]
                    y = y + _dot(w.astype(BF), jnp.where(hm, x, 0.0).astype(BF), NN)
                y_ref[rows, :] = y
                h = alpha[BLK - 1:BLK, :] * h + _dot(bi, (beta * x).astype(BF), TN)
                a_prev = a_end
            return h, a_prev

        lax.fori_loop(0, NB // SSD_CB, step, (jnp.zeros((BLK, LANES), F32), jnp.zeros((1, LANES), F32)))

    npair = D_SSM // LANES
    rowvec = pl.BlockSpec((None, NB, 2, BLK), lambda p: (p, 0, 0, 0))
    seq = pl.BlockSpec((S, LANES), lambda p: (0, p))
    return pl.pallas_call(
        kern, name="ssd_fwd", grid=(npair,),
        in_specs=[pl.BlockSpec((S, LANES), lambda p: (0, 12 + p // 2)),
                  pl.BlockSpec((S, LANES), lambda p: (0, 8 + p // 2)), seq, seq, seq, rowvec, rowvec],
        out_specs=[seq, seq], out_shape=[jax.ShapeDtypeStruct((S, D_SSM), F32)] * 2,
        compiler_params=_params("parallel"))(xbc, xbc, xbc, a_b, dt_b, at_r, dt_r)


def _ssd_post(y_ssd, xbc, proj, dskip_b, g_ssm):
    def body(i, n, y_ref, xs_ref, z_ref, d_ref, g_ref, o_ref):
        z = z_ref[...]
        y2 = (y_ref[...] + d_ref[...] * xs_ref[...]) * (z * _sigmoid(z))
        nx, _ = _seg_rms_n(y2, 256)
        o_ref[...] = (nx * g_ref[...]).astype(BF)

    return _rowwise(body, name="ssd_post", nrows=S, tile=ROW_TILE,
                    row_ins=[(y_ssd, D_SSM, 0), (xbc, D_SSM, 0), (proj, D_SSM, 3)], full_ins=[dskip_b, g_ssm],
                    row_outs=[(D_SSM, BF)])[0]


def _cross_heads(q, kv_ref, gq, gk):
    out = []
    for h in range(D_CROSS // CROSS_HEAD):
        sl = slice(CROSS_HEAD * h, CROSS_HEAD * (h + 1))
        nq, rq = _rms_n(q[:, sl])
        nk, rk = _rms_n(kv_ref[:, sl])
        v = kv_ref[:, D_CROSS + CROSS_HEAD * h:D_CROSS + CROSS_HEAD * (h + 1)]
        out.append((sl, nq, rq, nk, rk, v))
    return out


def _cross_fwd(qc, kv, g_cq, g_ck):
    scale = CROSS_HEAD ** -0.5

    def body(i, n, q_ref, kv_ref, gq_ref, gk_ref, o_ref):
        for sl, nq, _, nk, _, v in _cross_heads(q_ref[...], kv_ref, gq_ref[...], gk_ref[...]):
            qn = (nq * gq_ref[...] * scale).astype(BF)
            kn = (nk * gk_ref[...]).astype(BF)
            s = _dot(qn, kn, NT)
            e = jnp.exp(s - jnp.max(s, axis=-1, keepdims=True))
            p = e / jnp.sum(e, axis=-1, keepdims=True)
            o_ref[:, sl] = _dot(p.astype(BF), v.astype(BF), NN).astype(BF)

    return _rowwise(body, name="cross_fwd", nrows=S, tile=ROW_TILE, row_ins=[(qc, D_CROSS, 0)],
                    full_ins=[kv, g_cq, g_ck], row_outs=[(D_CROSS, BF)])[0]


def _rms_bwd_call(x, g, dh, dres, name, sumsq_res=False):
    rows = x.shape[0]
    has_res = dres is not None

    def body(i, n, *refs):
        if has_res:
            x_ref, dh_ref, dr_ref, g_ref, dx_ref, dxb_ref, dg_ref = refs[:7]
        else:
            x_ref, dh_ref, g_ref, dg_ref = refs
        dx, dg = _rms_bwd(x_ref[...], g_ref[...], dh_ref[...])
        if has_res:
            dr = dr_ref[...]
            dx = dx + dr
            dx_ref[...] = dx
            dxb_ref[...] = dx.astype(BF)
            if sumsq_res:
                _accum(refs[7], jnp.sum(jnp.sum(dr * dr, axis=0, keepdims=True), axis=1, keepdims=True), i)
        _accum(dg_ref, dg, i)

    row_ins = [(x, D, 0), (dh, D, 0)] + ([(dres, D, 0)] if has_res else [])
    return _rowwise(body, name=name, nrows=rows, tile=min(ROW_TILE, rows), row_ins=row_ins, full_ins=[g],
                    row_outs=[(D, F32), (D, BF)] if has_res else [],
                    acc_outs=[((1, D), F32)] + ([((1, 1), F32)] if sumsq_res else []))


def _cross_bwd(qc, doc, kv, g_cq, g_ck):
    scale = CROSS_HEAD ** -0.5

    def body(i, n, q_ref, do_ref, kv_ref, gq_ref, gk_ref, dq_ref, dkn_ref, dv_ref, dgq_ref):
        dgq = jnp.zeros((1, CROSS_HEAD), F32)
        dkn_parts, dv_parts = [], []
        for sl, nq, rq, nk, _, v in _cross_heads(q_ref[...], kv_ref, gq_ref[...], gk_ref[...]):
            qn = (nq * gq_ref[...] * scale).astype(BF)
            kn = (nk * gk_ref[...]).astype(BF)
            s = _dot(qn, kn, NT)
            e = jnp.exp(s - jnp.max(s, axis=-1, keepdims=True))
            p = e / jnp.sum(e, axis=-1, keepdims=True)
            do = do_ref[:, sl].astype(BF)
            dv_parts.append(_dot(p.astype(BF), do, TN))
            dp = _dot(do, v.astype(BF), NT)
            ds = (p * (dp - jnp.sum(dp * p, axis=-1, keepdims=True))).astype(BF)
            dqn = _dot(ds, kn, NN)
            dkn_parts.append(_dot(ds, qn, TN))
            dn = dqn * (gq_ref[...] * scale)
            dq_ref[:, sl] = (rq * (dn - nq * jnp.mean(dn * nq, axis=-1, keepdims=True))).astype(BF)
            dgq = dgq + jnp.sum(dqn * scale * nq, axis=0, keepdims=True)
        _accum(dkn_ref, jnp.concatenate(dkn_parts, axis=1), i)
        _accum(dv_ref, jnp.concatenate(dv_parts, axis=1), i)
        _accum(dgq_ref, dgq, i)

    return _rowwise(body, name="cross_bwd", nrows=S, tile=ROW_TILE, row_ins=[(qc, D_CROSS, 0), (doc, D_CROSS, 0)],
                    full_ins=[kv, g_cq, g_ck], row_outs=[(D_CROSS, BF)],
                    acc_outs=[((N_MEM, D_CROSS), F32), ((N_MEM, D_CROSS), F32), ((1, CROSS_HEAD), F32)])


def _cross_kv_bwd(kv, dkn, dv, g_ck):
    def body(i, n, kv_ref, dkn_ref, dv_ref, gk_ref, dkv_ref, dgk_ref):
        dgk = jnp.zeros((1, CROSS_HEAD), F32)
        for h in range(D_CROSS // CROSS_HEAD):
            sl = slice(CROSS_HEAD * h, CROSS_HEAD * (h + 1))
            dk, dg = _rms_bwd(kv_ref[:, sl], gk_ref[...], dkn_ref[:, sl])
            dkv_ref[:, sl] = dk.astype(BF)
            dgk = dgk + dg
        dkv_ref[:, D_CROSS:] = dv_ref[...].astype(BF)
        dgk_ref[...] = dgk

    return _rowwise(body, name="cross_kv_bwd", nrows=N_MEM, tile=N_MEM,
                    row_ins=[(kv, 2 * D_CROSS, 0), (dkn, D_CROSS, 0), (dv, D_CROSS, 0)], full_ins=[g_ck],
                    row_outs=[(2 * D_CROSS, BF)], acc_outs=[((1, CROSS_HEAD), F32)])


def _attn_merge_bwd(dmix, attn, g_attn):
    def body(i, n, dn_ref, a_ref, g_ref, da_ref, dl_ref, dg_ref):
        attn_v = a_ref[...]
        da, dg = _rms_bwd(attn_v, g_ref[...], dn_ref[...])
        da_ref[...] = da
        dl_ref[...] = _seg_sum(da * attn_v, HEAD)
        _accum(dg_ref, dg, i)

    return _rowwise(body, name="attn_merge_bwd", nrows=S, tile=ROW_TILE,
                    row_ins=[(dmix, D_ATTN, 0), (attn, D_ATTN, 0)], full_ins=[g_attn],
                    row_outs=[(D_ATTN, F32), (D_ATTN, F32)], acc_outs=[((1, D_ATTN), F32)])


def _attn_bwd(q, k, v, do, lse, delta):
    def kern(q_ref, k_ref, v_ref, do_ref, l_ref, d_ref, dq_ref, dk_ref, dv_ref):
        dk_ref[...] = jnp.zeros_like(dk_ref)
        dv_ref[...] = jnp.zeros_like(dv_ref)
        half0 = lax.broadcasted_iota(jnp.int32, (BLK, LANES), 1) < HEAD
        for gi, d in enumerate(DILATIONS):

            def group(g, carry, d=d, first=(gi == 0)):
                updates = []
                for bb in range(ATTN_QB):
                    q_rows, k_rows, valid = _band(g * ATTN_QB + bb, d)
                    q, do = q_ref[q_rows, :], do_ref[q_rows, :]
                    lse_t, del_t = l_ref[q_rows, :], d_ref[q_rows, :]
                    kb, vb = k_ref[k_rows, :].astype(BF), v_ref[k_rows, :].astype(BF)
                    q2 = jnp.concatenate([jnp.where(half0, q, 0.0), jnp.where(half0, 0.0, q)], axis=0).astype(BF)
                    do2 = jnp.concatenate([jnp.where(half0, do, 0.0), jnp.where(half0, 0.0, do)], axis=0).astype(BF)
                    lse2 = jnp.concatenate([lse_t[:, 0:1], lse_t[:, HEAD:HEAD + 1]], axis=0)
                    del2 = jnp.concatenate([del_t[:, 0:1], del_t[:, HEAD:HEAD + 1]], axis=0)
                    s = jnp.where(valid, _dot(q2, kb, NT), NEG)
                    p = jnp.exp(s - lse2)
                    ds = (p * (_dot(do2, vb, NT) - del2)).astype(BF)
                    dq2 = _dot(ds, kb, NN)
                    dq = jnp.where(half0, dq2[:BLK], dq2[BLK:])
                    if first:
                        dq_ref[q_rows, :] = dq
                    else:
                        dq_ref[q_rows, :] += dq
                    updates.append((k_rows, _dot(ds, q2, TN), _dot(p.astype(BF), do2, TN)))
                for k_rows, dk, dv in updates:
                    dk_ref[k_rows, :] += dk
                    dv_ref[k_rows, :] += dv
                return carry

            lax.fori_loop(0, NB // ATTN_QB, group, 0)

    seq = pl.BlockSpec((S, LANES), lambda p: (0, p))
    return pl.pallas_call(
        kern, name="attn_bwd", grid=(D_ATTN // LANES,), in_specs=[seq, seq, V_COLS, seq, seq, seq],
        out_specs=[seq, seq, seq], out_shape=[jax.ShapeDtypeStruct((S, D_ATTN), F32)] * 3,
        compiler_params=_params("parallel"))(q, k, v, do, lse, delta)


def _qk_bwd(dq_rot, dk_rot, dv, proj, pos_col, gq128, gk128, inv_tab):
    scale = HEAD ** -0.5

    def body(i, n, dq_ref, dk_ref, dv_ref, q_ref, k_ref, p_ref, gq_ref, gk_ref, it_ref,
             dqo_ref, dko_ref, dvo_ref, dgq_ref, dgk_ref):
        t = q_ref.shape[0]
        c, sg, lo = _rope_tables(p_ref[...], it_ref[...], t)
        dgq = jnp.zeros((1, LANES), F32)
        dgk = jnp.zeros((1, LANES), F32)
        for j in range(D_ATTN // LANES):
            sl = slice(LANES * j, LANES * (j + 1))
            dqr = _rope(dq_ref[:, sl], c, -sg, lo) * scale
            dkr = _rope(dk_ref[:, sl], c, -sg, lo)
            dq, gq = _seg_rms_bwd(q_ref[:, sl], gq_ref[...], dqr, HEAD)
            dk, gk = _seg_rms_bwd(k_ref[:, sl], gk_ref[...], dkr, HEAD)
            dqo_ref[:, sl] = dq.astype(BF)
            dko_ref[:, sl] = dk.astype(BF)
            dgq, dgk = dgq + gq, dgk + gk
        dvo_ref[...] = dv_ref[...].astype(BF)
        _accum(dgq_ref, _fold_heads(dgq), i)
        _accum(dgk_ref, _fold_heads(dgk), i)

    return _rowwise(body, name="qk_bwd", nrows=S, tile=ROW_TILE,
                    row_ins=[(a, D_ATTN, 0) for a in (dq_rot, dk_rot, dv)]
                    + [(proj, D_ATTN, 0), (proj, D_ATTN, 1), (pos_col, 1, 0)],
                    full_ins=[gq128, gk128, inv_tab], row_outs=[(D_ATTN, BF)] * 3,
                    acc_outs=[((1, LANES), F32)] * 2)


def _ssd_post_bwd(y_ssd, xbc, proj, dmix, dskip_b, g_ssm):
    def body(i, n, y_ref, xs_ref, z_ref, do_ref, d_ref, g_ref, dy_ref, dz_ref, dxs_ref, dd_ref, dg_ref):
        z, xs = z_ref[...], xs_ref[...]
        sg = _sigmoid(z)
        gate = z * sg
        y = y_ref[...] + d_ref[...] * xs
        dy2, dg = _seg_rms_bwd(y * gate, g_ref[...], do_ref[...], 256)
        dy = dy2 * gate
        dy_ref[...] = dy.astype(BF)
        dz_ref[...] = (dy2 * y * (sg * (1.0 + z * (1.0 - sg)))).astype(BF)
        dxs_ref[...] = dy * d_ref[...]
        _accum(dd_ref, jnp.sum(dy * xs, axis=0, keepdims=True), i)
        _accum(dg_ref, dg, i)

    return _rowwise(body, name="ssd_post_bwd", nrows=S, tile=ROW_TILE,
                    row_ins=[(y_ssd, D_SSM, 0), (xbc, D_SSM, 0), (proj, D_SSM, 3), (dmix, D_SSM, 1)],
                    full_ins=[dskip_b, g_ssm], row_outs=[(D_SSM, BF), (D_SSM, BF), (D_SSM, F32)],
                    acc_outs=[((1, D_SSM), F32), ((1, D_SSM), F32)])


def _ssd_bwd(xbc, dy, h_all, a_b, dt_b, at_r, dt_r):
    def kern(c_ref, b_ref, x_ref, dy_ref, hall_ref, ab_ref, dtb_ref, at_ref, dt_ref,
             dc_ref, daq_ref, dx_ref, db_ref, dak_ref, ddt_ref, ddtb_ref):
        half0 = lax.broadcasted_iota(jnp.int32, (BLK, LANES), 1) < HEAD
        hms = (half0, jnp.logical_not(half0))
        causal = lax.broadcasted_iota(jnp.int32, (BLK, BLK), 1) <= lax.broadcasted_iota(jnp.int32, (BLK, BLK), 0)
        row = lax.broadcasted_iota(jnp.int32, (BLK, LANES), 0)

        def step(t, carry_in):
            dh_next, carry = carry_in
            for cc in reversed(range(SSD_CB)):
                chunk = (NB // SSD_CB - 1 - t) * SSD_CB + cc
                rows = pl.ds(pl.multiple_of(chunk * BLK, BLK), BLK)
                ci, bi, x = c_ref[rows, :].astype(BF), b_ref[rows, :].astype(BF), x_ref[rows, :]
                dyv = dy_ref[rows, :].astype(F32)
                ab, dtb = ab_ref[rows, :], dtb_ref[rows, :]
                before = ab_ref[pl.ds(pl.multiple_of(jnp.maximum(chunk * BLK - 8, 0), 8), 8), :][7:8, :]
                a_prev = jnp.where(chunk == 0, 0.0, before)
                a_end = ab[BLK - 1:BLK, :]
                alpha = jnp.exp(ab - a_prev)
                e_end = jnp.exp(a_end - ab)
                beta = e_end * dtb
                t_all = alpha[BLK - 1:BLK, :]
                hc = hall_ref[rows, :]
                hcb, dhb = hc.astype(BF), dh_next.astype(BF)

                cb = _dot(ci, bi, NT)
                at, dtj = at_ref[chunk], dt_ref[chunk]
                dcb = jnp.zeros((BLK, BLK), F32)
                dx = jnp.zeros((BLK, LANES), F32)
                rsum = []
                for hd in range(2):
                    dym = jnp.where(hms[hd], dyv, 0.0).astype(BF)
                    lm = _ssd_decay(ab[:, HEAD * hd:HEAD * hd + 1], at[hd:hd + 1, :], causal)
                    dth = dtj[hd:hd + 1, :]
                    dw = _dot(dym, jnp.where(hms[hd], x, 0.0).astype(BF), NT)
                    g = dw * cb * lm
                    dx = dx + _dot((cb * lm * dth).astype(BF), dym, TN)
                    gcol = jnp.sum(g, axis=0, keepdims=True)
                    ddt_ref[chunk, hd:hd + 1, :] = gcol
                    dak_ref[chunk, hd:hd + 1, :] = gcol * dth
                    rsum.append(jnp.sum(g * dth, axis=1, keepdims=True))
                    dcb = dcb + dw * lm * dth
                dcb = dcb.astype(BF)

                g1 = (alpha * dyv).astype(BF)
                dalpha = dyv * _dot(ci, hcb, NN)
                g2 = _dot(bi, dhb, NN)
                dbeta = x * g2
                bx = (beta * x).astype(BF)
                dc_ref[rows, :] = _dot(dcb, bi, NN) + _dot(g1, hcb, NT)
                db_ref[rows, :] = _dot(dcb, ci, TN) + _dot(bx, dhb, NT)
                dx_ref[rows, :] = dx + beta * g2
                ddtb_ref[rows, :] = _seg_sum(e_end * dbeta, HEAD)
                ada, bdb = alpha * dalpha, beta * dbeta
                t_dt = t_all * jnp.sum(dh_next * hc, axis=0, keepdims=True)
                end_term = _seg_sum(jnp.broadcast_to(jnp.sum(bdb, axis=0, keepdims=True) + t_dt, (8, LANES)), HEAD)[0:1]
                prev_term = _seg_sum(jnp.broadcast_to(jnp.sum(ada, axis=0, keepdims=True) + t_dt, (8, LANES)), HEAD)[0:1]
                daq = jnp.where(half0, rsum[0], rsum[1]) + _seg_sum(ada - bdb, HEAD)
                daq_ref[rows, :] = daq + jnp.where(row == BLK - 1, end_term - carry, 0.0)
                carry = prev_term
                dh_next = t_all * dh_next + _dot(ci, g1, TN)
            return dh_next, carry

        lax.fori_loop(0, NB // SSD_CB, step, (jnp.zeros((BLK, LANES), F32), jnp.zeros((1, LANES), F32)))

    npair = D_SSM // LANES
    rowvec = pl.BlockSpec((None, NB, 2, BLK), lambda p: (p, 0, 0, 0))
    seq = pl.BlockSpec((S, LANES), lambda p: (0, p))
    return pl.pallas_call(
        kern, name="ssd_bwd", grid=(npair,),
        in_specs=[pl.BlockSpec((S, LANES), lambda p: (0, 12 + p // 2)),
                  pl.BlockSpec((S, LANES), lambda p: (0, 8 + p // 2)),
                  seq, seq, seq, seq, seq, rowvec, rowvec],
        out_specs=[seq, seq, seq, seq, rowvec, rowvec, seq],
        out_shape=[jax.ShapeDtypeStruct((S, D_SSM), F32)] * 4
        + [jax.ShapeDtypeStruct((npair, NB, 2, BLK), F32)] * 2 + [jax.ShapeDtypeStruct((S, D_SSM), F32)],
        compiler_params=_params("parallel"))(xbc, xbc, xbc, dy, h_all, a_b, dt_b, at_r, dt_r)


def _ssd_prep_bwd(daq, dak, ddt_row, ddt_lane, dt, proj, dt_bias128, a_log128):
    def body(i, n, daq_ref, dak_ref, dd_ref, dd2_ref, dt_ref, raw_ref, b_ref, al_ref, draw_ref, dal_ref, db_ref,
             carry):
        @pl.when(i == 0)
        def _():
            carry[...] = jnp.zeros_like(carry)

        a = -jnp.exp(al_ref[...])
        tri = (lax.broadcasted_iota(jnp.int32, (BLK, BLK), 0) <= lax.broadcasted_iota(jnp.int32, (BLK, BLK), 1))
        rev = _dot3(tri.astype(BF), _collect_heads(daq_ref[...]) - dak_ref[...]) + carry[0:1, :]
        carry[...] = jnp.broadcast_to(rev[0:1, :], carry.shape)
        dtv = dt_ref[...]
        draw = (dd_ref[...] + _collect_heads(dd2_ref[...]) + a * rev) * _sigmoid(raw_ref[...] + b_ref[...])
        draw_ref[...] = draw.astype(BF)
        _accum(dal_ref, jnp.sum(dtv * rev, axis=0, keepdims=True) * a, i)
        _accum(db_ref, jnp.sum(draw, axis=0, keepdims=True), i)

    return _rowwise(body, name="ssd_prep_bwd", nrows=S, tile=BLK, reverse=True,
                    row_ins=[(daq, D_SSM, 0), (dak, LANES, 0), (ddt_row, LANES, 0), (ddt_lane, D_SSM, 0), (dt, LANES, 0),
                             (proj, LANES, DT_COLBLK)],
                    full_ins=[dt_bias128, a_log128], row_outs=[(LANES, BF)],
                    acc_outs=[((1, LANES), F32), ((1, LANES), F32)], scratch=[pltpu.VMEM((8, LANES), F32)])


def _conv_bwd(proj, conv_w, conv_b, d1, d2, map1, map2, col0, ntiles, name):
    base = (4 * D_ATTN + col0) // LANES

    def kern(x_ref, w_ref, b_ref, d1_ref, d2_ref, dx_ref, dw_ref, db_ref):
        x = x_ref[...]
        c, shifted = _conv_taps(x, w_ref)
        c = c + b_ref[...]
        sg = _sigmoid(c)
        dc = (d1_ref[...] + d2_ref[...]) * (sg * (1.0 + c * (1.0 - sg)))
        rows = lax.broadcasted_iota(jnp.int32, x.shape, 0)
        dx = jnp.zeros_like(x)
        for w in range(CONV_W):
            k = CONV_W - 1 - w
            up = dc if k == 0 else jnp.where(rows < S - k, pltpu.roll(dc, S - k, 0), 0.0)
            dx = dx + up * w_ref[w:w + 1, :]
            dw_ref[w:w + 1, :] = jnp.sum(dc * shifted[w], axis=0, keepdims=True)
        dx_ref[...] = dx.astype(BF)
        db_ref[...] = jnp.sum(dc, axis=0, keepdims=True)

    c0 = col0 // LANES
    return pl.pallas_call(
        kern, name=name, grid=(ntiles,),
        in_specs=[pl.BlockSpec((S, LANES), lambda c: (0, base + c)),
                  pl.BlockSpec((CONV_W, LANES), lambda c: (0, c0 + c)),
                  pl.BlockSpec((1, LANES), lambda c: (0, c0 + c)),
                  pl.BlockSpec((S, LANES), lambda c: (0, map1(c))),
                  pl.BlockSpec((S, LANES), lambda c: (0, map2(c)))],
        out_specs=[pl.BlockSpec((S, LANES), lambda c: (0, c)), pl.BlockSpec((CONV_W, LANES), lambda c: (0, c)),
                   pl.BlockSpec((1, LANES), lambda c: (0, c))],
        out_shape=[jax.ShapeDtypeStruct((S, ntiles * LANES), BF), jax.ShapeDtypeStruct((CONV_W, ntiles * LANES), F32),
                   jax.ShapeDtypeStruct((1, ntiles * LANES), F32)],
        compiler_params=_params("parallel"))(proj, conv_w, conv_b, d1, d2)


def _adamw(w, m, v, parts, name):
    r, c = w.shape
    n_parts = parts.shape[0]
    tile = r if r <= 512 else (128 if c > 1024 else 256)
    assert r % tile == 0
    c1 = 1.0 - ADAM_B1 ** ADAM_STEP
    c2 = 1.0 - ADAM_B2 ** ADAM_STEP

    def kern(w_ref, m_ref, v_ref, p_ref, g_ref, d_ref, mo_ref, vo_ref):
        g = p_ref[0].astype(F32)
        for k in range(1, n_parts):
            g = g + p_ref[k].astype(F32)
        mn = ADAM_B1 * m_ref[...] + (1.0 - ADAM_B1) * g
        vn = ADAM_B2 * v_ref[...] + (1.0 - ADAM_B2) * (g * g)
        g_ref[...] = g
        mo_ref[...] = mn
        vo_ref[...] = vn
        d_ref[...] = -ADAM_LR * ((mn / c1) / (jnp.sqrt(vn / c2) + ADAM_EPS) + ADAM_WD * w_ref[...])

    blk = pl.BlockSpec((tile, c), lambda i: (i, 0))
    return pl.pallas_call(
        kern, name=name, grid=(r // tile,),
        in_specs=[blk, blk, blk, pl.BlockSpec((n_parts, tile, c), lambda i: (0, i, 0))],
        out_specs=[blk] * 4, out_shape=[jax.ShapeDtypeStruct((r, c), F32)] * 4,
        compiler_params=_params("parallel"))(w, m, v, parts)


MESH = pl.DeviceIdType.MESH
ANY = pl.BlockSpec(memory_space=pl.ANY)


def _put_own(gathered, shard):
    me = 4 * lax.axis_index("x") + 2 * lax.axis_index("y") + lax.axis_index("c")
    return lax.dynamic_update_slice(gathered, shard[None], (me,) + (0,) * shard.ndim)


def _all_gather(arrs, name, after=None):
    na = len(arrs)
    n_after = 0 if after is None else 1

    def kern(*refs):
        ins, outs = refs[:na], refs[na + n_after:2 * na + n_after]
        send_sems, recv_sems = refs[2 * na + n_after:]
        x, y, c = lax.axis_index("x"), lax.axis_index("y"), lax.axis_index("c")
        me, sibling = (x, y, c), (x, y, 1 - c)
        a_chip = (jnp.where(c == 0, 1 - x, x), jnp.where(c == 0, y, 1 - y))
        b_chip = (jnp.where(c == 0, x, 1 - x), jnp.where(c == 0, 1 - y, y))
        chips = [a_chip, b_chip, (1 - x, 1 - y)]

        def copy(a, k, block, to, src=None):
            px, py, pc = block
            dst = outs[a].at[4 * px + 2 * py + pc]
            return pltpu.make_async_remote_copy(
                src_ref=dst if src is None else src, dst_ref=dst, send_sem=send_sems.at[a, k],
                recv_sem=recv_sems.at[a, k], device_id=to, device_id_type=MESH)

        sends = []
        for a in range(na):
            sends.append(copy(a, 0, me, sibling, src=ins[a]))
            sends += [copy(a, 1 + j, me, (*chips[j], c), src=ins[a]) for j in range(2)]
        for cp in sends:
            cp.start()
        for a in range(na):
            for j, chip in enumerate(chips):
                copy(a, 1 + j, (*chip, c), me).wait_recv()
                later = [copy(a, 4 + j, (*chip, c), sibling)]
                if j == 0:
                    later.append(copy(a, 3, (*a_chip, c), (*b_chip, c)))
                for cp in later:
                    cp.start()
                sends += later
        for a in range(na):
            copy(a, 0, sibling, me).wait_recv()
            for j, chip in enumerate(chips):
                copy(a, 4 + j, (*chip, 1 - c), me).wait_recv()
        for cp in sends:
            cp.wait_send()

    outs = pl.pallas_call(
        kern, name=name, in_specs=[ANY] * (na + n_after), out_specs=[ANY] * na,
        out_shape=[jax.ShapeDtypeStruct((N_DEV,) + a.shape, a.dtype) for a in arrs],
        scratch_shapes=[pltpu.SemaphoreType.DMA((na, 7)), pltpu.SemaphoreType.DMA((na, 7))],
    )(*arrs, *([] if after is None else [after]))
    return [_put_own(o, a) for o, a in zip(outs, arrs)]


HBM = pl.BlockSpec(memory_space=pltpu.HBM)
SEM = pl.BlockSpec(memory_space=pltpu.SEMAPHORE)
EFFECT = pltpu.SideEffectType.DATAFLOW_SIDE_EFFECTING
N_CHIP = 4
N_COPIES = {"gather": 3, "scatter": 3, "forward": 4, "pair": 4}


def _in_hbm(a):
    return pltpu.with_memory_space_constraint(a, pltpu.HBM)


def _chip_copies(kind, src_refs, land_refs, send_sems, recv_sems):
    x, y, c = lax.axis_index("x"), lax.axis_index("y"), lax.axis_index("c")
    chips = [(1 - x, y), (x, 1 - y), (1 - x, 1 - y)]
    n = N_COPIES[kind]
    cps = []

    def add(a, j, src, dst, to):
        cps.append(pltpu.make_async_remote_copy(
            src_ref=src, dst_ref=dst, send_sem=send_sems.at[n * a + j], recv_sem=recv_sems.at[n * a + j],
            device_id=to, device_id_type=MESH))

    for a in range(len(src_refs)):
        if kind == "gather":
            for j, (px, py) in enumerate(chips):
                add(a, j, src_refs[a], land_refs[a].at[4 * x + 2 * y + c], (px, py, c))
        elif kind == "scatter":
            for j, (px, py) in enumerate(chips):
                add(a, j, src_refs[a].at[2 * px + py], land_refs[a].at[2 * x + y], (px, py, c))
        elif kind == "forward":
            add(a, 0, src_refs[a], land_refs[a].at[4 * x + 2 * y + c], (x, y, 1 - c))
            for j, (px, py) in enumerate(chips):
                slot = land_refs[a].at[4 * px + 2 * py + c]
                add(a, 1 + j, slot, slot, (x, y, 1 - c))
        else:
            for q in range(N_CHIP):
                add(a, q, src_refs[a].at[2 * q + 1 - c], land_refs[a].at[q], (x, y, 1 - c))
    return cps


def _exchange_start(kind, srcs, lands, after, name):
    na = len(srcs)
    n_after = 0 if after is None else 1

    def kern(*refs):
        src_refs, land_refs = refs[:na], refs[na:2 * na]
        send_sems, recv_sems = refs[2 * na + n_after], refs[2 * na + n_after + 1]
        token = refs[-1]
        for cp in _chip_copies(kind, src_refs, land_refs, send_sems, recv_sems):
            cp.start()
        token[...] = jnp.zeros_like(token)

    bufs = list(srcs) + list(lands)
    res = pl.pallas_call(
        kern, name=name,
        out_shape=(pltpu.SemaphoreType.DMA((N_COPIES[kind] * na,)), pltpu.SemaphoreType.DMA((N_COPIES[kind] * na,)))
        + tuple(pltpu.HBM(b.shape, b.dtype) for b in bufs) + (jax.ShapeDtypeStruct((8, LANES), F32),),
        in_specs=[HBM] * (2 * na) + [ANY] * n_after,
        out_specs=(SEM, SEM) + (HBM,) * (2 * na) + (pl.BlockSpec(memory_space=pltpu.VMEM),),
        input_output_aliases={i: 2 + i for i in range(2 * na)},
        compiler_params=pltpu.CompilerParams(has_side_effects=EFFECT),
    )(*[_in_hbm(b) for b in bufs], *([] if after is None else [after]))
    return (res[0], res[1]), list(res[2:2 + na]), list(res[2 + na:2 + 2 * na]), res[-1]


def _exchange_wait(kind, sems, srcs, lands, after, name):
    na = len(srcs)

    def kern(*refs):
        src_refs, land_refs = refs[:na], refs[na:2 * na]
        send_sems, recv_sems = refs[2 * na], refs[2 * na + 1]
        for cp in _chip_copies(kind, src_refs, land_refs, send_sems, recv_sems):
            cp.wait_send()
            cp.wait_recv()

    bufs = list(srcs) + list(lands)
    res = pl.pallas_call(
        kern, name=name, out_shape=tuple(pltpu.HBM(b.shape, b.dtype) for b in bufs),
        in_specs=[HBM] * (2 * na) + [SEM, SEM, ANY], out_specs=(HBM,) * (2 * na),
        input_output_aliases={i: i for i in range(2 * na)},
        compiler_params=pltpu.CompilerParams(has_side_effects=EFFECT),
    )(*bufs, sems[0], sems[1], after)
    return list(res[:na]), list(res[na:])


def _gather_finish(shards, lands, name):
    na = len(shards)

    def kern(*refs):
        ins, outs = refs[:na], refs[2 * na:3 * na]
        send_sems, recv_sems = refs[3 * na:]
        x, y, c = lax.axis_index("x"), lax.axis_index("y"), lax.axis_index("c")
        chips = [(1 - x, y), (x, 1 - y), (1 - x, 1 - y)]

        def copy(a, k, slot, pc, src=None):
            dst = outs[a].at[slot + pc]
            return pltpu.make_async_remote_copy(
                src_ref=dst if src is None else src, dst_ref=dst, send_sem=send_sems.at[a, k],
                recv_sem=recv_sems.at[a, k], device_id=(x, y, 1 - c), device_id_type=MESH)

        sends = []
        for a in range(na):
            sends.append(copy(a, 0, 4 * x + 2 * y, c, src=ins[a]))
            sends += [copy(a, 1 + j, 4 * px + 2 * py, c) for j, (px, py) in enumerate(chips)]
        for cp in sends:
            cp.start()
        for a in range(na):
            copy(a, 0, 4 * x + 2 * y, 1 - c).wait_recv()
            for j, (px, py) in enumerate(chips):
                copy(a, 1 + j, 4 * px + 2 * py, 1 - c).wait_recv()
        for cp in sends:
            cp.wait_send()

    return pl.pallas_call(
        kern, name=name, in_specs=[ANY] * (2 * na), out_specs=[ANY] * na,
        out_shape=[jax.ShapeDtypeStruct(l.shape, l.dtype) for l in lands],
        input_output_aliases={na + a: a for a in range(na)},
        scratch_shapes=[pltpu.SemaphoreType.DMA((na, 4)), pltpu.SemaphoreType.DMA((na, 4))])(*shards, *lands)


def _pair_exchange(parts, name):
    na = len(parts)

    def kern(*refs):
        ins, got = refs[:na], refs[na:2 * na]
        send_sems, recv_sems = refs[2 * na:]
        x, y, c = lax.axis_index("x"), lax.axis_index("y"), lax.axis_index("c")
        sends = []
        for a in range(na):
            for q in range(N_CHIP):
                sends.append(pltpu.make_async_remote_copy(
                    src_ref=ins[a].at[2 * q + 1 - c], dst_ref=got[a].at[q], send_sem=send_sems.at[a, q],
                    recv_sem=recv_sems.at[a, q], device_id=(x, y, 1 - c), device_id_type=MESH))
        for cp in sends:
            cp.start()
        for cp in sends:
            cp.wait()

    return pl.pallas_call(
        kern, name=name, in_specs=[ANY] * na, out_specs=[ANY] * na,
        out_shape=[jax.ShapeDtypeStruct((N_CHIP,) + p.shape[1:], p.dtype) for p in parts],
        scratch_shapes=[pltpu.SemaphoreType.DMA((na, N_CHIP)), pltpu.SemaphoreType.DMA((na, N_CHIP))])(*parts)


def _pair_sum(parts, got, name):
    _, r, cdim = parts.shape
    tile = min(r, ROW_TILE)
    x, y, c = lax.axis_index("x"), lax.axis_index("y"), lax.axis_index("c")
    where = jnp.stack([c, 2 * x + y]).astype(jnp.int32)

    def kern(w_ref, a_ref, b_ref, q_ref, own_ref):
        s = (a_ref[...].astype(F32) + b_ref[...].astype(F32)).astype(BF)
        q_ref[...] = s

        @pl.when(pl.program_id(1) == w_ref[1])
        def _():
            own_ref[...] = s

    blk = pl.BlockSpec((None, tile, cdim), lambda i, q, w_ref: (q, i, 0))
    sums, own = pl.pallas_call(
        kern, name=name,
        out_shape=[jax.ShapeDtypeStruct((N_CHIP, r, cdim), BF), jax.ShapeDtypeStruct((r, cdim), BF)],
        grid_spec=pltpu.PrefetchScalarGridSpec(
            num_scalar_prefetch=1, grid=(r // tile, N_CHIP),
            in_specs=[pl.BlockSpec((None, tile, cdim), lambda i, q, w_ref: (2 * q + w_ref[0], i, 0)), blk],
            out_specs=[blk, pl.BlockSpec((tile, cdim), lambda i, q, w_ref: (i, 0))]),
        compiler_params=_params("parallel", "arbitrary"))(where, parts, got)
    land = lax.dynamic_update_slice(lax.empty((N_CHIP, r, cdim), BF), own[None], (2 * x + y, 0, 0))
    return sums, land


W_IN_COLS = D_IN // N_DEV


def _w_in_from_blocks(w8):
    def kern(x_ref, o_ref):
        for j in range(N_DEV):
            o_ref[:, W_IN_COLS * j:W_IN_COLS * (j + 1)] = x_ref[j]
        o_ref[:, D_IN:] = jnp.zeros((ROW_TILE, D_INP - D_IN), o_ref.dtype)

    return pl.pallas_call(
        kern, name="w_in_from_blocks", grid=(D // ROW_TILE,),
        in_specs=[pl.BlockSpec((N_DEV, ROW_TILE, W_IN_COLS), lambda i: (0, i, 0))],
        out_specs=pl.BlockSpec((ROW_TILE, D_INP), lambda i: (i, 0)),
        out_shape=jax.ShapeDtypeStruct((D, D_INP), w8.dtype), compiler_params=_params("parallel"))(w8)


def _w_in_to_blocks(g):
    def kern(x_ref, o_ref):
        for j in range(N_DEV):
            o_ref[j] = x_ref[:, W_IN_COLS * j:W_IN_COLS * (j + 1)]

    return pl.pallas_call(
        kern, name="w_in_to_blocks", grid=(D // ROW_TILE,),
        in_specs=[pl.BlockSpec((ROW_TILE, D_INP), lambda i: (i, 0))],
        out_specs=pl.BlockSpec((N_DEV, ROW_TILE, W_IN_COLS), lambda i: (0, i, 0)),
        out_shape=jax.ShapeDtypeStruct((N_DEV, D, W_IN_COLS), g.dtype), compiler_params=_params("parallel"))(g)


def _pad_lanes(v, width=LANES):
    return jnp.pad(v, ((0, 0), (0, width - v.shape[1])))


def _row_layout(t16):
    return t16.T.reshape(N_HEADS // 2, 2, NB, BLK).transpose(0, 2, 1, 3)


def _row_layout_inv(r):
    return r.transpose(0, 2, 1, 3).reshape(N_HEADS, S).T


SMALL = (("g_mix", D), ("g_q", HEAD), ("g_k", HEAD), ("g_attn_out", D_ATTN), ("conv_b", D_CONV),
         ("dt_bias", 16), ("a_log", 16), ("d_skip", 16), ("g_ssm_out", D_SSM), ("g_cross", D),
         ("g_mem", D), ("g_cq", CROSS_HEAD), ("g_ck", CROSS_HEAD), ("g_mlp", D))
SMALL_ROWS = -(-sum(-(-w // LANES) for _, w in SMALL) // 8) * 8


def _pack_small(vals):
    rows = [_pad_lanes(vals[n], -(-w // LANES) * LANES).reshape(-1, LANES) for n, w in SMALL]
    packed = jnp.concatenate(rows, axis=0)
    return jnp.pad(packed, ((0, SMALL_ROWS - packed.shape[0]), (0, 0)))


def _unpack_small(packed):
    out, r = {}, 0
    for n, w in SMALL:
        nr = -(-w // LANES)
        out[n] = packed[r:r + nr].reshape(1, nr * LANES)[:, :w]
        r += nr
    return out


BIG = ("w_in", "w_out", "w_cq", "w_ckv", "w_co", "w_up", "w_down")
WEIGHTS = ("g_mix", "w_in", "g_q", "g_k", "g_attn_out", "conv_w", "conv_b", "dt_bias", "a_log", "d_skip",
           "g_ssm_out", "w_out", "g_cross", "g_mem", "w_cq", "w_ckv", "g_cq", "g_ck", "w_co", "g_mlp", "w_up", "w_down")


REST = ("w_out", "w_cq", "w_ckv", "w_co", "w_up", "w_down")


class _Overlap:
    def __init__(self, shards, after):
        self.gather, self.forward = {}, {}
        self.names = {"a": REST[:4], "b": REST[4:5], "c": REST[5:]}
        for tag, names in self.names.items():
            lands = [_put_own(lax.empty((N_DEV,) + shards[n].shape, BF), shards[n]) for n in names]
            self.gather[tag] = _exchange_start("gather", [shards[n] for n in names], lands, after, "ag_start_" + tag)
            after = self.gather[tag][3]
        self.token = after
        self.groups = []
        self.pairs = {}

    def rest_mid(self, tag, after):
        sems, srcs, lands, _ = self.gather[tag]
        srcs, lands = _exchange_wait("gather", sems, srcs, lands, after, "ag_wait_" + tag)
        self.forward[tag] = _exchange_start("forward", srcs, lands, None, "ag_fwd_start_" + tag)
        return self.forward[tag][3]

    def rest(self, tag, after):
        sems, srcs, lands, _ = self.forward[tag]
        _, lands = _exchange_wait("forward", sems, srcs, lands, after, "ag_fwd_wait_" + tag)
        wf = dict(zip(self.names[tag], lands))
        for n in wf:
            if n in ("w_out", "w_cq", "w_ckv", "w_down"):
                wf[n] = wf[n].reshape(-1, wf[n].shape[-1])
        return wf

    def pair_start(self, name, grad):
        got = lax.empty((N_CHIP,) + grad.shape[1:], grad.dtype)
        self.pairs[name] = _exchange_start("pair", [grad], [got], None, "rs_pair_start_" + name)
        return self.pairs[name][3]

    def emit(self, grads, after):
        names, tag = list(grads), "_%d" % len(self.groups)
        grads, got = dict(grads), {}
        for n in names:
            if n in self.pairs:
                sems, srcs, lands, _ = self.pairs[n]
                srcs, lands = _exchange_wait("pair", sems, srcs, lands, after, "rs_pair_wait_" + n)
                grads[n], got[n] = srcs[0], lands[0]
        sync = [n for n in names if n not in got]
        if sync:
            got.update(zip(sync, _pair_exchange([grads[n] for n in sync], "rs_pair" + tag)))
        sums = [_pair_sum(grads[n], got[n], "rs_sum_" + n) for n in names]
        sems, srcs, lands, token = _exchange_start("scatter", [q for q, _ in sums], [l for _, l in sums], None,
                                                   "rs_chip_start" + tag)
        self.groups.append((names, sems, srcs, lands))
        return token

    def finish(self, gi, after):
        names, sems, srcs, lands = self.groups[gi]
        _, lands = _exchange_wait("scatter", sems, srcs, lands, after, "rs_chip_wait_%d" % gi)
        return dict(zip(names, lands))


def _tie(v, token):
    return v if token is None else v + token[0:1, 0:1]


def _local_step(x, mem, pos_col, target, p, wf, hooks=None):
    p = dict(p)
    inv = np.zeros((1, LANES), np.float32)
    freq = (ROPE_THETA ** (-2.0 * np.arange(ROT // 2, dtype=np.float32) / ROT)).astype(np.float32)
    for l in range(LANES):
        if l % HEAD < ROT:
            inv[0, l] = freq[(l % HEAD) % (ROT // 2)]
    inv_tab = jnp.asarray(inv)
    gq128, gk128 = jnp.tile(p["g_q"], (1, 2)), jnp.tile(p["g_k"], (1, 2))
    dtb128, alog128 = _pad_lanes(p["dt_bias"]), _pad_lanes(p["a_log"])
    dskip_b = jnp.repeat(p["d_skip"], HEAD, axis=1)
    conv_w, conv_b = wf["conv_w"], p["conv_b"]

    h = _rms_fwd(x, p["g_mix"], "rms_mix")
    proj = _matmul(h, wf["w_in"], mode="nn", name="mm_in", tm=1024, tn=1280, tk=D)
    qr, kr = _qk_prep(proj, pos_col, gq128, gk128, inv_tab)
    attn, lse = _attn_fwd(qr, kr, proj)
    attn_n = _attn_norm(attn, p["g_attn_out"])

    xbc = _conv_fwd(proj, conv_w, conv_b)
    token = None if hooks is None else hooks.rest_mid("a", xbc)
    dt, a_cs, dt_b, a_b = _ssd_prep(proj, _tie(dtb128, token), alog128)
    at_r, dt_r = _row_layout(a_cs[:, :N_HEADS]), _row_layout(dt[:, :N_HEADS])
    y_ssd, h_all = _ssd_fwd(xbc, a_b, dt_b, at_r, dt_r)
    ssm_n = _ssd_post(y_ssd, xbc, proj, dskip_b, p["g_ssm_out"])

    if hooks is not None:
        wf = {**wf, **hooks.rest("a", ssm_n)}
    mix = jnp.concatenate([attn_n, ssm_n], axis=1)
    x1 = _matmul(mix, wf["w_out"], mode="nn", name="mm_out", tm=1024, tn=1024, tk=D,
                 extras=(x,), epilogue=lambda acc, r: (acc + r,))
    hc = _rms_fwd(x1, _tie(p["g_cross"], None if hooks is None else hooks.rest_mid("b", x1)), "rms_cross")
    memh = _rms_fwd(mem, p["g_mem"], "rms_mem")
    qc = _matmul(hc, wf["w_cq"], mode="nn", name="mm_cq", tm=1024, tn=512, tk=D)
    kv = _matmul(memh, wf["w_ckv"], mode="nn", name="mm_ckv", tm=N_MEM, tn=1024, tk=D)
    oc = _cross_fwd(qc, kv, p["g_cq"], p["g_ck"])
    x2 = _matmul(oc, wf["w_co"], mode="nn", name="mm_co", tm=1024, tn=256, tk=512, b_cb=256,
                 extras=(x1,), epilogue=lambda acc, r: (acc + r,))
    hm = _rms_fwd(x2, p["g_mlp"], "rms_mlp")
    token = None
    if hooks is not None:
        wf = {**wf, **hooks.rest("b", hm)}
        token = hooks.rest_mid("c", hm)

    def up_epi(acc):
        ru = jnp.maximum(acc, 0.0)
        return ru * ru, 2.0 * ru

    act, relu2 = _matmul(hm, wf["w_up"], mode="nn", name="mm_up", tm=1024, tn=1024, tk=D, b_cb=1024,
                         out_dtypes=(BF, BF), epilogue=up_epi, after=token)
    if hooks is not None:
        wf = {**wf, **hooks.rest("c", act)}

    def loss_epi(acc, r, t):
        d = (acc + r - t) * (1.0 / D)
        return d, d

    dy, dyb = _matmul(act, wf["w_down"], mode="nn", name="mm_down", tm=512, tn=1024, tk=2048, extras=(x2, target),
                      out_dtypes=(F32, BF), epilogue=loss_epi)

    gb, gs = {}, {}
    gb["w_down"] = _matmul(act, dyb, mode="tn", name="mm_down_dw", tm=1024, tn=1024, tk=S,
                           out_dtypes=(BF,)).reshape(N_DEV, D_FF // N_DEV, D)
    token = None if hooks is None else hooks.pair_start("w_down", gb["w_down"])
    du = _matmul(dyb, wf["w_down"], mode="nt", name="mm_down_dx", tm=1024, tn=1024, tk=D, extras=(relu2,),
                 out_dtypes=(BF,), epilogue=lambda acc, r: (acc * r.astype(F32),), after=token)
    gb["w_up"] = _matmul(hm, du, mode="tn", name="mm_up_dw", tm=1024, tn=1024, tk=S, out_dtypes=(BF,), out_cb=1024)
    token = None if hooks is None else hooks.pair_start("w_up", gb["w_up"])
    dhm = _matmul(du, wf["w_up"], mode="nt", name="mm_up_dx", tm=1024, tn=1024, tk=2048, b_cb=1024, after=token)
    if hooks is not None:
        p["g_mlp"] = _tie(p["g_mlp"], hooks.emit({n: gb[n] for n in ("w_down", "w_up")}, dhm))
    dx2, dx2b, gs["g_mlp"], sumsq_dy = _rms_bwd_call(x2, p["g_mlp"], dhm, dy, "rms_mlp_bwd", sumsq_res=True)
    loss_part = sumsq_dy[0, 0] * (0.5 * D)

    doc = _matmul(dx2b, wf["w_co"], mode="nt", name="mm_co_dx", tm=1024, tn=512, tk=256, b_cb=256)
    gb["w_co"] = _matmul(oc, dx2b, mode="tn", name="mm_co_dw", tm=512, tn=256, tk=S, out_dtypes=(BF,), out_cb=256)
    dqc, dkn, dvc, gs["g_cq"] = _cross_bwd(qc, doc, kv, p["g_cq"], p["g_ck"])
    dkv, gs["g_ck"] = _cross_kv_bwd(kv, dkn, dvc, p["g_ck"])
    gb["w_cq"] = _matmul(hc, dqc, mode="tn", name="mm_cq_dw", tm=1024, tn=512, tk=S,
                         out_dtypes=(BF,)).reshape(N_DEV, D // N_DEV, D_CROSS)
    dhc = _matmul(dqc, wf["w_cq"], mode="nt", name="mm_cq_dx", tm=1024, tn=1024, tk=512)
    gb["w_ckv"] = _matmul(memh, dkv, mode="tn", name="mm_ckv_dw", tm=1024, tn=1024, tk=N_MEM,
                          out_dtypes=(BF,)).reshape(N_DEV, D // N_DEV, 2 * D_CROSS)
    dmemh = _matmul(dkv, wf["w_ckv"], mode="nt", name="mm_ckv_dx", tm=N_MEM, tn=1024, tk=1024)
    (gs["g_mem"],) = _rms_bwd_call(mem, p["g_mem"], dmemh, None, "rms_mem_bwd")
    dx1, dx1b, gs["g_cross"] = _rms_bwd_call(x1, p["g_cross"], dhc, dx2, "rms_cross_bwd")

    dmix = _matmul(dx1b, wf["w_out"], mode="nt", name="mm_out_dx", tm=1024, tn=1024, tk=D)
    gb["w_out"] = _matmul(mix, dx1b, mode="tn", name="mm_out_dw", tm=1024, tn=1024, tk=S,
                          out_dtypes=(BF,)).reshape(N_DEV, D // N_DEV, D)
    if hooks is not None:
        p["g_attn_out"] = _tie(p["g_attn_out"],
                               hooks.emit({n: gb[n] for n in ("w_co", "w_cq", "w_ckv", "w_out")}, dmix))

    dattn, delta, gs["g_attn_out"] = _attn_merge_bwd(dmix, attn, p["g_attn_out"])
    dq_rot, dk_rot, dv_attn = _attn_bwd(qr, kr, proj, dattn, lse, delta)
    dq_pre, dk_pre, dv_pre, dgq, dgk = _qk_bwd(dq_rot, dk_rot, dv_attn, proj, pos_col, gq128, gk128, inv_tab)
    gs["g_q"], gs["g_k"] = dgq[:, :HEAD], dgk[:, :HEAD]

    dy_ssd, dz, dxs_skip, dd_lane, gs["g_ssm_out"] = _ssd_post_bwd(y_ssd, xbc, proj, dmix, dskip_b, p["g_ssm_out"])
    gs["d_skip"] = dd_lane.reshape(N_HEADS, HEAD).sum(axis=1).reshape(1, N_HEADS)
    dc_pair, daq_b, dx_ssd, db_pair, dak_r, ddt_r, ddt_b = _ssd_bwd(xbc, dy_ssd, h_all, a_b, dt_b, at_r, dt_r)
    dak = _pad_lanes(_row_layout_inv(dak_r))
    ddt_direct = _pad_lanes(_row_layout_inv(ddt_r))
    ddt_raw, dalog, dbias = _ssd_prep_bwd(daq_b, dak, ddt_direct, ddt_b, dt, proj, dtb128, alog128)
    gs["a_log"], gs["dt_bias"] = dalog[:, :N_HEADS], dbias[:, :N_HEADS]
    same = lambda c: c
    dxbc_x, dcw_x, dcb_x = _conv_bwd(proj, conv_w, conv_b, dxs_skip, dx_ssd, same, same, 0, 8, "conv_bwd_x")
    dxbc_b, dcw_b, dcb_b = _conv_bwd(proj, conv_w, conv_b, db_pair, db_pair, lambda c: 2 * c, lambda c: 2 * c + 1,
                                     D_SSM, 4, "conv_bwd_b")
    dxbc_c, dcw_c, dcb_c = _conv_bwd(proj, conv_w, conv_b, dc_pair, dc_pair, lambda c: 2 * c, lambda c: 2 * c + 1,
                                     D_SSM + 512, 4, "conv_bwd_c")
    g_conv_w = jnp.concatenate([dcw_x, dcw_b, dcw_c], axis=1)
    gs["conv_b"] = jnp.concatenate([dcb_x, dcb_b, dcb_c], axis=1)

    pieces = [dq_pre, dk_pre, dv_pre, dz, dxbc_x, dxbc_b, dxbc_c, ddt_raw]
    pieces.append(jnp.zeros((S, D_INP - sum(t.shape[1] for t in pieces)), BF))
    dproj = jnp.concatenate(pieces, axis=1)
    dw_in = _matmul(h, dproj, mode="tn", name="mm_in_dw", tm=1024, tn=1280, tk=S, out_dtypes=(BF,))
    gb["w_in"] = _w_in_to_blocks(dw_in)
    token = None if hooks is None else hooks.emit({"w_in": gb["w_in"]}, dw_in)
    dh = _matmul(dproj, wf["w_in"], mode="nt", name="mm_in_dx", tm=1024, tn=512, tk=D_INP // 2, after=token)
    grad_x, _, gs["g_mix"] = _rms_bwd_call(x, p["g_mix"], dh, dx1, "rms_mix_bwd")
    return loss_part, grad_x, gb, gs, g_conv_w


def kernel(x, mem, positions, g_mix, w_in, g_q, g_k, g_attn_out, conv_w, conv_b, dt_bias, a_log, d_skip, g_ssm_out, w_out, g_cross, g_mem, w_cq, w_ckv, g_cq, g_ck, w_co, g_mlp, w_up, w_down, loss_target, m_g_mix, m_w_in, m_g_q, m_g_k, m_g_attn_out, m_conv_w, m_conv_b, m_dt_bias, m_a_log, m_d_skip, m_g_ssm_out, m_w_out, m_g_cross, m_g_mem, m_w_cq, m_w_ckv, m_g_cq, m_g_ck, m_w_co, m_g_mlp, m_w_up, m_w_down, v_g_mix, v_w_in, v_g_q, v_g_k, v_g_attn_out, v_conv_w, v_conv_b, v_dt_bias, v_a_log, v_d_skip, v_g_ssm_out, v_w_out, v_g_cross, v_g_mem, v_w_cq, v_w_ckv, v_g_cq, v_g_ck, v_w_co, v_g_mlp, v_w_up, v_w_down):
    args = dict(locals())
    w = {n: args[n] for n in WEIGHTS}
    m = {n: args["m_" + n] for n in WEIGHTS}
    v = {n: args["v_" + n] for n in WEIGHTS}
    small = {n: w[n] for n, _ in SMALL}
    me = 4 * lax.axis_index("x") + 2 * lax.axis_index("y") + lax.axis_index("c")

    w_in_g, conv_w_g = _all_gather([w["w_in"][0].astype(BF), w["conv_w"][0]], "ag_first")
    wf = {"w_in": _w_in_from_blocks(w_in_g), "conv_w": conv_w_g.transpose(1, 0, 2).reshape(CONV_W, D_CONV)}
    overlap = _Overlap({n: w[n][0].astype(BF) for n in REST}, w_in_g)
    p = dict(small)
    p["g_mix"] = _tie(p["g_mix"], overlap.token)

    loss_part, grad_x, _, gs, g_conv_w = _local_step(
        x[0], mem[0], positions.reshape(S, 1), loss_target[0], p, wf, overlap)
    loss = lax.psum(loss_part, ("x", "y", "c"))

    out = {}
    last = grad_x
    for gi in range(3):
        for n, parts in overlap.finish(gi, last).items():
            res_n = _adamw(w[n][0], m[n][0], v[n][0], parts, "adamw_" + n)
            out[n] = [t.reshape(w[n].shape) for t in res_n]
            last = res_n[0]

    sm_parts, cw_parts = _all_gather([_pack_small(gs), g_conv_w], "ag_small_grads", after=last)
    sm = [_unpack_small(t) for t in _adamw(_pack_small(small), _pack_small({n: m[n] for n, _ in SMALL}),
                                           _pack_small({n: v[n] for n, _ in SMALL}), sm_parts, "adamw_small")]
    for n, _ in SMALL:
        out[n] = [t[n] for t in sm]
    cols = D_CONV // N_DEV
    cw_mine = lax.dynamic_slice_in_dim(cw_parts, me * cols, cols, axis=2)
    out["conv_w"] = [t.reshape(w["conv_w"].shape) for t in
                     _adamw(w["conv_w"][0], m["conv_w"][0], v["conv_w"][0], cw_mine, "adamw_conv_w")]

    res = [loss, grad_x.reshape(x.shape)]
    for k in range(4):
        res += [out[n][k] for n in WEIGHTS]
    return tuple(res)
```

```python
import functools
import math

import numpy as np
import jax
import jax.numpy as jnp
from jax import lax
from jax.experimental import pallas as pl
from jax.experimental.pallas import tpu as pltpu

F32 = jnp.float32
BF = jnp.bfloat16

N_DEV = 8
S = 2048
D = 2048
D_ATTN = 1024
N_HEADS = 16
HEAD = 64
ROT = 16
ROPE_THETA = 500000.0
D_SSM = 1024
D_CONV = 2048
CONV_W = 4
N_MEM = 256
D_CROSS = 512
CROSS_HEAD = 128
D_FF = 8192
D_IN = 6160
D_INP = 6400
DT_COLBLK = (4 * D_ATTN + D_CONV) // 128
EPS = 1e-6
BLK = 128
NB = S // BLK
LANES = 128
NEG = -1e30

ADAM_LR = 0.001
ADAM_B1 = 0.9
ADAM_B2 = 0.999
ADAM_EPS = 1e-08
ADAM_WD = 0.01
ADAM_STEP = 10

VMEM_LIMIT_BYTES = 48 * 1024 * 1024
ROW_TILE = 256


def _params(*sem):
    return pltpu.CompilerParams(dimension_semantics=sem, vmem_limit_bytes=VMEM_LIMIT_BYTES)


def _matmul(a, b, *, mode, name, tm, tn, tk, out_dtypes=(F32,), b_cb=None, out_cb=None,
            extras=(), epilogue=None, after=None):
    if mode == "tn":
        kk, m = a.shape
    else:
        m, kk = a.shape
    if b_cb is None:
        br, bc = b.shape
    else:
        br, bc = b.shape[1], N_DEV * b_cb
    n = br if mode == "nt" else bc
    assert m % tm == 0 and n % tn == 0 and kk % tk == 0, (name, m, n, kk)
    nk = kk // tk
    grid = (m // tm, n // tn, nk)

    if mode == "tn":
        a_spec = pl.BlockSpec((tk, tm), lambda i, j, k: (k, i))
    else:
        a_spec = pl.BlockSpec((tm, tk), lambda i, j, k: (i, k))
    if mode == "nt":
        b_blk, b_idx = (tn, tk), (lambda i, j, k: (j, k))
    else:
        b_blk, b_idx = (tk, tn), (lambda i, j, k: (k, j))
    group = 1
    if b_cb is None:
        b_spec = pl.BlockSpec(b_blk, b_idx)
    elif mode == "nt" and tk > b_cb:
        assert tk % b_cb == 0
        group = tk // b_cb
        b_spec = pl.BlockSpec((group, tn, b_cb), lambda i, j, k: (k, j, 0))
    else:
        tc = b_blk[1]
        assert b_cb % tc == 0
        per = b_cb // tc

        def b_idx3(i, j, k):
            r, c = b_idx(i, j, k)
            return (c // per, r, c % per)
        b_spec = pl.BlockSpec((None,) + b_blk, b_idx3)
    ex_specs = [pl.BlockSpec((tm, tn), lambda i, j, k: (i, j)) for _ in extras]
    if out_cb is None:
        out_shape = [jax.ShapeDtypeStruct((m, n), dt) for dt in out_dtypes]
        out_specs = [pl.BlockSpec((tm, tn), lambda i, j, k: (i, j)) for _ in out_dtypes]
    else:
        assert out_cb % tn == 0
        pero = out_cb // tn
        out_shape = [jax.ShapeDtypeStruct((N_DEV, m, out_cb), dt) for dt in out_dtypes]
        out_specs = [pl.BlockSpec((None, tm, tn), lambda i, j, k: (j // pero, i, j % pero)) for _ in out_dtypes]
    dn = {"nn": (((1,), (0,)), ((), ())), "nt": (((1,), (1,)), ((), ())), "tn": (((0,), (0,)), ((), ()))}[mode]
    ne, no = len(extras), len(out_dtypes)
    n_after = 0 if after is None else 1

    def kern(a_ref, b_ref, *rest):
        ex_refs, out_refs = rest[:ne], rest[ne + n_after:ne + n_after + no]

        def finish(acc):
            outs = (acc,) if epilogue is None else epilogue(acc, *[r[...] for r in ex_refs])
            for r, o in zip(out_refs, outs):
                r[...] = o.astype(r.dtype)

        if group == 1:
            part = lax.dot_general(a_ref[...].astype(BF), b_ref[...].astype(BF), dn, preferred_element_type=F32)
        else:
            part = sum(lax.dot_general(a_ref[:, g * b_cb:(g + 1) * b_cb].astype(BF), b_ref[g].astype(BF), dn,
                                       preferred_element_type=F32) for g in range(group))
        if nk == 1:
            finish(part)
            return
        acc_ref = rest[ne + n_after + no]
        k = pl.program_id(2)

        @pl.when(k == 0)
        def _():
            acc_ref[...] = part

        @pl.when(k > 0)
        def _():
            acc_ref[...] += part

        @pl.when(k == nk - 1)
        def _():
            finish(acc_ref[...])

    res = pl.pallas_call(
        kern, name=name, grid=grid, in_specs=[a_spec, b_spec] + ex_specs + [ANY] * n_after, out_specs=out_specs,
        out_shape=out_shape, scratch_shapes=[pltpu.VMEM((tm, tn), F32)] if nk > 1 else [],
        compiler_params=_params("parallel", "parallel", "arbitrary"),
    )(a, b, *extras, *([] if after is None else [after]))
    return res[0] if no == 1 else tuple(res)


def _rowwise(body, *, name, nrows, tile, row_ins, full_ins=(), row_outs=(), acc_outs=(), scratch=(),
             reverse=False):
    n = nrows // tile

    def ridx(i):
        return (n - 1 - i) if reverse else i

    in_specs, args = [], []
    for arr, width, cb in row_ins:
        in_specs.append(pl.BlockSpec((tile, width), lambda i, cb=cb: (ridx(i), cb)))
        args.append(arr)
    for arr in full_ins:
        in_specs.append(pl.BlockSpec(arr.shape, lambda i, nd=arr.ndim: (0,) * nd))
        args.append(arr)
    out_shape, out_specs = [], []
    for width, dt in row_outs:
        out_shape.append(jax.ShapeDtypeStruct((nrows, width), dt))
        out_specs.append(pl.BlockSpec((tile, width), lambda i: (ridx(i), 0)))
    for shp, dt in acc_outs:
        out_shape.append(jax.ShapeDtypeStruct(shp, dt))
        out_specs.append(pl.BlockSpec(shp, lambda i, nd=len(shp): (0,) * nd))

    def kern(*refs):
        body(pl.program_id(0), n, *refs)

    return pl.pallas_call(kern, name=name, grid=(n,), in_specs=in_specs, out_specs=out_specs,
                          out_shape=out_shape, scratch_shapes=list(scratch),
                          compiler_params=_params("arbitrary"))(*args)


def _accum(ref, val, i):
    @pl.when(i == 0)
    def _():
        ref[...] = val

    @pl.when(i > 0)
    def _():
        ref[...] += val


def _sigmoid(x):
    return 1.0 / (1.0 + jnp.exp(-x))


def _rms_n(x):
    r = lax.rsqrt(jnp.mean(x * x, axis=-1, keepdims=True) + EPS)
    return x * r, r


def _rms_bwd(x, g, dh):
    n, r = _rms_n(x)
    dn = dh * g
    dx = r * (dn - n * jnp.mean(dn * n, axis=-1, keepdims=True))
    return dx, jnp.sum(dh * n, axis=0, keepdims=True)


def _seg_sum(x, seg):
    t, w = x.shape
    tiles = [x[:, LANES * j:LANES * (j + 1)] for j in range(w // LANES)]
    outs = []
    if seg == 64:
        lo = lax.broadcasted_iota(jnp.int32, (t, LANES), 1) < 64
        for xt in tiles:
            s_lo = jnp.sum(jnp.where(lo, xt, 0.0), axis=-1, keepdims=True)
            s_hi = jnp.sum(jnp.where(lo, 0.0, xt), axis=-1, keepdims=True)
            outs.append(jnp.where(lo, s_lo, s_hi))
    else:
        sums = [jnp.sum(xt, axis=-1, keepdims=True) for xt in tiles]
        if seg == 256:
            sums = [sums[2 * (j // 2)] + sums[2 * (j // 2) + 1] for j in range(len(sums))]
        else:
            assert seg == 128
        outs = [jnp.broadcast_to(s, (t, LANES)) for s in sums]
    return outs[0] if len(outs) == 1 else jnp.concatenate(outs, axis=1)


def _seg_rms_n(x, seg):
    r = lax.rsqrt(_seg_sum(x * x, seg) * (1.0 / seg) + EPS)
    return x * r, r


def _seg_rms_bwd(x, g, dy, seg):
    n, r = _seg_rms_n(x, seg)
    dn = dy * g
    dx = r * (dn - n * (_seg_sum(dn * n, seg) * (1.0 / seg)))
    return dx, jnp.sum(dy * n, axis=0, keepdims=True)


def _rope_tables(pos_col, inv_tab, t):
    ang = pos_col.astype(F32) * inv_tab
    idx = lax.broadcasted_iota(jnp.int32, (t, LANES), 1) % HEAD
    cos, sin = jnp.cos(ang), jnp.sin(ang)
    c = jnp.where(idx < ROT, cos, 1.0)
    sg = jnp.where(idx < ROT // 2, -sin, jnp.where(idx < ROT, sin, 0.0))
    return c, sg, idx < ROT // 2


def _rope(x, c, sg, lo):
    partner = jnp.where(lo, pltpu.roll(x, LANES - ROT // 2, 1), pltpu.roll(x, ROT // 2, 1))
    return x * c + partner * sg


def _fold_heads(v):
    v8 = jnp.broadcast_to(v, (8, LANES))
    return (v8 + pltpu.roll(v8, HEAD, 1))[0:1]


def _dot(a, b, dn):
    return lax.dot_general(a, b, (dn, ((), ())), preferred_element_type=F32)


NN = ((1,), (0,))
NT = ((1,), (1,))
TN = ((0,), (0,))


def _dot3(l01, x):
    x1 = x.astype(BF)
    r1 = x - x1.astype(F32)
    x2 = r1.astype(BF)
    x3 = (r1 - x2.astype(F32)).astype(BF)
    return _dot(l01, x1, NN) + _dot(l01, x2, NN) + _dot(l01, x3, NN)


def _rms_fwd(x, g, name):
    rows = x.shape[0]

    def body(i, n, x_ref, g_ref, o_ref):
        nx, _ = _rms_n(x_ref[...])
        o_ref[...] = (nx * g_ref[...]).astype(BF)

    return _rowwise(body, name=name, nrows=rows, tile=min(ROW_TILE, rows), row_ins=[(x, D, 0)],
                    full_ins=[g], row_outs=[(D, BF)])[0]


def _qk_prep(proj, pos_col, gq128, gk128, inv_tab):
    scale = HEAD ** -0.5

    def body(i, n, q_ref, k_ref, p_ref, gq_ref, gk_ref, it_ref, qo_ref, ko_ref):
        t = q_ref.shape[0]
        c, sg, lo = _rope_tables(p_ref[...], it_ref[...], t)
        for j in range(D_ATTN // LANES):
            sl = slice(LANES * j, LANES * (j + 1))
            qn, _ = _seg_rms_n(q_ref[:, sl], HEAD)
            kn, _ = _seg_rms_n(k_ref[:, sl], HEAD)
            qo_ref[:, sl] = _rope(qn * gq_ref[...], c, sg, lo) * scale
            ko_ref[:, sl] = _rope(kn * gk_ref[...], c, sg, lo)

    return _rowwise(body, name="qk_prep", nrows=S, tile=ROW_TILE,
                    row_ins=[(proj, D_ATTN, 0), (proj, D_ATTN, 1), (pos_col, 1, 0)],
                    full_ins=[gq128, gk128, inv_tab], row_outs=[(D_ATTN, F32)] * 2)


ATTN_QB = 16
V_COLS = pl.BlockSpec((S, LANES), lambda p: (0, 2 * D_ATTN // LANES + p))


def _band(b, d):
    nb = NB // d
    r, n = b // nb, b % nb
    width, kn = (BLK, n) if nb == 1 else (2 * BLK, jnp.maximum(n - 1, 0))

    def rows(first_member, count):
        if d == 1:
            return pl.ds(pl.multiple_of(first_member, BLK), count)
        return pl.ds(first_member * d + r, count, stride=d)

    qm = n * BLK + (lax.broadcasted_iota(jnp.int32, (2 * BLK, width), 0) & (BLK - 1))
    km = kn * BLK + lax.broadcasted_iota(jnp.int32, (2 * BLK, width), 1)
    valid = km <= qm
    if nb > 1:
        valid = valid & (km >= qm - BLK)
    return rows(n * BLK, BLK), rows(kn * BLK, width), valid


DILATIONS = (1, 4, 16)


def _attn_fwd(q, k, v):
    def kern(q_ref, k_ref, v_ref, a_ref, lse_ref, *scr):
        half0 = lax.broadcasted_iota(jnp.int32, (BLK, LANES), 1) < HEAD
        for gi, d in enumerate(DILATIONS):
            o_ref, l_ref = scr[2 * gi], scr[2 * gi + 1]

            def group(g, carry, d=d, o_ref=o_ref, l_ref=l_ref):
                for bb in range(ATTN_QB):
                    q_rows, k_rows, valid = _band(g * ATTN_QB + bb, d)
                    q = q_ref[q_rows, :]
                    kb, vb = k_ref[k_rows, :].astype(BF), v_ref[k_rows, :].astype(BF)
                    q2 = jnp.concatenate([jnp.where(half0, q, 0.0), jnp.where(half0, 0.0, q)], axis=0).astype(BF)
                    s = jnp.where(valid, _dot(q2, kb, NT), NEG)
                    m = jnp.max(s, axis=-1, keepdims=True)
                    p = jnp.exp(s - m)
                    den = jnp.sum(p, axis=-1, keepdims=True)
                    o2 = _dot(p.astype(BF), vb, NN) / den
                    l2 = m + jnp.log(den)
                    o_ref[q_rows, :] = jnp.where(half0, o2[:BLK], o2[BLK:])
                    l_ref[q_rows, :] = jnp.where(half0, l2[:BLK], l2[BLK:])
                return carry

            lax.fori_loop(0, NB // ATTN_QB, group, 0)

        def merge(i, carry):
            rows = pl.ds(pl.multiple_of(i * ROW_TILE, ROW_TILE), ROW_TILE)
            la, lb, lc = scr[1][rows, :], scr[3][rows, :], scr[5][rows, :]
            m = jnp.maximum(jnp.maximum(la, lb), lc)
            ea, eb, ec = jnp.exp(la - m), jnp.exp(lb - m), jnp.exp(lc - m)
            den = ea + eb + ec
            a_ref[rows, :] = (ea * scr[0][rows, :] + eb * scr[2][rows, :] + ec * scr[4][rows, :]) / den
            lse_ref[rows, :] = m + jnp.log(den)
            return carry

        lax.fori_loop(0, S // ROW_TILE, merge, 0)

    seq = pl.BlockSpec((S, LANES), lambda p: (0, p))
    return pl.pallas_call(
        kern, name="attn_fwd", grid=(D_ATTN // LANES,), in_specs=[seq, seq, V_COLS], out_specs=[seq, seq],
        out_shape=[jax.ShapeDtypeStruct((S, D_ATTN), F32)] * 2,
        scratch_shapes=[pltpu.VMEM((S, LANES), F32)] * (2 * len(DILATIONS)),
        compiler_params=_params("parallel"))(q, k, v)


def _attn_norm(attn, g_attn):
    def body(i, n, a_ref, g_ref, an_ref):
        nx, _ = _rms_n(a_ref[...])
        an_ref[...] = (nx * g_ref[...]).astype(BF)

    return _rowwise(body, name="attn_norm", nrows=S, tile=ROW_TILE, row_ins=[(attn, D_ATTN, 0)],
                    full_ins=[g_attn], row_outs=[(D_ATTN, BF)])[0]


def _conv_taps(x, w_ref):
    rows = lax.broadcasted_iota(jnp.int32, x.shape, 0)
    shifted = []
    for w in range(CONV_W):
        k = CONV_W - 1 - w
        shifted.append(x if k == 0 else jnp.where(rows >= k, pltpu.roll(x, k, 0), 0.0))
    acc = shifted[0] * w_ref[0:1, :]
    for w in range(1, CONV_W):
        acc = acc + shifted[w] * w_ref[w:w + 1, :]
    return acc, shifted


def _conv_fwd(proj, conv_w, conv_b):
    tc = 256

    def kern(x_ref, w_ref, b_ref, o_ref):
        c, _ = _conv_taps(x_ref[...], w_ref)
        c = c + b_ref[...]
        o_ref[...] = c * _sigmoid(c)

    return pl.pallas_call(
        kern, name="conv_fwd", grid=(D_CONV // tc,),
        in_specs=[pl.BlockSpec((S, tc), lambda c: (0, 4 * D_ATTN // tc + c)),
                  pl.BlockSpec((CONV_W, tc), lambda c: (0, c)), pl.BlockSpec((1, tc), lambda c: (0, c))],
        out_specs=pl.BlockSpec((S, tc), lambda c: (0, c)),
        out_shape=jax.ShapeDtypeStruct((S, D_CONV), F32), compiler_params=_params("parallel"))(proj, conv_w, conv_b)


def _softplus(x):
    return jnp.maximum(x, 0.0) + jnp.log1p(jnp.exp(-jnp.abs(x)))


def _dot3r(x, r01):
    x1 = x.astype(BF)
    r1 = x - x1.astype(F32)
    x2 = r1.astype(BF)
    x3 = (r1 - x2.astype(F32)).astype(BF)
    return _dot(x1, r01, NN) + _dot(x2, r01, NN) + _dot(x3, r01, NN)


def _spread_heads(v):
    sel = (lax.broadcasted_iota(jnp.int32, (LANES, D_SSM), 1) // HEAD
           == lax.broadcasted_iota(jnp.int32, (LANES, D_SSM), 0))
    return _dot3r(v, sel.astype(BF))


def _collect_heads(v):
    sel = (lax.broadcasted_iota(jnp.int32, (D_SSM, LANES), 0)
           == HEAD * lax.broadcasted_iota(jnp.int32, (D_SSM, LANES), 1))
    return _dot3r(v, sel.astype(BF))


def _ssd_prep(proj, dt_bias128, a_log128):
    def body(i, n, raw_ref, b_ref, al_ref, dt_ref, a_ref, dtb_ref, ab_ref, carry):
        @pl.when(i == 0)
        def _():
            carry[...] = jnp.zeros_like(carry)

        dt = _softplus(raw_ref[...] + b_ref[...])
        da = dt * (-jnp.exp(al_ref[...]))
        tri = (lax.broadcasted_iota(jnp.int32, (BLK, BLK), 0) >= lax.broadcasted_iota(jnp.int32, (BLK, BLK), 1))
        cs = _dot3(tri.astype(BF), da) + carry[0:1, :]
        dt_ref[...] = dt
        a_ref[...] = cs
        dtb_ref[...] = _spread_heads(dt)
        ab_ref[...] = _spread_heads(cs)
        carry[...] = jnp.broadcast_to(cs[BLK - 1:BLK, :], carry.shape)

    return _rowwise(body, name="ssd_prep", nrows=S, tile=BLK, row_ins=[(proj, LANES, DT_COLBLK)],
                    full_ins=[dt_bias128, a_log128],
                    row_outs=[(LANES, F32), (LANES, F32), (D_SSM, F32), (D_SSM, F32)],
                    scratch=[pltpu.VMEM((8, LANES), F32)])


def _ssd_decay(a_col, at_row, causal):
    diff = jnp.where(causal, a_col - at_row, 0.0)
    return jnp.where(causal, jnp.exp(diff), 0.0)


SSD_CB = 16


def _ssd_fwd(xbc, a_b, dt_b, at_r, dt_r):
    def kern(c_ref, b_ref, x_ref, ab_ref, dtb_ref, at_ref, dt_ref, y_ref, hall_ref):
        half0 = lax.broadcasted_iota(jnp.int32, (BLK, LANES), 1) < HEAD
        causal = lax.broadcasted_iota(jnp.int32, (BLK, BLK), 1) <= lax.broadcasted_iota(jnp.int32, (BLK, BLK), 0)

        def step(i, carry):
            h, a_prev = carry
            for cc in range(SSD_CB):
                chunk = i * SSD_CB + cc
                rows = pl.ds(pl.multiple_of(chunk * BLK, BLK), BLK)
                ci, bi, x = c_ref[rows, :].astype(BF), b_ref[rows, :].astype(BF), x_ref[rows, :]
                ab = ab_ref[rows, :]
                a_end = ab[BLK - 1:BLK, :]
                alpha = jnp.exp(ab - a_prev)
                beta = jnp.exp(a_end - ab) * dtb_ref[rows, :]
                hall_ref[rows, :] = h
                cb = _dot(ci, bi, NT)
                at, dtj = at_ref[chunk], dt_ref[chunk]
                y = alpha * _dot(ci, h.astype(BF), NN)
                for hd in range(2):
                    hm = half0 if hd == 0 else jnp.logical_not(half0)
                    w = cb * _ssd_decay(ab[:, HEAD * hd:HEAD * hd + 1], at[hd:hd + 1, :], causal) * dtj[hd:hd + 1, :]
                    y = y + _dot(w.astype(BF), jnp.where(hm, x, 0.0).astype(BF), NN)
                y_ref[rows, :] = y
                h = alpha[BLK - 1:BLK, :] * h + _dot(bi, (beta * x).astype(BF), TN)
                a_prev = a_end
            return h, a_prev

        lax.fori_loop(0, NB // SSD_CB, step, (jnp.zeros((BLK, LANES), F32), jnp.zeros((1, LANES), F32)))

    npair = D_SSM // LANES
    rowvec = pl.BlockSpec((None, NB, 2, BLK), lambda p: (p, 0, 0, 0))
    seq = pl.BlockSpec((S, LANES), lambda p: (0, p))
    return pl.pallas_call(
        kern, name="ssd_fwd", grid=(npair,),
        in_specs=[pl.BlockSpec((S, LANES), lambda p: (0, 12 + p // 2)),
                  pl.BlockSpec((S, LANES), lambda p: (0, 8 + p // 2)), seq, seq, seq, rowvec, rowvec],
        out_specs=[seq, seq], out_shape=[jax.ShapeDtypeStruct((S, D_SSM), F32)] * 2,
        compiler_params=_params("parallel"))(xbc, xbc, xbc, a_b, dt_b, at_r, dt_r)


def _ssd_post(y_ssd, xbc, proj, dskip_b, g_ssm):
    def body(i, n, y_ref, xs_ref, z_ref, d_ref, g_ref, o_ref):
        z = z_ref[...]
        y2 = (y_ref[...] + d_ref[...] * xs_ref[...]) * (z * _sigmoid(z))
        nx, _ = _seg_rms_n(y2, 256)
        o_ref[...] = (nx * g_ref[...]).astype(BF)

    return _rowwise(body, name="ssd_post", nrows=S, tile=ROW_TILE,
                    row_ins=[(y_ssd, D_SSM, 0), (xbc, D_SSM, 0), (proj, D_SSM, 3)], full_ins=[dskip_b, g_ssm],
                    row_outs=[(D_SSM, BF)])[0]


def _cross_heads(q, kv_ref, gq, gk):
    out = []
    for h in range(D_CROSS // CROSS_HEAD):
        sl = slice(CROSS_HEAD * h, CROSS_HEAD * (h + 1))
        nq, rq = _rms_n(q[:, sl])
        nk, rk = _rms_n(kv_ref[:, sl])
        v = kv_ref[:, D_CROSS + CROSS_HEAD * h:D_CROSS + CROSS_HEAD * (h + 1)]
        out.append((sl, nq, rq, nk, rk, v))
    return out


def _cross_fwd(qc, kv, g_cq, g_ck):
    scale = CROSS_HEAD ** -0.5

    def body(i, n, q_ref, kv_ref, gq_ref, gk_ref, o_ref):
        for sl, nq, _, nk, _, v in _cross_heads(q_ref[...], kv_ref, gq_ref[...], gk_ref[...]):
            qn = (nq * gq_ref[...] * scale).astype(BF)
            kn = (nk * gk_ref[...]).astype(BF)
            s = _dot(qn, kn, NT)
            e = jnp.exp(s - jnp.max(s, axis=-1, keepdims=True))
            p = e / jnp.sum(e, axis=-1, keepdims=True)
            o_ref[:, sl] = _dot(p.astype(BF), v.astype(BF), NN).astype(BF)

    return _rowwise(body, name="cross_fwd", nrows=S, tile=ROW_TILE, row_ins=[(qc, D_CROSS, 0)],
                    full_ins=[kv, g_cq, g_ck], row_outs=[(D_CROSS, BF)])[0]


def _rms_bwd_call(x, g, dh, dres, name, sumsq_res=False):
    rows = x.shape[0]
    has_res = dres is not None

    def body(i, n, *refs):
        if has_res:
            x_ref, dh_ref, dr_ref, g_ref, dx_ref, dxb_ref, dg_ref = refs[:7]
        else:
            x_ref, dh_ref, g_ref, dg_ref = refs
        dx, dg = _rms_bwd(x_ref[...], g_ref[...], dh_ref[...])
        if has_res:
            dr = dr_ref[...]
            dx = dx + dr
            dx_ref[...] = dx
            dxb_ref[...] = dx.astype(BF)
            if sumsq_res:
                _accum(refs[7], jnp.sum(jnp.sum(dr * dr, axis=0, keepdims=True), axis=1, keepdims=True), i)
        _accum(dg_ref, dg, i)

    row_ins = [(x, D, 0), (dh, D, 0)] + ([(dres, D, 0)] if has_res else [])
    return _rowwise(body, name=name, nrows=rows, tile=min(ROW_TILE, rows), row_ins=row_ins, full_ins=[g],
                    row_outs=[(D, F32), (D, BF)] if has_res else [],
                    acc_outs=[((1, D), F32)] + ([((1, 1), F32)] if sumsq_res else []))


def _cross_bwd(qc, doc, kv, g_cq, g_ck):
    scale = CROSS_HEAD ** -0.5

    def body(i, n, q_ref, do_ref, kv_ref, gq_ref, gk_ref, dq_ref, dkn_ref, dv_ref, dgq_ref):
        dgq = jnp.zeros((1, CROSS_HEAD), F32)
        dkn_parts, dv_parts = [], []
        for sl, nq, rq, nk, _, v in _cross_heads(q_ref[...], kv_ref, gq_ref[...], gk_ref[...]):
            qn = (nq * gq_ref[...] * scale).astype(BF)
            kn = (nk * gk_ref[...]).astype(BF)
            s = _dot(qn, kn, NT)
            e = jnp.exp(s - jnp.max(s, axis=-1, keepdims=True))
            p = e / jnp.sum(e, axis=-1, keepdims=True)
            do = do_ref[:, sl].astype(BF)
            dv_parts.append(_dot(p.astype(BF), do, TN))
            dp = _dot(do, v.astype(BF), NT)
            ds = (p * (dp - jnp.sum(dp * p, axis=-1, keepdims=True))).astype(BF)
            dqn = _dot(ds, kn, NN)
            dkn_parts.append(_dot(ds, qn, TN))
            dn = dqn * (gq_ref[...] * scale)
            dq_ref[:, sl] = (rq * (dn - nq * jnp.mean(dn * nq, axis=-1, keepdims=True))).astype(BF)
            dgq = dgq + jnp.sum(dqn * scale * nq, axis=0, keepdims=True)
        _accum(dkn_ref, jnp.concatenate(dkn_parts, axis=1), i)
        _accum(dv_ref, jnp.concatenate(dv_parts, axis=1), i)
        _accum(dgq_ref, dgq, i)

    return _rowwise(body, name="cross_bwd", nrows=S, tile=ROW_TILE, row_ins=[(qc, D_CROSS, 0), (doc, D_CROSS, 0)],
                    full_ins=[kv, g_cq, g_ck], row_outs=[(D_CROSS, BF)],
                    acc_outs=[((N_MEM, D_CROSS), F32), ((N_MEM, D_CROSS), F32), ((1, CROSS_HEAD), F32)])


def _cross_kv_bwd(kv, dkn, dv, g_ck):
    def body(i, n, kv_ref, dkn_ref, dv_ref, gk_ref, dkv_ref, dgk_ref):
        dgk = jnp.zeros((1, CROSS_HEAD), F32)
        for h in range(D_CROSS // CROSS_HEAD):
            sl = slice(CROSS_HEAD * h, CROSS_HEAD * (h + 1))
            dk, dg = _rms_bwd(kv_ref[:, sl], gk_ref[...], dkn_ref[:, sl])
            dkv_ref[:, sl] = dk.astype(BF)
            dgk = dgk + dg
        dkv_ref[:, D_CROSS:] = dv_ref[...].astype(BF)
        dgk_ref[...] = dgk

    return _rowwise(body, name="cross_kv_bwd", nrows=N_MEM, tile=N_MEM,
                    row_ins=[(kv, 2 * D_CROSS, 0), (dkn, D_CROSS, 0), (dv, D_CROSS, 0)], full_ins=[g_ck],
                    row_outs=[(2 * D_CROSS, BF)], acc_outs=[((1, CROSS_HEAD), F32)])


def _attn_merge_bwd(dmix, attn, g_attn):
    def body(i, n, dn_ref, a_ref, g_ref, da_ref, dl_ref, dg_ref):
        attn_v = a_ref[...]
        da, dg = _rms_bwd(attn_v, g_ref[...], dn_ref[...])
        da_ref[...] = da
        dl_ref[...] = _seg_sum(da * attn_v, HEAD)
        _accum(dg_ref, dg, i)

    return _rowwise(body, name="attn_merge_bwd", nrows=S, tile=ROW_TILE,
                    row_ins=[(dmix, D_ATTN, 0), (attn, D_ATTN, 0)], full_ins=[g_attn],
                    row_outs=[(D_ATTN, F32), (D_ATTN, F32)], acc_outs=[((1, D_ATTN), F32)])


def _attn_bwd(q, k, v, do, lse, delta):
    def kern(q_ref, k_ref, v_ref, do_ref, l_ref, d_ref, dq_ref, dk_ref, dv_ref):
        dk_ref[...] = jnp.zeros_like(dk_ref)
        dv_ref[...] = jnp.zeros_like(dv_ref)
        half0 = lax.broadcasted_iota(jnp.int32, (BLK, LANES), 1) < HEAD
        for gi, d in enumerate(DILATIONS):

            def group(g, carry, d=d, first=(gi == 0)):
                updates = []
                for bb in range(ATTN_QB):
                    q_rows, k_rows, valid = _band(g * ATTN_QB + bb, d)
                    q, do = q_ref[q_rows, :], do_ref[q_rows, :]
                    lse_t, del_t = l_ref[q_rows, :], d_ref[q_rows, :]
                    kb, vb = k_ref[k_rows, :].astype(BF), v_ref[k_rows, :].astype(BF)
                    q2 = jnp.concatenate([jnp.where(half0, q, 0.0), jnp.where(half0, 0.0, q)], axis=0).astype(BF)
                    do2 = jnp.concatenate([jnp.where(half0, do, 0.0), jnp.where(half0, 0.0, do)], axis=0).astype(BF)
                    lse2 = jnp.concatenate([lse_t[:, 0:1], lse_t[:, HEAD:HEAD + 1]], axis=0)
                    del2 = jnp.concatenate([del_t[:, 0:1], del_t[:, HEAD:HEAD + 1]], axis=0)
                    s = jnp.where(valid, _dot(q2, kb, NT), NEG)
                    p = jnp.exp(s - lse2)
                    ds = (p * (_dot(do2, vb, NT) - del2)).astype(BF)
                    dq2 = _dot(ds, kb, NN)
                    dq = jnp.where(half0, dq2[:BLK], dq2[BLK:])
                    if first:
                        dq_ref[q_rows, :] = dq
                    else:
                        dq_ref[q_rows, :] += dq
                    updates.append((k_rows, _dot(ds, q2, TN), _dot(p.astype(BF), do2, TN)))
                for k_rows, dk, dv in updates:
                    dk_ref[k_rows, :] += dk
                    dv_ref[k_rows, :] += dv
                return carry

            lax.fori_loop(0, NB // ATTN_QB, group, 0)

    seq = pl.BlockSpec((S, LANES), lambda p: (0, p))
    return pl.pallas_call(
        kern, name="attn_bwd", grid=(D_ATTN // LANES,), in_specs=[seq, seq, V_COLS, seq, seq, seq],
        out_specs=[seq, seq, seq], out_shape=[jax.ShapeDtypeStruct((S, D_ATTN), F32)] * 3,
        compiler_params=_params("parallel"))(q, k, v, do, lse, delta)


def _qk_bwd(dq_rot, dk_rot, dv, proj, pos_col, gq128, gk128, inv_tab):
    scale = HEAD ** -0.5

    def body(i, n, dq_ref, dk_ref, dv_ref, q_ref, k_ref, p_ref, gq_ref, gk_ref, it_ref,
             dqo_ref, dko_ref, dvo_ref, dgq_ref, dgk_ref):
        t = q_ref.shape[0]
        c, sg, lo = _rope_tables(p_ref[...], it_ref[...], t)
        dgq = jnp.zeros((1, LANES), F32)
        dgk = jnp.zeros((1, LANES), F32)
        for j in range(D_ATTN // LANES):
            sl = slice(LANES * j, LANES * (j + 1))
            dqr = _rope(dq_ref[:, sl], c, -sg, lo) * scale
            dkr = _rope(dk_ref[:, sl], c, -sg, lo)
            dq, gq = _seg_rms_bwd(q_ref[:, sl], gq_ref[...], dqr, HEAD)
            dk, gk = _seg_rms_bwd(k_ref[:, sl], gk_ref[...], dkr, HEAD)
            dqo_ref[:, sl] = dq.astype(BF)
            dko_ref[:, sl] = dk.astype(BF)
            dgq, dgk = dgq + gq, dgk + gk
        dvo_ref[...] = dv_ref[...].astype(BF)
        _accum(dgq_ref, _fold_heads(dgq), i)
        _accum(dgk_ref, _fold_heads(dgk), i)

    return _rowwise(body, name="qk_bwd", nrows=S, tile=ROW_TILE,
                    row_ins=[(a, D_ATTN, 0) for a in (dq_rot, dk_rot, dv)]
                    + [(proj, D_ATTN, 0), (proj, D_ATTN, 1), (pos_col, 1, 0)],
                    full_ins=[gq128, gk128, inv_tab], row_outs=[(D_ATTN, BF)] * 3,
                    acc_outs=[((1, LANES), F32)] * 2)


def _ssd_post_bwd(y_ssd, xbc, proj, dmix, dskip_b, g_ssm):
    def body(i, n, y_ref, xs_ref, z_ref, do_ref, d_ref, g_ref, dy_ref, dz_ref, dxs_ref, dd_ref, dg_ref):
        z, xs = z_ref[...], xs_ref[...]
        sg = _sigmoid(z)
        gate = z * sg
        y = y_ref[...] + d_ref[...] * xs
        dy2, dg = _seg_rms_bwd(y * gate, g_ref[...], do_ref[...], 256)
        dy = dy2 * gate
        dy_ref[...] = dy.astype(BF)
        dz_ref[...] = (dy2 * y * (sg * (1.0 + z * (1.0 - sg)))).astype(BF)
        dxs_ref[...] = dy * d_ref[...]
        _accum(dd_ref, jnp.sum(dy * xs, axis=0, keepdims=True), i)
        _accum(dg_ref, dg, i)

    return _rowwise(body, name="ssd_post_bwd", nrows=S, tile=ROW_TILE,
                    row_ins=[(y_ssd, D_SSM, 0), (xbc, D_SSM, 0), (proj, D_SSM, 3), (dmix, D_SSM, 1)],
                    full_ins=[dskip_b, g_ssm], row_outs=[(D_SSM, BF), (D_SSM, BF), (D_SSM, F32)],
                    acc_outs=[((1, D_SSM), F32), ((1, D_SSM), F32)])


def _ssd_bwd(xbc, dy, h_all, a_b, dt_b, at_r, dt_r):
    def kern(c_ref, b_ref, x_ref, dy_ref, hall_ref, ab_ref, dtb_ref, at_ref, dt_ref,
             dc_ref, daq_ref, dx_ref, db_ref, dak_ref, ddt_ref, ddtb_ref):
        half0 = lax.broadcasted_iota(jnp.int32, (BLK, LANES), 1) < HEAD
        hms = (half0, jnp.logical_not(half0))
        causal = lax.broadcasted_iota(jnp.int32, (BLK, BLK), 1) <= lax.broadcasted_iota(jnp.int32, (BLK, BLK), 0)
        row = lax.broadcasted_iota(jnp.int32, (BLK, LANES), 0)

        def step(t, carry_in):
            dh_next, carry = carry_in
            for cc in reversed(range(SSD_CB)):
                chunk = (NB // SSD_CB - 1 - t) * SSD_CB + cc
                rows = pl.ds(pl.multiple_of(chunk * BLK, BLK), BLK)
                ci, bi, x = c_ref[rows, :].astype(BF), b_ref[rows, :].astype(BF), x_ref[rows, :]
                dyv = dy_ref[rows, :].astype(F32)
                ab, dtb = ab_ref[rows, :], dtb_ref[rows, :]
                before = ab_ref[pl.ds(pl.multiple_of(jnp.maximum(chunk * BLK - 8, 0), 8), 8), :][7:8, :]
                a_prev = jnp.where(chunk == 0, 0.0, before)
                a_end = ab[BLK - 1:BLK, :]
                alpha = jnp.exp(ab - a_prev)
                e_end = jnp.exp(a_end - ab)
                beta = e_end * dtb
                t_all = alpha[BLK - 1:BLK, :]
                hc = hall_ref[rows, :]
                hcb, dhb = hc.astype(BF), dh_next.astype(BF)

                cb = _dot(ci, bi, NT)
                at, dtj = at_ref[chunk], dt_ref[chunk]
                dcb = jnp.zeros((BLK, BLK), F32)
                dx = jnp.zeros((BLK, LANES), F32)
                rsum = []
                for hd in range(2):
                    dym = jnp.where(hms[hd], dyv, 0.0).astype(BF)
                    lm = _ssd_decay(ab[:, HEAD * hd:HEAD * hd + 1], at[hd:hd + 1, :], causal)
                    dth = dtj[hd:hd + 1, :]
                    dw = _dot(dym, jnp.where(hms[hd], x, 0.0).astype(BF), NT)
                    g = dw * cb * lm
                    dx = dx + _dot((cb * lm * dth).astype(BF), dym, TN)
                    gcol = jnp.sum(g, axis=0, keepdims=True)
                    ddt_ref[chunk, hd:hd + 1, :] = gcol
                    dak_ref[chunk, hd:hd + 1, :] = gcol * dth
                    rsum.append(jnp.sum(g * dth, axis=1, keepdims=True))
                    dcb = dcb + dw * lm * dth
                dcb = dcb.astype(BF)

                g1 = (alpha * dyv).astype(BF)
                dalpha = dyv * _dot(ci, hcb, NN)
                g2 = _dot(bi, dhb, NN)
                dbeta = x * g2
                bx = (beta * x).astype(BF)
                dc_ref[rows, :] = _dot(dcb, bi, NN) + _dot(g1, hcb, NT)
                db_ref[rows, :] = _dot(dcb, ci, TN) + _dot(bx, dhb, NT)
                dx_ref[rows, :] = dx + beta * g2
                ddtb_ref[rows, :] = _seg_sum(e_end * dbeta, HEAD)
                ada, bdb = alpha * dalpha, beta * dbeta
                t_dt = t_all * jnp.sum(dh_next * hc, axis=0, keepdims=True)
                end_term = _seg_sum(jnp.broadcast_to(jnp.sum(bdb, axis=0, keepdims=True) + t_dt, (8, LANES)), HEAD)[0:1]
                prev_term = _seg_sum(jnp.broadcast_to(jnp.sum(ada, axis=0, keepdims=True) + t_dt, (8, LANES)), HEAD)[0:1]
                daq = jnp.where(half0, rsum[0], rsum[1]) + _seg_sum(ada - bdb, HEAD)
                daq_ref[rows, :] = daq + jnp.where(row == BLK - 1, end_term - carry, 0.0)
                carry = prev_term
                dh_next = t_all * dh_next + _dot(ci, g1, TN)
            return dh_next, carry

        lax.fori_loop(0, NB // SSD_CB, step, (jnp.zeros((BLK, LANES), F32), jnp.zeros((1, LANES), F32)))

    npair = D_SSM // LANES
    rowvec = pl.BlockSpec((None, NB, 2, BLK), lambda p: (p, 0, 0, 0))
    seq = pl.BlockSpec((S, LANES), lambda p: (0, p))
    return pl.pallas_call(
        kern, name="ssd_bwd", grid=(npair,),
        in_specs=[pl.BlockSpec((S, LANES), lambda p: (0, 12 + p // 2)),
                  pl.BlockSpec((S, LANES), lambda p: (0, 8 + p // 2)),
                  seq, seq, seq, seq, seq, rowvec, rowvec],
        out_specs=[seq, seq, seq, seq, rowvec, rowvec, seq],
        out_shape=[jax.ShapeDtypeStruct((S, D_SSM), F32)] * 4
        + [jax.ShapeDtypeStruct((npair, NB, 2, BLK), F32)] * 2 + [jax.ShapeDtypeStruct((S, D_SSM), F32)],
        compiler_params=_params("parallel"))(xbc, xbc, xbc, dy, h_all, a_b, dt_b, at_r, dt_r)


def _ssd_prep_bwd(daq, dak, ddt_row, ddt_lane, dt, proj, dt_bias128, a_log128):
    def body(i, n, daq_ref, dak_ref, dd_ref, dd2_ref, dt_ref, raw_ref, b_ref, al_ref, draw_ref, dal_ref, db_ref,
             carry):
        @pl.when(i == 0)
        def _():
            carry[...] = jnp.zeros_like(carry)

        a = -jnp.exp(al_ref[...])
        tri = (lax.broadcasted_iota(jnp.int32, (BLK, BLK), 0) <= lax.broadcasted_iota(jnp.int32, (BLK, BLK), 1))
        rev = _dot3(tri.astype(BF), _collect_heads(daq_ref[...]) - dak_ref[...]) + carry[0:1, :]
        carry[...] = jnp.broadcast_to(rev[0:1, :], carry.shape)
        dtv = dt_ref[...]
        draw = (dd_ref[...] + _collect_heads(dd2_ref[...]) + a * rev) * _sigmoid(raw_ref[...] + b_ref[...])
        draw_ref[...] = draw.astype(BF)
        _accum(dal_ref, jnp.sum(dtv * rev, axis=0, keepdims=True) * a, i)
        _accum(db_ref, jnp.sum(draw, axis=0, keepdims=True), i)

    return _rowwise(body, name="ssd_prep_bwd", nrows=S, tile=BLK, reverse=True,
                    row_ins=[(daq, D_SSM, 0), (dak, LANES, 0), (ddt_row, LANES, 0), (ddt_lane, D_SSM, 0), (dt, LANES, 0),
                             (proj, LANES, DT_COLBLK)],
                    full_ins=[dt_bias128, a_log128], row_outs=[(LANES, BF)],
                    acc_outs=[((1, LANES), F32), ((1, LANES), F32)], scratch=[pltpu.VMEM((8, LANES), F32)])


def _conv_bwd(proj, conv_w, conv_b, d1, d2, map1, map2, col0, ntiles, name):
    base = (4 * D_ATTN + col0) // LANES

    def kern(x_ref, w_ref, b_ref, d1_ref, d2_ref, dx_ref, dw_ref, db_ref):
        x = x_ref[...]
        c, shifted = _conv_taps(x, w_ref)
        c = c + b_ref[...]
        sg = _sigmoid(c)
        dc = (d1_ref[...] + d2_ref[...]) * (sg * (1.0 + c * (1.0 - sg)))
        rows = lax.broadcasted_iota(jnp.int32, x.shape, 0)
        dx = jnp.zeros_like(x)
        for w in range(CONV_W):
            k = CONV_W - 1 - w
            up = dc if k == 0 else jnp.where(rows < S - k, pltpu.roll(dc, S - k, 0), 0.0)
            dx = dx + up * w_ref[w:w + 1, :]
            dw_ref[w:w + 1, :] = jnp.sum(dc * shifted[w], axis=0, keepdims=True)
        dx_ref[...] = dx.astype(BF)
        db_ref[...] = jnp.sum(dc, axis=0, keepdims=True)

    c0 = col0 // LANES
    return pl.pallas_call(
        kern, name=name, grid=(ntiles,),
        in_specs=[pl.BlockSpec((S, LANES), lambda c: (0, base + c)),
                  pl.BlockSpec((CONV_W, LANES), lambda c: (0, c0 + c)),
                  pl.BlockSpec((1, LANES), lambda c: (0, c0 + c)),
                  pl.BlockSpec((S, LANES), lambda c: (0, map1(c))),
                  pl.BlockSpec((S, LANES), lambda c: (0, map2(c)))],
        out_specs=[pl.BlockSpec((S, LANES), lambda c: (0, c)), pl.BlockSpec((CONV_W, LANES), lambda c: (0, c)),
                   pl.BlockSpec((1, LANES), lambda c: (0, c))],
        out_shape=[jax.ShapeDtypeStruct((S, ntiles * LANES), BF), jax.ShapeDtypeStruct((CONV_W, ntiles * LANES), F32),
                   jax.ShapeDtypeStruct((1, ntiles * LANES), F32)],
        compiler_params=_params("parallel"))(proj, conv_w, conv_b, d1, d2)


def _adamw(w, m, v, parts, name):
    r, c = w.shape
    n_parts = parts.shape[0]
    tile = r if r <= 512 else (128 if c > 1024 else 256)
    assert r % tile == 0
    c1 = 1.0 - ADAM_B1 ** ADAM_STEP
    c2 = 1.0 - ADAM_B2 ** ADAM_STEP

    def kern(w_ref, m_ref, v_ref, p_ref, g_ref, d_ref, mo_ref, vo_ref):
        g = p_ref[0].astype(F32)
        for k in range(1, n_parts):
            g = g + p_ref[k].astype(F32)
        mn = ADAM_B1 * m_ref[...] + (1.0 - ADAM_B1) * g
        vn = ADAM_B2 * v_ref[...] + (1.0 - ADAM_B2) * (g * g)
        g_ref[...] = g
        mo_ref[...] = mn
        vo_ref[...] = vn
        d_ref[...] = -ADAM_LR * ((mn / c1) / (jnp.sqrt(vn / c2) + ADAM_EPS) + ADAM_WD * w_ref[...])

    blk = pl.BlockSpec((tile, c), lambda i: (i, 0))
    return pl.pallas_call(
        kern, name=name, grid=(r // tile,),
        in_specs=[blk, blk, blk, pl.BlockSpec((n_parts, tile, c), lambda i: (0, i, 0))],
        out_specs=[blk] * 4, out_shape=[jax.ShapeDtypeStruct((r, c), F32)] * 4,
        compiler_params=_params("parallel"))(w, m, v, parts)


MESH = pl.DeviceIdType.MESH
ANY = pl.BlockSpec(memory_space=pl.ANY)


def _put_own(gathered, shard):
    me = 4 * lax.axis_index("x") + 2 * lax.axis_index("y") + lax.axis_index("c")
    return lax.dynamic_update_slice(gathered, shard[None], (me,) + (0,) * shard.ndim)


def _all_gather(arrs, name, after=None):
    na = len(arrs)
    n_after = 0 if after is None else 1

    def kern(*refs):
        ins, outs = refs[:na], refs[na + n_after:2 * na + n_after]
        send_sems, recv_sems = refs[2 * na + n_after:]
        x, y, c = lax.axis_index("x"), lax.axis_index("y"), lax.axis_index("c")
        me, sibling = (x, y, c), (x, y, 1 - c)
        a_chip = (jnp.where(c == 0, 1 - x, x), jnp.where(c == 0, y, 1 - y))
        b_chip = (jnp.where(c == 0, x, 1 - x), jnp.where(c == 0, 1 - y, y))
        chips = [a_chip, b_chip, (1 - x, 1 - y)]

        def copy(a, k, block, to, src=None):
            px, py, pc = block
            dst = outs[a].at[4 * px + 2 * py + pc]
            return pltpu.make_async_remote_copy(
                src_ref=dst if src is None else src, dst_ref=dst, send_sem=send_sems.at[a, k],
                recv_sem=recv_sems.at[a, k], device_id=to, device_id_type=MESH)

        sends = []
        for a in range(na):
            sends.append(copy(a, 0, me, sibling, src=ins[a]))
            sends += [copy(a, 1 + j, me, (*chips[j], c), src=ins[a]) for j in range(2)]
        for cp in sends:
            cp.start()
        for a in range(na):
            for j, chip in enumerate(chips):
                copy(a, 1 + j, (*chip, c), me).wait_recv()
                later = [copy(a, 4 + j, (*chip, c), sibling)]
                if j == 0:
                    later.append(copy(a, 3, (*a_chip, c), (*b_chip, c)))
                for cp in later:
                    cp.start()
                sends += later
        for a in range(na):
            copy(a, 0, sibling, me).wait_recv()
            for j, chip in enumerate(chips):
                copy(a, 4 + j, (*chip, 1 - c), me).wait_recv()
        for cp in sends:
            cp.wait_send()

    outs = pl.pallas_call(
        kern, name=name, in_specs=[ANY] * (na + n_after), out_specs=[ANY] * na,
        out_shape=[jax.ShapeDtypeStruct((N_DEV,) + a.shape, a.dtype) for a in arrs],
        scratch_shapes=[pltpu.SemaphoreType.DMA((na, 7)), pltpu.SemaphoreType.DMA((na, 7))],
    )(*arrs, *([] if after is None else [after]))
    return [_put_own(o, a) for o, a in zip(outs, arrs)]


HBM = pl.BlockSpec(memory_space=pltpu.HBM)
SEM = pl.BlockSpec(memory_space=pltpu.SEMAPHORE)
EFFECT = pltpu.SideEffectType.DATAFLOW_SIDE_EFFECTING
N_CHIP = 4
N_COPIES = {"gather": 3, "scatter": 3, "forward": 4, "pair": 4}


def _in_hbm(a):
    return pltpu.with_memory_space_constraint(a, pltpu.HBM)


def _chip_copies(kind, src_refs, land_refs, send_sems, recv_sems):
    x, y, c = lax.axis_index("x"), lax.axis_index("y"), lax.axis_index("c")
    chips = [(1 - x, y), (x, 1 - y), (1 - x, 1 - y)]
    n = N_COPIES[kind]
    cps = []

    def add(a, j, src, dst, to):
        cps.append(pltpu.make_async_remote_copy(
            src_ref=src, dst_ref=dst, send_sem=send_sems.at[n * a + j], recv_sem=recv_sems.at[n * a + j],
            device_id=to, device_id_type=MESH))

    for a in range(len(src_refs)):
        if kind == "gather":
            for j, (px, py) in enumerate(chips):
                add(a, j, src_refs[a], land_refs[a].at[4 * x + 2 * y + c], (px, py, c))
        elif kind == "scatter":
            for j, (px, py) in enumerate(chips):
                add(a, j, src_refs[a].at[2 * px + py], land_refs[a].at[2 * x + y], (px, py, c))
        elif kind == "forward":
            add(a, 0, src_refs[a], land_refs[a].at[4 * x + 2 * y + c], (x, y, 1 - c))
            for j, (px, py) in enumerate(chips):
                slot = land_refs[a].at[4 * px + 2 * py + c]
                add(a, 1 + j, slot, slot, (x, y, 1 - c))
        else:
            for q in range(N_CHIP):
                add(a, q, src_refs[a].at[2 * q + 1 - c], land_refs[a].at[q], (x, y, 1 - c))
    return cps


def _exchange_start(kind, srcs, lands, after, name):
    na = len(srcs)
    n_after = 0 if after is None else 1

    def kern(*refs):
        src_refs, land_refs = refs[:na], refs[na:2 * na]
        send_sems, recv_sems = refs[2 * na + n_after], refs[2 * na + n_after + 1]
        token = refs[-1]
        for cp in _chip_copies(kind, src_refs, land_refs, send_sems, recv_sems):
            cp.start()
        token[...] = jnp.zeros_like(token)

    bufs = list(srcs) + list(lands)
    res = pl.pallas_call(
        kern, name=name,
        out_shape=(pltpu.SemaphoreType.DMA((N_COPIES[kind] * na,)), pltpu.SemaphoreType.DMA((N_COPIES[kind] * na,)))
        + tuple(pltpu.HBM(b.shape, b.dtype) for b in bufs) + (jax.ShapeDtypeStruct((8, LANES), F32),),
        in_specs=[HBM] * (2 * na) + [ANY] * n_after,
        out_specs=(SEM, SEM) + (HBM,) * (2 * na) + (pl.BlockSpec(memory_space=pltpu.VMEM),),
        input_output_aliases={i: 2 + i for i in range(2 * na)},
        compiler_params=pltpu.CompilerParams(has_side_effects=EFFECT),
    )(*[_in_hbm(b) for b in bufs], *([] if after is None else [after]))
    return (res[0], res[1]), list(res[2:2 + na]), list(res[2 + na:2 + 2 * na]), res[-1]


def _exchange_wait(kind, sems, srcs, lands, after, name):
    na = len(srcs)

    def kern(*refs):
        src_refs, land_refs = refs[:na], refs[na:2 * na]
        send_sems, recv_sems = refs[2 * na], refs[2 * na + 1]
        for cp in _chip_copies(kind, src_refs, land_refs, send_sems, recv_sems):
            cp.wait_send()
            cp.wait_recv()

    bufs = list(srcs) + list(lands)
    res = pl.pallas_call(
        kern, name=name, out_shape=tuple(pltpu.HBM(b.shape, b.dtype) for b in bufs),
        in_specs=[HBM] * (2 * na) + [SEM, SEM, ANY], out_specs=(HBM,) * (2 * na),
        input_output_aliases={i: i for i in range(2 * na)},
        compiler_params=pltpu.CompilerParams(has_side_effects=EFFECT),
    )(*bufs, sems[0], sems[1], after)
    return list(res[:na]), list(res[na:])


def _pair_exchange(parts, name):
    na = len(parts)

    def kern(*refs):
        ins, got = refs[:na], refs[na:2 * na]
        send_sems, recv_sems = refs[2 * na:]
        x, y, c = lax.axis_index("x"), lax.axis_index("y"), lax.axis_index("c")
        sends = []
        for a in range(na):
            for q in range(N_CHIP):
                sends.append(pltpu.make_async_remote_copy(
                    src_ref=ins[a].at[2 * q + 1 - c], dst_ref=got[a].at[q], send_sem=send_sems.at[a, q],
                    recv_sem=recv_sems.at[a, q], device_id=(x, y, 1 - c), device_id_type=MESH))
        for cp in sends:
            cp.start()
        for cp in sends:
            cp.wait()

    return pl.pallas_call(
        kern, name=name, in_specs=[ANY] * na, out_specs=[ANY] * na,
        out_shape=[jax.ShapeDtypeStruct((N_CHIP,) + p.shape[1:], p.dtype) for p in parts],
        scratch_shapes=[pltpu.SemaphoreType.DMA((na, N_CHIP)), pltpu.SemaphoreType.DMA((na, N_CHIP))])(*parts)


def _pair_sum(parts, got, name):
    _, r, cdim = parts.shape
    tile = min(r, ROW_TILE)
    x, y, c = lax.axis_index("x"), lax.axis_index("y"), lax.axis_index("c")
    where = jnp.stack([c, 2 * x + y]).astype(jnp.int32)

    def kern(w_ref, a_ref, b_ref, q_ref, own_ref):
        s = (a_ref[...].astype(F32) + b_ref[...].astype(F32)).astype(BF)
        q_ref[...] = s

        @pl.when(pl.program_id(1) == w_ref[1])
        def _():
            own_ref[...] = s

    blk = pl.BlockSpec((None, tile, cdim), lambda i, q, w_ref: (q, i, 0))
    sums, own = pl.pallas_call(
        kern, name=name,
        out_shape=[jax.ShapeDtypeStruct((N_CHIP, r, cdim), BF), jax.ShapeDtypeStruct((r, cdim), BF)],
        grid_spec=pltpu.PrefetchScalarGridSpec(
            num_scalar_prefetch=1, grid=(r // tile, N_CHIP),
            in_specs=[pl.BlockSpec((None, tile, cdim), lambda i, q, w_ref: (2 * q + w_ref[0], i, 0)), blk],
            out_specs=[blk, pl.BlockSpec((tile, cdim), lambda i, q, w_ref: (i, 0))]),
        compiler_params=_params("parallel", "arbitrary"))(where, parts, got)
    land = lax.dynamic_update_slice(lax.empty((N_CHIP, r, cdim), BF), own[None], (2 * x + y, 0, 0))
    return sums, land


W_IN_COLS = D_IN // N_DEV


def _w_in_from_blocks(w8):
    def kern(x_ref, o_ref):
        for j in range(N_DEV):
            o_ref[:, W_IN_COLS * j:W_IN_COLS * (j + 1)] = x_ref[j]
        o_ref[:, D_IN:] = jnp.zeros((ROW_TILE, D_INP - D_IN), o_ref.dtype)

    return pl.pallas_call(
        kern, name="w_in_from_blocks", grid=(D // ROW_TILE,),
        in_specs=[pl.BlockSpec((N_DEV, ROW_TILE, W_IN_COLS), lambda i: (0, i, 0))],
        out_specs=pl.BlockSpec((ROW_TILE, D_INP), lambda i: (i, 0)),
        out_shape=jax.ShapeDtypeStruct((D, D_INP), w8.dtype), compiler_params=_params("parallel"))(w8)


def _w_in_to_blocks(g):
    def kern(x_ref, o_ref):
        for j in range(N_DEV):
            o_ref[j] = x_ref[:, W_IN_COLS * j:W_IN_COLS * (j + 1)]

    return pl.pallas_call(
        kern, name="w_in_to_blocks", grid=(D // ROW_TILE,),
        in_specs=[pl.BlockSpec((ROW_TILE, D_INP), lambda i: (i, 0))],
        out_specs=pl.BlockSpec((N_DEV, ROW_TILE, W_IN_COLS), lambda i: (0, i, 0)),
        out_shape=jax.ShapeDtypeStruct((N_DEV, D, W_IN_COLS), g.dtype), compiler_params=_params("parallel"))(g)


def _pad_lanes(v, width=LANES):
    return jnp.pad(v, ((0, 0), (0, width - v.shape[1])))


def _row_layout(t16):
    return t16.T.reshape(N_HEADS // 2, 2, NB, BLK).transpose(0, 2, 1, 3)


def _row_layout_inv(r):
    return r.transpose(0, 2, 1, 3).reshape(N_HEADS, S).T


SMALL = (("g_mix", D), ("g_q", HEAD), ("g_k", HEAD), ("g_attn_out", D_ATTN), ("conv_b", D_CONV),
         ("dt_bias", 16), ("a_log", 16), ("d_skip", 16), ("g_ssm_out", D_SSM), ("g_cross", D),
         ("g_mem", D), ("g_cq", CROSS_HEAD), ("g_ck", CROSS_HEAD), ("g_mlp", D))
SMALL_ROWS = -(-sum(-(-w // LANES) for _, w in SMALL) // 8) * 8


def _pack_small(vals):
    rows = [_pad_lanes(vals[n], -(-w // LANES) * LANES).reshape(-1, LANES) for n, w in SMALL]
    packed = jnp.concatenate(rows, axis=0)
    return jnp.pad(packed, ((0, SMALL_ROWS - packed.shape[0]), (0, 0)))


def _unpack_small(packed):
    out, r = {}, 0
    for n, w in SMALL:
        nr = -(-w // LANES)
        out[n] = packed[r:r + nr].reshape(1, nr * LANES)[:, :w]
        r += nr
    return out


BIG = ("w_in", "w_out", "w_cq", "w_ckv", "w_co", "w_up", "w_down")
WEIGHTS = ("g_mix", "w_in", "g_q", "g_k", "g_attn_out", "conv_w", "conv_b", "dt_bias", "a_log", "d_skip",
           "g_ssm_out", "w_out", "g_cross", "g_mem", "w_cq", "w_ckv", "g_cq", "g_ck", "w_co", "g_mlp", "w_up", "w_down")


REST = ("w_out", "w_cq", "w_ckv", "w_co", "w_up", "w_down")


class _Overlap:
    def __init__(self, shards, after):
        self.gather, self.forward = {}, {}
        self.names = {"a": REST[:4], "b": REST[4:5], "c": REST[5:]}
        for tag, names in self.names.items():
            lands = [_put_own(lax.empty((N_DEV,) + shards[n].shape, BF), shards[n]) for n in names]
            self.gather[tag] = _exchange_start("gather", [shards[n] for n in names], lands, after, "ag_start_" + tag)
            after = self.gather[tag][3]
        self.token = after
        self.groups = []
        self.pairs = {}

    def rest_mid(self, tag, after):
        sems, srcs, lands, _ = self.gather[tag]
        srcs, lands = _exchange_wait("gather", sems, srcs, lands, after, "ag_wait_" + tag)
        self.forward[tag] = _exchange_start("forward", srcs, lands, None, "ag_fwd_start_" + tag)
        return self.forward[tag][3]

    def rest(self, tag, after):
        sems, srcs, lands, _ = self.forward[tag]
        _, lands = _exchange_wait("forward", sems, srcs, lands, after, "ag_fwd_wait_" + tag)
        wf = dict(zip(self.names[tag], lands))
        for n in wf:
            if n in ("w_out", "w_cq", "w_ckv", "w_down"):
                wf[n] = wf[n].reshape(-1, wf[n].shape[-1])
        return wf

    def pair_start(self, name, grad):
        got = lax.empty((N_CHIP,) + grad.shape[1:], grad.dtype)
        self.pairs[name] = _exchange_start("pair", [grad], [got], None, "rs_pair_start_" + name)
        return self.pairs[name][3]

    def emit(self, grads, after):
        names, tag = list(grads), "_%d" % len(self.groups)
        grads, got = dict(grads), {}
        for n in names:
            if n in self.pairs:
                sems, srcs, lands, _ = self.pairs[n]
                srcs, lands = _exchange_wait("pair", sems, srcs, lands, after, "rs_pair_wait_" + n)
                grads[n], got[n] = srcs[0], lands[0]
        sync = [n for n in names if n not in got]
        if sync:
            got.update(zip(sync, _pair_exchange([grads[n] for n in sync], "rs_pair" + tag)))
        sums = [_pair_sum(grads[n], got[n], "rs_sum_" + n) for n in names]
        sems, srcs, lands, token = _exchange_start("scatter", [q for q, _ in sums], [l for _, l in sums], None,
                                                   "rs_chip_start" + tag)
        self.groups.append((names, sems, srcs, lands))
        return token

    def finish(self, gi, after):
        names, sems, srcs, lands = self.groups[gi]
        _, lands = _exchange_wait("scatter", sems, srcs, lands, after, "rs_chip_wait_%d" % gi)
        return dict(zip(names, lands))


def _tie(v, token):
    return v if token is None else v + token[0:1, 0:1]


def _local_step(x, mem, pos_col, target, p, wf, hooks=None):
    p = dict(p)
    inv = np.zeros((1, LANES), np.float32)
    freq = (ROPE_THETA ** (-2.0 * np.arange(ROT // 2, dtype=np.float32) / ROT)).astype(np.float32)
    for l in range(LANES):
        if l % HEAD < ROT:
            inv[0, l] = freq[(l % HEAD) % (ROT // 2)]
    inv_tab = jnp.asarray(inv)
    gq128, gk128 = jnp.tile(p["g_q"], (1, 2)), jnp.tile(p["g_k"], (1, 2))
    dtb128, alog128 = _pad_lanes(p["dt_bias"]), _pad_lanes(p["a_log"])
    dskip_b = jnp.repeat(p["d_skip"], HEAD, axis=1)
    conv_w, conv_b = wf["conv_w"], p["conv_b"]

    h = _rms_fwd(x, p["g_mix"], "rms_mix")
    proj = _matmul(h, wf["w_in"], mode="nn", name="mm_in", tm=1024, tn=1280, tk=D)
    qr, kr = _qk_prep(proj, pos_col, gq128, gk128, inv_tab)
    attn, lse = _attn_fwd(qr, kr, proj)
    attn_n = _attn_norm(attn, p["g_attn_out"])

    xbc = _conv_fwd(proj, conv_w, conv_b)
    token = None if hooks is None else hooks.rest_mid("a", xbc)
    dt, a_cs, dt_b, a_b = _ssd_prep(proj, _tie(dtb128, token), alog128)
    at_r, dt_r = _row_layout(a_cs[:, :N_HEADS]), _row_layout(dt[:, :N_HEADS])
    y_ssd, h_all = _ssd_fwd(xbc, a_b, dt_b, at_r, dt_r)
    ssm_n = _ssd_post(y_ssd, xbc, proj, dskip_b, p["g_ssm_out"])

    if hooks is not None:
        wf = {**wf, **hooks.rest("a", ssm_n)}
    mix = jnp.concatenate([attn_n, ssm_n], axis=1)
    x1 = _matmul(mix, wf["w_out"], mode="nn", name="mm_out", tm=1024, tn=1024, tk=D,
                 extras=(x,), epilogue=lambda acc, r: (acc + r,))
    hc = _rms_fwd(x1, _tie(p["g_cross"], None if hooks is None else hooks.rest_mid("b", x1)), "rms_cross")
    memh = _rms_fwd(mem, p["g_mem"], "rms_mem")
    qc = _matmul(hc, wf["w_cq"], mode="nn", name="mm_cq", tm=1024, tn=512, tk=D)
    kv = _matmul(memh, wf["w_ckv"], mode="nn", name="mm_ckv", tm=N_MEM, tn=1024, tk=D)
    oc = _cross_fwd(qc, kv, p["g_cq"], p["g_ck"])
    x2 = _matmul(oc, wf["w_co"], mode="nn", name="mm_co", tm=1024, tn=256, tk=512, b_cb=256,
                 extras=(x1,), epilogue=lambda acc, r: (acc + r,))
    hm = _rms_fwd(x2, p["g_mlp"], "rms_mlp")
    if hooks is not None:
        wf = {**wf, **hooks.rest("b", hm)}

    def up_epi(acc):
        ru = jnp.maximum(acc, 0.0)
        return ru * ru, 2.0 * ru

    act, relu2 = _matmul(hm, wf["w_up"], mode="nn", name="mm_up", tm=1024, tn=1024, tk=D, b_cb=1024,
                         out_dtypes=(BF, BF), epilogue=up_epi)
    if hooks is not None:
        hooks.rest_mid("c", act)
        wf = {**wf, **hooks.rest("c", relu2)}

    def loss_epi(acc, r, t):
        d = (acc + r - t) * (1.0 / D)
        return d, d

    dy, dyb = _matmul(act, wf["w_down"], mode="nn", name="mm_down", tm=512, tn=1024, tk=2048, extras=(x2, target),
                      out_dtypes=(F32, BF), epilogue=loss_epi)

    gb, gs = {}, {}
    gb["w_down"] = _matmul(act, dyb, mode="tn", name="mm_down_dw", tm=1024, tn=1024, tk=S,
                           out_dtypes=(BF,)).reshape(N_DEV, D_FF // N_DEV, D)
    token = None if hooks is None else hooks.pair_start("w_down", gb["w_down"])
    du = _matmul(dyb, wf["w_down"], mode="nt", name="mm_down_dx", tm=1024, tn=1024, tk=D, extras=(relu2,),
                 out_dtypes=(BF,), epilogue=lambda acc, r: (acc * r.astype(F32),), after=token)
    gb["w_up"] = _matmul(hm, du, mode="tn", name="mm_up_dw", tm=1024, tn=1024, tk=S, out_dtypes=(BF,), out_cb=1024)
    token = None if hooks is None else hooks.pair_start("w_up", gb["w_up"])
    dhm = _matmul(du, wf["w_up"], mode="nt", name="mm_up_dx", tm=1024, tn=1024, tk=2048, b_cb=1024, after=token)
    if hooks is not None:
        p["g_mlp"] = _tie(p["g_mlp"], hooks.emit({n: gb[n] for n in ("w_down", "w_up")}, dhm))
    dx2, dx2b, gs["g_mlp"], sumsq_dy = _rms_bwd_call(x2, p["g_mlp"], dhm, dy, "rms_mlp_bwd", sumsq_res=True)
    loss_part = sumsq_dy[0, 0] * (0.5 * D)

    doc = _matmul(dx2b, wf["w_co"], mode="nt", name="mm_co_dx", tm=1024, tn=512, tk=256, b_cb=256)
    gb["w_co"] = _matmul(oc, dx2b, mode="tn", name="mm_co_dw", tm=512, tn=256, tk=S, out_dtypes=(BF,), out_cb=256)
    dqc, dkn, dvc, gs["g_cq"] = _cross_bwd(qc, doc, kv, p["g_cq"], p["g_ck"])
    dkv, gs["g_ck"] = _cross_kv_bwd(kv, dkn, dvc, p["g_ck"])
    gb["w_cq"] = _matmul(hc, dqc, mode="tn", name="mm_cq_dw", tm=1024, tn=512, tk=S,
                         out_dtypes=(BF,)).reshape(N_DEV, D // N_DEV, D_CROSS)
    dhc = _matmul(dqc, wf["w_cq"], mode="nt", name="mm_cq_dx", tm=1024, tn=1024, tk=512)
    gb["w_ckv"] = _matmul(memh, dkv, mode="tn", name="mm_ckv_dw", tm=1024, tn=1024, tk=N_MEM,
                          out_dtypes=(BF,)).reshape(N_DEV, D // N_DEV, 2 * D_CROSS)
    dmemh = _matmul(dkv, wf["w_ckv"], mode="nt", name="mm_ckv_dx", tm=N_MEM, tn=1024, tk=1024)
    (gs["g_mem"],) = _rms_bwd_call(mem, p["g_mem"], dmemh, None, "rms_mem_bwd")
    dx1, dx1b, gs["g_cross"] = _rms_bwd_call(x1, p["g_cross"], dhc, dx2, "rms_cross_bwd")

    dmix = _matmul(dx1b, wf["w_out"], mode="nt", name="mm_out_dx", tm=1024, tn=1024, tk=D)
    gb["w_out"] = _matmul(mix, dx1b, mode="tn", name="mm_out_dw", tm=1024, tn=1024, tk=S,
                          out_dtypes=(BF,)).reshape(N_DEV, D // N_DEV, D)
    if hooks is not None:
        p["g_attn_out"] = _tie(p["g_attn_out"],
                               hooks.emit({n: gb[n] for n in ("w_co", "w_cq", "w_ckv", "w_out")}, dmix))

    dattn, delta, gs["g_attn_out"] = _attn_merge_bwd(dmix, attn, p["g_attn_out"])
    dq_rot, dk_rot, dv_attn = _attn_bwd(qr, kr, proj, dattn, lse, delta)
    dq_pre, dk_pre, dv_pre, dgq, dgk = _qk_bwd(dq_rot, dk_rot, dv_attn, proj, pos_col, gq128, gk128, inv_tab)
    gs["g_q"], gs["g_k"] = dgq[:, :HEAD], dgk[:, :HEAD]

    dy_ssd, dz, dxs_skip, dd_lane, gs["g_ssm_out"] = _ssd_post_bwd(y_ssd, xbc, proj, dmix, dskip_b, p["g_ssm_out"])
    gs["d_skip"] = dd_lane.reshape(N_HEADS, HEAD).sum(axis=1).reshape(1, N_HEADS)
    dc_pair, daq_b, dx_ssd, db_pair, dak_r, ddt_r, ddt_b = _ssd_bwd(xbc, dy_ssd, h_all, a_b, dt_b, at_r, dt_r)
    dak = _pad_lanes(_row_layout_inv(dak_r))
    ddt_direct = _pad_lanes(_row_layout_inv(ddt_r))
    ddt_raw, dalog, dbias = _ssd_prep_bwd(daq_b, dak, ddt_direct, ddt_b, dt, proj, dtb128, alog128)
    gs["a_log"], gs["dt_bias"] = dalog[:, :N_HEADS], dbias[:, :N_HEADS]
    same = lambda c: c
    dxbc_x, dcw_x, dcb_x = _conv_bwd(proj, conv_w, conv_b, dxs_skip, dx_ssd, same, same, 0, 8, "conv_bwd_x")
    dxbc_b, dcw_b, dcb_b = _conv_bwd(proj, conv_w, conv_b, db_pair, db_pair, lambda c: 2 * c, lambda c: 2 * c + 1,
                                     D_SSM, 4, "conv_bwd_b")
    dxbc_c, dcw_c, dcb_c = _conv_bwd(proj, conv_w, conv_b, dc_pair, dc_pair, lambda c: 2 * c, lambda c: 2 * c + 1,
                                     D_SSM + 512, 4, "conv_bwd_c")
    g_conv_w = jnp.concatenate([dcw_x, dcw_b, dcw_c], axis=1)
    gs["conv_b"] = jnp.concatenate([dcb_x, dcb_b, dcb_c], axis=1)

    pieces = [dq_pre, dk_pre, dv_pre, dz, dxbc_x, dxbc_b, dxbc_c, ddt_raw]
    pieces.append(jnp.zeros((S, D_INP - sum(t.shape[1] for t in pieces)), BF))
    dproj = jnp.concatenate(pieces, axis=1)
    dw_in = _matmul(h, dproj, mode="tn", name="mm_in_dw", tm=1024, tn=1280, tk=S, out_dtypes=(BF,))
    gb["w_in"] = _w_in_to_blocks(dw_in)
    token = None if hooks is None else hooks.emit({"w_in": gb["w_in"]}, dw_in)
    dh = _matmul(dproj, wf["w_in"], mode="nt", name="mm_in_dx", tm=1024, tn=512, tk=D_INP // 2, after=token)
    grad_x, _, gs["g_mix"] = _rms_bwd_call(x, p["g_mix"], dh, dx1, "rms_mix_bwd")
    return loss_part, grad_x, gb, gs, g_conv_w


def kernel(x, mem, positions, g_mix, w_in, g_q, g_k, g_attn_out, conv_w, conv_b, dt_bias, a_log, d_skip, g_ssm_out, w_out, g_cross, g_mem, w_cq, w_ckv, g_cq, g_ck, w_co, g_mlp, w_up, w_down, loss_target, m_g_mix, m_w_in, m_g_q, m_g_k, m_g_attn_out, m_conv_w, m_conv_b, m_dt_bias, m_a_log, m_d_skip, m_g_ssm_out, m_w_out, m_g_cross, m_g_mem, m_w_cq, m_w_ckv, m_g_cq, m_g_ck, m_w_co, m_g_mlp, m_w_up, m_w_down, v_g_mix, v_w_in, v_g_q, v_g_k, v_g_attn_out, v_conv_w, v_conv_b, v_dt_bias, v_a_log, v_d_skip, v_g_ssm_out, v_w_out, v_g_cross, v_g_mem, v_w_cq, v_w_ckv, v_g_cq, v_g_ck, v_w_co, v_g_mlp, v_w_up, v_w_down):
    args = dict(locals())
    w = {n: args[n] for n in WEIGHTS}
    m = {n: args["m_" + n] for n in WEIGHTS}
    v = {n: args["v_" + n] for n in WEIGHTS}
    small = {n: w[n] for n, _ in SMALL}
    me = 4 * lax.axis_index("x") + 2 * lax.axis_index("y") + lax.axis_index("c")

    w_in_g, conv_w_g = _all_gather([w["w_in"][0].astype(BF), w["conv_w"][0]], "ag_first")
    wf = {"w_in": _w_in_from_blocks(w_in_g), "conv_w": conv_w_g.transpose(1, 0, 2).reshape(CONV_W, D_CONV)}
    overlap = _Overlap({n: w[n][0].astype(BF) for n in REST}, w_in_g)
    p = dict(small)
    p["g_mix"] = _tie(p["g_mix"], overlap.token)

    loss_part, grad_x, _, gs, g_conv_w = _local_step(
        x[0], mem[0], positions.reshape(S, 1), loss_target[0], p, wf, overlap)
    loss = lax.psum(loss_part, ("x", "y", "c"))

    out = {}
    last = grad_x
    for gi in range(3):
        for n, parts in overlap.finish(gi, last).items():
            res_n = _adamw(w[n][0], m[n][0], v[n][0], parts, "adamw_" + n)
            out[n] = [t.reshape(w[n].shape) for t in res_n]
            last = res_n[0]

    sm_parts, cw_parts = _all_gather([_pack_small(gs), g_conv_w], "ag_small_grads", after=last)
    sm = [_unpack_small(t) for t in _adamw(_pack_small(small), _pack_small({n: m[n] for n, _ in SMALL}),
                                           _pack_small({n: v[n] for n, _ in SMALL}), sm_parts, "adamw_small")]
    for n, _ in SMALL:
        out[n] = [t[n] for t in sm]
    cols = D_CONV // N_DEV
    cw_mine = lax.dynamic_slice_in_dim(cw_parts, me * cols, cols, axis=2)
    out["conv_w"] = [t.reshape(w["conv_w"].shape) for t in
                     _adamw(w["conv_w"][0], m["conv_w"][0], v["conv_w"][0], cw_mine, "adamw_conv_w")]

    res = [loss, grad_x.reshape(x.shape)]
    for k in range(4):
        res += [out[n][k] for n in WEIGHTS]
    return tuple(res)
```

```python
import functools
import math

import numpy as np
import jax
import jax.numpy as jnp
from jax import lax
from jax.experimental import pallas as pl
from jax.experimental.pallas import tpu as pltpu

F32 = jnp.float32
BF = jnp.bfloat16

N_DEV = 8
S = 2048
D = 2048
D_ATTN = 1024
N_HEADS = 16
HEAD = 64
ROT = 16
ROPE_THETA = 500000.0
D_SSM = 1024
D_CONV = 2048
CONV_W = 4
N_MEM = 256
D_CROSS = 512
CROSS_HEAD = 128
D_FF = 8192
D_IN = 6160
D_INP = 6400
DT_COLBLK = (4 * D_ATTN + D_CONV) // 128
EPS = 1e-6
BLK = 128
NB = S // BLK
LANES = 128
NEG = -1e30

ADAM_LR = 0.001
ADAM_B1 = 0.9
ADAM_B2 = 0.999
ADAM_EPS = 1e-08
ADAM_WD = 0.01
ADAM_STEP = 10

VMEM_LIMIT_BYTES = 48 * 1024 * 1024
ROW_TILE = 256


def _params(*sem):
    return pltpu.CompilerParams(dimension_semantics=sem, vmem_limit_bytes=VMEM_LIMIT_BYTES)


def _matmul(a, b, *, mode, name, tm, tn, tk, out_dtypes=(F32,), b_cb=None, out_cb=None,
            extras=(), epilogue=None, after=None):
    if mode == "tn":
        kk, m = a.shape
    else:
        m, kk = a.shape
    if b_cb is None:
        br, bc = b.shape
    else:
        br, bc = b.shape[1], N_DEV * b_cb
    n = br if mode == "nt" else bc
    assert m % tm == 0 and n % tn == 0 and kk % tk == 0, (name, m, n, kk)
    nk = kk // tk
    grid = (m // tm, n // tn, nk)

    if mode == "tn":
        a_spec = pl.BlockSpec((tk, tm), lambda i, j, k: (k, i))
    else:
        a_spec = pl.BlockSpec((tm, tk), lambda i, j, k: (i, k))
    if mode == "nt":
        b_blk, b_idx = (tn, tk), (lambda i, j, k: (j, k))
    else:
        b_blk, b_idx = (tk, tn), (lambda i, j, k: (k, j))
    group = 1
    if b_cb is None:
        b_spec = pl.BlockSpec(b_blk, b_idx)
    elif mode == "nt" and tk > b_cb:
        assert tk % b_cb == 0
        group = tk // b_cb
        b_spec = pl.BlockSpec((group, tn, b_cb), lambda i, j, k: (k, j, 0))
    else:
        tc = b_blk[1]
        assert b_cb % tc == 0
        per = b_cb // tc

        def b_idx3(i, j, k):
            r, c = b_idx(i, j, k)
            return (c // per, r, c % per)
        b_spec = pl.BlockSpec((None,) + b_blk, b_idx3)
    ex_specs = [pl.BlockSpec((tm, tn), lambda i, j, k: (i, j)) for _ in extras]
    if out_cb is None:
        out_shape = [jax.ShapeDtypeStruct((m, n), dt) for dt in out_dtypes]
        out_specs = [pl.BlockSpec((tm, tn), lambda i, j, k: (i, j)) for _ in out_dtypes]
    else:
        assert out_cb % tn == 0
        pero = out_cb // tn
        out_shape = [jax.ShapeDtypeStruct((N_DEV, m, out_cb), dt) for dt in out_dtypes]
        out_specs = [pl.BlockSpec((None, tm, tn), lambda i, j, k: (j // pero, i, j % pero)) for _ in out_dtypes]
    dn = {"nn": (((1,), (0,)), ((), ())), "nt": (((1,), (1,)), ((), ())), "tn": (((0,), (0,)), ((), ()))}[mode]
    ne, no = len(extras), len(out_dtypes)
    n_after = 0 if after is None else 1

    def kern(a_ref, b_ref, *rest):
        ex_refs, out_refs = rest[:ne], rest[ne + n_after:ne + n_after + no]

        def finish(acc):
            outs = (acc,) if epilogue is None else epilogue(acc, *[r[...] for r in ex_refs])
            for r, o in zip(out_refs, outs):
                r[...] = o.astype(r.dtype)

        if group == 1:
            part = lax.dot_general(a_ref[...].astype(BF), b_ref[...].astype(BF), dn, preferred_element_type=F32)
        else:
            part = sum(lax.dot_general(a_ref[:, g * b_cb:(g + 1) * b_cb].astype(BF), b_ref[g].astype(BF), dn,
                                       preferred_element_type=F32) for g in range(group))
        if nk == 1:
            finish(part)
            return
        acc_ref = rest[ne + n_after + no]
        k = pl.program_id(2)

        @pl.when(k == 0)
        def _():
            acc_ref[...] = part

        @pl.when(k > 0)
        def _():
            acc_ref[...] += part

        @pl.when(k == nk - 1)
        def _():
            finish(acc_ref[...])

    res = pl.pallas_call(
        kern, name=name, grid=grid, in_specs=[a_spec, b_spec] + ex_specs + [ANY] * n_after, out_specs=out_specs,
        out_shape=out_shape, scratch_shapes=[pltpu.VMEM((tm, tn), F32)] if nk > 1 else [],
        compiler_params=_params("parallel", "parallel", "arbitrary"),
    )(a, b, *extras, *([] if after is None else [after]))
    return res[0] if no == 1 else tuple(res)


def _rowwise(body, *, name, nrows, tile, row_ins, full_ins=(), row_outs=(), acc_outs=(), scratch=(),
             reverse=False):
    n = nrows // tile

    def ridx(i):
        return (n - 1 - i) if reverse else i

    in_specs, args = [], []
    for arr, width, cb in row_ins:
        in_specs.append(pl.BlockSpec((tile, width), lambda i, cb=cb: (ridx(i), cb)))
        args.append(arr)
    for arr in full_ins:
        in_specs.append(pl.BlockSpec(arr.shape, lambda i, nd=arr.ndim: (0,) * nd))
        args.append(arr)
    out_shape, out_specs = [], []
    for width, dt in row_outs:
        out_shape.append(jax.ShapeDtypeStruct((nrows, width), dt))
        out_specs.append(pl.BlockSpec((tile, width), lambda i: (ridx(i), 0)))
    for shp, dt in acc_outs:
        out_shape.append(jax.ShapeDtypeStruct(shp, dt))
        out_specs.append(pl.BlockSpec(shp, lambda i, nd=len(shp): (0,) * nd))

    def kern(*refs):
        body(pl.program_id(0), n, *refs)

    return pl.pallas_call(kern, name=name, grid=(n,), in_specs=in_specs, out_specs=out_specs,
                          out_shape=out_shape, scratch_shapes=list(scratch),
                          compiler_params=_params("arbitrary"))(*args)


def _accum(ref, val, i):
    @pl.when(i == 0)
    def _():
        ref[...] = val

    @pl.when(i > 0)
    def _():
        ref[...] += val


def _sigmoid(x):
    return 1.0 / (1.0 + jnp.exp(-x))


def _rms_n(x):
    r = lax.rsqrt(jnp.mean(x * x, axis=-1, keepdims=True) + EPS)
    return x * r, r


def _rms_bwd(x, g, dh):
    n, r = _rms_n(x)
    dn = dh * g
    dx = r * (dn - n * jnp.mean(dn * n, axis=-1, keepdims=True))
    return dx, jnp.sum(dh * n, axis=0, keepdims=True)


def _seg_sum(x, seg):
    t, w = x.shape
    tiles = [x[:, LANES * j:LANES * (j + 1)] for j in range(w // LANES)]
    outs = []
    if seg == 64:
        lo = lax.broadcasted_iota(jnp.int32, (t, LANES), 1) < 64
        for xt in tiles:
            s_lo = jnp.sum(jnp.where(lo, xt, 0.0), axis=-1, keepdims=True)
            s_hi = jnp.sum(jnp.where(lo, 0.0, xt), axis=-1, keepdims=True)
            outs.append(jnp.where(lo, s_lo, s_hi))
    else:
        sums = [jnp.sum(xt, axis=-1, keepdims=True) for xt in tiles]
        if seg == 256:
            sums = [sums[2 * (j // 2)] + sums[2 * (j // 2) + 1] for j in range(len(sums))]
        else:
            assert seg == 128
        outs = [jnp.broadcast_to(s, (t, LANES)) for s in sums]
    return outs[0] if len(outs) == 1 else jnp.concatenate(outs, axis=1)


def _seg_rms_n(x, seg):
    r = lax.rsqrt(_seg_sum(x * x, seg) * (1.0 / seg) + EPS)
    return x * r, r


def _seg_rms_bwd(x, g, dy, seg):
    n, r = _seg_rms_n(x, seg)
    dn = dy * g
    dx = r * (dn - n * (_seg_sum(dn * n, seg) * (1.0 / seg)))
    return dx, jnp.sum(dy * n, axis=0, keepdims=True)


def _rope_tables(pos_col, inv_tab, t):
    ang = pos_col.astype(F32) * inv_tab
    idx = lax.broadcasted_iota(jnp.int32, (t, LANES), 1) % HEAD
    cos, sin = jnp.cos(ang), jnp.sin(ang)
    c = jnp.where(idx < ROT, cos, 1.0)
    sg = jnp.where(idx < ROT // 2, -sin, jnp.where(idx < ROT, sin, 0.0))
    return c, sg, idx < ROT // 2


def _rope(x, c, sg, lo):
    partner = jnp.where(lo, pltpu.roll(x, LANES - ROT // 2, 1), pltpu.roll(x, ROT // 2, 1))
    return x * c + partner * sg


def _fold_heads(v):
    v8 = jnp.broadcast_to(v, (8, LANES))
    return (v8 + pltpu.roll(v8, HEAD, 1))[0:1]


def _dot(a, b, dn):
    return lax.dot_general(a, b, (dn, ((), ())), preferred_element_type=F32)


NN = ((1,), (0,))
NT = ((1,), (1,))
TN = ((0,), (0,))


def _dot3(l01, x):
    x1 = x.astype(BF)
    r1 = x - x1.astype(F32)
    x2 = r1.astype(BF)
    x3 = (r1 - x2.astype(F32)).astype(BF)
    return _dot(l01, x1, NN) + _dot(l01, x2, NN) + _dot(l01, x3, NN)


def _rms_fwd(x, g, name):
    rows = x.shape[0]

    def body(i, n, x_ref, g_ref, o_ref):
        nx, _ = _rms_n(x_ref[...])
        o_ref[...] = (nx * g_ref[...]).astype(BF)

    return _rowwise(body, name=name, nrows=rows, tile=min(ROW_TILE, rows), row_ins=[(x, D, 0)],
                    full_ins=[g], row_outs=[(D, BF)])[0]


def _qk_prep(proj, pos_col, gq128, gk128, inv_tab):
    scale = HEAD ** -0.5

    def body(i, n, q_ref, k_ref, p_ref, gq_ref, gk_ref, it_ref, qo_ref, ko_ref):
        t = q_ref.shape[0]
        c, sg, lo = _rope_tables(p_ref[...], it_ref[...], t)
        for j in range(D_ATTN // LANES):
            sl = slice(LANES * j, LANES * (j + 1))
            qn, _ = _seg_rms_n(q_ref[:, sl], HEAD)
            kn, _ = _seg_rms_n(k_ref[:, sl], HEAD)
            qo_ref[:, sl] = _rope(qn * gq_ref[...], c, sg, lo) * scale
            ko_ref[:, sl] = _rope(kn * gk_ref[...], c, sg, lo)

    return _rowwise(body, name="qk_prep", nrows=S, tile=ROW_TILE,
                    row_ins=[(proj, D_ATTN, 0), (proj, D_ATTN, 1), (pos_col, 1, 0)],
                    full_ins=[gq128, gk128, inv_tab], row_outs=[(D_ATTN, F32)] * 2)


ATTN_QB = 16
V_COLS = pl.BlockSpec((S, LANES), lambda p: (0, 2 * D_ATTN // LANES + p))


def _band(b, d):
    nb = NB // d
    r, n = b // nb, b % nb
    width, kn = (BLK, n) if nb == 1 else (2 * BLK, jnp.maximum(n - 1, 0))

    def rows(first_member, count):
        if d == 1:
            return pl.ds(pl.multiple_of(first_member, BLK), count)
        return pl.ds(first_member * d + r, count, stride=d)

    qm = n * BLK + (lax.broadcasted_iota(jnp.int32, (2 * BLK, width), 0) & (BLK - 1))
    km = kn * BLK + lax.broadcasted_iota(jnp.int32, (2 * BLK, width), 1)
    valid = km <= qm
    if nb > 1:
        valid = valid & (km >= qm - BLK)
    return rows(n * BLK, BLK), rows(kn * BLK, width), valid


DILATIONS = (1, 4, 16)


def _attn_fwd(q, k, v):
    def kern(q_ref, k_ref, v_ref, a_ref, lse_ref, *scr):
        half0 = lax.broadcasted_iota(jnp.int32, (BLK, LANES), 1) < HEAD
        for gi, d in enumerate(DILATIONS):
            o_ref, l_ref = scr[2 * gi], scr[2 * gi + 1]

            def group(g, carry, d=d, o_ref=o_ref, l_ref=l_ref):
                for bb in range(ATTN_QB):
                    q_rows, k_rows, valid = _band(g * ATTN_QB + bb, d)
                    q = q_ref[q_rows, :]
                    kb, vb = k_ref[k_rows, :].astype(BF), v_ref[k_rows, :].astype(BF)
                    q2 = jnp.concatenate([jnp.where(half0, q, 0.0), jnp.where(half0, 0.0, q)], axis=0).astype(BF)
                    s = jnp.where(valid, _dot(q2, kb, NT), NEG)
                    m = jnp.max(s, axis=-1, keepdims=True)
                    p = jnp.exp(s - m)
                    den = jnp.sum(p, axis=-1, keepdims=True)
                    o2 = _dot(p.astype(BF), vb, NN) / den
                    l2 = m + jnp.log(den)
                    o_ref[q_rows, :] = jnp.where(half0, o2[:BLK], o2[BLK:])
                    l_ref[q_rows, :] = jnp.where(half0, l2[:BLK], l2[BLK:])
                return carry

            lax.fori_loop(0, NB // ATTN_QB, group, 0)

        def merge(i, carry):
            rows = pl.ds(pl.multiple_of(i * ROW_TILE, ROW_TILE), ROW_TILE)
            la, lb, lc = scr[1][rows, :], scr[3][rows, :], scr[5][rows, :]
            m = jnp.maximum(jnp.maximum(la, lb), lc)
            ea, eb, ec = jnp.exp(la - m), jnp.exp(lb - m), jnp.exp(lc - m)
            den = ea + eb + ec
            a_ref[rows, :] = (ea * scr[0][rows, :] + eb * scr[2][rows, :] + ec * scr[4][rows, :]) / den
            lse_ref[rows, :] = m + jnp.log(den)
            return carry

        lax.fori_loop(0, S // ROW_TILE, merge, 0)

    seq = pl.BlockSpec((S, LANES), lambda p: (0, p))
    return pl.pallas_call(
        kern, name="attn_fwd", grid=(D_ATTN // LANES,), in_specs=[seq, seq, V_COLS], out_specs=[seq, seq],
        out_shape=[jax.ShapeDtypeStruct((S, D_ATTN), F32)] * 2,
        scratch_shapes=[pltpu.VMEM((S, LANES), F32)] * (2 * len(DILATIONS)),
        compiler_params=_params("parallel"))(q, k, v)


def _attn_norm(attn, g_attn):
    def body(i, n, a_ref, g_ref, an_ref):
        nx, _ = _rms_n(a_ref[...])
        an_ref[...] = (nx * g_ref[...]).astype(BF)

    return _rowwise(body, name="attn_norm", nrows=S, tile=ROW_TILE, row_ins=[(attn, D_ATTN, 0)],
                    full_ins=[g_attn], row_outs=[(D_ATTN, BF)])[0]


def _conv_taps(x, w_ref):
    rows = lax.broadcasted_iota(jnp.int32, x.shape, 0)
    shifted = []
    for w in range(CONV_W):
        k = CONV_W - 1 - w
        shifted.append(x if k == 0 else jnp.where(rows >= k, pltpu.roll(x, k, 0), 0.0))
    acc = shifted[0] * w_ref[0:1, :]
    for w in range(1, CONV_W):
        acc = acc + shifted[w] * w_ref[w:w + 1, :]
    return acc, shifted


def _conv_fwd(proj, conv_w, conv_b):
    tc = 256

    def kern(x_ref, w_ref, b_ref, o_ref):
        c, _ = _conv_taps(x_ref[...], w_ref)
        c = c + b_ref[...]
        o_ref[...] = c * _sigmoid(c)

    return pl.pallas_call(
        kern, name="conv_fwd", grid=(D_CONV // tc,),
        in_specs=[pl.BlockSpec((S, tc), lambda c: (0, 4 * D_ATTN // tc + c)),
                  pl.BlockSpec((CONV_W, tc), lambda c: (0, c)), pl.BlockSpec((1, tc), lambda c: (0, c))],
        out_specs=pl.BlockSpec((S, tc), lambda c: (0, c)),
        out_shape=jax.ShapeDtypeStruct((S, D_CONV), F32), compiler_params=_params("parallel"))(proj, conv_w, conv_b)


def _softplus(x):
    return jnp.maximum(x, 0.0) + jnp.log1p(jnp.exp(-jnp.abs(x)))


def _dot3r(x, r01):
    x1 = x.astype(BF)
    r1 = x - x1.astype(F32)
    x2 = r1.astype(BF)
    x3 = (r1 - x2.astype(F32)).astype(BF)
    return _dot(x1, r01, NN) + _dot(x2, r01, NN) + _dot(x3, r01, NN)


def _spread_heads(v):
    sel = (lax.broadcasted_iota(jnp.int32, (LANES, D_SSM), 1) // HEAD
           == lax.broadcasted_iota(jnp.int32, (LANES, D_SSM), 0))
    return _dot3r(v, sel.astype(BF))


def _collect_heads(v):
    sel = (lax.broadcasted_iota(jnp.int32, (D_SSM, LANES), 0)
           == HEAD * lax.broadcasted_iota(jnp.int32, (D_SSM, LANES), 1))
    return _dot3r(v, sel.astype(BF))


def _ssd_prep(proj, dt_bias128, a_log128):
    def body(i, n, raw_ref, b_ref, al_ref, dt_ref, a_ref, dtb_ref, ab_ref, carry):
        @pl.when(i == 0)
        def _():
            carry[...] = jnp.zeros_like(carry)

        dt = _softplus(raw_ref[...] + b_ref[...])
        da = dt * (-jnp.exp(al_ref[...]))
        tri = (lax.broadcasted_iota(jnp.int32, (BLK, BLK), 0) >= lax.broadcasted_iota(jnp.int32, (BLK, BLK), 1))
        cs = _dot3(tri.astype(BF), da) + carry[0:1, :]
        dt_ref[...] = dt
        a_ref[...] = cs
        dtb_ref[...] = _spread_heads(dt)
        ab_ref[...] = _spread_heads(cs)
        carry[...] = jnp.broadcast_to(cs[BLK - 1:BLK, :], carry.shape)

    return _rowwise(body, name="ssd_prep", nrows=S, tile=BLK, row_ins=[(proj, LANES, DT_COLBLK)],
                    full_ins=[dt_bias128, a_log128],
                    row_outs=[(LANES, F32), (LANES, F32), (D_SSM, F32), (D_SSM, F32)],
                    scratch=[pltpu.VMEM((8, LANES), F32)])


def _ssd_decay(a_col, at_row, causal):
    diff = jnp.where(causal, a_col - at_row, 0.0)
    return jnp.where(causal, jnp.exp(diff), 0.0)


SSD_CB = 16


def _ssd_fwd(xbc, a_b, dt_b, at_r, dt_r):
    def kern(c_ref, b_ref, x_ref, ab_ref, dtb_ref, at_ref, dt_ref, y_ref, hall_ref):
        half0 = lax.broadcasted_iota(jnp.int32, (BLK, LANES), 1) < HEAD
        causal = lax.broadcasted_iota(jnp.int32, (BLK, BLK), 1) <= lax.broadcasted_iota(jnp.int32, (BLK, BLK), 0)

        def step(i, carry):
            h, a_prev = carry
            for cc in range(SSD_CB):
                chunk = i * SSD_CB + cc
                rows = pl.ds(pl.multiple_of(chunk * BLK, BLK), BLK)
                ci, bi, x = c_ref[rows, :].astype(BF), b_ref[rows, :].astype(BF), x_ref[rows, :]
                ab = ab_ref[rows, :]
                a_end = ab[BLK - 1:BLK, :]
                alpha = jnp.exp(ab - a_prev)
                beta = jnp.exp(a_end - ab) * dtb_ref[rows, :]
                hall_ref[rows, :] = h
                cb = _dot(ci, bi, NT)
                at, dtj = at_ref[chunk], dt_ref[chunk]
                y = alpha * _dot(ci, h.astype(BF), NN)
                for hd in range(2):
                    hm = half0 if hd == 0 else jnp.logical_not(half0)
                    w = cb * _ssd_decay(ab[:, HEAD * hd:HEAD * hd + 1], at[hd:hd + 1, :], causal) * dtj[hd:hd + 1, :]
                    y = y + _dot(w.astype(BF), jnp.where(hm, x, 0.0).astype(BF), NN)
                y_ref[rows, :] = y
                h = alpha[BLK - 1:BLK, :] * h + _dot(bi, (beta * x).astype(BF), TN)
                a_prev = a_end
            return h, a_prev

        lax.fori_loop(0, NB // SSD_CB, step, (jnp.zeros((BLK, LANES), F32), jnp.zeros((1, LANES), F32)))

    npair = D_SSM // LANES
    rowvec = pl.BlockSpec((None, NB, 2, BLK), lambda p: (p, 0, 0, 0))
    seq = pl.BlockSpec((S, LANES), lambda p: (0, p))
    return pl.pallas_call(
        kern, name="ssd_fwd", grid=(npair,),
        in_specs=[pl.BlockSpec((S, LANES), lambda p: (0, 12 + p // 2)),
                  pl.BlockSpec((S, LANES), lambda p: (0, 8 + p // 2)), seq, seq, seq, rowvec, rowvec],
        out_specs=[seq, seq], out_shape=[jax.ShapeDtypeStruct((S, D_SSM), F32)] * 2,
        compiler_params=_params("parallel"))(xbc, xbc, xbc, a_b, dt_b, at_r, dt_r)


def _ssd_post(y_ssd, xbc, proj, dskip_b, g_ssm):
    def body(i, n, y_ref, xs_ref, z_ref, d_ref, g_ref, o_ref):
        z = z_ref[...]
        y2 = (y_ref[...] + d_ref[...] * xs_ref[...]) * (z * _sigmoid(z))
        nx, _ = _seg_rms_n(y2, 256)
        o_ref[...] = (nx * g_ref[...]).astype(BF)

    return _rowwise(body, name="ssd_post", nrows=S, tile=ROW_TILE,
                    row_ins=[(y_ssd, D_SSM, 0), (xbc, D_SSM, 0), (proj, D_SSM, 3)], full_ins=[dskip_b, g_ssm],
                    row_outs=[(D_SSM, BF)])[0]


def _cross_heads(q, kv_ref, gq, gk):
    out = []
    for h in range(D_CROSS // CROSS_HEAD):
        sl = slice(CROSS_HEAD * h, CROSS_HEAD * (h + 1))
        nq, rq = _rms_n(q[:, sl])
        nk, rk = _rms_n(kv_ref[:, sl])
        v = kv_ref[:, D_CROSS + CROSS_HEAD * h:D_CROSS + CROSS_HEAD * (h + 1)]
        out.append((sl, nq, rq, nk, rk, v))
    return out


def _cross_fwd(qc, kv, g_cq, g_ck):
    scale = CROSS_HEAD ** -0.5

    def body(i, n, q_ref, kv_ref, gq_ref, gk_ref, o_ref):
        for sl, nq, _, nk, _, v in _cross_heads(q_ref[...], kv_ref, gq_ref[...], gk_ref[...]):
            qn = (nq * gq_ref[...] * scale).astype(BF)
            kn = (nk * gk_ref[...]).astype(BF)
            s = _dot(qn, kn, NT)
            e = jnp.exp(s - jnp.max(s, axis=-1, keepdims=True))
            p = e / jnp.sum(e, axis=-1, keepdims=True)
            o_ref[:, sl] = _dot(p.astype(BF), v.astype(BF), NN).astype(BF)

    return _rowwise(body, name="cross_fwd", nrows=S, tile=ROW_TILE, row_ins=[(qc, D_CROSS, 0)],
                    full_ins=[kv, g_cq, g_ck], row_outs=[(D_CROSS, BF)])[0]


def _rms_bwd_call(x, g, dh, dres, name, sumsq_res=False):
    rows = x.shape[0]
    has_res = dres is not None

    def body(i, n, *refs):
        if has_res:
            x_ref, dh_ref, dr_ref, g_ref, dx_ref, dxb_ref, dg_ref = refs[:7]
        else:
            x_ref, dh_ref, g_ref, dg_ref = refs
        dx, dg = _rms_bwd(x_ref[...], g_ref[...], dh_ref[...])
        if has_res:
            dr = dr_ref[...]
            dx = dx + dr
            dx_ref[...] = dx
            dxb_ref[...] = dx.astype(BF)
            if sumsq_res:
                _accum(refs[7], jnp.sum(jnp.sum(dr * dr, axis=0, keepdims=True), axis=1, keepdims=True), i)
        _accum(dg_ref, dg, i)

    row_ins = [(x, D, 0), (dh, D, 0)] + ([(dres, D, 0)] if has_res else [])
    return _rowwise(body, name=name, nrows=rows, tile=min(ROW_TILE, rows), row_ins=row_ins, full_ins=[g],
                    row_outs=[(D, F32), (D, BF)] if has_res else [],
                    acc_outs=[((1, D), F32)] + ([((1, 1), F32)] if sumsq_res else []))


def _cross_bwd(qc, doc, kv, g_cq, g_ck):
    scale = CROSS_HEAD ** -0.5

    def body(i, n, q_ref, do_ref, kv_ref, gq_ref, gk_ref, dq_ref, dkn_ref, dv_ref, dgq_ref):
        dgq = jnp.zeros((1, CROSS_HEAD), F32)
        dkn_parts, dv_parts = [], []
        for sl, nq, rq, nk, _, v in _cross_heads(q_ref[...], kv_ref, gq_ref[...], gk_ref[...]):
            qn = (nq * gq_ref[...] * scale).astype(BF)
            kn = (nk * gk_ref[...]).astype(BF)
            s = _dot(qn, kn, NT)
            e = jnp.exp(s - jnp.max(s, axis=-1, keepdims=True))
            p = e / jnp.sum(e, axis=-1, keepdims=True)
            do = do_ref[:, sl].astype(BF)
            dv_parts.append(_dot(p.astype(BF), do, TN))
            dp = _dot(do, v.astype(BF), NT)
            ds = (p * (dp - jnp.sum(dp * p, axis=-1, keepdims=True))).astype(BF)
            dqn = _dot(ds, kn, NN)
            dkn_parts.append(_dot(ds, qn, TN))
            dn = dqn * (gq_ref[...] * scale)
            dq_ref[:, sl] = (rq * (dn - nq * jnp.mean(dn * nq, axis=-1, keepdims=True))).astype(BF)
            dgq = dgq + jnp.sum(dqn * scale * nq, axis=0, keepdims=True)
        _accum(dkn_ref, jnp.concatenate(dkn_parts, axis=1), i)
        _accum(dv_ref, jnp.concatenate(dv_parts, axis=1), i)
        _accum(dgq_ref, dgq, i)

    return _rowwise(body, name="cross_bwd", nrows=S, tile=ROW_TILE, row_ins=[(qc, D_CROSS, 0), (doc, D_CROSS, 0)],
                    full_ins=[kv, g_cq, g_ck], row_outs=[(D_CROSS, BF)],
                    acc_outs=[((N_MEM, D_CROSS), F32), ((N_MEM, D_CROSS), F32), ((1, CROSS_HEAD), F32)])


def _cross_kv_bwd(kv, dkn, dv, g_ck):
    def body(i, n, kv_ref, dkn_ref, dv_ref, gk_ref, dkv_ref, dgk_ref):
        dgk = jnp.zeros((1, CROSS_HEAD), F32)
        for h in range(D_CROSS // CROSS_HEAD):
            sl = slice(CROSS_HEAD * h, CROSS_HEAD * (h + 1))
            dk, dg = _rms_bwd(kv_ref[:, sl], gk_ref[...], dkn_ref[:, sl])
            dkv_ref[:, sl] = dk.astype(BF)
            dgk = dgk + dg
        dkv_ref[:, D_CROSS:] = dv_ref[...].astype(BF)
        dgk_ref[...] = dgk

    return _rowwise(body, name="cross_kv_bwd", nrows=N_MEM, tile=N_MEM,
                    row_ins=[(kv, 2 * D_CROSS, 0), (dkn, D_CROSS, 0), (dv, D_CROSS, 0)], full_ins=[g_ck],
                    row_outs=[(2 * D_CROSS, BF)], acc_outs=[((1, CROSS_HEAD), F32)])


def _attn_merge_bwd(dmix, attn, g_attn):
    def body(i, n, dn_ref, a_ref, g_ref, da_ref, dl_ref, dg_ref):
        attn_v = a_ref[...]
        da, dg = _rms_bwd(attn_v, g_ref[...], dn_ref[...])
        da_ref[...] = da
        dl_ref[...] = _seg_sum(da * attn_v, HEAD)
        _accum(dg_ref, dg, i)

    return _rowwise(body, name="attn_merge_bwd", nrows=S, tile=ROW_TILE,
                    row_ins=[(dmix, D_ATTN, 0), (attn, D_ATTN, 0)], full_ins=[g_attn],
                    row_outs=[(D_ATTN, F32), (D_ATTN, F32)], acc_outs=[((1, D_ATTN), F32)])


def _attn_bwd(q, k, v, do, lse, delta):
    def kern(q_ref, k_ref, v_ref, do_ref, l_ref, d_ref, dq_ref, dk_ref, dv_ref):
        dk_ref[...] = jnp.zeros_like(dk_ref)
        dv_ref[...] = jnp.zeros_like(dv_ref)
        half0 = lax.broadcasted_iota(jnp.int32, (BLK, LANES), 1) < HEAD
        for gi, d in enumerate(DILATIONS):

            def group(g, carry, d=d, first=(gi == 0)):
                updates = []
                for bb in range(ATTN_QB):
                    q_rows, k_rows, valid = _band(g * ATTN_QB + bb, d)
                    q, do = q_ref[q_rows, :], do_ref[q_rows, :]
                    lse_t, del_t = l_ref[q_rows, :], d_ref[q_rows, :]
                    kb, vb = k_ref[k_rows, :].astype(BF), v_ref[k_rows, :].astype(BF)
                    q2 = jnp.concatenate([jnp.where(half0, q, 0.0), jnp.where(half0, 0.0, q)], axis=0).astype(BF)
                    do2 = jnp.concatenate([jnp.where(half0, do, 0.0), jnp.where(half0, 0.0, do)], axis=0).astype(BF)
                    lse2 = jnp.concatenate([lse_t[:, 0:1], lse_t[:, HEAD:HEAD + 1]], axis=0)
                    del2 = jnp.concatenate([del_t[:, 0:1], del_t[:, HEAD:HEAD + 1]], axis=0)
                    s = jnp.where(valid, _dot(q2, kb, NT), NEG)
                    p = jnp.exp(s - lse2)
                    ds = (p * (_dot(do2, vb, NT) - del2)).astype(BF)
                    dq2 = _dot(ds, kb, NN)
                    dq = jnp.where(half0, dq2[:BLK], dq2[BLK:])
                    if first:
                        dq_ref[q_rows, :] = dq
                    else:
                        dq_ref[q_rows, :] += dq
                    updates.append((k_rows, _dot(ds, q2, TN), _dot(p.astype(BF), do2, TN)))
                for k_rows, dk, dv in updates:
                    dk_ref[k_rows, :] += dk
                    dv_ref[k_rows, :] += dv
                return carry

            lax.fori_loop(0, NB // ATTN_QB, group, 0)

    seq = pl.BlockSpec((S, LANES), lambda p: (0, p))
    return pl.pallas_call(
        kern, name="attn_bwd", grid=(D_ATTN // LANES,), in_specs=[seq, seq, V_COLS, seq, seq, seq],
        out_specs=[seq, seq, seq], out_shape=[jax.ShapeDtypeStruct((S, D_ATTN), F32)] * 3,
        compiler_params=_params("parallel"))(q, k, v, do, lse, delta)


def _qk_bwd(dq_rot, dk_rot, dv, proj, pos_col, gq128, gk128, inv_tab):
    scale = HEAD ** -0.5

    def body(i, n, dq_ref, dk_ref, dv_ref, q_ref, k_ref, p_ref, gq_ref, gk_ref, it_ref,
             dqo_ref, dko_ref, dvo_ref, dgq_ref, dgk_ref):
        t = q_ref.shape[0]
        c, sg, lo = _rope_tables(p_ref[...], it_ref[...], t)
        dgq = jnp.zeros((1, LANES), F32)
        dgk = jnp.zeros((1, LANES), F32)
        for j in range(D_ATTN // LANES):
            sl = slice(LANES * j, LANES * (j + 1))
            dqr = _rope(dq_ref[:, sl], c, -sg, lo) * scale
            dkr = _rope(dk_ref[:, sl], c, -sg, lo)
            dq, gq = _seg_rms_bwd(q_ref[:, sl], gq_ref[...], dqr, HEAD)
            dk, gk = _seg_rms_bwd(k_ref[:, sl], gk_ref[...], dkr, HEAD)
            dqo_ref[:, sl] = dq.astype(BF)
            dko_ref[:, sl] = dk.astype(BF)
            dgq, dgk = dgq + gq, dgk + gk
        dvo_ref[...] = dv_ref[...].astype(BF)
        _accum(dgq_ref, _fold_heads(dgq), i)
        _accum(dgk_ref, _fold_heads(dgk), i)

    return _rowwise(body, name="qk_bwd", nrows=S, tile=ROW_TILE,
                    row_ins=[(a, D_ATTN, 0) for a in (dq_rot, dk_rot, dv)]
                    + [(proj, D_ATTN, 0), (proj, D_ATTN, 1), (pos_col, 1, 0)],
                    full_ins=[gq128, gk128, inv_tab], row_outs=[(D_ATTN, BF)] * 3,
                    acc_outs=[((1, LANES), F32)] * 2)


def _ssd_post_bwd(y_ssd, xbc, proj, dmix, dskip_b, g_ssm):
    def body(i, n, y_ref, xs_ref, z_ref, do_ref, d_ref, g_ref, dy_ref, dz_ref, dxs_ref, dd_ref, dg_ref):
        z, xs = z_ref[...], xs_ref[...]
        sg = _sigmoid(z)
        gate = z * sg
        y = y_ref[...] + d_ref[...] * xs
        dy2, dg = _seg_rms_bwd(y * gate, g_ref[...], do_ref[...], 256)
        dy = dy2 * gate
        dy_ref[...] = dy.astype(BF)
        dz_ref[...] = (dy2 * y * (sg * (1.0 + z * (1.0 - sg)))).astype(BF)
        dxs_ref[...] = dy * d_ref[...]
        _accum(dd_ref, jnp.sum(dy * xs, axis=0, keepdims=True), i)
        _accum(dg_ref, dg, i)

    return _rowwise(body, name="ssd_post_bwd", nrows=S, tile=ROW_TILE,
                    row_ins=[(y_ssd, D_SSM, 0), (xbc, D_SSM, 0), (proj, D_SSM, 3), (dmix, D_SSM, 1)],
                    full_ins=[dskip_b, g_ssm], row_outs=[(D_SSM, BF), (D_SSM, BF), (D_SSM, F32)],
                    acc_outs=[((1, D_SSM), F32), ((1, D_SSM), F32)])


def _ssd_bwd(xbc, dy, h_all, a_b, dt_b, at_r, dt_r):
    def kern(c_ref, b_ref, x_ref, dy_ref, hall_ref, ab_ref, dtb_ref, at_ref, dt_ref,
             dc_ref, daq_ref, dx_ref, db_ref, dak_ref, ddt_ref, ddtb_ref):
        half0 = lax.broadcasted_iota(jnp.int32, (BLK, LANES), 1) < HEAD
        hms = (half0, jnp.logical_not(half0))
        causal = lax.broadcasted_iota(jnp.int32, (BLK, BLK), 1) <= lax.broadcasted_iota(jnp.int32, (BLK, BLK), 0)
        row = lax.broadcasted_iota(jnp.int32, (BLK, LANES), 0)

        def step(t, carry_in):
            dh_next, carry = carry_in
            for cc in reversed(range(SSD_CB)):
                chunk = (NB // SSD_CB - 1 - t) * SSD_CB + cc
                rows = pl.ds(pl.multiple_of(chunk * BLK, BLK), BLK)
                ci, bi, x = c_ref[rows, :].astype(BF), b_ref[rows, :].astype(BF), x_ref[rows, :]
                dyv = dy_ref[rows, :].astype(F32)
                ab, dtb = ab_ref[rows, :], dtb_ref[rows, :]
                before = ab_ref[pl.ds(pl.multiple_of(jnp.maximum(chunk * BLK - 8, 0), 8), 8), :][7:8, :]
                a_prev = jnp.where(chunk == 0, 0.0, before)
                a_end = ab[BLK - 1:BLK, :]
                alpha = jnp.exp(ab - a_prev)
                e_end = jnp.exp(a_end - ab)
                beta = e_end * dtb
                t_all = alpha[BLK - 1:BLK, :]
                hc = hall_ref[rows, :]
                hcb, dhb = hc.astype(BF), dh_next.astype(BF)

                cb = _dot(ci, bi, NT)
                at, dtj = at_ref[chunk], dt_ref[chunk]
                dcb = jnp.zeros((BLK, BLK), F32)
                dx = jnp.zeros((BLK, LANES), F32)
                rsum = []
                for hd in range(2):
                    dym = jnp.where(hms[hd], dyv, 0.0).astype(BF)
                    lm = _ssd_decay(ab[:, HEAD * hd:HEAD * hd + 1], at[hd:hd + 1, :], causal)
                    dth = dtj[hd:hd + 1, :]
                    dw = _dot(dym, jnp.where(hms[hd], x, 0.0).astype(BF), NT)
                    g = dw * cb * lm
                    dx = dx + _dot((cb * lm * dth).astype(BF), dym, TN)
                    gcol = jnp.sum(g, axis=0, keepdims=True)
                    ddt_ref[chunk, hd:hd + 1, :] = gcol
                    dak_ref[chunk, hd:hd + 1, :] = gcol * dth
                    rsum.append(jnp.sum(g * dth, axis=1, keepdims=True))
                    dcb = dcb + dw * lm * dth
                dcb = dcb.astype(BF)

                g1 = (alpha * dyv).astype(BF)
                dalpha = dyv * _dot(ci, hcb, NN)
                g2 = _dot(bi, dhb, NN)
                dbeta = x * g2
                bx = (beta * x).astype(BF)
                dc_ref[rows, :] = _dot(dcb, bi, NN) + _dot(g1, hcb, NT)
                db_ref[rows, :] = _dot(dcb, ci, TN) + _dot(bx, dhb, NT)
                dx_ref[rows, :] = dx + beta * g2
                ddtb_ref[rows, :] = _seg_sum(e_end * dbeta, HEAD)
                ada, bdb = alpha * dalpha, beta * dbeta
                t_dt = t_all * jnp.sum(dh_next * hc, axis=0, keepdims=True)
                end_term = _seg_sum(jnp.broadcast_to(jnp.sum(bdb, axis=0, keepdims=True) + t_dt, (8, LANES)), HEAD)[0:1]
                prev_term = _seg_sum(jnp.broadcast_to(jnp.sum(ada, axis=0, keepdims=True) + t_dt, (8, LANES)), HEAD)[0:1]
                daq = jnp.where(half0, rsum[0], rsum[1]) + _seg_sum(ada - bdb, HEAD)
                daq_ref[rows, :] = daq + jnp.where(row == BLK - 1, end_term - carry, 0.0)
                carry = prev_term
                dh_next = t_all * dh_next + _dot(ci, g1, TN)
            return dh_next, carry

        lax.fori_loop(0, NB // SSD_CB, step, (jnp.zeros((BLK, LANES), F32), jnp.zeros((1, LANES), F32)))

    npair = D_SSM // LANES
    rowvec = pl.BlockSpec((None, NB, 2, BLK), lambda p: (p, 0, 0, 0))
    seq = pl.BlockSpec((S, LANES), lambda p: (0, p))
    return pl.pallas_call(
        kern, name="ssd_bwd", grid=(npair,),
        in_specs=[pl.BlockSpec((S, LANES), lambda p: (0, 12 + p // 2)),
                  pl.BlockSpec((S, LANES), lambda p: (0, 8 + p // 2)),
                  seq, seq, seq, seq, seq, rowvec, rowvec],
        out_specs=[seq, seq, seq, seq, rowvec, rowvec, seq],
        out_shape=[jax.ShapeDtypeStruct((S, D_SSM), F32)] * 4
        + [jax.ShapeDtypeStruct((npair, NB, 2, BLK), F32)] * 2 + [jax.ShapeDtypeStruct((S, D_SSM), F32)],
        compiler_params=_params("parallel"))(xbc, xbc, xbc, dy, h_all, a_b, dt_b, at_r, dt_r)


def _ssd_prep_bwd(daq, dak, ddt_row, ddt_lane, dt, proj, dt_bias128, a_log128):
    def body(i, n, daq_ref, dak_ref, dd_ref, dd2_ref, dt_ref, raw_ref, b_ref, al_ref, draw_ref, dal_ref, db_ref,
             carry):
        @pl.when(i == 0)
        def _():
            carry[...] = jnp.zeros_like(carry)

        a = -jnp.exp(al_ref[...])
        tri = (lax.broadcasted_iota(jnp.int32, (BLK, BLK), 0) <= lax.broadcasted_iota(jnp.int32, (BLK, BLK), 1))
        rev = _dot3(tri.astype(BF), _collect_heads(daq_ref[...]) - dak_ref[...]) + carry[0:1, :]
        carry[...] = jnp.broadcast_to(rev[0:1, :], carry.shape)
        dtv = dt_ref[...]
        draw = (dd_ref[...] + _collect_heads(dd2_ref[...]) + a * rev) * _sigmoid(raw_ref[...] + b_ref[...])
        draw_ref[...] = draw.astype(BF)
        _accum(dal_ref, jnp.sum(dtv * rev, axis=0, keepdims=True) * a, i)
        _accum(db_ref, jnp.sum(draw, axis=0, keepdims=True), i)

    return _rowwise(body, name="ssd_prep_bwd", nrows=S, tile=BLK, reverse=True,
                    row_ins=[(daq, D_SSM, 0), (dak, LANES, 0), (ddt_row, LANES, 0), (ddt_lane, D_SSM, 0), (dt, LANES, 0),
                             (proj, LANES, DT_COLBLK)],
                    full_ins=[dt_bias128, a_log128], row_outs=[(LANES, BF)],
                    acc_outs=[((1, LANES), F32), ((1, LANES), F32)], scratch=[pltpu.VMEM((8, LANES), F32)])


def _conv_bwd(proj, conv_w, conv_b, d1, d2, map1, map2, col0, ntiles, name):
    base = (4 * D_ATTN + col0) // LANES

    def kern(x_ref, w_ref, b_ref, d1_ref, d2_ref, dx_ref, dw_ref, db_ref):
        x = x_ref[...]
        c, shifted = _conv_taps(x, w_ref)
        c = c + b_ref[...]
        sg = _sigmoid(c)
        dc = (d1_ref[...] + d2_ref[...]) * (sg * (1.0 + c * (1.0 - sg)))
        rows = lax.broadcasted_iota(jnp.int32, x.shape, 0)
        dx = jnp.zeros_like(x)
        for w in range(CONV_W):
            k = CONV_W - 1 - w
            up = dc if k == 0 else jnp.where(rows < S - k, pltpu.roll(dc, S - k, 0), 0.0)
            dx = dx + up * w_ref[w:w + 1, :]
            dw_ref[w:w + 1, :] = jnp.sum(dc * shifted[w], axis=0, keepdims=True)
        dx_ref[...] = dx.astype(BF)
        db_ref[...] = jnp.sum(dc, axis=0, keepdims=True)

    c0 = col0 // LANES
    return pl.pallas_call(
        kern, name=name, grid=(ntiles,),
        in_specs=[pl.BlockSpec((S, LANES), lambda c: (0, base + c)),
                  pl.BlockSpec((CONV_W, LANES), lambda c: (0, c0 + c)),
                  pl.BlockSpec((1, LANES), lambda c: (0, c0 + c)),
                  pl.BlockSpec((S, LANES), lambda c: (0, map1(c))),
                  pl.BlockSpec((S, LANES), lambda c: (0, map2(c)))],
        out_specs=[pl.BlockSpec((S, LANES), lambda c: (0, c)), pl.BlockSpec((CONV_W, LANES), lambda c: (0, c)),
                   pl.BlockSpec((1, LANES), lambda c: (0, c))],
        out_shape=[jax.ShapeDtypeStruct((S, ntiles * LANES), BF), jax.ShapeDtypeStruct((CONV_W, ntiles * LANES), F32),
                   jax.ShapeDtypeStruct((1, ntiles * LANES), F32)],
        compiler_params=_params("parallel"))(proj, conv_w, conv_b, d1, d2)


def _adamw(w, m, v, parts, name):
    r, c = w.shape
    n_parts = parts.shape[0]
    tile = r if r <= 512 else (128 if c > 1024 else 256)
    assert r % tile == 0
    c1 = 1.0 - ADAM_B1 ** ADAM_STEP
    c2 = 1.0 - ADAM_B2 ** ADAM_STEP

    def kern(w_ref, m_ref, v_ref, p_ref, g_ref, d_ref, mo_ref, vo_ref):
        g = p_ref[0].astype(F32)
        for k in range(1, n_parts):
            g = g + p_ref[k].astype(F32)
        mn = ADAM_B1 * m_ref[...] + (1.0 - ADAM_B1) * g
        vn = ADAM_B2 * v_ref[...] + (1.0 - ADAM_B2) * (g * g)
        g_ref[...] = g
        mo_ref[...] = mn
        vo_ref[...] = vn
        d_ref[...] = -ADAM_LR * ((mn / c1) / (jnp.sqrt(vn / c2) + ADAM_EPS) + ADAM_WD * w_ref[...])

    blk = pl.BlockSpec((tile, c), lambda i: (i, 0))
    return pl.pallas_call(
        kern, name=name, grid=(r // tile,),
        in_specs=[blk, blk, blk, pl.BlockSpec((n_parts, tile, c), lambda i: (0, i, 0))],
        out_specs=[blk] * 4, out_shape=[jax.ShapeDtypeStruct((r, c), F32)] * 4,
        compiler_params=_params("parallel"))(w, m, v, parts)


MESH = pl.DeviceIdType.MESH
ANY = pl.BlockSpec(memory_space=pl.ANY)


def _put_own(gathered, shard):
    me = 4 * lax.axis_index("x") + 2 * lax.axis_index("y") + lax.axis_index("c")
    return lax.dynamic_update_slice(gathered, shard[None], (me,) + (0,) * shard.ndim)


def _all_gather(arrs, name, after=None):
    na = len(arrs)
    n_after = 0 if after is None else 1

    def kern(*refs):
        ins, outs = refs[:na], refs[na + n_after:2 * na + n_after]
        send_sems, recv_sems = refs[2 * na + n_after:]
        x, y, c = lax.axis_index("x"), lax.axis_index("y"), lax.axis_index("c")
        me, sibling = (x, y, c), (x, y, 1 - c)
        a_chip = (jnp.where(c == 0, 1 - x, x), jnp.where(c == 0, y, 1 - y))
        b_chip = (jnp.where(c == 0, x, 1 - x), jnp.where(c == 0, 1 - y, y))
        chips = [a_chip, b_chip, (1 - x, 1 - y)]

        def copy(a, k, block, to, src=None):
            px, py, pc = block
            dst = outs[a].at[4 * px + 2 * py + pc]
            return pltpu.make_async_remote_copy(
                src_ref=dst if src is None else src, dst_ref=dst, send_sem=send_sems.at[a, k],
                recv_sem=recv_sems.at[a, k], device_id=to, device_id_type=MESH)

        sends = []
        for a in range(na):
            sends.append(copy(a, 0, me, sibling, src=ins[a]))
            sends += [copy(a, 1 + j, me, (*chips[j], c), src=ins[a]) for j in range(2)]
        for cp in sends:
            cp.start()
        for a in range(na):
            for j, chip in enumerate(chips):
                copy(a, 1 + j, (*chip, c), me).wait_recv()
                later = [copy(a, 4 + j, (*chip, c), sibling)]
                if j == 0:
                    later.append(copy(a, 3, (*a_chip, c), (*b_chip, c)))
                for cp in later:
                    cp.start()
                sends += later
        for a in range(na):
            copy(a, 0, sibling, me).wait_recv()
            for j, chip in enumerate(chips):
                copy(a, 4 + j, (*chip, 1 - c), me).wait_recv()
        for cp in sends:
            cp.wait_send()

    outs = pl.pallas_call(
        kern, name=name, in_specs=[ANY] * (na + n_after), out_specs=[ANY] * na,
        out_shape=[jax.ShapeDtypeStruct((N_DEV,) + a.shape, a.dtype) for a in arrs],
        scratch_shapes=[pltpu.SemaphoreType.DMA((na, 7)), pltpu.SemaphoreType.DMA((na, 7))],
    )(*arrs, *([] if after is None else [after]))
    return [_put_own(o, a) for o, a in zip(outs, arrs)]


HBM = pl.BlockSpec(memory_space=pltpu.HBM)
SEM = pl.BlockSpec(memory_space=pltpu.SEMAPHORE)
EFFECT = pltpu.SideEffectType.DATAFLOW_SIDE_EFFECTING
N_CHIP = 4
N_COPIES = {"gather": 3, "scatter": 3, "forward": 4, "pair": 4, "direct": 2, "relay": 4, "diag": 1}


def _in_hbm(a):
    return pltpu.with_memory_space_constraint(a, pltpu.HBM)


def _chip_copies(kind, src_refs, land_refs, send_sems, recv_sems):
    x, y, c = lax.axis_index("x"), lax.axis_index("y"), lax.axis_index("c")
    chips = [(1 - x, y), (x, 1 - y), (1 - x, 1 - y)]
    a_chip = (jnp.where(c == 0, 1 - x, x), jnp.where(c == 0, y, 1 - y))
    b_chip = (jnp.where(c == 0, x, 1 - x), jnp.where(c == 0, 1 - y, y))
    n = N_COPIES[kind]
    cps = []

    def add(a, j, src, dst, to):
        cps.append(pltpu.make_async_remote_copy(
            src_ref=src, dst_ref=dst, send_sem=send_sems.at[n * a + j], recv_sem=recv_sems.at[n * a + j],
            device_id=to, device_id_type=MESH))

    def slot(a, chip, core):
        return land_refs[a].at[4 * chip[0] + 2 * chip[1] + core]

    for a in range(len(src_refs)):
        if kind == "direct":
            add(a, 0, src_refs[a], slot(a, (x, y), c), (*a_chip, c))
            add(a, 1, src_refs[a], slot(a, (x, y), c), (*b_chip, c))
        elif kind == "relay":
            add(a, 0, slot(a, a_chip, c), slot(a, a_chip, c), (*b_chip, c))
            add(a, 1, src_refs[a], slot(a, (x, y), c), (x, y, 1 - c))
            add(a, 2, slot(a, a_chip, c), slot(a, a_chip, c), (x, y, 1 - c))
            add(a, 3, slot(a, b_chip, c), slot(a, b_chip, c), (x, y, 1 - c))
        elif kind == "diag":
            add(a, 0, slot(a, (1 - x, 1 - y), c), slot(a, (1 - x, 1 - y), c), (x, y, 1 - c))
        elif kind == "gather":
            for j, (px, py) in enumerate(chips):
                add(a, j, src_refs[a], land_refs[a].at[4 * x + 2 * y + c], (px, py, c))
        elif kind == "scatter":
            for j, (px, py) in enumerate(chips):
                add(a, j, src_refs[a].at[2 * px + py], land_refs[a].at[2 * x + y], (px, py, c))
        elif kind == "forward":
            add(a, 0, src_refs[a], land_refs[a].at[4 * x + 2 * y + c], (x, y, 1 - c))
            for j, (px, py) in enumerate(chips):
                slot = land_refs[a].at[4 * px + 2 * py + c]
                add(a, 1 + j, slot, slot, (x, y, 1 - c))
        else:
            for q in range(N_CHIP):
                add(a, q, src_refs[a].at[2 * q + 1 - c], land_refs[a].at[q], (x, y, 1 - c))
    return cps


def _exchange_start(kind, srcs, lands, after, name):
    na = len(srcs)
    n_after = 0 if after is None else 1

    def kern(*refs):
        src_refs, land_refs = refs[:na], refs[na:2 * na]
        send_sems, recv_sems = refs[2 * na + n_after], refs[2 * na + n_after + 1]
        token = refs[-1]
        for cp in _chip_copies(kind, src_refs, land_refs, send_sems, recv_sems):
            cp.start()
        token[...] = jnp.zeros_like(token)

    bufs = list(srcs) + list(lands)
    res = pl.pallas_call(
        kern, name=name,
        out_shape=(pltpu.SemaphoreType.DMA((N_COPIES[kind] * na,)), pltpu.SemaphoreType.DMA((N_COPIES[kind] * na,)))
        + tuple(pltpu.HBM(b.shape, b.dtype) for b in bufs) + (jax.ShapeDtypeStruct((8, LANES), F32),),
        in_specs=[HBM] * (2 * na) + [ANY] * n_after,
        out_specs=(SEM, SEM) + (HBM,) * (2 * na) + (pl.BlockSpec(memory_space=pltpu.VMEM),),
        input_output_aliases={i: 2 + i for i in range(2 * na)},
        compiler_params=pltpu.CompilerParams(has_side_effects=EFFECT),
    )(*[_in_hbm(b) for b in bufs], *([] if after is None else [after]))
    return (res[0], res[1]), list(res[2:2 + na]), list(res[2 + na:2 + 2 * na]), res[-1]


def _exchange_wait(kind, sems, srcs, lands, after, name):
    na = len(srcs)
    afters = list(after) if isinstance(after, (list, tuple)) else [after]

    def kern(*refs):
        src_refs, land_refs = refs[:na], refs[na:2 * na]
        send_sems, recv_sems = refs[2 * na], refs[2 * na + 1]
        for cp in _chip_copies(kind, src_refs, land_refs, send_sems, recv_sems):
            cp.wait_send()
            cp.wait_recv()

    bufs = list(srcs) + list(lands)
    res = pl.pallas_call(
        kern, name=name, out_shape=tuple(pltpu.HBM(b.shape, b.dtype) for b in bufs),
        in_specs=[HBM] * (2 * na) + [SEM, SEM] + [ANY] * len(afters), out_specs=(HBM,) * (2 * na),
        input_output_aliases={i: i for i in range(2 * na)},
        compiler_params=pltpu.CompilerParams(has_side_effects=EFFECT),
    )(*bufs, sems[0], sems[1], *afters)
    return list(res[:na]), list(res[na:])


def _pair_exchange(parts, name):
    na = len(parts)

    def kern(*refs):
        ins, got = refs[:na], refs[na:2 * na]
        send_sems, recv_sems = refs[2 * na:]
        x, y, c = lax.axis_index("x"), lax.axis_index("y"), lax.axis_index("c")
        sends = []
        for a in range(na):
            for q in range(N_CHIP):
                sends.append(pltpu.make_async_remote_copy(
                    src_ref=ins[a].at[2 * q + 1 - c], dst_ref=got[a].at[q], send_sem=send_sems.at[a, q],
                    recv_sem=recv_sems.at[a, q], device_id=(x, y, 1 - c), device_id_type=MESH))
        for cp in sends:
            cp.start()
        for cp in sends:
            cp.wait()

    return pl.pallas_call(
        kern, name=name, in_specs=[ANY] * na, out_specs=[ANY] * na,
        out_shape=[jax.ShapeDtypeStruct((N_CHIP,) + p.shape[1:], p.dtype) for p in parts],
        scratch_shapes=[pltpu.SemaphoreType.DMA((na, N_CHIP)), pltpu.SemaphoreType.DMA((na, N_CHIP))])(*parts)


def _pair_sum(parts, got, name):
    _, r, cdim = parts.shape
    tile = min(r, ROW_TILE)
    x, y, c = lax.axis_index("x"), lax.axis_index("y"), lax.axis_index("c")
    where = jnp.stack([c, 2 * x + y]).astype(jnp.int32)

    def kern(w_ref, a_ref, b_ref, q_ref, own_ref):
        s = (a_ref[...].astype(F32) + b_ref[...].astype(F32)).astype(BF)
        q_ref[...] = s

        @pl.when(pl.program_id(1) == w_ref[1])
        def _():
            own_ref[...] = s

    blk = pl.BlockSpec((None, tile, cdim), lambda i, q, w_ref: (q, i, 0))
    sums, own = pl.pallas_call(
        kern, name=name,
        out_shape=[jax.ShapeDtypeStruct((N_CHIP, r, cdim), BF), jax.ShapeDtypeStruct((r, cdim), BF)],
        grid_spec=pltpu.PrefetchScalarGridSpec(
            num_scalar_prefetch=1, grid=(r // tile, N_CHIP),
            in_specs=[pl.BlockSpec((None, tile, cdim), lambda i, q, w_ref: (2 * q + w_ref[0], i, 0)), blk],
            out_specs=[blk, pl.BlockSpec((tile, cdim), lambda i, q, w_ref: (i, 0))]),
        compiler_params=_params("parallel", "arbitrary"))(where, parts, got)
    land = lax.dynamic_update_slice(lax.empty((N_CHIP, r, cdim), BF), own[None], (2 * x + y, 0, 0))
    return sums, land


W_IN_COLS = D_IN // N_DEV


def _w_in_from_blocks(w8):
    def kern(x_ref, o_ref):
        for j in range(N_DEV):
            o_ref[:, W_IN_COLS * j:W_IN_COLS * (j + 1)] = x_ref[j]
        o_ref[:, D_IN:] = jnp.zeros((ROW_TILE, D_INP - D_IN), o_ref.dtype)

    return pl.pallas_call(
        kern, name="w_in_from_blocks", grid=(D // ROW_TILE,),
        in_specs=[pl.BlockSpec((N_DEV, ROW_TILE, W_IN_COLS), lambda i: (0, i, 0))],
        out_specs=pl.BlockSpec((ROW_TILE, D_INP), lambda i: (i, 0)),
        out_shape=jax.ShapeDtypeStruct((D, D_INP), w8.dtype), compiler_params=_params("parallel"))(w8)


def _w_in_to_blocks(g):
    def kern(x_ref, o_ref):
        for j in range(N_DEV):
            o_ref[j] = x_ref[:, W_IN_COLS * j:W_IN_COLS * (j + 1)]

    return pl.pallas_call(
        kern, name="w_in_to_blocks", grid=(D // ROW_TILE,),
        in_specs=[pl.BlockSpec((ROW_TILE, D_INP), lambda i: (i, 0))],
        out_specs=pl.BlockSpec((N_DEV, ROW_TILE, W_IN_COLS), lambda i: (0, i, 0)),
        out_shape=jax.ShapeDtypeStruct((N_DEV, D, W_IN_COLS), g.dtype), compiler_params=_params("parallel"))(g)


def _pad_lanes(v, width=LANES):
    return jnp.pad(v, ((0, 0), (0, width - v.shape[1])))


def _row_layout(t16):
    return t16.T.reshape(N_HEADS // 2, 2, NB, BLK).transpose(0, 2, 1, 3)


def _row_layout_inv(r):
    return r.transpose(0, 2, 1, 3).reshape(N_HEADS, S).T


SMALL = (("g_mix", D), ("g_q", HEAD), ("g_k", HEAD), ("g_attn_out", D_ATTN), ("conv_b", D_CONV),
         ("dt_bias", 16), ("a_log", 16), ("d_skip", 16), ("g_ssm_out", D_SSM), ("g_cross", D),
         ("g_mem", D), ("g_cq", CROSS_HEAD), ("g_ck", CROSS_HEAD), ("g_mlp", D))
SMALL_ROWS = -(-sum(-(-w // LANES) for _, w in SMALL) // 8) * 8


def _pack_small(vals):
    rows = [_pad_lanes(vals[n], -(-w // LANES) * LANES).reshape(-1, LANES) for n, w in SMALL]
    packed = jnp.concatenate(rows, axis=0)
    return jnp.pad(packed, ((0, SMALL_ROWS - packed.shape[0]), (0, 0)))


def _unpack_small(packed):
    out, r = {}, 0
    for n, w in SMALL:
        nr = -(-w // LANES)
        out[n] = packed[r:r + nr].reshape(1, nr * LANES)[:, :w]
        r += nr
    return out


BIG = ("w_in", "w_out", "w_cq", "w_ckv", "w_co", "w_up", "w_down")
WEIGHTS = ("g_mix", "w_in", "g_q", "g_k", "g_attn_out", "conv_w", "conv_b", "dt_bias", "a_log", "d_skip",
           "g_ssm_out", "w_out", "g_cross", "g_mem", "w_cq", "w_ckv", "g_cq", "g_ck", "w_co", "g_mlp", "w_up", "w_down")


REST = ("w_out", "w_cq", "w_ckv", "w_co", "w_up", "w_down")


class _Overlap:
    def __init__(self, first, shards):
        self.first_names = list(first)
        srcs = [first[n] for n in self.first_names]
        lands = [_put_own(lax.empty((N_DEV,) + s.shape, s.dtype), s) for s in srcs]
        self.direct = _exchange_start("direct", srcs, lands, None, "ag_first_start")
        self.shards = shards
        self.gather, self.forward = {}, {}
        self.names = {"a": REST[:4], "b": REST[4:5], "c": REST[5:]}
        self.groups = []
        self.pairs = {}

    def first(self, after):
        sems, srcs, lands, _ = self.direct
        srcs, lands = _exchange_wait("direct", sems, srcs, lands, [after] + list(self.shards.values()),
                                     "ag_first_wait")
        sems, srcs, lands, token = _exchange_start("relay", srcs, lands, None, "ag_relay_start")
        srcs, lands = _exchange_wait("relay", sems, srcs, lands, token, "ag_relay_wait")
        sems, srcs, lands, token = _exchange_start("diag", srcs, lands, None, "ag_diag_start")
        for tag, names in self.names.items():
            rest_lands = [_put_own(lax.empty((N_DEV,) + self.shards[n].shape, BF), self.shards[n]) for n in names]
            self.gather[tag] = _exchange_start("gather", [self.shards[n] for n in names], rest_lands, token,
                                               "ag_start_" + tag)
            token = self.gather[tag][3]
        _, lands = _exchange_wait("diag", sems, srcs, lands, token, "ag_diag_wait")
        return dict(zip(self.first_names, lands))

    def rest_mid(self, tag, after):
        sems, srcs, lands, _ = self.gather[tag]
        srcs, lands = _exchange_wait("gather", sems, srcs, lands, after, "ag_wait_" + tag)
        self.forward[tag] = _exchange_start("forward", srcs, lands, None, "ag_fwd_start_" + tag)
        return self.forward[tag][3]

    def rest(self, tag, after):
        sems, srcs, lands, _ = self.forward[tag]
        _, lands = _exchange_wait("forward", sems, srcs, lands, after, "ag_fwd_wait_" + tag)
        wf = dict(zip(self.names[tag], lands))
        for n in wf:
            if n in ("w_out", "w_cq", "w_ckv", "w_down"):
                wf[n] = wf[n].reshape(-1, wf[n].shape[-1])
        return wf

    def pair_start(self, name, grad):
        got = lax.empty((N_CHIP,) + grad.shape[1:], grad.dtype)
        self.pairs[name] = _exchange_start("pair", [grad], [got], None, "rs_pair_start_" + name)
        return self.pairs[name][3]

    def emit(self, grads, after):
        names, tag = list(grads), "_%d" % len(self.groups)
        grads, got = dict(grads), {}
        for n in names:
            if n in self.pairs:
                sems, srcs, lands, _ = self.pairs[n]
                srcs, lands = _exchange_wait("pair", sems, srcs, lands, after, "rs_pair_wait_" + n)
                grads[n], got[n] = srcs[0], lands[0]
        sync = [n for n in names if n not in got]
        if sync:
            got.update(zip(sync, _pair_exchange([grads[n] for n in sync], "rs_pair" + tag)))
        sums = [_pair_sum(grads[n], got[n], "rs_sum_" + n) for n in names]
        sems, srcs, lands, token = _exchange_start("scatter", [q for q, _ in sums], [l for _, l in sums], None,
                                                   "rs_chip_start" + tag)
        self.groups.append((names, sems, srcs, lands))
        return token

    def finish(self, gi, after):
        names, sems, srcs, lands = self.groups[gi]
        _, lands = _exchange_wait("scatter", sems, srcs, lands, after, "rs_chip_wait_%d" % gi)
        return dict(zip(names, lands))


def _tie(v, token):
    return v if token is None else v + token[0:1, 0:1]


def _local_step(x, mem, pos_col, target, p, wf, hooks=None):
    p = dict(p)
    inv = np.zeros((1, LANES), np.float32)
    freq = (ROPE_THETA ** (-2.0 * np.arange(ROT // 2, dtype=np.float32) / ROT)).astype(np.float32)
    for l in range(LANES):
        if l % HEAD < ROT:
            inv[0, l] = freq[(l % HEAD) % (ROT // 2)]
    inv_tab = jnp.asarray(inv)
    gq128, gk128 = jnp.tile(p["g_q"], (1, 2)), jnp.tile(p["g_k"], (1, 2))
    dtb128, alog128 = _pad_lanes(p["dt_bias"]), _pad_lanes(p["a_log"])
    dskip_b = jnp.repeat(p["d_skip"], HEAD, axis=1)

    h = _rms_fwd(x, p["g_mix"], "rms_mix")
    if hooks is not None:
        got = hooks.first(h)
        wf = {**wf, "w_in": _w_in_from_blocks(got["w_in"]),
              "conv_w": got["conv_w"].transpose(1, 0, 2).reshape(CONV_W, D_CONV)}
    conv_w, conv_b = wf["conv_w"], p["conv_b"]
    proj = _matmul(h, wf["w_in"], mode="nn", name="mm_in", tm=1024, tn=1280, tk=D)
    qr, kr = _qk_prep(proj, pos_col, gq128, gk128, inv_tab)
    attn, lse = _attn_fwd(qr, kr, proj)
    attn_n = _attn_norm(attn, p["g_attn_out"])

    xbc = _conv_fwd(proj, conv_w, conv_b)
    token = None if hooks is None else hooks.rest_mid("a", xbc)
    dt, a_cs, dt_b, a_b = _ssd_prep(proj, _tie(dtb128, token), alog128)
    at_r, dt_r = _row_layout(a_cs[:, :N_HEADS]), _row_layout(dt[:, :N_HEADS])
    y_ssd, h_all = _ssd_fwd(xbc, a_b, dt_b, at_r, dt_r)
    ssm_n = _ssd_post(y_ssd, xbc, proj, dskip_b, p["g_ssm_out"])

    if hooks is not None:
        wf = {**wf, **hooks.rest("a", ssm_n)}
    mix = jnp.concatenate([attn_n, ssm_n], axis=1)
    x1 = _matmul(mix, wf["w_out"], mode="nn", name="mm_out", tm=1024, tn=1024, tk=D,
                 extras=(x,), epilogue=lambda acc, r: (acc + r,))
    hc = _rms_fwd(x1, _tie(p["g_cross"], None if hooks is None else hooks.rest_mid("b", x1)), "rms_cross")
    memh = _rms_fwd(mem, p["g_mem"], "rms_mem")
    qc = _matmul(hc, wf["w_cq"], mode="nn", name="mm_cq", tm=1024, tn=512, tk=D)
    kv = _matmul(memh, wf["w_ckv"], mode="nn", name="mm_ckv", tm=N_MEM, tn=1024, tk=D)
    oc = _cross_fwd(qc, kv, p["g_cq"], p["g_ck"])
    x2 = _matmul(oc, wf["w_co"], mode="nn", name="mm_co", tm=1024, tn=256, tk=512, b_cb=256,
                 extras=(x1,), epilogue=lambda acc, r: (acc + r,))
    hm = _rms_fwd(x2, p["g_mlp"], "rms_mlp")
    if hooks is not None:
        wf = {**wf, **hooks.rest("b", hm)}

    def up_epi(acc):
        ru = jnp.maximum(acc, 0.0)
        return ru * ru, 2.0 * ru

    act, relu2 = _matmul(hm, wf["w_up"], mode="nn", name="mm_up", tm=1024, tn=1024, tk=D, b_cb=1024,
                         out_dtypes=(BF, BF), epilogue=up_epi)
    if hooks is not None:
        hooks.rest_mid("c", act)
        wf = {**wf, **hooks.rest("c", relu2)}

    def loss_epi(acc, r, t):
        d = (acc + r - t) * (1.0 / D)
        return d, d

    dy, dyb = _matmul(act, wf["w_down"], mode="nn", name="mm_down", tm=512, tn=1024, tk=2048, extras=(x2, target),
                      out_dtypes=(F32, BF), epilogue=loss_epi)

    gb, gs = {}, {}
    gb["w_down"] = _matmul(act, dyb, mode="tn", name="mm_down_dw", tm=1024, tn=1024, tk=S,
                           out_dtypes=(BF,)).reshape(N_DEV, D_FF // N_DEV, D)
    token = None if hooks is None else hooks.pair_start("w_down", gb["w_down"])
    du = _matmul(dyb, wf["w_down"], mode="nt", name="mm_down_dx", tm=1024, tn=1024, tk=D, extras=(relu2,),
                 out_dtypes=(BF,), epilogue=lambda acc, r: (acc * r.astype(F32),), after=token)
    gb["w_up"] = _matmul(hm, du, mode="tn", name="mm_up_dw", tm=1024, tn=1024, tk=S, out_dtypes=(BF,), out_cb=1024)
    token = None if hooks is None else hooks.pair_start("w_up", gb["w_up"])
    dhm = _matmul(du, wf["w_up"], mode="nt", name="mm_up_dx", tm=1024, tn=1024, tk=2048, b_cb=1024, after=token)
    if hooks is not None:
        p["g_mlp"] = _tie(p["g_mlp"], hooks.emit({n: gb[n] for n in ("w_down", "w_up")}, dhm))
    dx2, dx2b, gs["g_mlp"], sumsq_dy = _rms_bwd_call(x2, p["g_mlp"], dhm, dy, "rms_mlp_bwd", sumsq_res=True)
    loss_part = sumsq_dy[0, 0] * (0.5 * D)

    doc = _matmul(dx2b, wf["w_co"], mode="nt", name="mm_co_dx", tm=1024, tn=512, tk=256, b_cb=256)
    gb["w_co"] = _matmul(oc, dx2b, mode="tn", name="mm_co_dw", tm=512, tn=256, tk=S, out_dtypes=(BF,), out_cb=256)
    dqc, dkn, dvc, gs["g_cq"] = _cross_bwd(qc, doc, kv, p["g_cq"], p["g_ck"])
    dkv, gs["g_ck"] = _cross_kv_bwd(kv, dkn, dvc, p["g_ck"])
    gb["w_cq"] = _matmul(hc, dqc, mode="tn", name="mm_cq_dw", tm=1024, tn=512, tk=S,
                         out_dtypes=(BF,)).reshape(N_DEV, D // N_DEV, D_CROSS)
    dhc = _matmul(dqc, wf["w_cq"], mode="nt", name="mm_cq_dx", tm=1024, tn=1024, tk=512)
    gb["w_ckv"] = _matmul(memh, dkv, mode="tn", name="mm_ckv_dw", tm=1024, tn=1024, tk=N_MEM,
                          out_dtypes=(BF,)).reshape(N_DEV, D // N_DEV, 2 * D_CROSS)
    dmemh = _matmul(dkv, wf["w_ckv"], mode="nt", name="mm_ckv_dx", tm=N_MEM, tn=1024, tk=1024)
    (gs["g_mem"],) = _rms_bwd_call(mem, p["g_mem"], dmemh, None, "rms_mem_bwd")
    dx1, dx1b, gs["g_cross"] = _rms_bwd_call(x1, p["g_cross"], dhc, dx2, "rms_cross_bwd")

    dmix = _matmul(dx1b, wf["w_out"], mode="nt", name="mm_out_dx", tm=1024, tn=1024, tk=D)
    gb["w_out"] = _matmul(mix, dx1b, mode="tn", name="mm_out_dw", tm=1024, tn=1024, tk=S,
                          out_dtypes=(BF,)).reshape(N_DEV, D // N_DEV, D)
    if hooks is not None:
        p["g_attn_out"] = _tie(p["g_attn_out"],
                               hooks.emit({n: gb[n] for n in ("w_co", "w_cq", "w_ckv", "w_out")}, dmix))

    dattn, delta, gs["g_attn_out"] = _attn_merge_bwd(dmix, attn, p["g_attn_out"])
    dq_rot, dk_rot, dv_attn = _attn_bwd(qr, kr, proj, dattn, lse, delta)
    dq_pre, dk_pre, dv_pre, dgq, dgk = _qk_bwd(dq_rot, dk_rot, dv_attn, proj, pos_col, gq128, gk128, inv_tab)
    gs["g_q"], gs["g_k"] = dgq[:, :HEAD], dgk[:, :HEAD]

    dy_ssd, dz, dxs_skip, dd_lane, gs["g_ssm_out"] = _ssd_post_bwd(y_ssd, xbc, proj, dmix, dskip_b, p["g_ssm_out"])
    gs["d_skip"] = dd_lane.reshape(N_HEADS, HEAD).sum(axis=1).reshape(1, N_HEADS)
    dc_pair, daq_b, dx_ssd, db_pair, dak_r, ddt_r, ddt_b = _ssd_bwd(xbc, dy_ssd, h_all, a_b, dt_b, at_r, dt_r)
    dak = _pad_lanes(_row_layout_inv(dak_r))
    ddt_direct = _pad_lanes(_row_layout_inv(ddt_r))
    ddt_raw, dalog, dbias = _ssd_prep_bwd(daq_b, dak, ddt_direct, ddt_b, dt, proj, dtb128, alog128)
    gs["a_log"], gs["dt_bias"] = dalog[:, :N_HEADS], dbias[:, :N_HEADS]
    same = lambda c: c
    dxbc_x, dcw_x, dcb_x = _conv_bwd(proj, conv_w, conv_b, dxs_skip, dx_ssd, same, same, 0, 8, "conv_bwd_x")
    dxbc_b, dcw_b, dcb_b = _conv_bwd(proj, conv_w, conv_b, db_pair, db_pair, lambda c: 2 * c, lambda c: 2 * c + 1,
                                     D_SSM, 4, "conv_bwd_b")
    dxbc_c, dcw_c, dcb_c = _conv_bwd(proj, conv_w, conv_b, dc_pair, dc_pair, lambda c: 2 * c, lambda c: 2 * c + 1,
                                     D_SSM + 512, 4, "conv_bwd_c")
    g_conv_w = jnp.concatenate([dcw_x, dcw_b, dcw_c], axis=1)
    gs["conv_b"] = jnp.concatenate([dcb_x, dcb_b, dcb_c], axis=1)

    pieces = [dq_pre, dk_pre, dv_pre, dz, dxbc_x, dxbc_b, dxbc_c, ddt_raw]
    pieces.append(jnp.zeros((S, D_INP - sum(t.shape[1] for t in pieces)), BF))
    dproj = jnp.concatenate(pieces, axis=1)
    dw_in = _matmul(h, dproj, mode="tn", name="mm_in_dw", tm=1024, tn=1280, tk=S, out_dtypes=(BF,))
    gb["w_in"] = _w_in_to_blocks(dw_in)
    token = None if hooks is None else hooks.emit({"w_in": gb["w_in"]}, dw_in)
    dh = _matmul(dproj, wf["w_in"], mode="nt", name="mm_in_dx", tm=1024, tn=512, tk=D_INP // 2, after=token)
    grad_x, _, gs["g_mix"] = _rms_bwd_call(x, p["g_mix"], dh, dx1, "rms_mix_bwd")
    return loss_part, grad_x, gb, gs, g_conv_w


def kernel(x, mem, positions, g_mix, w_in, g_q, g_k, g_attn_out, conv_w, conv_b, dt_bias, a_log, d_skip, g_ssm_out, w_out, g_cross, g_mem, w_cq, w_ckv, g_cq, g_ck, w_co, g_mlp, w_up, w_down, loss_target, m_g_mix, m_w_in, m_g_q, m_g_k, m_g_attn_out, m_conv_w, m_conv_b, m_dt_bias, m_a_log, m_d_skip, m_g_ssm_out, m_w_out, m_g_cross, m_g_mem, m_w_cq, m_w_ckv, m_g_cq, m_g_ck, m_w_co, m_g_mlp, m_w_up, m_w_down, v_g_mix, v_w_in, v_g_q, v_g_k, v_g_attn_out, v_conv_w, v_conv_b, v_dt_bias, v_a_log, v_d_skip, v_g_ssm_out, v_w_out, v_g_cross, v_g_mem, v_w_cq, v_w_ckv, v_g_cq, v_g_ck, v_w_co, v_g_mlp, v_w_up, v_w_down):
    args = dict(locals())
    w = {n: args[n] for n in WEIGHTS}
    m = {n: args["m_" + n] for n in WEIGHTS}
    v = {n: args["v_" + n] for n in WEIGHTS}
    small = {n: w[n] for n, _ in SMALL}
    me = 4 * lax.axis_index("x") + 2 * lax.axis_index("y") + lax.axis_index("c")

    overlap = _Overlap({"w_in": w["w_in"][0].astype(BF), "conv_w": w["conv_w"][0]},
                       {n: w[n][0].astype(BF) for n in REST})
    loss_part, grad_x, _, gs, g_conv_w = _local_step(
        x[0], mem[0], positions.reshape(S, 1), loss_target[0], dict(small), {}, overlap)
    loss = lax.psum(loss_part, ("x", "y", "c"))

    out = {}
    last = grad_x
    for gi in range(3):
        for n, parts in overlap.finish(gi, last).items():
            res_n = _adamw(w[n][0], m[n][0], v[n][0], parts, "adamw_" + n)
            out[n] = [t.reshape(w[n].shape) for t in res_n]
            last = res_n[0]

    sm_parts, cw_parts = _all_gather([_pack_small(gs), g_conv_w], "ag_small_grads", after=last)
    sm = [_unpack_small(t) for t in _adamw(_pack_small(small), _pack_small({n: m[n] for n, _ in SMALL}),
                                           _pack_small({n: v[n] for n, _ in SMALL}), sm_parts, "adamw_small")]
    for n, _ in SMALL:
        out[n] = [t[n] for t in sm]
    cols = D_CONV // N_DEV
    cw_mine = lax.dynamic_slice_in_dim(cw_parts, me * cols, cols, axis=2)
    out["conv_w"] = [t.reshape(w["conv_w"].shape) for t in
                     _adamw(w["conv_w"][0], m["conv_w"][0], v["conv_w"][0], cw_mine, "adamw_conv_w")]

    res = [loss, grad_x.reshape(x.shape)]
    for k in range(4):
        res += [out[n][k] for n in WEIGHTS]
    return tuple(res)
```

```python
import functools
import math

import numpy as np
import jax
import jax.numpy as jnp
from jax import lax
from jax.experimental import pallas as pl
from jax.experimental.pallas import tpu as pltpu

F32 = jnp.float32
BF = jnp.bfloat16

N_DEV = 8
S = 2048
D = 2048
D_ATTN = 1024
N_HEADS = 16
HEAD = 64
ROT = 16
ROPE_THETA = 500000.0
D_SSM = 1024
D_CONV = 2048
CONV_W = 4
N_MEM = 256
D_CROSS = 512
CROSS_HEAD = 128
D_FF = 8192
D_IN = 6160
D_INP = 6400
DT_COLBLK = (4 * D_ATTN + D_CONV) // 128
EPS = 1e-6
BLK = 128
NB = S // BLK
LANES = 128
NEG = -1e30

ADAM_LR = 0.001
ADAM_B1 = 0.9
ADAM_B2 = 0.999
ADAM_EPS = 1e-08
ADAM_WD = 0.01
ADAM_STEP = 10

VMEM_LIMIT_BYTES = 48 * 1024 * 1024
ROW_TILE = 256


def _params(*sem):
    return pltpu.CompilerParams(dimension_semantics=sem, vmem_limit_bytes=VMEM_LIMIT_BYTES)


def _matmul(a, b, *, mode, name, tm, tn, tk, out_dtypes=(F32,), b_cb=None, out_cb=None,
            extras=(), epilogue=None, after=None):
    if mode == "tn":
        kk, m = a.shape
    else:
        m, kk = a.shape
    if b_cb is None:
        br, bc = b.shape
    else:
        br, bc = b.shape[1], N_DEV * b_cb
    n = br if mode == "nt" else bc
    assert m % tm == 0 and n % tn == 0 and kk % tk == 0, (name, m, n, kk)
    nk = kk // tk
    grid = (m // tm, n // tn, nk)

    if mode == "tn":
        a_spec = pl.BlockSpec((tk, tm), lambda i, j, k: (k, i))
    else:
        a_spec = pl.BlockSpec((tm, tk), lambda i, j, k: (i, k))
    if mode == "nt":
        b_blk, b_idx = (tn, tk), (lambda i, j, k: (j, k))
    else:
        b_blk, b_idx = (tk, tn), (lambda i, j, k: (k, j))
    group = 1
    if b_cb is None:
        b_spec = pl.BlockSpec(b_blk, b_idx)
    elif mode == "nt" and tk > b_cb:
        assert tk % b_cb == 0
        group = tk // b_cb
        b_spec = pl.BlockSpec((group, tn, b_cb), lambda i, j, k: (k, j, 0))
    else:
        tc = b_blk[1]
        assert b_cb % tc == 0
        per = b_cb // tc

        def b_idx3(i, j, k):
            r, c = b_idx(i, j, k)
            return (c // per, r, c % per)
        b_spec = pl.BlockSpec((None,) + b_blk, b_idx3)
    ex_specs = [pl.BlockSpec((tm, tn), lambda i, j, k: (i, j)) for _ in extras]
    if out_cb is None:
        out_shape = [jax.ShapeDtypeStruct((m, n), dt) for dt in out_dtypes]
        out_specs = [pl.BlockSpec((tm, tn), lambda i, j, k: (i, j)) for _ in out_dtypes]
    else:
        assert out_cb % tn == 0
        pero = out_cb // tn
        out_shape = [jax.ShapeDtypeStruct((N_DEV, m, out_cb), dt) for dt in out_dtypes]
        out_specs = [pl.BlockSpec((None, tm, tn), lambda i, j, k: (j // pero, i, j % pero)) for _ in out_dtypes]
    dn = {"nn": (((1,), (0,)), ((), ())), "nt": (((1,), (1,)), ((), ())), "tn": (((0,), (0,)), ((), ()))}[mode]
    ne, no = len(extras), len(out_dtypes)
    n_after = 0 if after is None else 1

    def kern(a_ref, b_ref, *rest):
        ex_refs, out_refs = rest[:ne], rest[ne + n_after:ne + n_after + no]

        def finish(acc):
            outs = (acc,) if epilogue is None else epilogue(acc, *[r[...] for r in ex_refs])
            for r, o in zip(out_refs, outs):
                r[...] = o.astype(r.dtype)

        if group == 1:
            part = lax.dot_general(a_ref[...].astype(BF), b_ref[...].astype(BF), dn, preferred_element_type=F32)
        else:
            part = sum(lax.dot_general(a_ref[:, g * b_cb:(g + 1) * b_cb].astype(BF), b_ref[g].astype(BF), dn,
                                       preferred_element_type=F32) for g in range(group))
        if nk == 1:
            finish(part)
            return
        acc_ref = rest[ne + n_after + no]
        k = pl.program_id(2)

        @pl.when(k == 0)
        def _():
            acc_ref[...] = part

        @pl.when(k > 0)
        def _():
            acc_ref[...] += part

        @pl.when(k == nk - 1)
        def _():
            finish(acc_ref[...])

    res = pl.pallas_call(
        kern, name=name, grid=grid, in_specs=[a_spec, b_spec] + ex_specs + [ANY] * n_after, out_specs=out_specs,
        out_shape=out_shape, scratch_shapes=[pltpu.VMEM((tm, tn), F32)] if nk > 1 else [],
        compiler_params=_params("parallel", "parallel", "arbitrary"),
    )(a, b, *extras, *([] if after is None else [after]))
    return res[0] if no == 1 else tuple(res)


def _rowwise(body, *, name, nrows, tile, row_ins, full_ins=(), row_outs=(), acc_outs=(), scratch=(),
             reverse=False):
    n = nrows // tile

    def ridx(i):
        return (n - 1 - i) if reverse else i

    in_specs, args = [], []
    for arr, width, cb in row_ins:
        in_specs.append(pl.BlockSpec((tile, width), lambda i, cb=cb: (ridx(i), cb)))
        args.append(arr)
    for arr in full_ins:
        in_specs.append(pl.BlockSpec(arr.shape, lambda i, nd=arr.ndim: (0,) * nd))
        args.append(arr)
    out_shape, out_specs = [], []
    for width, dt in row_outs:
        out_shape.append(jax.ShapeDtypeStruct((nrows, width), dt))
        out_specs.append(pl.BlockSpec((tile, width), lambda i: (ridx(i), 0)))
    for shp, dt in acc_outs:
        out_shape.append(jax.ShapeDtypeStruct(shp, dt))
        out_specs.append(pl.BlockSpec(shp, lambda i, nd=len(shp): (0,) * nd))

    def kern(*refs):
        body(pl.program_id(0), n, *refs)

    return pl.pallas_call(kern, name=name, grid=(n,), in_specs=in_specs, out_specs=out_specs,
                          out_shape=out_shape, scratch_shapes=list(scratch),
                          compiler_params=_params("arbitrary"))(*args)


def _accum(ref, val, i):
    @pl.when(i == 0)
    def _():
        ref[...] = val

    @pl.when(i > 0)
    def _():
        ref[...] += val


def _sigmoid(x):
    return 1.0 / (1.0 + jnp.exp(-x))


def _rms_n(x):
    r = lax.rsqrt(jnp.mean(x * x, axis=-1, keepdims=True) + EPS)
    return x * r, r


def _rms_bwd(x, g, dh):
    n, r = _rms_n(x)
    dn = dh * g
    dx = r * (dn - n * jnp.mean(dn * n, axis=-1, keepdims=True))
    return dx, jnp.sum(dh * n, axis=0, keepdims=True)


def _seg_sum(x, seg):
    t, w = x.shape
    tiles = [x[:, LANES * j:LANES * (j + 1)] for j in range(w // LANES)]
    outs = []
    if seg == 64:
        lo = lax.broadcasted_iota(jnp.int32, (t, LANES), 1) < 64
        for xt in tiles:
            s_lo = jnp.sum(jnp.where(lo, xt, 0.0), axis=-1, keepdims=True)
            s_hi = jnp.sum(jnp.where(lo, 0.0, xt), axis=-1, keepdims=True)
            outs.append(jnp.where(lo, s_lo, s_hi))
    else:
        sums = [jnp.sum(xt, axis=-1, keepdims=True) for xt in tiles]
        if seg == 256:
            sums = [sums[2 * (j // 2)] + sums[2 * (j // 2) + 1] for j in range(len(sums))]
        else:
            assert seg == 128
        outs = [jnp.broadcast_to(s, (t, LANES)) for s in sums]
    return outs[0] if len(outs) == 1 else jnp.concatenate(outs, axis=1)


def _seg_rms_n(x, seg):
    r = lax.rsqrt(_seg_sum(x * x, seg) * (1.0 / seg) + EPS)
    return x * r, r


def _seg_rms_bwd(x, g, dy, seg):
    n, r = _seg_rms_n(x, seg)
    dn = dy * g
    dx = r * (dn - n * (_seg_sum(dn * n, seg) * (1.0 / seg)))
    return dx, jnp.sum(dy * n, axis=0, keepdims=True)


def _rope_tables(pos_col, inv_tab, t):
    ang = pos_col.astype(F32) * inv_tab
    idx = lax.broadcasted_iota(jnp.int32, (t, LANES), 1) % HEAD
    cos, sin = jnp.cos(ang), jnp.sin(ang)
    c = jnp.where(idx < ROT, cos, 1.0)
    sg = jnp.where(idx < ROT // 2, -sin, jnp.where(idx < ROT, sin, 0.0))
    return c, sg, idx < ROT // 2


def _rope(x, c, sg, lo):
    partner = jnp.where(lo, pltpu.roll(x, LANES - ROT // 2, 1), pltpu.roll(x, ROT // 2, 1))
    return x * c + partner * sg


def _fold_heads(v):
    v8 = jnp.broadcast_to(v, (8, LANES))
    return (v8 + pltpu.roll(v8, HEAD, 1))[0:1]


def _dot(a, b, dn):
    return lax.dot_general(a, b, (dn, ((), ())), preferred_element_type=F32)


NN = ((1,), (0,))
NT = ((1,), (1,))
TN = ((0,), (0,))


def _dot3(l01, x):
    x1 = x.astype(BF)
    r1 = x - x1.astype(F32)
    x2 = r1.astype(BF)
    x3 = (r1 - x2.astype(F32)).astype(BF)
    return _dot(l01, x1, NN) + _dot(l01, x2, NN) + _dot(l01, x3, NN)


def _rms_fwd(x, g, name):
    rows = x.shape[0]

    def body(i, n, x_ref, g_ref, o_ref):
        nx, _ = _rms_n(x_ref[...])
        o_ref[...] = (nx * g_ref[...]).astype(BF)

    return _rowwise(body, name=name, nrows=rows, tile=min(ROW_TILE, rows), row_ins=[(x, D, 0)],
                    full_ins=[g], row_outs=[(D, BF)])[0]


def _qk_prep(proj, pos_col, gq128, gk128, inv_tab):
    scale = HEAD ** -0.5

    def body(i, n, q_ref, k_ref, p_ref, gq_ref, gk_ref, it_ref, qo_ref, ko_ref):
        t = q_ref.shape[0]
        c, sg, lo = _rope_tables(p_ref[...], it_ref[...], t)
        for j in range(D_ATTN // LANES):
            sl = slice(LANES * j, LANES * (j + 1))
            qn, _ = _seg_rms_n(q_ref[:, sl], HEAD)
            kn, _ = _seg_rms_n(k_ref[:, sl], HEAD)
            qo_ref[:, sl] = _rope(qn * gq_ref[...], c, sg, lo) * scale
            ko_ref[:, sl] = _rope(kn * gk_ref[...], c, sg, lo)

    return _rowwise(body, name="qk_prep", nrows=S, tile=ROW_TILE,
                    row_ins=[(proj, D_ATTN, 0), (proj, D_ATTN, 1), (pos_col, 1, 0)],
                    full_ins=[gq128, gk128, inv_tab], row_outs=[(D_ATTN, F32)] * 2)


ATTN_QB = 16
V_COLS = pl.BlockSpec((S, LANES), lambda p: (0, 2 * D_ATTN // LANES + p))


def _band(b, d):
    nb = NB // d
    r, n = b // nb, b % nb
    width, kn = (BLK, n) if nb == 1 else (2 * BLK, jnp.maximum(n - 1, 0))

    def rows(first_member, count):
        if d == 1:
            return pl.ds(pl.multiple_of(first_member, BLK), count)
        return pl.ds(first_member * d + r, count, stride=d)

    qm = n * BLK + (lax.broadcasted_iota(jnp.int32, (2 * BLK, width), 0) & (BLK - 1))
    km = kn * BLK + lax.broadcasted_iota(jnp.int32, (2 * BLK, width), 1)
    valid = km <= qm
    if nb > 1:
        valid = valid & (km >= qm - BLK)
    return rows(n * BLK, BLK), rows(kn * BLK, width), valid


DILATIONS = (1, 4, 16)


def _attn_fwd(q, k, v):
    def kern(q_ref, k_ref, v_ref, a_ref, lse_ref, *scr):
        half0 = lax.broadcasted_iota(jnp.int32, (BLK, LANES), 1) < HEAD
        for gi, d in enumerate(DILATIONS):
            o_ref, l_ref = scr[2 * gi], scr[2 * gi + 1]

            def group(g, carry, d=d, o_ref=o_ref, l_ref=l_ref):
                for bb in range(ATTN_QB):
                    q_rows, k_rows, valid = _band(g * ATTN_QB + bb, d)
                    q = q_ref[q_rows, :]
                    kb, vb = k_ref[k_rows, :].astype(BF), v_ref[k_rows, :].astype(BF)
                    q2 = jnp.concatenate([jnp.where(half0, q, 0.0), jnp.where(half0, 0.0, q)], axis=0).astype(BF)
                    s = jnp.where(valid, _dot(q2, kb, NT), NEG)
                    m = jnp.max(s, axis=-1, keepdims=True)
                    p = jnp.exp(s - m)
                    den = jnp.sum(p, axis=-1, keepdims=True)
                    o2 = _dot(p.astype(BF), vb, NN) / den
                    l2 = m + jnp.log(den)
                    o_ref[q_rows, :] = jnp.where(half0, o2[:BLK], o2[BLK:])
                    l_ref[q_rows, :] = jnp.where(half0, l2[:BLK], l2[BLK:])
                return carry

            lax.fori_loop(0, NB // ATTN_QB, group, 0)

        def merge(i, carry):
            rows = pl.ds(pl.multiple_of(i * ROW_TILE, ROW_TILE), ROW_TILE)
            la, lb, lc = scr[1][rows, :], scr[3][rows, :], scr[5][rows, :]
            m = jnp.maximum(jnp.maximum(la, lb), lc)
            ea, eb, ec = jnp.exp(la - m), jnp.exp(lb - m), jnp.exp(lc - m)
            den = ea + eb + ec
            a_ref[rows, :] = (ea * scr[0][rows, :] + eb * scr[2][rows, :] + ec * scr[4][rows, :]) / den
            lse_ref[rows, :] = m + jnp.log(den)
            return carry

        lax.fori_loop(0, S // ROW_TILE, merge, 0)

    seq = pl.BlockSpec((S, LANES), lambda p: (0, p))
    return pl.pallas_call(
        kern, name="attn_fwd", grid=(D_ATTN // LANES,), in_specs=[seq, seq, V_COLS], out_specs=[seq, seq],
        out_shape=[jax.ShapeDtypeStruct((S, D_ATTN), F32)] * 2,
        scratch_shapes=[pltpu.VMEM((S, LANES), F32)] * (2 * len(DILATIONS)),
        compiler_params=_params("parallel"))(q, k, v)


def _attn_norm(attn, g_attn):
    def body(i, n, a_ref, g_ref, an_ref):
        nx, _ = _rms_n(a_ref[...])
        an_ref[...] = (nx * g_ref[...]).astype(BF)

    return _rowwise(body, name="attn_norm", nrows=S, tile=ROW_TILE, row_ins=[(attn, D_ATTN, 0)],
                    full_ins=[g_attn], row_outs=[(D_ATTN, BF)])[0]


def _conv_taps(x, w_ref):
    rows = lax.broadcasted_iota(jnp.int32, x.shape, 0)
    shifted = []
    for w in range(CONV_W):
        k = CONV_W - 1 - w
        shifted.append(x if k == 0 else jnp.where(rows >= k, pltpu.roll(x, k, 0), 0.0))
    acc = shifted[0] * w_ref[0:1, :]
    for w in range(1, CONV_W):
        acc = acc + shifted[w] * w_ref[w:w + 1, :]
    return acc, shifted


def _conv_fwd(proj, conv_w, conv_b):
    tc = 256

    def kern(x_ref, w_ref, b_ref, o_ref):
        c, _ = _conv_taps(x_ref[...], w_ref)
        c = c + b_ref[...]
        o_ref[...] = c * _sigmoid(c)

    return pl.pallas_call(
        kern, name="conv_fwd", grid=(D_CONV // tc,),
        in_specs=[pl.BlockSpec((S, tc), lambda c: (0, 4 * D_ATTN // tc + c)),
                  pl.BlockSpec((CONV_W, tc), lambda c: (0, c)), pl.BlockSpec((1, tc), lambda c: (0, c))],
        out_specs=pl.BlockSpec((S, tc), lambda c: (0, c)),
        out_shape=jax.ShapeDtypeStruct((S, D_CONV), F32), compiler_params=_params("parallel"))(proj, conv_w, conv_b)


def _softplus(x):
    return jnp.maximum(x, 0.0) + jnp.log1p(jnp.exp(-jnp.abs(x)))


def _dot3r(x, r01):
    x1 = x.astype(BF)
    r1 = x - x1.astype(F32)
    x2 = r1.astype(BF)
    x3 = (r1 - x2.astype(F32)).astype(BF)
    return _dot(x1, r01, NN) + _dot(x2, r01, NN) + _dot(x3, r01, NN)


def _spread_heads(v):
    sel = (lax.broadcasted_iota(jnp.int32, (LANES, D_SSM), 1) // HEAD
           == lax.broadcasted_iota(jnp.int32, (LANES, D_SSM), 0))
    return _dot3r(v, sel.astype(BF))


def _collect_heads(v):
    sel = (lax.broadcasted_iota(jnp.int32, (D_SSM, LANES), 0)
           == HEAD * lax.broadcasted_iota(jnp.int32, (D_SSM, LANES), 1))
    return _dot3r(v, sel.astype(BF))


def _ssd_prep(proj, dt_bias128, a_log128):
    def body(i, n, raw_ref, b_ref, al_ref, dt_ref, a_ref, dtb_ref, ab_ref, carry):
        @pl.when(i == 0)
        def _():
            carry[...] = jnp.zeros_like(carry)

        dt = _softplus(raw_ref[...] + b_ref[...])
        da = dt * (-jnp.exp(al_ref[...]))
        tri = (lax.broadcasted_iota(jnp.int32, (BLK, BLK), 0) >= lax.broadcasted_iota(jnp.int32, (BLK, BLK), 1))
        cs = _dot3(tri.astype(BF), da) + carry[0:1, :]
        dt_ref[...] = dt
        a_ref[...] = cs
        dtb_ref[...] = _spread_heads(dt)
        ab_ref[...] = _spread_heads(cs)
        carry[...] = jnp.broadcast_to(cs[BLK - 1:BLK, :], carry.shape)

    return _rowwise(body, name="ssd_prep", nrows=S, tile=BLK, row_ins=[(proj, LANES, DT_COLBLK)],
                    full_ins=[dt_bias128, a_log128],
                    row_outs=[(LANES, F32), (LANES, F32), (D_SSM, F32), (D_SSM, F32)],
                    scratch=[pltpu.VMEM((8, LANES), F32)])


def _ssd_decay(a_col, at_row, causal):
    diff = jnp.where(causal, a_col - at_row, 0.0)
    return jnp.where(causal, jnp.exp(diff), 0.0)


SSD_CB = 16


def _ssd_fwd(xbc, a_b, dt_b, at_r, dt_r):
    def kern(c_ref, b_ref, x_ref, ab_ref, dtb_ref, at_ref, dt_ref, y_ref, hall_ref):
        half0 = lax.broadcasted_iota(jnp.int32, (BLK, LANES), 1) < HEAD
        causal = lax.broadcasted_iota(jnp.int32, (BLK, BLK), 1) <= lax.broadcasted_iota(jnp.int32, (BLK, BLK), 0)

        def step(i, carry):
            h, a_prev = carry
            for cc in range(SSD_CB):
                chunk = i * SSD_CB + cc
                rows = pl.ds(pl.multiple_of(chunk * BLK, BLK), BLK)
                ci, bi, x = c_ref[rows, :].astype(BF), b_ref[rows, :].astype(BF), x_ref[rows, :]
                ab = ab_ref[rows, :]
                a_end = ab[BLK - 1:BLK, :]
                alpha = jnp.exp(ab - a_prev)
                beta = jnp.exp(a_end - ab) * dtb_ref[rows, :]
                hall_ref[rows, :] = h
                cb = _dot(ci, bi, NT)
                at, dtj = at_ref[chunk], dt_ref[chunk]
                y = alpha * _dot(ci, h.astype(BF), NN)
                for hd in range(2):
                    hm = half0 if hd == 0 else jnp.logical_not(half0)
                    w = cb * _ssd_decay(ab[:, HEAD * hd:HEAD * hd + 1], at[hd:hd + 1, :], causal) * dtj[hd:hd + 1, :]
                    y = y + _dot(w.astype(BF), jnp.where(hm, x, 0.0).astype(BF), NN)
                y_ref[rows, :] = y
                h = alpha[BLK - 1:BLK, :] * h + _dot(bi, (beta * x).astype(BF), TN)
                a_prev = a_end
            return h, a_prev

        lax.fori_loop(0, NB // SSD_CB, step, (jnp.zeros((BLK, LANES), F32), jnp.zeros((1, LANES), F32)))

    npair = D_SSM // LANES
    rowvec = pl.BlockSpec((None, NB, 2, BLK), lambda p: (p, 0, 0, 0))
    seq = pl.BlockSpec((S, LANES), lambda p: (0, p))
    return pl.pallas_call(
        kern, name="ssd_fwd", grid=(npair,),
        in_specs=[pl.BlockSpec((S, LANES), lambda p: (0, 12 + p // 2)),
                  pl.BlockSpec((S, LANES), lambda p: (0, 8 + p // 2)), seq, seq, seq, rowvec, rowvec],
        out_specs=[seq, seq], out_shape=[jax.ShapeDtypeStruct((S, D_SSM), F32)] * 2,
        compiler_params=_params("parallel"))(xbc, xbc, xbc, a_b, dt_b, at_r, dt_r)


def _ssd_post(y_ssd, xbc, proj, dskip_b, g_ssm):
    def body(i, n, y_ref, xs_ref, z_ref, d_ref, g_ref, o_ref):
        z = z_ref[...]
        y2 = (y_ref[...] + d_ref[...] * xs_ref[...]) * (z * _sigmoid(z))
        nx, _ = _seg_rms_n(y2, 256)
        o_ref[...] = (nx * g_ref[...]).astype(BF)

    return _rowwise(body, name="ssd_post", nrows=S, tile=ROW_TILE,
                    row_ins=[(y_ssd, D_SSM, 0), (xbc, D_SSM, 0), (proj, D_SSM, 3)], full_ins=[dskip_b, g_ssm],
                    row_outs=[(D_SSM, BF)])[0]


def _cross_heads(q, kv_ref, gq, gk):
    out = []
    for h in range(D_CROSS // CROSS_HEAD):
        sl = slice(CROSS_HEAD * h, CROSS_HEAD * (h + 1))
        nq, rq = _rms_n(q[:, sl])
        nk, rk = _rms_n(kv_ref[:, sl])
        v = kv_ref[:, D_CROSS + CROSS_HEAD * h:D_CROSS + CROSS_HEAD * (h + 1)]
        out.append((sl, nq, rq, nk, rk, v))
    return out


def _cross_fwd(qc, kv, g_cq, g_ck):
    scale = CROSS_HEAD ** -0.5

    def body(i, n, q_ref, kv_ref, gq_ref, gk_ref, o_ref):
        for sl, nq, _, nk, _, v in _cross_heads(q_ref[...], kv_ref, gq_ref[...], gk_ref[...]):
            qn = (nq * gq_ref[...] * scale).astype(BF)
            kn = (nk * gk_ref[...]).astype(BF)
            s = _dot(qn, kn, NT)
            e = jnp.exp(s - jnp.max(s, axis=-1, keepdims=True))
            p = e / jnp.sum(e, axis=-1, keepdims=True)
            o_ref[:, sl] = _dot(p.astype(BF), v.astype(BF), NN).astype(BF)

    return _rowwise(body, name="cross_fwd", nrows=S, tile=ROW_TILE, row_ins=[(qc, D_CROSS, 0)],
                    full_ins=[kv, g_cq, g_ck], row_outs=[(D_CROSS, BF)])[0]


def _rms_bwd_call(x, g, dh, dres, name, sumsq_res=False):
    rows = x.shape[0]
    has_res = dres is not None

    def body(i, n, *refs):
        if has_res:
            x_ref, dh_ref, dr_ref, g_ref, dx_ref, dxb_ref, dg_ref = refs[:7]
        else:
            x_ref, dh_ref, g_ref, dg_ref = refs
        dx, dg = _rms_bwd(x_ref[...], g_ref[...], dh_ref[...])
        if has_res:
            dr = dr_ref[...]
            dx = dx + dr
            dx_ref[...] = dx
            dxb_ref[...] = dx.astype(BF)
            if sumsq_res:
                _accum(refs[7], jnp.sum(jnp.sum(dr * dr, axis=0, keepdims=True), axis=1, keepdims=True), i)
        _accum(dg_ref, dg, i)

    row_ins = [(x, D, 0), (dh, D, 0)] + ([(dres, D, 0)] if has_res else [])
    return _rowwise(body, name=name, nrows=rows, tile=min(ROW_TILE, rows), row_ins=row_ins, full_ins=[g],
                    row_outs=[(D, F32), (D, BF)] if has_res else [],
                    acc_outs=[((1, D), F32)] + ([((1, 1), F32)] if sumsq_res else []))


def _cross_bwd(qc, doc, kv, g_cq, g_ck):
    scale = CROSS_HEAD ** -0.5

    def body(i, n, q_ref, do_ref, kv_ref, gq_ref, gk_ref, dq_ref, dkn_ref, dv_ref, dgq_ref):
        dgq = jnp.zeros((1, CROSS_HEAD), F32)
        dkn_parts, dv_parts = [], []
        for sl, nq, rq, nk, _, v in _cross_heads(q_ref[...], kv_ref, gq_ref[...], gk_ref[...]):
            qn = (nq * gq_ref[...] * scale).astype(BF)
            kn = (nk * gk_ref[...]).astype(BF)
            s = _dot(qn, kn, NT)
            e = jnp.exp(s - jnp.max(s, axis=-1, keepdims=True))
            p = e / jnp.sum(e, axis=-1, keepdims=True)
            do = do_ref[:, sl].astype(BF)
            dv_parts.append(_dot(p.astype(BF), do, TN))
            dp = _dot(do, v.astype(BF), NT)
            ds = (p * (dp - jnp.sum(dp * p, axis=-1, keepdims=True))).astype(BF)
            dqn = _dot(ds, kn, NN)
            dkn_parts.append(_dot(ds, qn, TN))
            dn = dqn * (gq_ref[...] * scale)
            dq_ref[:, sl] = (rq * (dn - nq * jnp.mean(dn * nq, axis=-1, keepdims=True))).astype(BF)
            dgq = dgq + jnp.sum(dqn * scale * nq, axis=0, keepdims=True)
        _accum(dkn_ref, jnp.concatenate(dkn_parts, axis=1), i)
        _accum(dv_ref, jnp.concatenate(dv_parts, axis=1), i)
        _accum(dgq_ref, dgq, i)

    return _rowwise(body, name="cross_bwd", nrows=S, tile=ROW_TILE, row_ins=[(qc, D_CROSS, 0), (doc, D_CROSS, 0)],
                    full_ins=[kv, g_cq, g_ck], row_outs=[(D_CROSS, BF)],
                    acc_outs=[((N_MEM, D_CROSS), F32), ((N_MEM, D_CROSS), F32), ((1, CROSS_HEAD), F32)])


def _cross_kv_bwd(kv, dkn, dv, g_ck):
    def body(i, n, kv_ref, dkn_ref, dv_ref, gk_ref, dkv_ref, dgk_ref):
        dgk = jnp.zeros((1, CROSS_HEAD), F32)
        for h in range(D_CROSS // CROSS_HEAD):
            sl = slice(CROSS_HEAD * h, CROSS_HEAD * (h + 1))
            dk, dg = _rms_bwd(kv_ref[:, sl], gk_ref[...], dkn_ref[:, sl])
            dkv_ref[:, sl] = dk.astype(BF)
            dgk = dgk + dg
        dkv_ref[:, D_CROSS:] = dv_ref[...].astype(BF)
        dgk_ref[...] = dgk

    return _rowwise(body, name="cross_kv_bwd", nrows=N_MEM, tile=N_MEM,
                    row_ins=[(kv, 2 * D_CROSS, 0), (dkn, D_CROSS, 0), (dv, D_CROSS, 0)], full_ins=[g_ck],
                    row_outs=[(2 * D_CROSS, BF)], acc_outs=[((1, CROSS_HEAD), F32)])


def _attn_merge_bwd(dmix, attn, g_attn):
    def body(i, n, dn_ref, a_ref, g_ref, da_ref, dl_ref, dg_ref):
        attn_v = a_ref[...]
        da, dg = _rms_bwd(attn_v, g_ref[...], dn_ref[...])
        da_ref[...] = da
        dl_ref[...] = _seg_sum(da * attn_v, HEAD)
        _accum(dg_ref, dg, i)

    return _rowwise(body, name="attn_merge_bwd", nrows=S, tile=ROW_TILE,
                    row_ins=[(dmix, D_ATTN, 0), (attn, D_ATTN, 0)], full_ins=[g_attn],
                    row_outs=[(D_ATTN, F32), (D_ATTN, F32)], acc_outs=[((1, D_ATTN), F32)])


def _attn_bwd(q, k, v, do, lse, delta):
    def kern(q_ref, k_ref, v_ref, do_ref, l_ref, d_ref, dq_ref, dk_ref, dv_ref):
        dk_ref[...] = jnp.zeros_like(dk_ref)
        dv_ref[...] = jnp.zeros_like(dv_ref)
        half0 = lax.broadcasted_iota(jnp.int32, (BLK, LANES), 1) < HEAD
        for gi, d in enumerate(DILATIONS):

            def group(g, carry, d=d, first=(gi == 0)):
                updates = []
                for bb in range(ATTN_QB):
                    q_rows, k_rows, valid = _band(g * ATTN_QB + bb, d)
                    q, do = q_ref[q_rows, :], do_ref[q_rows, :]
                    lse_t, del_t = l_ref[q_rows, :], d_ref[q_rows, :]
                    kb, vb = k_ref[k_rows, :].astype(BF), v_ref[k_rows, :].astype(BF)
                    q2 = jnp.concatenate([jnp.where(half0, q, 0.0), jnp.where(half0, 0.0, q)], axis=0).astype(BF)
                    do2 = jnp.concatenate([jnp.where(half0, do, 0.0), jnp.where(half0, 0.0, do)], axis=0).astype(BF)
                    lse2 = jnp.concatenate([lse_t[:, 0:1], lse_t[:, HEAD:HEAD + 1]], axis=0)
                    del2 = jnp.concatenate([del_t[:, 0:1], del_t[:, HEAD:HEAD + 1]], axis=0)
                    s = jnp.where(valid, _dot(q2, kb, NT), NEG)
                    p = jnp.exp(s - lse2)
                    ds = (p * (_dot(do2, vb, NT) - del2)).astype(BF)
                    dq2 = _dot(ds, kb, NN)
                    dq = jnp.where(half0, dq2[:BLK], dq2[BLK:])
                    if first:
                        dq_ref[q_rows, :] = dq
                    else:
                        dq_ref[q_rows, :] += dq
                    updates.append((k_rows, _dot(ds, q2, TN), _dot(p.astype(BF), do2, TN)))
                for k_rows, dk, dv in updates:
                    dk_ref[k_rows, :] += dk
                    dv_ref[k_rows, :] += dv
                return carry

            lax.fori_loop(0, NB // ATTN_QB, group, 0)

    seq = pl.BlockSpec((S, LANES), lambda p: (0, p))
    return pl.pallas_call(
        kern, name="attn_bwd", grid=(D_ATTN // LANES,), in_specs=[seq, seq, V_COLS, seq, seq, seq],
        out_specs=[seq, seq, seq], out_shape=[jax.ShapeDtypeStruct((S, D_ATTN), F32)] * 3,
        compiler_params=_params("parallel"))(q, k, v, do, lse, delta)


def _qk_bwd(dq_rot, dk_rot, dv, proj, pos_col, gq128, gk128, inv_tab):
    scale = HEAD ** -0.5

    def body(i, n, dq_ref, dk_ref, dv_ref, q_ref, k_ref, p_ref, gq_ref, gk_ref, it_ref,
             dqo_ref, dko_ref, dvo_ref, dgq_ref, dgk_ref):
        t = q_ref.shape[0]
        c, sg, lo = _rope_tables(p_ref[...], it_ref[...], t)
        dgq = jnp.zeros((1, LANES), F32)
        dgk = jnp.zeros((1, LANES), F32)
        for j in range(D_ATTN // LANES):
            sl = slice(LANES * j, LANES * (j + 1))
            dqr = _rope(dq_ref[:, sl], c, -sg, lo) * scale
            dkr = _rope(dk_ref[:, sl], c, -sg, lo)
            dq, gq = _seg_rms_bwd(q_ref[:, sl], gq_ref[...], dqr, HEAD)
            dk, gk = _seg_rms_bwd(k_ref[:, sl], gk_ref[...], dkr, HEAD)
            dqo_ref[:, sl] = dq.astype(BF)
            dko_ref[:, sl] = dk.astype(BF)
            dgq, dgk = dgq + gq, dgk + gk
        dvo_ref[...] = dv_ref[...].astype(BF)
        _accum(dgq_ref, _fold_heads(dgq), i)
        _accum(dgk_ref, _fold_heads(dgk), i)

    return _rowwise(body, name="qk_bwd", nrows=S, tile=ROW_TILE,
                    row_ins=[(a, D_ATTN, 0) for a in (dq_rot, dk_rot, dv)]
                    + [(proj, D_ATTN, 0), (proj, D_ATTN, 1), (pos_col, 1, 0)],
                    full_ins=[gq128, gk128, inv_tab], row_outs=[(D_ATTN, BF)] * 3,
                    acc_outs=[((1, LANES), F32)] * 2)


def _ssd_post_bwd(y_ssd, xbc, proj, dmix, dskip_b, g_ssm):
    def body(i, n, y_ref, xs_ref, z_ref, do_ref, d_ref, g_ref, dy_ref, dz_ref, dxs_ref, dd_ref, dg_ref):
        z, xs = z_ref[...], xs_ref[...]
        sg = _sigmoid(z)
        gate = z * sg
        y = y_ref[...] + d_ref[...] * xs
        dy2, dg = _seg_rms_bwd(y * gate, g_ref[...], do_ref[...], 256)
        dy = dy2 * gate
        dy_ref[...] = dy.astype(BF)
        dz_ref[...] = (dy2 * y * (sg * (1.0 + z * (1.0 - sg)))).astype(BF)
        dxs_ref[...] = dy * d_ref[...]
        _accum(dd_ref, jnp.sum(dy * xs, axis=0, keepdims=True), i)
        _accum(dg_ref, dg, i)

    return _rowwise(body, name="ssd_post_bwd", nrows=S, tile=ROW_TILE,
                    row_ins=[(y_ssd, D_SSM, 0), (xbc, D_SSM, 0), (proj, D_SSM, 3), (dmix, D_SSM, 1)],
                    full_ins=[dskip_b, g_ssm], row_outs=[(D_SSM, BF), (D_SSM, BF), (D_SSM, F32)],
                    acc_outs=[((1, D_SSM), F32), ((1, D_SSM), F32)])


def _ssd_bwd(xbc, dy, h_all, a_b, dt_b, at_r, dt_r):
    def kern(c_ref, b_ref, x_ref, dy_ref, hall_ref, ab_ref, dtb_ref, at_ref, dt_ref,
             dc_ref, daq_ref, dx_ref, db_ref, dak_ref, ddt_ref, ddtb_ref):
        half0 = lax.broadcasted_iota(jnp.int32, (BLK, LANES), 1) < HEAD
        hms = (half0, jnp.logical_not(half0))
        causal = lax.broadcasted_iota(jnp.int32, (BLK, BLK), 1) <= lax.broadcasted_iota(jnp.int32, (BLK, BLK), 0)
        row = lax.broadcasted_iota(jnp.int32, (BLK, LANES), 0)

        def step(t, carry_in):
            dh_next, carry = carry_in
            for cc in reversed(range(SSD_CB)):
                chunk = (NB // SSD_CB - 1 - t) * SSD_CB + cc
                rows = pl.ds(pl.multiple_of(chunk * BLK, BLK), BLK)
                ci, bi, x = c_ref[rows, :].astype(BF), b_ref[rows, :].astype(BF), x_ref[rows, :]
                dyv = dy_ref[rows, :].astype(F32)
                ab, dtb = ab_ref[rows, :], dtb_ref[rows, :]
                before = ab_ref[pl.ds(pl.multiple_of(jnp.maximum(chunk * BLK - 8, 0), 8), 8), :][7:8, :]
                a_prev = jnp.where(chunk == 0, 0.0, before)
                a_end = ab[BLK - 1:BLK, :]
                alpha = jnp.exp(ab - a_prev)
                e_end = jnp.exp(a_end - ab)
                beta = e_end * dtb
                t_all = alpha[BLK - 1:BLK, :]
                hc = hall_ref[rows, :]
                hcb, dhb = hc.astype(BF), dh_next.astype(BF)

                cb = _dot(ci, bi, NT)
                at, dtj = at_ref[chunk], dt_ref[chunk]
                dcb = jnp.zeros((BLK, BLK), F32)
                dx = jnp.zeros((BLK, LANES), F32)
                rsum = []
                for hd in range(2):
                    dym = jnp.where(hms[hd], dyv, 0.0).astype(BF)
                    lm = _ssd_decay(ab[:, HEAD * hd:HEAD * hd + 1], at[hd:hd + 1, :], causal)
                    dth = dtj[hd:hd + 1, :]
                    dw = _dot(dym, jnp.where(hms[hd], x, 0.0).astype(BF), NT)
                    g = dw * cb * lm
                    dx = dx + _dot((cb * lm * dth).astype(BF), dym, TN)
                    gcol = jnp.sum(g, axis=0, keepdims=True)
                    ddt_ref[chunk, hd:hd + 1, :] = gcol
                    dak_ref[chunk, hd:hd + 1, :] = gcol * dth
                    rsum.append(jnp.sum(g * dth, axis=1, keepdims=True))
                    dcb = dcb + dw * lm * dth
                dcb = dcb.astype(BF)

                g1 = (alpha * dyv).astype(BF)
                dalpha = dyv * _dot(ci, hcb, NN)
                g2 = _dot(bi, dhb, NN)
                dbeta = x * g2
                bx = (beta * x).astype(BF)
                dc_ref[rows, :] = _dot(dcb, bi, NN) + _dot(g1, hcb, NT)
                db_ref[rows, :] = _dot(dcb, ci, TN) + _dot(bx, dhb, NT)
                dx_ref[rows, :] = dx + beta * g2
                ddtb_ref[rows, :] = _seg_sum(e_end * dbeta, HEAD)
                ada, bdb = alpha * dalpha, beta * dbeta
                t_dt = t_all * jnp.sum(dh_next * hc, axis=0, keepdims=True)
                end_term = _seg_sum(jnp.broadcast_to(jnp.sum(bdb, axis=0, keepdims=True) + t_dt, (8, LANES)), HEAD)[0:1]
                prev_term = _seg_sum(jnp.broadcast_to(jnp.sum(ada, axis=0, keepdims=True) + t_dt, (8, LANES)), HEAD)[0:1]
                daq = jnp.where(half0, rsum[0], rsum[1]) + _seg_sum(ada - bdb, HEAD)
                daq_ref[rows, :] = daq + jnp.where(row == BLK - 1, end_term - carry, 0.0)
                carry = prev_term
                dh_next = t_all * dh_next + _dot(ci, g1, TN)
            return dh_next, carry

        lax.fori_loop(0, NB // SSD_CB, step, (jnp.zeros((BLK, LANES), F32), jnp.zeros((1, LANES), F32)))

    npair = D_SSM // LANES
    rowvec = pl.BlockSpec((None, NB, 2, BLK), lambda p: (p, 0, 0, 0))
    seq = pl.BlockSpec((S, LANES), lambda p: (0, p))
    return pl.pallas_call(
        kern, name="ssd_bwd", grid=(npair,),
        in_specs=[pl.BlockSpec((S, LANES), lambda p: (0, 12 + p // 2)),
                  pl.BlockSpec((S, LANES), lambda p: (0, 8 + p // 2)),
                  seq, seq, seq, seq, seq, rowvec, rowvec],
        out_specs=[seq, seq, seq, seq, rowvec, rowvec, seq],
        out_shape=[jax.ShapeDtypeStruct((S, D_SSM), F32)] * 4
        + [jax.ShapeDtypeStruct((npair, NB, 2, BLK), F32)] * 2 + [jax.ShapeDtypeStruct((S, D_SSM), F32)],
        compiler_params=_params("parallel"))(xbc, xbc, xbc, dy, h_all, a_b, dt_b, at_r, dt_r)


def _ssd_prep_bwd(daq, dak, ddt_row, ddt_lane, dt, proj, dt_bias128, a_log128):
    def body(i, n, daq_ref, dak_ref, dd_ref, dd2_ref, dt_ref, raw_ref, b_ref, al_ref, draw_ref, dal_ref, db_ref,
             carry):
        @pl.when(i == 0)
        def _():
            carry[...] = jnp.zeros_like(carry)

        a = -jnp.exp(al_ref[...])
        tri = (lax.broadcasted_iota(jnp.int32, (BLK, BLK), 0) <= lax.broadcasted_iota(jnp.int32, (BLK, BLK), 1))
        rev = _dot3(tri.astype(BF), _collect_heads(daq_ref[...]) - dak_ref[...]) + carry[0:1, :]
        carry[...] = jnp.broadcast_to(rev[0:1, :], carry.shape)
        dtv = dt_ref[...]
        draw = (dd_ref[...] + _collect_heads(dd2_ref[...]) + a * rev) * _sigmoid(raw_ref[...] + b_ref[...])
        draw_ref[...] = draw.astype(BF)
        _accum(dal_ref, jnp.sum(dtv * rev, axis=0, keepdims=True) * a, i)
        _accum(db_ref, jnp.sum(draw, axis=0, keepdims=True), i)

    return _rowwise(body, name="ssd_prep_bwd", nrows=S, tile=BLK, reverse=True,
                    row_ins=[(daq, D_SSM, 0), (dak, LANES, 0), (ddt_row, LANES, 0), (ddt_lane, D_SSM, 0), (dt, LANES, 0),
                             (proj, LANES, DT_COLBLK)],
                    full_ins=[dt_bias128, a_log128], row_outs=[(LANES, BF)],
                    acc_outs=[((1, LANES), F32), ((1, LANES), F32)], scratch=[pltpu.VMEM((8, LANES), F32)])


def _conv_bwd(proj, conv_w, conv_b, d1, d2, map1, map2, col0, ntiles, name):
    base = (4 * D_ATTN + col0) // LANES

    def kern(x_ref, w_ref, b_ref, d1_ref, d2_ref, dx_ref, dw_ref, db_ref):
        x = x_ref[...]
        c, shifted = _conv_taps(x, w_ref)
        c = c + b_ref[...]
        sg = _sigmoid(c)
        dc = (d1_ref[...] + d2_ref[...]) * (sg * (1.0 + c * (1.0 - sg)))
        rows = lax.broadcasted_iota(jnp.int32, x.shape, 0)
        dx = jnp.zeros_like(x)
        for w in range(CONV_W):
            k = CONV_W - 1 - w
            up = dc if k == 0 else jnp.where(rows < S - k, pltpu.roll(dc, S - k, 0), 0.0)
            dx = dx + up * w_ref[w:w + 1, :]
            dw_ref[w:w + 1, :] = jnp.sum(dc * shifted[w], axis=0, keepdims=True)
        dx_ref[...] = dx.astype(BF)
        db_ref[...] = jnp.sum(dc, axis=0, keepdims=True)

    c0 = col0 // LANES
    return pl.pallas_call(
        kern, name=name, grid=(ntiles,),
        in_specs=[pl.BlockSpec((S, LANES), lambda c: (0, base + c)),
                  pl.BlockSpec((CONV_W, LANES), lambda c: (0, c0 + c)),
                  pl.BlockSpec((1, LANES), lambda c: (0, c0 + c)),
                  pl.BlockSpec((S, LANES), lambda c: (0, map1(c))),
                  pl.BlockSpec((S, LANES), lambda c: (0, map2(c)))],
        out_specs=[pl.BlockSpec((S, LANES), lambda c: (0, c)), pl.BlockSpec((CONV_W, LANES), lambda c: (0, c)),
                   pl.BlockSpec((1, LANES), lambda c: (0, c))],
        out_shape=[jax.ShapeDtypeStruct((S, ntiles * LANES), BF), jax.ShapeDtypeStruct((CONV_W, ntiles * LANES), F32),
                   jax.ShapeDtypeStruct((1, ntiles * LANES), F32)],
        compiler_params=_params("parallel"))(proj, conv_w, conv_b, d1, d2)


def _adamw(w, m, v, parts, name):
    r, c = w.shape
    n_parts = parts.shape[0]
    tile = r if r <= 512 else (128 if c > 1024 else 256)
    assert r % tile == 0
    c1 = 1.0 - ADAM_B1 ** ADAM_STEP
    c2 = 1.0 - ADAM_B2 ** ADAM_STEP

    def kern(w_ref, m_ref, v_ref, p_ref, g_ref, d_ref, mo_ref, vo_ref):
        g = p_ref[0].astype(F32)
        for k in range(1, n_parts):
            g = g + p_ref[k].astype(F32)
        mn = ADAM_B1 * m_ref[...] + (1.0 - ADAM_B1) * g
        vn = ADAM_B2 * v_ref[...] + (1.0 - ADAM_B2) * (g * g)
        g_ref[...] = g
        mo_ref[...] = mn
        vo_ref[...] = vn
        d_ref[...] = -ADAM_LR * ((mn / c1) / (jnp.sqrt(vn / c2) + ADAM_EPS) + ADAM_WD * w_ref[...])

    blk = pl.BlockSpec((tile, c), lambda i: (i, 0))
    return pl.pallas_call(
        kern, name=name, grid=(r // tile,),
        in_specs=[blk, blk, blk, pl.BlockSpec((n_parts, tile, c), lambda i: (0, i, 0))],
        out_specs=[blk] * 4, out_shape=[jax.ShapeDtypeStruct((r, c), F32)] * 4,
        compiler_params=_params("parallel"))(w, m, v, parts)


MESH = pl.DeviceIdType.MESH
ANY = pl.BlockSpec(memory_space=pl.ANY)


def _put_own(gathered, shard):
    me = 4 * lax.axis_index("x") + 2 * lax.axis_index("y") + lax.axis_index("c")
    return lax.dynamic_update_slice(gathered, shard[None], (me,) + (0,) * shard.ndim)


def _all_gather(arrs, name, after=None):
    na = len(arrs)
    n_after = 0 if after is None else 1

    def kern(*refs):
        ins, outs = refs[:na], refs[na + n_after:2 * na + n_after]
        send_sems, recv_sems = refs[2 * na + n_after:]
        x, y, c = lax.axis_index("x"), lax.axis_index("y"), lax.axis_index("c")
        me, sibling = (x, y, c), (x, y, 1 - c)
        a_chip = (jnp.where(c == 0, 1 - x, x), jnp.where(c == 0, y, 1 - y))
        b_chip = (jnp.where(c == 0, x, 1 - x), jnp.where(c == 0, 1 - y, y))
        chips = [a_chip, b_chip, (1 - x, 1 - y)]

        def copy(a, k, block, to, src=None):
            px, py, pc = block
            dst = outs[a].at[4 * px + 2 * py + pc]
            return pltpu.make_async_remote_copy(
                src_ref=dst if src is None else src, dst_ref=dst, send_sem=send_sems.at[a, k],
                recv_sem=recv_sems.at[a, k], device_id=to, device_id_type=MESH)

        sends = []
        for a in range(na):
            sends.append(copy(a, 0, me, sibling, src=ins[a]))
            sends += [copy(a, 1 + j, me, (*chips[j], c), src=ins[a]) for j in range(2)]
        for cp in sends:
            cp.start()
        for a in range(na):
            for j, chip in enumerate(chips):
                copy(a, 1 + j, (*chip, c), me).wait_recv()
                later = [copy(a, 4 + j, (*chip, c), sibling)]
                if j == 0:
                    later.append(copy(a, 3, (*a_chip, c), (*b_chip, c)))
                for cp in later:
                    cp.start()
                sends += later
        for a in range(na):
            copy(a, 0, sibling, me).wait_recv()
            for j, chip in enumerate(chips):
                copy(a, 4 + j, (*chip, 1 - c), me).wait_recv()
        for cp in sends:
            cp.wait_send()

    outs = pl.pallas_call(
        kern, name=name, in_specs=[ANY] * (na + n_after), out_specs=[ANY] * na,
        out_shape=[jax.ShapeDtypeStruct((N_DEV,) + a.shape, a.dtype) for a in arrs],
        scratch_shapes=[pltpu.SemaphoreType.DMA((na, 7)), pltpu.SemaphoreType.DMA((na, 7))],
    )(*arrs, *([] if after is None else [after]))
    return [_put_own(o, a) for o, a in zip(outs, arrs)]


HBM = pl.BlockSpec(memory_space=pltpu.HBM)
SEM = pl.BlockSpec(memory_space=pltpu.SEMAPHORE)
EFFECT = pltpu.SideEffectType.DATAFLOW_SIDE_EFFECTING
N_CHIP = 4
N_COPIES = {"gather": 3, "scatter": 3, "forward": 4, "pair": 4, "direct": 2, "relay": 4, "diag": 1}


def _in_hbm(a):
    return pltpu.with_memory_space_constraint(a, pltpu.HBM)


def _chip_copies(kind, src_refs, land_refs, send_sems, recv_sems):
    x, y, c = lax.axis_index("x"), lax.axis_index("y"), lax.axis_index("c")
    chips = [(1 - x, y), (x, 1 - y), (1 - x, 1 - y)]
    a_chip = (jnp.where(c == 0, 1 - x, x), jnp.where(c == 0, y, 1 - y))
    b_chip = (jnp.where(c == 0, x, 1 - x), jnp.where(c == 0, 1 - y, y))
    n = N_COPIES[kind]
    cps = []

    def add(a, j, src, dst, to):
        cps.append(pltpu.make_async_remote_copy(
            src_ref=src, dst_ref=dst, send_sem=send_sems.at[n * a + j], recv_sem=recv_sems.at[n * a + j],
            device_id=to, device_id_type=MESH))

    def slot(a, chip, core):
        return land_refs[a].at[4 * chip[0] + 2 * chip[1] + core]

    for a in range(len(src_refs)):
        if kind == "direct":
            add(a, 0, src_refs[a], slot(a, (x, y), c), (*a_chip, c))
            add(a, 1, src_refs[a], slot(a, (x, y), c), (*b_chip, c))
        elif kind == "relay":
            add(a, 0, slot(a, a_chip, c), slot(a, a_chip, c), (*b_chip, c))
            add(a, 1, src_refs[a], slot(a, (x, y), c), (x, y, 1 - c))
            add(a, 2, slot(a, a_chip, c), slot(a, a_chip, c), (x, y, 1 - c))
            add(a, 3, slot(a, b_chip, c), slot(a, b_chip, c), (x, y, 1 - c))
        elif kind == "diag":
            add(a, 0, slot(a, (1 - x, 1 - y), c), slot(a, (1 - x, 1 - y), c), (x, y, 1 - c))
        elif kind == "gather":
            for j, (px, py) in enumerate(chips):
                add(a, j, src_refs[a], land_refs[a].at[4 * x + 2 * y + c], (px, py, c))
        elif kind == "scatter":
            for j, (px, py) in enumerate(chips):
                add(a, j, src_refs[a].at[2 * px + py], land_refs[a].at[2 * x + y], (px, py, c))
        elif kind == "forward":
            add(a, 0, src_refs[a], land_refs[a].at[4 * x + 2 * y + c], (x, y, 1 - c))
            for j, (px, py) in enumerate(chips):
                slot = land_refs[a].at[4 * px + 2 * py + c]
                add(a, 1 + j, slot, slot, (x, y, 1 - c))
        else:
            for q in range(N_CHIP):
                add(a, q, src_refs[a].at[2 * q + 1 - c], land_refs[a].at[q], (x, y, 1 - c))
    return cps


def _exchange_start(kind, srcs, lands, after, name):
    na = len(srcs)
    n_after = 0 if after is None else 1

    def kern(*refs):
        src_refs, land_refs = refs[:na], refs[na:2 * na]
        send_sems, recv_sems = refs[2 * na + n_after], refs[2 * na + n_after + 1]
        token = refs[-1]
        for cp in _chip_copies(kind, src_refs, land_refs, send_sems, recv_sems):
            cp.start()
        token[...] = jnp.zeros_like(token)

    bufs = list(srcs) + list(lands)
    res = pl.pallas_call(
        kern, name=name,
        out_shape=(pltpu.SemaphoreType.DMA((N_COPIES[kind] * na,)), pltpu.SemaphoreType.DMA((N_COPIES[kind] * na,)))
        + tuple(pltpu.HBM(b.shape, b.dtype) for b in bufs) + (jax.ShapeDtypeStruct((8, LANES), F32),),
        in_specs=[HBM] * (2 * na) + [ANY] * n_after,
        out_specs=(SEM, SEM) + (HBM,) * (2 * na) + (pl.BlockSpec(memory_space=pltpu.VMEM),),
        input_output_aliases={i: 2 + i for i in range(2 * na)},
        compiler_params=pltpu.CompilerParams(has_side_effects=EFFECT),
    )(*[_in_hbm(b) for b in bufs], *([] if after is None else [after]))
    return (res[0], res[1]), list(res[2:2 + na]), list(res[2 + na:2 + 2 * na]), res[-1]


def _exchange_wait(kind, sems, srcs, lands, after, name):
    na = len(srcs)
    afters = list(after) if isinstance(after, (list, tuple)) else [after]

    def kern(*refs):
        src_refs, land_refs = refs[:na], refs[na:2 * na]
        send_sems, recv_sems = refs[2 * na], refs[2 * na + 1]
        for cp in _chip_copies(kind, src_refs, land_refs, send_sems, recv_sems):
            cp.wait_send()
            cp.wait_recv()

    bufs = list(srcs) + list(lands)
    res = pl.pallas_call(
        kern, name=name, out_shape=tuple(pltpu.HBM(b.shape, b.dtype) for b in bufs),
        in_specs=[HBM] * (2 * na) + [SEM, SEM] + [ANY] * len(afters), out_specs=(HBM,) * (2 * na),
        input_output_aliases={i: i for i in range(2 * na)},
        compiler_params=pltpu.CompilerParams(has_side_effects=EFFECT),
    )(*bufs, sems[0], sems[1], *afters)
    return list(res[:na]), list(res[na:])


def _pair_exchange(parts, name):
    na = len(parts)

    def kern(*refs):
        ins, got = refs[:na], refs[na:2 * na]
        send_sems, recv_sems = refs[2 * na:]
        x, y, c = lax.axis_index("x"), lax.axis_index("y"), lax.axis_index("c")
        sends = []
        for a in range(na):
            for q in range(N_CHIP):
                sends.append(pltpu.make_async_remote_copy(
                    src_ref=ins[a].at[2 * q + 1 - c], dst_ref=got[a].at[q], send_sem=send_sems.at[a, q],
                    recv_sem=recv_sems.at[a, q], device_id=(x, y, 1 - c), device_id_type=MESH))
        for cp in sends:
            cp.start()
        for cp in sends:
            cp.wait()

    return pl.pallas_call(
        kern, name=name, in_specs=[ANY] * na, out_specs=[ANY] * na,
        out_shape=[jax.ShapeDtypeStruct((N_CHIP,) + p.shape[1:], p.dtype) for p in parts],
        scratch_shapes=[pltpu.SemaphoreType.DMA((na, N_CHIP)), pltpu.SemaphoreType.DMA((na, N_CHIP))])(*parts)


def _pair_sum(parts, got, name):
    _, r, cdim = parts.shape
    tile = min(r, ROW_TILE)
    x, y, c = lax.axis_index("x"), lax.axis_index("y"), lax.axis_index("c")
    where = jnp.stack([c, 2 * x + y]).astype(jnp.int32)

    def kern(w_ref, a_ref, b_ref, q_ref, own_ref):
        s = (a_ref[...].astype(F32) + b_ref[...].astype(F32)).astype(BF)
        q_ref[...] = s

        @pl.when(pl.program_id(1) == w_ref[1])
        def _():
            own_ref[...] = s

    blk = pl.BlockSpec((None, tile, cdim), lambda i, q, w_ref: (q, i, 0))
    sums, own = pl.pallas_call(
        kern, name=name,
        out_shape=[jax.ShapeDtypeStruct((N_CHIP, r, cdim), BF), jax.ShapeDtypeStruct((r, cdim), BF)],
        grid_spec=pltpu.PrefetchScalarGridSpec(
            num_scalar_prefetch=1, grid=(r // tile, N_CHIP),
            in_specs=[pl.BlockSpec((None, tile, cdim), lambda i, q, w_ref: (2 * q + w_ref[0], i, 0)), blk],
            out_specs=[blk, pl.BlockSpec((tile, cdim), lambda i, q, w_ref: (i, 0))]),
        compiler_params=_params("parallel", "arbitrary"))(where, parts, got)
    land = lax.dynamic_update_slice(lax.empty((N_CHIP, r, cdim), BF), own[None], (2 * x + y, 0, 0))
    return sums, land


W_IN_COLS = D_IN // N_DEV


def _w_in_from_blocks(w8):
    def kern(x_ref, o_ref):
        for j in range(N_DEV):
            o_ref[:, W_IN_COLS * j:W_IN_COLS * (j + 1)] = x_ref[j]
        o_ref[:, D_IN:] = jnp.zeros((ROW_TILE, D_INP - D_IN), o_ref.dtype)

    return pl.pallas_call(
        kern, name="w_in_from_blocks", grid=(D // ROW_TILE,),
        in_specs=[pl.BlockSpec((N_DEV, ROW_TILE, W_IN_COLS), lambda i: (0, i, 0))],
        out_specs=pl.BlockSpec((ROW_TILE, D_INP), lambda i: (i, 0)),
        out_shape=jax.ShapeDtypeStruct((D, D_INP), w8.dtype), compiler_params=_params("parallel"))(w8)


def _w_in_to_blocks(g):
    def kern(x_ref, o_ref):
        for j in range(N_DEV):
            o_ref[j] = x_ref[:, W_IN_COLS * j:W_IN_COLS * (j + 1)]

    return pl.pallas_call(
        kern, name="w_in_to_blocks", grid=(D // ROW_TILE,),
        in_specs=[pl.BlockSpec((ROW_TILE, D_INP), lambda i: (i, 0))],
        out_specs=pl.BlockSpec((N_DEV, ROW_TILE, W_IN_COLS), lambda i: (0, i, 0)),
        out_shape=jax.ShapeDtypeStruct((N_DEV, D, W_IN_COLS), g.dtype), compiler_params=_params("parallel"))(g)


def _pad_lanes(v, width=LANES):
    return jnp.pad(v, ((0, 0), (0, width - v.shape[1])))


def _row_layout(t16):
    return t16.T.reshape(N_HEADS // 2, 2, NB, BLK).transpose(0, 2, 1, 3)


def _row_layout_inv(r):
    return r.transpose(0, 2, 1, 3).reshape(N_HEADS, S).T


SMALL = (("g_mix", D), ("g_q", HEAD), ("g_k", HEAD), ("g_attn_out", D_ATTN), ("conv_b", D_CONV),
         ("dt_bias", 16), ("a_log", 16), ("d_skip", 16), ("g_ssm_out", D_SSM), ("g_cross", D),
         ("g_mem", D), ("g_cq", CROSS_HEAD), ("g_ck", CROSS_HEAD), ("g_mlp", D))
SMALL_ROWS = -(-sum(-(-w // LANES) for _, w in SMALL) // 8) * 8


def _pack_small(vals):
    rows = [_pad_lanes(vals[n], -(-w // LANES) * LANES).reshape(-1, LANES) for n, w in SMALL]
    packed = jnp.concatenate(rows, axis=0)
    return jnp.pad(packed, ((0, SMALL_ROWS - packed.shape[0]), (0, 0)))


def _unpack_small(packed):
    out, r = {}, 0
    for n, w in SMALL:
        nr = -(-w // LANES)
        out[n] = packed[r:r + nr].reshape(1, nr * LANES)[:, :w]
        r += nr
    return out


BIG = ("w_in", "w_out", "w_cq", "w_ckv", "w_co", "w_up", "w_down")
WEIGHTS = ("g_mix", "w_in", "g_q", "g_k", "g_attn_out", "conv_w", "conv_b", "dt_bias", "a_log", "d_skip",
           "g_ssm_out", "w_out", "g_cross", "g_mem", "w_cq", "w_ckv", "g_cq", "g_ck", "w_co", "g_mlp", "w_up", "w_down")


REST = ("w_out", "w_cq", "w_ckv", "w_co", "w_up", "w_down")


class _Overlap:
    def __init__(self, first):
        self.first_names = list(first)
        srcs = [first[n] for n in self.first_names]
        lands = [_put_own(lax.empty((N_DEV,) + s.shape, s.dtype), s) for s in srcs]
        self.direct = _exchange_start("direct", srcs, lands, None, "ag_first_start")
        self.token = self.direct[3]
        self.shards, self.behind = {}, []
        self.gather, self.forward = {}, {}
        self.names = {"a": REST[:4], "b": REST[4:5], "c": REST[5:]}
        self.groups = []
        self.pairs = {}

    def first(self, after):
        sems, srcs, lands, _ = self.direct
        srcs, lands = _exchange_wait("direct", sems, srcs, lands,
                                     [after] + list(self.shards.values()) + list(self.behind), "ag_first_wait")
        sems, srcs, lands, token = _exchange_start("relay", srcs, lands, None, "ag_relay_start")
        srcs, lands = _exchange_wait("relay", sems, srcs, lands, token, "ag_relay_wait")
        sems, srcs, lands, token = _exchange_start("diag", srcs, lands, None, "ag_diag_start")
        for tag, names in self.names.items():
            rest_lands = [_put_own(lax.empty((N_DEV,) + self.shards[n].shape, BF), self.shards[n]) for n in names]
            self.gather[tag] = _exchange_start("gather", [self.shards[n] for n in names], rest_lands, token,
                                               "ag_start_" + tag)
            token = self.gather[tag][3]
        _, lands = _exchange_wait("diag", sems, srcs, lands, token, "ag_diag_wait")
        return dict(zip(self.first_names, lands))

    def rest_mid(self, tag, after):
        sems, srcs, lands, _ = self.gather[tag]
        srcs, lands = _exchange_wait("gather", sems, srcs, lands, after, "ag_wait_" + tag)
        self.forward[tag] = _exchange_start("forward", srcs, lands, None, "ag_fwd_start_" + tag)
        return self.forward[tag][3]

    def rest(self, tag, after):
        sems, srcs, lands, _ = self.forward[tag]
        _, lands = _exchange_wait("forward", sems, srcs, lands, after, "ag_fwd_wait_" + tag)
        wf = dict(zip(self.names[tag], lands))
        for n in wf:
            if n in ("w_out", "w_cq", "w_ckv", "w_down"):
                wf[n] = wf[n].reshape(-1, wf[n].shape[-1])
        return wf

    def pair_start(self, name, grad):
        got = lax.empty((N_CHIP,) + grad.shape[1:], grad.dtype)
        self.pairs[name] = _exchange_start("pair", [grad], [got], None, "rs_pair_start_" + name)
        return self.pairs[name][3]

    def emit(self, grads, after):
        names, tag = list(grads), "_%d" % len(self.groups)
        grads, got = dict(grads), {}
        for n in names:
            if n in self.pairs:
                sems, srcs, lands, _ = self.pairs[n]
                srcs, lands = _exchange_wait("pair", sems, srcs, lands, after, "rs_pair_wait_" + n)
                grads[n], got[n] = srcs[0], lands[0]
        sync = [n for n in names if n not in got]
        if sync:
            got.update(zip(sync, _pair_exchange([grads[n] for n in sync], "rs_pair" + tag)))
        sums = [_pair_sum(grads[n], got[n], "rs_sum_" + n) for n in names]
        sems, srcs, lands, token = _exchange_start("scatter", [q for q, _ in sums], [l for _, l in sums], None,
                                                   "rs_chip_start" + tag)
        self.groups.append((names, sems, srcs, lands))
        return token

    def finish(self, gi, after):
        names, sems, srcs, lands = self.groups[gi]
        _, lands = _exchange_wait("scatter", sems, srcs, lands, after, "rs_chip_wait_%d" % gi)
        return dict(zip(names, lands))


def _tie(v, token):
    return v if token is None else v + token[0:1, 0:1]


def _local_step(x, mem, pos_col, target, p, wf, hooks=None):
    p = dict(p)
    inv = np.zeros((1, LANES), np.float32)
    freq = (ROPE_THETA ** (-2.0 * np.arange(ROT // 2, dtype=np.float32) / ROT)).astype(np.float32)
    for l in range(LANES):
        if l % HEAD < ROT:
            inv[0, l] = freq[(l % HEAD) % (ROT // 2)]
    inv_tab = jnp.asarray(inv)
    gq128, gk128 = jnp.tile(p["g_q"], (1, 2)), jnp.tile(p["g_k"], (1, 2))
    dtb128, alog128 = _pad_lanes(p["dt_bias"]), _pad_lanes(p["a_log"])
    dskip_b = jnp.repeat(p["d_skip"], HEAD, axis=1)

    h = _rms_fwd(x, p["g_mix"], "rms_mix")
    if hooks is not None:
        got = hooks.first(h)
        wf = {**wf, "w_in": _w_in_from_blocks(got["w_in"]),
              "conv_w": got["conv_w"].transpose(1, 0, 2).reshape(CONV_W, D_CONV)}
    conv_w, conv_b = wf["conv_w"], p["conv_b"]
    proj = _matmul(h, wf["w_in"], mode="nn", name="mm_in", tm=1024, tn=1280, tk=D)
    qr, kr = _qk_prep(proj, pos_col, gq128, gk128, inv_tab)
    attn, lse = _attn_fwd(qr, kr, proj)
    attn_n = _attn_norm(attn, p["g_attn_out"])

    xbc = _conv_fwd(proj, conv_w, conv_b)
    token = None if hooks is None else hooks.rest_mid("a", xbc)
    dt, a_cs, dt_b, a_b = _ssd_prep(proj, _tie(dtb128, token), alog128)
    at_r, dt_r = _row_layout(a_cs[:, :N_HEADS]), _row_layout(dt[:, :N_HEADS])
    y_ssd, h_all = _ssd_fwd(xbc, a_b, dt_b, at_r, dt_r)
    ssm_n = _ssd_post(y_ssd, xbc, proj, dskip_b, p["g_ssm_out"])

    if hooks is not None:
        wf = {**wf, **hooks.rest("a", ssm_n)}
    mix = jnp.concatenate([attn_n, ssm_n], axis=1)
    x1 = _matmul(mix, wf["w_out"], mode="nn", name="mm_out", tm=1024, tn=1024, tk=D,
                 extras=(x,), epilogue=lambda acc, r: (acc + r,))
    hc = _rms_fwd(x1, _tie(p["g_cross"], None if hooks is None else hooks.rest_mid("b", x1)), "rms_cross")
    memh = _rms_fwd(mem, p["g_mem"], "rms_mem")
    qc = _matmul(hc, wf["w_cq"], mode="nn", name="mm_cq", tm=1024, tn=512, tk=D)
    kv = _matmul(memh, wf["w_ckv"], mode="nn", name="mm_ckv", tm=N_MEM, tn=1024, tk=D)
    oc = _cross_fwd(qc, kv, p["g_cq"], p["g_ck"])
    x2 = _matmul(oc, wf["w_co"], mode="nn", name="mm_co", tm=1024, tn=256, tk=512, b_cb=256,
                 extras=(x1,), epilogue=lambda acc, r: (acc + r,))
    hm = _rms_fwd(x2, p["g_mlp"], "rms_mlp")
    if hooks is not None:
        wf = {**wf, **hooks.rest("b", hm)}

    def up_epi(acc):
        ru = jnp.maximum(acc, 0.0)
        return ru * ru, 2.0 * ru

    act, relu2 = _matmul(hm, wf["w_up"], mode="nn", name="mm_up", tm=1024, tn=1024, tk=D, b_cb=1024,
                         out_dtypes=(BF, BF), epilogue=up_epi)
    if hooks is not None:
        hooks.rest_mid("c", act)
        wf = {**wf, **hooks.rest("c", relu2)}

    def loss_epi(acc, r, t):
        d = (acc + r - t) * (1.0 / D)
        return d, d

    dy, dyb = _matmul(act, wf["w_down"], mode="nn", name="mm_down", tm=512, tn=1024, tk=2048, extras=(x2, target),
                      out_dtypes=(F32, BF), epilogue=loss_epi)

    gb, gs = {}, {}
    gb["w_down"] = _matmul(act, dyb, mode="tn", name="mm_down_dw", tm=1024, tn=1024, tk=S,
                           out_dtypes=(BF,)).reshape(N_DEV, D_FF // N_DEV, D)
    token = None if hooks is None else hooks.pair_start("w_down", gb["w_down"])
    du = _matmul(dyb, wf["w_down"], mode="nt", name="mm_down_dx", tm=1024, tn=1024, tk=D, extras=(relu2,),
                 out_dtypes=(BF,), epilogue=lambda acc, r: (acc * r.astype(F32),), after=token)
    gb["w_up"] = _matmul(hm, du, mode="tn", name="mm_up_dw", tm=1024, tn=1024, tk=S, out_dtypes=(BF,), out_cb=1024)
    token = None if hooks is None else hooks.pair_start("w_up", gb["w_up"])
    dhm = _matmul(du, wf["w_up"], mode="nt", name="mm_up_dx", tm=1024, tn=1024, tk=2048, b_cb=1024, after=token)
    if hooks is not None:
        p["g_mlp"] = _tie(p["g_mlp"], hooks.emit({n: gb[n] for n in ("w_down", "w_up")}, dhm))
    dx2, dx2b, gs["g_mlp"], sumsq_dy = _rms_bwd_call(x2, p["g_mlp"], dhm, dy, "rms_mlp_bwd", sumsq_res=True)
    loss_part = sumsq_dy[0, 0] * (0.5 * D)

    doc = _matmul(dx2b, wf["w_co"], mode="nt", name="mm_co_dx", tm=1024, tn=512, tk=256, b_cb=256)
    gb["w_co"] = _matmul(oc, dx2b, mode="tn", name="mm_co_dw", tm=512, tn=256, tk=S, out_dtypes=(BF,), out_cb=256)
    dqc, dkn, dvc, gs["g_cq"] = _cross_bwd(qc, doc, kv, p["g_cq"], p["g_ck"])
    dkv, gs["g_ck"] = _cross_kv_bwd(kv, dkn, dvc, p["g_ck"])
    gb["w_cq"] = _matmul(hc, dqc, mode="tn", name="mm_cq_dw", tm=1024, tn=512, tk=S,
                         out_dtypes=(BF,)).reshape(N_DEV, D // N_DEV, D_CROSS)
    dhc = _matmul(dqc, wf["w_cq"], mode="nt", name="mm_cq_dx", tm=1024, tn=1024, tk=512)
    gb["w_ckv"] = _matmul(memh, dkv, mode="tn", name="mm_ckv_dw", tm=1024, tn=1024, tk=N_MEM,
                          out_dtypes=(BF,)).reshape(N_DEV, D // N_DEV, 2 * D_CROSS)
    dmemh = _matmul(dkv, wf["w_ckv"], mode="nt", name="mm_ckv_dx", tm=N_MEM, tn=1024, tk=1024)
    (gs["g_mem"],) = _rms_bwd_call(mem, p["g_mem"], dmemh, None, "rms_mem_bwd")
    dx1, dx1b, gs["g_cross"] = _rms_bwd_call(x1, p["g_cross"], dhc, dx2, "rms_cross_bwd")

    dmix = _matmul(dx1b, wf["w_out"], mode="nt", name="mm_out_dx", tm=1024, tn=1024, tk=D)
    gb["w_out"] = _matmul(mix, dx1b, mode="tn", name="mm_out_dw", tm=1024, tn=1024, tk=S,
                          out_dtypes=(BF,)).reshape(N_DEV, D // N_DEV, D)
    if hooks is not None:
        p["g_attn_out"] = _tie(p["g_attn_out"],
                               hooks.emit({n: gb[n] for n in ("w_co", "w_cq", "w_ckv", "w_out")}, dmix))

    dattn, delta, gs["g_attn_out"] = _attn_merge_bwd(dmix, attn, p["g_attn_out"])
    dq_rot, dk_rot, dv_attn = _attn_bwd(qr, kr, proj, dattn, lse, delta)
    dq_pre, dk_pre, dv_pre, dgq, dgk = _qk_bwd(dq_rot, dk_rot, dv_attn, proj, pos_col, gq128, gk128, inv_tab)
    gs["g_q"], gs["g_k"] = dgq[:, :HEAD], dgk[:, :HEAD]

    dy_ssd, dz, dxs_skip, dd_lane, gs["g_ssm_out"] = _ssd_post_bwd(y_ssd, xbc, proj, dmix, dskip_b, p["g_ssm_out"])
    gs["d_skip"] = dd_lane.reshape(N_HEADS, HEAD).sum(axis=1).reshape(1, N_HEADS)
    dc_pair, daq_b, dx_ssd, db_pair, dak_r, ddt_r, ddt_b = _ssd_bwd(xbc, dy_ssd, h_all, a_b, dt_b, at_r, dt_r)
    dak = _pad_lanes(_row_layout_inv(dak_r))
    ddt_direct = _pad_lanes(_row_layout_inv(ddt_r))
    ddt_raw, dalog, dbias = _ssd_prep_bwd(daq_b, dak, ddt_direct, ddt_b, dt, proj, dtb128, alog128)
    gs["a_log"], gs["dt_bias"] = dalog[:, :N_HEADS], dbias[:, :N_HEADS]
    same = lambda c: c
    dxbc_x, dcw_x, dcb_x = _conv_bwd(proj, conv_w, conv_b, dxs_skip, dx_ssd, same, same, 0, 8, "conv_bwd_x")
    dxbc_b, dcw_b, dcb_b = _conv_bwd(proj, conv_w, conv_b, db_pair, db_pair, lambda c: 2 * c, lambda c: 2 * c + 1,
                                     D_SSM, 4, "conv_bwd_b")
    dxbc_c, dcw_c, dcb_c = _conv_bwd(proj, conv_w, conv_b, dc_pair, dc_pair, lambda c: 2 * c, lambda c: 2 * c + 1,
                                     D_SSM + 512, 4, "conv_bwd_c")
    g_conv_w = jnp.concatenate([dcw_x, dcw_b, dcw_c], axis=1)
    gs["conv_b"] = jnp.concatenate([dcb_x, dcb_b, dcb_c], axis=1)

    pieces = [dq_pre, dk_pre, dv_pre, dz, dxbc_x, dxbc_b, dxbc_c, ddt_raw]
    pieces.append(jnp.zeros((S, D_INP - sum(t.shape[1] for t in pieces)), BF))
    dproj = jnp.concatenate(pieces, axis=1)
    dw_in = _matmul(h, dproj, mode="tn", name="mm_in_dw", tm=1024, tn=1280, tk=S, out_dtypes=(BF,))
    gb["w_in"] = _w_in_to_blocks(dw_in)
    token = None if hooks is None else hooks.emit({"w_in": gb["w_in"]}, dw_in)
    dh = _matmul(dproj, wf["w_in"], mode="nt", name="mm_in_dx", tm=1024, tn=512, tk=D_INP // 2, after=token)
    grad_x, _, gs["g_mix"] = _rms_bwd_call(x, p["g_mix"], dh, dx1, "rms_mix_bwd")
    return loss_part, grad_x, gb, gs, g_conv_w


def kernel(x, mem, positions, g_mix, w_in, g_q, g_k, g_attn_out, conv_w, conv_b, dt_bias, a_log, d_skip, g_ssm_out, w_out, g_cross, g_mem, w_cq, w_ckv, g_cq, g_ck, w_co, g_mlp, w_up, w_down, loss_target, m_g_mix, m_w_in, m_g_q, m_g_k, m_g_attn_out, m_conv_w, m_conv_b, m_dt_bias, m_a_log, m_d_skip, m_g_ssm_out, m_w_out, m_g_cross, m_g_mem, m_w_cq, m_w_ckv, m_g_cq, m_g_ck, m_w_co, m_g_mlp, m_w_up, m_w_down, v_g_mix, v_w_in, v_g_q, v_g_k, v_g_attn_out, v_conv_w, v_conv_b, v_dt_bias, v_a_log, v_d_skip, v_g_ssm_out, v_w_out, v_g_cross, v_g_mem, v_w_cq, v_w_ckv, v_g_cq, v_g_ck, v_w_co, v_g_mlp, v_w_up, v_w_down):
    args = dict(locals())
    w = {n: args[n] for n in WEIGHTS}
    m = {n: args["m_" + n] for n in WEIGHTS}
    v = {n: args["v_" + n] for n in WEIGHTS}
    small = {n: w[n] for n, _ in SMALL}
    me = 4 * lax.axis_index("x") + 2 * lax.axis_index("y") + lax.axis_index("c")

    overlap = _Overlap({"w_in": w["w_in"][0].astype(BF), "conv_w": w["conv_w"][0]})
    tok = overlap.token
    overlap.shards = {n: _tie(w[n][0], tok).astype(BF) for n in REST}
    adam_in = {"w_in": tuple(_tie(t["w_in"][0], tok) for t in (w, m, v))}
    overlap.behind = list(adam_in["w_in"])
    p = dict(small)
    p["g_mix"] = _tie(p["g_mix"], tok)
    loss_part, grad_x, _, gs, g_conv_w = _local_step(
        x[0], mem[0], positions.reshape(S, 1), loss_target[0], p, {}, overlap)
    loss = lax.psum(loss_part, ("x", "y", "c"))

    out = {}
    last = grad_x
    for gi in range(3):
        for n, parts in overlap.finish(gi, last).items():
            w_n, m_n, v_n = adam_in.get(n, (w[n][0], m[n][0], v[n][0]))
            res_n = _adamw(w_n, m_n, v_n, parts, "adamw_" + n)
            out[n] = [t.reshape(w[n].shape) for t in res_n]
            last = res_n[0]

    sm_parts, cw_parts = _all_gather([_pack_small(gs), g_conv_w], "ag_small_grads", after=last)
    sm = [_unpack_small(t) for t in _adamw(_pack_small(small), _pack_small({n: m[n] for n, _ in SMALL}),
                                           _pack_small({n: v[n] for n, _ in SMALL}), sm_parts, "adamw_small")]
    for n, _ in SMALL:
        out[n] = [t[n] for t in sm]
    cols = D_CONV // N_DEV
    cw_mine = lax.dynamic_slice_in_dim(cw_parts, me * cols, cols, axis=2)
    out["conv_w"] = [t.reshape(w["conv_w"].shape) for t in
                     _adamw(w["conv_w"][0], m["conv_w"][0], v["conv_w"][0], cw_mine, "adamw_conv_w")]

    res = [loss, grad_x.reshape(x.shape)]
    for k in range(4):
        res += [out[n][k] for n in WEIGHTS]
    return tuple(res)
```

```python
import functools
import math

import numpy as np
import jax
import jax.numpy as jnp
from jax import lax
from jax.experimental import pallas as pl
from jax.experimental.pallas import tpu as pltpu

F32 = jnp.float32
BF = jnp.bfloat16

N_DEV = 8
S = 2048
D = 2048
D_ATTN = 1024
N_HEADS = 16
HEAD = 64
ROT = 16
ROPE_THETA = 500000.0
D_SSM = 1024
D_CONV = 2048
CONV_W = 4
N_MEM = 256
D_CROSS = 512
CROSS_HEAD = 128
D_FF = 8192
D_IN = 6160
D_INP = 6400
DT_COLBLK = (4 * D_ATTN + D_CONV) // 128
EPS = 1e-6
BLK = 128
NB = S // BLK
LANES = 128
NEG = -1e30

ADAM_LR = 0.001
ADAM_B1 = 0.9
ADAM_B2 = 0.999
ADAM_EPS = 1e-08
ADAM_WD = 0.01
ADAM_STEP = 10

VMEM_LIMIT_BYTES = 48 * 1024 * 1024
ROW_TILE = 256


def _params(*sem):
    return pltpu.CompilerParams(dimension_semantics=sem, vmem_limit_bytes=VMEM_LIMIT_BYTES)


def _matmul(a, b, *, mode, name, tm, tn, tk, out_dtypes=(F32,), b_cb=None, out_cb=None,
            extras=(), epilogue=None, after=None):
    if mode == "tn":
        kk, m = a.shape
    else:
        m, kk = a.shape
    if b_cb is None:
        br, bc = b.shape
    else:
        br, bc = b.shape[1], N_DEV * b_cb
    n = br if mode == "nt" else bc
    assert m % tm == 0 and n % tn == 0 and kk % tk == 0, (name, m, n, kk)
    nk = kk // tk
    grid = (m // tm, n // tn, nk)

    if mode == "tn":
        a_spec = pl.BlockSpec((tk, tm), lambda i, j, k: (k, i))
    else:
        a_spec = pl.BlockSpec((tm, tk), lambda i, j, k: (i, k))
    if mode == "nt":
        b_blk, b_idx = (tn, tk), (lambda i, j, k: (j, k))
    else:
        b_blk, b_idx = (tk, tn), (lambda i, j, k: (k, j))
    group = 1
    if b_cb is None:
        b_spec = pl.BlockSpec(b_blk, b_idx)
    elif mode == "nt" and tk > b_cb:
        assert tk % b_cb == 0
        group = tk // b_cb
        b_spec = pl.BlockSpec((group, tn, b_cb), lambda i, j, k: (k, j, 0))
    else:
        tc = b_blk[1]
        assert b_cb % tc == 0
        per = b_cb // tc

        def b_idx3(i, j, k):
            r, c = b_idx(i, j, k)
            return (c // per, r, c % per)
        b_spec = pl.BlockSpec((None,) + b_blk, b_idx3)
    ex_specs = [pl.BlockSpec((tm, tn), lambda i, j, k: (i, j)) for _ in extras]
    if out_cb is None:
        out_shape = [jax.ShapeDtypeStruct((m, n), dt) for dt in out_dtypes]
        out_specs = [pl.BlockSpec((tm, tn), lambda i, j, k: (i, j)) for _ in out_dtypes]
    else:
        assert out_cb % tn == 0
        pero = out_cb // tn
        out_shape = [jax.ShapeDtypeStruct((N_DEV, m, out_cb), dt) for dt in out_dtypes]
        out_specs = [pl.BlockSpec((None, tm, tn), lambda i, j, k: (j // pero, i, j % pero)) for _ in out_dtypes]
    dn = {"nn": (((1,), (0,)), ((), ())), "nt": (((1,), (1,)), ((), ())), "tn": (((0,), (0,)), ((), ()))}[mode]
    ne, no = len(extras), len(out_dtypes)
    n_after = 0 if after is None else 1

    def kern(a_ref, b_ref, *rest):
        ex_refs, out_refs = rest[:ne], rest[ne + n_after:ne + n_after + no]

        def finish(acc):
            outs = (acc,) if epilogue is None else epilogue(acc, *[r[...] for r in ex_refs])
            for r, o in zip(out_refs, outs):
                r[...] = o.astype(r.dtype)

        if group == 1:
            part = lax.dot_general(a_ref[...].astype(BF), b_ref[...].astype(BF), dn, preferred_element_type=F32)
        else:
            part = sum(lax.dot_general(a_ref[:, g * b_cb:(g + 1) * b_cb].astype(BF), b_ref[g].astype(BF), dn,
                                       preferred_element_type=F32) for g in range(group))
        if nk == 1:
            finish(part)
            return
        acc_ref = rest[ne + n_after + no]
        k = pl.program_id(2)

        @pl.when(k == 0)
        def _():
            acc_ref[...] = part

        @pl.when(k > 0)
        def _():
            acc_ref[...] += part

        @pl.when(k == nk - 1)
        def _():
            finish(acc_ref[...])

    res = pl.pallas_call(
        kern, name=name, grid=grid, in_specs=[a_spec, b_spec] + ex_specs + [ANY] * n_after, out_specs=out_specs,
        out_shape=out_shape, scratch_shapes=[pltpu.VMEM((tm, tn), F32)] if nk > 1 else [],
        compiler_params=_params("parallel", "parallel", "arbitrary"),
    )(a, b, *extras, *([] if after is None else [after]))
    return res[0] if no == 1 else tuple(res)


def _rowwise(body, *, name, nrows, tile, row_ins, full_ins=(), row_outs=(), acc_outs=(), scratch=(),
             reverse=False):
    n = nrows // tile

    def ridx(i):
        return (n - 1 - i) if reverse else i

    in_specs, args = [], []
    for arr, width, cb in row_ins:
        in_specs.append(pl.BlockSpec((tile, width), lambda i, cb=cb: (ridx(i), cb)))
        args.append(arr)
    for arr in full_ins:
        in_specs.append(pl.BlockSpec(arr.shape, lambda i, nd=arr.ndim: (0,) * nd))
        args.append(arr)
    out_shape, out_specs = [], []
    for width, dt in row_outs:
        out_shape.append(jax.ShapeDtypeStruct((nrows, width), dt))
        out_specs.append(pl.BlockSpec((tile, width), lambda i: (ridx(i), 0)))
    for shp, dt in acc_outs:
        out_shape.append(jax.ShapeDtypeStruct(shp, dt))
        out_specs.append(pl.BlockSpec(shp, lambda i, nd=len(shp): (0,) * nd))

    def kern(*refs):
        body(pl.program_id(0), n, *refs)

    return pl.pallas_call(kern, name=name, grid=(n,), in_specs=in_specs, out_specs=out_specs,
                          out_shape=out_shape, scratch_shapes=list(scratch),
                          compiler_params=_params("arbitrary"))(*args)


def _accum(ref, val, i):
    @pl.when(i == 0)
    def _():
        ref[...] = val

    @pl.when(i > 0)
    def _():
        ref[...] += val


def _sigmoid(x):
    return 1.0 / (1.0 + jnp.exp(-x))


def _rms_n(x):
    r = lax.rsqrt(jnp.mean(x * x, axis=-1, keepdims=True) + EPS)
    return x * r, r


def _rms_bwd(x, g, dh):
    n, r = _rms_n(x)
    dn = dh * g
    dx = r * (dn - n * jnp.mean(dn * n, axis=-1, keepdims=True))
    return dx, jnp.sum(dh * n, axis=0, keepdims=True)


def _seg_sum(x, seg):
    t, w = x.shape
    tiles = [x[:, LANES * j:LANES * (j + 1)] for j in range(w // LANES)]
    outs = []
    if seg == 64:
        lo = lax.broadcasted_iota(jnp.int32, (t, LANES), 1) < 64
        for xt in tiles:
            s_lo = jnp.sum(jnp.where(lo, xt, 0.0), axis=-1, keepdims=True)
            s_hi = jnp.sum(jnp.where(lo, 0.0, xt), axis=-1, keepdims=True)
            outs.append(jnp.where(lo, s_lo, s_hi))
    else:
        sums = [jnp.sum(xt, axis=-1, keepdims=True) for xt in tiles]
        if seg == 256:
            sums = [sums[2 * (j // 2)] + sums[2 * (j // 2) + 1] for j in range(len(sums))]
        else:
            assert seg == 128
        outs = [jnp.broadcast_to(s, (t, LANES)) for s in sums]
    return outs[0] if len(outs) == 1 else jnp.concatenate(outs, axis=1)


def _seg_rms_n(x, seg):
    r = lax.rsqrt(_seg_sum(x * x, seg) * (1.0 / seg) + EPS)
    return x * r, r


def _seg_rms_bwd(x, g, dy, seg):
    n, r = _seg_rms_n(x, seg)
    dn = dy * g
    dx = r * (dn - n * (_seg_sum(dn * n, seg) * (1.0 / seg)))
    return dx, jnp.sum(dy * n, axis=0, keepdims=True)


def _rope_tables(pos_col, inv_tab, t):
    ang = pos_col.astype(F32) * inv_tab
    idx = lax.broadcasted_iota(jnp.int32, (t, LANES), 1) % HEAD
    cos, sin = jnp.cos(ang), jnp.sin(ang)
    c = jnp.where(idx < ROT, cos, 1.0)
    sg = jnp.where(idx < ROT // 2, -sin, jnp.where(idx < ROT, sin, 0.0))
    return c, sg, idx < ROT // 2


def _rope(x, c, sg, lo):
    partner = jnp.where(lo, pltpu.roll(x, LANES - ROT // 2, 1), pltpu.roll(x, ROT // 2, 1))
    return x * c + partner * sg


def _fold_heads(v):
    v8 = jnp.broadcast_to(v, (8, LANES))
    return (v8 + pltpu.roll(v8, HEAD, 1))[0:1]


def _dot(a, b, dn):
    return lax.dot_general(a, b, (dn, ((), ())), preferred_element_type=F32)


NN = ((1,), (0,))
NT = ((1,), (1,))
TN = ((0,), (0,))


def _dot3(l01, x):
    x1 = x.astype(BF)
    r1 = x - x1.astype(F32)
    x2 = r1.astype(BF)
    x3 = (r1 - x2.astype(F32)).astype(BF)
    return _dot(l01, x1, NN) + _dot(l01, x2, NN) + _dot(l01, x3, NN)


def _rms_fwd(x, g, name):
    rows = x.shape[0]

    def body(i, n, x_ref, g_ref, o_ref):
        nx, _ = _rms_n(x_ref[...])
        o_ref[...] = (nx * g_ref[...]).astype(BF)

    return _rowwise(body, name=name, nrows=rows, tile=min(ROW_TILE, rows), row_ins=[(x, D, 0)],
                    full_ins=[g], row_outs=[(D, BF)])[0]


def _qk_prep(proj, pos_col, gq128, gk128, inv_tab):
    scale = HEAD ** -0.5

    def body(i, n, q_ref, k_ref, p_ref, gq_ref, gk_ref, it_ref, qo_ref, ko_ref):
        t = q_ref.shape[0]
        c, sg, lo = _rope_tables(p_ref[...], it_ref[...], t)
        for j in range(D_ATTN // LANES):
            sl = slice(LANES * j, LANES * (j + 1))
            qn, _ = _seg_rms_n(q_ref[:, sl], HEAD)
            kn, _ = _seg_rms_n(k_ref[:, sl], HEAD)
            qo_ref[:, sl] = _rope(qn * gq_ref[...], c, sg, lo) * scale
            ko_ref[:, sl] = _rope(kn * gk_ref[...], c, sg, lo)

    return _rowwise(body, name="qk_prep", nrows=S, tile=ROW_TILE,
                    row_ins=[(proj, D_ATTN, 0), (proj, D_ATTN, 1), (pos_col, 1, 0)],
                    full_ins=[gq128, gk128, inv_tab], row_outs=[(D_ATTN, F32)] * 2)


ATTN_QB = 16
V_COLS = pl.BlockSpec((S, LANES), lambda p: (0, 2 * D_ATTN // LANES + p))


def _band(b, d):
    nb = NB // d
    r, n = b // nb, b % nb
    width, kn = (BLK, n) if nb == 1 else (2 * BLK, jnp.maximum(n - 1, 0))

    def rows(first_member, count):
        if d == 1:
            return pl.ds(pl.multiple_of(first_member, BLK), count)
        return pl.ds(first_member * d + r, count, stride=d)

    qm = n * BLK + (lax.broadcasted_iota(jnp.int32, (2 * BLK, width), 0) & (BLK - 1))
    km = kn * BLK + lax.broadcasted_iota(jnp.int32, (2 * BLK, width), 1)
    valid = km <= qm
    if nb > 1:
        valid = valid & (km >= qm - BLK)
    return rows(n * BLK, BLK), rows(kn * BLK, width), valid


DILATIONS = (1, 4, 16)


def _attn_fwd(q, k, v):
    def kern(q_ref, k_ref, v_ref, a_ref, lse_ref, *scr):
        half0 = lax.broadcasted_iota(jnp.int32, (BLK, LANES), 1) < HEAD
        for gi, d in enumerate(DILATIONS):
            o_ref, l_ref = scr[2 * gi], scr[2 * gi + 1]

            def group(g, carry, d=d, o_ref=o_ref, l_ref=l_ref):
                for bb in range(ATTN_QB):
                    q_rows, k_rows, valid = _band(g * ATTN_QB + bb, d)
                    q = q_ref[q_rows, :]
                    kb, vb = k_ref[k_rows, :].astype(BF), v_ref[k_rows, :].astype(BF)
                    q2 = jnp.concatenate([jnp.where(half0, q, 0.0), jnp.where(half0, 0.0, q)], axis=0).astype(BF)
                    s = jnp.where(valid, _dot(q2, kb, NT), NEG)
                    m = jnp.max(s, axis=-1, keepdims=True)
                    p = jnp.exp(s - m)
                    den = jnp.sum(p, axis=-1, keepdims=True)
                    o2 = _dot(p.astype(BF), vb, NN) / den
                    l2 = m + jnp.log(den)
                    o_ref[q_rows, :] = jnp.where(half0, o2[:BLK], o2[BLK:])
                    l_ref[q_rows, :] = jnp.where(half0, l2[:BLK], l2[BLK:])
                return carry

            lax.fori_loop(0, NB // ATTN_QB, group, 0)

        def merge(i, carry):
            rows = pl.ds(pl.multiple_of(i * ROW_TILE, ROW_TILE), ROW_TILE)
            la, lb, lc = scr[1][rows, :], scr[3][rows, :], scr[5][rows, :]
            m = jnp.maximum(jnp.maximum(la, lb), lc)
            ea, eb, ec = jnp.exp(la - m), jnp.exp(lb - m), jnp.exp(lc - m)
            den = ea + eb + ec
            a_ref[rows, :] = (ea * scr[0][rows, :] + eb * scr[2][rows, :] + ec * scr[4][rows, :]) / den
            lse_ref[rows, :] = m + jnp.log(den)
            return carry

        lax.fori_loop(0, S // ROW_TILE, merge, 0)

    seq = pl.BlockSpec((S, LANES), lambda p: (0, p))
    return pl.pallas_call(
        kern, name="attn_fwd", grid=(D_ATTN // LANES,), in_specs=[seq, seq, V_COLS], out_specs=[seq, seq],
        out_shape=[jax.ShapeDtypeStruct((S, D_ATTN), F32)] * 2,
        scratch_shapes=[pltpu.VMEM((S, LANES), F32)] * (2 * len(DILATIONS)),
        compiler_params=_params("parallel"))(q, k, v)


def _attn_norm(attn, g_attn):
    def body(i, n, a_ref, g_ref, an_ref):
        nx, _ = _rms_n(a_ref[...])
        an_ref[...] = (nx * g_ref[...]).astype(BF)

    return _rowwise(body, name="attn_norm", nrows=S, tile=ROW_TILE, row_ins=[(attn, D_ATTN, 0)],
                    full_ins=[g_attn], row_outs=[(D_ATTN, BF)])[0]


def _conv_taps(x, w_ref):
    rows = lax.broadcasted_iota(jnp.int32, x.shape, 0)
    shifted = []
    for w in range(CONV_W):
        k = CONV_W - 1 - w
        shifted.append(x if k == 0 else jnp.where(rows >= k, pltpu.roll(x, k, 0), 0.0))
    acc = shifted[0] * w_ref[0:1, :]
    for w in range(1, CONV_W):
        acc = acc + shifted[w] * w_ref[w:w + 1, :]
    return acc, shifted


def _conv_fwd(proj, conv_w, conv_b):
    tc = 256

    def kern(x_ref, w_ref, b_ref, o_ref):
        c, _ = _conv_taps(x_ref[...], w_ref)
        c = c + b_ref[...]
        o_ref[...] = c * _sigmoid(c)

    return pl.pallas_call(
        kern, name="conv_fwd", grid=(D_CONV // tc,),
        in_specs=[pl.BlockSpec((S, tc), lambda c: (0, 4 * D_ATTN // tc + c)),
                  pl.BlockSpec((CONV_W, tc), lambda c: (0, c)), pl.BlockSpec((1, tc), lambda c: (0, c))],
        out_specs=pl.BlockSpec((S, tc), lambda c: (0, c)),
        out_shape=jax.ShapeDtypeStruct((S, D_CONV), F32), compiler_params=_params("parallel"))(proj, conv_w, conv_b)


def _softplus(x):
    return jnp.maximum(x, 0.0) + jnp.log1p(jnp.exp(-jnp.abs(x)))


def _dot3r(x, r01):
    x1 = x.astype(BF)
    r1 = x - x1.astype(F32)
    x2 = r1.astype(BF)
    x3 = (r1 - x2.astype(F32)).astype(BF)
    return _dot(x1, r01, NN) + _dot(x2, r01, NN) + _dot(x3, r01, NN)


def _spread_heads(v):
    sel = (lax.broadcasted_iota(jnp.int32, (LANES, D_SSM), 1) // HEAD
           == lax.broadcasted_iota(jnp.int32, (LANES, D_SSM), 0))
    return _dot3r(v, sel.astype(BF))


def _collect_heads(v):
    sel = (lax.broadcasted_iota(jnp.int32, (D_SSM, LANES), 0)
           == HEAD * lax.broadcasted_iota(jnp.int32, (D_SSM, LANES), 1))
    return _dot3r(v, sel.astype(BF))


def _ssd_prep(proj, dt_bias128, a_log128):
    def body(i, n, raw_ref, b_ref, al_ref, dt_ref, a_ref, dtb_ref, ab_ref, carry):
        @pl.when(i == 0)
        def _():
            carry[...] = jnp.zeros_like(carry)

        dt = _softplus(raw_ref[...] + b_ref[...])
        da = dt * (-jnp.exp(al_ref[...]))
        tri = (lax.broadcasted_iota(jnp.int32, (BLK, BLK), 0) >= lax.broadcasted_iota(jnp.int32, (BLK, BLK), 1))
        cs = _dot3(tri.astype(BF), da) + carry[0:1, :]
        dt_ref[...] = dt
        a_ref[...] = cs
        dtb_ref[...] = _spread_heads(dt)
        ab_ref[...] = _spread_heads(cs)
        carry[...] = jnp.broadcast_to(cs[BLK - 1:BLK, :], carry.shape)

    return _rowwise(body, name="ssd_prep", nrows=S, tile=BLK, row_ins=[(proj, LANES, DT_COLBLK)],
                    full_ins=[dt_bias128, a_log128],
                    row_outs=[(LANES, F32), (LANES, F32), (D_SSM, F32), (D_SSM, F32)],
                    scratch=[pltpu.VMEM((8, LANES), F32)])


def _ssd_decay(a_col, at_row, causal):
    diff = jnp.where(causal, a_col - at_row, 0.0)
    return jnp.where(causal, jnp.exp(diff), 0.0)


SSD_CB = 16


def _ssd_fwd(xbc, a_b, dt_b, at_r, dt_r):
    def kern(c_ref, b_ref, x_ref, ab_ref, dtb_ref, at_ref, dt_ref, y_ref, hall_ref):
        half0 = lax.broadcasted_iota(jnp.int32, (BLK, LANES), 1) < HEAD
        causal = lax.broadcasted_iota(jnp.int32, (BLK, BLK), 1) <= lax.broadcasted_iota(jnp.int32, (BLK, BLK), 0)

        def step(i, carry):
            h, a_prev = carry
            for cc in range(SSD_CB):
                chunk = i * SSD_CB + cc
                rows = pl.ds(pl.multiple_of(chunk * BLK, BLK), BLK)
                ci, bi, x = c_ref[rows, :].astype(BF), b_ref[rows, :].astype(BF), x_ref[rows, :]
                ab = ab_ref[rows, :]
                a_end = ab[BLK - 1:BLK, :]
                alpha = jnp.exp(ab - a_prev)
                beta = jnp.exp(a_end - ab) * dtb_ref[rows, :]
                hall_ref[rows, :] = h
                cb = _dot(ci, bi, NT)
                at, dtj = at_ref[chunk], dt_ref[chunk]
                y = alpha * _dot(ci, h.astype(BF), NN)
                for hd in range(2):
                    hm = half0 if hd == 0 else jnp.logical_not(half0)
                    w = cb * _ssd_decay(ab[:, HEAD * hd:HEAD * hd + 1], at[hd:hd + 1, :], causal) * dtj[hd:hd + 1, :]
                    y = y + _dot(w.astype(BF), jnp.where(hm, x, 0.0).astype(BF), NN)
                y_ref[rows, :] = y
                h = alpha[BLK - 1:BLK, :] * h + _dot(bi, (beta * x).astype(BF), TN)
                a_prev = a_end
            return h, a_prev

        lax.fori_loop(0, NB // SSD_CB, step, (jnp.zeros((BLK, LANES), F32), jnp.zeros((1, LANES), F32)))

    npair = D_SSM // LANES
    rowvec = pl.BlockSpec((None, NB, 2, BLK), lambda p: (p, 0, 0, 0))
    seq = pl.BlockSpec((S, LANES), lambda p: (0, p))
    return pl.pallas_call(
        kern, name="ssd_fwd", grid=(npair,),
        in_specs=[pl.BlockSpec((S, LANES), lambda p: (0, 12 + p // 2)),
                  pl.BlockSpec((S, LANES), lambda p: (0, 8 + p // 2)), seq, seq, seq, rowvec, rowvec],
        out_specs=[seq, seq], out_shape=[jax.ShapeDtypeStruct((S, D_SSM), F32)] * 2,
        compiler_params=_params("parallel"))(xbc, xbc, xbc, a_b, dt_b, at_r, dt_r)


def _ssd_post(y_ssd, xbc, proj, dskip_b, g_ssm):
    def body(i, n, y_ref, xs_ref, z_ref, d_ref, g_ref, o_ref):
        z = z_ref[...]
        y2 = (y_ref[...] + d_ref[...] * xs_ref[...]) * (z * _sigmoid(z))
        nx, _ = _seg_rms_n(y2, 256)
        o_ref[...] = (nx * g_ref[...]).astype(BF)

    return _rowwise(body, name="ssd_post", nrows=S, tile=ROW_TILE,
                    row_ins=[(y_ssd, D_SSM, 0), (xbc, D_SSM, 0), (proj, D_SSM, 3)], full_ins=[dskip_b, g_ssm],
                    row_outs=[(D_SSM, BF)])[0]


def _cross_heads(q, kv_ref, gq, gk):
    out = []
    for h in range(D_CROSS // CROSS_HEAD):
        sl = slice(CROSS_HEAD * h, CROSS_HEAD * (h + 1))
        nq, rq = _rms_n(q[:, sl])
        nk, rk = _rms_n(kv_ref[:, sl])
        v = kv_ref[:, D_CROSS + CROSS_HEAD * h:D_CROSS + CROSS_HEAD * (h + 1)]
        out.append((sl, nq, rq, nk, rk, v))
    return out


def _cross_fwd(qc, kv, g_cq, g_ck):
    scale = CROSS_HEAD ** -0.5

    def body(i, n, q_ref, kv_ref, gq_ref, gk_ref, o_ref):
        for sl, nq, _, nk, _, v in _cross_heads(q_ref[...], kv_ref, gq_ref[...], gk_ref[...]):
            qn = (nq * gq_ref[...] * scale).astype(BF)
            kn = (nk * gk_ref[...]).astype(BF)
            s = _dot(qn, kn, NT)
            e = jnp.exp(s - jnp.max(s, axis=-1, keepdims=True))
            p = e / jnp.sum(e, axis=-1, keepdims=True)
            o_ref[:, sl] = _dot(p.astype(BF), v.astype(BF), NN).astype(BF)

    return _rowwise(body, name="cross_fwd", nrows=S, tile=ROW_TILE, row_ins=[(qc, D_CROSS, 0)],
                    full_ins=[kv, g_cq, g_ck], row_outs=[(D_CROSS, BF)])[0]


def _rms_bwd_call(x, g, dh, dres, name, sumsq_res=False):
    rows = x.shape[0]
    has_res = dres is not None

    def body(i, n, *refs):
        if has_res:
            x_ref, dh_ref, dr_ref, g_ref, dx_ref, dxb_ref, dg_ref = refs[:7]
        else:
            x_ref, dh_ref, g_ref, dg_ref = refs
        dx, dg = _rms_bwd(x_ref[...], g_ref[...], dh_ref[...])
        if has_res:
            dr = dr_ref[...]
            dx = dx + dr
            dx_ref[...] = dx
            dxb_ref[...] = dx.astype(BF)
            if sumsq_res:
                _accum(refs[7], jnp.sum(jnp.sum(dr * dr, axis=0, keepdims=True), axis=1, keepdims=True), i)
        _accum(dg_ref, dg, i)

    row_ins = [(x, D, 0), (dh, D, 0)] + ([(dres, D, 0)] if has_res else [])
    return _rowwise(body, name=name, nrows=rows, tile=min(ROW_TILE, rows), row_ins=row_ins, full_ins=[g],
                    row_outs=[(D, F32), (D, BF)] if has_res else [],
                    acc_outs=[((1, D), F32)] + ([((1, 1), F32)] if sumsq_res else []))


def _cross_bwd(qc, doc, kv, g_cq, g_ck):
    scale = CROSS_HEAD ** -0.5

    def body(i, n, q_ref, do_ref, kv_ref, gq_ref, gk_ref, dq_ref, dkn_ref, dv_ref, dgq_ref):
        dgq = jnp.zeros((1, CROSS_HEAD), F32)
        dkn_parts, dv_parts = [], []
        for sl, nq, rq, nk, _, v in _cross_heads(q_ref[...], kv_ref, gq_ref[...], gk_ref[...]):
            qn = (nq * gq_ref[...] * scale).astype(BF)
            kn = (nk * gk_ref[...]).astype(BF)
            s = _dot(qn, kn, NT)
            e = jnp.exp(s - jnp.max(s, axis=-1, keepdims=True))
            p = e / jnp.sum(e, axis=-1, keepdims=True)
            do = do_ref[:, sl].astype(BF)
            dv_parts.append(_dot(p.astype(BF), do, TN))
            dp = _dot(do, v.astype(BF), NT)
            ds = (p * (dp - jnp.sum(dp * p, axis=-1, keepdims=True))).astype(BF)
            dqn = _dot(ds, kn, NN)
            dkn_parts.append(_dot(ds, qn, TN))
            dn = dqn * (gq_ref[...] * scale)
            dq_ref[:, sl] = (rq * (dn - nq * jnp.mean(dn * nq, axis=-1, keepdims=True))).astype(BF)
            dgq = dgq + jnp.sum(dqn * scale * nq, axis=0, keepdims=True)
        _accum(dkn_ref, jnp.concatenate(dkn_parts, axis=1), i)
        _accum(dv_ref, jnp.concatenate(dv_parts, axis=1), i)
        _accum(dgq_ref, dgq, i)

    return _rowwise(body, name="cross_bwd", nrows=S, tile=ROW_TILE, row_ins=[(qc, D_CROSS, 0), (doc, D_CROSS, 0)],
                    full_ins=[kv, g_cq, g_ck], row_outs=[(D_CROSS, BF)],
                    acc_outs=[((N_MEM, D_CROSS), F32), ((N_MEM, D_CROSS), F32), ((1, CROSS_HEAD), F32)])


def _cross_kv_bwd(kv, dkn, dv, g_ck):
    def body(i, n, kv_ref, dkn_ref, dv_ref, gk_ref, dkv_ref, dgk_ref):
        dgk = jnp.zeros((1, CROSS_HEAD), F32)
        for h in range(D_CROSS // CROSS_HEAD):
            sl = slice(CROSS_HEAD * h, CROSS_HEAD * (h + 1))
            dk, dg = _rms_bwd(kv_ref[:, sl], gk_ref[...], dkn_ref[:, sl])
            dkv_ref[:, sl] = dk.astype(BF)
            dgk = dgk + dg
        dkv_ref[:, D_CROSS:] = dv_ref[...].astype(BF)
        dgk_ref[...] = dgk

    return _rowwise(body, name="cross_kv_bwd", nrows=N_MEM, tile=N_MEM,
                    row_ins=[(kv, 2 * D_CROSS, 0), (dkn, D_CROSS, 0), (dv, D_CROSS, 0)], full_ins=[g_ck],
                    row_outs=[(2 * D_CROSS, BF)], acc_outs=[((1, CROSS_HEAD), F32)])


def _attn_merge_bwd(dmix, attn, g_attn):
    def body(i, n, dn_ref, a_ref, g_ref, da_ref, dl_ref, dg_ref):
        attn_v = a_ref[...]
        da, dg = _rms_bwd(attn_v, g_ref[...], dn_ref[...])
        da_ref[...] = da
        dl_ref[...] = _seg_sum(da * attn_v, HEAD)
        _accum(dg_ref, dg, i)

    return _rowwise(body, name="attn_merge_bwd", nrows=S, tile=ROW_TILE,
                    row_ins=[(dmix, D_ATTN, 0), (attn, D_ATTN, 0)], full_ins=[g_attn],
                    row_outs=[(D_ATTN, F32), (D_ATTN, F32)], acc_outs=[((1, D_ATTN), F32)])


def _attn_bwd(q, k, v, do, lse, delta):
    def kern(q_ref, k_ref, v_ref, do_ref, l_ref, d_ref, dq_ref, dk_ref, dv_ref):
        dk_ref[...] = jnp.zeros_like(dk_ref)
        dv_ref[...] = jnp.zeros_like(dv_ref)
        half0 = lax.broadcasted_iota(jnp.int32, (BLK, LANES), 1) < HEAD
        for gi, d in enumerate(DILATIONS):

            def group(g, carry, d=d, first=(gi == 0)):
                updates = []
                for bb in range(ATTN_QB):
                    q_rows, k_rows, valid = _band(g * ATTN_QB + bb, d)
                    q, do = q_ref[q_rows, :], do_ref[q_rows, :]
                    lse_t, del_t = l_ref[q_rows, :], d_ref[q_rows, :]
                    kb, vb = k_ref[k_rows, :].astype(BF), v_ref[k_rows, :].astype(BF)
                    q2 = jnp.concatenate([jnp.where(half0, q, 0.0), jnp.where(half0, 0.0, q)], axis=0).astype(BF)
                    do2 = jnp.concatenate([jnp.where(half0, do, 0.0), jnp.where(half0, 0.0, do)], axis=0).astype(BF)
                    lse2 = jnp.concatenate([lse_t[:, 0:1], lse_t[:, HEAD:HEAD + 1]], axis=0)
                    del2 = jnp.concatenate([del_t[:, 0:1], del_t[:, HEAD:HEAD + 1]], axis=0)
                    s = jnp.where(valid, _dot(q2, kb, NT), NEG)
                    p = jnp.exp(s - lse2)
                    ds = (p * (_dot(do2, vb, NT) - del2)).astype(BF)
                    dq2 = _dot(ds, kb, NN)
                    dq = jnp.where(half0, dq2[:BLK], dq2[BLK:])
                    if first:
                        dq_ref[q_rows, :] = dq
                    else:
                        dq_ref[q_rows, :] += dq
                    updates.append((k_rows, _dot(ds, q2, TN), _dot(p.astype(BF), do2, TN)))
                for k_rows, dk, dv in updates:
                    dk_ref[k_rows, :] += dk
                    dv_ref[k_rows, :] += dv
                return carry

            lax.fori_loop(0, NB // ATTN_QB, group, 0)

    seq = pl.BlockSpec((S, LANES), lambda p: (0, p))
    return pl.pallas_call(
        kern, name="attn_bwd", grid=(D_ATTN // LANES,), in_specs=[seq, seq, V_COLS, seq, seq, seq],
        out_specs=[seq, seq, seq], out_shape=[jax.ShapeDtypeStruct((S, D_ATTN), F32)] * 3,
        compiler_params=_params("parallel"))(q, k, v, do, lse, delta)


def _qk_bwd(dq_rot, dk_rot, dv, proj, pos_col, gq128, gk128, inv_tab):
    scale = HEAD ** -0.5

    def body(i, n, dq_ref, dk_ref, dv_ref, q_ref, k_ref, p_ref, gq_ref, gk_ref, it_ref,
             dqo_ref, dko_ref, dvo_ref, dgq_ref, dgk_ref):
        t = q_ref.shape[0]
        c, sg, lo = _rope_tables(p_ref[...], it_ref[...], t)
        dgq = jnp.zeros((1, LANES), F32)
        dgk = jnp.zeros((1, LANES), F32)
        for j in range(D_ATTN // LANES):
            sl = slice(LANES * j, LANES * (j + 1))
            dqr = _rope(dq_ref[:, sl], c, -sg, lo) * scale
            dkr = _rope(dk_ref[:, sl], c, -sg, lo)
            dq, gq = _seg_rms_bwd(q_ref[:, sl], gq_ref[...], dqr, HEAD)
            dk, gk = _seg_rms_bwd(k_ref[:, sl], gk_ref[...], dkr, HEAD)
            dqo_ref[:, sl] = dq.astype(BF)
            dko_ref[:, sl] = dk.astype(BF)
            dgq, dgk = dgq + gq, dgk + gk
        dvo_ref[...] = dv_ref[...].astype(BF)
        _accum(dgq_ref, _fold_heads(dgq), i)
        _accum(dgk_ref, _fold_heads(dgk), i)

    return _rowwise(body, name="qk_bwd", nrows=S, tile=ROW_TILE,
                    row_ins=[(a, D_ATTN, 0) for a in (dq_rot, dk_rot, dv)]
                    + [(proj, D_ATTN, 0), (proj, D_ATTN, 1), (pos_col, 1, 0)],
                    full_ins=[gq128, gk128, inv_tab], row_outs=[(D_ATTN, BF)] * 3,
                    acc_outs=[((1, LANES), F32)] * 2)


def _ssd_post_bwd(y_ssd, xbc, proj, dmix, dskip_b, g_ssm):
    def body(i, n, y_ref, xs_ref, z_ref, do_ref, d_ref, g_ref, dy_ref, dz_ref, dxs_ref, dd_ref, dg_ref):
        z, xs = z_ref[...], xs_ref[...]
        sg = _sigmoid(z)
        gate = z * sg
        y = y_ref[...] + d_ref[...] * xs
        dy2, dg = _seg_rms_bwd(y * gate, g_ref[...], do_ref[...], 256)
        dy = dy2 * gate
        dy_ref[...] = dy.astype(BF)
        dz_ref[...] = (dy2 * y * (sg * (1.0 + z * (1.0 - sg)))).astype(BF)
        dxs_ref[...] = dy * d_ref[...]
        _accum(dd_ref, jnp.sum(dy * xs, axis=0, keepdims=True), i)
        _accum(dg_ref, dg, i)

    return _rowwise(body, name="ssd_post_bwd", nrows=S, tile=ROW_TILE,
                    row_ins=[(y_ssd, D_SSM, 0), (xbc, D_SSM, 0), (proj, D_SSM, 3), (dmix, D_SSM, 1)],
                    full_ins=[dskip_b, g_ssm], row_outs=[(D_SSM, BF), (D_SSM, BF), (D_SSM, F32)],
                    acc_outs=[((1, D_SSM), F32), ((1, D_SSM), F32)])


def _ssd_bwd(xbc, dy, h_all, a_b, dt_b, at_r, dt_r):
    def kern(c_ref, b_ref, x_ref, dy_ref, hall_ref, ab_ref, dtb_ref, at_ref, dt_ref,
             dc_ref, daq_ref, dx_ref, db_ref, dak_ref, ddt_ref, ddtb_ref):
        half0 = lax.broadcasted_iota(jnp.int32, (BLK, LANES), 1) < HEAD
        hms = (half0, jnp.logical_not(half0))
        causal = lax.broadcasted_iota(jnp.int32, (BLK, BLK), 1) <= lax.broadcasted_iota(jnp.int32, (BLK, BLK), 0)
        row = lax.broadcasted_iota(jnp.int32, (BLK, LANES), 0)

        def step(t, carry_in):
            dh_next, carry = carry_in
            for cc in reversed(range(SSD_CB)):
                chunk = (NB // SSD_CB - 1 - t) * SSD_CB + cc
                rows = pl.ds(pl.multiple_of(chunk * BLK, BLK), BLK)
                ci, bi, x = c_ref[rows, :].astype(BF), b_ref[rows, :].astype(BF), x_ref[rows, :]
                dyv = dy_ref[rows, :].astype(F32)
                ab, dtb = ab_ref[rows, :], dtb_ref[rows, :]
                before = ab_ref[pl.ds(pl.multiple_of(jnp.maximum(chunk * BLK - 8, 0), 8), 8), :][7:8, :]
                a_prev = jnp.where(chunk == 0, 0.0, before)
                a_end = ab[BLK - 1:BLK, :]
                alpha = jnp.exp(ab - a_prev)
                e_end = jnp.exp(a_end - ab)
                beta = e_end * dtb
                t_all = alpha[BLK - 1:BLK, :]
                hc = hall_ref[rows, :]
                hcb, dhb = hc.astype(BF), dh_next.astype(BF)

                cb = _dot(ci, bi, NT)
                at, dtj = at_ref[chunk], dt_ref[chunk]
                dcb = jnp.zeros((BLK, BLK), F32)
                dx = jnp.zeros((BLK, LANES), F32)
                rsum = []
                for hd in range(2):
                    dym = jnp.where(hms[hd], dyv, 0.0).astype(BF)
                    lm = _ssd_decay(ab[:, HEAD * hd:HEAD * hd + 1], at[hd:hd + 1, :], causal)
                    dth = dtj[hd:hd + 1, :]
                    dw = _dot(dym, jnp.where(hms[hd], x, 0.0).astype(BF), NT)
                    g = dw * cb * lm
                    dx = dx + _dot((cb * lm * dth).astype(BF), dym, TN)
                    gcol = jnp.sum(g, axis=0, keepdims=True)
                    ddt_ref[chunk, hd:hd + 1, :] = gcol
                    dak_ref[chunk, hd:hd + 1, :] = gcol * dth
                    rsum.append(jnp.sum(g * dth, axis=1, keepdims=True))
                    dcb = dcb + dw * lm * dth
                dcb = dcb.astype(BF)

                g1 = (alpha * dyv).astype(BF)
                dalpha = dyv * _dot(ci, hcb, NN)
                g2 = _dot(bi, dhb, NN)
                dbeta = x * g2
                bx = (beta * x).astype(BF)
                dc_ref[rows, :] = _dot(dcb, bi, NN) + _dot(g1, hcb, NT)
                db_ref[rows, :] = _dot(dcb, ci, TN) + _dot(bx, dhb, NT)
                dx_ref[rows, :] = dx + beta * g2
                ddtb_ref[rows, :] = _seg_sum(e_end * dbeta, HEAD)
                ada, bdb = alpha * dalpha, beta * dbeta
                t_dt = t_all * jnp.sum(dh_next * hc, axis=0, keepdims=True)
                end_term = _seg_sum(jnp.broadcast_to(jnp.sum(bdb, axis=0, keepdims=True) + t_dt, (8, LANES)), HEAD)[0:1]
                prev_term = _seg_sum(jnp.broadcast_to(jnp.sum(ada, axis=0, keepdims=True) + t_dt, (8, LANES)), HEAD)[0:1]
                daq = jnp.where(half0, rsum[0], rsum[1]) + _seg_sum(ada - bdb, HEAD)
                daq_ref[rows, :] = daq + jnp.where(row == BLK - 1, end_term - carry, 0.0)
                carry = prev_term
                dh_next = t_all * dh_next + _dot(ci, g1, TN)
            return dh_next, carry

        lax.fori_loop(0, NB // SSD_CB, step, (jnp.zeros((BLK, LANES), F32), jnp.zeros((1, LANES), F32)))

    npair = D_SSM // LANES
    rowvec = pl.BlockSpec((None, NB, 2, BLK), lambda p: (p, 0, 0, 0))
    seq = pl.BlockSpec((S, LANES), lambda p: (0, p))
    return pl.pallas_call(
        kern, name="ssd_bwd", grid=(npair,),
        in_specs=[pl.BlockSpec((S, LANES), lambda p: (0, 12 + p // 2)),
                  pl.BlockSpec((S, LANES), lambda p: (0, 8 + p // 2)),
                  seq, seq, seq, seq, seq, rowvec, rowvec],
        out_specs=[seq, seq, seq, seq, rowvec, rowvec, seq],
        out_shape=[jax.ShapeDtypeStruct((S, D_SSM), F32)] * 4
        + [jax.ShapeDtypeStruct((npair, NB, 2, BLK), F32)] * 2 + [jax.ShapeDtypeStruct((S, D_SSM), F32)],
        compiler_params=_params("parallel"))(xbc, xbc, xbc, dy, h_all, a_b, dt_b, at_r, dt_r)


def _ssd_prep_bwd(daq, dak, ddt_row, ddt_lane, dt, proj, dt_bias128, a_log128):
    def body(i, n, daq_ref, dak_ref, dd_ref, dd2_ref, dt_ref, raw_ref, b_ref, al_ref, draw_ref, dal_ref, db_ref,
             carry):
        @pl.when(i == 0)
        def _():
            carry[...] = jnp.zeros_like(carry)

        a = -jnp.exp(al_ref[...])
        tri = (lax.broadcasted_iota(jnp.int32, (BLK, BLK), 0) <= lax.broadcasted_iota(jnp.int32, (BLK, BLK), 1))
        rev = _dot3(tri.astype(BF), _collect_heads(daq_ref[...]) - dak_ref[...]) + carry[0:1, :]
        carry[...] = jnp.broadcast_to(rev[0:1, :], carry.shape)
        dtv = dt_ref[...]
        draw = (dd_ref[...] + _collect_heads(dd2_ref[...]) + a * rev) * _sigmoid(raw_ref[...] + b_ref[...])
        draw_ref[...] = draw.astype(BF)
        _accum(dal_ref, jnp.sum(dtv * rev, axis=0, keepdims=True) * a, i)
        _accum(db_ref, jnp.sum(draw, axis=0, keepdims=True), i)

    return _rowwise(body, name="ssd_prep_bwd", nrows=S, tile=BLK, reverse=True,
                    row_ins=[(daq, D_SSM, 0), (dak, LANES, 0), (ddt_row, LANES, 0), (ddt_lane, D_SSM, 0), (dt, LANES, 0),
                             (proj, LANES, DT_COLBLK)],
                    full_ins=[dt_bias128, a_log128], row_outs=[(LANES, BF)],
                    acc_outs=[((1, LANES), F32), ((1, LANES), F32)], scratch=[pltpu.VMEM((8, LANES), F32)])


def _conv_bwd(proj, conv_w, conv_b, d1, d2, map1, map2, col0, ntiles, name):
    base = (4 * D_ATTN + col0) // LANES

    def kern(x_ref, w_ref, b_ref, d1_ref, d2_ref, dx_ref, dw_ref, db_ref):
        x = x_ref[...]
        c, shifted = _conv_taps(x, w_ref)
        c = c + b_ref[...]
        sg = _sigmoid(c)
        dc = (d1_ref[...] + d2_ref[...]) * (sg * (1.0 + c * (1.0 - sg)))
        rows = lax.broadcasted_iota(jnp.int32, x.shape, 0)
        dx = jnp.zeros_like(x)
        for w in range(CONV_W):
            k = CONV_W - 1 - w
            up = dc if k == 0 else jnp.where(rows < S - k, pltpu.roll(dc, S - k, 0), 0.0)
            dx = dx + up * w_ref[w:w + 1, :]
            dw_ref[w:w + 1, :] = jnp.sum(dc * shifted[w], axis=0, keepdims=True)
        dx_ref[...] = dx.astype(BF)
        db_ref[...] = jnp.sum(dc, axis=0, keepdims=True)

    c0 = col0 // LANES
    return pl.pallas_call(
        kern, name=name, grid=(ntiles,),
        in_specs=[pl.BlockSpec((S, LANES), lambda c: (0, base + c)),
                  pl.BlockSpec((CONV_W, LANES), lambda c: (0, c0 + c)),
                  pl.BlockSpec((1, LANES), lambda c: (0, c0 + c)),
                  pl.BlockSpec((S, LANES), lambda c: (0, map1(c))),
                  pl.BlockSpec((S, LANES), lambda c: (0, map2(c)))],
        out_specs=[pl.BlockSpec((S, LANES), lambda c: (0, c)), pl.BlockSpec((CONV_W, LANES), lambda c: (0, c)),
                   pl.BlockSpec((1, LANES), lambda c: (0, c))],
        out_shape=[jax.ShapeDtypeStruct((S, ntiles * LANES), BF), jax.ShapeDtypeStruct((CONV_W, ntiles * LANES), F32),
                   jax.ShapeDtypeStruct((1, ntiles * LANES), F32)],
        compiler_params=_params("parallel"))(proj, conv_w, conv_b, d1, d2)


def _adamw(w, m, v, parts, name):
    r, c = w.shape
    n_parts = parts.shape[0]
    tile = r if r <= 512 else (128 if c > 1024 else 256)
    assert r % tile == 0
    c1 = 1.0 - ADAM_B1 ** ADAM_STEP
    c2 = 1.0 - ADAM_B2 ** ADAM_STEP

    def kern(w_ref, m_ref, v_ref, p_ref, g_ref, d_ref, mo_ref, vo_ref):
        g = p_ref[0].astype(F32)
        for k in range(1, n_parts):
            g = g + p_ref[k].astype(F32)
        mn = ADAM_B1 * m_ref[...] + (1.0 - ADAM_B1) * g
        vn = ADAM_B2 * v_ref[...] + (1.0 - ADAM_B2) * (g * g)
        g_ref[...] = g
        mo_ref[...] = mn
        vo_ref[...] = vn
        d_ref[...] = -ADAM_LR * ((mn / c1) / (jnp.sqrt(vn / c2) + ADAM_EPS) + ADAM_WD * w_ref[...])

    blk = pl.BlockSpec((tile, c), lambda i: (i, 0))
    return pl.pallas_call(
        kern, name=name, grid=(r // tile,),
        in_specs=[blk, blk, blk, pl.BlockSpec((n_parts, tile, c), lambda i: (0, i, 0))],
        out_specs=[blk] * 4, out_shape=[jax.ShapeDtypeStruct((r, c), F32)] * 4,
        compiler_params=_params("parallel"))(w, m, v, parts)


MESH = pl.DeviceIdType.MESH
ANY = pl.BlockSpec(memory_space=pl.ANY)


def _put_own(gathered, shard):
    me = 4 * lax.axis_index("x") + 2 * lax.axis_index("y") + lax.axis_index("c")
    return lax.dynamic_update_slice(gathered, shard[None], (me,) + (0,) * shard.ndim)


def _all_gather(arrs, name, after=None):
    na = len(arrs)
    n_after = 0 if after is None else 1

    def kern(*refs):
        ins, outs = refs[:na], refs[na + n_after:2 * na + n_after]
        send_sems, recv_sems = refs[2 * na + n_after:]
        x, y, c = lax.axis_index("x"), lax.axis_index("y"), lax.axis_index("c")
        me, sibling = (x, y, c), (x, y, 1 - c)
        a_chip = (jnp.where(c == 0, 1 - x, x), jnp.where(c == 0, y, 1 - y))
        b_chip = (jnp.where(c == 0, x, 1 - x), jnp.where(c == 0, 1 - y, y))
        chips = [a_chip, b_chip, (1 - x, 1 - y)]

        def copy(a, k, block, to, src=None):
            px, py, pc = block
            dst = outs[a].at[4 * px + 2 * py + pc]
            return pltpu.make_async_remote_copy(
                src_ref=dst if src is None else src, dst_ref=dst, send_sem=send_sems.at[a, k],
                recv_sem=recv_sems.at[a, k], device_id=to, device_id_type=MESH)

        sends = []
        for a in range(na):
            sends.append(copy(a, 0, me, sibling, src=ins[a]))
            sends += [copy(a, 1 + j, me, (*chips[j], c), src=ins[a]) for j in range(2)]
        for cp in sends:
            cp.start()
        for a in range(na):
            for j, chip in enumerate(chips):
                copy(a, 1 + j, (*chip, c), me).wait_recv()
                later = [copy(a, 4 + j, (*chip, c), sibling)]
                if j == 0:
                    later.append(copy(a, 3, (*a_chip, c), (*b_chip, c)))
                for cp in later:
                    cp.start()
                sends += later
        for a in range(na):
            copy(a, 0, sibling, me).wait_recv()
            for j, chip in enumerate(chips):
                copy(a, 4 + j, (*chip, 1 - c), me).wait_recv()
        for cp in sends:
            cp.wait_send()

    outs = pl.pallas_call(
        kern, name=name, in_specs=[ANY] * (na + n_after), out_specs=[ANY] * na,
        out_shape=[jax.ShapeDtypeStruct((N_DEV,) + a.shape, a.dtype) for a in arrs],
        scratch_shapes=[pltpu.SemaphoreType.DMA((na, 7)), pltpu.SemaphoreType.DMA((na, 7))],
    )(*arrs, *([] if after is None else [after]))
    return [_put_own(o, a) for o, a in zip(outs, arrs)]


HBM = pl.BlockSpec(memory_space=pltpu.HBM)
SEM = pl.BlockSpec(memory_space=pltpu.SEMAPHORE)
EFFECT = pltpu.SideEffectType.DATAFLOW_SIDE_EFFECTING
N_CHIP = 4
N_COPIES = {"gather": 3, "scatter": 3, "forward": 4, "pair": 4, "direct": 2, "relay": 4, "diag": 1}


def _in_hbm(a):
    return pltpu.with_memory_space_constraint(a, pltpu.HBM)


def _chip_copies(kind, src_refs, land_refs, send_sems, recv_sems):
    x, y, c = lax.axis_index("x"), lax.axis_index("y"), lax.axis_index("c")
    chips = [(1 - x, y), (x, 1 - y), (1 - x, 1 - y)]
    a_chip = (jnp.where(c == 0, 1 - x, x), jnp.where(c == 0, y, 1 - y))
    b_chip = (jnp.where(c == 0, x, 1 - x), jnp.where(c == 0, 1 - y, y))
    n = N_COPIES[kind]
    cps = []

    def add(a, j, src, dst, to):
        cps.append(pltpu.make_async_remote_copy(
            src_ref=src, dst_ref=dst, send_sem=send_sems.at[n * a + j], recv_sem=recv_sems.at[n * a + j],
            device_id=to, device_id_type=MESH))

    def slot(a, chip, core):
        return land_refs[a].at[4 * chip[0] + 2 * chip[1] + core]

    for a in range(len(src_refs)):
        if kind == "direct":
            add(a, 0, src_refs[a], slot(a, (x, y), c), (*a_chip, c))
            add(a, 1, src_refs[a], slot(a, (x, y), c), (*b_chip, c))
        elif kind == "relay":
            add(a, 0, slot(a, a_chip, c), slot(a, a_chip, c), (*b_chip, c))
            add(a, 1, src_refs[a], slot(a, (x, y), c), (x, y, 1 - c))
            add(a, 2, slot(a, a_chip, c), slot(a, a_chip, c), (x, y, 1 - c))
            add(a, 3, slot(a, b_chip, c), slot(a, b_chip, c), (x, y, 1 - c))
        elif kind == "diag":
            add(a, 0, slot(a, (1 - x, 1 - y), c), slot(a, (1 - x, 1 - y), c), (x, y, 1 - c))
        elif kind == "gather":
            for j, (px, py) in enumerate(chips):
                add(a, j, src_refs[a], land_refs[a].at[4 * x + 2 * y + c], (px, py, c))
        elif kind == "scatter":
            for j, (px, py) in enumerate(chips):
                add(a, j, src_refs[a].at[2 * px + py], land_refs[a].at[2 * x + y], (px, py, c))
        elif kind == "forward":
            add(a, 0, src_refs[a], land_refs[a].at[4 * x + 2 * y + c], (x, y, 1 - c))
            for j, (px, py) in enumerate(chips):
                slot = land_refs[a].at[4 * px + 2 * py + c]
                add(a, 1 + j, slot, slot, (x, y, 1 - c))
        else:
            for q in range(N_CHIP):
                add(a, q, src_refs[a].at[2 * q + 1 - c], land_refs[a].at[q], (x, y, 1 - c))
    return cps


def _exchange_start(kind, srcs, lands, after, name):
    na = len(srcs)
    n_after = 0 if after is None else 1

    def kern(*refs):
        src_refs, land_refs = refs[:na], refs[na:2 * na]
        send_sems, recv_sems = refs[2 * na + n_after], refs[2 * na + n_after + 1]
        token = refs[-1]
        for cp in _chip_copies(kind, src_refs, land_refs, send_sems, recv_sems):
            cp.start()
        token[...] = jnp.zeros_like(token)

    bufs = list(srcs) + list(lands)
    res = pl.pallas_call(
        kern, name=name,
        out_shape=(pltpu.SemaphoreType.DMA((N_COPIES[kind] * na,)), pltpu.SemaphoreType.DMA((N_COPIES[kind] * na,)))
        + tuple(pltpu.HBM(b.shape, b.dtype) for b in bufs) + (jax.ShapeDtypeStruct((8, LANES), F32),),
        in_specs=[HBM] * (2 * na) + [ANY] * n_after,
        out_specs=(SEM, SEM) + (HBM,) * (2 * na) + (pl.BlockSpec(memory_space=pltpu.VMEM),),
        input_output_aliases={i: 2 + i for i in range(2 * na)},
        compiler_params=pltpu.CompilerParams(has_side_effects=EFFECT),
    )(*[_in_hbm(b) for b in bufs], *([] if after is None else [after]))
    return (res[0], res[1]), list(res[2:2 + na]), list(res[2 + na:2 + 2 * na]), res[-1]


def _exchange_wait(kind, sems, srcs, lands, after, name):
    na = len(srcs)
    afters = list(after) if isinstance(after, (list, tuple)) else [after]

    def kern(*refs):
        src_refs, land_refs = refs[:na], refs[na:2 * na]
        send_sems, recv_sems = refs[2 * na], refs[2 * na + 1]
        for cp in _chip_copies(kind, src_refs, land_refs, send_sems, recv_sems):
            cp.wait_send()
            cp.wait_recv()

    bufs = list(srcs) + list(lands)
    res = pl.pallas_call(
        kern, name=name, out_shape=tuple(pltpu.HBM(b.shape, b.dtype) for b in bufs),
        in_specs=[HBM] * (2 * na) + [SEM, SEM] + [ANY] * len(afters), out_specs=(HBM,) * (2 * na),
        input_output_aliases={i: i for i in range(2 * na)},
        compiler_params=pltpu.CompilerParams(has_side_effects=EFFECT),
    )(*bufs, sems[0], sems[1], *afters)
    return list(res[:na]), list(res[na:])


def _pair_exchange(parts, name):
    na = len(parts)

    def kern(*refs):
        ins, got = refs[:na], refs[na:2 * na]
        send_sems, recv_sems = refs[2 * na:]
        x, y, c = lax.axis_index("x"), lax.axis_index("y"), lax.axis_index("c")
        sends = []
        for a in range(na):
            for q in range(N_CHIP):
                sends.append(pltpu.make_async_remote_copy(
                    src_ref=ins[a].at[2 * q + 1 - c], dst_ref=got[a].at[q], send_sem=send_sems.at[a, q],
                    recv_sem=recv_sems.at[a, q], device_id=(x, y, 1 - c), device_id_type=MESH))
        for cp in sends:
            cp.start()
        for cp in sends:
            cp.wait()

    return pl.pallas_call(
        kern, name=name, in_specs=[ANY] * na, out_specs=[ANY] * na,
        out_shape=[jax.ShapeDtypeStruct((N_CHIP,) + p.shape[1:], p.dtype) for p in parts],
        scratch_shapes=[pltpu.SemaphoreType.DMA((na, N_CHIP)), pltpu.SemaphoreType.DMA((na, N_CHIP))])(*parts)


def _pair_sum(parts, got, name):
    _, r, cdim = parts.shape
    tile = min(r, ROW_TILE)
    x, y, c = lax.axis_index("x"), lax.axis_index("y"), lax.axis_index("c")
    where = jnp.stack([c, 2 * x + y]).astype(jnp.int32)

    def kern(w_ref, a_ref, b_ref, q_ref, own_ref):
        s = (a_ref[...].astype(F32) + b_ref[...].astype(F32)).astype(BF)
        q_ref[...] = s

        @pl.when(pl.program_id(1) == w_ref[1])
        def _():
            own_ref[...] = s

    blk = pl.BlockSpec((None, tile, cdim), lambda i, q, w_ref: (q, i, 0))
    sums, own = pl.pallas_call(
        kern, name=name,
        out_shape=[jax.ShapeDtypeStruct((N_CHIP, r, cdim), BF), jax.ShapeDtypeStruct((r, cdim), BF)],
        grid_spec=pltpu.PrefetchScalarGridSpec(
            num_scalar_prefetch=1, grid=(r // tile, N_CHIP),
            in_specs=[pl.BlockSpec((None, tile, cdim), lambda i, q, w_ref: (2 * q + w_ref[0], i, 0)), blk],
            out_specs=[blk, pl.BlockSpec((tile, cdim), lambda i, q, w_ref: (i, 0))]),
        compiler_params=_params("parallel", "arbitrary"))(where, parts, got)
    land = lax.dynamic_update_slice(lax.empty((N_CHIP, r, cdim), BF), own[None], (2 * x + y, 0, 0))
    return sums, land


W_IN_COLS = D_IN // N_DEV


def _w_in_from_blocks(w8):
    def kern(x_ref, o_ref):
        for j in range(N_DEV):
            o_ref[:, W_IN_COLS * j:W_IN_COLS * (j + 1)] = x_ref[j]
        o_ref[:, D_IN:] = jnp.zeros((ROW_TILE, D_INP - D_IN), o_ref.dtype)

    return pl.pallas_call(
        kern, name="w_in_from_blocks", grid=(D // ROW_TILE,),
        in_specs=[pl.BlockSpec((N_DEV, ROW_TILE, W_IN_COLS), lambda i: (0, i, 0))],
        out_specs=pl.BlockSpec((ROW_TILE, D_INP), lambda i: (i, 0)),
        out_shape=jax.ShapeDtypeStruct((D, D_INP), w8.dtype), compiler_params=_params("parallel"))(w8)


def _w_in_to_blocks(g):
    def kern(x_ref, o_ref):
        for j in range(N_DEV):
            o_ref[j] = x_ref[:, W_IN_COLS * j:W_IN_COLS * (j + 1)]

    return pl.pallas_call(
        kern, name="w_in_to_blocks", grid=(D // ROW_TILE,),
        in_specs=[pl.BlockSpec((ROW_TILE, D_INP), lambda i: (i, 0))],
        out_specs=pl.BlockSpec((N_DEV, ROW_TILE, W_IN_COLS), lambda i: (0, i, 0)),
        out_shape=jax.ShapeDtypeStruct((N_DEV, D, W_IN_COLS), g.dtype), compiler_params=_params("parallel"))(g)


def _pad_lanes(v, width=LANES):
    return jnp.pad(v, ((0, 0), (0, width - v.shape[1])))


def _row_layout(t16):
    return t16.T.reshape(N_HEADS // 2, 2, NB, BLK).transpose(0, 2, 1, 3)


def _row_layout_inv(r):
    return r.transpose(0, 2, 1, 3).reshape(N_HEADS, S).T


SMALL = (("g_mix", D), ("g_q", HEAD), ("g_k", HEAD), ("g_attn_out", D_ATTN), ("conv_b", D_CONV),
         ("dt_bias", 16), ("a_log", 16), ("d_skip", 16), ("g_ssm_out", D_SSM), ("g_cross", D),
         ("g_mem", D), ("g_cq", CROSS_HEAD), ("g_ck", CROSS_HEAD), ("g_mlp", D))
SMALL_ROWS = -(-sum(-(-w // LANES) for _, w in SMALL) // 8) * 8


def _pack_small(vals):
    rows = [_pad_lanes(vals[n], -(-w // LANES) * LANES).reshape(-1, LANES) for n, w in SMALL]
    packed = jnp.concatenate(rows, axis=0)
    return jnp.pad(packed, ((0, SMALL_ROWS - packed.shape[0]), (0, 0)))


def _unpack_small(packed):
    out, r = {}, 0
    for n, w in SMALL:
        nr = -(-w // LANES)
        out[n] = packed[r:r + nr].reshape(1, nr * LANES)[:, :w]
        r += nr
    return out


BIG = ("w_in", "w_out", "w_cq", "w_ckv", "w_co", "w_up", "w_down")
WEIGHTS = ("g_mix", "w_in", "g_q", "g_k", "g_attn_out", "conv_w", "conv_b", "dt_bias", "a_log", "d_skip",
           "g_ssm_out", "w_out", "g_cross", "g_mem", "w_cq", "w_ckv", "g_cq", "g_ck", "w_co", "g_mlp", "w_up", "w_down")


REST = ("w_out", "w_cq", "w_ckv", "w_co", "w_up", "w_down")


class _Overlap:
    def __init__(self, first):
        self.first_names = list(first)
        srcs = [first[n] for n in self.first_names]
        lands = [_put_own(lax.empty((N_DEV,) + s.shape, s.dtype), s) for s in srcs]
        self.direct = _exchange_start("direct", srcs, lands, None, "ag_first_start")
        self.token = self.direct[3]
        self.shards, self.behind, self.late = {}, [], []
        self.gather, self.forward = {}, {}
        self.names = {"a": REST[:4], "b": REST[4:5], "c": REST[5:]}
        self.groups = []
        self.pairs = {}

    def first(self, after):
        sems, srcs, lands, _ = self.direct
        srcs, lands = _exchange_wait("direct", sems, srcs, lands,
                                     [after] + list(self.shards.values()) + list(self.behind), "ag_first_wait")
        sems, srcs, lands, token = _exchange_start("relay", srcs, lands, None, "ag_relay_start")
        self.late = [_tie(t, token) for t in self.late]
        srcs, lands = _exchange_wait("relay", sems, srcs, lands, [token] + self.late, "ag_relay_wait")
        sems, srcs, lands, token = _exchange_start("diag", srcs, lands, None, "ag_diag_start")
        for tag, names in self.names.items():
            rest_lands = [_put_own(lax.empty((N_DEV,) + self.shards[n].shape, BF), self.shards[n]) for n in names]
            self.gather[tag] = _exchange_start("gather", [self.shards[n] for n in names], rest_lands, token,
                                               "ag_start_" + tag)
            token = self.gather[tag][3]
        _, lands = _exchange_wait("diag", sems, srcs, lands, token, "ag_diag_wait")
        return dict(zip(self.first_names, lands))

    def rest_mid(self, tag, after):
        sems, srcs, lands, _ = self.gather[tag]
        srcs, lands = _exchange_wait("gather", sems, srcs, lands, after, "ag_wait_" + tag)
        self.forward[tag] = _exchange_start("forward", srcs, lands, None, "ag_fwd_start_" + tag)
        return self.forward[tag][3]

    def rest(self, tag, after):
        sems, srcs, lands, _ = self.forward[tag]
        _, lands = _exchange_wait("forward", sems, srcs, lands, after, "ag_fwd_wait_" + tag)
        wf = dict(zip(self.names[tag], lands))
        for n in wf:
            if n in ("w_out", "w_cq", "w_ckv", "w_down"):
                wf[n] = wf[n].reshape(-1, wf[n].shape[-1])
        return wf

    def pair_start(self, name, grad):
        got = lax.empty((N_CHIP,) + grad.shape[1:], grad.dtype)
        self.pairs[name] = _exchange_start("pair", [grad], [got], None, "rs_pair_start_" + name)
        return self.pairs[name][3]

    def emit(self, grads, after):
        names, tag = list(grads), "_%d" % len(self.groups)
        grads, got = dict(grads), {}
        for n in names:
            if n in self.pairs:
                sems, srcs, lands, _ = self.pairs[n]
                srcs, lands = _exchange_wait("pair", sems, srcs, lands, after, "rs_pair_wait_" + n)
                grads[n], got[n] = srcs[0], lands[0]
        sync = [n for n in names if n not in got]
        if sync:
            got.update(zip(sync, _pair_exchange([grads[n] for n in sync], "rs_pair" + tag)))
        sums = [_pair_sum(grads[n], got[n], "rs_sum_" + n) for n in names]
        sems, srcs, lands, token = _exchange_start("scatter", [q for q, _ in sums], [l for _, l in sums], None,
                                                   "rs_chip_start" + tag)
        self.groups.append((names, sems, srcs, lands))
        return token

    def finish(self, gi, after):
        names, sems, srcs, lands = self.groups[gi]
        _, lands = _exchange_wait("scatter", sems, srcs, lands, after, "rs_chip_wait_%d" % gi)
        return dict(zip(names, lands))


def _tie(v, token):
    return v if token is None else v + token[0:1, 0:1]


def _local_step(x, mem, pos_col, target, p, wf, hooks=None):
    p = dict(p)
    inv = np.zeros((1, LANES), np.float32)
    freq = (ROPE_THETA ** (-2.0 * np.arange(ROT // 2, dtype=np.float32) / ROT)).astype(np.float32)
    for l in range(LANES):
        if l % HEAD < ROT:
            inv[0, l] = freq[(l % HEAD) % (ROT // 2)]
    inv_tab = jnp.asarray(inv)
    gq128, gk128 = jnp.tile(p["g_q"], (1, 2)), jnp.tile(p["g_k"], (1, 2))
    dtb128, alog128 = _pad_lanes(p["dt_bias"]), _pad_lanes(p["a_log"])
    dskip_b = jnp.repeat(p["d_skip"], HEAD, axis=1)

    h = _rms_fwd(x, p["g_mix"], "rms_mix")
    if hooks is not None:
        got = hooks.first(h)
        wf = {**wf, "w_in": _w_in_from_blocks(got["w_in"]),
              "conv_w": got["conv_w"].transpose(1, 0, 2).reshape(CONV_W, D_CONV)}
    conv_w, conv_b = wf["conv_w"], p["conv_b"]
    proj = _matmul(h, wf["w_in"], mode="nn", name="mm_in", tm=1024, tn=1280, tk=D)
    qr, kr = _qk_prep(proj, pos_col, gq128, gk128, inv_tab)
    attn, lse = _attn_fwd(qr, kr, proj)
    attn_n = _attn_norm(attn, p["g_attn_out"])

    xbc = _conv_fwd(proj, conv_w, conv_b)
    token = None if hooks is None else hooks.rest_mid("a", xbc)
    dt, a_cs, dt_b, a_b = _ssd_prep(proj, _tie(dtb128, token), alog128)
    at_r, dt_r = _row_layout(a_cs[:, :N_HEADS]), _row_layout(dt[:, :N_HEADS])
    y_ssd, h_all = _ssd_fwd(xbc, a_b, dt_b, at_r, dt_r)
    ssm_n = _ssd_post(y_ssd, xbc, proj, dskip_b, p["g_ssm_out"])

    if hooks is not None:
        wf = {**wf, **hooks.rest("a", ssm_n)}
    mix = jnp.concatenate([attn_n, ssm_n], axis=1)
    x1 = _matmul(mix, wf["w_out"], mode="nn", name="mm_out", tm=1024, tn=1024, tk=D,
                 extras=(x,), epilogue=lambda acc, r: (acc + r,))
    hc = _rms_fwd(x1, _tie(p["g_cross"], None if hooks is None else hooks.rest_mid("b", x1)), "rms_cross")
    memh = _rms_fwd(mem, p["g_mem"], "rms_mem")
    qc = _matmul(hc, wf["w_cq"], mode="nn", name="mm_cq", tm=1024, tn=512, tk=D)
    kv = _matmul(memh, wf["w_ckv"], mode="nn", name="mm_ckv", tm=N_MEM, tn=1024, tk=D)
    oc = _cross_fwd(qc, kv, p["g_cq"], p["g_ck"])
    x2 = _matmul(oc, wf["w_co"], mode="nn", name="mm_co", tm=1024, tn=256, tk=512, b_cb=256,
                 extras=(x1,), epilogue=lambda acc, r: (acc + r,))
    hm = _rms_fwd(x2, p["g_mlp"], "rms_mlp")
    if hooks is not None:
        wf = {**wf, **hooks.rest("b", hm)}

    def up_epi(acc):
        ru = jnp.maximum(acc, 0.0)
        return ru * ru, 2.0 * ru

    act, relu2 = _matmul(hm, wf["w_up"], mode="nn", name="mm_up", tm=1024, tn=1024, tk=D, b_cb=1024,
                         out_dtypes=(BF, BF), epilogue=up_epi)
    if hooks is not None:
        hooks.rest_mid("c", act)
        wf = {**wf, **hooks.rest("c", relu2)}

    def loss_epi(acc, r, t):
        d = (acc + r - t) * (1.0 / D)
        return d, d

    dy, dyb = _matmul(act, wf["w_down"], mode="nn", name="mm_down", tm=512, tn=1024, tk=2048, extras=(x2, target),
                      out_dtypes=(F32, BF), epilogue=loss_epi)

    gb, gs = {}, {}
    gb["w_down"] = _matmul(act, dyb, mode="tn", name="mm_down_dw", tm=1024, tn=1024, tk=S,
                           out_dtypes=(BF,)).reshape(N_DEV, D_FF // N_DEV, D)
    token = None if hooks is None else hooks.pair_start("w_down", gb["w_down"])
    du = _matmul(dyb, wf["w_down"], mode="nt", name="mm_down_dx", tm=1024, tn=1024, tk=D, extras=(relu2,),
                 out_dtypes=(BF,), epilogue=lambda acc, r: (acc * r.astype(F32),), after=token)
    gb["w_up"] = _matmul(hm, du, mode="tn", name="mm_up_dw", tm=1024, tn=1024, tk=S, out_dtypes=(BF,), out_cb=1024)
    token = None if hooks is None else hooks.pair_start("w_up", gb["w_up"])
    dhm = _matmul(du, wf["w_up"], mode="nt", name="mm_up_dx", tm=1024, tn=1024, tk=2048, b_cb=1024, after=token)
    if hooks is not None:
        p["g_mlp"] = _tie(p["g_mlp"], hooks.emit({n: gb[n] for n in ("w_down", "w_up")}, dhm))
    dx2, dx2b, gs["g_mlp"], sumsq_dy = _rms_bwd_call(x2, p["g_mlp"], dhm, dy, "rms_mlp_bwd", sumsq_res=True)
    loss_part = sumsq_dy[0, 0] * (0.5 * D)

    doc = _matmul(dx2b, wf["w_co"], mode="nt", name="mm_co_dx", tm=1024, tn=512, tk=256, b_cb=256)
    gb["w_co"] = _matmul(oc, dx2b, mode="tn", name="mm_co_dw", tm=512, tn=256, tk=S, out_dtypes=(BF,), out_cb=256)
    dqc, dkn, dvc, gs["g_cq"] = _cross_bwd(qc, doc, kv, p["g_cq"], p["g_ck"])
    dkv, gs["g_ck"] = _cross_kv_bwd(kv, dkn, dvc, p["g_ck"])
    gb["w_cq"] = _matmul(hc, dqc, mode="tn", name="mm_cq_dw", tm=1024, tn=512, tk=S,
                         out_dtypes=(BF,)).reshape(N_DEV, D // N_DEV, D_CROSS)
    dhc = _matmul(dqc, wf["w_cq"], mode="nt", name="mm_cq_dx", tm=1024, tn=1024, tk=512)
    gb["w_ckv"] = _matmul(memh, dkv, mode="tn", name="mm_ckv_dw", tm=1024, tn=1024, tk=N_MEM,
                          out_dtypes=(BF,)).reshape(N_DEV, D // N_DEV, 2 * D_CROSS)
    dmemh = _matmul(dkv, wf["w_ckv"], mode="nt", name="mm_ckv_dx", tm=N_MEM, tn=1024, tk=1024)
    (gs["g_mem"],) = _rms_bwd_call(mem, p["g_mem"], dmemh, None, "rms_mem_bwd")
    dx1, dx1b, gs["g_cross"] = _rms_bwd_call(x1, p["g_cross"], dhc, dx2, "rms_cross_bwd")

    dmix = _matmul(dx1b, wf["w_out"], mode="nt", name="mm_out_dx", tm=1024, tn=1024, tk=D)
    gb["w_out"] = _matmul(mix, dx1b, mode="tn", name="mm_out_dw", tm=1024, tn=1024, tk=S,
                          out_dtypes=(BF,)).reshape(N_DEV, D // N_DEV, D)
    if hooks is not None:
        p["g_attn_out"] = _tie(p["g_attn_out"],
                               hooks.emit({n: gb[n] for n in ("w_co", "w_cq", "w_ckv", "w_out")}, dmix))

    dattn, delta, gs["g_attn_out"] = _attn_merge_bwd(dmix, attn, p["g_attn_out"])
    dq_rot, dk_rot, dv_attn = _attn_bwd(qr, kr, proj, dattn, lse, delta)
    dq_pre, dk_pre, dv_pre, dgq, dgk = _qk_bwd(dq_rot, dk_rot, dv_attn, proj, pos_col, gq128, gk128, inv_tab)
    gs["g_q"], gs["g_k"] = dgq[:, :HEAD], dgk[:, :HEAD]

    dy_ssd, dz, dxs_skip, dd_lane, gs["g_ssm_out"] = _ssd_post_bwd(y_ssd, xbc, proj, dmix, dskip_b, p["g_ssm_out"])
    gs["d_skip"] = dd_lane.reshape(N_HEADS, HEAD).sum(axis=1).reshape(1, N_HEADS)
    dc_pair, daq_b, dx_ssd, db_pair, dak_r, ddt_r, ddt_b = _ssd_bwd(xbc, dy_ssd, h_all, a_b, dt_b, at_r, dt_r)
    dak = _pad_lanes(_row_layout_inv(dak_r))
    ddt_direct = _pad_lanes(_row_layout_inv(ddt_r))
    ddt_raw, dalog, dbias = _ssd_prep_bwd(daq_b, dak, ddt_direct, ddt_b, dt, proj, dtb128, alog128)
    gs["a_log"], gs["dt_bias"] = dalog[:, :N_HEADS], dbias[:, :N_HEADS]
    same = lambda c: c
    dxbc_x, dcw_x, dcb_x = _conv_bwd(proj, conv_w, conv_b, dxs_skip, dx_ssd, same, same, 0, 8, "conv_bwd_x")
    dxbc_b, dcw_b, dcb_b = _conv_bwd(proj, conv_w, conv_b, db_pair, db_pair, lambda c: 2 * c, lambda c: 2 * c + 1,
                                     D_SSM, 4, "conv_bwd_b")
    dxbc_c, dcw_c, dcb_c = _conv_bwd(proj, conv_w, conv_b, dc_pair, dc_pair, lambda c: 2 * c, lambda c: 2 * c + 1,
                                     D_SSM + 512, 4, "conv_bwd_c")
    g_conv_w = jnp.concatenate([dcw_x, dcw_b, dcw_c], axis=1)
    gs["conv_b"] = jnp.concatenate([dcb_x, dcb_b, dcb_c], axis=1)

    pieces = [dq_pre, dk_pre, dv_pre, dz, dxbc_x, dxbc_b, dxbc_c, ddt_raw]
    pieces.append(jnp.zeros((S, D_INP - sum(t.shape[1] for t in pieces)), BF))
    dproj = jnp.concatenate(pieces, axis=1)
    dw_in = _matmul(h, dproj, mode="tn", name="mm_in_dw", tm=1024, tn=1280, tk=S, out_dtypes=(BF,))
    gb["w_in"] = _w_in_to_blocks(dw_in)
    token = None if hooks is None else hooks.emit({"w_in": gb["w_in"]}, dw_in)
    dh = _matmul(dproj, wf["w_in"], mode="nt", name="mm_in_dx", tm=1024, tn=512, tk=D_INP // 2, after=token)
    grad_x, _, gs["g_mix"] = _rms_bwd_call(x, p["g_mix"], dh, dx1, "rms_mix_bwd")
    return loss_part, grad_x, gb, gs, g_conv_w


def kernel(x, mem, positions, g_mix, w_in, g_q, g_k, g_attn_out, conv_w, conv_b, dt_bias, a_log, d_skip, g_ssm_out, w_out, g_cross, g_mem, w_cq, w_ckv, g_cq, g_ck, w_co, g_mlp, w_up, w_down, loss_target, m_g_mix, m_w_in, m_g_q, m_g_k, m_g_attn_out, m_conv_w, m_conv_b, m_dt_bias, m_a_log, m_d_skip, m_g_ssm_out, m_w_out, m_g_cross, m_g_mem, m_w_cq, m_w_ckv, m_g_cq, m_g_ck, m_w_co, m_g_mlp, m_w_up, m_w_down, v_g_mix, v_w_in, v_g_q, v_g_k, v_g_attn_out, v_conv_w, v_conv_b, v_dt_bias, v_a_log, v_d_skip, v_g_ssm_out, v_w_out, v_g_cross, v_g_mem, v_w_cq, v_w_ckv, v_g_cq, v_g_ck, v_w_co, v_g_mlp, v_w_up, v_w_down):
    args = dict(locals())
    w = {n: args[n] for n in WEIGHTS}
    m = {n: args["m_" + n] for n in WEIGHTS}
    v = {n: args["v_" + n] for n in WEIGHTS}
    small = {n: w[n] for n, _ in SMALL}
    me = 4 * lax.axis_index("x") + 2 * lax.axis_index("y") + lax.axis_index("c")

    overlap = _Overlap({"w_in": w["w_in"][0].astype(BF), "conv_w": w["conv_w"][0]})
    tok = overlap.token
    overlap.shards = {n: _tie(w[n][0], tok).astype(BF) for n in REST}
    w_in_tied = _tie(w["w_in"][0], tok)
    overlap.behind = [w_in_tied]
    overlap.late = [m["w_in"][0], v["w_in"][0]]
    p = dict(small)
    p["g_mix"] = _tie(p["g_mix"], tok)
    loss_part, grad_x, _, gs, g_conv_w = _local_step(
        x[0], mem[0], positions.reshape(S, 1), loss_target[0], p, {}, overlap)
    adam_in = {"w_in": (w_in_tied, *overlap.late)}
    loss = lax.psum(loss_part, ("x", "y", "c"))

    out = {}
    last = grad_x
    for gi in range(3):
        for n, parts in overlap.finish(gi, last).items():
            w_n, m_n, v_n = adam_in.get(n, (w[n][0], m[n][0], v[n][0]))
            res_n = _adamw(w_n, m_n, v_n, parts, "adamw_" + n)
            out[n] = [t.reshape(w[n].shape) for t in res_n]
            last = res_n[0]

    sm_parts, cw_parts = _all_gather([_pack_small(gs), g_conv_w], "ag_small_grads", after=last)
    sm = [_unpack_small(t) for t in _adamw(_pack_small(small), _pack_small({n: m[n] for n, _ in SMALL}),
                                           _pack_small({n: v[n] for n, _ in SMALL}), sm_parts, "adamw_small")]
    for n, _ in SMALL:
        out[n] = [t[n] for t in sm]
    cols = D_CONV // N_DEV
    cw_mine = lax.dynamic_slice_in_dim(cw_parts, me * cols, cols, axis=2)
    out["conv_w"] = [t.reshape(w["conv_w"].shape) for t in
                     _adamw(w["conv_w"][0], m["conv_w"][0], v["conv_w"][0], cw_mine, "adamw_conv_w")]

    res = [loss, grad_x.reshape(x.shape)]
    for k in range(4):
        res += [out[n][k] for n in WEIGHTS]
    return tuple(res)
```

```python
import functools
import math

import numpy as np
import jax
import jax.numpy as jnp
from jax import lax
from jax.experimental import pallas as pl
from jax.experimental.pallas import tpu as pltpu

F32 = jnp.float32
BF = jnp.bfloat16

N_DEV = 8
S = 2048
D = 2048
D_ATTN = 1024
N_HEADS = 16
HEAD = 64
ROT = 16
ROPE_THETA = 500000.0
D_SSM = 1024
D_CONV = 2048
CONV_W = 4
N_MEM = 256
D_CROSS = 512
CROSS_HEAD = 128
D_FF = 8192
D_IN = 6160
D_INP = 6400
DT_COLBLK = (4 * D_ATTN + D_CONV) // 128
EPS = 1e-6
BLK = 128
NB = S // BLK
LANES = 128
NEG = -1e30

ADAM_LR = 0.001
ADAM_B1 = 0.9
ADAM_B2 = 0.999
ADAM_EPS = 1e-08
ADAM_WD = 0.01
ADAM_STEP = 10

VMEM_LIMIT_BYTES = 48 * 1024 * 1024
ROW_TILE = 256


def _params(*sem):
    return pltpu.CompilerParams(dimension_semantics=sem, vmem_limit_bytes=VMEM_LIMIT_BYTES)


def _matmul(a, b, *, mode, name, tm, tn, tk, out_dtypes=(F32,), b_cb=None, out_cb=None,
            extras=(), epilogue=None, after=None):
    if mode == "tn":
        kk, m = a.shape
    else:
        m, kk = a.shape
    if b_cb is None:
        br, bc = b.shape
    else:
        br, bc = b.shape[1], N_DEV * b_cb
    n = br if mode == "nt" else bc
    assert m % tm == 0 and n % tn == 0 and kk % tk == 0, (name, m, n, kk)
    nk = kk // tk
    grid = (m // tm, n // tn, nk)

    if mode == "tn":
        a_spec = pl.BlockSpec((tk, tm), lambda i, j, k: (k, i))
    else:
        a_spec = pl.BlockSpec((tm, tk), lambda i, j, k: (i, k))
    if mode == "nt":
        b_blk, b_idx = (tn, tk), (lambda i, j, k: (j, k))
    else:
        b_blk, b_idx = (tk, tn), (lambda i, j, k: (k, j))
    group = 1
    if b_cb is None:
        b_spec = pl.BlockSpec(b_blk, b_idx)
    elif mode == "nt" and tk > b_cb:
        assert tk % b_cb == 0
        group = tk // b_cb
        b_spec = pl.BlockSpec((group, tn, b_cb), lambda i, j, k: (k, j, 0))
    else:
        tc = b_blk[1]
        assert b_cb % tc == 0
        per = b_cb // tc

        def b_idx3(i, j, k):
            r, c = b_idx(i, j, k)
            return (c // per, r, c % per)
        b_spec = pl.BlockSpec((None,) + b_blk, b_idx3)
    ex_specs = [pl.BlockSpec((tm, tn), lambda i, j, k: (i, j)) for _ in extras]
    if out_cb is None:
        out_shape = [jax.ShapeDtypeStruct((m, n), dt) for dt in out_dtypes]
        out_specs = [pl.BlockSpec((tm, tn), lambda i, j, k: (i, j)) for _ in out_dtypes]
    else:
        assert out_cb % tn == 0
        pero = out_cb // tn
        out_shape = [jax.ShapeDtypeStruct((N_DEV, m, out_cb), dt) for dt in out_dtypes]
        out_specs = [pl.BlockSpec((None, tm, tn), lambda i, j, k: (j // pero, i, j % pero)) for _ in out_dtypes]
    dn = {"nn": (((1,), (0,)), ((), ())), "nt": (((1,), (1,)), ((), ())), "tn": (((0,), (0,)), ((), ()))}[mode]
    ne, no = len(extras), len(out_dtypes)
    n_after = 0 if after is None else 1

    def kern(a_ref, b_ref, *rest):
        ex_refs, out_refs = rest[:ne], rest[ne + n_after:ne + n_after + no]

        def finish(acc):
            outs = (acc,) if epilogue is None else epilogue(acc, *[r[...] for r in ex_refs])
            for r, o in zip(out_refs, outs):
                r[...] = o.astype(r.dtype)

        if group == 1:
            part = lax.dot_general(a_ref[...].astype(BF), b_ref[...].astype(BF), dn, preferred_element_type=F32)
        else:
            part = sum(lax.dot_general(a_ref[:, g * b_cb:(g + 1) * b_cb].astype(BF), b_ref[g].astype(BF), dn,
                                       preferred_element_type=F32) for g in range(group))
        if nk == 1:
            finish(part)
            return
        acc_ref = rest[ne + n_after + no]
        k = pl.program_id(2)

        @pl.when(k == 0)
        def _():
            acc_ref[...] = part

        @pl.when(k > 0)
        def _():
            acc_ref[...] += part

        @pl.when(k == nk - 1)
        def _():
            finish(acc_ref[...])

    res = pl.pallas_call(
        kern, name=name, grid=grid, in_specs=[a_spec, b_spec] + ex_specs + [ANY] * n_after, out_specs=out_specs,
        out_shape=out_shape, scratch_shapes=[pltpu.VMEM((tm, tn), F32)] if nk > 1 else [],
        compiler_params=_params("parallel", "parallel", "arbitrary"),
    )(a, b, *extras, *([] if after is None else [after]))
    return res[0] if no == 1 else tuple(res)


def _rowwise(body, *, name, nrows, tile, row_ins, full_ins=(), row_outs=(), acc_outs=(), scratch=(),
             reverse=False):
    n = nrows // tile

    def ridx(i):
        return (n - 1 - i) if reverse else i

    in_specs, args = [], []
    for arr, width, cb in row_ins:
        in_specs.append(pl.BlockSpec((tile, width), lambda i, cb=cb: (ridx(i), cb)))
        args.append(arr)
    for arr in full_ins:
        in_specs.append(pl.BlockSpec(arr.shape, lambda i, nd=arr.ndim: (0,) * nd))
        args.append(arr)
    out_shape, out_specs = [], []
    for width, dt in row_outs:
        out_shape.append(jax.ShapeDtypeStruct((nrows, width), dt))
        out_specs.append(pl.BlockSpec((tile, width), lambda i: (ridx(i), 0)))
    for shp, dt in acc_outs:
        out_shape.append(jax.ShapeDtypeStruct(shp, dt))
        out_specs.append(pl.BlockSpec(shp, lambda i, nd=len(shp): (0,) * nd))

    def kern(*refs):
        body(pl.program_id(0), n, *refs)

    return pl.pallas_call(kern, name=name, grid=(n,), in_specs=in_specs, out_specs=out_specs,
                          out_shape=out_shape, scratch_shapes=list(scratch),
                          compiler_params=_params("arbitrary"))(*args)


def _accum(ref, val, i):
    @pl.when(i == 0)
    def _():
        ref[...] = val

    @pl.when(i > 0)
    def _():
        ref[...] += val


def _sigmoid(x):
    return 1.0 / (1.0 + jnp.exp(-x))


def _rms_n(x):
    r = lax.rsqrt(jnp.mean(x * x, axis=-1, keepdims=True) + EPS)
    return x * r, r


def _rms_bwd(x, g, dh):
    n, r = _rms_n(x)
    dn = dh * g
    dx = r * (dn - n * jnp.mean(dn * n, axis=-1, keepdims=True))
    return dx, jnp.sum(dh * n, axis=0, keepdims=True)


def _seg_sum(x, seg):
    t, w = x.shape
    tiles = [x[:, LANES * j:LANES * (j + 1)] for j in range(w // LANES)]
    outs = []
    if seg == 64:
        lo = lax.broadcasted_iota(jnp.int32, (t, LANES), 1) < 64
        for xt in tiles:
            s_lo = jnp.sum(jnp.where(lo, xt, 0.0), axis=-1, keepdims=True)
            s_hi = jnp.sum(jnp.where(lo, 0.0, xt), axis=-1, keepdims=True)
            outs.append(jnp.where(lo, s_lo, s_hi))
    else:
        sums = [jnp.sum(xt, axis=-1, keepdims=True) for xt in tiles]
        if seg == 256:
            sums = [sums[2 * (j // 2)] + sums[2 * (j // 2) + 1] for j in range(len(sums))]
        else:
            assert seg == 128
        outs = [jnp.broadcast_to(s, (t, LANES)) for s in sums]
    return outs[0] if len(outs) == 1 else jnp.concatenate(outs, axis=1)


def _seg_rms_n(x, seg):
    r = lax.rsqrt(_seg_sum(x * x, seg) * (1.0 / seg) + EPS)
    return x * r, r


def _seg_rms_bwd(x, g, dy, seg):
    n, r = _seg_rms_n(x, seg)
    dn = dy * g
    dx = r * (dn - n * (_seg_sum(dn * n, seg) * (1.0 / seg)))
    return dx, jnp.sum(dy * n, axis=0, keepdims=True)


def _rope_tables(pos_col, inv_tab, t):
    ang = pos_col.astype(F32) * inv_tab
    idx = lax.broadcasted_iota(jnp.int32, (t, LANES), 1) % HEAD
    cos, sin = jnp.cos(ang), jnp.sin(ang)
    c = jnp.where(idx < ROT, cos, 1.0)
    sg = jnp.where(idx < ROT // 2, -sin, jnp.where(idx < ROT, sin, 0.0))
    return c, sg, idx < ROT // 2


def _rope(x, c, sg, lo):
    partner = jnp.where(lo, pltpu.roll(x, LANES - ROT // 2, 1), pltpu.roll(x, ROT // 2, 1))
    return x * c + partner * sg


def _fold_heads(v):
    v8 = jnp.broadcast_to(v, (8, LANES))
    return (v8 + pltpu.roll(v8, HEAD, 1))[0:1]


def _dot(a, b, dn):
    return lax.dot_general(a, b, (dn, ((), ())), preferred_element_type=F32)


NN = ((1,), (0,))
NT = ((1,), (1,))
TN = ((0,), (0,))


def _dot3(l01, x):
    x1 = x.astype(BF)
    r1 = x - x1.astype(F32)
    x2 = r1.astype(BF)
    x3 = (r1 - x2.astype(F32)).astype(BF)
    return _dot(l01, x1, NN) + _dot(l01, x2, NN) + _dot(l01, x3, NN)


def _rms_fwd(x, g, name):
    rows = x.shape[0]

    def body(i, n, x_ref, g_ref, o_ref):
        nx, _ = _rms_n(x_ref[...])
        o_ref[...] = (nx * g_ref[...]).astype(BF)

    return _rowwise(body, name=name, nrows=rows, tile=min(ROW_TILE, rows), row_ins=[(x, D, 0)],
                    full_ins=[g], row_outs=[(D, BF)])[0]


def _qk_prep(proj, pos_col, gq128, gk128, inv_tab):
    scale = HEAD ** -0.5

    def body(i, n, q_ref, k_ref, p_ref, gq_ref, gk_ref, it_ref, qo_ref, ko_ref):
        t = q_ref.shape[0]
        c, sg, lo = _rope_tables(p_ref[...], it_ref[...], t)
        for j in range(D_ATTN // LANES):
            sl = slice(LANES * j, LANES * (j + 1))
            qn, _ = _seg_rms_n(q_ref[:, sl], HEAD)
            kn, _ = _seg_rms_n(k_ref[:, sl], HEAD)
            qo_ref[:, sl] = _rope(qn * gq_ref[...], c, sg, lo) * scale
            ko_ref[:, sl] = _rope(kn * gk_ref[...], c, sg, lo)

    return _rowwise(body, name="qk_prep", nrows=S, tile=ROW_TILE,
                    row_ins=[(proj, D_ATTN, 0), (proj, D_ATTN, 1), (pos_col, 1, 0)],
                    full_ins=[gq128, gk128, inv_tab], row_outs=[(D_ATTN, F32)] * 2)


ATTN_QB = 16
V_COLS = pl.BlockSpec((S, LANES), lambda p: (0, 2 * D_ATTN // LANES + p))


def _band(b, d):
    nb = NB // d
    r, n = b // nb, b % nb
    width, kn = (BLK, n) if nb == 1 else (2 * BLK, jnp.maximum(n - 1, 0))

    def rows(first_member, count):
        if d == 1:
            return pl.ds(pl.multiple_of(first_member, BLK), count)
        return pl.ds(first_member * d + r, count, stride=d)

    qm = n * BLK + (lax.broadcasted_iota(jnp.int32, (2 * BLK, width), 0) & (BLK - 1))
    km = kn * BLK + lax.broadcasted_iota(jnp.int32, (2 * BLK, width), 1)
    valid = km <= qm
    if nb > 1:
        valid = valid & (km >= qm - BLK)
    return rows(n * BLK, BLK), rows(kn * BLK, width), valid


DILATIONS = (1, 4, 16)


def _attn_fwd(q, k, v):
    def kern(q_ref, k_ref, v_ref, a_ref, lse_ref, *scr):
        half0 = lax.broadcasted_iota(jnp.int32, (BLK, LANES), 1) < HEAD
        for gi, d in enumerate(DILATIONS):
            o_ref, l_ref = scr[2 * gi], scr[2 * gi + 1]

            def group(g, carry, d=d, o_ref=o_ref, l_ref=l_ref):
                for bb in range(ATTN_QB):
                    q_rows, k_rows, valid = _band(g * ATTN_QB + bb, d)
                    q = q_ref[q_rows, :]
                    kb, vb = k_ref[k_rows, :].astype(BF), v_ref[k_rows, :].astype(BF)
                    q2 = jnp.concatenate([jnp.where(half0, q, 0.0), jnp.where(half0, 0.0, q)], axis=0).astype(BF)
                    s = jnp.where(valid, _dot(q2, kb, NT), NEG)
                    m = jnp.max(s, axis=-1, keepdims=True)
                    p = jnp.exp(s - m)
                    den = jnp.sum(p, axis=-1, keepdims=True)
                    o2 = _dot(p.astype(BF), vb, NN) / den
                    l2 = m + jnp.log(den)
                    o_ref[q_rows, :] = jnp.where(half0, o2[:BLK], o2[BLK:])
                    l_ref[q_rows, :] = jnp.where(half0, l2[:BLK], l2[BLK:])
                return carry

            lax.fori_loop(0, NB // ATTN_QB, group, 0)

        def merge(i, carry):
            rows = pl.ds(pl.multiple_of(i * ROW_TILE, ROW_TILE), ROW_TILE)
            la, lb, lc = scr[1][rows, :], scr[3][rows, :], scr[5][rows, :]
            m = jnp.maximum(jnp.maximum(la, lb), lc)
            ea, eb, ec = jnp.exp(la - m), jnp.exp(lb - m), jnp.exp(lc - m)
            den = ea + eb + ec
            a_ref[rows, :] = (ea * scr[0][rows, :] + eb * scr[2][rows, :] + ec * scr[4][rows, :]) / den
            lse_ref[rows, :] = m + jnp.log(den)
            return carry

        lax.fori_loop(0, S // ROW_TILE, merge, 0)

    seq = pl.BlockSpec((S, LANES), lambda p: (0, p))
    return pl.pallas_call(
        kern, name="attn_fwd", grid=(D_ATTN // LANES,), in_specs=[seq, seq, V_COLS], out_specs=[seq, seq],
        out_shape=[jax.ShapeDtypeStruct((S, D_ATTN), F32)] * 2,
        scratch_shapes=[pltpu.VMEM((S, LANES), F32)] * (2 * len(DILATIONS)),
        compiler_params=_params("parallel"))(q, k, v)


def _attn_norm(attn, g_attn):
    def body(i, n, a_ref, g_ref, an_ref):
        nx, _ = _rms_n(a_ref[...])
        an_ref[...] = (nx * g_ref[...]).astype(BF)

    return _rowwise(body, name="attn_norm", nrows=S, tile=ROW_TILE, row_ins=[(attn, D_ATTN, 0)],
                    full_ins=[g_attn], row_outs=[(D_ATTN, BF)])[0]


def _conv_taps(x, w_ref):
    rows = lax.broadcasted_iota(jnp.int32, x.shape, 0)
    shifted = []
    for w in range(CONV_W):
        k = CONV_W - 1 - w
        shifted.append(x if k == 0 else jnp.where(rows >= k, pltpu.roll(x, k, 0), 0.0))
    acc = shifted[0] * w_ref[0:1, :]
    for w in range(1, CONV_W):
        acc = acc + shifted[w] * w_ref[w:w + 1, :]
    return acc, shifted


def _conv_fwd(proj, conv_w, conv_b):
    tc = 256

    def kern(x_ref, w_ref, b_ref, o_ref):
        c, _ = _conv_taps(x_ref[...], w_ref)
        c = c + b_ref[...]
        o_ref[...] = c * _sigmoid(c)

    return pl.pallas_call(
        kern, name="conv_fwd", grid=(D_CONV // tc,),
        in_specs=[pl.BlockSpec((S, tc), lambda c: (0, 4 * D_ATTN // tc + c)),
                  pl.BlockSpec((CONV_W, tc), lambda c: (0, c)), pl.BlockSpec((1, tc), lambda c: (0, c))],
        out_specs=pl.BlockSpec((S, tc), lambda c: (0, c)),
        out_shape=jax.ShapeDtypeStruct((S, D_CONV), F32), compiler_params=_params("parallel"))(proj, conv_w, conv_b)


def _softplus(x):
    return jnp.maximum(x, 0.0) + jnp.log1p(jnp.exp(-jnp.abs(x)))


def _dot3r(x, r01):
    x1 = x.astype(BF)
    r1 = x - x1.astype(F32)
    x2 = r1.astype(BF)
    x3 = (r1 - x2.astype(F32)).astype(BF)
    return _dot(x1, r01, NN) + _dot(x2, r01, NN) + _dot(x3, r01, NN)


def _spread_heads(v):
    sel = (lax.broadcasted_iota(jnp.int32, (LANES, D_SSM), 1) // HEAD
           == lax.broadcasted_iota(jnp.int32, (LANES, D_SSM), 0))
    return _dot3r(v, sel.astype(BF))


def _collect_heads(v):
    sel = (lax.broadcasted_iota(jnp.int32, (D_SSM, LANES), 0)
           == HEAD * lax.broadcasted_iota(jnp.int32, (D_SSM, LANES), 1))
    return _dot3r(v, sel.astype(BF))


def _ssd_prep(proj, dt_bias128, a_log128):
    def body(i, n, raw_ref, b_ref, al_ref, dt_ref, a_ref, dtb_ref, ab_ref, carry):
        @pl.when(i == 0)
        def _():
            carry[...] = jnp.zeros_like(carry)

        dt = _softplus(raw_ref[...] + b_ref[...])
        da = dt * (-jnp.exp(al_ref[...]))
        tri = (lax.broadcasted_iota(jnp.int32, (BLK, BLK), 0) >= lax.broadcasted_iota(jnp.int32, (BLK, BLK), 1))
        cs = _dot3(tri.astype(BF), da) + carry[0:1, :]
        dt_ref[...] = dt
        a_ref[...] = cs
        dtb_ref[...] = _spread_heads(dt)
        ab_ref[...] = _spread_heads(cs)
        carry[...] = jnp.broadcast_to(cs[BLK - 1:BLK, :], carry.shape)

    return _rowwise(body, name="ssd_prep", nrows=S, tile=BLK, row_ins=[(proj, LANES, DT_COLBLK)],
                    full_ins=[dt_bias128, a_log128],
                    row_outs=[(LANES, F32), (LANES, F32), (D_SSM, F32), (D_SSM, F32)],
                    scratch=[pltpu.VMEM((8, LANES), F32)])


def _ssd_decay(a_col, at_row, causal):
    diff = jnp.where(causal, a_col - at_row, 0.0)
    return jnp.where(causal, jnp.exp(diff), 0.0)


SSD_CB = 16


def _ssd_fwd(xbc, a_b, dt_b, at_r, dt_r):
    def kern(c_ref, b_ref, x_ref, ab_ref, dtb_ref, at_ref, dt_ref, y_ref, hall_ref):
        half0 = lax.broadcasted_iota(jnp.int32, (BLK, LANES), 1) < HEAD
        causal = lax.broadcasted_iota(jnp.int32, (BLK, BLK), 1) <= lax.broadcasted_iota(jnp.int32, (BLK, BLK), 0)

        def step(i, carry):
            h, a_prev = carry
            for cc in range(SSD_CB):
                chunk = i * SSD_CB + cc
                rows = pl.ds(pl.multiple_of(chunk * BLK, BLK), BLK)
                ci, bi, x = c_ref[rows, :].astype(BF), b_ref[rows, :].astype(BF), x_ref[rows, :]
                ab = ab_ref[rows, :]
                a_end = ab[BLK - 1:BLK, :]
                alpha = jnp.exp(ab - a_prev)
                beta = jnp.exp(a_end - ab) * dtb_ref[rows, :]
                hall_ref[rows, :] = h
                cb = _dot(ci, bi, NT)
                at, dtj = at_ref[chunk], dt_ref[chunk]
                y = alpha * _dot(ci, h.astype(BF), NN)
                for hd in range(2):
                    hm = half0 if hd == 0 else jnp.logical_not(half0)
                    w = cb * _ssd_decay(ab[:, HEAD * hd:HEAD * hd + 1], at[hd:hd + 1, :], causal) * dtj[hd:hd + 1, :]
                    y = y + _dot(w.astype(BF), jnp.where(hm, x, 0.0).astype(BF), NN)
                y_ref[rows, :] = y
                h = alpha[BLK - 1:BLK, :] * h + _dot(bi, (beta * x).astype(BF), TN)
                a_prev = a_end
            return h, a_prev

        lax.fori_loop(0, NB // SSD_CB, step, (jnp.zeros((BLK, LANES), F32), jnp.zeros((1, LANES), F32)))

    npair = D_SSM // LANES
    rowvec = pl.BlockSpec((None, NB, 2, BLK), lambda p: (p, 0, 0, 0))
    seq = pl.BlockSpec((S, LANES), lambda p: (0, p))
    return pl.pallas_call(
        kern, name="ssd_fwd", grid=(npair,),
        in_specs=[pl.BlockSpec((S, LANES), lambda p: (0, 12 + p // 2)),
                  pl.BlockSpec((S, LANES), lambda p: (0, 8 + p // 2)), seq, seq, seq, rowvec, rowvec],
        out_specs=[seq, seq], out_shape=[jax.ShapeDtypeStruct((S, D_SSM), F32)] * 2,
        compiler_params=_params("parallel"))(xbc, xbc, xbc, a_b, dt_b, at_r, dt_r)


def _ssd_post(y_ssd, xbc, proj, dskip_b, g_ssm):
    def body(i, n, y_ref, xs_ref, z_ref, d_ref, g_ref, o_ref):
        z = z_ref[...]
        y2 = (y_ref[...] + d_ref[...] * xs_ref[...]) * (z * _sigmoid(z))
        nx, _ = _seg_rms_n(y2, 256)
        o_ref[...] = (nx * g_ref[...]).astype(BF)

    return _rowwise(body, name="ssd_post", nrows=S, tile=ROW_TILE,
                    row_ins=[(y_ssd, D_SSM, 0), (xbc, D_SSM, 0), (proj, D_SSM, 3)], full_ins=[dskip_b, g_ssm],
                    row_outs=[(D_SSM, BF)])[0]


def _cross_heads(q, kv_ref, gq, gk):
    out = []
    for h in range(D_CROSS // CROSS_HEAD):
        sl = slice(CROSS_HEAD * h, CROSS_HEAD * (h + 1))
        nq, rq = _rms_n(q[:, sl])
        nk, rk = _rms_n(kv_ref[:, sl])
        v = kv_ref[:, D_CROSS + CROSS_HEAD * h:D_CROSS + CROSS_HEAD * (h + 1)]
        out.append((sl, nq, rq, nk, rk, v))
    return out


def _cross_fwd(qc, kv, g_cq, g_ck):
    scale = CROSS_HEAD ** -0.5

    def body(i, n, q_ref, kv_ref, gq_ref, gk_ref, o_ref):
        for sl, nq, _, nk, _, v in _cross_heads(q_ref[...], kv_ref, gq_ref[...], gk_ref[...]):
            qn = (nq * gq_ref[...] * scale).astype(BF)
            kn = (nk * gk_ref[...]).astype(BF)
            s = _dot(qn, kn, NT)
            e = jnp.exp(s - jnp.max(s, axis=-1, keepdims=True))
            p = e / jnp.sum(e, axis=-1, keepdims=True)
            o_ref[:, sl] = _dot(p.astype(BF), v.astype(BF), NN).astype(BF)

    return _rowwise(body, name="cross_fwd", nrows=S, tile=ROW_TILE, row_ins=[(qc, D_CROSS, 0)],
                    full_ins=[kv, g_cq, g_ck], row_outs=[(D_CROSS, BF)])[0]


def _rms_bwd_call(x, g, dh, dres, name, sumsq_res=False):
    rows = x.shape[0]
    has_res = dres is not None

    def body(i, n, *refs):
        if has_res:
            x_ref, dh_ref, dr_ref, g_ref, dx_ref, dxb_ref, dg_ref = refs[:7]
        else:
            x_ref, dh_ref, g_ref, dg_ref = refs
        dx, dg = _rms_bwd(x_ref[...], g_ref[...], dh_ref[...])
        if has_res:
            dr = dr_ref[...]
            dx = dx + dr
            dx_ref[...] = dx
            dxb_ref[...] = dx.astype(BF)
            if sumsq_res:
                _accum(refs[7], jnp.sum(jnp.sum(dr * dr, axis=0, keepdims=True), axis=1, keepdims=True), i)
        _accum(dg_ref, dg, i)

    row_ins = [(x, D, 0), (dh, D, 0)] + ([(dres, D, 0)] if has_res else [])
    return _rowwise(body, name=name, nrows=rows, tile=min(ROW_TILE, rows), row_ins=row_ins, full_ins=[g],
                    row_outs=[(D, F32), (D, BF)] if has_res else [],
                    acc_outs=[((1, D), F32)] + ([((1, 1), F32)] if sumsq_res else []))


def _cross_bwd(qc, doc, kv, g_cq, g_ck):
    scale = CROSS_HEAD ** -0.5

    def body(i, n, q_ref, do_ref, kv_ref, gq_ref, gk_ref, dq_ref, dkn_ref, dv_ref, dgq_ref):
        dgq = jnp.zeros((1, CROSS_HEAD), F32)
        dkn_parts, dv_parts = [], []
        for sl, nq, rq, nk, _, v in _cross_heads(q_ref[...], kv_ref, gq_ref[...], gk_ref[...]):
            qn = (nq * gq_ref[...] * scale).astype(BF)
            kn = (nk * gk_ref[...]).astype(BF)
            s = _dot(qn, kn, NT)
            e = jnp.exp(s - jnp.max(s, axis=-1, keepdims=True))
            p = e / jnp.sum(e, axis=-1, keepdims=True)
            do = do_ref[:, sl].astype(BF)
            dv_parts.append(_dot(p.astype(BF), do, TN))
            dp = _dot(do, v.astype(BF), NT)
            ds = (p * (dp - jnp.sum(dp * p, axis=-1, keepdims=True))).astype(BF)
            dqn = _dot(ds, kn, NN)
            dkn_parts.append(_dot(ds, qn, TN))
            dn = dqn * (gq_ref[...] * scale)
            dq_ref[:, sl] = (rq * (dn - nq * jnp.mean(dn * nq, axis=-1, keepdims=True))).astype(BF)
            dgq = dgq + jnp.sum(dqn * scale * nq, axis=0, keepdims=True)
        _accum(dkn_ref, jnp.concatenate(dkn_parts, axis=1), i)
        _accum(dv_ref, jnp.concatenate(dv_parts, axis=1), i)
        _accum(dgq_ref, dgq, i)

    return _rowwise(body, name="cross_bwd", nrows=S, tile=ROW_TILE, row_ins=[(qc, D_CROSS, 0), (doc, D_CROSS, 0)],
                    full_ins=[kv, g_cq, g_ck], row_outs=[(D_CROSS, BF)],
                    acc_outs=[((N_MEM, D_CROSS), F32), ((N_MEM, D_CROSS), F32), ((1, CROSS_HEAD), F32)])


def _cross_kv_bwd(kv, dkn, dv, g_ck):
    def body(i, n, kv_ref, dkn_ref, dv_ref, gk_ref, dkv_ref, dgk_ref):
        dgk = jnp.zeros((1, CROSS_HEAD), F32)
        for h in range(D_CROSS // CROSS_HEAD):
            sl = slice(CROSS_HEAD * h, CROSS_HEAD * (h + 1))
            dk, dg = _rms_bwd(kv_ref[:, sl], gk_ref[...], dkn_ref[:, sl])
            dkv_ref[:, sl] = dk.astype(BF)
            dgk = dgk + dg
        dkv_ref[:, D_CROSS:] = dv_ref[...].astype(BF)
        dgk_ref[...] = dgk

    return _rowwise(body, name="cross_kv_bwd", nrows=N_MEM, tile=N_MEM,
                    row_ins=[(kv, 2 * D_CROSS, 0), (dkn, D_CROSS, 0), (dv, D_CROSS, 0)], full_ins=[g_ck],
                    row_outs=[(2 * D_CROSS, BF)], acc_outs=[((1, CROSS_HEAD), F32)])


def _attn_merge_bwd(dmix, attn, g_attn):
    def body(i, n, dn_ref, a_ref, g_ref, da_ref, dl_ref, dg_ref):
        attn_v = a_ref[...]
        da, dg = _rms_bwd(attn_v, g_ref[...], dn_ref[...])
        da_ref[...] = da
        dl_ref[...] = _seg_sum(da * attn_v, HEAD)
        _accum(dg_ref, dg, i)

    return _rowwise(body, name="attn_merge_bwd", nrows=S, tile=ROW_TILE,
                    row_ins=[(dmix, D_ATTN, 0), (attn, D_ATTN, 0)], full_ins=[g_attn],
                    row_outs=[(D_ATTN, F32), (D_ATTN, F32)], acc_outs=[((1, D_ATTN), F32)])


def _attn_bwd(q, k, v, do, lse, delta):
    def kern(q_ref, k_ref, v_ref, do_ref, l_ref, d_ref, dq_ref, dk_ref, dv_ref):
        dk_ref[...] = jnp.zeros_like(dk_ref)
        dv_ref[...] = jnp.zeros_like(dv_ref)
        half0 = lax.broadcasted_iota(jnp.int32, (BLK, LANES), 1) < HEAD
        for gi, d in enumerate(DILATIONS):

            def group(g, carry, d=d, first=(gi == 0)):
                updates = []
                for bb in range(ATTN_QB):
                    q_rows, k_rows, valid = _band(g * ATTN_QB + bb, d)
                    q, do = q_ref[q_rows, :], do_ref[q_rows, :]
                    lse_t, del_t = l_ref[q_rows, :], d_ref[q_rows, :]
                    kb, vb = k_ref[k_rows, :].astype(BF), v_ref[k_rows, :].astype(BF)
                    q2 = jnp.concatenate([jnp.where(half0, q, 0.0), jnp.where(half0, 0.0, q)], axis=0).astype(BF)
                    do2 = jnp.concatenate([jnp.where(half0, do, 0.0), jnp.where(half0, 0.0, do)], axis=0).astype(BF)
                    lse2 = jnp.concatenate([lse_t[:, 0:1], lse_t[:, HEAD:HEAD + 1]], axis=0)
                    del2 = jnp.concatenate([del_t[:, 0:1], del_t[:, HEAD:HEAD + 1]], axis=0)
                    s = jnp.where(valid, _dot(q2, kb, NT), NEG)
                    p = jnp.exp(s - lse2)
                    ds = (p * (_dot(do2, vb, NT) - del2)).astype(BF)
                    dq2 = _dot(ds, kb, NN)
                    dq = jnp.where(half0, dq2[:BLK], dq2[BLK:])
                    if first:
                        dq_ref[q_rows, :] = dq
                    else:
                        dq_ref[q_rows, :] += dq
                    updates.append((k_rows, _dot(ds, q2, TN), _dot(p.astype(BF), do2, TN)))
                for k_rows, dk, dv in updates:
                    dk_ref[k_rows, :] += dk
                    dv_ref[k_rows, :] += dv
                return carry

            lax.fori_loop(0, NB // ATTN_QB, group, 0)

    seq = pl.BlockSpec((S, LANES), lambda p: (0, p))
    return pl.pallas_call(
        kern, name="attn_bwd", grid=(D_ATTN // LANES,), in_specs=[seq, seq, V_COLS, seq, seq, seq],
        out_specs=[seq, seq, seq], out_shape=[jax.ShapeDtypeStruct((S, D_ATTN), F32)] * 3,
        compiler_params=_params("parallel"))(q, k, v, do, lse, delta)


def _qk_bwd(dq_rot, dk_rot, dv, proj, pos_col, gq128, gk128, inv_tab):
    scale = HEAD ** -0.5

    def body(i, n, dq_ref, dk_ref, dv_ref, q_ref, k_ref, p_ref, gq_ref, gk_ref, it_ref,
             dqo_ref, dko_ref, dvo_ref, dgq_ref, dgk_ref):
        t = q_ref.shape[0]
        c, sg, lo = _rope_tables(p_ref[...], it_ref[...], t)
        dgq = jnp.zeros((1, LANES), F32)
        dgk = jnp.zeros((1, LANES), F32)
        for j in range(D_ATTN // LANES):
            sl = slice(LANES * j, LANES * (j + 1))
            dqr = _rope(dq_ref[:, sl], c, -sg, lo) * scale
            dkr = _rope(dk_ref[:, sl], c, -sg, lo)
            dq, gq = _seg_rms_bwd(q_ref[:, sl], gq_ref[...], dqr, HEAD)
            dk, gk = _seg_rms_bwd(k_ref[:, sl], gk_ref[...], dkr, HEAD)
            dqo_ref[:, sl] = dq.astype(BF)
            dko_ref[:, sl] = dk.astype(BF)
            dgq, dgk = dgq + gq, dgk + gk
        dvo_ref[...] = dv_ref[...].astype(BF)
        _accum(dgq_ref, _fold_heads(dgq), i)
        _accum(dgk_ref, _fold_heads(dgk), i)

    return _rowwise(body, name="qk_bwd", nrows=S, tile=ROW_TILE,
                    row_ins=[(a, D_ATTN, 0) for a in (dq_rot, dk_rot, dv)]
                    + [(proj, D_ATTN, 0), (proj, D_ATTN, 1), (pos_col, 1, 0)],
                    full_ins=[gq128, gk128, inv_tab], row_outs=[(D_ATTN, BF)] * 3,
                    acc_outs=[((1, LANES), F32)] * 2)


def _ssd_post_bwd(y_ssd, xbc, proj, dmix, dskip_b, g_ssm):
    def body(i, n, y_ref, xs_ref, z_ref, do_ref, d_ref, g_ref, dy_ref, dz_ref, dxs_ref, dd_ref, dg_ref):
        z, xs = z_ref[...], xs_ref[...]
        sg = _sigmoid(z)
        gate = z * sg
        y = y_ref[...] + d_ref[...] * xs
        dy2, dg = _seg_rms_bwd(y * gate, g_ref[...], do_ref[...], 256)
        dy = dy2 * gate
        dy_ref[...] = dy.astype(BF)
        dz_ref[...] = (dy2 * y * (sg * (1.0 + z * (1.0 - sg)))).astype(BF)
        dxs_ref[...] = dy * d_ref[...]
        _accum(dd_ref, jnp.sum(dy * xs, axis=0, keepdims=True), i)
        _accum(dg_ref, dg, i)

    return _rowwise(body, name="ssd_post_bwd", nrows=S, tile=ROW_TILE,
                    row_ins=[(y_ssd, D_SSM, 0), (xbc, D_SSM, 0), (proj, D_SSM, 3), (dmix, D_SSM, 1)],
                    full_ins=[dskip_b, g_ssm], row_outs=[(D_SSM, BF), (D_SSM, BF), (D_SSM, F32)],
                    acc_outs=[((1, D_SSM), F32), ((1, D_SSM), F32)])


def _ssd_bwd(xbc, dy, h_all, a_b, dt_b, at_r, dt_r):
    def kern(c_ref, b_ref, x_ref, dy_ref, hall_ref, ab_ref, dtb_ref, at_ref, dt_ref,
             dc_ref, daq_ref, dx_ref, db_ref, dak_ref, ddt_ref, ddtb_ref):
        half0 = lax.broadcasted_iota(jnp.int32, (BLK, LANES), 1) < HEAD
        hms = (half0, jnp.logical_not(half0))
        causal = lax.broadcasted_iota(jnp.int32, (BLK, BLK), 1) <= lax.broadcasted_iota(jnp.int32, (BLK, BLK), 0)
        row = lax.broadcasted_iota(jnp.int32, (BLK, LANES), 0)

        def step(t, carry_in):
            dh_next, carry = carry_in
            for cc in reversed(range(SSD_CB)):
                chunk = (NB // SSD_CB - 1 - t) * SSD_CB + cc
                rows = pl.ds(pl.multiple_of(chunk * BLK, BLK), BLK)
                ci, bi, x = c_ref[rows, :].astype(BF), b_ref[rows, :].astype(BF), x_ref[rows, :]
                dyv = dy_ref[rows, :].astype(F32)
                ab, dtb = ab_ref[rows, :], dtb_ref[rows, :]
                before = ab_ref[pl.ds(pl.multiple_of(jnp.maximum(chunk * BLK - 8, 0), 8), 8), :][7:8, :]
                a_prev = jnp.where(chunk == 0, 0.0, before)
                a_end = ab[BLK - 1:BLK, :]
                alpha = jnp.exp(ab - a_prev)
                e_end = jnp.exp(a_end - ab)
                beta = e_end * dtb
                t_all = alpha[BLK - 1:BLK, :]
                hc = hall_ref[rows, :]
                hcb, dhb = hc.astype(BF), dh_next.astype(BF)

                cb = _dot(ci, bi, NT)
                at, dtj = at_ref[chunk], dt_ref[chunk]
                dcb = jnp.zeros((BLK, BLK), F32)
                dx = jnp.zeros((BLK, LANES), F32)
                rsum = []
                for hd in range(2):
                    dym = jnp.where(hms[hd], dyv, 0.0).astype(BF)
                    lm = _ssd_decay(ab[:, HEAD * hd:HEAD * hd + 1], at[hd:hd + 1, :], causal)
                    dth = dtj[hd:hd + 1, :]
                    dw = _dot(dym, jnp.where(hms[hd], x, 0.0).astype(BF), NT)
                    g = dw * cb * lm
                    dx = dx + _dot((cb * lm * dth).astype(BF), dym, TN)
                    gcol = jnp.sum(g, axis=0, keepdims=True)
                    ddt_ref[chunk, hd:hd + 1, :] = gcol
                    dak_ref[chunk, hd:hd + 1, :] = gcol * dth
                    rsum.append(jnp.sum(g * dth, axis=1, keepdims=True))
                    dcb = dcb + dw * lm * dth
                dcb = dcb.astype(BF)

                g1 = (alpha * dyv).astype(BF)
                dalpha = dyv * _dot(ci, hcb, NN)
                g2 = _dot(bi, dhb, NN)
                dbeta = x * g2
                bx = (beta * x).astype(BF)
                dc_ref[rows, :] = _dot(dcb, bi, NN) + _dot(g1, hcb, NT)
                db_ref[rows, :] = _dot(dcb, ci, TN) + _dot(bx, dhb, NT)
                dx_ref[rows, :] = dx + beta * g2
                ddtb_ref[rows, :] = _seg_sum(e_end * dbeta, HEAD)
                ada, bdb = alpha * dalpha, beta * dbeta
                t_dt = t_all * jnp.sum(dh_next * hc, axis=0, keepdims=True)
                end_term = _seg_sum(jnp.broadcast_to(jnp.sum(bdb, axis=0, keepdims=True) + t_dt, (8, LANES)), HEAD)[0:1]
                prev_term = _seg_sum(jnp.broadcast_to(jnp.sum(ada, axis=0, keepdims=True) + t_dt, (8, LANES)), HEAD)[0:1]
                daq = jnp.where(half0, rsum[0], rsum[1]) + _seg_sum(ada - bdb, HEAD)
                daq_ref[rows, :] = daq + jnp.where(row == BLK - 1, end_term - carry, 0.0)
                carry = prev_term
                dh_next = t_all * dh_next + _dot(ci, g1, TN)
            return dh_next, carry

        lax.fori_loop(0, NB // SSD_CB, step, (jnp.zeros((BLK, LANES), F32), jnp.zeros((1, LANES), F32)))

    npair = D_SSM // LANES
    rowvec = pl.BlockSpec((None, NB, 2, BLK), lambda p: (p, 0, 0, 0))
    seq = pl.BlockSpec((S, LANES), lambda p: (0, p))
    return pl.pallas_call(
        kern, name="ssd_bwd", grid=(npair,),
        in_specs=[pl.BlockSpec((S, LANES), lambda p: (0, 12 + p // 2)),
                  pl.BlockSpec((S, LANES), lambda p: (0, 8 + p // 2)),
                  seq, seq, seq, seq, seq, rowvec, rowvec],
        out_specs=[seq, seq, seq, seq, rowvec, rowvec, seq],
        out_shape=[jax.ShapeDtypeStruct((S, D_SSM), F32)] * 4
        + [jax.ShapeDtypeStruct((npair, NB, 2, BLK), F32)] * 2 + [jax.ShapeDtypeStruct((S, D_SSM), F32)],
        compiler_params=_params("parallel"))(xbc, xbc, xbc, dy, h_all, a_b, dt_b, at_r, dt_r)


def _ssd_prep_bwd(daq, dak, ddt_row, ddt_lane, dt, proj, dt_bias128, a_log128):
    def body(i, n, daq_ref, dak_ref, dd_ref, dd2_ref, dt_ref, raw_ref, b_ref, al_ref, draw_ref, dal_ref, db_ref,
             carry):
        @pl.when(i == 0)
        def _():
            carry[...] = jnp.zeros_like(carry)

        a = -jnp.exp(al_ref[...])
        tri = (lax.broadcasted_iota(jnp.int32, (BLK, BLK), 0) <= lax.broadcasted_iota(jnp.int32, (BLK, BLK), 1))
        rev = _dot3(tri.astype(BF), _collect_heads(daq_ref[...]) - dak_ref[...]) + carry[0:1, :]
        carry[...] = jnp.broadcast_to(rev[0:1, :], carry.shape)
        dtv = dt_ref[...]
        draw = (dd_ref[...] + _collect_heads(dd2_ref[...]) + a * rev) * _sigmoid(raw_ref[...] + b_ref[...])
        draw_ref[...] = draw.astype(BF)
        _accum(dal_ref, jnp.sum(dtv * rev, axis=0, keepdims=True) * a, i)
        _accum(db_ref, jnp.sum(draw, axis=0, keepdims=True), i)

    return _rowwise(body, name="ssd_prep_bwd", nrows=S, tile=BLK, reverse=True,
                    row_ins=[(daq, D_SSM, 0), (dak, LANES, 0), (ddt_row, LANES, 0), (ddt_lane, D_SSM, 0), (dt, LANES, 0),
                             (proj, LANES, DT_COLBLK)],
                    full_ins=[dt_bias128, a_log128], row_outs=[(LANES, BF)],
                    acc_outs=[((1, LANES), F32), ((1, LANES), F32)], scratch=[pltpu.VMEM((8, LANES), F32)])


def _conv_bwd(proj, conv_w, conv_b, d1, d2, map1, map2, col0, ntiles, name):
    base = (4 * D_ATTN + col0) // LANES

    def kern(x_ref, w_ref, b_ref, d1_ref, d2_ref, dx_ref, dw_ref, db_ref):
        x = x_ref[...]
        c, shifted = _conv_taps(x, w_ref)
        c = c + b_ref[...]
        sg = _sigmoid(c)
        dc = (d1_ref[...] + d2_ref[...]) * (sg * (1.0 + c * (1.0 - sg)))
        rows = lax.broadcasted_iota(jnp.int32, x.shape, 0)
        dx = jnp.zeros_like(x)
        for w in range(CONV_W):
            k = CONV_W - 1 - w
            up = dc if k == 0 else jnp.where(rows < S - k, pltpu.roll(dc, S - k, 0), 0.0)
            dx = dx + up * w_ref[w:w + 1, :]
            dw_ref[w:w + 1, :] = jnp.sum(dc * shifted[w], axis=0, keepdims=True)
        dx_ref[...] = dx.astype(BF)
        db_ref[...] = jnp.sum(dc, axis=0, keepdims=True)

    c0 = col0 // LANES
    return pl.pallas_call(
        kern, name=name, grid=(ntiles,),
        in_specs=[pl.BlockSpec((S, LANES), lambda c: (0, base + c)),
                  pl.BlockSpec((CONV_W, LANES), lambda c: (0, c0 + c)),
                  pl.BlockSpec((1, LANES), lambda c: (0, c0 + c)),
                  pl.BlockSpec((S, LANES), lambda c: (0, map1(c))),
                  pl.BlockSpec((S, LANES), lambda c: (0, map2(c)))],
        out_specs=[pl.BlockSpec((S, LANES), lambda c: (0, c)), pl.BlockSpec((CONV_W, LANES), lambda c: (0, c)),
                   pl.BlockSpec((1, LANES), lambda c: (0, c))],
        out_shape=[jax.ShapeDtypeStruct((S, ntiles * LANES), BF), jax.ShapeDtypeStruct((CONV_W, ntiles * LANES), F32),
                   jax.ShapeDtypeStruct((1, ntiles * LANES), F32)],
        compiler_params=_params("parallel"))(proj, conv_w, conv_b, d1, d2)


def _adamw(w, m, v, parts, name):
    r, c = w.shape
    n_parts = parts.shape[0]
    tile = r if r <= 512 else (128 if c > 1024 else 256)
    assert r % tile == 0
    c1 = 1.0 - ADAM_B1 ** ADAM_STEP
    c2 = 1.0 - ADAM_B2 ** ADAM_STEP

    def kern(w_ref, m_ref, v_ref, p_ref, g_ref, d_ref, mo_ref, vo_ref):
        g = p_ref[0].astype(F32)
        for k in range(1, n_parts):
            g = g + p_ref[k].astype(F32)
        mn = ADAM_B1 * m_ref[...] + (1.0 - ADAM_B1) * g
        vn = ADAM_B2 * v_ref[...] + (1.0 - ADAM_B2) * (g * g)
        g_ref[...] = g
        mo_ref[...] = mn
        vo_ref[...] = vn
        d_ref[...] = -ADAM_LR * ((mn / c1) / (jnp.sqrt(vn / c2) + ADAM_EPS) + ADAM_WD * w_ref[...])

    blk = pl.BlockSpec((tile, c), lambda i: (i, 0))
    return pl.pallas_call(
        kern, name=name, grid=(r // tile,),
        in_specs=[blk, blk, blk, pl.BlockSpec((n_parts, tile, c), lambda i: (0, i, 0))],
        out_specs=[blk] * 4, out_shape=[jax.ShapeDtypeStruct((r, c), F32)] * 4,
        compiler_params=_params("parallel"))(w, m, v, parts)


MESH = pl.DeviceIdType.MESH
ANY = pl.BlockSpec(memory_space=pl.ANY)


def _put_own(gathered, shard):
    me = 4 * lax.axis_index("x") + 2 * lax.axis_index("y") + lax.axis_index("c")
    return lax.dynamic_update_slice(gathered, shard[None], (me,) + (0,) * shard.ndim)


def _all_gather(arrs, name, after=None):
    na = len(arrs)
    n_after = 0 if after is None else 1

    def kern(*refs):
        ins, outs = refs[:na], refs[na + n_after:2 * na + n_after]
        send_sems, recv_sems = refs[2 * na + n_after:]
        x, y, c = lax.axis_index("x"), lax.axis_index("y"), lax.axis_index("c")
        me, sibling = (x, y, c), (x, y, 1 - c)
        a_chip = (jnp.where(c == 0, 1 - x, x), jnp.where(c == 0, y, 1 - y))
        b_chip = (jnp.where(c == 0, x, 1 - x), jnp.where(c == 0, 1 - y, y))
        chips = [a_chip, b_chip, (1 - x, 1 - y)]

        def copy(a, k, block, to, src=None):
            px, py, pc = block
            dst = outs[a].at[4 * px + 2 * py + pc]
            return pltpu.make_async_remote_copy(
                src_ref=dst if src is None else src, dst_ref=dst, send_sem=send_sems.at[a, k],
                recv_sem=recv_sems.at[a, k], device_id=to, device_id_type=MESH)

        sends = []
        for a in range(na):
            sends.append(copy(a, 0, me, sibling, src=ins[a]))
            sends += [copy(a, 1 + j, me, (*chips[j], c), src=ins[a]) for j in range(2)]
        for cp in sends:
            cp.start()
        for a in range(na):
            for j, chip in enumerate(chips):
                copy(a, 1 + j, (*chip, c), me).wait_recv()
                later = [copy(a, 4 + j, (*chip, c), sibling)]
                if j == 0:
                    later.append(copy(a, 3, (*a_chip, c), (*b_chip, c)))
                for cp in later:
                    cp.start()
                sends += later
        for a in range(na):
            copy(a, 0, sibling, me).wait_recv()
            for j, chip in enumerate(chips):
                copy(a, 4 + j, (*chip, 1 - c), me).wait_recv()
        for cp in sends:
            cp.wait_send()

    outs = pl.pallas_call(
        kern, name=name, in_specs=[ANY] * (na + n_after), out_specs=[ANY] * na,
        out_shape=[jax.ShapeDtypeStruct((N_DEV,) + a.shape, a.dtype) for a in arrs],
        scratch_shapes=[pltpu.SemaphoreType.DMA((na, 7)), pltpu.SemaphoreType.DMA((na, 7))],
    )(*arrs, *([] if after is None else [after]))
    return [_put_own(o, a) for o, a in zip(outs, arrs)]


HBM = pl.BlockSpec(memory_space=pltpu.HBM)
SEM = pl.BlockSpec(memory_space=pltpu.SEMAPHORE)
EFFECT = pltpu.SideEffectType.DATAFLOW_SIDE_EFFECTING
N_CHIP = 4
N_COPIES = {"gather": 3, "scatter": 3, "forward": 4, "pair": 4, "direct": 2, "relay": 4, "diag": 1}


def _in_hbm(a):
    return pltpu.with_memory_space_constraint(a, pltpu.HBM)


def _chip_copies(kind, src_refs, land_refs, send_sems, recv_sems):
    x, y, c = lax.axis_index("x"), lax.axis_index("y"), lax.axis_index("c")
    chips = [(1 - x, y), (x, 1 - y), (1 - x, 1 - y)]
    a_chip = (jnp.where(c == 0, 1 - x, x), jnp.where(c == 0, y, 1 - y))
    b_chip = (jnp.where(c == 0, x, 1 - x), jnp.where(c == 0, 1 - y, y))
    n = N_COPIES[kind]
    cps = []

    def add(a, j, src, dst, to):
        cps.append(pltpu.make_async_remote_copy(
            src_ref=src, dst_ref=dst, send_sem=send_sems.at[n * a + j], recv_sem=recv_sems.at[n * a + j],
            device_id=to, device_id_type=MESH))

    def slot(a, chip, core):
        return land_refs[a].at[4 * chip[0] + 2 * chip[1] + core]

    for a in range(len(src_refs)):
        if kind == "direct":
            add(a, 0, src_refs[a], slot(a, (x, y), c), (*a_chip, c))
            add(a, 1, src_refs[a], slot(a, (x, y), c), (*b_chip, c))
        elif kind == "relay":
            add(a, 0, slot(a, a_chip, c), slot(a, a_chip, c), (*b_chip, c))
            add(a, 1, src_refs[a], slot(a, (x, y), c), (x, y, 1 - c))
            add(a, 2, slot(a, a_chip, c), slot(a, a_chip, c), (x, y, 1 - c))
            add(a, 3, slot(a, b_chip, c), slot(a, b_chip, c), (x, y, 1 - c))
        elif kind == "diag":
            add(a, 0, slot(a, (1 - x, 1 - y), c), slot(a, (1 - x, 1 - y), c), (x, y, 1 - c))
        elif kind == "gather":
            for j, (px, py) in enumerate(chips):
                add(a, j, src_refs[a], land_refs[a].at[4 * x + 2 * y + c], (px, py, c))
        elif kind == "scatter":
            for j, (px, py) in enumerate(chips):
                add(a, j, src_refs[a].at[2 * px + py], land_refs[a].at[2 * x + y], (px, py, c))
        elif kind == "forward":
            add(a, 0, src_refs[a], land_refs[a].at[4 * x + 2 * y + c], (x, y, 1 - c))
            for j, (px, py) in enumerate(chips):
                slot = land_refs[a].at[4 * px + 2 * py + c]
                add(a, 1 + j, slot, slot, (x, y, 1 - c))
        else:
            for q in range(N_CHIP):
                add(a, q, src_refs[a].at[2 * q + 1 - c], land_refs[a].at[q], (x, y, 1 - c))
    return cps


def _exchange_start(kind, srcs, lands, after, name):
    na = len(srcs)
    n_after = 0 if after is None else 1

    def kern(*refs):
        src_refs, land_refs = refs[:na], refs[na:2 * na]
        send_sems, recv_sems = refs[2 * na + n_after], refs[2 * na + n_after + 1]
        token = refs[-1]
        for cp in _chip_copies(kind, src_refs, land_refs, send_sems, recv_sems):
            cp.start()
        token[...] = jnp.zeros_like(token)

    bufs = list(srcs) + list(lands)
    res = pl.pallas_call(
        kern, name=name,
        out_shape=(pltpu.SemaphoreType.DMA((N_COPIES[kind] * na,)), pltpu.SemaphoreType.DMA((N_COPIES[kind] * na,)))
        + tuple(pltpu.HBM(b.shape, b.dtype) for b in bufs) + (jax.ShapeDtypeStruct((8, LANES), F32),),
        in_specs=[HBM] * (2 * na) + [ANY] * n_after,
        out_specs=(SEM, SEM) + (HBM,) * (2 * na) + (pl.BlockSpec(memory_space=pltpu.VMEM),),
        input_output_aliases={i: 2 + i for i in range(2 * na)},
        compiler_params=pltpu.CompilerParams(has_side_effects=EFFECT),
    )(*[_in_hbm(b) for b in bufs], *([] if after is None else [after]))
    return (res[0], res[1]), list(res[2:2 + na]), list(res[2 + na:2 + 2 * na]), res[-1]


def _exchange_wait(kind, sems, srcs, lands, after, name):
    na = len(srcs)
    afters = list(after) if isinstance(after, (list, tuple)) else [after]

    def kern(*refs):
        src_refs, land_refs = refs[:na], refs[na:2 * na]
        send_sems, recv_sems = refs[2 * na], refs[2 * na + 1]
        for cp in _chip_copies(kind, src_refs, land_refs, send_sems, recv_sems):
            cp.wait_send()
            cp.wait_recv()

    bufs = list(srcs) + list(lands)
    res = pl.pallas_call(
        kern, name=name, out_shape=tuple(pltpu.HBM(b.shape, b.dtype) for b in bufs),
        in_specs=[HBM] * (2 * na) + [SEM, SEM] + [ANY] * len(afters), out_specs=(HBM,) * (2 * na),
        input_output_aliases={i: i for i in range(2 * na)},
        compiler_params=pltpu.CompilerParams(has_side_effects=EFFECT),
    )(*bufs, sems[0], sems[1], *afters)
    return list(res[:na]), list(res[na:])


def _pair_exchange(parts, name):
    na = len(parts)

    def kern(*refs):
        ins, got = refs[:na], refs[na:2 * na]
        send_sems, recv_sems = refs[2 * na:]
        x, y, c = lax.axis_index("x"), lax.axis_index("y"), lax.axis_index("c")
        sends = []
        for a in range(na):
            for q in range(N_CHIP):
                sends.append(pltpu.make_async_remote_copy(
                    src_ref=ins[a].at[2 * q + 1 - c], dst_ref=got[a].at[q], send_sem=send_sems.at[a, q],
                    recv_sem=recv_sems.at[a, q], device_id=(x, y, 1 - c), device_id_type=MESH))
        for cp in sends:
            cp.start()
        for cp in sends:
            cp.wait()

    return pl.pallas_call(
        kern, name=name, in_specs=[ANY] * na, out_specs=[ANY] * na,
        out_shape=[jax.ShapeDtypeStruct((N_CHIP,) + p.shape[1:], p.dtype) for p in parts],
        scratch_shapes=[pltpu.SemaphoreType.DMA((na, N_CHIP)), pltpu.SemaphoreType.DMA((na, N_CHIP))])(*parts)


def _pair_sum(parts, got, name):
    _, r, cdim = parts.shape
    tile = min(r, ROW_TILE)
    x, y, c = lax.axis_index("x"), lax.axis_index("y"), lax.axis_index("c")
    where = jnp.stack([c, 2 * x + y]).astype(jnp.int32)

    def kern(w_ref, a_ref, b_ref, q_ref, own_ref):
        s = (a_ref[...].astype(F32) + b_ref[...].astype(F32)).astype(BF)
        q_ref[...] = s

        @pl.when(pl.program_id(1) == w_ref[1])
        def _():
            own_ref[...] = s

    blk = pl.BlockSpec((None, tile, cdim), lambda i, q, w_ref: (q, i, 0))
    sums, own = pl.pallas_call(
        kern, name=name,
        out_shape=[jax.ShapeDtypeStruct((N_CHIP, r, cdim), BF), jax.ShapeDtypeStruct((r, cdim), BF)],
        grid_spec=pltpu.PrefetchScalarGridSpec(
            num_scalar_prefetch=1, grid=(r // tile, N_CHIP),
            in_specs=[pl.BlockSpec((None, tile, cdim), lambda i, q, w_ref: (2 * q + w_ref[0], i, 0)), blk],
            out_specs=[blk, pl.BlockSpec((tile, cdim), lambda i, q, w_ref: (i, 0))]),
        compiler_params=_params("parallel", "arbitrary"))(where, parts, got)
    land = lax.dynamic_update_slice(lax.empty((N_CHIP, r, cdim), BF), own[None], (2 * x + y, 0, 0))
    return sums, land


W_IN_COLS = D_IN // N_DEV


def _w_in_from_blocks(w8):
    def kern(x_ref, o_ref):
        for j in range(N_DEV):
            o_ref[:, W_IN_COLS * j:W_IN_COLS * (j + 1)] = x_ref[j]
        o_ref[:, D_IN:] = jnp.zeros((ROW_TILE, D_INP - D_IN), o_ref.dtype)

    return pl.pallas_call(
        kern, name="w_in_from_blocks", grid=(D // ROW_TILE,),
        in_specs=[pl.BlockSpec((N_DEV, ROW_TILE, W_IN_COLS), lambda i: (0, i, 0))],
        out_specs=pl.BlockSpec((ROW_TILE, D_INP), lambda i: (i, 0)),
        out_shape=jax.ShapeDtypeStruct((D, D_INP), w8.dtype), compiler_params=_params("parallel"))(w8)


def _w_in_to_blocks(g):
    def kern(x_ref, o_ref):
        for j in range(N_DEV):
            o_ref[j] = x_ref[:, W_IN_COLS * j:W_IN_COLS * (j + 1)]

    return pl.pallas_call(
        kern, name="w_in_to_blocks", grid=(D // ROW_TILE,),
        in_specs=[pl.BlockSpec((ROW_TILE, D_INP), lambda i: (i, 0))],
        out_specs=pl.BlockSpec((N_DEV, ROW_TILE, W_IN_COLS), lambda i: (0, i, 0)),
        out_shape=jax.ShapeDtypeStruct((N_DEV, D, W_IN_COLS), g.dtype), compiler_params=_params("parallel"))(g)


def _pad_lanes(v, width=LANES):
    return jnp.pad(v, ((0, 0), (0, width - v.shape[1])))


def _row_layout(t16):
    return t16.T.reshape(N_HEADS // 2, 2, NB, BLK).transpose(0, 2, 1, 3)


def _row_layout_inv(r):
    return r.transpose(0, 2, 1, 3).reshape(N_HEADS, S).T


SMALL = (("g_mix", D), ("g_q", HEAD), ("g_k", HEAD), ("g_attn_out", D_ATTN), ("conv_b", D_CONV),
         ("dt_bias", 16), ("a_log", 16), ("d_skip", 16), ("g_ssm_out", D_SSM), ("g_cross", D),
         ("g_mem", D), ("g_cq", CROSS_HEAD), ("g_ck", CROSS_HEAD), ("g_mlp", D))
SMALL_ROWS = -(-sum(-(-w // LANES) for _, w in SMALL) // 8) * 8


def _pack_small(vals):
    rows = [_pad_lanes(vals[n], -(-w // LANES) * LANES).reshape(-1, LANES) for n, w in SMALL]
    packed = jnp.concatenate(rows, axis=0)
    return jnp.pad(packed, ((0, SMALL_ROWS - packed.shape[0]), (0, 0)))


def _unpack_small(packed):
    out, r = {}, 0
    for n, w in SMALL:
        nr = -(-w // LANES)
        out[n] = packed[r:r + nr].reshape(1, nr * LANES)[:, :w]
        r += nr
    return out


BIG = ("w_in", "w_out", "w_cq", "w_ckv", "w_co", "w_up", "w_down")
WEIGHTS = ("g_mix", "w_in", "g_q", "g_k", "g_attn_out", "conv_w", "conv_b", "dt_bias", "a_log", "d_skip",
           "g_ssm_out", "w_out", "g_cross", "g_mem", "w_cq", "w_ckv", "g_cq", "g_ck", "w_co", "g_mlp", "w_up", "w_down")


REST = ("w_out", "w_cq", "w_ckv", "w_co", "w_up", "w_down")


class _Overlap:
    def __init__(self, first):
        self.first_names = list(first)
        srcs = [first[n] for n in self.first_names]
        lands = [_put_own(lax.empty((N_DEV,) + s.shape, s.dtype), s) for s in srcs]
        self.direct = _exchange_start("direct", srcs, lands, None, "ag_first_start")
        self.token = self.direct[3]
        self.shards, self.behind, self.late = {}, [], []
        self.gather, self.forward = {}, {}
        self.names = {"a": REST[:4], "b": REST[4:5], "c": REST[5:]}
        self.groups = []
        self.pairs = {}

    def first(self, after):
        sems, srcs, lands, _ = self.direct
        srcs, lands = _exchange_wait("direct", sems, srcs, lands,
                                     [after] + list(self.shards.values()) + list(self.behind), "ag_first_wait")
        sems, srcs, lands, token = _exchange_start("relay", srcs, lands, None, "ag_relay_start")
        self.late = [_tie(t, token) for t in self.late]
        srcs, lands = _exchange_wait("relay", sems, srcs, lands, [token] + self.late, "ag_relay_wait")
        sems, srcs, lands, token = _exchange_start("diag", srcs, lands, None, "ag_diag_start")
        for tag, names in self.names.items():
            rest_lands = [_put_own(lax.empty((N_DEV,) + self.shards[n].shape, BF), self.shards[n]) for n in names]
            self.gather[tag] = _exchange_start("gather", [self.shards[n] for n in names], rest_lands, token,
                                               "ag_start_" + tag)
            token = self.gather[tag][3]
        _, lands = _exchange_wait("diag", sems, srcs, lands, token, "ag_diag_wait")
        return dict(zip(self.first_names, lands))

    def rest_mid(self, tag, after):
        sems, srcs, lands, _ = self.gather[tag]
        srcs, lands = _exchange_wait("gather", sems, srcs, lands, after, "ag_wait_" + tag)
        self.forward[tag] = _exchange_start("forward", srcs, lands, None, "ag_fwd_start_" + tag)
        return self.forward[tag][3]

    def rest(self, tag, after):
        sems, srcs, lands, _ = self.forward[tag]
        _, lands = _exchange_wait("forward", sems, srcs, lands, after, "ag_fwd_wait_" + tag)
        wf = dict(zip(self.names[tag], lands))
        for n in wf:
            if n in ("w_out", "w_cq", "w_ckv", "w_down"):
                wf[n] = wf[n].reshape(-1, wf[n].shape[-1])
        return wf

    def pair_start(self, name, grad):
        got = lax.empty((N_CHIP,) + grad.shape[1:], grad.dtype)
        self.pairs[name] = _exchange_start("pair", [grad], [got], None, "rs_pair_start_" + name)
        return self.pairs[name][3]

    def emit(self, grads, after):
        names, tag = list(grads), "_%d" % len(self.groups)
        grads, got = dict(grads), {}
        for n in names:
            if n in self.pairs:
                sems, srcs, lands, _ = self.pairs[n]
                srcs, lands = _exchange_wait("pair", sems, srcs, lands, after, "rs_pair_wait_" + n)
                grads[n], got[n] = srcs[0], lands[0]
        sync = [n for n in names if n not in got]
        if sync:
            got.update(zip(sync, _pair_exchange([grads[n] for n in sync], "rs_pair" + tag)))
        sums = [_pair_sum(grads[n], got[n], "rs_sum_" + n) for n in names]
        sems, srcs, lands, token = _exchange_start("scatter", [q for q, _ in sums], [l for _, l in sums], None,
                                                   "rs_chip_start" + tag)
        self.groups.append((names, sems, srcs, lands))
        return token

    def finish(self, gi, after):
        names, sems, srcs, lands = self.groups[gi]
        _, lands = _exchange_wait("scatter", sems, srcs, lands, after, "rs_chip_wait_%d" % gi)
        return dict(zip(names, lands))


def _tie(v, token):
    return v if token is None else v + token[0:1, 0:1]


def _local_step(x, mem, pos_col, target, p, wf, hooks=None):
    p = dict(p)
    inv = np.zeros((1, LANES), np.float32)
    freq = (ROPE_THETA ** (-2.0 * np.arange(ROT // 2, dtype=np.float32) / ROT)).astype(np.float32)
    for l in range(LANES):
        if l % HEAD < ROT:
            inv[0, l] = freq[(l % HEAD) % (ROT // 2)]
    inv_tab = jnp.asarray(inv)
    gq128, gk128 = jnp.tile(p["g_q"], (1, 2)), jnp.tile(p["g_k"], (1, 2))
    dtb128, alog128 = _pad_lanes(p["dt_bias"]), _pad_lanes(p["a_log"])
    dskip_b = jnp.repeat(p["d_skip"], HEAD, axis=1)

    h = _rms_fwd(x, p["g_mix"], "rms_mix")
    if hooks is not None:
        got = hooks.first(h)
        wf = {**wf, "w_in": _w_in_from_blocks(got["w_in"]),
              "conv_w": got["conv_w"].transpose(1, 0, 2).reshape(CONV_W, D_CONV)}
    conv_w, conv_b = wf["conv_w"], p["conv_b"]
    proj = _matmul(h, wf["w_in"], mode="nn", name="mm_in", tm=1024, tn=1280, tk=D)
    qr, kr = _qk_prep(proj, pos_col, gq128, gk128, inv_tab)
    attn, lse = _attn_fwd(qr, kr, proj)
    attn_n = _attn_norm(attn, p["g_attn_out"])

    xbc = _conv_fwd(proj, conv_w, conv_b)
    token = None if hooks is None else hooks.rest_mid("a", xbc)
    dt, a_cs, dt_b, a_b = _ssd_prep(proj, _tie(dtb128, token), alog128)
    at_r, dt_r = _row_layout(a_cs[:, :N_HEADS]), _row_layout(dt[:, :N_HEADS])
    y_ssd, h_all = _ssd_fwd(xbc, a_b, dt_b, at_r, dt_r)
    ssm_n = _ssd_post(y_ssd, xbc, proj, dskip_b, p["g_ssm_out"])

    if hooks is not None:
        wf = {**wf, **hooks.rest("a", ssm_n)}
    mix = jnp.concatenate([attn_n, ssm_n], axis=1)
    x1 = _matmul(mix, wf["w_out"], mode="nn", name="mm_out", tm=1024, tn=1024, tk=D,
                 extras=(x,), epilogue=lambda acc, r: (acc + r,))
    hc = _rms_fwd(x1, _tie(p["g_cross"], None if hooks is None else hooks.rest_mid("b", x1)), "rms_cross")
    memh = _rms_fwd(mem, p["g_mem"], "rms_mem")
    qc = _matmul(hc, wf["w_cq"], mode="nn", name="mm_cq", tm=1024, tn=512, tk=D)
    kv = _matmul(memh, wf["w_ckv"], mode="nn", name="mm_ckv", tm=N_MEM, tn=1024, tk=D)
    oc = _cross_fwd(qc, kv, p["g_cq"], p["g_ck"])
    x2 = _matmul(oc, wf["w_co"], mode="nn", name="mm_co", tm=1024, tn=256, tk=512, b_cb=256,
                 extras=(x1,), epilogue=lambda acc, r: (acc + r,))
    hm = _rms_fwd(x2, p["g_mlp"], "rms_mlp")
    if hooks is not None:
        wf = {**wf, **hooks.rest("b", hm)}

    def up_epi(acc):
        ru = jnp.maximum(acc, 0.0)
        return ru * ru, 2.0 * ru

    act, relu2 = _matmul(hm, wf["w_up"], mode="nn", name="mm_up", tm=1024, tn=1024, tk=D, b_cb=1024,
                         out_dtypes=(BF, BF), epilogue=up_epi)
    if hooks is not None:
        hooks.rest_mid("c", act)
        wf = {**wf, **hooks.rest("c", relu2)}

    def loss_epi(acc, r, t):
        d = (acc + r - t) * (1.0 / D)
        return d, d

    dy, dyb = _matmul(act, wf["w_down"], mode="nn", name="mm_down", tm=512, tn=1024, tk=2048, extras=(x2, target),
                      out_dtypes=(F32, BF), epilogue=loss_epi)

    gb, gs = {}, {}
    gb["w_down"] = _matmul(act, dyb, mode="tn", name="mm_down_dw", tm=1024, tn=1024, tk=S,
                           out_dtypes=(BF,)).reshape(N_DEV, D_FF // N_DEV, D)
    token = None if hooks is None else hooks.pair_start("w_down", gb["w_down"])
    du = _matmul(dyb, wf["w_down"], mode="nt", name="mm_down_dx", tm=1024, tn=1024, tk=D, extras=(relu2,),
                 out_dtypes=(BF,), epilogue=lambda acc, r: (acc * r.astype(F32),), after=token)
    gb["w_up"] = _matmul(hm, du, mode="tn", name="mm_up_dw", tm=1024, tn=1024, tk=S, out_dtypes=(BF,), out_cb=1024)
    token = None if hooks is None else hooks.pair_start("w_up", gb["w_up"])
    dhm = _matmul(du, wf["w_up"], mode="nt", name="mm_up_dx", tm=1024, tn=1024, tk=2048, b_cb=1024, after=token)
    if hooks is not None:
        p["g_mlp"] = _tie(p["g_mlp"], hooks.emit({n: gb[n] for n in ("w_down", "w_up")}, dhm))
    dx2, dx2b, gs["g_mlp"], sumsq_dy = _rms_bwd_call(x2, p["g_mlp"], dhm, dy, "rms_mlp_bwd", sumsq_res=True)
    loss_part = sumsq_dy[0, 0] * (0.5 * D)

    doc = _matmul(dx2b, wf["w_co"], mode="nt", name="mm_co_dx", tm=1024, tn=512, tk=256, b_cb=256)
    gb["w_co"] = _matmul(oc, dx2b, mode="tn", name="mm_co_dw", tm=512, tn=256, tk=S, out_dtypes=(BF,), out_cb=256)
    dqc, dkn, dvc, gs["g_cq"] = _cross_bwd(qc, doc, kv, p["g_cq"], p["g_ck"])
    dkv, gs["g_ck"] = _cross_kv_bwd(kv, dkn, dvc, p["g_ck"])
    gb["w_cq"] = _matmul(hc, dqc, mode="tn", name="mm_cq_dw", tm=1024, tn=512, tk=S,
                         out_dtypes=(BF,)).reshape(N_DEV, D // N_DEV, D_CROSS)
    dhc = _matmul(dqc, wf["w_cq"], mode="nt", name="mm_cq_dx", tm=1024, tn=1024, tk=512)
    gb["w_ckv"] = _matmul(memh, dkv, mode="tn", name="mm_ckv_dw", tm=1024, tn=1024, tk=N_MEM,
                          out_dtypes=(BF,)).reshape(N_DEV, D // N_DEV, 2 * D_CROSS)
    dmemh = _matmul(dkv, wf["w_ckv"], mode="nt", name="mm_ckv_dx", tm=N_MEM, tn=1024, tk=1024)
    (gs["g_mem"],) = _rms_bwd_call(mem, p["g_mem"], dmemh, None, "rms_mem_bwd")
    dx1, dx1b, gs["g_cross"] = _rms_bwd_call(x1, p["g_cross"], dhc, dx2, "rms_cross_bwd")

    dmix = _matmul(dx1b, wf["w_out"], mode="nt", name="mm_out_dx", tm=1024, tn=1024, tk=D)
    gb["w_out"] = _matmul(mix, dx1b, mode="tn", name="mm_out_dw", tm=1024, tn=1024, tk=S,
                          out_dtypes=(BF,)).reshape(N_DEV, D // N_DEV, D)
    if hooks is not None:
        p["g_attn_out"] = _tie(p["g_attn_out"],
                               hooks.emit({n: gb[n] for n in ("w_co", "w_cq", "w_ckv", "w_out")}, dmix))

    dattn, delta, gs["g_attn_out"] = _attn_merge_bwd(dmix, attn, p["g_attn_out"])
    dq_rot, dk_rot, dv_attn = _attn_bwd(qr, kr, proj, dattn, lse, delta)
    dq_pre, dk_pre, dv_pre, dgq, dgk = _qk_bwd(dq_rot, dk_rot, dv_attn, proj, pos_col, gq128, gk128, inv_tab)
    gs["g_q"], gs["g_k"] = dgq[:, :HEAD], dgk[:, :HEAD]

    dy_ssd, dz, dxs_skip, dd_lane, gs["g_ssm_out"] = _ssd_post_bwd(y_ssd, xbc, proj, dmix, dskip_b, p["g_ssm_out"])
    gs["d_skip"] = dd_lane.reshape(N_HEADS, HEAD).sum(axis=1).reshape(1, N_HEADS)
    dc_pair, daq_b, dx_ssd, db_pair, dak_r, ddt_r, ddt_b = _ssd_bwd(xbc, dy_ssd, h_all, a_b, dt_b, at_r, dt_r)
    dak = _pad_lanes(_row_layout_inv(dak_r))
    ddt_direct = _pad_lanes(_row_layout_inv(ddt_r))
    ddt_raw, dalog, dbias = _ssd_prep_bwd(daq_b, dak, ddt_direct, ddt_b, dt, proj, dtb128, alog128)
    gs["a_log"], gs["dt_bias"] = dalog[:, :N_HEADS], dbias[:, :N_HEADS]
    same = lambda c: c
    dxbc_x, dcw_x, dcb_x = _conv_bwd(proj, conv_w, conv_b, dxs_skip, dx_ssd, same, same, 0, 8, "conv_bwd_x")
    dxbc_b, dcw_b, dcb_b = _conv_bwd(proj, conv_w, conv_b, db_pair, db_pair, lambda c: 2 * c, lambda c: 2 * c + 1,
                                     D_SSM, 4, "conv_bwd_b")
    dxbc_c, dcw_c, dcb_c = _conv_bwd(proj, conv_w, conv_b, dc_pair, dc_pair, lambda c: 2 * c, lambda c: 2 * c + 1,
                                     D_SSM + 512, 4, "conv_bwd_c")
    g_conv_w = jnp.concatenate([dcw_x, dcw_b, dcw_c], axis=1)
    gs["conv_b"] = jnp.concatenate([dcb_x, dcb_b, dcb_c], axis=1)

    pieces = [dq_pre, dk_pre, dv_pre, dz, dxbc_x, dxbc_b, dxbc_c, ddt_raw]
    pieces.append(jnp.zeros((S, D_INP - sum(t.shape[1] for t in pieces)), BF))
    dproj = jnp.concatenate(pieces, axis=1)
    dw_in = _matmul(h, dproj, mode="tn", name="mm_in_dw", tm=1024, tn=1280, tk=S, out_dtypes=(BF,))
    gb["w_in"] = _w_in_to_blocks(dw_in)
    token = None if hooks is None else hooks.emit({"w_in": gb["w_in"]}, dw_in)
    dh = _matmul(dproj, wf["w_in"], mode="nt", name="mm_in_dx", tm=1024, tn=512, tk=D_INP // 2, after=token)
    grad_x, _, gs["g_mix"] = _rms_bwd_call(x, p["g_mix"], dh, dx1, "rms_mix_bwd")
    return loss_part, grad_x, gb, gs, g_conv_w


def kernel(x, mem, positions, g_mix, w_in, g_q, g_k, g_attn_out, conv_w, conv_b, dt_bias, a_log, d_skip, g_ssm_out, w_out, g_cross, g_mem, w_cq, w_ckv, g_cq, g_ck, w_co, g_mlp, w_up, w_down, loss_target, m_g_mix, m_w_in, m_g_q, m_g_k, m_g_attn_out, m_conv_w, m_conv_b, m_dt_bias, m_a_log, m_d_skip, m_g_ssm_out, m_w_out, m_g_cross, m_g_mem, m_w_cq, m_w_ckv, m_g_cq, m_g_ck, m_w_co, m_g_mlp, m_w_up, m_w_down, v_g_mix, v_w_in, v_g_q, v_g_k, v_g_attn_out, v_conv_w, v_conv_b, v_dt_bias, v_a_log, v_d_skip, v_g_ssm_out, v_w_out, v_g_cross, v_g_mem, v_w_cq, v_w_ckv, v_g_cq, v_g_ck, v_w_co, v_g_mlp, v_w_up, v_w_down):
    args = dict(locals())
    w = {n: args[n] for n in WEIGHTS}
    m = {n: args["m_" + n] for n in WEIGHTS}
    v = {n: args["v_" + n] for n in WEIGHTS}
    small = {n: w[n] for n, _ in SMALL}
    me = 4 * lax.axis_index("x") + 2 * lax.axis_index("y") + lax.axis_index("c")

    overlap = _Overlap({"w_in": w["w_in"][0].astype(BF), "conv_w": w["conv_w"][0]})
    tok = overlap.token
    overlap.shards = {n: _tie(w[n][0], tok).astype(BF) for n in REST}
    w_in_tied, v_in_tied = _tie(w["w_in"][0], tok), _tie(v["w_in"][0], tok)
    overlap.behind = [w_in_tied, v_in_tied]
    overlap.late = [m["w_in"][0]]
    p = dict(small)
    p["g_mix"] = _tie(p["g_mix"], tok)
    loss_part, grad_x, _, gs, g_conv_w = _local_step(
        x[0], mem[0], positions.reshape(S, 1), loss_target[0], p, {}, overlap)
    adam_in = {"w_in": (w_in_tied, overlap.late[0], v_in_tied)}
    loss = lax.psum(loss_part, ("x", "y", "c"))

    out = {}
    last = grad_x
    for gi in range(3):
        for n, parts in overlap.finish(gi, last).items():
            w_n, m_n, v_n = adam_in.get(n, (w[n][0], m[n][0], v[n][0]))
            res_n = _adamw(w_n, m_n, v_n, parts, "adamw_" + n)
            out[n] = [t.reshape(w[n].shape) for t in res_n]
            last = res_n[0]

    sm_parts, cw_parts = _all_gather([_pack_small(gs), g_conv_w], "ag_small_grads", after=last)
    sm = [_unpack_small(t) for t in _adamw(_pack_small(small), _pack_small({n: m[n] for n, _ in SMALL}),
                                           _pack_small({n: v[n] for n, _ in SMALL}), sm_parts, "adamw_small")]
    for n, _ in SMALL:
        out[n] = [t[n] for t in sm]
    cols = D_CONV // N_DEV
    cw_mine = lax.dynamic_slice_in_dim(cw_parts, me * cols, cols, axis=2)
    out["conv_w"] = [t.reshape(w["conv_w"].shape) for t in
                     _adamw(w["conv_w"][0], m["conv_w"][0], v["conv_w"][0], cw_mine, "adamw_conv_w")]

    res = [loss, grad_x.reshape(x.shape)]
    for k in range(4):
        res += [out[n][k] for n in WEIGHTS]
    return tuple(res)
```

```python
import functools
import math

import numpy as np
import jax
import jax.numpy as jnp
from jax import lax
from jax.experimental import pallas as pl
from jax.experimental.pallas import tpu as pltpu

F32 = jnp.float32
BF = jnp.bfloat16

N_DEV = 8
S = 2048
D = 2048
D_ATTN = 1024
N_HEADS = 16
HEAD = 64
ROT = 16
ROPE_THETA = 500000.0
D_SSM = 1024
D_CONV = 2048
CONV_W = 4
N_MEM = 256
D_CROSS = 512
CROSS_HEAD = 128
D_FF = 8192
D_IN = 6160
D_INP = 6400
DT_COLBLK = (4 * D_ATTN + D_CONV) // 128
EPS = 1e-6
BLK = 128
NB = S // BLK
LANES = 128
NEG = -1e30

ADAM_LR = 0.001
ADAM_B1 = 0.9
ADAM_B2 = 0.999
ADAM_EPS = 1e-08
ADAM_WD = 0.01
ADAM_STEP = 10

VMEM_LIMIT_BYTES = 48 * 1024 * 1024
ROW_TILE = 256


def _params(*sem):
    return pltpu.CompilerParams(dimension_semantics=sem, vmem_limit_bytes=VMEM_LIMIT_BYTES)


def _matmul(a, b, *, mode, name, tm, tn, tk, out_dtypes=(F32,), b_cb=None, out_cb=None,
            extras=(), epilogue=None, after=None):
    if mode == "tn":
        kk, m = a.shape
    else:
        m, kk = a.shape
    if b_cb is None:
        br, bc = b.shape
    else:
        br, bc = b.shape[1], N_DEV * b_cb
    n = br if mode == "nt" else bc
    assert m % tm == 0 and n % tn == 0 and kk % tk == 0, (name, m, n, kk)
    nk = kk // tk
    grid = (m // tm, n // tn, nk)

    if mode == "tn":
        a_spec = pl.BlockSpec((tk, tm), lambda i, j, k: (k, i))
    else:
        a_spec = pl.BlockSpec((tm, tk), lambda i, j, k: (i, k))
    if mode == "nt":
        b_blk, b_idx = (tn, tk), (lambda i, j, k: (j, k))
    else:
        b_blk, b_idx = (tk, tn), (lambda i, j, k: (k, j))
    group = 1
    if b_cb is None:
        b_spec = pl.BlockSpec(b_blk, b_idx)
    elif mode == "nt" and tk > b_cb:
        assert tk % b_cb == 0
        group = tk // b_cb
        b_spec = pl.BlockSpec((group, tn, b_cb), lambda i, j, k: (k, j, 0))
    else:
        tc = b_blk[1]
        assert b_cb % tc == 0
        per = b_cb // tc

        def b_idx3(i, j, k):
            r, c = b_idx(i, j, k)
            return (c // per, r, c % per)
        b_spec = pl.BlockSpec((None,) + b_blk, b_idx3)
    ex_specs = [pl.BlockSpec((tm, tn), lambda i, j, k: (i, j)) for _ in extras]
    if out_cb is None:
        out_shape = [jax.ShapeDtypeStruct((m, n), dt) for dt in out_dtypes]
        out_specs = [pl.BlockSpec((tm, tn), lambda i, j, k: (i, j)) for _ in out_dtypes]
    else:
        assert out_cb % tn == 0
        pero = out_cb // tn
        out_shape = [jax.ShapeDtypeStruct((N_DEV, m, out_cb), dt) for dt in out_dtypes]
        out_specs = [pl.BlockSpec((None, tm, tn), lambda i, j, k: (j // pero, i, j % pero)) for _ in out_dtypes]
    dn = {"nn": (((1,), (0,)), ((), ())), "nt": (((1,), (1,)), ((), ())), "tn": (((0,), (0,)), ((), ()))}[mode]
    ne, no = len(extras), len(out_dtypes)
    n_after = 0 if after is None else 1

    def kern(a_ref, b_ref, *rest):
        ex_refs, out_refs = rest[:ne], rest[ne + n_after:ne + n_after + no]

        def finish(acc):
            outs = (acc,) if epilogue is None else epilogue(acc, *[r[...] for r in ex_refs])
            for r, o in zip(out_refs, outs):
                r[...] = o.astype(r.dtype)

        if group == 1:
            part = lax.dot_general(a_ref[...].astype(BF), b_ref[...].astype(BF), dn, preferred_element_type=F32)
        else:
            part = sum(lax.dot_general(a_ref[:, g * b_cb:(g + 1) * b_cb].astype(BF), b_ref[g].astype(BF), dn,
                                       preferred_element_type=F32) for g in range(group))
        if nk == 1:
            finish(part)
            return
        acc_ref = rest[ne + n_after + no]
        k = pl.program_id(2)

        @pl.when(k == 0)
        def _():
            acc_ref[...] = part

        @pl.when(k > 0)
        def _():
            acc_ref[...] += part

        @pl.when(k == nk - 1)
        def _():
            finish(acc_ref[...])

    res = pl.pallas_call(
        kern, name=name, grid=grid, in_specs=[a_spec, b_spec] + ex_specs + [ANY] * n_after, out_specs=out_specs,
        out_shape=out_shape, scratch_shapes=[pltpu.VMEM((tm, tn), F32)] if nk > 1 else [],
        compiler_params=_params("parallel", "parallel", "arbitrary"),
    )(a, b, *extras, *([] if after is None else [after]))
    return res[0] if no == 1 else tuple(res)


def _rowwise(body, *, name, nrows, tile, row_ins, full_ins=(), row_outs=(), acc_outs=(), scratch=(),
             reverse=False):
    n = nrows // tile

    def ridx(i):
        return (n - 1 - i) if reverse else i

    in_specs, args = [], []
    for arr, width, cb in row_ins:
        in_specs.append(pl.BlockSpec((tile, width), lambda i, cb=cb: (ridx(i), cb)))
        args.append(arr)
    for arr in full_ins:
        in_specs.append(pl.BlockSpec(arr.shape, lambda i, nd=arr.ndim: (0,) * nd))
        args.append(arr)
    out_shape, out_specs = [], []
    for width, dt in row_outs:
        out_shape.append(jax.ShapeDtypeStruct((nrows, width), dt))
        out_specs.append(pl.BlockSpec((tile, width), lambda i: (ridx(i), 0)))
    for shp, dt in acc_outs:
        out_shape.append(jax.ShapeDtypeStruct(shp, dt))
        out_specs.append(pl.BlockSpec(shp, lambda i, nd=len(shp): (0,) * nd))

    def kern(*refs):
        body(pl.program_id(0), n, *refs)

    return pl.pallas_call(kern, name=name, grid=(n,), in_specs=in_specs, out_specs=out_specs,
                          out_shape=out_shape, scratch_shapes=list(scratch),
                          compiler_params=_params("arbitrary"))(*args)


def _accum(ref, val, i):
    @pl.when(i == 0)
    def _():
        ref[...] = val

    @pl.when(i > 0)
    def _():
        ref[...] += val


def _sigmoid(x):
    return 1.0 / (1.0 + jnp.exp(-x))


def _rms_n(x):
    r = lax.rsqrt(jnp.mean(x * x, axis=-1, keepdims=True) + EPS)
    return x * r, r


def _rms_bwd(x, g, dh):
    n, r = _rms_n(x)
    dn = dh * g
    dx = r * (dn - n * jnp.mean(dn * n, axis=-1, keepdims=True))
    return dx, jnp.sum(dh * n, axis=0, keepdims=True)


def _seg_sum(x, seg):
    t, w = x.shape
    tiles = [x[:, LANES * j:LANES * (j + 1)] for j in range(w // LANES)]
    outs = []
    if seg == 64:
        lo = lax.broadcasted_iota(jnp.int32, (t, LANES), 1) < 64
        for xt in tiles:
            s_lo = jnp.sum(jnp.where(lo, xt, 0.0), axis=-1, keepdims=True)
            s_hi = jnp.sum(jnp.where(lo, 0.0, xt), axis=-1, keepdims=True)
            outs.append(jnp.where(lo, s_lo, s_hi))
    else:
        sums = [jnp.sum(xt, axis=-1, keepdims=True) for xt in tiles]
        if seg == 256:
            sums = [sums[2 * (j // 2)] + sums[2 * (j // 2) + 1] for j in range(len(sums))]
        else:
            assert seg == 128
        outs = [jnp.broadcast_to(s, (t, LANES)) for s in sums]
    return outs[0] if len(outs) == 1 else jnp.concatenate(outs, axis=1)


def _seg_rms_n(x, seg):
    r = lax.rsqrt(_seg_sum(x * x, seg) * (1.0 / seg) + EPS)
    return x * r, r


def _seg_rms_bwd(x, g, dy, seg):
    n, r = _seg_rms_n(x, seg)
    dn = dy * g
    dx = r * (dn - n * (_seg_sum(dn * n, seg) * (1.0 / seg)))
    return dx, jnp.sum(dy * n, axis=0, keepdims=True)


def _rope_tables(pos_col, inv_tab, t):
    ang = pos_col.astype(F32) * inv_tab
    idx = lax.broadcasted_iota(jnp.int32, (t, LANES), 1) % HEAD
    cos, sin = jnp.cos(ang), jnp.sin(ang)
    c = jnp.where(idx < ROT, cos, 1.0)
    sg = jnp.where(idx < ROT // 2, -sin, jnp.where(idx < ROT, sin, 0.0))
    return c, sg, idx < ROT // 2


def _rope(x, c, sg, lo):
    partner = jnp.where(lo, pltpu.roll(x, LANES - ROT // 2, 1), pltpu.roll(x, ROT // 2, 1))
    return x * c + partner * sg


def _fold_heads(v):
    v8 = jnp.broadcast_to(v, (8, LANES))
    return (v8 + pltpu.roll(v8, HEAD, 1))[0:1]


def _dot(a, b, dn):
    return lax.dot_general(a, b, (dn, ((), ())), preferred_element_type=F32)


NN = ((1,), (0,))
NT = ((1,), (1,))
TN = ((0,), (0,))


def _dot3(l01, x):
    x1 = x.astype(BF)
    r1 = x - x1.astype(F32)
    x2 = r1.astype(BF)
    x3 = (r1 - x2.astype(F32)).astype(BF)
    return _dot(l01, x1, NN) + _dot(l01, x2, NN) + _dot(l01, x3, NN)


def _rms_fwd(x, g, name):
    rows = x.shape[0]

    def body(i, n, x_ref, g_ref, o_ref):
        nx, _ = _rms_n(x_ref[...])
        o_ref[...] = (nx * g_ref[...]).astype(BF)

    return _rowwise(body, name=name, nrows=rows, tile=min(ROW_TILE, rows), row_ins=[(x, D, 0)],
                    full_ins=[g], row_outs=[(D, BF)])[0]


def _qk_prep(proj, pos_col, gq128, gk128, inv_tab):
    scale = HEAD ** -0.5

    def body(i, n, q_ref, k_ref, p_ref, gq_ref, gk_ref, it_ref, qo_ref, ko_ref):
        t = q_ref.shape[0]
        c, sg, lo = _rope_tables(p_ref[...], it_ref[...], t)
        for j in range(D_ATTN // LANES):
            sl = slice(LANES * j, LANES * (j + 1))
            qn, _ = _seg_rms_n(q_ref[:, sl], HEAD)
            kn, _ = _seg_rms_n(k_ref[:, sl], HEAD)
            qo_ref[:, sl] = _rope(qn * gq_ref[...], c, sg, lo) * scale
            ko_ref[:, sl] = _rope(kn * gk_ref[...], c, sg, lo)

    return _rowwise(body, name="qk_prep", nrows=S, tile=ROW_TILE,
                    row_ins=[(proj, D_ATTN, 0), (proj, D_ATTN, 1), (pos_col, 1, 0)],
                    full_ins=[gq128, gk128, inv_tab], row_outs=[(D_ATTN, F32)] * 2)


ATTN_QB = 16
V_COLS = pl.BlockSpec((S, LANES), lambda p: (0, 2 * D_ATTN // LANES + p))


def _band(b, d):
    nb = NB // d
    r, n = b // nb, b % nb
    width, kn = (BLK, n) if nb == 1 else (2 * BLK, jnp.maximum(n - 1, 0))

    def rows(first_member, count):
        if d == 1:
            return pl.ds(pl.multiple_of(first_member, BLK), count)
        return pl.ds(first_member * d + r, count, stride=d)

    qm = n * BLK + (lax.broadcasted_iota(jnp.int32, (2 * BLK, width), 0) & (BLK - 1))
    km = kn * BLK + lax.broadcasted_iota(jnp.int32, (2 * BLK, width), 1)
    valid = km <= qm
    if nb > 1:
        valid = valid & (km >= qm - BLK)
    return rows(n * BLK, BLK), rows(kn * BLK, width), valid


DILATIONS = (1, 4, 16)


def _attn_fwd(q, k, v):
    def kern(q_ref, k_ref, v_ref, a_ref, lse_ref, *scr):
        half0 = lax.broadcasted_iota(jnp.int32, (BLK, LANES), 1) < HEAD
        for gi, d in enumerate(DILATIONS):
            o_ref, l_ref = scr[2 * gi], scr[2 * gi + 1]

            def group(g, carry, d=d, o_ref=o_ref, l_ref=l_ref):
                for bb in range(ATTN_QB):
                    q_rows, k_rows, valid = _band(g * ATTN_QB + bb, d)
                    q = q_ref[q_rows, :]
                    kb, vb = k_ref[k_rows, :].astype(BF), v_ref[k_rows, :].astype(BF)
                    q2 = jnp.concatenate([jnp.where(half0, q, 0.0), jnp.where(half0, 0.0, q)], axis=0).astype(BF)
                    s = jnp.where(valid, _dot(q2, kb, NT), NEG)
                    m = jnp.max(s, axis=-1, keepdims=True)
                    p = jnp.exp(s - m)
                    den = jnp.sum(p, axis=-1, keepdims=True)
                    o2 = _dot(p.astype(BF), vb, NN) / den
                    l2 = m + jnp.log(den)
                    o_ref[q_rows, :] = jnp.where(half0, o2[:BLK], o2[BLK:])
                    l_ref[q_rows, :] = jnp.where(half0, l2[:BLK], l2[BLK:])
                return carry

            lax.fori_loop(0, NB // ATTN_QB, group, 0)

        def merge(i, carry):
            rows = pl.ds(pl.multiple_of(i * ROW_TILE, ROW_TILE), ROW_TILE)
            la, lb, lc = scr[1][rows, :], scr[3][rows, :], scr[5][rows, :]
            m = jnp.maximum(jnp.maximum(la, lb), lc)
            ea, eb, ec = jnp.exp(la - m), jnp.exp(lb - m), jnp.exp(lc - m)
            den = ea + eb + ec
            a_ref[rows, :] = (ea * scr[0][rows, :] + eb * scr[2][rows, :] + ec * scr[4][rows, :]) / den
            lse_ref[rows, :] = m + jnp.log(den)
            return carry

        lax.fori_loop(0, S // ROW_TILE, merge, 0)

    seq = pl.BlockSpec((S, LANES), lambda p: (0, p))
    return pl.pallas_call(
        kern, name="attn_fwd", grid=(D_ATTN // LANES,), in_specs=[seq, seq, V_COLS], out_specs=[seq, seq],
        out_shape=[jax.ShapeDtypeStruct((S, D_ATTN), F32)] * 2,
        scratch_shapes=[pltpu.VMEM((S, LANES), F32)] * (2 * len(DILATIONS)),
        compiler_params=_params("parallel"))(q, k, v)


def _attn_norm(attn, g_attn):
    def body(i, n, a_ref, g_ref, an_ref):
        nx, _ = _rms_n(a_ref[...])
        an_ref[...] = (nx * g_ref[...]).astype(BF)

    return _rowwise(body, name="attn_norm", nrows=S, tile=ROW_TILE, row_ins=[(attn, D_ATTN, 0)],
                    full_ins=[g_attn], row_outs=[(D_ATTN, BF)])[0]


def _conv_taps(x, w_ref):
    rows = lax.broadcasted_iota(jnp.int32, x.shape, 0)
    shifted = []
    for w in range(CONV_W):
        k = CONV_W - 1 - w
        shifted.append(x if k == 0 else jnp.where(rows >= k, pltpu.roll(x, k, 0), 0.0))
    acc = shifted[0] * w_ref[0:1, :]
    for w in range(1, CONV_W):
        acc = acc + shifted[w] * w_ref[w:w + 1, :]
    return acc, shifted


def _conv_fwd(proj, conv_w, conv_b):
    tc = 256

    def kern(x_ref, w_ref, b_ref, o_ref):
        c, _ = _conv_taps(x_ref[...], w_ref)
        c = c + b_ref[...]
        o_ref[...] = c * _sigmoid(c)

    return pl.pallas_call(
        kern, name="conv_fwd", grid=(D_CONV // tc,),
        in_specs=[pl.BlockSpec((S, tc), lambda c: (0, 4 * D_ATTN // tc + c)),
                  pl.BlockSpec((CONV_W, tc), lambda c: (0, c)), pl.BlockSpec((1, tc), lambda c: (0, c))],
        out_specs=pl.BlockSpec((S, tc), lambda c: (0, c)),
        out_shape=jax.ShapeDtypeStruct((S, D_CONV), F32), compiler_params=_params("parallel"))(proj, conv_w, conv_b)


def _softplus(x):
    return jnp.maximum(x, 0.0) + jnp.log1p(jnp.exp(-jnp.abs(x)))


def _dot3r(x, r01):
    x1 = x.astype(BF)
    r1 = x - x1.astype(F32)
    x2 = r1.astype(BF)
    x3 = (r1 - x2.astype(F32)).astype(BF)
    return _dot(x1, r01, NN) + _dot(x2, r01, NN) + _dot(x3, r01, NN)


def _spread_heads(v):
    sel = (lax.broadcasted_iota(jnp.int32, (LANES, D_SSM), 1) // HEAD
           == lax.broadcasted_iota(jnp.int32, (LANES, D_SSM), 0))
    return _dot3r(v, sel.astype(BF))


def _collect_heads(v):
    sel = (lax.broadcasted_iota(jnp.int32, (D_SSM, LANES), 0)
           == HEAD * lax.broadcasted_iota(jnp.int32, (D_SSM, LANES), 1))
    return _dot3r(v, sel.astype(BF))


def _ssd_prep(proj, dt_bias128, a_log128):
    def body(i, n, raw_ref, b_ref, al_ref, dt_ref, a_ref, dtb_ref, ab_ref, carry):
        @pl.when(i == 0)
        def _():
            carry[...] = jnp.zeros_like(carry)

        dt = _softplus(raw_ref[...] + b_ref[...])
        da = dt * (-jnp.exp(al_ref[...]))
        tri = (lax.broadcasted_iota(jnp.int32, (BLK, BLK), 0) >= lax.broadcasted_iota(jnp.int32, (BLK, BLK), 1))
        cs = _dot3(tri.astype(BF), da) + carry[0:1, :]
        dt_ref[...] = dt
        a_ref[...] = cs
        dtb_ref[...] = _spread_heads(dt)
        ab_ref[...] = _spread_heads(cs)
        carry[...] = jnp.broadcast_to(cs[BLK - 1:BLK, :], carry.shape)

    return _rowwise(body, name="ssd_prep", nrows=S, tile=BLK, row_ins=[(proj, LANES, DT_COLBLK)],
                    full_ins=[dt_bias128, a_log128],
                    row_outs=[(LANES, F32), (LANES, F32), (D_SSM, F32), (D_SSM, F32)],
                    scratch=[pltpu.VMEM((8, LANES), F32)])


def _ssd_decay(a_col, at_row, causal):
    diff = jnp.where(causal, a_col - at_row, 0.0)
    return jnp.where(causal, jnp.exp(diff), 0.0)


SSD_CB = 16


def _ssd_fwd(xbc, a_b, dt_b, at_r, dt_r):
    def kern(c_ref, b_ref, x_ref, ab_ref, dtb_ref, at_ref, dt_ref, y_ref, hall_ref):
        half0 = lax.broadcasted_iota(jnp.int32, (BLK, LANES), 1) < HEAD
        causal = lax.broadcasted_iota(jnp.int32, (BLK, BLK), 1) <= lax.broadcasted_iota(jnp.int32, (BLK, BLK), 0)

        def step(i, carry):
            h, a_prev = carry
            for cc in range(SSD_CB):
                chunk = i * SSD_CB + cc
                rows = pl.ds(pl.multiple_of(chunk * BLK, BLK), BLK)
                ci, bi, x = c_ref[rows, :].astype(BF), b_ref[rows, :].astype(BF), x_ref[rows, :]
                ab = ab_ref[rows, :]
                a_end = ab[BLK - 1:BLK, :]
                alpha = jnp.exp(ab - a_prev)
                beta = jnp.exp(a_end - ab) * dtb_ref[rows, :]
                hall_ref[rows, :] = h
                cb = _dot(ci, bi, NT)
                at, dtj = at_ref[chunk], dt_ref[chunk]
                y = alpha * _dot(ci, h.astype(BF), NN)
                for hd in range(2):
                    hm = half0 if hd == 0 else jnp.logical_not(half0)
                    w = cb * _ssd_decay(ab[:, HEAD * hd:HEAD * hd + 1], at[hd:hd + 1, :], causal) * dtj[hd:hd + 1, :]
                    y = y + _dot(w.astype(BF), jnp.where(hm, x, 0.0).astype(BF), NN)
                y_ref[rows, :] = y
                h = alpha[BLK - 1:BLK, :] * h + _dot(bi, (beta * x).astype(BF), TN)
                a_prev = a_end
            return h, a_prev

        lax.fori_loop(0, NB // SSD_CB, step, (jnp.zeros((BLK, LANES), F32), jnp.zeros((1, LANES), F32)))

    npair = D_SSM // LANES
    rowvec = pl.BlockSpec((None, NB, 2, BLK), lambda p: (p, 0, 0, 0))
    seq = pl.BlockSpec((S, LANES), lambda p: (0, p))
    return pl.pallas_call(
        kern, name="ssd_fwd", grid=(npair,),
        in_specs=[pl.BlockSpec((S, LANES), lambda p: (0, 12 + p // 2)),
                  pl.BlockSpec((S, LANES), lambda p: (0, 8 + p // 2)), seq, seq, seq, rowvec, rowvec],
        out_specs=[seq, seq], out_shape=[jax.ShapeDtypeStruct((S, D_SSM), F32)] * 2,
        compiler_params=_params("parallel"))(xbc, xbc, xbc, a_b, dt_b, at_r, dt_r)


def _ssd_post(y_ssd, xbc, proj, dskip_b, g_ssm):
    def body(i, n, y_ref, xs_ref, z_ref, d_ref, g_ref, o_ref):
        z = z_ref[...]
        y2 = (y_ref[...] + d_ref[...] * xs_ref[...]) * (z * _sigmoid(z))
        nx, _ = _seg_rms_n(y2, 256)
        o_ref[...] = (nx * g_ref[...]).astype(BF)

    return _rowwise(body, name="ssd_post", nrows=S, tile=ROW_TILE,
                    row_ins=[(y_ssd, D_SSM, 0), (xbc, D_SSM, 0), (proj, D_SSM, 3)], full_ins=[dskip_b, g_ssm],
                    row_outs=[(D_SSM, BF)])[0]


def _cross_heads(q, kv_ref, gq, gk):
    out = []
    for h in range(D_CROSS // CROSS_HEAD):
        sl = slice(CROSS_HEAD * h, CROSS_HEAD * (h + 1))
        nq, rq = _rms_n(q[:, sl])
        nk, rk = _rms_n(kv_ref[:, sl])
        v = kv_ref[:, D_CROSS + CROSS_HEAD * h:D_CROSS + CROSS_HEAD * (h + 1)]
        out.append((sl, nq, rq, nk, rk, v))
    return out


def _cross_fwd(qc, kv, g_cq, g_ck):
    scale = CROSS_HEAD ** -0.5

    def body(i, n, q_ref, kv_ref, gq_ref, gk_ref, o_ref):
        for sl, nq, _, nk, _, v in _cross_heads(q_ref[...], kv_ref, gq_ref[...], gk_ref[...]):
            qn = (nq * gq_ref[...] * scale).astype(BF)
            kn = (nk * gk_ref[...]).astype(BF)
            s = _dot(qn, kn, NT)
            e = jnp.exp(s - jnp.max(s, axis=-1, keepdims=True))
            p = e / jnp.sum(e, axis=-1, keepdims=True)
            o_ref[:, sl] = _dot(p.astype(BF), v.astype(BF), NN).astype(BF)

    return _rowwise(body, name="cross_fwd", nrows=S, tile=ROW_TILE, row_ins=[(qc, D_CROSS, 0)],
                    full_ins=[kv, g_cq, g_ck], row_outs=[(D_CROSS, BF)])[0]


def _rms_bwd_call(x, g, dh, dres, name, sumsq_res=False):
    rows = x.shape[0]
    has_res = dres is not None

    def body(i, n, *refs):
        if has_res:
            x_ref, dh_ref, dr_ref, g_ref, dx_ref, dxb_ref, dg_ref = refs[:7]
        else:
            x_ref, dh_ref, g_ref, dg_ref = refs
        dx, dg = _rms_bwd(x_ref[...], g_ref[...], dh_ref[...])
        if has_res:
            dr = dr_ref[...]
            dx = dx + dr
            dx_ref[...] = dx
            dxb_ref[...] = dx.astype(BF)
            if sumsq_res:
                _accum(refs[7], jnp.sum(jnp.sum(dr * dr, axis=0, keepdims=True), axis=1, keepdims=True), i)
        _accum(dg_ref, dg, i)

    row_ins = [(x, D, 0), (dh, D, 0)] + ([(dres, D, 0)] if has_res else [])
    return _rowwise(body, name=name, nrows=rows, tile=min(ROW_TILE, rows), row_ins=row_ins, full_ins=[g],
                    row_outs=[(D, F32), (D, BF)] if has_res else [],
                    acc_outs=[((1, D), F32)] + ([((1, 1), F32)] if sumsq_res else []))


def _cross_bwd(qc, doc, kv, g_cq, g_ck):
    scale = CROSS_HEAD ** -0.5

    def body(i, n, q_ref, do_ref, kv_ref, gq_ref, gk_ref, dq_ref, dkn_ref, dv_ref, dgq_ref):
        dgq = jnp.zeros((1, CROSS_HEAD), F32)
        dkn_parts, dv_parts = [], []
        for sl, nq, rq, nk, _, v in _cross_heads(q_ref[...], kv_ref, gq_ref[...], gk_ref[...]):
            qn = (nq * gq_ref[...] * scale).astype(BF)
            kn = (nk * gk_ref[...]).astype(BF)
            s = _dot(qn, kn, NT)
            e = jnp.exp(s - jnp.max(s, axis=-1, keepdims=True))
            p = e / jnp.sum(e, axis=-1, keepdims=True)
            do = do_ref[:, sl].astype(BF)
            dv_parts.append(_dot(p.astype(BF), do, TN))
            dp = _dot(do, v.astype(BF), NT)
            ds = (p * (dp - jnp.sum(dp * p, axis=-1, keepdims=True))).astype(BF)
            dqn = _dot(ds, kn, NN)
            dkn_parts.append(_dot(ds, qn, TN))
            dn = dqn * (gq_ref[...] * scale)
            dq_ref[:, sl] = (rq * (dn - nq * jnp.mean(dn * nq, axis=-1, keepdims=True))).astype(BF)
            dgq = dgq + jnp.sum(dqn * scale * nq, axis=0, keepdims=True)
        _accum(dkn_ref, jnp.concatenate(dkn_parts, axis=1), i)
        _accum(dv_ref, jnp.concatenate(dv_parts, axis=1), i)
        _accum(dgq_ref, dgq, i)

    return _rowwise(body, name="cross_bwd", nrows=S, tile=ROW_TILE, row_ins=[(qc, D_CROSS, 0), (doc, D_CROSS, 0)],
                    full_ins=[kv, g_cq, g_ck], row_outs=[(D_CROSS, BF)],
                    acc_outs=[((N_MEM, D_CROSS), F32), ((N_MEM, D_CROSS), F32), ((1, CROSS_HEAD), F32)])


def _cross_kv_bwd(kv, dkn, dv, g_ck):
    def body(i, n, kv_ref, dkn_ref, dv_ref, gk_ref, dkv_ref, dgk_ref):
        dgk = jnp.zeros((1, CROSS_HEAD), F32)
        for h in range(D_CROSS // CROSS_HEAD):
            sl = slice(CROSS_HEAD * h, CROSS_HEAD * (h + 1))
            dk, dg = _rms_bwd(kv_ref[:, sl], gk_ref[...], dkn_ref[:, sl])
            dkv_ref[:, sl] = dk.astype(BF)
            dgk = dgk + dg
        dkv_ref[:, D_CROSS:] = dv_ref[...].astype(BF)
        dgk_ref[...] = dgk

    return _rowwise(body, name="cross_kv_bwd", nrows=N_MEM, tile=N_MEM,
                    row_ins=[(kv, 2 * D_CROSS, 0), (dkn, D_CROSS, 0), (dv, D_CROSS, 0)], full_ins=[g_ck],
                    row_outs=[(2 * D_CROSS, BF)], acc_outs=[((1, CROSS_HEAD), F32)])


def _attn_merge_bwd(dmix, attn, g_attn):
    def body(i, n, dn_ref, a_ref, g_ref, da_ref, dl_ref, dg_ref):
        attn_v = a_ref[...]
        da, dg = _rms_bwd(attn_v, g_ref[...], dn_ref[...])
        da_ref[...] = da
        dl_ref[...] = _seg_sum(da * attn_v, HEAD)
        _accum(dg_ref, dg, i)

    return _rowwise(body, name="attn_merge_bwd", nrows=S, tile=ROW_TILE,
                    row_ins=[(dmix, D_ATTN, 0), (attn, D_ATTN, 0)], full_ins=[g_attn],
                    row_outs=[(D_ATTN, F32), (D_ATTN, F32)], acc_outs=[((1, D_ATTN), F32)])


def _attn_bwd(q, k, v, do, lse, delta):
    def kern(q_ref, k_ref, v_ref, do_ref, l_ref, d_ref, dq_ref, dk_ref, dv_ref):
        dk_ref[...] = jnp.zeros_like(dk_ref)
        dv_ref[...] = jnp.zeros_like(dv_ref)
        half0 = lax.broadcasted_iota(jnp.int32, (BLK, LANES), 1) < HEAD
        for gi, d in enumerate(DILATIONS):

            def group(g, carry, d=d, first=(gi == 0)):
                updates = []
                for bb in range(ATTN_QB):
                    q_rows, k_rows, valid = _band(g * ATTN_QB + bb, d)
                    q, do = q_ref[q_rows, :], do_ref[q_rows, :]
                    lse_t, del_t = l_ref[q_rows, :], d_ref[q_rows, :]
                    kb, vb = k_ref[k_rows, :].astype(BF), v_ref[k_rows, :].astype(BF)
                    q2 = jnp.concatenate([jnp.where(half0, q, 0.0), jnp.where(half0, 0.0, q)], axis=0).astype(BF)
                    do2 = jnp.concatenate([jnp.where(half0, do, 0.0), jnp.where(half0, 0.0, do)], axis=0).astype(BF)
                    lse2 = jnp.concatenate([lse_t[:, 0:1], lse_t[:, HEAD:HEAD + 1]], axis=0)
                    del2 = jnp.concatenate([del_t[:, 0:1], del_t[:, HEAD:HEAD + 1]], axis=0)
                    s = jnp.where(valid, _dot(q2, kb, NT), NEG)
                    p = jnp.exp(s - lse2)
                    ds = (p * (_dot(do2, vb, NT) - del2)).astype(BF)
                    dq2 = _dot(ds, kb, NN)
                    dq = jnp.where(half0, dq2[:BLK], dq2[BLK:])
                    if first:
                        dq_ref[q_rows, :] = dq
                    else:
                        dq_ref[q_rows, :] += dq
                    updates.append((k_rows, _dot(ds, q2, TN), _dot(p.astype(BF), do2, TN)))
                for k_rows, dk, dv in updates:
                    dk_ref[k_rows, :] += dk
                    dv_ref[k_rows, :] += dv
                return carry

            lax.fori_loop(0, NB // ATTN_QB, group, 0)

    seq = pl.BlockSpec((S, LANES), lambda p: (0, p))
    return pl.pallas_call(
        kern, name="attn_bwd", grid=(D_ATTN // LANES,), in_specs=[seq, seq, V_COLS, seq, seq, seq],
        out_specs=[seq, seq, seq], out_shape=[jax.ShapeDtypeStruct((S, D_ATTN), F32)] * 3,
        compiler_params=_params("parallel"))(q, k, v, do, lse, delta)


def _qk_bwd(dq_rot, dk_rot, dv, proj, pos_col, gq128, gk128, inv_tab):
    scale = HEAD ** -0.5

    def body(i, n, dq_ref, dk_ref, dv_ref, q_ref, k_ref, p_ref, gq_ref, gk_ref, it_ref,
             dqo_ref, dko_ref, dvo_ref, dgq_ref, dgk_ref):
        t = q_ref.shape[0]
        c, sg, lo = _rope_tables(p_ref[...], it_ref[...], t)
        dgq = jnp.zeros((1, LANES), F32)
        dgk = jnp.zeros((1, LANES), F32)
        for j in range(D_ATTN // LANES):
            sl = slice(LANES * j, LANES * (j + 1))
            dqr = _rope(dq_ref[:, sl], c, -sg, lo) * scale
            dkr = _rope(dk_ref[:, sl], c, -sg, lo)
            dq, gq = _seg_rms_bwd(q_ref[:, sl], gq_ref[...], dqr, HEAD)
            dk, gk = _seg_rms_bwd(k_ref[:, sl], gk_ref[...], dkr, HEAD)
            dqo_ref[:, sl] = dq.astype(BF)
            dko_ref[:, sl] = dk.astype(BF)
            dgq, dgk = dgq + gq, dgk + gk
        dvo_ref[...] = dv_ref[...].astype(BF)
        _accum(dgq_ref, _fold_heads(dgq), i)
        _accum(dgk_ref, _fold_heads(dgk), i)

    return _rowwise(body, name="qk_bwd", nrows=S, tile=ROW_TILE,
                    row_ins=[(a, D_ATTN, 0) for a in (dq_rot, dk_rot, dv)]
                    + [(proj, D_ATTN, 0), (proj, D_ATTN, 1), (pos_col, 1, 0)],
                    full_ins=[gq128, gk128, inv_tab], row_outs=[(D_ATTN, BF)] * 3,
                    acc_outs=[((1, LANES), F32)] * 2)


def _ssd_post_bwd(y_ssd, xbc, proj, dmix, dskip_b, g_ssm):
    def body(i, n, y_ref, xs_ref, z_ref, do_ref, d_ref, g_ref, dy_ref, dz_ref, dxs_ref, dd_ref, dg_ref):
        z, xs = z_ref[...], xs_ref[...]
        sg = _sigmoid(z)
        gate = z * sg
        y = y_ref[...] + d_ref[...] * xs
        dy2, dg = _seg_rms_bwd(y * gate, g_ref[...], do_ref[...], 256)
        dy = dy2 * gate
        dy_ref[...] = dy.astype(BF)
        dz_ref[...] = (dy2 * y * (sg * (1.0 + z * (1.0 - sg)))).astype(BF)
        dxs_ref[...] = dy * d_ref[...]
        _accum(dd_ref, jnp.sum(dy * xs, axis=0, keepdims=True), i)
        _accum(dg_ref, dg, i)

    return _rowwise(body, name="ssd_post_bwd", nrows=S, tile=ROW_TILE,
                    row_ins=[(y_ssd, D_SSM, 0), (xbc, D_SSM, 0), (proj, D_SSM, 3), (dmix, D_SSM, 1)],
                    full_ins=[dskip_b, g_ssm], row_outs=[(D_SSM, BF), (D_SSM, BF), (D_SSM, F32)],
                    acc_outs=[((1, D_SSM), F32), ((1, D_SSM), F32)])


def _ssd_bwd(xbc, dy, h_all, a_b, dt_b, at_r, dt_r):
    def kern(c_ref, b_ref, x_ref, dy_ref, hall_ref, ab_ref, dtb_ref, at_ref, dt_ref,
             dc_ref, daq_ref, dx_ref, db_ref, dak_ref, ddt_ref, ddtb_ref):
        half0 = lax.broadcasted_iota(jnp.int32, (BLK, LANES), 1) < HEAD
        hms = (half0, jnp.logical_not(half0))
        causal = lax.broadcasted_iota(jnp.int32, (BLK, BLK), 1) <= lax.broadcasted_iota(jnp.int32, (BLK, BLK), 0)
        row = lax.broadcasted_iota(jnp.int32, (BLK, LANES), 0)

        def step(t, carry_in):
            dh_next, carry = carry_in
            for cc in reversed(range(SSD_CB)):
                chunk = (NB // SSD_CB - 1 - t) * SSD_CB + cc
                rows = pl.ds(pl.multiple_of(chunk * BLK, BLK), BLK)
                ci, bi, x = c_ref[rows, :].astype(BF), b_ref[rows, :].astype(BF), x_ref[rows, :]
                dyv = dy_ref[rows, :].astype(F32)
                ab, dtb = ab_ref[rows, :], dtb_ref[rows, :]
                before = ab_ref[pl.ds(pl.multiple_of(jnp.maximum(chunk * BLK - 8, 0), 8), 8), :][7:8, :]
                a_prev = jnp.where(chunk == 0, 0.0, before)
                a_end = ab[BLK - 1:BLK, :]
                alpha = jnp.exp(ab - a_prev)
                e_end = jnp.exp(a_end - ab)
                beta = e_end * dtb
                t_all = alpha[BLK - 1:BLK, :]
                hc = hall_ref[rows, :]
                hcb, dhb = hc.astype(BF), dh_next.astype(BF)

                cb = _dot(ci, bi, NT)
                at, dtj = at_ref[chunk], dt_ref[chunk]
                dcb = jnp.zeros((BLK, BLK), F32)
                dx = jnp.zeros((BLK, LANES), F32)
                rsum = []
                for hd in range(2):
                    dym = jnp.where(hms[hd], dyv, 0.0).astype(BF)
                    lm = _ssd_decay(ab[:, HEAD * hd:HEAD * hd + 1], at[hd:hd + 1, :], causal)
                    dth = dtj[hd:hd + 1, :]
                    dw = _dot(dym, jnp.where(hms[hd], x, 0.0).astype(BF), NT)
                    g = dw * cb * lm
                    dx = dx + _dot((cb * lm * dth).astype(BF), dym, TN)
                    gcol = jnp.sum(g, axis=0, keepdims=True)
                    ddt_ref[chunk, hd:hd + 1, :] = gcol
                    dak_ref[chunk, hd:hd + 1, :] = gcol * dth
                    rsum.append(jnp.sum(g * dth, axis=1, keepdims=True))
                    dcb = dcb + dw * lm * dth
                dcb = dcb.astype(BF)

                g1 = (alpha * dyv).astype(BF)
                dalpha = dyv * _dot(ci, hcb, NN)
                g2 = _dot(bi, dhb, NN)
                dbeta = x * g2
                bx = (beta * x).astype(BF)
                dc_ref[rows, :] = _dot(dcb, bi, NN) + _dot(g1, hcb, NT)
                db_ref[rows, :] = _dot(dcb, ci, TN) + _dot(bx, dhb, NT)
                dx_ref[rows, :] = dx + beta * g2
                ddtb_ref[rows, :] = _seg_sum(e_end * dbeta, HEAD)
                ada, bdb = alpha * dalpha, beta * dbeta
                t_dt = t_all * jnp.sum(dh_next * hc, axis=0, keepdims=True)
                end_term = _seg_sum(jnp.broadcast_to(jnp.sum(bdb, axis=0, keepdims=True) + t_dt, (8, LANES)), HEAD)[0:1]
                prev_term = _seg_sum(jnp.broadcast_to(jnp.sum(ada, axis=0, keepdims=True) + t_dt, (8, LANES)), HEAD)[0:1]
                daq = jnp.where(half0, rsum[0], rsum[1]) + _seg_sum(ada - bdb, HEAD)
                daq_ref[rows, :] = daq + jnp.where(row == BLK - 1, end_term - carry, 0.0)
                carry = prev_term
                dh_next = t_all * dh_next + _dot(ci, g1, TN)
            return dh_next, carry

        lax.fori_loop(0, NB // SSD_CB, step, (jnp.zeros((BLK, LANES), F32), jnp.zeros((1, LANES), F32)))

    npair = D_SSM // LANES
    rowvec = pl.BlockSpec((None, NB, 2, BLK), lambda p: (p, 0, 0, 0))
    seq = pl.BlockSpec((S, LANES), lambda p: (0, p))
    return pl.pallas_call(
        kern, name="ssd_bwd", grid=(npair,),
        in_specs=[pl.BlockSpec((S, LANES), lambda p: (0, 12 + p // 2)),
                  pl.BlockSpec((S, LANES), lambda p: (0, 8 + p // 2)),
                  seq, seq, seq, seq, seq, rowvec, rowvec],
        out_specs=[seq, seq, seq, seq, rowvec, rowvec, seq],
        out_shape=[jax.ShapeDtypeStruct((S, D_SSM), F32)] * 4
        + [jax.ShapeDtypeStruct((npair, NB, 2, BLK), F32)] * 2 + [jax.ShapeDtypeStruct((S, D_SSM), F32)],
        compiler_params=_params("parallel"))(xbc, xbc, xbc, dy, h_all, a_b, dt_b, at_r, dt_r)


def _ssd_prep_bwd(daq, dak, ddt_row, ddt_lane, dt, proj, dt_bias128, a_log128):
    def body(i, n, daq_ref, dak_ref, dd_ref, dd2_ref, dt_ref, raw_ref, b_ref, al_ref, draw_ref, dal_ref, db_ref,
             carry):
        @pl.when(i == 0)
        def _():
            carry[...] = jnp.zeros_like(carry)

        a = -jnp.exp(al_ref[...])
        tri = (lax.broadcasted_iota(jnp.int32, (BLK, BLK), 0) <= lax.broadcasted_iota(jnp.int32, (BLK, BLK), 1))
        rev = _dot3(tri.astype(BF), _collect_heads(daq_ref[...]) - dak_ref[...]) + carry[0:1, :]
        carry[...] = jnp.broadcast_to(rev[0:1, :], carry.shape)
        dtv = dt_ref[...]
        draw = (dd_ref[...] + _collect_heads(dd2_ref[...]) + a * rev) * _sigmoid(raw_ref[...] + b_ref[...])
        draw_ref[...] = draw.astype(BF)
        _accum(dal_ref, jnp.sum(dtv * rev, axis=0, keepdims=True) * a, i)
        _accum(db_ref, jnp.sum(draw, axis=0, keepdims=True), i)

    return _rowwise(body, name="ssd_prep_bwd", nrows=S, tile=BLK, reverse=True,
                    row_ins=[(daq, D_SSM, 0), (dak, LANES, 0), (ddt_row, LANES, 0), (ddt_lane, D_SSM, 0), (dt, LANES, 0),
                             (proj, LANES, DT_COLBLK)],
                    full_ins=[dt_bias128, a_log128], row_outs=[(LANES, BF)],
                    acc_outs=[((1, LANES), F32), ((1, LANES), F32)], scratch=[pltpu.VMEM((8, LANES), F32)])


def _conv_bwd(proj, conv_w, conv_b, d1, d2, map1, map2, col0, ntiles, name):
    base = (4 * D_ATTN + col0) // LANES

    def kern(x_ref, w_ref, b_ref, d1_ref, d2_ref, dx_ref, dw_ref, db_ref):
        x = x_ref[...]
        c, shifted = _conv_taps(x, w_ref)
        c = c + b_ref[...]
        sg = _sigmoid(c)
        dc = (d1_ref[...] + d2_ref[...]) * (sg * (1.0 + c * (1.0 - sg)))
        rows = lax.broadcasted_iota(jnp.int32, x.shape, 0)
        dx = jnp.zeros_like(x)
        for w in range(CONV_W):
            k = CONV_W - 1 - w
            up = dc if k == 0 else jnp.where(rows < S - k, pltpu.roll(dc, S - k, 0), 0.0)
            dx = dx + up * w_ref[w:w + 1, :]
            dw_ref[w:w + 1, :] = jnp.sum(dc * shifted[w], axis=0, keepdims=True)
        dx_ref[...] = dx.astype(BF)
        db_ref[...] = jnp.sum(dc, axis=0, keepdims=True)

    c0 = col0 // LANES
    return pl.pallas_call(
        kern, name=name, grid=(ntiles,),
        in_specs=[pl.BlockSpec((S, LANES), lambda c: (0, base + c)),
                  pl.BlockSpec((CONV_W, LANES), lambda c: (0, c0 + c)),
                  pl.BlockSpec((1, LANES), lambda c: (0, c0 + c)),
                  pl.BlockSpec((S, LANES), lambda c: (0, map1(c))),
                  pl.BlockSpec((S, LANES), lambda c: (0, map2(c)))],
        out_specs=[pl.BlockSpec((S, LANES), lambda c: (0, c)), pl.BlockSpec((CONV_W, LANES), lambda c: (0, c)),
                   pl.BlockSpec((1, LANES), lambda c: (0, c))],
        out_shape=[jax.ShapeDtypeStruct((S, ntiles * LANES), BF), jax.ShapeDtypeStruct((CONV_W, ntiles * LANES), F32),
                   jax.ShapeDtypeStruct((1, ntiles * LANES), F32)],
        compiler_params=_params("parallel"))(proj, conv_w, conv_b, d1, d2)


def _adamw(w, m, v, parts, name):
    r, c = w.shape
    n_parts = parts.shape[0]
    tile = r if r <= 512 else (128 if c > 1024 else 256)
    assert r % tile == 0
    c1 = 1.0 - ADAM_B1 ** ADAM_STEP
    c2 = 1.0 - ADAM_B2 ** ADAM_STEP

    def kern(w_ref, m_ref, v_ref, p_ref, g_ref, d_ref, mo_ref, vo_ref):
        g = p_ref[0].astype(F32)
        for k in range(1, n_parts):
            g = g + p_ref[k].astype(F32)
        mn = ADAM_B1 * m_ref[...] + (1.0 - ADAM_B1) * g
        vn = ADAM_B2 * v_ref[...] + (1.0 - ADAM_B2) * (g * g)
        g_ref[...] = g
        mo_ref[...] = mn
        vo_ref[...] = vn
        d_ref[...] = -ADAM_LR * ((mn / c1) / (jnp.sqrt(vn / c2) + ADAM_EPS) + ADAM_WD * w_ref[...])

    blk = pl.BlockSpec((tile, c), lambda i: (i, 0))
    return pl.pallas_call(
        kern, name=name, grid=(r // tile,),
        in_specs=[blk, blk, blk, pl.BlockSpec((n_parts, tile, c), lambda i: (0, i, 0))],
        out_specs=[blk] * 4, out_shape=[jax.ShapeDtypeStruct((r, c), F32)] * 4,
        compiler_params=_params("parallel"))(w, m, v, parts)


MESH = pl.DeviceIdType.MESH
ANY = pl.BlockSpec(memory_space=pl.ANY)


def _put_own(gathered, shard):
    me = 4 * lax.axis_index("x") + 2 * lax.axis_index("y") + lax.axis_index("c")
    return lax.dynamic_update_slice(gathered, shard[None], (me,) + (0,) * shard.ndim)


def _all_gather(arrs, name, after=None):
    na = len(arrs)
    n_after = 0 if after is None else 1

    def kern(*refs):
        ins, outs = refs[:na], refs[na + n_after:2 * na + n_after]
        send_sems, recv_sems = refs[2 * na + n_after:]
        x, y, c = lax.axis_index("x"), lax.axis_index("y"), lax.axis_index("c")
        me, sibling = (x, y, c), (x, y, 1 - c)
        a_chip = (jnp.where(c == 0, 1 - x, x), jnp.where(c == 0, y, 1 - y))
        b_chip = (jnp.where(c == 0, x, 1 - x), jnp.where(c == 0, 1 - y, y))
        chips = [a_chip, b_chip, (1 - x, 1 - y)]

        def copy(a, k, block, to, src=None):
            px, py, pc = block
            dst = outs[a].at[4 * px + 2 * py + pc]
            return pltpu.make_async_remote_copy(
                src_ref=dst if src is None else src, dst_ref=dst, send_sem=send_sems.at[a, k],
                recv_sem=recv_sems.at[a, k], device_id=to, device_id_type=MESH)

        sends = []
        for a in range(na):
            sends.append(copy(a, 0, me, sibling, src=ins[a]))
            sends += [copy(a, 1 + j, me, (*chips[j], c), src=ins[a]) for j in range(2)]
        for cp in sends:
            cp.start()
        for a in range(na):
            for j, chip in enumerate(chips):
                copy(a, 1 + j, (*chip, c), me).wait_recv()
                later = [copy(a, 4 + j, (*chip, c), sibling)]
                if j == 0:
                    later.append(copy(a, 3, (*a_chip, c), (*b_chip, c)))
                for cp in later:
                    cp.start()
                sends += later
        for a in range(na):
            copy(a, 0, sibling, me).wait_recv()
            for j, chip in enumerate(chips):
                copy(a, 4 + j, (*chip, 1 - c), me).wait_recv()
        for cp in sends:
            cp.wait_send()

    outs = pl.pallas_call(
        kern, name=name, in_specs=[ANY] * (na + n_after), out_specs=[ANY] * na,
        out_shape=[jax.ShapeDtypeStruct((N_DEV,) + a.shape, a.dtype) for a in arrs],
        scratch_shapes=[pltpu.SemaphoreType.DMA((na, 7)), pltpu.SemaphoreType.DMA((na, 7))],
    )(*arrs, *([] if after is None else [after]))
    return [_put_own(o, a) for o, a in zip(outs, arrs)]


HBM = pl.BlockSpec(memory_space=pltpu.HBM)
SEM = pl.BlockSpec(memory_space=pltpu.SEMAPHORE)
EFFECT = pltpu.SideEffectType.DATAFLOW_SIDE_EFFECTING
N_CHIP = 4
N_COPIES = {"gather": 3, "scatter": 3, "forward": 4, "pair": 4, "direct": 2, "relay": 4, "diag": 1}


def _in_hbm(a):
    return pltpu.with_memory_space_constraint(a, pltpu.HBM)


def _chip_copies(kind, src_refs, land_refs, send_sems, recv_sems):
    x, y, c = lax.axis_index("x"), lax.axis_index("y"), lax.axis_index("c")
    chips = [(1 - x, y), (x, 1 - y), (1 - x, 1 - y)]
    a_chip = (jnp.where(c == 0, 1 - x, x), jnp.where(c == 0, y, 1 - y))
    b_chip = (jnp.where(c == 0, x, 1 - x), jnp.where(c == 0, 1 - y, y))
    n = N_COPIES[kind]
    cps = []

    def add(a, j, src, dst, to):
        cps.append(pltpu.make_async_remote_copy(
            src_ref=src, dst_ref=dst, send_sem=send_sems.at[n * a + j], recv_sem=recv_sems.at[n * a + j],
            device_id=to, device_id_type=MESH))

    def slot(a, chip, core):
        return land_refs[a].at[4 * chip[0] + 2 * chip[1] + core]

    for a in range(len(src_refs)):
        if kind == "direct":
            add(a, 0, src_refs[a], slot(a, (x, y), c), (*a_chip, c))
            add(a, 1, src_refs[a], slot(a, (x, y), c), (*b_chip, c))
        elif kind == "relay":
            add(a, 0, slot(a, a_chip, c), slot(a, a_chip, c), (*b_chip, c))
            add(a, 1, src_refs[a], slot(a, (x, y), c), (x, y, 1 - c))
            add(a, 2, slot(a, a_chip, c), slot(a, a_chip, c), (x, y, 1 - c))
            add(a, 3, slot(a, b_chip, c), slot(a, b_chip, c), (x, y, 1 - c))
        elif kind == "diag":
            add(a, 0, slot(a, (1 - x, 1 - y), c), slot(a, (1 - x, 1 - y), c), (x, y, 1 - c))
        elif kind == "gather":
            for j, (px, py) in enumerate(chips):
                add(a, j, src_refs[a], land_refs[a].at[4 * x + 2 * y + c], (px, py, c))
        elif kind == "scatter":
            for j, (px, py) in enumerate(chips):
                add(a, j, src_refs[a].at[2 * px + py], land_refs[a].at[2 * x + y], (px, py, c))
        elif kind == "forward":
            add(a, 0, src_refs[a], land_refs[a].at[4 * x + 2 * y + c], (x, y, 1 - c))
            for j, (px, py) in enumerate(chips):
                slot = land_refs[a].at[4 * px + 2 * py + c]
                add(a, 1 + j, slot, slot, (x, y, 1 - c))
        else:
            for q in range(N_CHIP):
                add(a, q, src_refs[a].at[2 * q + 1 - c], land_refs[a].at[q], (x, y, 1 - c))
    return cps


def _exchange_start(kind, srcs, lands, after, name):
    na = len(srcs)
    n_after = 0 if after is None else 1

    def kern(*refs):
        src_refs, land_refs = refs[:na], refs[na:2 * na]
        send_sems, recv_sems = refs[2 * na + n_after], refs[2 * na + n_after + 1]
        token = refs[-1]
        for cp in _chip_copies(kind, src_refs, land_refs, send_sems, recv_sems):
            cp.start()
        token[...] = jnp.zeros_like(token)

    bufs = list(srcs) + list(lands)
    res = pl.pallas_call(
        kern, name=name,
        out_shape=(pltpu.SemaphoreType.DMA((N_COPIES[kind] * na,)), pltpu.SemaphoreType.DMA((N_COPIES[kind] * na,)))
        + tuple(pltpu.HBM(b.shape, b.dtype) for b in bufs) + (jax.ShapeDtypeStruct((8, LANES), F32),),
        in_specs=[HBM] * (2 * na) + [ANY] * n_after,
        out_specs=(SEM, SEM) + (HBM,) * (2 * na) + (pl.BlockSpec(memory_space=pltpu.VMEM),),
        input_output_aliases={i: 2 + i for i in range(2 * na)},
        compiler_params=pltpu.CompilerParams(has_side_effects=EFFECT),
    )(*[_in_hbm(b) for b in bufs], *([] if after is None else [after]))
    return (res[0], res[1]), list(res[2:2 + na]), list(res[2 + na:2 + 2 * na]), res[-1]


def _exchange_wait(kind, sems, srcs, lands, after, name):
    na = len(srcs)
    afters = list(after) if isinstance(after, (list, tuple)) else [after]

    def kern(*refs):
        src_refs, land_refs = refs[:na], refs[na:2 * na]
        send_sems, recv_sems = refs[2 * na], refs[2 * na + 1]
        for cp in _chip_copies(kind, src_refs, land_refs, send_sems, recv_sems):
            cp.wait_send()
            cp.wait_recv()

    bufs = list(srcs) + list(lands)
    res = pl.pallas_call(
        kern, name=name, out_shape=tuple(pltpu.HBM(b.shape, b.dtype) for b in bufs),
        in_specs=[HBM] * (2 * na) + [SEM, SEM] + [ANY] * len(afters), out_specs=(HBM,) * (2 * na),
        input_output_aliases={i: i for i in range(2 * na)},
        compiler_params=pltpu.CompilerParams(has_side_effects=EFFECT),
    )(*bufs, sems[0], sems[1], *afters)
    return list(res[:na]), list(res[na:])


def _pair_exchange(parts, name):
    na = len(parts)

    def kern(*refs):
        ins, got = refs[:na], refs[na:2 * na]
        send_sems, recv_sems = refs[2 * na:]
        x, y, c = lax.axis_index("x"), lax.axis_index("y"), lax.axis_index("c")
        sends = []
        for a in range(na):
            for q in range(N_CHIP):
                sends.append(pltpu.make_async_remote_copy(
                    src_ref=ins[a].at[2 * q + 1 - c], dst_ref=got[a].at[q], send_sem=send_sems.at[a, q],
                    recv_sem=recv_sems.at[a, q], device_id=(x, y, 1 - c), device_id_type=MESH))
        for cp in sends:
            cp.start()
        for cp in sends:
            cp.wait()

    return pl.pallas_call(
        kern, name=name, in_specs=[ANY] * na, out_specs=[ANY] * na,
        out_shape=[jax.ShapeDtypeStruct((N_CHIP,) + p.shape[1:], p.dtype) for p in parts],
        scratch_shapes=[pltpu.SemaphoreType.DMA((na, N_CHIP)), pltpu.SemaphoreType.DMA((na, N_CHIP))])(*parts)


def _pair_sum(parts, got, name):
    _, r, cdim = parts.shape
    tile = min(r, ROW_TILE)
    x, y, c = lax.axis_index("x"), lax.axis_index("y"), lax.axis_index("c")
    where = jnp.stack([c, 2 * x + y]).astype(jnp.int32)

    def kern(w_ref, a_ref, b_ref, q_ref, own_ref):
        s = (a_ref[...].astype(F32) + b_ref[...].astype(F32)).astype(BF)
        q_ref[...] = s

        @pl.when(pl.program_id(1) == w_ref[1])
        def _():
            own_ref[...] = s

    blk = pl.BlockSpec((None, tile, cdim), lambda i, q, w_ref: (q, i, 0))
    sums, own = pl.pallas_call(
        kern, name=name,
        out_shape=[jax.ShapeDtypeStruct((N_CHIP, r, cdim), BF), jax.ShapeDtypeStruct((r, cdim), BF)],
        grid_spec=pltpu.PrefetchScalarGridSpec(
            num_scalar_prefetch=1, grid=(r // tile, N_CHIP),
            in_specs=[pl.BlockSpec((None, tile, cdim), lambda i, q, w_ref: (2 * q + w_ref[0], i, 0)), blk],
            out_specs=[blk, pl.BlockSpec((tile, cdim), lambda i, q, w_ref: (i, 0))]),
        compiler_params=_params("parallel", "arbitrary"))(where, parts, got)
    land = lax.dynamic_update_slice(lax.empty((N_CHIP, r, cdim), BF), own[None], (2 * x + y, 0, 0))
    return sums, land


W_IN_COLS = D_IN // N_DEV


def _w_in_from_blocks(w8):
    def kern(x_ref, o_ref):
        for j in range(N_DEV):
            o_ref[:, W_IN_COLS * j:W_IN_COLS * (j + 1)] = x_ref[j]
        o_ref[:, D_IN:] = jnp.zeros((ROW_TILE, D_INP - D_IN), o_ref.dtype)

    return pl.pallas_call(
        kern, name="w_in_from_blocks", grid=(D // ROW_TILE,),
        in_specs=[pl.BlockSpec((N_DEV, ROW_TILE, W_IN_COLS), lambda i: (0, i, 0))],
        out_specs=pl.BlockSpec((ROW_TILE, D_INP), lambda i: (i, 0)),
        out_shape=jax.ShapeDtypeStruct((D, D_INP), w8.dtype), compiler_params=_params("parallel"))(w8)


def _w_in_to_blocks(g):
    def kern(x_ref, o_ref):
        for j in range(N_DEV):
            o_ref[j] = x_ref[:, W_IN_COLS * j:W_IN_COLS * (j + 1)]

    return pl.pallas_call(
        kern, name="w_in_to_blocks", grid=(D // ROW_TILE,),
        in_specs=[pl.BlockSpec((ROW_TILE, D_INP), lambda i: (i, 0))],
        out_specs=pl.BlockSpec((N_DEV, ROW_TILE, W_IN_COLS), lambda i: (0, i, 0)),
        out_shape=jax.ShapeDtypeStruct((N_DEV, D, W_IN_COLS), g.dtype), compiler_params=_params("parallel"))(g)


def _pad_lanes(v, width=LANES):
    return jnp.pad(v, ((0, 0), (0, width - v.shape[1])))


def _row_layout(t16):
    return t16.T.reshape(N_HEADS // 2, 2, NB, BLK).transpose(0, 2, 1, 3)


def _row_layout_inv(r):
    return r.transpose(0, 2, 1, 3).reshape(N_HEADS, S).T


SMALL = (("g_mix", D), ("g_q", HEAD), ("g_k", HEAD), ("g_attn_out", D_ATTN), ("conv_b", D_CONV),
         ("dt_bias", 16), ("a_log", 16), ("d_skip", 16), ("g_ssm_out", D_SSM), ("g_cross", D),
         ("g_mem", D), ("g_cq", CROSS_HEAD), ("g_ck", CROSS_HEAD), ("g_mlp", D))
SMALL_ROWS = -(-sum(-(-w // LANES) for _, w in SMALL) // 8) * 8


def _pack_small(vals):
    rows = [_pad_lanes(vals[n], -(-w // LANES) * LANES).reshape(-1, LANES) for n, w in SMALL]
    packed = jnp.concatenate(rows, axis=0)
    return jnp.pad(packed, ((0, SMALL_ROWS - packed.shape[0]), (0, 0)))


def _unpack_small(packed):
    out, r = {}, 0
    for n, w in SMALL:
        nr = -(-w // LANES)
        out[n] = packed[r:r + nr].reshape(1, nr * LANES)[:, :w]
        r += nr
    return out


BIG = ("w_in", "w_out", "w_cq", "w_ckv", "w_co", "w_up", "w_down")
WEIGHTS = ("g_mix", "w_in", "g_q", "g_k", "g_attn_out", "conv_w", "conv_b", "dt_bias", "a_log", "d_skip",
           "g_ssm_out", "w_out", "g_cross", "g_mem", "w_cq", "w_ckv", "g_cq", "g_ck", "w_co", "g_mlp", "w_up", "w_down")


REST = ("w_out", "w_cq", "w_ckv", "w_co", "w_up", "w_down")


class _Overlap:
    def __init__(self, first):
        self.first_names = list(first)
        srcs = [first[n] for n in self.first_names]
        lands = [_put_own(lax.empty((N_DEV,) + s.shape, s.dtype), s) for s in srcs]
        self.direct = _exchange_start("direct", srcs, lands, None, "ag_first_start")
        self.token = self.direct[3]
        self.shards, self.behind, self.late = {}, [], []
        self.gather, self.forward = {}, {}
        self.names = {"a": REST[:4], "b": REST[4:5], "c": REST[5:]}
        self.groups = []
        self.pairs = {}

    def first(self, after):
        sems, srcs, lands, _ = self.direct
        srcs, lands = _exchange_wait("direct", sems, srcs, lands,
                                     [after] + list(self.shards.values()) + list(self.behind), "ag_first_wait")
        sems, srcs, lands, token = _exchange_start("relay", srcs, lands, None, "ag_relay_start")
        self.late = [_tie(t, token) for t in self.late]
        srcs, lands = _exchange_wait("relay", sems, srcs, lands, [token] + self.late, "ag_relay_wait")
        sems, srcs, lands, token = _exchange_start("diag", srcs, lands, None, "ag_diag_start")
        for tag, names in self.names.items():
            rest_lands = [_put_own(lax.empty((N_DEV,) + self.shards[n].shape, BF), self.shards[n]) for n in names]
            self.gather[tag] = _exchange_start("gather", [self.shards[n] for n in names], rest_lands, token,
                                               "ag_start_" + tag)
            token = self.gather[tag][3]
        _, lands = _exchange_wait("diag", sems, srcs, lands, token, "ag_diag_wait")
        return dict(zip(self.first_names, lands))

    def rest_mid(self, tag, after):
        sems, srcs, lands, _ = self.gather[tag]
        srcs, lands = _exchange_wait("gather", sems, srcs, lands, after, "ag_wait_" + tag)
        self.forward[tag] = _exchange_start("forward", srcs, lands, None, "ag_fwd_start_" + tag)
        return self.forward[tag][3]

    def rest(self, tag, after):
        sems, srcs, lands, _ = self.forward[tag]
        _, lands = _exchange_wait("forward", sems, srcs, lands, after, "ag_fwd_wait_" + tag)
        wf = dict(zip(self.names[tag], lands))
        for n in wf:
            if n in ("w_out", "w_cq", "w_ckv", "w_down"):
                wf[n] = wf[n].reshape(-1, wf[n].shape[-1])
        return wf

    def pair_start(self, name, grad):
        got = lax.empty((N_CHIP,) + grad.shape[1:], grad.dtype)
        self.pairs[name] = _exchange_start("pair", [grad], [got], None, "rs_pair_start_" + name)
        return self.pairs[name][3]

    def emit(self, grads, after):
        names, tag = list(grads), "_%d" % len(self.groups)
        grads, got = dict(grads), {}
        for n in names:
            if n in self.pairs:
                sems, srcs, lands, _ = self.pairs[n]
                srcs, lands = _exchange_wait("pair", sems, srcs, lands, after, "rs_pair_wait_" + n)
                grads[n], got[n] = srcs[0], lands[0]
        sync = [n for n in names if n not in got]
        if sync:
            got.update(zip(sync, _pair_exchange([grads[n] for n in sync], "rs_pair" + tag)))
        sums = [_pair_sum(grads[n], got[n], "rs_sum_" + n) for n in names]
        sems, srcs, lands, token = _exchange_start("scatter", [q for q, _ in sums], [l for _, l in sums], None,
                                                   "rs_chip_start" + tag)
        self.groups.append((names, sems, srcs, lands))
        return token

    def finish(self, gi, after):
        names, sems, srcs, lands = self.groups[gi]
        _, lands = _exchange_wait("scatter", sems, srcs, lands, after, "rs_chip_wait_%d" % gi)
        return dict(zip(names, lands))


def _tie(v, token):
    return v if token is None else v + token[0:1, 0:1]


def _local_step(x, mem, pos_col, target, p, wf, hooks=None):
    p = dict(p)
    inv = np.zeros((1, LANES), np.float32)
    freq = (ROPE_THETA ** (-2.0 * np.arange(ROT // 2, dtype=np.float32) / ROT)).astype(np.float32)
    for l in range(LANES):
        if l % HEAD < ROT:
            inv[0, l] = freq[(l % HEAD) % (ROT // 2)]
    inv_tab = jnp.asarray(inv)
    gq128, gk128 = jnp.tile(p["g_q"], (1, 2)), jnp.tile(p["g_k"], (1, 2))
    dtb128, alog128 = _pad_lanes(p["dt_bias"]), _pad_lanes(p["a_log"])
    dskip_b = jnp.repeat(p["d_skip"], HEAD, axis=1)

    h = _rms_fwd(x, p["g_mix"], "rms_mix")
    if hooks is not None:
        got = hooks.first(h)
        wf = {**wf, "w_in": _w_in_from_blocks(got["w_in"]),
              "conv_w": got["conv_w"].transpose(1, 0, 2).reshape(CONV_W, D_CONV)}
    conv_w, conv_b = wf["conv_w"], p["conv_b"]
    proj = _matmul(h, wf["w_in"], mode="nn", name="mm_in", tm=1024, tn=1280, tk=D)
    qr, kr = _qk_prep(proj, pos_col, gq128, gk128, inv_tab)
    attn, lse = _attn_fwd(qr, kr, proj)
    attn_n = _attn_norm(attn, p["g_attn_out"])

    xbc = _conv_fwd(proj, conv_w, conv_b)
    token = None if hooks is None else hooks.rest_mid("a", xbc)
    dt, a_cs, dt_b, a_b = _ssd_prep(proj, _tie(dtb128, token), alog128)
    at_r, dt_r = _row_layout(a_cs[:, :N_HEADS]), _row_layout(dt[:, :N_HEADS])
    y_ssd, h_all = _ssd_fwd(xbc, a_b, dt_b, at_r, dt_r)
    ssm_n = _ssd_post(y_ssd, xbc, proj, dskip_b, p["g_ssm_out"])

    if hooks is not None:
        wf = {**wf, **hooks.rest("a", ssm_n)}
    mix = jnp.concatenate([attn_n, ssm_n], axis=1)
    x1 = _matmul(mix, wf["w_out"], mode="nn", name="mm_out", tm=1024, tn=1024, tk=D,
                 extras=(x,), epilogue=lambda acc, r: (acc + r,))
    hc = _rms_fwd(x1, _tie(p["g_cross"], None if hooks is None else hooks.rest_mid("b", x1)), "rms_cross")
    memh = _rms_fwd(mem, p["g_mem"], "rms_mem")
    qc = _matmul(hc, wf["w_cq"], mode="nn", name="mm_cq", tm=1024, tn=512, tk=D)
    kv = _matmul(memh, wf["w_ckv"], mode="nn", name="mm_ckv", tm=N_MEM, tn=1024, tk=D)
    oc = _cross_fwd(qc, kv, p["g_cq"], p["g_ck"])
    x2 = _matmul(oc, wf["w_co"], mode="nn", name="mm_co", tm=1024, tn=256, tk=512, b_cb=256,
                 extras=(x1,), epilogue=lambda acc, r: (acc + r,))
    hm = _rms_fwd(x2, p["g_mlp"], "rms_mlp")
    if hooks is not None:
        wf = {**wf, **hooks.rest("b", hm)}

    def up_epi(acc):
        ru = jnp.maximum(acc, 0.0)
        return ru * ru, 2.0 * ru

    act, relu2 = _matmul(hm, wf["w_up"], mode="nn", name="mm_up", tm=1024, tn=1024, tk=D, b_cb=1024,
                         out_dtypes=(BF, BF), epilogue=up_epi)
    if hooks is not None:
        hooks.rest_mid("c", act)
        wf = {**wf, **hooks.rest("c", relu2)}

    def loss_epi(acc, r, t):
        d = (acc + r - t) * (1.0 / D)
        return d, d

    dy, dyb = _matmul(act, wf["w_down"], mode="nn", name="mm_down", tm=512, tn=1024, tk=2048, extras=(x2, target),
                      out_dtypes=(F32, BF), epilogue=loss_epi)

    gb, gs = {}, {}
    gb["w_down"] = _matmul(act, dyb, mode="tn", name="mm_down_dw", tm=1024, tn=1024, tk=S,
                           out_dtypes=(BF,)).reshape(N_DEV, D_FF // N_DEV, D)
    token = None if hooks is None else hooks.pair_start("w_down", gb["w_down"])
    du = _matmul(dyb, wf["w_down"], mode="nt", name="mm_down_dx", tm=1024, tn=1024, tk=D, extras=(relu2,),
                 out_dtypes=(BF,), epilogue=lambda acc, r: (acc * r.astype(F32),), after=token)
    gb["w_up"] = _matmul(hm, du, mode="tn", name="mm_up_dw", tm=1024, tn=1024, tk=S, out_dtypes=(BF,), out_cb=1024)
    token = None if hooks is None else hooks.pair_start("w_up", gb["w_up"])
    dhm = _matmul(du, wf["w_up"], mode="nt", name="mm_up_dx", tm=1024, tn=1024, tk=2048, b_cb=1024, after=token)
    if hooks is not None:
        p["g_mlp"] = _tie(p["g_mlp"], hooks.emit({n: gb[n] for n in ("w_down", "w_up")}, dhm))
    dx2, dx2b, gs["g_mlp"], sumsq_dy = _rms_bwd_call(x2, p["g_mlp"], dhm, dy, "rms_mlp_bwd", sumsq_res=True)
    loss_part = sumsq_dy[0, 0] * (0.5 * D)

    doc = _matmul(dx2b, wf["w_co"], mode="nt", name="mm_co_dx", tm=1024, tn=512, tk=256, b_cb=256)
    gb["w_co"] = _matmul(oc, dx2b, mode="tn", name="mm_co_dw", tm=512, tn=256, tk=S, out_dtypes=(BF,), out_cb=256)
    dqc, dkn, dvc, gs["g_cq"] = _cross_bwd(qc, doc, kv, p["g_cq"], p["g_ck"])
    dkv, gs["g_ck"] = _cross_kv_bwd(kv, dkn, dvc, p["g_ck"])
    gb["w_cq"] = _matmul(hc, dqc, mode="tn", name="mm_cq_dw", tm=1024, tn=512, tk=S,
                         out_dtypes=(BF,)).reshape(N_DEV, D // N_DEV, D_CROSS)
    dhc = _matmul(dqc, wf["w_cq"], mode="nt", name="mm_cq_dx", tm=1024, tn=1024, tk=512)
    gb["w_ckv"] = _matmul(memh, dkv, mode="tn", name="mm_ckv_dw", tm=1024, tn=1024, tk=N_MEM,
                          out_dtypes=(BF,)).reshape(N_DEV, D // N_DEV, 2 * D_CROSS)
    dmemh = _matmul(dkv, wf["w_ckv"], mode="nt", name="mm_ckv_dx", tm=N_MEM, tn=1024, tk=1024)
    (gs["g_mem"],) = _rms_bwd_call(mem, p["g_mem"], dmemh, None, "rms_mem_bwd")
    dx1, dx1b, gs["g_cross"] = _rms_bwd_call(x1, p["g_cross"], dhc, dx2, "rms_cross_bwd")

    dmix = _matmul(dx1b, wf["w_out"], mode="nt", name="mm_out_dx", tm=1024, tn=1024, tk=D)
    gb["w_out"] = _matmul(mix, dx1b, mode="tn", name="mm_out_dw", tm=1024, tn=1024, tk=S,
                          out_dtypes=(BF,)).reshape(N_DEV, D // N_DEV, D)
    if hooks is not None:
        p["g_attn_out"] = _tie(p["g_attn_out"],
                               hooks.emit({n: gb[n] for n in ("w_co", "w_cq", "w_ckv", "w_out")}, dmix))

    dattn, delta, gs["g_attn_out"] = _attn_merge_bwd(dmix, attn, p["g_attn_out"])
    dq_rot, dk_rot, dv_attn = _attn_bwd(qr, kr, proj, dattn, lse, delta)
    dq_pre, dk_pre, dv_pre, dgq, dgk = _qk_bwd(dq_rot, dk_rot, dv_attn, proj, pos_col, gq128, gk128, inv_tab)
    gs["g_q"], gs["g_k"] = dgq[:, :HEAD], dgk[:, :HEAD]

    dy_ssd, dz, dxs_skip, dd_lane, gs["g_ssm_out"] = _ssd_post_bwd(y_ssd, xbc, proj, dmix, dskip_b, p["g_ssm_out"])
    gs["d_skip"] = dd_lane.reshape(N_HEADS, HEAD).sum(axis=1).reshape(1, N_HEADS)
    dc_pair, daq_b, dx_ssd, db_pair, dak_r, ddt_r, ddt_b = _ssd_bwd(xbc, dy_ssd, h_all, a_b, dt_b, at_r, dt_r)
    dak = _pad_lanes(_row_layout_inv(dak_r))
    ddt_direct = _pad_lanes(_row_layout_inv(ddt_r))
    ddt_raw, dalog, dbias = _ssd_prep_bwd(daq_b, dak, ddt_direct, ddt_b, dt, proj, dtb128, alog128)
    gs["a_log"], gs["dt_bias"] = dalog[:, :N_HEADS], dbias[:, :N_HEADS]
    same = lambda c: c
    dxbc_x, dcw_x, dcb_x = _conv_bwd(proj, conv_w, conv_b, dxs_skip, dx_ssd, same, same, 0, 8, "conv_bwd_x")
    dxbc_b, dcw_b, dcb_b = _conv_bwd(proj, conv_w, conv_b, db_pair, db_pair, lambda c: 2 * c, lambda c: 2 * c + 1,
                                     D_SSM, 4, "conv_bwd_b")
    dxbc_c, dcw_c, dcb_c = _conv_bwd(proj, conv_w, conv_b, dc_pair, dc_pair, lambda c: 2 * c, lambda c: 2 * c + 1,
                                     D_SSM + 512, 4, "conv_bwd_c")
    g_conv_w = jnp.concatenate([dcw_x, dcw_b, dcw_c], axis=1)
    gs["conv_b"] = jnp.concatenate([dcb_x, dcb_b, dcb_c], axis=1)

    pieces = [dq_pre, dk_pre, dv_pre, dz, dxbc_x, dxbc_b, dxbc_c, ddt_raw]
    pieces.append(jnp.zeros((S, D_INP - sum(t.shape[1] for t in pieces)), BF))
    dproj = jnp.concatenate(pieces, axis=1)
    dw_in = _matmul(h, dproj, mode="tn", name="mm_in_dw", tm=1024, tn=1280, tk=S, out_dtypes=(BF,))
    gb["w_in"] = _w_in_to_blocks(dw_in)
    token = None if hooks is None else hooks.pair_start("w_in", gb["w_in"])
    dh = _matmul(dproj, wf["w_in"], mode="nt", name="mm_in_dx", tm=1024, tn=512, tk=D_INP // 2, after=token)
    if hooks is not None:
        p["g_mix"] = _tie(p["g_mix"], hooks.emit({"w_in": gb["w_in"]}, dh))
    grad_x, _, gs["g_mix"] = _rms_bwd_call(x, p["g_mix"], dh, dx1, "rms_mix_bwd")
    return loss_part, grad_x, gb, gs, g_conv_w


def kernel(x, mem, positions, g_mix, w_in, g_q, g_k, g_attn_out, conv_w, conv_b, dt_bias, a_log, d_skip, g_ssm_out, w_out, g_cross, g_mem, w_cq, w_ckv, g_cq, g_ck, w_co, g_mlp, w_up, w_down, loss_target, m_g_mix, m_w_in, m_g_q, m_g_k, m_g_attn_out, m_conv_w, m_conv_b, m_dt_bias, m_a_log, m_d_skip, m_g_ssm_out, m_w_out, m_g_cross, m_g_mem, m_w_cq, m_w_ckv, m_g_cq, m_g_ck, m_w_co, m_g_mlp, m_w_up, m_w_down, v_g_mix, v_w_in, v_g_q, v_g_k, v_g_attn_out, v_conv_w, v_conv_b, v_dt_bias, v_a_log, v_d_skip, v_g_ssm_out, v_w_out, v_g_cross, v_g_mem, v_w_cq, v_w_ckv, v_g_cq, v_g_ck, v_w_co, v_g_mlp, v_w_up, v_w_down):
    args = dict(locals())
    w = {n: args[n] for n in WEIGHTS}
    m = {n: args["m_" + n] for n in WEIGHTS}
    v = {n: args["v_" + n] for n in WEIGHTS}
    small = {n: w[n] for n, _ in SMALL}
    me = 4 * lax.axis_index("x") + 2 * lax.axis_index("y") + lax.axis_index("c")

    overlap = _Overlap({"w_in": w["w_in"][0].astype(BF), "conv_w": w["conv_w"][0]})
    tok = overlap.token
    overlap.shards = {n: _tie(w[n][0], tok).astype(BF) for n in REST}
    w_in_tied, v_in_tied = _tie(w["w_in"][0], tok), _tie(v["w_in"][0], tok)
    overlap.behind = [w_in_tied, v_in_tied]
    overlap.late = [m["w_in"][0]]
    p = dict(small)
    p["g_mix"] = _tie(p["g_mix"], tok)
    loss_part, grad_x, _, gs, g_conv_w = _local_step(
        x[0], mem[0], positions.reshape(S, 1), loss_target[0], p, {}, overlap)
    adam_in = {"w_in": (w_in_tied, overlap.late[0], v_in_tied)}
    loss = lax.psum(loss_part, ("x", "y", "c"))

    out = {}
    last = grad_x
    for gi in range(3):
        for n, parts in overlap.finish(gi, last).items():
            w_n, m_n, v_n = adam_in.get(n, (w[n][0], m[n][0], v[n][0]))
            res_n = _adamw(w_n, m_n, v_n, parts, "adamw_" + n)
            out[n] = [t.reshape(w[n].shape) for t in res_n]
            last = res_n[0]

    sm_parts, cw_parts = _all_gather([_pack_small(gs), g_conv_w], "ag_small_grads", after=last)
    sm = [_unpack_small(t) for t in _adamw(_pack_small(small), _pack_small({n: m[n] for n, _ in SMALL}),
                                           _pack_small({n: v[n] for n, _ in SMALL}), sm_parts, "adamw_small")]
    for n, _ in SMALL:
        out[n] = [t[n] for t in sm]
    cols = D_CONV // N_DEV
    cw_mine = lax.dynamic_slice_in_dim(cw_parts, me * cols, cols, axis=2)
    out["conv_w"] = [t.reshape(w["conv_w"].shape) for t in
                     _adamw(w["conv_w"][0], m["conv_w"][0], v["conv_w"][0], cw_mine, "adamw_conv_w")]

    res = [loss, grad_x.reshape(x.shape)]
    for k in range(4):
        res += [out[n][k] for n in WEIGHTS]
    return tuple(res)
```
